```python
import jax, jax.numpy as jnp
from jax import lax
import numpy as np

D_MODEL = 2048
BATCH = 8
SEQ = 2048
DEPTH = 2

N_EVEN = (DEPTH + 1) // 2
N_ODD = DEPTH // 2
MLA_HEADS = 8
Q_LORA = 512
KV_LORA = 512
QK_NOPE = 128
QK_ROPE = 64
V_HEAD = 128
ROPE_BASE = 10000.0
Q_BLOCK = 128
SGU_GROUPS = 8
SGU_CH = 128
CHUNK = 128
CONV_DIM = D_MODEL
CONV_WIDTH = 3
D_FF = 4 * D_MODEL
EPS = 1e-6

MLA_OUT = MLA_HEADS * V_HEAD
SGU_OUT = SGU_GROUPS * SGU_CH
MIX_WIDTH = MLA_OUT + SGU_OUT
EVEN_IN = Q_LORA + KV_LORA + QK_ROPE + 2 * SGU_OUT

kernel_name = "hybrid_mla_sgu_shortconv_block"


def rms_norm(x, g):
    xf = x.astype(jnp.float32)
    y = xf * lax.rsqrt(jnp.mean(xf * xf, axis=-1, keepdims=True) + EPS)
    return (y * g.astype(jnp.float32)).astype(x.dtype)


def group_layer_norm(v, g):
    vf = v.astype(jnp.float32)
    mu = jnp.mean(vf, axis=-1, keepdims=True)
    var = jnp.mean(jnp.square(vf - mu), axis=-1, keepdims=True)
    return ((vf - mu) * lax.rsqrt(var + EPS) * g.astype(jnp.float32)).astype(v.dtype)


def rope_tables(positions):
    inv_freq = ROPE_BASE ** (-jnp.arange(0, QK_ROPE, 2, dtype=jnp.float32) / QK_ROPE)
    ang = positions.astype(jnp.float32)[..., None] * inv_freq
    return jnp.cos(ang), jnp.sin(ang)


def apply_rope(x, cos, sin):
    xf = x.astype(jnp.float32)
    x1, x2 = xf[..., : QK_ROPE // 2], xf[..., QK_ROPE // 2:]
    return jnp.concatenate([x1 * cos - x2 * sin, x2 * cos + x1 * sin], axis=-1).astype(x.dtype)


def mla_mixer(c_q, c_kv, k_rope_raw, cos, sin, q_norm, w_uq, kv_norm, w_ukv):
    B, S, _ = c_q.shape
    q = (rms_norm(c_q, q_norm) @ w_uq).reshape(B, S, MLA_HEADS, QK_NOPE + QK_ROPE)
    q_nope = q[..., :QK_NOPE]
    q_rope = apply_rope(q[..., QK_NOPE:], cos[:, :, None], sin[:, :, None])
    kv = (rms_norm(c_kv, kv_norm) @ w_ukv).reshape(B, S, MLA_HEADS, QK_NOPE + V_HEAD)
    k_nope, v = kv[..., :QK_NOPE], kv[..., QK_NOPE:]
    k_rope = apply_rope(k_rope_raw, cos, sin)
    scale = (QK_NOPE + QK_ROPE) ** -0.5
    outs = []
    for i in range(S // Q_BLOCK):
        q0, kend = i * Q_BLOCK, (i + 1) * Q_BLOCK
        s = (jnp.einsum('bqhd,bkhd->bhqk', q_nope[:, q0:kend], k_nope[:, :kend])
             + jnp.einsum('bqhr,bkr->bhqk', q_rope[:, q0:kend], k_rope[:, :kend]))
        s = s.astype(jnp.float32) * scale
        q_idx = q0 + jnp.arange(Q_BLOCK)
        mask = jnp.arange(kend)[None, :] <= q_idx[:, None]
        p = jax.nn.softmax(jnp.where(mask, s, -jnp.inf), axis=-1).astype(v.dtype)
        outs.append(jnp.einsum('bhqk,bkhd->bqhd', p, v[:, :kend]))
    o = jnp.concatenate(outs, axis=1)
    return o.reshape(B, S, MLA_OUT)


def sgu_mixer(uv, v_norm, w_s, b_s):
    B, S, _ = uv.shape
    uv = jax.nn.gelu(uv)
    u, v = uv[..., :SGU_OUT], uv[..., SGU_OUT:]
    v = group_layer_norm(v.reshape(B, S, SGU_GROUPS, SGU_CH), v_norm)
    v = v.reshape(B, S // CHUNK, CHUNK, SGU_GROUPS, SGU_CH)
    w = jnp.tril(w_s)
    y = jnp.einsum('gts,bcsgd->bctgd', w, v) + b_s.T[None, None, :, :, None]
    return u * y.reshape(B, S, SGU_OUT)


def short_conv_mixer(h, w_in, conv_w, w_out):
    proj = h @ w_in
    b_gate = proj[..., :CONV_DIM]
    c_gate = proj[..., CONV_DIM:2 * CONV_DIM]
    xin = proj[..., 2 * CONV_DIM:]
    z = c_gate * xin
    z = lax.conv_general_dilated(
        z, conv_w[:, None, :].astype(z.dtype), window_strides=(1,),
        padding=[(CONV_WIDTH - 1, 0)], dimension_numbers=('NWC', 'WIO', 'NWC'),
        feature_group_count=CONV_DIM)
    return (b_gate * z) @ w_out


def sqrelu_mlp(h, w1, w2):
    a = jax.nn.relu(h @ w1)
    return (a * a) @ w2


def _fwd_setup_inputs(seed: int = 0) -> dict:
    key = jax.random.key(seed)
    ks = jax.random.split(key, 24)

    def nrm(k, shape, scale):
        return jax.random.normal(k, shape, jnp.float32) * scale

    def gain(k, shape):
        return 1.0 + 0.02 * jax.random.normal(k, shape, jnp.float32)

    x = jax.random.normal(ks[0], (BATCH, SEQ, D_MODEL), jnp.float32)
    offset = jax.random.randint(ks[1], (BATCH, 1), 0, 4096, dtype=jnp.int32)
    positions = offset + jnp.arange(SEQ, dtype=jnp.int32)[None, :]
    return {
        "x": x,
        "positions": positions,
        "e_norm_mix": gain(ks[2], (N_EVEN, D_MODEL)),
        "e_w_in": nrm(ks[3], (N_EVEN, D_MODEL, EVEN_IN), D_MODEL ** -0.5),
        "e_q_norm": gain(ks[4], (N_EVEN, Q_LORA)),
        "e_w_uq": nrm(ks[5], (N_EVEN, Q_LORA, MLA_HEADS * (QK_NOPE + QK_ROPE)), Q_LORA ** -0.5),
        "e_kv_norm": gain(ks[6], (N_EVEN, KV_LORA)),
        "e_w_ukv": nrm(ks[7], (N_EVEN, KV_LORA, MLA_HEADS * (QK_NOPE + V_HEAD)), KV_LORA ** -0.5),
        "e_v_norm": gain(ks[8], (N_EVEN, SGU_GROUPS, SGU_CH)),
        "e_sgu_w": nrm(ks[9], (N_EVEN, SGU_GROUPS, CHUNK, CHUNK), CHUNK ** -0.5),
        "e_sgu_b": 1.0 + nrm(ks[10], (N_EVEN, SGU_GROUPS, CHUNK), 0.1),
        "e_mla_out_norm": gain(ks[11], (N_EVEN, MLA_OUT)),
        "e_sgu_out_norm": gain(ks[12], (N_EVEN, SGU_OUT)),
        "e_w_out": nrm(ks[13], (N_EVEN, MIX_WIDTH, D_MODEL), MIX_WIDTH ** -0.5),
        "o_norm_mix": gain(ks[14], (N_ODD, D_MODEL)),
        "o_w_in": nrm(ks[15], (N_ODD, D_MODEL, 3 * CONV_DIM), D_MODEL ** -0.5),
        "o_conv_w": nrm(ks[16], (N_ODD, CONV_WIDTH, CONV_DIM), CONV_WIDTH ** -0.5),
        "o_w_out": nrm(ks[17], (N_ODD, CONV_DIM, D_MODEL), CONV_DIM ** -0.5),
        "mlp_norm": gain(ks[18], (DEPTH, D_MODEL)),
        "mlp_w1": nrm(ks[19], (DEPTH, D_MODEL, D_FF), D_MODEL ** -0.5),
        "mlp_w2": nrm(ks[20], (DEPTH, D_FF, D_MODEL), 0.5 * D_FF ** -0.5),
        "final_norm": gain(ks[21], (D_MODEL,)),
    }


def _fwd_reference(x, positions, e_norm_mix, e_w_in, e_q_norm, e_w_uq, e_kv_norm, e_w_ukv,
              e_v_norm, e_sgu_w, e_sgu_b, e_mla_out_norm, e_sgu_out_norm, e_w_out,
              o_norm_mix, o_w_in, o_conv_w, o_w_out, mlp_norm, mlp_w1, mlp_w2, final_norm):
    cos, sin = rope_tables(positions)
    c1 = Q_LORA
    c2 = c1 + KV_LORA
    c3 = c2 + QK_ROPE
    for layer in range(DEPTH):
        i = layer // 2
        if layer % 2 == 0:
            h = rms_norm(x, e_norm_mix[i])
            proj = h @ e_w_in[i]
            a = mla_mixer(proj[..., :c1], proj[..., c1:c2], proj[..., c2:c3], cos, sin,
                          e_q_norm[i], e_w_uq[i], e_kv_norm[i], e_w_ukv[i])
            s = sgu_mixer(proj[..., c3:], e_v_norm[i], e_sgu_w[i], e_sgu_b[i])
            mixed = jnp.concatenate([rms_norm(a, e_mla_out_norm[i]),
                                     rms_norm(s, e_sgu_out_norm[i])], axis=-1)
            x = x + mixed @ e_w_out[i]
        else:
            x = x + short_conv_mixer(rms_norm(x, o_norm_mix[i]), o_w_in[i], o_conv_w[i], o_w_out[i])
        x = x + sqrelu_mlp(rms_norm(x, mlp_norm[layer]), mlp_w1[layer], mlp_w2[layer])
    return rms_norm(x, final_norm)


import jax as _jax
import jax.numpy as _jnp

TWIN_FORMAT = 'train_step'
FWD_PARAMS = ['x', 'positions', 'e_norm_mix', 'e_w_in', 'e_q_norm', 'e_w_uq', 'e_kv_norm', 'e_w_ukv', 'e_v_norm', 'e_sgu_w', 'e_sgu_b', 'e_mla_out_norm', 'e_sgu_out_norm', 'e_w_out', 'o_norm_mix', 'o_w_in', 'o_conv_w', 'o_w_out', 'mlp_norm', 'mlp_w1', 'mlp_w2', 'final_norm']
TWIN_WEIGHTS = ['e_norm_mix', 'e_w_in', 'e_q_norm', 'e_w_uq', 'e_kv_norm', 'e_w_ukv', 'e_v_norm', 'e_sgu_w', 'e_sgu_b', 'e_mla_out_norm', 'e_sgu_out_norm', 'e_w_out', 'o_norm_mix', 'o_w_in', 'o_conv_w', 'o_w_out', 'mlp_norm', 'mlp_w1', 'mlp_w2', 'final_norm']
TWIN_DIFF_INPUT = 'x'
TWIN_INPUTS = ['x', 'positions', 'e_norm_mix', 'e_w_in', 'e_q_norm', 'e_w_uq', 'e_kv_norm', 'e_w_ukv', 'e_v_norm', 'e_sgu_w', 'e_sgu_b', 'e_mla_out_norm', 'e_sgu_out_norm', 'e_w_out', 'o_norm_mix', 'o_w_in', 'o_conv_w', 'o_w_out', 'mlp_norm', 'mlp_w1', 'mlp_w2', 'final_norm', 'loss_target', 'm_e_norm_mix', 'm_e_w_in', 'm_e_q_norm', 'm_e_w_uq', 'm_e_kv_norm', 'm_e_w_ukv', 'm_e_v_norm', 'm_e_sgu_w', 'm_e_sgu_b', 'm_e_mla_out_norm', 'm_e_sgu_out_norm', 'm_e_w_out', 'm_o_norm_mix', 'm_o_w_in', 'm_o_conv_w', 'm_o_w_out', 'm_mlp_norm', 'm_mlp_w1', 'm_mlp_w2', 'm_final_norm', 'v_e_norm_mix', 'v_e_w_in', 'v_e_q_norm', 'v_e_w_uq', 'v_e_kv_norm', 'v_e_w_ukv', 'v_e_v_norm', 'v_e_sgu_w', 'v_e_sgu_b', 'v_e_mla_out_norm', 'v_e_sgu_out_norm', 'v_e_w_out', 'v_o_norm_mix', 'v_o_w_in', 'v_o_conv_w', 'v_o_w_out', 'v_mlp_norm', 'v_mlp_w1', 'v_mlp_w2', 'v_final_norm']
TWIN_OUTPUTS = ['loss', 'grad_x', 'grad_e_norm_mix', 'grad_e_w_in', 'grad_e_q_norm', 'grad_e_w_uq', 'grad_e_kv_norm', 'grad_e_w_ukv', 'grad_e_v_norm', 'grad_e_sgu_w', 'grad_e_sgu_b', 'grad_e_mla_out_norm', 'grad_e_sgu_out_norm', 'grad_e_w_out', 'grad_o_norm_mix', 'grad_o_w_in', 'grad_o_conv_w', 'grad_o_w_out', 'grad_mlp_norm', 'grad_mlp_w1', 'grad_mlp_w2', 'grad_final_norm', 'delta_e_norm_mix', 'delta_e_w_in', 'delta_e_q_norm', 'delta_e_w_uq', 'delta_e_kv_norm', 'delta_e_w_ukv', 'delta_e_v_norm', 'delta_e_sgu_w', 'delta_e_sgu_b', 'delta_e_mla_out_norm', 'delta_e_sgu_out_norm', 'delta_e_w_out', 'delta_o_norm_mix', 'delta_o_w_in', 'delta_o_conv_w', 'delta_o_w_out', 'delta_mlp_norm', 'delta_mlp_w1', 'delta_mlp_w2', 'delta_final_norm', 'new_m_e_norm_mix', 'new_m_e_w_in', 'new_m_e_q_norm', 'new_m_e_w_uq', 'new_m_e_kv_norm', 'new_m_e_w_ukv', 'new_m_e_v_norm', 'new_m_e_sgu_w', 'new_m_e_sgu_b', 'new_m_e_mla_out_norm', 'new_m_e_sgu_out_norm', 'new_m_e_w_out', 'new_m_o_norm_mix', 'new_m_o_w_in', 'new_m_o_conv_w', 'new_m_o_w_out', 'new_m_mlp_norm', 'new_m_mlp_w1', 'new_m_mlp_w2', 'new_m_final_norm', 'new_v_e_norm_mix', 'new_v_e_w_in', 'new_v_e_q_norm', 'new_v_e_w_uq', 'new_v_e_kv_norm', 'new_v_e_w_ukv', 'new_v_e_v_norm', 'new_v_e_sgu_w', 'new_v_e_sgu_b', 'new_v_e_mla_out_norm', 'new_v_e_sgu_out_norm', 'new_v_e_w_out', 'new_v_o_norm_mix', 'new_v_o_w_in', 'new_v_o_conv_w', 'new_v_o_w_out', 'new_v_mlp_norm', 'new_v_mlp_w1', 'new_v_mlp_w2', 'new_v_final_norm']
TWIN_LEAF_KINDS = {'loss': 'loss', 'grad_x': 'grad_x', 'grad_e_norm_mix': 'grad_w', 'grad_e_w_in': 'grad_w', 'grad_e_q_norm': 'grad_w', 'grad_e_w_uq': 'grad_w', 'grad_e_kv_norm': 'grad_w', 'grad_e_w_ukv': 'grad_w', 'grad_e_v_norm': 'grad_w', 'grad_e_sgu_w': 'grad_w', 'grad_e_sgu_b': 'grad_w', 'grad_e_mla_out_norm': 'grad_w', 'grad_e_sgu_out_norm': 'grad_w', 'grad_e_w_out': 'grad_w', 'grad_o_norm_mix': 'grad_w', 'grad_o_w_in': 'grad_w', 'grad_o_conv_w': 'grad_w', 'grad_o_w_out': 'grad_w', 'grad_mlp_norm': 'grad_w', 'grad_mlp_w1': 'grad_w', 'grad_mlp_w2': 'grad_w', 'grad_final_norm': 'grad_w', 'delta_e_norm_mix': 'delta_w', 'delta_e_w_in': 'delta_w', 'delta_e_q_norm': 'delta_w', 'delta_e_w_uq': 'delta_w', 'delta_e_kv_norm': 'delta_w', 'delta_e_w_ukv': 'delta_w', 'delta_e_v_norm': 'delta_w', 'delta_e_sgu_w': 'delta_w', 'delta_e_sgu_b': 'delta_w', 'delta_e_mla_out_norm': 'delta_w', 'delta_e_sgu_out_norm': 'delta_w', 'delta_e_w_out': 'delta_w', 'delta_o_norm_mix': 'delta_w', 'delta_o_w_in': 'delta_w', 'delta_o_conv_w': 'delta_w', 'delta_o_w_out': 'delta_w', 'delta_mlp_norm': 'delta_w', 'delta_mlp_w1': 'delta_w', 'delta_mlp_w2': 'delta_w', 'delta_final_norm': 'delta_w', 'new_m_e_norm_mix': 'new_m', 'new_m_e_w_in': 'new_m', 'new_m_e_q_norm': 'new_m', 'new_m_e_w_uq': 'new_m', 'new_m_e_kv_norm': 'new_m', 'new_m_e_w_ukv': 'new_m', 'new_m_e_v_norm': 'new_m', 'new_m_e_sgu_w': 'new_m', 'new_m_e_sgu_b': 'new_m', 'new_m_e_mla_out_norm': 'new_m', 'new_m_e_sgu_out_norm': 'new_m', 'new_m_e_w_out': 'new_m', 'new_m_o_norm_mix': 'new_m', 'new_m_o_w_in': 'new_m', 'new_m_o_conv_w': 'new_m', 'new_m_o_w_out': 'new_m', 'new_m_mlp_norm': 'new_m', 'new_m_mlp_w1': 'new_m', 'new_m_mlp_w2': 'new_m', 'new_m_final_norm': 'new_m', 'new_v_e_norm_mix': 'new_v', 'new_v_e_w_in': 'new_v', 'new_v_e_q_norm': 'new_v', 'new_v_e_w_uq': 'new_v', 'new_v_e_kv_norm': 'new_v', 'new_v_e_w_ukv': 'new_v', 'new_v_e_v_norm': 'new_v', 'new_v_e_sgu_w': 'new_v', 'new_v_e_sgu_b': 'new_v', 'new_v_e_mla_out_norm': 'new_v', 'new_v_e_sgu_out_norm': 'new_v', 'new_v_e_w_out': 'new_v', 'new_v_o_norm_mix': 'new_v', 'new_v_o_w_in': 'new_v', 'new_v_o_conv_w': 'new_v', 'new_v_o_w_out': 'new_v', 'new_v_mlp_norm': 'new_v', 'new_v_mlp_w1': 'new_v', 'new_v_mlp_w2': 'new_v', 'new_v_final_norm': 'new_v'}


def _forward(args):
    return _fwd_reference(*[args[k] for k in FWD_PARAMS])


def _output_shape():
    out = _jax.eval_shape(lambda: _forward(_fwd_setup_inputs(0)))
    return out.shape, out.dtype

N_MICROBATCH = 1
ADAM_LR = 0.001
ADAM_B1 = 0.9
ADAM_B2 = 0.999
ADAM_EPS = 1e-08
ADAM_WD = 0.01
ADAM_STEP = 10
PER_EXAMPLE_BATCH_AXIS = {'x': 0, 'positions': 0, 'loss_target': 0}
SHARED_INPUTS = []
_WEIGHT_DTYPES = {'e_norm_mix': _jnp.float32, 'e_w_in': _jnp.float32, 'e_q_norm': _jnp.float32, 'e_w_uq': _jnp.float32, 'e_kv_norm': _jnp.float32, 'e_w_ukv': _jnp.float32, 'e_v_norm': _jnp.float32, 'e_sgu_w': _jnp.float32, 'e_sgu_b': _jnp.float32, 'e_mla_out_norm': _jnp.float32, 'e_sgu_out_norm': _jnp.float32, 'e_w_out': _jnp.float32, 'o_norm_mix': _jnp.float32, 'o_w_in': _jnp.float32, 'o_conv_w': _jnp.float32, 'o_w_out': _jnp.float32, 'mlp_norm': _jnp.float32, 'mlp_w1': _jnp.float32, 'mlp_w2': _jnp.float32, 'final_norm': _jnp.float32}
MOMENT_SCALE = {'e_norm_mix': 8.546360e-02, 'e_w_in': 6.764362e-02, 'e_q_norm': 8.779531e-02, 'e_w_uq': 4.810703e-02, 'e_kv_norm': 1.346580e-01, 'e_w_ukv': 5.577304e-02, 'e_v_norm': 3.431560e-02, 'e_sgu_w': 3.229649e-02, 'e_sgu_b': 4.793788e-02, 'e_mla_out_norm': 6.653960e-02, 'e_sgu_out_norm': 6.219267e-02, 'e_w_out': 5.919092e-02, 'o_norm_mix': 6.247315e-02, 'o_w_in': 3.454311e-02, 'o_conv_w': 3.559180e-02, 'o_w_out': 3.475877e-02, 'mlp_norm': 3.022548e-02, 'mlp_w1': 1.534039e-02, 'mlp_w2': 5.522466e-02, 'final_norm': 8.035139e+00}


def _to_microbatches(a, axis):
    t = _jnp.moveaxis(a, axis, 0)
    t = t.reshape((N_MICROBATCH, t.shape[0] // N_MICROBATCH) + t.shape[1:])
    return _jnp.moveaxis(t, 1, axis + 1)


def setup_inputs(seed: int = 0) -> dict:
    inp = _fwd_setup_inputs(seed)
    key = _jax.random.fold_in(_jax.random.key(seed), 7919)
    shape, _ = _output_shape()
    out = dict(inp)
    out["loss_target"] = _jax.random.normal(_jax.random.fold_in(key, 0), shape, _jnp.float32)
    for i, name in enumerate(TWIN_WEIGHTS):
        w = inp[name].astype(_jnp.float32)
        if MOMENT_SCALE is None:
            s = _jnp.sqrt(_jnp.mean(_jnp.square(w)) + 1e-30)
        else:
            s = MOMENT_SCALE[name]
        km, kv = _jax.random.split(_jax.random.fold_in(key, i + 1))
        out[name] = w
        out["m_" + name] = s * _jax.random.normal(km, w.shape, _jnp.float32)
        out["v_" + name] = (s * s) * _jax.random.uniform(kv, w.shape, _jnp.float32, 0.5, 1.5)
    if N_MICROBATCH > 1:
        for name, axis in PER_EXAMPLE_BATCH_AXIS.items():
            out[name] = _to_microbatches(out[name], axis)
    return {'x': out['x'], 'positions': out['positions'], 'e_norm_mix': out['e_norm_mix'], 'e_w_in': out['e_w_in'], 'e_q_norm': out['e_q_norm'], 'e_w_uq': out['e_w_uq'], 'e_kv_norm': out['e_kv_norm'], 'e_w_ukv': out['e_w_ukv'], 'e_v_norm': out['e_v_norm'], 'e_sgu_w': out['e_sgu_w'], 'e_sgu_b': out['e_sgu_b'], 'e_mla_out_norm': out['e_mla_out_norm'], 'e_sgu_out_norm': out['e_sgu_out_norm'], 'e_w_out': out['e_w_out'], 'o_norm_mix': out['o_norm_mix'], 'o_w_in': out['o_w_in'], 'o_conv_w': out['o_conv_w'], 'o_w_out': out['o_w_out'], 'mlp_norm': out['mlp_norm'], 'mlp_w1': out['mlp_w1'], 'mlp_w2': out['mlp_w2'], 'final_norm': out['final_norm'], 'loss_target': out['loss_target'], 'm_e_norm_mix': out['m_e_norm_mix'], 'm_e_w_in': out['m_e_w_in'], 'm_e_q_norm': out['m_e_q_norm'], 'm_e_w_uq': out['m_e_w_uq'], 'm_e_kv_norm': out['m_e_kv_norm'], 'm_e_w_ukv': out['m_e_w_ukv'], 'm_e_v_norm': out['m_e_v_norm'], 'm_e_sgu_w': out['m_e_sgu_w'], 'm_e_sgu_b': out['m_e_sgu_b'], 'm_e_mla_out_norm': out['m_e_mla_out_norm'], 'm_e_sgu_out_norm': out['m_e_sgu_out_norm'], 'm_e_w_out': out['m_e_w_out'], 'm_o_norm_mix': out['m_o_norm_mix'], 'm_o_w_in': out['m_o_w_in'], 'm_o_conv_w': out['m_o_conv_w'], 'm_o_w_out': out['m_o_w_out'], 'm_mlp_norm': out['m_mlp_norm'], 'm_mlp_w1': out['m_mlp_w1'], 'm_mlp_w2': out['m_mlp_w2'], 'm_final_norm': out['m_final_norm'], 'v_e_norm_mix': out['v_e_norm_mix'], 'v_e_w_in': out['v_e_w_in'], 'v_e_q_norm': out['v_e_q_norm'], 'v_e_w_uq': out['v_e_w_uq'], 'v_e_kv_norm': out['v_e_kv_norm'], 'v_e_w_ukv': out['v_e_w_ukv'], 'v_e_v_norm': out['v_e_v_norm'], 'v_e_sgu_w': out['v_e_sgu_w'], 'v_e_sgu_b': out['v_e_sgu_b'], 'v_e_mla_out_norm': out['v_e_mla_out_norm'], 'v_e_sgu_out_norm': out['v_e_sgu_out_norm'], 'v_e_w_out': out['v_e_w_out'], 'v_o_norm_mix': out['v_o_norm_mix'], 'v_o_w_in': out['v_o_w_in'], 'v_o_conv_w': out['v_o_conv_w'], 'v_o_w_out': out['v_o_w_out'], 'v_mlp_norm': out['v_mlp_norm'], 'v_mlp_w1': out['v_mlp_w1'], 'v_mlp_w2': out['v_mlp_w2'], 'v_final_norm': out['v_final_norm']}


def _loss(weights, diff, rest, loss_target):
    with _jax.named_scope("forward"):
        args = {**rest, TWIN_DIFF_INPUT: diff, **{k: w.astype(_WEIGHT_DTYPES[k]) for k, w in weights.items()}}
        y = _forward(args)
    with _jax.named_scope("loss_head"):
        err = _jnp.square(y.astype(_jnp.float32) - loss_target)
        return 0.5 * _jnp.sum(_jnp.mean(err, axis=-1)) if err.ndim else 0.5 * err


def _adamw(w, g, m, v):
    m = ADAM_B1 * m + (1.0 - ADAM_B1) * g
    v = ADAM_B2 * v + (1.0 - ADAM_B2) * _jnp.square(g)
    m_hat = m / (1.0 - ADAM_B1 ** ADAM_STEP)
    v_hat = v / (1.0 - ADAM_B2 ** ADAM_STEP)
    delta = -ADAM_LR * (m_hat / (_jnp.sqrt(v_hat) + ADAM_EPS) + ADAM_WD * w)
    return delta, m, v


def reference(x, positions, e_norm_mix, e_w_in, e_q_norm, e_w_uq, e_kv_norm, e_w_ukv, e_v_norm, e_sgu_w, e_sgu_b, e_mla_out_norm, e_sgu_out_norm, e_w_out, o_norm_mix, o_w_in, o_conv_w, o_w_out, mlp_norm, mlp_w1, mlp_w2, final_norm, loss_target, m_e_norm_mix, m_e_w_in, m_e_q_norm, m_e_w_uq, m_e_kv_norm, m_e_w_ukv, m_e_v_norm, m_e_sgu_w, m_e_sgu_b, m_e_mla_out_norm, m_e_sgu_out_norm, m_e_w_out, m_o_norm_mix, m_o_w_in, m_o_conv_w, m_o_w_out, m_mlp_norm, m_mlp_w1, m_mlp_w2, m_final_norm, v_e_norm_mix, v_e_w_in, v_e_q_norm, v_e_w_uq, v_e_kv_norm, v_e_w_ukv, v_e_v_norm, v_e_sgu_w, v_e_sgu_b, v_e_mla_out_norm, v_e_sgu_out_norm, v_e_w_out, v_o_norm_mix, v_o_w_in, v_o_conv_w, v_o_w_out, v_mlp_norm, v_mlp_w1, v_mlp_w2, v_final_norm):
    given = dict(x=x, positions=positions, e_norm_mix=e_norm_mix, e_w_in=e_w_in, e_q_norm=e_q_norm, e_w_uq=e_w_uq, e_kv_norm=e_kv_norm, e_w_ukv=e_w_ukv, e_v_norm=e_v_norm, e_sgu_w=e_sgu_w, e_sgu_b=e_sgu_b, e_mla_out_norm=e_mla_out_norm, e_sgu_out_norm=e_sgu_out_norm, e_w_out=e_w_out, o_norm_mix=o_norm_mix, o_w_in=o_w_in, o_conv_w=o_conv_w, o_w_out=o_w_out, mlp_norm=mlp_norm, mlp_w1=mlp_w1, mlp_w2=mlp_w2, final_norm=final_norm, loss_target=loss_target, m_e_norm_mix=m_e_norm_mix, m_e_w_in=m_e_w_in, m_e_q_norm=m_e_q_norm, m_e_w_uq=m_e_w_uq, m_e_kv_norm=m_e_kv_norm, m_e_w_ukv=m_e_w_ukv, m_e_v_norm=m_e_v_norm, m_e_sgu_w=m_e_sgu_w, m_e_sgu_b=m_e_sgu_b, m_e_mla_out_norm=m_e_mla_out_norm, m_e_sgu_out_norm=m_e_sgu_out_norm, m_e_w_out=m_e_w_out, m_o_norm_mix=m_o_norm_mix, m_o_w_in=m_o_w_in, m_o_conv_w=m_o_conv_w, m_o_w_out=m_o_w_out, m_mlp_norm=m_mlp_norm, m_mlp_w1=m_mlp_w1, m_mlp_w2=m_mlp_w2, m_final_norm=m_final_norm, v_e_norm_mix=v_e_norm_mix, v_e_w_in=v_e_w_in, v_e_q_norm=v_e_q_norm, v_e_w_uq=v_e_w_uq, v_e_kv_norm=v_e_kv_norm, v_e_w_ukv=v_e_w_ukv, v_e_v_norm=v_e_v_norm, v_e_sgu_w=v_e_sgu_w, v_e_sgu_b=v_e_sgu_b, v_e_mla_out_norm=v_e_mla_out_norm, v_e_sgu_out_norm=v_e_sgu_out_norm, v_e_w_out=v_e_w_out, v_o_norm_mix=v_o_norm_mix, v_o_w_in=v_o_w_in, v_o_conv_w=v_o_conv_w, v_o_w_out=v_o_w_out, v_mlp_norm=v_mlp_norm, v_mlp_w1=v_mlp_w1, v_mlp_w2=v_mlp_w2, v_final_norm=v_final_norm)
    weights = {n: given[n] for n in TWIN_WEIGHTS}
    shared = {n: given[n] for n in SHARED_INPUTS}
    per_example = {n: given[n] for n in ['x', 'positions']}
    grad_fn = _jax.value_and_grad(_loss, argnums=(0, 1))

    def one_microbatch(ex, loss_target):
        ex = dict(ex)
        diff = ex.pop(TWIN_DIFF_INPUT)
        return grad_fn(weights, diff, {**shared, **ex}, loss_target)

    if N_MICROBATCH == 1:
        loss, (grad_w, grad_x) = one_microbatch(per_example, given["loss_target"])
    else:
        def body(carry, xs):
            loss_sum, grad_sum = carry
            l_k, (gw_k, gx_k) = one_microbatch(xs[0], xs[1])
            with _jax.named_scope("update"):
                return (loss_sum + l_k, _jax.tree.map(_jnp.add, grad_sum, gw_k)), gx_k

        init = (_jnp.zeros((), _jnp.float32), _jax.tree.map(_jnp.zeros_like, weights))
        (loss, grad_w), grad_x = _jax.lax.scan(body, init, (per_example, given["loss_target"]))
    with _jax.named_scope("update"):
        delta_w, new_m, new_v = {}, {}, {}
        for n in TWIN_WEIGHTS:
            delta_w[n], new_m[n], new_v[n] = _adamw(weights[n], grad_w[n], given["m_" + n], given["v_" + n])
    return (loss, grad_x, *[grad_w[n] for n in TWIN_WEIGHTS], *[delta_w[n] for n in TWIN_WEIGHTS],
            *[new_m[n] for n in TWIN_WEIGHTS], *[new_v[n] for n in TWIN_WEIGHTS])
```

```python
import functools

import numpy as np
import jax
import jax.numpy as jnp
from jax import lax
from jax.experimental import pallas as pl
from jax.experimental.pallas import tpu as pltpu

BF = jnp.bfloat16
F32 = jnp.float32
MESH = pl.DeviceIdType.MESH
AXES = ("x", "y", "c")
N_DEV = 8

EPS = 1e-6
HEADS = 8
Q_LORA = 512
KV_LORA = 512
QK_NOPE = 128
QK_ROPE = 64
HALF_ROPE = QK_ROPE // 2
V_HEAD = 128
HEAD_PAD = 256
ROPE_BASE = 10000.0
GROUPS = 8
CH = 128
CHUNK = 128
SGU_OUT = GROUPS * CH
MLA_OUT = HEADS * V_HEAD
ATTN_SCALE = float((QK_NOPE + QK_ROPE) ** -0.5)

ADAM_LR = 0.001
ADAM_B1 = 0.9
ADAM_B2 = 0.999
ADAM_EPS = 1e-08
ADAM_WD = 0.01
ADAM_STEP = 10
ADAM_C1 = 1.0 - ADAM_B1 ** ADAM_STEP
ADAM_C2 = 1.0 - ADAM_B2 ** ADAM_STEP

V7X_VMEM_BYTES = 64 * 2 ** 20
VMEM_LIMIT_CAP = V7X_VMEM_BYTES - 6 * 2 ** 20
LANES = 128
ROW_TILE = 256
MM_TILE = 1024
MM_K_TILE = 2048


def _padded_bytes(block, dtype):
    dims = [d for d in block if d is not None]
    if len(dims) >= 1:
        dims[-1] = -(-dims[-1] // LANES) * LANES
    if len(dims) >= 2:
        dims[-2] = -(-dims[-2] // 16) * 16
    return int(np.prod(dims)) * jnp.dtype(dtype).itemsize


def _pcall(body, *, name, grid, ins, outs, scratch=(), semantics=None, aliases=None, prefetch=None):
    any_spec = pl.BlockSpec(memory_space=pl.ANY)
    in_specs = [any_spec if b is None else pl.BlockSpec(b, m) for _, b, m in ins]
    out_specs = [any_spec if b is None else pl.BlockSpec(b, m) for _, _, b, m in outs]
    out_shape = [jax.ShapeDtypeStruct(s, d) for s, d, _, _ in outs]
    est = 0
    for a, b, _ in ins:
        if b is not None:
            est += 2 * _padded_bytes(b, a.dtype)
    for _, d, b, _ in outs:
        if b is not None:
            est += 2 * _padded_bytes(b, d)
    for s in scratch:
        if hasattr(s, "shape") and hasattr(s, "dtype"):
            est += _padded_bytes(s.shape, s.dtype)
    limit = int(min(VMEM_LIMIT_CAP, est + 16 * 2 ** 20))
    params = pltpu.CompilerParams(
        dimension_semantics=semantics or ("arbitrary",) * len(grid), vmem_limit_bytes=limit)
    args = [a for a, _, _ in ins]
    if prefetch is not None:
        grid_spec = pltpu.PrefetchScalarGridSpec(
            num_scalar_prefetch=1, grid=grid, in_specs=in_specs, out_specs=out_specs, scratch_shapes=list(scratch))
        call = pl.pallas_call(body, out_shape=out_shape, grid_spec=grid_spec, name=name, compiler_params=params,
                              input_output_aliases=aliases or {})
        return call(prefetch, *args)
    call = pl.pallas_call(body, out_shape=out_shape, grid=grid, in_specs=in_specs, out_specs=out_specs,
                          scratch_shapes=list(scratch), name=name, compiler_params=params,
                          input_output_aliases=aliases or {})
    return call(*args)


def _tile(dim, pref, quantum=LANES):
    if dim <= pref:
        return dim
    t = (pref // quantum) * quantum
    while t >= quantum:
        if dim % t == 0:
            return t
        t -= quantum
    return dim


def _vshape(arr_shape):
    if len(arr_shape) == 2:
        return tuple(arr_shape)
    s, r, c = arr_shape
    return (r, s * c)


def _vblock(arr_shape, br, bc, rc):
    if len(arr_shape) == 2:
        return (br, bc), (lambda *g: rc(*g))
    _, _, c = arr_shape
    assert c % bc == 0, (arr_shape, bc)
    per = c // bc

    def imap(*g):
        ri, ci = rc(*g)
        return (ci // per, ri, ci % per)

    return (None, br, bc), imap


def _shard_width(*shapes):
    w = None
    for s in shapes:
        if len(s) == 3:
            w = s[2] if w is None else int(np.gcd(w, s[2]))
    return w


def mm(a, b, *, name, ta=False, tb=False, out=None, outs=None, epi=None, epi_ins=(), bm=None, bn=None, bk=None):
    av, bv = _vshape(a.shape), _vshape(b.shape)
    M, K = (av[1], av[0]) if ta else av
    K2, N = (bv[1], bv[0]) if tb else bv
    assert K == K2, (a.shape, b.shape, ta, tb)
    if outs is None:
        outs = [(out[0], out[1], None)]
    a_sw = _shard_width(a.shape)
    b_sw = _shard_width(b.shape)
    o_sw = _shard_width(*[o[0] for o in outs])
    m_lim = a_sw if (ta and a_sw) else None
    k_lim = [w for w in ((a_sw if not ta else None), (b_sw if tb else None)) if w]
    n_lim = [w for w in ((b_sw if not tb else None), o_sw) if w]
    if bm is None:
        bm = _tile(M, min([MM_TILE] + ([m_lim] if m_lim else [])))
    if bn is None:
        bn = _tile(N, min([MM_TILE] + n_lim))
    if bk is None:
        bk = K if (K <= 4096 and not k_lim) else _tile(K, min([MM_K_TILE] + k_lim))
    assert M % bm == 0 and N % bn == 0 and K % bk == 0, (name, M, N, K, bm, bn, bk)
    nk = K // bk
    grid = (M // bm, N // bn, nk)
    if ta:
        a_blk, a_map = _vblock(a.shape, bk, bm, lambda i, j, k: (k, i))
    else:
        a_blk, a_map = _vblock(a.shape, bm, bk, lambda i, j, k: (i, k))
    if tb:
        b_blk, b_map = _vblock(b.shape, bn, bk, lambda i, j, k: (j, k))
    else:
        b_blk, b_map = _vblock(b.shape, bk, bn, lambda i, j, k: (k, j))
    dn = (((0 if ta else 1,), (1 if tb else 0,)), ((), ()))
    ins = [(a, a_blk, a_map), (b, b_blk, b_map)] + list(epi_ins)
    out_list = []
    for shape, dtype, cols in outs:
        cols = cols or bn
        blk, imap = _vblock(shape, bm, cols, lambda i, j, k: (i, j))
        out_list.append((shape, dtype, blk, imap))
    n_e, n_o = len(epi_ins), len(out_list)

    def body(*refs):
        a_ref, b_ref = refs[0], refs[1]
        e_refs = refs[2:2 + n_e]
        o_refs = refs[2 + n_e:2 + n_e + n_o]

        def finish(acc):
            res = epi(acc, *e_refs) if epi is not None else (acc,)
            for o_ref, r in zip(o_refs, res):
                o_ref[...] = r.astype(o_ref.dtype)

        x = a_ref[...].astype(BF)
        y = b_ref[...].astype(BF)
        p = lax.dot_general(x, y, dn, preferred_element_type=F32)
        if nk == 1:
            finish(p)
        else:
            acc_ref = refs[-1]
            k = pl.program_id(2)

            @pl.when(k == 0)
            def _():
                acc_ref[...] = p

            @pl.when(k > 0)
            def _():
                acc_ref[...] += p

            @pl.when(k == nk - 1)
            def _():
                finish(acc_ref[...])

    scratch = [pltpu.VMEM((bm, bn), F32)] if nk > 1 else []
    res = _pcall(body, name=name, grid=grid, ins=ins, outs=out_list, scratch=scratch,
                 semantics=("parallel", "parallel", "arbitrary"))
    return res[0] if len(res) == 1 else res


_GELU_K = float(np.sqrt(2.0 / np.pi))
_GELU_C = 0.044715


def _gelu(x):
    t = jnp.tanh(_GELU_K * (x + _GELU_C * (x * x * x)))
    return 0.5 * x * (1.0 + t)


def _gelu_grad(x):
    t = jnp.tanh(_GELU_K * (x + _GELU_C * (x * x * x)))
    return 0.5 * (1.0 + t) + 0.5 * x * (1.0 - t * t) * (_GELU_K * (1.0 + 3.0 * _GELU_C * (x * x)))


def _rstd(x):
    return lax.rsqrt(jnp.mean(x * x, axis=-1, keepdims=True) + EPS)


def _rms_bwd(x, gain, dy):
    r = _rstd(x)
    xh = x * r
    gdy = dy * gain
    dx = r * (gdy - xh * jnp.mean(gdy * xh, axis=-1, keepdims=True))
    return dx, dy * xh


def _rope_fwd(x, cos_t, sin_t):
    return x * cos_t + pltpu.roll(x, 2 * HALF_ROPE, 1) * sin_t


def _rope_bwd(dy, cos_t, sin_t):
    return dy * cos_t + pltpu.roll(dy * sin_t, 2 * HALF_ROPE, 1)


def _acc_rows(ref, val, first):
    s = jnp.sum(val, axis=0, keepdims=True)

    @pl.when(first)
    def _():
        ref[...] = s

    @pl.when(jnp.logical_not(first))
    def _():
        ref[...] += s


def rms_fwd(x, gain, *, name, col_block=0, width=None):
    T = x.shape[0]
    width = width or x.shape[1]
    tm = _tile(T, ROW_TILE, 8)

    def body(x_ref, g_ref, o_ref):
        v = x_ref[...]
        o_ref[...] = (v * _rstd(v) * g_ref[...]).astype(BF)

    return _pcall(body, name=name, grid=(T // tm,),
                  ins=[(x, (tm, width), lambda i: (i, col_block)), (gain, (1, width), lambda i: (0, 0))],
                  outs=[((T, width), BF, (tm, width), lambda i: (i, 0))], semantics=("parallel",))[0]


def rms_bwd(x, gain, dy, *, name, col_block=0, dres=None, want_f32=True, want_bf=True):
    T, width = dy.shape
    tm = _tile(T, ROW_TILE, 8)
    has_res = dres is not None

    def body(*refs):
        x_ref, g_ref, dy_ref = refs[:3]
        pos = 3
        res_ref = None
        if has_res:
            res_ref = refs[pos]
            pos += 1
        outs = refs[pos:]
        dx, dg_rows = _rms_bwd(x_ref[...], g_ref[...], dy_ref[...])
        if has_res:
            dx = dx + res_ref[...]
        o = 0
        if want_f32:
            outs[o][...] = dx
            o += 1
        if want_bf:
            outs[o][...] = dx.astype(BF)
            o += 1
        _acc_rows(outs[o], dg_rows, pl.program_id(0) == 0)

    ins = [(x, (tm, width), lambda i: (i, col_block)), (gain, (1, width), lambda i: (0, 0)),
           (dy, (tm, width), lambda i: (i, 0))]
    if has_res:
        ins.append((dres, (tm, width), lambda i: (i, 0)))
    outs = []
    if want_f32:
        outs.append(((T, width), F32, (tm, width), lambda i: (i, 0)))
    if want_bf:
        outs.append(((T, width), BF, (tm, width), lambda i: (i, 0)))
    outs.append(((1, width), F32, (1, width), lambda i: (0, 0)))
    return _pcall(body, name=name, grid=(T // tm,), ins=ins, outs=outs)


def mla_prep(proj, q_norm, kv_norm, cos_t, sin_t, *, name):
    T = proj.shape[0]
    tm = _tile(T, ROW_TILE, 8)
    kr_block = (proj.shape[1] - LANES) // LANES

    def body(cq_ref, ckv_ref, kr_ref, qg_ref, kg_ref, cos_ref, sin_ref, qn_ref, kvn_ref, krope_ref):
        cq = cq_ref[...]
        qn_ref[...] = (cq * _rstd(cq) * qg_ref[...]).astype(BF)
        ckv = ckv_ref[...]
        kvn_ref[...] = (ckv * _rstd(ckv) * kg_ref[...]).astype(BF)
        krope_ref[...] = _rope_fwd(kr_ref[...], cos_ref[...], sin_ref[...]).astype(BF)

    return _pcall(
        body, name=name, grid=(T // tm,),
        ins=[(proj, (tm, Q_LORA), lambda i: (i, 0)), (proj, (tm, KV_LORA), lambda i: (i, 1)),
             (proj, (tm, LANES), lambda i: (i, kr_block)),
             (q_norm, (1, Q_LORA), lambda i: (0, 0)), (kv_norm, (1, KV_LORA), lambda i: (0, 0)),
             (cos_t, (tm, LANES), lambda i: (i, 0)), (sin_t, (tm, LANES), lambda i: (i, 0))],
        outs=[((T, Q_LORA), BF, (tm, Q_LORA), lambda i: (i, 0)), ((T, KV_LORA), BF, (tm, KV_LORA), lambda i: (i, 0)),
              ((T, LANES), BF, (tm, LANES), lambda i: (i, 0))],
        semantics=("parallel",))


def _attn_probs(q_ref, k_ref, tq, T):
    s = lax.dot_general(q_ref[...], k_ref[...], (((1,), (1,)), ((), ())), preferred_element_type=F32) * ATTN_SCALE
    row = pl.program_id(1) * tq + lax.broadcasted_iota(jnp.int32, (tq, T), 0)
    col = lax.broadcasted_iota(jnp.int32, (tq, T), 1)
    s = jnp.where(col <= row, s, -jnp.inf)
    e = jnp.exp(s - jnp.max(s, axis=-1, keepdims=True))
    return e / jnp.sum(e, axis=-1, keepdims=True)


def attn_fwd(q, k, v, *, name):
    T = q.shape[0]
    tq = _tile(T, ROW_TILE, 8)

    def body(q_ref, k_ref, v_ref, o_ref):
        p = _attn_probs(q_ref, k_ref, tq, T)
        o_ref[...] = jnp.dot(p.astype(BF), v_ref[...], preferred_element_type=F32)

    return _pcall(
        body, name=name, grid=(HEADS, T // tq),
        ins=[(q, (tq, HEAD_PAD), lambda h, i: (i, h)), (k, (T, HEAD_PAD), lambda h, i: (0, h)),
             (v, (T, V_HEAD), lambda h, i: (0, h))],
        outs=[((T, MLA_OUT), F32, (tq, V_HEAD), lambda h, i: (i, h))], semantics=("parallel", "parallel"))[0]


def attn_bwd(q, k, v, do, *, name):
    T = q.shape[0]
    tq = _tile(T, ROW_TILE, 8)

    def body(q_ref, k_ref, v_ref, do_ref, dq_ref, dk_ref, dv_ref):
        p = _attn_probs(q_ref, k_ref, tq, T)
        do_t = do_ref[...]
        dp = lax.dot_general(do_t, v_ref[...], (((1,), (1,)), ((), ())), preferred_element_type=F32)
        ds = (p * (dp - jnp.sum(p * dp, axis=-1, keepdims=True)) * ATTN_SCALE).astype(BF)
        dq_ref[...] = jnp.dot(ds, k_ref[...], preferred_element_type=F32)
        dk_t = lax.dot_general(ds, q_ref[...], (((0,), (0,)), ((), ())), preferred_element_type=F32)
        dv_t = lax.dot_general(p.astype(BF), do_t, (((0,), (0,)), ((), ())), preferred_element_type=F32)
        first = pl.program_id(1) == 0

        @pl.when(first)
        def _():
            dk_ref[...] = dk_t
            dv_ref[...] = dv_t

        @pl.when(jnp.logical_not(first))
        def _():
            dk_ref[...] += dk_t
            dv_ref[...] += dv_t

    return _pcall(
        body, name=name, grid=(HEADS, T // tq),
        ins=[(q, (tq, HEAD_PAD), lambda h, i: (i, h)), (k, (T, HEAD_PAD), lambda h, i: (0, h)),
             (v, (T, V_HEAD), lambda h, i: (0, h)), (do, (tq, V_HEAD), lambda h, i: (i, h))],
        outs=[((T, HEADS * HEAD_PAD), F32, (tq, HEAD_PAD), lambda h, i: (i, h)),
              ((T, HEADS * HEAD_PAD), F32, (T, HEAD_PAD), lambda h, i: (0, h)),
              ((T, MLA_OUT), F32, (T, V_HEAD), lambda h, i: (0, h))],
        semantics=("parallel", "arbitrary"))


def mla_bwd_prep(dq, dk, dv, cos_t, sin_t, *, name):
    T = dq.shape[0]
    tm = _tile(T, ROW_TILE, 8)

    def body(dq_ref, dk_ref, dv_ref, cos_ref, sin_ref, dql_ref, dkvl_ref, dkr_ref):
        cos_v, sin_v = cos_ref[...], sin_ref[...]
        kr = jnp.zeros((tm, LANES), F32)
        for h in range(HEADS):
            lo = h * HEAD_PAD
            dql_ref[:, lo:lo + QK_NOPE] = dq_ref[:, lo:lo + QK_NOPE].astype(BF)
            dql_ref[:, lo + QK_NOPE:lo + HEAD_PAD] = _rope_bwd(
                dq_ref[:, lo + QK_NOPE:lo + HEAD_PAD], cos_v, sin_v).astype(BF)
            dkvl_ref[:, lo:lo + QK_NOPE] = dk_ref[:, lo:lo + QK_NOPE].astype(BF)
            dkvl_ref[:, lo + QK_NOPE:lo + HEAD_PAD] = dv_ref[:, h * V_HEAD:(h + 1) * V_HEAD].astype(BF)
            kr = kr + dk_ref[:, lo + QK_NOPE:lo + HEAD_PAD]
        dkr_ref[...] = _rope_bwd(kr, cos_v, sin_v).astype(BF)

    W = HEADS * HEAD_PAD
    return _pcall(
        body, name=name, grid=(T // tm,),
        ins=[(dq, (tm, W), lambda i: (i, 0)), (dk, (tm, W), lambda i: (i, 0)), (dv, (tm, MLA_OUT), lambda i: (i, 0)),
             (cos_t, (tm, LANES), lambda i: (i, 0)), (sin_t, (tm, LANES), lambda i: (i, 0))],
        outs=[((T, W), BF, (tm, W), lambda i: (i, 0)), ((T, W), BF, (tm, W), lambda i: (i, 0)),
              ((T, LANES), BF, (tm, LANES), lambda i: (i, 0))],
        semantics=("parallel",))


def _group_norm_stats(vg):
    mu = jnp.mean(vg, axis=-1, keepdims=True)
    d = vg - mu
    r = lax.rsqrt(jnp.mean(d * d, axis=-1, keepdims=True) + EPS)
    return d * r, r


def mix_fwd(a, proj, g_mla, g_sgu, v_gain, w_tril, b_full, *, name):
    T = a.shape[0]
    tm = _tile(T, ROW_TILE, CHUNK)
    n_chunk = tm // CHUNK

    def body(a_ref, u_ref, v_ref, gm_ref, gs_ref, vg_ref, w_ref, b_ref, o_ref, s_scr):
        av = a_ref[...]
        o_ref[:, :MLA_OUT] = (av * _rstd(av) * gm_ref[...]).astype(BF)
        for g in range(GROUPS):
            sl = slice(g * CH, (g + 1) * CH)
            vhat, _ = _group_norm_stats(_gelu(v_ref[:, sl]))
            vn = (vhat * vg_ref[:, sl]).astype(BF)
            u = _gelu(u_ref[:, sl])
            for ci in range(n_chunk):
                rs = slice(ci * CHUNK, (ci + 1) * CHUNK)
                y = jnp.dot(w_ref[g], vn[rs], preferred_element_type=F32) + b_ref[:, sl]
                s_scr[rs, sl] = u[rs] * y
        s = s_scr[...]
        o_ref[:, MLA_OUT:] = (s * _rstd(s) * gs_ref[...]).astype(BF)

    return _pcall(
        body, name=name, grid=(T // tm,),
        ins=[(a, (tm, MLA_OUT), lambda i: (i, 0)), (proj, (tm, SGU_OUT), lambda i: (i, 1)),
             (proj, (tm, SGU_OUT), lambda i: (i, 2)), (g_mla, (1, MLA_OUT), lambda i: (0, 0)),
             (g_sgu, (1, SGU_OUT), lambda i: (0, 0)), (v_gain, (1, SGU_OUT), lambda i: (0, 0)),
             (w_tril, (GROUPS, CHUNK, CHUNK), lambda i: (0, 0, 0)), (b_full, (CHUNK, SGU_OUT), lambda i: (0, 0))],
        outs=[((T, MLA_OUT + SGU_OUT), BF, (tm, MLA_OUT + SGU_OUT), lambda i: (i, 0))],
        scratch=[pltpu.VMEM((tm, SGU_OUT), F32)], semantics=("parallel",))[0]


def mix_bwd(dmixed, a, proj, g_mla, g_sgu, v_gain, w_tril, w_tril_t, b_full, *, name):
    T = a.shape[0]
    tm = _tile(T, ROW_TILE, CHUNK)
    n_chunk = tm // CHUNK

    def body(dm_a_ref, dm_s_ref, a_ref, u_ref, v_ref, gm_ref, gs_ref, vg_ref, w_ref, wt_ref, b_ref,
             da_ref, duv_ref, dgm_ref, dgs_ref, dvg_ref, dw_ref, db_ref, s_scr, y_scr):
        first = pl.program_id(0) == 0
        da, dgm_rows = _rms_bwd(a_ref[...], gm_ref[...], dm_a_ref[...])
        da_ref[...] = da.astype(BF)
        _acc_rows(dgm_ref, dgm_rows, first)

        for g in range(GROUPS):
            sl = slice(g * CH, (g + 1) * CH)
            vhat, _ = _group_norm_stats(_gelu(v_ref[:, sl]))
            vn = (vhat * vg_ref[:, sl]).astype(BF)
            u = _gelu(u_ref[:, sl])
            for ci in range(n_chunk):
                rs = slice(ci * CHUNK, (ci + 1) * CHUNK)
                y = jnp.dot(w_ref[g], vn[rs], preferred_element_type=F32) + b_ref[:, sl]
                y_scr[rs, sl] = y
                s_scr[rs, sl] = u[rs] * y
        ds, dgs_rows = _rms_bwd(s_scr[...], gs_ref[...], dm_s_ref[...])
        _acc_rows(dgs_ref, dgs_rows, first)
        s_scr[...] = ds

        @pl.when(first)
        def _():
            dw_ref[...] = jnp.zeros_like(dw_ref)
            db_ref[...] = jnp.zeros_like(db_ref)

        for g in range(GROUPS):
            sl = slice(g * CH, (g + 1) * CH)
            upre = u_ref[:, sl]
            vpre = v_ref[:, sl]
            u = _gelu(upre)
            vhat, r = _group_norm_stats(_gelu(vpre))
            gain = vg_ref[:, sl]
            vn = (vhat * gain).astype(BF)
            dsg = s_scr[:, sl]
            duv_ref[:, sl] = (dsg * y_scr[:, sl] * _gelu_grad(upre)).astype(BF)
            dy = dsg * u
            dyb = dy.astype(BF)
            dvn_parts = []
            for ci in range(n_chunk):
                rs = slice(ci * CHUNK, (ci + 1) * CHUNK)
                dvn_parts.append(jnp.dot(wt_ref[g], dyb[rs], preferred_element_type=F32))
                dw_ref[g] += lax.dot_general(dyb[rs], vn[rs], (((1,), (1,)), ((), ())), preferred_element_type=F32)
                db_ref[:, sl] += jnp.broadcast_to(jnp.sum(dy[rs], axis=-1, keepdims=True), (CHUNK, CH))
            dvn = dvn_parts[0] if n_chunk == 1 else jnp.concatenate(dvn_parts, axis=0)
            _acc_rows(dvg_ref.at[:, sl], dvn * vhat, first)
            dvh = dvn * gain
            dvg = r * (dvh - jnp.mean(dvh, axis=-1, keepdims=True)
                       - vhat * jnp.mean(dvh * vhat, axis=-1, keepdims=True))
            duv_ref[:, SGU_OUT + g * CH:SGU_OUT + (g + 1) * CH] = (dvg * _gelu_grad(vpre)).astype(BF)

    return _pcall(
        body, name=name, grid=(T // tm,),
        ins=[(dmixed, (tm, MLA_OUT), lambda i: (i, 0)), (dmixed, (tm, SGU_OUT), lambda i: (i, 1)),
             (a, (tm, MLA_OUT), lambda i: (i, 0)), (proj, (tm, SGU_OUT), lambda i: (i, 1)),
             (proj, (tm, SGU_OUT), lambda i: (i, 2)), (g_mla, (1, MLA_OUT), lambda i: (0, 0)),
             (g_sgu, (1, SGU_OUT), lambda i: (0, 0)), (v_gain, (1, SGU_OUT), lambda i: (0, 0)),
             (w_tril, (GROUPS, CHUNK, CHUNK), lambda i: (0, 0, 0)), (w_tril_t, (GROUPS, CHUNK, CHUNK), lambda i: (0, 0, 0)),
             (b_full, (CHUNK, SGU_OUT), lambda i: (0, 0))],
        outs=[((T, MLA_OUT), BF, (tm, MLA_OUT), lambda i: (i, 0)),
              ((T, 2 * SGU_OUT), BF, (tm, 2 * SGU_OUT), lambda i: (i, 0)),
              ((1, MLA_OUT), F32, (1, MLA_OUT), lambda i: (0, 0)), ((1, SGU_OUT), F32, (1, SGU_OUT), lambda i: (0, 0)),
              ((1, SGU_OUT), F32, (1, SGU_OUT), lambda i: (0, 0)),
              ((GROUPS, CHUNK, CHUNK), F32, (GROUPS, CHUNK, CHUNK), lambda i: (0, 0, 0)),
              ((CHUNK, SGU_OUT), F32, (CHUNK, SGU_OUT), lambda i: (0, 0))],
        scratch=[pltpu.VMEM((tm, SGU_OUT), F32), pltpu.VMEM((tm, SGU_OUT), F32)])


def _shift_down(z, n, row):
    return jnp.where(row >= n, pltpu.roll(z, n, 0), 0.0)


def _shift_up(z, n, row, T):
    return jnp.where(row < T - n, pltpu.roll(z, T - n, 0), 0.0)


def conv_fwd(proj, conv_w, *, name):
    T, D3 = proj.shape
    D = D3 // 3
    tn = _tile(D, 256)
    nj = D // tn

    def body(b_ref, c_ref, x_ref, w_ref, o_ref):
        row = lax.broadcasted_iota(jnp.int32, (T, tn), 0)
        z = c_ref[...] * x_ref[...]
        zc = w_ref[2:3, :] * z + w_ref[1:2, :] * _shift_down(z, 1, row) + w_ref[0:1, :] * _shift_down(z, 2, row)
        o_ref[...] = (b_ref[...] * zc).astype(BF)

    return _pcall(
        body, name=name, grid=(nj,),
        ins=[(proj, (T, tn), lambda j: (0, j)), (proj, (T, tn), lambda j: (0, nj + j)),
             (proj, (T, tn), lambda j: (0, 2 * nj + j)), (conv_w, (3, tn), lambda j: (0, j))],
        outs=[((T, D), BF, (T, tn), lambda j: (0, j))], semantics=("parallel",))[0]


def conv_bwd(dg, proj, conv_w, *, name):
    T, D3 = proj.shape
    D = D3 // 3
    tn = _tile(D, 256)
    nj = D // tn

    def body(dg_ref, b_ref, c_ref, x_ref, w_ref, dp_ref, dw_ref, dc_scr, dx_scr):
        part = pl.program_id(1)

        @pl.when(part == 0)
        def _():
            row = lax.broadcasted_iota(jnp.int32, (T, tn), 0)
            c, x = c_ref[...], x_ref[...]
            z = c * x
            z1 = _shift_down(z, 1, row)
            z2 = _shift_down(z, 2, row)
            dgv = dg_ref[...]
            zc = w_ref[2:3, :] * z + w_ref[1:2, :] * z1 + w_ref[0:1, :] * z2
            dp_ref[...] = (dgv * zc).astype(BF)
            dzc = dgv * b_ref[...]
            dw_ref[0:1, :] = jnp.sum(dzc * z2, axis=0, keepdims=True)
            dw_ref[1:2, :] = jnp.sum(dzc * z1, axis=0, keepdims=True)
            dw_ref[2:3, :] = jnp.sum(dzc * z, axis=0, keepdims=True)
            dz = (w_ref[2:3, :] * dzc + w_ref[1:2, :] * _shift_up(dzc, 1, row, T)
                  + w_ref[0:1, :] * _shift_up(dzc, 2, row, T))
            dc_scr[...] = (dz * x).astype(BF)
            dx_scr[...] = (dz * c).astype(BF)

        @pl.when(part == 1)
        def _():
            dp_ref[...] = dc_scr[...]

        @pl.when(part == 2)
        def _():
            dp_ref[...] = dx_scr[...]

    return _pcall(
        body, name=name, grid=(nj, 3),
        ins=[(dg, (T, tn), lambda j, p: (0, j)), (proj, (T, tn), lambda j, p: (0, j)),
             (proj, (T, tn), lambda j, p: (0, nj + j)), (proj, (T, tn), lambda j, p: (0, 2 * nj + j)),
             (conv_w, (3, tn), lambda j, p: (0, j))],
        outs=[((T, D3), BF, (T, tn), lambda j, p: (0, p * nj + j)), ((3, D), F32, (3, tn), lambda j, p: (0, j))],
        scratch=[pltpu.VMEM((T, tn), BF), pltpu.VMEM((T, tn), BF)], semantics=("parallel", "arbitrary"))


def loss_bwd(x, gain, target, *, name):
    T, D = x.shape
    tm = _tile(T, ROW_TILE, 8)

    def body(x_ref, g_ref, t_ref, dx_ref, dxb_ref, dg_ref, loss_ref):
        first = pl.program_id(0) == 0
        xv = x_ref[...]
        r = _rstd(xv)
        xh = xv * r
        gain_v = g_ref[...]
        err = xh * gain_v - t_ref[...]
        part = 0.5 * jnp.sum(jnp.mean(err * err, axis=-1, keepdims=True), axis=0, keepdims=True)
        _acc_rows(loss_ref, jnp.broadcast_to(part, (1, LANES)), first)
        dy = err * (1.0 / D)
        gdy = dy * gain_v
        dx = r * (gdy - xh * jnp.mean(gdy * xh, axis=-1, keepdims=True))
        dx_ref[...] = dx
        dxb_ref[...] = dx.astype(BF)
        _acc_rows(dg_ref, dy * xh, first)

    return _pcall(
        body, name=name, grid=(T // tm,),
        ins=[(x, (tm, D), lambda i: (i, 0)), (gain, (1, D), lambda i: (0, 0)), (target, (tm, D), lambda i: (i, 0))],
        outs=[((T, D), F32, (tm, D), lambda i: (i, 0)), ((T, D), BF, (tm, D), lambda i: (i, 0)),
              ((1, D), F32, (1, D), lambda i: (0, 0)), ((1, LANES), F32, (1, LANES), lambda i: (0, 0))])


def _adamw(g, w, m, v):
    m = ADAM_B1 * m + (1.0 - ADAM_B1) * g
    v = ADAM_B2 * v + (1.0 - ADAM_B2) * (g * g)
    m_hat = m / ADAM_C1
    v_hat = v / ADAM_C2
    delta = -ADAM_LR * (m_hat / (jnp.sqrt(v_hat) + ADAM_EPS) + ADAM_WD * w)
    return delta, m, v


def adam_flat(g, w, m, v, *, name):
    def body(g_ref, w_ref, m_ref, v_ref, d_ref, nm_ref, nv_ref):
        d, nm, nv = _adamw(g_ref[...], w_ref[...], m_ref[...], v_ref[...])
        d_ref[...] = d
        nm_ref[...] = nm
        nv_ref[...] = nv

    blk = g.shape
    zero = lambda: (0, 0)
    return _pcall(body, name=name, grid=(),
                  ins=[(t, blk, zero) for t in (g, w, m, v)],
                  outs=[(blk, F32, blk, zero)] * 3)


def _chip_slots():
    x, y, c = lax.axis_index("x"), lax.axis_index("y"), lax.axis_index("c")
    chips = [(1 - x, y), (x, 1 - y), (1 - x, 1 - y)]
    return x, y, c, chips


def reduce_adam(gs, a_buf, b_buf, w, m, v, layer, prev, *, name):
    L, R, C = w.shape
    tr = _tile(R, 256, 8)
    x, y, c, _ = _chip_slots()
    idx = jnp.stack([4 * x + 2 * y + c, 2 * x + y]).astype(jnp.int32)
    n_prev = 0 if prev is None else 4

    def body(idx_ref, g_ref, a_ref, b0_ref, b1_ref, b2_ref, w_ref, m_ref, v_ref, *rest):
        outs = rest[n_prev:]
        g = ((((g_ref[...].astype(F32) + a_ref[...].astype(F32)) + b0_ref[...].astype(F32))
              + b1_ref[...].astype(F32)) + b2_ref[...].astype(F32))
        d, nm, nv = _adamw(g, w_ref[...], m_ref[...], v_ref[...])
        outs[0][...] = g
        outs[1][...] = d
        outs[2][...] = nm
        outs[3][...] = nv

    blk3 = (None, tr, C)
    ins = [(gs, blk3, lambda i, s: (s[0], i, 0)), (a_buf, blk3, lambda i, s: (s[1], i, 0)),
           (b_buf, blk3, lambda i, s: (0, i, 0)), (b_buf, blk3, lambda i, s: (1, i, 0)),
           (b_buf, blk3, lambda i, s: (2, i, 0)),
           (w, blk3, lambda i, s: (layer, i, 0)), (m, blk3, lambda i, s: (layer, i, 0)),
           (v, blk3, lambda i, s: (layer, i, 0))]
    aliases = {}
    if prev is not None:
        for o, p in enumerate(prev):
            ins.append((p, None, None))
            aliases[1 + 8 + o] = o
    outs = [((L, R, C), F32, blk3, lambda i, s: (layer, i, 0))] * 4
    return _pcall(body, name=name, grid=(R // tr,), ins=ins, outs=outs, prefetch=idx, aliases=aliases,
                  semantics=("parallel",))


def pair_sum(gs, a_buf, *, name):
    _, R, C = gs.shape
    tr = _tile(R, 256, 8)
    x, y, c, chips = _chip_slots()
    idx = jnp.stack([4 * cx + 2 * cy + c for cx, cy in chips] + [2 * cx + cy for cx, cy in chips]).astype(jnp.int32)

    def body(idx_ref, g_ref, a_ref, o_ref):
        o_ref[...] = (g_ref[...].astype(F32) + a_ref[...].astype(F32)).astype(BF)

    blk3 = (None, tr, C)
    return _pcall(body, name=name, grid=(3, R // tr),
                  ins=[(gs, blk3, lambda j, i, s: (s[j], i, 0)), (a_buf, blk3, lambda j, i, s: (s[3 + j], i, 0))],
                  outs=[((3, R, C), BF, blk3, lambda j, i, s: (j, i, 0))], prefetch=idx,
                  semantics=("parallel", "parallel"))[0]


def sum_rows8(gathered, rows, *, name):
    W = gathered.shape[1]

    def body(g_ref, o_ref):
        acc = g_ref[0:rows, :]
        for d in range(1, N_DEV):
            acc = acc + g_ref[d * rows:(d + 1) * rows, :]
        o_ref[...] = acc

    return _pcall(body, name=name, grid=(), ins=[(gathered, gathered.shape, lambda: (0, 0))],
                  outs=[((rows, W), F32, (rows, W), lambda: (0, 0))])[0]


def _comm_call(body, *, name, ins, out_shapes, n_sem_rows, n_sem_cols, extra_scratch=()):
    any_spec = pl.BlockSpec(memory_space=pl.ANY)
    return pl.pallas_call(
        body, name=name, out_shape=[jax.ShapeDtypeStruct(s, d) for s, d in out_shapes],
        in_specs=[any_spec] * len(ins), out_specs=[any_spec] * len(out_shapes),
        scratch_shapes=[pltpu.SemaphoreType.DMA((n_sem_rows, n_sem_cols)),
                        pltpu.SemaphoreType.DMA((n_sem_rows, n_sem_cols))] + list(extra_scratch),
    )(*ins)


def all_gather_hbm(shards, *, name):
    n = len(shards)

    def body(*refs):
        src, dst = refs[:n], refs[n:2 * n]
        send_sems, recv_sems, local_sems = refs[2 * n:]
        x, y, c, chips = _chip_slots()
        me, sibling = (x, y, c), (x, y, 1 - c)

        def slot(p):
            return 4 * p[0] + 2 * p[1] + p[2]

        def copy(t, k, block, to, from_src=False):
            return pltpu.make_async_remote_copy(
                src_ref=src[t] if from_src else dst[t].at[slot(block)], dst_ref=dst[t].at[slot(block)],
                send_sem=send_sems.at[t, k], recv_sem=recv_sems.at[t, k], device_id=to, device_id_type=MESH)

        mine = [pltpu.make_async_copy(src[t], dst[t].at[slot(me)], local_sems.at[t]) for t in range(n)]
        for cp in mine:
            cp.start()
        first = []
        for t in range(n):
            first.append(copy(t, 0, me, sibling, from_src=True))
            for j, chip in enumerate(chips):
                first.append(copy(t, 1 + j, me, (*chip, c), from_src=True))
        for cp in first:
            cp.start()
        passed = []
        for t in range(n):
            for j, chip in enumerate(chips):
                copy(t, 1 + j, (*chip, c), me).wait_recv()
                cp = copy(t, 4 + j, (*chip, c), sibling)
                cp.start()
                passed.append(cp)
        for t in range(n):
            copy(t, 0, sibling, me).wait_recv()
            for j, chip in enumerate(chips):
                copy(t, 4 + j, (*chip, 1 - c), me).wait_recv()
        for cp in first + passed:
            cp.wait_send()
        for cp in mine:
            cp.wait()

    return _comm_call(body, name=name, ins=list(shards),
                      out_shapes=[((N_DEV,) + s.shape, s.dtype) for s in shards], n_sem_rows=n, n_sem_cols=7,
                      extra_scratch=[pltpu.SemaphoreType.DMA((n,))])


def exchange_sibling(gs, *, name):
    n = len(gs)

    def body(*refs):
        src, dst = refs[:n], refs[n:2 * n]
        send_sems, recv_sems = refs[2 * n:]
        x, y, c, _ = _chip_slots()
        copies = []
        for t in range(n):
            for q in range(4):
                qx, qy = q // 2, q % 2
                copies.append(pltpu.make_async_remote_copy(
                    src_ref=src[t].at[4 * qx + 2 * qy + (1 - c)], dst_ref=dst[t].at[q],
                    send_sem=send_sems.at[t, q], recv_sem=recv_sems.at[t, q],
                    device_id=(x, y, 1 - c), device_id_type=MESH))
        for cp in copies:
            cp.start()
        for cp in copies:
            cp.wait()

    return _comm_call(body, name=name, ins=list(gs), out_shapes=[((4,) + g.shape[1:], g.dtype) for g in gs],
                      n_sem_rows=n, n_sem_cols=4)


def exchange_chips(rs, *, name):
    n = len(rs)

    def body(*refs):
        src, dst = refs[:n], refs[n:2 * n]
        send_sems, recv_sems = refs[2 * n:]
        x, y, c, chips = _chip_slots()
        copies = []
        for t in range(n):
            for j, chip in enumerate(chips):
                copies.append(pltpu.make_async_remote_copy(
                    src_ref=src[t].at[j], dst_ref=dst[t].at[j], send_sem=send_sems.at[t, j],
                    recv_sem=recv_sems.at[t, j], device_id=(*chip, c), device_id_type=MESH))
        for cp in copies:
            cp.start()
        for cp in copies:
            cp.wait()

    return _comm_call(body, name=name, ins=list(rs), out_shapes=[(r.shape, r.dtype) for r in rs],
                      n_sem_rows=n, n_sem_cols=3)


def all_gather_vmem(x_shard, *, name):
    m_per, n = x_shard.shape

    def body(x_ref, out_ref, send_sems, recv_sems, local_sem):
        x, y, c, chips = _chip_slots()
        me, sibling = (x, y, c), (x, y, 1 - c)

        def rows(px, py, pc):
            return out_ref.at[pl.ds((4 * px + 2 * py + pc) * m_per, m_per), :]

        def copy(k, block, to, src=None):
            return pltpu.make_async_remote_copy(
                src_ref=rows(*block) if src is None else src, dst_ref=rows(*block),
                send_sem=send_sems.at[k], recv_sem=recv_sems.at[k], device_id=to, device_id_type=MESH)

        mine = pltpu.make_async_copy(x_ref, rows(*me), local_sem)
        mine.start()
        first = [copy(0, me, sibling, src=x_ref)]
        first += [copy(1 + j, me, (*chip, c), src=x_ref) for j, chip in enumerate(chips)]
        for cp in first:
            cp.start()
        passed = [copy(4 + j, (*chip, c), sibling) for j, chip in enumerate(chips)]
        for j, chip in enumerate(chips):
            copy(1 + j, (*chip, c), me).wait_recv()
            passed[j].start()
        copy(0, sibling, me).wait_recv()
        for j, chip in enumerate(chips):
            copy(4 + j, (*chip, 1 - c), me).wait_recv()
        for cp in first + passed:
            cp.wait_send()
        mine.wait()

    vmem = pl.BlockSpec(memory_space=pltpu.VMEM)
    return pl.pallas_call(
        body, name=name, out_shape=jax.ShapeDtypeStruct((N_DEV * m_per, n), x_shard.dtype),
        in_specs=[vmem], out_specs=vmem,
        scratch_shapes=[pltpu.SemaphoreType.DMA((7,)), pltpu.SemaphoreType.DMA((7,)), pltpu.SemaphoreType.DMA],
        compiler_params=pltpu.CompilerParams(vmem_limit_bytes=int(min(
            VMEM_LIMIT_CAP, 2 * (N_DEV + 1) * m_per * n * x_shard.dtype.itemsize + 16 * 2 ** 20))),
    )(x_shard)


def _rope_slab(cols):
    z = jnp.zeros(cols.shape[:-1] + (HALF_ROPE,), cols.dtype)
    return jnp.concatenate([cols[..., :HALF_ROPE], z, cols[..., HALF_ROPE:], z], axis=-1)


def _rope_unslab(slab):
    return jnp.concatenate([slab[..., :HALF_ROPE], slab[..., 2 * HALF_ROPE:3 * HALF_ROPE]], axis=-1)


def _pack_w_in(w_g):
    s, d, c = w_g.shape
    w = jnp.transpose(w_g, (1, 0, 2)).reshape(d, s * c)
    c1, c2, c3 = Q_LORA, Q_LORA + KV_LORA, Q_LORA + KV_LORA + QK_ROPE
    return jnp.concatenate([w[:, :c2], w[:, c3:], _rope_slab(w[:, c2:c3])], axis=-1)


def _unpack_w_in_grad(dw):
    d = dw.shape[0]
    c2 = Q_LORA + KV_LORA
    uv = 2 * SGU_OUT
    g = jnp.concatenate([dw[:, :c2], _rope_unslab(dw[:, c2 + uv:]), dw[:, c2:c2 + uv]], axis=-1)
    return jnp.transpose(g.reshape(d, N_DEV, g.shape[1] // N_DEV), (1, 0, 2))


def _rope_tables(positions):
    inv_freq = ROPE_BASE ** (-jnp.arange(0, QK_ROPE, 2, dtype=F32) / QK_ROPE)
    ang = positions.astype(F32)[:, None] * inv_freq
    cos, sin = jnp.cos(ang), jnp.sin(ang)
    z = jnp.zeros_like(cos)
    return jnp.concatenate([cos, z, cos, z], axis=-1), jnp.concatenate([-sin, z, sin, z], axis=-1)


def _mlp_fwd(x, gain, w1, w2, tag):
    hn = rms_fwd(x, gain, name=f"mlp{tag}_norm")

    def act_epi(acc):
        a = jnp.maximum(acc, 0.0)
        return a, a * a

    T = x.shape[0]
    F = w1.shape[0] * w1.shape[2]
    a, act = mm(hn, w1, name=f"mlp{tag}_up", outs=[((T, F), BF, None), ((T, F), BF, None)], epi=act_epi)
    bn = _tile(x.shape[1], MM_TILE)
    bm = _tile(T, MM_TILE)
    x_out = mm(act, w2, name=f"mlp{tag}_down", out=(x.shape, F32), bm=bm, bn=bn,
               epi=lambda acc, r: (acc + r[...],), epi_ins=[(x, (bm, bn), lambda i, j, k: (i, j))])
    return x_out, (hn, a, act)


def _mlp_bwd(x_in, gain, w1, w2, saved, dx, dxb, tag):
    hn, a, act = saved
    T, D = x_in.shape
    F = a.shape[1]
    bm = _tile(T, MM_TILE)
    bn = _tile(F, min(MM_TILE, w1.shape[2]))
    dhid = mm(dxb, w2, tb=True, name=f"mlp{tag}_dhid", out=((T, F), BF), bm=bm, bn=bn,
              epi=lambda acc, a_ref: (2.0 * a_ref[...].astype(F32) * acc,),
              epi_ins=[(a, (bm, bn), lambda i, j, k: (i, j))])
    dw2 = mm(act, dxb, ta=True, name=f"mlp{tag}_dw2", out=((F, D), BF))
    dw1 = mm(hn, dhid, ta=True, name=f"mlp{tag}_dw1", out=(w1.shape, BF))
    dhn = mm(dhid, w1, tb=True, name=f"mlp{tag}_dhn", out=((T, D), F32))
    dx_in, dx_in_b, dgain = rms_bwd(x_in, gain, dhn, dres=dx, name=f"mlp{tag}_norm_bwd")
    return dx_in, dx_in_b, dgain, dw1, dw2.reshape(N_DEV, F // N_DEV, D)


def kernel(x, positions, e_norm_mix, e_w_in, e_q_norm, e_w_uq, e_kv_norm, e_w_ukv, e_v_norm, e_sgu_w, e_sgu_b, e_mla_out_norm, e_sgu_out_norm, e_w_out, o_norm_mix, o_w_in, o_conv_w, o_w_out, mlp_norm, mlp_w1, mlp_w2, final_norm, loss_target, m_e_norm_mix, m_e_w_in, m_e_q_norm, m_e_w_uq, m_e_kv_norm, m_e_w_ukv, m_e_v_norm, m_e_sgu_w, m_e_sgu_b, m_e_mla_out_norm, m_e_sgu_out_norm, m_e_w_out, m_o_norm_mix, m_o_w_in, m_o_conv_w, m_o_w_out, m_mlp_norm, m_mlp_w1, m_mlp_w2, m_final_norm, v_e_norm_mix, v_e_w_in, v_e_q_norm, v_e_w_uq, v_e_kv_norm, v_e_w_ukv, v_e_v_norm, v_e_sgu_w, v_e_sgu_b, v_e_mla_out_norm, v_e_sgu_out_norm, v_e_w_out, v_o_norm_mix, v_o_w_in, v_o_conv_w, v_o_w_out, v_mlp_norm, v_mlp_w1, v_mlp_w2, v_final_norm):
    T, D = x.shape[1], x.shape[2]
    d_shard = o_norm_mix.shape[1]
    x0 = x[0]
    target = loss_target[0]
    me = 4 * lax.axis_index("x") + 2 * lax.axis_index("y") + lax.axis_index("c")

    big_shards = [e_w_in[0], e_w_uq[0], e_w_ukv[0], e_w_out[0], o_w_in[0], o_w_out[0],
                  mlp_w1[0], mlp_w1[1], mlp_w2[0], mlp_w2[1]]
    (g_w_in, g_w_uq, g_w_ukv, g_w_out_e, g_w_in_o, g_w_out_o, g_w1_0, g_w1_1, g_w2_0, g_w2_1) = all_gather_hbm(
        [s.astype(BF) for s in big_shards], name="gather_weights")
    small_rows = jnp.concatenate([o_norm_mix, o_conv_w[0], jnp.zeros((4, d_shard), F32)], axis=0)
    small_g = all_gather_vmem(small_rows, name="gather_small").reshape(N_DEV, 8, d_shard)
    o_norm_full = small_g[:, 0, :].reshape(1, D)
    conv_w_full = jnp.transpose(small_g[:, 1:4, :], (1, 0, 2)).reshape(3, D)

    w_in_e = _pack_w_in(g_w_in)
    w_uq = jnp.concatenate([g_w_uq[..., :QK_NOPE], _rope_slab(g_w_uq[..., QK_NOPE:])], axis=-1)
    w_ukv = g_w_ukv
    w_out_e = g_w_out_e.reshape(-1, D)
    w_out_o = g_w_out_o.reshape(-1, D)
    w1 = [g_w1_0, g_w1_1]
    w2 = [g_w2_0.reshape(-1, D), g_w2_1.reshape(-1, D)]
    w_tril = jnp.tril(e_sgu_w[0])
    w_tril_b = w_tril.astype(BF)
    w_tril_tb = jnp.swapaxes(w_tril, 1, 2).astype(BF)
    b_full = jnp.repeat(e_sgu_b[0].T, CH, axis=1)
    v_gain = e_v_norm[0].reshape(1, SGU_OUT)
    cos_t, sin_t = _rope_tables(positions[0])
    mlp_gain = [mlp_norm[0:1], mlp_norm[1:2]]
    final_gain = final_norm.reshape(1, D)

    h0 = rms_fwd(x0, e_norm_mix, name="e_norm")
    proj = mm(h0, w_in_e, name="e_in", out=((T, w_in_e.shape[1]), F32), bn=_tile(w_in_e.shape[1], 640))
    qn, kvn, krope = mla_prep(proj, e_q_norm, e_kv_norm, cos_t, sin_t, name="mla_prep")
    bm = _tile(T, MM_TILE)

    def q_epi(acc, cos_ref, sin_ref):
        return (jnp.concatenate([acc[:, :QK_NOPE], _rope_fwd(acc[:, QK_NOPE:], cos_ref[...], sin_ref[...])], axis=-1),)

    q = mm(qn, w_uq, name="mla_q", out=((T, HEADS * HEAD_PAD), BF), bm=bm, bn=HEAD_PAD, epi=q_epi,
           epi_ins=[(cos_t, (bm, LANES), lambda i, j, k: (i, 0)), (sin_t, (bm, LANES), lambda i, j, k: (i, 0))])

    def kv_epi(acc, kr_ref):
        return jnp.concatenate([acc[:, :QK_NOPE].astype(BF), kr_ref[...]], axis=-1), acc[:, QK_NOPE:]

    k, v = mm(kvn, w_ukv, name="mla_kv", bm=bm, bn=HEAD_PAD, epi=kv_epi,
              outs=[((T, HEADS * HEAD_PAD), BF, HEAD_PAD), ((T, MLA_OUT), BF, V_HEAD)],
              epi_ins=[(krope, (bm, LANES), lambda i, j, k: (i, 0))])
    attn = attn_fwd(q, k, v, name="attn_fwd")
    mixed = mix_fwd(attn, proj, e_mla_out_norm, e_sgu_out_norm, v_gain, w_tril_b, b_full, name="mix_fwd")
    bn = _tile(D, MM_TILE)
    x1 = mm(mixed, w_out_e, name="e_out", out=((T, D), F32), bm=bm, bn=bn,
            epi=lambda acc, r: (acc + r[...],), epi_ins=[(x0, (bm, bn), lambda i, j, k: (i, j))])
    x2, mlp0_saved = _mlp_fwd(x1, mlp_gain[0], w1[0], w2[0], 0)
    ho = rms_fwd(x2, o_norm_full, name="o_norm")
    proj_o = mm(ho, g_w_in_o, name="o_in", out=((T, 3 * D), F32))
    gated = conv_fwd(proj_o, conv_w_full, name="conv_fwd")
    x3 = mm(gated, w_out_o, name="o_out", out=((T, D), F32), bm=bm, bn=bn,
            epi=lambda acc, r: (acc + r[...],), epi_ins=[(x2, (bm, bn), lambda i, j, k: (i, j))])
    x4, mlp1_saved = _mlp_fwd(x3, mlp_gain[1], w1[1], w2[1], 1)

    dx4, dx4b, d_final, loss_part = loss_bwd(x4, final_gain, target, name="loss_bwd")
    loss = lax.psum(loss_part[0, 0], AXES)

    dx3, dx3b, d_mlp1, dw1_1, dw2_1 = _mlp_bwd(x3, mlp_gain[1], w1[1], w2[1], mlp1_saved, dx4, dx4b, 1)

    dgated = mm(dx3b, w_out_o, tb=True, name="o_out_dx", out=((T, D), F32))
    dw_out_o = mm(gated, dx3b, ta=True, name="o_out_dw", out=((D, D), BF))
    dproj_o, dconv_full = conv_bwd(dgated, proj_o, conv_w_full, name="conv_bwd")
    dw_in_o = mm(ho, dproj_o, ta=True, name="o_in_dw", out=(g_w_in_o.shape, BF))
    dho = mm(dproj_o, g_w_in_o, tb=True, name="o_in_dx", out=((T, D), F32))
    dx2, dx2b, d_onorm_full = rms_bwd(x2, o_norm_full, dho, dres=dx3, name="o_norm_bwd")

    dx1, dx1b, d_mlp0, dw1_0, dw2_0 = _mlp_bwd(x1, mlp_gain[0], w1[0], w2[0], mlp0_saved, dx2, dx2b, 0)

    dmixed = mm(dx1b, w_out_e, tb=True, name="e_out_dx", out=((T, MLA_OUT + SGU_OUT), F32))
    dw_out_e = mm(mixed, dx1b, ta=True, name="e_out_dw", out=(w_out_e.shape, BF))
    (dattn, duv, d_mla_out, d_sgu_out, d_vgain, d_sgu_w, d_b_full) = mix_bwd(
        dmixed, attn, proj, e_mla_out_norm, e_sgu_out_norm, v_gain, w_tril_b, w_tril_tb, b_full, name="mix_bwd")
    dq, dk, dv = attn_bwd(q, k, v, dattn, name="attn_bwd")
    dq_lin, dkv_lin, dkr = mla_bwd_prep(dq, dk, dv, cos_t, sin_t, name="mla_bwd_prep")
    dw_uq_pad = mm(qn, dq_lin, ta=True, name="mla_q_dw", out=(w_uq.shape, F32))
    dqn = mm(dq_lin, w_uq, tb=True, name="mla_q_dx", out=((T, Q_LORA), F32))
    dw_ukv = mm(kvn, dkv_lin, ta=True, name="mla_kv_dw", out=(w_ukv.shape, BF))
    dkvn = mm(dkv_lin, w_ukv, tb=True, name="mla_kv_dx", out=((T, KV_LORA), F32))
    dcq, d_qnorm = rms_bwd(proj, e_q_norm, dqn, col_block=0, want_f32=False, name="q_norm_bwd")
    dckv, d_kvnorm = rms_bwd(proj, e_kv_norm, dkvn, col_block=1, want_f32=False, name="kv_norm_bwd")
    dproj = jnp.concatenate([dcq, dckv, duv, dkr], axis=-1)
    dw_in_e_pad = mm(h0, dproj, ta=True, name="e_in_dw", out=(w_in_e.shape, F32), bn=_tile(w_in_e.shape[1], 640))
    dh0 = mm(dproj, w_in_e, tb=True, name="e_in_dx", out=((T, D), F32))
    grad_x, d_enorm = rms_bwd(x0, e_norm_mix, dh0, dres=dx1, want_bf=False, name="e_norm_bwd")

    dw_in_e = _unpack_w_in_grad(dw_in_e_pad).astype(BF)
    dw_uq = jnp.concatenate([dw_uq_pad[..., :QK_NOPE], _rope_unslab(dw_uq_pad[..., QK_NOPE:])], axis=-1).astype(BF)
    big_grads = [dw_in_e, dw_uq, dw_ukv, dw_out_e.reshape(g_w_out_e.shape), dw_in_o, dw_out_o.reshape(g_w_out_o.shape),
                 dw1_0, dw1_1, dw2_0, dw2_1]
    a_bufs = exchange_sibling(big_grads, name="reduce_sibling")
    pair = [pair_sum(g, a, name=f"pair_sum{t}") for t, (g, a) in enumerate(zip(big_grads, a_bufs))]
    b_bufs = exchange_chips(pair, name="reduce_chips")

    def finish(t, w, m, v, layer=0, prev=None, tag=""):
        return reduce_adam(big_grads[t], a_bufs[t], b_bufs[t], w, m, v, layer, prev, name=f"adam_{tag}")

    r_w_in = finish(0, e_w_in, m_e_w_in, v_e_w_in, tag="e_w_in")
    r_w_uq = finish(1, e_w_uq, m_e_w_uq, v_e_w_uq, tag="e_w_uq")
    r_w_ukv = finish(2, e_w_ukv, m_e_w_ukv, v_e_w_ukv, tag="e_w_ukv")
    r_w_out_e = finish(3, e_w_out, m_e_w_out, v_e_w_out, tag="e_w_out")
    r_w_in_o = finish(4, o_w_in, m_o_w_in, v_o_w_in, tag="o_w_in")
    r_w_out_o = finish(5, o_w_out, m_o_w_out, v_o_w_out, tag="o_w_out")
    r_w1 = finish(6, mlp_w1, m_mlp_w1, v_mlp_w1, 0, None, tag="w1_l0")
    r_w1 = finish(7, mlp_w1, m_mlp_w1, v_mlp_w1, 1, r_w1, tag="w1_l1")
    r_w2 = finish(8, mlp_w2, m_mlp_w2, v_mlp_w2, 0, None, tag="w2_l0")
    r_w2 = finish(9, mlp_w2, m_mlp_w2, v_mlp_w2, 1, r_w2, tag="w2_l1")

    d_sgu_b = jnp.transpose(d_b_full[:, ::CH])
    d_sgu_w_tril = jnp.tril(d_sgu_w)
    rep = [("e_norm_mix", e_norm_mix, m_e_norm_mix, v_e_norm_mix, d_enorm),
           ("e_q_norm", e_q_norm, m_e_q_norm, v_e_q_norm, d_qnorm),
           ("e_kv_norm", e_kv_norm, m_e_kv_norm, v_e_kv_norm, d_kvnorm),
           ("e_v_norm", e_v_norm, m_e_v_norm, v_e_v_norm, d_vgain),
           ("e_sgu_w", e_sgu_w, m_e_sgu_w, v_e_sgu_w, d_sgu_w_tril),
           ("e_sgu_b", e_sgu_b, m_e_sgu_b, v_e_sgu_b, d_sgu_b),
           ("e_mla_out_norm", e_mla_out_norm, m_e_mla_out_norm, v_e_mla_out_norm, d_mla_out),
           ("e_sgu_out_norm", e_sgu_out_norm, m_e_sgu_out_norm, v_e_sgu_out_norm, d_sgu_out),
           ("mlp_norm", mlp_norm, m_mlp_norm, v_mlp_norm, jnp.concatenate([d_mlp0, d_mlp1], axis=0)),
           ("final_norm", final_norm, m_final_norm, v_final_norm, d_final)]
    sizes = [int(np.prod(r[1].shape)) for r in rep]
    n_rep = sum(sizes)
    n_all = n_rep + 4 * D
    width = -(-n_all // (8 * LANES)) * LANES
    pad = 8 * width - n_all
    flat = jnp.concatenate([r[4].reshape(-1) for r in rep]
                           + [d_onorm_full.reshape(-1), dconv_full.reshape(-1), jnp.zeros((pad,), F32)])
    summed = sum_rows8(all_gather_vmem(flat.reshape(8, width), name="gather_small_grads"), 8,
                       name="sum_small_grads").reshape(-1)

    def pack_rep(i):
        return jnp.concatenate([r[i].reshape(-1) for r in rep]).reshape(n_rep // LANES, LANES)

    g_rep = summed[:n_rep].reshape(n_rep // LANES, LANES)
    d_rep, nm_rep, nv_rep = adam_flat(g_rep, pack_rep(1), pack_rep(2), pack_rep(3), name="adam_replicated")

    def unpack_rep(flat2d):
        out, off = {}, 0
        f = flat2d.reshape(-1)
        for r, n in zip(rep, sizes):
            out[r[0]] = f[off:off + n].reshape(r[1].shape)
            off += n
        return out

    small = {"grad": unpack_rep(g_rep), "delta": unpack_rep(d_rep), "new_m": unpack_rep(nm_rep),
             "new_v": unpack_rep(nv_rep)}
    g_onorm = lax.dynamic_slice(summed[n_rep:n_rep + D].reshape(1, D), (0, me * d_shard), (1, d_shard))
    g_conv = lax.dynamic_slice(summed[n_rep + D:n_rep + 4 * D].reshape(3, D), (0, me * d_shard), (3, d_shard))

    def pack_sharded(norm_part, conv_part):
        return jnp.concatenate([norm_part, conv_part, jnp.zeros((4, d_shard), F32)], axis=0)

    g_sh = pack_sharded(g_onorm, g_conv)
    d_sh, nm_sh, nv_sh = adam_flat(g_sh, pack_sharded(o_norm_mix, o_conv_w[0]), pack_sharded(m_o_norm_mix, m_o_conv_w[0]),
                                   pack_sharded(v_o_norm_mix, v_o_conv_w[0]), name="adam_sharded_small")
    for kind, arr in (("grad", g_sh), ("delta", d_sh), ("new_m", nm_sh), ("new_v", nv_sh)):
        small[kind]["o_norm_mix"] = arr[0:1]
        small[kind]["o_conv_w"] = arr[1:4][None]

    big = {"e_w_in": r_w_in, "e_w_uq": r_w_uq, "e_w_ukv": r_w_ukv, "e_w_out": r_w_out_e, "o_w_in": r_w_in_o,
           "o_w_out": r_w_out_o, "mlp_w1": r_w1, "mlp_w2": r_w2}
    order = ["e_norm_mix", "e_w_in", "e_q_norm", "e_w_uq", "e_kv_norm", "e_w_ukv", "e_v_norm", "e_sgu_w", "e_sgu_b",
             "e_mla_out_norm", "e_sgu_out_norm", "e_w_out", "o_norm_mix", "o_w_in", "o_conv_w", "o_w_out", "mlp_norm",
             "mlp_w1", "mlp_w2", "final_norm"]
    result = [loss, grad_x[None]]
    for ki, kind in enumerate(("grad", "delta", "new_m", "new_v")):
        for nm in order:
            result.append(big[nm][ki] if nm in big else small[kind][nm])
    return tuple(result)
```

```python
import functools

import numpy as np
import jax
import jax.numpy as jnp
from jax import lax
from jax.experimental import pallas as pl
from jax.experimental.pallas import tpu as pltpu

BF = jnp.bfloat16
F32 = jnp.float32
MESH = pl.DeviceIdType.MESH
AXES = ("x", "y", "c")
N_DEV = 8

EPS = 1e-6
HEADS = 8
Q_LORA = 512
KV_LORA = 512
QK_NOPE = 128
QK_ROPE = 64
HALF_ROPE = QK_ROPE // 2
V_HEAD = 128
HEAD_PAD = 256
ROPE_BASE = 10000.0
GROUPS = 8
CH = 128
CHUNK = 128
SGU_OUT = GROUPS * CH
MLA_OUT = HEADS * V_HEAD
ATTN_SCALE = float((QK_NOPE + QK_ROPE) ** -0.5)

ADAM_LR = 0.001
ADAM_B1 = 0.9
ADAM_B2 = 0.999
ADAM_EPS = 1e-08
ADAM_WD = 0.01
ADAM_STEP = 10
ADAM_C1 = 1.0 - ADAM_B1 ** ADAM_STEP
ADAM_C2 = 1.0 - ADAM_B2 ** ADAM_STEP

V7X_VMEM_BYTES = 64 * 2 ** 20
VMEM_LIMIT_CAP = V7X_VMEM_BYTES - 6 * 2 ** 20
LANES = 128
ROW_TILE = 256
MM_TILE = 1024
MM_K_TILE = 2048


def _padded_bytes(block, dtype):
    dims = [d for d in block if d is not None]
    if len(dims) >= 1:
        dims[-1] = -(-dims[-1] // LANES) * LANES
    if len(dims) >= 2:
        dims[-2] = -(-dims[-2] // 16) * 16
    return int(np.prod(dims)) * jnp.dtype(dtype).itemsize


def _pcall(body, *, name, grid, ins, outs, scratch=(), semantics=None, aliases=None, prefetch=None, deps=()):
    any_spec = pl.BlockSpec(memory_space=pl.ANY)
    if deps:
        n_lead = len(ins) + (1 if prefetch is not None else 0)
        n_deps = len(deps)
        inner = body

        def body(*refs):
            inner(*refs[:n_lead], *refs[n_lead + n_deps:])

        ins = list(ins) + [(d, None, None) for d in deps]
    in_specs = [any_spec if b is None else pl.BlockSpec(b, m) for _, b, m in ins]
    out_specs = [any_spec if b is None else pl.BlockSpec(b, m) for _, _, b, m in outs]
    out_shape = [jax.ShapeDtypeStruct(s, d) for s, d, _, _ in outs]
    est = 0
    for a, b, _ in ins:
        if b is not None:
            est += 2 * _padded_bytes(b, a.dtype)
    for _, d, b, _ in outs:
        if b is not None:
            est += 2 * _padded_bytes(b, d)
    for s in scratch:
        if hasattr(s, "shape") and hasattr(s, "dtype"):
            est += _padded_bytes(s.shape, s.dtype)
    limit = int(min(VMEM_LIMIT_CAP, est + 16 * 2 ** 20))
    params = pltpu.CompilerParams(
        dimension_semantics=semantics or ("arbitrary",) * len(grid), vmem_limit_bytes=limit)
    args = [a for a, _, _ in ins]
    if prefetch is not None:
        grid_spec = pltpu.PrefetchScalarGridSpec(
            num_scalar_prefetch=1, grid=grid, in_specs=in_specs, out_specs=out_specs, scratch_shapes=list(scratch))
        call = pl.pallas_call(body, out_shape=out_shape, grid_spec=grid_spec, name=name, compiler_params=params,
                              input_output_aliases=aliases or {})
        return call(prefetch, *args)
    call = pl.pallas_call(body, out_shape=out_shape, grid=grid, in_specs=in_specs, out_specs=out_specs,
                          scratch_shapes=list(scratch), name=name, compiler_params=params,
                          input_output_aliases=aliases or {})
    return call(*args)


def _tile(dim, pref, quantum=LANES):
    if dim <= pref:
        return dim
    t = (pref // quantum) * quantum
    while t >= quantum:
        if dim % t == 0:
            return t
        t -= quantum
    return dim


def _vshape(arr_shape):
    if len(arr_shape) == 2:
        return tuple(arr_shape)
    s, r, c = arr_shape
    return (r, s * c)


def _vblock(arr_shape, br, bc, rc):
    if len(arr_shape) == 2:
        return (br, bc), (lambda *g: rc(*g))
    _, _, c = arr_shape
    assert c % bc == 0, (arr_shape, bc)
    per = c // bc

    def imap(*g):
        ri, ci = rc(*g)
        return (ci // per, ri, ci % per)

    return (None, br, bc), imap


def _shard_width(*shapes):
    w = None
    for s in shapes:
        if len(s) == 3:
            w = s[2] if w is None else int(np.gcd(w, s[2]))
    return w


def mm(a, b, *, name, ta=False, tb=False, out=None, outs=None, epi=None, epi_ins=(), bm=None, bn=None, bk=None,
       deps=()):
    av, bv = _vshape(a.shape), _vshape(b.shape)
    M, K = (av[1], av[0]) if ta else av
    K2, N = (bv[1], bv[0]) if tb else bv
    assert K == K2, (a.shape, b.shape, ta, tb)
    if outs is None:
        outs = [(out[0], out[1], None)]
    a_sw = _shard_width(a.shape)
    b_sw = _shard_width(b.shape)
    o_sw = _shard_width(*[o[0] for o in outs])
    m_lim = a_sw if (ta and a_sw) else None
    k_lim = [w for w in ((a_sw if not ta else None), (b_sw if tb else None)) if w]
    n_lim = [w for w in ((b_sw if not tb else None), o_sw) if w]
    if bm is None:
        bm = _tile(M, min([MM_TILE] + ([m_lim] if m_lim else [])))
    if bn is None:
        bn = _tile(N, min([MM_TILE] + n_lim))
    if bk is None:
        bk = K if (K <= 4096 and not k_lim) else _tile(K, min([MM_K_TILE] + k_lim))
    assert M % bm == 0 and N % bn == 0 and K % bk == 0, (name, M, N, K, bm, bn, bk)
    nk = K // bk
    grid = (M // bm, N // bn, nk)
    if ta:
        a_blk, a_map = _vblock(a.shape, bk, bm, lambda i, j, k: (k, i))
    else:
        a_blk, a_map = _vblock(a.shape, bm, bk, lambda i, j, k: (i, k))
    if tb:
        b_blk, b_map = _vblock(b.shape, bn, bk, lambda i, j, k: (j, k))
    else:
        b_blk, b_map = _vblock(b.shape, bk, bn, lambda i, j, k: (k, j))
    dn = (((0 if ta else 1,), (1 if tb else 0,)), ((), ()))
    ins = [(a, a_blk, a_map), (b, b_blk, b_map)] + list(epi_ins)
    out_list = []
    for shape, dtype, cols in outs:
        cols = cols or bn
        blk, imap = _vblock(shape, bm, cols, lambda i, j, k: (i, j))
        out_list.append((shape, dtype, blk, imap))
    n_e, n_o = len(epi_ins), len(out_list)

    def body(*refs):
        a_ref, b_ref = refs[0], refs[1]
        e_refs = refs[2:2 + n_e]
        o_refs = refs[2 + n_e:2 + n_e + n_o]

        def finish(acc):
            res = epi(acc, *e_refs) if epi is not None else (acc,)
            for o_ref, r in zip(o_refs, res):
                o_ref[...] = r.astype(o_ref.dtype)

        x = a_ref[...].astype(BF)
        y = b_ref[...].astype(BF)
        p = lax.dot_general(x, y, dn, preferred_element_type=F32)
        if nk == 1:
            finish(p)
        else:
            acc_ref = refs[-1]
            k = pl.program_id(2)

            @pl.when(k == 0)
            def _():
                acc_ref[...] = p

            @pl.when(k > 0)
            def _():
                acc_ref[...] += p

            @pl.when(k == nk - 1)
            def _():
                finish(acc_ref[...])

    scratch = [pltpu.VMEM((bm, bn), F32)] if nk > 1 else []
    res = _pcall(body, name=name, grid=grid, ins=ins, outs=out_list, scratch=scratch,
                 semantics=("parallel", "parallel", "arbitrary"), deps=deps)
    return res[0] if len(res) == 1 else res


_GELU_K = float(np.sqrt(2.0 / np.pi))
_GELU_C = 0.044715


def _gelu(x):
    t = jnp.tanh(_GELU_K * (x + _GELU_C * (x * x * x)))
    return 0.5 * x * (1.0 + t)


def _gelu_grad(x):
    t = jnp.tanh(_GELU_K * (x + _GELU_C * (x * x * x)))
    return 0.5 * (1.0 + t) + 0.5 * x * (1.0 - t * t) * (_GELU_K * (1.0 + 3.0 * _GELU_C * (x * x)))


def _rstd(x):
    return lax.rsqrt(jnp.mean(x * x, axis=-1, keepdims=True) + EPS)


def _rms_bwd(x, gain, dy):
    r = _rstd(x)
    xh = x * r
    gdy = dy * gain
    dx = r * (gdy - xh * jnp.mean(gdy * xh, axis=-1, keepdims=True))
    return dx, dy * xh


def _rope_fwd(x, cos_t, sin_t):
    return x * cos_t + pltpu.roll(x, 2 * HALF_ROPE, 1) * sin_t


def _rope_bwd(dy, cos_t, sin_t):
    return dy * cos_t + pltpu.roll(dy * sin_t, 2 * HALF_ROPE, 1)


def _acc_rows(ref, val, first):
    s = jnp.sum(val, axis=0, keepdims=True)

    @pl.when(first)
    def _():
        ref[...] = s

    @pl.when(jnp.logical_not(first))
    def _():
        ref[...] += s


def rms_fwd(x, gain, *, name, col_block=0, width=None):
    T = x.shape[0]
    width = width or x.shape[1]
    tm = _tile(T, ROW_TILE, 8)

    def body(x_ref, g_ref, o_ref):
        v = x_ref[...]
        o_ref[...] = (v * _rstd(v) * g_ref[...]).astype(BF)

    return _pcall(body, name=name, grid=(T // tm,),
                  ins=[(x, (tm, width), lambda i: (i, col_block)), (gain, (1, width), lambda i: (0, 0))],
                  outs=[((T, width), BF, (tm, width), lambda i: (i, 0))], semantics=("parallel",))[0]


def rms_bwd(x, gain, dy, *, name, col_block=0, dres=None, want_f32=True, want_bf=True, deps=()):
    T, width = dy.shape
    tm = _tile(T, ROW_TILE, 8)
    has_res = dres is not None

    def body(*refs):
        x_ref, g_ref, dy_ref = refs[:3]
        pos = 3
        res_ref = None
        if has_res:
            res_ref = refs[pos]
            pos += 1
        outs = refs[pos:]
        dx, dg_rows = _rms_bwd(x_ref[...], g_ref[...], dy_ref[...])
        if has_res:
            dx = dx + res_ref[...]
        o = 0
        if want_f32:
            outs[o][...] = dx
            o += 1
        if want_bf:
            outs[o][...] = dx.astype(BF)
            o += 1
        _acc_rows(outs[o], dg_rows, pl.program_id(0) == 0)

    ins = [(x, (tm, width), lambda i: (i, col_block)), (gain, (1, width), lambda i: (0, 0)),
           (dy, (tm, width), lambda i: (i, 0))]
    if has_res:
        ins.append((dres, (tm, width), lambda i: (i, 0)))
    outs = []
    if want_f32:
        outs.append(((T, width), F32, (tm, width), lambda i: (i, 0)))
    if want_bf:
        outs.append(((T, width), BF, (tm, width), lambda i: (i, 0)))
    outs.append(((1, width), F32, (1, width), lambda i: (0, 0)))
    return _pcall(body, name=name, grid=(T // tm,), ins=ins, outs=outs, deps=deps)


def mla_prep(proj, q_norm, kv_norm, cos_t, sin_t, *, name):
    T = proj.shape[0]
    tm = _tile(T, ROW_TILE, 8)
    kr_block = (proj.shape[1] - LANES) // LANES

    def body(cq_ref, ckv_ref, kr_ref, qg_ref, kg_ref, cos_ref, sin_ref, qn_ref, kvn_ref, krope_ref):
        cq = cq_ref[...]
        qn_ref[...] = (cq * _rstd(cq) * qg_ref[...]).astype(BF)
        ckv = ckv_ref[...]
        kvn_ref[...] = (ckv * _rstd(ckv) * kg_ref[...]).astype(BF)
        krope_ref[...] = _rope_fwd(kr_ref[...], cos_ref[...], sin_ref[...]).astype(BF)

    return _pcall(
        body, name=name, grid=(T // tm,),
        ins=[(proj, (tm, Q_LORA), lambda i: (i, 0)), (proj, (tm, KV_LORA), lambda i: (i, 1)),
             (proj, (tm, LANES), lambda i: (i, kr_block)),
             (q_norm, (1, Q_LORA), lambda i: (0, 0)), (kv_norm, (1, KV_LORA), lambda i: (0, 0)),
             (cos_t, (tm, LANES), lambda i: (i, 0)), (sin_t, (tm, LANES), lambda i: (i, 0))],
        outs=[((T, Q_LORA), BF, (tm, Q_LORA), lambda i: (i, 0)), ((T, KV_LORA), BF, (tm, KV_LORA), lambda i: (i, 0)),
              ((T, LANES), BF, (tm, LANES), lambda i: (i, 0))],
        semantics=("parallel",))


def _attn_probs(q_ref, k_ref, tq, T):
    s = lax.dot_general(q_ref[...], k_ref[...], (((1,), (1,)), ((), ())), preferred_element_type=F32) * ATTN_SCALE
    row = pl.program_id(1) * tq + lax.broadcasted_iota(jnp.int32, (tq, T), 0)
    col = lax.broadcasted_iota(jnp.int32, (tq, T), 1)
    s = jnp.where(col <= row, s, -jnp.inf)
    e = jnp.exp(s - jnp.max(s, axis=-1, keepdims=True))
    return e / jnp.sum(e, axis=-1, keepdims=True)


def attn_fwd(q, k, v, *, name):
    T = q.shape[0]
    tq = _tile(T, ROW_TILE, 8)

    def body(q_ref, k_ref, v_ref, o_ref):
        p = _attn_probs(q_ref, k_ref, tq, T)
        o_ref[...] = jnp.dot(p.astype(BF), v_ref[...], preferred_element_type=F32)

    return _pcall(
        body, name=name, grid=(HEADS, T // tq),
        ins=[(q, (tq, HEAD_PAD), lambda h, i: (i, h)), (k, (T, HEAD_PAD), lambda h, i: (0, h)),
             (v, (T, V_HEAD), lambda h, i: (0, h))],
        outs=[((T, MLA_OUT), F32, (tq, V_HEAD), lambda h, i: (i, h))], semantics=("parallel", "parallel"))[0]


def attn_bwd(q, k, v, do, *, name):
    T = q.shape[0]
    tq = _tile(T, ROW_TILE, 8)

    def body(q_ref, k_ref, v_ref, do_ref, dq_ref, dk_ref, dv_ref):
        p = _attn_probs(q_ref, k_ref, tq, T)
        do_t = do_ref[...]
        dp = lax.dot_general(do_t, v_ref[...], (((1,), (1,)), ((), ())), preferred_element_type=F32)
        ds = (p * (dp - jnp.sum(p * dp, axis=-1, keepdims=True)) * ATTN_SCALE).astype(BF)
        dq_ref[...] = jnp.dot(ds, k_ref[...], preferred_element_type=F32)
        dk_t = lax.dot_general(ds, q_ref[...], (((0,), (0,)), ((), ())), preferred_element_type=F32)
        dv_t = lax.dot_general(p.astype(BF), do_t, (((0,), (0,)), ((), ())), preferred_element_type=F32)
        first = pl.program_id(1) == 0

        @pl.when(first)
        def _():
            dk_ref[...] = dk_t
            dv_ref[...] = dv_t

        @pl.when(jnp.logical_not(first))
        def _():
            dk_ref[...] += dk_t
            dv_ref[...] += dv_t

    return _pcall(
        body, name=name, grid=(HEADS, T // tq),
        ins=[(q, (tq, HEAD_PAD), lambda h, i: (i, h)), (k, (T, HEAD_PAD), lambda h, i: (0, h)),
             (v, (T, V_HEAD), lambda h, i: (0, h)), (do, (tq, V_HEAD), lambda h, i: (i, h))],
        outs=[((T, HEADS * HEAD_PAD), F32, (tq, HEAD_PAD), lambda h, i: (i, h)),
              ((T, HEADS * HEAD_PAD), F32, (T, HEAD_PAD), lambda h, i: (0, h)),
              ((T, MLA_OUT), F32, (T, V_HEAD), lambda h, i: (0, h))],
        semantics=("parallel", "arbitrary"))


def mla_bwd_prep(dq, dk, dv, cos_t, sin_t, *, name):
    T = dq.shape[0]
    tm = _tile(T, ROW_TILE, 8)

    def body(dq_ref, dk_ref, dv_ref, cos_ref, sin_ref, dql_ref, dkvl_ref, dkr_ref):
        cos_v, sin_v = cos_ref[...], sin_ref[...]
        kr = jnp.zeros((tm, LANES), F32)
        for h in range(HEADS):
            lo = h * HEAD_PAD
            dql_ref[:, lo:lo + QK_NOPE] = dq_ref[:, lo:lo + QK_NOPE].astype(BF)
            dql_ref[:, lo + QK_NOPE:lo + HEAD_PAD] = _rope_bwd(
                dq_ref[:, lo + QK_NOPE:lo + HEAD_PAD], cos_v, sin_v).astype(BF)
            dkvl_ref[:, lo:lo + QK_NOPE] = dk_ref[:, lo:lo + QK_NOPE].astype(BF)
            dkvl_ref[:, lo + QK_NOPE:lo + HEAD_PAD] = dv_ref[:, h * V_HEAD:(h + 1) * V_HEAD].astype(BF)
            kr = kr + dk_ref[:, lo + QK_NOPE:lo + HEAD_PAD]
        dkr_ref[...] = _rope_bwd(kr, cos_v, sin_v).astype(BF)

    W = HEADS * HEAD_PAD
    return _pcall(
        body, name=name, grid=(T // tm,),
        ins=[(dq, (tm, W), lambda i: (i, 0)), (dk, (tm, W), lambda i: (i, 0)), (dv, (tm, MLA_OUT), lambda i: (i, 0)),
             (cos_t, (tm, LANES), lambda i: (i, 0)), (sin_t, (tm, LANES), lambda i: (i, 0))],
        outs=[((T, W), BF, (tm, W), lambda i: (i, 0)), ((T, W), BF, (tm, W), lambda i: (i, 0)),
              ((T, LANES), BF, (tm, LANES), lambda i: (i, 0))],
        semantics=("parallel",))


def _group_norm_stats(vg):
    mu = jnp.mean(vg, axis=-1, keepdims=True)
    d = vg - mu
    r = lax.rsqrt(jnp.mean(d * d, axis=-1, keepdims=True) + EPS)
    return d * r, r


def mix_fwd(a, proj, g_mla, g_sgu, v_gain, w_tril, b_full, *, name):
    T = a.shape[0]
    tm = _tile(T, ROW_TILE, CHUNK)
    n_chunk = tm // CHUNK

    def body(a_ref, u_ref, v_ref, gm_ref, gs_ref, vg_ref, w_ref, b_ref, o_ref, s_scr):
        av = a_ref[...]
        o_ref[:, :MLA_OUT] = (av * _rstd(av) * gm_ref[...]).astype(BF)
        for g in range(GROUPS):
            sl = slice(g * CH, (g + 1) * CH)
            vhat, _ = _group_norm_stats(_gelu(v_ref[:, sl]))
            vn = (vhat * vg_ref[:, sl]).astype(BF)
            u = _gelu(u_ref[:, sl])
            for ci in range(n_chunk):
                rs = slice(ci * CHUNK, (ci + 1) * CHUNK)
                y = jnp.dot(w_ref[g], vn[rs], preferred_element_type=F32) + b_ref[:, sl]
                s_scr[rs, sl] = u[rs] * y
        s = s_scr[...]
        o_ref[:, MLA_OUT:] = (s * _rstd(s) * gs_ref[...]).astype(BF)

    return _pcall(
        body, name=name, grid=(T // tm,),
        ins=[(a, (tm, MLA_OUT), lambda i: (i, 0)), (proj, (tm, SGU_OUT), lambda i: (i, 1)),
             (proj, (tm, SGU_OUT), lambda i: (i, 2)), (g_mla, (1, MLA_OUT), lambda i: (0, 0)),
             (g_sgu, (1, SGU_OUT), lambda i: (0, 0)), (v_gain, (1, SGU_OUT), lambda i: (0, 0)),
             (w_tril, (GROUPS, CHUNK, CHUNK), lambda i: (0, 0, 0)), (b_full, (CHUNK, SGU_OUT), lambda i: (0, 0))],
        outs=[((T, MLA_OUT + SGU_OUT), BF, (tm, MLA_OUT + SGU_OUT), lambda i: (i, 0))],
        scratch=[pltpu.VMEM((tm, SGU_OUT), F32)], semantics=("parallel",))[0]


def mix_bwd(dmixed, a, proj, g_mla, g_sgu, v_gain, w_tril, w_tril_t, b_full, *, name):
    T = a.shape[0]
    tm = _tile(T, ROW_TILE, CHUNK)
    n_chunk = tm // CHUNK

    def body(dm_a_ref, dm_s_ref, a_ref, u_ref, v_ref, gm_ref, gs_ref, vg_ref, w_ref, wt_ref, b_ref,
             da_ref, duv_ref, dgm_ref, dgs_ref, dvg_ref, dw_ref, db_ref, s_scr, y_scr):
        first = pl.program_id(0) == 0
        da, dgm_rows = _rms_bwd(a_ref[...], gm_ref[...], dm_a_ref[...])
        da_ref[...] = da.astype(BF)
        _acc_rows(dgm_ref, dgm_rows, first)

        for g in range(GROUPS):
            sl = slice(g * CH, (g + 1) * CH)
            vhat, _ = _group_norm_stats(_gelu(v_ref[:, sl]))
            vn = (vhat * vg_ref[:, sl]).astype(BF)
            u = _gelu(u_ref[:, sl])
            for ci in range(n_chunk):
                rs = slice(ci * CHUNK, (ci + 1) * CHUNK)
                y = jnp.dot(w_ref[g], vn[rs], preferred_element_type=F32) + b_ref[:, sl]
                y_scr[rs, sl] = y
                s_scr[rs, sl] = u[rs] * y
        ds, dgs_rows = _rms_bwd(s_scr[...], gs_ref[...], dm_s_ref[...])
        _acc_rows(dgs_ref, dgs_rows, first)
        s_scr[...] = ds

        @pl.when(first)
        def _():
            dw_ref[...] = jnp.zeros_like(dw_ref)
            db_ref[...] = jnp.zeros_like(db_ref)

        for g in range(GROUPS):
            sl = slice(g * CH, (g + 1) * CH)
            upre = u_ref[:, sl]
            vpre = v_ref[:, sl]
            u = _gelu(upre)
            vhat, r = _group_norm_stats(_gelu(vpre))
            gain = vg_ref[:, sl]
            vn = (vhat * gain).astype(BF)
            dsg = s_scr[:, sl]
            duv_ref[:, sl] = (dsg * y_scr[:, sl] * _gelu_grad(upre)).astype(BF)
            dy = dsg * u
            dyb = dy.astype(BF)
            dvn_parts = []
            for ci in range(n_chunk):
                rs = slice(ci * CHUNK, (ci + 1) * CHUNK)
                dvn_parts.append(jnp.dot(wt_ref[g], dyb[rs], preferred_element_type=F32))
                dw_ref[g] += lax.dot_general(dyb[rs], vn[rs], (((1,), (1,)), ((), ())), preferred_element_type=F32)
                db_ref[:, sl] += jnp.broadcast_to(jnp.sum(dy[rs], axis=-1, keepdims=True), (CHUNK, CH))
            dvn = dvn_parts[0] if n_chunk == 1 else jnp.concatenate(dvn_parts, axis=0)
            _acc_rows(dvg_ref.at[:, sl], dvn * vhat, first)
            dvh = dvn * gain
            dvg = r * (dvh - jnp.mean(dvh, axis=-1, keepdims=True)
                       - vhat * jnp.mean(dvh * vhat, axis=-1, keepdims=True))
            duv_ref[:, SGU_OUT + g * CH:SGU_OUT + (g + 1) * CH] = (dvg * _gelu_grad(vpre)).astype(BF)

    return _pcall(
        body, name=name, grid=(T // tm,),
        ins=[(dmixed, (tm, MLA_OUT), lambda i: (i, 0)), (dmixed, (tm, SGU_OUT), lambda i: (i, 1)),
             (a, (tm, MLA_OUT), lambda i: (i, 0)), (proj, (tm, SGU_OUT), lambda i: (i, 1)),
             (proj, (tm, SGU_OUT), lambda i: (i, 2)), (g_mla, (1, MLA_OUT), lambda i: (0, 0)),
             (g_sgu, (1, SGU_OUT), lambda i: (0, 0)), (v_gain, (1, SGU_OUT), lambda i: (0, 0)),
             (w_tril, (GROUPS, CHUNK, CHUNK), lambda i: (0, 0, 0)), (w_tril_t, (GROUPS, CHUNK, CHUNK), lambda i: (0, 0, 0)),
             (b_full, (CHUNK, SGU_OUT), lambda i: (0, 0))],
        outs=[((T, MLA_OUT), BF, (tm, MLA_OUT), lambda i: (i, 0)),
              ((T, 2 * SGU_OUT), BF, (tm, 2 * SGU_OUT), lambda i: (i, 0)),
              ((1, MLA_OUT), F32, (1, MLA_OUT), lambda i: (0, 0)), ((1, SGU_OUT), F32, (1, SGU_OUT), lambda i: (0, 0)),
              ((1, SGU_OUT), F32, (1, SGU_OUT), lambda i: (0, 0)),
              ((GROUPS, CHUNK, CHUNK), F32, (GROUPS, CHUNK, CHUNK), lambda i: (0, 0, 0)),
              ((CHUNK, SGU_OUT), F32, (CHUNK, SGU_OUT), lambda i: (0, 0))],
        scratch=[pltpu.VMEM((tm, SGU_OUT), F32), pltpu.VMEM((tm, SGU_OUT), F32)])


def _shift_down(z, n, row):
    return jnp.where(row >= n, pltpu.roll(z, n, 0), 0.0)


def _shift_up(z, n, row, T):
    return jnp.where(row < T - n, pltpu.roll(z, T - n, 0), 0.0)


def conv_fwd(proj, conv_w, *, name):
    T, D3 = proj.shape
    D = D3 // 3
    tn = _tile(D, 256)
    nj = D // tn

    def body(b_ref, c_ref, x_ref, w_ref, o_ref):
        row = lax.broadcasted_iota(jnp.int32, (T, tn), 0)
        z = c_ref[...] * x_ref[...]
        zc = w_ref[2:3, :] * z + w_ref[1:2, :] * _shift_down(z, 1, row) + w_ref[0:1, :] * _shift_down(z, 2, row)
        o_ref[...] = (b_ref[...] * zc).astype(BF)

    return _pcall(
        body, name=name, grid=(nj,),
        ins=[(proj, (T, tn), lambda j: (0, j)), (proj, (T, tn), lambda j: (0, nj + j)),
             (proj, (T, tn), lambda j: (0, 2 * nj + j)), (conv_w, (3, tn), lambda j: (0, j))],
        outs=[((T, D), BF, (T, tn), lambda j: (0, j))], semantics=("parallel",))[0]


def conv_bwd(dg, proj, conv_w, *, name):
    T, D3 = proj.shape
    D = D3 // 3
    tn = _tile(D, 256)
    nj = D // tn

    def body(dg_ref, b_ref, c_ref, x_ref, w_ref, dp_ref, dw_ref, dc_scr, dx_scr):
        part = pl.program_id(1)

        @pl.when(part == 0)
        def _():
            row = lax.broadcasted_iota(jnp.int32, (T, tn), 0)
            c, x = c_ref[...], x_ref[...]
            z = c * x
            z1 = _shift_down(z, 1, row)
            z2 = _shift_down(z, 2, row)
            dgv = dg_ref[...]
            zc = w_ref[2:3, :] * z + w_ref[1:2, :] * z1 + w_ref[0:1, :] * z2
            dp_ref[...] = (dgv * zc).astype(BF)
            dzc = dgv * b_ref[...]
            dw_ref[0:1, :] = jnp.sum(dzc * z2, axis=0, keepdims=True)
            dw_ref[1:2, :] = jnp.sum(dzc * z1, axis=0, keepdims=True)
            dw_ref[2:3, :] = jnp.sum(dzc * z, axis=0, keepdims=True)
            dz = (w_ref[2:3, :] * dzc + w_ref[1:2, :] * _shift_up(dzc, 1, row, T)
                  + w_ref[0:1, :] * _shift_up(dzc, 2, row, T))
            dc_scr[...] = (dz * x).astype(BF)
            dx_scr[...] = (dz * c).astype(BF)

        @pl.when(part == 1)
        def _():
            dp_ref[...] = dc_scr[...]

        @pl.when(part == 2)
        def _():
            dp_ref[...] = dx_scr[...]

    return _pcall(
        body, name=name, grid=(nj, 3),
        ins=[(dg, (T, tn), lambda j, p: (0, j)), (proj, (T, tn), lambda j, p: (0, j)),
             (proj, (T, tn), lambda j, p: (0, nj + j)), (proj, (T, tn), lambda j, p: (0, 2 * nj + j)),
             (conv_w, (3, tn), lambda j, p: (0, j))],
        outs=[((T, D3), BF, (T, tn), lambda j, p: (0, p * nj + j)), ((3, D), F32, (3, tn), lambda j, p: (0, j))],
        scratch=[pltpu.VMEM((T, tn), BF), pltpu.VMEM((T, tn), BF)], semantics=("parallel", "arbitrary"))


def loss_bwd(x, gain, target, *, name):
    T, D = x.shape
    tm = _tile(T, ROW_TILE, 8)

    def body(x_ref, g_ref, t_ref, dx_ref, dxb_ref, dg_ref, loss_ref):
        first = pl.program_id(0) == 0
        xv = x_ref[...]
        r = _rstd(xv)
        xh = xv * r
        gain_v = g_ref[...]
        err = xh * gain_v - t_ref[...]
        part = 0.5 * jnp.sum(jnp.mean(err * err, axis=-1, keepdims=True), axis=0, keepdims=True)
        _acc_rows(loss_ref, jnp.broadcast_to(part, (1, LANES)), first)
        dy = err * (1.0 / D)
        gdy = dy * gain_v
        dx = r * (gdy - xh * jnp.mean(gdy * xh, axis=-1, keepdims=True))
        dx_ref[...] = dx
        dxb_ref[...] = dx.astype(BF)
        _acc_rows(dg_ref, dy * xh, first)

    return _pcall(
        body, name=name, grid=(T // tm,),
        ins=[(x, (tm, D), lambda i: (i, 0)), (gain, (1, D), lambda i: (0, 0)), (target, (tm, D), lambda i: (i, 0))],
        outs=[((T, D), F32, (tm, D), lambda i: (i, 0)), ((T, D), BF, (tm, D), lambda i: (i, 0)),
              ((1, D), F32, (1, D), lambda i: (0, 0)), ((1, LANES), F32, (1, LANES), lambda i: (0, 0))])


def _adamw(g, w, m, v):
    m = ADAM_B1 * m + (1.0 - ADAM_B1) * g
    v = ADAM_B2 * v + (1.0 - ADAM_B2) * (g * g)
    m_hat = m / ADAM_C1
    v_hat = v / ADAM_C2
    delta = -ADAM_LR * (m_hat / (jnp.sqrt(v_hat) + ADAM_EPS) + ADAM_WD * w)
    return delta, m, v


def adam_flat(g, w, m, v, *, name):
    def body(g_ref, w_ref, m_ref, v_ref, d_ref, nm_ref, nv_ref):
        d, nm, nv = _adamw(g_ref[...], w_ref[...], m_ref[...], v_ref[...])
        d_ref[...] = d
        nm_ref[...] = nm
        nv_ref[...] = nv

    blk = g.shape
    zero = lambda: (0, 0)
    return _pcall(body, name=name, grid=(),
                  ins=[(t, blk, zero) for t in (g, w, m, v)],
                  outs=[(blk, F32, blk, zero)] * 3)


def _chip_slots():
    x, y, c = lax.axis_index("x"), lax.axis_index("y"), lax.axis_index("c")
    chips = [(1 - x, y), (x, 1 - y), (1 - x, 1 - y)]
    return x, y, c, chips


def reduce_adam(gs, a_buf, b_buf, w, m, v, layer, prev, *, name, deps=()):
    L, R, C = w.shape
    tr = _tile(R, 256, 8)
    x, y, c, _ = _chip_slots()
    idx = jnp.stack([4 * x + 2 * y + c, 2 * x + y]).astype(jnp.int32)
    n_prev = 0 if prev is None else 4

    def body(idx_ref, g_ref, a_ref, b0_ref, b1_ref, b2_ref, w_ref, m_ref, v_ref, *rest):
        outs = rest[n_prev:]
        g = ((((g_ref[...].astype(F32) + a_ref[...].astype(F32)) + b0_ref[...].astype(F32))
              + b1_ref[...].astype(F32)) + b2_ref[...].astype(F32))
        d, nm, nv = _adamw(g, w_ref[...], m_ref[...], v_ref[...])
        outs[0][...] = g
        outs[1][...] = d
        outs[2][...] = nm
        outs[3][...] = nv

    blk3 = (None, tr, C)
    ins = [(gs, blk3, lambda i, s: (s[0], i, 0)), (a_buf, blk3, lambda i, s: (s[1], i, 0)),
           (b_buf, blk3, lambda i, s: (0, i, 0)), (b_buf, blk3, lambda i, s: (1, i, 0)),
           (b_buf, blk3, lambda i, s: (2, i, 0)),
           (w, blk3, lambda i, s: (layer, i, 0)), (m, blk3, lambda i, s: (layer, i, 0)),
           (v, blk3, lambda i, s: (layer, i, 0))]
    aliases = {}
    if prev is not None:
        for o, p in enumerate(prev):
            ins.append((p, None, None))
            aliases[1 + 8 + o] = o
    outs = [((L, R, C), F32, blk3, lambda i, s: (layer, i, 0))] * 4
    return _pcall(body, name=name, grid=(R // tr,), ins=ins, outs=outs, prefetch=idx, aliases=aliases,
                  semantics=("parallel",), deps=deps)


def pair_sum(gs, a_buf, *, name):
    _, R, C = gs.shape
    tr = _tile(R, 256, 8)
    x, y, c, chips = _chip_slots()
    idx = jnp.stack([4 * cx + 2 * cy + c for cx, cy in chips] + [2 * cx + cy for cx, cy in chips]).astype(jnp.int32)

    def body(idx_ref, g_ref, a_ref, o_ref):
        o_ref[...] = (g_ref[...].astype(F32) + a_ref[...].astype(F32)).astype(BF)

    blk3 = (None, tr, C)
    return _pcall(body, name=name, grid=(3, R // tr),
                  ins=[(gs, blk3, lambda j, i, s: (s[j], i, 0)), (a_buf, blk3, lambda j, i, s: (s[3 + j], i, 0))],
                  outs=[((3, R, C), BF, blk3, lambda j, i, s: (j, i, 0))], prefetch=idx,
                  semantics=("parallel", "parallel"))[0]


def sum_rows8(gathered, rows, *, name):
    W = gathered.shape[1]

    def body(g_ref, o_ref):
        acc = g_ref[0:rows, :]
        for d in range(1, N_DEV):
            acc = acc + g_ref[d * rows:(d + 1) * rows, :]
        o_ref[...] = acc

    return _pcall(body, name=name, grid=(), ins=[(gathered, gathered.shape, lambda: (0, 0))],
                  outs=[((rows, W), F32, (rows, W), lambda: (0, 0))])[0]


def _comm_call(body, *, name, ins, out_shapes, n_sem_rows, n_sem_cols, extra_scratch=()):
    any_spec = pl.BlockSpec(memory_space=pl.ANY)
    return pl.pallas_call(
        body, name=name, out_shape=[jax.ShapeDtypeStruct(s, d) for s, d in out_shapes],
        in_specs=[any_spec] * len(ins), out_specs=[any_spec] * len(out_shapes),
        scratch_shapes=[pltpu.SemaphoreType.DMA((n_sem_rows, n_sem_cols)),
                        pltpu.SemaphoreType.DMA((n_sem_rows, n_sem_cols))] + list(extra_scratch),
    )(*ins)


HBM_SPEC = pl.BlockSpec(memory_space=pltpu.HBM)
SEM_SPEC = pl.BlockSpec(memory_space=pltpu.SEMAPHORE)
ANY_SPEC = pl.BlockSpec(memory_space=pl.ANY)
DATAFLOW = pltpu.SideEffectType.DATAFLOW_SIDE_EFFECTING


def _in_hbm(v):
    return pltpu.with_memory_space_constraint(v, pltpu.HBM)


def _slot(p):
    return 4 * p[0] + 2 * p[1] + p[2]


def _gather_peers():
    x, y, c, chips = _chip_slots()
    return (x, y, c), [(x, y, 1 - c)] + [(*chip, c) for chip in chips]


def gather_start(groups, *, name):
    flat = [s for g in groups for s in g]
    n, n_g = len(flat), len(groups)
    where = [(gi, ti) for gi, g in enumerate(groups) for ti in range(len(g))]

    def body(*refs):
        src, land = refs[:n], refs[n:2 * n]
        sems = refs[2 * n:2 * n + 2 * n_g]
        me, peers = _gather_peers()
        for t in range(n):
            gi, ti = where[t]
            for k, to in enumerate(peers):
                pltpu.make_async_remote_copy(
                    src_ref=src[t], dst_ref=land[t].at[_slot(me)], send_sem=sems[2 * gi].at[4 * ti + k],
                    recv_sem=sems[2 * gi + 1].at[4 * ti + k], device_id=to, device_id_type=MESH).start()

    out_shape = []
    for g in groups:
        out_shape += [pltpu.SemaphoreType.DMA((4 * len(g),)), pltpu.SemaphoreType.DMA((4 * len(g),))]
    out_shape += [pltpu.HBM(s.shape, s.dtype) for s in flat]
    out_shape += [pltpu.HBM((N_DEV,) + s.shape, s.dtype) for s in flat]
    aliases = {t: 2 * n_g + t for t in range(n)}
    aliases.update({n + t: 2 * n_g + n + t for t in range(n)})
    res = pl.pallas_call(
        body, name=name, out_shape=out_shape, in_specs=[HBM_SPEC] * (2 * n),
        out_specs=[SEM_SPEC] * (2 * n_g) + [HBM_SPEC] * (2 * n), input_output_aliases=aliases,
        compiler_params=pltpu.CompilerParams(has_side_effects=DATAFLOW),
    )(*[_in_hbm(s) for s in flat], *[_in_hbm(lax.empty((N_DEV,) + s.shape, s.dtype)) for s in flat])
    out, off = [], 0
    for gi, g in enumerate(groups):
        k = len(g)
        out.append((res[2 * gi], res[2 * gi + 1], res[2 * n_g + off:2 * n_g + off + k],
                    res[2 * n_g + n + off:2 * n_g + n + off + k]))
        off += k
    return out


def gather_wait(started, after, *, name):
    send_sems, recv_sems, srcs, lands = started
    n = len(srcs)

    def body(*refs):
        src, land = refs[:n], refs[n:2 * n]
        send, recv = refs[2 * n], refs[2 * n + 1]
        _, peers = _gather_peers()
        for t in range(n):
            for k, frm in enumerate(peers):
                cp = pltpu.make_async_remote_copy(
                    src_ref=src[t], dst_ref=land[t].at[_slot(frm)], send_sem=send.at[4 * t + k],
                    recv_sem=recv.at[4 * t + k],
                    device_id=frm, device_id_type=MESH)
                cp.wait_send()
                cp.wait_recv()

    res = pl.pallas_call(
        body, name=name,
        out_shape=[pltpu.HBM(s.shape, s.dtype) for s in srcs] + [pltpu.HBM(l.shape, l.dtype) for l in lands],
        in_specs=[HBM_SPEC] * (2 * n) + [SEM_SPEC, SEM_SPEC, ANY_SPEC], out_specs=[HBM_SPEC] * (2 * n),
        input_output_aliases={t: t for t in range(2 * n)},
        compiler_params=pltpu.CompilerParams(has_side_effects=DATAFLOW),
    )(*srcs, *lands, send_sems, recv_sems, after)
    return res[:n], res[n:]


def gather_finish(srcs, lands, *, name):
    n = len(srcs)

    def body(*refs):
        src = refs[:n]
        land = refs[2 * n:3 * n]
        send_sems, recv_sems, local_sems = refs[3 * n:]
        x, y, c, chips = _chip_slots()
        me, sibling = (x, y, c), (x, y, 1 - c)
        mine = [pltpu.make_async_copy(src[t], land[t].at[_slot(me)], local_sems.at[t]) for t in range(n)]
        for cp in mine:
            cp.start()

        def copy(t, j, block, to):
            return pltpu.make_async_remote_copy(
                src_ref=land[t].at[_slot(block)], dst_ref=land[t].at[_slot(block)], send_sem=send_sems.at[t, j],
                recv_sem=recv_sems.at[t, j], device_id=to, device_id_type=MESH)

        sends = [copy(t, j, (*chip, c), sibling) for t in range(n) for j, chip in enumerate(chips)]
        for cp in sends:
            cp.start()
        for t in range(n):
            for j, chip in enumerate(chips):
                copy(t, j, (*chip, 1 - c), me).wait_recv()
        for cp in sends:
            cp.wait_send()
        for cp in mine:
            cp.wait()

    return pl.pallas_call(
        body, name=name, out_shape=[jax.ShapeDtypeStruct(l.shape, l.dtype) for l in lands],
        in_specs=[ANY_SPEC] * (2 * n), out_specs=[ANY_SPEC] * n,
        input_output_aliases={n + t: t for t in range(n)},
        scratch_shapes=[pltpu.SemaphoreType.DMA((n, 3)), pltpu.SemaphoreType.DMA((n, 3)),
                        pltpu.SemaphoreType.DMA((n,))],
    )(*srcs, *lands)


def chips_start(pairs, *, name):
    n = len(pairs)

    def body(*refs):
        src, land = refs[:n], refs[n:2 * n]
        send, recv = refs[2 * n], refs[2 * n + 1]
        token = refs[-1]
        x, y, c, chips = _chip_slots()
        for t in range(n):
            for j, chip in enumerate(chips):
                pltpu.make_async_remote_copy(
                    src_ref=src[t].at[j], dst_ref=land[t].at[j], send_sem=send.at[3 * t + j],
                    recv_sem=recv.at[3 * t + j], device_id=(*chip, c), device_id_type=MESH).start()
        token[...] = jnp.zeros_like(token)

    res = pl.pallas_call(
        body, name=name,
        out_shape=[pltpu.SemaphoreType.DMA((3 * n,)), pltpu.SemaphoreType.DMA((3 * n,))]
        + [pltpu.HBM(p.shape, p.dtype) for p in pairs] * 2 + [jax.ShapeDtypeStruct((8, LANES), F32)],
        in_specs=[HBM_SPEC] * (2 * n),
        out_specs=[SEM_SPEC, SEM_SPEC] + [HBM_SPEC] * (2 * n) + [pl.BlockSpec(memory_space=pltpu.VMEM)],
        input_output_aliases={t: 2 + t for t in range(2 * n)},
        compiler_params=pltpu.CompilerParams(has_side_effects=DATAFLOW),
    )(*[_in_hbm(p) for p in pairs], *[_in_hbm(lax.empty(p.shape, p.dtype)) for p in pairs])
    return res[0], res[1], res[2:2 + n], res[2 + n:2 + 2 * n], res[-1]


def chips_wait(started, after, *, name):
    send_sems, recv_sems, srcs, lands, _ = started
    n = len(srcs)

    def body(*refs):
        src, land = refs[:n], refs[n:2 * n]
        send, recv = refs[2 * n], refs[2 * n + 1]
        x, y, c, chips = _chip_slots()
        for t in range(n):
            for j, chip in enumerate(chips):
                cp = pltpu.make_async_remote_copy(
                    src_ref=src[t].at[j], dst_ref=land[t].at[j], send_sem=send.at[3 * t + j],
                    recv_sem=recv.at[3 * t + j], device_id=(*chip, c), device_id_type=MESH)
                cp.wait_send()
                cp.wait_recv()

    res = pl.pallas_call(
        body, name=name, out_shape=[pltpu.HBM(s.shape, s.dtype) for s in srcs] * 2,
        in_specs=[HBM_SPEC] * (2 * n) + [SEM_SPEC, SEM_SPEC, ANY_SPEC], out_specs=[HBM_SPEC] * (2 * n),
        input_output_aliases={t: t for t in range(2 * n)},
        compiler_params=pltpu.CompilerParams(has_side_effects=DATAFLOW),
    )(*srcs, *lands, send_sems, recv_sems, after)
    return res[n:]


def exchange_sibling(gs, *, name):
    n = len(gs)

    def body(*refs):
        src, dst = refs[:n], refs[n:2 * n]
        send_sems, recv_sems = refs[2 * n:]
        x, y, c, _ = _chip_slots()
        copies = []
        for t in range(n):
            for q in range(4):
                qx, qy = q // 2, q % 2
                copies.append(pltpu.make_async_remote_copy(
                    src_ref=src[t].at[4 * qx + 2 * qy + (1 - c)], dst_ref=dst[t].at[q],
                    send_sem=send_sems.at[t, q], recv_sem=recv_sems.at[t, q],
                    device_id=(x, y, 1 - c), device_id_type=MESH))
        for cp in copies:
            cp.start()
        for cp in copies:
            cp.wait()

    return _comm_call(body, name=name, ins=list(gs), out_shapes=[((4,) + g.shape[1:], g.dtype) for g in gs],
                      n_sem_rows=n, n_sem_cols=4)


def all_gather_vmem(x_shard, *, name):
    m_per, n = x_shard.shape

    def body(x_ref, out_ref, send_sems, recv_sems, local_sem):
        x, y, c, chips = _chip_slots()
        me, sibling = (x, y, c), (x, y, 1 - c)

        def rows(px, py, pc):
            return out_ref.at[pl.ds((4 * px + 2 * py + pc) * m_per, m_per), :]

        def copy(k, block, to, src=None):
            return pltpu.make_async_remote_copy(
                src_ref=rows(*block) if src is None else src, dst_ref=rows(*block),
                send_sem=send_sems.at[k], recv_sem=recv_sems.at[k], device_id=to, device_id_type=MESH)

        mine = pltpu.make_async_copy(x_ref, rows(*me), local_sem)
        mine.start()
        first = [copy(0, me, sibling, src=x_ref)]
        first += [copy(1 + j, me, (*chip, c), src=x_ref) for j, chip in enumerate(chips)]
        for cp in first:
            cp.start()
        passed = [copy(4 + j, (*chip, c), sibling) for j, chip in enumerate(chips)]
        for j, chip in enumerate(chips):
            copy(1 + j, (*chip, c), me).wait_recv()
            passed[j].start()
        copy(0, sibling, me).wait_recv()
        for j, chip in enumerate(chips):
            copy(4 + j, (*chip, 1 - c), me).wait_recv()
        for cp in first + passed:
            cp.wait_send()
        mine.wait()

    vmem = pl.BlockSpec(memory_space=pltpu.VMEM)
    return pl.pallas_call(
        body, name=name, out_shape=jax.ShapeDtypeStruct((N_DEV * m_per, n), x_shard.dtype),
        in_specs=[vmem], out_specs=vmem,
        scratch_shapes=[pltpu.SemaphoreType.DMA((7,)), pltpu.SemaphoreType.DMA((7,)), pltpu.SemaphoreType.DMA],
        compiler_params=pltpu.CompilerParams(vmem_limit_bytes=int(min(
            VMEM_LIMIT_CAP, 2 * (N_DEV + 1) * m_per * n * x_shard.dtype.itemsize + 16 * 2 ** 20))),
    )(x_shard)


def _rope_slab(cols):
    z = jnp.zeros(cols.shape[:-1] + (HALF_ROPE,), cols.dtype)
    return jnp.concatenate([cols[..., :HALF_ROPE], z, cols[..., HALF_ROPE:], z], axis=-1)


def _rope_unslab(slab):
    return jnp.concatenate([slab[..., :HALF_ROPE], slab[..., 2 * HALF_ROPE:3 * HALF_ROPE]], axis=-1)


def _pack_w_in(w_g):
    s, d, c = w_g.shape
    w = jnp.transpose(w_g, (1, 0, 2)).reshape(d, s * c)
    c1, c2, c3 = Q_LORA, Q_LORA + KV_LORA, Q_LORA + KV_LORA + QK_ROPE
    return jnp.concatenate([w[:, :c2], w[:, c3:], _rope_slab(w[:, c2:c3])], axis=-1)


def _unpack_w_in_grad(dw):
    d = dw.shape[0]
    c2 = Q_LORA + KV_LORA
    uv = 2 * SGU_OUT
    g = jnp.concatenate([dw[:, :c2], _rope_unslab(dw[:, c2 + uv:]), dw[:, c2:c2 + uv]], axis=-1)
    return jnp.transpose(g.reshape(d, N_DEV, g.shape[1] // N_DEV), (1, 0, 2))


def _rope_tables(positions):
    inv_freq = ROPE_BASE ** (-jnp.arange(0, QK_ROPE, 2, dtype=F32) / QK_ROPE)
    ang = positions.astype(F32)[:, None] * inv_freq
    cos, sin = jnp.cos(ang), jnp.sin(ang)
    z = jnp.zeros_like(cos)
    return jnp.concatenate([cos, z, cos, z], axis=-1), jnp.concatenate([-sin, z, sin, z], axis=-1)


def _mlp_up(x, gain, w1, tag):
    hn = rms_fwd(x, gain, name=f"mlp{tag}_norm")

    def act_epi(acc):
        a = jnp.maximum(acc, 0.0)
        return a, a * a

    T = x.shape[0]
    F = w1.shape[0] * w1.shape[2]
    a, act = mm(hn, w1, name=f"mlp{tag}_up", outs=[((T, F), BF, None), ((T, F), BF, None)], epi=act_epi)
    return hn, a, act


def _mlp_down(x, act, w2, tag):
    bm = _tile(x.shape[0], MM_TILE)
    bn = _tile(x.shape[1], MM_TILE)
    return mm(act, w2, name=f"mlp{tag}_down", out=(x.shape, F32), bm=bm, bn=bn,
              epi=lambda acc, r: (acc + r[...],), epi_ins=[(x, (bm, bn), lambda i, j, k: (i, j))])


def _mlp_bwd_weights(w1, w2, saved, dxb, tag):
    hn, a, act = saved
    T, D = dxb.shape
    F = a.shape[1]
    bm = _tile(T, MM_TILE)
    bn = _tile(F, min(MM_TILE, w1.shape[2]))
    dhid = mm(dxb, w2, tb=True, name=f"mlp{tag}_dhid", out=((T, F), BF), bm=bm, bn=bn,
              epi=lambda acc, a_ref: (2.0 * a_ref[...].astype(F32) * acc,),
              epi_ins=[(a, (bm, bn), lambda i, j, k: (i, j))])
    dw2 = mm(act, dxb, ta=True, name=f"mlp{tag}_dw2", out=((F, D), BF))
    dw1 = mm(hn, dhid, ta=True, name=f"mlp{tag}_dw1", out=(w1.shape, BF))
    return dhid, dw1, dw2.reshape(N_DEV, F // N_DEV, D)


def _mlp_bwd_input(x_in, gain, w1, dhid, dx, tag, deps):
    dhn = mm(dhid, w1, tb=True, name=f"mlp{tag}_dhn", out=(x_in.shape, F32), deps=deps)
    return rms_bwd(x_in, gain, dhn, dres=dx, name=f"mlp{tag}_norm_bwd")


def _reduce_start(grads, tag):
    a_bufs = exchange_sibling(grads, name=f"reduce_sibling_{tag}")
    pairs = [pair_sum(g, a, name=f"pair_sum_{tag}{t}") for t, (g, a) in enumerate(zip(grads, a_bufs))]
    return a_bufs, chips_start(pairs, name=f"reduce_chips_start_{tag}")


def kernel(x, positions, e_norm_mix, e_w_in, e_q_norm, e_w_uq, e_kv_norm, e_w_ukv, e_v_norm, e_sgu_w, e_sgu_b, e_mla_out_norm, e_sgu_out_norm, e_w_out, o_norm_mix, o_w_in, o_conv_w, o_w_out, mlp_norm, mlp_w1, mlp_w2, final_norm, loss_target, m_e_norm_mix, m_e_w_in, m_e_q_norm, m_e_w_uq, m_e_kv_norm, m_e_w_ukv, m_e_v_norm, m_e_sgu_w, m_e_sgu_b, m_e_mla_out_norm, m_e_sgu_out_norm, m_e_w_out, m_o_norm_mix, m_o_w_in, m_o_conv_w, m_o_w_out, m_mlp_norm, m_mlp_w1, m_mlp_w2, m_final_norm, v_e_norm_mix, v_e_w_in, v_e_q_norm, v_e_w_uq, v_e_kv_norm, v_e_w_ukv, v_e_v_norm, v_e_sgu_w, v_e_sgu_b, v_e_mla_out_norm, v_e_sgu_out_norm, v_e_w_out, v_o_norm_mix, v_o_w_in, v_o_conv_w, v_o_w_out, v_mlp_norm, v_mlp_w1, v_mlp_w2, v_final_norm):
    T, D = x.shape[1], x.shape[2]
    d_shard = o_norm_mix.shape[1]
    x0 = x[0]
    target = loss_target[0]
    me = 4 * lax.axis_index("x") + 2 * lax.axis_index("y") + lax.axis_index("c")

    bf = lambda s: s.astype(BF)
    gather_groups = [[bf(e_w_in[0]), bf(e_w_uq[0]), bf(e_w_ukv[0])], [bf(e_w_out[0]), bf(mlp_w1[0])],
                     [bf(mlp_w2[0]), bf(o_w_in[0])], [bf(o_w_out[0]), bf(mlp_w1[1])], [bf(mlp_w2[1])]]
    started = gather_start(gather_groups, name="gather_start")

    def gathered(gi, after):
        srcs, lands = gather_wait(started[gi], after, name=f"gather_wait{gi}")
        return gather_finish(srcs, lands, name=f"gather_finish{gi}")

    small_rows = jnp.concatenate([o_norm_mix, o_conv_w[0], jnp.zeros((4, d_shard), F32)], axis=0)
    small_g = all_gather_vmem(small_rows, name="gather_small").reshape(N_DEV, 8, d_shard)
    o_norm_full = small_g[:, 0, :].reshape(1, D)
    conv_w_full = jnp.transpose(small_g[:, 1:4, :], (1, 0, 2)).reshape(3, D)
    w_tril = jnp.tril(e_sgu_w[0])
    w_tril_b = w_tril.astype(BF)
    w_tril_tb = jnp.swapaxes(w_tril, 1, 2).astype(BF)
    b_full = jnp.repeat(e_sgu_b[0].T, CH, axis=1)
    v_gain = e_v_norm[0].reshape(1, SGU_OUT)
    cos_t, sin_t = _rope_tables(positions[0])
    mlp_gain = [mlp_norm[0:1], mlp_norm[1:2]]
    final_gain = final_norm.reshape(1, D)

    h0 = rms_fwd(x0, e_norm_mix, name="e_norm")
    g_w_in, g_w_uq, w_ukv = gathered(0, h0)
    w_in_e = _pack_w_in(g_w_in)
    w_uq = jnp.concatenate([g_w_uq[..., :QK_NOPE], _rope_slab(g_w_uq[..., QK_NOPE:])], axis=-1)
    proj = mm(h0, w_in_e, name="e_in", out=((T, w_in_e.shape[1]), F32), bn=_tile(w_in_e.shape[1], 640))
    qn, kvn, krope = mla_prep(proj, e_q_norm, e_kv_norm, cos_t, sin_t, name="mla_prep")
    bm = _tile(T, MM_TILE)

    def q_epi(acc, cos_ref, sin_ref):
        return (jnp.concatenate([acc[:, :QK_NOPE], _rope_fwd(acc[:, QK_NOPE:], cos_ref[...], sin_ref[...])], axis=-1),)

    q = mm(qn, w_uq, name="mla_q", out=((T, HEADS * HEAD_PAD), BF), bm=bm, bn=HEAD_PAD, epi=q_epi,
           epi_ins=[(cos_t, (bm, LANES), lambda i, j, k: (i, 0)), (sin_t, (bm, LANES), lambda i, j, k: (i, 0))])

    def kv_epi(acc, kr_ref):
        return jnp.concatenate([acc[:, :QK_NOPE].astype(BF), kr_ref[...]], axis=-1), acc[:, QK_NOPE:]

    k, v = mm(kvn, w_ukv, name="mla_kv", bm=bm, bn=HEAD_PAD, epi=kv_epi,
              outs=[((T, HEADS * HEAD_PAD), BF, HEAD_PAD), ((T, MLA_OUT), BF, V_HEAD)],
              epi_ins=[(krope, (bm, LANES), lambda i, j, k: (i, 0))])
    attn = attn_fwd(q, k, v, name="attn_fwd")
    mixed = mix_fwd(attn, proj, e_mla_out_norm, e_sgu_out_norm, v_gain, w_tril_b, b_full, name="mix_fwd")
    bn = _tile(D, MM_TILE)
    g_w_out_e, w1_0 = gathered(1, mixed)
    w_out_e = g_w_out_e.reshape(-1, D)
    x1 = mm(mixed, w_out_e, name="e_out", out=((T, D), F32), bm=bm, bn=bn,
            epi=lambda acc, r: (acc + r[...],), epi_ins=[(x0, (bm, bn), lambda i, j, k: (i, j))])
    hn0, a0, act0 = _mlp_up(x1, mlp_gain[0], w1_0, 0)
    g_w2_0, g_w_in_o = gathered(2, act0)
    w2_0 = g_w2_0.reshape(-1, D)
    x2 = _mlp_down(x1, act0, w2_0, 0)
    ho = rms_fwd(x2, o_norm_full, name="o_norm")
    proj_o = mm(ho, g_w_in_o, name="o_in", out=((T, 3 * D), F32))
    gated = conv_fwd(proj_o, conv_w_full, name="conv_fwd")
    g_w_out_o, w1_1 = gathered(3, gated)
    w_out_o = g_w_out_o.reshape(-1, D)
    x3 = mm(gated, w_out_o, name="o_out", out=((T, D), F32), bm=bm, bn=bn,
            epi=lambda acc, r: (acc + r[...],), epi_ins=[(x2, (bm, bn), lambda i, j, k: (i, j))])
    hn1, a1, act1 = _mlp_up(x3, mlp_gain[1], w1_1, 1)
    (g_w2_1,) = gathered(4, act1)
    w2_1 = g_w2_1.reshape(-1, D)
    x4 = _mlp_down(x3, act1, w2_1, 1)
    w1, w2 = [w1_0, w1_1], [w2_0, w2_1]
    mlp0_saved, mlp1_saved = (hn0, a0, act0), (hn1, a1, act1)

    dx4, dx4b, d_final, loss_part = loss_bwd(x4, final_gain, target, name="loss_bwd")
    loss = lax.psum(loss_part[0, 0], AXES)

    dhid1, dw1_1, dw2_1 = _mlp_bwd_weights(w1[1], w2[1], mlp1_saved, dx4b, 1)
    grads_r0 = [dw1_1, dw2_1]
    a_r0, st_r0 = _reduce_start(grads_r0, "r0")
    dx3, dx3b, d_mlp1 = _mlp_bwd_input(x3, mlp_gain[1], w1[1], dhid1, dx4, 1, deps=[st_r0[-1]])

    dgated = mm(dx3b, w_out_o, tb=True, name="o_out_dx", out=((T, D), F32))
    dw_out_o = mm(gated, dx3b, ta=True, name="o_out_dw", out=((D, D), BF))
    dproj_o, dconv_full = conv_bwd(dgated, proj_o, conv_w_full, name="conv_bwd")
    dw_in_o = mm(ho, dproj_o, ta=True, name="o_in_dw", out=(g_w_in_o.shape, BF))
    grads_r1 = [dw_out_o.reshape(g_w_out_o.shape), dw_in_o]
    a_r1, st_r1 = _reduce_start(grads_r1, "r1")
    dho = mm(dproj_o, g_w_in_o, tb=True, name="o_in_dx", out=((T, D), F32), deps=[st_r1[-1]])
    dx2, dx2b, d_onorm_full = rms_bwd(x2, o_norm_full, dho, dres=dx3, name="o_norm_bwd")

    dhid0, dw1_0, dw2_0 = _mlp_bwd_weights(w1[0], w2[0], mlp0_saved, dx2b, 0)
    grads_r2 = [dw1_0, dw2_0]
    a_r2, st_r2 = _reduce_start(grads_r2, "r2")
    dx1, dx1b, d_mlp0 = _mlp_bwd_input(x1, mlp_gain[0], w1[0], dhid0, dx2, 0, deps=[st_r2[-1]])
    b_r0 = chips_wait(st_r0, dx1b, name="reduce_chips_wait_r0")

    dmixed = mm(dx1b, w_out_e, tb=True, name="e_out_dx", out=((T, MLA_OUT + SGU_OUT), F32))
    dw_out_e = mm(mixed, dx1b, ta=True, name="e_out_dw", out=(w_out_e.shape, BF))
    (dattn, duv, d_mla_out, d_sgu_out, d_vgain, d_sgu_w, d_b_full) = mix_bwd(
        dmixed, attn, proj, e_mla_out_norm, e_sgu_out_norm, v_gain, w_tril_b, w_tril_tb, b_full, name="mix_bwd")
    b_r1 = chips_wait(st_r1, dattn, name="reduce_chips_wait_r1")
    dq, dk, dv = attn_bwd(q, k, v, dattn, name="attn_bwd")
    dq_lin, dkv_lin, dkr = mla_bwd_prep(dq, dk, dv, cos_t, sin_t, name="mla_bwd_prep")
    dw_uq_pad = mm(qn, dq_lin, ta=True, name="mla_q_dw", out=(w_uq.shape, F32))
    dqn = mm(dq_lin, w_uq, tb=True, name="mla_q_dx", out=((T, Q_LORA), F32))
    dw_ukv = mm(kvn, dkv_lin, ta=True, name="mla_kv_dw", out=(w_ukv.shape, BF))
    dkvn = mm(dkv_lin, w_ukv, tb=True, name="mla_kv_dx", out=((T, KV_LORA), F32))
    dcq, d_qnorm = rms_bwd(proj, e_q_norm, dqn, col_block=0, want_f32=False, name="q_norm_bwd")
    dckv, d_kvnorm = rms_bwd(proj, e_kv_norm, dkvn, col_block=1, want_f32=False, name="kv_norm_bwd")
    dproj = jnp.concatenate([dcq, dckv, duv, dkr], axis=-1)
    dw_in_e_pad = mm(h0, dproj, ta=True, name="e_in_dw", out=(w_in_e.shape, F32), bn=_tile(w_in_e.shape[1], 640))
    dw_in_e = _unpack_w_in_grad(dw_in_e_pad).astype(BF)
    dw_uq = jnp.concatenate([dw_uq_pad[..., :QK_NOPE], _rope_unslab(dw_uq_pad[..., QK_NOPE:])], axis=-1).astype(BF)
    grads_r3 = [dw_out_e.reshape(g_w_out_e.shape), dw_uq, dw_ukv, dw_in_e]
    a_r3, st_r3 = _reduce_start(grads_r3, "r3")
    tok_r3 = st_r3[-1]
    dh0 = mm(dproj, w_in_e, tb=True, name="e_in_dx", out=((T, D), F32), deps=[tok_r3])
    grad_x, d_enorm = rms_bwd(x0, e_norm_mix, dh0, dres=dx1, want_bf=False, name="e_norm_bwd")
    b_r2 = chips_wait(st_r2, grad_x, name="reduce_chips_wait_r2")

    def finish(grads, a_bufs, b_bufs, t, w, m, v, layer=0, prev=None, tag="", deps=()):
        return reduce_adam(grads[t], a_bufs[t], b_bufs[t], w, m, v, layer, prev, name=f"adam_{tag}", deps=deps)

    r_w1 = finish(grads_r0, a_r0, b_r0, 0, mlp_w1, m_mlp_w1, v_mlp_w1, 1, None, tag="w1_l1")
    r_w2 = finish(grads_r0, a_r0, b_r0, 1, mlp_w2, m_mlp_w2, v_mlp_w2, 1, None, tag="w2_l1")
    r_w_out_o = finish(grads_r1, a_r1, b_r1, 0, o_w_out, m_o_w_out, v_o_w_out, tag="o_w_out")
    r_w_in_o = finish(grads_r1, a_r1, b_r1, 1, o_w_in, m_o_w_in, v_o_w_in, tag="o_w_in")
    r_w1 = finish(grads_r2, a_r2, b_r2, 0, mlp_w1, m_mlp_w1, v_mlp_w1, 0, r_w1, tag="w1_l0", deps=[tok_r3])
    r_w2 = finish(grads_r2, a_r2, b_r2, 1, mlp_w2, m_mlp_w2, v_mlp_w2, 0, r_w2, tag="w2_l0", deps=[tok_r3])
    b_r3 = chips_wait(st_r3, r_w2[1], name="reduce_chips_wait_r3")
    r_w_out_e = finish(grads_r3, a_r3, b_r3, 0, e_w_out, m_e_w_out, v_e_w_out, tag="e_w_out")
    r_w_uq = finish(grads_r3, a_r3, b_r3, 1, e_w_uq, m_e_w_uq, v_e_w_uq, tag="e_w_uq")
    r_w_ukv = finish(grads_r3, a_r3, b_r3, 2, e_w_ukv, m_e_w_ukv, v_e_w_ukv, tag="e_w_ukv")
    r_w_in = finish(grads_r3, a_r3, b_r3, 3, e_w_in, m_e_w_in, v_e_w_in, tag="e_w_in")

    d_sgu_b = jnp.transpose(d_b_full[:, ::CH])
    d_sgu_w_tril = jnp.tril(d_sgu_w)
    rep = [("e_norm_mix", e_norm_mix, m_e_norm_mix, v_e_norm_mix, d_enorm),
           ("e_q_norm", e_q_norm, m_e_q_norm, v_e_q_norm, d_qnorm),
           ("e_kv_norm", e_kv_norm, m_e_kv_norm, v_e_kv_norm, d_kvnorm),
           ("e_v_norm", e_v_norm, m_e_v_norm, v_e_v_norm, d_vgain),
           ("e_sgu_w", e_sgu_w, m_e_sgu_w, v_e_sgu_w, d_sgu_w_tril),
           ("e_sgu_b", e_sgu_b, m_e_sgu_b, v_e_sgu_b, d_sgu_b),
           ("e_mla_out_norm", e_mla_out_norm, m_e_mla_out_norm, v_e_mla_out_norm, d_mla_out),
           ("e_sgu_out_norm", e_sgu_out_norm, m_e_sgu_out_norm, v_e_sgu_out_norm, d_sgu_out),
           ("mlp_norm", mlp_norm, m_mlp_norm, v_mlp_norm, jnp.concatenate([d_mlp0, d_mlp1], axis=0)),
           ("final_norm", final_norm, m_final_norm, v_final_norm, d_final)]
    sizes = [int(np.prod(r[1].shape)) for r in rep]
    n_rep = sum(sizes)
    n_all = n_rep + 4 * D
    width = -(-n_all // (8 * LANES)) * LANES
    pad = 8 * width - n_all
    flat = jnp.concatenate([r[4].reshape(-1) for r in rep]
                           + [d_onorm_full.reshape(-1), dconv_full.reshape(-1), jnp.zeros((pad,), F32)])
    summed = sum_rows8(all_gather_vmem(flat.reshape(8, width), name="gather_small_grads"), 8,
                       name="sum_small_grads").reshape(-1)

    def pack_rep(i):
        return jnp.concatenate([r[i].reshape(-1) for r in rep]).reshape(n_rep // LANES, LANES)

    g_rep = summed[:n_rep].reshape(n_rep // LANES, LANES)
    d_rep, nm_rep, nv_rep = adam_flat(g_rep, pack_rep(1), pack_rep(2), pack_rep(3), name="adam_replicated")

    def unpack_rep(flat2d):
        out, off = {}, 0
        f = flat2d.reshape(-1)
        for r, n in zip(rep, sizes):
            out[r[0]] = f[off:off + n].reshape(r[1].shape)
            off += n
        return out

    small = {"grad": unpack_rep(g_rep), "delta": unpack_rep(d_rep), "new_m": unpack_rep(nm_rep),
             "new_v": unpack_rep(nv_rep)}
    g_onorm = lax.dynamic_slice(summed[n_rep:n_rep + D].reshape(1, D), (0, me * d_shard), (1, d_shard))
    g_conv = lax.dynamic_slice(summed[n_rep + D:n_rep + 4 * D].reshape(3, D), (0, me * d_shard), (3, d_shard))

    def pack_sharded(norm_part, conv_part):
        return jnp.concatenate([norm_part, conv_part, jnp.zeros((4, d_shard), F32)], axis=0)

    g_sh = pack_sharded(g_onorm, g_conv)
    d_sh, nm_sh, nv_sh = adam_flat(g_sh, pack_sharded(o_norm_mix, o_conv_w[0]), pack_sharded(m_o_norm_mix, m_o_conv_w[0]),
                                   pack_sharded(v_o_norm_mix, v_o_conv_w[0]), name="adam_sharded_small")
    for kind, arr in (("grad", g_sh), ("delta", d_sh), ("new_m", nm_sh), ("new_v", nv_sh)):
        small[kind]["o_norm_mix"] = arr[0:1]
        small[kind]["o_conv_w"] = arr[1:4][None]

    big = {"e_w_in": r_w_in, "e_w_uq": r_w_uq, "e_w_ukv": r_w_ukv, "e_w_out": r_w_out_e, "o_w_in": r_w_in_o,
           "o_w_out": r_w_out_o, "mlp_w1": r_w1, "mlp_w2": r_w2}
    order = ["e_norm_mix", "e_w_in", "e_q_norm", "e_w_uq", "e_kv_norm", "e_w_ukv", "e_v_norm", "e_sgu_w", "e_sgu_b",
             "e_mla_out_norm", "e_sgu_out_norm", "e_w_out", "o_norm_mix", "o_w_in", "o_conv_w", "o_w_out", "mlp_norm",
             "mlp_w1", "mlp_w2", "final_norm"]
    result = [loss, grad_x[None]]
    for ki, kind in enumerate(("grad", "delta", "new_m", "new_v")):
        for nm in order:
            result.append(big[nm][ki] if nm in big else small[kind][nm])
    return tuple(result)
```

```python
import functools

import numpy as np
import jax
import jax.numpy as jnp
from jax import lax
from jax.experimental import pallas as pl
from jax.experimental.pallas import tpu as pltpu

BF = jnp.bfloat16
F32 = jnp.float32
MESH = pl.DeviceIdType.MESH
AXES = ("x", "y", "c")
N_DEV = 8

EPS = 1e-6
HEADS = 8
Q_LORA = 512
KV_LORA = 512
QK_NOPE = 128
QK_ROPE = 64
HALF_ROPE = QK_ROPE // 2
V_HEAD = 128
HEAD_PAD = 256
ROPE_BASE = 10000.0
GROUPS = 8
CH = 128
CHUNK = 128
SGU_OUT = GROUPS * CH
MLA_OUT = HEADS * V_HEAD
ATTN_SCALE = float((QK_NOPE + QK_ROPE) ** -0.5)

ADAM_LR = 0.001
ADAM_B1 = 0.9
ADAM_B2 = 0.999
ADAM_EPS = 1e-08
ADAM_WD = 0.01
ADAM_STEP = 10
ADAM_C1 = 1.0 - ADAM_B1 ** ADAM_STEP
ADAM_C2 = 1.0 - ADAM_B2 ** ADAM_STEP

V7X_VMEM_BYTES = 64 * 2 ** 20
VMEM_LIMIT_CAP = V7X_VMEM_BYTES - 6 * 2 ** 20
LANES = 128
ROW_TILE = 256
MM_TILE = 1024
MM_K_TILE = 2048


def _padded_bytes(block, dtype):
    dims = [d for d in block if d is not None]
    if len(dims) >= 1:
        dims[-1] = -(-dims[-1] // LANES) * LANES
    if len(dims) >= 2:
        dims[-2] = -(-dims[-2] // 16) * 16
    return int(np.prod(dims)) * jnp.dtype(dtype).itemsize


def _pcall(body, *, name, grid, ins, outs, scratch=(), semantics=None, aliases=None, prefetch=None, deps=()):
    any_spec = pl.BlockSpec(memory_space=pl.ANY)
    if deps:
        n_lead = len(ins) + (1 if prefetch is not None else 0)
        n_deps = len(deps)
        inner = body

        def body(*refs):
            inner(*refs[:n_lead], *refs[n_lead + n_deps:])

        ins = list(ins) + [(d, None, None) for d in deps]
    in_specs = [any_spec if b is None else pl.BlockSpec(b, m) for _, b, m in ins]
    out_specs = [any_spec if b is None else pl.BlockSpec(b, m) for _, _, b, m in outs]
    out_shape = [jax.ShapeDtypeStruct(s, d) for s, d, _, _ in outs]
    est = 0
    for a, b, _ in ins:
        if b is not None:
            est += 2 * _padded_bytes(b, a.dtype)
    for _, d, b, _ in outs:
        if b is not None:
            est += 2 * _padded_bytes(b, d)
    for s in scratch:
        if hasattr(s, "shape") and hasattr(s, "dtype"):
            est += _padded_bytes(s.shape, s.dtype)
    limit = int(min(VMEM_LIMIT_CAP, est + 16 * 2 ** 20))
    params = pltpu.CompilerParams(
        dimension_semantics=semantics or ("arbitrary",) * len(grid), vmem_limit_bytes=limit)
    args = [a for a, _, _ in ins]
    if prefetch is not None:
        grid_spec = pltpu.PrefetchScalarGridSpec(
            num_scalar_prefetch=1, grid=grid, in_specs=in_specs, out_specs=out_specs, scratch_shapes=list(scratch))
        call = pl.pallas_call(body, out_shape=out_shape, grid_spec=grid_spec, name=name, compiler_params=params,
                              input_output_aliases=aliases or {})
        return call(prefetch, *args)
    call = pl.pallas_call(body, out_shape=out_shape, grid=grid, in_specs=in_specs, out_specs=out_specs,
                          scratch_shapes=list(scratch), name=name, compiler_params=params,
                          input_output_aliases=aliases or {})
    return call(*args)


def _tile(dim, pref, quantum=LANES):
    if dim <= pref:
        return dim
    t = (pref // quantum) * quantum
    while t >= quantum:
        if dim % t == 0:
            return t
        t -= quantum
    return dim


def _vshape(arr_shape):
    if len(arr_shape) == 2:
        return tuple(arr_shape)
    s, r, c = arr_shape
    return (r, s * c)


def _vblock(arr_shape, br, bc, rc):
    if len(arr_shape) == 2:
        return (br, bc), (lambda *g: rc(*g))
    _, _, c = arr_shape
    assert c % bc == 0, (arr_shape, bc)
    per = c // bc

    def imap(*g):
        ri, ci = rc(*g)
        return (ci // per, ri, ci % per)

    return (None, br, bc), imap


def _shard_width(*shapes):
    w = None
    for s in shapes:
        if len(s) == 3:
            w = s[2] if w is None else int(np.gcd(w, s[2]))
    return w


def mm(a, b, *, name, ta=False, tb=False, out=None, outs=None, epi=None, epi_ins=(), bm=None, bn=None, bk=None,
       deps=()):
    av, bv = _vshape(a.shape), _vshape(b.shape)
    M, K = (av[1], av[0]) if ta else av
    K2, N = (bv[1], bv[0]) if tb else bv
    assert K == K2, (a.shape, b.shape, ta, tb)
    if outs is None:
        outs = [(out[0], out[1], None)]
    a_sw = _shard_width(a.shape)
    b_sw = _shard_width(b.shape)
    o_sw = _shard_width(*[o[0] for o in outs])
    m_lim = a_sw if (ta and a_sw) else None
    k_lim = [w for w in ((a_sw if not ta else None), (b_sw if tb else None)) if w]
    n_lim = [w for w in ((b_sw if not tb else None), o_sw) if w]
    if bm is None:
        bm = _tile(M, min([MM_TILE] + ([m_lim] if m_lim else [])))
    if bn is None:
        bn = _tile(N, min([MM_TILE] + n_lim))
    if bk is None:
        bk = K if (K <= 4096 and not k_lim) else _tile(K, min([MM_K_TILE] + k_lim))
    assert M % bm == 0 and N % bn == 0 and K % bk == 0, (name, M, N, K, bm, bn, bk)
    nk = K // bk
    grid = (M // bm, N // bn, nk)
    if ta:
        a_blk, a_map = _vblock(a.shape, bk, bm, lambda i, j, k: (k, i))
    else:
        a_blk, a_map = _vblock(a.shape, bm, bk, lambda i, j, k: (i, k))
    if tb:
        b_blk, b_map = _vblock(b.shape, bn, bk, lambda i, j, k: (j, k))
    else:
        b_blk, b_map = _vblock(b.shape, bk, bn, lambda i, j, k: (k, j))
    dn = (((0 if ta else 1,), (1 if tb else 0,)), ((), ()))
    ins = [(a, a_blk, a_map), (b, b_blk, b_map)] + list(epi_ins)
    out_list = []
    for shape, dtype, cols in outs:
        cols = cols or bn
        blk, imap = _vblock(shape, bm, cols, lambda i, j, k: (i, j))
        out_list.append((shape, dtype, blk, imap))
    n_e, n_o = len(epi_ins), len(out_list)

    def body(*refs):
        a_ref, b_ref = refs[0], refs[1]
        e_refs = refs[2:2 + n_e]
        o_refs = refs[2 + n_e:2 + n_e + n_o]

        def finish(acc):
            res = epi(acc, *e_refs) if epi is not None else (acc,)
            for o_ref, r in zip(o_refs, res):
                o_ref[...] = r.astype(o_ref.dtype)

        x = a_ref[...].astype(BF)
        y = b_ref[...].astype(BF)
        p = lax.dot_general(x, y, dn, preferred_element_type=F32)
        if nk == 1:
            finish(p)
        else:
            acc_ref = refs[-1]
            k = pl.program_id(2)

            @pl.when(k == 0)
            def _():
                acc_ref[...] = p

            @pl.when(k > 0)
            def _():
                acc_ref[...] += p

            @pl.when(k == nk - 1)
            def _():
                finish(acc_ref[...])

    scratch = [pltpu.VMEM((bm, bn), F32)] if nk > 1 else []
    res = _pcall(body, name=name, grid=grid, ins=ins, outs=out_list, scratch=scratch,
                 semantics=("parallel", "parallel", "arbitrary"), deps=deps)
    return res[0] if len(res) == 1 else res


_GELU_K = float(np.sqrt(2.0 / np.pi))
_GELU_C = 0.044715


def _gelu(x):
    t = jnp.tanh(_GELU_K * (x + _GELU_C * (x * x * x)))
    return 0.5 * x * (1.0 + t)


def _gelu_grad(x):
    t = jnp.tanh(_GELU_K * (x + _GELU_C * (x * x * x)))
    return 0.5 * (1.0 + t) + 0.5 * x * (1.0 - t * t) * (_GELU_K * (1.0 + 3.0 * _GELU_C * (x * x)))


def _rstd(x):
    return lax.rsqrt(jnp.mean(x * x, axis=-1, keepdims=True) + EPS)


def _rms_bwd(x, gain, dy):
    r = _rstd(x)
    xh = x * r
    gdy = dy * gain
    dx = r * (gdy - xh * jnp.mean(gdy * xh, axis=-1, keepdims=True))
    return dx, dy * xh


def _rope_fwd(x, cos_t, sin_t):
    return x * cos_t + pltpu.roll(x, 2 * HALF_ROPE, 1) * sin_t


def _rope_bwd(dy, cos_t, sin_t):
    return dy * cos_t + pltpu.roll(dy * sin_t, 2 * HALF_ROPE, 1)


def _acc_rows(ref, val, first):
    s = jnp.sum(val, axis=0, keepdims=True)

    @pl.when(first)
    def _():
        ref[...] = s

    @pl.when(jnp.logical_not(first))
    def _():
        ref[...] += s


def rms_fwd(x, gain, *, name, col_block=0, width=None):
    T = x.shape[0]
    width = width or x.shape[1]
    tm = _tile(T, ROW_TILE, 8)

    def body(x_ref, g_ref, o_ref):
        v = x_ref[...]
        o_ref[...] = (v * _rstd(v) * g_ref[...]).astype(BF)

    return _pcall(body, name=name, grid=(T // tm,),
                  ins=[(x, (tm, width), lambda i: (i, col_block)), (gain, (1, width), lambda i: (0, 0))],
                  outs=[((T, width), BF, (tm, width), lambda i: (i, 0))], semantics=("parallel",))[0]


def rms_bwd(x, gain, dy, *, name, col_block=0, dres=None, want_f32=True, want_bf=True, deps=()):
    T, width = dy.shape
    tm = _tile(T, ROW_TILE, 8)
    has_res = dres is not None

    def body(*refs):
        x_ref, g_ref, dy_ref = refs[:3]
        pos = 3
        res_ref = None
        if has_res:
            res_ref = refs[pos]
            pos += 1
        outs = refs[pos:]
        dx, dg_rows = _rms_bwd(x_ref[...], g_ref[...], dy_ref[...])
        if has_res:
            dx = dx + res_ref[...]
        o = 0
        if want_f32:
            outs[o][...] = dx
            o += 1
        if want_bf:
            outs[o][...] = dx.astype(BF)
            o += 1
        _acc_rows(outs[o], dg_rows, pl.program_id(0) == 0)

    ins = [(x, (tm, width), lambda i: (i, col_block)), (gain, (1, width), lambda i: (0, 0)),
           (dy, (tm, width), lambda i: (i, 0))]
    if has_res:
        ins.append((dres, (tm, width), lambda i: (i, 0)))
    outs = []
    if want_f32:
        outs.append(((T, width), F32, (tm, width), lambda i: (i, 0)))
    if want_bf:
        outs.append(((T, width), BF, (tm, width), lambda i: (i, 0)))
    outs.append(((1, width), F32, (1, width), lambda i: (0, 0)))
    return _pcall(body, name=name, grid=(T // tm,), ins=ins, outs=outs, deps=deps)


def mla_prep(proj, q_norm, kv_norm, cos_t, sin_t, *, name):
    T = proj.shape[0]
    tm = _tile(T, ROW_TILE, 8)
    kr_block = (proj.shape[1] - LANES) // LANES

    def body(cq_ref, ckv_ref, kr_ref, qg_ref, kg_ref, cos_ref, sin_ref, qn_ref, kvn_ref, krope_ref):
        cq = cq_ref[...]
        qn_ref[...] = (cq * _rstd(cq) * qg_ref[...]).astype(BF)
        ckv = ckv_ref[...]
        kvn_ref[...] = (ckv * _rstd(ckv) * kg_ref[...]).astype(BF)
        krope_ref[...] = _rope_fwd(kr_ref[...], cos_ref[...], sin_ref[...]).astype(BF)

    return _pcall(
        body, name=name, grid=(T // tm,),
        ins=[(proj, (tm, Q_LORA), lambda i: (i, 0)), (proj, (tm, KV_LORA), lambda i: (i, 1)),
             (proj, (tm, LANES), lambda i: (i, kr_block)),
             (q_norm, (1, Q_LORA), lambda i: (0, 0)), (kv_norm, (1, KV_LORA), lambda i: (0, 0)),
             (cos_t, (tm, LANES), lambda i: (i, 0)), (sin_t, (tm, LANES), lambda i: (i, 0))],
        outs=[((T, Q_LORA), BF, (tm, Q_LORA), lambda i: (i, 0)), ((T, KV_LORA), BF, (tm, KV_LORA), lambda i: (i, 0)),
              ((T, LANES), BF, (tm, LANES), lambda i: (i, 0))],
        semantics=("parallel",))


def _attn_probs(q_ref, k_ref, tq, T):
    s = lax.dot_general(q_ref[...], k_ref[...], (((1,), (1,)), ((), ())), preferred_element_type=F32) * ATTN_SCALE
    row = pl.program_id(1) * tq + lax.broadcasted_iota(jnp.int32, (tq, T), 0)
    col = lax.broadcasted_iota(jnp.int32, (tq, T), 1)
    s = jnp.where(col <= row, s, -jnp.inf)
    e = jnp.exp(s - jnp.max(s, axis=-1, keepdims=True))
    return e / jnp.sum(e, axis=-1, keepdims=True)


def attn_fwd(q, k, v, *, name):
    T = q.shape[0]
    tq = _tile(T, ROW_TILE, 8)

    def body(q_ref, k_ref, v_ref, o_ref):
        p = _attn_probs(q_ref, k_ref, tq, T)
        o_ref[...] = jnp.dot(p.astype(BF), v_ref[...], preferred_element_type=F32)

    return _pcall(
        body, name=name, grid=(HEADS, T // tq),
        ins=[(q, (tq, HEAD_PAD), lambda h, i: (i, h)), (k, (T, HEAD_PAD), lambda h, i: (0, h)),
             (v, (T, V_HEAD), lambda h, i: (0, h))],
        outs=[((T, MLA_OUT), F32, (tq, V_HEAD), lambda h, i: (i, h))], semantics=("parallel", "parallel"))[0]


def attn_bwd(q, k, v, do, *, name):
    T = q.shape[0]
    tq = _tile(T, ROW_TILE, 8)

    def body(q_ref, k_ref, v_ref, do_ref, dq_ref, dk_ref, dv_ref):
        p = _attn_probs(q_ref, k_ref, tq, T)
        do_t = do_ref[...]
        dp = lax.dot_general(do_t, v_ref[...], (((1,), (1,)), ((), ())), preferred_element_type=F32)
        ds = (p * (dp - jnp.sum(p * dp, axis=-1, keepdims=True)) * ATTN_SCALE).astype(BF)
        dq_ref[...] = jnp.dot(ds, k_ref[...], preferred_element_type=F32)
        dk_t = lax.dot_general(ds, q_ref[...], (((0,), (0,)), ((), ())), preferred_element_type=F32)
        dv_t = lax.dot_general(p.astype(BF), do_t, (((0,), (0,)), ((), ())), preferred_element_type=F32)
        first = pl.program_id(1) == 0

        @pl.when(first)
        def _():
            dk_ref[...] = dk_t
            dv_ref[...] = dv_t

        @pl.when(jnp.logical_not(first))
        def _():
            dk_ref[...] += dk_t
            dv_ref[...] += dv_t

    return _pcall(
        body, name=name, grid=(HEADS, T // tq),
        ins=[(q, (tq, HEAD_PAD), lambda h, i: (i, h)), (k, (T, HEAD_PAD), lambda h, i: (0, h)),
             (v, (T, V_HEAD), lambda h, i: (0, h)), (do, (tq, V_HEAD), lambda h, i: (i, h))],
        outs=[((T, HEADS * HEAD_PAD), F32, (tq, HEAD_PAD), lambda h, i: (i, h)),
              ((T, HEADS * HEAD_PAD), F32, (T, HEAD_PAD), lambda h, i: (0, h)),
              ((T, MLA_OUT), F32, (T, V_HEAD), lambda h, i: (0, h))],
        semantics=("parallel", "arbitrary"))


def mla_bwd_prep(dq, dk, dv, cos_t, sin_t, *, name):
    T = dq.shape[0]
    tm = _tile(T, ROW_TILE, 8)

    def body(dq_ref, dk_ref, dv_ref, cos_ref, sin_ref, dql_ref, dkvl_ref, dkr_ref):
        cos_v, sin_v = cos_ref[...], sin_ref[...]
        kr = jnp.zeros((tm, LANES), F32)
        for h in range(HEADS):
            lo = h * HEAD_PAD
            dql_ref[:, lo:lo + QK_NOPE] = dq_ref[:, lo:lo + QK_NOPE].astype(BF)
            dql_ref[:, lo + QK_NOPE:lo + HEAD_PAD] = _rope_bwd(
                dq_ref[:, lo + QK_NOPE:lo + HEAD_PAD], cos_v, sin_v).astype(BF)
            dkvl_ref[:, lo:lo + QK_NOPE] = dk_ref[:, lo:lo + QK_NOPE].astype(BF)
            dkvl_ref[:, lo + QK_NOPE:lo + HEAD_PAD] = dv_ref[:, h * V_HEAD:(h + 1) * V_HEAD].astype(BF)
            kr = kr + dk_ref[:, lo + QK_NOPE:lo + HEAD_PAD]
        dkr_ref[...] = _rope_bwd(kr, cos_v, sin_v).astype(BF)

    W = HEADS * HEAD_PAD
    return _pcall(
        body, name=name, grid=(T // tm,),
        ins=[(dq, (tm, W), lambda i: (i, 0)), (dk, (tm, W), lambda i: (i, 0)), (dv, (tm, MLA_OUT), lambda i: (i, 0)),
             (cos_t, (tm, LANES), lambda i: (i, 0)), (sin_t, (tm, LANES), lambda i: (i, 0))],
        outs=[((T, W), BF, (tm, W), lambda i: (i, 0)), ((T, W), BF, (tm, W), lambda i: (i, 0)),
              ((T, LANES), BF, (tm, LANES), lambda i: (i, 0))],
        semantics=("parallel",))


def _group_norm_stats(vg):
    mu = jnp.mean(vg, axis=-1, keepdims=True)
    d = vg - mu
    r = lax.rsqrt(jnp.mean(d * d, axis=-1, keepdims=True) + EPS)
    return d * r, r


def mix_fwd(a, proj, g_mla, g_sgu, v_gain, w_tril, b_full, *, name):
    T = a.shape[0]
    tm = _tile(T, ROW_TILE, CHUNK)
    n_chunk = tm // CHUNK

    def body(a_ref, u_ref, v_ref, gm_ref, gs_ref, vg_ref, w_ref, b_ref, o_ref, s_scr):
        av = a_ref[...]
        o_ref[:, :MLA_OUT] = (av * _rstd(av) * gm_ref[...]).astype(BF)
        for g in range(GROUPS):
            sl = slice(g * CH, (g + 1) * CH)
            vhat, _ = _group_norm_stats(_gelu(v_ref[:, sl]))
            vn = (vhat * vg_ref[:, sl]).astype(BF)
            u = _gelu(u_ref[:, sl])
            for ci in range(n_chunk):
                rs = slice(ci * CHUNK, (ci + 1) * CHUNK)
                y = jnp.dot(w_ref[g], vn[rs], preferred_element_type=F32) + b_ref[:, sl]
                s_scr[rs, sl] = u[rs] * y
        s = s_scr[...]
        o_ref[:, MLA_OUT:] = (s * _rstd(s) * gs_ref[...]).astype(BF)

    return _pcall(
        body, name=name, grid=(T // tm,),
        ins=[(a, (tm, MLA_OUT), lambda i: (i, 0)), (proj, (tm, SGU_OUT), lambda i: (i, 1)),
             (proj, (tm, SGU_OUT), lambda i: (i, 2)), (g_mla, (1, MLA_OUT), lambda i: (0, 0)),
             (g_sgu, (1, SGU_OUT), lambda i: (0, 0)), (v_gain, (1, SGU_OUT), lambda i: (0, 0)),
             (w_tril, (GROUPS, CHUNK, CHUNK), lambda i: (0, 0, 0)), (b_full, (CHUNK, SGU_OUT), lambda i: (0, 0))],
        outs=[((T, MLA_OUT + SGU_OUT), BF, (tm, MLA_OUT + SGU_OUT), lambda i: (i, 0))],
        scratch=[pltpu.VMEM((tm, SGU_OUT), F32)], semantics=("parallel",))[0]


def mix_bwd(dmixed, a, proj, g_mla, g_sgu, v_gain, w_tril, w_tril_t, b_full, *, name):
    T = a.shape[0]
    tm = _tile(T, ROW_TILE, CHUNK)
    n_chunk = tm // CHUNK

    def body(dm_a_ref, dm_s_ref, a_ref, u_ref, v_ref, gm_ref, gs_ref, vg_ref, w_ref, wt_ref, b_ref,
             da_ref, duv_ref, dgm_ref, dgs_ref, dvg_ref, dw_ref, db_ref, s_scr, y_scr):
        first = pl.program_id(0) == 0
        da, dgm_rows = _rms_bwd(a_ref[...], gm_ref[...], dm_a_ref[...])
        da_ref[...] = da.astype(BF)
        _acc_rows(dgm_ref, dgm_rows, first)

        for g in range(GROUPS):
            sl = slice(g * CH, (g + 1) * CH)
            vhat, _ = _group_norm_stats(_gelu(v_ref[:, sl]))
            vn = (vhat * vg_ref[:, sl]).astype(BF)
            u = _gelu(u_ref[:, sl])
            for ci in range(n_chunk):
                rs = slice(ci * CHUNK, (ci + 1) * CHUNK)
                y = jnp.dot(w_ref[g], vn[rs], preferred_element_type=F32) + b_ref[:, sl]
                y_scr[rs, sl] = y
                s_scr[rs, sl] = u[rs] * y
        ds, dgs_rows = _rms_bwd(s_scr[...], gs_ref[...], dm_s_ref[...])
        _acc_rows(dgs_ref, dgs_rows, first)
        s_scr[...] = ds

        @pl.when(first)
        def _():
            dw_ref[...] = jnp.zeros_like(dw_ref)
            db_ref[...] = jnp.zeros_like(db_ref)

        for g in range(GROUPS):
            sl = slice(g * CH, (g + 1) * CH)
            upre = u_ref[:, sl]
            vpre = v_ref[:, sl]
            u = _gelu(upre)
            vhat, r = _group_norm_stats(_gelu(vpre))
            gain = vg_ref[:, sl]
            vn = (vhat * gain).astype(BF)
            dsg = s_scr[:, sl]
            duv_ref[:, sl] = (dsg * y_scr[:, sl] * _gelu_grad(upre)).astype(BF)
            dy = dsg * u
            dyb = dy.astype(BF)
            dvn_parts = []
            for ci in range(n_chunk):
                rs = slice(ci * CHUNK, (ci + 1) * CHUNK)
                dvn_parts.append(jnp.dot(wt_ref[g], dyb[rs], preferred_element_type=F32))
                dw_ref[g] += lax.dot_general(dyb[rs], vn[rs], (((1,), (1,)), ((), ())), preferred_element_type=F32)
                db_ref[:, sl] += jnp.broadcast_to(jnp.sum(dy[rs], axis=-1, keepdims=True), (CHUNK, CH))
            dvn = dvn_parts[0] if n_chunk == 1 else jnp.concatenate(dvn_parts, axis=0)
            _acc_rows(dvg_ref.at[:, sl], dvn * vhat, first)
            dvh = dvn * gain
            dvg = r * (dvh - jnp.mean(dvh, axis=-1, keepdims=True)
                       - vhat * jnp.mean(dvh * vhat, axis=-1, keepdims=True))
            duv_ref[:, SGU_OUT + g * CH:SGU_OUT + (g + 1) * CH] = (dvg * _gelu_grad(vpre)).astype(BF)

    return _pcall(
        body, name=name, grid=(T // tm,),
        ins=[(dmixed, (tm, MLA_OUT), lambda i: (i, 0)), (dmixed, (tm, SGU_OUT), lambda i: (i, 1)),
             (a, (tm, MLA_OUT), lambda i: (i, 0)), (proj, (tm, SGU_OUT), lambda i: (i, 1)),
             (proj, (tm, SGU_OUT), lambda i: (i, 2)), (g_mla, (1, MLA_OUT), lambda i: (0, 0)),
             (g_sgu, (1, SGU_OUT), lambda i: (0, 0)), (v_gain, (1, SGU_OUT), lambda i: (0, 0)),
             (w_tril, (GROUPS, CHUNK, CHUNK), lambda i: (0, 0, 0)), (w_tril_t, (GROUPS, CHUNK, CHUNK), lambda i: (0, 0, 0)),
             (b_full, (CHUNK, SGU_OUT), lambda i: (0, 0))],
        outs=[((T, MLA_OUT), BF, (tm, MLA_OUT), lambda i: (i, 0)),
              ((T, 2 * SGU_OUT), BF, (tm, 2 * SGU_OUT), lambda i: (i, 0)),
              ((1, MLA_OUT), F32, (1, MLA_OUT), lambda i: (0, 0)), ((1, SGU_OUT), F32, (1, SGU_OUT), lambda i: (0, 0)),
              ((1, SGU_OUT), F32, (1, SGU_OUT), lambda i: (0, 0)),
              ((GROUPS, CHUNK, CHUNK), F32, (GROUPS, CHUNK, CHUNK), lambda i: (0, 0, 0)),
              ((CHUNK, SGU_OUT), F32, (CHUNK, SGU_OUT), lambda i: (0, 0))],
        scratch=[pltpu.VMEM((tm, SGU_OUT), F32), pltpu.VMEM((tm, SGU_OUT), F32)])


def _shift_down(z, n, row):
    return jnp.where(row >= n, pltpu.roll(z, n, 0), 0.0)


def _shift_up(z, n, row, T):
    return jnp.where(row < T - n, pltpu.roll(z, T - n, 0), 0.0)


def conv_fwd(proj, conv_w, *, name):
    T, D3 = proj.shape
    D = D3 // 3
    tn = _tile(D, 256)
    nj = D // tn

    def body(b_ref, c_ref, x_ref, w_ref, o_ref):
        row = lax.broadcasted_iota(jnp.int32, (T, tn), 0)
        z = c_ref[...] * x_ref[...]
        zc = w_ref[2:3, :] * z + w_ref[1:2, :] * _shift_down(z, 1, row) + w_ref[0:1, :] * _shift_down(z, 2, row)
        o_ref[...] = (b_ref[...] * zc).astype(BF)

    return _pcall(
        body, name=name, grid=(nj,),
        ins=[(proj, (T, tn), lambda j: (0, j)), (proj, (T, tn), lambda j: (0, nj + j)),
             (proj, (T, tn), lambda j: (0, 2 * nj + j)), (conv_w, (3, tn), lambda j: (0, j))],
        outs=[((T, D), BF, (T, tn), lambda j: (0, j))], semantics=("parallel",))[0]


def conv_bwd(dg, proj, conv_w, *, name):
    T, D3 = proj.shape
    D = D3 // 3
    tn = _tile(D, 256)
    nj = D // tn

    def body(dg_ref, b_ref, c_ref, x_ref, w_ref, dp_ref, dw_ref, dc_scr, dx_scr):
        part = pl.program_id(1)

        @pl.when(part == 0)
        def _():
            row = lax.broadcasted_iota(jnp.int32, (T, tn), 0)
            c, x = c_ref[...], x_ref[...]
            z = c * x
            z1 = _shift_down(z, 1, row)
            z2 = _shift_down(z, 2, row)
            dgv = dg_ref[...]
            zc = w_ref[2:3, :] * z + w_ref[1:2, :] * z1 + w_ref[0:1, :] * z2
            dp_ref[...] = (dgv * zc).astype(BF)
            dzc = dgv * b_ref[...]
            dw_ref[0:1, :] = jnp.sum(dzc * z2, axis=0, keepdims=True)
            dw_ref[1:2, :] = jnp.sum(dzc * z1, axis=0, keepdims=True)
            dw_ref[2:3, :] = jnp.sum(dzc * z, axis=0, keepdims=True)
            dz = (w_ref[2:3, :] * dzc + w_ref[1:2, :] * _shift_up(dzc, 1, row, T)
                  + w_ref[0:1, :] * _shift_up(dzc, 2, row, T))
            dc_scr[...] = (dz * x).astype(BF)
            dx_scr[...] = (dz * c).astype(BF)

        @pl.when(part == 1)
        def _():
            dp_ref[...] = dc_scr[...]

        @pl.when(part == 2)
        def _():
            dp_ref[...] = dx_scr[...]

    return _pcall(
        body, name=name, grid=(nj, 3),
        ins=[(dg, (T, tn), lambda j, p: (0, j)), (proj, (T, tn), lambda j, p: (0, j)),
             (proj, (T, tn), lambda j, p: (0, nj + j)), (proj, (T, tn), lambda j, p: (0, 2 * nj + j)),
             (conv_w, (3, tn), lambda j, p: (0, j))],
        outs=[((T, D3), BF, (T, tn), lambda j, p: (0, p * nj + j)), ((3, D), F32, (3, tn), lambda j, p: (0, j))],
        scratch=[pltpu.VMEM((T, tn), BF), pltpu.VMEM((T, tn), BF)], semantics=("parallel", "arbitrary"))


def loss_bwd(x, gain, target, *, name):
    T, D = x.shape
    tm = _tile(T, ROW_TILE, 8)

    def body(x_ref, g_ref, t_ref, dx_ref, dxb_ref, dg_ref, loss_ref):
        first = pl.program_id(0) == 0
        xv = x_ref[...]
        r = _rstd(xv)
        xh = xv * r
        gain_v = g_ref[...]
        err = xh * gain_v - t_ref[...]
        part = 0.5 * jnp.sum(jnp.mean(err * err, axis=-1, keepdims=True), axis=0, keepdims=True)
        _acc_rows(loss_ref, jnp.broadcast_to(part, (1, LANES)), first)
        dy = err * (1.0 / D)
        gdy = dy * gain_v
        dx = r * (gdy - xh * jnp.mean(gdy * xh, axis=-1, keepdims=True))
        dx_ref[...] = dx
        dxb_ref[...] = dx.astype(BF)
        _acc_rows(dg_ref, dy * xh, first)

    return _pcall(
        body, name=name, grid=(T // tm,),
        ins=[(x, (tm, D), lambda i: (i, 0)), (gain, (1, D), lambda i: (0, 0)), (target, (tm, D), lambda i: (i, 0))],
        outs=[((T, D), F32, (tm, D), lambda i: (i, 0)), ((T, D), BF, (tm, D), lambda i: (i, 0)),
              ((1, D), F32, (1, D), lambda i: (0, 0)), ((1, LANES), F32, (1, LANES), lambda i: (0, 0))])


def _adamw(g, w, m, v):
    m = ADAM_B1 * m + (1.0 - ADAM_B1) * g
    v = ADAM_B2 * v + (1.0 - ADAM_B2) * (g * g)
    m_hat = m / ADAM_C1
    v_hat = v / ADAM_C2
    delta = -ADAM_LR * (m_hat / (jnp.sqrt(v_hat) + ADAM_EPS) + ADAM_WD * w)
    return delta, m, v


def adam_flat(g, w, m, v, *, name):
    def body(g_ref, w_ref, m_ref, v_ref, d_ref, nm_ref, nv_ref):
        d, nm, nv = _adamw(g_ref[...], w_ref[...], m_ref[...], v_ref[...])
        d_ref[...] = d
        nm_ref[...] = nm
        nv_ref[...] = nv

    blk = g.shape
    zero = lambda: (0, 0)
    return _pcall(body, name=name, grid=(),
                  ins=[(t, blk, zero) for t in (g, w, m, v)],
                  outs=[(blk, F32, blk, zero)] * 3)


def _chip_slots():
    x, y, c = lax.axis_index("x"), lax.axis_index("y"), lax.axis_index("c")
    chips = [(1 - x, y), (x, 1 - y), (1 - x, 1 - y)]
    return x, y, c, chips


def reduce_adam(gs, a_buf, b_buf, w, m, v, layer, prev, *, name, deps=()):
    L, R, C = w.shape
    tr = _tile(R, 256, 8)
    x, y, c, _ = _chip_slots()
    idx = jnp.stack([4 * x + 2 * y + c, 2 * x + y]).astype(jnp.int32)
    n_prev = 0 if prev is None else 4

    def body(idx_ref, g_ref, a_ref, b0_ref, b1_ref, b2_ref, w_ref, m_ref, v_ref, *rest):
        outs = rest[n_prev:]
        g = ((((g_ref[...].astype(F32) + a_ref[...].astype(F32)) + b0_ref[...].astype(F32))
              + b1_ref[...].astype(F32)) + b2_ref[...].astype(F32))
        d, nm, nv = _adamw(g, w_ref[...], m_ref[...], v_ref[...])
        outs[0][...] = g
        outs[1][...] = d
        outs[2][...] = nm
        outs[3][...] = nv

    blk3 = (None, tr, C)
    ins = [(gs, blk3, lambda i, s: (s[0], i, 0)), (a_buf, blk3, lambda i, s: (s[1], i, 0)),
           (b_buf, blk3, lambda i, s: (0, i, 0)), (b_buf, blk3, lambda i, s: (1, i, 0)),
           (b_buf, blk3, lambda i, s: (2, i, 0)),
           (w, blk3, lambda i, s: (layer, i, 0)), (m, blk3, lambda i, s: (layer, i, 0)),
           (v, blk3, lambda i, s: (layer, i, 0))]
    aliases = {}
    if prev is not None:
        for o, p in enumerate(prev):
            ins.append((p, None, None))
            aliases[1 + 8 + o] = o
    outs = [((L, R, C), F32, blk3, lambda i, s: (layer, i, 0))] * 4
    return _pcall(body, name=name, grid=(R // tr,), ins=ins, outs=outs, prefetch=idx, aliases=aliases,
                  semantics=("parallel",), deps=deps)


def pair_sum(gs, a_buf, *, name):
    _, R, C = gs.shape
    tr = _tile(R, 256, 8)
    x, y, c, chips = _chip_slots()
    idx = jnp.stack([4 * cx + 2 * cy + c for cx, cy in chips] + [2 * cx + cy for cx, cy in chips]).astype(jnp.int32)

    def body(idx_ref, g_ref, a_ref, o_ref):
        o_ref[...] = (g_ref[...].astype(F32) + a_ref[...].astype(F32)).astype(BF)

    blk3 = (None, tr, C)
    return _pcall(body, name=name, grid=(3, R // tr),
                  ins=[(gs, blk3, lambda j, i, s: (s[j], i, 0)), (a_buf, blk3, lambda j, i, s: (s[3 + j], i, 0))],
                  outs=[((3, R, C), BF, blk3, lambda j, i, s: (j, i, 0))], prefetch=idx,
                  semantics=("parallel", "parallel"))[0]


def sum_rows8(gathered, rows, *, name):
    W = gathered.shape[1]

    def body(g_ref, o_ref):
        acc = g_ref[0:rows, :]
        for d in range(1, N_DEV):
            acc = acc + g_ref[d * rows:(d + 1) * rows, :]
        o_ref[...] = acc

    return _pcall(body, name=name, grid=(), ins=[(gathered, gathered.shape, lambda: (0, 0))],
                  outs=[((rows, W), F32, (rows, W), lambda: (0, 0))])[0]


def _comm_call(body, *, name, ins, out_shapes, n_sem_rows, n_sem_cols, extra_scratch=()):
    any_spec = pl.BlockSpec(memory_space=pl.ANY)
    return pl.pallas_call(
        body, name=name, out_shape=[jax.ShapeDtypeStruct(s, d) for s, d in out_shapes],
        in_specs=[any_spec] * len(ins), out_specs=[any_spec] * len(out_shapes),
        scratch_shapes=[pltpu.SemaphoreType.DMA((n_sem_rows, n_sem_cols)),
                        pltpu.SemaphoreType.DMA((n_sem_rows, n_sem_cols))] + list(extra_scratch),
    )(*ins)


HBM_SPEC = pl.BlockSpec(memory_space=pltpu.HBM)
SEM_SPEC = pl.BlockSpec(memory_space=pltpu.SEMAPHORE)
ANY_SPEC = pl.BlockSpec(memory_space=pl.ANY)
DATAFLOW = pltpu.SideEffectType.DATAFLOW_SIDE_EFFECTING


def _in_hbm(v):
    return pltpu.with_memory_space_constraint(v, pltpu.HBM)


def _slot(p):
    return 4 * p[0] + 2 * p[1] + p[2]


def _gather_peers():
    x, y, c, chips = _chip_slots()
    return (x, y, c), [(x, y, 1 - c)] + [(*chip, c) for chip in chips]


def gather_start(groups, after, *, name):
    flat = [s for g in groups for s in g]
    n, n_g = len(flat), len(groups)
    where = [(gi, ti) for gi, g in enumerate(groups) for ti in range(len(g))]

    def body(*refs):
        src, land = refs[:n], refs[n:2 * n]
        sems = refs[2 * n + 1:2 * n + 1 + 2 * n_g]
        me, peers = _gather_peers()
        for t in range(n):
            gi, ti = where[t]
            for k, to in enumerate(peers):
                pltpu.make_async_remote_copy(
                    src_ref=src[t], dst_ref=land[t].at[_slot(me)], send_sem=sems[2 * gi].at[4 * ti + k],
                    recv_sem=sems[2 * gi + 1].at[4 * ti + k], device_id=to, device_id_type=MESH).start()

    out_shape = []
    for g in groups:
        out_shape += [pltpu.SemaphoreType.DMA((4 * len(g),)), pltpu.SemaphoreType.DMA((4 * len(g),))]
    out_shape += [pltpu.HBM(s.shape, s.dtype) for s in flat]
    out_shape += [pltpu.HBM((N_DEV,) + s.shape, s.dtype) for s in flat]
    aliases = {t: 2 * n_g + t for t in range(n)}
    aliases.update({n + t: 2 * n_g + n + t for t in range(n)})
    res = pl.pallas_call(
        body, name=name, out_shape=out_shape, in_specs=[HBM_SPEC] * (2 * n) + [ANY_SPEC],
        out_specs=[SEM_SPEC] * (2 * n_g) + [HBM_SPEC] * (2 * n), input_output_aliases=aliases,
        compiler_params=pltpu.CompilerParams(has_side_effects=DATAFLOW),
    )(*[_in_hbm(s) for s in flat], *[_in_hbm(lax.empty((N_DEV,) + s.shape, s.dtype)) for s in flat], after)
    out, off = [], 0
    for gi, g in enumerate(groups):
        k = len(g)
        out.append((res[2 * gi], res[2 * gi + 1], res[2 * n_g + off:2 * n_g + off + k],
                    res[2 * n_g + n + off:2 * n_g + n + off + k]))
        off += k
    return out


def gather_wait(started, after, *, name):
    send_sems, recv_sems, srcs, lands = started
    n = len(srcs)

    def body(*refs):
        src, land = refs[:n], refs[n:2 * n]
        send, recv = refs[2 * n], refs[2 * n + 1]
        _, peers = _gather_peers()
        for t in range(n):
            for k, frm in enumerate(peers):
                cp = pltpu.make_async_remote_copy(
                    src_ref=src[t], dst_ref=land[t].at[_slot(frm)], send_sem=send.at[4 * t + k],
                    recv_sem=recv.at[4 * t + k],
                    device_id=frm, device_id_type=MESH)
                cp.wait_send()
                cp.wait_recv()

    res = pl.pallas_call(
        body, name=name,
        out_shape=[pltpu.HBM(s.shape, s.dtype) for s in srcs] + [pltpu.HBM(l.shape, l.dtype) for l in lands],
        in_specs=[HBM_SPEC] * (2 * n) + [SEM_SPEC, SEM_SPEC, ANY_SPEC], out_specs=[HBM_SPEC] * (2 * n),
        input_output_aliases={t: t for t in range(2 * n)},
        compiler_params=pltpu.CompilerParams(has_side_effects=DATAFLOW),
    )(*srcs, *lands, send_sems, recv_sems, after)
    return res[:n], res[n:]


def place_own(src, land, *, name):
    R, C = src.shape
    tr = _tile(R, 512, 16)
    x, y, c, _ = _chip_slots()
    idx = jnp.stack([4 * x + 2 * y + c]).astype(jnp.int32)

    def body(idx_ref, s_ref, land_ref, o_ref):
        o_ref[...] = s_ref[...]

    return _pcall(body, name=name, grid=(R // tr,),
                  ins=[(src, (tr, C), lambda i, s: (i, 0)), (land, None, None)],
                  outs=[(land.shape, land.dtype, (None, tr, C), lambda i, s: (s[0], i, 0))],
                  prefetch=idx, aliases={2: 0}, semantics=("parallel",))[0]


def gather_finish(srcs, lands, *, name):
    n = len(srcs)

    def body(*refs):
        land = refs[n:2 * n]
        send_sems, recv_sems = refs[2 * n:]
        x, y, c, chips = _chip_slots()
        me, sibling = (x, y, c), (x, y, 1 - c)

        def copy(t, j, block, to):
            return pltpu.make_async_remote_copy(
                src_ref=land[t].at[_slot(block)], dst_ref=land[t].at[_slot(block)], send_sem=send_sems.at[t, j],
                recv_sem=recv_sems.at[t, j], device_id=to, device_id_type=MESH)

        sends = [copy(t, j, (*chip, c), sibling) for t in range(n) for j, chip in enumerate(chips)]
        for cp in sends:
            cp.start()
        for t in range(n):
            for j, chip in enumerate(chips):
                copy(t, j, (*chip, 1 - c), me).wait_recv()
        for cp in sends:
            cp.wait_send()

    passed = pl.pallas_call(
        body, name=name, out_shape=[jax.ShapeDtypeStruct(l.shape, l.dtype) for l in lands],
        in_specs=[ANY_SPEC] * n, out_specs=[ANY_SPEC] * n,
        input_output_aliases={t: t for t in range(n)},
        scratch_shapes=[pltpu.SemaphoreType.DMA((n, 3)), pltpu.SemaphoreType.DMA((n, 3))],
    )(*lands)
    return [place_own(s, l, name=f"{name}_own{t}") for t, (s, l) in enumerate(zip(srcs, passed))]


def chips_start(pairs, *, name):
    n = len(pairs)

    def body(*refs):
        src, land = refs[:n], refs[n:2 * n]
        send, recv = refs[2 * n], refs[2 * n + 1]
        token = refs[-1]
        x, y, c, chips = _chip_slots()
        for t in range(n):
            for j, chip in enumerate(chips):
                pltpu.make_async_remote_copy(
                    src_ref=src[t].at[j], dst_ref=land[t].at[j], send_sem=send.at[3 * t + j],
                    recv_sem=recv.at[3 * t + j], device_id=(*chip, c), device_id_type=MESH).start()
        token[...] = jnp.zeros_like(token)

    res = pl.pallas_call(
        body, name=name,
        out_shape=[pltpu.SemaphoreType.DMA((3 * n,)), pltpu.SemaphoreType.DMA((3 * n,))]
        + [pltpu.HBM(p.shape, p.dtype) for p in pairs] * 2 + [jax.ShapeDtypeStruct((8, LANES), F32)],
        in_specs=[HBM_SPEC] * (2 * n),
        out_specs=[SEM_SPEC, SEM_SPEC] + [HBM_SPEC] * (2 * n) + [pl.BlockSpec(memory_space=pltpu.VMEM)],
        input_output_aliases={t: 2 + t for t in range(2 * n)},
        compiler_params=pltpu.CompilerParams(has_side_effects=DATAFLOW),
    )(*[_in_hbm(p) for p in pairs], *[_in_hbm(lax.empty(p.shape, p.dtype)) for p in pairs])
    return res[0], res[1], res[2:2 + n], res[2 + n:2 + 2 * n], res[-1]


def chips_wait(started, after, *, name):
    send_sems, recv_sems, srcs, lands, _ = started
    n = len(srcs)

    def body(*refs):
        src, land = refs[:n], refs[n:2 * n]
        send, recv = refs[2 * n], refs[2 * n + 1]
        x, y, c, chips = _chip_slots()
        for t in range(n):
            for j, chip in enumerate(chips):
                cp = pltpu.make_async_remote_copy(
                    src_ref=src[t].at[j], dst_ref=land[t].at[j], send_sem=send.at[3 * t + j],
                    recv_sem=recv.at[3 * t + j], device_id=(*chip, c), device_id_type=MESH)
                cp.wait_send()
                cp.wait_recv()

    res = pl.pallas_call(
        body, name=name, out_shape=[pltpu.HBM(s.shape, s.dtype) for s in srcs] * 2,
        in_specs=[HBM_SPEC] * (2 * n) + [SEM_SPEC, SEM_SPEC, ANY_SPEC], out_specs=[HBM_SPEC] * (2 * n),
        input_output_aliases={t: t for t in range(2 * n)},
        compiler_params=pltpu.CompilerParams(has_side_effects=DATAFLOW),
    )(*srcs, *lands, send_sems, recv_sems, after)
    return res[n:]


def exchange_sibling(gs, *, name):
    n = len(gs)

    def body(*refs):
        src, dst = refs[:n], refs[n:2 * n]
        send_sems, recv_sems = refs[2 * n:]
        x, y, c, _ = _chip_slots()
        copies = []
        for t in range(n):
            for q in range(4):
                qx, qy = q // 2, q % 2
                copies.append(pltpu.make_async_remote_copy(
                    src_ref=src[t].at[4 * qx + 2 * qy + (1 - c)], dst_ref=dst[t].at[q],
                    send_sem=send_sems.at[t, q], recv_sem=recv_sems.at[t, q],
                    device_id=(x, y, 1 - c), device_id_type=MESH))
        for cp in copies:
            cp.start()
        for cp in copies:
            cp.wait()

    return _comm_call(body, name=name, ins=list(gs), out_shapes=[((4,) + g.shape[1:], g.dtype) for g in gs],
                      n_sem_rows=n, n_sem_cols=4)


def all_gather_vmem(x_shard, *, name, after=None):
    m_per, n = x_shard.shape
    n_after = 0 if after is None else 1

    def body(x_ref, *rest):
        out_ref, send_sems, recv_sems, local_sem = rest[n_after:]
        x, y, c, chips = _chip_slots()
        me, sibling = (x, y, c), (x, y, 1 - c)

        def rows(px, py, pc):
            return out_ref.at[pl.ds((4 * px + 2 * py + pc) * m_per, m_per), :]

        def copy(k, block, to, src=None):
            return pltpu.make_async_remote_copy(
                src_ref=rows(*block) if src is None else src, dst_ref=rows(*block),
                send_sem=send_sems.at[k], recv_sem=recv_sems.at[k], device_id=to, device_id_type=MESH)

        mine = pltpu.make_async_copy(x_ref, rows(*me), local_sem)
        mine.start()
        first = [copy(0, me, sibling, src=x_ref)]
        first += [copy(1 + j, me, (*chip, c), src=x_ref) for j, chip in enumerate(chips)]
        for cp in first:
            cp.start()
        passed = [copy(4 + j, (*chip, c), sibling) for j, chip in enumerate(chips)]
        for j, chip in enumerate(chips):
            copy(1 + j, (*chip, c), me).wait_recv()
            passed[j].start()
        copy(0, sibling, me).wait_recv()
        for j, chip in enumerate(chips):
            copy(4 + j, (*chip, 1 - c), me).wait_recv()
        for cp in first + passed:
            cp.wait_send()
        mine.wait()

    vmem = pl.BlockSpec(memory_space=pltpu.VMEM)
    return pl.pallas_call(
        body, name=name, out_shape=jax.ShapeDtypeStruct((N_DEV * m_per, n), x_shard.dtype),
        in_specs=[vmem] + [ANY_SPEC] * n_after, out_specs=vmem,
        scratch_shapes=[pltpu.SemaphoreType.DMA((7,)), pltpu.SemaphoreType.DMA((7,)), pltpu.SemaphoreType.DMA],
        compiler_params=pltpu.CompilerParams(vmem_limit_bytes=int(min(
            VMEM_LIMIT_CAP, 2 * (N_DEV + 1) * m_per * n * x_shard.dtype.itemsize + 16 * 2 ** 20))),
    )(x_shard, *([] if after is None else [after]))


def _rope_slab(cols):
    z = jnp.zeros(cols.shape[:-1] + (HALF_ROPE,), cols.dtype)
    return jnp.concatenate([cols[..., :HALF_ROPE], z, cols[..., HALF_ROPE:], z], axis=-1)


def _rope_unslab(slab):
    return jnp.concatenate([slab[..., :HALF_ROPE], slab[..., 2 * HALF_ROPE:3 * HALF_ROPE]], axis=-1)


def _pack_w_in(w_g):
    s, d, c = w_g.shape
    w = jnp.transpose(w_g, (1, 0, 2)).reshape(d, s * c)
    c1, c2, c3 = Q_LORA, Q_LORA + KV_LORA, Q_LORA + KV_LORA + QK_ROPE
    return jnp.concatenate([w[:, :c2], w[:, c3:], _rope_slab(w[:, c2:c3])], axis=-1)


def _unpack_w_in_grad(dw):
    d = dw.shape[0]
    c2 = Q_LORA + KV_LORA
    uv = 2 * SGU_OUT
    g = jnp.concatenate([dw[:, :c2], _rope_unslab(dw[:, c2 + uv:]), dw[:, c2:c2 + uv]], axis=-1)
    return jnp.transpose(g.reshape(d, N_DEV, g.shape[1] // N_DEV), (1, 0, 2))


def _rope_tables(positions):
    inv_freq = ROPE_BASE ** (-jnp.arange(0, QK_ROPE, 2, dtype=F32) / QK_ROPE)
    ang = positions.astype(F32)[:, None] * inv_freq
    cos, sin = jnp.cos(ang), jnp.sin(ang)
    z = jnp.zeros_like(cos)
    return jnp.concatenate([cos, z, cos, z], axis=-1), jnp.concatenate([-sin, z, sin, z], axis=-1)


def _mlp_up(x, gain, w1, tag):
    hn = rms_fwd(x, gain, name=f"mlp{tag}_norm")

    def act_epi(acc):
        a = jnp.maximum(acc, 0.0)
        return a, a * a

    T = x.shape[0]
    F = w1.shape[0] * w1.shape[2]
    a, act = mm(hn, w1, name=f"mlp{tag}_up", outs=[((T, F), BF, None), ((T, F), BF, None)], epi=act_epi)
    return hn, a, act


def _mlp_down(x, act, w2, tag):
    bm = _tile(x.shape[0], MM_TILE)
    bn = _tile(x.shape[1], MM_TILE)
    return mm(act, w2, name=f"mlp{tag}_down", out=(x.shape, F32), bm=bm, bn=bn,
              epi=lambda acc, r: (acc + r[...],), epi_ins=[(x, (bm, bn), lambda i, j, k: (i, j))])


def _mlp_bwd_weights(w1, w2, saved, dxb, tag):
    hn, a, act = saved
    T, D = dxb.shape
    F = a.shape[1]
    bm = _tile(T, MM_TILE)
    bn = _tile(F, min(MM_TILE, w1.shape[2]))
    dhid = mm(dxb, w2, tb=True, name=f"mlp{tag}_dhid", out=((T, F), BF), bm=bm, bn=bn,
              epi=lambda acc, a_ref: (2.0 * a_ref[...].astype(F32) * acc,),
              epi_ins=[(a, (bm, bn), lambda i, j, k: (i, j))])
    dw2 = mm(act, dxb, ta=True, name=f"mlp{tag}_dw2", out=((F, D), BF))
    dw1 = mm(hn, dhid, ta=True, name=f"mlp{tag}_dw1", out=(w1.shape, BF))
    return dhid, dw1, dw2.reshape(N_DEV, F // N_DEV, D)


def _mlp_bwd_input(x_in, gain, w1, dhid, dx, tag, deps):
    dhn = mm(dhid, w1, tb=True, name=f"mlp{tag}_dhn", out=(x_in.shape, F32), deps=deps)
    return rms_bwd(x_in, gain, dhn, dres=dx, name=f"mlp{tag}_norm_bwd")


def _reduce_start(grads, tag):
    a_bufs = exchange_sibling(grads, name=f"reduce_sibling_{tag}")
    pairs = [pair_sum(g, a, name=f"pair_sum_{tag}{t}") for t, (g, a) in enumerate(zip(grads, a_bufs))]
    return a_bufs, chips_start(pairs, name=f"reduce_chips_start_{tag}")


def kernel(x, positions, e_norm_mix, e_w_in, e_q_norm, e_w_uq, e_kv_norm, e_w_ukv, e_v_norm, e_sgu_w, e_sgu_b, e_mla_out_norm, e_sgu_out_norm, e_w_out, o_norm_mix, o_w_in, o_conv_w, o_w_out, mlp_norm, mlp_w1, mlp_w2, final_norm, loss_target, m_e_norm_mix, m_e_w_in, m_e_q_norm, m_e_w_uq, m_e_kv_norm, m_e_w_ukv, m_e_v_norm, m_e_sgu_w, m_e_sgu_b, m_e_mla_out_norm, m_e_sgu_out_norm, m_e_w_out, m_o_norm_mix, m_o_w_in, m_o_conv_w, m_o_w_out, m_mlp_norm, m_mlp_w1, m_mlp_w2, m_final_norm, v_e_norm_mix, v_e_w_in, v_e_q_norm, v_e_w_uq, v_e_kv_norm, v_e_w_ukv, v_e_v_norm, v_e_sgu_w, v_e_sgu_b, v_e_mla_out_norm, v_e_sgu_out_norm, v_e_w_out, v_o_norm_mix, v_o_w_in, v_o_conv_w, v_o_w_out, v_mlp_norm, v_mlp_w1, v_mlp_w2, v_final_norm):
    T, D = x.shape[1], x.shape[2]
    d_shard = o_norm_mix.shape[1]
    x0 = x[0]
    target = loss_target[0]
    me = 4 * lax.axis_index("x") + 2 * lax.axis_index("y") + lax.axis_index("c")

    bf = lambda s: s.astype(BF)
    gather_groups = [[bf(e_w_in[0]), bf(e_w_uq[0]), bf(e_w_ukv[0])], [bf(e_w_out[0]), bf(mlp_w1[0])],
                     [bf(mlp_w2[0]), bf(o_w_in[0])], [bf(o_w_out[0]), bf(mlp_w1[1])], [bf(mlp_w2[1])]]
    small_rows = jnp.concatenate([o_norm_mix, o_conv_w[0], jnp.zeros((4, d_shard), F32)], axis=0)
    small_flat = all_gather_vmem(small_rows, name="gather_small")
    started = gather_start(gather_groups, small_flat, name="gather_start")

    def gathered(gi, after):
        srcs, lands = gather_wait(started[gi], after, name=f"gather_wait{gi}")
        return gather_finish(srcs, lands, name=f"gather_finish{gi}")

    small_g = small_flat.reshape(N_DEV, 8, d_shard)
    o_norm_full = small_g[:, 0, :].reshape(1, D)
    conv_w_full = jnp.transpose(small_g[:, 1:4, :], (1, 0, 2)).reshape(3, D)
    w_tril = jnp.tril(e_sgu_w[0])
    w_tril_b = w_tril.astype(BF)
    w_tril_tb = jnp.swapaxes(w_tril, 1, 2).astype(BF)
    b_full = jnp.repeat(e_sgu_b[0].T, CH, axis=1)
    v_gain = e_v_norm[0].reshape(1, SGU_OUT)
    cos_t, sin_t = _rope_tables(positions[0])
    mlp_gain = [mlp_norm[0:1], mlp_norm[1:2]]
    final_gain = final_norm.reshape(1, D)

    h0 = rms_fwd(x0, e_norm_mix, name="e_norm")
    g_w_in, g_w_uq, w_ukv = gathered(0, h0)
    w_in_e = _pack_w_in(g_w_in)
    w_uq = jnp.concatenate([g_w_uq[..., :QK_NOPE], _rope_slab(g_w_uq[..., QK_NOPE:])], axis=-1)
    proj = mm(h0, w_in_e, name="e_in", out=((T, w_in_e.shape[1]), F32), bn=_tile(w_in_e.shape[1], 640))
    qn, kvn, krope = mla_prep(proj, e_q_norm, e_kv_norm, cos_t, sin_t, name="mla_prep")
    bm = _tile(T, MM_TILE)

    def q_epi(acc, cos_ref, sin_ref):
        return (jnp.concatenate([acc[:, :QK_NOPE], _rope_fwd(acc[:, QK_NOPE:], cos_ref[...], sin_ref[...])], axis=-1),)

    q = mm(qn, w_uq, name="mla_q", out=((T, HEADS * HEAD_PAD), BF), bm=bm, bn=HEAD_PAD, epi=q_epi,
           epi_ins=[(cos_t, (bm, LANES), lambda i, j, k: (i, 0)), (sin_t, (bm, LANES), lambda i, j, k: (i, 0))])

    def kv_epi(acc, kr_ref):
        return jnp.concatenate([acc[:, :QK_NOPE].astype(BF), kr_ref[...]], axis=-1), acc[:, QK_NOPE:]

    k, v = mm(kvn, w_ukv, name="mla_kv", bm=bm, bn=HEAD_PAD, epi=kv_epi,
              outs=[((T, HEADS * HEAD_PAD), BF, HEAD_PAD), ((T, MLA_OUT), BF, V_HEAD)],
              epi_ins=[(krope, (bm, LANES), lambda i, j, k: (i, 0))])
    attn = attn_fwd(q, k, v, name="attn_fwd")
    mixed = mix_fwd(attn, proj, e_mla_out_norm, e_sgu_out_norm, v_gain, w_tril_b, b_full, name="mix_fwd")
    bn = _tile(D, MM_TILE)
    g_w_out_e, w1_0 = gathered(1, mixed)
    w_out_e = g_w_out_e.reshape(-1, D)
    x1 = mm(mixed, w_out_e, name="e_out", out=((T, D), F32), bm=bm, bn=bn,
            epi=lambda acc, r: (acc + r[...],), epi_ins=[(x0, (bm, bn), lambda i, j, k: (i, j))])
    hn0, a0, act0 = _mlp_up(x1, mlp_gain[0], w1_0, 0)
    g_w2_0, g_w_in_o = gathered(2, act0)
    w2_0 = g_w2_0.reshape(-1, D)
    x2 = _mlp_down(x1, act0, w2_0, 0)
    ho = rms_fwd(x2, o_norm_full, name="o_norm")
    proj_o = mm(ho, g_w_in_o, name="o_in", out=((T, 3 * D), F32))
    gated = conv_fwd(proj_o, conv_w_full, name="conv_fwd")
    g_w_out_o, w1_1 = gathered(3, gated)
    w_out_o = g_w_out_o.reshape(-1, D)
    x3 = mm(gated, w_out_o, name="o_out", out=((T, D), F32), bm=bm, bn=bn,
            epi=lambda acc, r: (acc + r[...],), epi_ins=[(x2, (bm, bn), lambda i, j, k: (i, j))])
    hn1, a1, act1 = _mlp_up(x3, mlp_gain[1], w1_1, 1)
    (g_w2_1,) = gathered(4, act1)
    w2_1 = g_w2_1.reshape(-1, D)
    x4 = _mlp_down(x3, act1, w2_1, 1)
    w1, w2 = [w1_0, w1_1], [w2_0, w2_1]
    mlp0_saved, mlp1_saved = (hn0, a0, act0), (hn1, a1, act1)

    dx4, dx4b, d_final, loss_part = loss_bwd(x4, final_gain, target, name="loss_bwd")
    loss = lax.psum(loss_part[0, 0], AXES)

    dhid1, dw1_1, dw2_1 = _mlp_bwd_weights(w1[1], w2[1], mlp1_saved, dx4b, 1)
    grads_r0 = [dw1_1, dw2_1]
    a_r0, st_r0 = _reduce_start(grads_r0, "r0")
    dx3, dx3b, d_mlp1 = _mlp_bwd_input(x3, mlp_gain[1], w1[1], dhid1, dx4, 1, deps=[st_r0[-1]])

    dgated = mm(dx3b, w_out_o, tb=True, name="o_out_dx", out=((T, D), F32))
    dw_out_o = mm(gated, dx3b, ta=True, name="o_out_dw", out=((D, D), BF))
    dproj_o, dconv_full = conv_bwd(dgated, proj_o, conv_w_full, name="conv_bwd")
    dw_in_o = mm(ho, dproj_o, ta=True, name="o_in_dw", out=(g_w_in_o.shape, BF))
    grads_r1 = [dw_out_o.reshape(g_w_out_o.shape), dw_in_o]
    a_r1, st_r1 = _reduce_start(grads_r1, "r1")
    dho = mm(dproj_o, g_w_in_o, tb=True, name="o_in_dx", out=((T, D), F32), deps=[st_r1[-1]])
    dx2, dx2b, d_onorm_full = rms_bwd(x2, o_norm_full, dho, dres=dx3, name="o_norm_bwd")

    dhid0, dw1_0, dw2_0 = _mlp_bwd_weights(w1[0], w2[0], mlp0_saved, dx2b, 0)
    grads_r2 = [dw1_0, dw2_0]
    a_r2, st_r2 = _reduce_start(grads_r2, "r2")
    dx1, dx1b, d_mlp0 = _mlp_bwd_input(x1, mlp_gain[0], w1[0], dhid0, dx2, 0, deps=[st_r2[-1]])
    b_r0 = chips_wait(st_r0, dx1b, name="reduce_chips_wait_r0")

    dmixed = mm(dx1b, w_out_e, tb=True, name="e_out_dx", out=((T, MLA_OUT + SGU_OUT), F32))
    dw_out_e = mm(mixed, dx1b, ta=True, name="e_out_dw", out=(w_out_e.shape, BF))
    (dattn, duv, d_mla_out, d_sgu_out, d_vgain, d_sgu_w, d_b_full) = mix_bwd(
        dmixed, attn, proj, e_mla_out_norm, e_sgu_out_norm, v_gain, w_tril_b, w_tril_tb, b_full, name="mix_bwd")
    b_r1 = chips_wait(st_r1, dattn, name="reduce_chips_wait_r1")
    dq, dk, dv = attn_bwd(q, k, v, dattn, name="attn_bwd")
    dq_lin, dkv_lin, dkr = mla_bwd_prep(dq, dk, dv, cos_t, sin_t, name="mla_bwd_prep")
    dw_uq_pad = mm(qn, dq_lin, ta=True, name="mla_q_dw", out=(w_uq.shape, F32))
    dqn = mm(dq_lin, w_uq, tb=True, name="mla_q_dx", out=((T, Q_LORA), F32))
    dw_ukv = mm(kvn, dkv_lin, ta=True, name="mla_kv_dw", out=(w_ukv.shape, BF))
    dkvn = mm(dkv_lin, w_ukv, tb=True, name="mla_kv_dx", out=((T, KV_LORA), F32))
    dcq, d_qnorm = rms_bwd(proj, e_q_norm, dqn, col_block=0, want_f32=False, name="q_norm_bwd")
    dckv, d_kvnorm = rms_bwd(proj, e_kv_norm, dkvn, col_block=1, want_f32=False, name="kv_norm_bwd")
    dproj = jnp.concatenate([dcq, dckv, duv, dkr], axis=-1)
    dw_in_e_pad = mm(h0, dproj, ta=True, name="e_in_dw", out=(w_in_e.shape, F32), bn=_tile(w_in_e.shape[1], 640))
    dw_in_e = _unpack_w_in_grad(dw_in_e_pad).astype(BF)
    dw_uq = jnp.concatenate([dw_uq_pad[..., :QK_NOPE], _rope_unslab(dw_uq_pad[..., QK_NOPE:])], axis=-1).astype(BF)
    grads_r3 = [dw_out_e.reshape(g_w_out_e.shape), dw_uq, dw_ukv, dw_in_e]
    a_r3, st_r3 = _reduce_start(grads_r3, "r3")
    tok_r3 = st_r3[-1]
    dh0 = mm(dproj, w_in_e, tb=True, name="e_in_dx", out=((T, D), F32), deps=[tok_r3])
    grad_x, d_enorm = rms_bwd(x0, e_norm_mix, dh0, dres=dx1, want_bf=False, name="e_norm_bwd")
    b_r2 = chips_wait(st_r2, grad_x, name="reduce_chips_wait_r2")

    def finish(grads, a_bufs, b_bufs, t, w, m, v, layer=0, prev=None, tag="", deps=()):
        return reduce_adam(grads[t], a_bufs[t], b_bufs[t], w, m, v, layer, prev, name=f"adam_{tag}", deps=deps)

    r_w1 = finish(grads_r0, a_r0, b_r0, 0, mlp_w1, m_mlp_w1, v_mlp_w1, 1, None, tag="w1_l1")
    r_w2 = finish(grads_r0, a_r0, b_r0, 1, mlp_w2, m_mlp_w2, v_mlp_w2, 1, None, tag="w2_l1")
    r_w_out_o = finish(grads_r1, a_r1, b_r1, 0, o_w_out, m_o_w_out, v_o_w_out, tag="o_w_out")
    r_w_in_o = finish(grads_r1, a_r1, b_r1, 1, o_w_in, m_o_w_in, v_o_w_in, tag="o_w_in")
    r_w1 = finish(grads_r2, a_r2, b_r2, 0, mlp_w1, m_mlp_w1, v_mlp_w1, 0, r_w1, tag="w1_l0", deps=[tok_r3])
    r_w2 = finish(grads_r2, a_r2, b_r2, 1, mlp_w2, m_mlp_w2, v_mlp_w2, 0, r_w2, tag="w2_l0", deps=[tok_r3])
    b_r3 = chips_wait(st_r3, r_w2[1], name="reduce_chips_wait_r3")
    r_w_out_e = finish(grads_r3, a_r3, b_r3, 0, e_w_out, m_e_w_out, v_e_w_out, tag="e_w_out")
    r_w_uq = finish(grads_r3, a_r3, b_r3, 1, e_w_uq, m_e_w_uq, v_e_w_uq, tag="e_w_uq")
    r_w_ukv = finish(grads_r3, a_r3, b_r3, 2, e_w_ukv, m_e_w_ukv, v_e_w_ukv, tag="e_w_ukv")
    r_w_in = finish(grads_r3, a_r3, b_r3, 3, e_w_in, m_e_w_in, v_e_w_in, tag="e_w_in")

    d_sgu_b = jnp.transpose(d_b_full[:, ::CH])
    d_sgu_w_tril = jnp.tril(d_sgu_w)
    rep = [("e_norm_mix", e_norm_mix, m_e_norm_mix, v_e_norm_mix, d_enorm),
           ("e_q_norm", e_q_norm, m_e_q_norm, v_e_q_norm, d_qnorm),
           ("e_kv_norm", e_kv_norm, m_e_kv_norm, v_e_kv_norm, d_kvnorm),
           ("e_v_norm", e_v_norm, m_e_v_norm, v_e_v_norm, d_vgain),
           ("e_sgu_w", e_sgu_w, m_e_sgu_w, v_e_sgu_w, d_sgu_w_tril),
           ("e_sgu_b", e_sgu_b, m_e_sgu_b, v_e_sgu_b, d_sgu_b),
           ("e_mla_out_norm", e_mla_out_norm, m_e_mla_out_norm, v_e_mla_out_norm, d_mla_out),
           ("e_sgu_out_norm", e_sgu_out_norm, m_e_sgu_out_norm, v_e_sgu_out_norm, d_sgu_out),
           ("mlp_norm", mlp_norm, m_mlp_norm, v_mlp_norm, jnp.concatenate([d_mlp0, d_mlp1], axis=0)),
           ("final_norm", final_norm, m_final_norm, v_final_norm, d_final)]
    sizes = [int(np.prod(r[1].shape)) for r in rep]
    n_rep = sum(sizes)
    n_all = n_rep + 4 * D
    width = -(-n_all // (8 * LANES)) * LANES
    pad = 8 * width - n_all
    flat = jnp.concatenate([r[4].reshape(-1) for r in rep]
                           + [d_onorm_full.reshape(-1), dconv_full.reshape(-1), jnp.zeros((pad,), F32)])
    summed = sum_rows8(all_gather_vmem(flat.reshape(8, width), name="gather_small_grads", after=b_r3[0]), 8,
                       name="sum_small_grads").reshape(-1)

    def pack_rep(i):
        return jnp.concatenate([r[i].reshape(-1) for r in rep]).reshape(n_rep // LANES, LANES)

    g_rep = summed[:n_rep].reshape(n_rep // LANES, LANES)
    d_rep, nm_rep, nv_rep = adam_flat(g_rep, pack_rep(1), pack_rep(2), pack_rep(3), name="adam_replicated")

    def unpack_rep(flat2d):
        out, off = {}, 0
        f = flat2d.reshape(-1)
        for r, n in zip(rep, sizes):
            out[r[0]] = f[off:off + n].reshape(r[1].shape)
            off += n
        return out

    small = {"grad": unpack_rep(g_rep), "delta": unpack_rep(d_rep), "new_m": unpack_rep(nm_rep),
             "new_v": unpack_rep(nv_rep)}
    g_onorm = lax.dynamic_slice(summed[n_rep:n_rep + D].reshape(1, D), (0, me * d_shard), (1, d_shard))
    g_conv = lax.dynamic_slice(summed[n_rep + D:n_rep + 4 * D].reshape(3, D), (0, me * d_shard), (3, d_shard))

    def pack_sharded(norm_part, conv_part):
        return jnp.concatenate([norm_part, conv_part, jnp.zeros((4, d_shard), F32)], axis=0)

    g_sh = pack_sharded(g_onorm, g_conv)
    d_sh, nm_sh, nv_sh = adam_flat(g_sh, pack_sharded(o_norm_mix, o_conv_w[0]), pack_sharded(m_o_norm_mix, m_o_conv_w[0]),
                                   pack_sharded(v_o_norm_mix, v_o_conv_w[0]), name="adam_sharded_small")
    for kind, arr in (("grad", g_sh), ("delta", d_sh), ("new_m", nm_sh), ("new_v", nv_sh)):
        small[kind]["o_norm_mix"] = arr[0:1]
        small[kind]["o_conv_w"] = arr[1:4][None]

    big = {"e_w_in": r_w_in, "e_w_uq": r_w_uq, "e_w_ukv": r_w_ukv, "e_w_out": r_w_out_e, "o_w_in": r_w_in_o,
           "o_w_out": r_w_out_o, "mlp_w1": r_w1, "mlp_w2": r_w2}
    order = ["e_norm_mix", "e_w_in", "e_q_norm", "e_w_uq", "e_kv_norm", "e_w_ukv", "e_v_norm", "e_sgu_w", "e_sgu_b",
             "e_mla_out_norm", "e_sgu_out_norm", "e_w_out", "o_norm_mix", "o_w_in", "o_conv_w", "o_w_out", "mlp_norm",
             "mlp_w1", "mlp_w2", "final_norm"]
    result = [loss, grad_x[None]]
    for ki, kind in enumerate(("grad", "delta", "new_m", "new_v")):
        for nm in order:
            result.append(big[nm][ki] if nm in big else small[kind][nm])
    return tuple(result)
```

```python
import functools

import numpy as np
import jax
import jax.numpy as jnp
from jax import lax
from jax.experimental import pallas as pl
from jax.experimental.pallas import tpu as pltpu

BF = jnp.bfloat16
F32 = jnp.float32
MESH = pl.DeviceIdType.MESH
AXES = ("x", "y", "c")
N_DEV = 8

EPS = 1e-6
HEADS = 8
Q_LORA = 512
KV_LORA = 512
QK_NOPE = 128
QK_ROPE = 64
HALF_ROPE = QK_ROPE // 2
V_HEAD = 128
HEAD_PAD = 256
ROPE_BASE = 10000.0
GROUPS = 8
CH = 128
CHUNK = 128
SGU_OUT = GROUPS * CH
MLA_OUT = HEADS * V_HEAD
ATTN_SCALE = float((QK_NOPE + QK_ROPE) ** -0.5)

ADAM_LR = 0.001
ADAM_B1 = 0.9
ADAM_B2 = 0.999
ADAM_EPS = 1e-08
ADAM_WD = 0.01
ADAM_STEP = 10
ADAM_C1 = 1.0 - ADAM_B1 ** ADAM_STEP
ADAM_C2 = 1.0 - ADAM_B2 ** ADAM_STEP

V7X_VMEM_BYTES = 64 * 2 ** 20
VMEM_LIMIT_CAP = V7X_VMEM_BYTES - 6 * 2 ** 20
LANES = 128
ROW_TILE = 256
MM_TILE = 1024
MM_K_TILE = 2048


def _padded_bytes(block, dtype):
    dims = [d for d in block if d is not None]
    if len(dims) >= 1:
        dims[-1] = -(-dims[-1] // LANES) * LANES
    if len(dims) >= 2:
        dims[-2] = -(-dims[-2] // 16) * 16
    return int(np.prod(dims)) * jnp.dtype(dtype).itemsize


def _pcall(body, *, name, grid, ins, outs, scratch=(), semantics=None, aliases=None, prefetch=None, deps=()):
    any_spec = pl.BlockSpec(memory_space=pl.ANY)
    if deps:
        n_lead = len(ins) + (1 if prefetch is not None else 0)
        n_deps = len(deps)
        inner = body

        def body(*refs):
            inner(*refs[:n_lead], *refs[n_lead + n_deps:])

        ins = list(ins) + [(d, None, None) for d in deps]
    in_specs = [any_spec if b is None else pl.BlockSpec(b, m) for _, b, m in ins]
    out_specs = [any_spec if b is None else pl.BlockSpec(b, m) for _, _, b, m in outs]
    out_shape = [pltpu.HBM(s, d) for s, d, _, _ in outs]
    est = 0
    for a, b, _ in ins:
        if b is not None:
            est += 2 * _padded_bytes(b, a.dtype)
    for _, d, b, _ in outs:
        if b is not None:
            est += 2 * _padded_bytes(b, d)
    for s in scratch:
        if hasattr(s, "shape") and hasattr(s, "dtype"):
            est += _padded_bytes(s.shape, s.dtype)
    limit = int(min(VMEM_LIMIT_CAP, est + 16 * 2 ** 20))
    params = pltpu.CompilerParams(
        dimension_semantics=semantics or ("arbitrary",) * len(grid), vmem_limit_bytes=limit)
    args = [pltpu.with_memory_space_constraint(a, pltpu.HBM) for a, _, _ in ins]
    if prefetch is not None:
        grid_spec = pltpu.PrefetchScalarGridSpec(
            num_scalar_prefetch=1, grid=grid, in_specs=in_specs, out_specs=out_specs, scratch_shapes=list(scratch))
        call = pl.pallas_call(body, out_shape=out_shape, grid_spec=grid_spec, name=name, compiler_params=params,
                              input_output_aliases=aliases or {})
        return call(prefetch, *args)
    call = pl.pallas_call(body, out_shape=out_shape, grid=grid, in_specs=in_specs, out_specs=out_specs,
                          scratch_shapes=list(scratch), name=name, compiler_params=params,
                          input_output_aliases=aliases or {})
    return call(*args)


def _tile(dim, pref, quantum=LANES):
    if dim <= pref:
        return dim
    t = (pref // quantum) * quantum
    while t >= quantum:
        if dim % t == 0:
            return t
        t -= quantum
    return dim


def _vshape(arr_shape):
    if len(arr_shape) == 2:
        return tuple(arr_shape)
    s, r, c = arr_shape
    return (r, s * c)


def _vblock(arr_shape, br, bc, rc):
    if len(arr_shape) == 2:
        return (br, bc), (lambda *g: rc(*g))
    _, _, c = arr_shape
    assert c % bc == 0, (arr_shape, bc)
    per = c // bc

    def imap(*g):
        ri, ci = rc(*g)
        return (ci // per, ri, ci % per)

    return (None, br, bc), imap


def _shard_width(*shapes):
    w = None
    for s in shapes:
        if len(s) == 3:
            w = s[2] if w is None else int(np.gcd(w, s[2]))
    return w


def mm(a, b, *, name, ta=False, tb=False, out=None, outs=None, epi=None, epi_ins=(), bm=None, bn=None, bk=None,
       deps=()):
    av, bv = _vshape(a.shape), _vshape(b.shape)
    M, K = (av[1], av[0]) if ta else av
    K2, N = (bv[1], bv[0]) if tb else bv
    assert K == K2, (a.shape, b.shape, ta, tb)
    if outs is None:
        outs = [(out[0], out[1], None)]
    a_sw = _shard_width(a.shape)
    b_sw = _shard_width(b.shape)
    o_sw = _shard_width(*[o[0] for o in outs])
    m_lim = a_sw if (ta and a_sw) else None
    k_lim = [w for w in ((a_sw if not ta else None), (b_sw if tb else None)) if w]
    n_lim = [w for w in ((b_sw if not tb else None), o_sw) if w]
    if bm is None:
        bm = _tile(M, min([MM_TILE] + ([m_lim] if m_lim else [])))
    if bn is None:
        bn = _tile(N, min([MM_TILE] + n_lim))
    if bk is None:
        bk = K if (K <= 4096 and not k_lim) else _tile(K, min([MM_K_TILE] + k_lim))
    assert M % bm == 0 and N % bn == 0 and K % bk == 0, (name, M, N, K, bm, bn, bk)
    nk = K // bk
    grid = (M // bm, N // bn, nk)
    if ta:
        a_blk, a_map = _vblock(a.shape, bk, bm, lambda i, j, k: (k, i))
    else:
        a_blk, a_map = _vblock(a.shape, bm, bk, lambda i, j, k: (i, k))
    if tb:
        b_blk, b_map = _vblock(b.shape, bn, bk, lambda i, j, k: (j, k))
    else:
        b_blk, b_map = _vblock(b.shape, bk, bn, lambda i, j, k: (k, j))
    dn = (((0 if ta else 1,), (1 if tb else 0,)), ((), ()))
    ins = [(a, a_blk, a_map), (b, b_blk, b_map)] + list(epi_ins)
    out_list = []
    for shape, dtype, cols in outs:
        cols = cols or bn
        blk, imap = _vblock(shape, bm, cols, lambda i, j, k: (i, j))
        out_list.append((shape, dtype, blk, imap))
    n_e, n_o = len(epi_ins), len(out_list)

    def body(*refs):
        a_ref, b_ref = refs[0], refs[1]
        e_refs = refs[2:2 + n_e]
        o_refs = refs[2 + n_e:2 + n_e + n_o]

        def finish(acc):
            res = epi(acc, *e_refs) if epi is not None else (acc,)
            for o_ref, r in zip(o_refs, res):
                o_ref[...] = r.astype(o_ref.dtype)

        x = a_ref[...].astype(BF)
        y = b_ref[...].astype(BF)
        p = lax.dot_general(x, y, dn, preferred_element_type=F32)
        if nk == 1:
            finish(p)
        else:
            acc_ref = refs[-1]
            k = pl.program_id(2)

            @pl.when(k == 0)
            def _():
                acc_ref[...] = p

            @pl.when(k > 0)
            def _():
                acc_ref[...] += p

            @pl.when(k == nk - 1)
            def _():
                finish(acc_ref[...])

    scratch = [pltpu.VMEM((bm, bn), F32)] if nk > 1 else []
    res = _pcall(body, name=name, grid=grid, ins=ins, outs=out_list, scratch=scratch,
                 semantics=("parallel", "parallel", "arbitrary"), deps=deps)
    return res[0] if len(res) == 1 else res


_GELU_K = float(np.sqrt(2.0 / np.pi))
_GELU_C = 0.044715


def _gelu(x):
    t = jnp.tanh(_GELU_K * (x + _GELU_C * (x * x * x)))
    return 0.5 * x * (1.0 + t)


def _gelu_grad(x):
    t = jnp.tanh(_GELU_K * (x + _GELU_C * (x * x * x)))
    return 0.5 * (1.0 + t) + 0.5 * x * (1.0 - t * t) * (_GELU_K * (1.0 + 3.0 * _GELU_C * (x * x)))


def _rstd(x):
    return lax.rsqrt(jnp.mean(x * x, axis=-1, keepdims=True) + EPS)


def _rms_bwd(x, gain, dy):
    r = _rstd(x)
    xh = x * r
    gdy = dy * gain
    dx = r * (gdy - xh * jnp.mean(gdy * xh, axis=-1, keepdims=True))
    return dx, dy * xh


def _rope_fwd(x, cos_t, sin_t):
    return x * cos_t + pltpu.roll(x, 2 * HALF_ROPE, 1) * sin_t


def _rope_bwd(dy, cos_t, sin_t):
    return dy * cos_t + pltpu.roll(dy * sin_t, 2 * HALF_ROPE, 1)


def _acc_rows(ref, val, first):
    s = jnp.sum(val, axis=0, keepdims=True)

    @pl.when(first)
    def _():
        ref[...] = s

    @pl.when(jnp.logical_not(first))
    def _():
        ref[...] += s


def rms_fwd(x, gain, *, name, col_block=0, width=None):
    T = x.shape[0]
    width = width or x.shape[1]
    tm = _tile(T, ROW_TILE, 8)

    def body(x_ref, g_ref, o_ref):
        v = x_ref[...]
        o_ref[...] = (v * _rstd(v) * g_ref[...]).astype(BF)

    return _pcall(body, name=name, grid=(T // tm,),
                  ins=[(x, (tm, width), lambda i: (i, col_block)), (gain, (1, width), lambda i: (0, 0))],
                  outs=[((T, width), BF, (tm, width), lambda i: (i, 0))], semantics=("parallel",))[0]


def rms_bwd(x, gain, dy, *, name, col_block=0, dres=None, want_f32=True, want_bf=True, deps=()):
    T, width = dy.shape
    tm = _tile(T, ROW_TILE, 8)
    has_res = dres is not None

    def body(*refs):
        x_ref, g_ref, dy_ref = refs[:3]
        pos = 3
        res_ref = None
        if has_res:
            res_ref = refs[pos]
            pos += 1
        outs = refs[pos:]
        dx, dg_rows = _rms_bwd(x_ref[...], g_ref[...], dy_ref[...])
        if has_res:
            dx = dx + res_ref[...]
        o = 0
        if want_f32:
            outs[o][...] = dx
            o += 1
        if want_bf:
            outs[o][...] = dx.astype(BF)
            o += 1
        _acc_rows(outs[o], dg_rows, pl.program_id(0) == 0)

    ins = [(x, (tm, width), lambda i: (i, col_block)), (gain, (1, width), lambda i: (0, 0)),
           (dy, (tm, width), lambda i: (i, 0))]
    if has_res:
        ins.append((dres, (tm, width), lambda i: (i, 0)))
    outs = []
    if want_f32:
        outs.append(((T, width), F32, (tm, width), lambda i: (i, 0)))
    if want_bf:
        outs.append(((T, width), BF, (tm, width), lambda i: (i, 0)))
    outs.append(((1, width), F32, (1, width), lambda i: (0, 0)))
    return _pcall(body, name=name, grid=(T // tm,), ins=ins, outs=outs, deps=deps)


def mla_prep(proj, q_norm, kv_norm, cos_t, sin_t, *, name):
    T = proj.shape[0]
    tm = _tile(T, ROW_TILE, 8)
    kr_block = (proj.shape[1] - LANES) // LANES

    def body(cq_ref, ckv_ref, kr_ref, qg_ref, kg_ref, cos_ref, sin_ref, qn_ref, kvn_ref, krope_ref):
        cq = cq_ref[...]
        qn_ref[...] = (cq * _rstd(cq) * qg_ref[...]).astype(BF)
        ckv = ckv_ref[...]
        kvn_ref[...] = (ckv * _rstd(ckv) * kg_ref[...]).astype(BF)
        krope_ref[...] = _rope_fwd(kr_ref[...], cos_ref[...], sin_ref[...]).astype(BF)

    return _pcall(
        body, name=name, grid=(T // tm,),
        ins=[(proj, (tm, Q_LORA), lambda i: (i, 0)), (proj, (tm, KV_LORA), lambda i: (i, 1)),
             (proj, (tm, LANES), lambda i: (i, kr_block)),
             (q_norm, (1, Q_LORA), lambda i: (0, 0)), (kv_norm, (1, KV_LORA), lambda i: (0, 0)),
             (cos_t, (tm, LANES), lambda i: (i, 0)), (sin_t, (tm, LANES), lambda i: (i, 0))],
        outs=[((T, Q_LORA), BF, (tm, Q_LORA), lambda i: (i, 0)), ((T, KV_LORA), BF, (tm, KV_LORA), lambda i: (i, 0)),
              ((T, LANES), BF, (tm, LANES), lambda i: (i, 0))],
        semantics=("parallel",))


def _attn_probs(q_ref, k_ref, tq, T):
    s = lax.dot_general(q_ref[...], k_ref[...], (((1,), (1,)), ((), ())), preferred_element_type=F32) * ATTN_SCALE
    row = pl.program_id(1) * tq + lax.broadcasted_iota(jnp.int32, (tq, T), 0)
    col = lax.broadcasted_iota(jnp.int32, (tq, T), 1)
    s = jnp.where(col <= row, s, -jnp.inf)
    e = jnp.exp(s - jnp.max(s, axis=-1, keepdims=True))
    return e / jnp.sum(e, axis=-1, keepdims=True)


def attn_fwd(q, k, v, *, name):
    T = q.shape[0]
    tq = _tile(T, ROW_TILE, 8)

    def body(q_ref, k_ref, v_ref, o_ref):
        p = _attn_probs(q_ref, k_ref, tq, T)
        o_ref[...] = jnp.dot(p.astype(BF), v_ref[...], preferred_element_type=F32)

    return _pcall(
        body, name=name, grid=(HEADS, T // tq),
        ins=[(q, (tq, HEAD_PAD), lambda h, i: (i, h)), (k, (T, HEAD_PAD), lambda h, i: (0, h)),
             (v, (T, V_HEAD), lambda h, i: (0, h))],
        outs=[((T, MLA_OUT), F32, (tq, V_HEAD), lambda h, i: (i, h))], semantics=("parallel", "parallel"))[0]


def attn_bwd(q, k, v, do, *, name):
    T = q.shape[0]
    tq = _tile(T, ROW_TILE, 8)

    def body(q_ref, k_ref, v_ref, do_ref, dq_ref, dk_ref, dv_ref):
        p = _attn_probs(q_ref, k_ref, tq, T)
        do_t = do_ref[...]
        dp = lax.dot_general(do_t, v_ref[...], (((1,), (1,)), ((), ())), preferred_element_type=F32)
        ds = (p * (dp - jnp.sum(p * dp, axis=-1, keepdims=True)) * ATTN_SCALE).astype(BF)
        dq_ref[...] = jnp.dot(ds, k_ref[...], preferred_element_type=F32)
        dk_t = lax.dot_general(ds, q_ref[...], (((0,), (0,)), ((), ())), preferred_element_type=F32)
        dv_t = lax.dot_general(p.astype(BF), do_t, (((0,), (0,)), ((), ())), preferred_element_type=F32)
        first = pl.program_id(1) == 0

        @pl.when(first)
        def _():
            dk_ref[...] = dk_t
            dv_ref[...] = dv_t

        @pl.when(jnp.logical_not(first))
        def _():
            dk_ref[...] += dk_t
            dv_ref[...] += dv_t

    return _pcall(
        body, name=name, grid=(HEADS, T // tq),
        ins=[(q, (tq, HEAD_PAD), lambda h, i: (i, h)), (k, (T, HEAD_PAD), lambda h, i: (0, h)),
             (v, (T, V_HEAD), lambda h, i: (0, h)), (do, (tq, V_HEAD), lambda h, i: (i, h))],
        outs=[((T, HEADS * HEAD_PAD), F32, (tq, HEAD_PAD), lambda h, i: (i, h)),
              ((T, HEADS * HEAD_PAD), F32, (T, HEAD_PAD), lambda h, i: (0, h)),
              ((T, MLA_OUT), F32, (T, V_HEAD), lambda h, i: (0, h))],
        semantics=("parallel", "arbitrary"))


def mla_bwd_prep(dq, dk, dv, cos_t, sin_t, *, name):
    T = dq.shape[0]
    tm = _tile(T, ROW_TILE, 8)

    def body(dq_ref, dk_ref, dv_ref, cos_ref, sin_ref, dql_ref, dkvl_ref, dkr_ref):
        cos_v, sin_v = cos_ref[...], sin_ref[...]
        kr = jnp.zeros((tm, LANES), F32)
        for h in range(HEADS):
            lo = h * HEAD_PAD
            dql_ref[:, lo:lo + QK_NOPE] = dq_ref[:, lo:lo + QK_NOPE].astype(BF)
            dql_ref[:, lo + QK_NOPE:lo + HEAD_PAD] = _rope_bwd(
                dq_ref[:, lo + QK_NOPE:lo + HEAD_PAD], cos_v, sin_v).astype(BF)
            dkvl_ref[:, lo:lo + QK_NOPE] = dk_ref[:, lo:lo + QK_NOPE].astype(BF)
            dkvl_ref[:, lo + QK_NOPE:lo + HEAD_PAD] = dv_ref[:, h * V_HEAD:(h + 1) * V_HEAD].astype(BF)
            kr = kr + dk_ref[:, lo + QK_NOPE:lo + HEAD_PAD]
        dkr_ref[...] = _rope_bwd(kr, cos_v, sin_v).astype(BF)

    W = HEADS * HEAD_PAD
    return _pcall(
        body, name=name, grid=(T // tm,),
        ins=[(dq, (tm, W), lambda i: (i, 0)), (dk, (tm, W), lambda i: (i, 0)), (dv, (tm, MLA_OUT), lambda i: (i, 0)),
             (cos_t, (tm, LANES), lambda i: (i, 0)), (sin_t, (tm, LANES), lambda i: (i, 0))],
        outs=[((T, W), BF, (tm, W), lambda i: (i, 0)), ((T, W), BF, (tm, W), lambda i: (i, 0)),
              ((T, LANES), BF, (tm, LANES), lambda i: (i, 0))],
        semantics=("parallel",))


def _group_norm_stats(vg):
    mu = jnp.mean(vg, axis=-1, keepdims=True)
    d = vg - mu
    r = lax.rsqrt(jnp.mean(d * d, axis=-1, keepdims=True) + EPS)
    return d * r, r


def mix_fwd(a, proj, g_mla, g_sgu, v_gain, w_tril, b_full, *, name):
    T = a.shape[0]
    tm = _tile(T, ROW_TILE, CHUNK)
    n_chunk = tm // CHUNK

    def body(a_ref, u_ref, v_ref, gm_ref, gs_ref, vg_ref, w_ref, b_ref, o_ref, s_scr):
        av = a_ref[...]
        o_ref[:, :MLA_OUT] = (av * _rstd(av) * gm_ref[...]).astype(BF)
        for g in range(GROUPS):
            sl = slice(g * CH, (g + 1) * CH)
            vhat, _ = _group_norm_stats(_gelu(v_ref[:, sl]))
            vn = (vhat * vg_ref[:, sl]).astype(BF)
            u = _gelu(u_ref[:, sl])
            for ci in range(n_chunk):
                rs = slice(ci * CHUNK, (ci + 1) * CHUNK)
                y = jnp.dot(w_ref[g], vn[rs], preferred_element_type=F32) + b_ref[:, sl]
                s_scr[rs, sl] = u[rs] * y
        s = s_scr[...]
        o_ref[:, MLA_OUT:] = (s * _rstd(s) * gs_ref[...]).astype(BF)

    return _pcall(
        body, name=name, grid=(T // tm,),
        ins=[(a, (tm, MLA_OUT), lambda i: (i, 0)), (proj, (tm, SGU_OUT), lambda i: (i, 1)),
             (proj, (tm, SGU_OUT), lambda i: (i, 2)), (g_mla, (1, MLA_OUT), lambda i: (0, 0)),
             (g_sgu, (1, SGU_OUT), lambda i: (0, 0)), (v_gain, (1, SGU_OUT), lambda i: (0, 0)),
             (w_tril, (GROUPS, CHUNK, CHUNK), lambda i: (0, 0, 0)), (b_full, (CHUNK, SGU_OUT), lambda i: (0, 0))],
        outs=[((T, MLA_OUT + SGU_OUT), BF, (tm, MLA_OUT + SGU_OUT), lambda i: (i, 0))],
        scratch=[pltpu.VMEM((tm, SGU_OUT), F32)], semantics=("parallel",))[0]


def mix_bwd(dmixed, a, proj, g_mla, g_sgu, v_gain, w_tril, w_tril_t, b_full, *, name):
    T = a.shape[0]
    tm = _tile(T, ROW_TILE, CHUNK)
    n_chunk = tm // CHUNK

    def body(dm_a_ref, dm_s_ref, a_ref, u_ref, v_ref, gm_ref, gs_ref, vg_ref, w_ref, wt_ref, b_ref,
             da_ref, duv_ref, dgm_ref, dgs_ref, dvg_ref, dw_ref, db_ref, s_scr, y_scr):
        first = pl.program_id(0) == 0
        da, dgm_rows = _rms_bwd(a_ref[...], gm_ref[...], dm_a_ref[...])
        da_ref[...] = da.astype(BF)
        _acc_rows(dgm_ref, dgm_rows, first)

        for g in range(GROUPS):
            sl = slice(g * CH, (g + 1) * CH)
            vhat, _ = _group_norm_stats(_gelu(v_ref[:, sl]))
            vn = (vhat * vg_ref[:, sl]).astype(BF)
            u = _gelu(u_ref[:, sl])
            for ci in range(n_chunk):
                rs = slice(ci * CHUNK, (ci + 1) * CHUNK)
                y = jnp.dot(w_ref[g], vn[rs], preferred_element_type=F32) + b_ref[:, sl]
                y_scr[rs, sl] = y
                s_scr[rs, sl] = u[rs] * y
        ds, dgs_rows = _rms_bwd(s_scr[...], gs_ref[...], dm_s_ref[...])
        _acc_rows(dgs_ref, dgs_rows, first)
        s_scr[...] = ds

        @pl.when(first)
        def _():
            dw_ref[...] = jnp.zeros_like(dw_ref)
            db_ref[...] = jnp.zeros_like(db_ref)

        for g in range(GROUPS):
            sl = slice(g * CH, (g + 1) * CH)
            upre = u_ref[:, sl]
            vpre = v_ref[:, sl]
            u = _gelu(upre)
            vhat, r = _group_norm_stats(_gelu(vpre))
            gain = vg_ref[:, sl]
            vn = (vhat * gain).astype(BF)
            dsg = s_scr[:, sl]
            duv_ref[:, sl] = (dsg * y_scr[:, sl] * _gelu_grad(upre)).astype(BF)
            dy = dsg * u
            dyb = dy.astype(BF)
            dvn_parts = []
            for ci in range(n_chunk):
                rs = slice(ci * CHUNK, (ci + 1) * CHUNK)
                dvn_parts.append(jnp.dot(wt_ref[g], dyb[rs], preferred_element_type=F32))
                dw_ref[g] += lax.dot_general(dyb[rs], vn[rs], (((1,), (1,)), ((), ())), preferred_element_type=F32)
                db_ref[:, sl] += jnp.broadcast_to(jnp.sum(dy[rs], axis=-1, keepdims=True), (CHUNK, CH))
            dvn = dvn_parts[0] if n_chunk == 1 else jnp.concatenate(dvn_parts, axis=0)
            _acc_rows(dvg_ref.at[:, sl], dvn * vhat, first)
            dvh = dvn * gain
            dvg = r * (dvh - jnp.mean(dvh, axis=-1, keepdims=True)
                       - vhat * jnp.mean(dvh * vhat, axis=-1, keepdims=True))
            duv_ref[:, SGU_OUT + g * CH:SGU_OUT + (g + 1) * CH] = (dvg * _gelu_grad(vpre)).astype(BF)

    return _pcall(
        body, name=name, grid=(T // tm,),
        ins=[(dmixed, (tm, MLA_OUT), lambda i: (i, 0)), (dmixed, (tm, SGU_OUT), lambda i: (i, 1)),
             (a, (tm, MLA_OUT), lambda i: (i, 0)), (proj, (tm, SGU_OUT), lambda i: (i, 1)),
             (proj, (tm, SGU_OUT), lambda i: (i, 2)), (g_mla, (1, MLA_OUT), lambda i: (0, 0)),
             (g_sgu, (1, SGU_OUT), lambda i: (0, 0)), (v_gain, (1, SGU_OUT), lambda i: (0, 0)),
             (w_tril, (GROUPS, CHUNK, CHUNK), lambda i: (0, 0, 0)), (w_tril_t, (GROUPS, CHUNK, CHUNK), lambda i: (0, 0, 0)),
             (b_full, (CHUNK, SGU_OUT), lambda i: (0, 0))],
        outs=[((T, MLA_OUT), BF, (tm, MLA_OUT), lambda i: (i, 0)),
              ((T, 2 * SGU_OUT), BF, (tm, 2 * SGU_OUT), lambda i: (i, 0)),
              ((1, MLA_OUT), F32, (1, MLA_OUT), lambda i: (0, 0)), ((1, SGU_OUT), F32, (1, SGU_OUT), lambda i: (0, 0)),
              ((1, SGU_OUT), F32, (1, SGU_OUT), lambda i: (0, 0)),
              ((GROUPS, CHUNK, CHUNK), F32, (GROUPS, CHUNK, CHUNK), lambda i: (0, 0, 0)),
              ((CHUNK, SGU_OUT), F32, (CHUNK, SGU_OUT), lambda i: (0, 0))],
        scratch=[pltpu.VMEM((tm, SGU_OUT), F32), pltpu.VMEM((tm, SGU_OUT), F32)])


def _shift_down(z, n, row):
    return jnp.where(row >= n, pltpu.roll(z, n, 0), 0.0)


def _shift_up(z, n, row, T):
    return jnp.where(row < T - n, pltpu.roll(z, T - n, 0), 0.0)


def conv_fwd(proj, conv_w, *, name):
    T, D3 = proj.shape
    D = D3 // 3
    tn = _tile(D, 256)
    nj = D // tn

    def body(b_ref, c_ref, x_ref, w_ref, o_ref):
        row = lax.broadcasted_iota(jnp.int32, (T, tn), 0)
        z = c_ref[...] * x_ref[...]
        zc = w_ref[2:3, :] * z + w_ref[1:2, :] * _shift_down(z, 1, row) + w_ref[0:1, :] * _shift_down(z, 2, row)
        o_ref[...] = (b_ref[...] * zc).astype(BF)

    return _pcall(
        body, name=name, grid=(nj,),
        ins=[(proj, (T, tn), lambda j: (0, j)), (proj, (T, tn), lambda j: (0, nj + j)),
             (proj, (T, tn), lambda j: (0, 2 * nj + j)), (conv_w, (3, tn), lambda j: (0, j))],
        outs=[((T, D), BF, (T, tn), lambda j: (0, j))], semantics=("parallel",))[0]


def conv_bwd(dg, proj, conv_w, *, name):
    T, D3 = proj.shape
    D = D3 // 3
    tn = _tile(D, 256)
    nj = D // tn

    def body(dg_ref, b_ref, c_ref, x_ref, w_ref, dp_ref, dw_ref, dc_scr, dx_scr):
        part = pl.program_id(1)

        @pl.when(part == 0)
        def _():
            row = lax.broadcasted_iota(jnp.int32, (T, tn), 0)
            c, x = c_ref[...], x_ref[...]
            z = c * x
            z1 = _shift_down(z, 1, row)
            z2 = _shift_down(z, 2, row)
            dgv = dg_ref[...]
            zc = w_ref[2:3, :] * z + w_ref[1:2, :] * z1 + w_ref[0:1, :] * z2
            dp_ref[...] = (dgv * zc).astype(BF)
            dzc = dgv * b_ref[...]
            dw_ref[0:1, :] = jnp.sum(dzc * z2, axis=0, keepdims=True)
            dw_ref[1:2, :] = jnp.sum(dzc * z1, axis=0, keepdims=True)
            dw_ref[2:3, :] = jnp.sum(dzc * z, axis=0, keepdims=True)
            dz = (w_ref[2:3, :] * dzc + w_ref[1:2, :] * _shift_up(dzc, 1, row, T)
                  + w_ref[0:1, :] * _shift_up(dzc, 2, row, T))
            dc_scr[...] = (dz * x).astype(BF)
            dx_scr[...] = (dz * c).astype(BF)

        @pl.when(part == 1)
        def _():
            dp_ref[...] = dc_scr[...]

        @pl.when(part == 2)
        def _():
            dp_ref[...] = dx_scr[...]

    return _pcall(
        body, name=name, grid=(nj, 3),
        ins=[(dg, (T, tn), lambda j, p: (0, j)), (proj, (T, tn), lambda j, p: (0, j)),
             (proj, (T, tn), lambda j, p: (0, nj + j)), (proj, (T, tn), lambda j, p: (0, 2 * nj + j)),
             (conv_w, (3, tn), lambda j, p: (0, j))],
        outs=[((T, D3), BF, (T, tn), lambda j, p: (0, p * nj + j)), ((3, D), F32, (3, tn), lambda j, p: (0, j))],
        scratch=[pltpu.VMEM((T, tn), BF), pltpu.VMEM((T, tn), BF)], semantics=("parallel", "arbitrary"))


def loss_bwd(x, gain, target, *, name):
    T, D = x.shape
    tm = _tile(T, ROW_TILE, 8)

    def body(x_ref, g_ref, t_ref, dx_ref, dxb_ref, dg_ref, loss_ref):
        first = pl.program_id(0) == 0
        xv = x_ref[...]
        r = _rstd(xv)
        xh = xv * r
        gain_v = g_ref[...]
        err = xh * gain_v - t_ref[...]
        part = 0.5 * jnp.sum(jnp.mean(err * err, axis=-1, keepdims=True), axis=0, keepdims=True)
        _acc_rows(loss_ref, jnp.broadcast_to(part, (1, LANES)), first)
        dy = err * (1.0 / D)
        gdy = dy * gain_v
        dx = r * (gdy - xh * jnp.mean(gdy * xh, axis=-1, keepdims=True))
        dx_ref[...] = dx
        dxb_ref[...] = dx.astype(BF)
        _acc_rows(dg_ref, dy * xh, first)

    return _pcall(
        body, name=name, grid=(T // tm,),
        ins=[(x, (tm, D), lambda i: (i, 0)), (gain, (1, D), lambda i: (0, 0)), (target, (tm, D), lambda i: (i, 0))],
        outs=[((T, D), F32, (tm, D), lambda i: (i, 0)), ((T, D), BF, (tm, D), lambda i: (i, 0)),
              ((1, D), F32, (1, D), lambda i: (0, 0)), ((1, LANES), F32, (1, LANES), lambda i: (0, 0))])


def _adamw(g, w, m, v):
    m = ADAM_B1 * m + (1.0 - ADAM_B1) * g
    v = ADAM_B2 * v + (1.0 - ADAM_B2) * (g * g)
    m_hat = m / ADAM_C1
    v_hat = v / ADAM_C2
    delta = -ADAM_LR * (m_hat / (jnp.sqrt(v_hat) + ADAM_EPS) + ADAM_WD * w)
    return delta, m, v


def adam_flat(g, w, m, v, *, name):
    def body(g_ref, w_ref, m_ref, v_ref, d_ref, nm_ref, nv_ref):
        d, nm, nv = _adamw(g_ref[...], w_ref[...], m_ref[...], v_ref[...])
        d_ref[...] = d
        nm_ref[...] = nm
        nv_ref[...] = nv

    blk = g.shape
    zero = lambda: (0, 0)
    return _pcall(body, name=name, grid=(),
                  ins=[(t, blk, zero) for t in (g, w, m, v)],
                  outs=[(blk, F32, blk, zero)] * 3)


def _chip_slots():
    x, y, c = lax.axis_index("x"), lax.axis_index("y"), lax.axis_index("c")
    chips = [(1 - x, y), (x, 1 - y), (1 - x, 1 - y)]
    return x, y, c, chips


def reduce_adam(gs, a_buf, b_buf, w, m, v, layer, prev, *, name, deps=()):
    L, R, C = w.shape
    tr = _tile(R, 256, 8)
    x, y, c, _ = _chip_slots()
    idx = jnp.stack([4 * x + 2 * y + c, 2 * x + y]).astype(jnp.int32)
    n_prev = 0 if prev is None else 4

    def body(idx_ref, g_ref, a_ref, b0_ref, b1_ref, b2_ref, w_ref, m_ref, v_ref, *rest):
        outs = rest[n_prev:]
        g = ((((g_ref[...].astype(F32) + a_ref[...].astype(F32)) + b0_ref[...].astype(F32))
              + b1_ref[...].astype(F32)) + b2_ref[...].astype(F32))
        d, nm, nv = _adamw(g, w_ref[...], m_ref[...], v_ref[...])
        outs[0][...] = g
        outs[1][...] = d
        outs[2][...] = nm
        outs[3][...] = nv

    blk3 = (None, tr, C)
    ins = [(gs, blk3, lambda i, s: (s[0], i, 0)), (a_buf, blk3, lambda i, s: (s[1], i, 0)),
           (b_buf, blk3, lambda i, s: (0, i, 0)), (b_buf, blk3, lambda i, s: (1, i, 0)),
           (b_buf, blk3, lambda i, s: (2, i, 0)),
           (w, blk3, lambda i, s: (layer, i, 0)), (m, blk3, lambda i, s: (layer, i, 0)),
           (v, blk3, lambda i, s: (layer, i, 0))]
    aliases = {}
    if prev is not None:
        for o, p in enumerate(prev):
            ins.append((p, None, None))
            aliases[1 + 8 + o] = o
    outs = [((L, R, C), F32, blk3, lambda i, s: (layer, i, 0))] * 4
    return _pcall(body, name=name, grid=(R // tr,), ins=ins, outs=outs, prefetch=idx, aliases=aliases,
                  semantics=("parallel",), deps=deps)


def pair_sum(gs, a_buf, *, name):
    _, R, C = gs.shape
    tr = _tile(R, 256, 8)
    x, y, c, chips = _chip_slots()
    idx = jnp.stack([4 * cx + 2 * cy + c for cx, cy in chips] + [2 * cx + cy for cx, cy in chips]).astype(jnp.int32)

    def body(idx_ref, g_ref, a_ref, o_ref):
        o_ref[...] = (g_ref[...].astype(F32) + a_ref[...].astype(F32)).astype(BF)

    blk3 = (None, tr, C)
    return _pcall(body, name=name, grid=(3, R // tr),
                  ins=[(gs, blk3, lambda j, i, s: (s[j], i, 0)), (a_buf, blk3, lambda j, i, s: (s[3 + j], i, 0))],
                  outs=[((3, R, C), BF, blk3, lambda j, i, s: (j, i, 0))], prefetch=idx,
                  semantics=("parallel", "parallel"))[0]


def sum_rows8(gathered, rows, *, name):
    W = gathered.shape[1]

    def body(g_ref, o_ref):
        acc = g_ref[0:rows, :]
        for d in range(1, N_DEV):
            acc = acc + g_ref[d * rows:(d + 1) * rows, :]
        o_ref[...] = acc

    return _pcall(body, name=name, grid=(), ins=[(gathered, gathered.shape, lambda: (0, 0))],
                  outs=[((rows, W), F32, (rows, W), lambda: (0, 0))])[0]


def _comm_call(body, *, name, ins, out_shapes, n_sem_rows, n_sem_cols, extra_scratch=()):
    any_spec = pl.BlockSpec(memory_space=pl.ANY)
    return pl.pallas_call(
        body, name=name, out_shape=[jax.ShapeDtypeStruct(s, d) for s, d in out_shapes],
        in_specs=[any_spec] * len(ins), out_specs=[any_spec] * len(out_shapes),
        scratch_shapes=[pltpu.SemaphoreType.DMA((n_sem_rows, n_sem_cols)),
                        pltpu.SemaphoreType.DMA((n_sem_rows, n_sem_cols))] + list(extra_scratch),
    )(*ins)


HBM_SPEC = pl.BlockSpec(memory_space=pltpu.HBM)
SEM_SPEC = pl.BlockSpec(memory_space=pltpu.SEMAPHORE)
ANY_SPEC = pl.BlockSpec(memory_space=pl.ANY)
DATAFLOW = pltpu.SideEffectType.DATAFLOW_SIDE_EFFECTING


def _in_hbm(v):
    return pltpu.with_memory_space_constraint(v, pltpu.HBM)


def _slot(p):
    return 4 * p[0] + 2 * p[1] + p[2]


def _gather_peers():
    x, y, c, chips = _chip_slots()
    return (x, y, c), [(x, y, 1 - c)] + [(*chip, c) for chip in chips]


def gather_start(groups, after, *, name):
    flat = [s for g in groups for s in g]
    n, n_g = len(flat), len(groups)
    where = [(gi, ti) for gi, g in enumerate(groups) for ti in range(len(g))]

    def body(*refs):
        src, land = refs[:n], refs[n:2 * n]
        sems = refs[2 * n + 1:2 * n + 1 + 2 * n_g]
        me, peers = _gather_peers()
        for t in range(n):
            gi, ti = where[t]
            for k, to in enumerate(peers):
                pltpu.make_async_remote_copy(
                    src_ref=src[t], dst_ref=land[t].at[_slot(me)], send_sem=sems[2 * gi].at[4 * ti + k],
                    recv_sem=sems[2 * gi + 1].at[4 * ti + k], device_id=to, device_id_type=MESH).start()

    out_shape = []
    for g in groups:
        out_shape += [pltpu.SemaphoreType.DMA((4 * len(g),)), pltpu.SemaphoreType.DMA((4 * len(g),))]
    out_shape += [pltpu.HBM(s.shape, s.dtype) for s in flat]
    out_shape += [pltpu.HBM((N_DEV,) + s.shape, s.dtype) for s in flat]
    aliases = {t: 2 * n_g + t for t in range(n)}
    aliases.update({n + t: 2 * n_g + n + t for t in range(n)})
    res = pl.pallas_call(
        body, name=name, out_shape=out_shape, in_specs=[HBM_SPEC] * (2 * n) + [ANY_SPEC],
        out_specs=[SEM_SPEC] * (2 * n_g) + [HBM_SPEC] * (2 * n), input_output_aliases=aliases,
        compiler_params=pltpu.CompilerParams(has_side_effects=DATAFLOW),
    )(*[_in_hbm(s) for s in flat], *[_in_hbm(lax.empty((N_DEV,) + s.shape, s.dtype)) for s in flat], after)
    out, off = [], 0
    for gi, g in enumerate(groups):
        k = len(g)
        out.append((res[2 * gi], res[2 * gi + 1], res[2 * n_g + off:2 * n_g + off + k],
                    res[2 * n_g + n + off:2 * n_g + n + off + k]))
        off += k
    return out


def gather_wait(started, after, *, name):
    send_sems, recv_sems, srcs, lands = started
    n = len(srcs)

    def body(*refs):
        src, land = refs[:n], refs[n:2 * n]
        send, recv = refs[2 * n], refs[2 * n + 1]
        _, peers = _gather_peers()
        for t in range(n):
            for k, frm in enumerate(peers):
                cp = pltpu.make_async_remote_copy(
                    src_ref=src[t], dst_ref=land[t].at[_slot(frm)], send_sem=send.at[4 * t + k],
                    recv_sem=recv.at[4 * t + k],
                    device_id=frm, device_id_type=MESH)
                cp.wait_send()
                cp.wait_recv()

    res = pl.pallas_call(
        body, name=name,
        out_shape=[pltpu.HBM(s.shape, s.dtype) for s in srcs] + [pltpu.HBM(l.shape, l.dtype) for l in lands],
        in_specs=[HBM_SPEC] * (2 * n) + [SEM_SPEC, SEM_SPEC, ANY_SPEC], out_specs=[HBM_SPEC] * (2 * n),
        input_output_aliases={t: t for t in range(2 * n)},
        compiler_params=pltpu.CompilerParams(has_side_effects=DATAFLOW),
    )(*srcs, *lands, send_sems, recv_sems, after)
    return res[:n], res[n:]


def place_own(src, land, *, name):
    R, C = src.shape
    tr = _tile(R, 512, 16)
    x, y, c, _ = _chip_slots()
    idx = jnp.stack([4 * x + 2 * y + c]).astype(jnp.int32)

    def body(idx_ref, s_ref, land_ref, o_ref):
        o_ref[...] = s_ref[...]

    return _pcall(body, name=name, grid=(R // tr,),
                  ins=[(src, (tr, C), lambda i, s: (i, 0)), (land, None, None)],
                  outs=[(land.shape, land.dtype, (None, tr, C), lambda i, s: (s[0], i, 0))],
                  prefetch=idx, aliases={2: 0}, semantics=("parallel",))[0]


def gather_finish(srcs, lands, *, name):
    n = len(srcs)

    def body(*refs):
        land = refs[n:2 * n]
        send_sems, recv_sems = refs[2 * n:]
        x, y, c, chips = _chip_slots()
        me, sibling = (x, y, c), (x, y, 1 - c)

        def copy(t, j, block, to):
            return pltpu.make_async_remote_copy(
                src_ref=land[t].at[_slot(block)], dst_ref=land[t].at[_slot(block)], send_sem=send_sems.at[t, j],
                recv_sem=recv_sems.at[t, j], device_id=to, device_id_type=MESH)

        sends = [copy(t, j, (*chip, c), sibling) for t in range(n) for j, chip in enumerate(chips)]
        for cp in sends:
            cp.start()
        for t in range(n):
            for j, chip in enumerate(chips):
                copy(t, j, (*chip, 1 - c), me).wait_recv()
        for cp in sends:
            cp.wait_send()

    passed = pl.pallas_call(
        body, name=name, out_shape=[jax.ShapeDtypeStruct(l.shape, l.dtype) for l in lands],
        in_specs=[ANY_SPEC] * n, out_specs=[ANY_SPEC] * n,
        input_output_aliases={t: t for t in range(n)},
        scratch_shapes=[pltpu.SemaphoreType.DMA((n, 3)), pltpu.SemaphoreType.DMA((n, 3))],
    )(*lands)
    return [place_own(s, l, name=f"{name}_own{t}") for t, (s, l) in enumerate(zip(srcs, passed))]


def chips_start(pairs, *, name):
    n = len(pairs)

    def body(*refs):
        src, land = refs[:n], refs[n:2 * n]
        send, recv = refs[2 * n], refs[2 * n + 1]
        token = refs[-1]
        x, y, c, chips = _chip_slots()
        for t in range(n):
            for j, chip in enumerate(chips):
                pltpu.make_async_remote_copy(
                    src_ref=src[t].at[j], dst_ref=land[t].at[j], send_sem=send.at[3 * t + j],
                    recv_sem=recv.at[3 * t + j], device_id=(*chip, c), device_id_type=MESH).start()
        token[...] = jnp.zeros_like(token)

    res = pl.pallas_call(
        body, name=name,
        out_shape=[pltpu.SemaphoreType.DMA((3 * n,)), pltpu.SemaphoreType.DMA((3 * n,))]
        + [pltpu.HBM(p.shape, p.dtype) for p in pairs] * 2 + [jax.ShapeDtypeStruct((8, LANES), F32)],
        in_specs=[HBM_SPEC] * (2 * n),
        out_specs=[SEM_SPEC, SEM_SPEC] + [HBM_SPEC] * (2 * n) + [pl.BlockSpec(memory_space=pltpu.VMEM)],
        input_output_aliases={t: 2 + t for t in range(2 * n)},
        compiler_params=pltpu.CompilerParams(has_side_effects=DATAFLOW),
    )(*[_in_hbm(p) for p in pairs], *[_in_hbm(lax.empty(p.shape, p.dtype)) for p in pairs])
    return res[0], res[1], res[2:2 + n], res[2 + n:2 + 2 * n], res[-1]


def chips_wait(started, after, *, name):
    send_sems, recv_sems, srcs, lands, _ = started
    n = len(srcs)

    def body(*refs):
        src, land = refs[:n], refs[n:2 * n]
        send, recv = refs[2 * n], refs[2 * n + 1]
        x, y, c, chips = _chip_slots()
        for t in range(n):
            for j, chip in enumerate(chips):
                cp = pltpu.make_async_remote_copy(
                    src_ref=src[t].at[j], dst_ref=land[t].at[j], send_sem=send.at[3 * t + j],
                    recv_sem=recv.at[3 * t + j], device_id=(*chip, c), device_id_type=MESH)
                cp.wait_send()
                cp.wait_recv()

    res = pl.pallas_call(
        body, name=name, out_shape=[pltpu.HBM(s.shape, s.dtype) for s in srcs] * 2,
        in_specs=[HBM_SPEC] * (2 * n) + [SEM_SPEC, SEM_SPEC, ANY_SPEC], out_specs=[HBM_SPEC] * (2 * n),
        input_output_aliases={t: t for t in range(2 * n)},
        compiler_params=pltpu.CompilerParams(has_side_effects=DATAFLOW),
    )(*srcs, *lands, send_sems, recv_sems, after)
    return res[n:]


def exchange_sibling(gs, *, name):
    n = len(gs)

    def body(*refs):
        src, dst = refs[:n], refs[n:2 * n]
        send_sems, recv_sems = refs[2 * n:]
        x, y, c, _ = _chip_slots()
        copies = []
        for t in range(n):
            for q in range(4):
                qx, qy = q // 2, q % 2
                copies.append(pltpu.make_async_remote_copy(
                    src_ref=src[t].at[4 * qx + 2 * qy + (1 - c)], dst_ref=dst[t].at[q],
                    send_sem=send_sems.at[t, q], recv_sem=recv_sems.at[t, q],
                    device_id=(x, y, 1 - c), device_id_type=MESH))
        for cp in copies:
            cp.start()
        for cp in copies:
            cp.wait()

    return _comm_call(body, name=name, ins=list(gs), out_shapes=[((4,) + g.shape[1:], g.dtype) for g in gs],
                      n_sem_rows=n, n_sem_cols=4)


def all_gather_vmem(x_shard, *, name, after=None):
    m_per, n = x_shard.shape
    n_after = 0 if after is None else 1

    def body(x_ref, *rest):
        out_ref, send_sems, recv_sems, local_sem = rest[n_after:]
        x, y, c, chips = _chip_slots()
        me, sibling = (x, y, c), (x, y, 1 - c)

        def rows(px, py, pc):
            return out_ref.at[pl.ds((4 * px + 2 * py + pc) * m_per, m_per), :]

        def copy(k, block, to, src=None):
            return pltpu.make_async_remote_copy(
                src_ref=rows(*block) if src is None else src, dst_ref=rows(*block),
                send_sem=send_sems.at[k], recv_sem=recv_sems.at[k], device_id=to, device_id_type=MESH)

        mine = pltpu.make_async_copy(x_ref, rows(*me), local_sem)
        mine.start()
        first = [copy(0, me, sibling, src=x_ref)]
        first += [copy(1 + j, me, (*chip, c), src=x_ref) for j, chip in enumerate(chips)]
        for cp in first:
            cp.start()
        passed = [copy(4 + j, (*chip, c), sibling) for j, chip in enumerate(chips)]
        for j, chip in enumerate(chips):
            copy(1 + j, (*chip, c), me).wait_recv()
            passed[j].start()
        copy(0, sibling, me).wait_recv()
        for j, chip in enumerate(chips):
            copy(4 + j, (*chip, 1 - c), me).wait_recv()
        for cp in first + passed:
            cp.wait_send()
        mine.wait()

    vmem = pl.BlockSpec(memory_space=pltpu.VMEM)
    return pl.pallas_call(
        body, name=name, out_shape=jax.ShapeDtypeStruct((N_DEV * m_per, n), x_shard.dtype),
        in_specs=[vmem] + [ANY_SPEC] * n_after, out_specs=vmem,
        scratch_shapes=[pltpu.SemaphoreType.DMA((7,)), pltpu.SemaphoreType.DMA((7,)), pltpu.SemaphoreType.DMA],
        compiler_params=pltpu.CompilerParams(vmem_limit_bytes=int(min(
            VMEM_LIMIT_CAP, 2 * (N_DEV + 1) * m_per * n * x_shard.dtype.itemsize + 16 * 2 ** 20))),
    )(x_shard, *([] if after is None else [after]))


def _rope_slab(cols):
    z = jnp.zeros(cols.shape[:-1] + (HALF_ROPE,), cols.dtype)
    return jnp.concatenate([cols[..., :HALF_ROPE], z, cols[..., HALF_ROPE:], z], axis=-1)


def _rope_unslab(slab):
    return jnp.concatenate([slab[..., :HALF_ROPE], slab[..., 2 * HALF_ROPE:3 * HALF_ROPE]], axis=-1)


def _pack_w_in(w_g):
    s, d, c = w_g.shape
    w = jnp.transpose(w_g, (1, 0, 2)).reshape(d, s * c)
    c1, c2, c3 = Q_LORA, Q_LORA + KV_LORA, Q_LORA + KV_LORA + QK_ROPE
    return jnp.concatenate([w[:, :c2], w[:, c3:], _rope_slab(w[:, c2:c3])], axis=-1)


def _unpack_w_in_grad(dw):
    d = dw.shape[0]
    c2 = Q_LORA + KV_LORA
    uv = 2 * SGU_OUT
    g = jnp.concatenate([dw[:, :c2], _rope_unslab(dw[:, c2 + uv:]), dw[:, c2:c2 + uv]], axis=-1)
    return jnp.transpose(g.reshape(d, N_DEV, g.shape[1] // N_DEV), (1, 0, 2))


def _rope_tables(positions):
    inv_freq = ROPE_BASE ** (-jnp.arange(0, QK_ROPE, 2, dtype=F32) / QK_ROPE)
    ang = positions.astype(F32)[:, None] * inv_freq
    cos, sin = jnp.cos(ang), jnp.sin(ang)
    z = jnp.zeros_like(cos)
    return jnp.concatenate([cos, z, cos, z], axis=-1), jnp.concatenate([-sin, z, sin, z], axis=-1)


def _mlp_up(x, gain, w1, tag):
    hn = rms_fwd(x, gain, name=f"mlp{tag}_norm")

    def act_epi(acc):
        a = jnp.maximum(acc, 0.0)
        return a, a * a

    T = x.shape[0]
    F = w1.shape[0] * w1.shape[2]
    a, act = mm(hn, w1, name=f"mlp{tag}_up", outs=[((T, F), BF, None), ((T, F), BF, None)], epi=act_epi)
    return hn, a, act


def _mlp_down(x, act, w2, tag):
    bm = _tile(x.shape[0], MM_TILE)
    bn = _tile(x.shape[1], MM_TILE)
    return mm(act, w2, name=f"mlp{tag}_down", out=(x.shape, F32), bm=bm, bn=bn,
              epi=lambda acc, r: (acc + r[...],), epi_ins=[(x, (bm, bn), lambda i, j, k: (i, j))])


def _mlp_bwd_weights(w1, w2, saved, dxb, tag):
    hn, a, act = saved
    T, D = dxb.shape
    F = a.shape[1]
    bm = _tile(T, MM_TILE)
    bn = _tile(F, min(MM_TILE, w1.shape[2]))
    dhid = mm(dxb, w2, tb=True, name=f"mlp{tag}_dhid", out=((T, F), BF), bm=bm, bn=bn,
              epi=lambda acc, a_ref: (2.0 * a_ref[...].astype(F32) * acc,),
              epi_ins=[(a, (bm, bn), lambda i, j, k: (i, j))])
    dw2 = mm(act, dxb, ta=True, name=f"mlp{tag}_dw2", out=((F, D), BF))
    dw1 = mm(hn, dhid, ta=True, name=f"mlp{tag}_dw1", out=(w1.shape, BF))
    return dhid, dw1, dw2.reshape(N_DEV, F // N_DEV, D)


def _mlp_bwd_input(x_in, gain, w1, dhid, dx, tag, deps):
    dhn = mm(dhid, w1, tb=True, name=f"mlp{tag}_dhn", out=(x_in.shape, F32), deps=deps)
    return rms_bwd(x_in, gain, dhn, dres=dx, name=f"mlp{tag}_norm_bwd")


def _reduce_start(grads, tag):
    a_bufs = exchange_sibling(grads, name=f"reduce_sibling_{tag}")
    pairs = [pair_sum(g, a, name=f"pair_sum_{tag}{t}") for t, (g, a) in enumerate(zip(grads, a_bufs))]
    return a_bufs, chips_start(pairs, name=f"reduce_chips_start_{tag}")


def kernel(x, positions, e_norm_mix, e_w_in, e_q_norm, e_w_uq, e_kv_norm, e_w_ukv, e_v_norm, e_sgu_w, e_sgu_b, e_mla_out_norm, e_sgu_out_norm, e_w_out, o_norm_mix, o_w_in, o_conv_w, o_w_out, mlp_norm, mlp_w1, mlp_w2, final_norm, loss_target, m_e_norm_mix, m_e_w_in, m_e_q_norm, m_e_w_uq, m_e_kv_norm, m_e_w_ukv, m_e_v_norm, m_e_sgu_w, m_e_sgu_b, m_e_mla_out_norm, m_e_sgu_out_norm, m_e_w_out, m_o_norm_mix, m_o_w_in, m_o_conv_w, m_o_w_out, m_mlp_norm, m_mlp_w1, m_mlp_w2, m_final_norm, v_e_norm_mix, v_e_w_in, v_e_q_norm, v_e_w_uq, v_e_kv_norm, v_e_w_ukv, v_e_v_norm, v_e_sgu_w, v_e_sgu_b, v_e_mla_out_norm, v_e_sgu_out_norm, v_e_w_out, v_o_norm_mix, v_o_w_in, v_o_conv_w, v_o_w_out, v_mlp_norm, v_mlp_w1, v_mlp_w2, v_final_norm):
    T, D = x.shape[1], x.shape[2]
    d_shard = o_norm_mix.shape[1]
    x0 = x[0]
    target = loss_target[0]
    me = 4 * lax.axis_index("x") + 2 * lax.axis_index("y") + lax.axis_index("c")

    bf = lambda s: s.astype(BF)
    gather_groups = [[bf(e_w_in[0]), bf(e_w_uq[0]), bf(e_w_ukv[0])], [bf(e_w_out[0]), bf(mlp_w1[0])],
                     [bf(mlp_w2[0]), bf(o_w_in[0])], [bf(o_w_out[0]), bf(mlp_w1[1])], [bf(mlp_w2[1])]]
    small_rows = jnp.concatenate([o_norm_mix, o_conv_w[0], jnp.zeros((4, d_shard), F32)], axis=0)
    small_flat = all_gather_vmem(small_rows, name="gather_small")
    started = gather_start(gather_groups, small_flat, name="gather_start")

    def gathered(gi, after):
        srcs, lands = gather_wait(started[gi], after, name=f"gather_wait{gi}")
        return gather_finish(srcs, lands, name=f"gather_finish{gi}")

    small_g = small_flat.reshape(N_DEV, 8, d_shard)
    o_norm_full = small_g[:, 0, :].reshape(1, D)
    conv_w_full = jnp.transpose(small_g[:, 1:4, :], (1, 0, 2)).reshape(3, D)
    w_tril = jnp.tril(e_sgu_w[0])
    w_tril_b = w_tril.astype(BF)
    w_tril_tb = jnp.swapaxes(w_tril, 1, 2).astype(BF)
    b_full = jnp.repeat(e_sgu_b[0].T, CH, axis=1)
    v_gain = e_v_norm[0].reshape(1, SGU_OUT)
    cos_t, sin_t = _rope_tables(positions[0])
    mlp_gain = [mlp_norm[0:1], mlp_norm[1:2]]
    final_gain = final_norm.reshape(1, D)

    h0 = rms_fwd(x0, e_norm_mix, name="e_norm")
    g_w_in, g_w_uq, w_ukv = gathered(0, h0)
    w_in_e = _pack_w_in(g_w_in)
    w_uq = jnp.concatenate([g_w_uq[..., :QK_NOPE], _rope_slab(g_w_uq[..., QK_NOPE:])], axis=-1)
    proj = mm(h0, w_in_e, name="e_in", out=((T, w_in_e.shape[1]), F32), bn=_tile(w_in_e.shape[1], 640))
    qn, kvn, krope = mla_prep(proj, e_q_norm, e_kv_norm, cos_t, sin_t, name="mla_prep")
    bm = _tile(T, MM_TILE)

    def q_epi(acc, cos_ref, sin_ref):
        return (jnp.concatenate([acc[:, :QK_NOPE], _rope_fwd(acc[:, QK_NOPE:], cos_ref[...], sin_ref[...])], axis=-1),)

    q = mm(qn, w_uq, name="mla_q", out=((T, HEADS * HEAD_PAD), BF), bm=bm, bn=HEAD_PAD, epi=q_epi,
           epi_ins=[(cos_t, (bm, LANES), lambda i, j, k: (i, 0)), (sin_t, (bm, LANES), lambda i, j, k: (i, 0))])

    def kv_epi(acc, kr_ref):
        return jnp.concatenate([acc[:, :QK_NOPE].astype(BF), kr_ref[...]], axis=-1), acc[:, QK_NOPE:]

    k, v = mm(kvn, w_ukv, name="mla_kv", bm=bm, bn=HEAD_PAD, epi=kv_epi,
              outs=[((T, HEADS * HEAD_PAD), BF, HEAD_PAD), ((T, MLA_OUT), BF, V_HEAD)],
              epi_ins=[(krope, (bm, LANES), lambda i, j, k: (i, 0))])
    attn = attn_fwd(q, k, v, name="attn_fwd")
    mixed = mix_fwd(attn, proj, e_mla_out_norm, e_sgu_out_norm, v_gain, w_tril_b, b_full, name="mix_fwd")
    bn = _tile(D, MM_TILE)
    g_w_out_e, w1_0 = gathered(1, mixed)
    w_out_e = g_w_out_e.reshape(-1, D)
    x1 = mm(mixed, w_out_e, name="e_out", out=((T, D), F32), bm=bm, bn=bn,
            epi=lambda acc, r: (acc + r[...],), epi_ins=[(x0, (bm, bn), lambda i, j, k: (i, j))])
    hn0, a0, act0 = _mlp_up(x1, mlp_gain[0], w1_0, 0)
    g_w2_0, g_w_in_o = gathered(2, act0)
    w2_0 = g_w2_0.reshape(-1, D)
    x2 = _mlp_down(x1, act0, w2_0, 0)
    ho = rms_fwd(x2, o_norm_full, name="o_norm")
    proj_o = mm(ho, g_w_in_o, name="o_in", out=((T, 3 * D), F32))
    gated = conv_fwd(proj_o, conv_w_full, name="conv_fwd")
    g_w_out_o, w1_1 = gathered(3, gated)
    w_out_o = g_w_out_o.reshape(-1, D)
    x3 = mm(gated, w_out_o, name="o_out", out=((T, D), F32), bm=bm, bn=bn,
            epi=lambda acc, r: (acc + r[...],), epi_ins=[(x2, (bm, bn), lambda i, j, k: (i, j))])
    hn1, a1, act1 = _mlp_up(x3, mlp_gain[1], w1_1, 1)
    (g_w2_1,) = gathered(4, act1)
    w2_1 = g_w2_1.reshape(-1, D)
    x4 = _mlp_down(x3, act1, w2_1, 1)
    w1, w2 = [w1_0, w1_1], [w2_0, w2_1]
    mlp0_saved, mlp1_saved = (hn0, a0, act0), (hn1, a1, act1)

    dx4, dx4b, d_final, loss_part = loss_bwd(x4, final_gain, target, name="loss_bwd")
    loss = lax.psum(loss_part[0, 0], AXES)

    dhid1, dw1_1, dw2_1 = _mlp_bwd_weights(w1[1], w2[1], mlp1_saved, dx4b, 1)
    grads_r0 = [dw1_1, dw2_1]
    a_r0, st_r0 = _reduce_start(grads_r0, "r0")
    dx3, dx3b, d_mlp1 = _mlp_bwd_input(x3, mlp_gain[1], w1[1], dhid1, dx4, 1, deps=[st_r0[-1]])

    dgated = mm(dx3b, w_out_o, tb=True, name="o_out_dx", out=((T, D), F32))
    dw_out_o = mm(gated, dx3b, ta=True, name="o_out_dw", out=((D, D), BF))
    dproj_o, dconv_full = conv_bwd(dgated, proj_o, conv_w_full, name="conv_bwd")
    dw_in_o = mm(ho, dproj_o, ta=True, name="o_in_dw", out=(g_w_in_o.shape, BF))
    grads_r1 = [dw_out_o.reshape(g_w_out_o.shape), dw_in_o]
    a_r1, st_r1 = _reduce_start(grads_r1, "r1")
    dho = mm(dproj_o, g_w_in_o, tb=True, name="o_in_dx", out=((T, D), F32), deps=[st_r1[-1]])
    dx2, dx2b, d_onorm_full = rms_bwd(x2, o_norm_full, dho, dres=dx3, name="o_norm_bwd")

    dhid0, dw1_0, dw2_0 = _mlp_bwd_weights(w1[0], w2[0], mlp0_saved, dx2b, 0)
    grads_r2 = [dw1_0, dw2_0]
    a_r2, st_r2 = _reduce_start(grads_r2, "r2")
    dx1, dx1b, d_mlp0 = _mlp_bwd_input(x1, mlp_gain[0], w1[0], dhid0, dx2, 0, deps=[st_r2[-1]])
    b_r0 = chips_wait(st_r0, dx1b, name="reduce_chips_wait_r0")

    dmixed = mm(dx1b, w_out_e, tb=True, name="e_out_dx", out=((T, MLA_OUT + SGU_OUT), F32))
    dw_out_e = mm(mixed, dx1b, ta=True, name="e_out_dw", out=(w_out_e.shape, BF))
    (dattn, duv, d_mla_out, d_sgu_out, d_vgain, d_sgu_w, d_b_full) = mix_bwd(
        dmixed, attn, proj, e_mla_out_norm, e_sgu_out_norm, v_gain, w_tril_b, w_tril_tb, b_full, name="mix_bwd")
    b_r1 = chips_wait(st_r1, dattn, name="reduce_chips_wait_r1")
    dq, dk, dv = attn_bwd(q, k, v, dattn, name="attn_bwd")
    dq_lin, dkv_lin, dkr = mla_bwd_prep(dq, dk, dv, cos_t, sin_t, name="mla_bwd_prep")
    dw_uq_pad = mm(qn, dq_lin, ta=True, name="mla_q_dw", out=(w_uq.shape, F32))
    dqn = mm(dq_lin, w_uq, tb=True, name="mla_q_dx", out=((T, Q_LORA), F32))
    dw_ukv = mm(kvn, dkv_lin, ta=True, name="mla_kv_dw", out=(w_ukv.shape, BF))
    dkvn = mm(dkv_lin, w_ukv, tb=True, name="mla_kv_dx", out=((T, KV_LORA), F32))
    dcq, d_qnorm = rms_bwd(proj, e_q_norm, dqn, col_block=0, want_f32=False, name="q_norm_bwd")
    dckv, d_kvnorm = rms_bwd(proj, e_kv_norm, dkvn, col_block=1, want_f32=False, name="kv_norm_bwd")
    dproj = jnp.concatenate([dcq, dckv, duv, dkr], axis=-1)
    dw_in_e_pad = mm(h0, dproj, ta=True, name="e_in_dw", out=(w_in_e.shape, F32), bn=_tile(w_in_e.shape[1], 640))
    dw_in_e = _unpack_w_in_grad(dw_in_e_pad).astype(BF)
    dw_uq = jnp.concatenate([dw_uq_pad[..., :QK_NOPE], _rope_unslab(dw_uq_pad[..., QK_NOPE:])], axis=-1).astype(BF)
    grads_r3 = [dw_out_e.reshape(g_w_out_e.shape), dw_uq, dw_ukv, dw_in_e]
    a_r3, st_r3 = _reduce_start(grads_r3, "r3")
    tok_r3 = st_r3[-1]
    dh0 = mm(dproj, w_in_e, tb=True, name="e_in_dx", out=((T, D), F32), deps=[tok_r3])
    grad_x, d_enorm = rms_bwd(x0, e_norm_mix, dh0, dres=dx1, want_bf=False, name="e_norm_bwd")
    b_r2 = chips_wait(st_r2, grad_x, name="reduce_chips_wait_r2")

    def finish(grads, a_bufs, b_bufs, t, w, m, v, layer=0, prev=None, tag="", deps=()):
        return reduce_adam(grads[t], a_bufs[t], b_bufs[t], w, m, v, layer, prev, name=f"adam_{tag}", deps=deps)

    r_w1 = finish(grads_r0, a_r0, b_r0, 0, mlp_w1, m_mlp_w1, v_mlp_w1, 1, None, tag="w1_l1")
    r_w2 = finish(grads_r0, a_r0, b_r0, 1, mlp_w2, m_mlp_w2, v_mlp_w2, 1, None, tag="w2_l1")
    r_w_out_o = finish(grads_r1, a_r1, b_r1, 0, o_w_out, m_o_w_out, v_o_w_out, tag="o_w_out")
    r_w_in_o = finish(grads_r1, a_r1, b_r1, 1, o_w_in, m_o_w_in, v_o_w_in, tag="o_w_in")
    r_w1 = finish(grads_r2, a_r2, b_r2, 0, mlp_w1, m_mlp_w1, v_mlp_w1, 0, r_w1, tag="w1_l0", deps=[tok_r3])
    r_w2 = finish(grads_r2, a_r2, b_r2, 1, mlp_w2, m_mlp_w2, v_mlp_w2, 0, r_w2, tag="w2_l0", deps=[tok_r3])
    b_r3 = chips_wait(st_r3, r_w2[1], name="reduce_chips_wait_r3")
    r_w_out_e = finish(grads_r3, a_r3, b_r3, 0, e_w_out, m_e_w_out, v_e_w_out, tag="e_w_out")
    r_w_uq = finish(grads_r3, a_r3, b_r3, 1, e_w_uq, m_e_w_uq, v_e_w_uq, tag="e_w_uq")
    r_w_ukv = finish(grads_r3, a_r3, b_r3, 2, e_w_ukv, m_e_w_ukv, v_e_w_ukv, tag="e_w_ukv")
    r_w_in = finish(grads_r3, a_r3, b_r3, 3, e_w_in, m_e_w_in, v_e_w_in, tag="e_w_in")

    d_sgu_b = jnp.transpose(d_b_full[:, ::CH])
    d_sgu_w_tril = jnp.tril(d_sgu_w)
    rep = [("e_norm_mix", e_norm_mix, m_e_norm_mix, v_e_norm_mix, d_enorm),
           ("e_q_norm", e_q_norm, m_e_q_norm, v_e_q_norm, d_qnorm),
           ("e_kv_norm", e_kv_norm, m_e_kv_norm, v_e_kv_norm, d_kvnorm),
           ("e_v_norm", e_v_norm, m_e_v_norm, v_e_v_norm, d_vgain),
           ("e_sgu_w", e_sgu_w, m_e_sgu_w, v_e_sgu_w, d_sgu_w_tril),
           ("e_sgu_b", e_sgu_b, m_e_sgu_b, v_e_sgu_b, d_sgu_b),
           ("e_mla_out_norm", e_mla_out_norm, m_e_mla_out_norm, v_e_mla_out_norm, d_mla_out),
           ("e_sgu_out_norm", e_sgu_out_norm, m_e_sgu_out_norm, v_e_sgu_out_norm, d_sgu_out),
           ("mlp_norm", mlp_norm, m_mlp_norm, v_mlp_norm, jnp.concatenate([d_mlp0, d_mlp1], axis=0)),
           ("final_norm", final_norm, m_final_norm, v_final_norm, d_final)]
    sizes = [int(np.prod(r[1].shape)) for r in rep]
    n_rep = sum(sizes)
    n_all = n_rep + 4 * D
    width = -(-n_all // (8 * LANES)) * LANES
    pad = 8 * width - n_all
    flat = jnp.concatenate([r[4].reshape(-1) for r in rep]
                           + [d_onorm_full.reshape(-1), dconv_full.reshape(-1), jnp.zeros((pad,), F32)])
    summed = sum_rows8(all_gather_vmem(flat.reshape(8, width), name="gather_small_grads", after=b_r3[0]), 8,
                       name="sum_small_grads").reshape(-1)

    def pack_rep(i):
        return jnp.concatenate([r[i].reshape(-1) for r in rep]).reshape(n_rep // LANES, LANES)

    g_rep = summed[:n_rep].reshape(n_rep // LANES, LANES)
    d_rep, nm_rep, nv_rep = adam_flat(g_rep, pack_rep(1), pack_rep(2), pack_rep(3), name="adam_replicated")

    def unpack_rep(flat2d):
        out, off = {}, 0
        f = flat2d.reshape(-1)
        for r, n in zip(rep, sizes):
            out[r[0]] = f[off:off + n].reshape(r[1].shape)
            off += n
        return out

    small = {"grad": unpack_rep(g_rep), "delta": unpack_rep(d_rep), "new_m": unpack_rep(nm_rep),
             "new_v": unpack_rep(nv_rep)}
    g_onorm = lax.dynamic_slice(summed[n_rep:n_rep + D].reshape(1, D), (0, me * d_shard), (1, d_shard))
    g_conv = lax.dynamic_slice(summed[n_rep + D:n_rep + 4 * D].reshape(3, D), (0, me * d_shard), (3, d_shard))

    def pack_sharded(norm_part, conv_part):
        return jnp.concatenate([norm_part, conv_part, jnp.zeros((4, d_shard), F32)], axis=0)

    g_sh = pack_sharded(g_onorm, g_conv)
    d_sh, nm_sh, nv_sh = adam_flat(g_sh, pack_sharded(o_norm_mix, o_conv_w[0]), pack_sharded(m_o_norm_mix, m_o_conv_w[0]),
                                   pack_sharded(v_o_norm_mix, v_o_conv_w[0]), name="adam_sharded_small")
    for kind, arr in (("grad", g_sh), ("delta", d_sh), ("new_m", nm_sh), ("new_v", nv_sh)):
        small[kind]["o_norm_mix"] = arr[0:1]
        small[kind]["o_conv_w"] = arr[1:4][None]

    big = {"e_w_in": r_w_in, "e_w_uq": r_w_uq, "e_w_ukv": r_w_ukv, "e_w_out": r_w_out_e, "o_w_in": r_w_in_o,
           "o_w_out": r_w_out_o, "mlp_w1": r_w1, "mlp_w2": r_w2}
    order = ["e_norm_mix", "e_w_in", "e_q_norm", "e_w_uq", "e_kv_norm", "e_w_ukv", "e_v_norm", "e_sgu_w", "e_sgu_b",
             "e_mla_out_norm", "e_sgu_out_norm", "e_w_out", "o_norm_mix", "o_w_in", "o_conv_w", "o_w_out", "mlp_norm",
             "mlp_w1", "mlp_w2", "final_norm"]
    result = [loss, grad_x[None]]
    for ki, kind in enumerate(("grad", "delta", "new_m", "new_v")):
        for nm in order:
            result.append(big[nm][ki] if nm in big else small[kind][nm])
    return tuple(result)
```

```python
import functools

import numpy as np
import jax
import jax.numpy as jnp
from jax import lax
from jax.experimental import pallas as pl
from jax.experimental.pallas import tpu as pltpu

BF = jnp.bfloat16
F32 = jnp.float32
MESH = pl.DeviceIdType.MESH
AXES = ("x", "y", "c")
N_DEV = 8

EPS = 1e-6
HEADS = 8
Q_LORA = 512
KV_LORA = 512
QK_NOPE = 128
QK_ROPE = 64
HALF_ROPE = QK_ROPE // 2
V_HEAD = 128
HEAD_PAD = 256
ROPE_BASE = 10000.0
GROUPS = 8
CH = 128
CHUNK = 128
SGU_OUT = GROUPS * CH
MLA_OUT = HEADS * V_HEAD
ATTN_SCALE = float((QK_NOPE + QK_ROPE) ** -0.5)

ADAM_LR = 0.001
ADAM_B1 = 0.9
ADAM_B2 = 0.999
ADAM_EPS = 1e-08
ADAM_WD = 0.01
ADAM_STEP = 10
ADAM_C1 = 1.0 - ADAM_B1 ** ADAM_STEP
ADAM_C2 = 1.0 - ADAM_B2 ** ADAM_STEP

V7X_VMEM_BYTES = 64 * 2 ** 20
VMEM_LIMIT_CAP = V7X_VMEM_BYTES - 6 * 2 ** 20
LANES = 128
ROW_TILE = 256
MM_TILE = 1024
MM_K_TILE = 2048


def _padded_bytes(block, dtype):
    dims = [d for d in block if d is not None]
    if len(dims) >= 1:
        dims[-1] = -(-dims[-1] // LANES) * LANES
    if len(dims) >= 2:
        dims[-2] = -(-dims[-2] // 16) * 16
    return int(np.prod(dims)) * jnp.dtype(dtype).itemsize


def _pcall(body, *, name, grid, ins, outs, scratch=(), semantics=None, aliases=None, prefetch=None, deps=()):
    any_spec = pl.BlockSpec(memory_space=pl.ANY)
    if deps:
        n_lead = len(ins) + (1 if prefetch is not None else 0)
        n_deps = len(deps)
        inner = body

        def body(*refs):
            inner(*refs[:n_lead], *refs[n_lead + n_deps:])

        ins = list(ins) + [(d, None, None) for d in deps]
    in_specs = [any_spec if b is None else pl.BlockSpec(b, m) for _, b, m in ins]
    out_specs = [any_spec if b is None else pl.BlockSpec(b, m) for _, _, b, m in outs]
    out_shape = [pltpu.HBM(s, d) for s, d, _, _ in outs]
    est = 0
    for a, b, _ in ins:
        if b is not None:
            est += 2 * _padded_bytes(b, a.dtype)
    for _, d, b, _ in outs:
        if b is not None:
            est += 2 * _padded_bytes(b, d)
    for s in scratch:
        if hasattr(s, "shape") and hasattr(s, "dtype"):
            est += _padded_bytes(s.shape, s.dtype)
    limit = int(min(VMEM_LIMIT_CAP, est + 16 * 2 ** 20))
    params = pltpu.CompilerParams(
        dimension_semantics=semantics or ("arbitrary",) * len(grid), vmem_limit_bytes=limit)
    args = [pltpu.with_memory_space_constraint(a, pltpu.HBM) for a, _, _ in ins]
    if prefetch is not None:
        grid_spec = pltpu.PrefetchScalarGridSpec(
            num_scalar_prefetch=1, grid=grid, in_specs=in_specs, out_specs=out_specs, scratch_shapes=list(scratch))
        call = pl.pallas_call(body, out_shape=out_shape, grid_spec=grid_spec, name=name, compiler_params=params,
                              input_output_aliases=aliases or {})
        return call(prefetch, *args)
    call = pl.pallas_call(body, out_shape=out_shape, grid=grid, in_specs=in_specs, out_specs=out_specs,
                          scratch_shapes=list(scratch), name=name, compiler_params=params,
                          input_output_aliases=aliases or {})
    return call(*args)


def _tile(dim, pref, quantum=LANES):
    if dim <= pref:
        return dim
    t = (pref // quantum) * quantum
    while t >= quantum:
        if dim % t == 0:
            return t
        t -= quantum
    return dim


def _vshape(arr_shape):
    if len(arr_shape) == 2:
        return tuple(arr_shape)
    s, r, c = arr_shape
    return (r, s * c)


def _vblock(arr_shape, br, bc, rc):
    if len(arr_shape) == 2:
        return (br, bc), (lambda *g: rc(*g))
    _, _, c = arr_shape
    assert c % bc == 0, (arr_shape, bc)
    per = c // bc

    def imap(*g):
        ri, ci = rc(*g)
        return (ci // per, ri, ci % per)

    return (None, br, bc), imap


def _shard_width(*shapes):
    w = None
    for s in shapes:
        if len(s) == 3:
            w = s[2] if w is None else int(np.gcd(w, s[2]))
    return w


def mm(a, b, *, name, ta=False, tb=False, out=None, outs=None, epi=None, epi_ins=(), bm=None, bn=None, bk=None,
       deps=()):
    av, bv = _vshape(a.shape), _vshape(b.shape)
    M, K = (av[1], av[0]) if ta else av
    K2, N = (bv[1], bv[0]) if tb else bv
    assert K == K2, (a.shape, b.shape, ta, tb)
    if outs is None:
        outs = [(out[0], out[1], None)]
    a_sw = _shard_width(a.shape)
    b_sw = _shard_width(b.shape)
    o_sw = _shard_width(*[o[0] for o in outs])
    m_lim = a_sw if (ta and a_sw) else None
    k_lim = [w for w in ((a_sw if not ta else None), (b_sw if tb else None)) if w]
    n_lim = [w for w in ((b_sw if not tb else None), o_sw) if w]
    if bm is None:
        bm = _tile(M, min([MM_TILE] + ([m_lim] if m_lim else [])))
    if bn is None:
        bn = _tile(N, min([MM_TILE] + n_lim))
    if bk is None:
        bk = K if (K <= 4096 and not k_lim) else _tile(K, min([MM_K_TILE] + k_lim))
    assert M % bm == 0 and N % bn == 0 and K % bk == 0, (name, M, N, K, bm, bn, bk)
    nk = K // bk
    grid = (M // bm, N // bn, nk)
    if ta:
        a_blk, a_map = _vblock(a.shape, bk, bm, lambda i, j, k: (k, i))
    else:
        a_blk, a_map = _vblock(a.shape, bm, bk, lambda i, j, k: (i, k))
    if tb:
        b_blk, b_map = _vblock(b.shape, bn, bk, lambda i, j, k: (j, k))
    else:
        b_blk, b_map = _vblock(b.shape, bk, bn, lambda i, j, k: (k, j))
    dn = (((0 if ta else 1,), (1 if tb else 0,)), ((), ()))
    ins = [(a, a_blk, a_map), (b, b_blk, b_map)] + list(epi_ins)
    out_list = []
    for shape, dtype, cols in outs:
        cols = cols or bn
        blk, imap = _vblock(shape, bm, cols, lambda i, j, k: (i, j))
        out_list.append((shape, dtype, blk, imap))
    n_e, n_o = len(epi_ins), len(out_list)

    def body(*refs):
        a_ref, b_ref = refs[0], refs[1]
        e_refs = refs[2:2 + n_e]
        o_refs = refs[2 + n_e:2 + n_e + n_o]

        def finish(acc):
            res = epi(acc, *e_refs) if epi is not None else (acc,)
            for o_ref, r in zip(o_refs, res):
                o_ref[...] = r.astype(o_ref.dtype)

        x = a_ref[...].astype(BF)
        y = b_ref[...].astype(BF)
        p = lax.dot_general(x, y, dn, preferred_element_type=F32)
        if nk == 1:
            finish(p)
        else:
            acc_ref = refs[-1]
            k = pl.program_id(2)

            @pl.when(k == 0)
            def _():
                acc_ref[...] = p

            @pl.when(k > 0)
            def _():
                acc_ref[...] += p

            @pl.when(k == nk - 1)
            def _():
                finish(acc_ref[...])

    scratch = [pltpu.VMEM((bm, bn), F32)] if nk > 1 else []
    res = _pcall(body, name=name, grid=grid, ins=ins, outs=out_list, scratch=scratch,
                 semantics=("parallel", "parallel", "arbitrary"), deps=deps)
    return res[0] if len(res) == 1 else res


_GELU_K = float(np.sqrt(2.0 / np.pi))
_GELU_C = 0.044715


def _gelu(x):
    t = jnp.tanh(_GELU_K * (x + _GELU_C * (x * x * x)))
    return 0.5 * x * (1.0 + t)


def _gelu_grad(x):
    t = jnp.tanh(_GELU_K * (x + _GELU_C * (x * x * x)))
    return 0.5 * (1.0 + t) + 0.5 * x * (1.0 - t * t) * (_GELU_K * (1.0 + 3.0 * _GELU_C * (x * x)))


def _rstd(x):
    return lax.rsqrt(jnp.mean(x * x, axis=-1, keepdims=True) + EPS)


def _rms_bwd(x, gain, dy):
    r = _rstd(x)
    xh = x * r
    gdy = dy * gain
    dx = r * (gdy - xh * jnp.mean(gdy * xh, axis=-1, keepdims=True))
    return dx, dy * xh


def _rope_fwd(x, cos_t, sin_t):
    return x * cos_t + pltpu.roll(x, 2 * HALF_ROPE, 1) * sin_t


def _rope_bwd(dy, cos_t, sin_t):
    return dy * cos_t + pltpu.roll(dy * sin_t, 2 * HALF_ROPE, 1)


def _acc_rows(ref, val, first):
    s = jnp.sum(val, axis=0, keepdims=True)

    @pl.when(first)
    def _():
        ref[...] = s

    @pl.when(jnp.logical_not(first))
    def _():
        ref[...] += s


def rms_fwd(x, gain, *, name, col_block=0, width=None):
    T = x.shape[0]
    width = width or x.shape[1]
    tm = _tile(T, ROW_TILE, 8)

    def body(x_ref, g_ref, o_ref):
        v = x_ref[...]
        o_ref[...] = (v * _rstd(v) * g_ref[...]).astype(BF)

    return _pcall(body, name=name, grid=(T // tm,),
                  ins=[(x, (tm, width), lambda i: (i, col_block)), (gain, (1, width), lambda i: (0, 0))],
                  outs=[((T, width), BF, (tm, width), lambda i: (i, 0))], semantics=("parallel",))[0]


def rms_bwd(x, gain, dy, *, name, col_block=0, dres=None, want_f32=True, want_bf=True, deps=()):
    T, width = dy.shape
    tm = _tile(T, ROW_TILE, 8)
    has_res = dres is not None

    def body(*refs):
        x_ref, g_ref, dy_ref = refs[:3]
        pos = 3
        res_ref = None
        if has_res:
            res_ref = refs[pos]
            pos += 1
        outs = refs[pos:]
        dx, dg_rows = _rms_bwd(x_ref[...], g_ref[...], dy_ref[...])
        if has_res:
            dx = dx + res_ref[...]
        o = 0
        if want_f32:
            outs[o][...] = dx
            o += 1
        if want_bf:
            outs[o][...] = dx.astype(BF)
            o += 1
        _acc_rows(outs[o], dg_rows, pl.program_id(0) == 0)

    ins = [(x, (tm, width), lambda i: (i, col_block)), (gain, (1, width), lambda i: (0, 0)),
           (dy, (tm, width), lambda i: (i, 0))]
    if has_res:
        ins.append((dres, (tm, width), lambda i: (i, 0)))
    outs = []
    if want_f32:
        outs.append(((T, width), F32, (tm, width), lambda i: (i, 0)))
    if want_bf:
        outs.append(((T, width), BF, (tm, width), lambda i: (i, 0)))
    outs.append(((1, width), F32, (1, width), lambda i: (0, 0)))
    return _pcall(body, name=name, grid=(T // tm,), ins=ins, outs=outs, deps=deps)


def mla_prep(proj, q_norm, kv_norm, cos_t, sin_t, *, name):
    T = proj.shape[0]
    tm = _tile(T, ROW_TILE, 8)
    kr_block = (proj.shape[1] - LANES) // LANES

    def body(cq_ref, ckv_ref, kr_ref, qg_ref, kg_ref, cos_ref, sin_ref, qn_ref, kvn_ref, krope_ref):
        cq = cq_ref[...]
        qn_ref[...] = (cq * _rstd(cq) * qg_ref[...]).astype(BF)
        ckv = ckv_ref[...]
        kvn_ref[...] = (ckv * _rstd(ckv) * kg_ref[...]).astype(BF)
        krope_ref[...] = _rope_fwd(kr_ref[...], cos_ref[...], sin_ref[...]).astype(BF)

    return _pcall(
        body, name=name, grid=(T // tm,),
        ins=[(proj, (tm, Q_LORA), lambda i: (i, 0)), (proj, (tm, KV_LORA), lambda i: (i, 1)),
             (proj, (tm, LANES), lambda i: (i, kr_block)),
             (q_norm, (1, Q_LORA), lambda i: (0, 0)), (kv_norm, (1, KV_LORA), lambda i: (0, 0)),
             (cos_t, (tm, LANES), lambda i: (i, 0)), (sin_t, (tm, LANES), lambda i: (i, 0))],
        outs=[((T, Q_LORA), BF, (tm, Q_LORA), lambda i: (i, 0)), ((T, KV_LORA), BF, (tm, KV_LORA), lambda i: (i, 0)),
              ((T, LANES), BF, (tm, LANES), lambda i: (i, 0))],
        semantics=("parallel",))


def _attn_probs(q_ref, k_ref, tq, T):
    s = lax.dot_general(q_ref[...], k_ref[...], (((1,), (1,)), ((), ())), preferred_element_type=F32) * ATTN_SCALE
    row = pl.program_id(1) * tq + lax.broadcasted_iota(jnp.int32, (tq, T), 0)
    col = lax.broadcasted_iota(jnp.int32, (tq, T), 1)
    s = jnp.where(col <= row, s, -jnp.inf)
    e = jnp.exp(s - jnp.max(s, axis=-1, keepdims=True))
    return e / jnp.sum(e, axis=-1, keepdims=True)


def attn_fwd(q, k, v, *, name):
    T = q.shape[0]
    tq = _tile(T, ROW_TILE, 8)

    def body(q_ref, k_ref, v_ref, o_ref):
        p = _attn_probs(q_ref, k_ref, tq, T)
        o_ref[...] = jnp.dot(p.astype(BF), v_ref[...], preferred_element_type=F32)

    return _pcall(
        body, name=name, grid=(HEADS, T // tq),
        ins=[(q, (tq, HEAD_PAD), lambda h, i: (i, h)), (k, (T, HEAD_PAD), lambda h, i: (0, h)),
             (v, (T, V_HEAD), lambda h, i: (0, h))],
        outs=[((T, MLA_OUT), F32, (tq, V_HEAD), lambda h, i: (i, h))], semantics=("parallel", "parallel"))[0]


def attn_bwd(q, k, v, do, *, name):
    T = q.shape[0]
    tq = _tile(T, ROW_TILE, 8)

    def body(q_ref, k_ref, v_ref, do_ref, dq_ref, dk_ref, dv_ref):
        p = _attn_probs(q_ref, k_ref, tq, T)
        do_t = do_ref[...]
        dp = lax.dot_general(do_t, v_ref[...], (((1,), (1,)), ((), ())), preferred_element_type=F32)
        ds = (p * (dp - jnp.sum(p * dp, axis=-1, keepdims=True)) * ATTN_SCALE).astype(BF)
        dq_ref[...] = jnp.dot(ds, k_ref[...], preferred_element_type=F32)
        dk_t = lax.dot_general(ds, q_ref[...], (((0,), (0,)), ((), ())), preferred_element_type=F32)
        dv_t = lax.dot_general(p.astype(BF), do_t, (((0,), (0,)), ((), ())), preferred_element_type=F32)
        first = pl.program_id(1) == 0

        @pl.when(first)
        def _():
            dk_ref[...] = dk_t
            dv_ref[...] = dv_t

        @pl.when(jnp.logical_not(first))
        def _():
            dk_ref[...] += dk_t
            dv_ref[...] += dv_t

    return _pcall(
        body, name=name, grid=(HEADS, T // tq),
        ins=[(q, (tq, HEAD_PAD), lambda h, i: (i, h)), (k, (T, HEAD_PAD), lambda h, i: (0, h)),
             (v, (T, V_HEAD), lambda h, i: (0, h)), (do, (tq, V_HEAD), lambda h, i: (i, h))],
        outs=[((T, HEADS * HEAD_PAD), F32, (tq, HEAD_PAD), lambda h, i: (i, h)),
              ((T, HEADS * HEAD_PAD), F32, (T, HEAD_PAD), lambda h, i: (0, h)),
              ((T, MLA_OUT), F32, (T, V_HEAD), lambda h, i: (0, h))],
        semantics=("parallel", "arbitrary"))


def mla_bwd_prep(dq, dk, dv, cos_t, sin_t, *, name):
    T = dq.shape[0]
    tm = _tile(T, ROW_TILE, 8)

    def body(dq_ref, dk_ref, dv_ref, cos_ref, sin_ref, dql_ref, dkvl_ref, dkr_ref):
        cos_v, sin_v = cos_ref[...], sin_ref[...]
        kr = jnp.zeros((tm, LANES), F32)
        for h in range(HEADS):
            lo = h * HEAD_PAD
            dql_ref[:, lo:lo + QK_NOPE] = dq_ref[:, lo:lo + QK_NOPE].astype(BF)
            dql_ref[:, lo + QK_NOPE:lo + HEAD_PAD] = _rope_bwd(
                dq_ref[:, lo + QK_NOPE:lo + HEAD_PAD], cos_v, sin_v).astype(BF)
            dkvl_ref[:, lo:lo + QK_NOPE] = dk_ref[:, lo:lo + QK_NOPE].astype(BF)
            dkvl_ref[:, lo + QK_NOPE:lo + HEAD_PAD] = dv_ref[:, h * V_HEAD:(h + 1) * V_HEAD].astype(BF)
            kr = kr + dk_ref[:, lo + QK_NOPE:lo + HEAD_PAD]
        dkr_ref[...] = _rope_bwd(kr, cos_v, sin_v).astype(BF)

    W = HEADS * HEAD_PAD
    return _pcall(
        body, name=name, grid=(T // tm,),
        ins=[(dq, (tm, W), lambda i: (i, 0)), (dk, (tm, W), lambda i: (i, 0)), (dv, (tm, MLA_OUT), lambda i: (i, 0)),
             (cos_t, (tm, LANES), lambda i: (i, 0)), (sin_t, (tm, LANES), lambda i: (i, 0))],
        outs=[((T, W), BF, (tm, W), lambda i: (i, 0)), ((T, W), BF, (tm, W), lambda i: (i, 0)),
              ((T, LANES), BF, (tm, LANES), lambda i: (i, 0))],
        semantics=("parallel",))


def _group_norm_stats(vg):
    mu = jnp.mean(vg, axis=-1, keepdims=True)
    d = vg - mu
    r = lax.rsqrt(jnp.mean(d * d, axis=-1, keepdims=True) + EPS)
    return d * r, r


def mix_fwd(a, proj, g_mla, g_sgu, v_gain, w_tril, b_full, *, name):
    T = a.shape[0]
    tm = _tile(T, ROW_TILE, CHUNK)
    n_chunk = tm // CHUNK

    def body(a_ref, u_ref, v_ref, gm_ref, gs_ref, vg_ref, w_ref, b_ref, o_ref, s_scr):
        av = a_ref[...]
        o_ref[:, :MLA_OUT] = (av * _rstd(av) * gm_ref[...]).astype(BF)
        for g in range(GROUPS):
            sl = slice(g * CH, (g + 1) * CH)
            vhat, _ = _group_norm_stats(_gelu(v_ref[:, sl]))
            vn = (vhat * vg_ref[:, sl]).astype(BF)
            u = _gelu(u_ref[:, sl])
            for ci in range(n_chunk):
                rs = slice(ci * CHUNK, (ci + 1) * CHUNK)
                y = jnp.dot(w_ref[g], vn[rs], preferred_element_type=F32) + b_ref[:, sl]
                s_scr[rs, sl] = u[rs] * y
        s = s_scr[...]
        o_ref[:, MLA_OUT:] = (s * _rstd(s) * gs_ref[...]).astype(BF)

    return _pcall(
        body, name=name, grid=(T // tm,),
        ins=[(a, (tm, MLA_OUT), lambda i: (i, 0)), (proj, (tm, SGU_OUT), lambda i: (i, 1)),
             (proj, (tm, SGU_OUT), lambda i: (i, 2)), (g_mla, (1, MLA_OUT), lambda i: (0, 0)),
             (g_sgu, (1, SGU_OUT), lambda i: (0, 0)), (v_gain, (1, SGU_OUT), lambda i: (0, 0)),
             (w_tril, (GROUPS, CHUNK, CHUNK), lambda i: (0, 0, 0)), (b_full, (CHUNK, SGU_OUT), lambda i: (0, 0))],
        outs=[((T, MLA_OUT + SGU_OUT), BF, (tm, MLA_OUT + SGU_OUT), lambda i: (i, 0))],
        scratch=[pltpu.VMEM((tm, SGU_OUT), F32)], semantics=("parallel",))[0]


def mix_bwd(dmixed, a, proj, g_mla, g_sgu, v_gain, w_tril, w_tril_t, b_full, *, name):
    T = a.shape[0]
    tm = _tile(T, ROW_TILE, CHUNK)
    n_chunk = tm // CHUNK

    def body(dm_a_ref, dm_s_ref, a_ref, u_ref, v_ref, gm_ref, gs_ref, vg_ref, w_ref, wt_ref, b_ref,
             da_ref, duv_ref, dgm_ref, dgs_ref, dvg_ref, dw_ref, db_ref, s_scr, y_scr):
        first = pl.program_id(0) == 0
        da, dgm_rows = _rms_bwd(a_ref[...], gm_ref[...], dm_a_ref[...])
        da_ref[...] = da.astype(BF)
        _acc_rows(dgm_ref, dgm_rows, first)

        for g in range(GROUPS):
            sl = slice(g * CH, (g + 1) * CH)
            vhat, _ = _group_norm_stats(_gelu(v_ref[:, sl]))
            vn = (vhat * vg_ref[:, sl]).astype(BF)
            u = _gelu(u_ref[:, sl])
            for ci in range(n_chunk):
                rs = slice(ci * CHUNK, (ci + 1) * CHUNK)
                y = jnp.dot(w_ref[g], vn[rs], preferred_element_type=F32) + b_ref[:, sl]
                y_scr[rs, sl] = y
                s_scr[rs, sl] = u[rs] * y
        ds, dgs_rows = _rms_bwd(s_scr[...], gs_ref[...], dm_s_ref[...])
        _acc_rows(dgs_ref, dgs_rows, first)
        s_scr[...] = ds

        @pl.when(first)
        def _():
            dw_ref[...] = jnp.zeros_like(dw_ref)
            db_ref[...] = jnp.zeros_like(db_ref)

        for g in range(GROUPS):
            sl = slice(g * CH, (g + 1) * CH)
            upre = u_ref[:, sl]
            vpre = v_ref[:, sl]
            u = _gelu(upre)
            vhat, r = _group_norm_stats(_gelu(vpre))
            gain = vg_ref[:, sl]
            vn = (vhat * gain).astype(BF)
            dsg = s_scr[:, sl]
            duv_ref[:, sl] = (dsg * y_scr[:, sl] * _gelu_grad(upre)).astype(BF)
            dy = dsg * u
            dyb = dy.astype(BF)
            dvn_parts = []
            for ci in range(n_chunk):
                rs = slice(ci * CHUNK, (ci + 1) * CHUNK)
                dvn_parts.append(jnp.dot(wt_ref[g], dyb[rs], preferred_element_type=F32))
                dw_ref[g] += lax.dot_general(dyb[rs], vn[rs], (((1,), (1,)), ((), ())), preferred_element_type=F32)
                db_ref[:, sl] += jnp.broadcast_to(jnp.sum(dy[rs], axis=-1, keepdims=True), (CHUNK, CH))
            dvn = dvn_parts[0] if n_chunk == 1 else jnp.concatenate(dvn_parts, axis=0)
            _acc_rows(dvg_ref.at[:, sl], dvn * vhat, first)
            dvh = dvn * gain
            dvg = r * (dvh - jnp.mean(dvh, axis=-1, keepdims=True)
                       - vhat * jnp.mean(dvh * vhat, axis=-1, keepdims=True))
            duv_ref[:, SGU_OUT + g * CH:SGU_OUT + (g + 1) * CH] = (dvg * _gelu_grad(vpre)).astype(BF)

    return _pcall(
        body, name=name, grid=(T // tm,),
        ins=[(dmixed, (tm, MLA_OUT), lambda i: (i, 0)), (dmixed, (tm, SGU_OUT), lambda i: (i, 1)),
             (a, (tm, MLA_OUT), lambda i: (i, 0)), (proj, (tm, SGU_OUT), lambda i: (i, 1)),
             (proj, (tm, SGU_OUT), lambda i: (i, 2)), (g_mla, (1, MLA_OUT), lambda i: (0, 0)),
             (g_sgu, (1, SGU_OUT), lambda i: (0, 0)), (v_gain, (1, SGU_OUT), lambda i: (0, 0)),
             (w_tril, (GROUPS, CHUNK, CHUNK), lambda i: (0, 0, 0)), (w_tril_t, (GROUPS, CHUNK, CHUNK), lambda i: (0, 0, 0)),
             (b_full, (CHUNK, SGU_OUT), lambda i: (0, 0))],
        outs=[((T, MLA_OUT), BF, (tm, MLA_OUT), lambda i: (i, 0)),
              ((T, 2 * SGU_OUT), BF, (tm, 2 * SGU_OUT), lambda i: (i, 0)),
              ((1, MLA_OUT), F32, (1, MLA_OUT), lambda i: (0, 0)), ((1, SGU_OUT), F32, (1, SGU_OUT), lambda i: (0, 0)),
              ((1, SGU_OUT), F32, (1, SGU_OUT), lambda i: (0, 0)),
              ((GROUPS, CHUNK, CHUNK), F32, (GROUPS, CHUNK, CHUNK), lambda i: (0, 0, 0)),
              ((CHUNK, SGU_OUT), F32, (CHUNK, SGU_OUT), lambda i: (0, 0))],
        scratch=[pltpu.VMEM((tm, SGU_OUT), F32), pltpu.VMEM((tm, SGU_OUT), F32)])


def _shift_down(z, n, row):
    return jnp.where(row >= n, pltpu.roll(z, n, 0), 0.0)


def _shift_up(z, n, row, T):
    return jnp.where(row < T - n, pltpu.roll(z, T - n, 0), 0.0)


def conv_fwd(proj, conv_w, *, name):
    T, D3 = proj.shape
    D = D3 // 3
    tn = _tile(D, 256)
    nj = D // tn

    def body(b_ref, c_ref, x_ref, w_ref, o_ref):
        row = lax.broadcasted_iota(jnp.int32, (T, tn), 0)
        z = c_ref[...] * x_ref[...]
        zc = w_ref[2:3, :] * z + w_ref[1:2, :] * _shift_down(z, 1, row) + w_ref[0:1, :] * _shift_down(z, 2, row)
        o_ref[...] = (b_ref[...] * zc).astype(BF)

    return _pcall(
        body, name=name, grid=(nj,),
        ins=[(proj, (T, tn), lambda j: (0, j)), (proj, (T, tn), lambda j: (0, nj + j)),
             (proj, (T, tn), lambda j: (0, 2 * nj + j)), (conv_w, (3, tn), lambda j: (0, j))],
        outs=[((T, D), BF, (T, tn), lambda j: (0, j))], semantics=("parallel",))[0]


def conv_bwd(dg, proj, conv_w, *, name):
    T, D3 = proj.shape
    D = D3 // 3
    tn = _tile(D, 256)
    nj = D // tn

    def body(dg_ref, b_ref, c_ref, x_ref, w_ref, dp_ref, dw_ref, dc_scr, dx_scr):
        part = pl.program_id(1)

        @pl.when(part == 0)
        def _():
            row = lax.broadcasted_iota(jnp.int32, (T, tn), 0)
            c, x = c_ref[...], x_ref[...]
            z = c * x
            z1 = _shift_down(z, 1, row)
            z2 = _shift_down(z, 2, row)
            dgv = dg_ref[...]
            zc = w_ref[2:3, :] * z + w_ref[1:2, :] * z1 + w_ref[0:1, :] * z2
            dp_ref[...] = (dgv * zc).astype(BF)
            dzc = dgv * b_ref[...]
            dw_ref[0:1, :] = jnp.sum(dzc * z2, axis=0, keepdims=True)
            dw_ref[1:2, :] = jnp.sum(dzc * z1, axis=0, keepdims=True)
            dw_ref[2:3, :] = jnp.sum(dzc * z, axis=0, keepdims=True)
            dz = (w_ref[2:3, :] * dzc + w_ref[1:2, :] * _shift_up(dzc, 1, row, T)
                  + w_ref[0:1, :] * _shift_up(dzc, 2, row, T))
            dc_scr[...] = (dz * x).astype(BF)
            dx_scr[...] = (dz * c).astype(BF)

        @pl.when(part == 1)
        def _():
            dp_ref[...] = dc_scr[...]

        @pl.when(part == 2)
        def _():
            dp_ref[...] = dx_scr[...]

    return _pcall(
        body, name=name, grid=(nj, 3),
        ins=[(dg, (T, tn), lambda j, p: (0, j)), (proj, (T, tn), lambda j, p: (0, j)),
             (proj, (T, tn), lambda j, p: (0, nj + j)), (proj, (T, tn), lambda j, p: (0, 2 * nj + j)),
             (conv_w, (3, tn), lambda j, p: (0, j))],
        outs=[((T, D3), BF, (T, tn), lambda j, p: (0, p * nj + j)), ((3, D), F32, (3, tn), lambda j, p: (0, j))],
        scratch=[pltpu.VMEM((T, tn), BF), pltpu.VMEM((T, tn), BF)], semantics=("parallel", "arbitrary"))


def loss_bwd(x, gain, target, *, name):
    T, D = x.shape
    tm = _tile(T, ROW_TILE, 8)

    def body(x_ref, g_ref, t_ref, dx_ref, dxb_ref, dg_ref, loss_ref):
        first = pl.program_id(0) == 0
        xv = x_ref[...]
        r = _rstd(xv)
        xh = xv * r
        gain_v = g_ref[...]
        err = xh * gain_v - t_ref[...]
        part = 0.5 * jnp.sum(jnp.mean(err * err, axis=-1, keepdims=True), axis=0, keepdims=True)
        _acc_rows(loss_ref, jnp.broadcast_to(part, (1, LANES)), first)
        dy = err * (1.0 / D)
        gdy = dy * gain_v
        dx = r * (gdy - xh * jnp.mean(gdy * xh, axis=-1, keepdims=True))
        dx_ref[...] = dx
        dxb_ref[...] = dx.astype(BF)
        _acc_rows(dg_ref, dy * xh, first)

    return _pcall(
        body, name=name, grid=(T // tm,),
        ins=[(x, (tm, D), lambda i: (i, 0)), (gain, (1, D), lambda i: (0, 0)), (target, (tm, D), lambda i: (i, 0))],
        outs=[((T, D), F32, (tm, D), lambda i: (i, 0)), ((T, D), BF, (tm, D), lambda i: (i, 0)),
              ((1, D), F32, (1, D), lambda i: (0, 0)), ((1, LANES), F32, (1, LANES), lambda i: (0, 0))])


def _adamw(g, w, m, v):
    m = ADAM_B1 * m + (1.0 - ADAM_B1) * g
    v = ADAM_B2 * v + (1.0 - ADAM_B2) * (g * g)
    m_hat = m / ADAM_C1
    v_hat = v / ADAM_C2
    delta = -ADAM_LR * (m_hat / (jnp.sqrt(v_hat) + ADAM_EPS) + ADAM_WD * w)
    return delta, m, v


def adam_flat(g, w, m, v, *, name):
    def body(g_ref, w_ref, m_ref, v_ref, d_ref, nm_ref, nv_ref):
        d, nm, nv = _adamw(g_ref[...], w_ref[...], m_ref[...], v_ref[...])
        d_ref[...] = d
        nm_ref[...] = nm
        nv_ref[...] = nv

    blk = g.shape
    zero = lambda: (0, 0)
    return _pcall(body, name=name, grid=(),
                  ins=[(t, blk, zero) for t in (g, w, m, v)],
                  outs=[(blk, F32, blk, zero)] * 3)


def _chip_slots():
    x, y, c = lax.axis_index("x"), lax.axis_index("y"), lax.axis_index("c")
    chips = [(1 - x, y), (x, 1 - y), (1 - x, 1 - y)]
    return x, y, c, chips


def reduce_adam(gs, a_buf, b_buf, w, m, v, layer, prev, *, name, deps=()):
    L, R, C = w.shape
    tr = _tile(R, 256, 8)
    x, y, c, _ = _chip_slots()
    idx = jnp.stack([4 * x + 2 * y + c, 2 * x + y]).astype(jnp.int32)
    n_prev = 0 if prev is None else 4

    def body(idx_ref, g_ref, a_ref, b0_ref, b1_ref, b2_ref, w_ref, m_ref, v_ref, *rest):
        outs = rest[n_prev:]
        g = ((((g_ref[...].astype(F32) + a_ref[...].astype(F32)) + b0_ref[...].astype(F32))
              + b1_ref[...].astype(F32)) + b2_ref[...].astype(F32))
        d, nm, nv = _adamw(g, w_ref[...], m_ref[...], v_ref[...])
        outs[0][...] = g
        outs[1][...] = d
        outs[2][...] = nm
        outs[3][...] = nv

    blk3 = (None, tr, C)
    ins = [(gs, blk3, lambda i, s: (s[0], i, 0)), (a_buf, blk3, lambda i, s: (s[1], i, 0)),
           (b_buf, blk3, lambda i, s: (0, i, 0)), (b_buf, blk3, lambda i, s: (1, i, 0)),
           (b_buf, blk3, lambda i, s: (2, i, 0)),
           (w, blk3, lambda i, s: (layer, i, 0)), (m, blk3, lambda i, s: (layer, i, 0)),
           (v, blk3, lambda i, s: (layer, i, 0))]
    aliases = {}
    if prev is not None:
        for o, p in enumerate(prev):
            ins.append((p, None, None))
            aliases[1 + 8 + o] = o
    outs = [((L, R, C), F32, blk3, lambda i, s: (layer, i, 0))] * 4
    return _pcall(body, name=name, grid=(R // tr,), ins=ins, outs=outs, prefetch=idx, aliases=aliases,
                  semantics=("parallel",), deps=deps)


def pair_sum(gs, a_buf, *, name):
    _, R, C = gs.shape
    tr = _tile(R, 256, 8)
    x, y, c, chips = _chip_slots()
    idx = jnp.stack([4 * cx + 2 * cy + c for cx, cy in chips] + [2 * cx + cy for cx, cy in chips]).astype(jnp.int32)

    def body(idx_ref, g_ref, a_ref, o_ref):
        o_ref[...] = (g_ref[...].astype(F32) + a_ref[...].astype(F32)).astype(BF)

    blk3 = (None, tr, C)
    return _pcall(body, name=name, grid=(3, R // tr),
                  ins=[(gs, blk3, lambda j, i, s: (s[j], i, 0)), (a_buf, blk3, lambda j, i, s: (s[3 + j], i, 0))],
                  outs=[((3, R, C), BF, blk3, lambda j, i, s: (j, i, 0))], prefetch=idx,
                  semantics=("parallel", "parallel"))[0]


def sum_rows8(gathered, rows, *, name):
    W = gathered.shape[1]

    def body(g_ref, o_ref):
        acc = g_ref[0:rows, :]
        for d in range(1, N_DEV):
            acc = acc + g_ref[d * rows:(d + 1) * rows, :]
        o_ref[...] = acc

    return _pcall(body, name=name, grid=(), ins=[(gathered, gathered.shape, lambda: (0, 0))],
                  outs=[((rows, W), F32, (rows, W), lambda: (0, 0))])[0]


HBM_SPEC = pl.BlockSpec(memory_space=pltpu.HBM)
SEM_SPEC = pl.BlockSpec(memory_space=pltpu.SEMAPHORE)
ANY_SPEC = pl.BlockSpec(memory_space=pl.ANY)
DATAFLOW = pltpu.SideEffectType.DATAFLOW_SIDE_EFFECTING


def _in_hbm(v):
    return pltpu.with_memory_space_constraint(v, pltpu.HBM)


def _slot(p):
    return 4 * p[0] + 2 * p[1] + p[2]


def _gather_peers():
    x, y, c, chips = _chip_slots()
    return (x, y, c), [(x, y, 1 - c)] + [(*chip, c) for chip in chips]


def gather_start(groups, after, *, name):
    flat = [s for g in groups for s in g]
    n, n_g = len(flat), len(groups)
    where = [(gi, ti) for gi, g in enumerate(groups) for ti in range(len(g))]

    def body(*refs):
        src, land = refs[:n], refs[n:2 * n]
        sems = refs[2 * n + 1:2 * n + 1 + 2 * n_g]
        me, peers = _gather_peers()
        for t in range(n):
            gi, ti = where[t]
            for k, to in enumerate(peers):
                pltpu.make_async_remote_copy(
                    src_ref=src[t], dst_ref=land[t].at[_slot(me)], send_sem=sems[2 * gi].at[4 * ti + k],
                    recv_sem=sems[2 * gi + 1].at[4 * ti + k], device_id=to, device_id_type=MESH).start()

    out_shape = []
    for g in groups:
        out_shape += [pltpu.SemaphoreType.DMA((4 * len(g),)), pltpu.SemaphoreType.DMA((4 * len(g),))]
    out_shape += [pltpu.HBM(s.shape, s.dtype) for s in flat]
    out_shape += [pltpu.HBM((N_DEV,) + s.shape, s.dtype) for s in flat]
    aliases = {t: 2 * n_g + t for t in range(n)}
    aliases.update({n + t: 2 * n_g + n + t for t in range(n)})
    res = pl.pallas_call(
        body, name=name, out_shape=out_shape, in_specs=[HBM_SPEC] * (2 * n) + [ANY_SPEC],
        out_specs=[SEM_SPEC] * (2 * n_g) + [HBM_SPEC] * (2 * n), input_output_aliases=aliases,
        compiler_params=pltpu.CompilerParams(has_side_effects=DATAFLOW),
    )(*[_in_hbm(s) for s in flat], *[_in_hbm(lax.empty((N_DEV,) + s.shape, s.dtype)) for s in flat], after)
    out, off = [], 0
    for gi, g in enumerate(groups):
        k = len(g)
        out.append((res[2 * gi], res[2 * gi + 1], res[2 * n_g + off:2 * n_g + off + k],
                    res[2 * n_g + n + off:2 * n_g + n + off + k]))
        off += k
    return out


def gather_wait(started, after, *, name):
    send_sems, recv_sems, srcs, lands = started
    n = len(srcs)

    def body(*refs):
        src, land = refs[:n], refs[n:2 * n]
        send, recv = refs[2 * n], refs[2 * n + 1]
        _, peers = _gather_peers()
        for t in range(n):
            for k, frm in enumerate(peers):
                cp = pltpu.make_async_remote_copy(
                    src_ref=src[t], dst_ref=land[t].at[_slot(frm)], send_sem=send.at[4 * t + k],
                    recv_sem=recv.at[4 * t + k],
                    device_id=frm, device_id_type=MESH)
                cp.wait_send()
                cp.wait_recv()

    res = pl.pallas_call(
        body, name=name,
        out_shape=[pltpu.HBM(s.shape, s.dtype) for s in srcs] + [pltpu.HBM(l.shape, l.dtype) for l in lands],
        in_specs=[HBM_SPEC] * (2 * n) + [SEM_SPEC, SEM_SPEC, ANY_SPEC], out_specs=[HBM_SPEC] * (2 * n),
        input_output_aliases={t: t for t in range(2 * n)},
        compiler_params=pltpu.CompilerParams(has_side_effects=DATAFLOW),
    )(*srcs, *lands, send_sems, recv_sems, after)
    return res[:n], res[n:]


def place_own(src, land, *, name):
    R, C = src.shape
    tr = _tile(R, 512, 16)
    x, y, c, _ = _chip_slots()
    idx = jnp.stack([4 * x + 2 * y + c]).astype(jnp.int32)

    def body(idx_ref, s_ref, land_ref, o_ref):
        o_ref[...] = s_ref[...]

    return _pcall(body, name=name, grid=(R // tr,),
                  ins=[(src, (tr, C), lambda i, s: (i, 0)), (land, None, None)],
                  outs=[(land.shape, land.dtype, (None, tr, C), lambda i, s: (s[0], i, 0))],
                  prefetch=idx, aliases={2: 0}, semantics=("parallel",))[0]


def gather_finish(srcs, lands, *, name):
    n = len(srcs)

    def body(*refs):
        land = refs[n:2 * n]
        send_sems, recv_sems = refs[2 * n:]
        x, y, c, chips = _chip_slots()
        me, sibling = (x, y, c), (x, y, 1 - c)

        def copy(t, j, block, to):
            return pltpu.make_async_remote_copy(
                src_ref=land[t].at[_slot(block)], dst_ref=land[t].at[_slot(block)], send_sem=send_sems.at[t, j],
                recv_sem=recv_sems.at[t, j], device_id=to, device_id_type=MESH)

        sends = [copy(t, j, (*chip, c), sibling) for t in range(n) for j, chip in enumerate(chips)]
        for cp in sends:
            cp.start()
        for t in range(n):
            for j, chip in enumerate(chips):
                copy(t, j, (*chip, 1 - c), me).wait_recv()
        for cp in sends:
            cp.wait_send()

    passed = pl.pallas_call(
        body, name=name, out_shape=[jax.ShapeDtypeStruct(l.shape, l.dtype) for l in lands],
        in_specs=[ANY_SPEC] * n, out_specs=[ANY_SPEC] * n,
        input_output_aliases={t: t for t in range(n)},
        scratch_shapes=[pltpu.SemaphoreType.DMA((n, 3)), pltpu.SemaphoreType.DMA((n, 3))],
    )(*lands)
    return [place_own(s, l, name=f"{name}_own{t}") for t, (s, l) in enumerate(zip(srcs, passed))]


def chips_start(pairs, *, name):
    n = len(pairs)

    def body(*refs):
        src, land = refs[:n], refs[n:2 * n]
        send, recv = refs[2 * n], refs[2 * n + 1]
        token = refs[-1]
        x, y, c, chips = _chip_slots()
        for t in range(n):
            for j, chip in enumerate(chips):
                pltpu.make_async_remote_copy(
                    src_ref=src[t].at[j], dst_ref=land[t].at[j], send_sem=send.at[3 * t + j],
                    recv_sem=recv.at[3 * t + j], device_id=(*chip, c), device_id_type=MESH).start()
        token[...] = jnp.zeros_like(token)

    res = pl.pallas_call(
        body, name=name,
        out_shape=[pltpu.SemaphoreType.DMA((3 * n,)), pltpu.SemaphoreType.DMA((3 * n,))]
        + [pltpu.HBM(p.shape, p.dtype) for p in pairs] * 2 + [jax.ShapeDtypeStruct((8, LANES), F32)],
        in_specs=[HBM_SPEC] * (2 * n),
        out_specs=[SEM_SPEC, SEM_SPEC] + [HBM_SPEC] * (2 * n) + [pl.BlockSpec(memory_space=pltpu.VMEM)],
        input_output_aliases={t: 2 + t for t in range(2 * n)},
        compiler_params=pltpu.CompilerParams(has_side_effects=DATAFLOW),
    )(*[_in_hbm(p) for p in pairs], *[_in_hbm(lax.empty(p.shape, p.dtype)) for p in pairs])
    return res[0], res[1], res[2:2 + n], res[2 + n:2 + 2 * n], res[-1]


def chips_wait(started, after, *, name):
    send_sems, recv_sems, srcs, lands, _ = started
    n = len(srcs)

    def body(*refs):
        src, land = refs[:n], refs[n:2 * n]
        send, recv = refs[2 * n], refs[2 * n + 1]
        x, y, c, chips = _chip_slots()
        for t in range(n):
            for j, chip in enumerate(chips):
                cp = pltpu.make_async_remote_copy(
                    src_ref=src[t].at[j], dst_ref=land[t].at[j], send_sem=send.at[3 * t + j],
                    recv_sem=recv.at[3 * t + j], device_id=(*chip, c), device_id_type=MESH)
                cp.wait_send()
                cp.wait_recv()

    res = pl.pallas_call(
        body, name=name, out_shape=[pltpu.HBM(s.shape, s.dtype) for s in srcs] * 2,
        in_specs=[HBM_SPEC] * (2 * n) + [SEM_SPEC, SEM_SPEC, ANY_SPEC], out_specs=[HBM_SPEC] * (2 * n),
        input_output_aliases={t: t for t in range(2 * n)},
        compiler_params=pltpu.CompilerParams(has_side_effects=DATAFLOW),
    )(*srcs, *lands, send_sems, recv_sems, after)
    return res[n:]


def _sibling_copies(src, land, send, recv, n):
    x, y, c, _ = _chip_slots()
    return [pltpu.make_async_remote_copy(
        src_ref=src[t].at[4 * (q // 2) + 2 * (q % 2) + (1 - c)], dst_ref=land[t].at[q], send_sem=send.at[4 * t + q],
        recv_sem=recv.at[4 * t + q], device_id=(x, y, 1 - c), device_id_type=MESH)
        for t in range(n) for q in range(4)]


def sibling_start(gs, *, name):
    n = len(gs)

    def body(*refs):
        for cp in _sibling_copies(refs[:n], refs[n:2 * n], refs[2 * n], refs[2 * n + 1], n):
            cp.start()
        refs[-1][...] = jnp.zeros_like(refs[-1])

    lands = [lax.empty((4,) + g.shape[1:], g.dtype) for g in gs]
    res = pl.pallas_call(
        body, name=name,
        out_shape=[pltpu.SemaphoreType.DMA((4 * n,)), pltpu.SemaphoreType.DMA((4 * n,))]
        + [pltpu.HBM(g.shape, g.dtype) for g in gs] + [pltpu.HBM(l.shape, l.dtype) for l in lands]
        + [jax.ShapeDtypeStruct((8, LANES), F32)],
        in_specs=[HBM_SPEC] * (2 * n),
        out_specs=[SEM_SPEC, SEM_SPEC] + [HBM_SPEC] * (2 * n) + [pl.BlockSpec(memory_space=pltpu.VMEM)],
        input_output_aliases={t: 2 + t for t in range(2 * n)},
        compiler_params=pltpu.CompilerParams(has_side_effects=DATAFLOW),
    )(*[_in_hbm(g) for g in gs], *[_in_hbm(l) for l in lands])
    return res[0], res[1], res[2:2 + n], res[2 + n:2 + 2 * n], res[-1]


def sibling_wait(started, after, *, name):
    send_sems, recv_sems, srcs, lands, _ = started
    n = len(srcs)

    def body(*refs):
        for cp in _sibling_copies(refs[:n], refs[n:2 * n], refs[2 * n], refs[2 * n + 1], n):
            cp.wait_send()
            cp.wait_recv()

    res = pl.pallas_call(
        body, name=name,
        out_shape=[pltpu.HBM(s.shape, s.dtype) for s in srcs] + [pltpu.HBM(l.shape, l.dtype) for l in lands],
        in_specs=[HBM_SPEC] * (2 * n) + [SEM_SPEC, SEM_SPEC, ANY_SPEC], out_specs=[HBM_SPEC] * (2 * n),
        input_output_aliases={t: t for t in range(2 * n)},
        compiler_params=pltpu.CompilerParams(has_side_effects=DATAFLOW),
    )(*srcs, *lands, send_sems, recv_sems, after)
    return res[:n], res[n:]


def all_gather_vmem(x_shard, *, name, after=None):
    m_per, n = x_shard.shape
    n_after = 0 if after is None else 1

    def body(x_ref, *rest):
        out_ref, send_sems, recv_sems, local_sem = rest[n_after:]
        x, y, c, chips = _chip_slots()
        me, sibling = (x, y, c), (x, y, 1 - c)

        def rows(px, py, pc):
            return out_ref.at[pl.ds((4 * px + 2 * py + pc) * m_per, m_per), :]

        def copy(k, block, to, src=None):
            return pltpu.make_async_remote_copy(
                src_ref=rows(*block) if src is None else src, dst_ref=rows(*block),
                send_sem=send_sems.at[k], recv_sem=recv_sems.at[k], device_id=to, device_id_type=MESH)

        mine = pltpu.make_async_copy(x_ref, rows(*me), local_sem)
        mine.start()
        first = [copy(0, me, sibling, src=x_ref)]
        first += [copy(1 + j, me, (*chip, c), src=x_ref) for j, chip in enumerate(chips)]
        for cp in first:
            cp.start()
        passed = [copy(4 + j, (*chip, c), sibling) for j, chip in enumerate(chips)]
        for j, chip in enumerate(chips):
            copy(1 + j, (*chip, c), me).wait_recv()
            passed[j].start()
        copy(0, sibling, me).wait_recv()
        for j, chip in enumerate(chips):
            copy(4 + j, (*chip, 1 - c), me).wait_recv()
        for cp in first + passed:
            cp.wait_send()
        mine.wait()

    vmem = pl.BlockSpec(memory_space=pltpu.VMEM)
    return pl.pallas_call(
        body, name=name, out_shape=jax.ShapeDtypeStruct((N_DEV * m_per, n), x_shard.dtype),
        in_specs=[vmem] + [ANY_SPEC] * n_after, out_specs=vmem,
        scratch_shapes=[pltpu.SemaphoreType.DMA((7,)), pltpu.SemaphoreType.DMA((7,)), pltpu.SemaphoreType.DMA],
        compiler_params=pltpu.CompilerParams(vmem_limit_bytes=int(min(
            VMEM_LIMIT_CAP, 2 * (N_DEV + 1) * m_per * n * x_shard.dtype.itemsize + 16 * 2 ** 20))),
    )(x_shard, *([] if after is None else [after]))


def _rope_slab(cols):
    z = jnp.zeros(cols.shape[:-1] + (HALF_ROPE,), cols.dtype)
    return jnp.concatenate([cols[..., :HALF_ROPE], z, cols[..., HALF_ROPE:], z], axis=-1)


def _rope_unslab(slab):
    return jnp.concatenate([slab[..., :HALF_ROPE], slab[..., 2 * HALF_ROPE:3 * HALF_ROPE]], axis=-1)


def _pack_w_in(w_g):
    s, d, c = w_g.shape
    w = jnp.transpose(w_g, (1, 0, 2)).reshape(d, s * c)
    c1, c2, c3 = Q_LORA, Q_LORA + KV_LORA, Q_LORA + KV_LORA + QK_ROPE
    return jnp.concatenate([w[:, :c2], w[:, c3:], _rope_slab(w[:, c2:c3])], axis=-1)


def _unpack_w_in_grad(dw):
    d = dw.shape[0]
    c2 = Q_LORA + KV_LORA
    uv = 2 * SGU_OUT
    g = jnp.concatenate([dw[:, :c2], _rope_unslab(dw[:, c2 + uv:]), dw[:, c2:c2 + uv]], axis=-1)
    return jnp.transpose(g.reshape(d, N_DEV, g.shape[1] // N_DEV), (1, 0, 2))


def _rope_tables(positions):
    inv_freq = ROPE_BASE ** (-jnp.arange(0, QK_ROPE, 2, dtype=F32) / QK_ROPE)
    ang = positions.astype(F32)[:, None] * inv_freq
    cos, sin = jnp.cos(ang), jnp.sin(ang)
    z = jnp.zeros_like(cos)
    return jnp.concatenate([cos, z, cos, z], axis=-1), jnp.concatenate([-sin, z, sin, z], axis=-1)


def _mlp_up(x, gain, w1, tag):
    hn = rms_fwd(x, gain, name=f"mlp{tag}_norm")

    def act_epi(acc):
        a = jnp.maximum(acc, 0.0)
        return a, a * a

    T = x.shape[0]
    F = w1.shape[0] * w1.shape[2]
    a, act = mm(hn, w1, name=f"mlp{tag}_up", outs=[((T, F), BF, None), ((T, F), BF, None)], epi=act_epi)
    return hn, a, act


def _mlp_down(x, act, w2, tag):
    bm = _tile(x.shape[0], MM_TILE)
    bn = _tile(x.shape[1], MM_TILE)
    return mm(act, w2, name=f"mlp{tag}_down", out=(x.shape, F32), bm=bm, bn=bn,
              epi=lambda acc, r: (acc + r[...],), epi_ins=[(x, (bm, bn), lambda i, j, k: (i, j))])


def _mlp_bwd_weights(w1, w2, saved, dxb, tag):
    hn, a, act = saved
    T, D = dxb.shape
    F = a.shape[1]
    bm = _tile(T, MM_TILE)
    bn = _tile(F, min(MM_TILE, w1.shape[2]))
    dhid = mm(dxb, w2, tb=True, name=f"mlp{tag}_dhid", out=((T, F), BF), bm=bm, bn=bn,
              epi=lambda acc, a_ref: (2.0 * a_ref[...].astype(F32) * acc,),
              epi_ins=[(a, (bm, bn), lambda i, j, k: (i, j))])
    dw2 = mm(act, dxb, ta=True, name=f"mlp{tag}_dw2", out=((F, D), BF))
    dw1 = mm(hn, dhid, ta=True, name=f"mlp{tag}_dw1", out=(w1.shape, BF))
    return dhid, dw1, dw2.reshape(N_DEV, F // N_DEV, D)


def _reduce_begin(grads, tag):
    return sibling_start(grads, name=f"reduce_sibling_start_{tag}")


def _reduce_continue(sib, after, tag):
    grads, a_bufs = sibling_wait(sib, after, name=f"reduce_sibling_wait_{tag}")
    pairs = [pair_sum(g, a, name=f"pair_sum_{tag}{t}") for t, (g, a) in enumerate(zip(grads, a_bufs))]
    return grads, a_bufs, chips_start(pairs, name=f"reduce_chips_start_{tag}")


def _mlp_bwd_input(x_in, gain, w1, dhid, dx, tag, sib):
    dhn = mm(dhid, w1, tb=True, name=f"mlp{tag}_dhn", out=(x_in.shape, F32), deps=[sib[-1]])
    reduce_state = _reduce_continue(sib, dhn, f"r_mlp{tag}")
    return rms_bwd(x_in, gain, dhn, dres=dx, name=f"mlp{tag}_norm_bwd", deps=[reduce_state[2][-1]]), reduce_state


def kernel(x, positions, e_norm_mix, e_w_in, e_q_norm, e_w_uq, e_kv_norm, e_w_ukv, e_v_norm, e_sgu_w, e_sgu_b, e_mla_out_norm, e_sgu_out_norm, e_w_out, o_norm_mix, o_w_in, o_conv_w, o_w_out, mlp_norm, mlp_w1, mlp_w2, final_norm, loss_target, m_e_norm_mix, m_e_w_in, m_e_q_norm, m_e_w_uq, m_e_kv_norm, m_e_w_ukv, m_e_v_norm, m_e_sgu_w, m_e_sgu_b, m_e_mla_out_norm, m_e_sgu_out_norm, m_e_w_out, m_o_norm_mix, m_o_w_in, m_o_conv_w, m_o_w_out, m_mlp_norm, m_mlp_w1, m_mlp_w2, m_final_norm, v_e_norm_mix, v_e_w_in, v_e_q_norm, v_e_w_uq, v_e_kv_norm, v_e_w_ukv, v_e_v_norm, v_e_sgu_w, v_e_sgu_b, v_e_mla_out_norm, v_e_sgu_out_norm, v_e_w_out, v_o_norm_mix, v_o_w_in, v_o_conv_w, v_o_w_out, v_mlp_norm, v_mlp_w1, v_mlp_w2, v_final_norm):
    T, D = x.shape[1], x.shape[2]
    d_shard = o_norm_mix.shape[1]
    x0 = x[0]
    target = loss_target[0]
    me = 4 * lax.axis_index("x") + 2 * lax.axis_index("y") + lax.axis_index("c")

    bf = lambda s: s.astype(BF)
    gather_groups = [[bf(e_w_in[0]), bf(e_w_uq[0]), bf(e_w_ukv[0])], [bf(e_w_out[0]), bf(mlp_w1[0])],
                     [bf(mlp_w2[0]), bf(o_w_in[0])], [bf(o_w_out[0]), bf(mlp_w1[1])], [bf(mlp_w2[1])]]
    small_rows = jnp.concatenate([o_norm_mix, o_conv_w[0], jnp.zeros((4, d_shard), F32)], axis=0)
    small_flat = all_gather_vmem(small_rows, name="gather_small")
    started = gather_start(gather_groups, small_flat, name="gather_start")

    def gathered(gi, after):
        srcs, lands = gather_wait(started[gi], after, name=f"gather_wait{gi}")
        return gather_finish(srcs, lands, name=f"gather_finish{gi}")

    small_g = small_flat.reshape(N_DEV, 8, d_shard)
    o_norm_full = small_g[:, 0, :].reshape(1, D)
    conv_w_full = jnp.transpose(small_g[:, 1:4, :], (1, 0, 2)).reshape(3, D)
    w_tril = jnp.tril(e_sgu_w[0])
    w_tril_b = w_tril.astype(BF)
    w_tril_tb = jnp.swapaxes(w_tril, 1, 2).astype(BF)
    b_full = jnp.repeat(e_sgu_b[0].T, CH, axis=1)
    v_gain = e_v_norm[0].reshape(1, SGU_OUT)
    cos_t, sin_t = _rope_tables(positions[0])
    mlp_gain = [mlp_norm[0:1], mlp_norm[1:2]]
    final_gain = final_norm.reshape(1, D)

    h0 = rms_fwd(x0, e_norm_mix, name="e_norm")
    g_w_in, g_w_uq, w_ukv = gathered(0, h0)
    w_in_e = _pack_w_in(g_w_in)
    w_uq = jnp.concatenate([g_w_uq[..., :QK_NOPE], _rope_slab(g_w_uq[..., QK_NOPE:])], axis=-1)
    proj = mm(h0, w_in_e, name="e_in", out=((T, w_in_e.shape[1]), F32), bn=_tile(w_in_e.shape[1], 640))
    qn, kvn, krope = mla_prep(proj, e_q_norm, e_kv_norm, cos_t, sin_t, name="mla_prep")
    bm = _tile(T, MM_TILE)

    def q_epi(acc, cos_ref, sin_ref):
        return (jnp.concatenate([acc[:, :QK_NOPE], _rope_fwd(acc[:, QK_NOPE:], cos_ref[...], sin_ref[...])], axis=-1),)

    q = mm(qn, w_uq, name="mla_q", out=((T, HEADS * HEAD_PAD), BF), bm=bm, bn=HEAD_PAD, epi=q_epi,
           epi_ins=[(cos_t, (bm, LANES), lambda i, j, k: (i, 0)), (sin_t, (bm, LANES), lambda i, j, k: (i, 0))])

    def kv_epi(acc, kr_ref):
        return jnp.concatenate([acc[:, :QK_NOPE].astype(BF), kr_ref[...]], axis=-1), acc[:, QK_NOPE:]

    k, v = mm(kvn, w_ukv, name="mla_kv", bm=bm, bn=HEAD_PAD, epi=kv_epi,
              outs=[((T, HEADS * HEAD_PAD), BF, HEAD_PAD), ((T, MLA_OUT), BF, V_HEAD)],
              epi_ins=[(krope, (bm, LANES), lambda i, j, k: (i, 0))])
    attn = attn_fwd(q, k, v, name="attn_fwd")
    mixed = mix_fwd(attn, proj, e_mla_out_norm, e_sgu_out_norm, v_gain, w_tril_b, b_full, name="mix_fwd")
    bn = _tile(D, MM_TILE)
    g_w_out_e, w1_0 = gathered(1, mixed)
    w_out_e = g_w_out_e.reshape(-1, D)
    x1 = mm(mixed, w_out_e, name="e_out", out=((T, D), F32), bm=bm, bn=bn,
            epi=lambda acc, r: (acc + r[...],), epi_ins=[(x0, (bm, bn), lambda i, j, k: (i, j))])
    hn0, a0, act0 = _mlp_up(x1, mlp_gain[0], w1_0, 0)
    g_w2_0, g_w_in_o = gathered(2, act0)
    w2_0 = g_w2_0.reshape(-1, D)
    x2 = _mlp_down(x1, act0, w2_0, 0)
    ho = rms_fwd(x2, o_norm_full, name="o_norm")
    proj_o = mm(ho, g_w_in_o, name="o_in", out=((T, 3 * D), F32))
    gated = conv_fwd(proj_o, conv_w_full, name="conv_fwd")
    g_w_out_o, w1_1 = gathered(3, gated)
    w_out_o = g_w_out_o.reshape(-1, D)
    x3 = mm(gated, w_out_o, name="o_out", out=((T, D), F32), bm=bm, bn=bn,
            epi=lambda acc, r: (acc + r[...],), epi_ins=[(x2, (bm, bn), lambda i, j, k: (i, j))])
    hn1, a1, act1 = _mlp_up(x3, mlp_gain[1], w1_1, 1)
    (g_w2_1,) = gathered(4, act1)
    w2_1 = g_w2_1.reshape(-1, D)
    x4 = _mlp_down(x3, act1, w2_1, 1)
    w1, w2 = [w1_0, w1_1], [w2_0, w2_1]
    mlp0_saved, mlp1_saved = (hn0, a0, act0), (hn1, a1, act1)

    dx4, dx4b, d_final, loss_part = loss_bwd(x4, final_gain, target, name="loss_bwd")
    loss = lax.psum(loss_part[0, 0], AXES)

    dhid1, dw1_1, dw2_1 = _mlp_bwd_weights(w1[1], w2[1], mlp1_saved, dx4b, 1)
    sib_r0 = _reduce_begin([dw1_1, dw2_1], "r0")
    (dx3, dx3b, d_mlp1), (grads_r0, a_r0, st_r0) = _mlp_bwd_input(x3, mlp_gain[1], w1[1], dhid1, dx4, 1, sib_r0)

    dgated = mm(dx3b, w_out_o, tb=True, name="o_out_dx", out=((T, D), F32))
    dw_out_o = mm(gated, dx3b, ta=True, name="o_out_dw", out=((D, D), BF))
    dproj_o, dconv_full = conv_bwd(dgated, proj_o, conv_w_full, name="conv_bwd")
    dw_in_o = mm(ho, dproj_o, ta=True, name="o_in_dw", out=(g_w_in_o.shape, BF))
    sib_r1 = _reduce_begin([dw_out_o.reshape(g_w_out_o.shape), dw_in_o], "r1")
    dho = mm(dproj_o, g_w_in_o, tb=True, name="o_in_dx", out=((T, D), F32), deps=[sib_r1[-1]])
    grads_r1, a_r1, st_r1 = _reduce_continue(sib_r1, dho, "r1")
    dx2, dx2b, d_onorm_full = rms_bwd(x2, o_norm_full, dho, dres=dx3, name="o_norm_bwd", deps=[st_r1[-1]])

    dhid0, dw1_0, dw2_0 = _mlp_bwd_weights(w1[0], w2[0], mlp0_saved, dx2b, 0)
    sib_r2 = _reduce_begin([dw1_0, dw2_0], "r2")
    (dx1, dx1b, d_mlp0), (grads_r2, a_r2, st_r2) = _mlp_bwd_input(x1, mlp_gain[0], w1[0], dhid0, dx2, 0, sib_r2)
    b_r0 = chips_wait(st_r0, dx1b, name="reduce_chips_wait_r0")

    dmixed = mm(dx1b, w_out_e, tb=True, name="e_out_dx", out=((T, MLA_OUT + SGU_OUT), F32))
    dw_out_e = mm(mixed, dx1b, ta=True, name="e_out_dw", out=(w_out_e.shape, BF))
    (dattn, duv, d_mla_out, d_sgu_out, d_vgain, d_sgu_w, d_b_full) = mix_bwd(
        dmixed, attn, proj, e_mla_out_norm, e_sgu_out_norm, v_gain, w_tril_b, w_tril_tb, b_full, name="mix_bwd")
    b_r1 = chips_wait(st_r1, dattn, name="reduce_chips_wait_r1")
    dq, dk, dv = attn_bwd(q, k, v, dattn, name="attn_bwd")
    dq_lin, dkv_lin, dkr = mla_bwd_prep(dq, dk, dv, cos_t, sin_t, name="mla_bwd_prep")
    dw_uq_pad = mm(qn, dq_lin, ta=True, name="mla_q_dw", out=(w_uq.shape, F32))
    dqn = mm(dq_lin, w_uq, tb=True, name="mla_q_dx", out=((T, Q_LORA), F32))
    dw_ukv = mm(kvn, dkv_lin, ta=True, name="mla_kv_dw", out=(w_ukv.shape, BF))
    dkvn = mm(dkv_lin, w_ukv, tb=True, name="mla_kv_dx", out=((T, KV_LORA), F32))
    dcq, d_qnorm = rms_bwd(proj, e_q_norm, dqn, col_block=0, want_f32=False, name="q_norm_bwd")
    dckv, d_kvnorm = rms_bwd(proj, e_kv_norm, dkvn, col_block=1, want_f32=False, name="kv_norm_bwd")
    dproj = jnp.concatenate([dcq, dckv, duv, dkr], axis=-1)
    dw_in_e_pad = mm(h0, dproj, ta=True, name="e_in_dw", out=(w_in_e.shape, F32), bn=_tile(w_in_e.shape[1], 640))
    dw_in_e = _unpack_w_in_grad(dw_in_e_pad).astype(BF)
    dw_uq = jnp.concatenate([dw_uq_pad[..., :QK_NOPE], _rope_unslab(dw_uq_pad[..., QK_NOPE:])], axis=-1).astype(BF)
    sib_r3 = _reduce_begin([dw_out_e.reshape(g_w_out_e.shape), dw_uq, dw_ukv, dw_in_e], "r3")
    dh0 = mm(dproj, w_in_e, tb=True, name="e_in_dx", out=((T, D), F32), deps=[sib_r3[-1]])
    grads_r3, a_r3, st_r3 = _reduce_continue(sib_r3, dh0, "r3")
    tok_r3 = st_r3[-1]
    grad_x, d_enorm = rms_bwd(x0, e_norm_mix, dh0, dres=dx1, want_bf=False, name="e_norm_bwd", deps=[tok_r3])
    b_r2 = chips_wait(st_r2, grad_x, name="reduce_chips_wait_r2")

    def finish(grads, a_bufs, b_bufs, t, w, m, v, layer=0, prev=None, tag="", deps=()):
        return reduce_adam(grads[t], a_bufs[t], b_bufs[t], w, m, v, layer, prev, name=f"adam_{tag}", deps=deps)

    r_w1 = finish(grads_r0, a_r0, b_r0, 0, mlp_w1, m_mlp_w1, v_mlp_w1, 1, None, tag="w1_l1")
    r_w2 = finish(grads_r0, a_r0, b_r0, 1, mlp_w2, m_mlp_w2, v_mlp_w2, 1, None, tag="w2_l1")
    r_w_out_o = finish(grads_r1, a_r1, b_r1, 0, o_w_out, m_o_w_out, v_o_w_out, tag="o_w_out")
    r_w_in_o = finish(grads_r1, a_r1, b_r1, 1, o_w_in, m_o_w_in, v_o_w_in, tag="o_w_in")
    r_w1 = finish(grads_r2, a_r2, b_r2, 0, mlp_w1, m_mlp_w1, v_mlp_w1, 0, r_w1, tag="w1_l0", deps=[tok_r3])
    r_w2 = finish(grads_r2, a_r2, b_r2, 1, mlp_w2, m_mlp_w2, v_mlp_w2, 0, r_w2, tag="w2_l0", deps=[r_w1[1]])
    b_r3 = chips_wait(st_r3, r_w2[1], name="reduce_chips_wait_r3")
    r_w_out_e = finish(grads_r3, a_r3, b_r3, 0, e_w_out, m_e_w_out, v_e_w_out, tag="e_w_out")
    r_w_uq = finish(grads_r3, a_r3, b_r3, 1, e_w_uq, m_e_w_uq, v_e_w_uq, tag="e_w_uq")
    r_w_ukv = finish(grads_r3, a_r3, b_r3, 2, e_w_ukv, m_e_w_ukv, v_e_w_ukv, tag="e_w_ukv")
    r_w_in = finish(grads_r3, a_r3, b_r3, 3, e_w_in, m_e_w_in, v_e_w_in, tag="e_w_in")

    d_sgu_b = jnp.transpose(d_b_full[:, ::CH])
    d_sgu_w_tril = jnp.tril(d_sgu_w)
    rep = [("e_norm_mix", e_norm_mix, m_e_norm_mix, v_e_norm_mix, d_enorm),
           ("e_q_norm", e_q_norm, m_e_q_norm, v_e_q_norm, d_qnorm),
           ("e_kv_norm", e_kv_norm, m_e_kv_norm, v_e_kv_norm, d_kvnorm),
           ("e_v_norm", e_v_norm, m_e_v_norm, v_e_v_norm, d_vgain),
           ("e_sgu_w", e_sgu_w, m_e_sgu_w, v_e_sgu_w, d_sgu_w_tril),
           ("e_sgu_b", e_sgu_b, m_e_sgu_b, v_e_sgu_b, d_sgu_b),
           ("e_mla_out_norm", e_mla_out_norm, m_e_mla_out_norm, v_e_mla_out_norm, d_mla_out),
           ("e_sgu_out_norm", e_sgu_out_norm, m_e_sgu_out_norm, v_e_sgu_out_norm, d_sgu_out),
           ("mlp_norm", mlp_norm, m_mlp_norm, v_mlp_norm, jnp.concatenate([d_mlp0, d_mlp1], axis=0)),
           ("final_norm", final_norm, m_final_norm, v_final_norm, d_final)]
    sizes = [int(np.prod(r[1].shape)) for r in rep]
    n_rep = sum(sizes)
    n_all = n_rep + 4 * D
    width = -(-n_all // (8 * LANES)) * LANES
    pad = 8 * width - n_all
    flat = jnp.concatenate([r[4].reshape(-1) for r in rep]
                           + [d_onorm_full.reshape(-1), dconv_full.reshape(-1), jnp.zeros((pad,), F32)])
    summed = sum_rows8(all_gather_vmem(flat.reshape(8, width), name="gather_small_grads", after=b_r3[0]), 8,
                       name="sum_small_grads").reshape(-1)

    def pack_rep(i):
        return jnp.concatenate([r[i].reshape(-1) for r in rep]).reshape(n_rep // LANES, LANES)

    g_rep = summed[:n_rep].reshape(n_rep // LANES, LANES)
    d_rep, nm_rep, nv_rep = adam_flat(g_rep, pack_rep(1), pack_rep(2), pack_rep(3), name="adam_replicated")

    def unpack_rep(flat2d):
        out, off = {}, 0
        f = flat2d.reshape(-1)
        for r, n in zip(rep, sizes):
            out[r[0]] = f[off:off + n].reshape(r[1].shape)
            off += n
        return out

    small = {"grad": unpack_rep(g_rep), "delta": unpack_rep(d_rep), "new_m": unpack_rep(nm_rep),
             "new_v": unpack_rep(nv_rep)}
    g_onorm = lax.dynamic_slice(summed[n_rep:n_rep + D].reshape(1, D), (0, me * d_shard), (1, d_shard))
    g_conv = lax.dynamic_slice(summed[n_rep + D:n_rep + 4 * D].reshape(3, D), (0, me * d_shard), (3, d_shard))

    def pack_sharded(norm_part, conv_part):
        return jnp.concatenate([norm_part, conv_part, jnp.zeros((4, d_shard), F32)], axis=0)

    g_sh = pack_sharded(g_onorm, g_conv)
    d_sh, nm_sh, nv_sh = adam_flat(g_sh, pack_sharded(o_norm_mix, o_conv_w[0]), pack_sharded(m_o_norm_mix, m_o_conv_w[0]),
                                   pack_sharded(v_o_norm_mix, v_o_conv_w[0]), name="adam_sharded_small")
    for kind, arr in (("grad", g_sh), ("delta", d_sh), ("new_m", nm_sh), ("new_v", nv_sh)):
        small[kind]["o_norm_mix"] = arr[0:1]
        small[kind]["o_conv_w"] = arr[1:4][None]

    big = {"e_w_in": r_w_in, "e_w_uq": r_w_uq, "e_w_ukv": r_w_ukv, "e_w_out": r_w_out_e, "o_w_in": r_w_in_o,
           "o_w_out": r_w_out_o, "mlp_w1": r_w1, "mlp_w2": r_w2}
    order = ["e_norm_mix", "e_w_in", "e_q_norm", "e_w_uq", "e_kv_norm", "e_w_ukv", "e_v_norm", "e_sgu_w", "e_sgu_b",
             "e_mla_out_norm", "e_sgu_out_norm", "e_w_out", "o_norm_mix", "o_w_in", "o_conv_w", "o_w_out", "mlp_norm",
             "mlp_w1", "mlp_w2", "final_norm"]
    result = [loss, grad_x[None]]
    for ki, kind in enumerate(("grad", "delta", "new_m", "new_v")):
        for nm in order:
            result.append(big[nm][ki] if nm in big else small[kind][nm])
    return tuple(result)
```

```python
import functools

import numpy as np
import jax
import jax.numpy as jnp
from jax import lax
from jax.experimental import pallas as pl
from jax.experimental.pallas import tpu as pltpu

BF = jnp.bfloat16
F32 = jnp.float32
MESH = pl.DeviceIdType.MESH
AXES = ("x", "y", "c")
N_DEV = 8

EPS = 1e-6
HEADS = 8
Q_LORA = 512
KV_LORA = 512
QK_NOPE = 128
QK_ROPE = 64
HALF_ROPE = QK_ROPE // 2
V_HEAD = 128
HEAD_PAD = 256
ROPE_BASE = 10000.0
GROUPS = 8
CH = 128
CHUNK = 128
SGU_OUT = GROUPS * CH
MLA_OUT = HEADS * V_HEAD
ATTN_SCALE = float((QK_NOPE + QK_ROPE) ** -0.5)

ADAM_LR = 0.001
ADAM_B1 = 0.9
ADAM_B2 = 0.999
ADAM_EPS = 1e-08
ADAM_WD = 0.01
ADAM_STEP = 10
ADAM_C1 = 1.0 - ADAM_B1 ** ADAM_STEP
ADAM_C2 = 1.0 - ADAM_B2 ** ADAM_STEP

V7X_VMEM_BYTES = 64 * 2 ** 20
VMEM_LIMIT_CAP = V7X_VMEM_BYTES - 6 * 2 ** 20
LANES = 128
ROW_TILE = 256
ATTN_TILE = 512
MM_TILE = 1024
MM_K_TILE = 2048


def _padded_bytes(block, dtype):
    dims = [d for d in block if d is not None]
    if len(dims) >= 1:
        dims[-1] = -(-dims[-1] // LANES) * LANES
    if len(dims) >= 2:
        dims[-2] = -(-dims[-2] // 16) * 16
    return int(np.prod(dims)) * jnp.dtype(dtype).itemsize


def _pcall(body, *, name, grid, ins, outs, scratch=(), semantics=None, aliases=None, prefetch=None, deps=()):
    any_spec = pl.BlockSpec(memory_space=pl.ANY)
    if deps:
        n_lead = len(ins) + (1 if prefetch is not None else 0)
        n_deps = len(deps)
        inner = body

        def body(*refs):
            inner(*refs[:n_lead], *refs[n_lead + n_deps:])

        ins = list(ins) + [(d, None, None) for d in deps]
    in_specs = [any_spec if b is None else pl.BlockSpec(b, m) for _, b, m in ins]
    out_specs = [any_spec if b is None else pl.BlockSpec(b, m) for _, _, b, m in outs]
    out_shape = [pltpu.HBM(s, d) for s, d, _, _ in outs]
    est = 0
    for a, b, _ in ins:
        if b is not None:
            est += 2 * _padded_bytes(b, a.dtype)
    for _, d, b, _ in outs:
        if b is not None:
            est += 2 * _padded_bytes(b, d)
    for s in scratch:
        if hasattr(s, "shape") and hasattr(s, "dtype"):
            est += _padded_bytes(s.shape, s.dtype)
    limit = int(min(VMEM_LIMIT_CAP, est + 16 * 2 ** 20))
    params = pltpu.CompilerParams(
        dimension_semantics=semantics or ("arbitrary",) * len(grid), vmem_limit_bytes=limit)
    args = [pltpu.with_memory_space_constraint(a, pltpu.HBM) for a, _, _ in ins]
    if prefetch is not None:
        grid_spec = pltpu.PrefetchScalarGridSpec(
            num_scalar_prefetch=1, grid=grid, in_specs=in_specs, out_specs=out_specs, scratch_shapes=list(scratch))
        call = pl.pallas_call(body, out_shape=out_shape, grid_spec=grid_spec, name=name, compiler_params=params,
                              input_output_aliases=aliases or {})
        return call(prefetch, *args)
    call = pl.pallas_call(body, out_shape=out_shape, grid=grid, in_specs=in_specs, out_specs=out_specs,
                          scratch_shapes=list(scratch), name=name, compiler_params=params,
                          input_output_aliases=aliases or {})
    return call(*args)


def _tile(dim, pref, quantum=LANES):
    if dim <= pref:
        return dim
    t = (pref // quantum) * quantum
    while t >= quantum:
        if dim % t == 0:
            return t
        t -= quantum
    return dim


def _vshape(arr_shape):
    if len(arr_shape) == 2:
        return tuple(arr_shape)
    s, r, c = arr_shape
    return (r, s * c)


def _vblock(arr_shape, br, bc, rc):
    if len(arr_shape) == 2:
        return (br, bc), (lambda *g: rc(*g))
    _, _, c = arr_shape
    assert c % bc == 0, (arr_shape, bc)
    per = c // bc

    def imap(*g):
        ri, ci = rc(*g)
        return (ci // per, ri, ci % per)

    return (None, br, bc), imap


def _shard_width(*shapes):
    w = None
    for s in shapes:
        if len(s) == 3:
            w = s[2] if w is None else int(np.gcd(w, s[2]))
    return w


def mm(a, b, *, name, ta=False, tb=False, out=None, outs=None, epi=None, epi_ins=(), bm=None, bn=None, bk=None,
       deps=()):
    av, bv = _vshape(a.shape), _vshape(b.shape)
    M, K = (av[1], av[0]) if ta else av
    K2, N = (bv[1], bv[0]) if tb else bv
    assert K == K2, (a.shape, b.shape, ta, tb)
    if outs is None:
        outs = [(out[0], out[1], None)]
    a_sw = _shard_width(a.shape)
    b_sw = _shard_width(b.shape)
    o_sw = _shard_width(*[o[0] for o in outs])
    m_lim = a_sw if (ta and a_sw) else None
    k_lim = [w for w in ((a_sw if not ta else None), (b_sw if tb else None)) if w]
    n_lim = [w for w in ((b_sw if not tb else None), o_sw) if w]
    if bm is None:
        bm = _tile(M, min([MM_TILE] + ([m_lim] if m_lim else [])))
    if bn is None:
        bn = _tile(N, min([MM_TILE] + n_lim))
    if bk is None:
        bk = K if (K <= 4096 and not k_lim) else _tile(K, min([MM_K_TILE] + k_lim))
    assert M % bm == 0 and N % bn == 0 and K % bk == 0, (name, M, N, K, bm, bn, bk)
    nk = K // bk
    grid = (M // bm, N // bn, nk)
    if ta:
        a_blk, a_map = _vblock(a.shape, bk, bm, lambda i, j, k: (k, i))
    else:
        a_blk, a_map = _vblock(a.shape, bm, bk, lambda i, j, k: (i, k))
    if tb:
        b_blk, b_map = _vblock(b.shape, bn, bk, lambda i, j, k: (j, k))
    else:
        b_blk, b_map = _vblock(b.shape, bk, bn, lambda i, j, k: (k, j))
    dn = (((0 if ta else 1,), (1 if tb else 0,)), ((), ()))
    ins = [(a, a_blk, a_map), (b, b_blk, b_map)] + list(epi_ins)
    out_list = []
    for shape, dtype, cols in outs:
        cols = cols or bn
        blk, imap = _vblock(shape, bm, cols, lambda i, j, k: (i, j))
        out_list.append((shape, dtype, blk, imap))
    n_e, n_o = len(epi_ins), len(out_list)

    def body(*refs):
        a_ref, b_ref = refs[0], refs[1]
        e_refs = refs[2:2 + n_e]
        o_refs = refs[2 + n_e:2 + n_e + n_o]

        def finish(acc):
            res = epi(acc, *e_refs) if epi is not None else (acc,)
            for o_ref, r in zip(o_refs, res):
                o_ref[...] = r.astype(o_ref.dtype)

        x = a_ref[...].astype(BF)
        y = b_ref[...].astype(BF)
        p = lax.dot_general(x, y, dn, preferred_element_type=F32)
        if nk == 1:
            finish(p)
        else:
            acc_ref = refs[-1]
            k = pl.program_id(2)

            @pl.when(k == 0)
            def _():
                acc_ref[...] = p

            @pl.when(k > 0)
            def _():
                acc_ref[...] += p

            @pl.when(k == nk - 1)
            def _():
                finish(acc_ref[...])

    scratch = [pltpu.VMEM((bm, bn), F32)] if nk > 1 else []
    res = _pcall(body, name=name, grid=grid, ins=ins, outs=out_list, scratch=scratch,
                 semantics=("parallel", "parallel", "arbitrary"), deps=deps)
    return res[0] if len(res) == 1 else res


_GELU_K = float(np.sqrt(2.0 / np.pi))
_GELU_C = 0.044715


def _gelu(x):
    t = jnp.tanh(_GELU_K * (x + _GELU_C * (x * x * x)))
    return 0.5 * x * (1.0 + t)


def _gelu_grad(x):
    t = jnp.tanh(_GELU_K * (x + _GELU_C * (x * x * x)))
    return 0.5 * (1.0 + t) + 0.5 * x * (1.0 - t * t) * (_GELU_K * (1.0 + 3.0 * _GELU_C * (x * x)))


def _rstd(x):
    return lax.rsqrt(jnp.mean(x * x, axis=-1, keepdims=True) + EPS)


def _rms_bwd(x, gain, dy):
    r = _rstd(x)
    xh = x * r
    gdy = dy * gain
    dx = r * (gdy - xh * jnp.mean(gdy * xh, axis=-1, keepdims=True))
    return dx, dy * xh


def _rope_fwd(x, cos_t, sin_t):
    return x * cos_t + pltpu.roll(x, 2 * HALF_ROPE, 1) * sin_t


def _rope_bwd(dy, cos_t, sin_t):
    return dy * cos_t + pltpu.roll(dy * sin_t, 2 * HALF_ROPE, 1)


def _acc_rows(ref, val, first):
    s = jnp.sum(val, axis=0, keepdims=True)

    @pl.when(first)
    def _():
        ref[...] = s

    @pl.when(jnp.logical_not(first))
    def _():
        ref[...] += s


def rms_fwd(x, gain, *, name, col_block=0, width=None):
    T = x.shape[0]
    width = width or x.shape[1]
    tm = _tile(T, ROW_TILE, 8)

    def body(x_ref, g_ref, o_ref):
        v = x_ref[...]
        o_ref[...] = (v * _rstd(v) * g_ref[...]).astype(BF)

    return _pcall(body, name=name, grid=(T // tm,),
                  ins=[(x, (tm, width), lambda i: (i, col_block)), (gain, (1, width), lambda i: (0, 0))],
                  outs=[((T, width), BF, (tm, width), lambda i: (i, 0))], semantics=("parallel",))[0]


def rms_bwd(x, gain, dy, *, name, col_block=0, dres=None, want_f32=True, want_bf=True, deps=()):
    T, width = dy.shape
    tm = _tile(T, ROW_TILE, 8)
    has_res = dres is not None

    def body(*refs):
        x_ref, g_ref, dy_ref = refs[:3]
        pos = 3
        res_ref = None
        if has_res:
            res_ref = refs[pos]
            pos += 1
        outs = refs[pos:]
        dx, dg_rows = _rms_bwd(x_ref[...], g_ref[...], dy_ref[...])
        if has_res:
            dx = dx + res_ref[...]
        o = 0
        if want_f32:
            outs[o][...] = dx
            o += 1
        if want_bf:
            outs[o][...] = dx.astype(BF)
            o += 1
        _acc_rows(outs[o], dg_rows, pl.program_id(0) == 0)

    ins = [(x, (tm, width), lambda i: (i, col_block)), (gain, (1, width), lambda i: (0, 0)),
           (dy, (tm, width), lambda i: (i, 0))]
    if has_res:
        ins.append((dres, (tm, width), lambda i: (i, 0)))
    outs = []
    if want_f32:
        outs.append(((T, width), F32, (tm, width), lambda i: (i, 0)))
    if want_bf:
        outs.append(((T, width), BF, (tm, width), lambda i: (i, 0)))
    outs.append(((1, width), F32, (1, width), lambda i: (0, 0)))
    return _pcall(body, name=name, grid=(T // tm,), ins=ins, outs=outs, deps=deps)


def mla_prep(proj, q_norm, kv_norm, cos_t, sin_t, *, name):
    T = proj.shape[0]
    tm = _tile(T, ROW_TILE, 8)
    kr_block = (proj.shape[1] - LANES) // LANES

    def body(cq_ref, ckv_ref, kr_ref, qg_ref, kg_ref, cos_ref, sin_ref, qn_ref, kvn_ref, krope_ref):
        cq = cq_ref[...]
        qn_ref[...] = (cq * _rstd(cq) * qg_ref[...]).astype(BF)
        ckv = ckv_ref[...]
        kvn_ref[...] = (ckv * _rstd(ckv) * kg_ref[...]).astype(BF)
        krope_ref[...] = _rope_fwd(kr_ref[...], cos_ref[...], sin_ref[...]).astype(BF)

    return _pcall(
        body, name=name, grid=(T // tm,),
        ins=[(proj, (tm, Q_LORA), lambda i: (i, 0)), (proj, (tm, KV_LORA), lambda i: (i, 1)),
             (proj, (tm, LANES), lambda i: (i, kr_block)),
             (q_norm, (1, Q_LORA), lambda i: (0, 0)), (kv_norm, (1, KV_LORA), lambda i: (0, 0)),
             (cos_t, (tm, LANES), lambda i: (i, 0)), (sin_t, (tm, LANES), lambda i: (i, 0))],
        outs=[((T, Q_LORA), BF, (tm, Q_LORA), lambda i: (i, 0)), ((T, KV_LORA), BF, (tm, KV_LORA), lambda i: (i, 0)),
              ((T, LANES), BF, (tm, LANES), lambda i: (i, 0))],
        semantics=("parallel",))


def _attn_scores(q, k_blk, diagonal):
    s = lax.dot_general(q, k_blk, (((1,), (1,)), ((), ())), preferred_element_type=F32) * ATTN_SCALE
    if diagonal:
        row = lax.broadcasted_iota(jnp.int32, s.shape, 0)
        col = lax.broadcasted_iota(jnp.int32, s.shape, 1)
        s = jnp.where(col <= row, s, -jnp.inf)
    return s


def attn_fwd(q, k, v, *, name):
    T = q.shape[0]
    tq = _tile(T, ATTN_TILE, 8)

    def body(q_ref, k_ref, v_ref, o_ref, lse_ref):
        i = pl.program_id(1)
        qv = q_ref[...]

        def block(kb, carry, diagonal):
            m, l, acc = carry
            start = pl.multiple_of(kb * tq, tq)
            s = _attn_scores(qv, k_ref[pl.ds(start, tq), :], diagonal)
            m_new = jnp.maximum(m, jnp.max(s, axis=-1, keepdims=True))
            alpha = jnp.exp(m - m_new)
            p = jnp.exp(s - m_new)
            l = alpha * l + jnp.sum(p, axis=-1, keepdims=True)
            acc = alpha * acc + jnp.dot(p.astype(BF), v_ref[pl.ds(start, tq), :], preferred_element_type=F32)
            return m_new, l, acc

        init = (jnp.full((tq, 1), -jnp.inf, F32), jnp.zeros((tq, 1), F32), jnp.zeros((tq, V_HEAD), F32))
        carry = lax.fori_loop(0, i, lambda kb, c: block(kb, c, False), init)
        m, l, acc = block(i, carry, True)
        o_ref[...] = acc / l
        lse_ref[...] = jnp.broadcast_to(m + jnp.log(l), (tq, V_HEAD))

    return _pcall(
        body, name=name, grid=(HEADS, T // tq),
        ins=[(q, (tq, HEAD_PAD), lambda h, i: (i, h)), (k, (T, HEAD_PAD), lambda h, i: (0, h)),
             (v, (T, V_HEAD), lambda h, i: (0, h))],
        outs=[((T, MLA_OUT), F32, (tq, V_HEAD), lambda h, i: (i, h)),
              ((T, MLA_OUT), F32, (tq, V_HEAD), lambda h, i: (i, h))], semantics=("parallel", "parallel"))


def attn_bwd(q, k, v, o, lse, do, *, name):
    T = q.shape[0]
    tq = _tile(T, ATTN_TILE, 8)

    def body(q_ref, k_ref, v_ref, o_ref, lse_ref, do_ref, dq_ref, dk_ref, dv_ref):
        i = pl.program_id(1)

        @pl.when(i == 0)
        def _():
            dk_ref[...] = jnp.zeros_like(dk_ref)
            dv_ref[...] = jnp.zeros_like(dv_ref)

        qv = q_ref[...]
        do_t = do_ref[...]
        lse_v = lse_ref[:, 0:1]
        delta = jnp.sum(do_t.astype(F32) * o_ref[...], axis=-1, keepdims=True)

        def block(kb, dq, diagonal):
            start = pl.multiple_of(kb * tq, tq)
            k_blk = k_ref[pl.ds(start, tq), :]
            v_blk = v_ref[pl.ds(start, tq), :]
            p = jnp.exp(_attn_scores(qv, k_blk, diagonal) - lse_v)
            dp = lax.dot_general(do_t, v_blk, (((1,), (1,)), ((), ())), preferred_element_type=F32)
            ds = (p * (dp - delta) * ATTN_SCALE).astype(BF)
            dk_ref[pl.ds(start, tq), :] += lax.dot_general(ds, qv, (((0,), (0,)), ((), ())), preferred_element_type=F32)
            dv_ref[pl.ds(start, tq), :] += lax.dot_general(p.astype(BF), do_t, (((0,), (0,)), ((), ())),
                                                          preferred_element_type=F32)
            return dq + jnp.dot(ds, k_blk, preferred_element_type=F32)

        dq = lax.fori_loop(0, i, lambda kb, c: block(kb, c, False), jnp.zeros((tq, HEAD_PAD), F32))
        dq_ref[...] = block(i, dq, True)

    return _pcall(
        body, name=name, grid=(HEADS, T // tq),
        ins=[(q, (tq, HEAD_PAD), lambda h, i: (i, h)), (k, (T, HEAD_PAD), lambda h, i: (0, h)),
             (v, (T, V_HEAD), lambda h, i: (0, h)), (o, (tq, V_HEAD), lambda h, i: (i, h)),
             (lse, (tq, V_HEAD), lambda h, i: (i, h)), (do, (tq, V_HEAD), lambda h, i: (i, h))],
        outs=[((T, HEADS * HEAD_PAD), F32, (tq, HEAD_PAD), lambda h, i: (i, h)),
              ((T, HEADS * HEAD_PAD), F32, (T, HEAD_PAD), lambda h, i: (0, h)),
              ((T, MLA_OUT), F32, (T, V_HEAD), lambda h, i: (0, h))],
        semantics=("parallel", "arbitrary"))


def mla_bwd_prep(dq, dk, dv, cos_t, sin_t, *, name):
    T = dq.shape[0]
    tm = _tile(T, ROW_TILE, 8)

    def body(dq_ref, dk_ref, dv_ref, cos_ref, sin_ref, dql_ref, dkvl_ref, dkr_ref):
        cos_v, sin_v = cos_ref[...], sin_ref[...]
        kr = jnp.zeros((tm, LANES), F32)
        for h in range(HEADS):
            lo = h * HEAD_PAD
            dql_ref[:, lo:lo + QK_NOPE] = dq_ref[:, lo:lo + QK_NOPE].astype(BF)
            dql_ref[:, lo + QK_NOPE:lo + HEAD_PAD] = _rope_bwd(
                dq_ref[:, lo + QK_NOPE:lo + HEAD_PAD], cos_v, sin_v).astype(BF)
            dkvl_ref[:, lo:lo + QK_NOPE] = dk_ref[:, lo:lo + QK_NOPE].astype(BF)
            dkvl_ref[:, lo + QK_NOPE:lo + HEAD_PAD] = dv_ref[:, h * V_HEAD:(h + 1) * V_HEAD].astype(BF)
            kr = kr + dk_ref[:, lo + QK_NOPE:lo + HEAD_PAD]
        dkr_ref[...] = _rope_bwd(kr, cos_v, sin_v).astype(BF)

    W = HEADS * HEAD_PAD
    return _pcall(
        body, name=name, grid=(T // tm,),
        ins=[(dq, (tm, W), lambda i: (i, 0)), (dk, (tm, W), lambda i: (i, 0)), (dv, (tm, MLA_OUT), lambda i: (i, 0)),
             (cos_t, (tm, LANES), lambda i: (i, 0)), (sin_t, (tm, LANES), lambda i: (i, 0))],
        outs=[((T, W), BF, (tm, W), lambda i: (i, 0)), ((T, W), BF, (tm, W), lambda i: (i, 0)),
              ((T, LANES), BF, (tm, LANES), lambda i: (i, 0))],
        semantics=("parallel",))


def _group_norm_stats(vg):
    mu = jnp.mean(vg, axis=-1, keepdims=True)
    d = vg - mu
    r = lax.rsqrt(jnp.mean(d * d, axis=-1, keepdims=True) + EPS)
    return d * r, r


def mix_fwd(a, proj, g_mla, g_sgu, v_gain, w_tril, b_full, *, name):
    T = a.shape[0]
    tm = _tile(T, ROW_TILE, CHUNK)
    n_chunk = tm // CHUNK

    def body(a_ref, u_ref, v_ref, gm_ref, gs_ref, vg_ref, w_ref, b_ref, o_ref, s_scr):
        av = a_ref[...]
        o_ref[:, :MLA_OUT] = (av * _rstd(av) * gm_ref[...]).astype(BF)
        for g in range(GROUPS):
            sl = slice(g * CH, (g + 1) * CH)
            vhat, _ = _group_norm_stats(_gelu(v_ref[:, sl]))
            vn = (vhat * vg_ref[:, sl]).astype(BF)
            u = _gelu(u_ref[:, sl])
            for ci in range(n_chunk):
                rs = slice(ci * CHUNK, (ci + 1) * CHUNK)
                y = jnp.dot(w_ref[g], vn[rs], preferred_element_type=F32) + b_ref[:, sl]
                s_scr[rs, sl] = u[rs] * y
        s = s_scr[...]
        o_ref[:, MLA_OUT:] = (s * _rstd(s) * gs_ref[...]).astype(BF)

    return _pcall(
        body, name=name, grid=(T // tm,),
        ins=[(a, (tm, MLA_OUT), lambda i: (i, 0)), (proj, (tm, SGU_OUT), lambda i: (i, 1)),
             (proj, (tm, SGU_OUT), lambda i: (i, 2)), (g_mla, (1, MLA_OUT), lambda i: (0, 0)),
             (g_sgu, (1, SGU_OUT), lambda i: (0, 0)), (v_gain, (1, SGU_OUT), lambda i: (0, 0)),
             (w_tril, (GROUPS, CHUNK, CHUNK), lambda i: (0, 0, 0)), (b_full, (CHUNK, SGU_OUT), lambda i: (0, 0))],
        outs=[((T, MLA_OUT + SGU_OUT), BF, (tm, MLA_OUT + SGU_OUT), lambda i: (i, 0))],
        scratch=[pltpu.VMEM((tm, SGU_OUT), F32)], semantics=("parallel",))[0]


def mix_bwd(dmixed, a, proj, g_mla, g_sgu, v_gain, w_tril, w_tril_t, b_full, *, name):
    T = a.shape[0]
    tm = _tile(T, ROW_TILE, CHUNK)
    n_chunk = tm // CHUNK

    def body(dm_a_ref, dm_s_ref, a_ref, u_ref, v_ref, gm_ref, gs_ref, vg_ref, w_ref, wt_ref, b_ref,
             da_ref, duv_ref, dgm_ref, dgs_ref, dvg_ref, dw_ref, db_ref, s_scr, y_scr):
        first = pl.program_id(0) == 0
        da, dgm_rows = _rms_bwd(a_ref[...], gm_ref[...], dm_a_ref[...])
        da_ref[...] = da.astype(BF)
        _acc_rows(dgm_ref, dgm_rows, first)

        for g in range(GROUPS):
            sl = slice(g * CH, (g + 1) * CH)
            vhat, _ = _group_norm_stats(_gelu(v_ref[:, sl]))
            vn = (vhat * vg_ref[:, sl]).astype(BF)
            u = _gelu(u_ref[:, sl])
            for ci in range(n_chunk):
                rs = slice(ci * CHUNK, (ci + 1) * CHUNK)
                y = jnp.dot(w_ref[g], vn[rs], preferred_element_type=F32) + b_ref[:, sl]
                y_scr[rs, sl] = y
                s_scr[rs, sl] = u[rs] * y
        ds, dgs_rows = _rms_bwd(s_scr[...], gs_ref[...], dm_s_ref[...])
        _acc_rows(dgs_ref, dgs_rows, first)
        s_scr[...] = ds

        @pl.when(first)
        def _():
            dw_ref[...] = jnp.zeros_like(dw_ref)
            db_ref[...] = jnp.zeros_like(db_ref)

        for g in range(GROUPS):
            sl = slice(g * CH, (g + 1) * CH)
            upre = u_ref[:, sl]
            vpre = v_ref[:, sl]
            u = _gelu(upre)
            vhat, r = _group_norm_stats(_gelu(vpre))
            gain = vg_ref[:, sl]
            vn = (vhat * gain).astype(BF)
            dsg = s_scr[:, sl]
            duv_ref[:, sl] = (dsg * y_scr[:, sl] * _gelu_grad(upre)).astype(BF)
            dy = dsg * u
            dyb = dy.astype(BF)
            dvn_parts = []
            for ci in range(n_chunk):
                rs = slice(ci * CHUNK, (ci + 1) * CHUNK)
                dvn_parts.append(jnp.dot(wt_ref[g], dyb[rs], preferred_element_type=F32))
                dw_ref[g] += lax.dot_general(dyb[rs], vn[rs], (((1,), (1,)), ((), ())), preferred_element_type=F32)
                db_ref[:, sl] += jnp.broadcast_to(jnp.sum(dy[rs], axis=-1, keepdims=True), (CHUNK, CH))
            dvn = dvn_parts[0] if n_chunk == 1 else jnp.concatenate(dvn_parts, axis=0)
            _acc_rows(dvg_ref.at[:, sl], dvn * vhat, first)
            dvh = dvn * gain
            dvg = r * (dvh - jnp.mean(dvh, axis=-1, keepdims=True)
                       - vhat * jnp.mean(dvh * vhat, axis=-1, keepdims=True))
            duv_ref[:, SGU_OUT + g * CH:SGU_OUT + (g + 1) * CH] = (dvg * _gelu_grad(vpre)).astype(BF)

    return _pcall(
        body, name=name, grid=(T // tm,),
        ins=[(dmixed, (tm, MLA_OUT), lambda i: (i, 0)), (dmixed, (tm, SGU_OUT), lambda i: (i, 1)),
             (a, (tm, MLA_OUT), lambda i: (i, 0)), (proj, (tm, SGU_OUT), lambda i: (i, 1)),
             (proj, (tm, SGU_OUT), lambda i: (i, 2)), (g_mla, (1, MLA_OUT), lambda i: (0, 0)),
             (g_sgu, (1, SGU_OUT), lambda i: (0, 0)), (v_gain, (1, SGU_OUT), lambda i: (0, 0)),
             (w_tril, (GROUPS, CHUNK, CHUNK), lambda i: (0, 0, 0)), (w_tril_t, (GROUPS, CHUNK, CHUNK), lambda i: (0, 0, 0)),
             (b_full, (CHUNK, SGU_OUT), lambda i: (0, 0))],
        outs=[((T, MLA_OUT), BF, (tm, MLA_OUT), lambda i: (i, 0)),
              ((T, 2 * SGU_OUT), BF, (tm, 2 * SGU_OUT), lambda i: (i, 0)),
              ((1, MLA_OUT), F32, (1, MLA_OUT), lambda i: (0, 0)), ((1, SGU_OUT), F32, (1, SGU_OUT), lambda i: (0, 0)),
              ((1, SGU_OUT), F32, (1, SGU_OUT), lambda i: (0, 0)),
              ((GROUPS, CHUNK, CHUNK), F32, (GROUPS, CHUNK, CHUNK), lambda i: (0, 0, 0)),
              ((CHUNK, SGU_OUT), F32, (CHUNK, SGU_OUT), lambda i: (0, 0))],
        scratch=[pltpu.VMEM((tm, SGU_OUT), F32), pltpu.VMEM((tm, SGU_OUT), F32)])


def _shift_down(z, n, row):
    return jnp.where(row >= n, pltpu.roll(z, n, 0), 0.0)


def _shift_up(z, n, row, T):
    return jnp.where(row < T - n, pltpu.roll(z, T - n, 0), 0.0)


def conv_fwd(proj, conv_w, *, name):
    T, D3 = proj.shape
    D = D3 // 3
    tn = _tile(D, 256)
    nj = D // tn

    def body(b_ref, c_ref, x_ref, w_ref, o_ref):
        row = lax.broadcasted_iota(jnp.int32, (T, tn), 0)
        z = c_ref[...] * x_ref[...]
        zc = w_ref[2:3, :] * z + w_ref[1:2, :] * _shift_down(z, 1, row) + w_ref[0:1, :] * _shift_down(z, 2, row)
        o_ref[...] = (b_ref[...] * zc).astype(BF)

    return _pcall(
        body, name=name, grid=(nj,),
        ins=[(proj, (T, tn), lambda j: (0, j)), (proj, (T, tn), lambda j: (0, nj + j)),
             (proj, (T, tn), lambda j: (0, 2 * nj + j)), (conv_w, (3, tn), lambda j: (0, j))],
        outs=[((T, D), BF, (T, tn), lambda j: (0, j))], semantics=("parallel",))[0]


def conv_bwd(dg, proj, conv_w, *, name):
    T, D3 = proj.shape
    D = D3 // 3
    tn = _tile(D, 256)
    nj = D // tn

    def body(dg_ref, b_ref, c_ref, x_ref, w_ref, dp_ref, dw_ref, dc_scr, dx_scr):
        part = pl.program_id(1)

        @pl.when(part == 0)
        def _():
            row = lax.broadcasted_iota(jnp.int32, (T, tn), 0)
            c, x = c_ref[...], x_ref[...]
            z = c * x
            z1 = _shift_down(z, 1, row)
            z2 = _shift_down(z, 2, row)
            dgv = dg_ref[...]
            zc = w_ref[2:3, :] * z + w_ref[1:2, :] * z1 + w_ref[0:1, :] * z2
            dp_ref[...] = (dgv * zc).astype(BF)
            dzc = dgv * b_ref[...]
            dw_ref[0:1, :] = jnp.sum(dzc * z2, axis=0, keepdims=True)
            dw_ref[1:2, :] = jnp.sum(dzc * z1, axis=0, keepdims=True)
            dw_ref[2:3, :] = jnp.sum(dzc * z, axis=0, keepdims=True)
            dz = (w_ref[2:3, :] * dzc + w_ref[1:2, :] * _shift_up(dzc, 1, row, T)
                  + w_ref[0:1, :] * _shift_up(dzc, 2, row, T))
            dc_scr[...] = (dz * x).astype(BF)
            dx_scr[...] = (dz * c).astype(BF)

        @pl.when(part == 1)
        def _():
            dp_ref[...] = dc_scr[...]

        @pl.when(part == 2)
        def _():
            dp_ref[...] = dx_scr[...]

    return _pcall(
        body, name=name, grid=(nj, 3),
        ins=[(dg, (T, tn), lambda j, p: (0, j)), (proj, (T, tn), lambda j, p: (0, j)),
             (proj, (T, tn), lambda j, p: (0, nj + j)), (proj, (T, tn), lambda j, p: (0, 2 * nj + j)),
             (conv_w, (3, tn), lambda j, p: (0, j))],
        outs=[((T, D3), BF, (T, tn), lambda j, p: (0, p * nj + j)), ((3, D), F32, (3, tn), lambda j, p: (0, j))],
        scratch=[pltpu.VMEM((T, tn), BF), pltpu.VMEM((T, tn), BF)], semantics=("parallel", "arbitrary"))


def loss_bwd(x, gain, target, *, name):
    T, D = x.shape
    tm = _tile(T, ROW_TILE, 8)

    def body(x_ref, g_ref, t_ref, dx_ref, dxb_ref, dg_ref, loss_ref):
        first = pl.program_id(0) == 0
        xv = x_ref[...]
        r = _rstd(xv)
        xh = xv * r
        gain_v = g_ref[...]
        err = xh * gain_v - t_ref[...]
        part = 0.5 * jnp.sum(jnp.mean(err * err, axis=-1, keepdims=True), axis=0, keepdims=True)
        _acc_rows(loss_ref, jnp.broadcast_to(part, (1, LANES)), first)
        dy = err * (1.0 / D)
        gdy = dy * gain_v
        dx = r * (gdy - xh * jnp.mean(gdy * xh, axis=-1, keepdims=True))
        dx_ref[...] = dx
        dxb_ref[...] = dx.astype(BF)
        _acc_rows(dg_ref, dy * xh, first)

    return _pcall(
        body, name=name, grid=(T // tm,),
        ins=[(x, (tm, D), lambda i: (i, 0)), (gain, (1, D), lambda i: (0, 0)), (target, (tm, D), lambda i: (i, 0))],
        outs=[((T, D), F32, (tm, D), lambda i: (i, 0)), ((T, D), BF, (tm, D), lambda i: (i, 0)),
              ((1, D), F32, (1, D), lambda i: (0, 0)), ((1, LANES), F32, (1, LANES), lambda i: (0, 0))])


def _adamw(g, w, m, v):
    m = ADAM_B1 * m + (1.0 - ADAM_B1) * g
    v = ADAM_B2 * v + (1.0 - ADAM_B2) * (g * g)
    m_hat = m / ADAM_C1
    v_hat = v / ADAM_C2
    delta = -ADAM_LR * (m_hat / (jnp.sqrt(v_hat) + ADAM_EPS) + ADAM_WD * w)
    return delta, m, v


def adam_flat(g, w, m, v, *, name):
    def body(g_ref, w_ref, m_ref, v_ref, d_ref, nm_ref, nv_ref):
        d, nm, nv = _adamw(g_ref[...], w_ref[...], m_ref[...], v_ref[...])
        d_ref[...] = d
        nm_ref[...] = nm
        nv_ref[...] = nv

    blk = g.shape
    zero = lambda: (0, 0)
    return _pcall(body, name=name, grid=(),
                  ins=[(t, blk, zero) for t in (g, w, m, v)],
                  outs=[(blk, F32, blk, zero)] * 3)


def _chip_slots():
    x, y, c = lax.axis_index("x"), lax.axis_index("y"), lax.axis_index("c")
    chips = [(1 - x, y), (x, 1 - y), (1 - x, 1 - y)]
    return x, y, c, chips


def reduce_adam(gs, a_buf, b_buf, w, m, v, layer, prev, *, name, deps=()):
    L, R, C = w.shape
    tr = _tile(R, 256, 8)
    x, y, c, _ = _chip_slots()
    idx = jnp.stack([4 * x + 2 * y + c, 2 * x + y]).astype(jnp.int32)
    n_prev = 0 if prev is None else 4

    def body(idx_ref, g_ref, a_ref, b0_ref, b1_ref, b2_ref, w_ref, m_ref, v_ref, *rest):
        outs = rest[n_prev:]
        g = ((((g_ref[...].astype(F32) + a_ref[...].astype(F32)) + b0_ref[...].astype(F32))
              + b1_ref[...].astype(F32)) + b2_ref[...].astype(F32))
        d, nm, nv = _adamw(g, w_ref[...], m_ref[...], v_ref[...])
        outs[0][...] = g
        outs[1][...] = d
        outs[2][...] = nm
        outs[3][...] = nv

    blk3 = (None, tr, C)
    ins = [(gs, blk3, lambda i, s: (s[0], i, 0)), (a_buf, blk3, lambda i, s: (s[1], i, 0)),
           (b_buf, blk3, lambda i, s: (0, i, 0)), (b_buf, blk3, lambda i, s: (1, i, 0)),
           (b_buf, blk3, lambda i, s: (2, i, 0)),
           (w, blk3, lambda i, s: (layer, i, 0)), (m, blk3, lambda i, s: (layer, i, 0)),
           (v, blk3, lambda i, s: (layer, i, 0))]
    aliases = {}
    if prev is not None:
        for o, p in enumerate(prev):
            ins.append((p, None, None))
            aliases[1 + 8 + o] = o
    outs = [((L, R, C), F32, blk3, lambda i, s: (layer, i, 0))] * 4
    return _pcall(body, name=name, grid=(R // tr,), ins=ins, outs=outs, prefetch=idx, aliases=aliases,
                  semantics=("parallel",), deps=deps)


def pair_sum(gs, a_buf, *, name):
    _, R, C = gs.shape
    tr = _tile(R, 256, 8)
    x, y, c, chips = _chip_slots()
    idx = jnp.stack([4 * cx + 2 * cy + c for cx, cy in chips] + [2 * cx + cy for cx, cy in chips]).astype(jnp.int32)

    def body(idx_ref, g_ref, a_ref, o_ref):
        o_ref[...] = (g_ref[...].astype(F32) + a_ref[...].astype(F32)).astype(BF)

    blk3 = (None, tr, C)
    return _pcall(body, name=name, grid=(3, R // tr),
                  ins=[(gs, blk3, lambda j, i, s: (s[j], i, 0)), (a_buf, blk3, lambda j, i, s: (s[3 + j], i, 0))],
                  outs=[((3, R, C), BF, blk3, lambda j, i, s: (j, i, 0))], prefetch=idx,
                  semantics=("parallel", "parallel"))[0]


def sum_rows8(gathered, rows, *, name):
    W = gathered.shape[1]

    def body(g_ref, o_ref):
        acc = g_ref[0:rows, :]
        for d in range(1, N_DEV):
            acc = acc + g_ref[d * rows:(d + 1) * rows, :]
        o_ref[...] = acc

    return _pcall(body, name=name, grid=(), ins=[(gathered, gathered.shape, lambda: (0, 0))],
                  outs=[((rows, W), F32, (rows, W), lambda: (0, 0))])[0]


HBM_SPEC = pl.BlockSpec(memory_space=pltpu.HBM)
SEM_SPEC = pl.BlockSpec(memory_space=pltpu.SEMAPHORE)
ANY_SPEC = pl.BlockSpec(memory_space=pl.ANY)
DATAFLOW = pltpu.SideEffectType.DATAFLOW_SIDE_EFFECTING


def _in_hbm(v):
    return pltpu.with_memory_space_constraint(v, pltpu.HBM)


def _slot(p):
    return 4 * p[0] + 2 * p[1] + p[2]


def _gather_peers():
    x, y, c, chips = _chip_slots()
    return (x, y, c), [(x, y, 1 - c)] + [(*chip, c) for chip in chips]


def gather_start(groups, after, *, name):
    flat = [s for g in groups for s in g]
    n, n_g = len(flat), len(groups)
    where = [(gi, ti) for gi, g in enumerate(groups) for ti in range(len(g))]

    def body(*refs):
        src, land = refs[:n], refs[n:2 * n]
        sems = refs[2 * n + 1:2 * n + 1 + 2 * n_g]
        me, peers = _gather_peers()
        for t in range(n):
            gi, ti = where[t]
            for k, to in enumerate(peers):
                pltpu.make_async_remote_copy(
                    src_ref=src[t], dst_ref=land[t].at[_slot(me)], send_sem=sems[2 * gi].at[4 * ti + k],
                    recv_sem=sems[2 * gi + 1].at[4 * ti + k], device_id=to, device_id_type=MESH).start()

    out_shape = []
    for g in groups:
        out_shape += [pltpu.SemaphoreType.DMA((4 * len(g),)), pltpu.SemaphoreType.DMA((4 * len(g),))]
    out_shape += [pltpu.HBM(s.shape, s.dtype) for s in flat]
    out_shape += [pltpu.HBM((N_DEV,) + s.shape, s.dtype) for s in flat]
    aliases = {t: 2 * n_g + t for t in range(n)}
    aliases.update({n + t: 2 * n_g + n + t for t in range(n)})
    res = pl.pallas_call(
        body, name=name, out_shape=out_shape, in_specs=[HBM_SPEC] * (2 * n) + [ANY_SPEC],
        out_specs=[SEM_SPEC] * (2 * n_g) + [HBM_SPEC] * (2 * n), input_output_aliases=aliases,
        compiler_params=pltpu.CompilerParams(has_side_effects=DATAFLOW),
    )(*[_in_hbm(s) for s in flat], *[_in_hbm(lax.empty((N_DEV,) + s.shape, s.dtype)) for s in flat], after)
    out, off = [], 0
    for gi, g in enumerate(groups):
        k = len(g)
        out.append((res[2 * gi], res[2 * gi + 1], res[2 * n_g + off:2 * n_g + off + k],
                    res[2 * n_g + n + off:2 * n_g + n + off + k]))
        off += k
    return out


def gather_wait(started, after, *, name):
    send_sems, recv_sems, srcs, lands = started
    n = len(srcs)

    def body(*refs):
        src, land = refs[:n], refs[n:2 * n]
        send, recv = refs[2 * n], refs[2 * n + 1]
        _, peers = _gather_peers()
        for t in range(n):
            for k, frm in enumerate(peers):
                cp = pltpu.make_async_remote_copy(
                    src_ref=src[t], dst_ref=land[t].at[_slot(frm)], send_sem=send.at[4 * t + k],
                    recv_sem=recv.at[4 * t + k],
                    device_id=frm, device_id_type=MESH)
                cp.wait_send()
                cp.wait_recv()

    res = pl.pallas_call(
        body, name=name,
        out_shape=[pltpu.HBM(s.shape, s.dtype) for s in srcs] + [pltpu.HBM(l.shape, l.dtype) for l in lands],
        in_specs=[HBM_SPEC] * (2 * n) + [SEM_SPEC, SEM_SPEC, ANY_SPEC], out_specs=[HBM_SPEC] * (2 * n),
        input_output_aliases={t: t for t in range(2 * n)},
        compiler_params=pltpu.CompilerParams(has_side_effects=DATAFLOW),
    )(*srcs, *lands, send_sems, recv_sems, after)
    return res[:n], res[n:]


def place_own(src, land, *, name):
    R, C = src.shape
    tr = _tile(R, 512, 16)
    x, y, c, _ = _chip_slots()
    idx = jnp.stack([4 * x + 2 * y + c]).astype(jnp.int32)

    def body(idx_ref, s_ref, land_ref, o_ref):
        o_ref[...] = s_ref[...]

    return _pcall(body, name=name, grid=(R // tr,),
                  ins=[(src, (tr, C), lambda i, s: (i, 0)), (land, None, None)],
                  outs=[(land.shape, land.dtype, (None, tr, C), lambda i, s: (s[0], i, 0))],
                  prefetch=idx, aliases={2: 0}, semantics=("parallel",))[0]


def gather_finish(srcs, lands, *, name):
    n = len(srcs)

    def body(*refs):
        land = refs[n:2 * n]
        send_sems, recv_sems = refs[2 * n:]
        x, y, c, chips = _chip_slots()
        me, sibling = (x, y, c), (x, y, 1 - c)

        def copy(t, j, block, to):
            return pltpu.make_async_remote_copy(
                src_ref=land[t].at[_slot(block)], dst_ref=land[t].at[_slot(block)], send_sem=send_sems.at[t, j],
                recv_sem=recv_sems.at[t, j], device_id=to, device_id_type=MESH)

        sends = [copy(t, j, (*chip, c), sibling) for t in range(n) for j, chip in enumerate(chips)]
        for cp in sends:
            cp.start()
        for t in range(n):
            for j, chip in enumerate(chips):
                copy(t, j, (*chip, 1 - c), me).wait_recv()
        for cp in sends:
            cp.wait_send()

    passed = pl.pallas_call(
        body, name=name, out_shape=[jax.ShapeDtypeStruct(l.shape, l.dtype) for l in lands],
        in_specs=[ANY_SPEC] * n, out_specs=[ANY_SPEC] * n,
        input_output_aliases={t: t for t in range(n)},
        scratch_shapes=[pltpu.SemaphoreType.DMA((n, 3)), pltpu.SemaphoreType.DMA((n, 3))],
    )(*lands)
    return [place_own(s, l, name=f"{name}_own{t}") for t, (s, l) in enumerate(zip(srcs, passed))]


def chips_start(pairs, *, name):
    n = len(pairs)

    def body(*refs):
        src, land = refs[:n], refs[n:2 * n]
        send, recv = refs[2 * n], refs[2 * n + 1]
        token = refs[-1]
        x, y, c, chips = _chip_slots()
        for t in range(n):
            for j, chip in enumerate(chips):
                pltpu.make_async_remote_copy(
                    src_ref=src[t].at[j], dst_ref=land[t].at[j], send_sem=send.at[3 * t + j],
                    recv_sem=recv.at[3 * t + j], device_id=(*chip, c), device_id_type=MESH).start()
        token[...] = jnp.zeros_like(token)

    res = pl.pallas_call(
        body, name=name,
        out_shape=[pltpu.SemaphoreType.DMA((3 * n,)), pltpu.SemaphoreType.DMA((3 * n,))]
        + [pltpu.HBM(p.shape, p.dtype) for p in pairs] * 2 + [jax.ShapeDtypeStruct((8, LANES), F32)],
        in_specs=[HBM_SPEC] * (2 * n),
        out_specs=[SEM_SPEC, SEM_SPEC] + [HBM_SPEC] * (2 * n) + [pl.BlockSpec(memory_space=pltpu.VMEM)],
        input_output_aliases={t: 2 + t for t in range(2 * n)},
        compiler_params=pltpu.CompilerParams(has_side_effects=DATAFLOW),
    )(*[_in_hbm(p) for p in pairs], *[_in_hbm(lax.empty(p.shape, p.dtype)) for p in pairs])
    return res[0], res[1], res[2:2 + n], res[2 + n:2 + 2 * n], res[-1]


def chips_wait(started, after, *, name):
    send_sems, recv_sems, srcs, lands, _ = started
    n = len(srcs)

    def body(*refs):
        src, land = refs[:n], refs[n:2 * n]
        send, recv = refs[2 * n], refs[2 * n + 1]
        x, y, c, chips = _chip_slots()
        for t in range(n):
            for j, chip in enumerate(chips):
                cp = pltpu.make_async_remote_copy(
                    src_ref=src[t].at[j], dst_ref=land[t].at[j], send_sem=send.at[3 * t + j],
                    recv_sem=recv.at[3 * t + j], device_id=(*chip, c), device_id_type=MESH)
                cp.wait_send()
                cp.wait_recv()

    res = pl.pallas_call(
        body, name=name, out_shape=[pltpu.HBM(s.shape, s.dtype) for s in srcs] * 2,
        in_specs=[HBM_SPEC] * (2 * n) + [SEM_SPEC, SEM_SPEC, ANY_SPEC], out_specs=[HBM_SPEC] * (2 * n),
        input_output_aliases={t: t for t in range(2 * n)},
        compiler_params=pltpu.CompilerParams(has_side_effects=DATAFLOW),
    )(*srcs, *lands, send_sems, recv_sems, after)
    return res[n:]


def _sibling_copies(src, land, send, recv, n):
    x, y, c, _ = _chip_slots()
    return [pltpu.make_async_remote_copy(
        src_ref=src[t].at[4 * (q // 2) + 2 * (q % 2) + (1 - c)], dst_ref=land[t].at[q], send_sem=send.at[4 * t + q],
        recv_sem=recv.at[4 * t + q], device_id=(x, y, 1 - c), device_id_type=MESH)
        for t in range(n) for q in range(4)]


def sibling_start(gs, *, name):
    n = len(gs)

    def body(*refs):
        for cp in _sibling_copies(refs[:n], refs[n:2 * n], refs[2 * n], refs[2 * n + 1], n):
            cp.start()
        refs[-1][...] = jnp.zeros_like(refs[-1])

    lands = [lax.empty((4,) + g.shape[1:], g.dtype) for g in gs]
    res = pl.pallas_call(
        body, name=name,
        out_shape=[pltpu.SemaphoreType.DMA((4 * n,)), pltpu.SemaphoreType.DMA((4 * n,))]
        + [pltpu.HBM(g.shape, g.dtype) for g in gs] + [pltpu.HBM(l.shape, l.dtype) for l in lands]
        + [jax.ShapeDtypeStruct((8, LANES), F32)],
        in_specs=[HBM_SPEC] * (2 * n),
        out_specs=[SEM_SPEC, SEM_SPEC] + [HBM_SPEC] * (2 * n) + [pl.BlockSpec(memory_space=pltpu.VMEM)],
        input_output_aliases={t: 2 + t for t in range(2 * n)},
        compiler_params=pltpu.CompilerParams(has_side_effects=DATAFLOW),
    )(*[_in_hbm(g) for g in gs], *[_in_hbm(l) for l in lands])
    return res[0], res[1], res[2:2 + n], res[2 + n:2 + 2 * n], res[-1]


def sibling_wait(started, after, *, name):
    send_sems, recv_sems, srcs, lands, _ = started
    n = len(srcs)

    def body(*refs):
        for cp in _sibling_copies(refs[:n], refs[n:2 * n], refs[2 * n], refs[2 * n + 1], n):
            cp.wait_send()
            cp.wait_recv()

    res = pl.pallas_call(
        body, name=name,
        out_shape=[pltpu.HBM(s.shape, s.dtype) for s in srcs] + [pltpu.HBM(l.shape, l.dtype) for l in lands],
        in_specs=[HBM_SPEC] * (2 * n) + [SEM_SPEC, SEM_SPEC, ANY_SPEC], out_specs=[HBM_SPEC] * (2 * n),
        input_output_aliases={t: t for t in range(2 * n)},
        compiler_params=pltpu.CompilerParams(has_side_effects=DATAFLOW),
    )(*srcs, *lands, send_sems, recv_sems, after)
    return res[:n], res[n:]


def all_gather_vmem(x_shard, *, name, after=None):
    m_per, n = x_shard.shape
    n_after = 0 if after is None else 1

    def body(x_ref, *rest):
        out_ref, send_sems, recv_sems, local_sem = rest[n_after:]
        x, y, c, chips = _chip_slots()
        me, sibling = (x, y, c), (x, y, 1 - c)

        def rows(px, py, pc):
            return out_ref.at[pl.ds((4 * px + 2 * py + pc) * m_per, m_per), :]

        def copy(k, block, to, src=None):
            return pltpu.make_async_remote_copy(
                src_ref=rows(*block) if src is None else src, dst_ref=rows(*block),
                send_sem=send_sems.at[k], recv_sem=recv_sems.at[k], device_id=to, device_id_type=MESH)

        mine = pltpu.make_async_copy(x_ref, rows(*me), local_sem)
        mine.start()
        first = [copy(0, me, sibling, src=x_ref)]
        first += [copy(1 + j, me, (*chip, c), src=x_ref) for j, chip in enumerate(chips)]
        for cp in first:
            cp.start()
        passed = [copy(4 + j, (*chip, c), sibling) for j, chip in enumerate(chips)]
        for j, chip in enumerate(chips):
            copy(1 + j, (*chip, c), me).wait_recv()
            passed[j].start()
        copy(0, sibling, me).wait_recv()
        for j, chip in enumerate(chips):
            copy(4 + j, (*chip, 1 - c), me).wait_recv()
        for cp in first + passed:
            cp.wait_send()
        mine.wait()

    vmem = pl.BlockSpec(memory_space=pltpu.VMEM)
    return pl.pallas_call(
        body, name=name, out_shape=jax.ShapeDtypeStruct((N_DEV * m_per, n), x_shard.dtype),
        in_specs=[vmem] + [ANY_SPEC] * n_after, out_specs=vmem,
        scratch_shapes=[pltpu.SemaphoreType.DMA((7,)), pltpu.SemaphoreType.DMA((7,)), pltpu.SemaphoreType.DMA],
        compiler_params=pltpu.CompilerParams(vmem_limit_bytes=int(min(
            VMEM_LIMIT_CAP, 2 * (N_DEV + 1) * m_per * n * x_shard.dtype.itemsize + 16 * 2 ** 20))),
    )(x_shard, *([] if after is None else [after]))


def _rope_slab(cols):
    z = jnp.zeros(cols.shape[:-1] + (HALF_ROPE,), cols.dtype)
    return jnp.concatenate([cols[..., :HALF_ROPE], z, cols[..., HALF_ROPE:], z], axis=-1)


def _rope_unslab(slab):
    return jnp.concatenate([slab[..., :HALF_ROPE], slab[..., 2 * HALF_ROPE:3 * HALF_ROPE]], axis=-1)


def _pack_w_in(w_g):
    s, d, c = w_g.shape
    w = jnp.transpose(w_g, (1, 0, 2)).reshape(d, s * c)
    c1, c2, c3 = Q_LORA, Q_LORA + KV_LORA, Q_LORA + KV_LORA + QK_ROPE
    return jnp.concatenate([w[:, :c2], w[:, c3:], _rope_slab(w[:, c2:c3])], axis=-1)


def _unpack_w_in_grad(dw):
    d = dw.shape[0]
    c2 = Q_LORA + KV_LORA
    uv = 2 * SGU_OUT
    g = jnp.concatenate([dw[:, :c2], _rope_unslab(dw[:, c2 + uv:]), dw[:, c2:c2 + uv]], axis=-1)
    return jnp.transpose(g.reshape(d, N_DEV, g.shape[1] // N_DEV), (1, 0, 2))


def _rope_tables(positions):
    inv_freq = ROPE_BASE ** (-jnp.arange(0, QK_ROPE, 2, dtype=F32) / QK_ROPE)
    ang = positions.astype(F32)[:, None] * inv_freq
    cos, sin = jnp.cos(ang), jnp.sin(ang)
    z = jnp.zeros_like(cos)
    return jnp.concatenate([cos, z, cos, z], axis=-1), jnp.concatenate([-sin, z, sin, z], axis=-1)


def _mlp_up(x, gain, w1, tag):
    hn = rms_fwd(x, gain, name=f"mlp{tag}_norm")

    def act_epi(acc):
        a = jnp.maximum(acc, 0.0)
        return a, a * a

    T = x.shape[0]
    F = w1.shape[0] * w1.shape[2]
    a, act = mm(hn, w1, name=f"mlp{tag}_up", outs=[((T, F), BF, None), ((T, F), BF, None)], epi=act_epi)
    return hn, a, act


def _mlp_down(x, act, w2, tag):
    bm = _tile(x.shape[0], MM_TILE)
    bn = _tile(x.shape[1], MM_TILE)
    return mm(act, w2, name=f"mlp{tag}_down", out=(x.shape, F32), bm=bm, bn=bn,
              epi=lambda acc, r: (acc + r[...],), epi_ins=[(x, (bm, bn), lambda i, j, k: (i, j))])


def _mlp_bwd_weights(w1, w2, saved, dxb, tag):
    hn, a, act = saved
    T, D = dxb.shape
    F = a.shape[1]
    bm = _tile(T, MM_TILE)
    bn = _tile(F, min(MM_TILE, w1.shape[2]))
    dhid = mm(dxb, w2, tb=True, name=f"mlp{tag}_dhid", out=((T, F), BF), bm=bm, bn=bn,
              epi=lambda acc, a_ref: (2.0 * a_ref[...].astype(F32) * acc,),
              epi_ins=[(a, (bm, bn), lambda i, j, k: (i, j))])
    dw2 = mm(act, dxb, ta=True, name=f"mlp{tag}_dw2", out=((F, D), BF))
    dw1 = mm(hn, dhid, ta=True, name=f"mlp{tag}_dw1", out=(w1.shape, BF))
    return dhid, dw1, dw2.reshape(N_DEV, F // N_DEV, D)


def _reduce_begin(grads, tag):
    return sibling_start(grads, name=f"reduce_sibling_start_{tag}")


def _reduce_continue(sib, after, tag):
    grads, a_bufs = sibling_wait(sib, after, name=f"reduce_sibling_wait_{tag}")
    pairs = [pair_sum(g, a, name=f"pair_sum_{tag}{t}") for t, (g, a) in enumerate(zip(grads, a_bufs))]
    return grads, a_bufs, chips_start(pairs, name=f"reduce_chips_start_{tag}")


def _mlp_bwd_input(x_in, gain, w1, dhid, dx, tag, sib):
    dhn = mm(dhid, w1, tb=True, name=f"mlp{tag}_dhn", out=(x_in.shape, F32), deps=[sib[-1]])
    reduce_state = _reduce_continue(sib, dhn, f"r_mlp{tag}")
    return rms_bwd(x_in, gain, dhn, dres=dx, name=f"mlp{tag}_norm_bwd", deps=[reduce_state[2][-1]]), reduce_state


def kernel(x, positions, e_norm_mix, e_w_in, e_q_norm, e_w_uq, e_kv_norm, e_w_ukv, e_v_norm, e_sgu_w, e_sgu_b, e_mla_out_norm, e_sgu_out_norm, e_w_out, o_norm_mix, o_w_in, o_conv_w, o_w_out, mlp_norm, mlp_w1, mlp_w2, final_norm, loss_target, m_e_norm_mix, m_e_w_in, m_e_q_norm, m_e_w_uq, m_e_kv_norm, m_e_w_ukv, m_e_v_norm, m_e_sgu_w, m_e_sgu_b, m_e_mla_out_norm, m_e_sgu_out_norm, m_e_w_out, m_o_norm_mix, m_o_w_in, m_o_conv_w, m_o_w_out, m_mlp_norm, m_mlp_w1, m_mlp_w2, m_final_norm, v_e_norm_mix, v_e_w_in, v_e_q_norm, v_e_w_uq, v_e_kv_norm, v_e_w_ukv, v_e_v_norm, v_e_sgu_w, v_e_sgu_b, v_e_mla_out_norm, v_e_sgu_out_norm, v_e_w_out, v_o_norm_mix, v_o_w_in, v_o_conv_w, v_o_w_out, v_mlp_norm, v_mlp_w1, v_mlp_w2, v_final_norm):
    T, D = x.shape[1], x.shape[2]
    d_shard = o_norm_mix.shape[1]
    x0 = x[0]
    target = loss_target[0]
    me = 4 * lax.axis_index("x") + 2 * lax.axis_index("y") + lax.axis_index("c")

    bf = lambda s: s.astype(BF)
    gather_groups = [[bf(e_w_in[0]), bf(e_w_uq[0]), bf(e_w_ukv[0])], [bf(e_w_out[0]), bf(mlp_w1[0])],
                     [bf(mlp_w2[0]), bf(o_w_in[0])], [bf(o_w_out[0]), bf(mlp_w1[1])], [bf(mlp_w2[1])]]
    small_rows = jnp.concatenate([o_norm_mix, o_conv_w[0], jnp.zeros((4, d_shard), F32)], axis=0)
    small_flat = all_gather_vmem(small_rows, name="gather_small")
    started = gather_start(gather_groups, small_flat, name="gather_start")

    def gathered(gi, after):
        srcs, lands = gather_wait(started[gi], after, name=f"gather_wait{gi}")
        return gather_finish(srcs, lands, name=f"gather_finish{gi}")

    small_g = small_flat.reshape(N_DEV, 8, d_shard)
    o_norm_full = small_g[:, 0, :].reshape(1, D)
    conv_w_full = jnp.transpose(small_g[:, 1:4, :], (1, 0, 2)).reshape(3, D)
    w_tril = jnp.tril(e_sgu_w[0])
    w_tril_b = w_tril.astype(BF)
    w_tril_tb = jnp.swapaxes(w_tril, 1, 2).astype(BF)
    b_full = jnp.repeat(e_sgu_b[0].T, CH, axis=1)
    v_gain = e_v_norm[0].reshape(1, SGU_OUT)
    cos_t, sin_t = _rope_tables(positions[0])
    mlp_gain = [mlp_norm[0:1], mlp_norm[1:2]]
    final_gain = final_norm.reshape(1, D)

    h0 = rms_fwd(x0, e_norm_mix, name="e_norm")
    g_w_in, g_w_uq, w_ukv = gathered(0, h0)
    w_in_e = _pack_w_in(g_w_in)
    w_uq = jnp.concatenate([g_w_uq[..., :QK_NOPE], _rope_slab(g_w_uq[..., QK_NOPE:])], axis=-1)
    proj = mm(h0, w_in_e, name="e_in", out=((T, w_in_e.shape[1]), F32), bn=_tile(w_in_e.shape[1], 640))
    qn, kvn, krope = mla_prep(proj, e_q_norm, e_kv_norm, cos_t, sin_t, name="mla_prep")
    bm = _tile(T, MM_TILE)

    def q_epi(acc, cos_ref, sin_ref):
        return (jnp.concatenate([acc[:, :QK_NOPE], _rope_fwd(acc[:, QK_NOPE:], cos_ref[...], sin_ref[...])], axis=-1),)

    q = mm(qn, w_uq, name="mla_q", out=((T, HEADS * HEAD_PAD), BF), bm=bm, bn=HEAD_PAD, epi=q_epi,
           epi_ins=[(cos_t, (bm, LANES), lambda i, j, k: (i, 0)), (sin_t, (bm, LANES), lambda i, j, k: (i, 0))])

    def kv_epi(acc, kr_ref):
        return jnp.concatenate([acc[:, :QK_NOPE].astype(BF), kr_ref[...]], axis=-1), acc[:, QK_NOPE:]

    k, v = mm(kvn, w_ukv, name="mla_kv", bm=bm, bn=HEAD_PAD, epi=kv_epi,
              outs=[((T, HEADS * HEAD_PAD), BF, HEAD_PAD), ((T, MLA_OUT), BF, V_HEAD)],
              epi_ins=[(krope, (bm, LANES), lambda i, j, k: (i, 0))])
    attn, attn_lse = attn_fwd(q, k, v, name="attn_fwd")
    mixed = mix_fwd(attn, proj, e_mla_out_norm, e_sgu_out_norm, v_gain, w_tril_b, b_full, name="mix_fwd")
    bn = _tile(D, MM_TILE)
    g_w_out_e, w1_0 = gathered(1, mixed)
    w_out_e = g_w_out_e.reshape(-1, D)
    x1 = mm(mixed, w_out_e, name="e_out", out=((T, D), F32), bm=bm, bn=bn,
            epi=lambda acc, r: (acc + r[...],), epi_ins=[(x0, (bm, bn), lambda i, j, k: (i, j))])
    hn0, a0, act0 = _mlp_up(x1, mlp_gain[0], w1_0, 0)
    g_w2_0, g_w_in_o = gathered(2, act0)
    w2_0 = g_w2_0.reshape(-1, D)
    x2 = _mlp_down(x1, act0, w2_0, 0)
    ho = rms_fwd(x2, o_norm_full, name="o_norm")
    proj_o = mm(ho, g_w_in_o, name="o_in", out=((T, 3 * D), F32))
    gated = conv_fwd(proj_o, conv_w_full, name="conv_fwd")
    g_w_out_o, w1_1 = gathered(3, gated)
    w_out_o = g_w_out_o.reshape(-1, D)
    x3 = mm(gated, w_out_o, name="o_out", out=((T, D), F32), bm=bm, bn=bn,
            epi=lambda acc, r: (acc + r[...],), epi_ins=[(x2, (bm, bn), lambda i, j, k: (i, j))])
    hn1, a1, act1 = _mlp_up(x3, mlp_gain[1], w1_1, 1)
    (g_w2_1,) = gathered(4, act1)
    w2_1 = g_w2_1.reshape(-1, D)
    x4 = _mlp_down(x3, act1, w2_1, 1)
    w1, w2 = [w1_0, w1_1], [w2_0, w2_1]
    mlp0_saved, mlp1_saved = (hn0, a0, act0), (hn1, a1, act1)

    dx4, dx4b, d_final, loss_part = loss_bwd(x4, final_gain, target, name="loss_bwd")
    loss = lax.psum(loss_part[0, 0], AXES)

    dhid1, dw1_1, dw2_1 = _mlp_bwd_weights(w1[1], w2[1], mlp1_saved, dx4b, 1)
    sib_r0 = _reduce_begin([dw1_1, dw2_1], "r0")
    (dx3, dx3b, d_mlp1), (grads_r0, a_r0, st_r0) = _mlp_bwd_input(x3, mlp_gain[1], w1[1], dhid1, dx4, 1, sib_r0)

    dgated = mm(dx3b, w_out_o, tb=True, name="o_out_dx", out=((T, D), F32))
    dw_out_o = mm(gated, dx3b, ta=True, name="o_out_dw", out=((D, D), BF))
    dproj_o, dconv_full = conv_bwd(dgated, proj_o, conv_w_full, name="conv_bwd")
    dw_in_o = mm(ho, dproj_o, ta=True, name="o_in_dw", out=(g_w_in_o.shape, BF))
    sib_r1 = _reduce_begin([dw_out_o.reshape(g_w_out_o.shape), dw_in_o], "r1")
    dho = mm(dproj_o, g_w_in_o, tb=True, name="o_in_dx", out=((T, D), F32), deps=[sib_r1[-1]])
    grads_r1, a_r1, st_r1 = _reduce_continue(sib_r1, dho, "r1")
    dx2, dx2b, d_onorm_full = rms_bwd(x2, o_norm_full, dho, dres=dx3, name="o_norm_bwd", deps=[st_r1[-1]])

    dhid0, dw1_0, dw2_0 = _mlp_bwd_weights(w1[0], w2[0], mlp0_saved, dx2b, 0)
    sib_r2 = _reduce_begin([dw1_0, dw2_0], "r2")
    (dx1, dx1b, d_mlp0), (grads_r2, a_r2, st_r2) = _mlp_bwd_input(x1, mlp_gain[0], w1[0], dhid0, dx2, 0, sib_r2)
    b_r0 = chips_wait(st_r0, dx1b, name="reduce_chips_wait_r0")

    dmixed = mm(dx1b, w_out_e, tb=True, name="e_out_dx", out=((T, MLA_OUT + SGU_OUT), F32))
    dw_out_e = mm(mixed, dx1b, ta=True, name="e_out_dw", out=(w_out_e.shape, BF))
    (dattn, duv, d_mla_out, d_sgu_out, d_vgain, d_sgu_w, d_b_full) = mix_bwd(
        dmixed, attn, proj, e_mla_out_norm, e_sgu_out_norm, v_gain, w_tril_b, w_tril_tb, b_full, name="mix_bwd")
    b_r1 = chips_wait(st_r1, dattn, name="reduce_chips_wait_r1")
    dq, dk, dv = attn_bwd(q, k, v, attn, attn_lse, dattn, name="attn_bwd")
    dq_lin, dkv_lin, dkr = mla_bwd_prep(dq, dk, dv, cos_t, sin_t, name="mla_bwd_prep")
    dw_uq_pad = mm(qn, dq_lin, ta=True, name="mla_q_dw", out=(w_uq.shape, F32))
    dw_ukv = mm(kvn, dkv_lin, ta=True, name="mla_kv_dw", out=(w_ukv.shape, BF))
    dw_uq = jnp.concatenate([dw_uq_pad[..., :QK_NOPE], _rope_unslab(dw_uq_pad[..., QK_NOPE:])], axis=-1).astype(BF)
    sib_r2b = _reduce_begin([dw_out_e.reshape(g_w_out_e.shape), dw_uq, dw_ukv], "r2b")
    dqn = mm(dq_lin, w_uq, tb=True, name="mla_q_dx", out=((T, Q_LORA), F32), deps=[sib_r2b[-1]])
    dkvn = mm(dkv_lin, w_ukv, tb=True, name="mla_kv_dx", out=((T, KV_LORA), F32), deps=[sib_r2b[-1]])
    grads_r2b, a_r2b, st_r2b = _reduce_continue(sib_r2b, dkvn, "r2b")
    dcq, d_qnorm = rms_bwd(proj, e_q_norm, dqn, col_block=0, want_f32=False, name="q_norm_bwd", deps=[st_r2b[-1]])
    dckv, d_kvnorm = rms_bwd(proj, e_kv_norm, dkvn, col_block=1, want_f32=False, name="kv_norm_bwd")
    dproj = jnp.concatenate([dcq, dckv, duv, dkr], axis=-1)
    dw_in_e_pad = mm(h0, dproj, ta=True, name="e_in_dw", out=(w_in_e.shape, F32), bn=_tile(w_in_e.shape[1], 640))
    dw_in_e = _unpack_w_in_grad(dw_in_e_pad).astype(BF)
    sib_r3 = _reduce_begin([dw_in_e], "r3")
    dh0 = mm(dproj, w_in_e, tb=True, name="e_in_dx", out=((T, D), F32), deps=[sib_r3[-1]])
    grads_r3, a_r3, st_r3 = _reduce_continue(sib_r3, dh0, "r3")
    tok_r3 = st_r3[-1]
    grad_x, d_enorm = rms_bwd(x0, e_norm_mix, dh0, dres=dx1, want_bf=False, name="e_norm_bwd", deps=[tok_r3])
    b_r2 = chips_wait(st_r2, grad_x, name="reduce_chips_wait_r2")

    def finish(grads, a_bufs, b_bufs, t, w, m, v, layer=0, prev=None, tag="", deps=()):
        return reduce_adam(grads[t], a_bufs[t], b_bufs[t], w, m, v, layer, prev, name=f"adam_{tag}", deps=deps)

    r_w1 = finish(grads_r0, a_r0, b_r0, 0, mlp_w1, m_mlp_w1, v_mlp_w1, 1, None, tag="w1_l1")
    r_w2 = finish(grads_r0, a_r0, b_r0, 1, mlp_w2, m_mlp_w2, v_mlp_w2, 1, None, tag="w2_l1")
    r_w_out_o = finish(grads_r1, a_r1, b_r1, 0, o_w_out, m_o_w_out, v_o_w_out, tag="o_w_out")
    r_w_in_o = finish(grads_r1, a_r1, b_r1, 1, o_w_in, m_o_w_in, v_o_w_in, tag="o_w_in")
    r_w1 = finish(grads_r2, a_r2, b_r2, 0, mlp_w1, m_mlp_w1, v_mlp_w1, 0, r_w1, tag="w1_l0", deps=[tok_r3])
    r_w2 = finish(grads_r2, a_r2, b_r2, 1, mlp_w2, m_mlp_w2, v_mlp_w2, 0, r_w2, tag="w2_l0", deps=[r_w1[1]])
    b_r2b = chips_wait(st_r2b, r_w2[1], name="reduce_chips_wait_r2b")
    r_w_out_e = finish(grads_r2b, a_r2b, b_r2b, 0, e_w_out, m_e_w_out, v_e_w_out, tag="e_w_out")
    r_w_uq = finish(grads_r2b, a_r2b, b_r2b, 1, e_w_uq, m_e_w_uq, v_e_w_uq, tag="e_w_uq")
    r_w_ukv = finish(grads_r2b, a_r2b, b_r2b, 2, e_w_ukv, m_e_w_ukv, v_e_w_ukv, tag="e_w_ukv")
    b_r3 = chips_wait(st_r3, r_w_out_e[1], name="reduce_chips_wait_r3")
    r_w_in = finish(grads_r3, a_r3, b_r3, 0, e_w_in, m_e_w_in, v_e_w_in, tag="e_w_in")

    d_sgu_b = jnp.transpose(d_b_full[:, ::CH])
    d_sgu_w_tril = jnp.tril(d_sgu_w)
    rep = [("e_norm_mix", e_norm_mix, m_e_norm_mix, v_e_norm_mix, d_enorm),
           ("e_q_norm", e_q_norm, m_e_q_norm, v_e_q_norm, d_qnorm),
           ("e_kv_norm", e_kv_norm, m_e_kv_norm, v_e_kv_norm, d_kvnorm),
           ("e_v_norm", e_v_norm, m_e_v_norm, v_e_v_norm, d_vgain),
           ("e_sgu_w", e_sgu_w, m_e_sgu_w, v_e_sgu_w, d_sgu_w_tril),
           ("e_sgu_b", e_sgu_b, m_e_sgu_b, v_e_sgu_b, d_sgu_b),
           ("e_mla_out_norm", e_mla_out_norm, m_e_mla_out_norm, v_e_mla_out_norm, d_mla_out),
           ("e_sgu_out_norm", e_sgu_out_norm, m_e_sgu_out_norm, v_e_sgu_out_norm, d_sgu_out),
           ("mlp_norm", mlp_norm, m_mlp_norm, v_mlp_norm, jnp.concatenate([d_mlp0, d_mlp1], axis=0)),
           ("final_norm", final_norm, m_final_norm, v_final_norm, d_final)]
    sizes = [int(np.prod(r[1].shape)) for r in rep]
    n_rep = sum(sizes)
    n_all = n_rep + 4 * D
    width = -(-n_all // (8 * LANES)) * LANES
    pad = 8 * width - n_all
    flat = jnp.concatenate([r[4].reshape(-1) for r in rep]
                           + [d_onorm_full.reshape(-1), dconv_full.reshape(-1), jnp.zeros((pad,), F32)])
    summed = sum_rows8(all_gather_vmem(flat.reshape(8, width), name="gather_small_grads", after=b_r3[0]), 8,
                       name="sum_small_grads").reshape(-1)

    def pack_rep(i):
        return jnp.concatenate([r[i].reshape(-1) for r in rep]).reshape(n_rep // LANES, LANES)

    g_rep = summed[:n_rep].reshape(n_rep // LANES, LANES)
    d_rep, nm_rep, nv_rep = adam_flat(g_rep, pack_rep(1), pack_rep(2), pack_rep(3), name="adam_replicated")

    def unpack_rep(flat2d):
        out, off = {}, 0
        f = flat2d.reshape(-1)
        for r, n in zip(rep, sizes):
            out[r[0]] = f[off:off + n].reshape(r[1].shape)
            off += n
        return out

    small = {"grad": unpack_rep(g_rep), "delta": unpack_rep(d_rep), "new_m": unpack_rep(nm_rep),
             "new_v": unpack_rep(nv_rep)}
    g_onorm = lax.dynamic_slice(summed[n_rep:n_rep + D].reshape(1, D), (0, me * d_shard), (1, d_shard))
    g_conv = lax.dynamic_slice(summed[n_rep + D:n_rep + 4 * D].reshape(3, D), (0, me * d_shard), (3, d_shard))

    def pack_sharded(norm_part, conv_part):
        return jnp.concatenate([norm_part, conv_part, jnp.zeros((4, d_shard), F32)], axis=0)

    g_sh = pack_sharded(g_onorm, g_conv)
    d_sh, nm_sh, nv_sh = adam_flat(g_sh, pack_sharded(o_norm_mix, o_conv_w[0]), pack_sharded(m_o_norm_mix, m_o_conv_w[0]),
                                   pack_sharded(v_o_norm_mix, v_o_conv_w[0]), name="adam_sharded_small")
    for kind, arr in (("grad", g_sh), ("delta", d_sh), ("new_m", nm_sh), ("new_v", nv_sh)):
        small[kind]["o_norm_mix"] = arr[0:1]
        small[kind]["o_conv_w"] = arr[1:4][None]

    big = {"e_w_in": r_w_in, "e_w_uq": r_w_uq, "e_w_ukv": r_w_ukv, "e_w_out": r_w_out_e, "o_w_in": r_w_in_o,
           "o_w_out": r_w_out_o, "mlp_w1": r_w1, "mlp_w2": r_w2}
    order = ["e_norm_mix", "e_w_in", "e_q_norm", "e_w_uq", "e_kv_norm", "e_w_ukv", "e_v_norm", "e_sgu_w", "e_sgu_b",
             "e_mla_out_norm", "e_sgu_out_norm", "e_w_out", "o_norm_mix", "o_w_in", "o_conv_w", "o_w_out", "mlp_norm",
             "mlp_w1", "mlp_w2", "final_norm"]
    result = [loss, grad_x[None]]
    for ki, kind in enumerate(("grad", "delta", "new_m", "new_v")):
        for nm in order:
            result.append(big[nm][ki] if nm in big else small[kind][nm])
    return tuple(result)
```

```python
import functools

import numpy as np
import jax
import jax.numpy as jnp
from jax import lax
from jax.experimental import pallas as pl
from jax.experimental.pallas import tpu as pltpu

BF = jnp.bfloat16
F32 = jnp.float32
MESH = pl.DeviceIdType.MESH
AXES = ("x", "y", "c")
N_DEV = 8

EPS = 1e-6
HEADS = 8
Q_LORA = 512
KV_LORA = 512
QK_NOPE = 128
QK_ROPE = 64
HALF_ROPE = QK_ROPE // 2
V_HEAD = 128
HEAD_PAD = 256
ROPE_BASE = 10000.0
GROUPS = 8
CH = 128
CHUNK = 128
SGU_OUT = GROUPS * CH
MLA_OUT = HEADS * V_HEAD
ATTN_SCALE = float((QK_NOPE + QK_ROPE) ** -0.5)

ADAM_LR = 0.001
ADAM_B1 = 0.9
ADAM_B2 = 0.999
ADAM_EPS = 1e-08
ADAM_WD = 0.01
ADAM_STEP = 10
ADAM_C1 = 1.0 - ADAM_B1 ** ADAM_STEP
ADAM_C2 = 1.0 - ADAM_B2 ** ADAM_STEP

V7X_VMEM_BYTES = 64 * 2 ** 20
VMEM_LIMIT_CAP = V7X_VMEM_BYTES - 6 * 2 ** 20
LANES = 128
ROW_TILE = 256
ATTN_TILE = 512
MM_TILE = 1024
MM_K_TILE = 2048
MM_K_BLOCK_MAX = 3072


def _padded_bytes(block, dtype):
    dims = [d for d in block if d is not None]
    if len(dims) >= 1:
        dims[-1] = -(-dims[-1] // LANES) * LANES
    if len(dims) >= 2:
        dims[-2] = -(-dims[-2] // 16) * 16
    return int(np.prod(dims)) * jnp.dtype(dtype).itemsize


def _pcall(body, *, name, grid, ins, outs, scratch=(), semantics=None, aliases=None, prefetch=None, deps=()):
    any_spec = pl.BlockSpec(memory_space=pl.ANY)
    if deps:
        n_lead = len(ins) + (1 if prefetch is not None else 0)
        n_deps = len(deps)
        inner = body

        def body(*refs):
            inner(*refs[:n_lead], *refs[n_lead + n_deps:])

        ins = list(ins) + [(d, None, None) for d in deps]
    in_specs = [any_spec if b is None else pl.BlockSpec(b, m) for _, b, m in ins]
    out_specs = [any_spec if b is None else pl.BlockSpec(b, m) for _, _, b, m in outs]
    out_shape = [pltpu.HBM(s, d) for s, d, _, _ in outs]
    est = 0
    for a, b, _ in ins:
        if b is not None:
            est += 2 * _padded_bytes(b, a.dtype)
    for _, d, b, _ in outs:
        if b is not None:
            est += 2 * _padded_bytes(b, d)
    for s in scratch:
        if hasattr(s, "shape") and hasattr(s, "dtype"):
            est += _padded_bytes(s.shape, s.dtype)
    limit = int(min(VMEM_LIMIT_CAP, est + 16 * 2 ** 20))
    params = pltpu.CompilerParams(
        dimension_semantics=semantics or ("arbitrary",) * len(grid), vmem_limit_bytes=limit)
    args = [pltpu.with_memory_space_constraint(a, pltpu.HBM) for a, _, _ in ins]
    if prefetch is not None:
        grid_spec = pltpu.PrefetchScalarGridSpec(
            num_scalar_prefetch=1, grid=grid, in_specs=in_specs, out_specs=out_specs, scratch_shapes=list(scratch))
        call = pl.pallas_call(body, out_shape=out_shape, grid_spec=grid_spec, name=name, compiler_params=params,
                              input_output_aliases=aliases or {})
        return call(prefetch, *args)
    call = pl.pallas_call(body, out_shape=out_shape, grid=grid, in_specs=in_specs, out_specs=out_specs,
                          scratch_shapes=list(scratch), name=name, compiler_params=params,
                          input_output_aliases=aliases or {})
    return call(*args)


def _tile(dim, pref, quantum=LANES):
    if dim <= pref:
        return dim
    t = (pref // quantum) * quantum
    while t >= quantum:
        if dim % t == 0:
            return t
        t -= quantum
    return dim


def _vshape(arr_shape):
    if len(arr_shape) == 2:
        return tuple(arr_shape)
    s, r, c = arr_shape
    return (r, s * c)


def _vblock(arr_shape, br, bc, rc):
    if len(arr_shape) == 2:
        return (br, bc), (lambda *g: rc(*g))
    _, _, c = arr_shape
    assert c % bc == 0, (arr_shape, bc)
    per = c // bc

    def imap(*g):
        ri, ci = rc(*g)
        return (ci // per, ri, ci % per)

    return (None, br, bc), imap


def _shard_width(*shapes):
    w = None
    for s in shapes:
        if len(s) == 3:
            w = s[2] if w is None else int(np.gcd(w, s[2]))
    return w


def mm(a, b, *, name, ta=False, tb=False, out=None, outs=None, epi=None, epi_ins=(), bm=None, bn=None, bk=None,
       deps=()):
    av, bv = _vshape(a.shape), _vshape(b.shape)
    M, K = (av[1], av[0]) if ta else av
    K2, N = (bv[1], bv[0]) if tb else bv
    assert K == K2, (a.shape, b.shape, ta, tb)
    if outs is None:
        outs = [(out[0], out[1], None)]
    a_sw = _shard_width(a.shape)
    b_sw = _shard_width(b.shape)
    o_sw = _shard_width(*[o[0] for o in outs])
    m_lim = a_sw if (ta and a_sw) else None
    k_lim = [w for w in ((a_sw if not ta else None), (b_sw if tb else None)) if w]
    n_lim = [w for w in ((b_sw if not tb else None), o_sw) if w]
    if bm is None:
        bm = _tile(M, min([MM_TILE] + ([m_lim] if m_lim else [])))
    if bn is None:
        bn = _tile(N, min([MM_TILE] + n_lim))
    k_shards = 0
    if tb and len(b.shape) == 3 and bk is None and not (a_sw and not ta):
        k_shards = 1
        while 2 * k_shards <= b.shape[0] and 2 * k_shards * b_sw <= MM_K_BLOCK_MAX:
            k_shards *= 2
        bk = k_shards * b_sw
    if bk is None:
        bk = K if (K <= 4096 and not k_lim) else _tile(K, min([MM_K_TILE] + k_lim))
    assert M % bm == 0 and N % bn == 0 and K % bk == 0, (name, M, N, K, bm, bn, bk)
    nk = K // bk
    grid = (M // bm, N // bn, nk)
    if ta:
        a_blk, a_map = _vblock(a.shape, bk, bm, lambda i, j, k: (k, i))
    else:
        a_blk, a_map = _vblock(a.shape, bm, bk, lambda i, j, k: (i, k))
    if k_shards:
        b_blk, b_map = (k_shards, bn, b_sw), (lambda i, j, k: (k, j, 0))
    elif tb:
        b_blk, b_map = _vblock(b.shape, bn, bk, lambda i, j, k: (j, k))
    else:
        b_blk, b_map = _vblock(b.shape, bk, bn, lambda i, j, k: (k, j))
    dn = (((0 if ta else 1,), (1 if tb else 0,)), ((), ()))
    ins = [(a, a_blk, a_map), (b, b_blk, b_map)] + list(epi_ins)
    out_list = []
    for shape, dtype, cols in outs:
        cols = cols or bn
        blk, imap = _vblock(shape, bm, cols, lambda i, j, k: (i, j))
        out_list.append((shape, dtype, blk, imap))
    n_e, n_o = len(epi_ins), len(out_list)

    def body(*refs):
        a_ref, b_ref = refs[0], refs[1]
        e_refs = refs[2:2 + n_e]
        o_refs = refs[2 + n_e:2 + n_e + n_o]

        def finish(acc):
            res = epi(acc, *e_refs) if epi is not None else (acc,)
            for o_ref, r in zip(o_refs, res):
                o_ref[...] = r.astype(o_ref.dtype)

        x = a_ref[...].astype(BF)
        y = b_ref[...].astype(BF)
        if k_shards:
            p = None
            for s in range(k_shards):
                part = lax.dot_general(x[:, s * b_sw:(s + 1) * b_sw], y[s], dn, preferred_element_type=F32)
                p = part if p is None else p + part
        else:
            p = lax.dot_general(x, y, dn, preferred_element_type=F32)
        if nk == 1:
            finish(p)
        else:
            acc_ref = refs[-1]
            k = pl.program_id(2)

            @pl.when(k == 0)
            def _():
                acc_ref[...] = p

            @pl.when(k > 0)
            def _():
                acc_ref[...] += p

            @pl.when(k == nk - 1)
            def _():
                finish(acc_ref[...])

    scratch = [pltpu.VMEM((bm, bn), F32)] if nk > 1 else []
    res = _pcall(body, name=name, grid=grid, ins=ins, outs=out_list, scratch=scratch,
                 semantics=("parallel", "parallel", "arbitrary"), deps=deps)
    return res[0] if len(res) == 1 else res


_GELU_K = float(np.sqrt(2.0 / np.pi))
_GELU_C = 0.044715


def _gelu(x):
    t = jnp.tanh(_GELU_K * (x + _GELU_C * (x * x * x)))
    return 0.5 * x * (1.0 + t)


def _gelu_grad(x):
    t = jnp.tanh(_GELU_K * (x + _GELU_C * (x * x * x)))
    return 0.5 * (1.0 + t) + 0.5 * x * (1.0 - t * t) * (_GELU_K * (1.0 + 3.0 * _GELU_C * (x * x)))


def _rstd(x):
    return lax.rsqrt(jnp.mean(x * x, axis=-1, keepdims=True) + EPS)


def _rms_bwd(x, gain, dy):
    r = _rstd(x)
    xh = x * r
    gdy = dy * gain
    dx = r * (gdy - xh * jnp.mean(gdy * xh, axis=-1, keepdims=True))
    return dx, dy * xh


def _rope_fwd(x, cos_t, sin_t):
    return x * cos_t + pltpu.roll(x, 2 * HALF_ROPE, 1) * sin_t


def _rope_bwd(dy, cos_t, sin_t):
    return dy * cos_t + pltpu.roll(dy * sin_t, 2 * HALF_ROPE, 1)


def _acc_rows(ref, val, first):
    s = jnp.sum(val, axis=0, keepdims=True)

    @pl.when(first)
    def _():
        ref[...] = s

    @pl.when(jnp.logical_not(first))
    def _():
        ref[...] += s


def rms_fwd(x, gain, *, name, col_block=0, width=None):
    T = x.shape[0]
    width = width or x.shape[1]
    tm = _tile(T, ROW_TILE, 8)

    def body(x_ref, g_ref, o_ref):
        v = x_ref[...]
        o_ref[...] = (v * _rstd(v) * g_ref[...]).astype(BF)

    return _pcall(body, name=name, grid=(T // tm,),
                  ins=[(x, (tm, width), lambda i: (i, col_block)), (gain, (1, width), lambda i: (0, 0))],
                  outs=[((T, width), BF, (tm, width), lambda i: (i, 0))], semantics=("parallel",))[0]


def rms_bwd(x, gain, dy, *, name, col_block=0, dres=None, want_f32=True, want_bf=True, deps=()):
    T, width = dy.shape
    tm = _tile(T, ROW_TILE, 8)
    has_res = dres is not None

    def body(*refs):
        x_ref, g_ref, dy_ref = refs[:3]
        pos = 3
        res_ref = None
        if has_res:
            res_ref = refs[pos]
            pos += 1
        outs = refs[pos:]
        dx, dg_rows = _rms_bwd(x_ref[...], g_ref[...], dy_ref[...])
        if has_res:
            dx = dx + res_ref[...]
        o = 0
        if want_f32:
            outs[o][...] = dx
            o += 1
        if want_bf:
            outs[o][...] = dx.astype(BF)
            o += 1
        _acc_rows(outs[o], dg_rows, pl.program_id(0) == 0)

    ins = [(x, (tm, width), lambda i: (i, col_block)), (gain, (1, width), lambda i: (0, 0)),
           (dy, (tm, width), lambda i: (i, 0))]
    if has_res:
        ins.append((dres, (tm, width), lambda i: (i, 0)))
    outs = []
    if want_f32:
        outs.append(((T, width), F32, (tm, width), lambda i: (i, 0)))
    if want_bf:
        outs.append(((T, width), BF, (tm, width), lambda i: (i, 0)))
    outs.append(((1, width), F32, (1, width), lambda i: (0, 0)))
    return _pcall(body, name=name, grid=(T // tm,), ins=ins, outs=outs, deps=deps)


def mla_prep(proj, q_norm, kv_norm, cos_t, sin_t, *, name):
    T = proj.shape[0]
    tm = _tile(T, ROW_TILE, 8)
    kr_block = (proj.shape[1] - LANES) // LANES

    def body(cq_ref, ckv_ref, kr_ref, qg_ref, kg_ref, cos_ref, sin_ref, qn_ref, kvn_ref, krope_ref):
        cq = cq_ref[...]
        qn_ref[...] = (cq * _rstd(cq) * qg_ref[...]).astype(BF)
        ckv = ckv_ref[...]
        kvn_ref[...] = (ckv * _rstd(ckv) * kg_ref[...]).astype(BF)
        krope_ref[...] = _rope_fwd(kr_ref[...], cos_ref[...], sin_ref[...]).astype(BF)

    return _pcall(
        body, name=name, grid=(T // tm,),
        ins=[(proj, (tm, Q_LORA), lambda i: (i, 0)), (proj, (tm, KV_LORA), lambda i: (i, 1)),
             (proj, (tm, LANES), lambda i: (i, kr_block)),
             (q_norm, (1, Q_LORA), lambda i: (0, 0)), (kv_norm, (1, KV_LORA), lambda i: (0, 0)),
             (cos_t, (tm, LANES), lambda i: (i, 0)), (sin_t, (tm, LANES), lambda i: (i, 0))],
        outs=[((T, Q_LORA), BF, (tm, Q_LORA), lambda i: (i, 0)), ((T, KV_LORA), BF, (tm, KV_LORA), lambda i: (i, 0)),
              ((T, LANES), BF, (tm, LANES), lambda i: (i, 0))],
        semantics=("parallel",))


def _attn_scores(q, k_blk, diagonal):
    s = lax.dot_general(q, k_blk, (((1,), (1,)), ((), ())), preferred_element_type=F32) * ATTN_SCALE
    if diagonal:
        row = lax.broadcasted_iota(jnp.int32, s.shape, 0)
        col = lax.broadcasted_iota(jnp.int32, s.shape, 1)
        s = jnp.where(col <= row, s, -jnp.inf)
    return s


def attn_fwd(q, k, v, *, name):
    T = q.shape[0]
    tq = _tile(T, ATTN_TILE, 8)

    def body(q_ref, k_ref, v_ref, o_ref, lse_ref):
        i = pl.program_id(1)
        qv = q_ref[...]

        def block(kb, carry, diagonal):
            m, l, acc = carry
            start = pl.multiple_of(kb * tq, tq)
            s = _attn_scores(qv, k_ref[pl.ds(start, tq), :], diagonal)
            m_new = jnp.maximum(m, jnp.max(s, axis=-1, keepdims=True))
            alpha = jnp.exp(m - m_new)
            p = jnp.exp(s - m_new)
            l = alpha * l + jnp.sum(p, axis=-1, keepdims=True)
            acc = alpha * acc + jnp.dot(p.astype(BF), v_ref[pl.ds(start, tq), :], preferred_element_type=F32)
            return m_new, l, acc

        init = (jnp.full((tq, 1), -jnp.inf, F32), jnp.zeros((tq, 1), F32), jnp.zeros((tq, V_HEAD), F32))
        carry = lax.fori_loop(0, i, lambda kb, c: block(kb, c, False), init)
        m, l, acc = block(i, carry, True)
        o_ref[...] = acc / l
        lse_ref[...] = jnp.broadcast_to(m + jnp.log(l), (tq, V_HEAD))

    return _pcall(
        body, name=name, grid=(HEADS, T // tq),
        ins=[(q, (tq, HEAD_PAD), lambda h, i: (i, h)), (k, (T, HEAD_PAD), lambda h, i: (0, h)),
             (v, (T, V_HEAD), lambda h, i: (0, h))],
        outs=[((T, MLA_OUT), F32, (tq, V_HEAD), lambda h, i: (i, h)),
              ((T, MLA_OUT), F32, (tq, V_HEAD), lambda h, i: (i, h))], semantics=("parallel", "parallel"))


def attn_bwd(q, k, v, o, lse, do, *, name):
    T = q.shape[0]
    tq = _tile(T, ATTN_TILE, 8)

    def body(q_ref, k_ref, v_ref, o_ref, lse_ref, do_ref, dq_ref, dk_ref, dv_ref):
        i = pl.program_id(1)

        @pl.when(i == 0)
        def _():
            dk_ref[...] = jnp.zeros_like(dk_ref)
            dv_ref[...] = jnp.zeros_like(dv_ref)

        qv = q_ref[...]
        do_t = do_ref[...]
        lse_v = lse_ref[:, 0:1]
        delta = jnp.sum(do_t.astype(F32) * o_ref[...], axis=-1, keepdims=True)

        def block(kb, dq, diagonal):
            start = pl.multiple_of(kb * tq, tq)
            k_blk = k_ref[pl.ds(start, tq), :]
            v_blk = v_ref[pl.ds(start, tq), :]
            p = jnp.exp(_attn_scores(qv, k_blk, diagonal) - lse_v)
            dp = lax.dot_general(do_t, v_blk, (((1,), (1,)), ((), ())), preferred_element_type=F32)
            ds = (p * (dp - delta) * ATTN_SCALE).astype(BF)
            dk_ref[pl.ds(start, tq), :] += lax.dot_general(ds, qv, (((0,), (0,)), ((), ())), preferred_element_type=F32)
            dv_ref[pl.ds(start, tq), :] += lax.dot_general(p.astype(BF), do_t, (((0,), (0,)), ((), ())),
                                                          preferred_element_type=F32)
            return dq + jnp.dot(ds, k_blk, preferred_element_type=F32)

        dq = lax.fori_loop(0, i, lambda kb, c: block(kb, c, False), jnp.zeros((tq, HEAD_PAD), F32))
        dq_ref[...] = block(i, dq, True)

    return _pcall(
        body, name=name, grid=(HEADS, T // tq),
        ins=[(q, (tq, HEAD_PAD), lambda h, i: (i, h)), (k, (T, HEAD_PAD), lambda h, i: (0, h)),
             (v, (T, V_HEAD), lambda h, i: (0, h)), (o, (tq, V_HEAD), lambda h, i: (i, h)),
             (lse, (tq, V_HEAD), lambda h, i: (i, h)), (do, (tq, V_HEAD), lambda h, i: (i, h))],
        outs=[((T, HEADS * HEAD_PAD), F32, (tq, HEAD_PAD), lambda h, i: (i, h)),
              ((T, HEADS * HEAD_PAD), F32, (T, HEAD_PAD), lambda h, i: (0, h)),
              ((T, MLA_OUT), F32, (T, V_HEAD), lambda h, i: (0, h))],
        semantics=("parallel", "arbitrary"))


def mla_bwd_prep(dq, dk, dv, cos_t, sin_t, *, name):
    T = dq.shape[0]
    tm = _tile(T, ROW_TILE, 8)

    def body(dq_ref, dk_ref, dv_ref, cos_ref, sin_ref, dql_ref, dkvl_ref, dkr_ref):
        cos_v, sin_v = cos_ref[...], sin_ref[...]
        kr = jnp.zeros((tm, LANES), F32)
        for h in range(HEADS):
            lo = h * HEAD_PAD
            dql_ref[:, lo:lo + QK_NOPE] = dq_ref[:, lo:lo + QK_NOPE].astype(BF)
            dql_ref[:, lo + QK_NOPE:lo + HEAD_PAD] = _rope_bwd(
                dq_ref[:, lo + QK_NOPE:lo + HEAD_PAD], cos_v, sin_v).astype(BF)
            dkvl_ref[:, lo:lo + QK_NOPE] = dk_ref[:, lo:lo + QK_NOPE].astype(BF)
            dkvl_ref[:, lo + QK_NOPE:lo + HEAD_PAD] = dv_ref[:, h * V_HEAD:(h + 1) * V_HEAD].astype(BF)
            kr = kr + dk_ref[:, lo + QK_NOPE:lo + HEAD_PAD]
        dkr_ref[...] = _rope_bwd(kr, cos_v, sin_v).astype(BF)

    W = HEADS * HEAD_PAD
    return _pcall(
        body, name=name, grid=(T // tm,),
        ins=[(dq, (tm, W), lambda i: (i, 0)), (dk, (tm, W), lambda i: (i, 0)), (dv, (tm, MLA_OUT), lambda i: (i, 0)),
             (cos_t, (tm, LANES), lambda i: (i, 0)), (sin_t, (tm, LANES), lambda i: (i, 0))],
        outs=[((T, W), BF, (tm, W), lambda i: (i, 0)), ((T, W), BF, (tm, W), lambda i: (i, 0)),
              ((T, LANES), BF, (tm, LANES), lambda i: (i, 0))],
        semantics=("parallel",))


def _group_norm_stats(vg):
    mu = jnp.mean(vg, axis=-1, keepdims=True)
    d = vg - mu
    r = lax.rsqrt(jnp.mean(d * d, axis=-1, keepdims=True) + EPS)
    return d * r, r


def mix_fwd(a, proj, g_mla, g_sgu, v_gain, w_tril, b_full, *, name):
    T = a.shape[0]
    tm = _tile(T, ROW_TILE, CHUNK)
    n_chunk = tm // CHUNK

    def body(a_ref, u_ref, v_ref, gm_ref, gs_ref, vg_ref, w_ref, b_ref, o_ref, s_scr):
        av = a_ref[...]
        o_ref[:, :MLA_OUT] = (av * _rstd(av) * gm_ref[...]).astype(BF)
        for g in range(GROUPS):
            sl = slice(g * CH, (g + 1) * CH)
            vhat, _ = _group_norm_stats(_gelu(v_ref[:, sl]))
            vn = (vhat * vg_ref[:, sl]).astype(BF)
            u = _gelu(u_ref[:, sl])
            for ci in range(n_chunk):
                rs = slice(ci * CHUNK, (ci + 1) * CHUNK)
                y = jnp.dot(w_ref[g], vn[rs], preferred_element_type=F32) + b_ref[:, sl]
                s_scr[rs, sl] = u[rs] * y
        s = s_scr[...]
        o_ref[:, MLA_OUT:] = (s * _rstd(s) * gs_ref[...]).astype(BF)

    return _pcall(
        body, name=name, grid=(T // tm,),
        ins=[(a, (tm, MLA_OUT), lambda i: (i, 0)), (proj, (tm, SGU_OUT), lambda i: (i, 1)),
             (proj, (tm, SGU_OUT), lambda i: (i, 2)), (g_mla, (1, MLA_OUT), lambda i: (0, 0)),
             (g_sgu, (1, SGU_OUT), lambda i: (0, 0)), (v_gain, (1, SGU_OUT), lambda i: (0, 0)),
             (w_tril, (GROUPS, CHUNK, CHUNK), lambda i: (0, 0, 0)), (b_full, (CHUNK, SGU_OUT), lambda i: (0, 0))],
        outs=[((T, MLA_OUT + SGU_OUT), BF, (tm, MLA_OUT + SGU_OUT), lambda i: (i, 0))],
        scratch=[pltpu.VMEM((tm, SGU_OUT), F32)], semantics=("parallel",))[0]


def mix_bwd(dmixed, a, proj, g_mla, g_sgu, v_gain, w_tril, w_tril_t, b_full, *, name):
    T = a.shape[0]
    tm = _tile(T, ROW_TILE, CHUNK)
    n_chunk = tm // CHUNK

    def body(dm_a_ref, dm_s_ref, a_ref, u_ref, v_ref, gm_ref, gs_ref, vg_ref, w_ref, wt_ref, b_ref,
             da_ref, duv_ref, dgm_ref, dgs_ref, dvg_ref, dw_ref, db_ref, s_scr, y_scr):
        first = pl.program_id(0) == 0
        da, dgm_rows = _rms_bwd(a_ref[...], gm_ref[...], dm_a_ref[...])
        da_ref[...] = da.astype(BF)
        _acc_rows(dgm_ref, dgm_rows, first)

        for g in range(GROUPS):
            sl = slice(g * CH, (g + 1) * CH)
            vhat, _ = _group_norm_stats(_gelu(v_ref[:, sl]))
            vn = (vhat * vg_ref[:, sl]).astype(BF)
            u = _gelu(u_ref[:, sl])
            for ci in range(n_chunk):
                rs = slice(ci * CHUNK, (ci + 1) * CHUNK)
                y = jnp.dot(w_ref[g], vn[rs], preferred_element_type=F32) + b_ref[:, sl]
                y_scr[rs, sl] = y
                s_scr[rs, sl] = u[rs] * y
        ds, dgs_rows = _rms_bwd(s_scr[...], gs_ref[...], dm_s_ref[...])
        _acc_rows(dgs_ref, dgs_rows, first)
        s_scr[...] = ds

        @pl.when(first)
        def _():
            dw_ref[...] = jnp.zeros_like(dw_ref)
            db_ref[...] = jnp.zeros_like(db_ref)

        for g in range(GROUPS):
            sl = slice(g * CH, (g + 1) * CH)
            upre = u_ref[:, sl]
            vpre = v_ref[:, sl]
            u = _gelu(upre)
            vhat, r = _group_norm_stats(_gelu(vpre))
            gain = vg_ref[:, sl]
            vn = (vhat * gain).astype(BF)
            dsg = s_scr[:, sl]
            duv_ref[:, sl] = (dsg * y_scr[:, sl] * _gelu_grad(upre)).astype(BF)
            dy = dsg * u
            dyb = dy.astype(BF)
            dvn_parts = []
            for ci in range(n_chunk):
                rs = slice(ci * CHUNK, (ci + 1) * CHUNK)
                dvn_parts.append(jnp.dot(wt_ref[g], dyb[rs], preferred_element_type=F32))
                dw_ref[g] += lax.dot_general(dyb[rs], vn[rs], (((1,), (1,)), ((), ())), preferred_element_type=F32)
                db_ref[:, sl] += jnp.broadcast_to(jnp.sum(dy[rs], axis=-1, keepdims=True), (CHUNK, CH))
            dvn = dvn_parts[0] if n_chunk == 1 else jnp.concatenate(dvn_parts, axis=0)
            _acc_rows(dvg_ref.at[:, sl], dvn * vhat, first)
            dvh = dvn * gain
            dvg = r * (dvh - jnp.mean(dvh, axis=-1, keepdims=True)
                       - vhat * jnp.mean(dvh * vhat, axis=-1, keepdims=True))
            duv_ref[:, SGU_OUT + g * CH:SGU_OUT + (g + 1) * CH] = (dvg * _gelu_grad(vpre)).astype(BF)

    return _pcall(
        body, name=name, grid=(T // tm,),
        ins=[(dmixed, (tm, MLA_OUT), lambda i: (i, 0)), (dmixed, (tm, SGU_OUT), lambda i: (i, 1)),
             (a, (tm, MLA_OUT), lambda i: (i, 0)), (proj, (tm, SGU_OUT), lambda i: (i, 1)),
             (proj, (tm, SGU_OUT), lambda i: (i, 2)), (g_mla, (1, MLA_OUT), lambda i: (0, 0)),
             (g_sgu, (1, SGU_OUT), lambda i: (0, 0)), (v_gain, (1, SGU_OUT), lambda i: (0, 0)),
             (w_tril, (GROUPS, CHUNK, CHUNK), lambda i: (0, 0, 0)), (w_tril_t, (GROUPS, CHUNK, CHUNK), lambda i: (0, 0, 0)),
             (b_full, (CHUNK, SGU_OUT), lambda i: (0, 0))],
        outs=[((T, MLA_OUT), BF, (tm, MLA_OUT), lambda i: (i, 0)),
              ((T, 2 * SGU_OUT), BF, (tm, 2 * SGU_OUT), lambda i: (i, 0)),
              ((1, MLA_OUT), F32, (1, MLA_OUT), lambda i: (0, 0)), ((1, SGU_OUT), F32, (1, SGU_OUT), lambda i: (0, 0)),
              ((1, SGU_OUT), F32, (1, SGU_OUT), lambda i: (0, 0)),
              ((GROUPS, CHUNK, CHUNK), F32, (GROUPS, CHUNK, CHUNK), lambda i: (0, 0, 0)),
              ((CHUNK, SGU_OUT), F32, (CHUNK, SGU_OUT), lambda i: (0, 0))],
        scratch=[pltpu.VMEM((tm, SGU_OUT), F32), pltpu.VMEM((tm, SGU_OUT), F32)])


def _shift_down(z, n, row):
    return jnp.where(row >= n, pltpu.roll(z, n, 0), 0.0)


def _shift_up(z, n, row, T):
    return jnp.where(row < T - n, pltpu.roll(z, T - n, 0), 0.0)


def conv_fwd(proj, conv_w, *, name):
    T, D3 = proj.shape
    D = D3 // 3
    tn = _tile(D, 256)
    nj = D // tn

    def body(b_ref, c_ref, x_ref, w_ref, o_ref):
        row = lax.broadcasted_iota(jnp.int32, (T, tn), 0)
        z = c_ref[...] * x_ref[...]
        zc = w_ref[2:3, :] * z + w_ref[1:2, :] * _shift_down(z, 1, row) + w_ref[0:1, :] * _shift_down(z, 2, row)
        o_ref[...] = (b_ref[...] * zc).astype(BF)

    return _pcall(
        body, name=name, grid=(nj,),
        ins=[(proj, (T, tn), lambda j: (0, j)), (proj, (T, tn), lambda j: (0, nj + j)),
             (proj, (T, tn), lambda j: (0, 2 * nj + j)), (conv_w, (3, tn), lambda j: (0, j))],
        outs=[((T, D), BF, (T, tn), lambda j: (0, j))], semantics=("parallel",))[0]


def conv_bwd(dg, proj, conv_w, *, name):
    T, D3 = proj.shape
    D = D3 // 3
    tn = _tile(D, 256)
    nj = D // tn

    def body(dg_ref, b_ref, c_ref, x_ref, w_ref, dp_ref, dw_ref, dc_scr, dx_scr):
        part = pl.program_id(1)

        @pl.when(part == 0)
        def _():
            row = lax.broadcasted_iota(jnp.int32, (T, tn), 0)
            c, x = c_ref[...], x_ref[...]
            z = c * x
            z1 = _shift_down(z, 1, row)
            z2 = _shift_down(z, 2, row)
            dgv = dg_ref[...]
            zc = w_ref[2:3, :] * z + w_ref[1:2, :] * z1 + w_ref[0:1, :] * z2
            dp_ref[...] = (dgv * zc).astype(BF)
            dzc = dgv * b_ref[...]
            dw_ref[0:1, :] = jnp.sum(dzc * z2, axis=0, keepdims=True)
            dw_ref[1:2, :] = jnp.sum(dzc * z1, axis=0, keepdims=True)
            dw_ref[2:3, :] = jnp.sum(dzc * z, axis=0, keepdims=True)
            dz = (w_ref[2:3, :] * dzc + w_ref[1:2, :] * _shift_up(dzc, 1, row, T)
                  + w_ref[0:1, :] * _shift_up(dzc, 2, row, T))
            dc_scr[...] = (dz * x).astype(BF)
            dx_scr[...] = (dz * c).astype(BF)

        @pl.when(part == 1)
        def _():
            dp_ref[...] = dc_scr[...]

        @pl.when(part == 2)
        def _():
            dp_ref[...] = dx_scr[...]

    return _pcall(
        body, name=name, grid=(nj, 3),
        ins=[(dg, (T, tn), lambda j, p: (0, j)), (proj, (T, tn), lambda j, p: (0, j)),
             (proj, (T, tn), lambda j, p: (0, nj + j)), (proj, (T, tn), lambda j, p: (0, 2 * nj + j)),
             (conv_w, (3, tn), lambda j, p: (0, j))],
        outs=[((T, D3), BF, (T, tn), lambda j, p: (0, p * nj + j)), ((3, D), F32, (3, tn), lambda j, p: (0, j))],
        scratch=[pltpu.VMEM((T, tn), BF), pltpu.VMEM((T, tn), BF)], semantics=("parallel", "arbitrary"))


def loss_bwd(x, gain, target, *, name):
    T, D = x.shape
    tm = _tile(T, ROW_TILE, 8)

    def body(x_ref, g_ref, t_ref, dx_ref, dxb_ref, dg_ref, loss_ref):
        first = pl.program_id(0) == 0
        xv = x_ref[...]
        r = _rstd(xv)
        xh = xv * r
        gain_v = g_ref[...]
        err = xh * gain_v - t_ref[...]
        part = 0.5 * jnp.sum(jnp.mean(err * err, axis=-1, keepdims=True), axis=0, keepdims=True)
        _acc_rows(loss_ref, jnp.broadcast_to(part, (1, LANES)), first)
        dy = err * (1.0 / D)
        gdy = dy * gain_v
        dx = r * (gdy - xh * jnp.mean(gdy * xh, axis=-1, keepdims=True))
        dx_ref[...] = dx
        dxb_ref[...] = dx.astype(BF)
        _acc_rows(dg_ref, dy * xh, first)

    return _pcall(
        body, name=name, grid=(T // tm,),
        ins=[(x, (tm, D), lambda i: (i, 0)), (gain, (1, D), lambda i: (0, 0)), (target, (tm, D), lambda i: (i, 0))],
        outs=[((T, D), F32, (tm, D), lambda i: (i, 0)), ((T, D), BF, (tm, D), lambda i: (i, 0)),
              ((1, D), F32, (1, D), lambda i: (0, 0)), ((1, LANES), F32, (1, LANES), lambda i: (0, 0))])


def _adamw(g, w, m, v):
    m = ADAM_B1 * m + (1.0 - ADAM_B1) * g
    v = ADAM_B2 * v + (1.0 - ADAM_B2) * (g * g)
    m_hat = m / ADAM_C1
    v_hat = v / ADAM_C2
    delta = -ADAM_LR * (m_hat / (jnp.sqrt(v_hat) + ADAM_EPS) + ADAM_WD * w)
    return delta, m, v


def adam_flat(g, w, m, v, *, name):
    def body(g_ref, w_ref, m_ref, v_ref, d_ref, nm_ref, nv_ref):
        d, nm, nv = _adamw(g_ref[...], w_ref[...], m_ref[...], v_ref[...])
        d_ref[...] = d
        nm_ref[...] = nm
        nv_ref[...] = nv

    blk = g.shape
    zero = lambda: (0, 0)
    return _pcall(body, name=name, grid=(),
                  ins=[(t, blk, zero) for t in (g, w, m, v)],
                  outs=[(blk, F32, blk, zero)] * 3)


def _chip_slots():
    x, y, c = lax.axis_index("x"), lax.axis_index("y"), lax.axis_index("c")
    chips = [(1 - x, y), (x, 1 - y), (1 - x, 1 - y)]
    return x, y, c, chips


def reduce_adam(gs, a_buf, b_buf, w, m, v, layer, prev, *, name, deps=()):
    L, R, C = w.shape
    tr = _tile(R, 256, 8)
    x, y, c, _ = _chip_slots()
    idx = jnp.stack([4 * x + 2 * y + c, 2 * x + y]).astype(jnp.int32)
    n_prev = 0 if prev is None else 4

    def body(idx_ref, g_ref, a_ref, b0_ref, b1_ref, b2_ref, w_ref, m_ref, v_ref, *rest):
        outs = rest[n_prev:]
        g = ((((g_ref[...].astype(F32) + a_ref[...].astype(F32)) + b0_ref[...].astype(F32))
              + b1_ref[...].astype(F32)) + b2_ref[...].astype(F32))
        d, nm, nv = _adamw(g, w_ref[...], m_ref[...], v_ref[...])
        outs[0][...] = g
        outs[1][...] = d
        outs[2][...] = nm
        outs[3][...] = nv

    blk3 = (None, tr, C)
    ins = [(gs, blk3, lambda i, s: (s[0], i, 0)), (a_buf, blk3, lambda i, s: (s[1], i, 0)),
           (b_buf, blk3, lambda i, s: (0, i, 0)), (b_buf, blk3, lambda i, s: (1, i, 0)),
           (b_buf, blk3, lambda i, s: (2, i, 0)),
           (w, blk3, lambda i, s: (layer, i, 0)), (m, blk3, lambda i, s: (layer, i, 0)),
           (v, blk3, lambda i, s: (layer, i, 0))]
    aliases = {}
    if prev is not None:
        for o, p in enumerate(prev):
            ins.append((p, None, None))
            aliases[1 + 8 + o] = o
    outs = [((L, R, C), F32, blk3, lambda i, s: (layer, i, 0))] * 4
    return _pcall(body, name=name, grid=(R // tr,), ins=ins, outs=outs, prefetch=idx, aliases=aliases,
                  semantics=("parallel",), deps=deps)


def pair_sum(gs, a_buf, *, name):
    _, R, C = gs.shape
    tr = _tile(R, 256, 8)
    x, y, c, chips = _chip_slots()
    idx = jnp.stack([4 * cx + 2 * cy + c for cx, cy in chips] + [2 * cx + cy for cx, cy in chips]).astype(jnp.int32)

    def body(idx_ref, g_ref, a_ref, o_ref):
        o_ref[...] = (g_ref[...].astype(F32) + a_ref[...].astype(F32)).astype(BF)

    blk3 = (None, tr, C)
    return _pcall(body, name=name, grid=(3, R // tr),
                  ins=[(gs, blk3, lambda j, i, s: (s[j], i, 0)), (a_buf, blk3, lambda j, i, s: (s[3 + j], i, 0))],
                  outs=[((3, R, C), BF, blk3, lambda j, i, s: (j, i, 0))], prefetch=idx,
                  semantics=("parallel", "parallel"))[0]


def sum_rows8(gathered, rows, *, name):
    W = gathered.shape[1]

    def body(g_ref, o_ref):
        acc = g_ref[0:rows, :]
        for d in range(1, N_DEV):
            acc = acc + g_ref[d * rows:(d + 1) * rows, :]
        o_ref[...] = acc

    return _pcall(body, name=name, grid=(), ins=[(gathered, gathered.shape, lambda: (0, 0))],
                  outs=[((rows, W), F32, (rows, W), lambda: (0, 0))])[0]


HBM_SPEC = pl.BlockSpec(memory_space=pltpu.HBM)
SEM_SPEC = pl.BlockSpec(memory_space=pltpu.SEMAPHORE)
ANY_SPEC = pl.BlockSpec(memory_space=pl.ANY)
DATAFLOW = pltpu.SideEffectType.DATAFLOW_SIDE_EFFECTING


def _in_hbm(v):
    return pltpu.with_memory_space_constraint(v, pltpu.HBM)


def _slot(p):
    return 4 * p[0] + 2 * p[1] + p[2]


def _gather_peers():
    x, y, c, chips = _chip_slots()
    return (x, y, c), [(x, y, 1 - c)] + [(*chip, c) for chip in chips]


def gather_start(groups, after, *, name):
    flat = [s for g in groups for s in g]
    n, n_g = len(flat), len(groups)
    where = [(gi, ti) for gi, g in enumerate(groups) for ti in range(len(g))]

    def body(*refs):
        src, land = refs[:n], refs[n:2 * n]
        sems = refs[2 * n + 1:2 * n + 1 + 2 * n_g]
        me, peers = _gather_peers()
        for t in range(n):
            gi, ti = where[t]
            for k, to in enumerate(peers):
                pltpu.make_async_remote_copy(
                    src_ref=src[t], dst_ref=land[t].at[_slot(me)], send_sem=sems[2 * gi].at[4 * ti + k],
                    recv_sem=sems[2 * gi + 1].at[4 * ti + k], device_id=to, device_id_type=MESH).start()
        refs[-1][...] = jnp.zeros_like(refs[-1])

    out_shape = []
    for g in groups:
        out_shape += [pltpu.SemaphoreType.DMA((4 * len(g),)), pltpu.SemaphoreType.DMA((4 * len(g),))]
    out_shape += [pltpu.HBM(s.shape, s.dtype) for s in flat]
    out_shape += [pltpu.HBM((N_DEV,) + s.shape, s.dtype) for s in flat]
    out_shape += [jax.ShapeDtypeStruct((8, LANES), F32)]
    aliases = {t: 2 * n_g + t for t in range(n)}
    aliases.update({n + t: 2 * n_g + n + t for t in range(n)})
    res = pl.pallas_call(
        body, name=name, out_shape=out_shape, in_specs=[HBM_SPEC] * (2 * n) + [ANY_SPEC],
        out_specs=[SEM_SPEC] * (2 * n_g) + [HBM_SPEC] * (2 * n) + [pl.BlockSpec(memory_space=pltpu.VMEM)],
        input_output_aliases=aliases, compiler_params=pltpu.CompilerParams(has_side_effects=DATAFLOW),
    )(*[_in_hbm(s) for s in flat], *[_in_hbm(lax.empty((N_DEV,) + s.shape, s.dtype)) for s in flat], after)
    out, off = [], 0
    for gi, g in enumerate(groups):
        k = len(g)
        out.append((res[2 * gi], res[2 * gi + 1], res[2 * n_g + off:2 * n_g + off + k],
                    res[2 * n_g + n + off:2 * n_g + n + off + k]))
        off += k
    return out, res[-1]


def gather_wait(started, after, *, name):
    send_sems, recv_sems, srcs, lands = started
    n = len(srcs)

    def body(*refs):
        src, land = refs[:n], refs[n:2 * n]
        send, recv = refs[2 * n], refs[2 * n + 1]
        _, peers = _gather_peers()
        for t in range(n):
            for k, frm in enumerate(peers):
                cp = pltpu.make_async_remote_copy(
                    src_ref=src[t], dst_ref=land[t].at[_slot(frm)], send_sem=send.at[4 * t + k],
                    recv_sem=recv.at[4 * t + k],
                    device_id=frm, device_id_type=MESH)
                cp.wait_send()
                cp.wait_recv()

    res = pl.pallas_call(
        body, name=name,
        out_shape=[pltpu.HBM(s.shape, s.dtype) for s in srcs] + [pltpu.HBM(l.shape, l.dtype) for l in lands],
        in_specs=[HBM_SPEC] * (2 * n) + [SEM_SPEC, SEM_SPEC, ANY_SPEC], out_specs=[HBM_SPEC] * (2 * n),
        input_output_aliases={t: t for t in range(2 * n)},
        compiler_params=pltpu.CompilerParams(has_side_effects=DATAFLOW),
    )(*srcs, *lands, send_sems, recv_sems, after)
    return res[:n], res[n:]


def place_own(src, land, *, name):
    R, C = src.shape
    tr = _tile(R, 512, 16)
    x, y, c, _ = _chip_slots()
    idx = jnp.stack([4 * x + 2 * y + c]).astype(jnp.int32)

    def body(idx_ref, s_ref, land_ref, o_ref):
        o_ref[...] = s_ref[...]

    return _pcall(body, name=name, grid=(R // tr,),
                  ins=[(src, (tr, C), lambda i, s: (i, 0)), (land, None, None)],
                  outs=[(land.shape, land.dtype, (None, tr, C), lambda i, s: (s[0], i, 0))],
                  prefetch=idx, aliases={2: 0}, semantics=("parallel",))[0]


def gather_finish(srcs, lands, *, name):
    n = len(srcs)

    def body(*refs):
        land = refs[n:2 * n]
        send_sems, recv_sems = refs[2 * n:]
        x, y, c, chips = _chip_slots()
        me, sibling = (x, y, c), (x, y, 1 - c)

        def copy(t, j, block, to):
            return pltpu.make_async_remote_copy(
                src_ref=land[t].at[_slot(block)], dst_ref=land[t].at[_slot(block)], send_sem=send_sems.at[t, j],
                recv_sem=recv_sems.at[t, j], device_id=to, device_id_type=MESH)

        sends = [copy(t, j, (*chip, c), sibling) for t in range(n) for j, chip in enumerate(chips)]
        for cp in sends:
            cp.start()
        for t in range(n):
            for j, chip in enumerate(chips):
                copy(t, j, (*chip, 1 - c), me).wait_recv()
        for cp in sends:
            cp.wait_send()

    passed = pl.pallas_call(
        body, name=name, out_shape=[jax.ShapeDtypeStruct(l.shape, l.dtype) for l in lands],
        in_specs=[ANY_SPEC] * n, out_specs=[ANY_SPEC] * n,
        input_output_aliases={t: t for t in range(n)},
        scratch_shapes=[pltpu.SemaphoreType.DMA((n, 3)), pltpu.SemaphoreType.DMA((n, 3))],
    )(*lands)
    return [place_own(s, l, name=f"{name}_own{t}") for t, (s, l) in enumerate(zip(srcs, passed))]


def chips_start(pairs, *, name):
    n = len(pairs)

    def body(*refs):
        src, land = refs[:n], refs[n:2 * n]
        send, recv = refs[2 * n], refs[2 * n + 1]
        token = refs[-1]
        x, y, c, chips = _chip_slots()
        for t in range(n):
            for j, chip in enumerate(chips):
                pltpu.make_async_remote_copy(
                    src_ref=src[t].at[j], dst_ref=land[t].at[j], send_sem=send.at[3 * t + j],
                    recv_sem=recv.at[3 * t + j], device_id=(*chip, c), device_id_type=MESH).start()
        token[...] = jnp.zeros_like(token)

    res = pl.pallas_call(
        body, name=name,
        out_shape=[pltpu.SemaphoreType.DMA((3 * n,)), pltpu.SemaphoreType.DMA((3 * n,))]
        + [pltpu.HBM(p.shape, p.dtype) for p in pairs] * 2 + [jax.ShapeDtypeStruct((8, LANES), F32)],
        in_specs=[HBM_SPEC] * (2 * n),
        out_specs=[SEM_SPEC, SEM_SPEC] + [HBM_SPEC] * (2 * n) + [pl.BlockSpec(memory_space=pltpu.VMEM)],
        input_output_aliases={t: 2 + t for t in range(2 * n)},
        compiler_params=pltpu.CompilerParams(has_side_effects=DATAFLOW),
    )(*[_in_hbm(p) for p in pairs], *[_in_hbm(lax.empty(p.shape, p.dtype)) for p in pairs])
    return res[0], res[1], res[2:2 + n], res[2 + n:2 + 2 * n], res[-1]


def chips_wait(started, after, *, name):
    send_sems, recv_sems, srcs, lands, _ = started
    n = len(srcs)

    def body(*refs):
        src, land = refs[:n], refs[n:2 * n]
        send, recv = refs[2 * n], refs[2 * n + 1]
        x, y, c, chips = _chip_slots()
        for t in range(n):
            for j, chip in enumerate(chips):
                cp = pltpu.make_async_remote_copy(
                    src_ref=src[t].at[j], dst_ref=land[t].at[j], send_sem=send.at[3 * t + j],
                    recv_sem=recv.at[3 * t + j], device_id=(*chip, c), device_id_type=MESH)
                cp.wait_send()
                cp.wait_recv()

    res = pl.pallas_call(
        body, name=name, out_shape=[pltpu.HBM(s.shape, s.dtype) for s in srcs] * 2,
        in_specs=[HBM_SPEC] * (2 * n) + [SEM_SPEC, SEM_SPEC, ANY_SPEC], out_specs=[HBM_SPEC] * (2 * n),
        input_output_aliases={t: t for t in range(2 * n)},
        compiler_params=pltpu.CompilerParams(has_side_effects=DATAFLOW),
    )(*srcs, *lands, send_sems, recv_sems, after)
    return res[n:]


def _sibling_copies(src, land, send, recv, n):
    x, y, c, _ = _chip_slots()
    return [pltpu.make_async_remote_copy(
        src_ref=src[t].at[4 * (q // 2) + 2 * (q % 2) + (1 - c)], dst_ref=land[t].at[q], send_sem=send.at[4 * t + q],
        recv_sem=recv.at[4 * t + q], device_id=(x, y, 1 - c), device_id_type=MESH)
        for t in range(n) for q in range(4)]


def sibling_start(gs, *, name):
    n = len(gs)

    def body(*refs):
        for cp in _sibling_copies(refs[:n], refs[n:2 * n], refs[2 * n], refs[2 * n + 1], n):
            cp.start()
        refs[-1][...] = jnp.zeros_like(refs[-1])

    lands = [lax.empty((4,) + g.shape[1:], g.dtype) for g in gs]
    res = pl.pallas_call(
        body, name=name,
        out_shape=[pltpu.SemaphoreType.DMA((4 * n,)), pltpu.SemaphoreType.DMA((4 * n,))]
        + [pltpu.HBM(g.shape, g.dtype) for g in gs] + [pltpu.HBM(l.shape, l.dtype) for l in lands]
        + [jax.ShapeDtypeStruct((8, LANES), F32)],
        in_specs=[HBM_SPEC] * (2 * n),
        out_specs=[SEM_SPEC, SEM_SPEC] + [HBM_SPEC] * (2 * n) + [pl.BlockSpec(memory_space=pltpu.VMEM)],
        input_output_aliases={t: 2 + t for t in range(2 * n)},
        compiler_params=pltpu.CompilerParams(has_side_effects=DATAFLOW),
    )(*[_in_hbm(g) for g in gs], *[_in_hbm(l) for l in lands])
    return res[0], res[1], res[2:2 + n], res[2 + n:2 + 2 * n], res[-1]


def sibling_wait(started, after, *, name):
    send_sems, recv_sems, srcs, lands, _ = started
    n = len(srcs)

    def body(*refs):
        for cp in _sibling_copies(refs[:n], refs[n:2 * n], refs[2 * n], refs[2 * n + 1], n):
            cp.wait_send()
            cp.wait_recv()

    res = pl.pallas_call(
        body, name=name,
        out_shape=[pltpu.HBM(s.shape, s.dtype) for s in srcs] + [pltpu.HBM(l.shape, l.dtype) for l in lands],
        in_specs=[HBM_SPEC] * (2 * n) + [SEM_SPEC, SEM_SPEC, ANY_SPEC], out_specs=[HBM_SPEC] * (2 * n),
        input_output_aliases={t: t for t in range(2 * n)},
        compiler_params=pltpu.CompilerParams(has_side_effects=DATAFLOW),
    )(*srcs, *lands, send_sems, recv_sems, after)
    return res[:n], res[n:]


def all_gather_vmem(x_shard, *, name, after=None):
    m_per, n = x_shard.shape
    n_after = 0 if after is None else 1

    def body(x_ref, *rest):
        out_ref, send_sems, recv_sems, local_sem = rest[n_after:]
        x, y, c, chips = _chip_slots()
        me, sibling = (x, y, c), (x, y, 1 - c)

        def rows(px, py, pc):
            return out_ref.at[pl.ds((4 * px + 2 * py + pc) * m_per, m_per), :]

        def copy(k, block, to, src=None):
            return pltpu.make_async_remote_copy(
                src_ref=rows(*block) if src is None else src, dst_ref=rows(*block),
                send_sem=send_sems.at[k], recv_sem=recv_sems.at[k], device_id=to, device_id_type=MESH)

        mine = pltpu.make_async_copy(x_ref, rows(*me), local_sem)
        mine.start()
        first = [copy(0, me, sibling, src=x_ref)]
        first += [copy(1 + j, me, (*chip, c), src=x_ref) for j, chip in enumerate(chips)]
        for cp in first:
            cp.start()
        passed = [copy(4 + j, (*chip, c), sibling) for j, chip in enumerate(chips)]
        for j, chip in enumerate(chips):
            copy(1 + j, (*chip, c), me).wait_recv()
            passed[j].start()
        copy(0, sibling, me).wait_recv()
        for j, chip in enumerate(chips):
            copy(4 + j, (*chip, 1 - c), me).wait_recv()
        for cp in first + passed:
            cp.wait_send()
        mine.wait()

    vmem = pl.BlockSpec(memory_space=pltpu.VMEM)
    return pl.pallas_call(
        body, name=name, out_shape=jax.ShapeDtypeStruct((N_DEV * m_per, n), x_shard.dtype),
        in_specs=[vmem] + [ANY_SPEC] * n_after, out_specs=vmem,
        scratch_shapes=[pltpu.SemaphoreType.DMA((7,)), pltpu.SemaphoreType.DMA((7,)), pltpu.SemaphoreType.DMA],
        compiler_params=pltpu.CompilerParams(vmem_limit_bytes=int(min(
            VMEM_LIMIT_CAP, 2 * (N_DEV + 1) * m_per * n * x_shard.dtype.itemsize + 16 * 2 ** 20))),
    )(x_shard, *([] if after is None else [after]))


def _rope_slab(cols):
    z = jnp.zeros(cols.shape[:-1] + (HALF_ROPE,), cols.dtype)
    return jnp.concatenate([cols[..., :HALF_ROPE], z, cols[..., HALF_ROPE:], z], axis=-1)


def _rope_unslab(slab):
    return jnp.concatenate([slab[..., :HALF_ROPE], slab[..., 2 * HALF_ROPE:3 * HALF_ROPE]], axis=-1)


def _pack_w_in(w_g):
    s, d, c = w_g.shape
    w = jnp.transpose(w_g, (1, 0, 2)).reshape(d, s * c)
    c1, c2, c3 = Q_LORA, Q_LORA + KV_LORA, Q_LORA + KV_LORA + QK_ROPE
    return jnp.concatenate([w[:, :c2], w[:, c3:], _rope_slab(w[:, c2:c3])], axis=-1)


def _unpack_w_in_grad(dw):
    d = dw.shape[0]
    c2 = Q_LORA + KV_LORA
    uv = 2 * SGU_OUT
    g = jnp.concatenate([dw[:, :c2], _rope_unslab(dw[:, c2 + uv:]), dw[:, c2:c2 + uv]], axis=-1)
    return jnp.transpose(g.reshape(d, N_DEV, g.shape[1] // N_DEV), (1, 0, 2))


def _rope_tables(positions):
    inv_freq = ROPE_BASE ** (-jnp.arange(0, QK_ROPE, 2, dtype=F32) / QK_ROPE)
    ang = positions.astype(F32)[:, None] * inv_freq
    cos, sin = jnp.cos(ang), jnp.sin(ang)
    z = jnp.zeros_like(cos)
    return jnp.concatenate([cos, z, cos, z], axis=-1), jnp.concatenate([-sin, z, sin, z], axis=-1)


def _mlp_up(x, gain, w1, tag):
    hn = rms_fwd(x, gain, name=f"mlp{tag}_norm")

    def act_epi(acc):
        a = jnp.maximum(acc, 0.0)
        return a, a * a

    T = x.shape[0]
    F = w1.shape[0] * w1.shape[2]
    a, act = mm(hn, w1, name=f"mlp{tag}_up", outs=[((T, F), BF, None), ((T, F), BF, None)], epi=act_epi)
    return hn, a, act


def _mlp_down(x, act, w2, tag):
    bm = _tile(x.shape[0], MM_TILE)
    bn = _tile(x.shape[1], MM_TILE)
    return mm(act, w2, name=f"mlp{tag}_down", out=(x.shape, F32), bm=bm, bn=bn,
              epi=lambda acc, r: (acc + r[...],), epi_ins=[(x, (bm, bn), lambda i, j, k: (i, j))])


def _mlp_bwd_weights(w1, w2, saved, dxb, tag):
    hn, a, act = saved
    T, D = dxb.shape
    F = a.shape[1]
    bm = _tile(T, MM_TILE)
    bn = _tile(F, min(MM_TILE, w1.shape[2]))
    dhid = mm(dxb, w2, tb=True, name=f"mlp{tag}_dhid", out=((T, F), BF), bm=bm, bn=bn,
              epi=lambda acc, a_ref: (2.0 * a_ref[...].astype(F32) * acc,),
              epi_ins=[(a, (bm, bn), lambda i, j, k: (i, j))])
    dw2 = mm(act, dxb, ta=True, name=f"mlp{tag}_dw2", out=((F, D), BF))
    dw1 = mm(hn, dhid, ta=True, name=f"mlp{tag}_dw1", out=(w1.shape, BF))
    return dhid, dw1, dw2.reshape(N_DEV, F // N_DEV, D)


def _reduce_begin(grads, tag):
    return sibling_start(grads, name=f"reduce_sibling_start_{tag}")


def _reduce_continue(sib, after, tag):
    grads, a_bufs = sibling_wait(sib, after, name=f"reduce_sibling_wait_{tag}")
    pairs = [pair_sum(g, a, name=f"pair_sum_{tag}{t}") for t, (g, a) in enumerate(zip(grads, a_bufs))]
    return grads, a_bufs, chips_start(pairs, name=f"reduce_chips_start_{tag}")


def _mlp_bwd_input(x_in, gain, w1, dhid, dx, tag, sib):
    dhn = mm(dhid, w1, tb=True, name=f"mlp{tag}_dhn", out=(x_in.shape, F32), deps=[sib[-1]])
    reduce_state = _reduce_continue(sib, dhn, f"r_mlp{tag}")
    return rms_bwd(x_in, gain, dhn, dres=dx, name=f"mlp{tag}_norm_bwd", deps=[reduce_state[2][-1]]), reduce_state


def kernel(x, positions, e_norm_mix, e_w_in, e_q_norm, e_w_uq, e_kv_norm, e_w_ukv, e_v_norm, e_sgu_w, e_sgu_b, e_mla_out_norm, e_sgu_out_norm, e_w_out, o_norm_mix, o_w_in, o_conv_w, o_w_out, mlp_norm, mlp_w1, mlp_w2, final_norm, loss_target, m_e_norm_mix, m_e_w_in, m_e_q_norm, m_e_w_uq, m_e_kv_norm, m_e_w_ukv, m_e_v_norm, m_e_sgu_w, m_e_sgu_b, m_e_mla_out_norm, m_e_sgu_out_norm, m_e_w_out, m_o_norm_mix, m_o_w_in, m_o_conv_w, m_o_w_out, m_mlp_norm, m_mlp_w1, m_mlp_w2, m_final_norm, v_e_norm_mix, v_e_w_in, v_e_q_norm, v_e_w_uq, v_e_kv_norm, v_e_w_ukv, v_e_v_norm, v_e_sgu_w, v_e_sgu_b, v_e_mla_out_norm, v_e_sgu_out_norm, v_e_w_out, v_o_norm_mix, v_o_w_in, v_o_conv_w, v_o_w_out, v_mlp_norm, v_mlp_w1, v_mlp_w2, v_final_norm):
    T, D = x.shape[1], x.shape[2]
    d_shard = o_norm_mix.shape[1]
    x0 = x[0]
    target = loss_target[0]
    me = 4 * lax.axis_index("x") + 2 * lax.axis_index("y") + lax.axis_index("c")

    bf = lambda s: s.astype(BF)
    gather_groups = [[bf(e_w_in[0]), bf(e_w_uq[0]), bf(e_w_ukv[0])], [bf(e_w_out[0]), bf(mlp_w1[0])],
                     [bf(mlp_w2[0]), bf(o_w_in[0])], [bf(o_w_out[0]), bf(mlp_w1[1])], [bf(mlp_w2[1])]]
    small_rows = jnp.concatenate([o_norm_mix, o_conv_w[0], jnp.zeros((4, d_shard), F32)], axis=0)
    small_flat = all_gather_vmem(small_rows, name="gather_small")
    started, start_token = gather_start(gather_groups[:1], small_flat, name="gather_start0")
    started += gather_start(gather_groups[1:], start_token, name="gather_start1")[0]

    def gathered(gi, after):
        srcs, lands = gather_wait(started[gi], after, name=f"gather_wait{gi}")
        return gather_finish(srcs, lands, name=f"gather_finish{gi}")

    small_g = small_flat.reshape(N_DEV, 8, d_shard)
    o_norm_full = small_g[:, 0, :].reshape(1, D)
    conv_w_full = jnp.transpose(small_g[:, 1:4, :], (1, 0, 2)).reshape(3, D)
    w_tril = jnp.tril(e_sgu_w[0])
    w_tril_b = w_tril.astype(BF)
    w_tril_tb = jnp.swapaxes(w_tril, 1, 2).astype(BF)
    b_full = jnp.repeat(e_sgu_b[0].T, CH, axis=1)
    v_gain = e_v_norm[0].reshape(1, SGU_OUT)
    cos_t, sin_t = _rope_tables(positions[0])
    mlp_gain = [mlp_norm[0:1], mlp_norm[1:2]]
    final_gain = final_norm.reshape(1, D)

    h0 = rms_fwd(x0, e_norm_mix, name="e_norm")
    g_w_in, g_w_uq, w_ukv = gathered(0, h0)
    w_in_e = _pack_w_in(g_w_in)
    w_uq = jnp.concatenate([g_w_uq[..., :QK_NOPE], _rope_slab(g_w_uq[..., QK_NOPE:])], axis=-1)
    proj = mm(h0, w_in_e, name="e_in", out=((T, w_in_e.shape[1]), F32), bn=_tile(w_in_e.shape[1], 640))
    qn, kvn, krope = mla_prep(proj, e_q_norm, e_kv_norm, cos_t, sin_t, name="mla_prep")
    bm = _tile(T, MM_TILE)

    def q_epi(acc, cos_ref, sin_ref):
        return (jnp.concatenate([acc[:, :QK_NOPE], _rope_fwd(acc[:, QK_NOPE:], cos_ref[...], sin_ref[...])], axis=-1),)

    q = mm(qn, w_uq, name="mla_q", out=((T, HEADS * HEAD_PAD), BF), bm=bm, bn=HEAD_PAD, epi=q_epi,
           epi_ins=[(cos_t, (bm, LANES), lambda i, j, k: (i, 0)), (sin_t, (bm, LANES), lambda i, j, k: (i, 0))])

    def kv_epi(acc, kr_ref):
        return jnp.concatenate([acc[:, :QK_NOPE].astype(BF), kr_ref[...]], axis=-1), acc[:, QK_NOPE:]

    k, v = mm(kvn, w_ukv, name="mla_kv", bm=bm, bn=HEAD_PAD, epi=kv_epi,
              outs=[((T, HEADS * HEAD_PAD), BF, HEAD_PAD), ((T, MLA_OUT), BF, V_HEAD)],
              epi_ins=[(krope, (bm, LANES), lambda i, j, k: (i, 0))])
    attn, attn_lse = attn_fwd(q, k, v, name="attn_fwd")
    mixed = mix_fwd(attn, proj, e_mla_out_norm, e_sgu_out_norm, v_gain, w_tril_b, b_full, name="mix_fwd")
    bn = _tile(D, MM_TILE)
    g_w_out_e, w1_0 = gathered(1, mixed)
    w_out_e = g_w_out_e.reshape(-1, D)
    x1 = mm(mixed, w_out_e, name="e_out", out=((T, D), F32), bm=bm, bn=bn,
            epi=lambda acc, r: (acc + r[...],), epi_ins=[(x0, (bm, bn), lambda i, j, k: (i, j))])
    hn0, a0, act0 = _mlp_up(x1, mlp_gain[0], w1_0, 0)
    g_w2_0, g_w_in_o = gathered(2, act0)
    w2_0 = g_w2_0.reshape(-1, D)
    x2 = _mlp_down(x1, act0, w2_0, 0)
    ho = rms_fwd(x2, o_norm_full, name="o_norm")
    proj_o = mm(ho, g_w_in_o, name="o_in", out=((T, 3 * D), F32))
    gated = conv_fwd(proj_o, conv_w_full, name="conv_fwd")
    g_w_out_o, w1_1 = gathered(3, gated)
    w_out_o = g_w_out_o.reshape(-1, D)
    x3 = mm(gated, w_out_o, name="o_out", out=((T, D), F32), bm=bm, bn=bn,
            epi=lambda acc, r: (acc + r[...],), epi_ins=[(x2, (bm, bn), lambda i, j, k: (i, j))])
    hn1, a1, act1 = _mlp_up(x3, mlp_gain[1], w1_1, 1)
    (g_w2_1,) = gathered(4, act1)
    w2_1 = g_w2_1.reshape(-1, D)
    x4 = _mlp_down(x3, act1, w2_1, 1)
    w1, w2 = [w1_0, w1_1], [w2_0, w2_1]
    mlp0_saved, mlp1_saved = (hn0, a0, act0), (hn1, a1, act1)

    dx4, dx4b, d_final, loss_part = loss_bwd(x4, final_gain, target, name="loss_bwd")
    loss = lax.psum(loss_part[0, 0], AXES)

    dhid1, dw1_1, dw2_1 = _mlp_bwd_weights(w1[1], w2[1], mlp1_saved, dx4b, 1)
    sib_r0 = _reduce_begin([dw1_1, dw2_1], "r0")
    (dx3, dx3b, d_mlp1), (grads_r0, a_r0, st_r0) = _mlp_bwd_input(x3, mlp_gain[1], w1[1], dhid1, dx4, 1, sib_r0)

    dgated = mm(dx3b, w_out_o, tb=True, name="o_out_dx", out=((T, D), F32))
    dw_out_o = mm(gated, dx3b, ta=True, name="o_out_dw", out=((D, D), BF))
    dproj_o, dconv_full = conv_bwd(dgated, proj_o, conv_w_full, name="conv_bwd")
    dw_in_o = mm(ho, dproj_o, ta=True, name="o_in_dw", out=(g_w_in_o.shape, BF))
    sib_r1 = _reduce_begin([dw_out_o.reshape(g_w_out_o.shape), dw_in_o], "r1")
    dho = mm(dproj_o, g_w_in_o, tb=True, name="o_in_dx", out=((T, D), F32), deps=[sib_r1[-1]])
    grads_r1, a_r1, st_r1 = _reduce_continue(sib_r1, dho, "r1")
    dx2, dx2b, d_onorm_full = rms_bwd(x2, o_norm_full, dho, dres=dx3, name="o_norm_bwd", deps=[st_r1[-1]])

    dhid0, dw1_0, dw2_0 = _mlp_bwd_weights(w1[0], w2[0], mlp0_saved, dx2b, 0)
    sib_r2 = _reduce_begin([dw1_0, dw2_0], "r2")
    (dx1, dx1b, d_mlp0), (grads_r2, a_r2, st_r2) = _mlp_bwd_input(x1, mlp_gain[0], w1[0], dhid0, dx2, 0, sib_r2)
    b_r0 = chips_wait(st_r0, dx1b, name="reduce_chips_wait_r0")

    dmixed = mm(dx1b, w_out_e, tb=True, name="e_out_dx", out=((T, MLA_OUT + SGU_OUT), F32))
    dw_out_e = mm(mixed, dx1b, ta=True, name="e_out_dw", out=(w_out_e.shape, BF))
    (dattn, duv, d_mla_out, d_sgu_out, d_vgain, d_sgu_w, d_b_full) = mix_bwd(
        dmixed, attn, proj, e_mla_out_norm, e_sgu_out_norm, v_gain, w_tril_b, w_tril_tb, b_full, name="mix_bwd")
    b_r1 = chips_wait(st_r1, dattn, name="reduce_chips_wait_r1")
    dq, dk, dv = attn_bwd(q, k, v, attn, attn_lse, dattn, name="attn_bwd")
    dq_lin, dkv_lin, dkr = mla_bwd_prep(dq, dk, dv, cos_t, sin_t, name="mla_bwd_prep")
    dw_uq_pad = mm(qn, dq_lin, ta=True, name="mla_q_dw", out=(w_uq.shape, F32))
    dw_ukv = mm(kvn, dkv_lin, ta=True, name="mla_kv_dw", out=(w_ukv.shape, BF))
    dw_uq = jnp.concatenate([dw_uq_pad[..., :QK_NOPE], _rope_unslab(dw_uq_pad[..., QK_NOPE:])], axis=-1).astype(BF)
    sib_r2b = _reduce_begin([dw_out_e.reshape(g_w_out_e.shape), dw_uq, dw_ukv], "r2b")
    dqn = mm(dq_lin, w_uq, tb=True, name="mla_q_dx", out=((T, Q_LORA), F32), deps=[sib_r2b[-1]])
    dkvn = mm(dkv_lin, w_ukv, tb=True, name="mla_kv_dx", out=((T, KV_LORA), F32), deps=[sib_r2b[-1]])
    grads_r2b, a_r2b, st_r2b = _reduce_continue(sib_r2b, dkvn, "r2b")
    dcq, d_qnorm = rms_bwd(proj, e_q_norm, dqn, col_block=0, want_f32=False, name="q_norm_bwd", deps=[st_r2b[-1]])
    dckv, d_kvnorm = rms_bwd(proj, e_kv_norm, dkvn, col_block=1, want_f32=False, name="kv_norm_bwd")
    dproj = jnp.concatenate([dcq, dckv, duv, dkr], axis=-1)
    dw_in_e_pad = mm(h0, dproj, ta=True, name="e_in_dw", out=(w_in_e.shape, F32), bn=_tile(w_in_e.shape[1], 640))
    dw_in_e = _unpack_w_in_grad(dw_in_e_pad).astype(BF)
    sib_r3 = _reduce_begin([dw_in_e], "r3")
    dh0 = mm(dproj, w_in_e, tb=True, name="e_in_dx", out=((T, D), F32), deps=[sib_r3[-1]])
    grads_r3, a_r3, st_r3 = _reduce_continue(sib_r3, dh0, "r3")
    tok_r3 = st_r3[-1]
    grad_x, d_enorm = rms_bwd(x0, e_norm_mix, dh0, dres=dx1, want_bf=False, name="e_norm_bwd", deps=[tok_r3])
    b_r2 = chips_wait(st_r2, grad_x, name="reduce_chips_wait_r2")

    def finish(grads, a_bufs, b_bufs, t, w, m, v, layer=0, prev=None, tag="", deps=()):
        return reduce_adam(grads[t], a_bufs[t], b_bufs[t], w, m, v, layer, prev, name=f"adam_{tag}", deps=deps)

    r_w1 = finish(grads_r0, a_r0, b_r0, 0, mlp_w1, m_mlp_w1, v_mlp_w1, 1, None, tag="w1_l1")
    r_w2 = finish(grads_r0, a_r0, b_r0, 1, mlp_w2, m_mlp_w2, v_mlp_w2, 1, None, tag="w2_l1")
    r_w_out_o = finish(grads_r1, a_r1, b_r1, 0, o_w_out, m_o_w_out, v_o_w_out, tag="o_w_out")
    r_w_in_o = finish(grads_r1, a_r1, b_r1, 1, o_w_in, m_o_w_in, v_o_w_in, tag="o_w_in")
    r_w1 = finish(grads_r2, a_r2, b_r2, 0, mlp_w1, m_mlp_w1, v_mlp_w1, 0, r_w1, tag="w1_l0", deps=[tok_r3])
    r_w2 = finish(grads_r2, a_r2, b_r2, 1, mlp_w2, m_mlp_w2, v_mlp_w2, 0, r_w2, tag="w2_l0", deps=[r_w1[1]])
    b_r2b = chips_wait(st_r2b, r_w2[1], name="reduce_chips_wait_r2b")
    r_w_out_e = finish(grads_r2b, a_r2b, b_r2b, 0, e_w_out, m_e_w_out, v_e_w_out, tag="e_w_out")
    r_w_uq = finish(grads_r2b, a_r2b, b_r2b, 1, e_w_uq, m_e_w_uq, v_e_w_uq, tag="e_w_uq")
    r_w_ukv = finish(grads_r2b, a_r2b, b_r2b, 2, e_w_ukv, m_e_w_ukv, v_e_w_ukv, tag="e_w_ukv")
    b_r3 = chips_wait(st_r3, r_w_out_e[1], name="reduce_chips_wait_r3")
    r_w_in = finish(grads_r3, a_r3, b_r3, 0, e_w_in, m_e_w_in, v_e_w_in, tag="e_w_in")

    d_sgu_b = jnp.transpose(d_b_full[:, ::CH])
    d_sgu_w_tril = jnp.tril(d_sgu_w)
    rep = [("e_norm_mix", e_norm_mix, m_e_norm_mix, v_e_norm_mix, d_enorm),
           ("e_q_norm", e_q_norm, m_e_q_norm, v_e_q_norm, d_qnorm),
           ("e_kv_norm", e_kv_norm, m_e_kv_norm, v_e_kv_norm, d_kvnorm),
           ("e_v_norm", e_v_norm, m_e_v_norm, v_e_v_norm, d_vgain),
           ("e_sgu_w", e_sgu_w, m_e_sgu_w, v_e_sgu_w, d_sgu_w_tril),
           ("e_sgu_b", e_sgu_b, m_e_sgu_b, v_e_sgu_b, d_sgu_b),
           ("e_mla_out_norm", e_mla_out_norm, m_e_mla_out_norm, v_e_mla_out_norm, d_mla_out),
           ("e_sgu_out_norm", e_sgu_out_norm, m_e_sgu_out_norm, v_e_sgu_out_norm, d_sgu_out),
           ("mlp_norm", mlp_norm, m_mlp_norm, v_mlp_norm, jnp.concatenate([d_mlp0, d_mlp1], axis=0)),
           ("final_norm", final_norm, m_final_norm, v_final_norm, d_final)]
    sizes = [int(np.prod(r[1].shape)) for r in rep]
    n_rep = sum(sizes)
    n_all = n_rep + 4 * D
    width = -(-n_all // (8 * LANES)) * LANES
    pad = 8 * width - n_all
    flat = jnp.concatenate([r[4].reshape(-1) for r in rep]
                           + [d_onorm_full.reshape(-1), dconv_full.reshape(-1), jnp.zeros((pad,), F32)])
    summed = sum_rows8(all_gather_vmem(flat.reshape(8, width), name="gather_small_grads", after=b_r3[0]), 8,
                       name="sum_small_grads").reshape(-1)

    def pack_rep(i):
        return jnp.concatenate([r[i].reshape(-1) for r in rep]).reshape(n_rep // LANES, LANES)

    g_rep = summed[:n_rep].reshape(n_rep // LANES, LANES)
    d_rep, nm_rep, nv_rep = adam_flat(g_rep, pack_rep(1), pack_rep(2), pack_rep(3), name="adam_replicated")

    def unpack_rep(flat2d):
        out, off = {}, 0
        f = flat2d.reshape(-1)
        for r, n in zip(rep, sizes):
            out[r[0]] = f[off:off + n].reshape(r[1].shape)
            off += n
        return out

    small = {"grad": unpack_rep(g_rep), "delta": unpack_rep(d_rep), "new_m": unpack_rep(nm_rep),
             "new_v": unpack_rep(nv_rep)}
    g_onorm = lax.dynamic_slice(summed[n_rep:n_rep + D].reshape(1, D), (0, me * d_shard), (1, d_shard))
    g_conv = lax.dynamic_slice(summed[n_rep + D:n_rep + 4 * D].reshape(3, D), (0, me * d_shard), (3, d_shard))

    def pack_sharded(norm_part, conv_part):
        return jnp.concatenate([norm_part, conv_part, jnp.zeros((4, d_shard), F32)], axis=0)

    g_sh = pack_sharded(g_onorm, g_conv)
    d_sh, nm_sh, nv_sh = adam_flat(g_sh, pack_sharded(o_norm_mix, o_conv_w[0]), pack_sharded(m_o_norm_mix, m_o_conv_w[0]),
                                   pack_sharded(v_o_norm_mix, v_o_conv_w[0]), name="adam_sharded_small")
    for kind, arr in (("grad", g_sh), ("delta", d_sh), ("new_m", nm_sh), ("new_v", nv_sh)):
        small[kind]["o_norm_mix"] = arr[0:1]
        small[kind]["o_conv_w"] = arr[1:4][None]

    big = {"e_w_in": r_w_in, "e_w_uq": r_w_uq, "e_w_ukv": r_w_ukv, "e_w_out": r_w_out_e, "o_w_in": r_w_in_o,
           "o_w_out": r_w_out_o, "mlp_w1": r_w1, "mlp_w2": r_w2}
    order = ["e_norm_mix", "e_w_in", "e_q_norm", "e_w_uq", "e_kv_norm", "e_w_ukv", "e_v_norm", "e_sgu_w", "e_sgu_b",
             "e_mla_out_norm", "e_sgu_out_norm", "e_w_out", "o_norm_mix", "o_w_in", "o_conv_w", "o_w_out", "mlp_norm",
             "mlp_w1", "mlp_w2", "final_norm"]
    result = [loss, grad_x[None]]
    for ki, kind in enumerate(("grad", "delta", "new_m", "new_v")):
        for nm in order:
            result.append(big[nm][ki] if nm in big else small[kind][nm])
    return tuple(result)
```

```python
import functools

import numpy as np
import jax
import jax.numpy as jnp
from jax import lax
from jax.experimental import pallas as pl
from jax.experimental.pallas import tpu as pltpu

BF = jnp.bfloat16
F32 = jnp.float32
MESH = pl.DeviceIdType.MESH
AXES = ("x", "y", "c")
N_DEV = 8

EPS = 1e-6
HEADS = 8
Q_LORA = 512
KV_LORA = 512
QK_NOPE = 128
QK_ROPE = 64
HALF_ROPE = QK_ROPE // 2
V_HEAD = 128
HEAD_PAD = 256
ROPE_BASE = 10000.0
GROUPS = 8
CH = 128
CHUNK = 128
SGU_OUT = GROUPS * CH
MLA_OUT = HEADS * V_HEAD
ATTN_SCALE = float((QK_NOPE + QK_ROPE) ** -0.5)

ADAM_LR = 0.001
ADAM_B1 = 0.9
ADAM_B2 = 0.999
ADAM_EPS = 1e-08
ADAM_WD = 0.01
ADAM_STEP = 10
ADAM_C1 = 1.0 - ADAM_B1 ** ADAM_STEP
ADAM_C2 = 1.0 - ADAM_B2 ** ADAM_STEP

V7X_VMEM_BYTES = 64 * 2 ** 20
VMEM_LIMIT_CAP = V7X_VMEM_BYTES - 6 * 2 ** 20
LANES = 128
ROW_TILE = 256
ATTN_TILE = 512
MM_TILE = 1024
MM_K_TILE = 2048
MM_K_BLOCK_MAX = 3072


def _padded_bytes(block, dtype):
    dims = [d for d in block if d is not None]
    if len(dims) >= 1:
        dims[-1] = -(-dims[-1] // LANES) * LANES
    if len(dims) >= 2:
        dims[-2] = -(-dims[-2] // 16) * 16
    return int(np.prod(dims)) * jnp.dtype(dtype).itemsize


def _pcall(body, *, name, grid, ins, outs, scratch=(), semantics=None, aliases=None, prefetch=None, deps=()):
    any_spec = pl.BlockSpec(memory_space=pl.ANY)
    if deps:
        n_lead = len(ins) + (1 if prefetch is not None else 0)
        n_deps = len(deps)
        inner = body

        def body(*refs):
            inner(*refs[:n_lead], *refs[n_lead + n_deps:])

        ins = list(ins) + [(d, None, None) for d in deps]
    in_specs = [any_spec if b is None else pl.BlockSpec(b, m) for _, b, m in ins]
    out_specs = [any_spec if b is None else pl.BlockSpec(b, m) for _, _, b, m in outs]
    out_shape = [pltpu.HBM(s, d) for s, d, _, _ in outs]
    est = 0
    for a, b, _ in ins:
        if b is not None:
            est += 2 * _padded_bytes(b, a.dtype)
    for _, d, b, _ in outs:
        if b is not None:
            est += 2 * _padded_bytes(b, d)
    for s in scratch:
        if hasattr(s, "shape") and hasattr(s, "dtype"):
            est += _padded_bytes(s.shape, s.dtype)
    limit = int(min(VMEM_LIMIT_CAP, est + 16 * 2 ** 20))
    params = pltpu.CompilerParams(
        dimension_semantics=semantics or ("arbitrary",) * len(grid), vmem_limit_bytes=limit)
    args = [pltpu.with_memory_space_constraint(a, pltpu.HBM) for a, _, _ in ins]
    if prefetch is not None:
        grid_spec = pltpu.PrefetchScalarGridSpec(
            num_scalar_prefetch=1, grid=grid, in_specs=in_specs, out_specs=out_specs, scratch_shapes=list(scratch))
        call = pl.pallas_call(body, out_shape=out_shape, grid_spec=grid_spec, name=name, compiler_params=params,
                              input_output_aliases=aliases or {})
        return call(prefetch, *args)
    call = pl.pallas_call(body, out_shape=out_shape, grid=grid, in_specs=in_specs, out_specs=out_specs,
                          scratch_shapes=list(scratch), name=name, compiler_params=params,
                          input_output_aliases=aliases or {})
    return call(*args)


def _tile(dim, pref, quantum=LANES):
    if dim <= pref:
        return dim
    t = (pref // quantum) * quantum
    while t >= quantum:
        if dim % t == 0:
            return t
        t -= quantum
    return dim


def _vshape(arr_shape):
    if len(arr_shape) == 2:
        return tuple(arr_shape)
    s, r, c = arr_shape
    return (r, s * c)


def _vblock(arr_shape, br, bc, rc):
    if len(arr_shape) == 2:
        return (br, bc), (lambda *g: rc(*g))
    _, _, c = arr_shape
    assert c % bc == 0, (arr_shape, bc)
    per = c // bc

    def imap(*g):
        ri, ci = rc(*g)
        return (ci // per, ri, ci % per)

    return (None, br, bc), imap


def _shard_width(*shapes):
    w = None
    for s in shapes:
        if len(s) == 3:
            w = s[2] if w is None else int(np.gcd(w, s[2]))
    return w


def mm(a, b, *, name, ta=False, tb=False, out=None, outs=None, epi=None, epi_ins=(), bm=None, bn=None, bk=None,
       deps=()):
    av, bv = _vshape(a.shape), _vshape(b.shape)
    M, K = (av[1], av[0]) if ta else av
    K2, N = (bv[1], bv[0]) if tb else bv
    assert K == K2, (a.shape, b.shape, ta, tb)
    if outs is None:
        outs = [(out[0], out[1], None)]
    a_sw = _shard_width(a.shape)
    b_sw = _shard_width(b.shape)
    o_sw = _shard_width(*[o[0] for o in outs])
    m_lim = a_sw if (ta and a_sw) else None
    k_lim = [w for w in ((a_sw if not ta else None), (b_sw if tb else None)) if w]
    n_lim = [w for w in ((b_sw if not tb else None), o_sw) if w]
    if bm is None:
        bm = _tile(M, min([MM_TILE] + ([m_lim] if m_lim else [])))
    if bn is None:
        bn = _tile(N, min([MM_TILE] + n_lim))
    k_shards = 0
    if tb and len(b.shape) == 3 and bk is None and not (a_sw and not ta):
        k_shards = 1
        while 2 * k_shards <= b.shape[0] and 2 * k_shards * b_sw <= MM_K_BLOCK_MAX:
            k_shards *= 2
        bk = k_shards * b_sw
    if bk is None:
        bk = K if (K <= 4096 and not k_lim) else _tile(K, min([MM_K_TILE] + k_lim))
    assert M % bm == 0 and N % bn == 0 and K % bk == 0, (name, M, N, K, bm, bn, bk)
    nk = K // bk
    grid = (M // bm, N // bn, nk)
    if ta:
        a_blk, a_map = _vblock(a.shape, bk, bm, lambda i, j, k: (k, i))
    else:
        a_blk, a_map = _vblock(a.shape, bm, bk, lambda i, j, k: (i, k))
    if k_shards:
        b_blk, b_map = (k_shards, bn, b_sw), (lambda i, j, k: (k, j, 0))
    elif tb:
        b_blk, b_map = _vblock(b.shape, bn, bk, lambda i, j, k: (j, k))
    else:
        b_blk, b_map = _vblock(b.shape, bk, bn, lambda i, j, k: (k, j))
    dn = (((0 if ta else 1,), (1 if tb else 0,)), ((), ()))
    ins = [(a, a_blk, a_map), (b, b_blk, b_map)] + list(epi_ins)
    out_list = []
    for shape, dtype, cols in outs:
        cols = cols or bn
        blk, imap = _vblock(shape, bm, cols, lambda i, j, k: (i, j))
        out_list.append((shape, dtype, blk, imap))
    n_e, n_o = len(epi_ins), len(out_list)

    def body(*refs):
        a_ref, b_ref = refs[0], refs[1]
        e_refs = refs[2:2 + n_e]
        o_refs = refs[2 + n_e:2 + n_e + n_o]

        def finish(acc):
            res = epi(acc, *e_refs) if epi is not None else (acc,)
            for o_ref, r in zip(o_refs, res):
                o_ref[...] = r.astype(o_ref.dtype)

        x = a_ref[...].astype(BF)
        y = b_ref[...].astype(BF)
        if k_shards:
            p = None
            for s in range(k_shards):
                part = lax.dot_general(x[:, s * b_sw:(s + 1) * b_sw], y[s], dn, preferred_element_type=F32)
                p = part if p is None else p + part
        else:
            p = lax.dot_general(x, y, dn, preferred_element_type=F32)
        if nk == 1:
            finish(p)
        else:
            acc_ref = refs[-1]
            k = pl.program_id(2)

            @pl.when(k == 0)
            def _():
                acc_ref[...] = p

            @pl.when(k > 0)
            def _():
                acc_ref[...] += p

            @pl.when(k == nk - 1)
            def _():
                finish(acc_ref[...])

    scratch = [pltpu.VMEM((bm, bn), F32)] if nk > 1 else []
    res = _pcall(body, name=name, grid=grid, ins=ins, outs=out_list, scratch=scratch,
                 semantics=("parallel", "parallel", "arbitrary"), deps=deps)
    return res[0] if len(res) == 1 else res


_GELU_K = float(np.sqrt(2.0 / np.pi))
_GELU_C = 0.044715


def _gelu(x):
    t = jnp.tanh(_GELU_K * (x + _GELU_C * (x * x * x)))
    return 0.5 * x * (1.0 + t)


def _gelu_grad(x):
    t = jnp.tanh(_GELU_K * (x + _GELU_C * (x * x * x)))
    return 0.5 * (1.0 + t) + 0.5 * x * (1.0 - t * t) * (_GELU_K * (1.0 + 3.0 * _GELU_C * (x * x)))


def _rstd(x):
    return lax.rsqrt(jnp.mean(x * x, axis=-1, keepdims=True) + EPS)


def _rms_bwd(x, gain, dy):
    r = _rstd(x)
    xh = x * r
    gdy = dy * gain
    dx = r * (gdy - xh * jnp.mean(gdy * xh, axis=-1, keepdims=True))
    return dx, dy * xh


def _rope_fwd(x, cos_t, sin_t):
    return x * cos_t + pltpu.roll(x, 2 * HALF_ROPE, 1) * sin_t


def _rope_bwd(dy, cos_t, sin_t):
    return dy * cos_t + pltpu.roll(dy * sin_t, 2 * HALF_ROPE, 1)


def _acc_rows(ref, val, first):
    s = jnp.sum(val, axis=0, keepdims=True)

    @pl.when(first)
    def _():
        ref[...] = s

    @pl.when(jnp.logical_not(first))
    def _():
        ref[...] += s


def rms_fwd(x, gain, *, name, col_block=0, width=None, deps=()):
    T = x.shape[0]
    width = width or x.shape[1]
    tm = _tile(T, ROW_TILE, 8)

    def body(x_ref, g_ref, o_ref):
        v = x_ref[...]
        o_ref[...] = (v * _rstd(v) * g_ref[...]).astype(BF)

    return _pcall(body, name=name, grid=(T // tm,),
                  ins=[(x, (tm, width), lambda i: (i, col_block)), (gain, (1, width), lambda i: (0, 0))],
                  outs=[((T, width), BF, (tm, width), lambda i: (i, 0))], semantics=("parallel",), deps=deps)[0]


def rms_bwd(x, gain, dy, *, name, col_block=0, dres=None, want_f32=True, want_bf=True, deps=()):
    T, width = dy.shape
    tm = _tile(T, ROW_TILE, 8)
    has_res = dres is not None

    def body(*refs):
        x_ref, g_ref, dy_ref = refs[:3]
        pos = 3
        res_ref = None
        if has_res:
            res_ref = refs[pos]
            pos += 1
        outs = refs[pos:]
        dx, dg_rows = _rms_bwd(x_ref[...], g_ref[...], dy_ref[...])
        if has_res:
            dx = dx + res_ref[...]
        o = 0
        if want_f32:
            outs[o][...] = dx
            o += 1
        if want_bf:
            outs[o][...] = dx.astype(BF)
            o += 1
        _acc_rows(outs[o], dg_rows, pl.program_id(0) == 0)

    ins = [(x, (tm, width), lambda i: (i, col_block)), (gain, (1, width), lambda i: (0, 0)),
           (dy, (tm, width), lambda i: (i, 0))]
    if has_res:
        ins.append((dres, (tm, width), lambda i: (i, 0)))
    outs = []
    if want_f32:
        outs.append(((T, width), F32, (tm, width), lambda i: (i, 0)))
    if want_bf:
        outs.append(((T, width), BF, (tm, width), lambda i: (i, 0)))
    outs.append(((1, width), F32, (1, width), lambda i: (0, 0)))
    return _pcall(body, name=name, grid=(T // tm,), ins=ins, outs=outs, deps=deps)


def mla_prep(proj, q_norm, kv_norm, cos_t, sin_t, *, name):
    T = proj.shape[0]
    tm = _tile(T, ROW_TILE, 8)
    kr_block = (proj.shape[1] - LANES) // LANES

    def body(cq_ref, ckv_ref, kr_ref, qg_ref, kg_ref, cos_ref, sin_ref, qn_ref, kvn_ref, krope_ref):
        cq = cq_ref[...]
        qn_ref[...] = (cq * _rstd(cq) * qg_ref[...]).astype(BF)
        ckv = ckv_ref[...]
        kvn_ref[...] = (ckv * _rstd(ckv) * kg_ref[...]).astype(BF)
        krope_ref[...] = _rope_fwd(kr_ref[...], cos_ref[...], sin_ref[...]).astype(BF)

    return _pcall(
        body, name=name, grid=(T // tm,),
        ins=[(proj, (tm, Q_LORA), lambda i: (i, 0)), (proj, (tm, KV_LORA), lambda i: (i, 1)),
             (proj, (tm, LANES), lambda i: (i, kr_block)),
             (q_norm, (1, Q_LORA), lambda i: (0, 0)), (kv_norm, (1, KV_LORA), lambda i: (0, 0)),
             (cos_t, (tm, LANES), lambda i: (i, 0)), (sin_t, (tm, LANES), lambda i: (i, 0))],
        outs=[((T, Q_LORA), BF, (tm, Q_LORA), lambda i: (i, 0)), ((T, KV_LORA), BF, (tm, KV_LORA), lambda i: (i, 0)),
              ((T, LANES), BF, (tm, LANES), lambda i: (i, 0))],
        semantics=("parallel",))


def _attn_scores(q, k_blk, diagonal):
    s = lax.dot_general(q, k_blk, (((1,), (1,)), ((), ())), preferred_element_type=F32) * ATTN_SCALE
    if diagonal:
        row = lax.broadcasted_iota(jnp.int32, s.shape, 0)
        col = lax.broadcasted_iota(jnp.int32, s.shape, 1)
        s = jnp.where(col <= row, s, -jnp.inf)
    return s


def attn_fwd(q, k, v, *, name):
    T = q.shape[0]
    tq = _tile(T, ATTN_TILE, 8)

    def body(q_ref, k_ref, v_ref, o_ref, lse_ref):
        i = pl.program_id(1)
        qv = q_ref[...]

        def block(kb, carry, diagonal):
            m, l, acc = carry
            start = pl.multiple_of(kb * tq, tq)
            s = _attn_scores(qv, k_ref[pl.ds(start, tq), :], diagonal)
            m_new = jnp.maximum(m, jnp.max(s, axis=-1, keepdims=True))
            alpha = jnp.exp(m - m_new)
            p = jnp.exp(s - m_new)
            l = alpha * l + jnp.sum(p, axis=-1, keepdims=True)
            acc = alpha * acc + jnp.dot(p.astype(BF), v_ref[pl.ds(start, tq), :], preferred_element_type=F32)
            return m_new, l, acc

        init = (jnp.full((tq, 1), -jnp.inf, F32), jnp.zeros((tq, 1), F32), jnp.zeros((tq, V_HEAD), F32))
        carry = lax.fori_loop(0, i, lambda kb, c: block(kb, c, False), init)
        m, l, acc = block(i, carry, True)
        o_ref[...] = acc / l
        lse_ref[...] = jnp.broadcast_to(m + jnp.log(l), (tq, V_HEAD))

    return _pcall(
        body, name=name, grid=(HEADS, T // tq),
        ins=[(q, (tq, HEAD_PAD), lambda h, i: (i, h)), (k, (T, HEAD_PAD), lambda h, i: (0, h)),
             (v, (T, V_HEAD), lambda h, i: (0, h))],
        outs=[((T, MLA_OUT), F32, (tq, V_HEAD), lambda h, i: (i, h)),
              ((T, MLA_OUT), F32, (tq, V_HEAD), lambda h, i: (i, h))], semantics=("parallel", "parallel"))


def attn_bwd(q, k, v, o, lse, do, *, name):
    T = q.shape[0]
    tq = _tile(T, ATTN_TILE, 8)

    def body(q_ref, k_ref, v_ref, o_ref, lse_ref, do_ref, dq_ref, dk_ref, dv_ref):
        i = pl.program_id(1)

        @pl.when(i == 0)
        def _():
            dk_ref[...] = jnp.zeros_like(dk_ref)
            dv_ref[...] = jnp.zeros_like(dv_ref)

        qv = q_ref[...]
        do_t = do_ref[...]
        lse_v = lse_ref[:, 0:1]
        delta = jnp.sum(do_t.astype(F32) * o_ref[...], axis=-1, keepdims=True)

        def block(kb, dq, diagonal):
            start = pl.multiple_of(kb * tq, tq)
            k_blk = k_ref[pl.ds(start, tq), :]
            v_blk = v_ref[pl.ds(start, tq), :]
            p = jnp.exp(_attn_scores(qv, k_blk, diagonal) - lse_v)
            dp = lax.dot_general(do_t, v_blk, (((1,), (1,)), ((), ())), preferred_element_type=F32)
            ds = (p * (dp - delta) * ATTN_SCALE).astype(BF)
            dk_ref[pl.ds(start, tq), :] += lax.dot_general(ds, qv, (((0,), (0,)), ((), ())), preferred_element_type=F32)
            dv_ref[pl.ds(start, tq), :] += lax.dot_general(p.astype(BF), do_t, (((0,), (0,)), ((), ())),
                                                          preferred_element_type=F32)
            return dq + jnp.dot(ds, k_blk, preferred_element_type=F32)

        dq = lax.fori_loop(0, i, lambda kb, c: block(kb, c, False), jnp.zeros((tq, HEAD_PAD), F32))
        dq_ref[...] = block(i, dq, True)

    return _pcall(
        body, name=name, grid=(HEADS, T // tq),
        ins=[(q, (tq, HEAD_PAD), lambda h, i: (i, h)), (k, (T, HEAD_PAD), lambda h, i: (0, h)),
             (v, (T, V_HEAD), lambda h, i: (0, h)), (o, (tq, V_HEAD), lambda h, i: (i, h)),
             (lse, (tq, V_HEAD), lambda h, i: (i, h)), (do, (tq, V_HEAD), lambda h, i: (i, h))],
        outs=[((T, HEADS * HEAD_PAD), F32, (tq, HEAD_PAD), lambda h, i: (i, h)),
              ((T, HEADS * HEAD_PAD), F32, (T, HEAD_PAD), lambda h, i: (0, h)),
              ((T, MLA_OUT), F32, (T, V_HEAD), lambda h, i: (0, h))],
        semantics=("parallel", "arbitrary"))


def mla_bwd_prep(dq, dk, dv, cos_t, sin_t, *, name):
    T = dq.shape[0]
    tm = _tile(T, ROW_TILE, 8)

    def body(dq_ref, dk_ref, dv_ref, cos_ref, sin_ref, dql_ref, dkvl_ref, dkr_ref):
        cos_v, sin_v = cos_ref[...], sin_ref[...]
        kr = jnp.zeros((tm, LANES), F32)
        for h in range(HEADS):
            lo = h * HEAD_PAD
            dql_ref[:, lo:lo + QK_NOPE] = dq_ref[:, lo:lo + QK_NOPE].astype(BF)
            dql_ref[:, lo + QK_NOPE:lo + HEAD_PAD] = _rope_bwd(
                dq_ref[:, lo + QK_NOPE:lo + HEAD_PAD], cos_v, sin_v).astype(BF)
            dkvl_ref[:, lo:lo + QK_NOPE] = dk_ref[:, lo:lo + QK_NOPE].astype(BF)
            dkvl_ref[:, lo + QK_NOPE:lo + HEAD_PAD] = dv_ref[:, h * V_HEAD:(h + 1) * V_HEAD].astype(BF)
            kr = kr + dk_ref[:, lo + QK_NOPE:lo + HEAD_PAD]
        dkr_ref[...] = _rope_bwd(kr, cos_v, sin_v).astype(BF)

    W = HEADS * HEAD_PAD
    return _pcall(
        body, name=name, grid=(T // tm,),
        ins=[(dq, (tm, W), lambda i: (i, 0)), (dk, (tm, W), lambda i: (i, 0)), (dv, (tm, MLA_OUT), lambda i: (i, 0)),
             (cos_t, (tm, LANES), lambda i: (i, 0)), (sin_t, (tm, LANES), lambda i: (i, 0))],
        outs=[((T, W), BF, (tm, W), lambda i: (i, 0)), ((T, W), BF, (tm, W), lambda i: (i, 0)),
              ((T, LANES), BF, (tm, LANES), lambda i: (i, 0))],
        semantics=("parallel",))


def _group_norm_stats(vg):
    mu = jnp.mean(vg, axis=-1, keepdims=True)
    d = vg - mu
    r = lax.rsqrt(jnp.mean(d * d, axis=-1, keepdims=True) + EPS)
    return d * r, r


def mix_fwd(a, proj, g_mla, g_sgu, v_gain, w_tril, b_full, *, name):
    T = a.shape[0]
    tm = _tile(T, ROW_TILE, CHUNK)
    n_chunk = tm // CHUNK

    def body(a_ref, u_ref, v_ref, gm_ref, gs_ref, vg_ref, w_ref, b_ref, o_ref, s_scr):
        av = a_ref[...]
        o_ref[:, :MLA_OUT] = (av * _rstd(av) * gm_ref[...]).astype(BF)
        for g in range(GROUPS):
            sl = slice(g * CH, (g + 1) * CH)
            vhat, _ = _group_norm_stats(_gelu(v_ref[:, sl]))
            vn = (vhat * vg_ref[:, sl]).astype(BF)
            u = _gelu(u_ref[:, sl])
            for ci in range(n_chunk):
                rs = slice(ci * CHUNK, (ci + 1) * CHUNK)
                y = jnp.dot(w_ref[g], vn[rs], preferred_element_type=F32) + b_ref[:, sl]
                s_scr[rs, sl] = u[rs] * y
        s = s_scr[...]
        o_ref[:, MLA_OUT:] = (s * _rstd(s) * gs_ref[...]).astype(BF)

    return _pcall(
        body, name=name, grid=(T // tm,),
        ins=[(a, (tm, MLA_OUT), lambda i: (i, 0)), (proj, (tm, SGU_OUT), lambda i: (i, 1)),
             (proj, (tm, SGU_OUT), lambda i: (i, 2)), (g_mla, (1, MLA_OUT), lambda i: (0, 0)),
             (g_sgu, (1, SGU_OUT), lambda i: (0, 0)), (v_gain, (1, SGU_OUT), lambda i: (0, 0)),
             (w_tril, (GROUPS, CHUNK, CHUNK), lambda i: (0, 0, 0)), (b_full, (CHUNK, SGU_OUT), lambda i: (0, 0))],
        outs=[((T, MLA_OUT + SGU_OUT), BF, (tm, MLA_OUT + SGU_OUT), lambda i: (i, 0))],
        scratch=[pltpu.VMEM((tm, SGU_OUT), F32)], semantics=("parallel",))[0]


def mix_bwd(dmixed, a, proj, g_mla, g_sgu, v_gain, w_tril, w_tril_t, b_full, *, name):
    T = a.shape[0]
    tm = _tile(T, ROW_TILE, CHUNK)
    n_chunk = tm // CHUNK

    def body(dm_a_ref, dm_s_ref, a_ref, u_ref, v_ref, gm_ref, gs_ref, vg_ref, w_ref, wt_ref, b_ref,
             da_ref, duv_ref, dgm_ref, dgs_ref, dvg_ref, dw_ref, db_ref, s_scr, y_scr):
        first = pl.program_id(0) == 0
        da, dgm_rows = _rms_bwd(a_ref[...], gm_ref[...], dm_a_ref[...])
        da_ref[...] = da.astype(BF)
        _acc_rows(dgm_ref, dgm_rows, first)

        for g in range(GROUPS):
            sl = slice(g * CH, (g + 1) * CH)
            vhat, _ = _group_norm_stats(_gelu(v_ref[:, sl]))
            vn = (vhat * vg_ref[:, sl]).astype(BF)
            u = _gelu(u_ref[:, sl])
            for ci in range(n_chunk):
                rs = slice(ci * CHUNK, (ci + 1) * CHUNK)
                y = jnp.dot(w_ref[g], vn[rs], preferred_element_type=F32) + b_ref[:, sl]
                y_scr[rs, sl] = y
                s_scr[rs, sl] = u[rs] * y
        ds, dgs_rows = _rms_bwd(s_scr[...], gs_ref[...], dm_s_ref[...])
        _acc_rows(dgs_ref, dgs_rows, first)
        s_scr[...] = ds

        @pl.when(first)
        def _():
            dw_ref[...] = jnp.zeros_like(dw_ref)
            db_ref[...] = jnp.zeros_like(db_ref)

        for g in range(GROUPS):
            sl = slice(g * CH, (g + 1) * CH)
            upre = u_ref[:, sl]
            vpre = v_ref[:, sl]
            u = _gelu(upre)
            vhat, r = _group_norm_stats(_gelu(vpre))
            gain = vg_ref[:, sl]
            vn = (vhat * gain).astype(BF)
            dsg = s_scr[:, sl]
            duv_ref[:, sl] = (dsg * y_scr[:, sl] * _gelu_grad(upre)).astype(BF)
            dy = dsg * u
            dyb = dy.astype(BF)
            dvn_parts = []
            for ci in range(n_chunk):
                rs = slice(ci * CHUNK, (ci + 1) * CHUNK)
                dvn_parts.append(jnp.dot(wt_ref[g], dyb[rs], preferred_element_type=F32))
                dw_ref[g] += lax.dot_general(dyb[rs], vn[rs], (((1,), (1,)), ((), ())), preferred_element_type=F32)
                db_ref[:, sl] += jnp.broadcast_to(jnp.sum(dy[rs], axis=-1, keepdims=True), (CHUNK, CH))
            dvn = dvn_parts[0] if n_chunk == 1 else jnp.concatenate(dvn_parts, axis=0)
            _acc_rows(dvg_ref.at[:, sl], dvn * vhat, first)
            dvh = dvn * gain
            dvg = r * (dvh - jnp.mean(dvh, axis=-1, keepdims=True)
                       - vhat * jnp.mean(dvh * vhat, axis=-1, keepdims=True))
            duv_ref[:, SGU_OUT + g * CH:SGU_OUT + (g + 1) * CH] = (dvg * _gelu_grad(vpre)).astype(BF)

    return _pcall(
        body, name=name, grid=(T // tm,),
        ins=[(dmixed, (tm, MLA_OUT), lambda i: (i, 0)), (dmixed, (tm, SGU_OUT), lambda i: (i, 1)),
             (a, (tm, MLA_OUT), lambda i: (i, 0)), (proj, (tm, SGU_OUT), lambda i: (i, 1)),
             (proj, (tm, SGU_OUT), lambda i: (i, 2)), (g_mla, (1, MLA_OUT), lambda i: (0, 0)),
             (g_sgu, (1, SGU_OUT), lambda i: (0, 0)), (v_gain, (1, SGU_OUT), lambda i: (0, 0)),
             (w_tril, (GROUPS, CHUNK, CHUNK), lambda i: (0, 0, 0)), (w_tril_t, (GROUPS, CHUNK, CHUNK), lambda i: (0, 0, 0)),
             (b_full, (CHUNK, SGU_OUT), lambda i: (0, 0))],
        outs=[((T, MLA_OUT), BF, (tm, MLA_OUT), lambda i: (i, 0)),
              ((T, 2 * SGU_OUT), BF, (tm, 2 * SGU_OUT), lambda i: (i, 0)),
              ((1, MLA_OUT), F32, (1, MLA_OUT), lambda i: (0, 0)), ((1, SGU_OUT), F32, (1, SGU_OUT), lambda i: (0, 0)),
              ((1, SGU_OUT), F32, (1, SGU_OUT), lambda i: (0, 0)),
              ((GROUPS, CHUNK, CHUNK), F32, (GROUPS, CHUNK, CHUNK), lambda i: (0, 0, 0)),
              ((CHUNK, SGU_OUT), F32, (CHUNK, SGU_OUT), lambda i: (0, 0))],
        scratch=[pltpu.VMEM((tm, SGU_OUT), F32), pltpu.VMEM((tm, SGU_OUT), F32)])


def _shift_down(z, n, row):
    return jnp.where(row >= n, pltpu.roll(z, n, 0), 0.0)


def _shift_up(z, n, row, T):
    return jnp.where(row < T - n, pltpu.roll(z, T - n, 0), 0.0)


def conv_fwd(proj, conv_w, *, name):
    T, D3 = proj.shape
    D = D3 // 3
    tn = _tile(D, 256)
    nj = D // tn

    def body(b_ref, c_ref, x_ref, w_ref, o_ref):
        row = lax.broadcasted_iota(jnp.int32, (T, tn), 0)
        z = c_ref[...] * x_ref[...]
        zc = w_ref[2:3, :] * z + w_ref[1:2, :] * _shift_down(z, 1, row) + w_ref[0:1, :] * _shift_down(z, 2, row)
        o_ref[...] = (b_ref[...] * zc).astype(BF)

    return _pcall(
        body, name=name, grid=(nj,),
        ins=[(proj, (T, tn), lambda j: (0, j)), (proj, (T, tn), lambda j: (0, nj + j)),
             (proj, (T, tn), lambda j: (0, 2 * nj + j)), (conv_w, (3, tn), lambda j: (0, j))],
        outs=[((T, D), BF, (T, tn), lambda j: (0, j))], semantics=("parallel",))[0]


def conv_bwd(dg, proj, conv_w, *, name):
    T, D3 = proj.shape
    D = D3 // 3
    tn = _tile(D, 256)
    nj = D // tn

    def body(dg_ref, b_ref, c_ref, x_ref, w_ref, dp_ref, dw_ref, dc_scr, dx_scr):
        part = pl.program_id(1)

        @pl.when(part == 0)
        def _():
            row = lax.broadcasted_iota(jnp.int32, (T, tn), 0)
            c, x = c_ref[...], x_ref[...]
            z = c * x
            z1 = _shift_down(z, 1, row)
            z2 = _shift_down(z, 2, row)
            dgv = dg_ref[...]
            zc = w_ref[2:3, :] * z + w_ref[1:2, :] * z1 + w_ref[0:1, :] * z2
            dp_ref[...] = (dgv * zc).astype(BF)
            dzc = dgv * b_ref[...]
            dw_ref[0:1, :] = jnp.sum(dzc * z2, axis=0, keepdims=True)
            dw_ref[1:2, :] = jnp.sum(dzc * z1, axis=0, keepdims=True)
            dw_ref[2:3, :] = jnp.sum(dzc * z, axis=0, keepdims=True)
            dz = (w_ref[2:3, :] * dzc + w_ref[1:2, :] * _shift_up(dzc, 1, row, T)
                  + w_ref[0:1, :] * _shift_up(dzc, 2, row, T))
            dc_scr[...] = (dz * x).astype(BF)
            dx_scr[...] = (dz * c).astype(BF)

        @pl.when(part == 1)
        def _():
            dp_ref[...] = dc_scr[...]

        @pl.when(part == 2)
        def _():
            dp_ref[...] = dx_scr[...]

    return _pcall(
        body, name=name, grid=(nj, 3),
        ins=[(dg, (T, tn), lambda j, p: (0, j)), (proj, (T, tn), lambda j, p: (0, j)),
             (proj, (T, tn), lambda j, p: (0, nj + j)), (proj, (T, tn), lambda j, p: (0, 2 * nj + j)),
             (conv_w, (3, tn), lambda j, p: (0, j))],
        outs=[((T, D3), BF, (T, tn), lambda j, p: (0, p * nj + j)), ((3, D), F32, (3, tn), lambda j, p: (0, j))],
        scratch=[pltpu.VMEM((T, tn), BF), pltpu.VMEM((T, tn), BF)], semantics=("parallel", "arbitrary"))


def loss_bwd(x, gain, target, *, name):
    T, D = x.shape
    tm = _tile(T, ROW_TILE, 8)

    def body(x_ref, g_ref, t_ref, dx_ref, dxb_ref, dg_ref, loss_ref):
        first = pl.program_id(0) == 0
        xv = x_ref[...]
        r = _rstd(xv)
        xh = xv * r
        gain_v = g_ref[...]
        err = xh * gain_v - t_ref[...]
        part = 0.5 * jnp.sum(jnp.mean(err * err, axis=-1, keepdims=True), axis=0, keepdims=True)
        _acc_rows(loss_ref, jnp.broadcast_to(part, (1, LANES)), first)
        dy = err * (1.0 / D)
        gdy = dy * gain_v
        dx = r * (gdy - xh * jnp.mean(gdy * xh, axis=-1, keepdims=True))
        dx_ref[...] = dx
        dxb_ref[...] = dx.astype(BF)
        _acc_rows(dg_ref, dy * xh, first)

    return _pcall(
        body, name=name, grid=(T // tm,),
        ins=[(x, (tm, D), lambda i: (i, 0)), (gain, (1, D), lambda i: (0, 0)), (target, (tm, D), lambda i: (i, 0))],
        outs=[((T, D), F32, (tm, D), lambda i: (i, 0)), ((T, D), BF, (tm, D), lambda i: (i, 0)),
              ((1, D), F32, (1, D), lambda i: (0, 0)), ((1, LANES), F32, (1, LANES), lambda i: (0, 0))])


def _adamw(g, w, m, v):
    m = ADAM_B1 * m + (1.0 - ADAM_B1) * g
    v = ADAM_B2 * v + (1.0 - ADAM_B2) * (g * g)
    m_hat = m / ADAM_C1
    v_hat = v / ADAM_C2
    delta = -ADAM_LR * (m_hat / (jnp.sqrt(v_hat) + ADAM_EPS) + ADAM_WD * w)
    return delta, m, v


def adam_flat(g, w, m, v, *, name):
    def body(g_ref, w_ref, m_ref, v_ref, d_ref, nm_ref, nv_ref):
        d, nm, nv = _adamw(g_ref[...], w_ref[...], m_ref[...], v_ref[...])
        d_ref[...] = d
        nm_ref[...] = nm
        nv_ref[...] = nv

    blk = g.shape
    zero = lambda: (0, 0)
    return _pcall(body, name=name, grid=(),
                  ins=[(t, blk, zero) for t in (g, w, m, v)],
                  outs=[(blk, F32, blk, zero)] * 3)


def _chip_slots():
    x, y, c = lax.axis_index("x"), lax.axis_index("y"), lax.axis_index("c")
    chips = [(1 - x, y), (x, 1 - y), (1 - x, 1 - y)]
    return x, y, c, chips


def reduce_adam(gs, a_buf, b_buf, w, m, v, layer, prev, *, name, deps=()):
    L, R, C = w.shape
    tr = _tile(R, 256, 8)
    x, y, c, _ = _chip_slots()
    idx = jnp.stack([4 * x + 2 * y + c, 2 * x + y]).astype(jnp.int32)
    n_prev = 0 if prev is None else 4

    def body(idx_ref, g_ref, a_ref, b0_ref, b1_ref, b2_ref, w_ref, m_ref, v_ref, *rest):
        outs = rest[n_prev:]
        g = ((((g_ref[...].astype(F32) + a_ref[...].astype(F32)) + b0_ref[...].astype(F32))
              + b1_ref[...].astype(F32)) + b2_ref[...].astype(F32))
        d, nm, nv = _adamw(g, w_ref[...], m_ref[...], v_ref[...])
        outs[0][...] = g
        outs[1][...] = d
        outs[2][...] = nm
        outs[3][...] = nv

    blk3 = (None, tr, C)
    ins = [(gs, blk3, lambda i, s: (s[0], i, 0)), (a_buf, blk3, lambda i, s: (s[1], i, 0)),
           (b_buf, blk3, lambda i, s: (0, i, 0)), (b_buf, blk3, lambda i, s: (1, i, 0)),
           (b_buf, blk3, lambda i, s: (2, i, 0)),
           (w, blk3, lambda i, s: (layer, i, 0)), (m, blk3, lambda i, s: (layer, i, 0)),
           (v, blk3, lambda i, s: (layer, i, 0))]
    aliases = {}
    if prev is not None:
        for o, p in enumerate(prev):
            ins.append((p, None, None))
            aliases[1 + 8 + o] = o
    outs = [((L, R, C), F32, blk3, lambda i, s: (layer, i, 0))] * 4
    return _pcall(body, name=name, grid=(R // tr,), ins=ins, outs=outs, prefetch=idx, aliases=aliases,
                  semantics=("parallel",), deps=deps)


def pair_sum(gs, a_buf, *, name):
    _, R, C = gs.shape
    tr = _tile(R, 256, 8)
    x, y, c, chips = _chip_slots()
    idx = jnp.stack([4 * cx + 2 * cy + c for cx, cy in chips] + [2 * cx + cy for cx, cy in chips]).astype(jnp.int32)

    def body(idx_ref, g_ref, a_ref, o_ref):
        o_ref[...] = (g_ref[...].astype(F32) + a_ref[...].astype(F32)).astype(BF)

    blk3 = (None, tr, C)
    return _pcall(body, name=name, grid=(3, R // tr),
                  ins=[(gs, blk3, lambda j, i, s: (s[j], i, 0)), (a_buf, blk3, lambda j, i, s: (s[3 + j], i, 0))],
                  outs=[((3, R, C), BF, blk3, lambda j, i, s: (j, i, 0))], prefetch=idx,
                  semantics=("parallel", "parallel"))[0]


def sum_rows8(gathered, rows, *, name):
    W = gathered.shape[1]

    def body(g_ref, o_ref):
        acc = g_ref[0:rows, :]
        for d in range(1, N_DEV):
            acc = acc + g_ref[d * rows:(d + 1) * rows, :]
        o_ref[...] = acc

    return _pcall(body, name=name, grid=(), ins=[(gathered, gathered.shape, lambda: (0, 0))],
                  outs=[((rows, W), F32, (rows, W), lambda: (0, 0))])[0]


HBM_SPEC = pl.BlockSpec(memory_space=pltpu.HBM)
SEM_SPEC = pl.BlockSpec(memory_space=pltpu.SEMAPHORE)
ANY_SPEC = pl.BlockSpec(memory_space=pl.ANY)
DATAFLOW = pltpu.SideEffectType.DATAFLOW_SIDE_EFFECTING


def _in_hbm(v):
    return pltpu.with_memory_space_constraint(v, pltpu.HBM)


def _slot(p):
    return 4 * p[0] + 2 * p[1] + p[2]


def _gather_peers():
    x, y, c, chips = _chip_slots()
    return (x, y, c), [(x, y, 1 - c)] + [(*chip, c) for chip in chips]


def gather_start(groups, after, *, name):
    flat = [s for g in groups for s in g]
    n, n_g = len(flat), len(groups)
    where = [(gi, ti) for gi, g in enumerate(groups) for ti in range(len(g))]

    def body(*refs):
        src, land = refs[:n], refs[n:2 * n]
        sems = refs[2 * n + 1:2 * n + 1 + 2 * n_g]
        me, peers = _gather_peers()
        for t in range(n):
            gi, ti = where[t]
            for k, to in enumerate(peers):
                pltpu.make_async_remote_copy(
                    src_ref=src[t], dst_ref=land[t].at[_slot(me)], send_sem=sems[2 * gi].at[4 * ti + k],
                    recv_sem=sems[2 * gi + 1].at[4 * ti + k], device_id=to, device_id_type=MESH).start()
        refs[-1][...] = jnp.zeros_like(refs[-1])

    out_shape = []
    for g in groups:
        out_shape += [pltpu.SemaphoreType.DMA((4 * len(g),)), pltpu.SemaphoreType.DMA((4 * len(g),))]
    out_shape += [pltpu.HBM(s.shape, s.dtype) for s in flat]
    out_shape += [pltpu.HBM((N_DEV,) + s.shape, s.dtype) for s in flat]
    out_shape += [jax.ShapeDtypeStruct((8, LANES), F32)]
    aliases = {t: 2 * n_g + t for t in range(n)}
    aliases.update({n + t: 2 * n_g + n + t for t in range(n)})
    res = pl.pallas_call(
        body, name=name, out_shape=out_shape, in_specs=[HBM_SPEC] * (2 * n) + [ANY_SPEC],
        out_specs=[SEM_SPEC] * (2 * n_g) + [HBM_SPEC] * (2 * n) + [pl.BlockSpec(memory_space=pltpu.VMEM)],
        input_output_aliases=aliases, compiler_params=pltpu.CompilerParams(has_side_effects=DATAFLOW),
    )(*[_in_hbm(s) for s in flat], *[_in_hbm(lax.empty((N_DEV,) + s.shape, s.dtype)) for s in flat], after)
    out, off = [], 0
    for gi, g in enumerate(groups):
        k = len(g)
        out.append((res[2 * gi], res[2 * gi + 1], res[2 * n_g + off:2 * n_g + off + k],
                    res[2 * n_g + n + off:2 * n_g + n + off + k]))
        off += k
    return out, res[-1]


def gather_wait(started, after, *, name):
    send_sems, recv_sems, srcs, lands = started
    n = len(srcs)

    def body(*refs):
        src, land = refs[:n], refs[n:2 * n]
        send, recv = refs[2 * n], refs[2 * n + 1]
        _, peers = _gather_peers()
        for t in range(n):
            for k, frm in enumerate(peers):
                cp = pltpu.make_async_remote_copy(
                    src_ref=src[t], dst_ref=land[t].at[_slot(frm)], send_sem=send.at[4 * t + k],
                    recv_sem=recv.at[4 * t + k],
                    device_id=frm, device_id_type=MESH)
                cp.wait_send()
                cp.wait_recv()

    res = pl.pallas_call(
        body, name=name,
        out_shape=[pltpu.HBM(s.shape, s.dtype) for s in srcs] + [pltpu.HBM(l.shape, l.dtype) for l in lands],
        in_specs=[HBM_SPEC] * (2 * n) + [SEM_SPEC, SEM_SPEC, ANY_SPEC], out_specs=[HBM_SPEC] * (2 * n),
        input_output_aliases={t: t for t in range(2 * n)},
        compiler_params=pltpu.CompilerParams(has_side_effects=DATAFLOW),
    )(*srcs, *lands, send_sems, recv_sems, after)
    return res[:n], res[n:]


def place_own(src, land, *, name):
    R, C = src.shape
    tr = _tile(R, 512, 16)
    x, y, c, _ = _chip_slots()
    idx = jnp.stack([4 * x + 2 * y + c]).astype(jnp.int32)

    def body(idx_ref, s_ref, land_ref, o_ref):
        o_ref[...] = s_ref[...]

    return _pcall(body, name=name, grid=(R // tr,),
                  ins=[(src, (tr, C), lambda i, s: (i, 0)), (land, None, None)],
                  outs=[(land.shape, land.dtype, (None, tr, C), lambda i, s: (s[0], i, 0))],
                  prefetch=idx, aliases={2: 0}, semantics=("parallel",))[0]


def gather_finish(srcs, lands, *, name):
    n = len(srcs)

    def body(*refs):
        land = refs[n:2 * n]
        send_sems, recv_sems = refs[2 * n:]
        x, y, c, chips = _chip_slots()
        me, sibling = (x, y, c), (x, y, 1 - c)

        def copy(t, j, block, to):
            return pltpu.make_async_remote_copy(
                src_ref=land[t].at[_slot(block)], dst_ref=land[t].at[_slot(block)], send_sem=send_sems.at[t, j],
                recv_sem=recv_sems.at[t, j], device_id=to, device_id_type=MESH)

        sends = [copy(t, j, (*chip, c), sibling) for t in range(n) for j, chip in enumerate(chips)]
        for cp in sends:
            cp.start()
        for t in range(n):
            for j, chip in enumerate(chips):
                copy(t, j, (*chip, 1 - c), me).wait_recv()
        for cp in sends:
            cp.wait_send()

    passed = pl.pallas_call(
        body, name=name, out_shape=[jax.ShapeDtypeStruct(l.shape, l.dtype) for l in lands],
        in_specs=[ANY_SPEC] * n, out_specs=[ANY_SPEC] * n,
        input_output_aliases={t: t for t in range(n)},
        scratch_shapes=[pltpu.SemaphoreType.DMA((n, 3)), pltpu.SemaphoreType.DMA((n, 3))],
    )(*lands)
    return [place_own(s, l, name=f"{name}_own{t}") for t, (s, l) in enumerate(zip(srcs, passed))]


def chips_start(pairs, *, name):
    n = len(pairs)

    def body(*refs):
        src, land = refs[:n], refs[n:2 * n]
        send, recv = refs[2 * n], refs[2 * n + 1]
        token = refs[-1]
        x, y, c, chips = _chip_slots()
        for t in range(n):
            for j, chip in enumerate(chips):
                pltpu.make_async_remote_copy(
                    src_ref=src[t].at[j], dst_ref=land[t].at[j], send_sem=send.at[3 * t + j],
                    recv_sem=recv.at[3 * t + j], device_id=(*chip, c), device_id_type=MESH).start()
        token[...] = jnp.zeros_like(token)

    res = pl.pallas_call(
        body, name=name,
        out_shape=[pltpu.SemaphoreType.DMA((3 * n,)), pltpu.SemaphoreType.DMA((3 * n,))]
        + [pltpu.HBM(p.shape, p.dtype) for p in pairs] * 2 + [jax.ShapeDtypeStruct((8, LANES), F32)],
        in_specs=[HBM_SPEC] * (2 * n),
        out_specs=[SEM_SPEC, SEM_SPEC] + [HBM_SPEC] * (2 * n) + [pl.BlockSpec(memory_space=pltpu.VMEM)],
        input_output_aliases={t: 2 + t for t in range(2 * n)},
        compiler_params=pltpu.CompilerParams(has_side_effects=DATAFLOW),
    )(*[_in_hbm(p) for p in pairs], *[_in_hbm(lax.empty(p.shape, p.dtype)) for p in pairs])
    return res[0], res[1], res[2:2 + n], res[2 + n:2 + 2 * n], res[-1]


def chips_wait(started, after, *, name):
    send_sems, recv_sems, srcs, lands, _ = started
    n = len(srcs)

    def body(*refs):
        src, land = refs[:n], refs[n:2 * n]
        send, recv = refs[2 * n], refs[2 * n + 1]
        x, y, c, chips = _chip_slots()
        for t in range(n):
            for j, chip in enumerate(chips):
                cp = pltpu.make_async_remote_copy(
                    src_ref=src[t].at[j], dst_ref=land[t].at[j], send_sem=send.at[3 * t + j],
                    recv_sem=recv.at[3 * t + j], device_id=(*chip, c), device_id_type=MESH)
                cp.wait_send()
                cp.wait_recv()

    res = pl.pallas_call(
        body, name=name, out_shape=[pltpu.HBM(s.shape, s.dtype) for s in srcs] * 2,
        in_specs=[HBM_SPEC] * (2 * n) + [SEM_SPEC, SEM_SPEC, ANY_SPEC], out_specs=[HBM_SPEC] * (2 * n),
        input_output_aliases={t: t for t in range(2 * n)},
        compiler_params=pltpu.CompilerParams(has_side_effects=DATAFLOW),
    )(*srcs, *lands, send_sems, recv_sems, after)
    return res[n:]


def _sibling_copies(src, land, send, recv, n):
    x, y, c, _ = _chip_slots()
    return [pltpu.make_async_remote_copy(
        src_ref=src[t].at[4 * (q // 2) + 2 * (q % 2) + (1 - c)], dst_ref=land[t].at[q], send_sem=send.at[4 * t + q],
        recv_sem=recv.at[4 * t + q], device_id=(x, y, 1 - c), device_id_type=MESH)
        for t in range(n) for q in range(4)]


def sibling_start(gs, *, name):
    n = len(gs)

    def body(*refs):
        for cp in _sibling_copies(refs[:n], refs[n:2 * n], refs[2 * n], refs[2 * n + 1], n):
            cp.start()
        refs[-1][...] = jnp.zeros_like(refs[-1])

    lands = [lax.empty((4,) + g.shape[1:], g.dtype) for g in gs]
    res = pl.pallas_call(
        body, name=name,
        out_shape=[pltpu.SemaphoreType.DMA((4 * n,)), pltpu.SemaphoreType.DMA((4 * n,))]
        + [pltpu.HBM(g.shape, g.dtype) for g in gs] + [pltpu.HBM(l.shape, l.dtype) for l in lands]
        + [jax.ShapeDtypeStruct((8, LANES), F32)],
        in_specs=[HBM_SPEC] * (2 * n),
        out_specs=[SEM_SPEC, SEM_SPEC] + [HBM_SPEC] * (2 * n) + [pl.BlockSpec(memory_space=pltpu.VMEM)],
        input_output_aliases={t: 2 + t for t in range(2 * n)},
        compiler_params=pltpu.CompilerParams(has_side_effects=DATAFLOW),
    )(*[_in_hbm(g) for g in gs], *[_in_hbm(l) for l in lands])
    return res[0], res[1], res[2:2 + n], res[2 + n:2 + 2 * n], res[-1]


def sibling_wait(started, after, *, name):
    send_sems, recv_sems, srcs, lands, _ = started
    n = len(srcs)

    def body(*refs):
        for cp in _sibling_copies(refs[:n], refs[n:2 * n], refs[2 * n], refs[2 * n + 1], n):
            cp.wait_send()
            cp.wait_recv()

    res = pl.pallas_call(
        body, name=name,
        out_shape=[pltpu.HBM(s.shape, s.dtype) for s in srcs] + [pltpu.HBM(l.shape, l.dtype) for l in lands],
        in_specs=[HBM_SPEC] * (2 * n) + [SEM_SPEC, SEM_SPEC, ANY_SPEC], out_specs=[HBM_SPEC] * (2 * n),
        input_output_aliases={t: t for t in range(2 * n)},
        compiler_params=pltpu.CompilerParams(has_side_effects=DATAFLOW),
    )(*srcs, *lands, send_sems, recv_sems, after)
    return res[:n], res[n:]


def all_gather_vmem(x_shard, *, name, after=None):
    m_per, n = x_shard.shape
    n_after = 0 if after is None else 1

    def body(x_ref, *rest):
        out_ref, send_sems, recv_sems, local_sem = rest[n_after:]
        x, y, c, chips = _chip_slots()
        me, sibling = (x, y, c), (x, y, 1 - c)

        def rows(px, py, pc):
            return out_ref.at[pl.ds((4 * px + 2 * py + pc) * m_per, m_per), :]

        def copy(k, block, to, src=None):
            return pltpu.make_async_remote_copy(
                src_ref=rows(*block) if src is None else src, dst_ref=rows(*block),
                send_sem=send_sems.at[k], recv_sem=recv_sems.at[k], device_id=to, device_id_type=MESH)

        mine = pltpu.make_async_copy(x_ref, rows(*me), local_sem)
        mine.start()
        first = [copy(0, me, sibling, src=x_ref)]
        first += [copy(1 + j, me, (*chip, c), src=x_ref) for j, chip in enumerate(chips)]
        for cp in first:
            cp.start()
        passed = [copy(4 + j, (*chip, c), sibling) for j, chip in enumerate(chips)]
        for j, chip in enumerate(chips):
            copy(1 + j, (*chip, c), me).wait_recv()
            passed[j].start()
        copy(0, sibling, me).wait_recv()
        for j, chip in enumerate(chips):
            copy(4 + j, (*chip, 1 - c), me).wait_recv()
        for cp in first + passed:
            cp.wait_send()
        mine.wait()

    vmem = pl.BlockSpec(memory_space=pltpu.VMEM)
    return pl.pallas_call(
        body, name=name, out_shape=jax.ShapeDtypeStruct((N_DEV * m_per, n), x_shard.dtype),
        in_specs=[vmem] + [ANY_SPEC] * n_after, out_specs=vmem,
        scratch_shapes=[pltpu.SemaphoreType.DMA((7,)), pltpu.SemaphoreType.DMA((7,)), pltpu.SemaphoreType.DMA],
        compiler_params=pltpu.CompilerParams(vmem_limit_bytes=int(min(
            VMEM_LIMIT_CAP, 2 * (N_DEV + 1) * m_per * n * x_shard.dtype.itemsize + 16 * 2 ** 20))),
    )(x_shard, *([] if after is None else [after]))


def _rope_slab(cols):
    z = jnp.zeros(cols.shape[:-1] + (HALF_ROPE,), cols.dtype)
    return jnp.concatenate([cols[..., :HALF_ROPE], z, cols[..., HALF_ROPE:], z], axis=-1)


def _rope_unslab(slab):
    return jnp.concatenate([slab[..., :HALF_ROPE], slab[..., 2 * HALF_ROPE:3 * HALF_ROPE]], axis=-1)


def _pack_w_in(w_g):
    s, d, c = w_g.shape
    w = jnp.transpose(w_g, (1, 0, 2)).reshape(d, s * c)
    c1, c2, c3 = Q_LORA, Q_LORA + KV_LORA, Q_LORA + KV_LORA + QK_ROPE
    return jnp.concatenate([w[:, :c2], w[:, c3:], _rope_slab(w[:, c2:c3])], axis=-1)


def _unpack_w_in_grad(dw):
    d = dw.shape[0]
    c2 = Q_LORA + KV_LORA
    uv = 2 * SGU_OUT
    g = jnp.concatenate([dw[:, :c2], _rope_unslab(dw[:, c2 + uv:]), dw[:, c2:c2 + uv]], axis=-1)
    return jnp.transpose(g.reshape(d, N_DEV, g.shape[1] // N_DEV), (1, 0, 2))


def _rope_tables(positions):
    inv_freq = ROPE_BASE ** (-jnp.arange(0, QK_ROPE, 2, dtype=F32) / QK_ROPE)
    ang = positions.astype(F32)[:, None] * inv_freq
    cos, sin = jnp.cos(ang), jnp.sin(ang)
    z = jnp.zeros_like(cos)
    return jnp.concatenate([cos, z, cos, z], axis=-1), jnp.concatenate([-sin, z, sin, z], axis=-1)


def _mlp_up(x, gain, w1, tag):
    hn = rms_fwd(x, gain, name=f"mlp{tag}_norm")

    def act_epi(acc):
        a = jnp.maximum(acc, 0.0)
        return a, a * a

    T = x.shape[0]
    F = w1.shape[0] * w1.shape[2]
    a, act = mm(hn, w1, name=f"mlp{tag}_up", outs=[((T, F), BF, None), ((T, F), BF, None)], epi=act_epi)
    return hn, a, act


def _mlp_down(x, act, w2, tag):
    bm = _tile(x.shape[0], MM_TILE)
    bn = _tile(x.shape[1], MM_TILE)
    return mm(act, w2, name=f"mlp{tag}_down", out=(x.shape, F32), bm=bm, bn=bn,
              epi=lambda acc, r: (acc + r[...],), epi_ins=[(x, (bm, bn), lambda i, j, k: (i, j))])


def _mlp_bwd_weights(w1, w2, saved, dxb, tag):
    hn, a, act = saved
    T, D = dxb.shape
    F = a.shape[1]
    bm = _tile(T, MM_TILE)
    bn = _tile(F, min(MM_TILE, w1.shape[2]))
    dhid = mm(dxb, w2, tb=True, name=f"mlp{tag}_dhid", out=((T, F), BF), bm=bm, bn=bn,
              epi=lambda acc, a_ref: (2.0 * a_ref[...].astype(F32) * acc,),
              epi_ins=[(a, (bm, bn), lambda i, j, k: (i, j))])
    dw2 = mm(act, dxb, ta=True, name=f"mlp{tag}_dw2", out=((F, D), BF))
    dw1 = mm(hn, dhid, ta=True, name=f"mlp{tag}_dw1", out=(w1.shape, BF))
    return dhid, dw1, dw2.reshape(N_DEV, F // N_DEV, D)


def _reduce_begin(grads, tag):
    return sibling_start(grads, name=f"reduce_sibling_start_{tag}")


def _reduce_continue(sib, after, tag):
    grads, a_bufs = sibling_wait(sib, after, name=f"reduce_sibling_wait_{tag}")
    pairs = [pair_sum(g, a, name=f"pair_sum_{tag}{t}") for t, (g, a) in enumerate(zip(grads, a_bufs))]
    return grads, a_bufs, chips_start(pairs, name=f"reduce_chips_start_{tag}")


def _mlp_bwd_input(x_in, gain, w1, dhid, dx, tag, sib):
    dhn = mm(dhid, w1, tb=True, name=f"mlp{tag}_dhn", out=(x_in.shape, F32), deps=[sib[-1]])
    reduce_state = _reduce_continue(sib, dhn, f"r_mlp{tag}")
    return rms_bwd(x_in, gain, dhn, dres=dx, name=f"mlp{tag}_norm_bwd", deps=[reduce_state[2][-1]]), reduce_state


def kernel(x, positions, e_norm_mix, e_w_in, e_q_norm, e_w_uq, e_kv_norm, e_w_ukv, e_v_norm, e_sgu_w, e_sgu_b, e_mla_out_norm, e_sgu_out_norm, e_w_out, o_norm_mix, o_w_in, o_conv_w, o_w_out, mlp_norm, mlp_w1, mlp_w2, final_norm, loss_target, m_e_norm_mix, m_e_w_in, m_e_q_norm, m_e_w_uq, m_e_kv_norm, m_e_w_ukv, m_e_v_norm, m_e_sgu_w, m_e_sgu_b, m_e_mla_out_norm, m_e_sgu_out_norm, m_e_w_out, m_o_norm_mix, m_o_w_in, m_o_conv_w, m_o_w_out, m_mlp_norm, m_mlp_w1, m_mlp_w2, m_final_norm, v_e_norm_mix, v_e_w_in, v_e_q_norm, v_e_w_uq, v_e_kv_norm, v_e_w_ukv, v_e_v_norm, v_e_sgu_w, v_e_sgu_b, v_e_mla_out_norm, v_e_sgu_out_norm, v_e_w_out, v_o_norm_mix, v_o_w_in, v_o_conv_w, v_o_w_out, v_mlp_norm, v_mlp_w1, v_mlp_w2, v_final_norm):
    T, D = x.shape[1], x.shape[2]
    d_shard = o_norm_mix.shape[1]
    x0 = x[0]
    target = loss_target[0]
    me = 4 * lax.axis_index("x") + 2 * lax.axis_index("y") + lax.axis_index("c")

    bf = lambda s: s.astype(BF)
    gather_groups = [[bf(e_w_in[0]), bf(e_w_uq[0]), bf(e_w_ukv[0])], [bf(e_w_out[0]), bf(mlp_w1[0])],
                     [bf(mlp_w2[0]), bf(o_w_in[0])], [bf(o_w_out[0]), bf(mlp_w1[1])], [bf(mlp_w2[1])]]
    small_rows = jnp.concatenate([o_norm_mix, o_conv_w[0], jnp.zeros((4, d_shard), F32)], axis=0)
    small_flat = all_gather_vmem(small_rows, name="gather_small")
    started, start_token = gather_start(gather_groups[:1], small_flat, name="gather_start0")
    started_rest, rest_token = gather_start(gather_groups[1:], start_token, name="gather_start1")
    started += started_rest

    def gathered(gi, after):
        srcs, lands = gather_wait(started[gi], after, name=f"gather_wait{gi}")
        return gather_finish(srcs, lands, name=f"gather_finish{gi}")

    small_g = small_flat.reshape(N_DEV, 8, d_shard)
    o_norm_full = small_g[:, 0, :].reshape(1, D)
    conv_w_full = jnp.transpose(small_g[:, 1:4, :], (1, 0, 2)).reshape(3, D)
    w_tril = jnp.tril(e_sgu_w[0])
    w_tril_b = w_tril.astype(BF)
    w_tril_tb = jnp.swapaxes(w_tril, 1, 2).astype(BF)
    b_full = jnp.repeat(e_sgu_b[0].T, CH, axis=1)
    v_gain = e_v_norm[0].reshape(1, SGU_OUT)
    cos_t, sin_t = _rope_tables(positions[0])
    mlp_gain = [mlp_norm[0:1], mlp_norm[1:2]]
    final_gain = final_norm.reshape(1, D)

    h0 = rms_fwd(x0, e_norm_mix, name="e_norm", deps=[rest_token])
    g_w_in, g_w_uq, w_ukv = gathered(0, h0)
    w_in_e = _pack_w_in(g_w_in)
    w_uq = jnp.concatenate([g_w_uq[..., :QK_NOPE], _rope_slab(g_w_uq[..., QK_NOPE:])], axis=-1)
    proj = mm(h0, w_in_e, name="e_in", out=((T, w_in_e.shape[1]), F32), bn=_tile(w_in_e.shape[1], 640))
    qn, kvn, krope = mla_prep(proj, e_q_norm, e_kv_norm, cos_t, sin_t, name="mla_prep")
    bm = _tile(T, MM_TILE)

    def q_epi(acc, cos_ref, sin_ref):
        return (jnp.concatenate([acc[:, :QK_NOPE], _rope_fwd(acc[:, QK_NOPE:], cos_ref[...], sin_ref[...])], axis=-1),)

    q = mm(qn, w_uq, name="mla_q", out=((T, HEADS * HEAD_PAD), BF), bm=bm, bn=HEAD_PAD, epi=q_epi,
           epi_ins=[(cos_t, (bm, LANES), lambda i, j, k: (i, 0)), (sin_t, (bm, LANES), lambda i, j, k: (i, 0))])

    def kv_epi(acc, kr_ref):
        return jnp.concatenate([acc[:, :QK_NOPE].astype(BF), kr_ref[...]], axis=-1), acc[:, QK_NOPE:]

    k, v = mm(kvn, w_ukv, name="mla_kv", bm=bm, bn=HEAD_PAD, epi=kv_epi,
              outs=[((T, HEADS * HEAD_PAD), BF, HEAD_PAD), ((T, MLA_OUT), BF, V_HEAD)],
              epi_ins=[(krope, (bm, LANES), lambda i, j, k: (i, 0))])
    attn, attn_lse = attn_fwd(q, k, v, name="attn_fwd")
    mixed = mix_fwd(attn, proj, e_mla_out_norm, e_sgu_out_norm, v_gain, w_tril_b, b_full, name="mix_fwd")
    bn = _tile(D, MM_TILE)
    g_w_out_e, w1_0 = gathered(1, mixed)
    w_out_e = g_w_out_e.reshape(-1, D)
    x1 = mm(mixed, w_out_e, name="e_out", out=((T, D), F32), bm=bm, bn=bn,
            epi=lambda acc, r: (acc + r[...],), epi_ins=[(x0, (bm, bn), lambda i, j, k: (i, j))])
    hn0, a0, act0 = _mlp_up(x1, mlp_gain[0], w1_0, 0)
    g_w2_0, g_w_in_o = gathered(2, act0)
    w2_0 = g_w2_0.reshape(-1, D)
    x2 = _mlp_down(x1, act0, w2_0, 0)
    ho = rms_fwd(x2, o_norm_full, name="o_norm")
    proj_o = mm(ho, g_w_in_o, name="o_in", out=((T, 3 * D), F32))
    gated = conv_fwd(proj_o, conv_w_full, name="conv_fwd")
    g_w_out_o, w1_1 = gathered(3, gated)
    w_out_o = g_w_out_o.reshape(-1, D)
    x3 = mm(gated, w_out_o, name="o_out", out=((T, D), F32), bm=bm, bn=bn,
            epi=lambda acc, r: (acc + r[...],), epi_ins=[(x2, (bm, bn), lambda i, j, k: (i, j))])
    hn1, a1, act1 = _mlp_up(x3, mlp_gain[1], w1_1, 1)
    (g_w2_1,) = gathered(4, act1)
    w2_1 = g_w2_1.reshape(-1, D)
    x4 = _mlp_down(x3, act1, w2_1, 1)
    w1, w2 = [w1_0, w1_1], [w2_0, w2_1]
    mlp0_saved, mlp1_saved = (hn0, a0, act0), (hn1, a1, act1)

    dx4, dx4b, d_final, loss_part = loss_bwd(x4, final_gain, target, name="loss_bwd")
    loss = lax.psum(loss_part[0, 0], AXES)

    dhid1, dw1_1, dw2_1 = _mlp_bwd_weights(w1[1], w2[1], mlp1_saved, dx4b, 1)
    sib_r0 = _reduce_begin([dw1_1, dw2_1], "r0")
    (dx3, dx3b, d_mlp1), (grads_r0, a_r0, st_r0) = _mlp_bwd_input(x3, mlp_gain[1], w1[1], dhid1, dx4, 1, sib_r0)

    dgated = mm(dx3b, w_out_o, tb=True, name="o_out_dx", out=((T, D), F32))
    dw_out_o = mm(gated, dx3b, ta=True, name="o_out_dw", out=((D, D), BF))
    dproj_o, dconv_full = conv_bwd(dgated, proj_o, conv_w_full, name="conv_bwd")
    dw_in_o = mm(ho, dproj_o, ta=True, name="o_in_dw", out=(g_w_in_o.shape, BF))
    sib_r1 = _reduce_begin([dw_out_o.reshape(g_w_out_o.shape), dw_in_o], "r1")
    dho = mm(dproj_o, g_w_in_o, tb=True, name="o_in_dx", out=((T, D), F32), deps=[sib_r1[-1]])
    grads_r1, a_r1, st_r1 = _reduce_continue(sib_r1, dho, "r1")
    dx2, dx2b, d_onorm_full = rms_bwd(x2, o_norm_full, dho, dres=dx3, name="o_norm_bwd", deps=[st_r1[-1]])

    dhid0, dw1_0, dw2_0 = _mlp_bwd_weights(w1[0], w2[0], mlp0_saved, dx2b, 0)
    sib_r2 = _reduce_begin([dw1_0, dw2_0], "r2")
    (dx1, dx1b, d_mlp0), (grads_r2, a_r2, st_r2) = _mlp_bwd_input(x1, mlp_gain[0], w1[0], dhid0, dx2, 0, sib_r2)
    b_r0 = chips_wait(st_r0, dx1b, name="reduce_chips_wait_r0")

    dmixed = mm(dx1b, w_out_e, tb=True, name="e_out_dx", out=((T, MLA_OUT + SGU_OUT), F32))
    dw_out_e = mm(mixed, dx1b, ta=True, name="e_out_dw", out=(w_out_e.shape, BF))
    (dattn, duv, d_mla_out, d_sgu_out, d_vgain, d_sgu_w, d_b_full) = mix_bwd(
        dmixed, attn, proj, e_mla_out_norm, e_sgu_out_norm, v_gain, w_tril_b, w_tril_tb, b_full, name="mix_bwd")
    b_r1 = chips_wait(st_r1, dattn, name="reduce_chips_wait_r1")
    dq, dk, dv = attn_bwd(q, k, v, attn, attn_lse, dattn, name="attn_bwd")
    dq_lin, dkv_lin, dkr = mla_bwd_prep(dq, dk, dv, cos_t, sin_t, name="mla_bwd_prep")
    dw_uq_pad = mm(qn, dq_lin, ta=True, name="mla_q_dw", out=(w_uq.shape, F32))
    dw_ukv = mm(kvn, dkv_lin, ta=True, name="mla_kv_dw", out=(w_ukv.shape, BF))
    dw_uq = jnp.concatenate([dw_uq_pad[..., :QK_NOPE], _rope_unslab(dw_uq_pad[..., QK_NOPE:])], axis=-1).astype(BF)
    sib_r2b = _reduce_begin([dw_out_e.reshape(g_w_out_e.shape), dw_uq, dw_ukv], "r2b")
    dqn = mm(dq_lin, w_uq, tb=True, name="mla_q_dx", out=((T, Q_LORA), F32), deps=[sib_r2b[-1]])
    dkvn = mm(dkv_lin, w_ukv, tb=True, name="mla_kv_dx", out=((T, KV_LORA), F32), deps=[sib_r2b[-1]])
    grads_r2b, a_r2b, st_r2b = _reduce_continue(sib_r2b, dkvn, "r2b")
    dcq, d_qnorm = rms_bwd(proj, e_q_norm, dqn, col_block=0, want_f32=False, name="q_norm_bwd", deps=[st_r2b[-1]])
    dckv, d_kvnorm = rms_bwd(proj, e_kv_norm, dkvn, col_block=1, want_f32=False, name="kv_norm_bwd")
    dproj = jnp.concatenate([dcq, dckv, duv, dkr], axis=-1)
    dw_in_e_pad = mm(h0, dproj, ta=True, name="e_in_dw", out=(w_in_e.shape, F32), bn=_tile(w_in_e.shape[1], 640))
    dw_in_e = _unpack_w_in_grad(dw_in_e_pad).astype(BF)
    sib_r3 = _reduce_begin([dw_in_e], "r3")
    dh0 = mm(dproj, w_in_e, tb=True, name="e_in_dx", out=((T, D), F32), deps=[sib_r3[-1]])
    grads_r3, a_r3, st_r3 = _reduce_continue(sib_r3, dh0, "r3")
    tok_r3 = st_r3[-1]
    grad_x, d_enorm = rms_bwd(x0, e_norm_mix, dh0, dres=dx1, want_bf=False, name="e_norm_bwd", deps=[tok_r3])
    b_r2 = chips_wait(st_r2, grad_x, name="reduce_chips_wait_r2")

    def finish(grads, a_bufs, b_bufs, t, w, m, v, layer=0, prev=None, tag="", deps=()):
        return reduce_adam(grads[t], a_bufs[t], b_bufs[t], w, m, v, layer, prev, name=f"adam_{tag}", deps=deps)

    r_w1 = finish(grads_r0, a_r0, b_r0, 0, mlp_w1, m_mlp_w1, v_mlp_w1, 1, None, tag="w1_l1")
    r_w2 = finish(grads_r0, a_r0, b_r0, 1, mlp_w2, m_mlp_w2, v_mlp_w2, 1, None, tag="w2_l1")
    r_w_out_o = finish(grads_r1, a_r1, b_r1, 0, o_w_out, m_o_w_out, v_o_w_out, tag="o_w_out")
    r_w_in_o = finish(grads_r1, a_r1, b_r1, 1, o_w_in, m_o_w_in, v_o_w_in, tag="o_w_in")
    r_w1 = finish(grads_r2, a_r2, b_r2, 0, mlp_w1, m_mlp_w1, v_mlp_w1, 0, r_w1, tag="w1_l0", deps=[tok_r3])
    r_w2 = finish(grads_r2, a_r2, b_r2, 1, mlp_w2, m_mlp_w2, v_mlp_w2, 0, r_w2, tag="w2_l0", deps=[r_w1[1]])
    b_r2b = chips_wait(st_r2b, r_w2[1], name="reduce_chips_wait_r2b")
    r_w_out_e = finish(grads_r2b, a_r2b, b_r2b, 0, e_w_out, m_e_w_out, v_e_w_out, tag="e_w_out")
    r_w_uq = finish(grads_r2b, a_r2b, b_r2b, 1, e_w_uq, m_e_w_uq, v_e_w_uq, tag="e_w_uq")
    r_w_ukv = finish(grads_r2b, a_r2b, b_r2b, 2, e_w_ukv, m_e_w_ukv, v_e_w_ukv, tag="e_w_ukv")
    b_r3 = chips_wait(st_r3, r_w_out_e[1], name="reduce_chips_wait_r3")
    r_w_in = finish(grads_r3, a_r3, b_r3, 0, e_w_in, m_e_w_in, v_e_w_in, tag="e_w_in")

    d_sgu_b = jnp.transpose(d_b_full[:, ::CH])
    d_sgu_w_tril = jnp.tril(d_sgu_w)
    rep = [("e_norm_mix", e_norm_mix, m_e_norm_mix, v_e_norm_mix, d_enorm),
           ("e_q_norm", e_q_norm, m_e_q_norm, v_e_q_norm, d_qnorm),
           ("e_kv_norm", e_kv_norm, m_e_kv_norm, v_e_kv_norm, d_kvnorm),
           ("e_v_norm", e_v_norm, m_e_v_norm, v_e_v_norm, d_vgain),
           ("e_sgu_w", e_sgu_w, m_e_sgu_w, v_e_sgu_w, d_sgu_w_tril),
           ("e_sgu_b", e_sgu_b, m_e_sgu_b, v_e_sgu_b, d_sgu_b),
           ("e_mla_out_norm", e_mla_out_norm, m_e_mla_out_norm, v_e_mla_out_norm, d_mla_out),
           ("e_sgu_out_norm", e_sgu_out_norm, m_e_sgu_out_norm, v_e_sgu_out_norm, d_sgu_out),
           ("mlp_norm", mlp_norm, m_mlp_norm, v_mlp_norm, jnp.concatenate([d_mlp0, d_mlp1], axis=0)),
           ("final_norm", final_norm, m_final_norm, v_final_norm, d_final)]
    sizes = [int(np.prod(r[1].shape)) for r in rep]
    n_rep = sum(sizes)
    n_all = n_rep + 4 * D
    width = -(-n_all // (8 * LANES)) * LANES
    pad = 8 * width - n_all
    flat = jnp.concatenate([r[4].reshape(-1) for r in rep]
                           + [d_onorm_full.reshape(-1), dconv_full.reshape(-1), jnp.zeros((pad,), F32)])
    summed = sum_rows8(all_gather_vmem(flat.reshape(8, width), name="gather_small_grads", after=b_r3[0]), 8,
                       name="sum_small_grads").reshape(-1)

    def pack_rep(i):
        return jnp.concatenate([r[i].reshape(-1) for r in rep]).reshape(n_rep // LANES, LANES)

    g_rep = summed[:n_rep].reshape(n_rep // LANES, LANES)
    d_rep, nm_rep, nv_rep = adam_flat(g_rep, pack_rep(1), pack_rep(2), pack_rep(3), name="adam_replicated")

    def unpack_rep(flat2d):
        out, off = {}, 0
        f = flat2d.reshape(-1)
        for r, n in zip(rep, sizes):
            out[r[0]] = f[off:off + n].reshape(r[1].shape)
            off += n
        return out

    small = {"grad": unpack_rep(g_rep), "delta": unpack_rep(d_rep), "new_m": unpack_rep(nm_rep),
             "new_v": unpack_rep(nv_rep)}
    g_onorm = lax.dynamic_slice(summed[n_rep:n_rep + D].reshape(1, D), (0, me * d_shard), (1, d_shard))
    g_conv = lax.dynamic_slice(summed[n_rep + D:n_rep + 4 * D].reshape(3, D), (0, me * d_shard), (3, d_shard))

    def pack_sharded(norm_part, conv_part):
        return jnp.concatenate([norm_part, conv_part, jnp.zeros((4, d_shard), F32)], axis=0)

    g_sh = pack_sharded(g_onorm, g_conv)
    d_sh, nm_sh, nv_sh = adam_flat(g_sh, pack_sharded(o_norm_mix, o_conv_w[0]), pack_sharded(m_o_norm_mix, m_o_conv_w[0]),
                                   pack_sharded(v_o_norm_mix, v_o_conv_w[0]), name="adam_sharded_small")
    for kind, arr in (("grad", g_sh), ("delta", d_sh), ("new_m", nm_sh), ("new_v", nv_sh)):
        small[kind]["o_norm_mix"] = arr[0:1]
        small[kind]["o_conv_w"] = arr[1:4][None]

    big = {"e_w_in": r_w_in, "e_w_uq": r_w_uq, "e_w_ukv": r_w_ukv, "e_w_out": r_w_out_e, "o_w_in": r_w_in_o,
           "o_w_out": r_w_out_o, "mlp_w1": r_w1, "mlp_w2": r_w2}
    order = ["e_norm_mix", "e_w_in", "e_q_norm", "e_w_uq", "e_kv_norm", "e_w_ukv", "e_v_norm", "e_sgu_w", "e_sgu_b",
             "e_mla_out_norm", "e_sgu_out_norm", "e_w_out", "o_norm_mix", "o_w_in", "o_conv_w", "o_w_out", "mlp_norm",
             "mlp_w1", "mlp_w2", "final_norm"]
    result = [loss, grad_x[None]]
    for ki, kind in enumerate(("grad", "delta", "new_m", "new_v")):
        for nm in order:
            result.append(big[nm][ki] if nm in big else small[kind][nm])
    return tuple(result)
```

```python
import functools

import numpy as np
import jax
import jax.numpy as jnp
from jax import lax
from jax.experimental import pallas as pl
from jax.experimental.pallas import tpu as pltpu

BF = jnp.bfloat16
F32 = jnp.float32
MESH = pl.DeviceIdType.MESH
AXES = ("x", "y", "c")
N_DEV = 8

EPS = 1e-6
HEADS = 8
Q_LORA = 512
KV_LORA = 512
QK_NOPE = 128
QK_ROPE = 64
HALF_ROPE = QK_ROPE // 2
V_HEAD = 128
HEAD_PAD = 256
ROPE_BASE = 10000.0
GROUPS = 8
CH = 128
CHUNK = 128
SGU_OUT = GROUPS * CH
MLA_OUT = HEADS * V_HEAD
ATTN_SCALE = float((QK_NOPE + QK_ROPE) ** -0.5)

ADAM_LR = 0.001
ADAM_B1 = 0.9
ADAM_B2 = 0.999
ADAM_EPS = 1e-08
ADAM_WD = 0.01
ADAM_STEP = 10
ADAM_C1 = 1.0 - ADAM_B1 ** ADAM_STEP
ADAM_C2 = 1.0 - ADAM_B2 ** ADAM_STEP

V7X_VMEM_BYTES = 64 * 2 ** 20
VMEM_LIMIT_CAP = V7X_VMEM_BYTES - 6 * 2 ** 20
LANES = 128
ROW_TILE = 256
ATTN_TILE = 512
MM_TILE = 1024
MM_K_TILE = 2048
MM_K_BLOCK_MAX = 3072


def _padded_bytes(block, dtype):
    dims = [d for d in block if d is not None]
    if len(dims) >= 1:
        dims[-1] = -(-dims[-1] // LANES) * LANES
    if len(dims) >= 2:
        dims[-2] = -(-dims[-2] // 16) * 16
    return int(np.prod(dims)) * jnp.dtype(dtype).itemsize


def _pcall(body, *, name, grid, ins, outs, scratch=(), semantics=None, aliases=None, prefetch=None, deps=()):
    any_spec = pl.BlockSpec(memory_space=pl.ANY)
    if deps:
        n_lead = len(ins) + (1 if prefetch is not None else 0)
        n_deps = len(deps)
        inner = body

        def body(*refs):
            inner(*refs[:n_lead], *refs[n_lead + n_deps:])

        ins = list(ins) + [(d, None, None) for d in deps]
    in_specs = [any_spec if b is None else pl.BlockSpec(b, m) for _, b, m in ins]
    out_specs = [any_spec if b is None else pl.BlockSpec(b, m) for _, _, b, m in outs]
    out_shape = [pltpu.HBM(s, d) for s, d, _, _ in outs]
    est = 0
    for a, b, _ in ins:
        if b is not None:
            est += 2 * _padded_bytes(b, a.dtype)
    for _, d, b, _ in outs:
        if b is not None:
            est += 2 * _padded_bytes(b, d)
    for s in scratch:
        if hasattr(s, "shape") and hasattr(s, "dtype"):
            est += _padded_bytes(s.shape, s.dtype)
    limit = int(min(VMEM_LIMIT_CAP, est + 16 * 2 ** 20))
    params = pltpu.CompilerParams(
        dimension_semantics=semantics or ("arbitrary",) * len(grid), vmem_limit_bytes=limit)
    args = [pltpu.with_memory_space_constraint(a, pltpu.HBM) for a, _, _ in ins]
    if prefetch is not None:
        grid_spec = pltpu.PrefetchScalarGridSpec(
            num_scalar_prefetch=1, grid=grid, in_specs=in_specs, out_specs=out_specs, scratch_shapes=list(scratch))
        call = pl.pallas_call(body, out_shape=out_shape, grid_spec=grid_spec, name=name, compiler_params=params,
                              input_output_aliases=aliases or {})
        return call(prefetch, *args)
    call = pl.pallas_call(body, out_shape=out_shape, grid=grid, in_specs=in_specs, out_specs=out_specs,
                          scratch_shapes=list(scratch), name=name, compiler_params=params,
                          input_output_aliases=aliases or {})
    return call(*args)


def _tile(dim, pref, quantum=LANES):
    if dim <= pref:
        return dim
    t = (pref // quantum) * quantum
    while t >= quantum:
        if dim % t == 0:
            return t
        t -= quantum
    return dim


def _vshape(arr_shape):
    if len(arr_shape) == 2:
        return tuple(arr_shape)
    s, r, c = arr_shape
    return (r, s * c)


def _vblock(arr_shape, br, bc, rc):
    if len(arr_shape) == 2:
        return (br, bc), (lambda *g: rc(*g))
    _, _, c = arr_shape
    assert c % bc == 0, (arr_shape, bc)
    per = c // bc

    def imap(*g):
        ri, ci = rc(*g)
        return (ci // per, ri, ci % per)

    return (None, br, bc), imap


def _shard_width(*shapes):
    w = None
    for s in shapes:
        if len(s) == 3:
            w = s[2] if w is None else int(np.gcd(w, s[2]))
    return w


def mm(a, b, *, name, ta=False, tb=False, out=None, outs=None, epi=None, epi_ins=(), bm=None, bn=None, bk=None,
       deps=()):
    av, bv = _vshape(a.shape), _vshape(b.shape)
    M, K = (av[1], av[0]) if ta else av
    K2, N = (bv[1], bv[0]) if tb else bv
    assert K == K2, (a.shape, b.shape, ta, tb)
    if outs is None:
        outs = [(out[0], out[1], None)]
    a_sw = _shard_width(a.shape)
    b_sw = _shard_width(b.shape)
    o_sw = _shard_width(*[o[0] for o in outs])
    m_lim = a_sw if (ta and a_sw) else None
    k_lim = [w for w in ((a_sw if not ta else None), (b_sw if tb else None)) if w]
    n_lim = [w for w in ((b_sw if not tb else None), o_sw) if w]
    if bm is None:
        bm = _tile(M, min([MM_TILE] + ([m_lim] if m_lim else [])))
    if bn is None:
        bn = _tile(N, min([MM_TILE] + n_lim))
    k_shards = 0
    if tb and len(b.shape) == 3 and bk is None and not (a_sw and not ta):
        k_shards = 1
        while 2 * k_shards <= b.shape[0] and 2 * k_shards * b_sw <= MM_K_BLOCK_MAX:
            k_shards *= 2
        bk = k_shards * b_sw
    if bk is None:
        bk = K if (K <= 4096 and not k_lim) else _tile(K, min([MM_K_TILE] + k_lim))
    assert M % bm == 0 and N % bn == 0 and K % bk == 0, (name, M, N, K, bm, bn, bk)
    nk = K // bk
    grid = (M // bm, N // bn, nk)
    if ta:
        a_blk, a_map = _vblock(a.shape, bk, bm, lambda i, j, k: (k, i))
    else:
        a_blk, a_map = _vblock(a.shape, bm, bk, lambda i, j, k: (i, k))
    if k_shards:
        b_blk, b_map = (k_shards, bn, b_sw), (lambda i, j, k: (k, j, 0))
    elif tb:
        b_blk, b_map = _vblock(b.shape, bn, bk, lambda i, j, k: (j, k))
    else:
        b_blk, b_map = _vblock(b.shape, bk, bn, lambda i, j, k: (k, j))
    dn = (((0 if ta else 1,), (1 if tb else 0,)), ((), ()))
    ins = [(a, a_blk, a_map), (b, b_blk, b_map)] + list(epi_ins)
    out_list = []
    for shape, dtype, cols in outs:
        cols = cols or bn
        blk, imap = _vblock(shape, bm, cols, lambda i, j, k: (i, j))
        out_list.append((shape, dtype, blk, imap))
    n_e, n_o = len(epi_ins), len(out_list)

    def body(*refs):
        a_ref, b_ref = refs[0], refs[1]
        e_refs = refs[2:2 + n_e]
        o_refs = refs[2 + n_e:2 + n_e + n_o]

        def finish(acc):
            res = epi(acc, *e_refs) if epi is not None else (acc,)
            for o_ref, r in zip(o_refs, res):
                o_ref[...] = r.astype(o_ref.dtype)

        x = a_ref[...].astype(BF)
        y = b_ref[...].astype(BF)
        if k_shards:
            p = None
            for s in range(k_shards):
                part = lax.dot_general(x[:, s * b_sw:(s + 1) * b_sw], y[s], dn, preferred_element_type=F32)
                p = part if p is None else p + part
        else:
            p = lax.dot_general(x, y, dn, preferred_element_type=F32)
        if nk == 1:
            finish(p)
        else:
            acc_ref = refs[-1]
            k = pl.program_id(2)

            @pl.when(k == 0)
            def _():
                acc_ref[...] = p

            @pl.when(k > 0)
            def _():
                acc_ref[...] += p

            @pl.when(k == nk - 1)
            def _():
                finish(acc_ref[...])

    scratch = [pltpu.VMEM((bm, bn), F32)] if nk > 1 else []
    res = _pcall(body, name=name, grid=grid, ins=ins, outs=out_list, scratch=scratch,
                 semantics=("parallel", "parallel", "arbitrary"), deps=deps)
    return res[0] if len(res) == 1 else res


_GELU_K = float(np.sqrt(2.0 / np.pi))
_GELU_C = 0.044715


def _gelu(x):
    t = jnp.tanh(_GELU_K * (x + _GELU_C * (x * x * x)))
    return 0.5 * x * (1.0 + t)


def _gelu_grad(x):
    t = jnp.tanh(_GELU_K * (x + _GELU_C * (x * x * x)))
    return 0.5 * (1.0 + t) + 0.5 * x * (1.0 - t * t) * (_GELU_K * (1.0 + 3.0 * _GELU_C * (x * x)))


def _rstd(x):
    return lax.rsqrt(jnp.mean(x * x, axis=-1, keepdims=True) + EPS)


def _rms_bwd(x, gain, dy):
    r = _rstd(x)
    xh = x * r
    gdy = dy * gain
    dx = r * (gdy - xh * jnp.mean(gdy * xh, axis=-1, keepdims=True))
    return dx, dy * xh


def _rope_fwd(x, cos_t, sin_t):
    return x * cos_t + pltpu.roll(x, 2 * HALF_ROPE, 1) * sin_t


def _rope_bwd(dy, cos_t, sin_t):
    return dy * cos_t + pltpu.roll(dy * sin_t, 2 * HALF_ROPE, 1)


def _acc_rows(ref, val, first):
    s = jnp.sum(val, axis=0, keepdims=True)

    @pl.when(first)
    def _():
        ref[...] = s

    @pl.when(jnp.logical_not(first))
    def _():
        ref[...] += s


def rms_fwd(x, gain, *, name, col_block=0, width=None, deps=()):
    T = x.shape[0]
    width = width or x.shape[1]
    tm = _tile(T, ROW_TILE, 8)

    def body(x_ref, g_ref, o_ref):
        v = x_ref[...]
        o_ref[...] = (v * _rstd(v) * g_ref[...]).astype(BF)

    return _pcall(body, name=name, grid=(T // tm,),
                  ins=[(x, (tm, width), lambda i: (i, col_block)), (gain, (1, width), lambda i: (0, 0))],
                  outs=[((T, width), BF, (tm, width), lambda i: (i, 0))], semantics=("parallel",), deps=deps)[0]


def rms_bwd(x, gain, dy, *, name, col_block=0, dres=None, want_f32=True, want_bf=True, deps=()):
    T, width = dy.shape
    tm = _tile(T, ROW_TILE, 8)
    has_res = dres is not None

    def body(*refs):
        x_ref, g_ref, dy_ref = refs[:3]
        pos = 3
        res_ref = None
        if has_res:
            res_ref = refs[pos]
            pos += 1
        outs = refs[pos:]
        dx, dg_rows = _rms_bwd(x_ref[...], g_ref[...], dy_ref[...])
        if has_res:
            dx = dx + res_ref[...]
        o = 0
        if want_f32:
            outs[o][...] = dx
            o += 1
        if want_bf:
            outs[o][...] = dx.astype(BF)
            o += 1
        _acc_rows(outs[o], dg_rows, pl.program_id(0) == 0)

    ins = [(x, (tm, width), lambda i: (i, col_block)), (gain, (1, width), lambda i: (0, 0)),
           (dy, (tm, width), lambda i: (i, 0))]
    if has_res:
        ins.append((dres, (tm, width), lambda i: (i, 0)))
    outs = []
    if want_f32:
        outs.append(((T, width), F32, (tm, width), lambda i: (i, 0)))
    if want_bf:
        outs.append(((T, width), BF, (tm, width), lambda i: (i, 0)))
    outs.append(((1, width), F32, (1, width), lambda i: (0, 0)))
    return _pcall(body, name=name, grid=(T // tm,), ins=ins, outs=outs, deps=deps)


def mla_prep(proj, q_norm, kv_norm, cos_t, sin_t, *, name):
    T = proj.shape[0]
    tm = _tile(T, ROW_TILE, 8)
    kr_block = (proj.shape[1] - LANES) // LANES

    def body(cq_ref, ckv_ref, kr_ref, qg_ref, kg_ref, cos_ref, sin_ref, qn_ref, kvn_ref, krope_ref):
        cq = cq_ref[...]
        qn_ref[...] = (cq * _rstd(cq) * qg_ref[...]).astype(BF)
        ckv = ckv_ref[...]
        kvn_ref[...] = (ckv * _rstd(ckv) * kg_ref[...]).astype(BF)
        krope_ref[...] = _rope_fwd(kr_ref[...], cos_ref[...], sin_ref[...]).astype(BF)

    return _pcall(
        body, name=name, grid=(T // tm,),
        ins=[(proj, (tm, Q_LORA), lambda i: (i, 0)), (proj, (tm, KV_LORA), lambda i: (i, 1)),
             (proj, (tm, LANES), lambda i: (i, kr_block)),
             (q_norm, (1, Q_LORA), lambda i: (0, 0)), (kv_norm, (1, KV_LORA), lambda i: (0, 0)),
             (cos_t, (tm, LANES), lambda i: (i, 0)), (sin_t, (tm, LANES), lambda i: (i, 0))],
        outs=[((T, Q_LORA), BF, (tm, Q_LORA), lambda i: (i, 0)), ((T, KV_LORA), BF, (tm, KV_LORA), lambda i: (i, 0)),
              ((T, LANES), BF, (tm, LANES), lambda i: (i, 0))],
        semantics=("parallel",))


def _attn_scores(q, k_blk, diagonal):
    s = lax.dot_general(q, k_blk, (((1,), (1,)), ((), ())), preferred_element_type=F32) * ATTN_SCALE
    if diagonal:
        row = lax.broadcasted_iota(jnp.int32, s.shape, 0)
        col = lax.broadcasted_iota(jnp.int32, s.shape, 1)
        s = jnp.where(col <= row, s, -jnp.inf)
    return s


def attn_fwd(q, k, v, *, name):
    T = q.shape[0]
    tq = _tile(T, ATTN_TILE, 8)

    def body(q_ref, k_ref, v_ref, o_ref, lse_ref):
        i = pl.program_id(1)
        qv = q_ref[...]

        def block(kb, carry, diagonal):
            m, l, acc = carry
            start = pl.multiple_of(kb * tq, tq)
            s = _attn_scores(qv, k_ref[pl.ds(start, tq), :], diagonal)
            m_new = jnp.maximum(m, jnp.max(s, axis=-1, keepdims=True))
            alpha = jnp.exp(m - m_new)
            p = jnp.exp(s - m_new)
            l = alpha * l + jnp.sum(p, axis=-1, keepdims=True)
            acc = alpha * acc + jnp.dot(p.astype(BF), v_ref[pl.ds(start, tq), :], preferred_element_type=F32)
            return m_new, l, acc

        init = (jnp.full((tq, 1), -jnp.inf, F32), jnp.zeros((tq, 1), F32), jnp.zeros((tq, V_HEAD), F32))
        carry = lax.fori_loop(0, i, lambda kb, c: block(kb, c, False), init)
        m, l, acc = block(i, carry, True)
        o_ref[...] = acc / l
        lse_ref[...] = jnp.broadcast_to(m + jnp.log(l), (tq, V_HEAD))

    return _pcall(
        body, name=name, grid=(HEADS, T // tq),
        ins=[(q, (tq, HEAD_PAD), lambda h, i: (i, h)), (k, (T, HEAD_PAD), lambda h, i: (0, h)),
             (v, (T, V_HEAD), lambda h, i: (0, h))],
        outs=[((T, MLA_OUT), F32, (tq, V_HEAD), lambda h, i: (i, h)),
              ((T, MLA_OUT), F32, (tq, V_HEAD), lambda h, i: (i, h))], semantics=("parallel", "parallel"))


def attn_bwd(q, k, v, o, lse, do, *, name):
    T = q.shape[0]
    tq = _tile(T, ATTN_TILE, 8)

    def body(q_ref, k_ref, v_ref, o_ref, lse_ref, do_ref, dq_ref, dk_ref, dv_ref):
        i = pl.program_id(1)

        @pl.when(i == 0)
        def _():
            dk_ref[...] = jnp.zeros_like(dk_ref)
            dv_ref[...] = jnp.zeros_like(dv_ref)

        qv = q_ref[...]
        do_t = do_ref[...]
        lse_v = lse_ref[:, 0:1]
        delta = jnp.sum(do_t.astype(F32) * o_ref[...], axis=-1, keepdims=True)

        def block(kb, dq, diagonal):
            start = pl.multiple_of(kb * tq, tq)
            k_blk = k_ref[pl.ds(start, tq), :]
            v_blk = v_ref[pl.ds(start, tq), :]
            p = jnp.exp(_attn_scores(qv, k_blk, diagonal) - lse_v)
            dp = lax.dot_general(do_t, v_blk, (((1,), (1,)), ((), ())), preferred_element_type=F32)
            ds = (p * (dp - delta) * ATTN_SCALE).astype(BF)
            dk_ref[pl.ds(start, tq), :] += lax.dot_general(ds, qv, (((0,), (0,)), ((), ())), preferred_element_type=F32)
            dv_ref[pl.ds(start, tq), :] += lax.dot_general(p.astype(BF), do_t, (((0,), (0,)), ((), ())),
                                                          preferred_element_type=F32)
            return dq + jnp.dot(ds, k_blk, preferred_element_type=F32)

        dq = lax.fori_loop(0, i, lambda kb, c: block(kb, c, False), jnp.zeros((tq, HEAD_PAD), F32))
        dq_ref[...] = block(i, dq, True)

    return _pcall(
        body, name=name, grid=(HEADS, T // tq),
        ins=[(q, (tq, HEAD_PAD), lambda h, i: (i, h)), (k, (T, HEAD_PAD), lambda h, i: (0, h)),
             (v, (T, V_HEAD), lambda h, i: (0, h)), (o, (tq, V_HEAD), lambda h, i: (i, h)),
             (lse, (tq, V_HEAD), lambda h, i: (i, h)), (do, (tq, V_HEAD), lambda h, i: (i, h))],
        outs=[((T, HEADS * HEAD_PAD), F32, (tq, HEAD_PAD), lambda h, i: (i, h)),
              ((T, HEADS * HEAD_PAD), F32, (T, HEAD_PAD), lambda h, i: (0, h)),
              ((T, MLA_OUT), F32, (T, V_HEAD), lambda h, i: (0, h))],
        semantics=("parallel", "arbitrary"))


def mla_bwd_prep(dq, dk, dv, cos_t, sin_t, *, name):
    T = dq.shape[0]
    tm = _tile(T, ROW_TILE, 8)

    def body(dq_ref, dk_ref, dv_ref, cos_ref, sin_ref, dql_ref, dkvl_ref, dkr_ref):
        cos_v, sin_v = cos_ref[...], sin_ref[...]
        kr = jnp.zeros((tm, LANES), F32)
        for h in range(HEADS):
            lo = h * HEAD_PAD
            dql_ref[:, lo:lo + QK_NOPE] = dq_ref[:, lo:lo + QK_NOPE].astype(BF)
            dql_ref[:, lo + QK_NOPE:lo + HEAD_PAD] = _rope_bwd(
                dq_ref[:, lo + QK_NOPE:lo + HEAD_PAD], cos_v, sin_v).astype(BF)
            dkvl_ref[:, lo:lo + QK_NOPE] = dk_ref[:, lo:lo + QK_NOPE].astype(BF)
            dkvl_ref[:, lo + QK_NOPE:lo + HEAD_PAD] = dv_ref[:, h * V_HEAD:(h + 1) * V_HEAD].astype(BF)
            kr = kr + dk_ref[:, lo + QK_NOPE:lo + HEAD_PAD]
        dkr_ref[...] = _rope_bwd(kr, cos_v, sin_v).astype(BF)

    W = HEADS * HEAD_PAD
    return _pcall(
        body, name=name, grid=(T // tm,),
        ins=[(dq, (tm, W), lambda i: (i, 0)), (dk, (tm, W), lambda i: (i, 0)), (dv, (tm, MLA_OUT), lambda i: (i, 0)),
             (cos_t, (tm, LANES), lambda i: (i, 0)), (sin_t, (tm, LANES), lambda i: (i, 0))],
        outs=[((T, W), BF, (tm, W), lambda i: (i, 0)), ((T, W), BF, (tm, W), lambda i: (i, 0)),
              ((T, LANES), BF, (tm, LANES), lambda i: (i, 0))],
        semantics=("parallel",))


def _group_norm_stats(vg):
    mu = jnp.mean(vg, axis=-1, keepdims=True)
    d = vg - mu
    r = lax.rsqrt(jnp.mean(d * d, axis=-1, keepdims=True) + EPS)
    return d * r, r


def mix_fwd(a, proj, g_mla, g_sgu, v_gain, w_tril, b_full, *, name):
    T = a.shape[0]
    tm = _tile(T, ROW_TILE, CHUNK)
    n_chunk = tm // CHUNK

    def body(a_ref, u_ref, v_ref, gm_ref, gs_ref, vg_ref, w_ref, b_ref, o_ref, s_scr):
        av = a_ref[...]
        o_ref[:, :MLA_OUT] = (av * _rstd(av) * gm_ref[...]).astype(BF)
        for g in range(GROUPS):
            sl = slice(g * CH, (g + 1) * CH)
            vhat, _ = _group_norm_stats(_gelu(v_ref[:, sl]))
            vn = (vhat * vg_ref[:, sl]).astype(BF)
            u = _gelu(u_ref[:, sl])
            for ci in range(n_chunk):
                rs = slice(ci * CHUNK, (ci + 1) * CHUNK)
                y = jnp.dot(w_ref[g], vn[rs], preferred_element_type=F32) + b_ref[:, sl]
                s_scr[rs, sl] = u[rs] * y
        s = s_scr[...]
        o_ref[:, MLA_OUT:] = (s * _rstd(s) * gs_ref[...]).astype(BF)

    return _pcall(
        body, name=name, grid=(T // tm,),
        ins=[(a, (tm, MLA_OUT), lambda i: (i, 0)), (proj, (tm, SGU_OUT), lambda i: (i, 1)),
             (proj, (tm, SGU_OUT), lambda i: (i, 2)), (g_mla, (1, MLA_OUT), lambda i: (0, 0)),
             (g_sgu, (1, SGU_OUT), lambda i: (0, 0)), (v_gain, (1, SGU_OUT), lambda i: (0, 0)),
             (w_tril, (GROUPS, CHUNK, CHUNK), lambda i: (0, 0, 0)), (b_full, (CHUNK, SGU_OUT), lambda i: (0, 0))],
        outs=[((T, MLA_OUT + SGU_OUT), BF, (tm, MLA_OUT + SGU_OUT), lambda i: (i, 0))],
        scratch=[pltpu.VMEM((tm, SGU_OUT), F32)], semantics=("parallel",))[0]


def mix_bwd(dmixed, a, proj, g_mla, g_sgu, v_gain, w_tril, w_tril_t, b_full, *, name):
    T = a.shape[0]
    tm = _tile(T, ROW_TILE, CHUNK)
    n_chunk = tm // CHUNK

    def body(dm_a_ref, dm_s_ref, a_ref, u_ref, v_ref, gm_ref, gs_ref, vg_ref, w_ref, wt_ref, b_ref,
             da_ref, duv_ref, dgm_ref, dgs_ref, dvg_ref, dw_ref, db_ref, s_scr, y_scr):
        first = pl.program_id(0) == 0
        da, dgm_rows = _rms_bwd(a_ref[...], gm_ref[...], dm_a_ref[...])
        da_ref[...] = da.astype(BF)
        _acc_rows(dgm_ref, dgm_rows, first)

        for g in range(GROUPS):
            sl = slice(g * CH, (g + 1) * CH)
            vhat, _ = _group_norm_stats(_gelu(v_ref[:, sl]))
            vn = (vhat * vg_ref[:, sl]).astype(BF)
            u = _gelu(u_ref[:, sl])
            for ci in range(n_chunk):
                rs = slice(ci * CHUNK, (ci + 1) * CHUNK)
                y = jnp.dot(w_ref[g], vn[rs], preferred_element_type=F32) + b_ref[:, sl]
                y_scr[rs, sl] = y
                s_scr[rs, sl] = u[rs] * y
        ds, dgs_rows = _rms_bwd(s_scr[...], gs_ref[...], dm_s_ref[...])
        _acc_rows(dgs_ref, dgs_rows, first)
        s_scr[...] = ds

        @pl.when(first)
        def _():
            dw_ref[...] = jnp.zeros_like(dw_ref)
            db_ref[...] = jnp.zeros_like(db_ref)

        for g in range(GROUPS):
            sl = slice(g * CH, (g + 1) * CH)
            upre = u_ref[:, sl]
            vpre = v_ref[:, sl]
            u = _gelu(upre)
            vhat, r = _group_norm_stats(_gelu(vpre))
            gain = vg_ref[:, sl]
            vn = (vhat * gain).astype(BF)
            dsg = s_scr[:, sl]
            duv_ref[:, sl] = (dsg * y_scr[:, sl] * _gelu_grad(upre)).astype(BF)
            dy = dsg * u
            dyb = dy.astype(BF)
            dvn_parts = []
            for ci in range(n_chunk):
                rs = slice(ci * CHUNK, (ci + 1) * CHUNK)
                dvn_parts.append(jnp.dot(wt_ref[g], dyb[rs], preferred_element_type=F32))
                dw_ref[g] += lax.dot_general(dyb[rs], vn[rs], (((1,), (1,)), ((), ())), preferred_element_type=F32)
                db_ref[:, sl] += jnp.broadcast_to(jnp.sum(dy[rs], axis=-1, keepdims=True), (CHUNK, CH))
            dvn = dvn_parts[0] if n_chunk == 1 else jnp.concatenate(dvn_parts, axis=0)
            _acc_rows(dvg_ref.at[:, sl], dvn * vhat, first)
            dvh = dvn * gain
            dvg = r * (dvh - jnp.mean(dvh, axis=-1, keepdims=True)
                       - vhat * jnp.mean(dvh * vhat, axis=-1, keepdims=True))
            duv_ref[:, SGU_OUT + g * CH:SGU_OUT + (g + 1) * CH] = (dvg * _gelu_grad(vpre)).astype(BF)

    return _pcall(
        body, name=name, grid=(T // tm,),
        ins=[(dmixed, (tm, MLA_OUT), lambda i: (i, 0)), (dmixed, (tm, SGU_OUT), lambda i: (i, 1)),
             (a, (tm, MLA_OUT), lambda i: (i, 0)), (proj, (tm, SGU_OUT), lambda i: (i, 1)),
             (proj, (tm, SGU_OUT), lambda i: (i, 2)), (g_mla, (1, MLA_OUT), lambda i: (0, 0)),
             (g_sgu, (1, SGU_OUT), lambda i: (0, 0)), (v_gain, (1, SGU_OUT), lambda i: (0, 0)),
             (w_tril, (GROUPS, CHUNK, CHUNK), lambda i: (0, 0, 0)), (w_tril_t, (GROUPS, CHUNK, CHUNK), lambda i: (0, 0, 0)),
             (b_full, (CHUNK, SGU_OUT), lambda i: (0, 0))],
        outs=[((T, MLA_OUT), BF, (tm, MLA_OUT), lambda i: (i, 0)),
              ((T, 2 * SGU_OUT), BF, (tm, 2 * SGU_OUT), lambda i: (i, 0)),
              ((1, MLA_OUT), F32, (1, MLA_OUT), lambda i: (0, 0)), ((1, SGU_OUT), F32, (1, SGU_OUT), lambda i: (0, 0)),
              ((1, SGU_OUT), F32, (1, SGU_OUT), lambda i: (0, 0)),
              ((GROUPS, CHUNK, CHUNK), F32, (GROUPS, CHUNK, CHUNK), lambda i: (0, 0, 0)),
              ((CHUNK, SGU_OUT), F32, (CHUNK, SGU_OUT), lambda i: (0, 0))],
        scratch=[pltpu.VMEM((tm, SGU_OUT), F32), pltpu.VMEM((tm, SGU_OUT), F32)])


def _shift_down(z, n, row):
    return jnp.where(row >= n, pltpu.roll(z, n, 0), 0.0)


def _shift_up(z, n, row, T):
    return jnp.where(row < T - n, pltpu.roll(z, T - n, 0), 0.0)


def conv_fwd(proj, conv_w, *, name):
    T, D3 = proj.shape
    D = D3 // 3
    tn = _tile(D, 256)
    nj = D // tn

    def body(b_ref, c_ref, x_ref, w_ref, o_ref):
        row = lax.broadcasted_iota(jnp.int32, (T, tn), 0)
        z = c_ref[...] * x_ref[...]
        zc = w_ref[2:3, :] * z + w_ref[1:2, :] * _shift_down(z, 1, row) + w_ref[0:1, :] * _shift_down(z, 2, row)
        o_ref[...] = (b_ref[...] * zc).astype(BF)

    return _pcall(
        body, name=name, grid=(nj,),
        ins=[(proj, (T, tn), lambda j: (0, j)), (proj, (T, tn), lambda j: (0, nj + j)),
             (proj, (T, tn), lambda j: (0, 2 * nj + j)), (conv_w, (3, tn), lambda j: (0, j))],
        outs=[((T, D), BF, (T, tn), lambda j: (0, j))], semantics=("parallel",))[0]


def conv_bwd(dg, proj, conv_w, *, name):
    T, D3 = proj.shape
    D = D3 // 3
    tn = _tile(D, 256)
    nj = D // tn

    def body(dg_ref, b_ref, c_ref, x_ref, w_ref, dp_ref, dw_ref, dc_scr, dx_scr):
        part = pl.program_id(1)

        @pl.when(part == 0)
        def _():
            row = lax.broadcasted_iota(jnp.int32, (T, tn), 0)
            c, x = c_ref[...], x_ref[...]
            z = c * x
            z1 = _shift_down(z, 1, row)
            z2 = _shift_down(z, 2, row)
            dgv = dg_ref[...]
            zc = w_ref[2:3, :] * z + w_ref[1:2, :] * z1 + w_ref[0:1, :] * z2
            dp_ref[...] = (dgv * zc).astype(BF)
            dzc = dgv * b_ref[...]
            dw_ref[0:1, :] = jnp.sum(dzc * z2, axis=0, keepdims=True)
            dw_ref[1:2, :] = jnp.sum(dzc * z1, axis=0, keepdims=True)
            dw_ref[2:3, :] = jnp.sum(dzc * z, axis=0, keepdims=True)
            dz = (w_ref[2:3, :] * dzc + w_ref[1:2, :] * _shift_up(dzc, 1, row, T)
                  + w_ref[0:1, :] * _shift_up(dzc, 2, row, T))
            dc_scr[...] = (dz * x).astype(BF)
            dx_scr[...] = (dz * c).astype(BF)

        @pl.when(part == 1)
        def _():
            dp_ref[...] = dc_scr[...]

        @pl.when(part == 2)
        def _():
            dp_ref[...] = dx_scr[...]

    return _pcall(
        body, name=name, grid=(nj, 3),
        ins=[(dg, (T, tn), lambda j, p: (0, j)), (proj, (T, tn), lambda j, p: (0, j)),
             (proj, (T, tn), lambda j, p: (0, nj + j)), (proj, (T, tn), lambda j, p: (0, 2 * nj + j)),
             (conv_w, (3, tn), lambda j, p: (0, j))],
        outs=[((T, D3), BF, (T, tn), lambda j, p: (0, p * nj + j)), ((3, D), F32, (3, tn), lambda j, p: (0, j))],
        scratch=[pltpu.VMEM((T, tn), BF), pltpu.VMEM((T, tn), BF)], semantics=("parallel", "arbitrary"))


def loss_bwd(x, gain, target, *, name):
    T, D = x.shape
    tm = _tile(T, ROW_TILE, 8)

    def body(x_ref, g_ref, t_ref, dx_ref, dxb_ref, dg_ref, loss_ref):
        first = pl.program_id(0) == 0
        xv = x_ref[...]
        r = _rstd(xv)
        xh = xv * r
        gain_v = g_ref[...]
        err = xh * gain_v - t_ref[...]
        part = 0.5 * jnp.sum(jnp.mean(err * err, axis=-1, keepdims=True), axis=0, keepdims=True)
        _acc_rows(loss_ref, jnp.broadcast_to(part, (1, LANES)), first)
        dy = err * (1.0 / D)
        gdy = dy * gain_v
        dx = r * (gdy - xh * jnp.mean(gdy * xh, axis=-1, keepdims=True))
        dx_ref[...] = dx
        dxb_ref[...] = dx.astype(BF)
        _acc_rows(dg_ref, dy * xh, first)

    return _pcall(
        body, name=name, grid=(T // tm,),
        ins=[(x, (tm, D), lambda i: (i, 0)), (gain, (1, D), lambda i: (0, 0)), (target, (tm, D), lambda i: (i, 0))],
        outs=[((T, D), F32, (tm, D), lambda i: (i, 0)), ((T, D), BF, (tm, D), lambda i: (i, 0)),
              ((1, D), F32, (1, D), lambda i: (0, 0)), ((1, LANES), F32, (1, LANES), lambda i: (0, 0))])


def _adamw(g, w, m, v):
    m = ADAM_B1 * m + (1.0 - ADAM_B1) * g
    v = ADAM_B2 * v + (1.0 - ADAM_B2) * (g * g)
    m_hat = m / ADAM_C1
    v_hat = v / ADAM_C2
    delta = -ADAM_LR * (m_hat / (jnp.sqrt(v_hat) + ADAM_EPS) + ADAM_WD * w)
    return delta, m, v


def adam_flat(g, w, m, v, *, name):
    def body(g_ref, w_ref, m_ref, v_ref, d_ref, nm_ref, nv_ref):
        d, nm, nv = _adamw(g_ref[...], w_ref[...], m_ref[...], v_ref[...])
        d_ref[...] = d
        nm_ref[...] = nm
        nv_ref[...] = nv

    blk = g.shape
    zero = lambda: (0, 0)
    return _pcall(body, name=name, grid=(),
                  ins=[(t, blk, zero) for t in (g, w, m, v)],
                  outs=[(blk, F32, blk, zero)] * 3)


def _chip_slots():
    x, y, c = lax.axis_index("x"), lax.axis_index("y"), lax.axis_index("c")
    chips = [(1 - x, y), (x, 1 - y), (1 - x, 1 - y)]
    return x, y, c, chips


def reduce_adam(gs, a_buf, b_buf, w, m, v, layer, prev, *, name, deps=()):
    L, R, C = w.shape
    tr = _tile(R, 256, 8)
    x, y, c, _ = _chip_slots()
    idx = jnp.stack([4 * x + 2 * y + c, 2 * x + y]).astype(jnp.int32)
    n_prev = 0 if prev is None else 4

    def body(idx_ref, g_ref, a_ref, b0_ref, b1_ref, b2_ref, w_ref, m_ref, v_ref, *rest):
        outs = rest[n_prev:]
        g = ((((g_ref[...].astype(F32) + a_ref[...].astype(F32)) + b0_ref[...].astype(F32))
              + b1_ref[...].astype(F32)) + b2_ref[...].astype(F32))
        d, nm, nv = _adamw(g, w_ref[...], m_ref[...], v_ref[...])
        outs[0][...] = g
        outs[1][...] = d
        outs[2][...] = nm
        outs[3][...] = nv

    blk3 = (None, tr, C)
    ins = [(gs, blk3, lambda i, s: (s[0], i, 0)), (a_buf, blk3, lambda i, s: (s[1], i, 0)),
           (b_buf, blk3, lambda i, s: (0, i, 0)), (b_buf, blk3, lambda i, s: (1, i, 0)),
           (b_buf, blk3, lambda i, s: (2, i, 0)),
           (w, blk3, lambda i, s: (layer, i, 0)), (m, blk3, lambda i, s: (layer, i, 0)),
           (v, blk3, lambda i, s: (layer, i, 0))]
    aliases = {}
    if prev is not None:
        for o, p in enumerate(prev):
            ins.append((p, None, None))
            aliases[1 + 8 + o] = o
    outs = [((L, R, C), F32, blk3, lambda i, s: (layer, i, 0))] * 4
    return _pcall(body, name=name, grid=(R // tr,), ins=ins, outs=outs, prefetch=idx, aliases=aliases,
                  semantics=("parallel",), deps=deps)


def reduce_sum(gs, a_buf, b_buf, *, name):
    _, R, C = gs.shape
    tr = _tile(R, 256, 16)
    x, y, c, _ = _chip_slots()
    idx = jnp.stack([4 * x + 2 * y + c, 2 * x + y]).astype(jnp.int32)

    def body(idx_ref, g_ref, a_ref, b0_ref, b1_ref, b2_ref, o_ref):
        o_ref[...] = ((((g_ref[...].astype(F32) + a_ref[...].astype(F32)) + b0_ref[...].astype(F32))
                       + b1_ref[...].astype(F32)) + b2_ref[...].astype(F32))

    blk3 = (None, tr, C)
    return _pcall(body, name=name, grid=(R // tr,),
                  ins=[(gs, blk3, lambda i, s: (s[0], i, 0)), (a_buf, blk3, lambda i, s: (s[1], i, 0)),
                       (b_buf, blk3, lambda i, s: (0, i, 0)), (b_buf, blk3, lambda i, s: (1, i, 0)),
                       (b_buf, blk3, lambda i, s: (2, i, 0))],
                  outs=[((R, C), F32, (tr, C), lambda i, s: (i, 0))], prefetch=idx, semantics=("parallel",))[0]


def adam_rows(g, w, m, v, *, name):
    R, C = g.shape
    tr = _tile(R, 256, 8)

    def body(g_ref, w_ref, m_ref, v_ref, d_ref, nm_ref, nv_ref):
        d, nm, nv = _adamw(g_ref[...], w_ref[...], m_ref[...], v_ref[...])
        d_ref[...] = d
        nm_ref[...] = nm
        nv_ref[...] = nv

    spec = ((tr, C), lambda i: (i, 0))
    return _pcall(body, name=name, grid=(R // tr,), ins=[(t, *spec) for t in (g, w, m, v)],
                  outs=[((R, C), F32, *spec)] * 3, semantics=("parallel",))


def pair_sum(gs, a_buf, *, name):
    _, R, C = gs.shape
    tr = _tile(R, 256, 16)
    x, y, c, chips = _chip_slots()
    idx = jnp.stack([4 * cx + 2 * cy + c for cx, cy in chips] + [2 * cx + cy for cx, cy in chips]).astype(jnp.int32)

    def body(idx_ref, g_ref, a_ref, o_ref):
        o_ref[...] = (g_ref[...].astype(F32) + a_ref[...].astype(F32)).astype(BF)

    blk3 = (None, tr, C)
    return _pcall(body, name=name, grid=(3, R // tr),
                  ins=[(gs, blk3, lambda j, i, s: (s[j], i, 0)), (a_buf, blk3, lambda j, i, s: (s[3 + j], i, 0))],
                  outs=[((3, R, C), BF, blk3, lambda j, i, s: (j, i, 0))], prefetch=idx,
                  semantics=("parallel", "parallel"))[0]


def sum_rows8(gathered, rows, *, name):
    W = gathered.shape[1]

    def body(g_ref, o_ref):
        acc = g_ref[0:rows, :]
        for d in range(1, N_DEV):
            acc = acc + g_ref[d * rows:(d + 1) * rows, :]
        o_ref[...] = acc

    return _pcall(body, name=name, grid=(), ins=[(gathered, gathered.shape, lambda: (0, 0))],
                  outs=[((rows, W), F32, (rows, W), lambda: (0, 0))])[0]


HBM_SPEC = pl.BlockSpec(memory_space=pltpu.HBM)
SEM_SPEC = pl.BlockSpec(memory_space=pltpu.SEMAPHORE)
ANY_SPEC = pl.BlockSpec(memory_space=pl.ANY)
DATAFLOW = pltpu.SideEffectType.DATAFLOW_SIDE_EFFECTING


def _in_hbm(v):
    return pltpu.with_memory_space_constraint(v, pltpu.HBM)


def _slot(p):
    return 4 * p[0] + 2 * p[1] + p[2]


def _gather_peers():
    x, y, c, chips = _chip_slots()
    return (x, y, c), [(x, y, 1 - c)] + [(*chip, c) for chip in chips]


def gather_start(groups, after, *, name):
    flat = [s for g in groups for s in g]
    n, n_g = len(flat), len(groups)
    where = [(gi, ti) for gi, g in enumerate(groups) for ti in range(len(g))]

    def body(*refs):
        src, land = refs[:n], refs[n:2 * n]
        sems = refs[2 * n + 1:2 * n + 1 + 2 * n_g]
        me, peers = _gather_peers()
        for t in range(n):
            gi, ti = where[t]
            for k, to in enumerate(peers):
                pltpu.make_async_remote_copy(
                    src_ref=src[t], dst_ref=land[t].at[_slot(me)], send_sem=sems[2 * gi].at[4 * ti + k],
                    recv_sem=sems[2 * gi + 1].at[4 * ti + k], device_id=to, device_id_type=MESH).start()
        refs[-1][...] = jnp.zeros_like(refs[-1])

    out_shape = []
    for g in groups:
        out_shape += [pltpu.SemaphoreType.DMA((4 * len(g),)), pltpu.SemaphoreType.DMA((4 * len(g),))]
    out_shape += [pltpu.HBM(s.shape, s.dtype) for s in flat]
    out_shape += [pltpu.HBM((N_DEV,) + s.shape, s.dtype) for s in flat]
    out_shape += [jax.ShapeDtypeStruct((8, LANES), F32)]
    aliases = {t: 2 * n_g + t for t in range(n)}
    aliases.update({n + t: 2 * n_g + n + t for t in range(n)})
    res = pl.pallas_call(
        body, name=name, out_shape=out_shape, in_specs=[HBM_SPEC] * (2 * n) + [ANY_SPEC],
        out_specs=[SEM_SPEC] * (2 * n_g) + [HBM_SPEC] * (2 * n) + [pl.BlockSpec(memory_space=pltpu.VMEM)],
        input_output_aliases=aliases, compiler_params=pltpu.CompilerParams(has_side_effects=DATAFLOW),
    )(*[_in_hbm(s) for s in flat], *[_in_hbm(lax.empty((N_DEV,) + s.shape, s.dtype)) for s in flat], after)
    out, off = [], 0
    for gi, g in enumerate(groups):
        k = len(g)
        out.append((res[2 * gi], res[2 * gi + 1], res[2 * n_g + off:2 * n_g + off + k],
                    res[2 * n_g + n + off:2 * n_g + n + off + k]))
        off += k
    return out, res[-1]


def gather_wait(started, after, *, name):
    send_sems, recv_sems, srcs, lands = started
    n = len(srcs)
    after = list(after)

    def body(*refs):
        src, land = refs[:n], refs[n:2 * n]
        send, recv = refs[2 * n], refs[2 * n + 1]
        _, peers = _gather_peers()
        for t in range(n):
            for k, frm in enumerate(peers):
                cp = pltpu.make_async_remote_copy(
                    src_ref=src[t], dst_ref=land[t].at[_slot(frm)], send_sem=send.at[4 * t + k],
                    recv_sem=recv.at[4 * t + k],
                    device_id=frm, device_id_type=MESH)
                cp.wait_send()
                cp.wait_recv()

    res = pl.pallas_call(
        body, name=name,
        out_shape=[pltpu.HBM(s.shape, s.dtype) for s in srcs] + [pltpu.HBM(l.shape, l.dtype) for l in lands],
        in_specs=[HBM_SPEC] * (2 * n) + [SEM_SPEC, SEM_SPEC] + [ANY_SPEC] * len(after),
        out_specs=[HBM_SPEC] * (2 * n), input_output_aliases={t: t for t in range(2 * n)},
        compiler_params=pltpu.CompilerParams(has_side_effects=DATAFLOW),
    )(*srcs, *lands, send_sems, recv_sems, *after)
    return res[:n], res[n:]


def place_own(src, land, *, name):
    R, C = src.shape
    tr = _tile(R, 512, 16)
    x, y, c, _ = _chip_slots()
    idx = jnp.stack([4 * x + 2 * y + c]).astype(jnp.int32)

    def body(idx_ref, s_ref, land_ref, o_ref):
        o_ref[...] = s_ref[...]

    return _pcall(body, name=name, grid=(R // tr,),
                  ins=[(src, (tr, C), lambda i, s: (i, 0)), (land, None, None)],
                  outs=[(land.shape, land.dtype, (None, tr, C), lambda i, s: (s[0], i, 0))],
                  prefetch=idx, aliases={2: 0}, semantics=("parallel",))[0]


def gather_finish(srcs, lands, *, name):
    n = len(srcs)

    def body(*refs):
        land = refs[n:2 * n]
        send_sems, recv_sems = refs[2 * n:]
        x, y, c, chips = _chip_slots()
        me, sibling = (x, y, c), (x, y, 1 - c)

        def copy(t, j, block, to):
            return pltpu.make_async_remote_copy(
                src_ref=land[t].at[_slot(block)], dst_ref=land[t].at[_slot(block)], send_sem=send_sems.at[t, j],
                recv_sem=recv_sems.at[t, j], device_id=to, device_id_type=MESH)

        sends = [copy(t, j, (*chip, c), sibling) for t in range(n) for j, chip in enumerate(chips)]
        for cp in sends:
            cp.start()
        for t in range(n):
            for j, chip in enumerate(chips):
                copy(t, j, (*chip, 1 - c), me).wait_recv()
        for cp in sends:
            cp.wait_send()

    passed = pl.pallas_call(
        body, name=name, out_shape=[jax.ShapeDtypeStruct(l.shape, l.dtype) for l in lands],
        in_specs=[ANY_SPEC] * n, out_specs=[ANY_SPEC] * n,
        input_output_aliases={t: t for t in range(n)},
        scratch_shapes=[pltpu.SemaphoreType.DMA((n, 3)), pltpu.SemaphoreType.DMA((n, 3))],
    )(*lands)
    return [place_own(s, l, name=f"{name}_own{t}") for t, (s, l) in enumerate(zip(srcs, passed))]


def chips_start(pairs, *, name):
    n = len(pairs)

    def body(*refs):
        src, land = refs[:n], refs[n:2 * n]
        send, recv = refs[2 * n], refs[2 * n + 1]
        token = refs[-1]
        x, y, c, chips = _chip_slots()
        for t in range(n):
            for j, chip in enumerate(chips):
                pltpu.make_async_remote_copy(
                    src_ref=src[t].at[j], dst_ref=land[t].at[j], send_sem=send.at[3 * t + j],
                    recv_sem=recv.at[3 * t + j], device_id=(*chip, c), device_id_type=MESH).start()
        token[...] = jnp.zeros_like(token)

    res = pl.pallas_call(
        body, name=name,
        out_shape=[pltpu.SemaphoreType.DMA((3 * n,)), pltpu.SemaphoreType.DMA((3 * n,))]
        + [pltpu.HBM(p.shape, p.dtype) for p in pairs] * 2 + [jax.ShapeDtypeStruct((8, LANES), F32)],
        in_specs=[HBM_SPEC] * (2 * n),
        out_specs=[SEM_SPEC, SEM_SPEC] + [HBM_SPEC] * (2 * n) + [pl.BlockSpec(memory_space=pltpu.VMEM)],
        input_output_aliases={t: 2 + t for t in range(2 * n)},
        compiler_params=pltpu.CompilerParams(has_side_effects=DATAFLOW),
    )(*[_in_hbm(p) for p in pairs], *[_in_hbm(lax.empty(p.shape, p.dtype)) for p in pairs])
    return res[0], res[1], res[2:2 + n], res[2 + n:2 + 2 * n], res[-1]


def chips_wait(started, after, *, name):
    send_sems, recv_sems, srcs, lands, _ = started
    n = len(srcs)

    def body(*refs):
        src, land = refs[:n], refs[n:2 * n]
        send, recv = refs[2 * n], refs[2 * n + 1]
        x, y, c, chips = _chip_slots()
        for t in range(n):
            for j, chip in enumerate(chips):
                cp = pltpu.make_async_remote_copy(
                    src_ref=src[t].at[j], dst_ref=land[t].at[j], send_sem=send.at[3 * t + j],
                    recv_sem=recv.at[3 * t + j], device_id=(*chip, c), device_id_type=MESH)
                cp.wait_send()
                cp.wait_recv()

    res = pl.pallas_call(
        body, name=name, out_shape=[pltpu.HBM(s.shape, s.dtype) for s in srcs] * 2,
        in_specs=[HBM_SPEC] * (2 * n) + [SEM_SPEC, SEM_SPEC, ANY_SPEC], out_specs=[HBM_SPEC] * (2 * n),
        input_output_aliases={t: t for t in range(2 * n)},
        compiler_params=pltpu.CompilerParams(has_side_effects=DATAFLOW),
    )(*srcs, *lands, send_sems, recv_sems, after)
    return res[n:]


def _sibling_copies(src, land, send, recv, n):
    x, y, c, _ = _chip_slots()
    return [pltpu.make_async_remote_copy(
        src_ref=src[t].at[4 * (q // 2) + 2 * (q % 2) + (1 - c)], dst_ref=land[t].at[q], send_sem=send.at[4 * t + q],
        recv_sem=recv.at[4 * t + q], device_id=(x, y, 1 - c), device_id_type=MESH)
        for t in range(n) for q in range(4)]


def sibling_start(gs, *, name):
    n = len(gs)

    def body(*refs):
        for cp in _sibling_copies(refs[:n], refs[n:2 * n], refs[2 * n], refs[2 * n + 1], n):
            cp.start()
        refs[-1][...] = jnp.zeros_like(refs[-1])

    lands = [lax.empty((4,) + g.shape[1:], g.dtype) for g in gs]
    res = pl.pallas_call(
        body, name=name,
        out_shape=[pltpu.SemaphoreType.DMA((4 * n,)), pltpu.SemaphoreType.DMA((4 * n,))]
        + [pltpu.HBM(g.shape, g.dtype) for g in gs] + [pltpu.HBM(l.shape, l.dtype) for l in lands]
        + [jax.ShapeDtypeStruct((8, LANES), F32)],
        in_specs=[HBM_SPEC] * (2 * n),
        out_specs=[SEM_SPEC, SEM_SPEC] + [HBM_SPEC] * (2 * n) + [pl.BlockSpec(memory_space=pltpu.VMEM)],
        input_output_aliases={t: 2 + t for t in range(2 * n)},
        compiler_params=pltpu.CompilerParams(has_side_effects=DATAFLOW),
    )(*[_in_hbm(g) for g in gs], *[_in_hbm(l) for l in lands])
    return res[0], res[1], res[2:2 + n], res[2 + n:2 + 2 * n], res[-1]


def sibling_wait(started, after, *, name):
    send_sems, recv_sems, srcs, lands, _ = started
    n = len(srcs)

    def body(*refs):
        for cp in _sibling_copies(refs[:n], refs[n:2 * n], refs[2 * n], refs[2 * n + 1], n):
            cp.wait_send()
            cp.wait_recv()

    res = pl.pallas_call(
        body, name=name,
        out_shape=[pltpu.HBM(s.shape, s.dtype) for s in srcs] + [pltpu.HBM(l.shape, l.dtype) for l in lands],
        in_specs=[HBM_SPEC] * (2 * n) + [SEM_SPEC, SEM_SPEC, ANY_SPEC], out_specs=[HBM_SPEC] * (2 * n),
        input_output_aliases={t: t for t in range(2 * n)},
        compiler_params=pltpu.CompilerParams(has_side_effects=DATAFLOW),
    )(*srcs, *lands, send_sems, recv_sems, after)
    return res[:n], res[n:]


def all_gather_vmem(x_shard, *, name, after=None):
    m_per, n = x_shard.shape
    n_after = 0 if after is None else 1

    def body(x_ref, *rest):
        out_ref, send_sems, recv_sems, local_sem = rest[n_after:]
        x, y, c, chips = _chip_slots()
        me, sibling = (x, y, c), (x, y, 1 - c)

        def rows(px, py, pc):
            return out_ref.at[pl.ds((4 * px + 2 * py + pc) * m_per, m_per), :]

        def copy(k, block, to, src=None):
            return pltpu.make_async_remote_copy(
                src_ref=rows(*block) if src is None else src, dst_ref=rows(*block),
                send_sem=send_sems.at[k], recv_sem=recv_sems.at[k], device_id=to, device_id_type=MESH)

        mine = pltpu.make_async_copy(x_ref, rows(*me), local_sem)
        mine.start()
        first = [copy(0, me, sibling, src=x_ref)]
        first += [copy(1 + j, me, (*chip, c), src=x_ref) for j, chip in enumerate(chips)]
        for cp in first:
            cp.start()
        passed = [copy(4 + j, (*chip, c), sibling) for j, chip in enumerate(chips)]
        for j, chip in enumerate(chips):
            copy(1 + j, (*chip, c), me).wait_recv()
            passed[j].start()
        copy(0, sibling, me).wait_recv()
        for j, chip in enumerate(chips):
            copy(4 + j, (*chip, 1 - c), me).wait_recv()
        for cp in first + passed:
            cp.wait_send()
        mine.wait()

    vmem = pl.BlockSpec(memory_space=pltpu.VMEM)
    return pl.pallas_call(
        body, name=name, out_shape=jax.ShapeDtypeStruct((N_DEV * m_per, n), x_shard.dtype),
        in_specs=[vmem] + [ANY_SPEC] * n_after, out_specs=vmem,
        scratch_shapes=[pltpu.SemaphoreType.DMA((7,)), pltpu.SemaphoreType.DMA((7,)), pltpu.SemaphoreType.DMA],
        compiler_params=pltpu.CompilerParams(vmem_limit_bytes=int(min(
            VMEM_LIMIT_CAP, 2 * (N_DEV + 1) * m_per * n * x_shard.dtype.itemsize + 16 * 2 ** 20))),
    )(x_shard, *([] if after is None else [after]))


def _rope_slab(cols):
    z = jnp.zeros(cols.shape[:-1] + (HALF_ROPE,), cols.dtype)
    return jnp.concatenate([cols[..., :HALF_ROPE], z, cols[..., HALF_ROPE:], z], axis=-1)


def _rope_unslab(slab):
    return jnp.concatenate([slab[..., :HALF_ROPE], slab[..., 2 * HALF_ROPE:3 * HALF_ROPE]], axis=-1)


def _pack_w_in_t(wt_g):
    s, c, d = wt_g.shape
    w = wt_g.reshape(s * c, d)
    c2, c3 = Q_LORA + KV_LORA, Q_LORA + KV_LORA + QK_ROPE
    r = w[c2:c3]
    z = jnp.zeros((HALF_ROPE, d), w.dtype)
    return jnp.concatenate([w[:c2], w[c3:], r[:HALF_ROPE], z, r[HALF_ROPE:], z], axis=0)


def _unpack_w_in_t_grad(dwt):
    d = dwt.shape[1]
    c2 = Q_LORA + KV_LORA
    uv = 2 * SGU_OUT
    slab = dwt[c2 + uv:]
    g = jnp.concatenate([dwt[:c2], slab[:HALF_ROPE], slab[2 * HALF_ROPE:3 * HALF_ROPE], dwt[c2:c2 + uv]], axis=0)
    return g.reshape(N_DEV, g.shape[0] // N_DEV, d)


def _rope_tables(positions):
    inv_freq = ROPE_BASE ** (-jnp.arange(0, QK_ROPE, 2, dtype=F32) / QK_ROPE)
    ang = positions.astype(F32)[:, None] * inv_freq
    cos, sin = jnp.cos(ang), jnp.sin(ang)
    z = jnp.zeros_like(cos)
    return jnp.concatenate([cos, z, cos, z], axis=-1), jnp.concatenate([-sin, z, sin, z], axis=-1)


def _mlp_up(x, gain, w1, tag):
    hn = rms_fwd(x, gain, name=f"mlp{tag}_norm")

    def act_epi(acc):
        a = jnp.maximum(acc, 0.0)
        return a, a * a

    T = x.shape[0]
    F = w1.shape[0] * w1.shape[2]
    a, act = mm(hn, w1, name=f"mlp{tag}_up", outs=[((T, F), BF, None), ((T, F), BF, None)], epi=act_epi)
    return hn, a, act


def _mlp_down(x, act, w2, tag):
    bm = _tile(x.shape[0], MM_TILE)
    bn = _tile(x.shape[1], MM_TILE)
    return mm(act, w2, name=f"mlp{tag}_down", out=(x.shape, F32), bm=bm, bn=bn,
              epi=lambda acc, r: (acc + r[...],), epi_ins=[(x, (bm, bn), lambda i, j, k: (i, j))])


def _mlp_bwd_weights(w1, w2, saved, dxb, tag):
    hn, a, act = saved
    T, D = dxb.shape
    F = a.shape[1]
    bm = _tile(T, MM_TILE)
    bn = _tile(F, min(MM_TILE, w1.shape[2]))
    dhid = mm(dxb, w2, tb=True, name=f"mlp{tag}_dhid", out=((T, F), BF), bm=bm, bn=bn,
              epi=lambda acc, a_ref: (2.0 * a_ref[...].astype(F32) * acc,),
              epi_ins=[(a, (bm, bn), lambda i, j, k: (i, j))])
    dw2 = mm(act, dxb, ta=True, name=f"mlp{tag}_dw2", out=((F, D), BF))
    dw1 = mm(hn, dhid, ta=True, name=f"mlp{tag}_dw1", out=(w1.shape, BF))
    return dhid, dw1, dw2.reshape(N_DEV, F // N_DEV, D)


def _reduce_begin(grads, tag):
    return sibling_start(grads, name=f"reduce_sibling_start_{tag}")


def _reduce_continue(sib, after, tag):
    grads, a_bufs = sibling_wait(sib, after, name=f"reduce_sibling_wait_{tag}")
    pairs = [pair_sum(g, a, name=f"pair_sum_{tag}{t}") for t, (g, a) in enumerate(zip(grads, a_bufs))]
    return grads, a_bufs, chips_start(pairs, name=f"reduce_chips_start_{tag}")


def _mlp_bwd_input(x_in, gain, w1, dhid, dx, tag, sib):
    dhn = mm(dhid, w1, tb=True, name=f"mlp{tag}_dhn", out=(x_in.shape, F32), deps=[sib[-1]])
    reduce_state = _reduce_continue(sib, dhn, f"r_mlp{tag}")
    return rms_bwd(x_in, gain, dhn, dres=dx, name=f"mlp{tag}_norm_bwd", deps=[reduce_state[2][-1]]), reduce_state


def kernel(x, positions, e_norm_mix, e_w_in, e_q_norm, e_w_uq, e_kv_norm, e_w_ukv, e_v_norm, e_sgu_w, e_sgu_b, e_mla_out_norm, e_sgu_out_norm, e_w_out, o_norm_mix, o_w_in, o_conv_w, o_w_out, mlp_norm, mlp_w1, mlp_w2, final_norm, loss_target, m_e_norm_mix, m_e_w_in, m_e_q_norm, m_e_w_uq, m_e_kv_norm, m_e_w_ukv, m_e_v_norm, m_e_sgu_w, m_e_sgu_b, m_e_mla_out_norm, m_e_sgu_out_norm, m_e_w_out, m_o_norm_mix, m_o_w_in, m_o_conv_w, m_o_w_out, m_mlp_norm, m_mlp_w1, m_mlp_w2, m_final_norm, v_e_norm_mix, v_e_w_in, v_e_q_norm, v_e_w_uq, v_e_kv_norm, v_e_w_ukv, v_e_v_norm, v_e_sgu_w, v_e_sgu_b, v_e_mla_out_norm, v_e_sgu_out_norm, v_e_w_out, v_o_norm_mix, v_o_w_in, v_o_conv_w, v_o_w_out, v_mlp_norm, v_mlp_w1, v_mlp_w2, v_final_norm):
    T, D = x.shape[1], x.shape[2]
    d_shard = o_norm_mix.shape[1]
    x0 = x[0]
    target = loss_target[0]
    me = 4 * lax.axis_index("x") + 2 * lax.axis_index("y") + lax.axis_index("c")

    bf = lambda s: s.astype(BF)
    gather_groups = [[bf(jnp.transpose(e_w_in[0])), bf(e_w_uq[0]), bf(e_w_ukv[0])], [bf(e_w_out[0]), bf(mlp_w1[0])],
                     [bf(mlp_w2[0]), bf(o_w_in[0])], [bf(o_w_out[0]), bf(mlp_w1[1])], [bf(mlp_w2[1])]]
    small_rows = jnp.concatenate([o_norm_mix, o_conv_w[0], jnp.zeros((4, d_shard), F32)], axis=0)
    small_flat = all_gather_vmem(small_rows, name="gather_small")
    started, start_token = gather_start(gather_groups[:1], small_flat, name="gather_start0")
    started_rest, rest_token = gather_start(gather_groups[1:], start_token, name="gather_start1")
    started += started_rest

    def gathered(gi, after):
        srcs, lands = gather_wait(started[gi], after, name=f"gather_wait{gi}")
        return gather_finish(srcs, lands, name=f"gather_finish{gi}")

    small_g = small_flat.reshape(N_DEV, 8, d_shard)
    o_norm_full = small_g[:, 0, :].reshape(1, D)
    conv_w_full = jnp.transpose(small_g[:, 1:4, :], (1, 0, 2)).reshape(3, D)
    w_tril = jnp.tril(e_sgu_w[0])
    w_tril_b = w_tril.astype(BF)
    w_tril_tb = jnp.swapaxes(w_tril, 1, 2).astype(BF)
    b_full = jnp.repeat(e_sgu_b[0].T, CH, axis=1)
    v_gain = e_v_norm[0].reshape(1, SGU_OUT)
    cos_t, sin_t = _rope_tables(positions[0])
    mlp_gain = [mlp_norm[0:1], mlp_norm[1:2]]
    final_gain = final_norm.reshape(1, D)

    h0 = rms_fwd(x0, e_norm_mix, name="e_norm", deps=[rest_token])
    g_w_in_t, g_w_uq, w_ukv = gathered(
        0, [h0, cos_t, sin_t, w_tril_b, w_tril_tb, b_full, o_norm_full, conv_w_full])
    w_in_t = _pack_w_in_t(g_w_in_t)
    w_uq = jnp.concatenate([g_w_uq[..., :QK_NOPE], _rope_slab(g_w_uq[..., QK_NOPE:])], axis=-1)
    proj = mm(h0, w_in_t, tb=True, name="e_in", out=((T, w_in_t.shape[0]), F32), bn=_tile(w_in_t.shape[0], 640))
    qn, kvn, krope = mla_prep(proj, e_q_norm, e_kv_norm, cos_t, sin_t, name="mla_prep")
    bm = _tile(T, MM_TILE)

    def q_epi(acc, cos_ref, sin_ref):
        return (jnp.concatenate([acc[:, :QK_NOPE], _rope_fwd(acc[:, QK_NOPE:], cos_ref[...], sin_ref[...])], axis=-1),)

    q = mm(qn, w_uq, name="mla_q", out=((T, HEADS * HEAD_PAD), BF), bm=bm, bn=HEAD_PAD, epi=q_epi,
           epi_ins=[(cos_t, (bm, LANES), lambda i, j, k: (i, 0)), (sin_t, (bm, LANES), lambda i, j, k: (i, 0))])

    def kv_epi(acc, kr_ref):
        return jnp.concatenate([acc[:, :QK_NOPE].astype(BF), kr_ref[...]], axis=-1), acc[:, QK_NOPE:]

    k, v = mm(kvn, w_ukv, name="mla_kv", bm=bm, bn=HEAD_PAD, epi=kv_epi,
              outs=[((T, HEADS * HEAD_PAD), BF, HEAD_PAD), ((T, MLA_OUT), BF, V_HEAD)],
              epi_ins=[(krope, (bm, LANES), lambda i, j, k: (i, 0))])
    attn, attn_lse = attn_fwd(q, k, v, name="attn_fwd")
    mixed = mix_fwd(attn, proj, e_mla_out_norm, e_sgu_out_norm, v_gain, w_tril_b, b_full, name="mix_fwd")
    bn = _tile(D, MM_TILE)
    g_w_out_e, w1_0 = gathered(1, [mixed])
    w_out_e = g_w_out_e.reshape(-1, D)
    x1 = mm(mixed, w_out_e, name="e_out", out=((T, D), F32), bm=bm, bn=bn,
            epi=lambda acc, r: (acc + r[...],), epi_ins=[(x0, (bm, bn), lambda i, j, k: (i, j))])
    hn0, a0, act0 = _mlp_up(x1, mlp_gain[0], w1_0, 0)
    g_w2_0, g_w_in_o = gathered(2, [act0])
    w2_0 = g_w2_0.reshape(-1, D)
    x2 = _mlp_down(x1, act0, w2_0, 0)
    ho = rms_fwd(x2, o_norm_full, name="o_norm")
    proj_o = mm(ho, g_w_in_o, name="o_in", out=((T, 3 * D), F32))
    gated = conv_fwd(proj_o, conv_w_full, name="conv_fwd")
    g_w_out_o, w1_1 = gathered(3, [gated])
    w_out_o = g_w_out_o.reshape(-1, D)
    x3 = mm(gated, w_out_o, name="o_out", out=((T, D), F32), bm=bm, bn=bn,
            epi=lambda acc, r: (acc + r[...],), epi_ins=[(x2, (bm, bn), lambda i, j, k: (i, j))])
    hn1, a1, act1 = _mlp_up(x3, mlp_gain[1], w1_1, 1)
    (g_w2_1,) = gathered(4, [act1])
    w2_1 = g_w2_1.reshape(-1, D)
    x4 = _mlp_down(x3, act1, w2_1, 1)
    w1, w2 = [w1_0, w1_1], [w2_0, w2_1]
    mlp0_saved, mlp1_saved = (hn0, a0, act0), (hn1, a1, act1)

    dx4, dx4b, d_final, loss_part = loss_bwd(x4, final_gain, target, name="loss_bwd")
    loss = lax.psum(loss_part[0, 0], AXES)

    dhid1, dw1_1, dw2_1 = _mlp_bwd_weights(w1[1], w2[1], mlp1_saved, dx4b, 1)
    sib_r0 = _reduce_begin([dw1_1, dw2_1], "r0")
    (dx3, dx3b, d_mlp1), (grads_r0, a_r0, st_r0) = _mlp_bwd_input(x3, mlp_gain[1], w1[1], dhid1, dx4, 1, sib_r0)

    dgated = mm(dx3b, w_out_o, tb=True, name="o_out_dx", out=((T, D), F32))
    dw_out_o = mm(gated, dx3b, ta=True, name="o_out_dw", out=((D, D), BF))
    dproj_o, dconv_full = conv_bwd(dgated, proj_o, conv_w_full, name="conv_bwd")
    dw_in_o = mm(ho, dproj_o, ta=True, name="o_in_dw", out=(g_w_in_o.shape, BF))
    sib_r1 = _reduce_begin([dw_out_o.reshape(g_w_out_o.shape), dw_in_o], "r1")
    dho = mm(dproj_o, g_w_in_o, tb=True, name="o_in_dx", out=((T, D), F32), deps=[sib_r1[-1]])
    grads_r1, a_r1, st_r1 = _reduce_continue(sib_r1, dho, "r1")
    dx2, dx2b, d_onorm_full = rms_bwd(x2, o_norm_full, dho, dres=dx3, name="o_norm_bwd", deps=[st_r1[-1]])

    dhid0, dw1_0, dw2_0 = _mlp_bwd_weights(w1[0], w2[0], mlp0_saved, dx2b, 0)
    sib_r2 = _reduce_begin([dw1_0, dw2_0], "r2")
    (dx1, dx1b, d_mlp0), (grads_r2, a_r2, st_r2) = _mlp_bwd_input(x1, mlp_gain[0], w1[0], dhid0, dx2, 0, sib_r2)
    b_r0 = chips_wait(st_r0, dx1b, name="reduce_chips_wait_r0")

    dmixed = mm(dx1b, w_out_e, tb=True, name="e_out_dx", out=((T, MLA_OUT + SGU_OUT), F32))
    dw_out_e = mm(mixed, dx1b, ta=True, name="e_out_dw", out=(w_out_e.shape, BF))
    (dattn, duv, d_mla_out, d_sgu_out, d_vgain, d_sgu_w, d_b_full) = mix_bwd(
        dmixed, attn, proj, e_mla_out_norm, e_sgu_out_norm, v_gain, w_tril_b, w_tril_tb, b_full, name="mix_bwd")
    b_r1 = chips_wait(st_r1, dattn, name="reduce_chips_wait_r1")
    dq, dk, dv = attn_bwd(q, k, v, attn, attn_lse, dattn, name="attn_bwd")
    dq_lin, dkv_lin, dkr = mla_bwd_prep(dq, dk, dv, cos_t, sin_t, name="mla_bwd_prep")
    dw_uq_pad = mm(qn, dq_lin, ta=True, name="mla_q_dw", out=(w_uq.shape, F32))
    dw_ukv = mm(kvn, dkv_lin, ta=True, name="mla_kv_dw", out=(w_ukv.shape, BF))
    dw_uq = jnp.concatenate([dw_uq_pad[..., :QK_NOPE], _rope_unslab(dw_uq_pad[..., QK_NOPE:])], axis=-1).astype(BF)
    sib_r2b = _reduce_begin([dw_out_e.reshape(g_w_out_e.shape), dw_uq, dw_ukv], "r2b")
    dqn = mm(dq_lin, w_uq, tb=True, name="mla_q_dx", out=((T, Q_LORA), F32), deps=[sib_r2b[-1]])
    dkvn = mm(dkv_lin, w_ukv, tb=True, name="mla_kv_dx", out=((T, KV_LORA), F32), deps=[sib_r2b[-1]])
    grads_r2b, a_r2b, st_r2b = _reduce_continue(sib_r2b, dkvn, "r2b")
    dcq, d_qnorm = rms_bwd(proj, e_q_norm, dqn, col_block=0, want_f32=False, name="q_norm_bwd", deps=[st_r2b[-1]])
    dckv, d_kvnorm = rms_bwd(proj, e_kv_norm, dkvn, col_block=1, want_f32=False, name="kv_norm_bwd")
    dproj = jnp.concatenate([dcq, dckv, duv, dkr], axis=-1)
    dw_in_t_pad = mm(dproj, h0, ta=True, name="e_in_dw", out=(w_in_t.shape, F32), bm=_tile(w_in_t.shape[0], 640))
    dw_in_t = _unpack_w_in_t_grad(dw_in_t_pad).astype(BF)
    sib_r3 = _reduce_begin([dw_in_t], "r3")
    dh0 = mm(dproj, w_in_t, name="e_in_dx", out=((T, D), F32), deps=[sib_r3[-1]])
    grads_r3, a_r3, st_r3 = _reduce_continue(sib_r3, dh0, "r3")
    tok_r3 = st_r3[-1]
    grad_x, d_enorm = rms_bwd(x0, e_norm_mix, dh0, dres=dx1, want_bf=False, name="e_norm_bwd", deps=[tok_r3])
    b_r2 = chips_wait(st_r2, grad_x, name="reduce_chips_wait_r2")

    def finish(grads, a_bufs, b_bufs, t, w, m, v, layer=0, prev=None, tag="", deps=()):
        return reduce_adam(grads[t], a_bufs[t], b_bufs[t], w, m, v, layer, prev, name=f"adam_{tag}", deps=deps)

    r_w1 = finish(grads_r0, a_r0, b_r0, 0, mlp_w1, m_mlp_w1, v_mlp_w1, 1, None, tag="w1_l1")
    r_w2 = finish(grads_r0, a_r0, b_r0, 1, mlp_w2, m_mlp_w2, v_mlp_w2, 1, None, tag="w2_l1")
    r_w_out_o = finish(grads_r1, a_r1, b_r1, 0, o_w_out, m_o_w_out, v_o_w_out, tag="o_w_out")
    r_w_in_o = finish(grads_r1, a_r1, b_r1, 1, o_w_in, m_o_w_in, v_o_w_in, tag="o_w_in")
    r_w1 = finish(grads_r2, a_r2, b_r2, 0, mlp_w1, m_mlp_w1, v_mlp_w1, 0, r_w1, tag="w1_l0", deps=[tok_r3])
    r_w2 = finish(grads_r2, a_r2, b_r2, 1, mlp_w2, m_mlp_w2, v_mlp_w2, 0, r_w2, tag="w2_l0", deps=[r_w1[1]])
    b_r2b = chips_wait(st_r2b, r_w2[1], name="reduce_chips_wait_r2b")
    r_w_out_e = finish(grads_r2b, a_r2b, b_r2b, 0, e_w_out, m_e_w_out, v_e_w_out, tag="e_w_out")
    r_w_uq = finish(grads_r2b, a_r2b, b_r2b, 1, e_w_uq, m_e_w_uq, v_e_w_uq, tag="e_w_uq")
    r_w_ukv = finish(grads_r2b, a_r2b, b_r2b, 2, e_w_ukv, m_e_w_ukv, v_e_w_ukv, tag="e_w_ukv")
    b_r3 = chips_wait(st_r3, r_w_out_e[1], name="reduce_chips_wait_r3")
    g_w_in = jnp.transpose(reduce_sum(grads_r3[0], a_r3[0], b_r3[0], name="sum_e_w_in"))
    r_w_in = [t[None] for t in (g_w_in, *adam_rows(g_w_in, e_w_in[0], m_e_w_in[0], v_e_w_in[0], name="adam_e_w_in"))]

    d_sgu_b = jnp.transpose(d_b_full[:, ::CH])
    d_sgu_w_tril = jnp.tril(d_sgu_w)
    rep = [("e_norm_mix", e_norm_mix, m_e_norm_mix, v_e_norm_mix, d_enorm),
           ("e_q_norm", e_q_norm, m_e_q_norm, v_e_q_norm, d_qnorm),
           ("e_kv_norm", e_kv_norm, m_e_kv_norm, v_e_kv_norm, d_kvnorm),
           ("e_v_norm", e_v_norm, m_e_v_norm, v_e_v_norm, d_vgain),
           ("e_sgu_w", e_sgu_w, m_e_sgu_w, v_e_sgu_w, d_sgu_w_tril),
           ("e_sgu_b", e_sgu_b, m_e_sgu_b, v_e_sgu_b, d_sgu_b),
           ("e_mla_out_norm", e_mla_out_norm, m_e_mla_out_norm, v_e_mla_out_norm, d_mla_out),
           ("e_sgu_out_norm", e_sgu_out_norm, m_e_sgu_out_norm, v_e_sgu_out_norm, d_sgu_out),
           ("mlp_norm", mlp_norm, m_mlp_norm, v_mlp_norm, jnp.concatenate([d_mlp0, d_mlp1], axis=0)),
           ("final_norm", final_norm, m_final_norm, v_final_norm, d_final)]
    sizes = [int(np.prod(r[1].shape)) for r in rep]
    n_rep = sum(sizes)
    n_all = n_rep + 4 * D
    width = -(-n_all // (8 * LANES)) * LANES
    pad = 8 * width - n_all
    flat = jnp.concatenate([r[4].reshape(-1) for r in rep]
                           + [d_onorm_full.reshape(-1), dconv_full.reshape(-1), jnp.zeros((pad,), F32)])
    summed = sum_rows8(all_gather_vmem(flat.reshape(8, width), name="gather_small_grads", after=b_r3[0]), 8,
                       name="sum_small_grads").reshape(-1)

    def pack_rep(i):
        return jnp.concatenate([r[i].reshape(-1) for r in rep]).reshape(n_rep // LANES, LANES)

    g_rep = summed[:n_rep].reshape(n_rep // LANES, LANES)
    d_rep, nm_rep, nv_rep = adam_flat(g_rep, pack_rep(1), pack_rep(2), pack_rep(3), name="adam_replicated")

    def unpack_rep(flat2d):
        out, off = {}, 0
        f = flat2d.reshape(-1)
        for r, n in zip(rep, sizes):
            out[r[0]] = f[off:off + n].reshape(r[1].shape)
            off += n
        return out

    small = {"grad": unpack_rep(g_rep), "delta": unpack_rep(d_rep), "new_m": unpack_rep(nm_rep),
             "new_v": unpack_rep(nv_rep)}
    g_onorm = lax.dynamic_slice(summed[n_rep:n_rep + D].reshape(1, D), (0, me * d_shard), (1, d_shard))
    g_conv = lax.dynamic_slice(summed[n_rep + D:n_rep + 4 * D].reshape(3, D), (0, me * d_shard), (3, d_shard))

    def pack_sharded(norm_part, conv_part):
        return jnp.concatenate([norm_part, conv_part, jnp.zeros((4, d_shard), F32)], axis=0)

    g_sh = pack_sharded(g_onorm, g_conv)
    d_sh, nm_sh, nv_sh = adam_flat(g_sh, pack_sharded(o_norm_mix, o_conv_w[0]), pack_sharded(m_o_norm_mix, m_o_conv_w[0]),
                                   pack_sharded(v_o_norm_mix, v_o_conv_w[0]), name="adam_sharded_small")
    for kind, arr in (("grad", g_sh), ("delta", d_sh), ("new_m", nm_sh), ("new_v", nv_sh)):
        small[kind]["o_norm_mix"] = arr[0:1]
        small[kind]["o_conv_w"] = arr[1:4][None]

    big = {"e_w_in": r_w_in, "e_w_uq": r_w_uq, "e_w_ukv": r_w_ukv, "e_w_out": r_w_out_e, "o_w_in": r_w_in_o,
           "o_w_out": r_w_out_o, "mlp_w1": r_w1, "mlp_w2": r_w2}
    order = ["e_norm_mix", "e_w_in", "e_q_norm", "e_w_uq", "e_kv_norm", "e_w_ukv", "e_v_norm", "e_sgu_w", "e_sgu_b",
             "e_mla_out_norm", "e_sgu_out_norm", "e_w_out", "o_norm_mix", "o_w_in", "o_conv_w", "o_w_out", "mlp_norm",
             "mlp_w1", "mlp_w2", "final_norm"]
    result = [loss, grad_x[None]]
    for ki, kind in enumerate(("grad", "delta", "new_m", "new_v")):
        for nm in order:
            result.append(big[nm][ki] if nm in big else small[kind][nm])
    return tuple(result)
```

```python
import functools

import numpy as np
import jax
import jax.numpy as jnp
from jax import lax
from jax.experimental import pallas as pl
from jax.experimental.pallas import tpu as pltpu

BF = jnp.bfloat16
F32 = jnp.float32
MESH = pl.DeviceIdType.MESH
AXES = ("x", "y", "c")
N_DEV = 8

EPS = 1e-6
HEADS = 8
Q_LORA = 512
KV_LORA = 512
QK_NOPE = 128
QK_ROPE = 64
HALF_ROPE = QK_ROPE // 2
V_HEAD = 128
HEAD_PAD = 256
ROPE_BASE = 10000.0
GROUPS = 8
CH = 128
CHUNK = 128
SGU_OUT = GROUPS * CH
MLA_OUT = HEADS * V_HEAD
ATTN_SCALE = float((QK_NOPE + QK_ROPE) ** -0.5)

ADAM_LR = 0.001
ADAM_B1 = 0.9
ADAM_B2 = 0.999
ADAM_EPS = 1e-08
ADAM_WD = 0.01
ADAM_STEP = 10
ADAM_C1 = 1.0 - ADAM_B1 ** ADAM_STEP
ADAM_C2 = 1.0 - ADAM_B2 ** ADAM_STEP

V7X_VMEM_BYTES = 64 * 2 ** 20
VMEM_LIMIT_CAP = V7X_VMEM_BYTES - 6 * 2 ** 20
LANES = 128
ROW_TILE = 256
ATTN_TILE = 512
STREAM_BLOCK_ELEMS = 512 * 1024
MM_TILE = 1024
MM_K_TILE = 2048
MM_K_BLOCK_MAX = 3072


def _padded_bytes(block, dtype):
    dims = [d for d in block if d is not None]
    if len(dims) >= 1:
        dims[-1] = -(-dims[-1] // LANES) * LANES
    if len(dims) >= 2:
        dims[-2] = -(-dims[-2] // 16) * 16
    return int(np.prod(dims)) * jnp.dtype(dtype).itemsize


def _pcall(body, *, name, grid, ins, outs, scratch=(), semantics=None, aliases=None, prefetch=None, deps=()):
    any_spec = pl.BlockSpec(memory_space=pl.ANY)
    if deps:
        n_lead = len(ins) + (1 if prefetch is not None else 0)
        n_deps = len(deps)
        inner = body

        def body(*refs):
            inner(*refs[:n_lead], *refs[n_lead + n_deps:])

        ins = list(ins) + [(d, None, None) for d in deps]
    in_specs = [any_spec if b is None else pl.BlockSpec(b, m) for _, b, m in ins]
    out_specs = [any_spec if b is None else pl.BlockSpec(b, m) for _, _, b, m in outs]
    out_shape = [pltpu.HBM(s, d) for s, d, _, _ in outs]
    est = 0
    for a, b, _ in ins:
        if b is not None:
            est += 2 * _padded_bytes(b, a.dtype)
    for _, d, b, _ in outs:
        if b is not None:
            est += 2 * _padded_bytes(b, d)
    for s in scratch:
        if hasattr(s, "shape") and hasattr(s, "dtype"):
            est += _padded_bytes(s.shape, s.dtype)
    limit = int(min(VMEM_LIMIT_CAP, est + 16 * 2 ** 20))
    params = pltpu.CompilerParams(
        dimension_semantics=semantics or ("arbitrary",) * len(grid), vmem_limit_bytes=limit)
    args = [pltpu.with_memory_space_constraint(a, pltpu.HBM) for a, _, _ in ins]
    if prefetch is not None:
        grid_spec = pltpu.PrefetchScalarGridSpec(
            num_scalar_prefetch=1, grid=grid, in_specs=in_specs, out_specs=out_specs, scratch_shapes=list(scratch))
        call = pl.pallas_call(body, out_shape=out_shape, grid_spec=grid_spec, name=name, compiler_params=params,
                              input_output_aliases=aliases or {})
        return call(prefetch, *args)
    call = pl.pallas_call(body, out_shape=out_shape, grid=grid, in_specs=in_specs, out_specs=out_specs,
                          scratch_shapes=list(scratch), name=name, compiler_params=params,
                          input_output_aliases=aliases or {})
    return call(*args)


def _tile(dim, pref, quantum=LANES):
    if dim <= pref:
        return dim
    t = (pref // quantum) * quantum
    while t >= quantum:
        if dim % t == 0:
            return t
        t -= quantum
    return dim


def _vshape(arr_shape):
    if len(arr_shape) == 2:
        return tuple(arr_shape)
    s, r, c = arr_shape
    return (r, s * c)


def _vblock(arr_shape, br, bc, rc):
    if len(arr_shape) == 2:
        return (br, bc), (lambda *g: rc(*g))
    _, _, c = arr_shape
    assert c % bc == 0, (arr_shape, bc)
    per = c // bc

    def imap(*g):
        ri, ci = rc(*g)
        return (ci // per, ri, ci % per)

    return (None, br, bc), imap


def _shard_width(*shapes):
    w = None
    for s in shapes:
        if len(s) == 3:
            w = s[2] if w is None else int(np.gcd(w, s[2]))
    return w


def mm(a, b, *, name, ta=False, tb=False, out=None, outs=None, epi=None, epi_ins=(), bm=None, bn=None, bk=None,
       deps=()):
    av, bv = _vshape(a.shape), _vshape(b.shape)
    M, K = (av[1], av[0]) if ta else av
    K2, N = (bv[1], bv[0]) if tb else bv
    assert K == K2, (a.shape, b.shape, ta, tb)
    if outs is None:
        outs = [(out[0], out[1], None)]
    a_sw = _shard_width(a.shape)
    b_sw = _shard_width(b.shape)
    o_sw = _shard_width(*[o[0] for o in outs])
    m_lim = a_sw if (ta and a_sw) else None
    k_lim = [w for w in ((a_sw if not ta else None), (b_sw if tb else None)) if w]
    n_lim = [w for w in ((b_sw if not tb else None), o_sw) if w]
    if bm is None:
        bm = _tile(M, min([MM_TILE] + ([m_lim] if m_lim else [])))
    if bn is None:
        bn = _tile(N, min([MM_TILE] + n_lim))
    k_shards = 0
    if tb and len(b.shape) == 3 and bk is None and not (a_sw and not ta):
        k_shards = 1
        while 2 * k_shards <= b.shape[0] and 2 * k_shards * b_sw <= MM_K_BLOCK_MAX:
            k_shards *= 2
        bk = k_shards * b_sw
    if bk is None:
        bk = K if (K <= 4096 and not k_lim) else _tile(K, min([MM_K_TILE] + k_lim))
    assert M % bm == 0 and N % bn == 0 and K % bk == 0, (name, M, N, K, bm, bn, bk)
    nk = K // bk
    grid = (M // bm, N // bn, nk)
    if ta:
        a_blk, a_map = _vblock(a.shape, bk, bm, lambda i, j, k: (k, i))
    else:
        a_blk, a_map = _vblock(a.shape, bm, bk, lambda i, j, k: (i, k))
    if k_shards:
        b_blk, b_map = (k_shards, bn, b_sw), (lambda i, j, k: (k, j, 0))
    elif tb:
        b_blk, b_map = _vblock(b.shape, bn, bk, lambda i, j, k: (j, k))
    else:
        b_blk, b_map = _vblock(b.shape, bk, bn, lambda i, j, k: (k, j))
    dn = (((0 if ta else 1,), (1 if tb else 0,)), ((), ()))
    ins = [(a, a_blk, a_map), (b, b_blk, b_map)] + list(epi_ins)
    out_list = []
    for shape, dtype, cols in outs:
        cols = cols or bn
        blk, imap = _vblock(shape, bm, cols, lambda i, j, k: (i, j))
        out_list.append((shape, dtype, blk, imap))
    n_e, n_o = len(epi_ins), len(out_list)

    def body(*refs):
        a_ref, b_ref = refs[0], refs[1]
        e_refs = refs[2:2 + n_e]
        o_refs = refs[2 + n_e:2 + n_e + n_o]

        def finish(acc):
            res = epi(acc, *e_refs) if epi is not None else (acc,)
            for o_ref, r in zip(o_refs, res):
                o_ref[...] = r.astype(o_ref.dtype)

        x = a_ref[...].astype(BF)
        y = b_ref[...].astype(BF)
        if k_shards:
            p = None
            for s in range(k_shards):
                part = lax.dot_general(x[:, s * b_sw:(s + 1) * b_sw], y[s], dn, preferred_element_type=F32)
                p = part if p is None else p + part
        else:
            p = lax.dot_general(x, y, dn, preferred_element_type=F32)
        if nk == 1:
            finish(p)
        else:
            acc_ref = refs[-1]
            k = pl.program_id(2)

            @pl.when(k == 0)
            def _():
                acc_ref[...] = p

            @pl.when(k > 0)
            def _():
                acc_ref[...] += p

            @pl.when(k == nk - 1)
            def _():
                finish(acc_ref[...])

    scratch = [pltpu.VMEM((bm, bn), F32)] if nk > 1 else []
    res = _pcall(body, name=name, grid=grid, ins=ins, outs=out_list, scratch=scratch,
                 semantics=("parallel", "parallel", "arbitrary"), deps=deps)
    return res[0] if len(res) == 1 else res


_GELU_K = float(np.sqrt(2.0 / np.pi))
_GELU_C = 0.044715


def _gelu(x):
    t = jnp.tanh(_GELU_K * (x + _GELU_C * (x * x * x)))
    return 0.5 * x * (1.0 + t)


def _gelu_grad(x):
    t = jnp.tanh(_GELU_K * (x + _GELU_C * (x * x * x)))
    return 0.5 * (1.0 + t) + 0.5 * x * (1.0 - t * t) * (_GELU_K * (1.0 + 3.0 * _GELU_C * (x * x)))


def _rstd(x):
    return lax.rsqrt(jnp.mean(x * x, axis=-1, keepdims=True) + EPS)


def _rms_bwd(x, gain, dy):
    r = _rstd(x)
    xh = x * r
    gdy = dy * gain
    dx = r * (gdy - xh * jnp.mean(gdy * xh, axis=-1, keepdims=True))
    return dx, dy * xh


def _rope_fwd(x, cos_t, sin_t):
    return x * cos_t + pltpu.roll(x, 2 * HALF_ROPE, 1) * sin_t


def _rope_bwd(dy, cos_t, sin_t):
    return dy * cos_t + pltpu.roll(dy * sin_t, 2 * HALF_ROPE, 1)


def _acc_rows(ref, val, first):
    s = jnp.sum(val, axis=0, keepdims=True)

    @pl.when(first)
    def _():
        ref[...] = s

    @pl.when(jnp.logical_not(first))
    def _():
        ref[...] += s


def rms_fwd(x, gain, *, name, col_block=0, width=None, deps=()):
    T = x.shape[0]
    width = width or x.shape[1]
    tm = _tile(T, ROW_TILE, 8)

    def body(x_ref, g_ref, o_ref):
        v = x_ref[...]
        o_ref[...] = (v * _rstd(v) * g_ref[...]).astype(BF)

    return _pcall(body, name=name, grid=(T // tm,),
                  ins=[(x, (tm, width), lambda i: (i, col_block)), (gain, (1, width), lambda i: (0, 0))],
                  outs=[((T, width), BF, (tm, width), lambda i: (i, 0))], semantics=("parallel",), deps=deps)[0]


def rms_bwd(x, gain, dy, *, name, col_block=0, dres=None, want_f32=True, want_bf=True, deps=()):
    T, width = dy.shape
    tm = _tile(T, ROW_TILE, 8)
    has_res = dres is not None

    def body(*refs):
        x_ref, g_ref, dy_ref = refs[:3]
        pos = 3
        res_ref = None
        if has_res:
            res_ref = refs[pos]
            pos += 1
        outs = refs[pos:]
        dx, dg_rows = _rms_bwd(x_ref[...], g_ref[...], dy_ref[...])
        if has_res:
            dx = dx + res_ref[...]
        o = 0
        if want_f32:
            outs[o][...] = dx
            o += 1
        if want_bf:
            outs[o][...] = dx.astype(BF)
            o += 1
        _acc_rows(outs[o], dg_rows, pl.program_id(0) == 0)

    ins = [(x, (tm, width), lambda i: (i, col_block)), (gain, (1, width), lambda i: (0, 0)),
           (dy, (tm, width), lambda i: (i, 0))]
    if has_res:
        ins.append((dres, (tm, width), lambda i: (i, 0)))
    outs = []
    if want_f32:
        outs.append(((T, width), F32, (tm, width), lambda i: (i, 0)))
    if want_bf:
        outs.append(((T, width), BF, (tm, width), lambda i: (i, 0)))
    outs.append(((1, width), F32, (1, width), lambda i: (0, 0)))
    return _pcall(body, name=name, grid=(T // tm,), ins=ins, outs=outs, deps=deps)


def mla_prep(proj, q_norm, kv_norm, cos_t, sin_t, *, name):
    T = proj.shape[0]
    tm = _tile(T, ROW_TILE, 8)
    kr_block = (proj.shape[1] - LANES) // LANES

    def body(cq_ref, ckv_ref, kr_ref, qg_ref, kg_ref, cos_ref, sin_ref, qn_ref, kvn_ref, krope_ref):
        cq = cq_ref[...]
        qn_ref[...] = (cq * _rstd(cq) * qg_ref[...]).astype(BF)
        ckv = ckv_ref[...]
        kvn_ref[...] = (ckv * _rstd(ckv) * kg_ref[...]).astype(BF)
        krope_ref[...] = _rope_fwd(kr_ref[...], cos_ref[...], sin_ref[...]).astype(BF)

    return _pcall(
        body, name=name, grid=(T // tm,),
        ins=[(proj, (tm, Q_LORA), lambda i: (i, 0)), (proj, (tm, KV_LORA), lambda i: (i, 1)),
             (proj, (tm, LANES), lambda i: (i, kr_block)),
             (q_norm, (1, Q_LORA), lambda i: (0, 0)), (kv_norm, (1, KV_LORA), lambda i: (0, 0)),
             (cos_t, (tm, LANES), lambda i: (i, 0)), (sin_t, (tm, LANES), lambda i: (i, 0))],
        outs=[((T, Q_LORA), BF, (tm, Q_LORA), lambda i: (i, 0)), ((T, KV_LORA), BF, (tm, KV_LORA), lambda i: (i, 0)),
              ((T, LANES), BF, (tm, LANES), lambda i: (i, 0))],
        semantics=("parallel",))


def _attn_scores(q, k_blk, diagonal):
    s = lax.dot_general(q, k_blk, (((1,), (1,)), ((), ())), preferred_element_type=F32) * ATTN_SCALE
    if diagonal:
        row = lax.broadcasted_iota(jnp.int32, s.shape, 0)
        col = lax.broadcasted_iota(jnp.int32, s.shape, 1)
        s = jnp.where(col <= row, s, -jnp.inf)
    return s


def attn_fwd(q, k, v, *, name):
    T = q.shape[0]
    tq = _tile(T, ATTN_TILE, 8)

    def body(q_ref, k_ref, v_ref, o_ref, lse_ref):
        i = pl.program_id(1)
        qv = q_ref[...]

        def block(kb, carry, diagonal):
            m, l, acc = carry
            start = pl.multiple_of(kb * tq, tq)
            s = _attn_scores(qv, k_ref[pl.ds(start, tq), :], diagonal)
            m_new = jnp.maximum(m, jnp.max(s, axis=-1, keepdims=True))
            alpha = jnp.exp(m - m_new)
            p = jnp.exp(s - m_new)
            l = alpha * l + jnp.sum(p, axis=-1, keepdims=True)
            acc = alpha * acc + jnp.dot(p.astype(BF), v_ref[pl.ds(start, tq), :], preferred_element_type=F32)
            return m_new, l, acc

        init = (jnp.full((tq, 1), -jnp.inf, F32), jnp.zeros((tq, 1), F32), jnp.zeros((tq, V_HEAD), F32))
        carry = lax.fori_loop(0, i, lambda kb, c: block(kb, c, False), init)
        m, l, acc = block(i, carry, True)
        o_ref[...] = acc / l
        lse_ref[...] = jnp.broadcast_to(m + jnp.log(l), (tq, V_HEAD))

    return _pcall(
        body, name=name, grid=(HEADS, T // tq),
        ins=[(q, (tq, HEAD_PAD), lambda h, i: (i, h)), (k, (T, HEAD_PAD), lambda h, i: (0, h)),
             (v, (T, V_HEAD), lambda h, i: (0, h))],
        outs=[((T, MLA_OUT), F32, (tq, V_HEAD), lambda h, i: (i, h)),
              ((T, MLA_OUT), F32, (tq, V_HEAD), lambda h, i: (i, h))], semantics=("parallel", "parallel"))


def attn_bwd(q, k, v, o, lse, do, *, name):
    T = q.shape[0]
    tq = _tile(T, ATTN_TILE, 8)

    def body(q_ref, k_ref, v_ref, o_ref, lse_ref, do_ref, dq_ref, dk_ref, dv_ref):
        i = pl.program_id(1)

        @pl.when(i == 0)
        def _():
            dk_ref[...] = jnp.zeros_like(dk_ref)
            dv_ref[...] = jnp.zeros_like(dv_ref)

        qv = q_ref[...]
        do_t = do_ref[...]
        lse_v = lse_ref[:, 0:1]
        delta = jnp.sum(do_t.astype(F32) * o_ref[...], axis=-1, keepdims=True)

        def block(kb, dq, diagonal):
            start = pl.multiple_of(kb * tq, tq)
            k_blk = k_ref[pl.ds(start, tq), :]
            v_blk = v_ref[pl.ds(start, tq), :]
            p = jnp.exp(_attn_scores(qv, k_blk, diagonal) - lse_v)
            dp = lax.dot_general(do_t, v_blk, (((1,), (1,)), ((), ())), preferred_element_type=F32)
            ds = (p * (dp - delta) * ATTN_SCALE).astype(BF)
            dk_ref[pl.ds(start, tq), :] += lax.dot_general(ds, qv, (((0,), (0,)), ((), ())), preferred_element_type=F32)
            dv_ref[pl.ds(start, tq), :] += lax.dot_general(p.astype(BF), do_t, (((0,), (0,)), ((), ())),
                                                          preferred_element_type=F32)
            return dq + jnp.dot(ds, k_blk, preferred_element_type=F32)

        dq = lax.fori_loop(0, i, lambda kb, c: block(kb, c, False), jnp.zeros((tq, HEAD_PAD), F32))
        dq_ref[...] = block(i, dq, True)

    return _pcall(
        body, name=name, grid=(HEADS, T // tq),
        ins=[(q, (tq, HEAD_PAD), lambda h, i: (i, h)), (k, (T, HEAD_PAD), lambda h, i: (0, h)),
             (v, (T, V_HEAD), lambda h, i: (0, h)), (o, (tq, V_HEAD), lambda h, i: (i, h)),
             (lse, (tq, V_HEAD), lambda h, i: (i, h)), (do, (tq, V_HEAD), lambda h, i: (i, h))],
        outs=[((T, HEADS * HEAD_PAD), F32, (tq, HEAD_PAD), lambda h, i: (i, h)),
              ((T, HEADS * HEAD_PAD), F32, (T, HEAD_PAD), lambda h, i: (0, h)),
              ((T, MLA_OUT), F32, (T, V_HEAD), lambda h, i: (0, h))],
        semantics=("parallel", "arbitrary"))


def mla_bwd_prep(dq, dk, dv, cos_t, sin_t, *, name):
    T = dq.shape[0]
    tm = _tile(T, ROW_TILE, 8)

    def body(dq_ref, dk_ref, dv_ref, cos_ref, sin_ref, dql_ref, dkvl_ref, dkr_ref):
        cos_v, sin_v = cos_ref[...], sin_ref[...]
        kr = jnp.zeros((tm, LANES), F32)
        for h in range(HEADS):
            lo = h * HEAD_PAD
            dql_ref[:, lo:lo + QK_NOPE] = dq_ref[:, lo:lo + QK_NOPE].astype(BF)
            dql_ref[:, lo + QK_NOPE:lo + HEAD_PAD] = _rope_bwd(
                dq_ref[:, lo + QK_NOPE:lo + HEAD_PAD], cos_v, sin_v).astype(BF)
            dkvl_ref[:, lo:lo + QK_NOPE] = dk_ref[:, lo:lo + QK_NOPE].astype(BF)
            dkvl_ref[:, lo + QK_NOPE:lo + HEAD_PAD] = dv_ref[:, h * V_HEAD:(h + 1) * V_HEAD].astype(BF)
            kr = kr + dk_ref[:, lo + QK_NOPE:lo + HEAD_PAD]
        dkr_ref[...] = _rope_bwd(kr, cos_v, sin_v).astype(BF)

    W = HEADS * HEAD_PAD
    return _pcall(
        body, name=name, grid=(T // tm,),
        ins=[(dq, (tm, W), lambda i: (i, 0)), (dk, (tm, W), lambda i: (i, 0)), (dv, (tm, MLA_OUT), lambda i: (i, 0)),
             (cos_t, (tm, LANES), lambda i: (i, 0)), (sin_t, (tm, LANES), lambda i: (i, 0))],
        outs=[((T, W), BF, (tm, W), lambda i: (i, 0)), ((T, W), BF, (tm, W), lambda i: (i, 0)),
              ((T, LANES), BF, (tm, LANES), lambda i: (i, 0))],
        semantics=("parallel",))


def _group_norm_stats(vg):
    mu = jnp.mean(vg, axis=-1, keepdims=True)
    d = vg - mu
    r = lax.rsqrt(jnp.mean(d * d, axis=-1, keepdims=True) + EPS)
    return d * r, r


def mix_fwd(a, proj, g_mla, g_sgu, v_gain, w_tril, b_full, *, name):
    T = a.shape[0]
    tm = _tile(T, ROW_TILE, CHUNK)
    n_chunk = tm // CHUNK

    def body(a_ref, u_ref, v_ref, gm_ref, gs_ref, vg_ref, w_ref, b_ref, o_ref, s_scr):
        av = a_ref[...]
        o_ref[:, :MLA_OUT] = (av * _rstd(av) * gm_ref[...]).astype(BF)
        for g in range(GROUPS):
            sl = slice(g * CH, (g + 1) * CH)
            vhat, _ = _group_norm_stats(_gelu(v_ref[:, sl]))
            vn = (vhat * vg_ref[:, sl]).astype(BF)
            u = _gelu(u_ref[:, sl])
            for ci in range(n_chunk):
                rs = slice(ci * CHUNK, (ci + 1) * CHUNK)
                y = jnp.dot(w_ref[g], vn[rs], preferred_element_type=F32) + b_ref[:, sl]
                s_scr[rs, sl] = u[rs] * y
        s = s_scr[...]
        o_ref[:, MLA_OUT:] = (s * _rstd(s) * gs_ref[...]).astype(BF)

    return _pcall(
        body, name=name, grid=(T // tm,),
        ins=[(a, (tm, MLA_OUT), lambda i: (i, 0)), (proj, (tm, SGU_OUT), lambda i: (i, 1)),
             (proj, (tm, SGU_OUT), lambda i: (i, 2)), (g_mla, (1, MLA_OUT), lambda i: (0, 0)),
             (g_sgu, (1, SGU_OUT), lambda i: (0, 0)), (v_gain, (1, SGU_OUT), lambda i: (0, 0)),
             (w_tril, (GROUPS, CHUNK, CHUNK), lambda i: (0, 0, 0)), (b_full, (CHUNK, SGU_OUT), lambda i: (0, 0))],
        outs=[((T, MLA_OUT + SGU_OUT), BF, (tm, MLA_OUT + SGU_OUT), lambda i: (i, 0))],
        scratch=[pltpu.VMEM((tm, SGU_OUT), F32)], semantics=("parallel",))[0]


def mix_bwd(dmixed, a, proj, g_mla, g_sgu, v_gain, w_tril, w_tril_t, b_full, *, name):
    T = a.shape[0]
    tm = _tile(T, ROW_TILE, CHUNK)
    n_chunk = tm // CHUNK

    def body(dm_a_ref, dm_s_ref, a_ref, u_ref, v_ref, gm_ref, gs_ref, vg_ref, w_ref, wt_ref, b_ref,
             da_ref, duv_ref, dgm_ref, dgs_ref, dvg_ref, dw_ref, db_ref, s_scr, y_scr):
        first = pl.program_id(0) == 0
        da, dgm_rows = _rms_bwd(a_ref[...], gm_ref[...], dm_a_ref[...])
        da_ref[...] = da.astype(BF)
        _acc_rows(dgm_ref, dgm_rows, first)

        for g in range(GROUPS):
            sl = slice(g * CH, (g + 1) * CH)
            vhat, _ = _group_norm_stats(_gelu(v_ref[:, sl]))
            vn = (vhat * vg_ref[:, sl]).astype(BF)
            u = _gelu(u_ref[:, sl])
            for ci in range(n_chunk):
                rs = slice(ci * CHUNK, (ci + 1) * CHUNK)
                y = jnp.dot(w_ref[g], vn[rs], preferred_element_type=F32) + b_ref[:, sl]
                y_scr[rs, sl] = y
                s_scr[rs, sl] = u[rs] * y
        ds, dgs_rows = _rms_bwd(s_scr[...], gs_ref[...], dm_s_ref[...])
        _acc_rows(dgs_ref, dgs_rows, first)
        s_scr[...] = ds

        @pl.when(first)
        def _():
            dw_ref[...] = jnp.zeros_like(dw_ref)
            db_ref[...] = jnp.zeros_like(db_ref)

        for g in range(GROUPS):
            sl = slice(g * CH, (g + 1) * CH)
            upre = u_ref[:, sl]
            vpre = v_ref[:, sl]
            u = _gelu(upre)
            vhat, r = _group_norm_stats(_gelu(vpre))
            gain = vg_ref[:, sl]
            vn = (vhat * gain).astype(BF)
            dsg = s_scr[:, sl]
            duv_ref[:, sl] = (dsg * y_scr[:, sl] * _gelu_grad(upre)).astype(BF)
            dy = dsg * u
            dyb = dy.astype(BF)
            dvn_parts = []
            for ci in range(n_chunk):
                rs = slice(ci * CHUNK, (ci + 1) * CHUNK)
                dvn_parts.append(jnp.dot(wt_ref[g], dyb[rs], preferred_element_type=F32))
                dw_ref[g] += lax.dot_general(dyb[rs], vn[rs], (((1,), (1,)), ((), ())), preferred_element_type=F32)
                db_ref[:, sl] += jnp.broadcast_to(jnp.sum(dy[rs], axis=-1, keepdims=True), (CHUNK, CH))
            dvn = dvn_parts[0] if n_chunk == 1 else jnp.concatenate(dvn_parts, axis=0)
            _acc_rows(dvg_ref.at[:, sl], dvn * vhat, first)
            dvh = dvn * gain
            dvg = r * (dvh - jnp.mean(dvh, axis=-1, keepdims=True)
                       - vhat * jnp.mean(dvh * vhat, axis=-1, keepdims=True))
            duv_ref[:, SGU_OUT + g * CH:SGU_OUT + (g + 1) * CH] = (dvg * _gelu_grad(vpre)).astype(BF)

    return _pcall(
        body, name=name, grid=(T // tm,),
        ins=[(dmixed, (tm, MLA_OUT), lambda i: (i, 0)), (dmixed, (tm, SGU_OUT), lambda i: (i, 1)),
             (a, (tm, MLA_OUT), lambda i: (i, 0)), (proj, (tm, SGU_OUT), lambda i: (i, 1)),
             (proj, (tm, SGU_OUT), lambda i: (i, 2)), (g_mla, (1, MLA_OUT), lambda i: (0, 0)),
             (g_sgu, (1, SGU_OUT), lambda i: (0, 0)), (v_gain, (1, SGU_OUT), lambda i: (0, 0)),
             (w_tril, (GROUPS, CHUNK, CHUNK), lambda i: (0, 0, 0)), (w_tril_t, (GROUPS, CHUNK, CHUNK), lambda i: (0, 0, 0)),
             (b_full, (CHUNK, SGU_OUT), lambda i: (0, 0))],
        outs=[((T, MLA_OUT), BF, (tm, MLA_OUT), lambda i: (i, 0)),
              ((T, 2 * SGU_OUT), BF, (tm, 2 * SGU_OUT), lambda i: (i, 0)),
              ((1, MLA_OUT), F32, (1, MLA_OUT), lambda i: (0, 0)), ((1, SGU_OUT), F32, (1, SGU_OUT), lambda i: (0, 0)),
              ((1, SGU_OUT), F32, (1, SGU_OUT), lambda i: (0, 0)),
              ((GROUPS, CHUNK, CHUNK), F32, (GROUPS, CHUNK, CHUNK), lambda i: (0, 0, 0)),
              ((CHUNK, SGU_OUT), F32, (CHUNK, SGU_OUT), lambda i: (0, 0))],
        scratch=[pltpu.VMEM((tm, SGU_OUT), F32), pltpu.VMEM((tm, SGU_OUT), F32)])


def _shift_down(z, n, row):
    return jnp.where(row >= n, pltpu.roll(z, n, 0), 0.0)


def _shift_up(z, n, row, T):
    return jnp.where(row < T - n, pltpu.roll(z, T - n, 0), 0.0)


def conv_fwd(proj, conv_w, *, name):
    T, D3 = proj.shape
    D = D3 // 3
    tn = _tile(D, 256)
    nj = D // tn

    def body(b_ref, c_ref, x_ref, w_ref, o_ref):
        row = lax.broadcasted_iota(jnp.int32, (T, tn), 0)
        z = c_ref[...] * x_ref[...]
        zc = w_ref[2:3, :] * z + w_ref[1:2, :] * _shift_down(z, 1, row) + w_ref[0:1, :] * _shift_down(z, 2, row)
        o_ref[...] = (b_ref[...] * zc).astype(BF)

    return _pcall(
        body, name=name, grid=(nj,),
        ins=[(proj, (T, tn), lambda j: (0, j)), (proj, (T, tn), lambda j: (0, nj + j)),
             (proj, (T, tn), lambda j: (0, 2 * nj + j)), (conv_w, (3, tn), lambda j: (0, j))],
        outs=[((T, D), BF, (T, tn), lambda j: (0, j))], semantics=("parallel",))[0]


def conv_bwd(dg, proj, conv_w, *, name):
    T, D3 = proj.shape
    D = D3 // 3
    tn = _tile(D, 256)
    nj = D // tn

    def body(dg_ref, b_ref, c_ref, x_ref, w_ref, dp_ref, dw_ref, dc_scr, dx_scr):
        part = pl.program_id(1)

        @pl.when(part == 0)
        def _():
            row = lax.broadcasted_iota(jnp.int32, (T, tn), 0)
            c, x = c_ref[...], x_ref[...]
            z = c * x
            z1 = _shift_down(z, 1, row)
            z2 = _shift_down(z, 2, row)
            dgv = dg_ref[...]
            zc = w_ref[2:3, :] * z + w_ref[1:2, :] * z1 + w_ref[0:1, :] * z2
            dp_ref[...] = (dgv * zc).astype(BF)
            dzc = dgv * b_ref[...]
            dw_ref[0:1, :] = jnp.sum(dzc * z2, axis=0, keepdims=True)
            dw_ref[1:2, :] = jnp.sum(dzc * z1, axis=0, keepdims=True)
            dw_ref[2:3, :] = jnp.sum(dzc * z, axis=0, keepdims=True)
            dz = (w_ref[2:3, :] * dzc + w_ref[1:2, :] * _shift_up(dzc, 1, row, T)
                  + w_ref[0:1, :] * _shift_up(dzc, 2, row, T))
            dc_scr[...] = (dz * x).astype(BF)
            dx_scr[...] = (dz * c).astype(BF)

        @pl.when(part == 1)
        def _():
            dp_ref[...] = dc_scr[...]

        @pl.when(part == 2)
        def _():
            dp_ref[...] = dx_scr[...]

    return _pcall(
        body, name=name, grid=(nj, 3),
        ins=[(dg, (T, tn), lambda j, p: (0, j)), (proj, (T, tn), lambda j, p: (0, j)),
             (proj, (T, tn), lambda j, p: (0, nj + j)), (proj, (T, tn), lambda j, p: (0, 2 * nj + j)),
             (conv_w, (3, tn), lambda j, p: (0, j))],
        outs=[((T, D3), BF, (T, tn), lambda j, p: (0, p * nj + j)), ((3, D), F32, (3, tn), lambda j, p: (0, j))],
        scratch=[pltpu.VMEM((T, tn), BF), pltpu.VMEM((T, tn), BF)], semantics=("parallel", "arbitrary"))


def loss_bwd(x, gain, target, *, name):
    T, D = x.shape
    tm = _tile(T, ROW_TILE, 8)

    def body(x_ref, g_ref, t_ref, dx_ref, dxb_ref, dg_ref, loss_ref):
        first = pl.program_id(0) == 0
        xv = x_ref[...]
        r = _rstd(xv)
        xh = xv * r
        gain_v = g_ref[...]
        err = xh * gain_v - t_ref[...]
        part = 0.5 * jnp.sum(jnp.mean(err * err, axis=-1, keepdims=True), axis=0, keepdims=True)
        _acc_rows(loss_ref, jnp.broadcast_to(part, (1, LANES)), first)
        dy = err * (1.0 / D)
        gdy = dy * gain_v
        dx = r * (gdy - xh * jnp.mean(gdy * xh, axis=-1, keepdims=True))
        dx_ref[...] = dx
        dxb_ref[...] = dx.astype(BF)
        _acc_rows(dg_ref, dy * xh, first)

    return _pcall(
        body, name=name, grid=(T // tm,),
        ins=[(x, (tm, D), lambda i: (i, 0)), (gain, (1, D), lambda i: (0, 0)), (target, (tm, D), lambda i: (i, 0))],
        outs=[((T, D), F32, (tm, D), lambda i: (i, 0)), ((T, D), BF, (tm, D), lambda i: (i, 0)),
              ((1, D), F32, (1, D), lambda i: (0, 0)), ((1, LANES), F32, (1, LANES), lambda i: (0, 0))])


def _adamw(g, w, m, v):
    m = ADAM_B1 * m + (1.0 - ADAM_B1) * g
    v = ADAM_B2 * v + (1.0 - ADAM_B2) * (g * g)
    m_hat = m / ADAM_C1
    v_hat = v / ADAM_C2
    delta = -ADAM_LR * (m_hat / (jnp.sqrt(v_hat) + ADAM_EPS) + ADAM_WD * w)
    return delta, m, v


def adam_flat(g, w, m, v, *, name):
    def body(g_ref, w_ref, m_ref, v_ref, d_ref, nm_ref, nv_ref):
        d, nm, nv = _adamw(g_ref[...], w_ref[...], m_ref[...], v_ref[...])
        d_ref[...] = d
        nm_ref[...] = nm
        nv_ref[...] = nv

    blk = g.shape
    zero = lambda: (0, 0)
    return _pcall(body, name=name, grid=(),
                  ins=[(t, blk, zero) for t in (g, w, m, v)],
                  outs=[(blk, F32, blk, zero)] * 3)


def _chip_slots():
    x, y, c = lax.axis_index("x"), lax.axis_index("y"), lax.axis_index("c")
    chips = [(1 - x, y), (x, 1 - y), (1 - x, 1 - y)]
    return x, y, c, chips


def reduce_adam(gs, a_buf, b_buf, w, m, v, layer, prev, *, name, deps=()):
    L, R, C = w.shape
    tr = _tile(R, max(16, STREAM_BLOCK_ELEMS // C), 16)
    x, y, c, _ = _chip_slots()
    idx = jnp.stack([4 * x + 2 * y + c, 2 * x + y]).astype(jnp.int32)
    n_prev = 0 if prev is None else 4

    def body(idx_ref, g_ref, a_ref, b0_ref, b1_ref, b2_ref, w_ref, m_ref, v_ref, *rest):
        outs = rest[n_prev:]
        g = ((((g_ref[...].astype(F32) + a_ref[...].astype(F32)) + b0_ref[...].astype(F32))
              + b1_ref[...].astype(F32)) + b2_ref[...].astype(F32))
        d, nm, nv = _adamw(g, w_ref[...], m_ref[...], v_ref[...])
        outs[0][...] = g
        outs[1][...] = d
        outs[2][...] = nm
        outs[3][...] = nv

    blk3 = (None, tr, C)
    ins = [(gs, blk3, lambda i, s: (s[0], i, 0)), (a_buf, blk3, lambda i, s: (s[1], i, 0)),
           (b_buf, blk3, lambda i, s: (0, i, 0)), (b_buf, blk3, lambda i, s: (1, i, 0)),
           (b_buf, blk3, lambda i, s: (2, i, 0)),
           (w, blk3, lambda i, s: (layer, i, 0)), (m, blk3, lambda i, s: (layer, i, 0)),
           (v, blk3, lambda i, s: (layer, i, 0))]
    aliases = {}
    if prev is not None:
        for o, p in enumerate(prev):
            ins.append((p, None, None))
            aliases[1 + 8 + o] = o
    outs = [((L, R, C), F32, blk3, lambda i, s: (layer, i, 0))] * 4
    return _pcall(body, name=name, grid=(R // tr,), ins=ins, outs=outs, prefetch=idx, aliases=aliases,
                  semantics=("parallel",), deps=deps)


def reduce_sum(gs, a_buf, b_buf, *, name):
    _, R, C = gs.shape
    tr = _tile(R, 256, 16)
    x, y, c, _ = _chip_slots()
    idx = jnp.stack([4 * x + 2 * y + c, 2 * x + y]).astype(jnp.int32)

    def body(idx_ref, g_ref, a_ref, b0_ref, b1_ref, b2_ref, o_ref):
        o_ref[...] = ((((g_ref[...].astype(F32) + a_ref[...].astype(F32)) + b0_ref[...].astype(F32))
                       + b1_ref[...].astype(F32)) + b2_ref[...].astype(F32))

    blk3 = (None, tr, C)
    return _pcall(body, name=name, grid=(R // tr,),
                  ins=[(gs, blk3, lambda i, s: (s[0], i, 0)), (a_buf, blk3, lambda i, s: (s[1], i, 0)),
                       (b_buf, blk3, lambda i, s: (0, i, 0)), (b_buf, blk3, lambda i, s: (1, i, 0)),
                       (b_buf, blk3, lambda i, s: (2, i, 0))],
                  outs=[((R, C), F32, (tr, C), lambda i, s: (i, 0))], prefetch=idx, semantics=("parallel",))[0]


def adam_rows(g, w, m, v, *, name):
    R, C = g.shape
    tr = _tile(R, 256, 8)

    def body(g_ref, w_ref, m_ref, v_ref, d_ref, nm_ref, nv_ref):
        d, nm, nv = _adamw(g_ref[...], w_ref[...], m_ref[...], v_ref[...])
        d_ref[...] = d
        nm_ref[...] = nm
        nv_ref[...] = nv

    spec = ((tr, C), lambda i: (i, 0))
    return _pcall(body, name=name, grid=(R // tr,), ins=[(t, *spec) for t in (g, w, m, v)],
                  outs=[((R, C), F32, *spec)] * 3, semantics=("parallel",))


def pair_sum(gs, a_buf, *, name):
    _, R, C = gs.shape
    tr = _tile(R, max(16, 2 * STREAM_BLOCK_ELEMS // C), 16)
    x, y, c, chips = _chip_slots()
    idx = jnp.stack([4 * cx + 2 * cy + c for cx, cy in chips] + [2 * cx + cy for cx, cy in chips]).astype(jnp.int32)

    def body(idx_ref, g_ref, a_ref, o_ref):
        o_ref[...] = (g_ref[...].astype(F32) + a_ref[...].astype(F32)).astype(BF)

    blk3 = (None, tr, C)
    return _pcall(body, name=name, grid=(3, R // tr),
                  ins=[(gs, blk3, lambda j, i, s: (s[j], i, 0)), (a_buf, blk3, lambda j, i, s: (s[3 + j], i, 0))],
                  outs=[((3, R, C), BF, blk3, lambda j, i, s: (j, i, 0))], prefetch=idx,
                  semantics=("parallel", "parallel"))[0]


def sum_rows8(gathered, rows, *, name):
    W = gathered.shape[1]

    def body(g_ref, o_ref):
        acc = g_ref[0:rows, :]
        for d in range(1, N_DEV):
            acc = acc + g_ref[d * rows:(d + 1) * rows, :]
        o_ref[...] = acc

    return _pcall(body, name=name, grid=(), ins=[(gathered, gathered.shape, lambda: (0, 0))],
                  outs=[((rows, W), F32, (rows, W), lambda: (0, 0))])[0]


HBM_SPEC = pl.BlockSpec(memory_space=pltpu.HBM)
SEM_SPEC = pl.BlockSpec(memory_space=pltpu.SEMAPHORE)
ANY_SPEC = pl.BlockSpec(memory_space=pl.ANY)
DATAFLOW = pltpu.SideEffectType.DATAFLOW_SIDE_EFFECTING


def _in_hbm(v):
    return pltpu.with_memory_space_constraint(v, pltpu.HBM)


def _slot(p):
    return 4 * p[0] + 2 * p[1] + p[2]


def _gather_peers():
    x, y, c, chips = _chip_slots()
    return (x, y, c), [(x, y, 1 - c)] + [(*chip, c) for chip in chips]


def gather_start(groups, after, *, name):
    flat = [s for g in groups for s in g]
    n, n_g = len(flat), len(groups)
    where = [(gi, ti) for gi, g in enumerate(groups) for ti in range(len(g))]

    def body(*refs):
        src, land = refs[:n], refs[n:2 * n]
        sems = refs[2 * n + 1:2 * n + 1 + 2 * n_g]
        me, peers = _gather_peers()
        for t in range(n):
            gi, ti = where[t]
            for k, to in enumerate(peers):
                pltpu.make_async_remote_copy(
                    src_ref=src[t], dst_ref=land[t].at[_slot(me)], send_sem=sems[2 * gi].at[4 * ti + k],
                    recv_sem=sems[2 * gi + 1].at[4 * ti + k], device_id=to, device_id_type=MESH).start()
        refs[-1][...] = jnp.zeros_like(refs[-1])

    out_shape = []
    for g in groups:
        out_shape += [pltpu.SemaphoreType.DMA((4 * len(g),)), pltpu.SemaphoreType.DMA((4 * len(g),))]
    out_shape += [pltpu.HBM(s.shape, s.dtype) for s in flat]
    out_shape += [pltpu.HBM((N_DEV,) + s.shape, s.dtype) for s in flat]
    out_shape += [jax.ShapeDtypeStruct((8, LANES), F32)]
    aliases = {t: 2 * n_g + t for t in range(n)}
    aliases.update({n + t: 2 * n_g + n + t for t in range(n)})
    res = pl.pallas_call(
        body, name=name, out_shape=out_shape, in_specs=[HBM_SPEC] * (2 * n) + [ANY_SPEC],
        out_specs=[SEM_SPEC] * (2 * n_g) + [HBM_SPEC] * (2 * n) + [pl.BlockSpec(memory_space=pltpu.VMEM)],
        input_output_aliases=aliases, compiler_params=pltpu.CompilerParams(has_side_effects=DATAFLOW),
    )(*[_in_hbm(s) for s in flat], *[_in_hbm(lax.empty((N_DEV,) + s.shape, s.dtype)) for s in flat], after)
    out, off = [], 0
    for gi, g in enumerate(groups):
        k = len(g)
        out.append((res[2 * gi], res[2 * gi + 1], res[2 * n_g + off:2 * n_g + off + k],
                    res[2 * n_g + n + off:2 * n_g + n + off + k]))
        off += k
    return out, res[-1]


def gather_wait(started, after, *, name):
    send_sems, recv_sems, srcs, lands = started
    n = len(srcs)
    after = list(after)

    def body(*refs):
        src, land = refs[:n], refs[n:2 * n]
        send, recv = refs[2 * n], refs[2 * n + 1]
        _, peers = _gather_peers()
        for t in range(n):
            for k, frm in enumerate(peers):
                cp = pltpu.make_async_remote_copy(
                    src_ref=src[t], dst_ref=land[t].at[_slot(frm)], send_sem=send.at[4 * t + k],
                    recv_sem=recv.at[4 * t + k],
                    device_id=frm, device_id_type=MESH)
                cp.wait_send()
                cp.wait_recv()

    res = pl.pallas_call(
        body, name=name,
        out_shape=[pltpu.HBM(s.shape, s.dtype) for s in srcs] + [pltpu.HBM(l.shape, l.dtype) for l in lands],
        in_specs=[HBM_SPEC] * (2 * n) + [SEM_SPEC, SEM_SPEC] + [ANY_SPEC] * len(after),
        out_specs=[HBM_SPEC] * (2 * n), input_output_aliases={t: t for t in range(2 * n)},
        compiler_params=pltpu.CompilerParams(has_side_effects=DATAFLOW),
    )(*srcs, *lands, send_sems, recv_sems, *after)
    return res[:n], res[n:]


def place_own(src, land, *, name):
    R, C = src.shape
    tr = _tile(R, 512, 16)
    x, y, c, _ = _chip_slots()
    idx = jnp.stack([4 * x + 2 * y + c]).astype(jnp.int32)

    def body(idx_ref, s_ref, land_ref, o_ref):
        o_ref[...] = s_ref[...]

    return _pcall(body, name=name, grid=(R // tr,),
                  ins=[(src, (tr, C), lambda i, s: (i, 0)), (land, None, None)],
                  outs=[(land.shape, land.dtype, (None, tr, C), lambda i, s: (s[0], i, 0))],
                  prefetch=idx, aliases={2: 0}, semantics=("parallel",))[0]


def gather_finish(srcs, lands, *, name):
    n = len(srcs)

    def body(*refs):
        land = refs[n:2 * n]
        send_sems, recv_sems = refs[2 * n:]
        x, y, c, chips = _chip_slots()
        me, sibling = (x, y, c), (x, y, 1 - c)

        def copy(t, j, block, to):
            return pltpu.make_async_remote_copy(
                src_ref=land[t].at[_slot(block)], dst_ref=land[t].at[_slot(block)], send_sem=send_sems.at[t, j],
                recv_sem=recv_sems.at[t, j], device_id=to, device_id_type=MESH)

        sends = [copy(t, j, (*chip, c), sibling) for t in range(n) for j, chip in enumerate(chips)]
        for cp in sends:
            cp.start()
        for t in range(n):
            for j, chip in enumerate(chips):
                copy(t, j, (*chip, 1 - c), me).wait_recv()
        for cp in sends:
            cp.wait_send()

    passed = pl.pallas_call(
        body, name=name, out_shape=[jax.ShapeDtypeStruct(l.shape, l.dtype) for l in lands],
        in_specs=[ANY_SPEC] * n, out_specs=[ANY_SPEC] * n,
        input_output_aliases={t: t for t in range(n)},
        scratch_shapes=[pltpu.SemaphoreType.DMA((n, 3)), pltpu.SemaphoreType.DMA((n, 3))],
    )(*lands)
    return [place_own(s, l, name=f"{name}_own{t}") for t, (s, l) in enumerate(zip(srcs, passed))]


def chips_start(pairs, *, name):
    n = len(pairs)

    def body(*refs):
        src, land = refs[:n], refs[n:2 * n]
        send, recv = refs[2 * n], refs[2 * n + 1]
        token = refs[-1]
        x, y, c, chips = _chip_slots()
        for t in range(n):
            for j, chip in enumerate(chips):
                pltpu.make_async_remote_copy(
                    src_ref=src[t].at[j], dst_ref=land[t].at[j], send_sem=send.at[3 * t + j],
                    recv_sem=recv.at[3 * t + j], device_id=(*chip, c), device_id_type=MESH).start()
        token[...] = jnp.zeros_like(token)

    res = pl.pallas_call(
        body, name=name,
        out_shape=[pltpu.SemaphoreType.DMA((3 * n,)), pltpu.SemaphoreType.DMA((3 * n,))]
        + [pltpu.HBM(p.shape, p.dtype) for p in pairs] * 2 + [jax.ShapeDtypeStruct((8, LANES), F32)],
        in_specs=[HBM_SPEC] * (2 * n),
        out_specs=[SEM_SPEC, SEM_SPEC] + [HBM_SPEC] * (2 * n) + [pl.BlockSpec(memory_space=pltpu.VMEM)],
        input_output_aliases={t: 2 + t for t in range(2 * n)},
        compiler_params=pltpu.CompilerParams(has_side_effects=DATAFLOW),
    )(*[_in_hbm(p) for p in pairs], *[_in_hbm(lax.empty(p.shape, p.dtype)) for p in pairs])
    return res[0], res[1], res[2:2 + n], res[2 + n:2 + 2 * n], res[-1]


def chips_wait(started, after, *, name):
    send_sems, recv_sems, srcs, lands, _ = started
    n = len(srcs)

    def body(*refs):
        src, land = refs[:n], refs[n:2 * n]
        send, recv = refs[2 * n], refs[2 * n + 1]
        x, y, c, chips = _chip_slots()
        for t in range(n):
            for j, chip in enumerate(chips):
                cp = pltpu.make_async_remote_copy(
                    src_ref=src[t].at[j], dst_ref=land[t].at[j], send_sem=send.at[3 * t + j],
                    recv_sem=recv.at[3 * t + j], device_id=(*chip, c), device_id_type=MESH)
                cp.wait_send()
                cp.wait_recv()

    res = pl.pallas_call(
        body, name=name, out_shape=[pltpu.HBM(s.shape, s.dtype) for s in srcs] * 2,
        in_specs=[HBM_SPEC] * (2 * n) + [SEM_SPEC, SEM_SPEC, ANY_SPEC], out_specs=[HBM_SPEC] * (2 * n),
        input_output_aliases={t: t for t in range(2 * n)},
        compiler_params=pltpu.CompilerParams(has_side_effects=DATAFLOW),
    )(*srcs, *lands, send_sems, recv_sems, after)
    return res[n:]


def _sibling_copies(src, land, send, recv, n):
    x, y, c, _ = _chip_slots()
    return [pltpu.make_async_remote_copy(
        src_ref=src[t].at[4 * (q // 2) + 2 * (q % 2) + (1 - c)], dst_ref=land[t].at[q], send_sem=send.at[4 * t + q],
        recv_sem=recv.at[4 * t + q], device_id=(x, y, 1 - c), device_id_type=MESH)
        for t in range(n) for q in range(4)]


def sibling_start(gs, *, name):
    n = len(gs)

    def body(*refs):
        for cp in _sibling_copies(refs[:n], refs[n:2 * n], refs[2 * n], refs[2 * n + 1], n):
            cp.start()
        refs[-1][...] = jnp.zeros_like(refs[-1])

    lands = [lax.empty((4,) + g.shape[1:], g.dtype) for g in gs]
    res = pl.pallas_call(
        body, name=name,
        out_shape=[pltpu.SemaphoreType.DMA((4 * n,)), pltpu.SemaphoreType.DMA((4 * n,))]
        + [pltpu.HBM(g.shape, g.dtype) for g in gs] + [pltpu.HBM(l.shape, l.dtype) for l in lands]
        + [jax.ShapeDtypeStruct((8, LANES), F32)],
        in_specs=[HBM_SPEC] * (2 * n),
        out_specs=[SEM_SPEC, SEM_SPEC] + [HBM_SPEC] * (2 * n) + [pl.BlockSpec(memory_space=pltpu.VMEM)],
        input_output_aliases={t: 2 + t for t in range(2 * n)},
        compiler_params=pltpu.CompilerParams(has_side_effects=DATAFLOW),
    )(*[_in_hbm(g) for g in gs], *[_in_hbm(l) for l in lands])
    return res[0], res[1], res[2:2 + n], res[2 + n:2 + 2 * n], res[-1]


def sibling_wait(started, after, *, name):
    send_sems, recv_sems, srcs, lands, _ = started
    n = len(srcs)

    def body(*refs):
        for cp in _sibling_copies(refs[:n], refs[n:2 * n], refs[2 * n], refs[2 * n + 1], n):
            cp.wait_send()
            cp.wait_recv()

    res = pl.pallas_call(
        body, name=name,
        out_shape=[pltpu.HBM(s.shape, s.dtype) for s in srcs] + [pltpu.HBM(l.shape, l.dtype) for l in lands],
        in_specs=[HBM_SPEC] * (2 * n) + [SEM_SPEC, SEM_SPEC, ANY_SPEC], out_specs=[HBM_SPEC] * (2 * n),
        input_output_aliases={t: t for t in range(2 * n)},
        compiler_params=pltpu.CompilerParams(has_side_effects=DATAFLOW),
    )(*srcs, *lands, send_sems, recv_sems, after)
    return res[:n], res[n:]


def all_gather_vmem(x_shard, *, name, after=None):
    m_per, n = x_shard.shape
    n_after = 0 if after is None else 1

    def body(x_ref, *rest):
        out_ref, send_sems, recv_sems, local_sem = rest[n_after:]
        x, y, c, chips = _chip_slots()
        me, sibling = (x, y, c), (x, y, 1 - c)

        def rows(px, py, pc):
            return out_ref.at[pl.ds((4 * px + 2 * py + pc) * m_per, m_per), :]

        def copy(k, block, to, src=None):
            return pltpu.make_async_remote_copy(
                src_ref=rows(*block) if src is None else src, dst_ref=rows(*block),
                send_sem=send_sems.at[k], recv_sem=recv_sems.at[k], device_id=to, device_id_type=MESH)

        mine = pltpu.make_async_copy(x_ref, rows(*me), local_sem)
        mine.start()
        first = [copy(0, me, sibling, src=x_ref)]
        first += [copy(1 + j, me, (*chip, c), src=x_ref) for j, chip in enumerate(chips)]
        for cp in first:
            cp.start()
        passed = [copy(4 + j, (*chip, c), sibling) for j, chip in enumerate(chips)]
        for j, chip in enumerate(chips):
            copy(1 + j, (*chip, c), me).wait_recv()
            passed[j].start()
        copy(0, sibling, me).wait_recv()
        for j, chip in enumerate(chips):
            copy(4 + j, (*chip, 1 - c), me).wait_recv()
        for cp in first + passed:
            cp.wait_send()
        mine.wait()

    vmem = pl.BlockSpec(memory_space=pltpu.VMEM)
    return pl.pallas_call(
        body, name=name, out_shape=jax.ShapeDtypeStruct((N_DEV * m_per, n), x_shard.dtype),
        in_specs=[vmem] + [ANY_SPEC] * n_after, out_specs=vmem,
        scratch_shapes=[pltpu.SemaphoreType.DMA((7,)), pltpu.SemaphoreType.DMA((7,)), pltpu.SemaphoreType.DMA],
        compiler_params=pltpu.CompilerParams(vmem_limit_bytes=int(min(
            VMEM_LIMIT_CAP, 2 * (N_DEV + 1) * m_per * n * x_shard.dtype.itemsize + 16 * 2 ** 20))),
    )(x_shard, *([] if after is None else [after]))


def _rope_slab(cols):
    z = jnp.zeros(cols.shape[:-1] + (HALF_ROPE,), cols.dtype)
    return jnp.concatenate([cols[..., :HALF_ROPE], z, cols[..., HALF_ROPE:], z], axis=-1)


def _rope_unslab(slab):
    return jnp.concatenate([slab[..., :HALF_ROPE], slab[..., 2 * HALF_ROPE:3 * HALF_ROPE]], axis=-1)


def _pack_w_in_t(wt_g):
    s, c, d = wt_g.shape
    w = wt_g.reshape(s * c, d)
    c2, c3 = Q_LORA + KV_LORA, Q_LORA + KV_LORA + QK_ROPE
    r = w[c2:c3]
    z = jnp.zeros((HALF_ROPE, d), w.dtype)
    return jnp.concatenate([w[:c2], w[c3:], r[:HALF_ROPE], z, r[HALF_ROPE:], z], axis=0)


def _unpack_w_in_t_grad(dwt):
    d = dwt.shape[1]
    c2 = Q_LORA + KV_LORA
    uv = 2 * SGU_OUT
    slab = dwt[c2 + uv:]
    g = jnp.concatenate([dwt[:c2], slab[:HALF_ROPE], slab[2 * HALF_ROPE:3 * HALF_ROPE], dwt[c2:c2 + uv]], axis=0)
    return g.reshape(N_DEV, g.shape[0] // N_DEV, d)


def _rope_tables(positions):
    inv_freq = ROPE_BASE ** (-jnp.arange(0, QK_ROPE, 2, dtype=F32) / QK_ROPE)
    ang = positions.astype(F32)[:, None] * inv_freq
    cos, sin = jnp.cos(ang), jnp.sin(ang)
    z = jnp.zeros_like(cos)
    return jnp.concatenate([cos, z, cos, z], axis=-1), jnp.concatenate([-sin, z, sin, z], axis=-1)


def _mlp_up(x, gain, w1, tag):
    hn = rms_fwd(x, gain, name=f"mlp{tag}_norm")

    def act_epi(acc):
        a = jnp.maximum(acc, 0.0)
        return a, a * a

    T = x.shape[0]
    F = w1.shape[0] * w1.shape[2]
    a, act = mm(hn, w1, name=f"mlp{tag}_up", outs=[((T, F), BF, None), ((T, F), BF, None)], epi=act_epi)
    return hn, a, act


def _mlp_down(x, act, w2, tag):
    bm = _tile(x.shape[0], MM_TILE)
    bn = _tile(x.shape[1], MM_TILE)
    return mm(act, w2, name=f"mlp{tag}_down", out=(x.shape, F32), bm=bm, bn=bn,
              epi=lambda acc, r: (acc + r[...],), epi_ins=[(x, (bm, bn), lambda i, j, k: (i, j))])


def _mlp_bwd_weights(w1, w2, saved, dxb, tag):
    hn, a, act = saved
    T, D = dxb.shape
    F = a.shape[1]
    bm = _tile(T, MM_TILE)
    bn = _tile(F, min(MM_TILE, w1.shape[2]))
    dhid = mm(dxb, w2, tb=True, name=f"mlp{tag}_dhid", out=((T, F), BF), bm=bm, bn=bn,
              epi=lambda acc, a_ref: (2.0 * a_ref[...].astype(F32) * acc,),
              epi_ins=[(a, (bm, bn), lambda i, j, k: (i, j))])
    dw2 = mm(act, dxb, ta=True, name=f"mlp{tag}_dw2", out=((F, D), BF))
    dw1 = mm(hn, dhid, ta=True, name=f"mlp{tag}_dw1", out=(w1.shape, BF))
    return dhid, dw1, dw2.reshape(N_DEV, F // N_DEV, D)


def _reduce_begin(grads, tag):
    return sibling_start(grads, name=f"reduce_sibling_start_{tag}")


def _reduce_continue(sib, after, tag):
    grads, a_bufs = sibling_wait(sib, after, name=f"reduce_sibling_wait_{tag}")
    pairs = [pair_sum(g, a, name=f"pair_sum_{tag}{t}") for t, (g, a) in enumerate(zip(grads, a_bufs))]
    return grads, a_bufs, chips_start(pairs, name=f"reduce_chips_start_{tag}")


def _mlp_bwd_input(x_in, gain, w1, dhid, dx, tag, sib):
    dhn = mm(dhid, w1, tb=True, name=f"mlp{tag}_dhn", out=(x_in.shape, F32), deps=[sib[-1]])
    reduce_state = _reduce_continue(sib, dhn, f"r_mlp{tag}")
    return rms_bwd(x_in, gain, dhn, dres=dx, name=f"mlp{tag}_norm_bwd", deps=[reduce_state[2][-1]]), reduce_state


def kernel(x, positions, e_norm_mix, e_w_in, e_q_norm, e_w_uq, e_kv_norm, e_w_ukv, e_v_norm, e_sgu_w, e_sgu_b, e_mla_out_norm, e_sgu_out_norm, e_w_out, o_norm_mix, o_w_in, o_conv_w, o_w_out, mlp_norm, mlp_w1, mlp_w2, final_norm, loss_target, m_e_norm_mix, m_e_w_in, m_e_q_norm, m_e_w_uq, m_e_kv_norm, m_e_w_ukv, m_e_v_norm, m_e_sgu_w, m_e_sgu_b, m_e_mla_out_norm, m_e_sgu_out_norm, m_e_w_out, m_o_norm_mix, m_o_w_in, m_o_conv_w, m_o_w_out, m_mlp_norm, m_mlp_w1, m_mlp_w2, m_final_norm, v_e_norm_mix, v_e_w_in, v_e_q_norm, v_e_w_uq, v_e_kv_norm, v_e_w_ukv, v_e_v_norm, v_e_sgu_w, v_e_sgu_b, v_e_mla_out_norm, v_e_sgu_out_norm, v_e_w_out, v_o_norm_mix, v_o_w_in, v_o_conv_w, v_o_w_out, v_mlp_norm, v_mlp_w1, v_mlp_w2, v_final_norm):
    T, D = x.shape[1], x.shape[2]
    d_shard = o_norm_mix.shape[1]
    x0 = x[0]
    target = loss_target[0]
    me = 4 * lax.axis_index("x") + 2 * lax.axis_index("y") + lax.axis_index("c")

    bf = lambda s: s.astype(BF)
    gather_groups = [[bf(jnp.transpose(e_w_in[0])), bf(e_w_uq[0]), bf(e_w_ukv[0])], [bf(e_w_out[0]), bf(mlp_w1[0])],
                     [bf(mlp_w2[0]), bf(o_w_in[0])], [bf(o_w_out[0]), bf(mlp_w1[1])], [bf(mlp_w2[1])]]
    small_rows = jnp.concatenate([o_norm_mix, o_conv_w[0], jnp.zeros((4, d_shard), F32)], axis=0)
    small_flat = all_gather_vmem(small_rows, name="gather_small")
    started, start_token = gather_start(gather_groups[:1], small_flat, name="gather_start0")
    started_rest, rest_token = gather_start(gather_groups[1:], start_token, name="gather_start1")
    started += started_rest

    def gathered(gi, after):
        srcs, lands = gather_wait(started[gi], after, name=f"gather_wait{gi}")
        return gather_finish(srcs, lands, name=f"gather_finish{gi}")

    small_g = small_flat.reshape(N_DEV, 8, d_shard)
    o_norm_full = small_g[:, 0, :].reshape(1, D)
    conv_w_full = jnp.transpose(small_g[:, 1:4, :], (1, 0, 2)).reshape(3, D)
    w_tril = jnp.tril(e_sgu_w[0])
    w_tril_b = w_tril.astype(BF)
    w_tril_tb = jnp.swapaxes(w_tril, 1, 2).astype(BF)
    b_full = jnp.repeat(e_sgu_b[0].T, CH, axis=1)
    v_gain = e_v_norm[0].reshape(1, SGU_OUT)
    cos_t, sin_t = _rope_tables(positions[0])
    mlp_gain = [mlp_norm[0:1], mlp_norm[1:2]]
    final_gain = final_norm.reshape(1, D)

    h0 = rms_fwd(x0, e_norm_mix, name="e_norm", deps=[rest_token])
    g_w_in_t, g_w_uq, w_ukv = gathered(
        0, [h0, cos_t, sin_t, w_tril_b, w_tril_tb, b_full, o_norm_full, conv_w_full])
    w_in_t = _pack_w_in_t(g_w_in_t)
    w_uq = jnp.concatenate([g_w_uq[..., :QK_NOPE], _rope_slab(g_w_uq[..., QK_NOPE:])], axis=-1)
    proj = mm(h0, w_in_t, tb=True, name="e_in", out=((T, w_in_t.shape[0]), F32), bn=_tile(w_in_t.shape[0], 640))
    qn, kvn, krope = mla_prep(proj, e_q_norm, e_kv_norm, cos_t, sin_t, name="mla_prep")
    bm = _tile(T, MM_TILE)

    def q_epi(acc, cos_ref, sin_ref):
        return (jnp.concatenate([acc[:, :QK_NOPE], _rope_fwd(acc[:, QK_NOPE:], cos_ref[...], sin_ref[...])], axis=-1),)

    q = mm(qn, w_uq, name="mla_q", out=((T, HEADS * HEAD_PAD), BF), bm=bm, bn=HEAD_PAD, epi=q_epi,
           epi_ins=[(cos_t, (bm, LANES), lambda i, j, k: (i, 0)), (sin_t, (bm, LANES), lambda i, j, k: (i, 0))])

    def kv_epi(acc, kr_ref):
        return jnp.concatenate([acc[:, :QK_NOPE].astype(BF), kr_ref[...]], axis=-1), acc[:, QK_NOPE:]

    k, v = mm(kvn, w_ukv, name="mla_kv", bm=bm, bn=HEAD_PAD, epi=kv_epi,
              outs=[((T, HEADS * HEAD_PAD), BF, HEAD_PAD), ((T, MLA_OUT), BF, V_HEAD)],
              epi_ins=[(krope, (bm, LANES), lambda i, j, k: (i, 0))])
    attn, attn_lse = attn_fwd(q, k, v, name="attn_fwd")
    mixed = mix_fwd(attn, proj, e_mla_out_norm, e_sgu_out_norm, v_gain, w_tril_b, b_full, name="mix_fwd")
    bn = _tile(D, MM_TILE)
    g_w_out_e, w1_0 = gathered(1, [mixed])
    w_out_e = g_w_out_e.reshape(-1, D)
    x1 = mm(mixed, w_out_e, name="e_out", out=((T, D), F32), bm=bm, bn=bn,
            epi=lambda acc, r: (acc + r[...],), epi_ins=[(x0, (bm, bn), lambda i, j, k: (i, j))])
    hn0, a0, act0 = _mlp_up(x1, mlp_gain[0], w1_0, 0)
    g_w2_0, g_w_in_o = gathered(2, [act0])
    w2_0 = g_w2_0.reshape(-1, D)
    x2 = _mlp_down(x1, act0, w2_0, 0)
    ho = rms_fwd(x2, o_norm_full, name="o_norm")
    proj_o = mm(ho, g_w_in_o, name="o_in", out=((T, 3 * D), F32))
    gated = conv_fwd(proj_o, conv_w_full, name="conv_fwd")
    g_w_out_o, w1_1 = gathered(3, [gated])
    w_out_o = g_w_out_o.reshape(-1, D)
    x3 = mm(gated, w_out_o, name="o_out", out=((T, D), F32), bm=bm, bn=bn,
            epi=lambda acc, r: (acc + r[...],), epi_ins=[(x2, (bm, bn), lambda i, j, k: (i, j))])
    hn1, a1, act1 = _mlp_up(x3, mlp_gain[1], w1_1, 1)
    (g_w2_1,) = gathered(4, [act1])
    w2_1 = g_w2_1.reshape(-1, D)
    x4 = _mlp_down(x3, act1, w2_1, 1)
    w1, w2 = [w1_0, w1_1], [w2_0, w2_1]
    mlp0_saved, mlp1_saved = (hn0, a0, act0), (hn1, a1, act1)

    dx4, dx4b, d_final, loss_part = loss_bwd(x4, final_gain, target, name="loss_bwd")

    dhid1, dw1_1, dw2_1 = _mlp_bwd_weights(w1[1], w2[1], mlp1_saved, dx4b, 1)
    sib_r0 = _reduce_begin([dw1_1, dw2_1], "r0")
    (dx3, dx3b, d_mlp1), (grads_r0, a_r0, st_r0) = _mlp_bwd_input(x3, mlp_gain[1], w1[1], dhid1, dx4, 1, sib_r0)

    dgated = mm(dx3b, w_out_o, tb=True, name="o_out_dx", out=((T, D), F32))
    dw_out_o = mm(gated, dx3b, ta=True, name="o_out_dw", out=((D, D), BF))
    dproj_o, dconv_full = conv_bwd(dgated, proj_o, conv_w_full, name="conv_bwd")
    dw_in_o = mm(ho, dproj_o, ta=True, name="o_in_dw", out=(g_w_in_o.shape, BF))
    sib_r1 = _reduce_begin([dw_out_o.reshape(g_w_out_o.shape), dw_in_o], "r1")
    dho = mm(dproj_o, g_w_in_o, tb=True, name="o_in_dx", out=((T, D), F32), deps=[sib_r1[-1]])
    grads_r1, a_r1, st_r1 = _reduce_continue(sib_r1, dho, "r1")
    dx2, dx2b, d_onorm_full = rms_bwd(x2, o_norm_full, dho, dres=dx3, name="o_norm_bwd", deps=[st_r1[-1]])

    dhid0, dw1_0, dw2_0 = _mlp_bwd_weights(w1[0], w2[0], mlp0_saved, dx2b, 0)
    sib_r2 = _reduce_begin([dw1_0, dw2_0], "r2")
    (dx1, dx1b, d_mlp0), (grads_r2, a_r2, st_r2) = _mlp_bwd_input(x1, mlp_gain[0], w1[0], dhid0, dx2, 0, sib_r2)
    b_r0 = chips_wait(st_r0, dx1b, name="reduce_chips_wait_r0")

    dmixed = mm(dx1b, w_out_e, tb=True, name="e_out_dx", out=((T, MLA_OUT + SGU_OUT), F32))
    dw_out_e = mm(mixed, dx1b, ta=True, name="e_out_dw", out=(w_out_e.shape, BF))
    (dattn, duv, d_mla_out, d_sgu_out, d_vgain, d_sgu_w, d_b_full) = mix_bwd(
        dmixed, attn, proj, e_mla_out_norm, e_sgu_out_norm, v_gain, w_tril_b, w_tril_tb, b_full, name="mix_bwd")
    b_r1 = chips_wait(st_r1, dattn, name="reduce_chips_wait_r1")
    dq, dk, dv = attn_bwd(q, k, v, attn, attn_lse, dattn, name="attn_bwd")
    dq_lin, dkv_lin, dkr = mla_bwd_prep(dq, dk, dv, cos_t, sin_t, name="mla_bwd_prep")
    dw_uq_pad = mm(qn, dq_lin, ta=True, name="mla_q_dw", out=(w_uq.shape, BF))
    dw_ukv = mm(kvn, dkv_lin, ta=True, name="mla_kv_dw", out=(w_ukv.shape, BF))
    dw_uq = jnp.concatenate([dw_uq_pad[..., :QK_NOPE], _rope_unslab(dw_uq_pad[..., QK_NOPE:])], axis=-1)
    sib_r2b = _reduce_begin([dw_out_e.reshape(g_w_out_e.shape), dw_uq, dw_ukv], "r2b")
    dqn = mm(dq_lin, w_uq, tb=True, name="mla_q_dx", out=((T, Q_LORA), F32), deps=[sib_r2b[-1]])
    dkvn = mm(dkv_lin, w_ukv, tb=True, name="mla_kv_dx", out=((T, KV_LORA), F32), deps=[sib_r2b[-1]])
    grads_r2b, a_r2b, st_r2b = _reduce_continue(sib_r2b, dkvn, "r2b")
    dcq, d_qnorm = rms_bwd(proj, e_q_norm, dqn, col_block=0, want_f32=False, name="q_norm_bwd", deps=[st_r2b[-1]])
    dckv, d_kvnorm = rms_bwd(proj, e_kv_norm, dkvn, col_block=1, want_f32=False, name="kv_norm_bwd")
    dproj = jnp.concatenate([dcq, dckv, duv, dkr], axis=-1)
    dw_in_t_pad = mm(dproj, h0, ta=True, name="e_in_dw", out=(w_in_t.shape, BF), bm=_tile(w_in_t.shape[0], 640))
    dw_in_t = _unpack_w_in_t_grad(dw_in_t_pad)
    sib_r3 = _reduce_begin([dw_in_t], "r3")
    dh0 = mm(dproj, w_in_t, name="e_in_dx", out=((T, D), F32), deps=[sib_r3[-1]])
    grads_r3, a_r3, st_r3 = _reduce_continue(sib_r3, dh0, "r3")
    tok_r3 = st_r3[-1]
    grad_x, d_enorm = rms_bwd(x0, e_norm_mix, dh0, dres=dx1, want_bf=False, name="e_norm_bwd", deps=[tok_r3])
    b_r2 = chips_wait(st_r2, grad_x, name="reduce_chips_wait_r2")

    def finish(grads, a_bufs, b_bufs, t, w, m, v, layer=0, prev=None, tag="", deps=()):
        return reduce_adam(grads[t], a_bufs[t], b_bufs[t], w, m, v, layer, prev, name=f"adam_{tag}", deps=deps)

    r_w1 = finish(grads_r0, a_r0, b_r0, 0, mlp_w1, m_mlp_w1, v_mlp_w1, 1, None, tag="w1_l1")
    r_w2 = finish(grads_r0, a_r0, b_r0, 1, mlp_w2, m_mlp_w2, v_mlp_w2, 1, None, tag="w2_l1")
    r_w_out_o = finish(grads_r1, a_r1, b_r1, 0, o_w_out, m_o_w_out, v_o_w_out, tag="o_w_out")
    r_w_in_o = finish(grads_r1, a_r1, b_r1, 1, o_w_in, m_o_w_in, v_o_w_in, tag="o_w_in")
    r_w1 = finish(grads_r2, a_r2, b_r2, 0, mlp_w1, m_mlp_w1, v_mlp_w1, 0, r_w1, tag="w1_l0", deps=[tok_r3])
    r_w2 = finish(grads_r2, a_r2, b_r2, 1, mlp_w2, m_mlp_w2, v_mlp_w2, 0, r_w2, tag="w2_l0", deps=[r_w1[1]])
    b_r2b = chips_wait(st_r2b, r_w2[1], name="reduce_chips_wait_r2b")
    r_w_out_e = finish(grads_r2b, a_r2b, b_r2b, 0, e_w_out, m_e_w_out, v_e_w_out, tag="e_w_out")
    r_w_uq = finish(grads_r2b, a_r2b, b_r2b, 1, e_w_uq, m_e_w_uq, v_e_w_uq, tag="e_w_uq")
    r_w_ukv = finish(grads_r2b, a_r2b, b_r2b, 2, e_w_ukv, m_e_w_ukv, v_e_w_ukv, tag="e_w_ukv")
    b_r3 = chips_wait(st_r3, r_w_out_e[1], name="reduce_chips_wait_r3")
    g_w_in = jnp.transpose(reduce_sum(grads_r3[0], a_r3[0], b_r3[0], name="sum_e_w_in"))
    r_w_in = [t[None] for t in (g_w_in, *adam_rows(g_w_in, e_w_in[0], m_e_w_in[0], v_e_w_in[0], name="adam_e_w_in"))]

    d_sgu_b = jnp.transpose(d_b_full[:, ::CH])
    d_sgu_w_tril = jnp.tril(d_sgu_w)
    rep = [("e_norm_mix", e_norm_mix, m_e_norm_mix, v_e_norm_mix, d_enorm),
           ("e_q_norm", e_q_norm, m_e_q_norm, v_e_q_norm, d_qnorm),
           ("e_kv_norm", e_kv_norm, m_e_kv_norm, v_e_kv_norm, d_kvnorm),
           ("e_v_norm", e_v_norm, m_e_v_norm, v_e_v_norm, d_vgain),
           ("e_sgu_w", e_sgu_w, m_e_sgu_w, v_e_sgu_w, d_sgu_w_tril),
           ("e_sgu_b", e_sgu_b, m_e_sgu_b, v_e_sgu_b, d_sgu_b),
           ("e_mla_out_norm", e_mla_out_norm, m_e_mla_out_norm, v_e_mla_out_norm, d_mla_out),
           ("e_sgu_out_norm", e_sgu_out_norm, m_e_sgu_out_norm, v_e_sgu_out_norm, d_sgu_out),
           ("mlp_norm", mlp_norm, m_mlp_norm, v_mlp_norm, jnp.concatenate([d_mlp0, d_mlp1], axis=0)),
           ("final_norm", final_norm, m_final_norm, v_final_norm, d_final)]
    sizes = [int(np.prod(r[1].shape)) for r in rep]
    n_rep = sum(sizes)
    n_all = n_rep + 4 * D + 1
    width = -(-n_all // (8 * LANES)) * LANES
    pad = 8 * width - n_all
    flat = jnp.concatenate([r[4].reshape(-1) for r in rep]
                           + [d_onorm_full.reshape(-1), dconv_full.reshape(-1), loss_part[0, :1],
                              jnp.zeros((pad,), F32)])
    summed = sum_rows8(all_gather_vmem(flat.reshape(8, width), name="gather_small_grads", after=b_r3[0]), 8,
                       name="sum_small_grads").reshape(-1)

    loss = summed[n_rep + 4 * D]

    def pack_rep(i):
        return jnp.concatenate([r[i].reshape(-1) for r in rep]).reshape(n_rep // LANES, LANES)

    g_rep = summed[:n_rep].reshape(n_rep // LANES, LANES)
    d_rep, nm_rep, nv_rep = adam_flat(g_rep, pack_rep(1), pack_rep(2), pack_rep(3), name="adam_replicated")

    def unpack_rep(flat2d):
        out, off = {}, 0
        f = flat2d.reshape(-1)
        for r, n in zip(rep, sizes):
            out[r[0]] = f[off:off + n].reshape(r[1].shape)
            off += n
        return out

    small = {"grad": unpack_rep(g_rep), "delta": unpack_rep(d_rep), "new_m": unpack_rep(nm_rep),
             "new_v": unpack_rep(nv_rep)}
    g_onorm = lax.dynamic_slice(summed[n_rep:n_rep + D].reshape(1, D), (0, me * d_shard), (1, d_shard))
    g_conv = lax.dynamic_slice(summed[n_rep + D:n_rep + 4 * D].reshape(3, D), (0, me * d_shard), (3, d_shard))

    def pack_sharded(norm_part, conv_part):
        return jnp.concatenate([norm_part, conv_part, jnp.zeros((4, d_shard), F32)], axis=0)

    g_sh = pack_sharded(g_onorm, g_conv)
    d_sh, nm_sh, nv_sh = adam_flat(g_sh, pack_sharded(o_norm_mix, o_conv_w[0]), pack_sharded(m_o_norm_mix, m_o_conv_w[0]),
                                   pack_sharded(v_o_norm_mix, v_o_conv_w[0]), name="adam_sharded_small")
    for kind, arr in (("grad", g_sh), ("delta", d_sh), ("new_m", nm_sh), ("new_v", nv_sh)):
        small[kind]["o_norm_mix"] = arr[0:1]
        small[kind]["o_conv_w"] = arr[1:4][None]

    big = {"e_w_in": r_w_in, "e_w_uq": r_w_uq, "e_w_ukv": r_w_ukv, "e_w_out": r_w_out_e, "o_w_in": r_w_in_o,
           "o_w_out": r_w_out_o, "mlp_w1": r_w1, "mlp_w2": r_w2}
    order = ["e_norm_mix", "e_w_in", "e_q_norm", "e_w_uq", "e_kv_norm", "e_w_ukv", "e_v_norm", "e_sgu_w", "e_sgu_b",
             "e_mla_out_norm", "e_sgu_out_norm", "e_w_out", "o_norm_mix", "o_w_in", "o_conv_w", "o_w_out", "mlp_norm",
             "mlp_w1", "mlp_w2", "final_norm"]
    result = [loss, grad_x[None]]
    for ki, kind in enumerate(("grad", "delta", "new_m", "new_v")):
        for nm in order:
            result.append(big[nm][ki] if nm in big else small[kind][nm])
    return tuple(result)
```

```python
import numpy as np
import jax
import jax.numpy as jnp
from jax import lax
from jax.experimental import pallas as pl
from jax.experimental.pallas import tpu as pltpu

BF = jnp.bfloat16
F32 = jnp.float32
MESH = pl.DeviceIdType.MESH
N_DEV = 8

EPS = 1e-6
HEADS = 8
Q_LORA = 512
KV_LORA = 512
QK_NOPE = 128
QK_ROPE = 64
HALF_ROPE = QK_ROPE // 2
V_HEAD = 128
HEAD_PAD = 256
ROPE_BASE = 10000.0
GROUPS = 8
CH = 128
CHUNK = 128
SGU_OUT = GROUPS * CH
MLA_OUT = HEADS * V_HEAD
ATTN_SCALE = float((QK_NOPE + QK_ROPE) ** -0.5)

ADAM_LR = 0.001
ADAM_B1 = 0.9
ADAM_B2 = 0.999
ADAM_EPS = 1e-08
ADAM_WD = 0.01
ADAM_STEP = 10
ADAM_C1 = 1.0 - ADAM_B1 ** ADAM_STEP
ADAM_C2 = 1.0 - ADAM_B2 ** ADAM_STEP

V7X_VMEM_BYTES = 64 * 2 ** 20
VMEM_LIMIT_CAP = V7X_VMEM_BYTES - 6 * 2 ** 20
LANES = 128
ROW_TILE = 256
ATTN_TILE = 512
STREAM_BLOCK_ELEMS = 512 * 1024
MM_TILE = 1024
MM_K_TILE = 2048
MM_K_BLOCK_MAX = 3072


def _padded_bytes(block, dtype):
    dims = [d for d in block if d is not None]
    if len(dims) >= 1:
        dims[-1] = -(-dims[-1] // LANES) * LANES
    if len(dims) >= 2:
        dims[-2] = -(-dims[-2] // 16) * 16
    return int(np.prod(dims)) * jnp.dtype(dtype).itemsize


def _pcall(body, *, name, grid, ins, outs, scratch=(), semantics=None, aliases=None, prefetch=None, deps=()):
    any_spec = pl.BlockSpec(memory_space=pl.ANY)
    if deps:
        n_lead = len(ins) + (1 if prefetch is not None else 0)
        n_deps = len(deps)
        inner = body

        def body(*refs):
            inner(*refs[:n_lead], *refs[n_lead + n_deps:])

        ins = list(ins) + [(d, None, None) for d in deps]
    in_specs = [any_spec if b is None else pl.BlockSpec(b, m) for _, b, m in ins]
    out_specs = [any_spec if b is None else pl.BlockSpec(b, m) for _, _, b, m in outs]
    out_shape = [pltpu.HBM(s, d) for s, d, _, _ in outs]
    est = 0
    for a, b, _ in ins:
        if b is not None:
            est += 2 * _padded_bytes(b, a.dtype)
    for _, d, b, _ in outs:
        if b is not None:
            est += 2 * _padded_bytes(b, d)
    for s in scratch:
        if hasattr(s, "shape") and hasattr(s, "dtype"):
            est += _padded_bytes(s.shape, s.dtype)
    limit = int(min(VMEM_LIMIT_CAP, est + 16 * 2 ** 20))
    params = pltpu.CompilerParams(
        dimension_semantics=semantics or ("arbitrary",) * len(grid), vmem_limit_bytes=limit)
    args = [pltpu.with_memory_space_constraint(a, pltpu.HBM) for a, _, _ in ins]
    if prefetch is not None:
        grid_spec = pltpu.PrefetchScalarGridSpec(
            num_scalar_prefetch=1, grid=grid, in_specs=in_specs, out_specs=out_specs, scratch_shapes=list(scratch))
        call = pl.pallas_call(body, out_shape=out_shape, grid_spec=grid_spec, name=name, compiler_params=params,
                              input_output_aliases=aliases or {})
        return call(prefetch, *args)
    call = pl.pallas_call(body, out_shape=out_shape, grid=grid, in_specs=in_specs, out_specs=out_specs,
                          scratch_shapes=list(scratch), name=name, compiler_params=params,
                          input_output_aliases=aliases or {})
    return call(*args)


def _tile(dim, pref, quantum=LANES):
    if dim <= pref:
        return dim
    t = (pref // quantum) * quantum
    while t >= quantum:
        if dim % t == 0:
            return t
        t -= quantum
    return dim


def _vshape(arr_shape):
    if len(arr_shape) == 2:
        return tuple(arr_shape)
    s, r, c = arr_shape
    return (r, s * c)


def _vblock(arr_shape, br, bc, rc):
    if len(arr_shape) == 2:
        return (br, bc), (lambda *g: rc(*g))
    _, _, c = arr_shape
    assert c % bc == 0, (arr_shape, bc)
    per = c // bc

    def imap(*g):
        ri, ci = rc(*g)
        return (ci // per, ri, ci % per)

    return (None, br, bc), imap


def _shard_width(*shapes):
    w = None
    for s in shapes:
        if len(s) == 3:
            w = s[2] if w is None else int(np.gcd(w, s[2]))
    return w


def mm(a, b, *, name, ta=False, tb=False, out=None, outs=None, epi=None, epi_ins=(), bm=None, bn=None, bk=None,
       deps=()):
    b_parts = list(b) if isinstance(b, (tuple, list)) else None
    if b_parts is not None:
        assert tb and bk is None
        b = b_parts[0]
        av, bv = _vshape(a.shape), (b.shape[0], b.shape[1] * len(b_parts))
    else:
        av, bv = _vshape(a.shape), _vshape(b.shape)
    M, K = (av[1], av[0]) if ta else av
    K2, N = (bv[1], bv[0]) if tb else bv
    assert K == K2, (a.shape, b.shape, ta, tb)
    if outs is None:
        outs = [(out[0], out[1], None)]
    a_sw = _shard_width(a.shape)
    b_sw = _shard_width(b.shape)
    o_sw = _shard_width(*[o[0] for o in outs])
    m_lim = a_sw if (ta and a_sw) else None
    k_lim = [w for w in ((a_sw if not ta else None), (b_sw if tb else None)) if w]
    n_lim = [w for w in ((b_sw if not tb else None), o_sw) if w]
    if bm is None:
        bm = _tile(M, min([MM_TILE] + ([m_lim] if m_lim else [])))
    if bn is None:
        bn = _tile(N, min([MM_TILE] + n_lim))
    k_shards = 0
    if b_parts is not None:
        k_shards, b_sw, bk = len(b_parts), b.shape[1], K
    elif tb and len(b.shape) == 3 and bk is None and not (a_sw and not ta):
        k_shards = 1
        while 2 * k_shards <= b.shape[0] and 2 * k_shards * b_sw <= MM_K_BLOCK_MAX:
            k_shards *= 2
        bk = k_shards * b_sw
    if bk is None:
        bk = K if (K <= 4096 and not k_lim) else _tile(K, min([MM_K_TILE] + k_lim))
    assert M % bm == 0 and N % bn == 0 and K % bk == 0, (name, M, N, K, bm, bn, bk)
    nk = K // bk
    grid = (M // bm, N // bn, nk)
    if ta:
        a_blk, a_map = _vblock(a.shape, bk, bm, lambda i, j, k: (k, i))
    else:
        a_blk, a_map = _vblock(a.shape, bm, bk, lambda i, j, k: (i, k))
    if b_parts is not None:
        b_ins = [(p, (bn, b_sw), lambda i, j, k: (j, 0)) for p in b_parts]
    elif k_shards:
        b_ins = [(b, (k_shards, bn, b_sw), lambda i, j, k: (k, j, 0))]
    elif tb:
        b_ins = [(b, *_vblock(b.shape, bn, bk, lambda i, j, k: (j, k)))]
    else:
        b_ins = [(b, *_vblock(b.shape, bk, bn, lambda i, j, k: (k, j)))]
    dn = (((0 if ta else 1,), (1 if tb else 0,)), ((), ()))
    ins = [(a, a_blk, a_map)] + b_ins + list(epi_ins)
    n_b = len(b_ins)
    out_list = []
    for shape, dtype, cols in outs:
        cols = cols or bn
        blk, imap = _vblock(shape, bm, cols, lambda i, j, k: (i, j))
        out_list.append((shape, dtype, blk, imap))
    n_e, n_o = len(epi_ins), len(out_list)

    def body(*refs):
        a_ref, b_refs = refs[0], refs[1:1 + n_b]
        e_refs = refs[1 + n_b:1 + n_b + n_e]
        o_refs = refs[1 + n_b + n_e:1 + n_b + n_e + n_o]

        def finish(acc):
            res = epi(acc, *e_refs) if epi is not None else (acc,)
            for o_ref, r in zip(o_refs, res):
                o_ref[...] = r.astype(o_ref.dtype)

        x = a_ref[...].astype(BF)
        if k_shards:
            p = None
            for s in range(k_shards):
                y = (b_refs[s][...] if b_parts is not None else b_refs[0][s]).astype(BF)
                part = lax.dot_general(x[:, s * b_sw:(s + 1) * b_sw], y, dn, preferred_element_type=F32)
                p = part if p is None else p + part
        else:
            p = lax.dot_general(x, b_refs[0][...].astype(BF), dn, preferred_element_type=F32)
        if nk == 1:
            finish(p)
        else:
            acc_ref = refs[-1]
            k = pl.program_id(2)

            @pl.when(k == 0)
            def _():
                acc_ref[...] = p

            @pl.when(k > 0)
            def _():
                acc_ref[...] += p

            @pl.when(k == nk - 1)
            def _():
                finish(acc_ref[...])

    scratch = [pltpu.VMEM((bm, bn), F32)] if nk > 1 else []
    res = _pcall(body, name=name, grid=grid, ins=ins, outs=out_list, scratch=scratch,
                 semantics=("parallel", "parallel", "arbitrary"), deps=deps)
    return res[0] if len(res) == 1 else res


_GELU_K = float(np.sqrt(2.0 / np.pi))
_GELU_C = 0.044715


def _gelu(x):
    t = jnp.tanh(_GELU_K * (x + _GELU_C * (x * x * x)))
    return 0.5 * x * (1.0 + t)


def _gelu_grad(x):
    t = jnp.tanh(_GELU_K * (x + _GELU_C * (x * x * x)))
    return 0.5 * (1.0 + t) + 0.5 * x * (1.0 - t * t) * (_GELU_K * (1.0 + 3.0 * _GELU_C * (x * x)))


def _rstd(x):
    return lax.rsqrt(jnp.mean(x * x, axis=-1, keepdims=True) + EPS)


def _rms_bwd(x, gain, dy):
    r = _rstd(x)
    xh = x * r
    gdy = dy * gain
    dx = r * (gdy - xh * jnp.mean(gdy * xh, axis=-1, keepdims=True))
    return dx, dy * xh


def _rope_fwd(x, cos_t, sin_t):
    return x * cos_t + pltpu.roll(x, 2 * HALF_ROPE, 1) * sin_t


def _rope_bwd(dy, cos_t, sin_t):
    return dy * cos_t + pltpu.roll(dy * sin_t, 2 * HALF_ROPE, 1)


def _acc_rows(ref, val, first):
    s = jnp.sum(val, axis=0, keepdims=True)

    @pl.when(first)
    def _():
        ref[...] = s

    @pl.when(jnp.logical_not(first))
    def _():
        ref[...] += s


def rms_fwd(x, gain, *, name, col_block=0, width=None, deps=()):
    T = x.shape[0]
    width = width or x.shape[1]
    tm = _tile(T, ROW_TILE, 8)

    def body(x_ref, g_ref, o_ref):
        v = x_ref[...]
        o_ref[...] = (v * _rstd(v) * g_ref[...]).astype(BF)

    return _pcall(body, name=name, grid=(T // tm,),
                  ins=[(x, (tm, width), lambda i: (i, col_block)), (gain, (1, width), lambda i: (0, 0))],
                  outs=[((T, width), BF, (tm, width), lambda i: (i, 0))], semantics=("parallel",), deps=deps)[0]


def rms_bwd(x, gain, dy, *, name, col_block=0, dres=None, want_f32=True, want_bf=True, deps=()):
    T, width = dy.shape
    tm = _tile(T, ROW_TILE, 8)
    has_res = dres is not None

    def body(*refs):
        x_ref, g_ref, dy_ref = refs[:3]
        pos = 3
        res_ref = None
        if has_res:
            res_ref = refs[pos]
            pos += 1
        outs = refs[pos:]
        dx, dg_rows = _rms_bwd(x_ref[...], g_ref[...], dy_ref[...])
        if has_res:
            dx = dx + res_ref[...]
        o = 0
        if want_f32:
            outs[o][...] = dx
            o += 1
        if want_bf:
            outs[o][...] = dx.astype(BF)
            o += 1
        _acc_rows(outs[o], dg_rows, pl.program_id(0) == 0)

    ins = [(x, (tm, width), lambda i: (i, col_block)), (gain, (1, width), lambda i: (0, 0)),
           (dy, (tm, width), lambda i: (i, 0))]
    if has_res:
        ins.append((dres, (tm, width), lambda i: (i, 0)))
    outs = []
    if want_f32:
        outs.append(((T, width), F32, (tm, width), lambda i: (i, 0)))
    if want_bf:
        outs.append(((T, width), BF, (tm, width), lambda i: (i, 0)))
    outs.append(((1, width), F32, (1, width), lambda i: (0, 0)))
    return _pcall(body, name=name, grid=(T // tm,), ins=ins, outs=outs, deps=deps)


def mla_prep(proj, q_norm, kv_norm, cos_t, sin_t, *, name):
    T = proj.shape[0]
    tm = _tile(T, ROW_TILE, 8)
    kr_block = (proj.shape[1] - LANES) // LANES

    def body(cq_ref, ckv_ref, kr_ref, qg_ref, kg_ref, cos_ref, sin_ref, qn_ref, kvn_ref, krope_ref):
        cq = cq_ref[...]
        qn_ref[...] = (cq * _rstd(cq) * qg_ref[...]).astype(BF)
        ckv = ckv_ref[...]
        kvn_ref[...] = (ckv * _rstd(ckv) * kg_ref[...]).astype(BF)
        krope_ref[...] = _rope_fwd(kr_ref[...], cos_ref[...], sin_ref[...]).astype(BF)

    return _pcall(
        body, name=name, grid=(T // tm,),
        ins=[(proj, (tm, Q_LORA), lambda i: (i, 0)), (proj, (tm, KV_LORA), lambda i: (i, 1)),
             (proj, (tm, LANES), lambda i: (i, kr_block)),
             (q_norm, (1, Q_LORA), lambda i: (0, 0)), (kv_norm, (1, KV_LORA), lambda i: (0, 0)),
             (cos_t, (tm, LANES), lambda i: (i, 0)), (sin_t, (tm, LANES), lambda i: (i, 0))],
        outs=[((T, Q_LORA), BF, (tm, Q_LORA), lambda i: (i, 0)), ((T, KV_LORA), BF, (tm, KV_LORA), lambda i: (i, 0)),
              ((T, LANES), BF, (tm, LANES), lambda i: (i, 0))],
        semantics=("parallel",))


def _attn_scores(q, k_blk, diagonal):
    s = lax.dot_general(q, k_blk, (((1,), (1,)), ((), ())), preferred_element_type=F32) * ATTN_SCALE
    if diagonal:
        row = lax.broadcasted_iota(jnp.int32, s.shape, 0)
        col = lax.broadcasted_iota(jnp.int32, s.shape, 1)
        s = jnp.where(col <= row, s, -jnp.inf)
    return s


def attn_fwd(q, k, v, *, name):
    T = q.shape[0]
    tq = _tile(T, ATTN_TILE, 8)

    def body(q_ref, k_ref, v_ref, o_ref, lse_ref):
        i = pl.program_id(1)
        qv = q_ref[...]

        def block(kb, carry, diagonal):
            m, l, acc = carry
            start = pl.multiple_of(kb * tq, tq)
            s = _attn_scores(qv, k_ref[pl.ds(start, tq), :], diagonal)
            m_new = jnp.maximum(m, jnp.max(s, axis=-1, keepdims=True))
            alpha = jnp.exp(m - m_new)
            p = jnp.exp(s - m_new)
            l = alpha * l + jnp.sum(p, axis=-1, keepdims=True)
            acc = alpha * acc + jnp.dot(p.astype(BF), v_ref[pl.ds(start, tq), :], preferred_element_type=F32)
            return m_new, l, acc

        init = (jnp.full((tq, 1), -jnp.inf, F32), jnp.zeros((tq, 1), F32), jnp.zeros((tq, V_HEAD), F32))
        carry = lax.fori_loop(0, i, lambda kb, c: block(kb, c, False), init)
        m, l, acc = block(i, carry, True)
        o_ref[...] = acc / l
        lse_ref[...] = jnp.broadcast_to(m + jnp.log(l), (tq, V_HEAD))

    return _pcall(
        body, name=name, grid=(HEADS, T // tq),
        ins=[(q, (tq, HEAD_PAD), lambda h, i: (i, h)), (k, (T, HEAD_PAD), lambda h, i: (0, h)),
             (v, (T, V_HEAD), lambda h, i: (0, h))],
        outs=[((T, MLA_OUT), F32, (tq, V_HEAD), lambda h, i: (i, h)),
              ((T, MLA_OUT), F32, (tq, V_HEAD), lambda h, i: (i, h))], semantics=("parallel", "parallel"))


def attn_bwd(q, k, v, o, lse, do, *, name):
    T = q.shape[0]
    tq = _tile(T, ATTN_TILE, 8)

    def body(q_ref, k_ref, v_ref, o_ref, lse_ref, do_ref, dq_ref, dk_ref, dv_ref):
        i = pl.program_id(1)

        @pl.when(i == 0)
        def _():
            dk_ref[...] = jnp.zeros_like(dk_ref)
            dv_ref[...] = jnp.zeros_like(dv_ref)

        qv = q_ref[...]
        do_t = do_ref[...]
        lse_v = lse_ref[:, 0:1]
        delta = jnp.sum(do_t.astype(F32) * o_ref[...], axis=-1, keepdims=True)

        def block(kb, dq, diagonal):
            start = pl.multiple_of(kb * tq, tq)
            k_blk = k_ref[pl.ds(start, tq), :]
            v_blk = v_ref[pl.ds(start, tq), :]
            p = jnp.exp(_attn_scores(qv, k_blk, diagonal) - lse_v)
            dp = lax.dot_general(do_t, v_blk, (((1,), (1,)), ((), ())), preferred_element_type=F32)
            ds = (p * (dp - delta) * ATTN_SCALE).astype(BF)
            dk_ref[pl.ds(start, tq), :] += lax.dot_general(ds, qv, (((0,), (0,)), ((), ())), preferred_element_type=F32)
            dv_ref[pl.ds(start, tq), :] += lax.dot_general(p.astype(BF), do_t, (((0,), (0,)), ((), ())),
                                                          preferred_element_type=F32)
            return dq + jnp.dot(ds, k_blk, preferred_element_type=F32)

        dq = lax.fori_loop(0, i, lambda kb, c: block(kb, c, False), jnp.zeros((tq, HEAD_PAD), F32))
        dq_ref[...] = block(i, dq, True)

    return _pcall(
        body, name=name, grid=(HEADS, T // tq),
        ins=[(q, (tq, HEAD_PAD), lambda h, i: (i, h)), (k, (T, HEAD_PAD), lambda h, i: (0, h)),
             (v, (T, V_HEAD), lambda h, i: (0, h)), (o, (tq, V_HEAD), lambda h, i: (i, h)),
             (lse, (tq, V_HEAD), lambda h, i: (i, h)), (do, (tq, V_HEAD), lambda h, i: (i, h))],
        outs=[((T, HEADS * HEAD_PAD), F32, (tq, HEAD_PAD), lambda h, i: (i, h)),
              ((T, HEADS * HEAD_PAD), F32, (T, HEAD_PAD), lambda h, i: (0, h)),
              ((T, MLA_OUT), F32, (T, V_HEAD), lambda h, i: (0, h))],
        semantics=("parallel", "arbitrary"))


def mla_bwd_prep(dq, dk, dv, cos_t, sin_t, *, name):
    T = dq.shape[0]
    tm = _tile(T, ROW_TILE, 8)

    def body(dq_ref, dk_ref, dv_ref, cos_ref, sin_ref, dql_ref, dkvl_ref, dkr_ref):
        cos_v, sin_v = cos_ref[...], sin_ref[...]
        kr = jnp.zeros((tm, LANES), F32)
        for h in range(HEADS):
            lo = h * HEAD_PAD
            dql_ref[:, lo:lo + QK_NOPE] = dq_ref[:, lo:lo + QK_NOPE].astype(BF)
            dql_ref[:, lo + QK_NOPE:lo + HEAD_PAD] = _rope_bwd(
                dq_ref[:, lo + QK_NOPE:lo + HEAD_PAD], cos_v, sin_v).astype(BF)
            dkvl_ref[:, lo:lo + QK_NOPE] = dk_ref[:, lo:lo + QK_NOPE].astype(BF)
            dkvl_ref[:, lo + QK_NOPE:lo + HEAD_PAD] = dv_ref[:, h * V_HEAD:(h + 1) * V_HEAD].astype(BF)
            kr = kr + dk_ref[:, lo + QK_NOPE:lo + HEAD_PAD]
        dkr_ref[...] = _rope_bwd(kr, cos_v, sin_v).astype(BF)

    W = HEADS * HEAD_PAD
    return _pcall(
        body, name=name, grid=(T // tm,),
        ins=[(dq, (tm, W), lambda i: (i, 0)), (dk, (tm, W), lambda i: (i, 0)), (dv, (tm, MLA_OUT), lambda i: (i, 0)),
             (cos_t, (tm, LANES), lambda i: (i, 0)), (sin_t, (tm, LANES), lambda i: (i, 0))],
        outs=[((T, W), BF, (tm, W), lambda i: (i, 0)), ((T, W), BF, (tm, W), lambda i: (i, 0)),
              ((T, LANES), BF, (tm, LANES), lambda i: (i, 0))],
        semantics=("parallel",))


def _group_norm_stats(vg):
    mu = jnp.mean(vg, axis=-1, keepdims=True)
    d = vg - mu
    r = lax.rsqrt(jnp.mean(d * d, axis=-1, keepdims=True) + EPS)
    return d * r, r


def mix_fwd(a, proj, g_mla, g_sgu, v_gain, w_tril, b_full, *, name):
    T = a.shape[0]
    tm = _tile(T, ROW_TILE, CHUNK)
    n_chunk = tm // CHUNK

    def body(a_ref, u_ref, v_ref, gm_ref, gs_ref, vg_ref, w_ref, b_ref, o_ref, s_scr):
        av = a_ref[...]
        o_ref[:, :MLA_OUT] = (av * _rstd(av) * gm_ref[...]).astype(BF)
        for g in range(GROUPS):
            sl = slice(g * CH, (g + 1) * CH)
            vhat, _ = _group_norm_stats(_gelu(v_ref[:, sl]))
            vn = (vhat * vg_ref[:, sl]).astype(BF)
            u = _gelu(u_ref[:, sl])
            for ci in range(n_chunk):
                rs = slice(ci * CHUNK, (ci + 1) * CHUNK)
                y = jnp.dot(w_ref[g], vn[rs], preferred_element_type=F32) + b_ref[:, sl]
                s_scr[rs, sl] = u[rs] * y
        s = s_scr[...]
        o_ref[:, MLA_OUT:] = (s * _rstd(s) * gs_ref[...]).astype(BF)

    return _pcall(
        body, name=name, grid=(T // tm,),
        ins=[(a, (tm, MLA_OUT), lambda i: (i, 0)), (proj, (tm, SGU_OUT), lambda i: (i, 1)),
             (proj, (tm, SGU_OUT), lambda i: (i, 2)), (g_mla, (1, MLA_OUT), lambda i: (0, 0)),
             (g_sgu, (1, SGU_OUT), lambda i: (0, 0)), (v_gain, (1, SGU_OUT), lambda i: (0, 0)),
             (w_tril, (GROUPS, CHUNK, CHUNK), lambda i: (0, 0, 0)), (b_full, (CHUNK, SGU_OUT), lambda i: (0, 0))],
        outs=[((T, MLA_OUT + SGU_OUT), BF, (tm, MLA_OUT + SGU_OUT), lambda i: (i, 0))],
        scratch=[pltpu.VMEM((tm, SGU_OUT), F32)], semantics=("parallel",))[0]


def mix_bwd(dmixed, a, proj, g_mla, g_sgu, v_gain, w_tril, w_tril_t, b_full, *, name):
    T = a.shape[0]
    tm = _tile(T, ROW_TILE, CHUNK)
    n_chunk = tm // CHUNK

    def body(dm_a_ref, dm_s_ref, a_ref, u_ref, v_ref, gm_ref, gs_ref, vg_ref, w_ref, wt_ref, b_ref,
             da_ref, duv_ref, dgm_ref, dgs_ref, dvg_ref, dw_ref, db_ref, s_scr, y_scr):
        first = pl.program_id(0) == 0
        da, dgm_rows = _rms_bwd(a_ref[...], gm_ref[...], dm_a_ref[...])
        da_ref[...] = da.astype(BF)
        _acc_rows(dgm_ref, dgm_rows, first)

        for g in range(GROUPS):
            sl = slice(g * CH, (g + 1) * CH)
            vhat, _ = _group_norm_stats(_gelu(v_ref[:, sl]))
            vn = (vhat * vg_ref[:, sl]).astype(BF)
            u = _gelu(u_ref[:, sl])
            for ci in range(n_chunk):
                rs = slice(ci * CHUNK, (ci + 1) * CHUNK)
                y = jnp.dot(w_ref[g], vn[rs], preferred_element_type=F32) + b_ref[:, sl]
                y_scr[rs, sl] = y
                s_scr[rs, sl] = u[rs] * y
        ds, dgs_rows = _rms_bwd(s_scr[...], gs_ref[...], dm_s_ref[...])
        _acc_rows(dgs_ref, dgs_rows, first)
        s_scr[...] = ds

        @pl.when(first)
        def _():
            dw_ref[...] = jnp.zeros_like(dw_ref)
            db_ref[...] = jnp.zeros_like(db_ref)

        for g in range(GROUPS):
            sl = slice(g * CH, (g + 1) * CH)
            upre = u_ref[:, sl]
            vpre = v_ref[:, sl]
            u = _gelu(upre)
            vhat, r = _group_norm_stats(_gelu(vpre))
            gain = vg_ref[:, sl]
            vn = (vhat * gain).astype(BF)
            dsg = s_scr[:, sl]
            duv_ref[:, sl] = (dsg * y_scr[:, sl] * _gelu_grad(upre)).astype(BF)
            dy = dsg * u
            dyb = dy.astype(BF)
            dvn_parts = []
            for ci in range(n_chunk):
                rs = slice(ci * CHUNK, (ci + 1) * CHUNK)
                dvn_parts.append(jnp.dot(wt_ref[g], dyb[rs], preferred_element_type=F32))
                dw_ref[g] += lax.dot_general(dyb[rs], vn[rs], (((1,), (1,)), ((), ())), preferred_element_type=F32)
                db_ref[:, sl] += jnp.broadcast_to(jnp.sum(dy[rs], axis=-1, keepdims=True), (CHUNK, CH))
            dvn = dvn_parts[0] if n_chunk == 1 else jnp.concatenate(dvn_parts, axis=0)
            _acc_rows(dvg_ref.at[:, sl], dvn * vhat, first)
            dvh = dvn * gain
            dvg = r * (dvh - jnp.mean(dvh, axis=-1, keepdims=True)
                       - vhat * jnp.mean(dvh * vhat, axis=-1, keepdims=True))
            duv_ref[:, SGU_OUT + g * CH:SGU_OUT + (g + 1) * CH] = (dvg * _gelu_grad(vpre)).astype(BF)

    return _pcall(
        body, name=name, grid=(T // tm,),
        ins=[(dmixed, (tm, MLA_OUT), lambda i: (i, 0)), (dmixed, (tm, SGU_OUT), lambda i: (i, 1)),
             (a, (tm, MLA_OUT), lambda i: (i, 0)), (proj, (tm, SGU_OUT), lambda i: (i, 1)),
             (proj, (tm, SGU_OUT), lambda i: (i, 2)), (g_mla, (1, MLA_OUT), lambda i: (0, 0)),
             (g_sgu, (1, SGU_OUT), lambda i: (0, 0)), (v_gain, (1, SGU_OUT), lambda i: (0, 0)),
             (w_tril, (GROUPS, CHUNK, CHUNK), lambda i: (0, 0, 0)), (w_tril_t, (GROUPS, CHUNK, CHUNK), lambda i: (0, 0, 0)),
             (b_full, (CHUNK, SGU_OUT), lambda i: (0, 0))],
        outs=[((T, MLA_OUT), BF, (tm, MLA_OUT), lambda i: (i, 0)),
              ((T, 2 * SGU_OUT), BF, (tm, 2 * SGU_OUT), lambda i: (i, 0)),
              ((1, MLA_OUT), F32, (1, MLA_OUT), lambda i: (0, 0)), ((1, SGU_OUT), F32, (1, SGU_OUT), lambda i: (0, 0)),
              ((1, SGU_OUT), F32, (1, SGU_OUT), lambda i: (0, 0)),
              ((GROUPS, CHUNK, CHUNK), F32, (GROUPS, CHUNK, CHUNK), lambda i: (0, 0, 0)),
              ((CHUNK, SGU_OUT), F32, (CHUNK, SGU_OUT), lambda i: (0, 0))],
        scratch=[pltpu.VMEM((tm, SGU_OUT), F32), pltpu.VMEM((tm, SGU_OUT), F32)])


def _shift_down(z, n, row):
    return jnp.where(row >= n, pltpu.roll(z, n, 0), 0.0)


def _shift_up(z, n, row, T):
    return jnp.where(row < T - n, pltpu.roll(z, T - n, 0), 0.0)


def conv_fwd(proj, conv_w, *, name):
    T, D3 = proj.shape
    D = D3 // 3
    tn = _tile(D, 256)
    nj = D // tn

    def body(b_ref, c_ref, x_ref, w_ref, o_ref):
        row = lax.broadcasted_iota(jnp.int32, (T, tn), 0)
        z = c_ref[...] * x_ref[...]
        zc = w_ref[2:3, :] * z + w_ref[1:2, :] * _shift_down(z, 1, row) + w_ref[0:1, :] * _shift_down(z, 2, row)
        o_ref[...] = (b_ref[...] * zc).astype(BF)

    return _pcall(
        body, name=name, grid=(nj,),
        ins=[(proj, (T, tn), lambda j: (0, j)), (proj, (T, tn), lambda j: (0, nj + j)),
             (proj, (T, tn), lambda j: (0, 2 * nj + j)), (conv_w, (3, tn), lambda j: (0, j))],
        outs=[((T, D), BF, (T, tn), lambda j: (0, j))], semantics=("parallel",))[0]


def conv_bwd(dg, proj, conv_w, *, name):
    T, D3 = proj.shape
    D = D3 // 3
    tn = _tile(D, 256)
    nj = D // tn

    def body(dg_ref, b_ref, c_ref, x_ref, w_ref, dp_ref, dw_ref, dc_scr, dx_scr):
        part = pl.program_id(1)

        @pl.when(part == 0)
        def _():
            row = lax.broadcasted_iota(jnp.int32, (T, tn), 0)
            c, x = c_ref[...], x_ref[...]
            z = c * x
            z1 = _shift_down(z, 1, row)
            z2 = _shift_down(z, 2, row)
            dgv = dg_ref[...]
            zc = w_ref[2:3, :] * z + w_ref[1:2, :] * z1 + w_ref[0:1, :] * z2
            dp_ref[...] = (dgv * zc).astype(BF)
            dzc = dgv * b_ref[...]
            dw_ref[0:1, :] = jnp.sum(dzc * z2, axis=0, keepdims=True)
            dw_ref[1:2, :] = jnp.sum(dzc * z1, axis=0, keepdims=True)
            dw_ref[2:3, :] = jnp.sum(dzc * z, axis=0, keepdims=True)
            dz = (w_ref[2:3, :] * dzc + w_ref[1:2, :] * _shift_up(dzc, 1, row, T)
                  + w_ref[0:1, :] * _shift_up(dzc, 2, row, T))
            dc_scr[...] = (dz * x).astype(BF)
            dx_scr[...] = (dz * c).astype(BF)

        @pl.when(part == 1)
        def _():
            dp_ref[...] = dc_scr[...]

        @pl.when(part == 2)
        def _():
            dp_ref[...] = dx_scr[...]

    return _pcall(
        body, name=name, grid=(nj, 3),
        ins=[(dg, (T, tn), lambda j, p: (0, j)), (proj, (T, tn), lambda j, p: (0, j)),
             (proj, (T, tn), lambda j, p: (0, nj + j)), (proj, (T, tn), lambda j, p: (0, 2 * nj + j)),
             (conv_w, (3, tn), lambda j, p: (0, j))],
        outs=[((T, D3), BF, (T, tn), lambda j, p: (0, p * nj + j)), ((3, D), F32, (3, tn), lambda j, p: (0, j))],
        scratch=[pltpu.VMEM((T, tn), BF), pltpu.VMEM((T, tn), BF)], semantics=("parallel", "arbitrary"))


def loss_bwd(x_parts, gain, target, *, name):
    T, D = target.shape
    tm = _tile(T, ROW_TILE, 8)
    n_x = len(x_parts)

    def body(*refs):
        x_refs = refs[:n_x]
        g_ref, t_ref, dx_ref, dxb_ref, dg_ref, loss_ref = refs[n_x:]
        first = pl.program_id(0) == 0
        xv = jnp.concatenate([r[...] for r in x_refs], axis=-1) if n_x > 1 else x_refs[0][...]
        r = _rstd(xv)
        xh = xv * r
        gain_v = g_ref[...]
        err = xh * gain_v - t_ref[...]
        part = 0.5 * jnp.sum(jnp.mean(err * err, axis=-1, keepdims=True), axis=0, keepdims=True)
        _acc_rows(loss_ref, jnp.broadcast_to(part, (1, LANES)), first)
        dy = err * (1.0 / D)
        gdy = dy * gain_v
        dx = r * (gdy - xh * jnp.mean(gdy * xh, axis=-1, keepdims=True))
        dx_ref[...] = dx
        dxb_ref[...] = dx.astype(BF)
        _acc_rows(dg_ref, dy * xh, first)

    return _pcall(
        body, name=name, grid=(T // tm,),
        ins=[(p, (tm, D // n_x), lambda i: (i, 0)) for p in x_parts]
        + [(gain, (1, D), lambda i: (0, 0)), (target, (tm, D), lambda i: (i, 0))],
        outs=[((T, D), F32, (tm, D), lambda i: (i, 0)), ((T, D), BF, (tm, D), lambda i: (i, 0)),
              ((1, D), F32, (1, D), lambda i: (0, 0)), ((1, LANES), F32, (1, LANES), lambda i: (0, 0))])


def _adamw(g, w, m, v):
    m = ADAM_B1 * m + (1.0 - ADAM_B1) * g
    v = ADAM_B2 * v + (1.0 - ADAM_B2) * (g * g)
    m_hat = m / ADAM_C1
    v_hat = v / ADAM_C2
    delta = -ADAM_LR * (m_hat / (jnp.sqrt(v_hat) + ADAM_EPS) + ADAM_WD * w)
    return delta, m, v


def adam_flat(g, w, m, v, *, name):
    def body(g_ref, w_ref, m_ref, v_ref, d_ref, nm_ref, nv_ref):
        d, nm, nv = _adamw(g_ref[...], w_ref[...], m_ref[...], v_ref[...])
        d_ref[...] = d
        nm_ref[...] = nm
        nv_ref[...] = nv

    blk = g.shape
    zero = lambda: (0, 0)
    return _pcall(body, name=name, grid=(),
                  ins=[(t, blk, zero) for t in (g, w, m, v)],
                  outs=[(blk, F32, blk, zero)] * 3)


def _chip_slots():
    x, y, c = lax.axis_index("x"), lax.axis_index("y"), lax.axis_index("c")
    chips = [(1 - x, y), (x, 1 - y), (1 - x, 1 - y)]
    return x, y, c, chips


def reduce_adam(gs, a_buf, b_buf, w, m, v, layer, prev, *, name, deps=()):
    L, R, C = w.shape
    tr = _tile(R, max(16, STREAM_BLOCK_ELEMS // C), 16)
    x, y, c, _ = _chip_slots()
    idx = jnp.stack([4 * x + 2 * y + c, 2 * x + y]).astype(jnp.int32)
    n_prev = 0 if prev is None else 4

    def body(idx_ref, g_ref, a_ref, b0_ref, b1_ref, b2_ref, w_ref, m_ref, v_ref, *rest):
        outs = rest[n_prev:]
        g = ((((g_ref[...].astype(F32) + a_ref[...].astype(F32)) + b0_ref[...].astype(F32))
              + b1_ref[...].astype(F32)) + b2_ref[...].astype(F32))
        d, nm, nv = _adamw(g, w_ref[...], m_ref[...], v_ref[...])
        outs[0][...] = g
        outs[1][...] = d
        outs[2][...] = nm
        outs[3][...] = nv

    blk3 = (None, tr, C)
    ins = [(gs, blk3, lambda i, s: (s[0], i, 0)), (a_buf, blk3, lambda i, s: (s[1], i, 0)),
           (b_buf, blk3, lambda i, s: (0, i, 0)), (b_buf, blk3, lambda i, s: (1, i, 0)),
           (b_buf, blk3, lambda i, s: (2, i, 0)),
           (w, blk3, lambda i, s: (layer, i, 0)), (m, blk3, lambda i, s: (layer, i, 0)),
           (v, blk3, lambda i, s: (layer, i, 0))]
    aliases = {}
    if prev is not None:
        for o, p in enumerate(prev):
            ins.append((p, None, None))
            aliases[1 + 8 + o] = o
    outs = [((L, R, C), F32, blk3, lambda i, s: (layer, i, 0))] * 4
    return _pcall(body, name=name, grid=(R // tr,), ins=ins, outs=outs, prefetch=idx, aliases=aliases,
                  semantics=("parallel",), deps=deps)


def reduce_sum(gs, a_buf, b_buf, *, name):
    _, R, C = gs.shape
    tr = _tile(R, 256, 16)
    x, y, c, _ = _chip_slots()
    idx = jnp.stack([4 * x + 2 * y + c, 2 * x + y]).astype(jnp.int32)

    def body(idx_ref, g_ref, a_ref, b0_ref, b1_ref, b2_ref, o_ref):
        o_ref[...] = ((((g_ref[...].astype(F32) + a_ref[...].astype(F32)) + b0_ref[...].astype(F32))
                       + b1_ref[...].astype(F32)) + b2_ref[...].astype(F32))

    blk3 = (None, tr, C)
    return _pcall(body, name=name, grid=(R // tr,),
                  ins=[(gs, blk3, lambda i, s: (s[0], i, 0)), (a_buf, blk3, lambda i, s: (s[1], i, 0)),
                       (b_buf, blk3, lambda i, s: (0, i, 0)), (b_buf, blk3, lambda i, s: (1, i, 0)),
                       (b_buf, blk3, lambda i, s: (2, i, 0))],
                  outs=[((R, C), F32, (tr, C), lambda i, s: (i, 0))], prefetch=idx, semantics=("parallel",))[0]


def adam_rows(g, w, m, v, *, name):
    R, C = g.shape
    tr = _tile(R, 256, 8)

    def body(g_ref, w_ref, m_ref, v_ref, d_ref, nm_ref, nv_ref):
        d, nm, nv = _adamw(g_ref[...], w_ref[...], m_ref[...], v_ref[...])
        d_ref[...] = d
        nm_ref[...] = nm
        nv_ref[...] = nv

    spec = ((tr, C), lambda i: (i, 0))
    return _pcall(body, name=name, grid=(R // tr,), ins=[(t, *spec) for t in (g, w, m, v)],
                  outs=[((R, C), F32, *spec)] * 3, semantics=("parallel",))


def pair_sum(gs, a_buf, *, name):
    _, R, C = gs.shape
    tr = _tile(R, max(16, 2 * STREAM_BLOCK_ELEMS // C), 16)
    x, y, c, chips = _chip_slots()
    idx = jnp.stack([4 * cx + 2 * cy + c for cx, cy in chips] + [2 * cx + cy for cx, cy in chips]).astype(jnp.int32)

    def body(idx_ref, g_ref, a_ref, o_ref):
        o_ref[...] = (g_ref[...].astype(F32) + a_ref[...].astype(F32)).astype(BF)

    blk3 = (None, tr, C)
    return _pcall(body, name=name, grid=(3, R // tr),
                  ins=[(gs, blk3, lambda j, i, s: (s[j], i, 0)), (a_buf, blk3, lambda j, i, s: (s[3 + j], i, 0))],
                  outs=[((3, R, C), BF, blk3, lambda j, i, s: (j, i, 0))], prefetch=idx,
                  semantics=("parallel", "parallel"))[0]


def sum_rows8(gathered, rows, *, name):
    W = gathered.shape[1]

    def body(g_ref, o_ref):
        acc = g_ref[0:rows, :]
        for d in range(1, N_DEV):
            acc = acc + g_ref[d * rows:(d + 1) * rows, :]
        o_ref[...] = acc

    return _pcall(body, name=name, grid=(), ins=[(gathered, gathered.shape, lambda: (0, 0))],
                  outs=[((rows, W), F32, (rows, W), lambda: (0, 0))])[0]


HBM_SPEC = pl.BlockSpec(memory_space=pltpu.HBM)
SEM_SPEC = pl.BlockSpec(memory_space=pltpu.SEMAPHORE)
ANY_SPEC = pl.BlockSpec(memory_space=pl.ANY)
DATAFLOW = pltpu.SideEffectType.DATAFLOW_SIDE_EFFECTING


def _in_hbm(v):
    return pltpu.with_memory_space_constraint(v, pltpu.HBM)


def _slot(p):
    return 4 * p[0] + 2 * p[1] + p[2]


def _gather_peers():
    x, y, c, chips = _chip_slots()
    return (x, y, c), [(x, y, 1 - c)] + [(*chip, c) for chip in chips]


def gather_start(groups, after, *, name):
    flat = [s for g in groups for s in g]
    n, n_g = len(flat), len(groups)
    where = [(gi, ti) for gi, g in enumerate(groups) for ti in range(len(g))]

    def body(*refs):
        src, land = refs[:n], refs[n:2 * n]
        sems = refs[2 * n + 1:2 * n + 1 + 2 * n_g]
        me, peers = _gather_peers()
        for t in range(n):
            gi, ti = where[t]
            for k, to in enumerate(peers):
                pltpu.make_async_remote_copy(
                    src_ref=src[t], dst_ref=land[t].at[_slot(me)], send_sem=sems[2 * gi].at[4 * ti + k],
                    recv_sem=sems[2 * gi + 1].at[4 * ti + k], device_id=to, device_id_type=MESH).start()
        refs[-1][...] = jnp.zeros_like(refs[-1])

    out_shape = []
    for g in groups:
        out_shape += [pltpu.SemaphoreType.DMA((4 * len(g),)), pltpu.SemaphoreType.DMA((4 * len(g),))]
    out_shape += [pltpu.HBM(s.shape, s.dtype) for s in flat]
    out_shape += [pltpu.HBM((N_DEV,) + s.shape, s.dtype) for s in flat]
    out_shape += [jax.ShapeDtypeStruct((8, LANES), F32)]
    aliases = {t: 2 * n_g + t for t in range(n)}
    aliases.update({n + t: 2 * n_g + n + t for t in range(n)})
    res = pl.pallas_call(
        body, name=name, out_shape=out_shape, in_specs=[HBM_SPEC] * (2 * n) + [ANY_SPEC],
        out_specs=[SEM_SPEC] * (2 * n_g) + [HBM_SPEC] * (2 * n) + [pl.BlockSpec(memory_space=pltpu.VMEM)],
        input_output_aliases=aliases, compiler_params=pltpu.CompilerParams(has_side_effects=DATAFLOW),
    )(*[_in_hbm(s) for s in flat], *[_in_hbm(lax.empty((N_DEV,) + s.shape, s.dtype)) for s in flat], after)
    out, off = [], 0
    for gi, g in enumerate(groups):
        k = len(g)
        out.append((res[2 * gi], res[2 * gi + 1], res[2 * n_g + off:2 * n_g + off + k],
                    res[2 * n_g + n + off:2 * n_g + n + off + k]))
        off += k
    return out, res[-1]


def gather_wait(started, after, *, name):
    send_sems, recv_sems, srcs, lands = started
    n = len(srcs)
    after = list(after)

    def body(*refs):
        src, land = refs[:n], refs[n:2 * n]
        send, recv = refs[2 * n], refs[2 * n + 1]
        _, peers = _gather_peers()
        for t in range(n):
            for k, frm in enumerate(peers):
                cp = pltpu.make_async_remote_copy(
                    src_ref=src[t], dst_ref=land[t].at[_slot(frm)], send_sem=send.at[4 * t + k],
                    recv_sem=recv.at[4 * t + k],
                    device_id=frm, device_id_type=MESH)
                cp.wait_send()
                cp.wait_recv()

    res = pl.pallas_call(
        body, name=name,
        out_shape=[pltpu.HBM(s.shape, s.dtype) for s in srcs] + [pltpu.HBM(l.shape, l.dtype) for l in lands],
        in_specs=[HBM_SPEC] * (2 * n) + [SEM_SPEC, SEM_SPEC] + [ANY_SPEC] * len(after),
        out_specs=[HBM_SPEC] * (2 * n), input_output_aliases={t: t for t in range(2 * n)},
        compiler_params=pltpu.CompilerParams(has_side_effects=DATAFLOW),
    )(*srcs, *lands, send_sems, recv_sems, *after)
    return res[:n], res[n:]


def place_own(src, land, *, name):
    R, C = src.shape
    tr = _tile(R, 512, 16)
    x, y, c, _ = _chip_slots()
    idx = jnp.stack([4 * x + 2 * y + c]).astype(jnp.int32)

    def body(idx_ref, s_ref, land_ref, o_ref):
        o_ref[...] = s_ref[...]

    return _pcall(body, name=name, grid=(R // tr,),
                  ins=[(src, (tr, C), lambda i, s: (i, 0)), (land, None, None)],
                  outs=[(land.shape, land.dtype, (None, tr, C), lambda i, s: (s[0], i, 0))],
                  prefetch=idx, aliases={2: 0}, semantics=("parallel",))[0]


def gather_finish(srcs, lands, *, name):
    n = len(srcs)

    def body(*refs):
        land = refs[n:2 * n]
        send_sems, recv_sems = refs[2 * n:]
        x, y, c, chips = _chip_slots()
        me, sibling = (x, y, c), (x, y, 1 - c)

        def copy(t, j, block, to):
            return pltpu.make_async_remote_copy(
                src_ref=land[t].at[_slot(block)], dst_ref=land[t].at[_slot(block)], send_sem=send_sems.at[t, j],
                recv_sem=recv_sems.at[t, j], device_id=to, device_id_type=MESH)

        sends = [copy(t, j, (*chip, c), sibling) for t in range(n) for j, chip in enumerate(chips)]
        for cp in sends:
            cp.start()
        for t in range(n):
            for j, chip in enumerate(chips):
                copy(t, j, (*chip, 1 - c), me).wait_recv()
        for cp in sends:
            cp.wait_send()

    passed = pl.pallas_call(
        body, name=name, out_shape=[jax.ShapeDtypeStruct(l.shape, l.dtype) for l in lands],
        in_specs=[ANY_SPEC] * n, out_specs=[ANY_SPEC] * n,
        input_output_aliases={t: t for t in range(n)},
        scratch_shapes=[pltpu.SemaphoreType.DMA((n, 3)), pltpu.SemaphoreType.DMA((n, 3))],
    )(*lands)
    return [place_own(s, l, name=f"{name}_own{t}") for t, (s, l) in enumerate(zip(srcs, passed))]


def chips_start(pairs, *, name):
    n = len(pairs)

    def body(*refs):
        src, land = refs[:n], refs[n:2 * n]
        send, recv = refs[2 * n], refs[2 * n + 1]
        token = refs[-1]
        x, y, c, chips = _chip_slots()
        for t in range(n):
            for j, chip in enumerate(chips):
                pltpu.make_async_remote_copy(
                    src_ref=src[t].at[j], dst_ref=land[t].at[j], send_sem=send.at[3 * t + j],
                    recv_sem=recv.at[3 * t + j], device_id=(*chip, c), device_id_type=MESH).start()
        token[...] = jnp.zeros_like(token)

    res = pl.pallas_call(
        body, name=name,
        out_shape=[pltpu.SemaphoreType.DMA((3 * n,)), pltpu.SemaphoreType.DMA((3 * n,))]
        + [pltpu.HBM(p.shape, p.dtype) for p in pairs] * 2 + [jax.ShapeDtypeStruct((8, LANES), F32)],
        in_specs=[HBM_SPEC] * (2 * n),
        out_specs=[SEM_SPEC, SEM_SPEC] + [HBM_SPEC] * (2 * n) + [pl.BlockSpec(memory_space=pltpu.VMEM)],
        input_output_aliases={t: 2 + t for t in range(2 * n)},
        compiler_params=pltpu.CompilerParams(has_side_effects=DATAFLOW),
    )(*[_in_hbm(p) for p in pairs], *[_in_hbm(lax.empty(p.shape, p.dtype)) for p in pairs])
    return res[0], res[1], res[2:2 + n], res[2 + n:2 + 2 * n], res[-1]


def chips_wait(started, after, *, name):
    send_sems, recv_sems, srcs, lands, _ = started
    n = len(srcs)

    def body(*refs):
        src, land = refs[:n], refs[n:2 * n]
        send, recv = refs[2 * n], refs[2 * n + 1]
        x, y, c, chips = _chip_slots()
        for t in range(n):
            for j, chip in enumerate(chips):
                cp = pltpu.make_async_remote_copy(
                    src_ref=src[t].at[j], dst_ref=land[t].at[j], send_sem=send.at[3 * t + j],
                    recv_sem=recv.at[3 * t + j], device_id=(*chip, c), device_id_type=MESH)
                cp.wait_send()
                cp.wait_recv()

    res = pl.pallas_call(
        body, name=name, out_shape=[pltpu.HBM(s.shape, s.dtype) for s in srcs] * 2,
        in_specs=[HBM_SPEC] * (2 * n) + [SEM_SPEC, SEM_SPEC, ANY_SPEC], out_specs=[HBM_SPEC] * (2 * n),
        input_output_aliases={t: t for t in range(2 * n)},
        compiler_params=pltpu.CompilerParams(has_side_effects=DATAFLOW),
    )(*srcs, *lands, send_sems, recv_sems, after)
    return res[n:]


def _sibling_copies(src, land, send, recv, n):
    x, y, c, _ = _chip_slots()
    return [pltpu.make_async_remote_copy(
        src_ref=src[t].at[4 * (q // 2) + 2 * (q % 2) + (1 - c)], dst_ref=land[t].at[q], send_sem=send.at[4 * t + q],
        recv_sem=recv.at[4 * t + q], device_id=(x, y, 1 - c), device_id_type=MESH)
        for t in range(n) for q in range(4)]


def sibling_start(gs, *, name):
    n = len(gs)

    def body(*refs):
        for cp in _sibling_copies(refs[:n], refs[n:2 * n], refs[2 * n], refs[2 * n + 1], n):
            cp.start()
        refs[-1][...] = jnp.zeros_like(refs[-1])

    lands = [lax.empty((4,) + g.shape[1:], g.dtype) for g in gs]
    res = pl.pallas_call(
        body, name=name,
        out_shape=[pltpu.SemaphoreType.DMA((4 * n,)), pltpu.SemaphoreType.DMA((4 * n,))]
        + [pltpu.HBM(g.shape, g.dtype) for g in gs] + [pltpu.HBM(l.shape, l.dtype) for l in lands]
        + [jax.ShapeDtypeStruct((8, LANES), F32)],
        in_specs=[HBM_SPEC] * (2 * n),
        out_specs=[SEM_SPEC, SEM_SPEC] + [HBM_SPEC] * (2 * n) + [pl.BlockSpec(memory_space=pltpu.VMEM)],
        input_output_aliases={t: 2 + t for t in range(2 * n)},
        compiler_params=pltpu.CompilerParams(has_side_effects=DATAFLOW),
    )(*[_in_hbm(g) for g in gs], *[_in_hbm(l) for l in lands])
    return res[0], res[1], res[2:2 + n], res[2 + n:2 + 2 * n], res[-1]


def sibling_wait(started, after, *, name):
    send_sems, recv_sems, srcs, lands, _ = started
    n = len(srcs)

    def body(*refs):
        for cp in _sibling_copies(refs[:n], refs[n:2 * n], refs[2 * n], refs[2 * n + 1], n):
            cp.wait_send()
            cp.wait_recv()

    res = pl.pallas_call(
        body, name=name,
        out_shape=[pltpu.HBM(s.shape, s.dtype) for s in srcs] + [pltpu.HBM(l.shape, l.dtype) for l in lands],
        in_specs=[HBM_SPEC] * (2 * n) + [SEM_SPEC, SEM_SPEC, ANY_SPEC], out_specs=[HBM_SPEC] * (2 * n),
        input_output_aliases={t: t for t in range(2 * n)},
        compiler_params=pltpu.CompilerParams(has_side_effects=DATAFLOW),
    )(*srcs, *lands, send_sems, recv_sems, after)
    return res[:n], res[n:]


def all_gather_vmem(x_shard, *, name, after=None):
    m_per, n = x_shard.shape
    n_after = 0 if after is None else 1

    def body(x_ref, *rest):
        out_ref, send_sems, recv_sems, local_sem = rest[n_after:]
        x, y, c, chips = _chip_slots()
        me, sibling = (x, y, c), (x, y, 1 - c)

        def rows(px, py, pc):
            return out_ref.at[pl.ds((4 * px + 2 * py + pc) * m_per, m_per), :]

        def copy(k, block, to, src=None):
            return pltpu.make_async_remote_copy(
                src_ref=rows(*block) if src is None else src, dst_ref=rows(*block),
                send_sem=send_sems.at[k], recv_sem=recv_sems.at[k], device_id=to, device_id_type=MESH)

        mine = pltpu.make_async_copy(x_ref, rows(*me), local_sem)
        mine.start()
        first = [copy(0, me, sibling, src=x_ref)]
        first += [copy(1 + j, me, (*chip, c), src=x_ref) for j, chip in enumerate(chips)]
        for cp in first:
            cp.start()
        passed = [copy(4 + j, (*chip, c), sibling) for j, chip in enumerate(chips)]
        for j, chip in enumerate(chips):
            copy(1 + j, (*chip, c), me).wait_recv()
            passed[j].start()
        copy(0, sibling, me).wait_recv()
        for j, chip in enumerate(chips):
            copy(4 + j, (*chip, 1 - c), me).wait_recv()
        for cp in first + passed:
            cp.wait_send()
        mine.wait()

    vmem = pl.BlockSpec(memory_space=pltpu.VMEM)
    return pl.pallas_call(
        body, name=name, out_shape=jax.ShapeDtypeStruct((N_DEV * m_per, n), x_shard.dtype),
        in_specs=[vmem] + [ANY_SPEC] * n_after, out_specs=vmem,
        scratch_shapes=[pltpu.SemaphoreType.DMA((7,)), pltpu.SemaphoreType.DMA((7,)), pltpu.SemaphoreType.DMA],
        compiler_params=pltpu.CompilerParams(vmem_limit_bytes=int(min(
            VMEM_LIMIT_CAP, 2 * (N_DEV + 1) * m_per * n * x_shard.dtype.itemsize + 16 * 2 ** 20))),
    )(x_shard, *([] if after is None else [after]))


def _rope_slab(cols):
    z = jnp.zeros(cols.shape[:-1] + (HALF_ROPE,), cols.dtype)
    return jnp.concatenate([cols[..., :HALF_ROPE], z, cols[..., HALF_ROPE:], z], axis=-1)


def _rope_unslab(slab):
    return jnp.concatenate([slab[..., :HALF_ROPE], slab[..., 2 * HALF_ROPE:3 * HALF_ROPE]], axis=-1)


def _pack_w_in_t(wt_g):
    s, c, d = wt_g.shape
    w = wt_g.reshape(s * c, d)
    c2, c3 = Q_LORA + KV_LORA, Q_LORA + KV_LORA + QK_ROPE
    r = w[c2:c3]
    z = jnp.zeros((HALF_ROPE, d), w.dtype)
    return jnp.concatenate([w[:c2], w[c3:], r[:HALF_ROPE], z, r[HALF_ROPE:], z], axis=0)


def _unpack_w_in_t_grad(dwt):
    d = dwt.shape[1]
    c2 = Q_LORA + KV_LORA
    uv = 2 * SGU_OUT
    slab = dwt[c2 + uv:]
    g = jnp.concatenate([dwt[:c2], slab[:HALF_ROPE], slab[2 * HALF_ROPE:3 * HALF_ROPE], dwt[c2:c2 + uv]], axis=0)
    return g.reshape(N_DEV, g.shape[0] // N_DEV, d)


def _rope_tables(positions):
    inv_freq = ROPE_BASE ** (-jnp.arange(0, QK_ROPE, 2, dtype=F32) / QK_ROPE)
    ang = positions.astype(F32)[:, None] * inv_freq
    cos, sin = jnp.cos(ang), jnp.sin(ang)
    z = jnp.zeros_like(cos)
    return jnp.concatenate([cos, z, cos, z], axis=-1), jnp.concatenate([-sin, z, sin, z], axis=-1)


def _mlp_up(x, gain, w1, tag):
    hn = rms_fwd(x, gain, name=f"mlp{tag}_norm")

    def act_epi(acc):
        a = jnp.maximum(acc, 0.0)
        return a, a * a

    T = x.shape[0]
    F = w1.shape[0] * w1.shape[2]
    a, act = mm(hn, w1, name=f"mlp{tag}_up", outs=[((T, F), BF, None), ((T, F), BF, None)], epi=act_epi)
    return hn, a, act


def _mlp_down(x, act, w2, tag, part=0):
    n = w2.shape[1]
    bm = _tile(x.shape[0], MM_TILE)
    bn = _tile(n, MM_TILE)
    per = n // bn
    return mm(act, w2, name=f"mlp{tag}_down{part}", out=((x.shape[0], n), F32), bm=bm, bn=bn,
              epi=lambda acc, r: (acc + r[...],), epi_ins=[(x, (bm, bn), lambda i, j, k: (i, part * per + j))])


def _mlp_bwd_weights(w1, w2, saved, dxb, tag):
    hn, a, act = saved
    T, D = dxb.shape
    F = a.shape[1]
    bm = _tile(T, MM_TILE)
    bn = _tile(F, min(MM_TILE, w1.shape[2]))
    dhid = mm(dxb, w2, tb=True, name=f"mlp{tag}_dhid", out=((T, F), BF), bm=bm, bn=bn,
              epi=lambda acc, a_ref: (2.0 * a_ref[...].astype(F32) * acc,),
              epi_ins=[(a, (bm, bn), lambda i, j, k: (i, j))])
    dw2 = mm(act, dxb, ta=True, name=f"mlp{tag}_dw2", out=((F, D), BF))
    dw1 = mm(hn, dhid, ta=True, name=f"mlp{tag}_dw1", out=(w1.shape, BF))
    return dhid, dw1, dw2.reshape(N_DEV, F // N_DEV, D)


def _reduce_begin(grads, tag):
    return sibling_start(grads, name=f"reduce_sibling_start_{tag}")


def _reduce_continue(sib, after, tag):
    grads, a_bufs = sibling_wait(sib, after, name=f"reduce_sibling_wait_{tag}")
    pairs = [pair_sum(g, a, name=f"pair_sum_{tag}{t}") for t, (g, a) in enumerate(zip(grads, a_bufs))]
    return grads, a_bufs, chips_start(pairs, name=f"reduce_chips_start_{tag}")


def _mlp_bwd_input(x_in, gain, w1, dhid, dx, tag, sib):
    dhn = mm(dhid, w1, tb=True, name=f"mlp{tag}_dhn", out=(x_in.shape, F32), deps=[sib[-1]])
    reduce_state = _reduce_continue(sib, dhn, f"r_mlp{tag}")
    return rms_bwd(x_in, gain, dhn, dres=dx, name=f"mlp{tag}_norm_bwd", deps=[reduce_state[2][-1]]), reduce_state


def kernel(x, positions, e_norm_mix, e_w_in, e_q_norm, e_w_uq, e_kv_norm, e_w_ukv, e_v_norm, e_sgu_w, e_sgu_b, e_mla_out_norm, e_sgu_out_norm, e_w_out, o_norm_mix, o_w_in, o_conv_w, o_w_out, mlp_norm, mlp_w1, mlp_w2, final_norm, loss_target, m_e_norm_mix, m_e_w_in, m_e_q_norm, m_e_w_uq, m_e_kv_norm, m_e_w_ukv, m_e_v_norm, m_e_sgu_w, m_e_sgu_b, m_e_mla_out_norm, m_e_sgu_out_norm, m_e_w_out, m_o_norm_mix, m_o_w_in, m_o_conv_w, m_o_w_out, m_mlp_norm, m_mlp_w1, m_mlp_w2, m_final_norm, v_e_norm_mix, v_e_w_in, v_e_q_norm, v_e_w_uq, v_e_kv_norm, v_e_w_ukv, v_e_v_norm, v_e_sgu_w, v_e_sgu_b, v_e_mla_out_norm, v_e_sgu_out_norm, v_e_w_out, v_o_norm_mix, v_o_w_in, v_o_conv_w, v_o_w_out, v_mlp_norm, v_mlp_w1, v_mlp_w2, v_final_norm):
    T, D = x.shape[1], x.shape[2]
    d_shard = o_norm_mix.shape[1]
    x0 = x[0]
    target = loss_target[0]
    me = 4 * lax.axis_index("x") + 2 * lax.axis_index("y") + lax.axis_index("c")

    bf = lambda s: s.astype(BF)
    gather_groups = [[bf(jnp.transpose(e_w_in[0])), bf(e_w_uq[0]), bf(e_w_ukv[0])], [bf(e_w_out[0]), bf(mlp_w1[0])],
                     [bf(mlp_w2[0]), bf(o_w_in[0])], [bf(o_w_out[0]), bf(mlp_w1[1])],
                     [bf(mlp_w2[1][:, :D // 2])], [bf(mlp_w2[1][:, D // 2:])]]
    small_rows = jnp.concatenate([o_norm_mix, o_conv_w[0], jnp.zeros((4, d_shard), F32)], axis=0)
    small_flat = all_gather_vmem(small_rows, name="gather_small")
    started, start_token = gather_start(gather_groups[:1], small_flat, name="gather_start0")
    started_rest, rest_token = gather_start(gather_groups[1:], start_token, name="gather_start1")
    started += started_rest

    def gathered(gi, after):
        srcs, lands = gather_wait(started[gi], after, name=f"gather_wait{gi}")
        return gather_finish(srcs, lands, name=f"gather_finish{gi}")

    small_g = small_flat.reshape(N_DEV, 8, d_shard)
    o_norm_full = small_g[:, 0, :].reshape(1, D)
    conv_w_full = jnp.transpose(small_g[:, 1:4, :], (1, 0, 2)).reshape(3, D)
    w_tril = jnp.tril(e_sgu_w[0])
    w_tril_b = w_tril.astype(BF)
    w_tril_tb = jnp.swapaxes(w_tril, 1, 2).astype(BF)
    b_full = jnp.repeat(e_sgu_b[0].T, CH, axis=1)
    v_gain = e_v_norm[0].reshape(1, SGU_OUT)
    cos_t, sin_t = _rope_tables(positions[0])
    mlp_gain = [mlp_norm[0:1], mlp_norm[1:2]]
    final_gain = final_norm.reshape(1, D)

    h0 = rms_fwd(x0, e_norm_mix, name="e_norm", deps=[rest_token])
    g_w_in_t, g_w_uq, w_ukv = gathered(
        0, [h0, cos_t, sin_t, w_tril_b, w_tril_tb, b_full, o_norm_full, conv_w_full])
    w_in_t = _pack_w_in_t(g_w_in_t)
    w_uq = jnp.concatenate([g_w_uq[..., :QK_NOPE], _rope_slab(g_w_uq[..., QK_NOPE:])], axis=-1)
    proj = mm(h0, w_in_t, tb=True, name="e_in", out=((T, w_in_t.shape[0]), F32), bn=_tile(w_in_t.shape[0], 640))
    qn, kvn, krope = mla_prep(proj, e_q_norm, e_kv_norm, cos_t, sin_t, name="mla_prep")
    bm = _tile(T, MM_TILE)

    def q_epi(acc, cos_ref, sin_ref):
        return (jnp.concatenate([acc[:, :QK_NOPE], _rope_fwd(acc[:, QK_NOPE:], cos_ref[...], sin_ref[...])], axis=-1),)

    q = mm(qn, w_uq, name="mla_q", out=((T, HEADS * HEAD_PAD), BF), bm=bm, bn=HEAD_PAD, epi=q_epi,
           epi_ins=[(cos_t, (bm, LANES), lambda i, j, k: (i, 0)), (sin_t, (bm, LANES), lambda i, j, k: (i, 0))])

    def kv_epi(acc, kr_ref):
        return jnp.concatenate([acc[:, :QK_NOPE].astype(BF), kr_ref[...]], axis=-1), acc[:, QK_NOPE:]

    k, v = mm(kvn, w_ukv, name="mla_kv", bm=bm, bn=HEAD_PAD, epi=kv_epi,
              outs=[((T, HEADS * HEAD_PAD), BF, HEAD_PAD), ((T, MLA_OUT), BF, V_HEAD)],
              epi_ins=[(krope, (bm, LANES), lambda i, j, k: (i, 0))])
    attn, attn_lse = attn_fwd(q, k, v, name="attn_fwd")
    mixed = mix_fwd(attn, proj, e_mla_out_norm, e_sgu_out_norm, v_gain, w_tril_b, b_full, name="mix_fwd")
    bn = _tile(D, MM_TILE)
    g_w_out_e, w1_0 = gathered(1, [mixed])
    w_out_e = g_w_out_e.reshape(-1, D)
    x1 = mm(mixed, w_out_e, name="e_out", out=((T, D), F32), bm=bm, bn=bn,
            epi=lambda acc, r: (acc + r[...],), epi_ins=[(x0, (bm, bn), lambda i, j, k: (i, j))])
    hn0, a0, act0 = _mlp_up(x1, mlp_gain[0], w1_0, 0)
    g_w2_0, g_w_in_o = gathered(2, [act0])
    w2_0 = g_w2_0.reshape(-1, D)
    x2 = _mlp_down(x1, act0, w2_0, 0)
    ho = rms_fwd(x2, o_norm_full, name="o_norm")
    proj_o = mm(ho, g_w_in_o, name="o_in", out=((T, 3 * D), F32))
    gated = conv_fwd(proj_o, conv_w_full, name="conv_fwd")
    g_w_out_o, w1_1 = gathered(3, [gated])
    w_out_o = g_w_out_o.reshape(-1, D)
    x3 = mm(gated, w_out_o, name="o_out", out=((T, D), F32), bm=bm, bn=bn,
            epi=lambda acc, r: (acc + r[...],), epi_ins=[(x2, (bm, bn), lambda i, j, k: (i, j))])
    hn1, a1, act1 = _mlp_up(x3, mlp_gain[1], w1_1, 1)
    (g_w2_1a,) = gathered(4, [act1])
    w2_1a = g_w2_1a.reshape(-1, D // 2)
    x4a = _mlp_down(x3, act1, w2_1a, 1, part=0)
    (g_w2_1b,) = gathered(5, [x4a])
    w2_1b = g_w2_1b.reshape(-1, D // 2)
    x4b = _mlp_down(x3, act1, w2_1b, 1, part=1)
    w1, w2 = [w1_0, w1_1], [w2_0, (w2_1a, w2_1b)]
    mlp0_saved, mlp1_saved = (hn0, a0, act0), (hn1, a1, act1)

    dx4, dx4b, d_final, loss_part = loss_bwd([x4a, x4b], final_gain, target, name="loss_bwd")

    dhid1, dw1_1, dw2_1 = _mlp_bwd_weights(w1[1], w2[1], mlp1_saved, dx4b, 1)
    sib_r0 = _reduce_begin([dw1_1, dw2_1], "r0")
    (dx3, dx3b, d_mlp1), (grads_r0, a_r0, st_r0) = _mlp_bwd_input(x3, mlp_gain[1], w1[1], dhid1, dx4, 1, sib_r0)

    dgated = mm(dx3b, w_out_o, tb=True, name="o_out_dx", out=((T, D), F32))
    dw_out_o = mm(gated, dx3b, ta=True, name="o_out_dw", out=((D, D), BF))
    dproj_o, dconv_full = conv_bwd(dgated, proj_o, conv_w_full, name="conv_bwd")
    dw_in_o = mm(ho, dproj_o, ta=True, name="o_in_dw", out=(g_w_in_o.shape, BF))
    sib_r1 = _reduce_begin([dw_out_o.reshape(g_w_out_o.shape), dw_in_o], "r1")
    dho = mm(dproj_o, g_w_in_o, tb=True, name="o_in_dx", out=((T, D), F32), deps=[sib_r1[-1]])
    grads_r1, a_r1, st_r1 = _reduce_continue(sib_r1, dho, "r1")
    dx2, dx2b, d_onorm_full = rms_bwd(x2, o_norm_full, dho, dres=dx3, name="o_norm_bwd", deps=[st_r1[-1]])

    dhid0, dw1_0, dw2_0 = _mlp_bwd_weights(w1[0], w2[0], mlp0_saved, dx2b, 0)
    sib_r2 = _reduce_begin([dw1_0, dw2_0], "r2")
    (dx1, dx1b, d_mlp0), (grads_r2, a_r2, st_r2) = _mlp_bwd_input(x1, mlp_gain[0], w1[0], dhid0, dx2, 0, sib_r2)
    b_r0 = chips_wait(st_r0, dx1b, name="reduce_chips_wait_r0")

    dmixed = mm(dx1b, w_out_e, tb=True, name="e_out_dx", out=((T, MLA_OUT + SGU_OUT), F32))
    dw_out_e = mm(mixed, dx1b, ta=True, name="e_out_dw", out=(w_out_e.shape, BF))
    (dattn, duv, d_mla_out, d_sgu_out, d_vgain, d_sgu_w, d_b_full) = mix_bwd(
        dmixed, attn, proj, e_mla_out_norm, e_sgu_out_norm, v_gain, w_tril_b, w_tril_tb, b_full, name="mix_bwd")
    b_r1 = chips_wait(st_r1, dattn, name="reduce_chips_wait_r1")
    dq, dk, dv = attn_bwd(q, k, v, attn, attn_lse, dattn, name="attn_bwd")
    dq_lin, dkv_lin, dkr = mla_bwd_prep(dq, dk, dv, cos_t, sin_t, name="mla_bwd_prep")
    dw_uq_pad = mm(qn, dq_lin, ta=True, name="mla_q_dw", out=(w_uq.shape, BF))
    dw_ukv = mm(kvn, dkv_lin, ta=True, name="mla_kv_dw", out=(w_ukv.shape, BF))
    dw_uq = jnp.concatenate([dw_uq_pad[..., :QK_NOPE], _rope_unslab(dw_uq_pad[..., QK_NOPE:])], axis=-1)
    sib_r2b = _reduce_begin([dw_out_e.reshape(g_w_out_e.shape), dw_uq, dw_ukv], "r2b")
    dqn = mm(dq_lin, w_uq, tb=True, name="mla_q_dx", out=((T, Q_LORA), F32), deps=[sib_r2b[-1]])
    dkvn = mm(dkv_lin, w_ukv, tb=True, name="mla_kv_dx", out=((T, KV_LORA), F32), deps=[sib_r2b[-1]])
    grads_r2b, a_r2b, st_r2b = _reduce_continue(sib_r2b, dkvn, "r2b")
    dcq, d_qnorm = rms_bwd(proj, e_q_norm, dqn, col_block=0, want_f32=False, name="q_norm_bwd", deps=[st_r2b[-1]])
    dckv, d_kvnorm = rms_bwd(proj, e_kv_norm, dkvn, col_block=1, want_f32=False, name="kv_norm_bwd")
    dproj = jnp.concatenate([dcq, dckv, duv, dkr], axis=-1)
    dw_in_t_pad = mm(dproj, h0, ta=True, name="e_in_dw", out=(w_in_t.shape, BF), bm=_tile(w_in_t.shape[0], 640))
    dw_in_t = _unpack_w_in_t_grad(dw_in_t_pad)
    sib_r3 = _reduce_begin([dw_in_t], "r3")
    dh0 = mm(dproj, w_in_t, name="e_in_dx", out=((T, D), F32), deps=[sib_r3[-1]])
    grads_r3, a_r3, st_r3 = _reduce_continue(sib_r3, dh0, "r3")
    tok_r3 = st_r3[-1]
    grad_x, d_enorm = rms_bwd(x0, e_norm_mix, dh0, dres=dx1, want_bf=False, name="e_norm_bwd", deps=[tok_r3])
    b_r2 = chips_wait(st_r2, grad_x, name="reduce_chips_wait_r2")

    def finish(grads, a_bufs, b_bufs, t, w, m, v, layer=0, prev=None, tag="", deps=()):
        return reduce_adam(grads[t], a_bufs[t], b_bufs[t], w, m, v, layer, prev, name=f"adam_{tag}", deps=deps)

    r_w1 = finish(grads_r0, a_r0, b_r0, 0, mlp_w1, m_mlp_w1, v_mlp_w1, 1, None, tag="w1_l1")
    r_w2 = finish(grads_r0, a_r0, b_r0, 1, mlp_w2, m_mlp_w2, v_mlp_w2, 1, None, tag="w2_l1")
    r_w_out_o = finish(grads_r1, a_r1, b_r1, 0, o_w_out, m_o_w_out, v_o_w_out, tag="o_w_out")
    r_w_in_o = finish(grads_r1, a_r1, b_r1, 1, o_w_in, m_o_w_in, v_o_w_in, tag="o_w_in")
    r_w1 = finish(grads_r2, a_r2, b_r2, 0, mlp_w1, m_mlp_w1, v_mlp_w1, 0, r_w1, tag="w1_l0", deps=[tok_r3])
    r_w2 = finish(grads_r2, a_r2, b_r2, 1, mlp_w2, m_mlp_w2, v_mlp_w2, 0, r_w2, tag="w2_l0", deps=[r_w1[1]])
    b_r2b = chips_wait(st_r2b, r_w2[1], name="reduce_chips_wait_r2b")
    r_w_out_e = finish(grads_r2b, a_r2b, b_r2b, 0, e_w_out, m_e_w_out, v_e_w_out, tag="e_w_out")
    r_w_uq = finish(grads_r2b, a_r2b, b_r2b, 1, e_w_uq, m_e_w_uq, v_e_w_uq, tag="e_w_uq")
    r_w_ukv = finish(grads_r2b, a_r2b, b_r2b, 2, e_w_ukv, m_e_w_ukv, v_e_w_ukv, tag="e_w_ukv")
    b_r3 = chips_wait(st_r3, r_w_out_e[1], name="reduce_chips_wait_r3")
    g_w_in = jnp.transpose(reduce_sum(grads_r3[0], a_r3[0], b_r3[0], name="sum_e_w_in"))
    r_w_in = [t[None] for t in (g_w_in, *adam_rows(g_w_in, e_w_in[0], m_e_w_in[0], v_e_w_in[0], name="adam_e_w_in"))]

    d_sgu_b = jnp.transpose(d_b_full[:, ::CH])
    d_sgu_w_tril = jnp.tril(d_sgu_w)
    rep = [("e_norm_mix", e_norm_mix, m_e_norm_mix, v_e_norm_mix, d_enorm),
           ("e_q_norm", e_q_norm, m_e_q_norm, v_e_q_norm, d_qnorm),
           ("e_kv_norm", e_kv_norm, m_e_kv_norm, v_e_kv_norm, d_kvnorm),
           ("e_v_norm", e_v_norm, m_e_v_norm, v_e_v_norm, d_vgain),
           ("e_sgu_w", e_sgu_w, m_e_sgu_w, v_e_sgu_w, d_sgu_w_tril),
           ("e_sgu_b", e_sgu_b, m_e_sgu_b, v_e_sgu_b, d_sgu_b),
           ("e_mla_out_norm", e_mla_out_norm, m_e_mla_out_norm, v_e_mla_out_norm, d_mla_out),
           ("e_sgu_out_norm", e_sgu_out_norm, m_e_sgu_out_norm, v_e_sgu_out_norm, d_sgu_out),
           ("mlp_norm", mlp_norm, m_mlp_norm, v_mlp_norm, jnp.concatenate([d_mlp0, d_mlp1], axis=0)),
           ("final_norm", final_norm, m_final_norm, v_final_norm, d_final)]
    sizes = [int(np.prod(r[1].shape)) for r in rep]
    n_rep = sum(sizes)
    n_all = n_rep + 4 * D + 1
    width = -(-n_all // (8 * LANES)) * LANES
    pad = 8 * width - n_all
    flat = jnp.concatenate([r[4].reshape(-1) for r in rep]
                           + [d_onorm_full.reshape(-1), dconv_full.reshape(-1), loss_part[0, :1],
                              jnp.zeros((pad,), F32)])
    summed = sum_rows8(all_gather_vmem(flat.reshape(8, width), name="gather_small_grads", after=b_r3[0]), 8,
                       name="sum_small_grads").reshape(-1)

    loss = summed[n_rep + 4 * D]

    def pack_rep(i):
        return jnp.concatenate([r[i].reshape(-1) for r in rep]).reshape(n_rep // LANES, LANES)

    g_rep = summed[:n_rep].reshape(n_rep // LANES, LANES)
    d_rep, nm_rep, nv_rep = adam_flat(g_rep, pack_rep(1), pack_rep(2), pack_rep(3), name="adam_replicated")

    def unpack_rep(flat2d):
        out, off = {}, 0
        f = flat2d.reshape(-1)
        for r, n in zip(rep, sizes):
            out[r[0]] = f[off:off + n].reshape(r[1].shape)
            off += n
        return out

    small = {"grad": unpack_rep(g_rep), "delta": unpack_rep(d_rep), "new_m": unpack_rep(nm_rep),
             "new_v": unpack_rep(nv_rep)}
    g_onorm = lax.dynamic_slice(summed[n_rep:n_rep + D].reshape(1, D), (0, me * d_shard), (1, d_shard))
    g_conv = lax.dynamic_slice(summed[n_rep + D:n_rep + 4 * D].reshape(3, D), (0, me * d_shard), (3, d_shard))

    def pack_sharded(norm_part, conv_part):
        return jnp.concatenate([norm_part, conv_part, jnp.zeros((4, d_shard), F32)], axis=0)

    g_sh = pack_sharded(g_onorm, g_conv)
    d_sh, nm_sh, nv_sh = adam_flat(g_sh, pack_sharded(o_norm_mix, o_conv_w[0]), pack_sharded(m_o_norm_mix, m_o_conv_w[0]),
                                   pack_sharded(v_o_norm_mix, v_o_conv_w[0]), name="adam_sharded_small")
    for kind, arr in (("grad", g_sh), ("delta", d_sh), ("new_m", nm_sh), ("new_v", nv_sh)):
        small[kind]["o_norm_mix"] = arr[0:1]
        small[kind]["o_conv_w"] = arr[1:4][None]

    big = {"e_w_in": r_w_in, "e_w_uq": r_w_uq, "e_w_ukv": r_w_ukv, "e_w_out": r_w_out_e, "o_w_in": r_w_in_o,
           "o_w_out": r_w_out_o, "mlp_w1": r_w1, "mlp_w2": r_w2}
    order = ["e_norm_mix", "e_w_in", "e_q_norm", "e_w_uq", "e_kv_norm", "e_w_ukv", "e_v_norm", "e_sgu_w", "e_sgu_b",
             "e_mla_out_norm", "e_sgu_out_norm", "e_w_out", "o_norm_mix", "o_w_in", "o_conv_w", "o_w_out", "mlp_norm",
             "mlp_w1", "mlp_w2", "final_norm"]
    result = [loss, grad_x[None]]
    for ki, kind in enumerate(("grad", "delta", "new_m", "new_v")):
        for nm in order:
            result.append(big[nm][ki] if nm in big else small[kind][nm])
    return tuple(result)
```

```python
import numpy as np
import jax
import jax.numpy as jnp
from jax import lax
from jax.experimental import pallas as pl
from jax.experimental.pallas import tpu as pltpu

BF = jnp.bfloat16
F32 = jnp.float32
MESH = pl.DeviceIdType.MESH
N_DEV = 8

EPS = 1e-6
HEADS = 8
Q_LORA = 512
KV_LORA = 512
QK_NOPE = 128
QK_ROPE = 64
HALF_ROPE = QK_ROPE // 2
V_HEAD = 128
HEAD_PAD = 256
ROPE_BASE = 10000.0
GROUPS = 8
CH = 128
CHUNK = 128
SGU_OUT = GROUPS * CH
MLA_OUT = HEADS * V_HEAD
ATTN_SCALE = float((QK_NOPE + QK_ROPE) ** -0.5)

ADAM_LR = 0.001
ADAM_B1 = 0.9
ADAM_B2 = 0.999
ADAM_EPS = 1e-08
ADAM_WD = 0.01
ADAM_STEP = 10
ADAM_C1 = 1.0 - ADAM_B1 ** ADAM_STEP
ADAM_C2 = 1.0 - ADAM_B2 ** ADAM_STEP

V7X_VMEM_BYTES = 64 * 2 ** 20
VMEM_LIMIT_CAP = V7X_VMEM_BYTES - 6 * 2 ** 20
LANES = 128
ROW_TILE = 256
ATTN_TILE = 512
STREAM_BLOCK_ELEMS = 512 * 1024
MM_TILE = 1024
MM_K_TILE = 2048
MM_K_BLOCK_MAX = 3072


def _padded_bytes(block, dtype):
    dims = [d for d in block if d is not None]
    if len(dims) >= 1:
        dims[-1] = -(-dims[-1] // LANES) * LANES
    if len(dims) >= 2:
        dims[-2] = -(-dims[-2] // 16) * 16
    return int(np.prod(dims)) * jnp.dtype(dtype).itemsize


def _pcall(body, *, name, grid, ins, outs, scratch=(), semantics=None, aliases=None, prefetch=None, deps=()):
    any_spec = pl.BlockSpec(memory_space=pl.ANY)
    if deps:
        n_lead = len(ins) + (1 if prefetch is not None else 0)
        n_deps = len(deps)
        inner = body

        def body(*refs):
            inner(*refs[:n_lead], *refs[n_lead + n_deps:])

        ins = list(ins) + [(d, None, None) for d in deps]
    in_specs = [any_spec if b is None else pl.BlockSpec(b, m) for _, b, m in ins]
    out_specs = [any_spec if b is None else pl.BlockSpec(b, m) for _, _, b, m in outs]
    out_shape = [pltpu.HBM(s, d) for s, d, _, _ in outs]
    est = 0
    for a, b, _ in ins:
        if b is not None:
            est += 2 * _padded_bytes(b, a.dtype)
    for _, d, b, _ in outs:
        if b is not None:
            est += 2 * _padded_bytes(b, d)
    for s in scratch:
        if hasattr(s, "shape") and hasattr(s, "dtype"):
            est += _padded_bytes(s.shape, s.dtype)
    limit = int(min(VMEM_LIMIT_CAP, est + 16 * 2 ** 20))
    params = pltpu.CompilerParams(
        dimension_semantics=semantics or ("arbitrary",) * len(grid), vmem_limit_bytes=limit)
    args = [pltpu.with_memory_space_constraint(a, pltpu.HBM) for a, _, _ in ins]
    if prefetch is not None:
        grid_spec = pltpu.PrefetchScalarGridSpec(
            num_scalar_prefetch=1, grid=grid, in_specs=in_specs, out_specs=out_specs, scratch_shapes=list(scratch))
        call = pl.pallas_call(body, out_shape=out_shape, grid_spec=grid_spec, name=name, compiler_params=params,
                              input_output_aliases=aliases or {})
        return call(prefetch, *args)
    call = pl.pallas_call(body, out_shape=out_shape, grid=grid, in_specs=in_specs, out_specs=out_specs,
                          scratch_shapes=list(scratch), name=name, compiler_params=params,
                          input_output_aliases=aliases or {})
    return call(*args)


def _tile(dim, pref, quantum=LANES):
    if dim <= pref:
        return dim
    t = (pref // quantum) * quantum
    while t >= quantum:
        if dim % t == 0:
            return t
        t -= quantum
    return dim


def _vshape(arr_shape):
    if len(arr_shape) == 2:
        return tuple(arr_shape)
    s, r, c = arr_shape
    return (r, s * c)


def _vblock(arr_shape, br, bc, rc):
    if len(arr_shape) == 2:
        return (br, bc), (lambda *g: rc(*g))
    _, _, c = arr_shape
    assert c % bc == 0, (arr_shape, bc)
    per = c // bc

    def imap(*g):
        ri, ci = rc(*g)
        return (ci // per, ri, ci % per)

    return (None, br, bc), imap


def _shard_width(*shapes):
    w = None
    for s in shapes:
        if len(s) == 3:
            w = s[2] if w is None else int(np.gcd(w, s[2]))
    return w


def mm(a, b, *, name, ta=False, tb=False, out=None, outs=None, epi=None, epi_ins=(), bm=None, bn=None, bk=None,
       deps=(), jobs=(), job_index=None):
    av, bv = _vshape(a.shape), _vshape(b.shape)
    M, K = (av[1], av[0]) if ta else av
    K2, N = (bv[1], bv[0]) if tb else bv
    assert K == K2, (a.shape, b.shape, ta, tb)
    if outs is None:
        outs = [(out[0], out[1], None)]
    a_sw = _shard_width(a.shape)
    b_sw = _shard_width(b.shape)
    o_sw = _shard_width(*[o[0] for o in outs])
    m_lim = a_sw if (ta and a_sw) else None
    k_lim = [w for w in ((a_sw if not ta else None), (b_sw if tb else None)) if w]
    n_lim = [w for w in ((b_sw if not tb else None), o_sw) if w]
    if bm is None:
        bm = _tile(M, min([MM_TILE] + ([m_lim] if m_lim else [])))
    if bn is None:
        bn = _tile(N, min([MM_TILE] + n_lim))
    k_shards = 0
    if tb and len(b.shape) == 3 and bk is None and not (a_sw and not ta):
        k_shards = 1
        while 2 * k_shards <= b.shape[0] and 2 * k_shards * b_sw <= MM_K_BLOCK_MAX:
            k_shards *= 2
        bk = k_shards * b_sw
    if bk is None:
        bk = K if (K <= 4096 and not k_lim) else _tile(K, min([MM_K_TILE] + k_lim))
    assert M % bm == 0 and N % bn == 0 and K % bk == 0, (name, M, N, K, bm, bn, bk)
    nk = K // bk
    grid = (M // bm, N // bn, nk)
    if ta:
        a_blk, a_map = _vblock(a.shape, bk, bm, lambda i, j, k: (k, i))
    else:
        a_blk, a_map = _vblock(a.shape, bm, bk, lambda i, j, k: (i, k))
    if k_shards:
        b_blk, b_map = (k_shards, bn, b_sw), (lambda i, j, k: (k, j, 0))
    elif tb:
        b_blk, b_map = _vblock(b.shape, bn, bk, lambda i, j, k: (j, k))
    else:
        b_blk, b_map = _vblock(b.shape, bk, bn, lambda i, j, k: (k, j))
    dn = (((0 if ta else 1,), (1 if tb else 0,)), ((), ()))
    ins = [(a, a_blk, a_map), (b, b_blk, b_map)] + list(epi_ins)
    out_list = []
    for shape, dtype, cols in outs:
        cols = cols or bn
        blk, imap = _vblock(shape, bm, cols, lambda i, j, k: (i, j))
        out_list.append((shape, dtype, blk, imap))
    n_e, n_o = len(epi_ins), len(out_list)

    n_steps = grid[0] * grid[1] * nk
    built = [job(n_steps) for job in jobs]
    aliases = {}
    job_slices = []
    if built:
        def lin(i, j, k):
            return (i * grid[1] + j) * nk + k

        ins = [(arr, blk, None if blk is None else (lambda i, j, k, s, f=f: f(i, j, k))) for arr, blk, f in ins]
        out_list = [(sh, dt, blk, (lambda i, j, k, s, f=f: f(i, j, k))) for sh, dt, blk, f in out_list]
        n_main_in, n_main_out = len(ins), len(out_list)
        for jb in built:
            i0, o0 = len(ins), len(out_list)
            ins += [(arr, blk, None if blk is None else (lambda i, j, k, s, f=f: f(lin(i, j, k), s)))
                    for arr, blk, f in jb["ins"]]
            out_list += [(sh, dt, blk, (lambda i, j, k, s, f=f: f(lin(i, j, k), s))) for sh, dt, blk, f in jb["outs"]]
            aliases.update({1 + i0 + ai: o0 + ao for ai, ao in jb["aliases"].items()})
            job_slices.append((i0, len(jb["ins"]), o0, len(jb["outs"])))
    n_in_total = len(ins)

    def body(*refs):
        if built:
            refs = refs[1:]
        a_ref, b_ref = refs[0], refs[1]
        e_refs = refs[2:2 + n_e]
        o_refs = refs[n_in_total:n_in_total + n_o]
        for jb, (i0, ni, o0, no) in zip(built, job_slices):
            jb["fn"](refs[i0:i0 + ni], refs[n_in_total + o0:n_in_total + o0 + no])

        def finish(acc):
            res = epi(acc, *e_refs) if epi is not None else (acc,)
            for o_ref, r in zip(o_refs, res):
                o_ref[...] = r.astype(o_ref.dtype)

        x = a_ref[...].astype(BF)
        y = b_ref[...].astype(BF)
        if k_shards:
            p = None
            for s in range(k_shards):
                part = lax.dot_general(x[:, s * b_sw:(s + 1) * b_sw], y[s], dn, preferred_element_type=F32)
                p = part if p is None else p + part
        else:
            p = lax.dot_general(x, y, dn, preferred_element_type=F32)
        if nk == 1:
            finish(p)
        else:
            acc_ref = refs[-1]
            k = pl.program_id(2)

            @pl.when(k == 0)
            def _():
                acc_ref[...] = p

            @pl.when(k > 0)
            def _():
                acc_ref[...] += p

            @pl.when(k == nk - 1)
            def _():
                finish(acc_ref[...])

    scratch = [pltpu.VMEM((bm, bn), F32)] if nk > 1 else []
    res = _pcall(body, name=name, grid=grid, ins=ins, outs=out_list, scratch=scratch, deps=deps,
                 semantics=("parallel", "parallel", "arbitrary"), prefetch=job_index if built else None, aliases=aliases)
    main = res[0] if n_o == 1 else res[:n_o]
    if not built:
        return main
    return main, [res[o0:o0 + no] for _, _, o0, no in job_slices]


_GELU_K = float(np.sqrt(2.0 / np.pi))
_GELU_C = 0.044715


def _gelu(x):
    t = jnp.tanh(_GELU_K * (x + _GELU_C * (x * x * x)))
    return 0.5 * x * (1.0 + t)


def _gelu_grad(x):
    t = jnp.tanh(_GELU_K * (x + _GELU_C * (x * x * x)))
    return 0.5 * (1.0 + t) + 0.5 * x * (1.0 - t * t) * (_GELU_K * (1.0 + 3.0 * _GELU_C * (x * x)))


def _rstd(x):
    return lax.rsqrt(jnp.mean(x * x, axis=-1, keepdims=True) + EPS)


def _rms_bwd(x, gain, dy):
    r = _rstd(x)
    xh = x * r
    gdy = dy * gain
    dx = r * (gdy - xh * jnp.mean(gdy * xh, axis=-1, keepdims=True))
    return dx, dy * xh


def _rope_fwd(x, cos_t, sin_t):
    return x * cos_t + pltpu.roll(x, 2 * HALF_ROPE, 1) * sin_t


def _rope_bwd(dy, cos_t, sin_t):
    return dy * cos_t + pltpu.roll(dy * sin_t, 2 * HALF_ROPE, 1)


def _acc_rows(ref, val, first):
    s = jnp.sum(val, axis=0, keepdims=True)

    @pl.when(first)
    def _():
        ref[...] = s

    @pl.when(jnp.logical_not(first))
    def _():
        ref[...] += s


def rms_fwd(x, gain, *, name, col_block=0, width=None, deps=()):
    T = x.shape[0]
    width = width or x.shape[1]
    tm = _tile(T, ROW_TILE, 8)

    def body(x_ref, g_ref, o_ref):
        v = x_ref[...]
        o_ref[...] = (v * _rstd(v) * g_ref[...]).astype(BF)

    return _pcall(body, name=name, grid=(T // tm,),
                  ins=[(x, (tm, width), lambda i: (i, col_block)), (gain, (1, width), lambda i: (0, 0))],
                  outs=[((T, width), BF, (tm, width), lambda i: (i, 0))], semantics=("parallel",), deps=deps)[0]


def rms_bwd(x, gain, dy, *, name, col_block=0, dres=None, want_f32=True, want_bf=True, deps=()):
    T, width = dy.shape
    tm = _tile(T, ROW_TILE, 8)
    has_res = dres is not None

    def body(*refs):
        x_ref, g_ref, dy_ref = refs[:3]
        pos = 3
        res_ref = None
        if has_res:
            res_ref = refs[pos]
            pos += 1
        outs = refs[pos:]
        dx, dg_rows = _rms_bwd(x_ref[...], g_ref[...], dy_ref[...])
        if has_res:
            dx = dx + res_ref[...]
        o = 0
        if want_f32:
            outs[o][...] = dx
            o += 1
        if want_bf:
            outs[o][...] = dx.astype(BF)
            o += 1
        _acc_rows(outs[o], dg_rows, pl.program_id(0) == 0)

    ins = [(x, (tm, width), lambda i: (i, col_block)), (gain, (1, width), lambda i: (0, 0)),
           (dy, (tm, width), lambda i: (i, 0))]
    if has_res:
        ins.append((dres, (tm, width), lambda i: (i, 0)))
    outs = []
    if want_f32:
        outs.append(((T, width), F32, (tm, width), lambda i: (i, 0)))
    if want_bf:
        outs.append(((T, width), BF, (tm, width), lambda i: (i, 0)))
    outs.append(((1, width), F32, (1, width), lambda i: (0, 0)))
    return _pcall(body, name=name, grid=(T // tm,), ins=ins, outs=outs, deps=deps)


def mla_prep(proj, q_norm, kv_norm, cos_t, sin_t, *, name):
    T = proj.shape[0]
    tm = _tile(T, ROW_TILE, 8)
    kr_block = (proj.shape[1] - LANES) // LANES

    def body(cq_ref, ckv_ref, kr_ref, qg_ref, kg_ref, cos_ref, sin_ref, qn_ref, kvn_ref, krope_ref):
        cq = cq_ref[...]
        qn_ref[...] = (cq * _rstd(cq) * qg_ref[...]).astype(BF)
        ckv = ckv_ref[...]
        kvn_ref[...] = (ckv * _rstd(ckv) * kg_ref[...]).astype(BF)
        krope_ref[...] = _rope_fwd(kr_ref[...], cos_ref[...], sin_ref[...]).astype(BF)

    return _pcall(
        body, name=name, grid=(T // tm,),
        ins=[(proj, (tm, Q_LORA), lambda i: (i, 0)), (proj, (tm, KV_LORA), lambda i: (i, 1)),
             (proj, (tm, LANES), lambda i: (i, kr_block)),
             (q_norm, (1, Q_LORA), lambda i: (0, 0)), (kv_norm, (1, KV_LORA), lambda i: (0, 0)),
             (cos_t, (tm, LANES), lambda i: (i, 0)), (sin_t, (tm, LANES), lambda i: (i, 0))],
        outs=[((T, Q_LORA), BF, (tm, Q_LORA), lambda i: (i, 0)), ((T, KV_LORA), BF, (tm, KV_LORA), lambda i: (i, 0)),
              ((T, LANES), BF, (tm, LANES), lambda i: (i, 0))],
        semantics=("parallel",))


def _attn_scores(q, k_blk, diagonal):
    s = lax.dot_general(q, k_blk, (((1,), (1,)), ((), ())), preferred_element_type=F32) * ATTN_SCALE
    if diagonal:
        row = lax.broadcasted_iota(jnp.int32, s.shape, 0)
        col = lax.broadcasted_iota(jnp.int32, s.shape, 1)
        s = jnp.where(col <= row, s, -jnp.inf)
    return s


def attn_fwd(q, k, v, *, name):
    T = q.shape[0]
    tq = _tile(T, ATTN_TILE, 8)

    def body(q_ref, k_ref, v_ref, o_ref, lse_ref):
        i = pl.program_id(1)
        qv = q_ref[...]

        def block(kb, carry, diagonal):
            m, l, acc = carry
            start = pl.multiple_of(kb * tq, tq)
            s = _attn_scores(qv, k_ref[pl.ds(start, tq), :], diagonal)
            m_new = jnp.maximum(m, jnp.max(s, axis=-1, keepdims=True))
            alpha = jnp.exp(m - m_new)
            p = jnp.exp(s - m_new)
            l = alpha * l + jnp.sum(p, axis=-1, keepdims=True)
            acc = alpha * acc + jnp.dot(p.astype(BF), v_ref[pl.ds(start, tq), :], preferred_element_type=F32)
            return m_new, l, acc

        init = (jnp.full((tq, 1), -jnp.inf, F32), jnp.zeros((tq, 1), F32), jnp.zeros((tq, V_HEAD), F32))
        carry = lax.fori_loop(0, i, lambda kb, c: block(kb, c, False), init)
        m, l, acc = block(i, carry, True)
        o_ref[...] = acc / l
        lse_ref[...] = jnp.broadcast_to(m + jnp.log(l), (tq, V_HEAD))

    return _pcall(
        body, name=name, grid=(HEADS, T // tq),
        ins=[(q, (tq, HEAD_PAD), lambda h, i: (i, h)), (k, (T, HEAD_PAD), lambda h, i: (0, h)),
             (v, (T, V_HEAD), lambda h, i: (0, h))],
        outs=[((T, MLA_OUT), F32, (tq, V_HEAD), lambda h, i: (i, h)),
              ((T, MLA_OUT), F32, (tq, V_HEAD), lambda h, i: (i, h))], semantics=("parallel", "parallel"))


def attn_bwd(q, k, v, o, lse, do, *, name):
    T = q.shape[0]
    tq = _tile(T, ATTN_TILE, 8)

    def body(q_ref, k_ref, v_ref, o_ref, lse_ref, do_ref, dq_ref, dk_ref, dv_ref):
        i = pl.program_id(1)

        @pl.when(i == 0)
        def _():
            dk_ref[...] = jnp.zeros_like(dk_ref)
            dv_ref[...] = jnp.zeros_like(dv_ref)

        qv = q_ref[...]
        do_t = do_ref[...]
        lse_v = lse_ref[:, 0:1]
        delta = jnp.sum(do_t.astype(F32) * o_ref[...], axis=-1, keepdims=True)

        def block(kb, dq, diagonal):
            start = pl.multiple_of(kb * tq, tq)
            k_blk = k_ref[pl.ds(start, tq), :]
            v_blk = v_ref[pl.ds(start, tq), :]
            p = jnp.exp(_attn_scores(qv, k_blk, diagonal) - lse_v)
            dp = lax.dot_general(do_t, v_blk, (((1,), (1,)), ((), ())), preferred_element_type=F32)
            ds = (p * (dp - delta) * ATTN_SCALE).astype(BF)
            dk_ref[pl.ds(start, tq), :] += lax.dot_general(ds, qv, (((0,), (0,)), ((), ())), preferred_element_type=F32)
            dv_ref[pl.ds(start, tq), :] += lax.dot_general(p.astype(BF), do_t, (((0,), (0,)), ((), ())),
                                                          preferred_element_type=F32)
            return dq + jnp.dot(ds, k_blk, preferred_element_type=F32)

        dq = lax.fori_loop(0, i, lambda kb, c: block(kb, c, False), jnp.zeros((tq, HEAD_PAD), F32))
        dq_ref[...] = block(i, dq, True)

    return _pcall(
        body, name=name, grid=(HEADS, T // tq),
        ins=[(q, (tq, HEAD_PAD), lambda h, i: (i, h)), (k, (T, HEAD_PAD), lambda h, i: (0, h)),
             (v, (T, V_HEAD), lambda h, i: (0, h)), (o, (tq, V_HEAD), lambda h, i: (i, h)),
             (lse, (tq, V_HEAD), lambda h, i: (i, h)), (do, (tq, V_HEAD), lambda h, i: (i, h))],
        outs=[((T, HEADS * HEAD_PAD), F32, (tq, HEAD_PAD), lambda h, i: (i, h)),
              ((T, HEADS * HEAD_PAD), F32, (T, HEAD_PAD), lambda h, i: (0, h)),
              ((T, MLA_OUT), F32, (T, V_HEAD), lambda h, i: (0, h))],
        semantics=("parallel", "arbitrary"))


def mla_bwd_prep(dq, dk, dv, cos_t, sin_t, *, name):
    T = dq.shape[0]
    tm = _tile(T, ROW_TILE, 8)

    def body(dq_ref, dk_ref, dv_ref, cos_ref, sin_ref, dql_ref, dkvl_ref, dkr_ref):
        cos_v, sin_v = cos_ref[...], sin_ref[...]
        kr = jnp.zeros((tm, LANES), F32)
        for h in range(HEADS):
            lo = h * HEAD_PAD
            dql_ref[:, lo:lo + QK_NOPE] = dq_ref[:, lo:lo + QK_NOPE].astype(BF)
            dql_ref[:, lo + QK_NOPE:lo + HEAD_PAD] = _rope_bwd(
                dq_ref[:, lo + QK_NOPE:lo + HEAD_PAD], cos_v, sin_v).astype(BF)
            dkvl_ref[:, lo:lo + QK_NOPE] = dk_ref[:, lo:lo + QK_NOPE].astype(BF)
            dkvl_ref[:, lo + QK_NOPE:lo + HEAD_PAD] = dv_ref[:, h * V_HEAD:(h + 1) * V_HEAD].astype(BF)
            kr = kr + dk_ref[:, lo + QK_NOPE:lo + HEAD_PAD]
        dkr_ref[...] = _rope_bwd(kr, cos_v, sin_v).astype(BF)

    W = HEADS * HEAD_PAD
    return _pcall(
        body, name=name, grid=(T // tm,),
        ins=[(dq, (tm, W), lambda i: (i, 0)), (dk, (tm, W), lambda i: (i, 0)), (dv, (tm, MLA_OUT), lambda i: (i, 0)),
             (cos_t, (tm, LANES), lambda i: (i, 0)), (sin_t, (tm, LANES), lambda i: (i, 0))],
        outs=[((T, W), BF, (tm, W), lambda i: (i, 0)), ((T, W), BF, (tm, W), lambda i: (i, 0)),
              ((T, LANES), BF, (tm, LANES), lambda i: (i, 0))],
        semantics=("parallel",))


def _group_norm_stats(vg):
    mu = jnp.mean(vg, axis=-1, keepdims=True)
    d = vg - mu
    r = lax.rsqrt(jnp.mean(d * d, axis=-1, keepdims=True) + EPS)
    return d * r, r


def mix_fwd(a, proj, g_mla, g_sgu, v_gain, w_tril, b_full, *, name):
    T = a.shape[0]
    tm = _tile(T, ROW_TILE, CHUNK)
    n_chunk = tm // CHUNK

    def body(a_ref, u_ref, v_ref, gm_ref, gs_ref, vg_ref, w_ref, b_ref, o_ref, s_scr):
        av = a_ref[...]
        o_ref[:, :MLA_OUT] = (av * _rstd(av) * gm_ref[...]).astype(BF)
        for g in range(GROUPS):
            sl = slice(g * CH, (g + 1) * CH)
            vhat, _ = _group_norm_stats(_gelu(v_ref[:, sl]))
            vn = (vhat * vg_ref[:, sl]).astype(BF)
            u = _gelu(u_ref[:, sl])
            for ci in range(n_chunk):
                rs = slice(ci * CHUNK, (ci + 1) * CHUNK)
                y = jnp.dot(w_ref[g], vn[rs], preferred_element_type=F32) + b_ref[:, sl]
                s_scr[rs, sl] = u[rs] * y
        s = s_scr[...]
        o_ref[:, MLA_OUT:] = (s * _rstd(s) * gs_ref[...]).astype(BF)

    return _pcall(
        body, name=name, grid=(T // tm,),
        ins=[(a, (tm, MLA_OUT), lambda i: (i, 0)), (proj, (tm, SGU_OUT), lambda i: (i, 1)),
             (proj, (tm, SGU_OUT), lambda i: (i, 2)), (g_mla, (1, MLA_OUT), lambda i: (0, 0)),
             (g_sgu, (1, SGU_OUT), lambda i: (0, 0)), (v_gain, (1, SGU_OUT), lambda i: (0, 0)),
             (w_tril, (GROUPS, CHUNK, CHUNK), lambda i: (0, 0, 0)), (b_full, (CHUNK, SGU_OUT), lambda i: (0, 0))],
        outs=[((T, MLA_OUT + SGU_OUT), BF, (tm, MLA_OUT + SGU_OUT), lambda i: (i, 0))],
        scratch=[pltpu.VMEM((tm, SGU_OUT), F32)], semantics=("parallel",))[0]


def mix_bwd(dmixed, a, proj, g_mla, g_sgu, v_gain, w_tril, w_tril_t, b_full, *, name, deps=()):
    T = a.shape[0]
    tm = _tile(T, ROW_TILE, CHUNK)
    n_chunk = tm // CHUNK

    def body(dm_a_ref, dm_s_ref, a_ref, u_ref, v_ref, gm_ref, gs_ref, vg_ref, w_ref, wt_ref, b_ref,
             da_ref, duv_ref, dgm_ref, dgs_ref, dvg_ref, dw_ref, db_ref, s_scr, y_scr):
        first = pl.program_id(0) == 0
        da, dgm_rows = _rms_bwd(a_ref[...], gm_ref[...], dm_a_ref[...])
        da_ref[...] = da.astype(BF)
        _acc_rows(dgm_ref, dgm_rows, first)

        for g in range(GROUPS):
            sl = slice(g * CH, (g + 1) * CH)
            vhat, _ = _group_norm_stats(_gelu(v_ref[:, sl]))
            vn = (vhat * vg_ref[:, sl]).astype(BF)
            u = _gelu(u_ref[:, sl])
            for ci in range(n_chunk):
                rs = slice(ci * CHUNK, (ci + 1) * CHUNK)
                y = jnp.dot(w_ref[g], vn[rs], preferred_element_type=F32) + b_ref[:, sl]
                y_scr[rs, sl] = y
                s_scr[rs, sl] = u[rs] * y
        ds, dgs_rows = _rms_bwd(s_scr[...], gs_ref[...], dm_s_ref[...])
        _acc_rows(dgs_ref, dgs_rows, first)
        s_scr[...] = ds

        @pl.when(first)
        def _():
            dw_ref[...] = jnp.zeros_like(dw_ref)
            db_ref[...] = jnp.zeros_like(db_ref)

        for g in range(GROUPS):
            sl = slice(g * CH, (g + 1) * CH)
            upre = u_ref[:, sl]
            vpre = v_ref[:, sl]
            u = _gelu(upre)
            vhat, r = _group_norm_stats(_gelu(vpre))
            gain = vg_ref[:, sl]
            vn = (vhat * gain).astype(BF)
            dsg = s_scr[:, sl]
            duv_ref[:, sl] = (dsg * y_scr[:, sl] * _gelu_grad(upre)).astype(BF)
            dy = dsg * u
            dyb = dy.astype(BF)
            dvn_parts = []
            for ci in range(n_chunk):
                rs = slice(ci * CHUNK, (ci + 1) * CHUNK)
                dvn_parts.append(jnp.dot(wt_ref[g], dyb[rs], preferred_element_type=F32))
                dw_ref[g] += lax.dot_general(dyb[rs], vn[rs], (((1,), (1,)), ((), ())), preferred_element_type=F32)
                db_ref[:, sl] += jnp.broadcast_to(jnp.sum(dy[rs], axis=-1, keepdims=True), (CHUNK, CH))
            dvn = dvn_parts[0] if n_chunk == 1 else jnp.concatenate(dvn_parts, axis=0)
            _acc_rows(dvg_ref.at[:, sl], dvn * vhat, first)
            dvh = dvn * gain
            dvg = r * (dvh - jnp.mean(dvh, axis=-1, keepdims=True)
                       - vhat * jnp.mean(dvh * vhat, axis=-1, keepdims=True))
            duv_ref[:, SGU_OUT + g * CH:SGU_OUT + (g + 1) * CH] = (dvg * _gelu_grad(vpre)).astype(BF)

    return _pcall(
        body, name=name, grid=(T // tm,),
        ins=[(dmixed, (tm, MLA_OUT), lambda i: (i, 0)), (dmixed, (tm, SGU_OUT), lambda i: (i, 1)),
             (a, (tm, MLA_OUT), lambda i: (i, 0)), (proj, (tm, SGU_OUT), lambda i: (i, 1)),
             (proj, (tm, SGU_OUT), lambda i: (i, 2)), (g_mla, (1, MLA_OUT), lambda i: (0, 0)),
             (g_sgu, (1, SGU_OUT), lambda i: (0, 0)), (v_gain, (1, SGU_OUT), lambda i: (0, 0)),
             (w_tril, (GROUPS, CHUNK, CHUNK), lambda i: (0, 0, 0)), (w_tril_t, (GROUPS, CHUNK, CHUNK), lambda i: (0, 0, 0)),
             (b_full, (CHUNK, SGU_OUT), lambda i: (0, 0))],
        outs=[((T, MLA_OUT), BF, (tm, MLA_OUT), lambda i: (i, 0)),
              ((T, 2 * SGU_OUT), BF, (tm, 2 * SGU_OUT), lambda i: (i, 0)),
              ((1, MLA_OUT), F32, (1, MLA_OUT), lambda i: (0, 0)), ((1, SGU_OUT), F32, (1, SGU_OUT), lambda i: (0, 0)),
              ((1, SGU_OUT), F32, (1, SGU_OUT), lambda i: (0, 0)),
              ((GROUPS, CHUNK, CHUNK), F32, (GROUPS, CHUNK, CHUNK), lambda i: (0, 0, 0)),
              ((CHUNK, SGU_OUT), F32, (CHUNK, SGU_OUT), lambda i: (0, 0))],
        scratch=[pltpu.VMEM((tm, SGU_OUT), F32), pltpu.VMEM((tm, SGU_OUT), F32)], deps=deps)


def _shift_down(z, n, row):
    return jnp.where(row >= n, pltpu.roll(z, n, 0), 0.0)


def _shift_up(z, n, row, T):
    return jnp.where(row < T - n, pltpu.roll(z, T - n, 0), 0.0)


def conv_fwd(proj, conv_w, *, name):
    T, D3 = proj.shape
    D = D3 // 3
    tn = _tile(D, 256)
    nj = D // tn

    def body(b_ref, c_ref, x_ref, w_ref, o_ref):
        row = lax.broadcasted_iota(jnp.int32, (T, tn), 0)
        z = c_ref[...] * x_ref[...]
        zc = w_ref[2:3, :] * z + w_ref[1:2, :] * _shift_down(z, 1, row) + w_ref[0:1, :] * _shift_down(z, 2, row)
        o_ref[...] = (b_ref[...] * zc).astype(BF)

    return _pcall(
        body, name=name, grid=(nj,),
        ins=[(proj, (T, tn), lambda j: (0, j)), (proj, (T, tn), lambda j: (0, nj + j)),
             (proj, (T, tn), lambda j: (0, 2 * nj + j)), (conv_w, (3, tn), lambda j: (0, j))],
        outs=[((T, D), BF, (T, tn), lambda j: (0, j))], semantics=("parallel",))[0]


def conv_bwd(dg, proj, conv_w, *, name, deps=()):
    T, D3 = proj.shape
    D = D3 // 3
    tn = _tile(D, 256)
    nj = D // tn

    def body(dg_ref, b_ref, c_ref, x_ref, w_ref, dp_ref, dw_ref, dc_scr, dx_scr):
        part = pl.program_id(1)

        @pl.when(part == 0)
        def _():
            row = lax.broadcasted_iota(jnp.int32, (T, tn), 0)
            c, x = c_ref[...], x_ref[...]
            z = c * x
            z1 = _shift_down(z, 1, row)
            z2 = _shift_down(z, 2, row)
            dgv = dg_ref[...]
            zc = w_ref[2:3, :] * z + w_ref[1:2, :] * z1 + w_ref[0:1, :] * z2
            dp_ref[...] = (dgv * zc).astype(BF)
            dzc = dgv * b_ref[...]
            dw_ref[0:1, :] = jnp.sum(dzc * z2, axis=0, keepdims=True)
            dw_ref[1:2, :] = jnp.sum(dzc * z1, axis=0, keepdims=True)
            dw_ref[2:3, :] = jnp.sum(dzc * z, axis=0, keepdims=True)
            dz = (w_ref[2:3, :] * dzc + w_ref[1:2, :] * _shift_up(dzc, 1, row, T)
                  + w_ref[0:1, :] * _shift_up(dzc, 2, row, T))
            dc_scr[...] = (dz * x).astype(BF)
            dx_scr[...] = (dz * c).astype(BF)

        @pl.when(part == 1)
        def _():
            dp_ref[...] = dc_scr[...]

        @pl.when(part == 2)
        def _():
            dp_ref[...] = dx_scr[...]

    return _pcall(
        body, name=name, grid=(nj, 3),
        ins=[(dg, (T, tn), lambda j, p: (0, j)), (proj, (T, tn), lambda j, p: (0, j)),
             (proj, (T, tn), lambda j, p: (0, nj + j)), (proj, (T, tn), lambda j, p: (0, 2 * nj + j)),
             (conv_w, (3, tn), lambda j, p: (0, j))],
        outs=[((T, D3), BF, (T, tn), lambda j, p: (0, p * nj + j)), ((3, D), F32, (3, tn), lambda j, p: (0, j))],
        scratch=[pltpu.VMEM((T, tn), BF), pltpu.VMEM((T, tn), BF)], semantics=("parallel", "arbitrary"), deps=deps)


def loss_bwd(x_parts, gain, target, *, name):
    T, D = target.shape
    tm = _tile(T, ROW_TILE, 8)
    n_x = len(x_parts)

    def body(*refs):
        x_refs = refs[:n_x]
        g_ref, t_ref, dx_ref, dxb_ref, dg_ref, loss_ref = refs[n_x:]
        first = pl.program_id(0) == 0
        xv = jnp.concatenate([r[...] for r in x_refs], axis=-1) if n_x > 1 else x_refs[0][...]
        r = _rstd(xv)
        xh = xv * r
        gain_v = g_ref[...]
        err = xh * gain_v - t_ref[...]
        part = 0.5 * jnp.sum(jnp.mean(err * err, axis=-1, keepdims=True), axis=0, keepdims=True)
        _acc_rows(loss_ref, jnp.broadcast_to(part, (1, LANES)), first)
        dy = err * (1.0 / D)
        gdy = dy * gain_v
        dx = r * (gdy - xh * jnp.mean(gdy * xh, axis=-1, keepdims=True))
        dx_ref[...] = dx
        dxb_ref[...] = dx.astype(BF)
        _acc_rows(dg_ref, dy * xh, first)

    return _pcall(
        body, name=name, grid=(T // tm,),
        ins=[(p, (tm, D // n_x), lambda i: (i, 0)) for p in x_parts]
        + [(gain, (1, D), lambda i: (0, 0)), (target, (tm, D), lambda i: (i, 0))],
        outs=[((T, D), F32, (tm, D), lambda i: (i, 0)), ((T, D), BF, (tm, D), lambda i: (i, 0)),
              ((1, D), F32, (1, D), lambda i: (0, 0)), ((1, LANES), F32, (1, LANES), lambda i: (0, 0))])


def _adamw(g, w, m, v):
    m = ADAM_B1 * m + (1.0 - ADAM_B1) * g
    v = ADAM_B2 * v + (1.0 - ADAM_B2) * (g * g)
    m_hat = m / ADAM_C1
    v_hat = v / ADAM_C2
    delta = -ADAM_LR * (m_hat / (jnp.sqrt(v_hat) + ADAM_EPS) + ADAM_WD * w)
    return delta, m, v


def adam_flat(g, w, m, v, *, name):
    def body(g_ref, w_ref, m_ref, v_ref, d_ref, nm_ref, nv_ref):
        d, nm, nv = _adamw(g_ref[...], w_ref[...], m_ref[...], v_ref[...])
        d_ref[...] = d
        nm_ref[...] = nm
        nv_ref[...] = nv

    blk = g.shape
    zero = lambda: (0, 0)
    return _pcall(body, name=name, grid=(),
                  ins=[(t, blk, zero) for t in (g, w, m, v)],
                  outs=[(blk, F32, blk, zero)] * 3)


def _chip_slots():
    x, y, c = lax.axis_index("x"), lax.axis_index("y"), lax.axis_index("c")
    chips = [(1 - x, y), (x, 1 - y), (1 - x, 1 - y)]
    return x, y, c, chips


def reduce_adam(gs, a_buf, b_buf, w, m, v, layer, prev, *, name, deps=()):
    L, R, C = w.shape
    tr = _tile(R, max(16, STREAM_BLOCK_ELEMS // C), 16)
    x, y, c, _ = _chip_slots()
    idx = jnp.stack([4 * x + 2 * y + c, 2 * x + y]).astype(jnp.int32)
    n_prev = 0 if prev is None else 4

    def body(idx_ref, g_ref, a_ref, b0_ref, b1_ref, b2_ref, w_ref, m_ref, v_ref, *rest):
        outs = rest[n_prev:]
        g = ((((g_ref[...].astype(F32) + a_ref[...].astype(F32)) + b0_ref[...].astype(F32))
              + b1_ref[...].astype(F32)) + b2_ref[...].astype(F32))
        d, nm, nv = _adamw(g, w_ref[...], m_ref[...], v_ref[...])
        outs[0][...] = g
        outs[1][...] = d
        outs[2][...] = nm
        outs[3][...] = nv

    blk3 = (None, tr, C)
    ins = [(gs, blk3, lambda i, s: (s[0], i, 0)), (a_buf, blk3, lambda i, s: (s[1], i, 0)),
           (b_buf, blk3, lambda i, s: (0, i, 0)), (b_buf, blk3, lambda i, s: (1, i, 0)),
           (b_buf, blk3, lambda i, s: (2, i, 0)),
           (w, blk3, lambda i, s: (layer, i, 0)), (m, blk3, lambda i, s: (layer, i, 0)),
           (v, blk3, lambda i, s: (layer, i, 0))]
    aliases = {}
    if prev is not None:
        for o, p in enumerate(prev):
            ins.append((p, None, None))
            aliases[1 + 8 + o] = o
    outs = [((L, R, C), F32, blk3, lambda i, s: (layer, i, 0))] * 4
    return _pcall(body, name=name, grid=(R // tr,), ins=ins, outs=outs, prefetch=idx, aliases=aliases,
                  semantics=("parallel",), deps=deps)


def device_index():
    x, y, c, chips = _chip_slots()
    return jnp.stack([4 * x + 2 * y + c, 2 * x + y] + [4 * cx + 2 * cy + c for cx, cy in chips]
                     + [2 * cx + cy for cx, cy in chips]).astype(jnp.int32)


def _job_rows(R, n_steps):
    n_blk = max(d for d in range(1, n_steps + 1) if R % d == 0 and (R // d) % 16 == 0)
    return R // n_blk, n_blk


def adam_job(gs, a_buf, b_buf, w, m, v, layer, prev):
    L, R, C = w.shape

    def build(n_steps):
        tr, n_blk = _job_rows(R, n_steps)
        blk = (None, tr, C)
        row = lambda t: jnp.minimum(t, n_blk - 1)
        ins = [(gs, blk, lambda t, s: (s[0], row(t), 0)), (a_buf, blk, lambda t, s: (s[1], row(t), 0))]
        ins += [(b_buf, blk, lambda t, s, j=j: (j, row(t), 0)) for j in range(3)]
        ins += [(p, blk, lambda t, s: (layer, row(t), 0)) for p in (w, m, v)]
        ins += [(p, None, None) for p in (prev or [])]

        def fn(i, o):
            g = ((((i[0][...].astype(F32) + i[1][...].astype(F32)) + i[2][...].astype(F32))
                  + i[3][...].astype(F32)) + i[4][...].astype(F32))
            d, nm, nv = _adamw(g, i[5][...], i[6][...], i[7][...])
            o[0][...] = g
            o[1][...] = d
            o[2][...] = nm
            o[3][...] = nv

        return dict(ins=ins, outs=[((L, R, C), F32, blk, lambda t, s: (layer, row(t), 0))] * 4, fn=fn,
                    aliases={8 + o: o for o in range(4)} if prev else {})

    return build


def pair_job(gs, a_buf):
    _, R, C = gs.shape

    def build(n_steps):
        tr, n_blk = _job_rows(R, n_steps)
        blk = (None, tr, C)
        row = lambda t: jnp.minimum(t, n_blk - 1)
        ins = [(gs, blk, lambda t, s, j=j: (s[2 + j], row(t), 0)) for j in range(3)]
        ins += [(a_buf, blk, lambda t, s, j=j: (s[5 + j], row(t), 0)) for j in range(3)]

        def fn(i, o):
            for j in range(3):
                o[0][j] = (i[j][...].astype(F32) + i[3 + j][...].astype(F32)).astype(BF)

        return dict(ins=ins, outs=[((3, R, C), BF, (3, tr, C), lambda t, s: (0, row(t), 0))], fn=fn, aliases={})

    return build


def reduce_sum(gs, a_buf, b_buf, *, name):
    _, R, C = gs.shape
    tr = _tile(R, 256, 16)
    x, y, c, _ = _chip_slots()
    idx = jnp.stack([4 * x + 2 * y + c, 2 * x + y]).astype(jnp.int32)

    def body(idx_ref, g_ref, a_ref, b0_ref, b1_ref, b2_ref, o_ref):
        o_ref[...] = ((((g_ref[...].astype(F32) + a_ref[...].astype(F32)) + b0_ref[...].astype(F32))
                       + b1_ref[...].astype(F32)) + b2_ref[...].astype(F32))

    blk3 = (None, tr, C)
    return _pcall(body, name=name, grid=(R // tr,),
                  ins=[(gs, blk3, lambda i, s: (s[0], i, 0)), (a_buf, blk3, lambda i, s: (s[1], i, 0)),
                       (b_buf, blk3, lambda i, s: (0, i, 0)), (b_buf, blk3, lambda i, s: (1, i, 0)),
                       (b_buf, blk3, lambda i, s: (2, i, 0))],
                  outs=[((R, C), F32, (tr, C), lambda i, s: (i, 0))], prefetch=idx, semantics=("parallel",))[0]


def adam_rows(g, w, m, v, *, name):
    R, C = g.shape
    tr = _tile(R, 256, 8)

    def body(g_ref, w_ref, m_ref, v_ref, d_ref, nm_ref, nv_ref):
        d, nm, nv = _adamw(g_ref[...], w_ref[...], m_ref[...], v_ref[...])
        d_ref[...] = d
        nm_ref[...] = nm
        nv_ref[...] = nv

    spec = ((tr, C), lambda i: (i, 0))
    return _pcall(body, name=name, grid=(R // tr,), ins=[(t, *spec) for t in (g, w, m, v)],
                  outs=[((R, C), F32, *spec)] * 3, semantics=("parallel",))


def pair_sum(gs, a_buf, *, name):
    _, R, C = gs.shape
    tr = _tile(R, max(16, 2 * STREAM_BLOCK_ELEMS // C), 16)
    x, y, c, chips = _chip_slots()
    idx = jnp.stack([4 * cx + 2 * cy + c for cx, cy in chips] + [2 * cx + cy for cx, cy in chips]).astype(jnp.int32)

    def body(idx_ref, g_ref, a_ref, o_ref):
        o_ref[...] = (g_ref[...].astype(F32) + a_ref[...].astype(F32)).astype(BF)

    blk3 = (None, tr, C)
    return _pcall(body, name=name, grid=(3, R // tr),
                  ins=[(gs, blk3, lambda j, i, s: (s[j], i, 0)), (a_buf, blk3, lambda j, i, s: (s[3 + j], i, 0))],
                  outs=[((3, R, C), BF, blk3, lambda j, i, s: (j, i, 0))], prefetch=idx,
                  semantics=("parallel", "parallel"))[0]


def sum_rows8(gathered, rows, *, name):
    W = gathered.shape[1]

    def body(g_ref, o_ref):
        acc = g_ref[0:rows, :]
        for d in range(1, N_DEV):
            acc = acc + g_ref[d * rows:(d + 1) * rows, :]
        o_ref[...] = acc

    return _pcall(body, name=name, grid=(), ins=[(gathered, gathered.shape, lambda: (0, 0))],
                  outs=[((rows, W), F32, (rows, W), lambda: (0, 0))])[0]


HBM_SPEC = pl.BlockSpec(memory_space=pltpu.HBM)
SEM_SPEC = pl.BlockSpec(memory_space=pltpu.SEMAPHORE)
ANY_SPEC = pl.BlockSpec(memory_space=pl.ANY)
DATAFLOW = pltpu.SideEffectType.DATAFLOW_SIDE_EFFECTING


def _in_hbm(v):
    return pltpu.with_memory_space_constraint(v, pltpu.HBM)


def _slot(p):
    return 4 * p[0] + 2 * p[1] + p[2]


def _gather_peers():
    x, y, c, chips = _chip_slots()
    return (x, y, c), [(x, y, 1 - c)] + [(*chip, c) for chip in chips]


def gather_start(groups, after, *, name):
    flat = [s for g in groups for s in g]
    n, n_g = len(flat), len(groups)
    where = [(gi, ti) for gi, g in enumerate(groups) for ti in range(len(g))]

    def body(*refs):
        src, land = refs[:n], refs[n:2 * n]
        sems = refs[2 * n + 1:2 * n + 1 + 2 * n_g]
        me, peers = _gather_peers()
        for t in range(n):
            gi, ti = where[t]
            for k, to in enumerate(peers):
                pltpu.make_async_remote_copy(
                    src_ref=src[t], dst_ref=land[t].at[_slot(me)], send_sem=sems[2 * gi].at[4 * ti + k],
                    recv_sem=sems[2 * gi + 1].at[4 * ti + k], device_id=to, device_id_type=MESH).start()
        refs[-1][...] = jnp.zeros_like(refs[-1])

    out_shape = []
    for g in groups:
        out_shape += [pltpu.SemaphoreType.DMA((4 * len(g),)), pltpu.SemaphoreType.DMA((4 * len(g),))]
    out_shape += [pltpu.HBM(s.shape, s.dtype) for s in flat]
    out_shape += [pltpu.HBM((N_DEV,) + s.shape, s.dtype) for s in flat]
    out_shape += [jax.ShapeDtypeStruct((8, LANES), F32)]
    aliases = {t: 2 * n_g + t for t in range(n)}
    aliases.update({n + t: 2 * n_g + n + t for t in range(n)})
    res = pl.pallas_call(
        body, name=name, out_shape=out_shape, in_specs=[HBM_SPEC] * (2 * n) + [ANY_SPEC],
        out_specs=[SEM_SPEC] * (2 * n_g) + [HBM_SPEC] * (2 * n) + [pl.BlockSpec(memory_space=pltpu.VMEM)],
        input_output_aliases=aliases, compiler_params=pltpu.CompilerParams(has_side_effects=DATAFLOW),
    )(*[_in_hbm(s) for s in flat], *[_in_hbm(lax.empty((N_DEV,) + s.shape, s.dtype)) for s in flat], after)
    out, off = [], 0
    for gi, g in enumerate(groups):
        k = len(g)
        out.append((res[2 * gi], res[2 * gi + 1], res[2 * n_g + off:2 * n_g + off + k],
                    res[2 * n_g + n + off:2 * n_g + n + off + k]))
        off += k
    return out, res[-1]


def gather_wait(started, after, *, name):
    send_sems, recv_sems, srcs, lands = started
    n = len(srcs)
    after = list(after)

    def body(*refs):
        src, land = refs[:n], refs[n:2 * n]
        send, recv = refs[2 * n], refs[2 * n + 1]
        _, peers = _gather_peers()
        for t in range(n):
            for k, frm in enumerate(peers):
                cp = pltpu.make_async_remote_copy(
                    src_ref=src[t], dst_ref=land[t].at[_slot(frm)], send_sem=send.at[4 * t + k],
                    recv_sem=recv.at[4 * t + k],
                    device_id=frm, device_id_type=MESH)
                cp.wait_send()
                cp.wait_recv()

    res = pl.pallas_call(
        body, name=name,
        out_shape=[pltpu.HBM(s.shape, s.dtype) for s in srcs] + [pltpu.HBM(l.shape, l.dtype) for l in lands],
        in_specs=[HBM_SPEC] * (2 * n) + [SEM_SPEC, SEM_SPEC] + [ANY_SPEC] * len(after),
        out_specs=[HBM_SPEC] * (2 * n), input_output_aliases={t: t for t in range(2 * n)},
        compiler_params=pltpu.CompilerParams(has_side_effects=DATAFLOW),
    )(*srcs, *lands, send_sems, recv_sems, *after)
    return res[:n], res[n:]


def place_own(src, land, *, name):
    R, C = src.shape
    tr = _tile(R, 512, 16)
    x, y, c, _ = _chip_slots()
    idx = jnp.stack([4 * x + 2 * y + c]).astype(jnp.int32)

    def body(idx_ref, s_ref, land_ref, o_ref):
        o_ref[...] = s_ref[...]

    return _pcall(body, name=name, grid=(R // tr,),
                  ins=[(src, (tr, C), lambda i, s: (i, 0)), (land, None, None)],
                  outs=[(land.shape, land.dtype, (None, tr, C), lambda i, s: (s[0], i, 0))],
                  prefetch=idx, aliases={2: 0}, semantics=("parallel",))[0]


def gather_finish(srcs, lands, *, name):
    n = len(srcs)

    def body(*refs):
        land = refs[n:2 * n]
        send_sems, recv_sems = refs[2 * n:]
        x, y, c, chips = _chip_slots()
        me, sibling = (x, y, c), (x, y, 1 - c)

        def copy(t, j, block, to):
            return pltpu.make_async_remote_copy(
                src_ref=land[t].at[_slot(block)], dst_ref=land[t].at[_slot(block)], send_sem=send_sems.at[t, j],
                recv_sem=recv_sems.at[t, j], device_id=to, device_id_type=MESH)

        sends = [copy(t, j, (*chip, c), sibling) for t in range(n) for j, chip in enumerate(chips)]
        for cp in sends:
            cp.start()
        for t in range(n):
            for j, chip in enumerate(chips):
                copy(t, j, (*chip, 1 - c), me).wait_recv()
        for cp in sends:
            cp.wait_send()

    passed = pl.pallas_call(
        body, name=name, out_shape=[jax.ShapeDtypeStruct(l.shape, l.dtype) for l in lands],
        in_specs=[ANY_SPEC] * n, out_specs=[ANY_SPEC] * n,
        input_output_aliases={t: t for t in range(n)},
        scratch_shapes=[pltpu.SemaphoreType.DMA((n, 3)), pltpu.SemaphoreType.DMA((n, 3))],
    )(*lands)
    return [place_own(s, l, name=f"{name}_own{t}") for t, (s, l) in enumerate(zip(srcs, passed))]


def chips_start(pairs, *, name):
    n = len(pairs)

    def body(*refs):
        src, land = refs[:n], refs[n:2 * n]
        send, recv = refs[2 * n], refs[2 * n + 1]
        token = refs[-1]
        x, y, c, chips = _chip_slots()
        for t in range(n):
            for j, chip in enumerate(chips):
                pltpu.make_async_remote_copy(
                    src_ref=src[t].at[j], dst_ref=land[t].at[j], send_sem=send.at[3 * t + j],
                    recv_sem=recv.at[3 * t + j], device_id=(*chip, c), device_id_type=MESH).start()
        token[...] = jnp.zeros_like(token)

    res = pl.pallas_call(
        body, name=name,
        out_shape=[pltpu.SemaphoreType.DMA((3 * n,)), pltpu.SemaphoreType.DMA((3 * n,))]
        + [pltpu.HBM(p.shape, p.dtype) for p in pairs] * 2 + [jax.ShapeDtypeStruct((8, LANES), F32)],
        in_specs=[HBM_SPEC] * (2 * n),
        out_specs=[SEM_SPEC, SEM_SPEC] + [HBM_SPEC] * (2 * n) + [pl.BlockSpec(memory_space=pltpu.VMEM)],
        input_output_aliases={t: 2 + t for t in range(2 * n)},
        compiler_params=pltpu.CompilerParams(has_side_effects=DATAFLOW),
    )(*[_in_hbm(p) for p in pairs], *[_in_hbm(lax.empty(p.shape, p.dtype)) for p in pairs])
    return res[0], res[1], res[2:2 + n], res[2 + n:2 + 2 * n], res[-1]


def chips_wait(started, after, *, name):
    send_sems, recv_sems, srcs, lands, _ = started
    n = len(srcs)

    def body(*refs):
        src, land = refs[:n], refs[n:2 * n]
        send, recv = refs[2 * n], refs[2 * n + 1]
        x, y, c, chips = _chip_slots()
        for t in range(n):
            for j, chip in enumerate(chips):
                cp = pltpu.make_async_remote_copy(
                    src_ref=src[t].at[j], dst_ref=land[t].at[j], send_sem=send.at[3 * t + j],
                    recv_sem=recv.at[3 * t + j], device_id=(*chip, c), device_id_type=MESH)
                cp.wait_send()
                cp.wait_recv()

    res = pl.pallas_call(
        body, name=name, out_shape=[pltpu.HBM(s.shape, s.dtype) for s in srcs] * 2,
        in_specs=[HBM_SPEC] * (2 * n) + [SEM_SPEC, SEM_SPEC, ANY_SPEC], out_specs=[HBM_SPEC] * (2 * n),
        input_output_aliases={t: t for t in range(2 * n)},
        compiler_params=pltpu.CompilerParams(has_side_effects=DATAFLOW),
    )(*srcs, *lands, send_sems, recv_sems, after)
    return res[n:]


def _sibling_copies(src, land, send, recv, n):
    x, y, c, _ = _chip_slots()
    return [pltpu.make_async_remote_copy(
        src_ref=src[t].at[4 * (q // 2) + 2 * (q % 2) + (1 - c)], dst_ref=land[t].at[q], send_sem=send.at[4 * t + q],
        recv_sem=recv.at[4 * t + q], device_id=(x, y, 1 - c), device_id_type=MESH)
        for t in range(n) for q in range(4)]


def sibling_start(gs, *, name):
    n = len(gs)

    def body(*refs):
        for cp in _sibling_copies(refs[:n], refs[n:2 * n], refs[2 * n], refs[2 * n + 1], n):
            cp.start()
        refs[-1][...] = jnp.zeros_like(refs[-1])

    lands = [lax.empty((4,) + g.shape[1:], g.dtype) for g in gs]
    res = pl.pallas_call(
        body, name=name,
        out_shape=[pltpu.SemaphoreType.DMA((4 * n,)), pltpu.SemaphoreType.DMA((4 * n,))]
        + [pltpu.HBM(g.shape, g.dtype) for g in gs] + [pltpu.HBM(l.shape, l.dtype) for l in lands]
        + [jax.ShapeDtypeStruct((8, LANES), F32)],
        in_specs=[HBM_SPEC] * (2 * n),
        out_specs=[SEM_SPEC, SEM_SPEC] + [HBM_SPEC] * (2 * n) + [pl.BlockSpec(memory_space=pltpu.VMEM)],
        input_output_aliases={t: 2 + t for t in range(2 * n)},
        compiler_params=pltpu.CompilerParams(has_side_effects=DATAFLOW),
    )(*[_in_hbm(g) for g in gs], *[_in_hbm(l) for l in lands])
    return res[0], res[1], res[2:2 + n], res[2 + n:2 + 2 * n], res[-1]


def sibling_wait(started, after, *, name):
    send_sems, recv_sems, srcs, lands, _ = started
    n = len(srcs)

    def body(*refs):
        for cp in _sibling_copies(refs[:n], refs[n:2 * n], refs[2 * n], refs[2 * n + 1], n):
            cp.wait_send()
            cp.wait_recv()

    res = pl.pallas_call(
        body, name=name,
        out_shape=[pltpu.HBM(s.shape, s.dtype) for s in srcs] + [pltpu.HBM(l.shape, l.dtype) for l in lands],
        in_specs=[HBM_SPEC] * (2 * n) + [SEM_SPEC, SEM_SPEC, ANY_SPEC], out_specs=[HBM_SPEC] * (2 * n),
        input_output_aliases={t: t for t in range(2 * n)},
        compiler_params=pltpu.CompilerParams(has_side_effects=DATAFLOW),
    )(*srcs, *lands, send_sems, recv_sems, after)
    return res[:n], res[n:]


def all_gather_vmem(x_shard, *, name, after=None):
    m_per, n = x_shard.shape
    n_after = 0 if after is None else 1

    def body(x_ref, *rest):
        out_ref, send_sems, recv_sems, local_sem = rest[n_after:]
        x, y, c, chips = _chip_slots()
        me, sibling = (x, y, c), (x, y, 1 - c)

        def rows(px, py, pc):
            return out_ref.at[pl.ds((4 * px + 2 * py + pc) * m_per, m_per), :]

        def copy(k, block, to, src=None):
            return pltpu.make_async_remote_copy(
                src_ref=rows(*block) if src is None else src, dst_ref=rows(*block),
                send_sem=send_sems.at[k], recv_sem=recv_sems.at[k], device_id=to, device_id_type=MESH)

        mine = pltpu.make_async_copy(x_ref, rows(*me), local_sem)
        mine.start()
        first = [copy(0, me, sibling, src=x_ref)]
        first += [copy(1 + j, me, (*chip, c), src=x_ref) for j, chip in enumerate(chips)]
        for cp in first:
            cp.start()
        passed = [copy(4 + j, (*chip, c), sibling) for j, chip in enumerate(chips)]
        for j, chip in enumerate(chips):
            copy(1 + j, (*chip, c), me).wait_recv()
            passed[j].start()
        copy(0, sibling, me).wait_recv()
        for j, chip in enumerate(chips):
            copy(4 + j, (*chip, 1 - c), me).wait_recv()
        for cp in first + passed:
            cp.wait_send()
        mine.wait()

    vmem = pl.BlockSpec(memory_space=pltpu.VMEM)
    return pl.pallas_call(
        body, name=name, out_shape=jax.ShapeDtypeStruct((N_DEV * m_per, n), x_shard.dtype),
        in_specs=[vmem] + [ANY_SPEC] * n_after, out_specs=vmem,
        scratch_shapes=[pltpu.SemaphoreType.DMA((7,)), pltpu.SemaphoreType.DMA((7,)), pltpu.SemaphoreType.DMA],
        compiler_params=pltpu.CompilerParams(vmem_limit_bytes=int(min(
            VMEM_LIMIT_CAP, 2 * (N_DEV + 1) * m_per * n * x_shard.dtype.itemsize + 16 * 2 ** 20))),
    )(x_shard, *([] if after is None else [after]))


def _rope_slab(cols):
    z = jnp.zeros(cols.shape[:-1] + (HALF_ROPE,), cols.dtype)
    return jnp.concatenate([cols[..., :HALF_ROPE], z, cols[..., HALF_ROPE:], z], axis=-1)


def _rope_unslab(slab):
    return jnp.concatenate([slab[..., :HALF_ROPE], slab[..., 2 * HALF_ROPE:3 * HALF_ROPE]], axis=-1)


def _pack_w_in_t(wt_g):
    s, c, d = wt_g.shape
    w = wt_g.reshape(s * c, d)
    c2, c3 = Q_LORA + KV_LORA, Q_LORA + KV_LORA + QK_ROPE
    r = w[c2:c3]
    z = jnp.zeros((HALF_ROPE, d), w.dtype)
    return jnp.concatenate([w[:c2], w[c3:], r[:HALF_ROPE], z, r[HALF_ROPE:], z], axis=0)


def _unpack_w_in_t_grad(dwt):
    d = dwt.shape[1]
    c2 = Q_LORA + KV_LORA
    uv = 2 * SGU_OUT
    slab = dwt[c2 + uv:]
    g = jnp.concatenate([dwt[:c2], slab[:HALF_ROPE], slab[2 * HALF_ROPE:3 * HALF_ROPE], dwt[c2:c2 + uv]], axis=0)
    return g.reshape(N_DEV, g.shape[0] // N_DEV, d)


def _rope_tables(positions):
    inv_freq = ROPE_BASE ** (-jnp.arange(0, QK_ROPE, 2, dtype=F32) / QK_ROPE)
    ang = positions.astype(F32)[:, None] * inv_freq
    cos, sin = jnp.cos(ang), jnp.sin(ang)
    z = jnp.zeros_like(cos)
    return jnp.concatenate([cos, z, cos, z], axis=-1), jnp.concatenate([-sin, z, sin, z], axis=-1)


def _mlp_up(x, gain, w1, tag):
    hn = rms_fwd(x, gain, name=f"mlp{tag}_norm")

    def act_epi(acc):
        a = jnp.maximum(acc, 0.0)
        return a, a * a

    T = x.shape[0]
    F = w1.shape[0] * w1.shape[2]
    a, act = mm(hn, w1, name=f"mlp{tag}_up", outs=[((T, F), BF, None), ((T, F), BF, None)], epi=act_epi)
    return hn, a, act


def _mlp_down(x, act, w2, tag, part=0):
    n = w2.shape[1]
    bm = _tile(x.shape[0], MM_TILE)
    bn = _tile(n, MM_TILE)
    per = n // bn
    return mm(act, w2, name=f"mlp{tag}_down{part}", out=((x.shape[0], n), F32), bm=bm, bn=bn,
              epi=lambda acc, r: (acc + r[...],), epi_ins=[(x, (bm, bn), lambda i, j, k: (i, part * per + j))])


def _mlp_bwd_weights(w1, w2, saved, dxb, tag):
    hn, a, act = saved
    T, D = dxb.shape
    F = a.shape[1]
    bm = _tile(T, MM_TILE)
    bn = _tile(F, min(MM_TILE, w1.shape[2]))
    dhid = mm(dxb, w2, tb=True, name=f"mlp{tag}_dhid", out=((T, F), BF), bm=bm, bn=bn,
              epi=lambda acc, a_ref: (2.0 * a_ref[...].astype(F32) * acc,),
              epi_ins=[(a, (bm, bn), lambda i, j, k: (i, j))])
    dw2 = mm(act, dxb, ta=True, name=f"mlp{tag}_dw2", out=((F, D), BF))
    dw1 = mm(hn, dhid, ta=True, name=f"mlp{tag}_dw1", out=(w1.shape, BF))
    return dhid, dw1, dw2.reshape(N_DEV, F // N_DEV, D)


def _reduce_begin(grads, tag):
    return sibling_start(grads, name=f"reduce_sibling_start_{tag}")


def _reduce_continue(sib, after, tag):
    grads, a_bufs = sibling_wait(sib, after, name=f"reduce_sibling_wait_{tag}")
    pairs = [pair_sum(g, a, name=f"pair_sum_{tag}{t}") for t, (g, a) in enumerate(zip(grads, a_bufs))]
    return grads, a_bufs, chips_start(pairs, name=f"reduce_chips_start_{tag}")


def kernel(x, positions, e_norm_mix, e_w_in, e_q_norm, e_w_uq, e_kv_norm, e_w_ukv, e_v_norm, e_sgu_w, e_sgu_b, e_mla_out_norm, e_sgu_out_norm, e_w_out, o_norm_mix, o_w_in, o_conv_w, o_w_out, mlp_norm, mlp_w1, mlp_w2, final_norm, loss_target, m_e_norm_mix, m_e_w_in, m_e_q_norm, m_e_w_uq, m_e_kv_norm, m_e_w_ukv, m_e_v_norm, m_e_sgu_w, m_e_sgu_b, m_e_mla_out_norm, m_e_sgu_out_norm, m_e_w_out, m_o_norm_mix, m_o_w_in, m_o_conv_w, m_o_w_out, m_mlp_norm, m_mlp_w1, m_mlp_w2, m_final_norm, v_e_norm_mix, v_e_w_in, v_e_q_norm, v_e_w_uq, v_e_kv_norm, v_e_w_ukv, v_e_v_norm, v_e_sgu_w, v_e_sgu_b, v_e_mla_out_norm, v_e_sgu_out_norm, v_e_w_out, v_o_norm_mix, v_o_w_in, v_o_conv_w, v_o_w_out, v_mlp_norm, v_mlp_w1, v_mlp_w2, v_final_norm):
    T, D = x.shape[1], x.shape[2]
    d_shard = o_norm_mix.shape[1]
    x0 = x[0]
    target = loss_target[0]
    me = 4 * lax.axis_index("x") + 2 * lax.axis_index("y") + lax.axis_index("c")

    bf = lambda s: s.astype(BF)
    gather_groups = [[bf(jnp.transpose(e_w_in[0])), bf(e_w_uq[0]), bf(e_w_ukv[0])], [bf(e_w_out[0]), bf(mlp_w1[0])],
                     [bf(mlp_w2[0]), bf(o_w_in[0])], [bf(o_w_out[0]), bf(mlp_w1[1])], [bf(mlp_w2[1])]]
    small_rows = jnp.concatenate([o_norm_mix, o_conv_w[0], jnp.zeros((4, d_shard), F32)], axis=0)
    small_flat = all_gather_vmem(small_rows, name="gather_small")
    started, start_token = gather_start(gather_groups[:1], small_flat, name="gather_start0")
    started_rest, rest_token = gather_start(gather_groups[1:], start_token, name="gather_start1")
    started += started_rest

    def gathered(gi, after):
        srcs, lands = gather_wait(started[gi], after, name=f"gather_wait{gi}")
        return gather_finish(srcs, lands, name=f"gather_finish{gi}")

    small_g = small_flat.reshape(N_DEV, 8, d_shard)
    o_norm_full = small_g[:, 0, :].reshape(1, D)
    conv_w_full = jnp.transpose(small_g[:, 1:4, :], (1, 0, 2)).reshape(3, D)
    w_tril = jnp.tril(e_sgu_w[0])
    w_tril_b = w_tril.astype(BF)
    w_tril_tb = jnp.swapaxes(w_tril, 1, 2).astype(BF)
    b_full = jnp.repeat(e_sgu_b[0].T, CH, axis=1)
    v_gain = e_v_norm[0].reshape(1, SGU_OUT)
    cos_t, sin_t = _rope_tables(positions[0])
    mlp_gain = [mlp_norm[0:1], mlp_norm[1:2]]
    final_gain = final_norm.reshape(1, D)

    h0 = rms_fwd(x0, e_norm_mix, name="e_norm", deps=[rest_token])
    g_w_in_t, g_w_uq, w_ukv = gathered(
        0, [h0, cos_t, sin_t, w_tril_b, w_tril_tb, b_full, o_norm_full, conv_w_full])
    w_in_t = _pack_w_in_t(g_w_in_t)
    w_uq = jnp.concatenate([g_w_uq[..., :QK_NOPE], _rope_slab(g_w_uq[..., QK_NOPE:])], axis=-1)
    proj = mm(h0, w_in_t, tb=True, name="e_in", out=((T, w_in_t.shape[0]), F32), bn=_tile(w_in_t.shape[0], 640))
    qn, kvn, krope = mla_prep(proj, e_q_norm, e_kv_norm, cos_t, sin_t, name="mla_prep")
    bm = _tile(T, MM_TILE)

    def q_epi(acc, cos_ref, sin_ref):
        return (jnp.concatenate([acc[:, :QK_NOPE], _rope_fwd(acc[:, QK_NOPE:], cos_ref[...], sin_ref[...])], axis=-1),)

    q = mm(qn, w_uq, name="mla_q", out=((T, HEADS * HEAD_PAD), BF), bm=bm, bn=HEAD_PAD, epi=q_epi,
           epi_ins=[(cos_t, (bm, LANES), lambda i, j, k: (i, 0)), (sin_t, (bm, LANES), lambda i, j, k: (i, 0))])

    def kv_epi(acc, kr_ref):
        return jnp.concatenate([acc[:, :QK_NOPE].astype(BF), kr_ref[...]], axis=-1), acc[:, QK_NOPE:]

    k, v = mm(kvn, w_ukv, name="mla_kv", bm=bm, bn=HEAD_PAD, epi=kv_epi,
              outs=[((T, HEADS * HEAD_PAD), BF, HEAD_PAD), ((T, MLA_OUT), BF, V_HEAD)],
              epi_ins=[(krope, (bm, LANES), lambda i, j, k: (i, 0))])
    attn, attn_lse = attn_fwd(q, k, v, name="attn_fwd")
    mixed = mix_fwd(attn, proj, e_mla_out_norm, e_sgu_out_norm, v_gain, w_tril_b, b_full, name="mix_fwd")
    bn = _tile(D, MM_TILE)
    g_w_out_e, w1_0 = gathered(1, [mixed])
    w_out_e = g_w_out_e.reshape(-1, D)
    x1 = mm(mixed, w_out_e, name="e_out", out=((T, D), F32), bm=bm, bn=bn,
            epi=lambda acc, r: (acc + r[...],), epi_ins=[(x0, (bm, bn), lambda i, j, k: (i, j))])
    hn0, a0, act0 = _mlp_up(x1, mlp_gain[0], w1_0, 0)
    g_w2_0, g_w_in_o = gathered(2, [act0])
    w2_0 = g_w2_0.reshape(-1, D)
    x2 = _mlp_down(x1, act0, w2_0, 0)
    ho = rms_fwd(x2, o_norm_full, name="o_norm")
    proj_o = mm(ho, g_w_in_o, name="o_in", out=((T, 3 * D), F32))
    gated = conv_fwd(proj_o, conv_w_full, name="conv_fwd")
    g_w_out_o, w1_1 = gathered(3, [gated])
    w_out_o = g_w_out_o.reshape(-1, D)
    x3 = mm(gated, w_out_o, name="o_out", out=((T, D), F32), bm=bm, bn=bn,
            epi=lambda acc, r: (acc + r[...],), epi_ins=[(x2, (bm, bn), lambda i, j, k: (i, j))])
    hn1, a1, act1 = _mlp_up(x3, mlp_gain[1], w1_1, 1)
    (g_w2_1,) = gathered(4, [act1])
    w2_1 = g_w2_1.reshape(-1, D)
    x4 = _mlp_down(x3, act1, w2_1, 1)
    w1, w2 = [w1_0, w1_1], [w2_0, w2_1]
    mlp0_saved, mlp1_saved = (hn0, a0, act0), (hn1, a1, act1)

    dx4, dx4b, d_final, loss_part = loss_bwd([x4], final_gain, target, name="loss_bwd")

    hosted = dict(job_index=device_index())
    dhid1, dw1_1, dw2_1 = _mlp_bwd_weights(w1[1], w2[1], mlp1_saved, dx4b, 1)
    sib_r0 = _reduce_begin([dw1_1, dw2_1], "r0")
    dhn1 = mm(dhid1, w1[1], tb=True, name="mlp1_dhn", out=((T, D), F32), deps=[sib_r0[-1]])
    grads_r0, a_r0 = sibling_wait(sib_r0, dhn1, name="reduce_sibling_wait_r0")
    dx3, dx3b, d_mlp1 = rms_bwd(x3, mlp_gain[1], dhn1, dres=dx4, name="mlp1_norm_bwd")

    dgated, ((pair_r0a,),) = mm(dx3b, w_out_o, tb=True, name="o_out_dx", out=((T, D), F32),
                                jobs=[pair_job(grads_r0[0], a_r0[0])], **hosted)
    dw_out_o, ((pair_r0b,),) = mm(gated, dx3b, ta=True, name="o_out_dw", out=((D, D), BF),
                                  jobs=[pair_job(grads_r0[1], a_r0[1])], **hosted)
    st_r0 = chips_start([pair_r0a, pair_r0b], name="reduce_chips_start_r0")
    dproj_o, dconv_full = conv_bwd(dgated, proj_o, conv_w_full, name="conv_bwd", deps=[st_r0[-1]])
    dw_in_o = mm(ho, dproj_o, ta=True, name="o_in_dw", out=(g_w_in_o.shape, BF))
    sib_r1 = _reduce_begin([dw_out_o.reshape(g_w_out_o.shape), dw_in_o], "r1")
    dho = mm(dproj_o, g_w_in_o, tb=True, name="o_in_dx", out=((T, D), F32), deps=[sib_r1[-1]])
    grads_r1, a_r1 = sibling_wait(sib_r1, dho, name="reduce_sibling_wait_r1")
    dx2, dx2b, d_onorm_full = rms_bwd(x2, o_norm_full, dho, dres=dx3, name="o_norm_bwd")

    hn0, a0, act0 = mlp0_saved
    d_ff = a0.shape[1]
    bm_h, bn_h = _tile(T, MM_TILE), _tile(d_ff, min(MM_TILE, w1[0].shape[2]))
    dhid0, ((pair_r1a,), (pair_r1b,)) = mm(
        dx2b, w2[0], tb=True, name="mlp0_dhid", out=((T, d_ff), BF), bm=bm_h, bn=bn_h,
        epi=lambda acc, a_ref: (2.0 * a_ref[...].astype(F32) * acc,),
        epi_ins=[(a0, (bm_h, bn_h), lambda i, j, k: (i, j))],
        jobs=[pair_job(grads_r1[0], a_r1[0]), pair_job(grads_r1[1], a_r1[1])], **hosted)
    st_r1 = chips_start([pair_r1a, pair_r1b], name="reduce_chips_start_r1")
    dw2_0 = mm(act0, dx2b, ta=True, name="mlp0_dw2", out=((d_ff, D), BF), deps=[st_r1[-1]])
    b_r0 = chips_wait(st_r0, dw2_0, name="reduce_chips_wait_r0")
    dw1_0, (r_w1, r_w2) = mm(
        hn0, dhid0, ta=True, name="mlp0_dw1", out=(w1[0].shape, BF),
        jobs=[adam_job(grads_r0[0], a_r0[0], b_r0[0], mlp_w1, m_mlp_w1, v_mlp_w1, 1, None),
              adam_job(grads_r0[1], a_r0[1], b_r0[1], mlp_w2, m_mlp_w2, v_mlp_w2, 1, None)], **hosted)
    sib_r2 = _reduce_begin([dw1_0, dw2_0.reshape(N_DEV, d_ff // N_DEV, D)], "r2")
    dhn0 = mm(dhid0, w1[0], tb=True, name="mlp0_dhn", out=((T, D), F32), deps=[sib_r2[-1]])
    grads_r2, a_r2 = sibling_wait(sib_r2, dhn0, name="reduce_sibling_wait_r2")
    dx1, dx1b, d_mlp0 = rms_bwd(x1, mlp_gain[0], dhn0, dres=dx2, name="mlp0_norm_bwd")

    dmixed, ((pair_r2a,),) = mm(dx1b, w_out_e, tb=True, name="e_out_dx", out=((T, MLA_OUT + SGU_OUT), F32),
                                jobs=[pair_job(grads_r2[0], a_r2[0])], **hosted)
    dw_out_e, ((pair_r2b,),) = mm(mixed, dx1b, ta=True, name="e_out_dw", out=(w_out_e.shape, BF),
                                  jobs=[pair_job(grads_r2[1], a_r2[1])], **hosted)
    st_r2 = chips_start([pair_r2a, pair_r2b], name="reduce_chips_start_r2")
    (dattn, duv, d_mla_out, d_sgu_out, d_vgain, d_sgu_w, d_b_full) = mix_bwd(
        dmixed, attn, proj, e_mla_out_norm, e_sgu_out_norm, v_gain, w_tril_b, w_tril_tb, b_full, name="mix_bwd",
        deps=[st_r2[-1]])
    b_r1 = chips_wait(st_r1, dattn, name="reduce_chips_wait_r1")
    dq, dk, dv = attn_bwd(q, k, v, attn, attn_lse, dattn, name="attn_bwd")
    dq_lin, dkv_lin, dkr = mla_bwd_prep(dq, dk, dv, cos_t, sin_t, name="mla_bwd_prep")
    dw_uq_pad = mm(qn, dq_lin, ta=True, name="mla_q_dw", out=(w_uq.shape, BF))
    dw_ukv = mm(kvn, dkv_lin, ta=True, name="mla_kv_dw", out=(w_ukv.shape, BF))
    dw_uq = jnp.concatenate([dw_uq_pad[..., :QK_NOPE], _rope_unslab(dw_uq_pad[..., QK_NOPE:])], axis=-1)
    sib_r2b = _reduce_begin([dw_out_e.reshape(g_w_out_e.shape), dw_uq, dw_ukv], "r2b")
    dqn = mm(dq_lin, w_uq, tb=True, name="mla_q_dx", out=((T, Q_LORA), F32), deps=[sib_r2b[-1]])
    dkvn = mm(dkv_lin, w_ukv, tb=True, name="mla_kv_dx", out=((T, KV_LORA), F32), deps=[sib_r2b[-1]])
    grads_r2b, a_r2b, st_r2b = _reduce_continue(sib_r2b, dkvn, "r2b")
    dcq, d_qnorm = rms_bwd(proj, e_q_norm, dqn, col_block=0, want_f32=False, name="q_norm_bwd", deps=[st_r2b[-1]])
    dckv, d_kvnorm = rms_bwd(proj, e_kv_norm, dkvn, col_block=1, want_f32=False, name="kv_norm_bwd")
    dproj = jnp.concatenate([dcq, dckv, duv, dkr], axis=-1)
    dw_in_t_pad, (r_w_out_o, r_w_in_o) = mm(
        dproj, h0, ta=True, name="e_in_dw", out=(w_in_t.shape, BF), bm=_tile(w_in_t.shape[0], 640),
        jobs=[adam_job(grads_r1[0], a_r1[0], b_r1[0], o_w_out, m_o_w_out, v_o_w_out, 0, None),
              adam_job(grads_r1[1], a_r1[1], b_r1[1], o_w_in, m_o_w_in, v_o_w_in, 0, None)], **hosted)
    dw_in_t = _unpack_w_in_t_grad(dw_in_t_pad)
    sib_r3 = _reduce_begin([dw_in_t], "r3")
    dh0 = mm(dproj, w_in_t, name="e_in_dx", out=((T, D), F32), deps=[sib_r3[-1]])
    grads_r3, a_r3, st_r3 = _reduce_continue(sib_r3, dh0, "r3")
    tok_r3 = st_r3[-1]
    grad_x, d_enorm = rms_bwd(x0, e_norm_mix, dh0, dres=dx1, want_bf=False, name="e_norm_bwd", deps=[tok_r3])
    b_r2 = chips_wait(st_r2, grad_x, name="reduce_chips_wait_r2")

    def finish(grads, a_bufs, b_bufs, t, w, m, v, layer=0, prev=None, tag="", deps=()):
        return reduce_adam(grads[t], a_bufs[t], b_bufs[t], w, m, v, layer, prev, name=f"adam_{tag}", deps=deps)

    r_w1 = finish(grads_r2, a_r2, b_r2, 0, mlp_w1, m_mlp_w1, v_mlp_w1, 0, r_w1, tag="w1_l0", deps=[tok_r3])
    r_w2 = finish(grads_r2, a_r2, b_r2, 1, mlp_w2, m_mlp_w2, v_mlp_w2, 0, r_w2, tag="w2_l0", deps=[r_w1[1]])
    b_r2b = chips_wait(st_r2b, r_w2[1], name="reduce_chips_wait_r2b")
    r_w_out_e = finish(grads_r2b, a_r2b, b_r2b, 0, e_w_out, m_e_w_out, v_e_w_out, tag="e_w_out")
    r_w_uq = finish(grads_r2b, a_r2b, b_r2b, 1, e_w_uq, m_e_w_uq, v_e_w_uq, tag="e_w_uq")
    r_w_ukv = finish(grads_r2b, a_r2b, b_r2b, 2, e_w_ukv, m_e_w_ukv, v_e_w_ukv, tag="e_w_ukv")
    b_r3 = chips_wait(st_r3, r_w_out_e[1], name="reduce_chips_wait_r3")
    g_w_in = jnp.transpose(reduce_sum(grads_r3[0], a_r3[0], b_r3[0], name="sum_e_w_in"))
    r_w_in = [t[None] for t in (g_w_in, *adam_rows(g_w_in, e_w_in[0], m_e_w_in[0], v_e_w_in[0], name="adam_e_w_in"))]

    d_sgu_b = jnp.transpose(d_b_full[:, ::CH])
    d_sgu_w_tril = jnp.tril(d_sgu_w)
    rep = [("e_norm_mix", e_norm_mix, m_e_norm_mix, v_e_norm_mix, d_enorm),
           ("e_q_norm", e_q_norm, m_e_q_norm, v_e_q_norm, d_qnorm),
           ("e_kv_norm", e_kv_norm, m_e_kv_norm, v_e_kv_norm, d_kvnorm),
           ("e_v_norm", e_v_norm, m_e_v_norm, v_e_v_norm, d_vgain),
           ("e_sgu_w", e_sgu_w, m_e_sgu_w, v_e_sgu_w, d_sgu_w_tril),
           ("e_sgu_b", e_sgu_b, m_e_sgu_b, v_e_sgu_b, d_sgu_b),
           ("e_mla_out_norm", e_mla_out_norm, m_e_mla_out_norm, v_e_mla_out_norm, d_mla_out),
           ("e_sgu_out_norm", e_sgu_out_norm, m_e_sgu_out_norm, v_e_sgu_out_norm, d_sgu_out),
           ("mlp_norm", mlp_norm, m_mlp_norm, v_mlp_norm, jnp.concatenate([d_mlp0, d_mlp1], axis=0)),
           ("final_norm", final_norm, m_final_norm, v_final_norm, d_final)]
    sizes = [int(np.prod(r[1].shape)) for r in rep]
    n_rep = sum(sizes)
    n_all = n_rep + 4 * D + 1
    width = -(-n_all // (8 * LANES)) * LANES
    pad = 8 * width - n_all
    flat = jnp.concatenate([r[4].reshape(-1) for r in rep]
                           + [d_onorm_full.reshape(-1), dconv_full.reshape(-1), loss_part[0, :1],
                              jnp.zeros((pad,), F32)])
    summed = sum_rows8(all_gather_vmem(flat.reshape(8, width), name="gather_small_grads", after=b_r3[0]), 8,
                       name="sum_small_grads").reshape(-1)

    loss = summed[n_rep + 4 * D]

    def pack_rep(i):
        return jnp.concatenate([r[i].reshape(-1) for r in rep]).reshape(n_rep // LANES, LANES)

    g_rep = summed[:n_rep].reshape(n_rep // LANES, LANES)
    d_rep, nm_rep, nv_rep = adam_flat(g_rep, pack_rep(1), pack_rep(2), pack_rep(3), name="adam_replicated")

    def unpack_rep(flat2d):
        out, off = {}, 0
        f = flat2d.reshape(-1)
        for r, n in zip(rep, sizes):
            out[r[0]] = f[off:off + n].reshape(r[1].shape)
            off += n
        return out

    small = {"grad": unpack_rep(g_rep), "delta": unpack_rep(d_rep), "new_m": unpack_rep(nm_rep),
             "new_v": unpack_rep(nv_rep)}
    g_onorm = lax.dynamic_slice(summed[n_rep:n_rep + D].reshape(1, D), (0, me * d_shard), (1, d_shard))
    g_conv = lax.dynamic_slice(summed[n_rep + D:n_rep + 4 * D].reshape(3, D), (0, me * d_shard), (3, d_shard))

    def pack_sharded(norm_part, conv_part):
        return jnp.concatenate([norm_part, conv_part, jnp.zeros((4, d_shard), F32)], axis=0)

    g_sh = pack_sharded(g_onorm, g_conv)
    d_sh, nm_sh, nv_sh = adam_flat(g_sh, pack_sharded(o_norm_mix, o_conv_w[0]), pack_sharded(m_o_norm_mix, m_o_conv_w[0]),
                                   pack_sharded(v_o_norm_mix, v_o_conv_w[0]), name="adam_sharded_small")
    for kind, arr in (("grad", g_sh), ("delta", d_sh), ("new_m", nm_sh), ("new_v", nv_sh)):
        small[kind]["o_norm_mix"] = arr[0:1]
        small[kind]["o_conv_w"] = arr[1:4][None]

    big = {"e_w_in": r_w_in, "e_w_uq": r_w_uq, "e_w_ukv": r_w_ukv, "e_w_out": r_w_out_e, "o_w_in": r_w_in_o,
           "o_w_out": r_w_out_o, "mlp_w1": r_w1, "mlp_w2": r_w2}
    order = ["e_norm_mix", "e_w_in", "e_q_norm", "e_w_uq", "e_kv_norm", "e_w_ukv", "e_v_norm", "e_sgu_w", "e_sgu_b",
             "e_mla_out_norm", "e_sgu_out_norm", "e_w_out", "o_norm_mix", "o_w_in", "o_conv_w", "o_w_out", "mlp_norm",
             "mlp_w1", "mlp_w2", "final_norm"]
    result = [loss, grad_x[None]]
    for ki, kind in enumerate(("grad", "delta", "new_m", "new_v")):
        for nm in order:
            result.append(big[nm][ki] if nm in big else small[kind][nm])
    return tuple(result)
```

```python
import numpy as np
import jax
import jax.numpy as jnp
from jax import lax
from jax.experimental import pallas as pl
from jax.experimental.pallas import tpu as pltpu

BF = jnp.bfloat16
F32 = jnp.float32
MESH = pl.DeviceIdType.MESH
N_DEV = 8

EPS = 1e-6
HEADS = 8
Q_LORA = 512
KV_LORA = 512
QK_NOPE = 128
QK_ROPE = 64
HALF_ROPE = QK_ROPE // 2
V_HEAD = 128
HEAD_PAD = 256
ROPE_BASE = 10000.0
GROUPS = 8
CH = 128
CHUNK = 128
SGU_OUT = GROUPS * CH
MLA_OUT = HEADS * V_HEAD
ATTN_SCALE = float((QK_NOPE + QK_ROPE) ** -0.5)

ADAM_LR = 0.001
ADAM_B1 = 0.9
ADAM_B2 = 0.999
ADAM_EPS = 1e-08
ADAM_WD = 0.01
ADAM_STEP = 10
ADAM_C1 = 1.0 - ADAM_B1 ** ADAM_STEP
ADAM_C2 = 1.0 - ADAM_B2 ** ADAM_STEP

V7X_VMEM_BYTES = 64 * 2 ** 20
VMEM_LIMIT_CAP = V7X_VMEM_BYTES - 6 * 2 ** 20
LANES = 128
ROW_TILE = 256
ATTN_TILE = 512
STREAM_BLOCK_ELEMS = 512 * 1024
MM_TILE = 1024
MM_K_TILE = 2048
MM_K_BLOCK_MAX = 3072


def _padded_bytes(block, dtype):
    dims = [d for d in block if d is not None]
    if len(dims) >= 1:
        dims[-1] = -(-dims[-1] // LANES) * LANES
    if len(dims) >= 2:
        dims[-2] = -(-dims[-2] // 16) * 16
    return int(np.prod(dims)) * jnp.dtype(dtype).itemsize


def _pcall(body, *, name, grid, ins, outs, scratch=(), semantics=None, aliases=None, prefetch=None, deps=()):
    any_spec = pl.BlockSpec(memory_space=pl.ANY)
    if deps:
        n_lead = len(ins) + (1 if prefetch is not None else 0)
        n_deps = len(deps)
        inner = body

        def body(*refs):
            inner(*refs[:n_lead], *refs[n_lead + n_deps:])

        ins = list(ins) + [(d, None, None) for d in deps]
    in_specs = [any_spec if b is None else pl.BlockSpec(b, m) for _, b, m in ins]
    out_specs = [any_spec if b is None else pl.BlockSpec(b, m) for _, _, b, m in outs]
    out_shape = [pltpu.HBM(s, d) for s, d, _, _ in outs]
    est = 0
    for a, b, _ in ins:
        if b is not None:
            est += 2 * _padded_bytes(b, a.dtype)
    for _, d, b, _ in outs:
        if b is not None:
            est += 2 * _padded_bytes(b, d)
    for s in scratch:
        if hasattr(s, "shape") and hasattr(s, "dtype"):
            est += _padded_bytes(s.shape, s.dtype)
    limit = int(min(VMEM_LIMIT_CAP, est + 16 * 2 ** 20))
    params = pltpu.CompilerParams(
        dimension_semantics=semantics or ("arbitrary",) * len(grid), vmem_limit_bytes=limit)
    args = [pltpu.with_memory_space_constraint(a, pltpu.HBM) for a, _, _ in ins]
    if prefetch is not None:
        grid_spec = pltpu.PrefetchScalarGridSpec(
            num_scalar_prefetch=1, grid=grid, in_specs=in_specs, out_specs=out_specs, scratch_shapes=list(scratch))
        call = pl.pallas_call(body, out_shape=out_shape, grid_spec=grid_spec, name=name, compiler_params=params,
                              input_output_aliases=aliases or {})
        return call(prefetch, *args)
    call = pl.pallas_call(body, out_shape=out_shape, grid=grid, in_specs=in_specs, out_specs=out_specs,
                          scratch_shapes=list(scratch), name=name, compiler_params=params,
                          input_output_aliases=aliases or {})
    return call(*args)


def _tile(dim, pref, quantum=LANES):
    if dim <= pref:
        return dim
    t = (pref // quantum) * quantum
    while t >= quantum:
        if dim % t == 0:
            return t
        t -= quantum
    return dim


def _vshape(arr_shape):
    if len(arr_shape) == 2:
        return tuple(arr_shape)
    s, r, c = arr_shape
    return (r, s * c)


def _vblock(arr_shape, br, bc, rc):
    if len(arr_shape) == 2:
        return (br, bc), (lambda *g: rc(*g))
    _, _, c = arr_shape
    assert c % bc == 0, (arr_shape, bc)
    per = c // bc

    def imap(*g):
        ri, ci = rc(*g)
        return (ci // per, ri, ci % per)

    return (None, br, bc), imap


def _shard_width(*shapes):
    w = None
    for s in shapes:
        if len(s) == 3:
            w = s[2] if w is None else int(np.gcd(w, s[2]))
    return w


def mm(a, b, *, name, ta=False, tb=False, out=None, outs=None, epi=None, epi_ins=(), bm=None, bn=None, bk=None,
       deps=(), jobs=(), job_index=None):
    av, bv = _vshape(a.shape), _vshape(b.shape)
    M, K = (av[1], av[0]) if ta else av
    K2, N = (bv[1], bv[0]) if tb else bv
    assert K == K2, (a.shape, b.shape, ta, tb)
    if outs is None:
        outs = [(out[0], out[1], None)]
    a_sw = _shard_width(a.shape)
    b_sw = _shard_width(b.shape)
    o_sw = _shard_width(*[o[0] for o in outs])
    m_lim = a_sw if (ta and a_sw) else None
    k_lim = [w for w in ((a_sw if not ta else None), (b_sw if tb else None)) if w]
    n_lim = [w for w in ((b_sw if not tb else None), o_sw) if w]
    if bm is None:
        bm = _tile(M, min([MM_TILE] + ([m_lim] if m_lim else [])))
    if bn is None:
        bn = _tile(N, min([MM_TILE] + n_lim))
    k_shards = 0
    if tb and len(b.shape) == 3 and bk is None and not (a_sw and not ta):
        k_shards = 1
        while 2 * k_shards <= b.shape[0] and 2 * k_shards * b_sw <= MM_K_BLOCK_MAX:
            k_shards *= 2
        bk = k_shards * b_sw
    if bk is None:
        bk = K if (K <= 4096 and not k_lim) else _tile(K, min([MM_K_TILE] + k_lim))
    assert M % bm == 0 and N % bn == 0 and K % bk == 0, (name, M, N, K, bm, bn, bk)
    nk = K // bk
    grid = (M // bm, N // bn, nk)
    if ta:
        a_blk, a_map = _vblock(a.shape, bk, bm, lambda i, j, k: (k, i))
    else:
        a_blk, a_map = _vblock(a.shape, bm, bk, lambda i, j, k: (i, k))
    if k_shards:
        b_blk, b_map = (k_shards, bn, b_sw), (lambda i, j, k: (k, j, 0))
    elif tb:
        b_blk, b_map = _vblock(b.shape, bn, bk, lambda i, j, k: (j, k))
    else:
        b_blk, b_map = _vblock(b.shape, bk, bn, lambda i, j, k: (k, j))
    dn = (((0 if ta else 1,), (1 if tb else 0,)), ((), ()))
    ins = [(a, a_blk, a_map), (b, b_blk, b_map)] + list(epi_ins)
    out_list = []
    for shape, dtype, cols in outs:
        cols = cols or bn
        blk, imap = _vblock(shape, bm, cols, lambda i, j, k: (i, j))
        out_list.append((shape, dtype, blk, imap))
    n_e, n_o = len(epi_ins), len(out_list)

    n_steps = grid[0] * grid[1] * nk
    built = [job(n_steps) for job in jobs]
    aliases = {}
    job_slices = []
    if built:
        def lin(i, j, k):
            return (i * grid[1] + j) * nk + k

        ins = [(arr, blk, None if blk is None else (lambda i, j, k, s, f=f: f(i, j, k))) for arr, blk, f in ins]
        out_list = [(sh, dt, blk, (lambda i, j, k, s, f=f: f(i, j, k))) for sh, dt, blk, f in out_list]
        n_main_in, n_main_out = len(ins), len(out_list)
        for jb in built:
            i0, o0 = len(ins), len(out_list)
            ins += [(arr, blk, None if blk is None else (lambda i, j, k, s, f=f: f(lin(i, j, k), s)))
                    for arr, blk, f in jb["ins"]]
            out_list += [(sh, dt, blk, (lambda i, j, k, s, f=f: f(lin(i, j, k), s))) for sh, dt, blk, f in jb["outs"]]
            aliases.update({1 + i0 + ai: o0 + ao for ai, ao in jb["aliases"].items()})
            job_slices.append((i0, len(jb["ins"]), o0, len(jb["outs"])))
    n_in_total = len(ins)

    def body(*refs):
        if built:
            refs = refs[1:]
        a_ref, b_ref = refs[0], refs[1]
        e_refs = refs[2:2 + n_e]
        o_refs = refs[n_in_total:n_in_total + n_o]
        for jb, (i0, ni, o0, no) in zip(built, job_slices):
            jb["fn"](refs[i0:i0 + ni], refs[n_in_total + o0:n_in_total + o0 + no])

        def finish(acc):
            res = epi(acc, *e_refs) if epi is not None else (acc,)
            for o_ref, r in zip(o_refs, res):
                o_ref[...] = r.astype(o_ref.dtype)

        x = a_ref[...].astype(BF)
        y = b_ref[...].astype(BF)
        if k_shards:
            p = None
            for s in range(k_shards):
                part = lax.dot_general(x[:, s * b_sw:(s + 1) * b_sw], y[s], dn, preferred_element_type=F32)
                p = part if p is None else p + part
        else:
            p = lax.dot_general(x, y, dn, preferred_element_type=F32)
        if nk == 1:
            finish(p)
        else:
            acc_ref = refs[-1]
            k = pl.program_id(2)

            @pl.when(k == 0)
            def _():
                acc_ref[...] = p

            @pl.when(k > 0)
            def _():
                acc_ref[...] += p

            @pl.when(k == nk - 1)
            def _():
                finish(acc_ref[...])

    scratch = [pltpu.VMEM((bm, bn), F32)] if nk > 1 else []
    res = _pcall(body, name=name, grid=grid, ins=ins, outs=out_list, scratch=scratch, deps=deps,
                 semantics=("parallel", "parallel", "arbitrary"), prefetch=job_index if built else None, aliases=aliases)
    main = res[0] if n_o == 1 else res[:n_o]
    if not built:
        return main
    return main, [res[o0:o0 + no] for _, _, o0, no in job_slices]


_GELU_K = float(np.sqrt(2.0 / np.pi))
_GELU_C = 0.044715


def _gelu(x):
    t = jnp.tanh(_GELU_K * (x + _GELU_C * (x * x * x)))
    return 0.5 * x * (1.0 + t)


def _gelu_grad(x):
    t = jnp.tanh(_GELU_K * (x + _GELU_C * (x * x * x)))
    return 0.5 * (1.0 + t) + 0.5 * x * (1.0 - t * t) * (_GELU_K * (1.0 + 3.0 * _GELU_C * (x * x)))


def _rstd(x):
    return lax.rsqrt(jnp.mean(x * x, axis=-1, keepdims=True) + EPS)


def _rms_bwd(x, gain, dy):
    r = _rstd(x)
    xh = x * r
    gdy = dy * gain
    dx = r * (gdy - xh * jnp.mean(gdy * xh, axis=-1, keepdims=True))
    return dx, dy * xh


def _rope_fwd(x, cos_t, sin_t):
    return x * cos_t + pltpu.roll(x, 2 * HALF_ROPE, 1) * sin_t


def _rope_bwd(dy, cos_t, sin_t):
    return dy * cos_t + pltpu.roll(dy * sin_t, 2 * HALF_ROPE, 1)


def _acc_rows(ref, val, first):
    s = jnp.sum(val, axis=0, keepdims=True)

    @pl.when(first)
    def _():
        ref[...] = s

    @pl.when(jnp.logical_not(first))
    def _():
        ref[...] += s


def rms_fwd(x, gain, *, name, col_block=0, width=None, deps=()):
    T = x.shape[0]
    width = width or x.shape[1]
    tm = _tile(T, ROW_TILE, 8)

    def body(x_ref, g_ref, o_ref):
        v = x_ref[...]
        o_ref[...] = (v * _rstd(v) * g_ref[...]).astype(BF)

    return _pcall(body, name=name, grid=(T // tm,),
                  ins=[(x, (tm, width), lambda i: (i, col_block)), (gain, (1, width), lambda i: (0, 0))],
                  outs=[((T, width), BF, (tm, width), lambda i: (i, 0))], semantics=("parallel",), deps=deps)[0]


def rms_bwd(x, gain, dy, *, name, col_block=0, dres=None, want_f32=True, want_bf=True, deps=()):
    T, width = dy.shape
    tm = _tile(T, ROW_TILE, 8)
    has_res = dres is not None

    def body(*refs):
        x_ref, g_ref, dy_ref = refs[:3]
        pos = 3
        res_ref = None
        if has_res:
            res_ref = refs[pos]
            pos += 1
        outs = refs[pos:]
        dx, dg_rows = _rms_bwd(x_ref[...], g_ref[...], dy_ref[...])
        if has_res:
            dx = dx + res_ref[...]
        o = 0
        if want_f32:
            outs[o][...] = dx
            o += 1
        if want_bf:
            outs[o][...] = dx.astype(BF)
            o += 1
        _acc_rows(outs[o], dg_rows, pl.program_id(0) == 0)

    ins = [(x, (tm, width), lambda i: (i, col_block)), (gain, (1, width), lambda i: (0, 0)),
           (dy, (tm, width), lambda i: (i, 0))]
    if has_res:
        ins.append((dres, (tm, width), lambda i: (i, 0)))
    outs = []
    if want_f32:
        outs.append(((T, width), F32, (tm, width), lambda i: (i, 0)))
    if want_bf:
        outs.append(((T, width), BF, (tm, width), lambda i: (i, 0)))
    outs.append(((1, width), F32, (1, width), lambda i: (0, 0)))
    return _pcall(body, name=name, grid=(T // tm,), ins=ins, outs=outs, deps=deps)


def mla_prep(proj, q_norm, kv_norm, cos_t, sin_t, *, name):
    T = proj.shape[0]
    tm = _tile(T, ROW_TILE, 8)
    kr_block = (proj.shape[1] - LANES) // LANES

    def body(cq_ref, ckv_ref, kr_ref, qg_ref, kg_ref, cos_ref, sin_ref, qn_ref, kvn_ref, krope_ref):
        cq = cq_ref[...]
        qn_ref[...] = (cq * _rstd(cq) * qg_ref[...]).astype(BF)
        ckv = ckv_ref[...]
        kvn_ref[...] = (ckv * _rstd(ckv) * kg_ref[...]).astype(BF)
        krope_ref[...] = _rope_fwd(kr_ref[...], cos_ref[...], sin_ref[...]).astype(BF)

    return _pcall(
        body, name=name, grid=(T // tm,),
        ins=[(proj, (tm, Q_LORA), lambda i: (i, 0)), (proj, (tm, KV_LORA), lambda i: (i, 1)),
             (proj, (tm, LANES), lambda i: (i, kr_block)),
             (q_norm, (1, Q_LORA), lambda i: (0, 0)), (kv_norm, (1, KV_LORA), lambda i: (0, 0)),
             (cos_t, (tm, LANES), lambda i: (i, 0)), (sin_t, (tm, LANES), lambda i: (i, 0))],
        outs=[((T, Q_LORA), BF, (tm, Q_LORA), lambda i: (i, 0)), ((T, KV_LORA), BF, (tm, KV_LORA), lambda i: (i, 0)),
              ((T, LANES), BF, (tm, LANES), lambda i: (i, 0))],
        semantics=("parallel",))


def _attn_scores(q, k_blk, diagonal):
    s = lax.dot_general(q, k_blk, (((1,), (1,)), ((), ())), preferred_element_type=F32) * ATTN_SCALE
    if diagonal:
        row = lax.broadcasted_iota(jnp.int32, s.shape, 0)
        col = lax.broadcasted_iota(jnp.int32, s.shape, 1)
        s = jnp.where(col <= row, s, -jnp.inf)
    return s


def attn_fwd(q, k, v, *, name):
    T = q.shape[0]
    tq = _tile(T, ATTN_TILE, 8)

    def body(q_ref, k_ref, v_ref, o_ref, lse_ref):
        i = pl.program_id(1)
        qv = q_ref[...]

        def block(kb, carry, diagonal):
            m, l, acc = carry
            start = pl.multiple_of(kb * tq, tq)
            s = _attn_scores(qv, k_ref[pl.ds(start, tq), :], diagonal)
            m_new = jnp.maximum(m, jnp.max(s, axis=-1, keepdims=True))
            alpha = jnp.exp(m - m_new)
            p = jnp.exp(s - m_new)
            l = alpha * l + jnp.sum(p, axis=-1, keepdims=True)
            acc = alpha * acc + jnp.dot(p.astype(BF), v_ref[pl.ds(start, tq), :], preferred_element_type=F32)
            return m_new, l, acc

        init = (jnp.full((tq, 1), -jnp.inf, F32), jnp.zeros((tq, 1), F32), jnp.zeros((tq, V_HEAD), F32))
        carry = lax.fori_loop(0, i, lambda kb, c: block(kb, c, False), init)
        m, l, acc = block(i, carry, True)
        o_ref[...] = acc / l
        lse_ref[...] = jnp.broadcast_to(m + jnp.log(l), (tq, V_HEAD))

    return _pcall(
        body, name=name, grid=(HEADS, T // tq),
        ins=[(q, (tq, HEAD_PAD), lambda h, i: (i, h)), (k, (T, HEAD_PAD), lambda h, i: (0, h)),
             (v, (T, V_HEAD), lambda h, i: (0, h))],
        outs=[((T, MLA_OUT), F32, (tq, V_HEAD), lambda h, i: (i, h)),
              ((T, MLA_OUT), F32, (tq, V_HEAD), lambda h, i: (i, h))], semantics=("parallel", "parallel"))


def attn_bwd(q, k, v, o, lse, do, *, name):
    T = q.shape[0]
    tq = _tile(T, ATTN_TILE, 8)

    def body(q_ref, k_ref, v_ref, o_ref, lse_ref, do_ref, dq_ref, dk_ref, dv_ref):
        i = pl.program_id(1)

        @pl.when(i == 0)
        def _():
            dk_ref[...] = jnp.zeros_like(dk_ref)
            dv_ref[...] = jnp.zeros_like(dv_ref)

        qv = q_ref[...]
        do_t = do_ref[...]
        lse_v = lse_ref[:, 0:1]
        delta = jnp.sum(do_t.astype(F32) * o_ref[...], axis=-1, keepdims=True)

        def block(kb, dq, diagonal):
            start = pl.multiple_of(kb * tq, tq)
            k_blk = k_ref[pl.ds(start, tq), :]
            v_blk = v_ref[pl.ds(start, tq), :]
            p = jnp.exp(_attn_scores(qv, k_blk, diagonal) - lse_v)
            dp = lax.dot_general(do_t, v_blk, (((1,), (1,)), ((), ())), preferred_element_type=F32)
            ds = (p * (dp - delta) * ATTN_SCALE).astype(BF)
            dk_ref[pl.ds(start, tq), :] += lax.dot_general(ds, qv, (((0,), (0,)), ((), ())), preferred_element_type=F32)
            dv_ref[pl.ds(start, tq), :] += lax.dot_general(p.astype(BF), do_t, (((0,), (0,)), ((), ())),
                                                          preferred_element_type=F32)
            return dq + jnp.dot(ds, k_blk, preferred_element_type=F32)

        dq = lax.fori_loop(0, i, lambda kb, c: block(kb, c, False), jnp.zeros((tq, HEAD_PAD), F32))
        dq_ref[...] = block(i, dq, True)

    return _pcall(
        body, name=name, grid=(HEADS, T // tq),
        ins=[(q, (tq, HEAD_PAD), lambda h, i: (i, h)), (k, (T, HEAD_PAD), lambda h, i: (0, h)),
             (v, (T, V_HEAD), lambda h, i: (0, h)), (o, (tq, V_HEAD), lambda h, i: (i, h)),
             (lse, (tq, V_HEAD), lambda h, i: (i, h)), (do, (tq, V_HEAD), lambda h, i: (i, h))],
        outs=[((T, HEADS * HEAD_PAD), F32, (tq, HEAD_PAD), lambda h, i: (i, h)),
              ((T, HEADS * HEAD_PAD), F32, (T, HEAD_PAD), lambda h, i: (0, h)),
              ((T, MLA_OUT), F32, (T, V_HEAD), lambda h, i: (0, h))],
        semantics=("parallel", "arbitrary"))


def mla_bwd_prep(dq, dk, dv, cos_t, sin_t, *, name):
    T = dq.shape[0]
    tm = _tile(T, ROW_TILE, 8)

    def body(dq_ref, dk_ref, dv_ref, cos_ref, sin_ref, dql_ref, dkvl_ref, dkr_ref):
        cos_v, sin_v = cos_ref[...], sin_ref[...]
        kr = jnp.zeros((tm, LANES), F32)
        for h in range(HEADS):
            lo = h * HEAD_PAD
            dql_ref[:, lo:lo + QK_NOPE] = dq_ref[:, lo:lo + QK_NOPE].astype(BF)
            dql_ref[:, lo + QK_NOPE:lo + HEAD_PAD] = _rope_bwd(
                dq_ref[:, lo + QK_NOPE:lo + HEAD_PAD], cos_v, sin_v).astype(BF)
            dkvl_ref[:, lo:lo + QK_NOPE] = dk_ref[:, lo:lo + QK_NOPE].astype(BF)
            dkvl_ref[:, lo + QK_NOPE:lo + HEAD_PAD] = dv_ref[:, h * V_HEAD:(h + 1) * V_HEAD].astype(BF)
            kr = kr + dk_ref[:, lo + QK_NOPE:lo + HEAD_PAD]
        dkr_ref[...] = _rope_bwd(kr, cos_v, sin_v).astype(BF)

    W = HEADS * HEAD_PAD
    return _pcall(
        body, name=name, grid=(T // tm,),
        ins=[(dq, (tm, W), lambda i: (i, 0)), (dk, (tm, W), lambda i: (i, 0)), (dv, (tm, MLA_OUT), lambda i: (i, 0)),
             (cos_t, (tm, LANES), lambda i: (i, 0)), (sin_t, (tm, LANES), lambda i: (i, 0))],
        outs=[((T, W), BF, (tm, W), lambda i: (i, 0)), ((T, W), BF, (tm, W), lambda i: (i, 0)),
              ((T, LANES), BF, (tm, LANES), lambda i: (i, 0))],
        semantics=("parallel",))


def _group_norm_stats(vg):
    mu = jnp.mean(vg, axis=-1, keepdims=True)
    d = vg - mu
    r = lax.rsqrt(jnp.mean(d * d, axis=-1, keepdims=True) + EPS)
    return d * r, r


def mix_fwd(a, proj, g_mla, g_sgu, v_gain, w_tril, b_full, *, name):
    T = a.shape[0]
    tm = _tile(T, ROW_TILE, CHUNK)
    n_chunk = tm // CHUNK

    def body(a_ref, u_ref, v_ref, gm_ref, gs_ref, vg_ref, w_ref, b_ref, o_ref, s_scr):
        av = a_ref[...]
        o_ref[:, :MLA_OUT] = (av * _rstd(av) * gm_ref[...]).astype(BF)
        for g in range(GROUPS):
            sl = slice(g * CH, (g + 1) * CH)
            vhat, _ = _group_norm_stats(_gelu(v_ref[:, sl]))
            vn = (vhat * vg_ref[:, sl]).astype(BF)
            u = _gelu(u_ref[:, sl])
            for ci in range(n_chunk):
                rs = slice(ci * CHUNK, (ci + 1) * CHUNK)
                y = jnp.dot(w_ref[g], vn[rs], preferred_element_type=F32) + b_ref[:, sl]
                s_scr[rs, sl] = u[rs] * y
        s = s_scr[...]
        o_ref[:, MLA_OUT:] = (s * _rstd(s) * gs_ref[...]).astype(BF)

    return _pcall(
        body, name=name, grid=(T // tm,),
        ins=[(a, (tm, MLA_OUT), lambda i: (i, 0)), (proj, (tm, SGU_OUT), lambda i: (i, 1)),
             (proj, (tm, SGU_OUT), lambda i: (i, 2)), (g_mla, (1, MLA_OUT), lambda i: (0, 0)),
             (g_sgu, (1, SGU_OUT), lambda i: (0, 0)), (v_gain, (1, SGU_OUT), lambda i: (0, 0)),
             (w_tril, (GROUPS, CHUNK, CHUNK), lambda i: (0, 0, 0)), (b_full, (CHUNK, SGU_OUT), lambda i: (0, 0))],
        outs=[((T, MLA_OUT + SGU_OUT), BF, (tm, MLA_OUT + SGU_OUT), lambda i: (i, 0))],
        scratch=[pltpu.VMEM((tm, SGU_OUT), F32)], semantics=("parallel",))[0]


def mix_bwd(dmixed, a, proj, g_mla, g_sgu, v_gain, w_tril, w_tril_t, b_full, *, name, deps=()):
    T = a.shape[0]
    tm = _tile(T, ROW_TILE, CHUNK)
    n_chunk = tm // CHUNK

    def body(dm_a_ref, dm_s_ref, a_ref, u_ref, v_ref, gm_ref, gs_ref, vg_ref, w_ref, wt_ref, b_ref,
             da_ref, duv_ref, dgm_ref, dgs_ref, dvg_ref, dw_ref, db_ref, s_scr, y_scr):
        first = pl.program_id(0) == 0
        da, dgm_rows = _rms_bwd(a_ref[...], gm_ref[...], dm_a_ref[...])
        da_ref[...] = da.astype(BF)
        _acc_rows(dgm_ref, dgm_rows, first)

        for g in range(GROUPS):
            sl = slice(g * CH, (g + 1) * CH)
            vhat, _ = _group_norm_stats(_gelu(v_ref[:, sl]))
            vn = (vhat * vg_ref[:, sl]).astype(BF)
            u = _gelu(u_ref[:, sl])
            for ci in range(n_chunk):
                rs = slice(ci * CHUNK, (ci + 1) * CHUNK)
                y = jnp.dot(w_ref[g], vn[rs], preferred_element_type=F32) + b_ref[:, sl]
                y_scr[rs, sl] = y
                s_scr[rs, sl] = u[rs] * y
        ds, dgs_rows = _rms_bwd(s_scr[...], gs_ref[...], dm_s_ref[...])
        _acc_rows(dgs_ref, dgs_rows, first)
        s_scr[...] = ds

        @pl.when(first)
        def _():
            dw_ref[...] = jnp.zeros_like(dw_ref)
            db_ref[...] = jnp.zeros_like(db_ref)

        for g in range(GROUPS):
            sl = slice(g * CH, (g + 1) * CH)
            upre = u_ref[:, sl]
            vpre = v_ref[:, sl]
            u = _gelu(upre)
            vhat, r = _group_norm_stats(_gelu(vpre))
            gain = vg_ref[:, sl]
            vn = (vhat * gain).astype(BF)
            dsg = s_scr[:, sl]
            duv_ref[:, sl] = (dsg * y_scr[:, sl] * _gelu_grad(upre)).astype(BF)
            dy = dsg * u
            dyb = dy.astype(BF)
            dvn_parts = []
            for ci in range(n_chunk):
                rs = slice(ci * CHUNK, (ci + 1) * CHUNK)
                dvn_parts.append(jnp.dot(wt_ref[g], dyb[rs], preferred_element_type=F32))
                dw_ref[g] += lax.dot_general(dyb[rs], vn[rs], (((1,), (1,)), ((), ())), preferred_element_type=F32)
                db_ref[:, sl] += jnp.broadcast_to(jnp.sum(dy[rs], axis=-1, keepdims=True), (CHUNK, CH))
            dvn = dvn_parts[0] if n_chunk == 1 else jnp.concatenate(dvn_parts, axis=0)
            _acc_rows(dvg_ref.at[:, sl], dvn * vhat, first)
            dvh = dvn * gain
            dvg = r * (dvh - jnp.mean(dvh, axis=-1, keepdims=True)
                       - vhat * jnp.mean(dvh * vhat, axis=-1, keepdims=True))
            duv_ref[:, SGU_OUT + g * CH:SGU_OUT + (g + 1) * CH] = (dvg * _gelu_grad(vpre)).astype(BF)

    return _pcall(
        body, name=name, grid=(T // tm,),
        ins=[(dmixed, (tm, MLA_OUT), lambda i: (i, 0)), (dmixed, (tm, SGU_OUT), lambda i: (i, 1)),
             (a, (tm, MLA_OUT), lambda i: (i, 0)), (proj, (tm, SGU_OUT), lambda i: (i, 1)),
             (proj, (tm, SGU_OUT), lambda i: (i, 2)), (g_mla, (1, MLA_OUT), lambda i: (0, 0)),
             (g_sgu, (1, SGU_OUT), lambda i: (0, 0)), (v_gain, (1, SGU_OUT), lambda i: (0, 0)),
             (w_tril, (GROUPS, CHUNK, CHUNK), lambda i: (0, 0, 0)), (w_tril_t, (GROUPS, CHUNK, CHUNK), lambda i: (0, 0, 0)),
             (b_full, (CHUNK, SGU_OUT), lambda i: (0, 0))],
        outs=[((T, MLA_OUT), BF, (tm, MLA_OUT), lambda i: (i, 0)),
              ((T, 2 * SGU_OUT), BF, (tm, 2 * SGU_OUT), lambda i: (i, 0)),
              ((1, MLA_OUT), F32, (1, MLA_OUT), lambda i: (0, 0)), ((1, SGU_OUT), F32, (1, SGU_OUT), lambda i: (0, 0)),
              ((1, SGU_OUT), F32, (1, SGU_OUT), lambda i: (0, 0)),
              ((GROUPS, CHUNK, CHUNK), F32, (GROUPS, CHUNK, CHUNK), lambda i: (0, 0, 0)),
              ((CHUNK, SGU_OUT), F32, (CHUNK, SGU_OUT), lambda i: (0, 0))],
        scratch=[pltpu.VMEM((tm, SGU_OUT), F32), pltpu.VMEM((tm, SGU_OUT), F32)], deps=deps)


def _shift_down(z, n, row):
    return jnp.where(row >= n, pltpu.roll(z, n, 0), 0.0)


def _shift_up(z, n, row, T):
    return jnp.where(row < T - n, pltpu.roll(z, T - n, 0), 0.0)


def conv_fwd(proj, conv_w, *, name):
    T, D3 = proj.shape
    D = D3 // 3
    tn = _tile(D, 256)
    nj = D // tn

    def body(b_ref, c_ref, x_ref, w_ref, o_ref):
        row = lax.broadcasted_iota(jnp.int32, (T, tn), 0)
        z = c_ref[...] * x_ref[...]
        zc = w_ref[2:3, :] * z + w_ref[1:2, :] * _shift_down(z, 1, row) + w_ref[0:1, :] * _shift_down(z, 2, row)
        o_ref[...] = (b_ref[...] * zc).astype(BF)

    return _pcall(
        body, name=name, grid=(nj,),
        ins=[(proj, (T, tn), lambda j: (0, j)), (proj, (T, tn), lambda j: (0, nj + j)),
             (proj, (T, tn), lambda j: (0, 2 * nj + j)), (conv_w, (3, tn), lambda j: (0, j))],
        outs=[((T, D), BF, (T, tn), lambda j: (0, j))], semantics=("parallel",))[0]


def conv_bwd(dg, proj, conv_w, *, name, deps=()):
    T, D3 = proj.shape
    D = D3 // 3
    tn = _tile(D, 256)
    nj = D // tn

    def body(dg_ref, b_ref, c_ref, x_ref, w_ref, dp_ref, dw_ref, dc_scr, dx_scr):
        part = pl.program_id(1)

        @pl.when(part == 0)
        def _():
            row = lax.broadcasted_iota(jnp.int32, (T, tn), 0)
            c, x = c_ref[...], x_ref[...]
            z = c * x
            z1 = _shift_down(z, 1, row)
            z2 = _shift_down(z, 2, row)
            dgv = dg_ref[...]
            zc = w_ref[2:3, :] * z + w_ref[1:2, :] * z1 + w_ref[0:1, :] * z2
            dp_ref[...] = (dgv * zc).astype(BF)
            dzc = dgv * b_ref[...]
            dw_ref[0:1, :] = jnp.sum(dzc * z2, axis=0, keepdims=True)
            dw_ref[1:2, :] = jnp.sum(dzc * z1, axis=0, keepdims=True)
            dw_ref[2:3, :] = jnp.sum(dzc * z, axis=0, keepdims=True)
            dz = (w_ref[2:3, :] * dzc + w_ref[1:2, :] * _shift_up(dzc, 1, row, T)
                  + w_ref[0:1, :] * _shift_up(dzc, 2, row, T))
            dc_scr[...] = (dz * x).astype(BF)
            dx_scr[...] = (dz * c).astype(BF)

        @pl.when(part == 1)
        def _():
            dp_ref[...] = dc_scr[...]

        @pl.when(part == 2)
        def _():
            dp_ref[...] = dx_scr[...]

    return _pcall(
        body, name=name, grid=(nj, 3),
        ins=[(dg, (T, tn), lambda j, p: (0, j)), (proj, (T, tn), lambda j, p: (0, j)),
             (proj, (T, tn), lambda j, p: (0, nj + j)), (proj, (T, tn), lambda j, p: (0, 2 * nj + j)),
             (conv_w, (3, tn), lambda j, p: (0, j))],
        outs=[((T, D3), BF, (T, tn), lambda j, p: (0, p * nj + j)), ((3, D), F32, (3, tn), lambda j, p: (0, j))],
        scratch=[pltpu.VMEM((T, tn), BF), pltpu.VMEM((T, tn), BF)], semantics=("parallel", "arbitrary"), deps=deps)


def loss_bwd(x_parts, gain, target, *, name):
    T, D = target.shape
    tm = _tile(T, ROW_TILE, 8)
    n_x = len(x_parts)

    def body(*refs):
        x_refs = refs[:n_x]
        g_ref, t_ref, dx_ref, dxb_ref, dg_ref, loss_ref = refs[n_x:]
        first = pl.program_id(0) == 0
        xv = jnp.concatenate([r[...] for r in x_refs], axis=-1) if n_x > 1 else x_refs[0][...]
        r = _rstd(xv)
        xh = xv * r
        gain_v = g_ref[...]
        err = xh * gain_v - t_ref[...]
        part = 0.5 * jnp.sum(jnp.mean(err * err, axis=-1, keepdims=True), axis=0, keepdims=True)
        _acc_rows(loss_ref, jnp.broadcast_to(part, (1, LANES)), first)
        dy = err * (1.0 / D)
        gdy = dy * gain_v
        dx = r * (gdy - xh * jnp.mean(gdy * xh, axis=-1, keepdims=True))
        dx_ref[...] = dx
        dxb_ref[...] = dx.astype(BF)
        _acc_rows(dg_ref, dy * xh, first)

    return _pcall(
        body, name=name, grid=(T // tm,),
        ins=[(p, (tm, D // n_x), lambda i: (i, 0)) for p in x_parts]
        + [(gain, (1, D), lambda i: (0, 0)), (target, (tm, D), lambda i: (i, 0))],
        outs=[((T, D), F32, (tm, D), lambda i: (i, 0)), ((T, D), BF, (tm, D), lambda i: (i, 0)),
              ((1, D), F32, (1, D), lambda i: (0, 0)), ((1, LANES), F32, (1, LANES), lambda i: (0, 0))])


def _adamw(g, w, m, v):
    m = ADAM_B1 * m + (1.0 - ADAM_B1) * g
    v = ADAM_B2 * v + (1.0 - ADAM_B2) * (g * g)
    m_hat = m / ADAM_C1
    v_hat = v / ADAM_C2
    delta = -ADAM_LR * (m_hat / (jnp.sqrt(v_hat) + ADAM_EPS) + ADAM_WD * w)
    return delta, m, v


def adam_flat(g, w, m, v, *, name):
    def body(g_ref, w_ref, m_ref, v_ref, d_ref, nm_ref, nv_ref):
        d, nm, nv = _adamw(g_ref[...], w_ref[...], m_ref[...], v_ref[...])
        d_ref[...] = d
        nm_ref[...] = nm
        nv_ref[...] = nv

    blk = g.shape
    zero = lambda: (0, 0)
    return _pcall(body, name=name, grid=(),
                  ins=[(t, blk, zero) for t in (g, w, m, v)],
                  outs=[(blk, F32, blk, zero)] * 3)


def _chip_slots():
    x, y, c = lax.axis_index("x"), lax.axis_index("y"), lax.axis_index("c")
    chips = [(1 - x, y), (x, 1 - y), (1 - x, 1 - y)]
    return x, y, c, chips


def device_index():
    x, y, c, chips = _chip_slots()
    return jnp.stack([4 * x + 2 * y + c, 2 * x + y] + [4 * cx + 2 * cy + c for cx, cy in chips]
                     + [2 * cx + cy for cx, cy in chips]).astype(jnp.int32)


def _job_rows(R, C, n_steps):
    if n_steps is None:
        n_steps = max(1, R * C // STREAM_BLOCK_ELEMS)
    n_blk = max([d for d in range(1, n_steps + 1) if R % d == 0 and (R // d) % 16 == 0] or [1])
    return R // n_blk, n_blk


def run_job(job, *, index, name, deps=()):
    jb = job(None)
    n_in = len(jb["ins"])

    def body(idx_ref, *refs):
        jb["fn"](refs[:n_in], refs[n_in:n_in + len(jb["outs"])])

    return _pcall(body, name=name, grid=(jb["n_blk"],), ins=jb["ins"], outs=jb["outs"], prefetch=index,
                  aliases={1 + a: o for a, o in jb["aliases"].items()}, semantics=("parallel",), deps=deps)


def adam_job(gs, a_buf, b_buf, w, m, v, layer, prev):
    L, R, C = w.shape

    def build(n_steps):
        tr, n_blk = _job_rows(R, C, n_steps)
        blk = (None, tr, C)
        row = lambda t: jnp.minimum(t, n_blk - 1)
        ins = [(gs, blk, lambda t, s: (s[0], row(t), 0)), (a_buf, blk, lambda t, s: (s[1], row(t), 0))]
        ins += [(b_buf, blk, lambda t, s, j=j: (j, row(t), 0)) for j in range(3)]
        ins += [(p, blk, lambda t, s: (layer, row(t), 0)) for p in (w, m, v)]
        ins += [(p, None, None) for p in (prev or [])]

        def fn(i, o):
            g = ((((i[0][...].astype(F32) + i[1][...].astype(F32)) + i[2][...].astype(F32))
                  + i[3][...].astype(F32)) + i[4][...].astype(F32))
            d, nm, nv = _adamw(g, i[5][...], i[6][...], i[7][...])
            o[0][...] = g
            o[1][...] = d
            o[2][...] = nm
            o[3][...] = nv

        return dict(ins=ins, outs=[((L, R, C), F32, blk, lambda t, s: (layer, row(t), 0))] * 4, fn=fn,
                    aliases={8 + o: o for o in range(4)} if prev else {}, n_blk=n_blk)

    return build


def pair_job(gs, a_buf):
    _, R, C = gs.shape

    def build(n_steps):
        tr, n_blk = _job_rows(R, C, n_steps)
        blk = (None, tr, C)
        row = lambda t: jnp.minimum(t, n_blk - 1)
        ins = [(gs, blk, lambda t, s, j=j: (s[2 + j], row(t), 0)) for j in range(3)]
        ins += [(a_buf, blk, lambda t, s, j=j: (s[5 + j], row(t), 0)) for j in range(3)]

        def fn(i, o):
            for j in range(3):
                o[0][j] = (i[j][...].astype(F32) + i[3 + j][...].astype(F32)).astype(BF)

        return dict(ins=ins, outs=[((3, R, C), BF, (3, tr, C), lambda t, s: (0, row(t), 0))], fn=fn, aliases={},
                    n_blk=n_blk)

    return build


def reduce_sum(gs, a_buf, b_buf, *, name):
    _, R, C = gs.shape
    tr = _tile(R, 256, 16)
    x, y, c, _ = _chip_slots()
    idx = jnp.stack([4 * x + 2 * y + c, 2 * x + y]).astype(jnp.int32)

    def body(idx_ref, g_ref, a_ref, b0_ref, b1_ref, b2_ref, o_ref):
        o_ref[...] = ((((g_ref[...].astype(F32) + a_ref[...].astype(F32)) + b0_ref[...].astype(F32))
                       + b1_ref[...].astype(F32)) + b2_ref[...].astype(F32))

    blk3 = (None, tr, C)
    return _pcall(body, name=name, grid=(R // tr,),
                  ins=[(gs, blk3, lambda i, s: (s[0], i, 0)), (a_buf, blk3, lambda i, s: (s[1], i, 0)),
                       (b_buf, blk3, lambda i, s: (0, i, 0)), (b_buf, blk3, lambda i, s: (1, i, 0)),
                       (b_buf, blk3, lambda i, s: (2, i, 0))],
                  outs=[((R, C), F32, (tr, C), lambda i, s: (i, 0))], prefetch=idx, semantics=("parallel",))[0]


def adam_rows(g, w, m, v, *, name):
    R, C = g.shape
    tr = _tile(R, 256, 8)

    def body(g_ref, w_ref, m_ref, v_ref, d_ref, nm_ref, nv_ref):
        d, nm, nv = _adamw(g_ref[...], w_ref[...], m_ref[...], v_ref[...])
        d_ref[...] = d
        nm_ref[...] = nm
        nv_ref[...] = nv

    spec = ((tr, C), lambda i: (i, 0))
    return _pcall(body, name=name, grid=(R // tr,), ins=[(t, *spec) for t in (g, w, m, v)],
                  outs=[((R, C), F32, *spec)] * 3, semantics=("parallel",))


def sum_rows8(gathered, rows, *, name):
    W = gathered.shape[1]

    def body(g_ref, o_ref):
        acc = g_ref[0:rows, :]
        for d in range(1, N_DEV):
            acc = acc + g_ref[d * rows:(d + 1) * rows, :]
        o_ref[...] = acc

    return _pcall(body, name=name, grid=(), ins=[(gathered, gathered.shape, lambda: (0, 0))],
                  outs=[((rows, W), F32, (rows, W), lambda: (0, 0))])[0]


HBM_SPEC = pl.BlockSpec(memory_space=pltpu.HBM)
SEM_SPEC = pl.BlockSpec(memory_space=pltpu.SEMAPHORE)
ANY_SPEC = pl.BlockSpec(memory_space=pl.ANY)
DATAFLOW = pltpu.SideEffectType.DATAFLOW_SIDE_EFFECTING


def _in_hbm(v):
    return pltpu.with_memory_space_constraint(v, pltpu.HBM)


def _slot(p):
    return 4 * p[0] + 2 * p[1] + p[2]


def _gather_peers():
    x, y, c, chips = _chip_slots()
    return (x, y, c), [(x, y, 1 - c)] + [(*chip, c) for chip in chips]


def gather_start(groups, after, *, name):
    flat = [s for g in groups for s in g]
    n, n_g = len(flat), len(groups)
    where = [(gi, ti) for gi, g in enumerate(groups) for ti in range(len(g))]

    def body(*refs):
        src, land = refs[:n], refs[n:2 * n]
        sems = refs[2 * n + 1:2 * n + 1 + 2 * n_g]
        me, peers = _gather_peers()
        for t in range(n):
            gi, ti = where[t]
            for k, to in enumerate(peers):
                pltpu.make_async_remote_copy(
                    src_ref=src[t], dst_ref=land[t].at[_slot(me)], send_sem=sems[2 * gi].at[4 * ti + k],
                    recv_sem=sems[2 * gi + 1].at[4 * ti + k], device_id=to, device_id_type=MESH).start()
        refs[-1][...] = jnp.zeros_like(refs[-1])

    out_shape = []
    for g in groups:
        out_shape += [pltpu.SemaphoreType.DMA((4 * len(g),)), pltpu.SemaphoreType.DMA((4 * len(g),))]
    out_shape += [pltpu.HBM(s.shape, s.dtype) for s in flat]
    out_shape += [pltpu.HBM((N_DEV,) + s.shape, s.dtype) for s in flat]
    out_shape += [jax.ShapeDtypeStruct((8, LANES), F32)]
    aliases = {t: 2 * n_g + t for t in range(n)}
    aliases.update({n + t: 2 * n_g + n + t for t in range(n)})
    res = pl.pallas_call(
        body, name=name, out_shape=out_shape, in_specs=[HBM_SPEC] * (2 * n) + [ANY_SPEC],
        out_specs=[SEM_SPEC] * (2 * n_g) + [HBM_SPEC] * (2 * n) + [pl.BlockSpec(memory_space=pltpu.VMEM)],
        input_output_aliases=aliases, compiler_params=pltpu.CompilerParams(has_side_effects=DATAFLOW),
    )(*[_in_hbm(s) for s in flat], *[_in_hbm(lax.empty((N_DEV,) + s.shape, s.dtype)) for s in flat], after)
    out, off = [], 0
    for gi, g in enumerate(groups):
        k = len(g)
        out.append((res[2 * gi], res[2 * gi + 1], res[2 * n_g + off:2 * n_g + off + k],
                    res[2 * n_g + n + off:2 * n_g + n + off + k]))
        off += k
    return out, res[-1]


def gather_wait(started, after, *, name):
    send_sems, recv_sems, srcs, lands = started
    n = len(srcs)
    after = list(after)

    def body(*refs):
        src, land = refs[:n], refs[n:2 * n]
        send, recv = refs[2 * n], refs[2 * n + 1]
        _, peers = _gather_peers()
        for t in range(n):
            for k, frm in enumerate(peers):
                cp = pltpu.make_async_remote_copy(
                    src_ref=src[t], dst_ref=land[t].at[_slot(frm)], send_sem=send.at[4 * t + k],
                    recv_sem=recv.at[4 * t + k],
                    device_id=frm, device_id_type=MESH)
                cp.wait_send()
                cp.wait_recv()

    res = pl.pallas_call(
        body, name=name,
        out_shape=[pltpu.HBM(s.shape, s.dtype) for s in srcs] + [pltpu.HBM(l.shape, l.dtype) for l in lands],
        in_specs=[HBM_SPEC] * (2 * n) + [SEM_SPEC, SEM_SPEC] + [ANY_SPEC] * len(after),
        out_specs=[HBM_SPEC] * (2 * n), input_output_aliases={t: t for t in range(2 * n)},
        compiler_params=pltpu.CompilerParams(has_side_effects=DATAFLOW),
    )(*srcs, *lands, send_sems, recv_sems, *after)
    return res[:n], res[n:]


def place_own(src, land, *, name):
    R, C = src.shape
    tr = _tile(R, 512, 16)
    x, y, c, _ = _chip_slots()
    idx = jnp.stack([4 * x + 2 * y + c]).astype(jnp.int32)

    def body(idx_ref, s_ref, land_ref, o_ref):
        o_ref[...] = s_ref[...]

    return _pcall(body, name=name, grid=(R // tr,),
                  ins=[(src, (tr, C), lambda i, s: (i, 0)), (land, None, None)],
                  outs=[(land.shape, land.dtype, (None, tr, C), lambda i, s: (s[0], i, 0))],
                  prefetch=idx, aliases={2: 0}, semantics=("parallel",))[0]


def gather_finish(srcs, lands, *, name):
    n = len(srcs)

    def body(*refs):
        land = refs[n:2 * n]
        send_sems, recv_sems = refs[2 * n:]
        x, y, c, chips = _chip_slots()
        me, sibling = (x, y, c), (x, y, 1 - c)

        def copy(t, j, block, to):
            return pltpu.make_async_remote_copy(
                src_ref=land[t].at[_slot(block)], dst_ref=land[t].at[_slot(block)], send_sem=send_sems.at[t, j],
                recv_sem=recv_sems.at[t, j], device_id=to, device_id_type=MESH)

        sends = [copy(t, j, (*chip, c), sibling) for t in range(n) for j, chip in enumerate(chips)]
        for cp in sends:
            cp.start()
        for t in range(n):
            for j, chip in enumerate(chips):
                copy(t, j, (*chip, 1 - c), me).wait_recv()
        for cp in sends:
            cp.wait_send()

    passed = pl.pallas_call(
        body, name=name, out_shape=[jax.ShapeDtypeStruct(l.shape, l.dtype) for l in lands],
        in_specs=[ANY_SPEC] * n, out_specs=[ANY_SPEC] * n,
        input_output_aliases={t: t for t in range(n)},
        scratch_shapes=[pltpu.SemaphoreType.DMA((n, 3)), pltpu.SemaphoreType.DMA((n, 3))],
    )(*lands)
    return [place_own(s, l, name=f"{name}_own{t}") for t, (s, l) in enumerate(zip(srcs, passed))]


def chips_start(pairs, *, name):
    n = len(pairs)

    def body(*refs):
        src, land = refs[:n], refs[n:2 * n]
        send, recv = refs[2 * n], refs[2 * n + 1]
        token = refs[-1]
        x, y, c, chips = _chip_slots()
        for t in range(n):
            for j, chip in enumerate(chips):
                pltpu.make_async_remote_copy(
                    src_ref=src[t].at[j], dst_ref=land[t].at[j], send_sem=send.at[3 * t + j],
                    recv_sem=recv.at[3 * t + j], device_id=(*chip, c), device_id_type=MESH).start()
        token[...] = jnp.zeros_like(token)

    res = pl.pallas_call(
        body, name=name,
        out_shape=[pltpu.SemaphoreType.DMA((3 * n,)), pltpu.SemaphoreType.DMA((3 * n,))]
        + [pltpu.HBM(p.shape, p.dtype) for p in pairs] * 2 + [jax.ShapeDtypeStruct((8, LANES), F32)],
        in_specs=[HBM_SPEC] * (2 * n),
        out_specs=[SEM_SPEC, SEM_SPEC] + [HBM_SPEC] * (2 * n) + [pl.BlockSpec(memory_space=pltpu.VMEM)],
        input_output_aliases={t: 2 + t for t in range(2 * n)},
        compiler_params=pltpu.CompilerParams(has_side_effects=DATAFLOW),
    )(*[_in_hbm(p) for p in pairs], *[_in_hbm(lax.empty(p.shape, p.dtype)) for p in pairs])
    return res[0], res[1], res[2:2 + n], res[2 + n:2 + 2 * n], res[-1]


def chips_wait(started, after, *, name):
    send_sems, recv_sems, srcs, lands, _ = started
    n = len(srcs)

    def body(*refs):
        src, land = refs[:n], refs[n:2 * n]
        send, recv = refs[2 * n], refs[2 * n + 1]
        x, y, c, chips = _chip_slots()
        for t in range(n):
            for j, chip in enumerate(chips):
                cp = pltpu.make_async_remote_copy(
                    src_ref=src[t].at[j], dst_ref=land[t].at[j], send_sem=send.at[3 * t + j],
                    recv_sem=recv.at[3 * t + j], device_id=(*chip, c), device_id_type=MESH)
                cp.wait_send()
                cp.wait_recv()

    res = pl.pallas_call(
        body, name=name, out_shape=[pltpu.HBM(s.shape, s.dtype) for s in srcs] * 2,
        in_specs=[HBM_SPEC] * (2 * n) + [SEM_SPEC, SEM_SPEC, ANY_SPEC], out_specs=[HBM_SPEC] * (2 * n),
        input_output_aliases={t: t for t in range(2 * n)},
        compiler_params=pltpu.CompilerParams(has_side_effects=DATAFLOW),
    )(*srcs, *lands, send_sems, recv_sems, after)
    return res[n:]


def _sibling_copies(src, land, send, recv, n):
    x, y, c, _ = _chip_slots()
    return [pltpu.make_async_remote_copy(
        src_ref=src[t].at[4 * (q // 2) + 2 * (q % 2) + (1 - c)], dst_ref=land[t].at[q], send_sem=send.at[4 * t + q],
        recv_sem=recv.at[4 * t + q], device_id=(x, y, 1 - c), device_id_type=MESH)
        for t in range(n) for q in range(4)]


def sibling_start(gs, *, name):
    n = len(gs)

    def body(*refs):
        for cp in _sibling_copies(refs[:n], refs[n:2 * n], refs[2 * n], refs[2 * n + 1], n):
            cp.start()
        refs[-1][...] = jnp.zeros_like(refs[-1])

    lands = [lax.empty((4,) + g.shape[1:], g.dtype) for g in gs]
    res = pl.pallas_call(
        body, name=name,
        out_shape=[pltpu.SemaphoreType.DMA((4 * n,)), pltpu.SemaphoreType.DMA((4 * n,))]
        + [pltpu.HBM(g.shape, g.dtype) for g in gs] + [pltpu.HBM(l.shape, l.dtype) for l in lands]
        + [jax.ShapeDtypeStruct((8, LANES), F32)],
        in_specs=[HBM_SPEC] * (2 * n),
        out_specs=[SEM_SPEC, SEM_SPEC] + [HBM_SPEC] * (2 * n) + [pl.BlockSpec(memory_space=pltpu.VMEM)],
        input_output_aliases={t: 2 + t for t in range(2 * n)},
        compiler_params=pltpu.CompilerParams(has_side_effects=DATAFLOW),
    )(*[_in_hbm(g) for g in gs], *[_in_hbm(l) for l in lands])
    return res[0], res[1], res[2:2 + n], res[2 + n:2 + 2 * n], res[-1]


def sibling_wait(started, after, *, name):
    send_sems, recv_sems, srcs, lands, _ = started
    n = len(srcs)

    def body(*refs):
        for cp in _sibling_copies(refs[:n], refs[n:2 * n], refs[2 * n], refs[2 * n + 1], n):
            cp.wait_send()
            cp.wait_recv()

    res = pl.pallas_call(
        body, name=name,
        out_shape=[pltpu.HBM(s.shape, s.dtype) for s in srcs] + [pltpu.HBM(l.shape, l.dtype) for l in lands],
        in_specs=[HBM_SPEC] * (2 * n) + [SEM_SPEC, SEM_SPEC, ANY_SPEC], out_specs=[HBM_SPEC] * (2 * n),
        input_output_aliases={t: t for t in range(2 * n)},
        compiler_params=pltpu.CompilerParams(has_side_effects=DATAFLOW),
    )(*srcs, *lands, send_sems, recv_sems, after)
    return res[:n], res[n:]


def all_gather_vmem(x_shard, *, name, after=None):
    m_per, n = x_shard.shape
    n_after = 0 if after is None else 1

    def body(x_ref, *rest):
        out_ref, send_sems, recv_sems, local_sem = rest[n_after:]
        x, y, c, chips = _chip_slots()
        me, sibling = (x, y, c), (x, y, 1 - c)

        def rows(px, py, pc):
            return out_ref.at[pl.ds((4 * px + 2 * py + pc) * m_per, m_per), :]

        def copy(k, block, to, src=None):
            return pltpu.make_async_remote_copy(
                src_ref=rows(*block) if src is None else src, dst_ref=rows(*block),
                send_sem=send_sems.at[k], recv_sem=recv_sems.at[k], device_id=to, device_id_type=MESH)

        mine = pltpu.make_async_copy(x_ref, rows(*me), local_sem)
        mine.start()
        first = [copy(0, me, sibling, src=x_ref)]
        first += [copy(1 + j, me, (*chip, c), src=x_ref) for j, chip in enumerate(chips)]
        for cp in first:
            cp.start()
        passed = [copy(4 + j, (*chip, c), sibling) for j, chip in enumerate(chips)]
        for j, chip in enumerate(chips):
            copy(1 + j, (*chip, c), me).wait_recv()
            passed[j].start()
        copy(0, sibling, me).wait_recv()
        for j, chip in enumerate(chips):
            copy(4 + j, (*chip, 1 - c), me).wait_recv()
        for cp in first + passed:
            cp.wait_send()
        mine.wait()

    vmem = pl.BlockSpec(memory_space=pltpu.VMEM)
    return pl.pallas_call(
        body, name=name, out_shape=jax.ShapeDtypeStruct((N_DEV * m_per, n), x_shard.dtype),
        in_specs=[vmem] + [ANY_SPEC] * n_after, out_specs=vmem,
        scratch_shapes=[pltpu.SemaphoreType.DMA((7,)), pltpu.SemaphoreType.DMA((7,)), pltpu.SemaphoreType.DMA],
        compiler_params=pltpu.CompilerParams(vmem_limit_bytes=int(min(
            VMEM_LIMIT_CAP, 2 * (N_DEV + 1) * m_per * n * x_shard.dtype.itemsize + 16 * 2 ** 20))),
    )(x_shard, *([] if after is None else [after]))


def _rope_slab(cols):
    z = jnp.zeros(cols.shape[:-1] + (HALF_ROPE,), cols.dtype)
    return jnp.concatenate([cols[..., :HALF_ROPE], z, cols[..., HALF_ROPE:], z], axis=-1)


def _rope_unslab(slab):
    return jnp.concatenate([slab[..., :HALF_ROPE], slab[..., 2 * HALF_ROPE:3 * HALF_ROPE]], axis=-1)


def _pack_w_in_t(wt_g):
    s, c, d = wt_g.shape
    w = wt_g.reshape(s * c, d)
    c2, c3 = Q_LORA + KV_LORA, Q_LORA + KV_LORA + QK_ROPE
    r = w[c2:c3]
    z = jnp.zeros((HALF_ROPE, d), w.dtype)
    return jnp.concatenate([w[:c2], w[c3:], r[:HALF_ROPE], z, r[HALF_ROPE:], z], axis=0)


def _unpack_w_in_t_grad(dwt):
    d = dwt.shape[1]
    c2 = Q_LORA + KV_LORA
    uv = 2 * SGU_OUT
    slab = dwt[c2 + uv:]
    g = jnp.concatenate([dwt[:c2], slab[:HALF_ROPE], slab[2 * HALF_ROPE:3 * HALF_ROPE], dwt[c2:c2 + uv]], axis=0)
    return g.reshape(N_DEV, g.shape[0] // N_DEV, d)


def _rope_tables(positions):
    inv_freq = ROPE_BASE ** (-jnp.arange(0, QK_ROPE, 2, dtype=F32) / QK_ROPE)
    ang = positions.astype(F32)[:, None] * inv_freq
    cos, sin = jnp.cos(ang), jnp.sin(ang)
    z = jnp.zeros_like(cos)
    return jnp.concatenate([cos, z, cos, z], axis=-1), jnp.concatenate([-sin, z, sin, z], axis=-1)


def _mlp_up(x, gain, w1, tag):
    hn = rms_fwd(x, gain, name=f"mlp{tag}_norm")

    def act_epi(acc):
        a = jnp.maximum(acc, 0.0)
        return a, a * a

    T = x.shape[0]
    F = w1.shape[0] * w1.shape[2]
    a, act = mm(hn, w1, name=f"mlp{tag}_up", outs=[((T, F), BF, None), ((T, F), BF, None)], epi=act_epi)
    return hn, a, act


def _mlp_down(x, act, w2, tag, part=0):
    n = w2.shape[1]
    bm = _tile(x.shape[0], MM_TILE)
    bn = _tile(n, MM_TILE)
    per = n // bn
    return mm(act, w2, name=f"mlp{tag}_down{part}", out=((x.shape[0], n), F32), bm=bm, bn=bn,
              epi=lambda acc, r: (acc + r[...],), epi_ins=[(x, (bm, bn), lambda i, j, k: (i, part * per + j))])


def _mlp_bwd_weights(w1, w2, saved, dxb, tag):
    hn, a, act = saved
    T, D = dxb.shape
    F = a.shape[1]
    bm = _tile(T, MM_TILE)
    bn = _tile(F, min(MM_TILE, w1.shape[2]))
    dhid = mm(dxb, w2, tb=True, name=f"mlp{tag}_dhid", out=((T, F), BF), bm=bm, bn=bn,
              epi=lambda acc, a_ref: (2.0 * a_ref[...].astype(F32) * acc,),
              epi_ins=[(a, (bm, bn), lambda i, j, k: (i, j))])
    dw2 = mm(act, dxb, ta=True, name=f"mlp{tag}_dw2", out=((F, D), BF))
    dw1 = mm(hn, dhid, ta=True, name=f"mlp{tag}_dw1", out=(w1.shape, BF))
    return dhid, dw1, dw2.reshape(N_DEV, F // N_DEV, D)


def _reduce_begin(grads, tag):
    return sibling_start(grads, name=f"reduce_sibling_start_{tag}")


def _reduce_continue(sib, after, tag, index):
    grads, a_bufs = sibling_wait(sib, after, name=f"reduce_sibling_wait_{tag}")
    pairs = [run_job(pair_job(g, a), index=index, name=f"pair_sum_{tag}{t}")[0]
             for t, (g, a) in enumerate(zip(grads, a_bufs))]
    return grads, a_bufs, chips_start(pairs, name=f"reduce_chips_start_{tag}")


def kernel(x, positions, e_norm_mix, e_w_in, e_q_norm, e_w_uq, e_kv_norm, e_w_ukv, e_v_norm, e_sgu_w, e_sgu_b, e_mla_out_norm, e_sgu_out_norm, e_w_out, o_norm_mix, o_w_in, o_conv_w, o_w_out, mlp_norm, mlp_w1, mlp_w2, final_norm, loss_target, m_e_norm_mix, m_e_w_in, m_e_q_norm, m_e_w_uq, m_e_kv_norm, m_e_w_ukv, m_e_v_norm, m_e_sgu_w, m_e_sgu_b, m_e_mla_out_norm, m_e_sgu_out_norm, m_e_w_out, m_o_norm_mix, m_o_w_in, m_o_conv_w, m_o_w_out, m_mlp_norm, m_mlp_w1, m_mlp_w2, m_final_norm, v_e_norm_mix, v_e_w_in, v_e_q_norm, v_e_w_uq, v_e_kv_norm, v_e_w_ukv, v_e_v_norm, v_e_sgu_w, v_e_sgu_b, v_e_mla_out_norm, v_e_sgu_out_norm, v_e_w_out, v_o_norm_mix, v_o_w_in, v_o_conv_w, v_o_w_out, v_mlp_norm, v_mlp_w1, v_mlp_w2, v_final_norm):
    T, D = x.shape[1], x.shape[2]
    d_shard = o_norm_mix.shape[1]
    x0 = x[0]
    target = loss_target[0]
    me = 4 * lax.axis_index("x") + 2 * lax.axis_index("y") + lax.axis_index("c")

    bf = lambda s: s.astype(BF)
    gather_groups = [[bf(jnp.transpose(e_w_in[0])), bf(e_w_uq[0]), bf(e_w_ukv[0])], [bf(e_w_out[0]), bf(mlp_w1[0])],
                     [bf(mlp_w2[0]), bf(o_w_in[0])], [bf(o_w_out[0]), bf(mlp_w1[1])], [bf(mlp_w2[1])]]
    small_rows = jnp.concatenate([o_norm_mix, o_conv_w[0], jnp.zeros((4, d_shard), F32)], axis=0)
    small_flat = all_gather_vmem(small_rows, name="gather_small")
    started, start_token = gather_start(gather_groups[:1], small_flat, name="gather_start0")
    started_rest, rest_token = gather_start(gather_groups[1:], start_token, name="gather_start1")
    started += started_rest

    def gathered(gi, after):
        srcs, lands = gather_wait(started[gi], after, name=f"gather_wait{gi}")
        return gather_finish(srcs, lands, name=f"gather_finish{gi}")

    small_g = small_flat.reshape(N_DEV, 8, d_shard)
    o_norm_full = small_g[:, 0, :].reshape(1, D)
    conv_w_full = jnp.transpose(small_g[:, 1:4, :], (1, 0, 2)).reshape(3, D)
    w_tril = jnp.tril(e_sgu_w[0])
    w_tril_b = w_tril.astype(BF)
    w_tril_tb = jnp.swapaxes(w_tril, 1, 2).astype(BF)
    b_full = jnp.repeat(e_sgu_b[0].T, CH, axis=1)
    v_gain = e_v_norm[0].reshape(1, SGU_OUT)
    cos_t, sin_t = _rope_tables(positions[0])
    mlp_gain = [mlp_norm[0:1], mlp_norm[1:2]]
    final_gain = final_norm.reshape(1, D)

    h0 = rms_fwd(x0, e_norm_mix, name="e_norm", deps=[rest_token])
    g_w_in_t, g_w_uq, w_ukv = gathered(
        0, [h0, cos_t, sin_t, w_tril_b, w_tril_tb, b_full, o_norm_full, conv_w_full])
    w_in_t = _pack_w_in_t(g_w_in_t)
    w_uq = jnp.concatenate([g_w_uq[..., :QK_NOPE], _rope_slab(g_w_uq[..., QK_NOPE:])], axis=-1)
    proj = mm(h0, w_in_t, tb=True, name="e_in", out=((T, w_in_t.shape[0]), F32), bn=_tile(w_in_t.shape[0], 640))
    qn, kvn, krope = mla_prep(proj, e_q_norm, e_kv_norm, cos_t, sin_t, name="mla_prep")
    bm = _tile(T, MM_TILE)

    def q_epi(acc, cos_ref, sin_ref):
        return (jnp.concatenate([acc[:, :QK_NOPE], _rope_fwd(acc[:, QK_NOPE:], cos_ref[...], sin_ref[...])], axis=-1),)

    q = mm(qn, w_uq, name="mla_q", out=((T, HEADS * HEAD_PAD), BF), bm=bm, bn=HEAD_PAD, epi=q_epi,
           epi_ins=[(cos_t, (bm, LANES), lambda i, j, k: (i, 0)), (sin_t, (bm, LANES), lambda i, j, k: (i, 0))])

    def kv_epi(acc, kr_ref):
        return jnp.concatenate([acc[:, :QK_NOPE].astype(BF), kr_ref[...]], axis=-1), acc[:, QK_NOPE:]

    k, v = mm(kvn, w_ukv, name="mla_kv", bm=bm, bn=HEAD_PAD, epi=kv_epi,
              outs=[((T, HEADS * HEAD_PAD), BF, HEAD_PAD), ((T, MLA_OUT), BF, V_HEAD)],
              epi_ins=[(krope, (bm, LANES), lambda i, j, k: (i, 0))])
    attn, attn_lse = attn_fwd(q, k, v, name="attn_fwd")
    mixed = mix_fwd(attn, proj, e_mla_out_norm, e_sgu_out_norm, v_gain, w_tril_b, b_full, name="mix_fwd")
    bn = _tile(D, MM_TILE)
    g_w_out_e, w1_0 = gathered(1, [mixed])
    w_out_e = g_w_out_e.reshape(-1, D)
    x1 = mm(mixed, w_out_e, name="e_out", out=((T, D), F32), bm=bm, bn=bn,
            epi=lambda acc, r: (acc + r[...],), epi_ins=[(x0, (bm, bn), lambda i, j, k: (i, j))])
    hn0, a0, act0 = _mlp_up(x1, mlp_gain[0], w1_0, 0)
    g_w2_0, g_w_in_o = gathered(2, [act0])
    w2_0 = g_w2_0.reshape(-1, D)
    x2 = _mlp_down(x1, act0, w2_0, 0)
    ho = rms_fwd(x2, o_norm_full, name="o_norm")
    proj_o = mm(ho, g_w_in_o, name="o_in", out=((T, 3 * D), F32))
    gated = conv_fwd(proj_o, conv_w_full, name="conv_fwd")
    g_w_out_o, w1_1 = gathered(3, [gated])
    w_out_o = g_w_out_o.reshape(-1, D)
    x3 = mm(gated, w_out_o, name="o_out", out=((T, D), F32), bm=bm, bn=bn,
            epi=lambda acc, r: (acc + r[...],), epi_ins=[(x2, (bm, bn), lambda i, j, k: (i, j))])
    hn1, a1, act1 = _mlp_up(x3, mlp_gain[1], w1_1, 1)
    (g_w2_1,) = gathered(4, [act1])
    w2_1 = g_w2_1.reshape(-1, D)
    x4 = _mlp_down(x3, act1, w2_1, 1)
    w1, w2 = [w1_0, w1_1], [w2_0, w2_1]
    mlp0_saved, mlp1_saved = (hn0, a0, act0), (hn1, a1, act1)

    dx4, dx4b, d_final, loss_part = loss_bwd([x4], final_gain, target, name="loss_bwd")

    hosted = dict(job_index=device_index())
    dhid1, dw1_1, dw2_1 = _mlp_bwd_weights(w1[1], w2[1], mlp1_saved, dx4b, 1)
    sib_r0 = _reduce_begin([dw1_1, dw2_1], "r0")
    dhn1 = mm(dhid1, w1[1], tb=True, name="mlp1_dhn", out=((T, D), F32), deps=[sib_r0[-1]])
    grads_r0, a_r0 = sibling_wait(sib_r0, dhn1, name="reduce_sibling_wait_r0")
    dx3, dx3b, d_mlp1 = rms_bwd(x3, mlp_gain[1], dhn1, dres=dx4, name="mlp1_norm_bwd")

    dgated, ((pair_r0a,),) = mm(dx3b, w_out_o, tb=True, name="o_out_dx", out=((T, D), F32),
                                jobs=[pair_job(grads_r0[0], a_r0[0])], **hosted)
    dw_out_o, ((pair_r0b,),) = mm(gated, dx3b, ta=True, name="o_out_dw", out=((D, D), BF),
                                  jobs=[pair_job(grads_r0[1], a_r0[1])], **hosted)
    st_r0 = chips_start([pair_r0a, pair_r0b], name="reduce_chips_start_r0")
    dproj_o, dconv_full = conv_bwd(dgated, proj_o, conv_w_full, name="conv_bwd", deps=[st_r0[-1]])
    dw_in_o = mm(ho, dproj_o, ta=True, name="o_in_dw", out=(g_w_in_o.shape, BF))
    sib_r1 = _reduce_begin([dw_out_o.reshape(g_w_out_o.shape), dw_in_o], "r1")
    dho = mm(dproj_o, g_w_in_o, tb=True, name="o_in_dx", out=((T, D), F32), deps=[sib_r1[-1]])
    grads_r1, a_r1 = sibling_wait(sib_r1, dho, name="reduce_sibling_wait_r1")
    dx2, dx2b, d_onorm_full = rms_bwd(x2, o_norm_full, dho, dres=dx3, name="o_norm_bwd")

    hn0, a0, act0 = mlp0_saved
    d_ff = a0.shape[1]
    bm_h, bn_h = _tile(T, MM_TILE), _tile(d_ff, min(MM_TILE, w1[0].shape[2]))
    dhid0, ((pair_r1a,), (pair_r1b,)) = mm(
        dx2b, w2[0], tb=True, name="mlp0_dhid", out=((T, d_ff), BF), bm=bm_h, bn=bn_h,
        epi=lambda acc, a_ref: (2.0 * a_ref[...].astype(F32) * acc,),
        epi_ins=[(a0, (bm_h, bn_h), lambda i, j, k: (i, j))],
        jobs=[pair_job(grads_r1[0], a_r1[0]), pair_job(grads_r1[1], a_r1[1])], **hosted)
    st_r1 = chips_start([pair_r1a, pair_r1b], name="reduce_chips_start_r1")
    dw2_0 = mm(act0, dx2b, ta=True, name="mlp0_dw2", out=((d_ff, D), BF), deps=[st_r1[-1]])
    b_r0 = chips_wait(st_r0, dw2_0, name="reduce_chips_wait_r0")
    dw1_0, (r_w1, r_w2) = mm(
        hn0, dhid0, ta=True, name="mlp0_dw1", out=(w1[0].shape, BF),
        jobs=[adam_job(grads_r0[0], a_r0[0], b_r0[0], mlp_w1, m_mlp_w1, v_mlp_w1, 1, None),
              adam_job(grads_r0[1], a_r0[1], b_r0[1], mlp_w2, m_mlp_w2, v_mlp_w2, 1, None)], **hosted)
    sib_r2 = _reduce_begin([dw1_0, dw2_0.reshape(N_DEV, d_ff // N_DEV, D)], "r2")
    dhn0 = mm(dhid0, w1[0], tb=True, name="mlp0_dhn", out=((T, D), F32), deps=[sib_r2[-1]])
    grads_r2, a_r2 = sibling_wait(sib_r2, dhn0, name="reduce_sibling_wait_r2")
    dx1, dx1b, d_mlp0 = rms_bwd(x1, mlp_gain[0], dhn0, dres=dx2, name="mlp0_norm_bwd")

    dmixed, ((pair_r2a,),) = mm(dx1b, w_out_e, tb=True, name="e_out_dx", out=((T, MLA_OUT + SGU_OUT), F32),
                                jobs=[pair_job(grads_r2[0], a_r2[0])], **hosted)
    dw_out_e, ((pair_r2b,),) = mm(mixed, dx1b, ta=True, name="e_out_dw", out=(w_out_e.shape, BF),
                                  jobs=[pair_job(grads_r2[1], a_r2[1])], **hosted)
    st_r2 = chips_start([pair_r2a, pair_r2b], name="reduce_chips_start_r2")
    (dattn, duv, d_mla_out, d_sgu_out, d_vgain, d_sgu_w, d_b_full) = mix_bwd(
        dmixed, attn, proj, e_mla_out_norm, e_sgu_out_norm, v_gain, w_tril_b, w_tril_tb, b_full, name="mix_bwd",
        deps=[st_r2[-1]])
    b_r1 = chips_wait(st_r1, dattn, name="reduce_chips_wait_r1")
    dq, dk, dv = attn_bwd(q, k, v, attn, attn_lse, dattn, name="attn_bwd")
    dq_lin, dkv_lin, dkr = mla_bwd_prep(dq, dk, dv, cos_t, sin_t, name="mla_bwd_prep")
    dw_uq_pad = mm(qn, dq_lin, ta=True, name="mla_q_dw", out=(w_uq.shape, BF))
    dw_ukv = mm(kvn, dkv_lin, ta=True, name="mla_kv_dw", out=(w_ukv.shape, BF))
    dw_uq = jnp.concatenate([dw_uq_pad[..., :QK_NOPE], _rope_unslab(dw_uq_pad[..., QK_NOPE:])], axis=-1)
    sib_r2b = _reduce_begin([dw_out_e.reshape(g_w_out_e.shape), dw_uq, dw_ukv], "r2b")
    dqn = mm(dq_lin, w_uq, tb=True, name="mla_q_dx", out=((T, Q_LORA), F32), deps=[sib_r2b[-1]])
    dkvn = mm(dkv_lin, w_ukv, tb=True, name="mla_kv_dx", out=((T, KV_LORA), F32), deps=[sib_r2b[-1]])
    grads_r2b, a_r2b, st_r2b = _reduce_continue(sib_r2b, dkvn, "r2b", hosted["job_index"])
    dcq, d_qnorm = rms_bwd(proj, e_q_norm, dqn, col_block=0, want_f32=False, name="q_norm_bwd", deps=[st_r2b[-1]])
    dckv, d_kvnorm = rms_bwd(proj, e_kv_norm, dkvn, col_block=1, want_f32=False, name="kv_norm_bwd")
    dproj = jnp.concatenate([dcq, dckv, duv, dkr], axis=-1)
    dw_in_t_pad, (r_w_out_o, r_w_in_o) = mm(
        dproj, h0, ta=True, name="e_in_dw", out=(w_in_t.shape, BF), bm=_tile(w_in_t.shape[0], 640),
        jobs=[adam_job(grads_r1[0], a_r1[0], b_r1[0], o_w_out, m_o_w_out, v_o_w_out, 0, None),
              adam_job(grads_r1[1], a_r1[1], b_r1[1], o_w_in, m_o_w_in, v_o_w_in, 0, None)], **hosted)
    dw_in_t = _unpack_w_in_t_grad(dw_in_t_pad)
    sib_r3 = _reduce_begin([dw_in_t], "r3")
    dh0 = mm(dproj, w_in_t, name="e_in_dx", out=((T, D), F32), deps=[sib_r3[-1]])
    grads_r3, a_r3, st_r3 = _reduce_continue(sib_r3, dh0, "r3", hosted["job_index"])
    tok_r3 = st_r3[-1]
    grad_x, d_enorm = rms_bwd(x0, e_norm_mix, dh0, dres=dx1, want_bf=False, name="e_norm_bwd", deps=[tok_r3])
    b_r2 = chips_wait(st_r2, grad_x, name="reduce_chips_wait_r2")

    def finish(grads, a_bufs, b_bufs, t, w, m, v, layer=0, prev=None, tag="", deps=()):
        return run_job(adam_job(grads[t], a_bufs[t], b_bufs[t], w, m, v, layer, prev), index=hosted["job_index"],
                       name=f"adam_{tag}", deps=deps)

    r_w1 = finish(grads_r2, a_r2, b_r2, 0, mlp_w1, m_mlp_w1, v_mlp_w1, 0, r_w1, tag="w1_l0", deps=[tok_r3])
    r_w2 = finish(grads_r2, a_r2, b_r2, 1, mlp_w2, m_mlp_w2, v_mlp_w2, 0, r_w2, tag="w2_l0", deps=[r_w1[1]])
    b_r2b = chips_wait(st_r2b, r_w2[1], name="reduce_chips_wait_r2b")
    r_w_out_e = finish(grads_r2b, a_r2b, b_r2b, 0, e_w_out, m_e_w_out, v_e_w_out, tag="e_w_out")
    r_w_uq = finish(grads_r2b, a_r2b, b_r2b, 1, e_w_uq, m_e_w_uq, v_e_w_uq, tag="e_w_uq")
    r_w_ukv = finish(grads_r2b, a_r2b, b_r2b, 2, e_w_ukv, m_e_w_ukv, v_e_w_ukv, tag="e_w_ukv")
    b_r3 = chips_wait(st_r3, r_w_out_e[1], name="reduce_chips_wait_r3")
    g_w_in_t = reduce_sum(grads_r3[0], a_r3[0], b_r3[0], name="sum_e_w_in")
    w_in_upd_t = adam_rows(g_w_in_t, jnp.transpose(e_w_in[0]), jnp.transpose(m_e_w_in[0]), jnp.transpose(v_e_w_in[0]),
                           name="adam_e_w_in")
    r_w_in = [jnp.transpose(t)[None] for t in (g_w_in_t, *w_in_upd_t)]

    d_sgu_b = jnp.transpose(d_b_full[:, ::CH])
    d_sgu_w_tril = jnp.tril(d_sgu_w)
    rep = [("e_norm_mix", e_norm_mix, m_e_norm_mix, v_e_norm_mix, d_enorm),
           ("e_q_norm", e_q_norm, m_e_q_norm, v_e_q_norm, d_qnorm),
           ("e_kv_norm", e_kv_norm, m_e_kv_norm, v_e_kv_norm, d_kvnorm),
           ("e_v_norm", e_v_norm, m_e_v_norm, v_e_v_norm, d_vgain),
           ("e_sgu_w", e_sgu_w, m_e_sgu_w, v_e_sgu_w, d_sgu_w_tril),
           ("e_sgu_b", e_sgu_b, m_e_sgu_b, v_e_sgu_b, d_sgu_b),
           ("e_mla_out_norm", e_mla_out_norm, m_e_mla_out_norm, v_e_mla_out_norm, d_mla_out),
           ("e_sgu_out_norm", e_sgu_out_norm, m_e_sgu_out_norm, v_e_sgu_out_norm, d_sgu_out),
           ("mlp_norm", mlp_norm, m_mlp_norm, v_mlp_norm, jnp.concatenate([d_mlp0, d_mlp1], axis=0)),
           ("final_norm", final_norm, m_final_norm, v_final_norm, d_final)]
    sizes = [int(np.prod(r[1].shape)) for r in rep]
    n_rep = sum(sizes)
    n_all = n_rep + 4 * D + 1
    width = -(-n_all // (8 * LANES)) * LANES
    pad = 8 * width - n_all
    flat = jnp.concatenate([r[4].reshape(-1) for r in rep]
                           + [d_onorm_full.reshape(-1), dconv_full.reshape(-1), loss_part[0, :1],
                              jnp.zeros((pad,), F32)])
    summed = sum_rows8(all_gather_vmem(flat.reshape(8, width), name="gather_small_grads", after=b_r3[0]), 8,
                       name="sum_small_grads").reshape(-1)

    loss = summed[n_rep + 4 * D]

    def pack_rep(i):
        return jnp.concatenate([r[i].reshape(-1) for r in rep]).reshape(n_rep // LANES, LANES)

    g_rep = summed[:n_rep].reshape(n_rep // LANES, LANES)
    d_rep, nm_rep, nv_rep = adam_flat(g_rep, pack_rep(1), pack_rep(2), pack_rep(3), name="adam_replicated")

    def unpack_rep(flat2d):
        out, off = {}, 0
        f = flat2d.reshape(-1)
        for r, n in zip(rep, sizes):
            out[r[0]] = f[off:off + n].reshape(r[1].shape)
            off += n
        return out

    small = {"grad": unpack_rep(g_rep), "delta": unpack_rep(d_rep), "new_m": unpack_rep(nm_rep),
             "new_v": unpack_rep(nv_rep)}
    g_onorm = lax.dynamic_slice(summed[n_rep:n_rep + D].reshape(1, D), (0, me * d_shard), (1, d_shard))
    g_conv = lax.dynamic_slice(summed[n_rep + D:n_rep + 4 * D].reshape(3, D), (0, me * d_shard), (3, d_shard))

    def pack_sharded(norm_part, conv_part):
        return jnp.concatenate([norm_part, conv_part, jnp.zeros((4, d_shard), F32)], axis=0)

    g_sh = pack_sharded(g_onorm, g_conv)
    d_sh, nm_sh, nv_sh = adam_flat(g_sh, pack_sharded(o_norm_mix, o_conv_w[0]), pack_sharded(m_o_norm_mix, m_o_conv_w[0]),
                                   pack_sharded(v_o_norm_mix, v_o_conv_w[0]), name="adam_sharded_small")
    for kind, arr in (("grad", g_sh), ("delta", d_sh), ("new_m", nm_sh), ("new_v", nv_sh)):
        small[kind]["o_norm_mix"] = arr[0:1]
        small[kind]["o_conv_w"] = arr[1:4][None]

    big = {"e_w_in": r_w_in, "e_w_uq": r_w_uq, "e_w_ukv": r_w_ukv, "e_w_out": r_w_out_e, "o_w_in": r_w_in_o,
           "o_w_out": r_w_out_o, "mlp_w1": r_w1, "mlp_w2": r_w2}
    order = ["e_norm_mix", "e_w_in", "e_q_norm", "e_w_uq", "e_kv_norm", "e_w_ukv", "e_v_norm", "e_sgu_w", "e_sgu_b",
             "e_mla_out_norm", "e_sgu_out_norm", "e_w_out", "o_norm_mix", "o_w_in", "o_conv_w", "o_w_out", "mlp_norm",
             "mlp_w1", "mlp_w2", "final_norm"]
    result = [loss, grad_x[None]]
    for ki, kind in enumerate(("grad", "delta", "new_m", "new_v")):
        for nm in order:
            result.append(big[nm][ki] if nm in big else small[kind][nm])
    return tuple(result)
```

```python
import numpy as np
import jax
import jax.numpy as jnp
from jax import lax
from jax.experimental import pallas as pl
from jax.experimental.pallas import tpu as pltpu

BF = jnp.bfloat16
F32 = jnp.float32
MESH = pl.DeviceIdType.MESH
N_DEV = 8

EPS = 1e-6
HEADS = 8
Q_LORA = 512
KV_LORA = 512
QK_NOPE = 128
QK_ROPE = 64
HALF_ROPE = QK_ROPE // 2
V_HEAD = 128
HEAD_PAD = 256
ROPE_BASE = 10000.0
GROUPS = 8
CH = 128
CHUNK = 128
SGU_OUT = GROUPS * CH
MLA_OUT = HEADS * V_HEAD
ATTN_SCALE = float((QK_NOPE + QK_ROPE) ** -0.5)

ADAM_LR = 0.001
ADAM_B1 = 0.9
ADAM_B2 = 0.999
ADAM_EPS = 1e-08
ADAM_WD = 0.01
ADAM_STEP = 10
ADAM_C1 = 1.0 - ADAM_B1 ** ADAM_STEP
ADAM_C2 = 1.0 - ADAM_B2 ** ADAM_STEP

V7X_VMEM_BYTES = 64 * 2 ** 20
VMEM_LIMIT_CAP = V7X_VMEM_BYTES - 6 * 2 ** 20
LANES = 128
ROW_TILE = 256
ATTN_TILE = 512
STREAM_BLOCK_ELEMS = 512 * 1024
MM_TILE = 1024
HOST_TILE = 512
MM_K_TILE = 2048
MM_K_BLOCK_MAX = 3072


def _padded_bytes(block, dtype):
    dims = [d for d in block if d is not None]
    if len(dims) >= 1:
        dims[-1] = -(-dims[-1] // LANES) * LANES
    if len(dims) >= 2:
        dims[-2] = -(-dims[-2] // 16) * 16
    return int(np.prod(dims)) * jnp.dtype(dtype).itemsize


def _pcall(body, *, name, grid, ins, outs, scratch=(), semantics=None, aliases=None, prefetch=None, deps=()):
    any_spec = pl.BlockSpec(memory_space=pl.ANY)
    if deps:
        n_lead = len(ins) + (1 if prefetch is not None else 0)
        n_deps = len(deps)
        inner = body

        def body(*refs):
            inner(*refs[:n_lead], *refs[n_lead + n_deps:])

        ins = list(ins) + [(d, None, None) for d in deps]
    in_specs = [any_spec if b is None else pl.BlockSpec(b, m) for _, b, m in ins]
    out_specs = [any_spec if b is None else pl.BlockSpec(b, m) for _, _, b, m in outs]
    out_shape = [pltpu.HBM(s, d) for s, d, _, _ in outs]
    est = 0
    for a, b, _ in ins:
        if b is not None:
            est += 2 * _padded_bytes(b, a.dtype)
    for _, d, b, _ in outs:
        if b is not None:
            est += 2 * _padded_bytes(b, d)
    for s in scratch:
        if hasattr(s, "shape") and hasattr(s, "dtype"):
            est += _padded_bytes(s.shape, s.dtype)
    limit = int(min(VMEM_LIMIT_CAP, est + 16 * 2 ** 20))
    params = pltpu.CompilerParams(
        dimension_semantics=semantics or ("arbitrary",) * len(grid), vmem_limit_bytes=limit)
    args = [pltpu.with_memory_space_constraint(a, pltpu.HBM) for a, _, _ in ins]
    if prefetch is not None:
        grid_spec = pltpu.PrefetchScalarGridSpec(
            num_scalar_prefetch=1, grid=grid, in_specs=in_specs, out_specs=out_specs, scratch_shapes=list(scratch))
        call = pl.pallas_call(body, out_shape=out_shape, grid_spec=grid_spec, name=name, compiler_params=params,
                              input_output_aliases=aliases or {})
        return call(prefetch, *args)
    call = pl.pallas_call(body, out_shape=out_shape, grid=grid, in_specs=in_specs, out_specs=out_specs,
                          scratch_shapes=list(scratch), name=name, compiler_params=params,
                          input_output_aliases=aliases or {})
    return call(*args)


def _tile(dim, pref, quantum=LANES):
    if dim <= pref:
        return dim
    t = (pref // quantum) * quantum
    while t >= quantum:
        if dim % t == 0:
            return t
        t -= quantum
    return dim


def _vshape(arr_shape):
    if len(arr_shape) == 2:
        return tuple(arr_shape)
    s, r, c = arr_shape
    return (r, s * c)


def _vblock(arr_shape, br, bc, rc):
    if len(arr_shape) == 2:
        return (br, bc), (lambda *g: rc(*g))
    _, _, c = arr_shape
    assert c % bc == 0, (arr_shape, bc)
    per = c // bc

    def imap(*g):
        ri, ci = rc(*g)
        return (ci // per, ri, ci % per)

    return (None, br, bc), imap


def _shard_width(*shapes):
    w = None
    for s in shapes:
        if len(s) == 3:
            w = s[2] if w is None else int(np.gcd(w, s[2]))
    return w


def mm(a, b, *, name, ta=False, tb=False, out=None, outs=None, epi=None, epi_ins=(), bm=None, bn=None, bk=None,
       deps=(), jobs=(), job_index=None):
    av, bv = _vshape(a.shape), _vshape(b.shape)
    M, K = (av[1], av[0]) if ta else av
    K2, N = (bv[1], bv[0]) if tb else bv
    assert K == K2, (a.shape, b.shape, ta, tb)
    if outs is None:
        outs = [(out[0], out[1], None)]
    a_sw = _shard_width(a.shape)
    b_sw = _shard_width(b.shape)
    o_sw = _shard_width(*[o[0] for o in outs])
    m_lim = a_sw if (ta and a_sw) else None
    k_lim = [w for w in ((a_sw if not ta else None), (b_sw if tb else None)) if w]
    n_lim = [w for w in ((b_sw if not tb else None), o_sw) if w]
    if bm is None:
        bm = _tile(M, min([MM_TILE] + ([m_lim] if m_lim else [])))
    if bn is None:
        bn = _tile(N, min([MM_TILE] + n_lim))
    k_shards = 0
    if tb and len(b.shape) == 3 and bk is None and not (a_sw and not ta):
        k_shards = 1
        while 2 * k_shards <= b.shape[0] and 2 * k_shards * b_sw <= MM_K_BLOCK_MAX:
            k_shards *= 2
        bk = k_shards * b_sw
    if bk is None:
        bk = K if (K <= 4096 and not k_lim) else _tile(K, min([MM_K_TILE] + k_lim))
    assert M % bm == 0 and N % bn == 0 and K % bk == 0, (name, M, N, K, bm, bn, bk)
    nk = K // bk
    grid = (M // bm, N // bn, nk)
    if ta:
        a_blk, a_map = _vblock(a.shape, bk, bm, lambda i, j, k: (k, i))
    else:
        a_blk, a_map = _vblock(a.shape, bm, bk, lambda i, j, k: (i, k))
    if k_shards:
        b_blk, b_map = (k_shards, bn, b_sw), (lambda i, j, k: (k, j, 0))
    elif tb:
        b_blk, b_map = _vblock(b.shape, bn, bk, lambda i, j, k: (j, k))
    else:
        b_blk, b_map = _vblock(b.shape, bk, bn, lambda i, j, k: (k, j))
    dn = (((0 if ta else 1,), (1 if tb else 0,)), ((), ()))
    ins = [(a, a_blk, a_map), (b, b_blk, b_map)] + list(epi_ins)
    out_list = []
    for shape, dtype, cols in outs:
        cols = cols or bn
        blk, imap = _vblock(shape, bm, cols, lambda i, j, k: (i, j))
        out_list.append((shape, dtype, blk, imap))
    n_e, n_o = len(epi_ins), len(out_list)

    n_steps = grid[0] * grid[1] * nk
    built = [job(n_steps) for job in jobs]
    aliases = {}
    job_slices = []
    if built:
        def lin(i, j, k):
            return (i * grid[1] + j) * nk + k

        ins = [(arr, blk, None if blk is None else (lambda i, j, k, s, f=f: f(i, j, k))) for arr, blk, f in ins]
        out_list = [(sh, dt, blk, (lambda i, j, k, s, f=f: f(i, j, k))) for sh, dt, blk, f in out_list]
        n_main_in, n_main_out = len(ins), len(out_list)
        for jb in built:
            i0, o0 = len(ins), len(out_list)
            ins += [(arr, blk, None if blk is None else (lambda i, j, k, s, f=f: f(lin(i, j, k), s)))
                    for arr, blk, f in jb["ins"]]
            out_list += [(sh, dt, blk, (lambda i, j, k, s, f=f: f(lin(i, j, k), s))) for sh, dt, blk, f in jb["outs"]]
            aliases.update({1 + i0 + ai: o0 + ao for ai, ao in jb["aliases"].items()})
            job_slices.append((i0, len(jb["ins"]), o0, len(jb["outs"])))
    n_in_total = len(ins)

    def body(*refs):
        if built:
            refs = refs[1:]
        a_ref, b_ref = refs[0], refs[1]
        e_refs = refs[2:2 + n_e]
        o_refs = refs[n_in_total:n_in_total + n_o]
        for jb, (i0, ni, o0, no) in zip(built, job_slices):
            jb["fn"](refs[i0:i0 + ni], refs[n_in_total + o0:n_in_total + o0 + no])

        def finish(acc):
            res = epi(acc, *e_refs) if epi is not None else (acc,)
            for o_ref, r in zip(o_refs, res):
                o_ref[...] = r.astype(o_ref.dtype)

        x = a_ref[...].astype(BF)
        y = b_ref[...].astype(BF)
        if k_shards:
            p = None
            for s in range(k_shards):
                part = lax.dot_general(x[:, s * b_sw:(s + 1) * b_sw], y[s], dn, preferred_element_type=F32)
                p = part if p is None else p + part
        else:
            p = lax.dot_general(x, y, dn, preferred_element_type=F32)
        if nk == 1:
            finish(p)
        else:
            acc_ref = refs[-1]
            k = pl.program_id(2)

            @pl.when(k == 0)
            def _():
                acc_ref[...] = p

            @pl.when(k > 0)
            def _():
                acc_ref[...] += p

            @pl.when(k == nk - 1)
            def _():
                finish(acc_ref[...])

    scratch = [pltpu.VMEM((bm, bn), F32)] if nk > 1 else []
    res = _pcall(body, name=name, grid=grid, ins=ins, outs=out_list, scratch=scratch, deps=deps,
                 semantics=("parallel", "parallel", "arbitrary"), prefetch=job_index if built else None, aliases=aliases)
    main = res[0] if n_o == 1 else res[:n_o]
    if not built:
        return main
    return main, [res[o0:o0 + no] for _, _, o0, no in job_slices]


_GELU_K = float(np.sqrt(2.0 / np.pi))
_GELU_C = 0.044715


def _gelu(x):
    t = jnp.tanh(_GELU_K * (x + _GELU_C * (x * x * x)))
    return 0.5 * x * (1.0 + t)


def _gelu_grad(x):
    t = jnp.tanh(_GELU_K * (x + _GELU_C * (x * x * x)))
    return 0.5 * (1.0 + t) + 0.5 * x * (1.0 - t * t) * (_GELU_K * (1.0 + 3.0 * _GELU_C * (x * x)))


def _rstd(x):
    return lax.rsqrt(jnp.mean(x * x, axis=-1, keepdims=True) + EPS)


def _rms_bwd(x, gain, dy):
    r = _rstd(x)
    xh = x * r
    gdy = dy * gain
    dx = r * (gdy - xh * jnp.mean(gdy * xh, axis=-1, keepdims=True))
    return dx, dy * xh


def _rope_fwd(x, cos_t, sin_t):
    return x * cos_t + pltpu.roll(x, 2 * HALF_ROPE, 1) * sin_t


def _rope_bwd(dy, cos_t, sin_t):
    return dy * cos_t + pltpu.roll(dy * sin_t, 2 * HALF_ROPE, 1)


def _acc_rows(ref, val, first):
    s = jnp.sum(val, axis=0, keepdims=True)

    @pl.when(first)
    def _():
        ref[...] = s

    @pl.when(jnp.logical_not(first))
    def _():
        ref[...] += s


def rms_fwd(x, gain, *, name, col_block=0, width=None, deps=()):
    T = x.shape[0]
    width = width or x.shape[1]
    tm = _tile(T, ROW_TILE, 8)

    def body(x_ref, g_ref, o_ref):
        v = x_ref[...]
        o_ref[...] = (v * _rstd(v) * g_ref[...]).astype(BF)

    return _pcall(body, name=name, grid=(T // tm,),
                  ins=[(x, (tm, width), lambda i: (i, col_block)), (gain, (1, width), lambda i: (0, 0))],
                  outs=[((T, width), BF, (tm, width), lambda i: (i, 0))], semantics=("parallel",), deps=deps)[0]


def rms_bwd(x, gain, dy, *, name, col_block=0, dres=None, want_f32=True, want_bf=True, deps=()):
    T, width = dy.shape
    tm = _tile(T, ROW_TILE, 8)
    has_res = dres is not None

    def body(*refs):
        x_ref, g_ref, dy_ref = refs[:3]
        pos = 3
        res_ref = None
        if has_res:
            res_ref = refs[pos]
            pos += 1
        outs = refs[pos:]
        dx, dg_rows = _rms_bwd(x_ref[...], g_ref[...], dy_ref[...])
        if has_res:
            dx = dx + res_ref[...]
        o = 0
        if want_f32:
            outs[o][...] = dx
            o += 1
        if want_bf:
            outs[o][...] = dx.astype(BF)
            o += 1
        _acc_rows(outs[o], dg_rows, pl.program_id(0) == 0)

    ins = [(x, (tm, width), lambda i: (i, col_block)), (gain, (1, width), lambda i: (0, 0)),
           (dy, (tm, width), lambda i: (i, 0))]
    if has_res:
        ins.append((dres, (tm, width), lambda i: (i, 0)))
    outs = []
    if want_f32:
        outs.append(((T, width), F32, (tm, width), lambda i: (i, 0)))
    if want_bf:
        outs.append(((T, width), BF, (tm, width), lambda i: (i, 0)))
    outs.append(((1, width), F32, (1, width), lambda i: (0, 0)))
    return _pcall(body, name=name, grid=(T // tm,), ins=ins, outs=outs, deps=deps)


def mla_prep(proj, q_norm, kv_norm, cos_t, sin_t, *, name):
    T = proj.shape[0]
    tm = _tile(T, ROW_TILE, 8)
    kr_block = (proj.shape[1] - LANES) // LANES

    def body(cq_ref, ckv_ref, kr_ref, qg_ref, kg_ref, cos_ref, sin_ref, qn_ref, kvn_ref, krope_ref):
        cq = cq_ref[...]
        qn_ref[...] = (cq * _rstd(cq) * qg_ref[...]).astype(BF)
        ckv = ckv_ref[...]
        kvn_ref[...] = (ckv * _rstd(ckv) * kg_ref[...]).astype(BF)
        krope_ref[...] = _rope_fwd(kr_ref[...], cos_ref[...], sin_ref[...]).astype(BF)

    return _pcall(
        body, name=name, grid=(T // tm,),
        ins=[(proj, (tm, Q_LORA), lambda i: (i, 0)), (proj, (tm, KV_LORA), lambda i: (i, 1)),
             (proj, (tm, LANES), lambda i: (i, kr_block)),
             (q_norm, (1, Q_LORA), lambda i: (0, 0)), (kv_norm, (1, KV_LORA), lambda i: (0, 0)),
             (cos_t, (tm, LANES), lambda i: (i, 0)), (sin_t, (tm, LANES), lambda i: (i, 0))],
        outs=[((T, Q_LORA), BF, (tm, Q_LORA), lambda i: (i, 0)), ((T, KV_LORA), BF, (tm, KV_LORA), lambda i: (i, 0)),
              ((T, LANES), BF, (tm, LANES), lambda i: (i, 0))],
        semantics=("parallel",))


def _attn_scores(q, k_blk, diagonal):
    s = lax.dot_general(q, k_blk, (((1,), (1,)), ((), ())), preferred_element_type=F32) * ATTN_SCALE
    if diagonal:
        row = lax.broadcasted_iota(jnp.int32, s.shape, 0)
        col = lax.broadcasted_iota(jnp.int32, s.shape, 1)
        s = jnp.where(col <= row, s, -jnp.inf)
    return s


def attn_fwd(q, k, v, *, name):
    T = q.shape[0]
    tq = _tile(T, ATTN_TILE, 8)

    def body(q_ref, k_ref, v_ref, o_ref, lse_ref):
        i = pl.program_id(1)
        qv = q_ref[...]

        def block(kb, carry, diagonal):
            m, l, acc = carry
            start = pl.multiple_of(kb * tq, tq)
            s = _attn_scores(qv, k_ref[pl.ds(start, tq), :], diagonal)
            m_new = jnp.maximum(m, jnp.max(s, axis=-1, keepdims=True))
            alpha = jnp.exp(m - m_new)
            p = jnp.exp(s - m_new)
            l = alpha * l + jnp.sum(p, axis=-1, keepdims=True)
            acc = alpha * acc + jnp.dot(p.astype(BF), v_ref[pl.ds(start, tq), :], preferred_element_type=F32)
            return m_new, l, acc

        init = (jnp.full((tq, 1), -jnp.inf, F32), jnp.zeros((tq, 1), F32), jnp.zeros((tq, V_HEAD), F32))
        carry = lax.fori_loop(0, i, lambda kb, c: block(kb, c, False), init)
        m, l, acc = block(i, carry, True)
        o_ref[...] = acc / l
        lse_ref[...] = jnp.broadcast_to(m + jnp.log(l), (tq, V_HEAD))

    return _pcall(
        body, name=name, grid=(HEADS, T // tq),
        ins=[(q, (tq, HEAD_PAD), lambda h, i: (i, h)), (k, (T, HEAD_PAD), lambda h, i: (0, h)),
             (v, (T, V_HEAD), lambda h, i: (0, h))],
        outs=[((T, MLA_OUT), F32, (tq, V_HEAD), lambda h, i: (i, h)),
              ((T, MLA_OUT), F32, (tq, V_HEAD), lambda h, i: (i, h))], semantics=("parallel", "parallel"))


def attn_bwd(q, k, v, o, lse, do, *, name):
    T = q.shape[0]
    tq = _tile(T, ATTN_TILE, 8)

    def body(q_ref, k_ref, v_ref, o_ref, lse_ref, do_ref, dq_ref, dk_ref, dv_ref):
        i = pl.program_id(1)

        @pl.when(i == 0)
        def _():
            dk_ref[...] = jnp.zeros_like(dk_ref)
            dv_ref[...] = jnp.zeros_like(dv_ref)

        qv = q_ref[...]
        do_t = do_ref[...]
        lse_v = lse_ref[:, 0:1]
        delta = jnp.sum(do_t.astype(F32) * o_ref[...], axis=-1, keepdims=True)

        def block(kb, dq, diagonal):
            start = pl.multiple_of(kb * tq, tq)
            k_blk = k_ref[pl.ds(start, tq), :]
            v_blk = v_ref[pl.ds(start, tq), :]
            p = jnp.exp(_attn_scores(qv, k_blk, diagonal) - lse_v)
            dp = lax.dot_general(do_t, v_blk, (((1,), (1,)), ((), ())), preferred_element_type=F32)
            ds = (p * (dp - delta) * ATTN_SCALE).astype(BF)
            dk_ref[pl.ds(start, tq), :] += lax.dot_general(ds, qv, (((0,), (0,)), ((), ())), preferred_element_type=F32)
            dv_ref[pl.ds(start, tq), :] += lax.dot_general(p.astype(BF), do_t, (((0,), (0,)), ((), ())),
                                                          preferred_element_type=F32)
            return dq + jnp.dot(ds, k_blk, preferred_element_type=F32)

        dq = lax.fori_loop(0, i, lambda kb, c: block(kb, c, False), jnp.zeros((tq, HEAD_PAD), F32))
        dq_ref[...] = block(i, dq, True)

    return _pcall(
        body, name=name, grid=(HEADS, T // tq),
        ins=[(q, (tq, HEAD_PAD), lambda h, i: (i, h)), (k, (T, HEAD_PAD), lambda h, i: (0, h)),
             (v, (T, V_HEAD), lambda h, i: (0, h)), (o, (tq, V_HEAD), lambda h, i: (i, h)),
             (lse, (tq, V_HEAD), lambda h, i: (i, h)), (do, (tq, V_HEAD), lambda h, i: (i, h))],
        outs=[((T, HEADS * HEAD_PAD), F32, (tq, HEAD_PAD), lambda h, i: (i, h)),
              ((T, HEADS * HEAD_PAD), F32, (T, HEAD_PAD), lambda h, i: (0, h)),
              ((T, MLA_OUT), F32, (T, V_HEAD), lambda h, i: (0, h))],
        semantics=("parallel", "arbitrary"))


def mla_bwd_prep(dq, dk, dv, cos_t, sin_t, *, name):
    T = dq.shape[0]
    tm = _tile(T, ROW_TILE, 8)

    def body(dq_ref, dk_ref, dv_ref, cos_ref, sin_ref, dql_ref, dkvl_ref, dkr_ref):
        cos_v, sin_v = cos_ref[...], sin_ref[...]
        kr = jnp.zeros((tm, LANES), F32)
        for h in range(HEADS):
            lo = h * HEAD_PAD
            dql_ref[:, lo:lo + QK_NOPE] = dq_ref[:, lo:lo + QK_NOPE].astype(BF)
            dql_ref[:, lo + QK_NOPE:lo + HEAD_PAD] = _rope_bwd(
                dq_ref[:, lo + QK_NOPE:lo + HEAD_PAD], cos_v, sin_v).astype(BF)
            dkvl_ref[:, lo:lo + QK_NOPE] = dk_ref[:, lo:lo + QK_NOPE].astype(BF)
            dkvl_ref[:, lo + QK_NOPE:lo + HEAD_PAD] = dv_ref[:, h * V_HEAD:(h + 1) * V_HEAD].astype(BF)
            kr = kr + dk_ref[:, lo + QK_NOPE:lo + HEAD_PAD]
        dkr_ref[...] = _rope_bwd(kr, cos_v, sin_v).astype(BF)

    W = HEADS * HEAD_PAD
    return _pcall(
        body, name=name, grid=(T // tm,),
        ins=[(dq, (tm, W), lambda i: (i, 0)), (dk, (tm, W), lambda i: (i, 0)), (dv, (tm, MLA_OUT), lambda i: (i, 0)),
             (cos_t, (tm, LANES), lambda i: (i, 0)), (sin_t, (tm, LANES), lambda i: (i, 0))],
        outs=[((T, W), BF, (tm, W), lambda i: (i, 0)), ((T, W), BF, (tm, W), lambda i: (i, 0)),
              ((T, LANES), BF, (tm, LANES), lambda i: (i, 0))],
        semantics=("parallel",))


def _group_norm_stats(vg):
    mu = jnp.mean(vg, axis=-1, keepdims=True)
    d = vg - mu
    r = lax.rsqrt(jnp.mean(d * d, axis=-1, keepdims=True) + EPS)
    return d * r, r


def mix_fwd(a, proj, g_mla, g_sgu, v_gain, w_tril, b_full, *, name):
    T = a.shape[0]
    tm = _tile(T, ROW_TILE, CHUNK)
    n_chunk = tm // CHUNK

    def body(a_ref, u_ref, v_ref, gm_ref, gs_ref, vg_ref, w_ref, b_ref, o_ref, s_scr):
        av = a_ref[...]
        o_ref[:, :MLA_OUT] = (av * _rstd(av) * gm_ref[...]).astype(BF)
        for g in range(GROUPS):
            sl = slice(g * CH, (g + 1) * CH)
            vhat, _ = _group_norm_stats(_gelu(v_ref[:, sl]))
            vn = (vhat * vg_ref[:, sl]).astype(BF)
            u = _gelu(u_ref[:, sl])
            for ci in range(n_chunk):
                rs = slice(ci * CHUNK, (ci + 1) * CHUNK)
                y = jnp.dot(w_ref[g], vn[rs], preferred_element_type=F32) + b_ref[:, sl]
                s_scr[rs, sl] = u[rs] * y
        s = s_scr[...]
        o_ref[:, MLA_OUT:] = (s * _rstd(s) * gs_ref[...]).astype(BF)

    return _pcall(
        body, name=name, grid=(T // tm,),
        ins=[(a, (tm, MLA_OUT), lambda i: (i, 0)), (proj, (tm, SGU_OUT), lambda i: (i, 1)),
             (proj, (tm, SGU_OUT), lambda i: (i, 2)), (g_mla, (1, MLA_OUT), lambda i: (0, 0)),
             (g_sgu, (1, SGU_OUT), lambda i: (0, 0)), (v_gain, (1, SGU_OUT), lambda i: (0, 0)),
             (w_tril, (GROUPS, CHUNK, CHUNK), lambda i: (0, 0, 0)), (b_full, (CHUNK, SGU_OUT), lambda i: (0, 0))],
        outs=[((T, MLA_OUT + SGU_OUT), BF, (tm, MLA_OUT + SGU_OUT), lambda i: (i, 0))],
        scratch=[pltpu.VMEM((tm, SGU_OUT), F32)], semantics=("parallel",))[0]


def mix_bwd(dmixed, a, proj, g_mla, g_sgu, v_gain, w_tril, w_tril_t, b_full, *, name, deps=()):
    T = a.shape[0]
    tm = _tile(T, ROW_TILE, CHUNK)
    n_chunk = tm // CHUNK

    def body(dm_a_ref, dm_s_ref, a_ref, u_ref, v_ref, gm_ref, gs_ref, vg_ref, w_ref, wt_ref, b_ref,
             da_ref, duv_ref, dgm_ref, dgs_ref, dvg_ref, dw_ref, db_ref, s_scr, y_scr):
        first = pl.program_id(0) == 0
        da, dgm_rows = _rms_bwd(a_ref[...], gm_ref[...], dm_a_ref[...])
        da_ref[...] = da.astype(BF)
        _acc_rows(dgm_ref, dgm_rows, first)

        for g in range(GROUPS):
            sl = slice(g * CH, (g + 1) * CH)
            vhat, _ = _group_norm_stats(_gelu(v_ref[:, sl]))
            vn = (vhat * vg_ref[:, sl]).astype(BF)
            u = _gelu(u_ref[:, sl])
            for ci in range(n_chunk):
                rs = slice(ci * CHUNK, (ci + 1) * CHUNK)
                y = jnp.dot(w_ref[g], vn[rs], preferred_element_type=F32) + b_ref[:, sl]
                y_scr[rs, sl] = y
                s_scr[rs, sl] = u[rs] * y
        ds, dgs_rows = _rms_bwd(s_scr[...], gs_ref[...], dm_s_ref[...])
        _acc_rows(dgs_ref, dgs_rows, first)
        s_scr[...] = ds

        @pl.when(first)
        def _():
            dw_ref[...] = jnp.zeros_like(dw_ref)
            db_ref[...] = jnp.zeros_like(db_ref)

        for g in range(GROUPS):
            sl = slice(g * CH, (g + 1) * CH)
            upre = u_ref[:, sl]
            vpre = v_ref[:, sl]
            u = _gelu(upre)
            vhat, r = _group_norm_stats(_gelu(vpre))
            gain = vg_ref[:, sl]
            vn = (vhat * gain).astype(BF)
            dsg = s_scr[:, sl]
            duv_ref[:, sl] = (dsg * y_scr[:, sl] * _gelu_grad(upre)).astype(BF)
            dy = dsg * u
            dyb = dy.astype(BF)
            dvn_parts = []
            for ci in range(n_chunk):
                rs = slice(ci * CHUNK, (ci + 1) * CHUNK)
                dvn_parts.append(jnp.dot(wt_ref[g], dyb[rs], preferred_element_type=F32))
                dw_ref[g] += lax.dot_general(dyb[rs], vn[rs], (((1,), (1,)), ((), ())), preferred_element_type=F32)
                db_ref[:, sl] += jnp.broadcast_to(jnp.sum(dy[rs], axis=-1, keepdims=True), (CHUNK, CH))
            dvn = dvn_parts[0] if n_chunk == 1 else jnp.concatenate(dvn_parts, axis=0)
            _acc_rows(dvg_ref.at[:, sl], dvn * vhat, first)
            dvh = dvn * gain
            dvg = r * (dvh - jnp.mean(dvh, axis=-1, keepdims=True)
                       - vhat * jnp.mean(dvh * vhat, axis=-1, keepdims=True))
            duv_ref[:, SGU_OUT + g * CH:SGU_OUT + (g + 1) * CH] = (dvg * _gelu_grad(vpre)).astype(BF)

    return _pcall(
        body, name=name, grid=(T // tm,),
        ins=[(dmixed, (tm, MLA_OUT), lambda i: (i, 0)), (dmixed, (tm, SGU_OUT), lambda i: (i, 1)),
             (a, (tm, MLA_OUT), lambda i: (i, 0)), (proj, (tm, SGU_OUT), lambda i: (i, 1)),
             (proj, (tm, SGU_OUT), lambda i: (i, 2)), (g_mla, (1, MLA_OUT), lambda i: (0, 0)),
             (g_sgu, (1, SGU_OUT), lambda i: (0, 0)), (v_gain, (1, SGU_OUT), lambda i: (0, 0)),
             (w_tril, (GROUPS, CHUNK, CHUNK), lambda i: (0, 0, 0)), (w_tril_t, (GROUPS, CHUNK, CHUNK), lambda i: (0, 0, 0)),
             (b_full, (CHUNK, SGU_OUT), lambda i: (0, 0))],
        outs=[((T, MLA_OUT), BF, (tm, MLA_OUT), lambda i: (i, 0)),
              ((T, 2 * SGU_OUT), BF, (tm, 2 * SGU_OUT), lambda i: (i, 0)),
              ((1, MLA_OUT), F32, (1, MLA_OUT), lambda i: (0, 0)), ((1, SGU_OUT), F32, (1, SGU_OUT), lambda i: (0, 0)),
              ((1, SGU_OUT), F32, (1, SGU_OUT), lambda i: (0, 0)),
              ((GROUPS, CHUNK, CHUNK), F32, (GROUPS, CHUNK, CHUNK), lambda i: (0, 0, 0)),
              ((CHUNK, SGU_OUT), F32, (CHUNK, SGU_OUT), lambda i: (0, 0))],
        scratch=[pltpu.VMEM((tm, SGU_OUT), F32), pltpu.VMEM((tm, SGU_OUT), F32)], deps=deps)


def _shift_down(z, n, row):
    return jnp.where(row >= n, pltpu.roll(z, n, 0), 0.0)


def _shift_up(z, n, row, T):
    return jnp.where(row < T - n, pltpu.roll(z, T - n, 0), 0.0)


def conv_fwd(proj, conv_w, *, name):
    T, D3 = proj.shape
    D = D3 // 3
    tn = _tile(D, 256)
    nj = D // tn

    def body(b_ref, c_ref, x_ref, w_ref, o_ref):
        row = lax.broadcasted_iota(jnp.int32, (T, tn), 0)
        z = c_ref[...] * x_ref[...]
        zc = w_ref[2:3, :] * z + w_ref[1:2, :] * _shift_down(z, 1, row) + w_ref[0:1, :] * _shift_down(z, 2, row)
        o_ref[...] = (b_ref[...] * zc).astype(BF)

    return _pcall(
        body, name=name, grid=(nj,),
        ins=[(proj, (T, tn), lambda j: (0, j)), (proj, (T, tn), lambda j: (0, nj + j)),
             (proj, (T, tn), lambda j: (0, 2 * nj + j)), (conv_w, (3, tn), lambda j: (0, j))],
        outs=[((T, D), BF, (T, tn), lambda j: (0, j))], semantics=("parallel",))[0]


def conv_bwd(dg, proj, conv_w, *, name, deps=()):
    T, D3 = proj.shape
    D = D3 // 3
    tn = _tile(D, 256)
    nj = D // tn

    def body(dg_ref, b_ref, c_ref, x_ref, w_ref, dp_ref, dw_ref, dc_scr, dx_scr):
        part = pl.program_id(1)

        @pl.when(part == 0)
        def _():
            row = lax.broadcasted_iota(jnp.int32, (T, tn), 0)
            c, x = c_ref[...], x_ref[...]
            z = c * x
            z1 = _shift_down(z, 1, row)
            z2 = _shift_down(z, 2, row)
            dgv = dg_ref[...]
            zc = w_ref[2:3, :] * z + w_ref[1:2, :] * z1 + w_ref[0:1, :] * z2
            dp_ref[...] = (dgv * zc).astype(BF)
            dzc = dgv * b_ref[...]
            dw_ref[0:1, :] = jnp.sum(dzc * z2, axis=0, keepdims=True)
            dw_ref[1:2, :] = jnp.sum(dzc * z1, axis=0, keepdims=True)
            dw_ref[2:3, :] = jnp.sum(dzc * z, axis=0, keepdims=True)
            dz = (w_ref[2:3, :] * dzc + w_ref[1:2, :] * _shift_up(dzc, 1, row, T)
                  + w_ref[0:1, :] * _shift_up(dzc, 2, row, T))
            dc_scr[...] = (dz * x).astype(BF)
            dx_scr[...] = (dz * c).astype(BF)

        @pl.when(part == 1)
        def _():
            dp_ref[...] = dc_scr[...]

        @pl.when(part == 2)
        def _():
            dp_ref[...] = dx_scr[...]

    return _pcall(
        body, name=name, grid=(nj, 3),
        ins=[(dg, (T, tn), lambda j, p: (0, j)), (proj, (T, tn), lambda j, p: (0, j)),
             (proj, (T, tn), lambda j, p: (0, nj + j)), (proj, (T, tn), lambda j, p: (0, 2 * nj + j)),
             (conv_w, (3, tn), lambda j, p: (0, j))],
        outs=[((T, D3), BF, (T, tn), lambda j, p: (0, p * nj + j)), ((3, D), F32, (3, tn), lambda j, p: (0, j))],
        scratch=[pltpu.VMEM((T, tn), BF), pltpu.VMEM((T, tn), BF)], semantics=("parallel", "arbitrary"), deps=deps)


def loss_bwd(x_parts, gain, target, *, name):
    T, D = target.shape
    tm = _tile(T, ROW_TILE, 8)
    n_x = len(x_parts)

    def body(*refs):
        x_refs = refs[:n_x]
        g_ref, t_ref, dx_ref, dxb_ref, dg_ref, loss_ref = refs[n_x:]
        first = pl.program_id(0) == 0
        xv = jnp.concatenate([r[...] for r in x_refs], axis=-1) if n_x > 1 else x_refs[0][...]
        r = _rstd(xv)
        xh = xv * r
        gain_v = g_ref[...]
        err = xh * gain_v - t_ref[...]
        part = 0.5 * jnp.sum(jnp.mean(err * err, axis=-1, keepdims=True), axis=0, keepdims=True)
        _acc_rows(loss_ref, jnp.broadcast_to(part, (1, LANES)), first)
        dy = err * (1.0 / D)
        gdy = dy * gain_v
        dx = r * (gdy - xh * jnp.mean(gdy * xh, axis=-1, keepdims=True))
        dx_ref[...] = dx
        dxb_ref[...] = dx.astype(BF)
        _acc_rows(dg_ref, dy * xh, first)

    return _pcall(
        body, name=name, grid=(T // tm,),
        ins=[(p, (tm, D // n_x), lambda i: (i, 0)) for p in x_parts]
        + [(gain, (1, D), lambda i: (0, 0)), (target, (tm, D), lambda i: (i, 0))],
        outs=[((T, D), F32, (tm, D), lambda i: (i, 0)), ((T, D), BF, (tm, D), lambda i: (i, 0)),
              ((1, D), F32, (1, D), lambda i: (0, 0)), ((1, LANES), F32, (1, LANES), lambda i: (0, 0))])


def _adamw(g, w, m, v):
    m = ADAM_B1 * m + (1.0 - ADAM_B1) * g
    v = ADAM_B2 * v + (1.0 - ADAM_B2) * (g * g)
    m_hat = m / ADAM_C1
    v_hat = v / ADAM_C2
    delta = -ADAM_LR * (m_hat / (jnp.sqrt(v_hat) + ADAM_EPS) + ADAM_WD * w)
    return delta, m, v


def adam_flat(g, w, m, v, *, name):
    def body(g_ref, w_ref, m_ref, v_ref, d_ref, nm_ref, nv_ref):
        d, nm, nv = _adamw(g_ref[...], w_ref[...], m_ref[...], v_ref[...])
        d_ref[...] = d
        nm_ref[...] = nm
        nv_ref[...] = nv

    blk = g.shape
    zero = lambda: (0, 0)
    return _pcall(body, name=name, grid=(),
                  ins=[(t, blk, zero) for t in (g, w, m, v)],
                  outs=[(blk, F32, blk, zero)] * 3)


def _chip_slots():
    x, y, c = lax.axis_index("x"), lax.axis_index("y"), lax.axis_index("c")
    chips = [(1 - x, y), (x, 1 - y), (1 - x, 1 - y)]
    return x, y, c, chips


def device_index():
    x, y, c, chips = _chip_slots()
    return jnp.stack([4 * x + 2 * y + c, 2 * x + y] + [4 * cx + 2 * cy + c for cx, cy in chips]
                     + [2 * cx + cy for cx, cy in chips]).astype(jnp.int32)


def _job_rows(R, C, n_steps):
    if n_steps is None:
        n_steps = max(1, R * C // STREAM_BLOCK_ELEMS)
    n_blk = max([d for d in range(1, n_steps + 1) if R % d == 0 and (R // d) % 16 == 0] or [1])
    return R // n_blk, n_blk


def run_job(job, *, index, name, deps=()):
    jb = job(None)
    n_in = len(jb["ins"])

    def body(idx_ref, *refs):
        jb["fn"](refs[:n_in], refs[n_in:n_in + len(jb["outs"])])

    return _pcall(body, name=name, grid=(jb["n_blk"],), ins=jb["ins"], outs=jb["outs"], prefetch=index,
                  aliases={1 + a: o for a, o in jb["aliases"].items()}, semantics=("parallel",), deps=deps)


def adam_job(gs, a_buf, b_buf, w, m, v, layer, prev):
    L, R, C = w.shape

    def build(n_steps):
        tr, n_blk = _job_rows(R, C, n_steps)
        blk = (None, tr, C)
        row = lambda t: jnp.minimum(t, n_blk - 1)
        ins = [(gs, blk, lambda t, s: (s[0], row(t), 0)), (a_buf, blk, lambda t, s: (s[1], row(t), 0))]
        ins += [(b_buf, blk, lambda t, s, j=j: (j, row(t), 0)) for j in range(3)]
        ins += [(p, blk, lambda t, s: (layer, row(t), 0)) for p in (w, m, v)]
        ins += [(p, None, None) for p in (prev or [])]

        def fn(i, o):
            g = ((((i[0][...].astype(F32) + i[1][...].astype(F32)) + i[2][...].astype(F32))
                  + i[3][...].astype(F32)) + i[4][...].astype(F32))
            d, nm, nv = _adamw(g, i[5][...], i[6][...], i[7][...])
            o[0][...] = g
            o[1][...] = d
            o[2][...] = nm
            o[3][...] = nv

        return dict(ins=ins, outs=[((L, R, C), F32, blk, lambda t, s: (layer, row(t), 0))] * 4, fn=fn,
                    aliases={8 + o: o for o in range(4)} if prev else {}, n_blk=n_blk)

    return build


def pair_job(gs, a_buf):
    _, R, C = gs.shape

    def build(n_steps):
        tr, n_blk = _job_rows(R, C, n_steps)
        blk = (None, tr, C)
        row = lambda t: jnp.minimum(t, n_blk - 1)
        ins = [(gs, blk, lambda t, s, j=j: (s[2 + j], row(t), 0)) for j in range(3)]
        ins += [(a_buf, blk, lambda t, s, j=j: (s[5 + j], row(t), 0)) for j in range(3)]

        def fn(i, o):
            for j in range(3):
                o[0][j] = (i[j][...].astype(F32) + i[3 + j][...].astype(F32)).astype(BF)

        return dict(ins=ins, outs=[((3, R, C), BF, (3, tr, C), lambda t, s: (0, row(t), 0))], fn=fn, aliases={},
                    n_blk=n_blk)

    return build


def reduce_sum(gs, a_buf, b_buf, *, name):
    _, R, C = gs.shape
    tr = _tile(R, 256, 16)
    x, y, c, _ = _chip_slots()
    idx = jnp.stack([4 * x + 2 * y + c, 2 * x + y]).astype(jnp.int32)

    def body(idx_ref, g_ref, a_ref, b0_ref, b1_ref, b2_ref, o_ref):
        o_ref[...] = ((((g_ref[...].astype(F32) + a_ref[...].astype(F32)) + b0_ref[...].astype(F32))
                       + b1_ref[...].astype(F32)) + b2_ref[...].astype(F32))

    blk3 = (None, tr, C)
    return _pcall(body, name=name, grid=(R // tr,),
                  ins=[(gs, blk3, lambda i, s: (s[0], i, 0)), (a_buf, blk3, lambda i, s: (s[1], i, 0)),
                       (b_buf, blk3, lambda i, s: (0, i, 0)), (b_buf, blk3, lambda i, s: (1, i, 0)),
                       (b_buf, blk3, lambda i, s: (2, i, 0))],
                  outs=[((R, C), F32, (tr, C), lambda i, s: (i, 0))], prefetch=idx, semantics=("parallel",))[0]


def adam_rows(g, w, m, v, *, name):
    R, C = g.shape
    tr = _tile(R, 256, 8)

    def body(g_ref, w_ref, m_ref, v_ref, d_ref, nm_ref, nv_ref):
        d, nm, nv = _adamw(g_ref[...], w_ref[...], m_ref[...], v_ref[...])
        d_ref[...] = d
        nm_ref[...] = nm
        nv_ref[...] = nv

    spec = ((tr, C), lambda i: (i, 0))
    return _pcall(body, name=name, grid=(R // tr,), ins=[(t, *spec) for t in (g, w, m, v)],
                  outs=[((R, C), F32, *spec)] * 3, semantics=("parallel",))


def sum_rows8(gathered, rows, *, name):
    W = gathered.shape[1]

    def body(g_ref, o_ref):
        acc = g_ref[0:rows, :]
        for d in range(1, N_DEV):
            acc = acc + g_ref[d * rows:(d + 1) * rows, :]
        o_ref[...] = acc

    return _pcall(body, name=name, grid=(), ins=[(gathered, gathered.shape, lambda: (0, 0))],
                  outs=[((rows, W), F32, (rows, W), lambda: (0, 0))])[0]


HBM_SPEC = pl.BlockSpec(memory_space=pltpu.HBM)
SEM_SPEC = pl.BlockSpec(memory_space=pltpu.SEMAPHORE)
ANY_SPEC = pl.BlockSpec(memory_space=pl.ANY)
DATAFLOW = pltpu.SideEffectType.DATAFLOW_SIDE_EFFECTING


def _in_hbm(v):
    return pltpu.with_memory_space_constraint(v, pltpu.HBM)


def _slot(p):
    return 4 * p[0] + 2 * p[1] + p[2]


def _gather_peers():
    x, y, c, chips = _chip_slots()
    return (x, y, c), [(x, y, 1 - c)] + [(*chip, c) for chip in chips]


def gather_start(groups, after, *, name):
    flat = [s for g in groups for s in g]
    n, n_g = len(flat), len(groups)
    where = [(gi, ti) for gi, g in enumerate(groups) for ti in range(len(g))]

    def body(*refs):
        src, land = refs[:n], refs[n:2 * n]
        sems = refs[2 * n + 1:2 * n + 1 + 2 * n_g]
        me, peers = _gather_peers()
        for t in range(n):
            gi, ti = where[t]
            for k, to in enumerate(peers):
                pltpu.make_async_remote_copy(
                    src_ref=src[t], dst_ref=land[t].at[_slot(me)], send_sem=sems[2 * gi].at[4 * ti + k],
                    recv_sem=sems[2 * gi + 1].at[4 * ti + k], device_id=to, device_id_type=MESH).start()
        refs[-1][...] = jnp.zeros_like(refs[-1])

    out_shape = []
    for g in groups:
        out_shape += [pltpu.SemaphoreType.DMA((4 * len(g),)), pltpu.SemaphoreType.DMA((4 * len(g),))]
    out_shape += [pltpu.HBM(s.shape, s.dtype) for s in flat]
    out_shape += [pltpu.HBM((N_DEV,) + s.shape, s.dtype) for s in flat]
    out_shape += [jax.ShapeDtypeStruct((8, LANES), F32)]
    aliases = {t: 2 * n_g + t for t in range(n)}
    aliases.update({n + t: 2 * n_g + n + t for t in range(n)})
    res = pl.pallas_call(
        body, name=name, out_shape=out_shape, in_specs=[HBM_SPEC] * (2 * n) + [ANY_SPEC],
        out_specs=[SEM_SPEC] * (2 * n_g) + [HBM_SPEC] * (2 * n) + [pl.BlockSpec(memory_space=pltpu.VMEM)],
        input_output_aliases=aliases, compiler_params=pltpu.CompilerParams(has_side_effects=DATAFLOW),
    )(*[_in_hbm(s) for s in flat], *[_in_hbm(lax.empty((N_DEV,) + s.shape, s.dtype)) for s in flat], after)
    out, off = [], 0
    for gi, g in enumerate(groups):
        k = len(g)
        out.append((res[2 * gi], res[2 * gi + 1], res[2 * n_g + off:2 * n_g + off + k],
                    res[2 * n_g + n + off:2 * n_g + n + off + k]))
        off += k
    return out, res[-1]


def gather_wait(started, after, *, name):
    send_sems, recv_sems, srcs, lands = started
    n = len(srcs)
    after = list(after)

    def body(*refs):
        src, land = refs[:n], refs[n:2 * n]
        send, recv = refs[2 * n], refs[2 * n + 1]
        _, peers = _gather_peers()
        for t in range(n):
            for k, frm in enumerate(peers):
                cp = pltpu.make_async_remote_copy(
                    src_ref=src[t], dst_ref=land[t].at[_slot(frm)], send_sem=send.at[4 * t + k],
                    recv_sem=recv.at[4 * t + k],
                    device_id=frm, device_id_type=MESH)
                cp.wait_send()
                cp.wait_recv()

    res = pl.pallas_call(
        body, name=name,
        out_shape=[pltpu.HBM(s.shape, s.dtype) for s in srcs] + [pltpu.HBM(l.shape, l.dtype) for l in lands],
        in_specs=[HBM_SPEC] * (2 * n) + [SEM_SPEC, SEM_SPEC] + [ANY_SPEC] * len(after),
        out_specs=[HBM_SPEC] * (2 * n), input_output_aliases={t: t for t in range(2 * n)},
        compiler_params=pltpu.CompilerParams(has_side_effects=DATAFLOW),
    )(*srcs, *lands, send_sems, recv_sems, *after)
    return res[:n], res[n:]


def place_own(src, land, *, name):
    R, C = src.shape
    tr = _tile(R, 512, 16)
    x, y, c, _ = _chip_slots()
    idx = jnp.stack([4 * x + 2 * y + c]).astype(jnp.int32)

    def body(idx_ref, s_ref, land_ref, o_ref):
        o_ref[...] = s_ref[...]

    return _pcall(body, name=name, grid=(R // tr,),
                  ins=[(src, (tr, C), lambda i, s: (i, 0)), (land, None, None)],
                  outs=[(land.shape, land.dtype, (None, tr, C), lambda i, s: (s[0], i, 0))],
                  prefetch=idx, aliases={2: 0}, semantics=("parallel",))[0]


def gather_finish(srcs, lands, *, name):
    n = len(srcs)

    def body(*refs):
        land = refs[n:2 * n]
        send_sems, recv_sems = refs[2 * n:]
        x, y, c, chips = _chip_slots()
        me, sibling = (x, y, c), (x, y, 1 - c)

        def copy(t, j, block, to):
            return pltpu.make_async_remote_copy(
                src_ref=land[t].at[_slot(block)], dst_ref=land[t].at[_slot(block)], send_sem=send_sems.at[t, j],
                recv_sem=recv_sems.at[t, j], device_id=to, device_id_type=MESH)

        sends = [copy(t, j, (*chip, c), sibling) for t in range(n) for j, chip in enumerate(chips)]
        for cp in sends:
            cp.start()
        for t in range(n):
            for j, chip in enumerate(chips):
                copy(t, j, (*chip, 1 - c), me).wait_recv()
        for cp in sends:
            cp.wait_send()

    passed = pl.pallas_call(
        body, name=name, out_shape=[jax.ShapeDtypeStruct(l.shape, l.dtype) for l in lands],
        in_specs=[ANY_SPEC] * n, out_specs=[ANY_SPEC] * n,
        input_output_aliases={t: t for t in range(n)},
        scratch_shapes=[pltpu.SemaphoreType.DMA((n, 3)), pltpu.SemaphoreType.DMA((n, 3))],
    )(*lands)
    return [place_own(s, l, name=f"{name}_own{t}") for t, (s, l) in enumerate(zip(srcs, passed))]


def chips_start(pairs, *, name):
    n = len(pairs)

    def body(*refs):
        src, land = refs[:n], refs[n:2 * n]
        send, recv = refs[2 * n], refs[2 * n + 1]
        token = refs[-1]
        x, y, c, chips = _chip_slots()
        for t in range(n):
            for j, chip in enumerate(chips):
                pltpu.make_async_remote_copy(
                    src_ref=src[t].at[j], dst_ref=land[t].at[j], send_sem=send.at[3 * t + j],
                    recv_sem=recv.at[3 * t + j], device_id=(*chip, c), device_id_type=MESH).start()
        token[...] = jnp.zeros_like(token)

    res = pl.pallas_call(
        body, name=name,
        out_shape=[pltpu.SemaphoreType.DMA((3 * n,)), pltpu.SemaphoreType.DMA((3 * n,))]
        + [pltpu.HBM(p.shape, p.dtype) for p in pairs] * 2 + [jax.ShapeDtypeStruct((8, LANES), F32)],
        in_specs=[HBM_SPEC] * (2 * n),
        out_specs=[SEM_SPEC, SEM_SPEC] + [HBM_SPEC] * (2 * n) + [pl.BlockSpec(memory_space=pltpu.VMEM)],
        input_output_aliases={t: 2 + t for t in range(2 * n)},
        compiler_params=pltpu.CompilerParams(has_side_effects=DATAFLOW),
    )(*[_in_hbm(p) for p in pairs], *[_in_hbm(lax.empty(p.shape, p.dtype)) for p in pairs])
    return res[0], res[1], res[2:2 + n], res[2 + n:2 + 2 * n], res[-1]


def chips_wait(started, after, *, name):
    send_sems, recv_sems, srcs, lands, _ = started
    n = len(srcs)

    def body(*refs):
        src, land = refs[:n], refs[n:2 * n]
        send, recv = refs[2 * n], refs[2 * n + 1]
        x, y, c, chips = _chip_slots()
        for t in range(n):
            for j, chip in enumerate(chips):
                cp = pltpu.make_async_remote_copy(
                    src_ref=src[t].at[j], dst_ref=land[t].at[j], send_sem=send.at[3 * t + j],
                    recv_sem=recv.at[3 * t + j], device_id=(*chip, c), device_id_type=MESH)
                cp.wait_send()
                cp.wait_recv()

    res = pl.pallas_call(
        body, name=name, out_shape=[pltpu.HBM(s.shape, s.dtype) for s in srcs] * 2,
        in_specs=[HBM_SPEC] * (2 * n) + [SEM_SPEC, SEM_SPEC, ANY_SPEC], out_specs=[HBM_SPEC] * (2 * n),
        input_output_aliases={t: t for t in range(2 * n)},
        compiler_params=pltpu.CompilerParams(has_side_effects=DATAFLOW),
    )(*srcs, *lands, send_sems, recv_sems, after)
    return res[n:]


def _sibling_copies(src, land, send, recv, n):
    x, y, c, _ = _chip_slots()
    return [pltpu.make_async_remote_copy(
        src_ref=src[t].at[4 * (q // 2) + 2 * (q % 2) + (1 - c)], dst_ref=land[t].at[q], send_sem=send.at[4 * t + q],
        recv_sem=recv.at[4 * t + q], device_id=(x, y, 1 - c), device_id_type=MESH)
        for t in range(n) for q in range(4)]


def sibling_start(gs, *, name):
    n = len(gs)

    def body(*refs):
        for cp in _sibling_copies(refs[:n], refs[n:2 * n], refs[2 * n], refs[2 * n + 1], n):
            cp.start()
        refs[-1][...] = jnp.zeros_like(refs[-1])

    lands = [lax.empty((4,) + g.shape[1:], g.dtype) for g in gs]
    res = pl.pallas_call(
        body, name=name,
        out_shape=[pltpu.SemaphoreType.DMA((4 * n,)), pltpu.SemaphoreType.DMA((4 * n,))]
        + [pltpu.HBM(g.shape, g.dtype) for g in gs] + [pltpu.HBM(l.shape, l.dtype) for l in lands]
        + [jax.ShapeDtypeStruct((8, LANES), F32)],
        in_specs=[HBM_SPEC] * (2 * n),
        out_specs=[SEM_SPEC, SEM_SPEC] + [HBM_SPEC] * (2 * n) + [pl.BlockSpec(memory_space=pltpu.VMEM)],
        input_output_aliases={t: 2 + t for t in range(2 * n)},
        compiler_params=pltpu.CompilerParams(has_side_effects=DATAFLOW),
    )(*[_in_hbm(g) for g in gs], *[_in_hbm(l) for l in lands])
    return res[0], res[1], res[2:2 + n], res[2 + n:2 + 2 * n], res[-1]


def sibling_wait(started, after, *, name):
    send_sems, recv_sems, srcs, lands, _ = started
    n = len(srcs)

    def body(*refs):
        for cp in _sibling_copies(refs[:n], refs[n:2 * n], refs[2 * n], refs[2 * n + 1], n):
            cp.wait_send()
            cp.wait_recv()

    res = pl.pallas_call(
        body, name=name,
        out_shape=[pltpu.HBM(s.shape, s.dtype) for s in srcs] + [pltpu.HBM(l.shape, l.dtype) for l in lands],
        in_specs=[HBM_SPEC] * (2 * n) + [SEM_SPEC, SEM_SPEC, ANY_SPEC], out_specs=[HBM_SPEC] * (2 * n),
        input_output_aliases={t: t for t in range(2 * n)},
        compiler_params=pltpu.CompilerParams(has_side_effects=DATAFLOW),
    )(*srcs, *lands, send_sems, recv_sems, after)
    return res[:n], res[n:]


def all_gather_vmem(x_shard, *, name, after=None):
    m_per, n = x_shard.shape
    n_after = 0 if after is None else 1

    def body(x_ref, *rest):
        out_ref, send_sems, recv_sems, local_sem = rest[n_after:]
        x, y, c, chips = _chip_slots()
        me, sibling = (x, y, c), (x, y, 1 - c)

        def rows(px, py, pc):
            return out_ref.at[pl.ds((4 * px + 2 * py + pc) * m_per, m_per), :]

        def copy(k, block, to, src=None):
            return pltpu.make_async_remote_copy(
                src_ref=rows(*block) if src is None else src, dst_ref=rows(*block),
                send_sem=send_sems.at[k], recv_sem=recv_sems.at[k], device_id=to, device_id_type=MESH)

        mine = pltpu.make_async_copy(x_ref, rows(*me), local_sem)
        mine.start()
        first = [copy(0, me, sibling, src=x_ref)]
        first += [copy(1 + j, me, (*chip, c), src=x_ref) for j, chip in enumerate(chips)]
        for cp in first:
            cp.start()
        passed = [copy(4 + j, (*chip, c), sibling) for j, chip in enumerate(chips)]
        for j, chip in enumerate(chips):
            copy(1 + j, (*chip, c), me).wait_recv()
            passed[j].start()
        copy(0, sibling, me).wait_recv()
        for j, chip in enumerate(chips):
            copy(4 + j, (*chip, 1 - c), me).wait_recv()
        for cp in first + passed:
            cp.wait_send()
        mine.wait()

    vmem = pl.BlockSpec(memory_space=pltpu.VMEM)
    return pl.pallas_call(
        body, name=name, out_shape=jax.ShapeDtypeStruct((N_DEV * m_per, n), x_shard.dtype),
        in_specs=[vmem] + [ANY_SPEC] * n_after, out_specs=vmem,
        scratch_shapes=[pltpu.SemaphoreType.DMA((7,)), pltpu.SemaphoreType.DMA((7,)), pltpu.SemaphoreType.DMA],
        compiler_params=pltpu.CompilerParams(vmem_limit_bytes=int(min(
            VMEM_LIMIT_CAP, 2 * (N_DEV + 1) * m_per * n * x_shard.dtype.itemsize + 16 * 2 ** 20))),
    )(x_shard, *([] if after is None else [after]))


def _rope_slab(cols):
    z = jnp.zeros(cols.shape[:-1] + (HALF_ROPE,), cols.dtype)
    return jnp.concatenate([cols[..., :HALF_ROPE], z, cols[..., HALF_ROPE:], z], axis=-1)


def _rope_unslab(slab):
    return jnp.concatenate([slab[..., :HALF_ROPE], slab[..., 2 * HALF_ROPE:3 * HALF_ROPE]], axis=-1)


def _pack_w_in_t(wt_g):
    s, c, d = wt_g.shape
    w = wt_g.reshape(s * c, d)
    c2, c3 = Q_LORA + KV_LORA, Q_LORA + KV_LORA + QK_ROPE
    r = w[c2:c3]
    z = jnp.zeros((HALF_ROPE, d), w.dtype)
    return jnp.concatenate([w[:c2], w[c3:], r[:HALF_ROPE], z, r[HALF_ROPE:], z], axis=0)


def _unpack_w_in_t_grad(dwt):
    d = dwt.shape[1]
    c2 = Q_LORA + KV_LORA
    uv = 2 * SGU_OUT
    slab = dwt[c2 + uv:]
    g = jnp.concatenate([dwt[:c2], slab[:HALF_ROPE], slab[2 * HALF_ROPE:3 * HALF_ROPE], dwt[c2:c2 + uv]], axis=0)
    return g.reshape(N_DEV, g.shape[0] // N_DEV, d)


def _rope_tables(positions):
    inv_freq = ROPE_BASE ** (-jnp.arange(0, QK_ROPE, 2, dtype=F32) / QK_ROPE)
    ang = positions.astype(F32)[:, None] * inv_freq
    cos, sin = jnp.cos(ang), jnp.sin(ang)
    z = jnp.zeros_like(cos)
    return jnp.concatenate([cos, z, cos, z], axis=-1), jnp.concatenate([-sin, z, sin, z], axis=-1)


def _mlp_up(x, gain, w1, tag):
    hn = rms_fwd(x, gain, name=f"mlp{tag}_norm")

    def act_epi(acc):
        a = jnp.maximum(acc, 0.0)
        return a, a * a

    T = x.shape[0]
    F = w1.shape[0] * w1.shape[2]
    a, act = mm(hn, w1, name=f"mlp{tag}_up", outs=[((T, F), BF, None), ((T, F), BF, None)], epi=act_epi)
    return hn, a, act


def _mlp_down(x, act, w2, tag, part=0):
    n = w2.shape[1]
    bm = _tile(x.shape[0], MM_TILE)
    bn = _tile(n, MM_TILE)
    per = n // bn
    return mm(act, w2, name=f"mlp{tag}_down{part}", out=((x.shape[0], n), F32), bm=bm, bn=bn,
              epi=lambda acc, r: (acc + r[...],), epi_ins=[(x, (bm, bn), lambda i, j, k: (i, part * per + j))])


def _mlp_bwd_weights(w1, w2, saved, dxb, tag):
    hn, a, act = saved
    T, D = dxb.shape
    F = a.shape[1]
    bm = _tile(T, MM_TILE)
    bn = _tile(F, min(MM_TILE, w1.shape[2]))
    dhid = mm(dxb, w2, tb=True, name=f"mlp{tag}_dhid", out=((T, F), BF), bm=bm, bn=bn,
              epi=lambda acc, a_ref: (2.0 * a_ref[...].astype(F32) * acc,),
              epi_ins=[(a, (bm, bn), lambda i, j, k: (i, j))])
    dw2 = mm(act, dxb, ta=True, name=f"mlp{tag}_dw2", out=((F, D), BF))
    dw1 = mm(hn, dhid, ta=True, name=f"mlp{tag}_dw1", out=(w1.shape, BF))
    return dhid, dw1, dw2.reshape(N_DEV, F // N_DEV, D)


def _reduce_begin(grads, tag):
    return sibling_start(grads, name=f"reduce_sibling_start_{tag}")


def _reduce_continue(sib, after, tag, index):
    grads, a_bufs = sibling_wait(sib, after, name=f"reduce_sibling_wait_{tag}")
    pairs = [run_job(pair_job(g, a), index=index, name=f"pair_sum_{tag}{t}")[0]
             for t, (g, a) in enumerate(zip(grads, a_bufs))]
    return grads, a_bufs, chips_start(pairs, name=f"reduce_chips_start_{tag}")


def kernel(x, positions, e_norm_mix, e_w_in, e_q_norm, e_w_uq, e_kv_norm, e_w_ukv, e_v_norm, e_sgu_w, e_sgu_b, e_mla_out_norm, e_sgu_out_norm, e_w_out, o_norm_mix, o_w_in, o_conv_w, o_w_out, mlp_norm, mlp_w1, mlp_w2, final_norm, loss_target, m_e_norm_mix, m_e_w_in, m_e_q_norm, m_e_w_uq, m_e_kv_norm, m_e_w_ukv, m_e_v_norm, m_e_sgu_w, m_e_sgu_b, m_e_mla_out_norm, m_e_sgu_out_norm, m_e_w_out, m_o_norm_mix, m_o_w_in, m_o_conv_w, m_o_w_out, m_mlp_norm, m_mlp_w1, m_mlp_w2, m_final_norm, v_e_norm_mix, v_e_w_in, v_e_q_norm, v_e_w_uq, v_e_kv_norm, v_e_w_ukv, v_e_v_norm, v_e_sgu_w, v_e_sgu_b, v_e_mla_out_norm, v_e_sgu_out_norm, v_e_w_out, v_o_norm_mix, v_o_w_in, v_o_conv_w, v_o_w_out, v_mlp_norm, v_mlp_w1, v_mlp_w2, v_final_norm):
    T, D = x.shape[1], x.shape[2]
    d_shard = o_norm_mix.shape[1]
    x0 = x[0]
    target = loss_target[0]
    me = 4 * lax.axis_index("x") + 2 * lax.axis_index("y") + lax.axis_index("c")

    bf = lambda s: s.astype(BF)
    gather_groups = [[bf(jnp.transpose(e_w_in[0])), bf(e_w_uq[0]), bf(e_w_ukv[0])], [bf(e_w_out[0]), bf(mlp_w1[0])],
                     [bf(mlp_w2[0]), bf(o_w_in[0])], [bf(o_w_out[0]), bf(mlp_w1[1])], [bf(mlp_w2[1])]]
    small_rows = jnp.concatenate([o_norm_mix, o_conv_w[0], jnp.zeros((4, d_shard), F32)], axis=0)
    small_flat = all_gather_vmem(small_rows, name="gather_small")
    started, start_token = gather_start(gather_groups[:1], small_flat, name="gather_start0")
    started_rest, rest_token = gather_start(gather_groups[1:], start_token, name="gather_start1")
    started += started_rest

    def gathered(gi, after):
        srcs, lands = gather_wait(started[gi], after, name=f"gather_wait{gi}")
        return gather_finish(srcs, lands, name=f"gather_finish{gi}")

    small_g = small_flat.reshape(N_DEV, 8, d_shard)
    o_norm_full = small_g[:, 0, :].reshape(1, D)
    conv_w_full = jnp.transpose(small_g[:, 1:4, :], (1, 0, 2)).reshape(3, D)
    w_tril = jnp.tril(e_sgu_w[0])
    w_tril_b = w_tril.astype(BF)
    w_tril_tb = jnp.swapaxes(w_tril, 1, 2).astype(BF)
    b_full = jnp.repeat(e_sgu_b[0].T, CH, axis=1)
    v_gain = e_v_norm[0].reshape(1, SGU_OUT)
    cos_t, sin_t = _rope_tables(positions[0])
    mlp_gain = [mlp_norm[0:1], mlp_norm[1:2]]
    final_gain = final_norm.reshape(1, D)

    h0 = rms_fwd(x0, e_norm_mix, name="e_norm", deps=[rest_token])
    g_w_in_t, g_w_uq, w_ukv = gathered(
        0, [h0, cos_t, sin_t, w_tril_b, w_tril_tb, b_full, o_norm_full, conv_w_full])
    w_in_t = _pack_w_in_t(g_w_in_t)
    w_uq = jnp.concatenate([g_w_uq[..., :QK_NOPE], _rope_slab(g_w_uq[..., QK_NOPE:])], axis=-1)
    proj = mm(h0, w_in_t, tb=True, name="e_in", out=((T, w_in_t.shape[0]), F32), bn=_tile(w_in_t.shape[0], 640))
    qn, kvn, krope = mla_prep(proj, e_q_norm, e_kv_norm, cos_t, sin_t, name="mla_prep")
    bm = _tile(T, MM_TILE)

    def q_epi(acc, cos_ref, sin_ref):
        return (jnp.concatenate([acc[:, :QK_NOPE], _rope_fwd(acc[:, QK_NOPE:], cos_ref[...], sin_ref[...])], axis=-1),)

    q = mm(qn, w_uq, name="mla_q", out=((T, HEADS * HEAD_PAD), BF), bm=bm, bn=HEAD_PAD, epi=q_epi,
           epi_ins=[(cos_t, (bm, LANES), lambda i, j, k: (i, 0)), (sin_t, (bm, LANES), lambda i, j, k: (i, 0))])

    def kv_epi(acc, kr_ref):
        return jnp.concatenate([acc[:, :QK_NOPE].astype(BF), kr_ref[...]], axis=-1), acc[:, QK_NOPE:]

    k, v = mm(kvn, w_ukv, name="mla_kv", bm=bm, bn=HEAD_PAD, epi=kv_epi,
              outs=[((T, HEADS * HEAD_PAD), BF, HEAD_PAD), ((T, MLA_OUT), BF, V_HEAD)],
              epi_ins=[(krope, (bm, LANES), lambda i, j, k: (i, 0))])
    attn, attn_lse = attn_fwd(q, k, v, name="attn_fwd")
    mixed = mix_fwd(attn, proj, e_mla_out_norm, e_sgu_out_norm, v_gain, w_tril_b, b_full, name="mix_fwd")
    bn = _tile(D, MM_TILE)
    g_w_out_e, w1_0 = gathered(1, [mixed])
    w_out_e = g_w_out_e.reshape(-1, D)
    x1 = mm(mixed, w_out_e, name="e_out", out=((T, D), F32), bm=bm, bn=bn,
            epi=lambda acc, r: (acc + r[...],), epi_ins=[(x0, (bm, bn), lambda i, j, k: (i, j))])
    hn0, a0, act0 = _mlp_up(x1, mlp_gain[0], w1_0, 0)
    g_w2_0, g_w_in_o = gathered(2, [act0])
    w2_0 = g_w2_0.reshape(-1, D)
    x2 = _mlp_down(x1, act0, w2_0, 0)
    ho = rms_fwd(x2, o_norm_full, name="o_norm")
    proj_o = mm(ho, g_w_in_o, name="o_in", out=((T, 3 * D), F32))
    gated = conv_fwd(proj_o, conv_w_full, name="conv_fwd")
    g_w_out_o, w1_1 = gathered(3, [gated])
    w_out_o = g_w_out_o.reshape(-1, D)
    x3 = mm(gated, w_out_o, name="o_out", out=((T, D), F32), bm=bm, bn=bn,
            epi=lambda acc, r: (acc + r[...],), epi_ins=[(x2, (bm, bn), lambda i, j, k: (i, j))])
    hn1, a1, act1 = _mlp_up(x3, mlp_gain[1], w1_1, 1)
    (g_w2_1,) = gathered(4, [act1])
    w2_1 = g_w2_1.reshape(-1, D)
    x4 = _mlp_down(x3, act1, w2_1, 1)
    w1, w2 = [w1_0, w1_1], [w2_0, w2_1]

    dx4, dx4b, d_final, loss_part = loss_bwd([x4], final_gain, target, name="loss_bwd")

    hosted = dict(job_index=device_index())
    ht = _tile(D, HOST_TILE)
    dhid1, dw1_1, dw2_1 = _mlp_bwd_weights(w1[1], w2[1], (hn1, a1, act1), dx4b, 1)
    sib_r0 = _reduce_begin([dw1_1, dw2_1], "r0")
    dhn1 = mm(dhid1, w1[1], tb=True, name="mlp1_dhn", out=((T, D), F32), deps=[sib_r0[-1]])
    grads_r0, a_r0 = sibling_wait(sib_r0, dhn1, name="reduce_sibling_wait_r0")
    dx3, dx3b, d_mlp1 = rms_bwd(x3, mlp_gain[1], dhn1, dres=dx4, name="mlp1_norm_bwd")

    dgated, ((pair_r0a,),) = mm(dx3b, w_out_o, tb=True, name="o_out_dx", out=((T, D), F32),
                                jobs=[pair_job(grads_r0[0], a_r0[0])], bm=ht, bn=ht, **hosted)
    dw_out_o, ((pair_r0b,),) = mm(gated, dx3b, ta=True, name="o_out_dw", out=((D, D), BF),
                                  jobs=[pair_job(grads_r0[1], a_r0[1])], bm=ht, bn=ht, **hosted)
    st_r0 = chips_start([pair_r0a, pair_r0b], name="reduce_chips_start_r0")
    dproj_o, dconv_full = conv_bwd(dgated, proj_o, conv_w_full, name="conv_bwd", deps=[st_r0[-1]])
    dw_in_o = mm(ho, dproj_o, ta=True, name="o_in_dw", out=(g_w_in_o.shape, BF))
    sib_r1 = _reduce_begin([dw_out_o.reshape(g_w_out_o.shape), dw_in_o], "r1")
    dho = mm(dproj_o, g_w_in_o, tb=True, name="o_in_dx", out=((T, D), F32), deps=[sib_r1[-1]])
    grads_r1, a_r1 = sibling_wait(sib_r1, dho, name="reduce_sibling_wait_r1")
    dx2, dx2b, d_onorm_full = rms_bwd(x2, o_norm_full, dho, dres=dx3, name="o_norm_bwd")

    d_ff = a0.shape[1]
    bm_h, bn_h = _tile(T, MM_TILE), _tile(d_ff, min(MM_TILE, w1[0].shape[2]))
    dhid0, ((pair_r1a,), (pair_r1b,)) = mm(
        dx2b, w2[0], tb=True, name="mlp0_dhid", out=((T, d_ff), BF), bm=bm_h, bn=bn_h,
        epi=lambda acc, a_ref: (2.0 * a_ref[...].astype(F32) * acc,),
        epi_ins=[(a0, (bm_h, bn_h), lambda i, j, k: (i, j))],
        jobs=[pair_job(grads_r1[0], a_r1[0]), pair_job(grads_r1[1], a_r1[1])], **hosted)
    st_r1 = chips_start([pair_r1a, pair_r1b], name="reduce_chips_start_r1")
    dw2_0 = mm(act0, dx2b, ta=True, name="mlp0_dw2", out=((d_ff, D), BF), deps=[st_r1[-1]])
    b_r0 = chips_wait(st_r0, dw2_0, name="reduce_chips_wait_r0")
    dw1_0, (r_w1, r_w2) = mm(
        hn0, dhid0, ta=True, name="mlp0_dw1", out=(w1[0].shape, BF),
        jobs=[adam_job(grads_r0[0], a_r0[0], b_r0[0], mlp_w1, m_mlp_w1, v_mlp_w1, 1, None),
              adam_job(grads_r0[1], a_r0[1], b_r0[1], mlp_w2, m_mlp_w2, v_mlp_w2, 1, None)], **hosted)
    sib_r2 = _reduce_begin([dw1_0, dw2_0.reshape(N_DEV, d_ff // N_DEV, D)], "r2")
    dhn0 = mm(dhid0, w1[0], tb=True, name="mlp0_dhn", out=((T, D), F32), deps=[sib_r2[-1]])
    grads_r2, a_r2 = sibling_wait(sib_r2, dhn0, name="reduce_sibling_wait_r2")
    dx1, dx1b, d_mlp0 = rms_bwd(x1, mlp_gain[0], dhn0, dres=dx2, name="mlp0_norm_bwd")

    dmixed, ((pair_r2a,),) = mm(dx1b, w_out_e, tb=True, name="e_out_dx", out=((T, MLA_OUT + SGU_OUT), F32),
                                jobs=[pair_job(grads_r2[0], a_r2[0])], bm=ht, bn=ht, **hosted)
    dw_out_e, ((pair_r2b,),) = mm(mixed, dx1b, ta=True, name="e_out_dw", out=(w_out_e.shape, BF),
                                  jobs=[pair_job(grads_r2[1], a_r2[1])], bm=ht, bn=ht, **hosted)
    st_r2 = chips_start([pair_r2a, pair_r2b], name="reduce_chips_start_r2")
    (dattn, duv, d_mla_out, d_sgu_out, d_vgain, d_sgu_w, d_b_full) = mix_bwd(
        dmixed, attn, proj, e_mla_out_norm, e_sgu_out_norm, v_gain, w_tril_b, w_tril_tb, b_full, name="mix_bwd",
        deps=[st_r2[-1]])
    b_r1 = chips_wait(st_r1, dattn, name="reduce_chips_wait_r1")
    dq, dk, dv = attn_bwd(q, k, v, attn, attn_lse, dattn, name="attn_bwd")
    dq_lin, dkv_lin, dkr = mla_bwd_prep(dq, dk, dv, cos_t, sin_t, name="mla_bwd_prep")
    dw_uq_pad = mm(qn, dq_lin, ta=True, name="mla_q_dw", out=(w_uq.shape, BF))
    dw_ukv = mm(kvn, dkv_lin, ta=True, name="mla_kv_dw", out=(w_ukv.shape, BF))
    dw_uq = jnp.concatenate([dw_uq_pad[..., :QK_NOPE], _rope_unslab(dw_uq_pad[..., QK_NOPE:])], axis=-1)
    sib_r2b = _reduce_begin([dw_out_e.reshape(g_w_out_e.shape), dw_uq, dw_ukv], "r2b")
    dqn = mm(dq_lin, w_uq, tb=True, name="mla_q_dx", out=((T, Q_LORA), F32), deps=[sib_r2b[-1]])
    dkvn = mm(dkv_lin, w_ukv, tb=True, name="mla_kv_dx", out=((T, KV_LORA), F32), deps=[sib_r2b[-1]])
    grads_r2b, a_r2b, st_r2b = _reduce_continue(sib_r2b, dkvn, "r2b", hosted["job_index"])
    dcq, d_qnorm = rms_bwd(proj, e_q_norm, dqn, col_block=0, want_f32=False, name="q_norm_bwd", deps=[st_r2b[-1]])
    dckv, d_kvnorm = rms_bwd(proj, e_kv_norm, dkvn, col_block=1, want_f32=False, name="kv_norm_bwd")
    dproj = jnp.concatenate([dcq, dckv, duv, dkr], axis=-1)
    dw_in_t_pad, (r_w_out_o, r_w_in_o) = mm(
        dproj, h0, ta=True, name="e_in_dw", out=(w_in_t.shape, BF), bm=_tile(w_in_t.shape[0], 640),
        jobs=[adam_job(grads_r1[0], a_r1[0], b_r1[0], o_w_out, m_o_w_out, v_o_w_out, 0, None),
              adam_job(grads_r1[1], a_r1[1], b_r1[1], o_w_in, m_o_w_in, v_o_w_in, 0, None)], **hosted)
    dw_in_t = _unpack_w_in_t_grad(dw_in_t_pad)
    sib_r3 = _reduce_begin([dw_in_t], "r3")
    dh0 = mm(dproj, w_in_t, name="e_in_dx", out=((T, D), F32), deps=[sib_r3[-1]])
    grads_r3, a_r3, st_r3 = _reduce_continue(sib_r3, dh0, "r3", hosted["job_index"])
    tok_r3 = st_r3[-1]
    grad_x, d_enorm = rms_bwd(x0, e_norm_mix, dh0, dres=dx1, want_bf=False, name="e_norm_bwd", deps=[tok_r3])
    b_r2 = chips_wait(st_r2, grad_x, name="reduce_chips_wait_r2")

    def finish(grads, a_bufs, b_bufs, t, w, m, v, layer=0, prev=None, tag="", deps=()):
        return run_job(adam_job(grads[t], a_bufs[t], b_bufs[t], w, m, v, layer, prev), index=hosted["job_index"],
                       name=f"adam_{tag}", deps=deps)

    r_w1 = finish(grads_r2, a_r2, b_r2, 0, mlp_w1, m_mlp_w1, v_mlp_w1, 0, r_w1, tag="w1_l0", deps=[tok_r3])
    r_w2 = finish(grads_r2, a_r2, b_r2, 1, mlp_w2, m_mlp_w2, v_mlp_w2, 0, r_w2, tag="w2_l0", deps=[r_w1[1]])
    b_r2b = chips_wait(st_r2b, r_w2[1], name="reduce_chips_wait_r2b")
    r_w_out_e = finish(grads_r2b, a_r2b, b_r2b, 0, e_w_out, m_e_w_out, v_e_w_out, tag="e_w_out")
    r_w_uq = finish(grads_r2b, a_r2b, b_r2b, 1, e_w_uq, m_e_w_uq, v_e_w_uq, tag="e_w_uq")
    r_w_ukv = finish(grads_r2b, a_r2b, b_r2b, 2, e_w_ukv, m_e_w_ukv, v_e_w_ukv, tag="e_w_ukv")
    b_r3 = chips_wait(st_r3, r_w_out_e[1], name="reduce_chips_wait_r3")
    g_w_in_t = reduce_sum(grads_r3[0], a_r3[0], b_r3[0], name="sum_e_w_in")
    w_in_upd_t = adam_rows(g_w_in_t, jnp.transpose(e_w_in[0]), jnp.transpose(m_e_w_in[0]), jnp.transpose(v_e_w_in[0]),
                           name="adam_e_w_in")
    r_w_in = [jnp.transpose(t)[None] for t in (g_w_in_t, *w_in_upd_t)]

    d_sgu_b = jnp.transpose(d_b_full[:, ::CH])
    d_sgu_w_tril = jnp.tril(d_sgu_w)
    rep = [("e_norm_mix", e_norm_mix, m_e_norm_mix, v_e_norm_mix, d_enorm),
           ("e_q_norm", e_q_norm, m_e_q_norm, v_e_q_norm, d_qnorm),
           ("e_kv_norm", e_kv_norm, m_e_kv_norm, v_e_kv_norm, d_kvnorm),
           ("e_v_norm", e_v_norm, m_e_v_norm, v_e_v_norm, d_vgain),
           ("e_sgu_w", e_sgu_w, m_e_sgu_w, v_e_sgu_w, d_sgu_w_tril),
           ("e_sgu_b", e_sgu_b, m_e_sgu_b, v_e_sgu_b, d_sgu_b),
           ("e_mla_out_norm", e_mla_out_norm, m_e_mla_out_norm, v_e_mla_out_norm, d_mla_out),
           ("e_sgu_out_norm", e_sgu_out_norm, m_e_sgu_out_norm, v_e_sgu_out_norm, d_sgu_out),
           ("mlp_norm", mlp_norm, m_mlp_norm, v_mlp_norm, jnp.concatenate([d_mlp0, d_mlp1], axis=0)),
           ("final_norm", final_norm, m_final_norm, v_final_norm, d_final)]
    sizes = [int(np.prod(r[1].shape)) for r in rep]
    n_rep = sum(sizes)
    n_all = n_rep + 4 * D + 1
    width = -(-n_all // (8 * LANES)) * LANES
    pad = 8 * width - n_all
    flat = jnp.concatenate([r[4].reshape(-1) for r in rep]
                           + [d_onorm_full.reshape(-1), dconv_full.reshape(-1), loss_part[0, :1],
                              jnp.zeros((pad,), F32)])
    summed = sum_rows8(all_gather_vmem(flat.reshape(8, width), name="gather_small_grads", after=b_r3[0]), 8,
                       name="sum_small_grads").reshape(-1)

    loss = summed[n_rep + 4 * D]

    def pack_rep(i):
        return jnp.concatenate([r[i].reshape(-1) for r in rep]).reshape(n_rep // LANES, LANES)

    g_rep = summed[:n_rep].reshape(n_rep // LANES, LANES)
    d_rep, nm_rep, nv_rep = adam_flat(g_rep, pack_rep(1), pack_rep(2), pack_rep(3), name="adam_replicated")

    def unpack_rep(flat2d):
        out, off = {}, 0
        f = flat2d.reshape(-1)
        for r, n in zip(rep, sizes):
            out[r[0]] = f[off:off + n].reshape(r[1].shape)
            off += n
        return out

    small = {"grad": unpack_rep(g_rep), "delta": unpack_rep(d_rep), "new_m": unpack_rep(nm_rep),
             "new_v": unpack_rep(nv_rep)}
    g_onorm = lax.dynamic_slice(summed[n_rep:n_rep + D].reshape(1, D), (0, me * d_shard), (1, d_shard))
    g_conv = lax.dynamic_slice(summed[n_rep + D:n_rep + 4 * D].reshape(3, D), (0, me * d_shard), (3, d_shard))

    def pack_sharded(norm_part, conv_part):
        return jnp.concatenate([norm_part, conv_part, jnp.zeros((4, d_shard), F32)], axis=0)

    g_sh = pack_sharded(g_onorm, g_conv)
    d_sh, nm_sh, nv_sh = adam_flat(g_sh, pack_sharded(o_norm_mix, o_conv_w[0]), pack_sharded(m_o_norm_mix, m_o_conv_w[0]),
                                   pack_sharded(v_o_norm_mix, v_o_conv_w[0]), name="adam_sharded_small")
    for kind, arr in (("grad", g_sh), ("delta", d_sh), ("new_m", nm_sh), ("new_v", nv_sh)):
        small[kind]["o_norm_mix"] = arr[0:1]
        small[kind]["o_conv_w"] = arr[1:4][None]

    big = {"e_w_in": r_w_in, "e_w_uq": r_w_uq, "e_w_ukv": r_w_ukv, "e_w_out": r_w_out_e, "o_w_in": r_w_in_o,
           "o_w_out": r_w_out_o, "mlp_w1": r_w1, "mlp_w2": r_w2}
    order = ["e_norm_mix", "e_w_in", "e_q_norm", "e_w_uq", "e_kv_norm", "e_w_ukv", "e_v_norm", "e_sgu_w", "e_sgu_b",
             "e_mla_out_norm", "e_sgu_out_norm", "e_w_out", "o_norm_mix", "o_w_in", "o_conv_w", "o_w_out", "mlp_norm",
             "mlp_w1", "mlp_w2", "final_norm"]
    result = [loss, grad_x[None]]
    for ki, kind in enumerate(("grad", "delta", "new_m", "new_v")):
        for nm in order:
            result.append(big[nm][ki] if nm in big else small[kind][nm])
    return tuple(result)
```

```python
import numpy as np
import jax
import jax.numpy as jnp
from jax import lax
from jax.experimental import pallas as pl
from jax.experimental.pallas import tpu as pltpu

BF = jnp.bfloat16
F32 = jnp.float32
MESH = pl.DeviceIdType.MESH
N_DEV = 8

EPS = 1e-6
HEADS = 8
Q_LORA = 512
KV_LORA = 512
QK_NOPE = 128
QK_ROPE = 64
HALF_ROPE = QK_ROPE // 2
V_HEAD = 128
HEAD_PAD = 256
ROPE_BASE = 10000.0
GROUPS = 8
CH = 128
CHUNK = 128
SGU_OUT = GROUPS * CH
MLA_OUT = HEADS * V_HEAD
ATTN_SCALE = float((QK_NOPE + QK_ROPE) ** -0.5)

ADAM_LR = 0.001
ADAM_B1 = 0.9
ADAM_B2 = 0.999
ADAM_EPS = 1e-08
ADAM_WD = 0.01
ADAM_STEP = 10
ADAM_C1 = 1.0 - ADAM_B1 ** ADAM_STEP
ADAM_C2 = 1.0 - ADAM_B2 ** ADAM_STEP

V7X_VMEM_BYTES = 64 * 2 ** 20
VMEM_LIMIT_CAP = V7X_VMEM_BYTES - 6 * 2 ** 20
LANES = 128
ROW_TILE = 256
ATTN_TILE = 512
STREAM_BLOCK_ELEMS = 512 * 1024
MM_TILE = 1024
MM_K_TILE = 2048
MM_K_BLOCK_MAX = 3072


def _padded_bytes(block, dtype):
    dims = [d for d in block if d is not None]
    if len(dims) >= 1:
        dims[-1] = -(-dims[-1] // LANES) * LANES
    if len(dims) >= 2:
        dims[-2] = -(-dims[-2] // 16) * 16
    return int(np.prod(dims)) * jnp.dtype(dtype).itemsize


def _pcall(body, *, name, grid, ins, outs, scratch=(), semantics=None, aliases=None, prefetch=None, deps=()):
    any_spec = pl.BlockSpec(memory_space=pl.ANY)
    if deps:
        n_lead = len(ins) + (1 if prefetch is not None else 0)
        n_deps = len(deps)
        inner = body

        def body(*refs):
            inner(*refs[:n_lead], *refs[n_lead + n_deps:])

        ins = list(ins) + [(d, None, None) for d in deps]
    in_specs = [any_spec if b is None else pl.BlockSpec(b, m) for _, b, m in ins]
    out_specs = [any_spec if b is None else pl.BlockSpec(b, m) for _, _, b, m in outs]
    out_shape = [pltpu.HBM(s, d) for s, d, _, _ in outs]
    est = 0
    for a, b, _ in ins:
        if b is not None:
            est += 2 * _padded_bytes(b, a.dtype)
    for _, d, b, _ in outs:
        if b is not None:
            est += 2 * _padded_bytes(b, d)
    for s in scratch:
        if hasattr(s, "shape") and hasattr(s, "dtype"):
            est += _padded_bytes(s.shape, s.dtype)
    limit = int(min(VMEM_LIMIT_CAP, est + 16 * 2 ** 20))
    params = pltpu.CompilerParams(
        dimension_semantics=semantics or ("arbitrary",) * len(grid), vmem_limit_bytes=limit)
    args = [pltpu.with_memory_space_constraint(a, pltpu.HBM) for a, _, _ in ins]
    if prefetch is not None:
        grid_spec = pltpu.PrefetchScalarGridSpec(
            num_scalar_prefetch=1, grid=grid, in_specs=in_specs, out_specs=out_specs, scratch_shapes=list(scratch))
        call = pl.pallas_call(body, out_shape=out_shape, grid_spec=grid_spec, name=name, compiler_params=params,
                              input_output_aliases=aliases or {})
        return call(prefetch, *args)
    call = pl.pallas_call(body, out_shape=out_shape, grid=grid, in_specs=in_specs, out_specs=out_specs,
                          scratch_shapes=list(scratch), name=name, compiler_params=params,
                          input_output_aliases=aliases or {})
    return call(*args)


def _tile(dim, pref, quantum=LANES):
    if dim <= pref:
        return dim
    t = (pref // quantum) * quantum
    while t >= quantum:
        if dim % t == 0:
            return t
        t -= quantum
    return dim


def _vshape(arr_shape):
    if len(arr_shape) == 2:
        return tuple(arr_shape)
    s, r, c = arr_shape
    return (r, s * c)


def _vblock(arr_shape, br, bc, rc):
    if len(arr_shape) == 2:
        return (br, bc), (lambda *g: rc(*g))
    _, _, c = arr_shape
    assert c % bc == 0, (arr_shape, bc)
    per = c // bc

    def imap(*g):
        ri, ci = rc(*g)
        return (ci // per, ri, ci % per)

    return (None, br, bc), imap


def _shard_width(*shapes):
    w = None
    for s in shapes:
        if len(s) == 3:
            w = s[2] if w is None else int(np.gcd(w, s[2]))
    return w


def mm(a, b, *, name, ta=False, tb=False, out=None, outs=None, epi=None, epi_ins=(), bm=None, bn=None, bk=None,
       deps=(), jobs=(), job_index=None):
    av, bv = _vshape(a.shape), _vshape(b.shape)
    M, K = (av[1], av[0]) if ta else av
    K2, N = (bv[1], bv[0]) if tb else bv
    assert K == K2, (a.shape, b.shape, ta, tb)
    if outs is None:
        outs = [(out[0], out[1], None)]
    a_sw = _shard_width(a.shape)
    b_sw = _shard_width(b.shape)
    o_sw = _shard_width(*[o[0] for o in outs])
    m_lim = a_sw if (ta and a_sw) else None
    k_lim = [w for w in ((a_sw if not ta else None), (b_sw if tb else None)) if w]
    n_lim = [w for w in ((b_sw if not tb else None), o_sw) if w]
    if bm is None:
        bm = _tile(M, min([MM_TILE] + ([m_lim] if m_lim else [])))
    if bn is None:
        bn = _tile(N, min([MM_TILE] + n_lim))
    k_shards = 0
    if tb and len(b.shape) == 3 and bk is None and not (a_sw and not ta):
        k_shards = 1
        while 2 * k_shards <= b.shape[0] and 2 * k_shards * b_sw <= MM_K_BLOCK_MAX:
            k_shards *= 2
        bk = k_shards * b_sw
    if bk is None:
        bk = K if (K <= 4096 and not k_lim) else _tile(K, min([MM_K_TILE] + k_lim))
    assert M % bm == 0 and N % bn == 0 and K % bk == 0, (name, M, N, K, bm, bn, bk)
    nk = K // bk
    grid = (M // bm, N // bn, nk)
    if ta:
        a_blk, a_map = _vblock(a.shape, bk, bm, lambda i, j, k: (k, i))
    else:
        a_blk, a_map = _vblock(a.shape, bm, bk, lambda i, j, k: (i, k))
    if k_shards:
        b_blk, b_map = (k_shards, bn, b_sw), (lambda i, j, k: (k, j, 0))
    elif tb:
        b_blk, b_map = _vblock(b.shape, bn, bk, lambda i, j, k: (j, k))
    else:
        b_blk, b_map = _vblock(b.shape, bk, bn, lambda i, j, k: (k, j))
    dn = (((0 if ta else 1,), (1 if tb else 0,)), ((), ()))
    ins = [(a, a_blk, a_map), (b, b_blk, b_map)] + list(epi_ins)
    out_list = []
    for shape, dtype, cols in outs:
        cols = cols or bn
        blk, imap = _vblock(shape, bm, cols, lambda i, j, k: (i, j))
        out_list.append((shape, dtype, blk, imap))
    n_e, n_o = len(epi_ins), len(out_list)

    n_steps = grid[0] * grid[1] * nk
    built = [job(n_steps) for job in jobs]
    aliases = {}
    job_slices = []
    if built:
        def lin(i, j, k):
            return (i * grid[1] + j) * nk + k

        ins = [(arr, blk, None if blk is None else (lambda i, j, k, s, f=f: f(i, j, k))) for arr, blk, f in ins]
        out_list = [(sh, dt, blk, (lambda i, j, k, s, f=f: f(i, j, k))) for sh, dt, blk, f in out_list]
        n_main_in, n_main_out = len(ins), len(out_list)
        for jb in built:
            i0, o0 = len(ins), len(out_list)
            ins += [(arr, blk, None if blk is None else (lambda i, j, k, s, f=f: f(lin(i, j, k), s)))
                    for arr, blk, f in jb["ins"]]
            out_list += [(sh, dt, blk, (lambda i, j, k, s, f=f: f(lin(i, j, k), s))) for sh, dt, blk, f in jb["outs"]]
            aliases.update({1 + i0 + ai: o0 + ao for ai, ao in jb["aliases"].items()})
            job_slices.append((i0, len(jb["ins"]), o0, len(jb["outs"])))
    n_in_total = len(ins)

    def body(*refs):
        if built:
            refs = refs[1:]
        a_ref, b_ref = refs[0], refs[1]
        e_refs = refs[2:2 + n_e]
        o_refs = refs[n_in_total:n_in_total + n_o]
        for jb, (i0, ni, o0, no) in zip(built, job_slices):
            jb["fn"](refs[i0:i0 + ni], refs[n_in_total + o0:n_in_total + o0 + no])

        def finish(acc):
            res = epi(acc, *e_refs) if epi is not None else (acc,)
            for o_ref, r in zip(o_refs, res):
                o_ref[...] = r.astype(o_ref.dtype)

        x = a_ref[...].astype(BF)
        y = b_ref[...].astype(BF)
        if k_shards:
            p = None
            for s in range(k_shards):
                part = lax.dot_general(x[:, s * b_sw:(s + 1) * b_sw], y[s], dn, preferred_element_type=F32)
                p = part if p is None else p + part
        else:
            p = lax.dot_general(x, y, dn, preferred_element_type=F32)
        if nk == 1:
            finish(p)
        else:
            acc_ref = refs[-1]
            k = pl.program_id(2)

            @pl.when(k == 0)
            def _():
                acc_ref[...] = p

            @pl.when(k > 0)
            def _():
                acc_ref[...] += p

            @pl.when(k == nk - 1)
            def _():
                finish(acc_ref[...])

    scratch = [pltpu.VMEM((bm, bn), F32)] if nk > 1 else []
    res = _pcall(body, name=name, grid=grid, ins=ins, outs=out_list, scratch=scratch, deps=deps,
                 semantics=("parallel", "parallel", "arbitrary"), prefetch=job_index if built else None, aliases=aliases)
    main = res[0] if n_o == 1 else res[:n_o]
    if not built:
        return main
    return main, [res[o0:o0 + no] for _, _, o0, no in job_slices]


_GELU_K = float(np.sqrt(2.0 / np.pi))
_GELU_C = 0.044715


def _gelu(x):
    t = jnp.tanh(_GELU_K * (x + _GELU_C * (x * x * x)))
    return 0.5 * x * (1.0 + t)


def _gelu_grad(x):
    t = jnp.tanh(_GELU_K * (x + _GELU_C * (x * x * x)))
    return 0.5 * (1.0 + t) + 0.5 * x * (1.0 - t * t) * (_GELU_K * (1.0 + 3.0 * _GELU_C * (x * x)))


def _rstd(x):
    return lax.rsqrt(jnp.mean(x * x, axis=-1, keepdims=True) + EPS)


def _rms_bwd(x, gain, dy):
    r = _rstd(x)
    xh = x * r
    gdy = dy * gain
    dx = r * (gdy - xh * jnp.mean(gdy * xh, axis=-1, keepdims=True))
    return dx, dy * xh


def _rope_fwd(x, cos_t, sin_t):
    return x * cos_t + pltpu.roll(x, 2 * HALF_ROPE, 1) * sin_t


def _rope_bwd(dy, cos_t, sin_t):
    return dy * cos_t + pltpu.roll(dy * sin_t, 2 * HALF_ROPE, 1)


def _acc_rows(ref, val, first):
    s = jnp.sum(val, axis=0, keepdims=True)

    @pl.when(first)
    def _():
        ref[...] = s

    @pl.when(jnp.logical_not(first))
    def _():
        ref[...] += s


def rms_fwd(x, gain, *, name, col_block=0, width=None, deps=()):
    T = x.shape[0]
    width = width or x.shape[1]
    tm = _tile(T, ROW_TILE, 8)

    def body(x_ref, g_ref, o_ref):
        v = x_ref[...]
        o_ref[...] = (v * _rstd(v) * g_ref[...]).astype(BF)

    return _pcall(body, name=name, grid=(T // tm,),
                  ins=[(x, (tm, width), lambda i: (i, col_block)), (gain, (1, width), lambda i: (0, 0))],
                  outs=[((T, width), BF, (tm, width), lambda i: (i, 0))], semantics=("parallel",), deps=deps)[0]


def rms_bwd(x, gain, dy, *, name, col_block=0, dres=None, want_f32=True, want_bf=True, deps=()):
    T, width = dy.shape
    tm = _tile(T, ROW_TILE, 8)
    has_res = dres is not None

    def body(*refs):
        x_ref, g_ref, dy_ref = refs[:3]
        pos = 3
        res_ref = None
        if has_res:
            res_ref = refs[pos]
            pos += 1
        outs = refs[pos:]
        dx, dg_rows = _rms_bwd(x_ref[...], g_ref[...], dy_ref[...])
        if has_res:
            dx = dx + res_ref[...]
        o = 0
        if want_f32:
            outs[o][...] = dx
            o += 1
        if want_bf:
            outs[o][...] = dx.astype(BF)
            o += 1
        _acc_rows(outs[o], dg_rows, pl.program_id(0) == 0)

    ins = [(x, (tm, width), lambda i: (i, col_block)), (gain, (1, width), lambda i: (0, 0)),
           (dy, (tm, width), lambda i: (i, 0))]
    if has_res:
        ins.append((dres, (tm, width), lambda i: (i, 0)))
    outs = []
    if want_f32:
        outs.append(((T, width), F32, (tm, width), lambda i: (i, 0)))
    if want_bf:
        outs.append(((T, width), BF, (tm, width), lambda i: (i, 0)))
    outs.append(((1, width), F32, (1, width), lambda i: (0, 0)))
    return _pcall(body, name=name, grid=(T // tm,), ins=ins, outs=outs, deps=deps)


def mla_prep(proj, q_norm, kv_norm, cos_t, sin_t, *, name):
    T = proj.shape[0]
    tm = _tile(T, ROW_TILE, 8)
    kr_block = (proj.shape[1] - LANES) // LANES

    def body(cq_ref, ckv_ref, kr_ref, qg_ref, kg_ref, cos_ref, sin_ref, qn_ref, kvn_ref, krope_ref):
        cq = cq_ref[...]
        qn_ref[...] = (cq * _rstd(cq) * qg_ref[...]).astype(BF)
        ckv = ckv_ref[...]
        kvn_ref[...] = (ckv * _rstd(ckv) * kg_ref[...]).astype(BF)
        krope_ref[...] = _rope_fwd(kr_ref[...], cos_ref[...], sin_ref[...]).astype(BF)

    return _pcall(
        body, name=name, grid=(T // tm,),
        ins=[(proj, (tm, Q_LORA), lambda i: (i, 0)), (proj, (tm, KV_LORA), lambda i: (i, 1)),
             (proj, (tm, LANES), lambda i: (i, kr_block)),
             (q_norm, (1, Q_LORA), lambda i: (0, 0)), (kv_norm, (1, KV_LORA), lambda i: (0, 0)),
             (cos_t, (tm, LANES), lambda i: (i, 0)), (sin_t, (tm, LANES), lambda i: (i, 0))],
        outs=[((T, Q_LORA), BF, (tm, Q_LORA), lambda i: (i, 0)), ((T, KV_LORA), BF, (tm, KV_LORA), lambda i: (i, 0)),
              ((T, LANES), BF, (tm, LANES), lambda i: (i, 0))],
        semantics=("parallel",))


def _attn_scores(q, k_blk, diagonal):
    s = lax.dot_general(q, k_blk, (((1,), (1,)), ((), ())), preferred_element_type=F32) * ATTN_SCALE
    if diagonal:
        row = lax.broadcasted_iota(jnp.int32, s.shape, 0)
        col = lax.broadcasted_iota(jnp.int32, s.shape, 1)
        s = jnp.where(col <= row, s, -jnp.inf)
    return s


def attn_fwd(q, k, v, *, name):
    T = q.shape[0]
    tq = _tile(T, ATTN_TILE, 8)

    def body(q_ref, k_ref, v_ref, o_ref, lse_ref):
        i = pl.program_id(1)
        qv = q_ref[...]

        def block(kb, carry, diagonal):
            m, l, acc = carry
            start = pl.multiple_of(kb * tq, tq)
            s = _attn_scores(qv, k_ref[pl.ds(start, tq), :], diagonal)
            m_new = jnp.maximum(m, jnp.max(s, axis=-1, keepdims=True))
            alpha = jnp.exp(m - m_new)
            p = jnp.exp(s - m_new)
            l = alpha * l + jnp.sum(p, axis=-1, keepdims=True)
            acc = alpha * acc + jnp.dot(p.astype(BF), v_ref[pl.ds(start, tq), :], preferred_element_type=F32)
            return m_new, l, acc

        init = (jnp.full((tq, 1), -jnp.inf, F32), jnp.zeros((tq, 1), F32), jnp.zeros((tq, V_HEAD), F32))
        carry = lax.fori_loop(0, i, lambda kb, c: block(kb, c, False), init)
        m, l, acc = block(i, carry, True)
        o_ref[...] = acc / l
        lse_ref[...] = jnp.broadcast_to(m + jnp.log(l), (tq, V_HEAD))

    return _pcall(
        body, name=name, grid=(HEADS, T // tq),
        ins=[(q, (tq, HEAD_PAD), lambda h, i: (i, h)), (k, (T, HEAD_PAD), lambda h, i: (0, h)),
             (v, (T, V_HEAD), lambda h, i: (0, h))],
        outs=[((T, MLA_OUT), F32, (tq, V_HEAD), lambda h, i: (i, h)),
              ((T, MLA_OUT), F32, (tq, V_HEAD), lambda h, i: (i, h))], semantics=("parallel", "parallel"))


def attn_bwd(q, k, v, o, lse, do, *, name):
    T = q.shape[0]
    tq = _tile(T, ATTN_TILE, 8)

    def body(q_ref, k_ref, v_ref, o_ref, lse_ref, do_ref, dq_ref, dk_ref, dv_ref):
        i = pl.program_id(1)

        @pl.when(i == 0)
        def _():
            dk_ref[...] = jnp.zeros_like(dk_ref)
            dv_ref[...] = jnp.zeros_like(dv_ref)

        qv = q_ref[...]
        do_t = do_ref[...]
        lse_v = lse_ref[:, 0:1]
        delta = jnp.sum(do_t.astype(F32) * o_ref[...], axis=-1, keepdims=True)

        def block(kb, dq, diagonal):
            start = pl.multiple_of(kb * tq, tq)
            k_blk = k_ref[pl.ds(start, tq), :]
            v_blk = v_ref[pl.ds(start, tq), :]
            p = jnp.exp(_attn_scores(qv, k_blk, diagonal) - lse_v)
            dp = lax.dot_general(do_t, v_blk, (((1,), (1,)), ((), ())), preferred_element_type=F32)
            ds = (p * (dp - delta) * ATTN_SCALE).astype(BF)
            dk_ref[pl.ds(start, tq), :] += lax.dot_general(ds, qv, (((0,), (0,)), ((), ())), preferred_element_type=F32)
            dv_ref[pl.ds(start, tq), :] += lax.dot_general(p.astype(BF), do_t, (((0,), (0,)), ((), ())),
                                                          preferred_element_type=F32)
            return dq + jnp.dot(ds, k_blk, preferred_element_type=F32)

        dq = lax.fori_loop(0, i, lambda kb, c: block(kb, c, False), jnp.zeros((tq, HEAD_PAD), F32))
        dq_ref[...] = block(i, dq, True)

    return _pcall(
        body, name=name, grid=(HEADS, T // tq),
        ins=[(q, (tq, HEAD_PAD), lambda h, i: (i, h)), (k, (T, HEAD_PAD), lambda h, i: (0, h)),
             (v, (T, V_HEAD), lambda h, i: (0, h)), (o, (tq, V_HEAD), lambda h, i: (i, h)),
             (lse, (tq, V_HEAD), lambda h, i: (i, h)), (do, (tq, V_HEAD), lambda h, i: (i, h))],
        outs=[((T, HEADS * HEAD_PAD), F32, (tq, HEAD_PAD), lambda h, i: (i, h)),
              ((T, HEADS * HEAD_PAD), F32, (T, HEAD_PAD), lambda h, i: (0, h)),
              ((T, MLA_OUT), F32, (T, V_HEAD), lambda h, i: (0, h))],
        semantics=("parallel", "arbitrary"))


def mla_bwd_prep(dq, dk, dv, cos_t, sin_t, *, name):
    T = dq.shape[0]
    tm = _tile(T, ROW_TILE, 8)

    def body(dq_ref, dk_ref, dv_ref, cos_ref, sin_ref, dql_ref, dkvl_ref, dkr_ref):
        cos_v, sin_v = cos_ref[...], sin_ref[...]
        kr = jnp.zeros((tm, LANES), F32)
        for h in range(HEADS):
            lo = h * HEAD_PAD
            dql_ref[:, lo:lo + QK_NOPE] = dq_ref[:, lo:lo + QK_NOPE].astype(BF)
            dql_ref[:, lo + QK_NOPE:lo + HEAD_PAD] = _rope_bwd(
                dq_ref[:, lo + QK_NOPE:lo + HEAD_PAD], cos_v, sin_v).astype(BF)
            dkvl_ref[:, lo:lo + QK_NOPE] = dk_ref[:, lo:lo + QK_NOPE].astype(BF)
            dkvl_ref[:, lo + QK_NOPE:lo + HEAD_PAD] = dv_ref[:, h * V_HEAD:(h + 1) * V_HEAD].astype(BF)
            kr = kr + dk_ref[:, lo + QK_NOPE:lo + HEAD_PAD]
        dkr_ref[...] = _rope_bwd(kr, cos_v, sin_v).astype(BF)

    W = HEADS * HEAD_PAD
    return _pcall(
        body, name=name, grid=(T // tm,),
        ins=[(dq, (tm, W), lambda i: (i, 0)), (dk, (tm, W), lambda i: (i, 0)), (dv, (tm, MLA_OUT), lambda i: (i, 0)),
             (cos_t, (tm, LANES), lambda i: (i, 0)), (sin_t, (tm, LANES), lambda i: (i, 0))],
        outs=[((T, W), BF, (tm, W), lambda i: (i, 0)), ((T, W), BF, (tm, W), lambda i: (i, 0)),
              ((T, LANES), BF, (tm, LANES), lambda i: (i, 0))],
        semantics=("parallel",))


def _group_norm_stats(vg):
    mu = jnp.mean(vg, axis=-1, keepdims=True)
    d = vg - mu
    r = lax.rsqrt(jnp.mean(d * d, axis=-1, keepdims=True) + EPS)
    return d * r, r


def mix_fwd(a, proj, g_mla, g_sgu, v_gain, w_tril, b_full, *, name):
    T = a.shape[0]
    tm = _tile(T, ROW_TILE, CHUNK)
    n_chunk = tm // CHUNK

    def body(a_ref, u_ref, v_ref, gm_ref, gs_ref, vg_ref, w_ref, b_ref, o_ref, s_scr):
        av = a_ref[...]
        o_ref[:, :MLA_OUT] = (av * _rstd(av) * gm_ref[...]).astype(BF)
        for g in range(GROUPS):
            sl = slice(g * CH, (g + 1) * CH)
            vhat, _ = _group_norm_stats(_gelu(v_ref[:, sl]))
            vn = (vhat * vg_ref[:, sl]).astype(BF)
            u = _gelu(u_ref[:, sl])
            for ci in range(n_chunk):
                rs = slice(ci * CHUNK, (ci + 1) * CHUNK)
                y = jnp.dot(w_ref[g], vn[rs], preferred_element_type=F32) + b_ref[:, sl]
                s_scr[rs, sl] = u[rs] * y
        s = s_scr[...]
        o_ref[:, MLA_OUT:] = (s * _rstd(s) * gs_ref[...]).astype(BF)

    return _pcall(
        body, name=name, grid=(T // tm,),
        ins=[(a, (tm, MLA_OUT), lambda i: (i, 0)), (proj, (tm, SGU_OUT), lambda i: (i, 1)),
             (proj, (tm, SGU_OUT), lambda i: (i, 2)), (g_mla, (1, MLA_OUT), lambda i: (0, 0)),
             (g_sgu, (1, SGU_OUT), lambda i: (0, 0)), (v_gain, (1, SGU_OUT), lambda i: (0, 0)),
             (w_tril, (GROUPS, CHUNK, CHUNK), lambda i: (0, 0, 0)), (b_full, (CHUNK, SGU_OUT), lambda i: (0, 0))],
        outs=[((T, MLA_OUT + SGU_OUT), BF, (tm, MLA_OUT + SGU_OUT), lambda i: (i, 0))],
        scratch=[pltpu.VMEM((tm, SGU_OUT), F32)], semantics=("parallel",))[0]


def mix_bwd(dmixed, a, proj, g_mla, g_sgu, v_gain, w_tril, w_tril_t, b_full, *, name, deps=()):
    T = a.shape[0]
    tm = _tile(T, ROW_TILE, CHUNK)
    n_chunk = tm // CHUNK

    def body(dm_a_ref, dm_s_ref, a_ref, u_ref, v_ref, gm_ref, gs_ref, vg_ref, w_ref, wt_ref, b_ref,
             da_ref, duv_ref, dgm_ref, dgs_ref, dvg_ref, dw_ref, db_ref, s_scr, y_scr):
        first = pl.program_id(0) == 0
        da, dgm_rows = _rms_bwd(a_ref[...], gm_ref[...], dm_a_ref[...])
        da_ref[...] = da.astype(BF)
        _acc_rows(dgm_ref, dgm_rows, first)

        for g in range(GROUPS):
            sl = slice(g * CH, (g + 1) * CH)
            vhat, _ = _group_norm_stats(_gelu(v_ref[:, sl]))
            vn = (vhat * vg_ref[:, sl]).astype(BF)
            u = _gelu(u_ref[:, sl])
            for ci in range(n_chunk):
                rs = slice(ci * CHUNK, (ci + 1) * CHUNK)
                y = jnp.dot(w_ref[g], vn[rs], preferred_element_type=F32) + b_ref[:, sl]
                y_scr[rs, sl] = y
                s_scr[rs, sl] = u[rs] * y
        ds, dgs_rows = _rms_bwd(s_scr[...], gs_ref[...], dm_s_ref[...])
        _acc_rows(dgs_ref, dgs_rows, first)
        s_scr[...] = ds

        @pl.when(first)
        def _():
            dw_ref[...] = jnp.zeros_like(dw_ref)
            db_ref[...] = jnp.zeros_like(db_ref)

        for g in range(GROUPS):
            sl = slice(g * CH, (g + 1) * CH)
            upre = u_ref[:, sl]
            vpre = v_ref[:, sl]
            u = _gelu(upre)
            vhat, r = _group_norm_stats(_gelu(vpre))
            gain = vg_ref[:, sl]
            vn = (vhat * gain).astype(BF)
            dsg = s_scr[:, sl]
            duv_ref[:, sl] = (dsg * y_scr[:, sl] * _gelu_grad(upre)).astype(BF)
            dy = dsg * u
            dyb = dy.astype(BF)
            dvn_parts = []
            for ci in range(n_chunk):
                rs = slice(ci * CHUNK, (ci + 1) * CHUNK)
                dvn_parts.append(jnp.dot(wt_ref[g], dyb[rs], preferred_element_type=F32))
                dw_ref[g] += lax.dot_general(dyb[rs], vn[rs], (((1,), (1,)), ((), ())), preferred_element_type=F32)
                db_ref[:, sl] += jnp.broadcast_to(jnp.sum(dy[rs], axis=-1, keepdims=True), (CHUNK, CH))
            dvn = dvn_parts[0] if n_chunk == 1 else jnp.concatenate(dvn_parts, axis=0)
            _acc_rows(dvg_ref.at[:, sl], dvn * vhat, first)
            dvh = dvn * gain
            dvg = r * (dvh - jnp.mean(dvh, axis=-1, keepdims=True)
                       - vhat * jnp.mean(dvh * vhat, axis=-1, keepdims=True))
            duv_ref[:, SGU_OUT + g * CH:SGU_OUT + (g + 1) * CH] = (dvg * _gelu_grad(vpre)).astype(BF)

    return _pcall(
        body, name=name, grid=(T // tm,),
        ins=[(dmixed, (tm, MLA_OUT), lambda i: (i, 0)), (dmixed, (tm, SGU_OUT), lambda i: (i, 1)),
             (a, (tm, MLA_OUT), lambda i: (i, 0)), (proj, (tm, SGU_OUT), lambda i: (i, 1)),
             (proj, (tm, SGU_OUT), lambda i: (i, 2)), (g_mla, (1, MLA_OUT), lambda i: (0, 0)),
             (g_sgu, (1, SGU_OUT), lambda i: (0, 0)), (v_gain, (1, SGU_OUT), lambda i: (0, 0)),
             (w_tril, (GROUPS, CHUNK, CHUNK), lambda i: (0, 0, 0)), (w_tril_t, (GROUPS, CHUNK, CHUNK), lambda i: (0, 0, 0)),
             (b_full, (CHUNK, SGU_OUT), lambda i: (0, 0))],
        outs=[((T, MLA_OUT), BF, (tm, MLA_OUT), lambda i: (i, 0)),
              ((T, 2 * SGU_OUT), BF, (tm, 2 * SGU_OUT), lambda i: (i, 0)),
              ((1, MLA_OUT), F32, (1, MLA_OUT), lambda i: (0, 0)), ((1, SGU_OUT), F32, (1, SGU_OUT), lambda i: (0, 0)),
              ((1, SGU_OUT), F32, (1, SGU_OUT), lambda i: (0, 0)),
              ((GROUPS, CHUNK, CHUNK), F32, (GROUPS, CHUNK, CHUNK), lambda i: (0, 0, 0)),
              ((CHUNK, SGU_OUT), F32, (CHUNK, SGU_OUT), lambda i: (0, 0))],
        scratch=[pltpu.VMEM((tm, SGU_OUT), F32), pltpu.VMEM((tm, SGU_OUT), F32)], deps=deps)


def _shift_down(z, n, row):
    return jnp.where(row >= n, pltpu.roll(z, n, 0), 0.0)


def _shift_up(z, n, row, T):
    return jnp.where(row < T - n, pltpu.roll(z, T - n, 0), 0.0)


def conv_fwd(proj, conv_w, *, name):
    T, D3 = proj.shape
    D = D3 // 3
    tn = _tile(D, 256)
    nj = D // tn

    def body(b_ref, c_ref, x_ref, w_ref, o_ref):
        row = lax.broadcasted_iota(jnp.int32, (T, tn), 0)
        z = c_ref[...] * x_ref[...]
        zc = w_ref[2:3, :] * z + w_ref[1:2, :] * _shift_down(z, 1, row) + w_ref[0:1, :] * _shift_down(z, 2, row)
        o_ref[...] = (b_ref[...] * zc).astype(BF)

    return _pcall(
        body, name=name, grid=(nj,),
        ins=[(proj, (T, tn), lambda j: (0, j)), (proj, (T, tn), lambda j: (0, nj + j)),
             (proj, (T, tn), lambda j: (0, 2 * nj + j)), (conv_w, (3, tn), lambda j: (0, j))],
        outs=[((T, D), BF, (T, tn), lambda j: (0, j))], semantics=("parallel",))[0]


def conv_bwd(dg, proj, conv_w, *, name, deps=()):
    T, D3 = proj.shape
    D = D3 // 3
    tn = _tile(D, 256)
    nj = D // tn

    def body(dg_ref, b_ref, c_ref, x_ref, w_ref, dp_ref, dw_ref, dc_scr, dx_scr):
        part = pl.program_id(1)

        @pl.when(part == 0)
        def _():
            row = lax.broadcasted_iota(jnp.int32, (T, tn), 0)
            c, x = c_ref[...], x_ref[...]
            z = c * x
            z1 = _shift_down(z, 1, row)
            z2 = _shift_down(z, 2, row)
            dgv = dg_ref[...]
            zc = w_ref[2:3, :] * z + w_ref[1:2, :] * z1 + w_ref[0:1, :] * z2
            dp_ref[...] = (dgv * zc).astype(BF)
            dzc = dgv * b_ref[...]
            dw_ref[0:1, :] = jnp.sum(dzc * z2, axis=0, keepdims=True)
            dw_ref[1:2, :] = jnp.sum(dzc * z1, axis=0, keepdims=True)
            dw_ref[2:3, :] = jnp.sum(dzc * z, axis=0, keepdims=True)
            dz = (w_ref[2:3, :] * dzc + w_ref[1:2, :] * _shift_up(dzc, 1, row, T)
                  + w_ref[0:1, :] * _shift_up(dzc, 2, row, T))
            dc_scr[...] = (dz * x).astype(BF)
            dx_scr[...] = (dz * c).astype(BF)

        @pl.when(part == 1)
        def _():
            dp_ref[...] = dc_scr[...]

        @pl.when(part == 2)
        def _():
            dp_ref[...] = dx_scr[...]

    return _pcall(
        body, name=name, grid=(nj, 3),
        ins=[(dg, (T, tn), lambda j, p: (0, j)), (proj, (T, tn), lambda j, p: (0, j)),
             (proj, (T, tn), lambda j, p: (0, nj + j)), (proj, (T, tn), lambda j, p: (0, 2 * nj + j)),
             (conv_w, (3, tn), lambda j, p: (0, j))],
        outs=[((T, D3), BF, (T, tn), lambda j, p: (0, p * nj + j)), ((3, D), F32, (3, tn), lambda j, p: (0, j))],
        scratch=[pltpu.VMEM((T, tn), BF), pltpu.VMEM((T, tn), BF)], semantics=("parallel", "arbitrary"), deps=deps)


def loss_bwd(x_parts, gain, target, *, name):
    T, D = target.shape
    tm = _tile(T, ROW_TILE, 8)
    n_x = len(x_parts)

    def body(*refs):
        x_refs = refs[:n_x]
        g_ref, t_ref, dx_ref, dxb_ref, dg_ref, loss_ref = refs[n_x:]
        first = pl.program_id(0) == 0
        xv = jnp.concatenate([r[...] for r in x_refs], axis=-1) if n_x > 1 else x_refs[0][...]
        r = _rstd(xv)
        xh = xv * r
        gain_v = g_ref[...]
        err = xh * gain_v - t_ref[...]
        part = 0.5 * jnp.sum(jnp.mean(err * err, axis=-1, keepdims=True), axis=0, keepdims=True)
        _acc_rows(loss_ref, jnp.broadcast_to(part, (1, LANES)), first)
        dy = err * (1.0 / D)
        gdy = dy * gain_v
        dx = r * (gdy - xh * jnp.mean(gdy * xh, axis=-1, keepdims=True))
        dx_ref[...] = dx
        dxb_ref[...] = dx.astype(BF)
        _acc_rows(dg_ref, dy * xh, first)

    return _pcall(
        body, name=name, grid=(T // tm,),
        ins=[(p, (tm, D // n_x), lambda i: (i, 0)) for p in x_parts]
        + [(gain, (1, D), lambda i: (0, 0)), (target, (tm, D), lambda i: (i, 0))],
        outs=[((T, D), F32, (tm, D), lambda i: (i, 0)), ((T, D), BF, (tm, D), lambda i: (i, 0)),
              ((1, D), F32, (1, D), lambda i: (0, 0)), ((1, LANES), F32, (1, LANES), lambda i: (0, 0))])


def _adamw(g, w, m, v):
    m = ADAM_B1 * m + (1.0 - ADAM_B1) * g
    v = ADAM_B2 * v + (1.0 - ADAM_B2) * (g * g)
    m_hat = m / ADAM_C1
    v_hat = v / ADAM_C2
    delta = -ADAM_LR * (m_hat / (jnp.sqrt(v_hat) + ADAM_EPS) + ADAM_WD * w)
    return delta, m, v


def adam_flat(g, w, m, v, *, name):
    def body(g_ref, w_ref, m_ref, v_ref, d_ref, nm_ref, nv_ref):
        d, nm, nv = _adamw(g_ref[...], w_ref[...], m_ref[...], v_ref[...])
        d_ref[...] = d
        nm_ref[...] = nm
        nv_ref[...] = nv

    blk = g.shape
    zero = lambda: (0, 0)
    return _pcall(body, name=name, grid=(),
                  ins=[(t, blk, zero) for t in (g, w, m, v)],
                  outs=[(blk, F32, blk, zero)] * 3)


def _chip_slots():
    x, y, c = lax.axis_index("x"), lax.axis_index("y"), lax.axis_index("c")
    chips = [(1 - x, y), (x, 1 - y), (1 - x, 1 - y)]
    return x, y, c, chips


def device_index():
    x, y, c, chips = _chip_slots()
    return jnp.stack([4 * x + 2 * y + c, 2 * x + y] + [4 * cx + 2 * cy + c for cx, cy in chips]
                     + [2 * cx + cy for cx, cy in chips]).astype(jnp.int32)


def _job_rows(R, C, n_steps):
    if n_steps is None:
        n_steps = max(1, R * C // STREAM_BLOCK_ELEMS)
    n_blk = max([d for d in range(1, n_steps + 1) if R % d == 0 and (R // d) % 16 == 0] or [1])
    return R // n_blk, n_blk


def run_job(job, *, index, name, deps=()):
    jb = job(None)
    n_in = len(jb["ins"])

    def body(idx_ref, *refs):
        jb["fn"](refs[:n_in], refs[n_in:n_in + len(jb["outs"])])

    return _pcall(body, name=name, grid=(jb["n_blk"],), ins=jb["ins"], outs=jb["outs"], prefetch=index,
                  aliases={1 + a: o for a, o in jb["aliases"].items()}, semantics=("parallel",), deps=deps)


def adam_job(gs, a_buf, b_buf, w, m, v, layer, prev):
    L, R, C = w.shape

    def build(n_steps):
        tr, n_blk = _job_rows(R, C, n_steps)
        blk = (None, tr, C)
        row = lambda t: jnp.minimum(t, n_blk - 1)
        ins = [(gs, blk, lambda t, s: (s[0], row(t), 0)), (a_buf, blk, lambda t, s: (s[1], row(t), 0))]
        ins += [(b_buf, blk, lambda t, s, j=j: (j, row(t), 0)) for j in range(3)]
        ins += [(p, blk, lambda t, s: (layer, row(t), 0)) for p in (w, m, v)]
        ins += [(p, None, None) for p in (prev or [])]

        def fn(i, o):
            g = ((((i[0][...].astype(F32) + i[1][...].astype(F32)) + i[2][...].astype(F32))
                  + i[3][...].astype(F32)) + i[4][...].astype(F32))
            d, nm, nv = _adamw(g, i[5][...], i[6][...], i[7][...])
            o[0][...] = g
            o[1][...] = d
            o[2][...] = nm
            o[3][...] = nv

        return dict(ins=ins, outs=[((L, R, C), F32, blk, lambda t, s: (layer, row(t), 0))] * 4, fn=fn,
                    aliases={8 + o: o for o in range(4)} if prev else {}, n_blk=n_blk)

    return build


def pair_job(gs, a_buf):
    _, R, C = gs.shape

    def build(n_steps):
        tr, n_blk = _job_rows(R, C, n_steps)
        blk = (None, tr, C)
        row = lambda t: jnp.minimum(t, n_blk - 1)
        ins = [(gs, blk, lambda t, s, j=j: (s[2 + j], row(t), 0)) for j in range(3)]
        ins += [(a_buf, blk, lambda t, s, j=j: (s[5 + j], row(t), 0)) for j in range(3)]

        def fn(i, o):
            for j in range(3):
                o[0][j] = (i[j][...].astype(F32) + i[3 + j][...].astype(F32)).astype(BF)

        return dict(ins=ins, outs=[((3, R, C), BF, (3, tr, C), lambda t, s: (0, row(t), 0))], fn=fn, aliases={},
                    n_blk=n_blk)

    return build


def reduce_sum(gs, a_buf, b_buf, *, name):
    _, R, C = gs.shape
    tr = _tile(R, 256, 16)
    x, y, c, _ = _chip_slots()
    idx = jnp.stack([4 * x + 2 * y + c, 2 * x + y]).astype(jnp.int32)

    def body(idx_ref, g_ref, a_ref, b0_ref, b1_ref, b2_ref, o_ref):
        o_ref[...] = ((((g_ref[...].astype(F32) + a_ref[...].astype(F32)) + b0_ref[...].astype(F32))
                       + b1_ref[...].astype(F32)) + b2_ref[...].astype(F32))

    blk3 = (None, tr, C)
    return _pcall(body, name=name, grid=(R // tr,),
                  ins=[(gs, blk3, lambda i, s: (s[0], i, 0)), (a_buf, blk3, lambda i, s: (s[1], i, 0)),
                       (b_buf, blk3, lambda i, s: (0, i, 0)), (b_buf, blk3, lambda i, s: (1, i, 0)),
                       (b_buf, blk3, lambda i, s: (2, i, 0))],
                  outs=[((R, C), F32, (tr, C), lambda i, s: (i, 0))], prefetch=idx, semantics=("parallel",))[0]


def adam_rows(g, w, m, v, *, name):
    R, C = g.shape
    tr = _tile(R, 256, 8)

    def body(g_ref, w_ref, m_ref, v_ref, d_ref, nm_ref, nv_ref):
        d, nm, nv = _adamw(g_ref[...], w_ref[...], m_ref[...], v_ref[...])
        d_ref[...] = d
        nm_ref[...] = nm
        nv_ref[...] = nv

    spec = ((tr, C), lambda i: (i, 0))
    return _pcall(body, name=name, grid=(R // tr,), ins=[(t, *spec) for t in (g, w, m, v)],
                  outs=[((R, C), F32, *spec)] * 3, semantics=("parallel",))


def sum_rows8(gathered, rows, *, name):
    W = gathered.shape[1]

    def body(g_ref, o_ref):
        acc = g_ref[0:rows, :]
        for d in range(1, N_DEV):
            acc = acc + g_ref[d * rows:(d + 1) * rows, :]
        o_ref[...] = acc

    return _pcall(body, name=name, grid=(), ins=[(gathered, gathered.shape, lambda: (0, 0))],
                  outs=[((rows, W), F32, (rows, W), lambda: (0, 0))])[0]


HBM_SPEC = pl.BlockSpec(memory_space=pltpu.HBM)
SEM_SPEC = pl.BlockSpec(memory_space=pltpu.SEMAPHORE)
ANY_SPEC = pl.BlockSpec(memory_space=pl.ANY)
DATAFLOW = pltpu.SideEffectType.DATAFLOW_SIDE_EFFECTING


def _in_hbm(v):
    return pltpu.with_memory_space_constraint(v, pltpu.HBM)


def _slot(p):
    return 4 * p[0] + 2 * p[1] + p[2]


def _gather_peers():
    x, y, c, chips = _chip_slots()
    return (x, y, c), [(x, y, 1 - c)] + [(*chip, c) for chip in chips]


def gather_start(groups, after, *, name):
    flat = [s for g in groups for s in g]
    n, n_g = len(flat), len(groups)
    where = [(gi, ti) for gi, g in enumerate(groups) for ti in range(len(g))]

    def body(*refs):
        src, land = refs[:n], refs[n:2 * n]
        sems = refs[2 * n + 1:2 * n + 1 + 2 * n_g]
        me, peers = _gather_peers()
        for t in range(n):
            gi, ti = where[t]
            for k, to in enumerate(peers):
                pltpu.make_async_remote_copy(
                    src_ref=src[t], dst_ref=land[t].at[_slot(me)], send_sem=sems[2 * gi].at[4 * ti + k],
                    recv_sem=sems[2 * gi + 1].at[4 * ti + k], device_id=to, device_id_type=MESH).start()
        refs[-1][...] = jnp.zeros_like(refs[-1])

    out_shape = []
    for g in groups:
        out_shape += [pltpu.SemaphoreType.DMA((4 * len(g),)), pltpu.SemaphoreType.DMA((4 * len(g),))]
    out_shape += [pltpu.HBM(s.shape, s.dtype) for s in flat]
    out_shape += [pltpu.HBM((N_DEV,) + s.shape, s.dtype) for s in flat]
    out_shape += [jax.ShapeDtypeStruct((8, LANES), F32)]
    aliases = {t: 2 * n_g + t for t in range(n)}
    aliases.update({n + t: 2 * n_g + n + t for t in range(n)})
    res = pl.pallas_call(
        body, name=name, out_shape=out_shape, in_specs=[HBM_SPEC] * (2 * n) + [ANY_SPEC],
        out_specs=[SEM_SPEC] * (2 * n_g) + [HBM_SPEC] * (2 * n) + [pl.BlockSpec(memory_space=pltpu.VMEM)],
        input_output_aliases=aliases, compiler_params=pltpu.CompilerParams(has_side_effects=DATAFLOW),
    )(*[_in_hbm(s) for s in flat], *[_in_hbm(lax.empty((N_DEV,) + s.shape, s.dtype)) for s in flat], after)
    out, off = [], 0
    for gi, g in enumerate(groups):
        k = len(g)
        out.append((res[2 * gi], res[2 * gi + 1], res[2 * n_g + off:2 * n_g + off + k],
                    res[2 * n_g + n + off:2 * n_g + n + off + k]))
        off += k
    return out, res[-1]


def gather_wait(started, after, *, name):
    send_sems, recv_sems, srcs, lands = started
    n = len(srcs)
    after = list(after)

    def body(*refs):
        src, land = refs[:n], refs[n:2 * n]
        send, recv = refs[2 * n], refs[2 * n + 1]
        _, peers = _gather_peers()
        for t in range(n):
            for k, frm in enumerate(peers):
                cp = pltpu.make_async_remote_copy(
                    src_ref=src[t], dst_ref=land[t].at[_slot(frm)], send_sem=send.at[4 * t + k],
                    recv_sem=recv.at[4 * t + k],
                    device_id=frm, device_id_type=MESH)
                cp.wait_send()
                cp.wait_recv()

    res = pl.pallas_call(
        body, name=name,
        out_shape=[pltpu.HBM(s.shape, s.dtype) for s in srcs] + [pltpu.HBM(l.shape, l.dtype) for l in lands],
        in_specs=[HBM_SPEC] * (2 * n) + [SEM_SPEC, SEM_SPEC] + [ANY_SPEC] * len(after),
        out_specs=[HBM_SPEC] * (2 * n), input_output_aliases={t: t for t in range(2 * n)},
        compiler_params=pltpu.CompilerParams(has_side_effects=DATAFLOW),
    )(*srcs, *lands, send_sems, recv_sems, *after)
    return res[:n], res[n:]


def place_own(src, land, *, name):
    R, C = src.shape
    tr = _tile(R, 512, 16)
    x, y, c, _ = _chip_slots()
    idx = jnp.stack([4 * x + 2 * y + c]).astype(jnp.int32)

    def body(idx_ref, s_ref, land_ref, o_ref):
        o_ref[...] = s_ref[...]

    return _pcall(body, name=name, grid=(R // tr,),
                  ins=[(src, (tr, C), lambda i, s: (i, 0)), (land, None, None)],
                  outs=[(land.shape, land.dtype, (None, tr, C), lambda i, s: (s[0], i, 0))],
                  prefetch=idx, aliases={2: 0}, semantics=("parallel",))[0]


def gather_finish(srcs, lands, *, name):
    n = len(srcs)

    def body(*refs):
        land = refs[n:2 * n]
        send_sems, recv_sems = refs[2 * n:]
        x, y, c, chips = _chip_slots()
        me, sibling = (x, y, c), (x, y, 1 - c)

        def copy(t, j, block, to):
            return pltpu.make_async_remote_copy(
                src_ref=land[t].at[_slot(block)], dst_ref=land[t].at[_slot(block)], send_sem=send_sems.at[t, j],
                recv_sem=recv_sems.at[t, j], device_id=to, device_id_type=MESH)

        sends = [copy(t, j, (*chip, c), sibling) for t in range(n) for j, chip in enumerate(chips)]
        for cp in sends:
            cp.start()
        for t in range(n):
            for j, chip in enumerate(chips):
                copy(t, j, (*chip, 1 - c), me).wait_recv()
        for cp in sends:
            cp.wait_send()

    passed = pl.pallas_call(
        body, name=name, out_shape=[jax.ShapeDtypeStruct(l.shape, l.dtype) for l in lands],
        in_specs=[ANY_SPEC] * n, out_specs=[ANY_SPEC] * n,
        input_output_aliases={t: t for t in range(n)},
        scratch_shapes=[pltpu.SemaphoreType.DMA((n, 3)), pltpu.SemaphoreType.DMA((n, 3))],
    )(*lands)
    return [place_own(s, l, name=f"{name}_own{t}") for t, (s, l) in enumerate(zip(srcs, passed))]


def chips_start(pairs, *, name):
    n = len(pairs)

    def body(*refs):
        src, land = refs[:n], refs[n:2 * n]
        send, recv = refs[2 * n], refs[2 * n + 1]
        token = refs[-1]
        x, y, c, chips = _chip_slots()
        for t in range(n):
            for j, chip in enumerate(chips):
                pltpu.make_async_remote_copy(
                    src_ref=src[t].at[j], dst_ref=land[t].at[j], send_sem=send.at[3 * t + j],
                    recv_sem=recv.at[3 * t + j], device_id=(*chip, c), device_id_type=MESH).start()
        token[...] = jnp.zeros_like(token)

    res = pl.pallas_call(
        body, name=name,
        out_shape=[pltpu.SemaphoreType.DMA((3 * n,)), pltpu.SemaphoreType.DMA((3 * n,))]
        + [pltpu.HBM(p.shape, p.dtype) for p in pairs] * 2 + [jax.ShapeDtypeStruct((8, LANES), F32)],
        in_specs=[HBM_SPEC] * (2 * n),
        out_specs=[SEM_SPEC, SEM_SPEC] + [HBM_SPEC] * (2 * n) + [pl.BlockSpec(memory_space=pltpu.VMEM)],
        input_output_aliases={t: 2 + t for t in range(2 * n)},
        compiler_params=pltpu.CompilerParams(has_side_effects=DATAFLOW),
    )(*[_in_hbm(p) for p in pairs], *[_in_hbm(lax.empty(p.shape, p.dtype)) for p in pairs])
    return res[0], res[1], res[2:2 + n], res[2 + n:2 + 2 * n], res[-1]


def chips_wait(started, after, *, name):
    send_sems, recv_sems, srcs, lands, _ = started
    n = len(srcs)

    def body(*refs):
        src, land = refs[:n], refs[n:2 * n]
        send, recv = refs[2 * n], refs[2 * n + 1]
        x, y, c, chips = _chip_slots()
        for t in range(n):
            for j, chip in enumerate(chips):
                cp = pltpu.make_async_remote_copy(
                    src_ref=src[t].at[j], dst_ref=land[t].at[j], send_sem=send.at[3 * t + j],
                    recv_sem=recv.at[3 * t + j], device_id=(*chip, c), device_id_type=MESH)
                cp.wait_send()
                cp.wait_recv()

    res = pl.pallas_call(
        body, name=name, out_shape=[pltpu.HBM(s.shape, s.dtype) for s in srcs] * 2,
        in_specs=[HBM_SPEC] * (2 * n) + [SEM_SPEC, SEM_SPEC, ANY_SPEC], out_specs=[HBM_SPEC] * (2 * n),
        input_output_aliases={t: t for t in range(2 * n)},
        compiler_params=pltpu.CompilerParams(has_side_effects=DATAFLOW),
    )(*srcs, *lands, send_sems, recv_sems, after)
    return res[n:]


def _sibling_copies(src, land, send, recv, n):
    x, y, c, _ = _chip_slots()
    return [pltpu.make_async_remote_copy(
        src_ref=src[t].at[4 * (q // 2) + 2 * (q % 2) + (1 - c)], dst_ref=land[t].at[q], send_sem=send.at[4 * t + q],
        recv_sem=recv.at[4 * t + q], device_id=(x, y, 1 - c), device_id_type=MESH)
        for t in range(n) for q in range(4)]


def sibling_start(gs, *, name):
    n = len(gs)

    def body(*refs):
        for cp in _sibling_copies(refs[:n], refs[n:2 * n], refs[2 * n], refs[2 * n + 1], n):
            cp.start()
        refs[-1][...] = jnp.zeros_like(refs[-1])

    lands = [lax.empty((4,) + g.shape[1:], g.dtype) for g in gs]
    res = pl.pallas_call(
        body, name=name,
        out_shape=[pltpu.SemaphoreType.DMA((4 * n,)), pltpu.SemaphoreType.DMA((4 * n,))]
        + [pltpu.HBM(g.shape, g.dtype) for g in gs] + [pltpu.HBM(l.shape, l.dtype) for l in lands]
        + [jax.ShapeDtypeStruct((8, LANES), F32)],
        in_specs=[HBM_SPEC] * (2 * n),
        out_specs=[SEM_SPEC, SEM_SPEC] + [HBM_SPEC] * (2 * n) + [pl.BlockSpec(memory_space=pltpu.VMEM)],
        input_output_aliases={t: 2 + t for t in range(2 * n)},
        compiler_params=pltpu.CompilerParams(has_side_effects=DATAFLOW),
    )(*[_in_hbm(g) for g in gs], *[_in_hbm(l) for l in lands])
    return res[0], res[1], res[2:2 + n], res[2 + n:2 + 2 * n], res[-1]


def sibling_wait(started, after, *, name):
    send_sems, recv_sems, srcs, lands, _ = started
    n = len(srcs)

    def body(*refs):
        for cp in _sibling_copies(refs[:n], refs[n:2 * n], refs[2 * n], refs[2 * n + 1], n):
            cp.wait_send()
            cp.wait_recv()

    res = pl.pallas_call(
        body, name=name,
        out_shape=[pltpu.HBM(s.shape, s.dtype) for s in srcs] + [pltpu.HBM(l.shape, l.dtype) for l in lands],
        in_specs=[HBM_SPEC] * (2 * n) + [SEM_SPEC, SEM_SPEC, ANY_SPEC], out_specs=[HBM_SPEC] * (2 * n),
        input_output_aliases={t: t for t in range(2 * n)},
        compiler_params=pltpu.CompilerParams(has_side_effects=DATAFLOW),
    )(*srcs, *lands, send_sems, recv_sems, after)
    return res[:n], res[n:]


def all_gather_vmem(x_shard, *, name, after=None):
    m_per, n = x_shard.shape
    n_after = 0 if after is None else 1

    def body(x_ref, *rest):
        out_ref, send_sems, recv_sems, local_sem = rest[n_after:]
        x, y, c, chips = _chip_slots()
        me, sibling = (x, y, c), (x, y, 1 - c)

        def rows(px, py, pc):
            return out_ref.at[pl.ds((4 * px + 2 * py + pc) * m_per, m_per), :]

        def copy(k, block, to, src=None):
            return pltpu.make_async_remote_copy(
                src_ref=rows(*block) if src is None else src, dst_ref=rows(*block),
                send_sem=send_sems.at[k], recv_sem=recv_sems.at[k], device_id=to, device_id_type=MESH)

        mine = pltpu.make_async_copy(x_ref, rows(*me), local_sem)
        mine.start()
        first = [copy(0, me, sibling, src=x_ref)]
        first += [copy(1 + j, me, (*chip, c), src=x_ref) for j, chip in enumerate(chips)]
        for cp in first:
            cp.start()
        passed = [copy(4 + j, (*chip, c), sibling) for j, chip in enumerate(chips)]
        for j, chip in enumerate(chips):
            copy(1 + j, (*chip, c), me).wait_recv()
            passed[j].start()
        copy(0, sibling, me).wait_recv()
        for j, chip in enumerate(chips):
            copy(4 + j, (*chip, 1 - c), me).wait_recv()
        for cp in first + passed:
            cp.wait_send()
        mine.wait()

    vmem = pl.BlockSpec(memory_space=pltpu.VMEM)
    return pl.pallas_call(
        body, name=name, out_shape=jax.ShapeDtypeStruct((N_DEV * m_per, n), x_shard.dtype),
        in_specs=[vmem] + [ANY_SPEC] * n_after, out_specs=vmem,
        scratch_shapes=[pltpu.SemaphoreType.DMA((7,)), pltpu.SemaphoreType.DMA((7,)), pltpu.SemaphoreType.DMA],
        compiler_params=pltpu.CompilerParams(vmem_limit_bytes=int(min(
            VMEM_LIMIT_CAP, 2 * (N_DEV + 1) * m_per * n * x_shard.dtype.itemsize + 16 * 2 ** 20))),
    )(x_shard, *([] if after is None else [after]))


def _rope_slab(cols):
    z = jnp.zeros(cols.shape[:-1] + (HALF_ROPE,), cols.dtype)
    return jnp.concatenate([cols[..., :HALF_ROPE], z, cols[..., HALF_ROPE:], z], axis=-1)


def _rope_unslab(slab):
    return jnp.concatenate([slab[..., :HALF_ROPE], slab[..., 2 * HALF_ROPE:3 * HALF_ROPE]], axis=-1)


def _pack_w_in_t(wt_g):
    s, c, d = wt_g.shape
    w = wt_g.reshape(s * c, d)
    c2, c3 = Q_LORA + KV_LORA, Q_LORA + KV_LORA + QK_ROPE
    r = w[c2:c3]
    z = jnp.zeros((HALF_ROPE, d), w.dtype)
    return jnp.concatenate([w[:c2], w[c3:], r[:HALF_ROPE], z, r[HALF_ROPE:], z], axis=0)


def _unpack_w_in_t_grad(dwt):
    d = dwt.shape[1]
    c2 = Q_LORA + KV_LORA
    uv = 2 * SGU_OUT
    slab = dwt[c2 + uv:]
    g = jnp.concatenate([dwt[:c2], slab[:HALF_ROPE], slab[2 * HALF_ROPE:3 * HALF_ROPE], dwt[c2:c2 + uv]], axis=0)
    return g.reshape(N_DEV, g.shape[0] // N_DEV, d)


def _rope_tables(positions):
    inv_freq = ROPE_BASE ** (-jnp.arange(0, QK_ROPE, 2, dtype=F32) / QK_ROPE)
    ang = positions.astype(F32)[:, None] * inv_freq
    cos, sin = jnp.cos(ang), jnp.sin(ang)
    z = jnp.zeros_like(cos)
    return jnp.concatenate([cos, z, cos, z], axis=-1), jnp.concatenate([-sin, z, sin, z], axis=-1)


def _mlp_up(x, gain, w1, tag):
    hn = rms_fwd(x, gain, name=f"mlp{tag}_norm")

    def act_epi(acc):
        a = jnp.maximum(acc, 0.0)
        return a, a * a

    T = x.shape[0]
    F = w1.shape[0] * w1.shape[2]
    a, act = mm(hn, w1, name=f"mlp{tag}_up", outs=[((T, F), BF, None), ((T, F), BF, None)], epi=act_epi)
    return hn, a, act


def _mlp_down(x, act, w2, tag, part=0):
    n = w2.shape[1]
    bm = _tile(x.shape[0], MM_TILE)
    bn = _tile(n, MM_TILE)
    per = n // bn
    return mm(act, w2, name=f"mlp{tag}_down{part}", out=((x.shape[0], n), F32), bm=bm, bn=bn,
              epi=lambda acc, r: (acc + r[...],), epi_ins=[(x, (bm, bn), lambda i, j, k: (i, part * per + j))])


def _mlp_bwd_weights(w1, w2, saved, dxb, tag):
    hn, a, act = saved
    T, D = dxb.shape
    F = a.shape[1]
    bm = _tile(T, MM_TILE)
    bn = _tile(F, min(MM_TILE, w1.shape[2]))
    dhid = mm(dxb, w2, tb=True, name=f"mlp{tag}_dhid", out=((T, F), BF), bm=bm, bn=bn,
              epi=lambda acc, a_ref: (2.0 * a_ref[...].astype(F32) * acc,),
              epi_ins=[(a, (bm, bn), lambda i, j, k: (i, j))])
    dw2 = mm(act, dxb, ta=True, name=f"mlp{tag}_dw2", out=((F, D), BF))
    dw1 = mm(hn, dhid, ta=True, name=f"mlp{tag}_dw1", out=(w1.shape, BF))
    return dhid, dw1, dw2.reshape(N_DEV, F // N_DEV, D)


def _reduce_begin(grads, tag):
    return sibling_start(grads, name=f"reduce_sibling_start_{tag}")


def _reduce_continue(sib, after, tag, index):
    grads, a_bufs = sibling_wait(sib, after, name=f"reduce_sibling_wait_{tag}")
    pairs = [run_job(pair_job(g, a), index=index, name=f"pair_sum_{tag}{t}")[0]
             for t, (g, a) in enumerate(zip(grads, a_bufs))]
    return grads, a_bufs, chips_start(pairs, name=f"reduce_chips_start_{tag}")


def kernel(x, positions, e_norm_mix, e_w_in, e_q_norm, e_w_uq, e_kv_norm, e_w_ukv, e_v_norm, e_sgu_w, e_sgu_b, e_mla_out_norm, e_sgu_out_norm, e_w_out, o_norm_mix, o_w_in, o_conv_w, o_w_out, mlp_norm, mlp_w1, mlp_w2, final_norm, loss_target, m_e_norm_mix, m_e_w_in, m_e_q_norm, m_e_w_uq, m_e_kv_norm, m_e_w_ukv, m_e_v_norm, m_e_sgu_w, m_e_sgu_b, m_e_mla_out_norm, m_e_sgu_out_norm, m_e_w_out, m_o_norm_mix, m_o_w_in, m_o_conv_w, m_o_w_out, m_mlp_norm, m_mlp_w1, m_mlp_w2, m_final_norm, v_e_norm_mix, v_e_w_in, v_e_q_norm, v_e_w_uq, v_e_kv_norm, v_e_w_ukv, v_e_v_norm, v_e_sgu_w, v_e_sgu_b, v_e_mla_out_norm, v_e_sgu_out_norm, v_e_w_out, v_o_norm_mix, v_o_w_in, v_o_conv_w, v_o_w_out, v_mlp_norm, v_mlp_w1, v_mlp_w2, v_final_norm):
    T, D = x.shape[1], x.shape[2]
    d_shard = o_norm_mix.shape[1]
    x0 = x[0]
    target = loss_target[0]
    me = 4 * lax.axis_index("x") + 2 * lax.axis_index("y") + lax.axis_index("c")

    bf = lambda s: s.astype(BF)
    gather_groups = [[bf(jnp.transpose(e_w_in[0])), bf(e_w_uq[0]), bf(e_w_ukv[0])], [bf(e_w_out[0]), bf(mlp_w1[0])],
                     [bf(mlp_w2[0]), bf(o_w_in[0])], [bf(o_w_out[0]), bf(mlp_w1[1])], [bf(mlp_w2[1])]]
    small_rows = jnp.concatenate([o_norm_mix, o_conv_w[0], jnp.zeros((4, d_shard), F32)], axis=0)
    small_flat = all_gather_vmem(small_rows, name="gather_small")
    started, start_token = gather_start(gather_groups[:1], small_flat, name="gather_start0")
    started_rest, rest_token = gather_start(gather_groups[1:], start_token, name="gather_start1")
    started += started_rest

    def gathered(gi, after):
        srcs, lands = gather_wait(started[gi], after, name=f"gather_wait{gi}")
        return gather_finish(srcs, lands, name=f"gather_finish{gi}")

    small_g = small_flat.reshape(N_DEV, 8, d_shard)
    o_norm_full = small_g[:, 0, :].reshape(1, D)
    conv_w_full = jnp.transpose(small_g[:, 1:4, :], (1, 0, 2)).reshape(3, D)
    w_tril = jnp.tril(e_sgu_w[0])
    w_tril_b = w_tril.astype(BF)
    w_tril_tb = jnp.swapaxes(w_tril, 1, 2).astype(BF)
    b_full = jnp.repeat(e_sgu_b[0].T, CH, axis=1)
    v_gain = e_v_norm[0].reshape(1, SGU_OUT)
    cos_t, sin_t = _rope_tables(positions[0])
    mlp_gain = [mlp_norm[0:1], mlp_norm[1:2]]
    final_gain = final_norm.reshape(1, D)

    h0 = rms_fwd(x0, e_norm_mix, name="e_norm", deps=[rest_token])
    g_w_in_t, g_w_uq, w_ukv = gathered(
        0, [h0, cos_t, sin_t, w_tril_b, w_tril_tb, b_full, o_norm_full, conv_w_full])
    w_in_t = _pack_w_in_t(g_w_in_t)
    w_uq = jnp.concatenate([g_w_uq[..., :QK_NOPE], _rope_slab(g_w_uq[..., QK_NOPE:])], axis=-1)
    proj = mm(h0, w_in_t, tb=True, name="e_in", out=((T, w_in_t.shape[0]), F32), bn=_tile(w_in_t.shape[0], 640))
    qn, kvn, krope = mla_prep(proj, e_q_norm, e_kv_norm, cos_t, sin_t, name="mla_prep")
    bm = _tile(T, MM_TILE)

    def q_epi(acc, cos_ref, sin_ref):
        return (jnp.concatenate([acc[:, :QK_NOPE], _rope_fwd(acc[:, QK_NOPE:], cos_ref[...], sin_ref[...])], axis=-1),)

    q = mm(qn, w_uq, name="mla_q", out=((T, HEADS * HEAD_PAD), BF), bm=bm, bn=HEAD_PAD, epi=q_epi,
           epi_ins=[(cos_t, (bm, LANES), lambda i, j, k: (i, 0)), (sin_t, (bm, LANES), lambda i, j, k: (i, 0))])

    def kv_epi(acc, kr_ref):
        return jnp.concatenate([acc[:, :QK_NOPE].astype(BF), kr_ref[...]], axis=-1), acc[:, QK_NOPE:]

    k, v = mm(kvn, w_ukv, name="mla_kv", bm=bm, bn=HEAD_PAD, epi=kv_epi,
              outs=[((T, HEADS * HEAD_PAD), BF, HEAD_PAD), ((T, MLA_OUT), BF, V_HEAD)],
              epi_ins=[(krope, (bm, LANES), lambda i, j, k: (i, 0))])
    attn, attn_lse = attn_fwd(q, k, v, name="attn_fwd")
    mixed = mix_fwd(attn, proj, e_mla_out_norm, e_sgu_out_norm, v_gain, w_tril_b, b_full, name="mix_fwd")
    bn = _tile(D, MM_TILE)
    g_w_out_e, w1_0 = gathered(1, [mixed])
    w_out_e = g_w_out_e.reshape(-1, D)
    x1 = mm(mixed, w_out_e, name="e_out", out=((T, D), F32), bm=bm, bn=bn,
            epi=lambda acc, r: (acc + r[...],), epi_ins=[(x0, (bm, bn), lambda i, j, k: (i, j))])
    hn0, a0, act0 = _mlp_up(x1, mlp_gain[0], w1_0, 0)
    g_w2_0, g_w_in_o = gathered(2, [act0])
    w2_0 = g_w2_0.reshape(-1, D)
    x2 = _mlp_down(x1, act0, w2_0, 0)
    ho = rms_fwd(x2, o_norm_full, name="o_norm")
    proj_o = mm(ho, g_w_in_o, name="o_in", out=((T, 3 * D), F32))
    gated = conv_fwd(proj_o, conv_w_full, name="conv_fwd")
    g_w_out_o, w1_1 = gathered(3, [gated])
    w_out_o = g_w_out_o.reshape(-1, D)
    x3 = mm(gated, w_out_o, name="o_out", out=((T, D), F32), bm=bm, bn=bn,
            epi=lambda acc, r: (acc + r[...],), epi_ins=[(x2, (bm, bn), lambda i, j, k: (i, j))])
    hn1, a1, act1 = _mlp_up(x3, mlp_gain[1], w1_1, 1)
    (g_w2_1,) = gathered(4, [act1])
    w2_1 = g_w2_1.reshape(-1, D)
    x4 = _mlp_down(x3, act1, w2_1, 1)
    w1, w2 = [w1_0, w1_1], [w2_0, w2_1]

    dx4, dx4b, d_final, loss_part = loss_bwd([x4], final_gain, target, name="loss_bwd")

    hosted = dict(job_index=device_index())
    dhid1, dw1_1, dw2_1 = _mlp_bwd_weights(w1[1], w2[1], (hn1, a1, act1), dx4b, 1)
    sib_r0 = _reduce_begin([dw1_1, dw2_1], "r0")
    dhn1 = mm(dhid1, w1[1], tb=True, name="mlp1_dhn", out=((T, D), F32), deps=[sib_r0[-1]])
    grads_r0, a_r0 = sibling_wait(sib_r0, dhn1, name="reduce_sibling_wait_r0")
    dx3, dx3b, d_mlp1 = rms_bwd(x3, mlp_gain[1], dhn1, dres=dx4, name="mlp1_norm_bwd")

    dgated, ((pair_r0a,),) = mm(dx3b, w_out_o, tb=True, name="o_out_dx", out=((T, D), F32),
                                jobs=[pair_job(grads_r0[0], a_r0[0])], **hosted)
    dw_out_o, ((pair_r0b,),) = mm(gated, dx3b, ta=True, name="o_out_dw", out=((D, D), BF),
                                  jobs=[pair_job(grads_r0[1], a_r0[1])], **hosted)
    st_r0 = chips_start([pair_r0a, pair_r0b], name="reduce_chips_start_r0")
    dproj_o, dconv_full = conv_bwd(dgated, proj_o, conv_w_full, name="conv_bwd", deps=[st_r0[-1]])
    dw_in_o = mm(ho, dproj_o, ta=True, name="o_in_dw", out=(g_w_in_o.shape, BF))
    sib_r1 = _reduce_begin([dw_out_o.reshape(g_w_out_o.shape), dw_in_o], "r1")
    dho = mm(dproj_o, g_w_in_o, tb=True, name="o_in_dx", out=((T, D), F32), deps=[sib_r1[-1]])
    grads_r1, a_r1 = sibling_wait(sib_r1, dho, name="reduce_sibling_wait_r1")
    dx2, dx2b, d_onorm_full = rms_bwd(x2, o_norm_full, dho, dres=dx3, name="o_norm_bwd")

    d_ff = a0.shape[1]
    bm_h, bn_h = _tile(T, MM_TILE), _tile(d_ff, min(MM_TILE, w1[0].shape[2]))
    dhid0, ((pair_r1a,), (pair_r1b,)) = mm(
        dx2b, w2[0], tb=True, name="mlp0_dhid", out=((T, d_ff), BF), bm=bm_h, bn=bn_h,
        epi=lambda acc, a_ref: (2.0 * a_ref[...].astype(F32) * acc,),
        epi_ins=[(a0, (bm_h, bn_h), lambda i, j, k: (i, j))],
        jobs=[pair_job(grads_r1[0], a_r1[0]), pair_job(grads_r1[1], a_r1[1])], **hosted)
    st_r1 = chips_start([pair_r1a, pair_r1b], name="reduce_chips_start_r1")
    dw2_0 = mm(act0, dx2b, ta=True, name="mlp0_dw2", out=((d_ff, D), BF), deps=[st_r1[-1]])
    b_r0 = chips_wait(st_r0, dw2_0, name="reduce_chips_wait_r0")
    dw1_0, (r_w1, r_w2) = mm(
        hn0, dhid0, ta=True, name="mlp0_dw1", out=(w1[0].shape, BF),
        jobs=[adam_job(grads_r0[0], a_r0[0], b_r0[0], mlp_w1, m_mlp_w1, v_mlp_w1, 1, None),
              adam_job(grads_r0[1], a_r0[1], b_r0[1], mlp_w2, m_mlp_w2, v_mlp_w2, 1, None)], **hosted)
    sib_r2 = _reduce_begin([dw1_0, dw2_0.reshape(N_DEV, d_ff // N_DEV, D)], "r2")
    dhn0 = mm(dhid0, w1[0], tb=True, name="mlp0_dhn", out=((T, D), F32), deps=[sib_r2[-1]])
    grads_r2, a_r2 = sibling_wait(sib_r2, dhn0, name="reduce_sibling_wait_r2")
    dx1, dx1b, d_mlp0 = rms_bwd(x1, mlp_gain[0], dhn0, dres=dx2, name="mlp0_norm_bwd")

    dmixed, ((pair_r2a,),) = mm(dx1b, w_out_e, tb=True, name="e_out_dx", out=((T, MLA_OUT + SGU_OUT), F32),
                                jobs=[pair_job(grads_r2[0], a_r2[0])], **hosted)
    dw_out_e, ((pair_r2b,),) = mm(mixed, dx1b, ta=True, name="e_out_dw", out=(w_out_e.shape, BF),
                                  jobs=[pair_job(grads_r2[1], a_r2[1])], **hosted)
    st_r2 = chips_start([pair_r2a, pair_r2b], name="reduce_chips_start_r2")
    (dattn, duv, d_mla_out, d_sgu_out, d_vgain, d_sgu_w, d_b_full) = mix_bwd(
        dmixed, attn, proj, e_mla_out_norm, e_sgu_out_norm, v_gain, w_tril_b, w_tril_tb, b_full, name="mix_bwd",
        deps=[st_r2[-1]])
    b_r1 = chips_wait(st_r1, dattn, name="reduce_chips_wait_r1")
    dq, dk, dv = attn_bwd(q, k, v, attn, attn_lse, dattn, name="attn_bwd")
    dq_lin, dkv_lin, dkr = mla_bwd_prep(dq, dk, dv, cos_t, sin_t, name="mla_bwd_prep")
    dw_uq_pad = mm(qn, dq_lin, ta=True, name="mla_q_dw", out=(w_uq.shape, BF))
    dw_ukv = mm(kvn, dkv_lin, ta=True, name="mla_kv_dw", out=(w_ukv.shape, BF))
    dw_uq = jnp.concatenate([dw_uq_pad[..., :QK_NOPE], _rope_unslab(dw_uq_pad[..., QK_NOPE:])], axis=-1)
    sib_r2b = _reduce_begin([dw_out_e.reshape(g_w_out_e.shape), dw_uq, dw_ukv], "r2b")
    dqn = mm(dq_lin, w_uq, tb=True, name="mla_q_dx", out=((T, Q_LORA), F32), deps=[sib_r2b[-1]])
    dkvn = mm(dkv_lin, w_ukv, tb=True, name="mla_kv_dx", out=((T, KV_LORA), F32), deps=[sib_r2b[-1]])
    grads_r2b, a_r2b, st_r2b = _reduce_continue(sib_r2b, dkvn, "r2b", hosted["job_index"])
    dcq, d_qnorm = rms_bwd(proj, e_q_norm, dqn, col_block=0, want_f32=False, name="q_norm_bwd", deps=[st_r2b[-1]])
    dckv, d_kvnorm = rms_bwd(proj, e_kv_norm, dkvn, col_block=1, want_f32=False, name="kv_norm_bwd")
    dproj = jnp.concatenate([dcq, dckv, duv, dkr], axis=-1)
    dw_in_t_pad, (r_w_out_o, r_w_in_o) = mm(
        dproj, h0, ta=True, name="e_in_dw", out=(w_in_t.shape, BF), bm=_tile(w_in_t.shape[0], 640),
        jobs=[adam_job(grads_r1[0], a_r1[0], b_r1[0], o_w_out, m_o_w_out, v_o_w_out, 0, None),
              adam_job(grads_r1[1], a_r1[1], b_r1[1], o_w_in, m_o_w_in, v_o_w_in, 0, None)], **hosted)
    dw_in_t = _unpack_w_in_t_grad(dw_in_t_pad)
    sib_r3 = _reduce_begin([dw_in_t], "r3")
    b_r2 = chips_wait(st_r2, sib_r3[-1], name="reduce_chips_wait_r2")
    st = _tile(D, 512)
    dh0, (r_w1, r_w2) = mm(
        dproj, w_in_t, name="e_in_dx", out=((T, D), F32), bm=st, bn=st,
        jobs=[adam_job(grads_r2[0], a_r2[0], b_r2[0], mlp_w1, m_mlp_w1, v_mlp_w1, 0, r_w1),
              adam_job(grads_r2[1], a_r2[1], b_r2[1], mlp_w2, m_mlp_w2, v_mlp_w2, 0, r_w2)], **hosted)
    grads_r3, a_r3, st_r3 = _reduce_continue(sib_r3, dh0, "r3", hosted["job_index"])
    grad_x, d_enorm = rms_bwd(x0, e_norm_mix, dh0, dres=dx1, want_bf=False, name="e_norm_bwd", deps=[st_r3[-1]])

    def finish(grads, a_bufs, b_bufs, t, w, m, v, layer=0, prev=None, tag="", deps=()):
        return run_job(adam_job(grads[t], a_bufs[t], b_bufs[t], w, m, v, layer, prev), index=hosted["job_index"],
                       name=f"adam_{tag}", deps=deps)

    b_r2b = chips_wait(st_r2b, grad_x, name="reduce_chips_wait_r2b")
    r_w_out_e = finish(grads_r2b, a_r2b, b_r2b, 0, e_w_out, m_e_w_out, v_e_w_out, tag="e_w_out")
    r_w_uq = finish(grads_r2b, a_r2b, b_r2b, 1, e_w_uq, m_e_w_uq, v_e_w_uq, tag="e_w_uq")
    r_w_ukv = finish(grads_r2b, a_r2b, b_r2b, 2, e_w_ukv, m_e_w_ukv, v_e_w_ukv, tag="e_w_ukv")
    b_r3 = chips_wait(st_r3, r_w_out_e[1], name="reduce_chips_wait_r3")
    g_w_in_t = reduce_sum(grads_r3[0], a_r3[0], b_r3[0], name="sum_e_w_in")
    w_in_upd_t = adam_rows(g_w_in_t, jnp.transpose(e_w_in[0]), jnp.transpose(m_e_w_in[0]), jnp.transpose(v_e_w_in[0]),
                           name="adam_e_w_in")
    r_w_in = [jnp.transpose(t)[None] for t in (g_w_in_t, *w_in_upd_t)]

    d_sgu_b = jnp.transpose(d_b_full[:, ::CH])
    d_sgu_w_tril = jnp.tril(d_sgu_w)
    rep = [("e_norm_mix", e_norm_mix, m_e_norm_mix, v_e_norm_mix, d_enorm),
           ("e_q_norm", e_q_norm, m_e_q_norm, v_e_q_norm, d_qnorm),
           ("e_kv_norm", e_kv_norm, m_e_kv_norm, v_e_kv_norm, d_kvnorm),
           ("e_v_norm", e_v_norm, m_e_v_norm, v_e_v_norm, d_vgain),
           ("e_sgu_w", e_sgu_w, m_e_sgu_w, v_e_sgu_w, d_sgu_w_tril),
           ("e_sgu_b", e_sgu_b, m_e_sgu_b, v_e_sgu_b, d_sgu_b),
           ("e_mla_out_norm", e_mla_out_norm, m_e_mla_out_norm, v_e_mla_out_norm, d_mla_out),
           ("e_sgu_out_norm", e_sgu_out_norm, m_e_sgu_out_norm, v_e_sgu_out_norm, d_sgu_out),
           ("mlp_norm", mlp_norm, m_mlp_norm, v_mlp_norm, jnp.concatenate([d_mlp0, d_mlp1], axis=0)),
           ("final_norm", final_norm, m_final_norm, v_final_norm, d_final)]
    sizes = [int(np.prod(r[1].shape)) for r in rep]
    n_rep = sum(sizes)
    n_all = n_rep + 4 * D + 1
    width = -(-n_all // (8 * LANES)) * LANES
    pad = 8 * width - n_all
    flat = jnp.concatenate([r[4].reshape(-1) for r in rep]
                           + [d_onorm_full.reshape(-1), dconv_full.reshape(-1), loss_part[0, :1],
                              jnp.zeros((pad,), F32)])
    summed = sum_rows8(all_gather_vmem(flat.reshape(8, width), name="gather_small_grads", after=b_r3[0]), 8,
                       name="sum_small_grads").reshape(-1)

    loss = summed[n_rep + 4 * D]

    def pack_rep(i):
        return jnp.concatenate([r[i].reshape(-1) for r in rep]).reshape(n_rep // LANES, LANES)

    g_rep = summed[:n_rep].reshape(n_rep // LANES, LANES)
    d_rep, nm_rep, nv_rep = adam_flat(g_rep, pack_rep(1), pack_rep(2), pack_rep(3), name="adam_replicated")

    def unpack_rep(flat2d):
        out, off = {}, 0
        f = flat2d.reshape(-1)
        for r, n in zip(rep, sizes):
            out[r[0]] = f[off:off + n].reshape(r[1].shape)
            off += n
        return out

    small = {"grad": unpack_rep(g_rep), "delta": unpack_rep(d_rep), "new_m": unpack_rep(nm_rep),
             "new_v": unpack_rep(nv_rep)}
    g_onorm = lax.dynamic_slice(summed[n_rep:n_rep + D].reshape(1, D), (0, me * d_shard), (1, d_shard))
    g_conv = lax.dynamic_slice(summed[n_rep + D:n_rep + 4 * D].reshape(3, D), (0, me * d_shard), (3, d_shard))

    def pack_sharded(norm_part, conv_part):
        return jnp.concatenate([norm_part, conv_part, jnp.zeros((4, d_shard), F32)], axis=0)

    g_sh = pack_sharded(g_onorm, g_conv)
    d_sh, nm_sh, nv_sh = adam_flat(g_sh, pack_sharded(o_norm_mix, o_conv_w[0]), pack_sharded(m_o_norm_mix, m_o_conv_w[0]),
                                   pack_sharded(v_o_norm_mix, v_o_conv_w[0]), name="adam_sharded_small")
    for kind, arr in (("grad", g_sh), ("delta", d_sh), ("new_m", nm_sh), ("new_v", nv_sh)):
        small[kind]["o_norm_mix"] = arr[0:1]
        small[kind]["o_conv_w"] = arr[1:4][None]

    big = {"e_w_in": r_w_in, "e_w_uq": r_w_uq, "e_w_ukv": r_w_ukv, "e_w_out": r_w_out_e, "o_w_in": r_w_in_o,
           "o_w_out": r_w_out_o, "mlp_w1": r_w1, "mlp_w2": r_w2}
    order = ["e_norm_mix", "e_w_in", "e_q_norm", "e_w_uq", "e_kv_norm", "e_w_ukv", "e_v_norm", "e_sgu_w", "e_sgu_b",
             "e_mla_out_norm", "e_sgu_out_norm", "e_w_out", "o_norm_mix", "o_w_in", "o_conv_w", "o_w_out", "mlp_norm",
             "mlp_w1", "mlp_w2", "final_norm"]
    result = [loss, grad_x[None]]
    for ki, kind in enumerate(("grad", "delta", "new_m", "new_v")):
        for nm in order:
            result.append(big[nm][ki] if nm in big else small[kind][nm])
    return tuple(result)
```

```python
import numpy as np
import jax
import jax.numpy as jnp
from jax import lax
from jax.experimental import pallas as pl
from jax.experimental.pallas import tpu as pltpu

BF = jnp.bfloat16
F32 = jnp.float32
MESH = pl.DeviceIdType.MESH
N_DEV = 8

EPS = 1e-6
HEADS = 8
Q_LORA = 512
KV_LORA = 512
QK_NOPE = 128
QK_ROPE = 64
HALF_ROPE = QK_ROPE // 2
V_HEAD = 128
HEAD_PAD = 256
ROPE_BASE = 10000.0
GROUPS = 8
CH = 128
CHUNK = 128
SGU_OUT = GROUPS * CH
MLA_OUT = HEADS * V_HEAD
ATTN_SCALE = float((QK_NOPE + QK_ROPE) ** -0.5)

ADAM_LR = 0.001
ADAM_B1 = 0.9
ADAM_B2 = 0.999
ADAM_EPS = 1e-08
ADAM_WD = 0.01
ADAM_STEP = 10
ADAM_C1 = 1.0 - ADAM_B1 ** ADAM_STEP
ADAM_C2 = 1.0 - ADAM_B2 ** ADAM_STEP

V7X_VMEM_BYTES = 64 * 2 ** 20
VMEM_LIMIT_CAP = V7X_VMEM_BYTES - 6 * 2 ** 20
LANES = 128
ROW_TILE = 256
ATTN_TILE = 512
STREAM_BLOCK_ELEMS = 512 * 1024
MM_TILE = 1024
MM_NARROW_TILE = 512
MM_K_TILE = 2048
MM_K_BLOCK_MAX = 3072


def _padded_bytes(block, dtype):
    dims = [d for d in block if d is not None]
    if len(dims) >= 1:
        dims[-1] = -(-dims[-1] // LANES) * LANES
    if len(dims) >= 2:
        dims[-2] = -(-dims[-2] // 16) * 16
    return int(np.prod(dims)) * jnp.dtype(dtype).itemsize


def _pcall(body, *, name, grid, ins, outs, scratch=(), semantics=None, aliases=None, prefetch=None, deps=()):
    any_spec = pl.BlockSpec(memory_space=pl.ANY)
    if deps:
        n_lead = len(ins) + (1 if prefetch is not None else 0)
        n_deps = len(deps)
        inner = body

        def body(*refs):
            inner(*refs[:n_lead], *refs[n_lead + n_deps:])

        ins = list(ins) + [(d, None, None) for d in deps]
    in_specs = [any_spec if b is None else pl.BlockSpec(b, m) for _, b, m in ins]
    out_specs = [any_spec if b is None else pl.BlockSpec(b, m) for _, _, b, m in outs]
    out_shape = [pltpu.HBM(s, d) for s, d, _, _ in outs]
    est = 0
    for a, b, _ in ins:
        if b is not None:
            est += 2 * _padded_bytes(b, a.dtype)
    for _, d, b, _ in outs:
        if b is not None:
            est += 2 * _padded_bytes(b, d)
    for s in scratch:
        if hasattr(s, "shape") and hasattr(s, "dtype"):
            est += _padded_bytes(s.shape, s.dtype)
    limit = int(min(VMEM_LIMIT_CAP, est + 16 * 2 ** 20))
    params = pltpu.CompilerParams(
        dimension_semantics=semantics or ("arbitrary",) * len(grid), vmem_limit_bytes=limit)
    args = [pltpu.with_memory_space_constraint(a, pltpu.HBM) for a, _, _ in ins]
    if prefetch is not None:
        grid_spec = pltpu.PrefetchScalarGridSpec(
            num_scalar_prefetch=1, grid=grid, in_specs=in_specs, out_specs=out_specs, scratch_shapes=list(scratch))
        call = pl.pallas_call(body, out_shape=out_shape, grid_spec=grid_spec, name=name, compiler_params=params,
                              input_output_aliases=aliases or {})
        return call(prefetch, *args)
    call = pl.pallas_call(body, out_shape=out_shape, grid=grid, in_specs=in_specs, out_specs=out_specs,
                          scratch_shapes=list(scratch), name=name, compiler_params=params,
                          input_output_aliases=aliases or {})
    return call(*args)


def _tile(dim, pref, quantum=LANES):
    if dim <= pref:
        return dim
    t = (pref // quantum) * quantum
    while t >= quantum:
        if dim % t == 0:
            return t
        t -= quantum
    return dim


def _vshape(arr_shape):
    if len(arr_shape) == 2:
        return tuple(arr_shape)
    s, r, c = arr_shape
    return (r, s * c)


def _vblock(arr_shape, br, bc, rc):
    if len(arr_shape) == 2:
        return (br, bc), (lambda *g: rc(*g))
    _, _, c = arr_shape
    assert c % bc == 0, (arr_shape, bc)
    per = c // bc

    def imap(*g):
        ri, ci = rc(*g)
        return (ci // per, ri, ci % per)

    return (None, br, bc), imap


def _shard_width(*shapes):
    w = None
    for s in shapes:
        if len(s) == 3:
            w = s[2] if w is None else int(np.gcd(w, s[2]))
    return w


def mm(a, b, *, name, ta=False, tb=False, out=None, outs=None, epi=None, epi_ins=(), bm=None, bn=None, bk=None,
       deps=(), jobs=(), job_index=None):
    av, bv = _vshape(a.shape), _vshape(b.shape)
    M, K = (av[1], av[0]) if ta else av
    K2, N = (bv[1], bv[0]) if tb else bv
    assert K == K2, (a.shape, b.shape, ta, tb)
    if outs is None:
        outs = [(out[0], out[1], None)]
    a_sw = _shard_width(a.shape)
    b_sw = _shard_width(b.shape)
    o_sw = _shard_width(*[o[0] for o in outs])
    m_lim = a_sw if (ta and a_sw) else None
    k_lim = [w for w in ((a_sw if not ta else None), (b_sw if tb else None)) if w]
    n_lim = [w for w in ((b_sw if not tb else None), o_sw) if w]
    if bm is None:
        bm = _tile(M, min([MM_TILE] + ([m_lim] if m_lim else [])))
    if bn is None:
        bn = _tile(N, min([MM_TILE] + n_lim))
    k_shards = 0
    if tb and len(b.shape) == 3 and bk is None and not (a_sw and not ta):
        k_shards = 1
        while 2 * k_shards <= b.shape[0] and 2 * k_shards * b_sw <= MM_K_BLOCK_MAX:
            k_shards *= 2
        bk = k_shards * b_sw
    if bk is None:
        bk = K if (K <= 4096 and not k_lim) else _tile(K, min([MM_K_TILE] + k_lim))
    assert M % bm == 0 and N % bn == 0 and K % bk == 0, (name, M, N, K, bm, bn, bk)
    nk = K // bk
    grid = (M // bm, N // bn, nk)
    if ta:
        a_blk, a_map = _vblock(a.shape, bk, bm, lambda i, j, k: (k, i))
    else:
        a_blk, a_map = _vblock(a.shape, bm, bk, lambda i, j, k: (i, k))
    if k_shards:
        b_blk, b_map = (k_shards, bn, b_sw), (lambda i, j, k: (k, j, 0))
    elif tb:
        b_blk, b_map = _vblock(b.shape, bn, bk, lambda i, j, k: (j, k))
    else:
        b_blk, b_map = _vblock(b.shape, bk, bn, lambda i, j, k: (k, j))
    dn = (((0 if ta else 1,), (1 if tb else 0,)), ((), ()))
    ins = [(a, a_blk, a_map), (b, b_blk, b_map)] + list(epi_ins)
    out_list = []
    for shape, dtype, cols in outs:
        cols = cols or bn
        blk, imap = _vblock(shape, bm, cols, lambda i, j, k: (i, j))
        out_list.append((shape, dtype, blk, imap))
    n_e, n_o = len(epi_ins), len(out_list)

    n_steps = grid[0] * grid[1] * nk
    built = [job(n_steps) for job in jobs]
    aliases = {}
    job_slices = []
    if built:
        def lin(i, j, k):
            return (i * grid[1] + j) * nk + k

        ins = [(arr, blk, None if blk is None else (lambda i, j, k, s, f=f: f(i, j, k))) for arr, blk, f in ins]
        out_list = [(sh, dt, blk, (lambda i, j, k, s, f=f: f(i, j, k))) for sh, dt, blk, f in out_list]
        n_main_in, n_main_out = len(ins), len(out_list)
        for jb in built:
            i0, o0 = len(ins), len(out_list)
            ins += [(arr, blk, None if blk is None else (lambda i, j, k, s, f=f: f(lin(i, j, k), s)))
                    for arr, blk, f in jb["ins"]]
            out_list += [(sh, dt, blk, (lambda i, j, k, s, f=f: f(lin(i, j, k), s))) for sh, dt, blk, f in jb["outs"]]
            aliases.update({1 + i0 + ai: o0 + ao for ai, ao in jb["aliases"].items()})
            job_slices.append((i0, len(jb["ins"]), o0, len(jb["outs"])))
    n_in_total = len(ins)

    def body(*refs):
        if built:
            refs = refs[1:]
        a_ref, b_ref = refs[0], refs[1]
        e_refs = refs[2:2 + n_e]
        o_refs = refs[n_in_total:n_in_total + n_o]
        for jb, (i0, ni, o0, no) in zip(built, job_slices):
            jb["fn"](refs[i0:i0 + ni], refs[n_in_total + o0:n_in_total + o0 + no])

        def finish(acc):
            res = epi(acc, *e_refs) if epi is not None else (acc,)
            for o_ref, r in zip(o_refs, res):
                o_ref[...] = r.astype(o_ref.dtype)

        x = a_ref[...].astype(BF)
        y = b_ref[...].astype(BF)
        if k_shards:
            p = None
            for s in range(k_shards):
                part = lax.dot_general(x[:, s * b_sw:(s + 1) * b_sw], y[s], dn, preferred_element_type=F32)
                p = part if p is None else p + part
        else:
            p = lax.dot_general(x, y, dn, preferred_element_type=F32)
        if nk == 1:
            finish(p)
        else:
            acc_ref = refs[-1]
            k = pl.program_id(2)

            @pl.when(k == 0)
            def _():
                acc_ref[...] = p

            @pl.when(k > 0)
            def _():
                acc_ref[...] += p

            @pl.when(k == nk - 1)
            def _():
                finish(acc_ref[...])

    scratch = [pltpu.VMEM((bm, bn), F32)] if nk > 1 else []
    res = _pcall(body, name=name, grid=grid, ins=ins, outs=out_list, scratch=scratch, deps=deps,
                 semantics=("parallel", "parallel", "arbitrary"), prefetch=job_index if built else None, aliases=aliases)
    main = res[0] if n_o == 1 else res[:n_o]
    if not built:
        return main
    return main, [res[o0:o0 + no] for _, _, o0, no in job_slices]


_GELU_K = float(np.sqrt(2.0 / np.pi))
_GELU_C = 0.044715


def _gelu(x):
    t = jnp.tanh(_GELU_K * (x + _GELU_C * (x * x * x)))
    return 0.5 * x * (1.0 + t)


def _gelu_grad(x):
    t = jnp.tanh(_GELU_K * (x + _GELU_C * (x * x * x)))
    return 0.5 * (1.0 + t) + 0.5 * x * (1.0 - t * t) * (_GELU_K * (1.0 + 3.0 * _GELU_C * (x * x)))


def _rstd(x):
    return lax.rsqrt(jnp.mean(x * x, axis=-1, keepdims=True) + EPS)


def _rms_bwd(x, gain, dy):
    r = _rstd(x)
    xh = x * r
    gdy = dy * gain
    dx = r * (gdy - xh * jnp.mean(gdy * xh, axis=-1, keepdims=True))
    return dx, dy * xh


def _rope_fwd(x, cos_t, sin_t):
    return x * cos_t + pltpu.roll(x, 2 * HALF_ROPE, 1) * sin_t


def _rope_bwd(dy, cos_t, sin_t):
    return dy * cos_t + pltpu.roll(dy * sin_t, 2 * HALF_ROPE, 1)


def _acc_rows(ref, val, first):
    s = jnp.sum(val, axis=0, keepdims=True)

    @pl.when(first)
    def _():
        ref[...] = s

    @pl.when(jnp.logical_not(first))
    def _():
        ref[...] += s


def rms_fwd(x, gain, *, name, col_block=0, width=None, deps=()):
    T = x.shape[0]
    width = width or x.shape[1]
    tm = _tile(T, ROW_TILE, 8)

    def body(x_ref, g_ref, o_ref):
        v = x_ref[...]
        o_ref[...] = (v * _rstd(v) * g_ref[...]).astype(BF)

    return _pcall(body, name=name, grid=(T // tm,),
                  ins=[(x, (tm, width), lambda i: (i, col_block)), (gain, (1, width), lambda i: (0, 0))],
                  outs=[((T, width), BF, (tm, width), lambda i: (i, 0))], semantics=("parallel",), deps=deps)[0]


def rms_bwd(x, gain, dy, *, name, col_block=0, dres=None, want_f32=True, want_bf=True, deps=()):
    T, width = dy.shape
    tm = _tile(T, ROW_TILE, 8)
    has_res = dres is not None

    def body(*refs):
        x_ref, g_ref, dy_ref = refs[:3]
        pos = 3
        res_ref = None
        if has_res:
            res_ref = refs[pos]
            pos += 1
        outs = refs[pos:]
        dx, dg_rows = _rms_bwd(x_ref[...], g_ref[...], dy_ref[...])
        if has_res:
            dx = dx + res_ref[...]
        o = 0
        if want_f32:
            outs[o][...] = dx
            o += 1
        if want_bf:
            outs[o][...] = dx.astype(BF)
            o += 1
        _acc_rows(outs[o], dg_rows, pl.program_id(0) == 0)

    ins = [(x, (tm, width), lambda i: (i, col_block)), (gain, (1, width), lambda i: (0, 0)),
           (dy, (tm, width), lambda i: (i, 0))]
    if has_res:
        ins.append((dres, (tm, width), lambda i: (i, 0)))
    outs = []
    if want_f32:
        outs.append(((T, width), F32, (tm, width), lambda i: (i, 0)))
    if want_bf:
        outs.append(((T, width), BF, (tm, width), lambda i: (i, 0)))
    outs.append(((1, width), F32, (1, width), lambda i: (0, 0)))
    return _pcall(body, name=name, grid=(T // tm,), ins=ins, outs=outs, deps=deps)


def mla_prep(proj, q_norm, kv_norm, cos_t, sin_t, *, name):
    T = proj.shape[0]
    tm = _tile(T, ROW_TILE, 8)
    kr_block = (proj.shape[1] - LANES) // LANES

    def body(cq_ref, ckv_ref, kr_ref, qg_ref, kg_ref, cos_ref, sin_ref, qn_ref, kvn_ref, krope_ref):
        cq = cq_ref[...]
        qn_ref[...] = (cq * _rstd(cq) * qg_ref[...]).astype(BF)
        ckv = ckv_ref[...]
        kvn_ref[...] = (ckv * _rstd(ckv) * kg_ref[...]).astype(BF)
        krope_ref[...] = _rope_fwd(kr_ref[...], cos_ref[...], sin_ref[...]).astype(BF)

    return _pcall(
        body, name=name, grid=(T // tm,),
        ins=[(proj, (tm, Q_LORA), lambda i: (i, 0)), (proj, (tm, KV_LORA), lambda i: (i, 1)),
             (proj, (tm, LANES), lambda i: (i, kr_block)),
             (q_norm, (1, Q_LORA), lambda i: (0, 0)), (kv_norm, (1, KV_LORA), lambda i: (0, 0)),
             (cos_t, (tm, LANES), lambda i: (i, 0)), (sin_t, (tm, LANES), lambda i: (i, 0))],
        outs=[((T, Q_LORA), BF, (tm, Q_LORA), lambda i: (i, 0)), ((T, KV_LORA), BF, (tm, KV_LORA), lambda i: (i, 0)),
              ((T, LANES), BF, (tm, LANES), lambda i: (i, 0))],
        semantics=("parallel",))


def _attn_scores(q, k_blk, diagonal):
    s = lax.dot_general(q, k_blk, (((1,), (1,)), ((), ())), preferred_element_type=F32) * ATTN_SCALE
    if diagonal:
        row = lax.broadcasted_iota(jnp.int32, s.shape, 0)
        col = lax.broadcasted_iota(jnp.int32, s.shape, 1)
        s = jnp.where(col <= row, s, -jnp.inf)
    return s


def attn_fwd(q, k, v, *, name):
    T = q.shape[0]
    tq = _tile(T, ATTN_TILE, 8)

    def body(q_ref, k_ref, v_ref, o_ref, lse_ref):
        i = pl.program_id(1)
        qv = q_ref[...]

        def block(kb, carry, diagonal):
            m, l, acc = carry
            start = pl.multiple_of(kb * tq, tq)
            s = _attn_scores(qv, k_ref[pl.ds(start, tq), :], diagonal)
            m_new = jnp.maximum(m, jnp.max(s, axis=-1, keepdims=True))
            alpha = jnp.exp(m - m_new)
            p = jnp.exp(s - m_new)
            l = alpha * l + jnp.sum(p, axis=-1, keepdims=True)
            acc = alpha * acc + jnp.dot(p.astype(BF), v_ref[pl.ds(start, tq), :], preferred_element_type=F32)
            return m_new, l, acc

        init = (jnp.full((tq, 1), -jnp.inf, F32), jnp.zeros((tq, 1), F32), jnp.zeros((tq, V_HEAD), F32))
        carry = lax.fori_loop(0, i, lambda kb, c: block(kb, c, False), init)
        m, l, acc = block(i, carry, True)
        o_ref[...] = acc / l
        lse_ref[...] = jnp.broadcast_to(m + jnp.log(l), (tq, V_HEAD))

    return _pcall(
        body, name=name, grid=(HEADS, T // tq),
        ins=[(q, (tq, HEAD_PAD), lambda h, i: (i, h)), (k, (T, HEAD_PAD), lambda h, i: (0, h)),
             (v, (T, V_HEAD), lambda h, i: (0, h))],
        outs=[((T, MLA_OUT), F32, (tq, V_HEAD), lambda h, i: (i, h)),
              ((T, MLA_OUT), F32, (tq, V_HEAD), lambda h, i: (i, h))], semantics=("parallel", "parallel"))


def attn_bwd(q, k, v, o, lse, do, *, name):
    T = q.shape[0]
    tq = _tile(T, ATTN_TILE, 8)

    def body(q_ref, k_ref, v_ref, o_ref, lse_ref, do_ref, dq_ref, dk_ref, dv_ref):
        i = pl.program_id(1)

        @pl.when(i == 0)
        def _():
            dk_ref[...] = jnp.zeros_like(dk_ref)
            dv_ref[...] = jnp.zeros_like(dv_ref)

        qv = q_ref[...]
        do_t = do_ref[...]
        lse_v = lse_ref[:, 0:1]
        delta = jnp.sum(do_t.astype(F32) * o_ref[...], axis=-1, keepdims=True)

        def block(kb, dq, diagonal):
            start = pl.multiple_of(kb * tq, tq)
            k_blk = k_ref[pl.ds(start, tq), :]
            v_blk = v_ref[pl.ds(start, tq), :]
            p = jnp.exp(_attn_scores(qv, k_blk, diagonal) - lse_v)
            dp = lax.dot_general(do_t, v_blk, (((1,), (1,)), ((), ())), preferred_element_type=F32)
            ds = (p * (dp - delta) * ATTN_SCALE).astype(BF)
            dk_ref[pl.ds(start, tq), :] += lax.dot_general(ds, qv, (((0,), (0,)), ((), ())), preferred_element_type=F32)
            dv_ref[pl.ds(start, tq), :] += lax.dot_general(p.astype(BF), do_t, (((0,), (0,)), ((), ())),
                                                          preferred_element_type=F32)
            return dq + jnp.dot(ds, k_blk, preferred_element_type=F32)

        dq = lax.fori_loop(0, i, lambda kb, c: block(kb, c, False), jnp.zeros((tq, HEAD_PAD), F32))
        dq_ref[...] = block(i, dq, True)

    return _pcall(
        body, name=name, grid=(HEADS, T // tq),
        ins=[(q, (tq, HEAD_PAD), lambda h, i: (i, h)), (k, (T, HEAD_PAD), lambda h, i: (0, h)),
             (v, (T, V_HEAD), lambda h, i: (0, h)), (o, (tq, V_HEAD), lambda h, i: (i, h)),
             (lse, (tq, V_HEAD), lambda h, i: (i, h)), (do, (tq, V_HEAD), lambda h, i: (i, h))],
        outs=[((T, HEADS * HEAD_PAD), F32, (tq, HEAD_PAD), lambda h, i: (i, h)),
              ((T, HEADS * HEAD_PAD), F32, (T, HEAD_PAD), lambda h, i: (0, h)),
              ((T, MLA_OUT), F32, (T, V_HEAD), lambda h, i: (0, h))],
        semantics=("parallel", "arbitrary"))


def mla_bwd_prep(dq, dk, dv, cos_t, sin_t, *, name):
    T = dq.shape[0]
    tm = _tile(T, ROW_TILE, 8)

    def body(dq_ref, dk_ref, dv_ref, cos_ref, sin_ref, dql_ref, dkvl_ref, dkr_ref):
        cos_v, sin_v = cos_ref[...], sin_ref[...]
        kr = jnp.zeros((tm, LANES), F32)
        for h in range(HEADS):
            lo = h * HEAD_PAD
            dql_ref[:, lo:lo + QK_NOPE] = dq_ref[:, lo:lo + QK_NOPE].astype(BF)
            dql_ref[:, lo + QK_NOPE:lo + HEAD_PAD] = _rope_bwd(
                dq_ref[:, lo + QK_NOPE:lo + HEAD_PAD], cos_v, sin_v).astype(BF)
            dkvl_ref[:, lo:lo + QK_NOPE] = dk_ref[:, lo:lo + QK_NOPE].astype(BF)
            dkvl_ref[:, lo + QK_NOPE:lo + HEAD_PAD] = dv_ref[:, h * V_HEAD:(h + 1) * V_HEAD].astype(BF)
            kr = kr + dk_ref[:, lo + QK_NOPE:lo + HEAD_PAD]
        dkr_ref[...] = _rope_bwd(kr, cos_v, sin_v).astype(BF)

    W = HEADS * HEAD_PAD
    return _pcall(
        body, name=name, grid=(T // tm,),
        ins=[(dq, (tm, W), lambda i: (i, 0)), (dk, (tm, W), lambda i: (i, 0)), (dv, (tm, MLA_OUT), lambda i: (i, 0)),
             (cos_t, (tm, LANES), lambda i: (i, 0)), (sin_t, (tm, LANES), lambda i: (i, 0))],
        outs=[((T, W), BF, (tm, W), lambda i: (i, 0)), ((T, W), BF, (tm, W), lambda i: (i, 0)),
              ((T, LANES), BF, (tm, LANES), lambda i: (i, 0))],
        semantics=("parallel",))


def _group_norm_stats(vg):
    mu = jnp.mean(vg, axis=-1, keepdims=True)
    d = vg - mu
    r = lax.rsqrt(jnp.mean(d * d, axis=-1, keepdims=True) + EPS)
    return d * r, r


def mix_fwd(a, proj, g_mla, g_sgu, v_gain, w_tril, b_full, *, name):
    T = a.shape[0]
    tm = _tile(T, ROW_TILE, CHUNK)
    n_chunk = tm // CHUNK

    def body(a_ref, u_ref, v_ref, gm_ref, gs_ref, vg_ref, w_ref, b_ref, o_ref, s_scr):
        av = a_ref[...]
        o_ref[:, :MLA_OUT] = (av * _rstd(av) * gm_ref[...]).astype(BF)
        for g in range(GROUPS):
            sl = slice(g * CH, (g + 1) * CH)
            vhat, _ = _group_norm_stats(_gelu(v_ref[:, sl]))
            vn = (vhat * vg_ref[:, sl]).astype(BF)
            u = _gelu(u_ref[:, sl])
            for ci in range(n_chunk):
                rs = slice(ci * CHUNK, (ci + 1) * CHUNK)
                y = jnp.dot(w_ref[g], vn[rs], preferred_element_type=F32) + b_ref[:, sl]
                s_scr[rs, sl] = u[rs] * y
        s = s_scr[...]
        o_ref[:, MLA_OUT:] = (s * _rstd(s) * gs_ref[...]).astype(BF)

    return _pcall(
        body, name=name, grid=(T // tm,),
        ins=[(a, (tm, MLA_OUT), lambda i: (i, 0)), (proj, (tm, SGU_OUT), lambda i: (i, 1)),
             (proj, (tm, SGU_OUT), lambda i: (i, 2)), (g_mla, (1, MLA_OUT), lambda i: (0, 0)),
             (g_sgu, (1, SGU_OUT), lambda i: (0, 0)), (v_gain, (1, SGU_OUT), lambda i: (0, 0)),
             (w_tril, (GROUPS, CHUNK, CHUNK), lambda i: (0, 0, 0)), (b_full, (CHUNK, SGU_OUT), lambda i: (0, 0))],
        outs=[((T, MLA_OUT + SGU_OUT), BF, (tm, MLA_OUT + SGU_OUT), lambda i: (i, 0))],
        scratch=[pltpu.VMEM((tm, SGU_OUT), F32)], semantics=("parallel",))[0]


def mix_bwd(dmixed, a, proj, g_mla, g_sgu, v_gain, w_tril, w_tril_t, b_full, *, name, deps=()):
    T = a.shape[0]
    tm = _tile(T, ROW_TILE, CHUNK)
    n_chunk = tm // CHUNK

    def body(dm_a_ref, dm_s_ref, a_ref, u_ref, v_ref, gm_ref, gs_ref, vg_ref, w_ref, wt_ref, b_ref,
             da_ref, duv_ref, dgm_ref, dgs_ref, dvg_ref, dw_ref, db_ref, s_scr, y_scr):
        first = pl.program_id(0) == 0
        da, dgm_rows = _rms_bwd(a_ref[...], gm_ref[...], dm_a_ref[...])
        da_ref[...] = da.astype(BF)
        _acc_rows(dgm_ref, dgm_rows, first)

        for g in range(GROUPS):
            sl = slice(g * CH, (g + 1) * CH)
            vhat, _ = _group_norm_stats(_gelu(v_ref[:, sl]))
            vn = (vhat * vg_ref[:, sl]).astype(BF)
            u = _gelu(u_ref[:, sl])
            for ci in range(n_chunk):
                rs = slice(ci * CHUNK, (ci + 1) * CHUNK)
                y = jnp.dot(w_ref[g], vn[rs], preferred_element_type=F32) + b_ref[:, sl]
                y_scr[rs, sl] = y
                s_scr[rs, sl] = u[rs] * y
        ds, dgs_rows = _rms_bwd(s_scr[...], gs_ref[...], dm_s_ref[...])
        _acc_rows(dgs_ref, dgs_rows, first)
        s_scr[...] = ds

        @pl.when(first)
        def _():
            dw_ref[...] = jnp.zeros_like(dw_ref)
            db_ref[...] = jnp.zeros_like(db_ref)

        for g in range(GROUPS):
            sl = slice(g * CH, (g + 1) * CH)
            upre = u_ref[:, sl]
            vpre = v_ref[:, sl]
            u = _gelu(upre)
            vhat, r = _group_norm_stats(_gelu(vpre))
            gain = vg_ref[:, sl]
            vn = (vhat * gain).astype(BF)
            dsg = s_scr[:, sl]
            duv_ref[:, sl] = (dsg * y_scr[:, sl] * _gelu_grad(upre)).astype(BF)
            dy = dsg * u
            dyb = dy.astype(BF)
            dvn_parts = []
            for ci in range(n_chunk):
                rs = slice(ci * CHUNK, (ci + 1) * CHUNK)
                dvn_parts.append(jnp.dot(wt_ref[g], dyb[rs], preferred_element_type=F32))
                dw_ref[g] += lax.dot_general(dyb[rs], vn[rs], (((1,), (1,)), ((), ())), preferred_element_type=F32)
                db_ref[:, sl] += jnp.broadcast_to(jnp.sum(dy[rs], axis=-1, keepdims=True), (CHUNK, CH))
            dvn = dvn_parts[0] if n_chunk == 1 else jnp.concatenate(dvn_parts, axis=0)
            _acc_rows(dvg_ref.at[:, sl], dvn * vhat, first)
            dvh = dvn * gain
            dvg = r * (dvh - jnp.mean(dvh, axis=-1, keepdims=True)
                       - vhat * jnp.mean(dvh * vhat, axis=-1, keepdims=True))
            duv_ref[:, SGU_OUT + g * CH:SGU_OUT + (g + 1) * CH] = (dvg * _gelu_grad(vpre)).astype(BF)

    return _pcall(
        body, name=name, grid=(T // tm,),
        ins=[(dmixed, (tm, MLA_OUT), lambda i: (i, 0)), (dmixed, (tm, SGU_OUT), lambda i: (i, 1)),
             (a, (tm, MLA_OUT), lambda i: (i, 0)), (proj, (tm, SGU_OUT), lambda i: (i, 1)),
             (proj, (tm, SGU_OUT), lambda i: (i, 2)), (g_mla, (1, MLA_OUT), lambda i: (0, 0)),
             (g_sgu, (1, SGU_OUT), lambda i: (0, 0)), (v_gain, (1, SGU_OUT), lambda i: (0, 0)),
             (w_tril, (GROUPS, CHUNK, CHUNK), lambda i: (0, 0, 0)), (w_tril_t, (GROUPS, CHUNK, CHUNK), lambda i: (0, 0, 0)),
             (b_full, (CHUNK, SGU_OUT), lambda i: (0, 0))],
        outs=[((T, MLA_OUT), BF, (tm, MLA_OUT), lambda i: (i, 0)),
              ((T, 2 * SGU_OUT), BF, (tm, 2 * SGU_OUT), lambda i: (i, 0)),
              ((1, MLA_OUT), F32, (1, MLA_OUT), lambda i: (0, 0)), ((1, SGU_OUT), F32, (1, SGU_OUT), lambda i: (0, 0)),
              ((1, SGU_OUT), F32, (1, SGU_OUT), lambda i: (0, 0)),
              ((GROUPS, CHUNK, CHUNK), F32, (GROUPS, CHUNK, CHUNK), lambda i: (0, 0, 0)),
              ((CHUNK, SGU_OUT), F32, (CHUNK, SGU_OUT), lambda i: (0, 0))],
        scratch=[pltpu.VMEM((tm, SGU_OUT), F32), pltpu.VMEM((tm, SGU_OUT), F32)], deps=deps)


def _shift_down(z, n, row):
    return jnp.where(row >= n, pltpu.roll(z, n, 0), 0.0)


def _shift_up(z, n, row, T):
    return jnp.where(row < T - n, pltpu.roll(z, T - n, 0), 0.0)


def conv_fwd(proj, conv_w, *, name):
    T, D3 = proj.shape
    D = D3 // 3
    tn = _tile(D, 256)
    nj = D // tn

    def body(b_ref, c_ref, x_ref, w_ref, o_ref):
        row = lax.broadcasted_iota(jnp.int32, (T, tn), 0)
        z = c_ref[...] * x_ref[...]
        zc = w_ref[2:3, :] * z + w_ref[1:2, :] * _shift_down(z, 1, row) + w_ref[0:1, :] * _shift_down(z, 2, row)
        o_ref[...] = (b_ref[...] * zc).astype(BF)

    return _pcall(
        body, name=name, grid=(nj,),
        ins=[(proj, (T, tn), lambda j: (0, j)), (proj, (T, tn), lambda j: (0, nj + j)),
             (proj, (T, tn), lambda j: (0, 2 * nj + j)), (conv_w, (3, tn), lambda j: (0, j))],
        outs=[((T, D), BF, (T, tn), lambda j: (0, j))], semantics=("parallel",))[0]


def conv_bwd(dg, proj, conv_w, *, name, deps=()):
    T, D3 = proj.shape
    D = D3 // 3
    tn = _tile(D, 256)
    nj = D // tn

    def body(dg_ref, b_ref, c_ref, x_ref, w_ref, dp_ref, dw_ref, dc_scr, dx_scr):
        part = pl.program_id(1)

        @pl.when(part == 0)
        def _():
            row = lax.broadcasted_iota(jnp.int32, (T, tn), 0)
            c, x = c_ref[...], x_ref[...]
            z = c * x
            z1 = _shift_down(z, 1, row)
            z2 = _shift_down(z, 2, row)
            dgv = dg_ref[...]
            zc = w_ref[2:3, :] * z + w_ref[1:2, :] * z1 + w_ref[0:1, :] * z2
            dp_ref[...] = (dgv * zc).astype(BF)
            dzc = dgv * b_ref[...]
            dw_ref[0:1, :] = jnp.sum(dzc * z2, axis=0, keepdims=True)
            dw_ref[1:2, :] = jnp.sum(dzc * z1, axis=0, keepdims=True)
            dw_ref[2:3, :] = jnp.sum(dzc * z, axis=0, keepdims=True)
            dz = (w_ref[2:3, :] * dzc + w_ref[1:2, :] * _shift_up(dzc, 1, row, T)
                  + w_ref[0:1, :] * _shift_up(dzc, 2, row, T))
            dc_scr[...] = (dz * x).astype(BF)
            dx_scr[...] = (dz * c).astype(BF)

        @pl.when(part == 1)
        def _():
            dp_ref[...] = dc_scr[...]

        @pl.when(part == 2)
        def _():
            dp_ref[...] = dx_scr[...]

    return _pcall(
        body, name=name, grid=(nj, 3),
        ins=[(dg, (T, tn), lambda j, p: (0, j)), (proj, (T, tn), lambda j, p: (0, j)),
             (proj, (T, tn), lambda j, p: (0, nj + j)), (proj, (T, tn), lambda j, p: (0, 2 * nj + j)),
             (conv_w, (3, tn), lambda j, p: (0, j))],
        outs=[((T, D3), BF, (T, tn), lambda j, p: (0, p * nj + j)), ((3, D), F32, (3, tn), lambda j, p: (0, j))],
        scratch=[pltpu.VMEM((T, tn), BF), pltpu.VMEM((T, tn), BF)], semantics=("parallel", "arbitrary"), deps=deps)


def loss_bwd(x_parts, gain, target, *, name):
    T, D = target.shape
    tm = _tile(T, ROW_TILE, 8)
    n_x = len(x_parts)

    def body(*refs):
        x_refs = refs[:n_x]
        g_ref, t_ref, dx_ref, dxb_ref, dg_ref, loss_ref = refs[n_x:]
        first = pl.program_id(0) == 0
        xv = jnp.concatenate([r[...] for r in x_refs], axis=-1) if n_x > 1 else x_refs[0][...]
        r = _rstd(xv)
        xh = xv * r
        gain_v = g_ref[...]
        err = xh * gain_v - t_ref[...]
        part = 0.5 * jnp.sum(jnp.mean(err * err, axis=-1, keepdims=True), axis=0, keepdims=True)
        _acc_rows(loss_ref, jnp.broadcast_to(part, (1, LANES)), first)
        dy = err * (1.0 / D)
        gdy = dy * gain_v
        dx = r * (gdy - xh * jnp.mean(gdy * xh, axis=-1, keepdims=True))
        dx_ref[...] = dx
        dxb_ref[...] = dx.astype(BF)
        _acc_rows(dg_ref, dy * xh, first)

    return _pcall(
        body, name=name, grid=(T // tm,),
        ins=[(p, (tm, D // n_x), lambda i: (i, 0)) for p in x_parts]
        + [(gain, (1, D), lambda i: (0, 0)), (target, (tm, D), lambda i: (i, 0))],
        outs=[((T, D), F32, (tm, D), lambda i: (i, 0)), ((T, D), BF, (tm, D), lambda i: (i, 0)),
              ((1, D), F32, (1, D), lambda i: (0, 0)), ((1, LANES), F32, (1, LANES), lambda i: (0, 0))])


def _adamw(g, w, m, v):
    m = ADAM_B1 * m + (1.0 - ADAM_B1) * g
    v = ADAM_B2 * v + (1.0 - ADAM_B2) * (g * g)
    m_hat = m / ADAM_C1
    v_hat = v / ADAM_C2
    delta = -ADAM_LR * (m_hat / (jnp.sqrt(v_hat) + ADAM_EPS) + ADAM_WD * w)
    return delta, m, v


def adam_flat(g, w, m, v, *, name):
    def body(g_ref, w_ref, m_ref, v_ref, d_ref, nm_ref, nv_ref):
        d, nm, nv = _adamw(g_ref[...], w_ref[...], m_ref[...], v_ref[...])
        d_ref[...] = d
        nm_ref[...] = nm
        nv_ref[...] = nv

    blk = g.shape
    zero = lambda: (0, 0)
    return _pcall(body, name=name, grid=(),
                  ins=[(t, blk, zero) for t in (g, w, m, v)],
                  outs=[(blk, F32, blk, zero)] * 3)


def _chip_slots():
    x, y, c = lax.axis_index("x"), lax.axis_index("y"), lax.axis_index("c")
    chips = [(1 - x, y), (x, 1 - y), (1 - x, 1 - y)]
    return x, y, c, chips


def device_index():
    x, y, c, chips = _chip_slots()
    return jnp.stack([4 * x + 2 * y + c, 2 * x + y] + [4 * cx + 2 * cy + c for cx, cy in chips]
                     + [2 * cx + cy for cx, cy in chips]).astype(jnp.int32)


def _job_rows(R, C, n_steps):
    if n_steps is None:
        n_steps = max(1, R * C // STREAM_BLOCK_ELEMS)
    n_blk = max([d for d in range(1, n_steps + 1) if R % d == 0 and (R // d) % 16 == 0] or [1])
    return R // n_blk, n_blk


def run_job(job, *, index, name, deps=()):
    jb = job(None)
    n_in = len(jb["ins"])

    def body(idx_ref, *refs):
        jb["fn"](refs[:n_in], refs[n_in:n_in + len(jb["outs"])])

    return _pcall(body, name=name, grid=(jb["n_blk"],), ins=jb["ins"], outs=jb["outs"], prefetch=index,
                  aliases={1 + a: o for a, o in jb["aliases"].items()}, semantics=("parallel",), deps=deps)


def adam_job(gs, a_buf, b_buf, w, m, v, layer, prev):
    L, R, C = w.shape

    def build(n_steps):
        tr, n_blk = _job_rows(R, C, n_steps)
        blk = (None, tr, C)
        row = lambda t: jnp.minimum(t, n_blk - 1)
        ins = [(gs, blk, lambda t, s: (s[0], row(t), 0)), (a_buf, blk, lambda t, s: (s[1], row(t), 0))]
        ins += [(b_buf, blk, lambda t, s, j=j: (j, row(t), 0)) for j in range(3)]
        ins += [(p, blk, lambda t, s: (layer, row(t), 0)) for p in (w, m, v)]
        ins += [(p, None, None) for p in (prev or [])]

        def fn(i, o):
            g = ((((i[0][...].astype(F32) + i[1][...].astype(F32)) + i[2][...].astype(F32))
                  + i[3][...].astype(F32)) + i[4][...].astype(F32))
            d, nm, nv = _adamw(g, i[5][...], i[6][...], i[7][...])
            o[0][...] = g
            o[1][...] = d
            o[2][...] = nm
            o[3][...] = nv

        return dict(ins=ins, outs=[((L, R, C), F32, blk, lambda t, s: (layer, row(t), 0))] * 4, fn=fn,
                    aliases={8 + o: o for o in range(4)} if prev else {}, n_blk=n_blk)

    return build


def pair_job(gs, a_buf):
    _, R, C = gs.shape

    def build(n_steps):
        tr, n_blk = _job_rows(R, C, n_steps)
        blk = (None, tr, C)
        row = lambda t: jnp.minimum(t, n_blk - 1)
        ins = [(gs, blk, lambda t, s, j=j: (s[2 + j], row(t), 0)) for j in range(3)]
        ins += [(a_buf, blk, lambda t, s, j=j: (s[5 + j], row(t), 0)) for j in range(3)]

        def fn(i, o):
            for j in range(3):
                o[0][j] = (i[j][...].astype(F32) + i[3 + j][...].astype(F32)).astype(BF)

        return dict(ins=ins, outs=[((3, R, C), BF, (3, tr, C), lambda t, s: (0, row(t), 0))], fn=fn, aliases={},
                    n_blk=n_blk)

    return build


def reduce_sum(gs, a_buf, b_buf, *, name):
    _, R, C = gs.shape
    tr = _tile(R, 256, 16)
    x, y, c, _ = _chip_slots()
    idx = jnp.stack([4 * x + 2 * y + c, 2 * x + y]).astype(jnp.int32)

    def body(idx_ref, g_ref, a_ref, b0_ref, b1_ref, b2_ref, o_ref):
        o_ref[...] = ((((g_ref[...].astype(F32) + a_ref[...].astype(F32)) + b0_ref[...].astype(F32))
                       + b1_ref[...].astype(F32)) + b2_ref[...].astype(F32))

    blk3 = (None, tr, C)
    return _pcall(body, name=name, grid=(R // tr,),
                  ins=[(gs, blk3, lambda i, s: (s[0], i, 0)), (a_buf, blk3, lambda i, s: (s[1], i, 0)),
                       (b_buf, blk3, lambda i, s: (0, i, 0)), (b_buf, blk3, lambda i, s: (1, i, 0)),
                       (b_buf, blk3, lambda i, s: (2, i, 0))],
                  outs=[((R, C), F32, (tr, C), lambda i, s: (i, 0))], prefetch=idx, semantics=("parallel",))[0]


def adam_rows(g, w, m, v, *, name):
    R, C = g.shape
    tr = _tile(R, 256, 8)

    def body(g_ref, w_ref, m_ref, v_ref, d_ref, nm_ref, nv_ref):
        d, nm, nv = _adamw(g_ref[...], w_ref[...], m_ref[...], v_ref[...])
        d_ref[...] = d
        nm_ref[...] = nm
        nv_ref[...] = nv

    spec = ((tr, C), lambda i: (i, 0))
    return _pcall(body, name=name, grid=(R // tr,), ins=[(t, *spec) for t in (g, w, m, v)],
                  outs=[((R, C), F32, *spec)] * 3, semantics=("parallel",))


def sum_rows8(gathered, rows, *, name):
    W = gathered.shape[1]

    def body(g_ref, o_ref):
        acc = g_ref[0:rows, :]
        for d in range(1, N_DEV):
            acc = acc + g_ref[d * rows:(d + 1) * rows, :]
        o_ref[...] = acc

    return _pcall(body, name=name, grid=(), ins=[(gathered, gathered.shape, lambda: (0, 0))],
                  outs=[((rows, W), F32, (rows, W), lambda: (0, 0))])[0]


HBM_SPEC = pl.BlockSpec(memory_space=pltpu.HBM)
SEM_SPEC = pl.BlockSpec(memory_space=pltpu.SEMAPHORE)
ANY_SPEC = pl.BlockSpec(memory_space=pl.ANY)
DATAFLOW = pltpu.SideEffectType.DATAFLOW_SIDE_EFFECTING


def _in_hbm(v):
    return pltpu.with_memory_space_constraint(v, pltpu.HBM)


def _slot(p):
    return 4 * p[0] + 2 * p[1] + p[2]


def _gather_peers():
    x, y, c, chips = _chip_slots()
    return (x, y, c), [(x, y, 1 - c)] + [(*chip, c) for chip in chips]


def gather_start(groups, after, *, name):
    flat = [s for g in groups for s in g]
    n, n_g = len(flat), len(groups)
    where = [(gi, ti) for gi, g in enumerate(groups) for ti in range(len(g))]

    def body(*refs):
        src, land = refs[:n], refs[n:2 * n]
        sems = refs[2 * n + 1:2 * n + 1 + 2 * n_g]
        me, peers = _gather_peers()
        for t in range(n):
            gi, ti = where[t]
            for k, to in enumerate(peers):
                pltpu.make_async_remote_copy(
                    src_ref=src[t], dst_ref=land[t].at[_slot(me)], send_sem=sems[2 * gi].at[4 * ti + k],
                    recv_sem=sems[2 * gi + 1].at[4 * ti + k], device_id=to, device_id_type=MESH).start()
        refs[-1][...] = jnp.zeros_like(refs[-1])

    out_shape = []
    for g in groups:
        out_shape += [pltpu.SemaphoreType.DMA((4 * len(g),)), pltpu.SemaphoreType.DMA((4 * len(g),))]
    out_shape += [pltpu.HBM(s.shape, s.dtype) for s in flat]
    out_shape += [pltpu.HBM((N_DEV,) + s.shape, s.dtype) for s in flat]
    out_shape += [jax.ShapeDtypeStruct((8, LANES), F32)]
    aliases = {t: 2 * n_g + t for t in range(n)}
    aliases.update({n + t: 2 * n_g + n + t for t in range(n)})
    res = pl.pallas_call(
        body, name=name, out_shape=out_shape, in_specs=[HBM_SPEC] * (2 * n) + [ANY_SPEC],
        out_specs=[SEM_SPEC] * (2 * n_g) + [HBM_SPEC] * (2 * n) + [pl.BlockSpec(memory_space=pltpu.VMEM)],
        input_output_aliases=aliases, compiler_params=pltpu.CompilerParams(has_side_effects=DATAFLOW),
    )(*[_in_hbm(s) for s in flat], *[_in_hbm(lax.empty((N_DEV,) + s.shape, s.dtype)) for s in flat], after)
    out, off = [], 0
    for gi, g in enumerate(groups):
        k = len(g)
        out.append((res[2 * gi], res[2 * gi + 1], res[2 * n_g + off:2 * n_g + off + k],
                    res[2 * n_g + n + off:2 * n_g + n + off + k]))
        off += k
    return out, res[-1]


def gather_wait(started, after, *, name):
    send_sems, recv_sems, srcs, lands = started
    n = len(srcs)
    after = list(after)

    def body(*refs):
        src, land = refs[:n], refs[n:2 * n]
        send, recv = refs[2 * n], refs[2 * n + 1]
        _, peers = _gather_peers()
        for t in range(n):
            for k, frm in enumerate(peers):
                cp = pltpu.make_async_remote_copy(
                    src_ref=src[t], dst_ref=land[t].at[_slot(frm)], send_sem=send.at[4 * t + k],
                    recv_sem=recv.at[4 * t + k],
                    device_id=frm, device_id_type=MESH)
                cp.wait_send()
                cp.wait_recv()

    res = pl.pallas_call(
        body, name=name,
        out_shape=[pltpu.HBM(s.shape, s.dtype) for s in srcs] + [pltpu.HBM(l.shape, l.dtype) for l in lands],
        in_specs=[HBM_SPEC] * (2 * n) + [SEM_SPEC, SEM_SPEC] + [ANY_SPEC] * len(after),
        out_specs=[HBM_SPEC] * (2 * n), input_output_aliases={t: t for t in range(2 * n)},
        compiler_params=pltpu.CompilerParams(has_side_effects=DATAFLOW),
    )(*srcs, *lands, send_sems, recv_sems, *after)
    return res[:n], res[n:]


def place_own(src, land, *, name):
    R, C = src.shape
    tr = _tile(R, 512, 16)
    x, y, c, _ = _chip_slots()
    idx = jnp.stack([4 * x + 2 * y + c]).astype(jnp.int32)

    def body(idx_ref, s_ref, land_ref, o_ref):
        o_ref[...] = s_ref[...]

    return _pcall(body, name=name, grid=(R // tr,),
                  ins=[(src, (tr, C), lambda i, s: (i, 0)), (land, None, None)],
                  outs=[(land.shape, land.dtype, (None, tr, C), lambda i, s: (s[0], i, 0))],
                  prefetch=idx, aliases={2: 0}, semantics=("parallel",))[0]


def gather_finish(srcs, lands, *, name):
    n = len(srcs)

    def body(*refs):
        land = refs[n:2 * n]
        send_sems, recv_sems = refs[2 * n:]
        x, y, c, chips = _chip_slots()
        me, sibling = (x, y, c), (x, y, 1 - c)

        def copy(t, j, block, to):
            return pltpu.make_async_remote_copy(
                src_ref=land[t].at[_slot(block)], dst_ref=land[t].at[_slot(block)], send_sem=send_sems.at[t, j],
                recv_sem=recv_sems.at[t, j], device_id=to, device_id_type=MESH)

        sends = [copy(t, j, (*chip, c), sibling) for t in range(n) for j, chip in enumerate(chips)]
        for cp in sends:
            cp.start()
        for t in range(n):
            for j, chip in enumerate(chips):
                copy(t, j, (*chip, 1 - c), me).wait_recv()
        for cp in sends:
            cp.wait_send()

    passed = pl.pallas_call(
        body, name=name, out_shape=[jax.ShapeDtypeStruct(l.shape, l.dtype) for l in lands],
        in_specs=[ANY_SPEC] * n, out_specs=[ANY_SPEC] * n,
        input_output_aliases={t: t for t in range(n)},
        scratch_shapes=[pltpu.SemaphoreType.DMA((n, 3)), pltpu.SemaphoreType.DMA((n, 3))],
    )(*lands)
    return [place_own(s, l, name=f"{name}_own{t}") for t, (s, l) in enumerate(zip(srcs, passed))]


def chips_start(pairs, *, name):
    n = len(pairs)

    def body(*refs):
        src, land = refs[:n], refs[n:2 * n]
        send, recv = refs[2 * n], refs[2 * n + 1]
        token = refs[-1]
        x, y, c, chips = _chip_slots()
        for t in range(n):
            for j, chip in enumerate(chips):
                pltpu.make_async_remote_copy(
                    src_ref=src[t].at[j], dst_ref=land[t].at[j], send_sem=send.at[3 * t + j],
                    recv_sem=recv.at[3 * t + j], device_id=(*chip, c), device_id_type=MESH).start()
        token[...] = jnp.zeros_like(token)

    res = pl.pallas_call(
        body, name=name,
        out_shape=[pltpu.SemaphoreType.DMA((3 * n,)), pltpu.SemaphoreType.DMA((3 * n,))]
        + [pltpu.HBM(p.shape, p.dtype) for p in pairs] * 2 + [jax.ShapeDtypeStruct((8, LANES), F32)],
        in_specs=[HBM_SPEC] * (2 * n),
        out_specs=[SEM_SPEC, SEM_SPEC] + [HBM_SPEC] * (2 * n) + [pl.BlockSpec(memory_space=pltpu.VMEM)],
        input_output_aliases={t: 2 + t for t in range(2 * n)},
        compiler_params=pltpu.CompilerParams(has_side_effects=DATAFLOW),
    )(*[_in_hbm(p) for p in pairs], *[_in_hbm(lax.empty(p.shape, p.dtype)) for p in pairs])
    return res[0], res[1], res[2:2 + n], res[2 + n:2 + 2 * n], res[-1]


def chips_wait(started, after, *, name):
    send_sems, recv_sems, srcs, lands, _ = started
    n = len(srcs)

    def body(*refs):
        src, land = refs[:n], refs[n:2 * n]
        send, recv = refs[2 * n], refs[2 * n + 1]
        x, y, c, chips = _chip_slots()
        for t in range(n):
            for j, chip in enumerate(chips):
                cp = pltpu.make_async_remote_copy(
                    src_ref=src[t].at[j], dst_ref=land[t].at[j], send_sem=send.at[3 * t + j],
                    recv_sem=recv.at[3 * t + j], device_id=(*chip, c), device_id_type=MESH)
                cp.wait_send()
                cp.wait_recv()

    res = pl.pallas_call(
        body, name=name, out_shape=[pltpu.HBM(s.shape, s.dtype) for s in srcs] * 2,
        in_specs=[HBM_SPEC] * (2 * n) + [SEM_SPEC, SEM_SPEC, ANY_SPEC], out_specs=[HBM_SPEC] * (2 * n),
        input_output_aliases={t: t for t in range(2 * n)},
        compiler_params=pltpu.CompilerParams(has_side_effects=DATAFLOW),
    )(*srcs, *lands, send_sems, recv_sems, after)
    return res[n:]


def _sibling_copies(src, land, send, recv, n):
    x, y, c, _ = _chip_slots()
    return [pltpu.make_async_remote_copy(
        src_ref=src[t].at[4 * (q // 2) + 2 * (q % 2) + (1 - c)], dst_ref=land[t].at[q], send_sem=send.at[4 * t + q],
        recv_sem=recv.at[4 * t + q], device_id=(x, y, 1 - c), device_id_type=MESH)
        for t in range(n) for q in range(4)]


def sibling_start(gs, *, name):
    n = len(gs)

    def body(*refs):
        for cp in _sibling_copies(refs[:n], refs[n:2 * n], refs[2 * n], refs[2 * n + 1], n):
            cp.start()
        refs[-1][...] = jnp.zeros_like(refs[-1])

    lands = [lax.empty((4,) + g.shape[1:], g.dtype) for g in gs]
    res = pl.pallas_call(
        body, name=name,
        out_shape=[pltpu.SemaphoreType.DMA((4 * n,)), pltpu.SemaphoreType.DMA((4 * n,))]
        + [pltpu.HBM(g.shape, g.dtype) for g in gs] + [pltpu.HBM(l.shape, l.dtype) for l in lands]
        + [jax.ShapeDtypeStruct((8, LANES), F32)],
        in_specs=[HBM_SPEC] * (2 * n),
        out_specs=[SEM_SPEC, SEM_SPEC] + [HBM_SPEC] * (2 * n) + [pl.BlockSpec(memory_space=pltpu.VMEM)],
        input_output_aliases={t: 2 + t for t in range(2 * n)},
        compiler_params=pltpu.CompilerParams(has_side_effects=DATAFLOW),
    )(*[_in_hbm(g) for g in gs], *[_in_hbm(l) for l in lands])
    return res[0], res[1], res[2:2 + n], res[2 + n:2 + 2 * n], res[-1]


def sibling_wait(started, after, *, name):
    send_sems, recv_sems, srcs, lands, _ = started
    n = len(srcs)

    def body(*refs):
        for cp in _sibling_copies(refs[:n], refs[n:2 * n], refs[2 * n], refs[2 * n + 1], n):
            cp.wait_send()
            cp.wait_recv()

    res = pl.pallas_call(
        body, name=name,
        out_shape=[pltpu.HBM(s.shape, s.dtype) for s in srcs] + [pltpu.HBM(l.shape, l.dtype) for l in lands],
        in_specs=[HBM_SPEC] * (2 * n) + [SEM_SPEC, SEM_SPEC, ANY_SPEC], out_specs=[HBM_SPEC] * (2 * n),
        input_output_aliases={t: t for t in range(2 * n)},
        compiler_params=pltpu.CompilerParams(has_side_effects=DATAFLOW),
    )(*srcs, *lands, send_sems, recv_sems, after)
    return res[:n], res[n:]


def all_gather_vmem(x_shard, *, name, after=None):
    m_per, n = x_shard.shape
    n_after = 0 if after is None else 1

    def body(x_ref, *rest):
        out_ref, send_sems, recv_sems, local_sem = rest[n_after:]
        x, y, c, chips = _chip_slots()
        me, sibling = (x, y, c), (x, y, 1 - c)

        def rows(px, py, pc):
            return out_ref.at[pl.ds((4 * px + 2 * py + pc) * m_per, m_per), :]

        def copy(k, block, to, src=None):
            return pltpu.make_async_remote_copy(
                src_ref=rows(*block) if src is None else src, dst_ref=rows(*block),
                send_sem=send_sems.at[k], recv_sem=recv_sems.at[k], device_id=to, device_id_type=MESH)

        mine = pltpu.make_async_copy(x_ref, rows(*me), local_sem)
        mine.start()
        first = [copy(0, me, sibling, src=x_ref)]
        first += [copy(1 + j, me, (*chip, c), src=x_ref) for j, chip in enumerate(chips)]
        for cp in first:
            cp.start()
        passed = [copy(4 + j, (*chip, c), sibling) for j, chip in enumerate(chips)]
        for j, chip in enumerate(chips):
            copy(1 + j, (*chip, c), me).wait_recv()
            passed[j].start()
        copy(0, sibling, me).wait_recv()
        for j, chip in enumerate(chips):
            copy(4 + j, (*chip, 1 - c), me).wait_recv()
        for cp in first + passed:
            cp.wait_send()
        mine.wait()

    vmem = pl.BlockSpec(memory_space=pltpu.VMEM)
    return pl.pallas_call(
        body, name=name, out_shape=jax.ShapeDtypeStruct((N_DEV * m_per, n), x_shard.dtype),
        in_specs=[vmem] + [ANY_SPEC] * n_after, out_specs=vmem,
        scratch_shapes=[pltpu.SemaphoreType.DMA((7,)), pltpu.SemaphoreType.DMA((7,)), pltpu.SemaphoreType.DMA],
        compiler_params=pltpu.CompilerParams(vmem_limit_bytes=int(min(
            VMEM_LIMIT_CAP, 2 * (N_DEV + 1) * m_per * n * x_shard.dtype.itemsize + 16 * 2 ** 20))),
    )(x_shard, *([] if after is None else [after]))


def _rope_slab(cols):
    z = jnp.zeros(cols.shape[:-1] + (HALF_ROPE,), cols.dtype)
    return jnp.concatenate([cols[..., :HALF_ROPE], z, cols[..., HALF_ROPE:], z], axis=-1)


def _rope_unslab(slab):
    return jnp.concatenate([slab[..., :HALF_ROPE], slab[..., 2 * HALF_ROPE:3 * HALF_ROPE]], axis=-1)


def _pack_w_in_t(wt_g):
    s, c, d = wt_g.shape
    w = wt_g.reshape(s * c, d)
    c2, c3 = Q_LORA + KV_LORA, Q_LORA + KV_LORA + QK_ROPE
    r = w[c2:c3]
    z = jnp.zeros((HALF_ROPE, d), w.dtype)
    return jnp.concatenate([w[:c2], w[c3:], r[:HALF_ROPE], z, r[HALF_ROPE:], z], axis=0)


def _unpack_w_in_t_grad(dwt):
    d = dwt.shape[1]
    c2 = Q_LORA + KV_LORA
    uv = 2 * SGU_OUT
    slab = dwt[c2 + uv:]
    g = jnp.concatenate([dwt[:c2], slab[:HALF_ROPE], slab[2 * HALF_ROPE:3 * HALF_ROPE], dwt[c2:c2 + uv]], axis=0)
    return g.reshape(N_DEV, g.shape[0] // N_DEV, d)


def _rope_tables(positions):
    inv_freq = ROPE_BASE ** (-jnp.arange(0, QK_ROPE, 2, dtype=F32) / QK_ROPE)
    ang = positions.astype(F32)[:, None] * inv_freq
    cos, sin = jnp.cos(ang), jnp.sin(ang)
    z = jnp.zeros_like(cos)
    return jnp.concatenate([cos, z, cos, z], axis=-1), jnp.concatenate([-sin, z, sin, z], axis=-1)


def _mlp_up(x, gain, w1, tag):
    hn = rms_fwd(x, gain, name=f"mlp{tag}_norm")

    def act_epi(acc):
        a = jnp.maximum(acc, 0.0)
        return a, a * a

    T = x.shape[0]
    F = w1.shape[0] * w1.shape[2]
    a, act = mm(hn, w1, name=f"mlp{tag}_up", outs=[((T, F), BF, None), ((T, F), BF, None)], epi=act_epi)
    return hn, a, act


def _mlp_down(x, act, w2, tag, part=0):
    n = w2.shape[1]
    bm = _tile(x.shape[0], MM_TILE)
    bn = _tile(n, MM_TILE)
    per = n // bn
    return mm(act, w2, name=f"mlp{tag}_down{part}", out=((x.shape[0], n), F32), bm=bm, bn=bn,
              epi=lambda acc, r: (acc + r[...],), epi_ins=[(x, (bm, bn), lambda i, j, k: (i, part * per + j))])


def _mlp_bwd_weights(w1, w2, saved, dxb, tag):
    hn, a, act = saved
    T, D = dxb.shape
    F = a.shape[1]
    bn = _tile(F, MM_NARROW_TILE)
    dhid = mm(dxb, w2, tb=True, name=f"mlp{tag}_dhid", out=((T, F), BF), bm=T, bn=bn,
              epi=lambda acc, a_ref: (2.0 * a_ref[...].astype(F32) * acc,),
              epi_ins=[(a, (T, bn), lambda i, j, k: (i, j))])
    dw2 = mm(act, dxb, ta=True, name=f"mlp{tag}_dw2", out=((F, D), BF), bm=_tile(F, MM_TILE), bn=D)
    dw1 = mm(hn, dhid, ta=True, name=f"mlp{tag}_dw1", out=(w1.shape, BF), bm=D, bn=_tile(w1.shape[2], MM_TILE))
    return dhid, dw1, dw2.reshape(N_DEV, F // N_DEV, D)


def _reduce_begin(grads, tag):
    return sibling_start(grads, name=f"reduce_sibling_start_{tag}")


def _reduce_continue(sib, after, tag, index):
    grads, a_bufs = sibling_wait(sib, after, name=f"reduce_sibling_wait_{tag}")
    pairs = [run_job(pair_job(g, a), index=index, name=f"pair_sum_{tag}{t}")[0]
             for t, (g, a) in enumerate(zip(grads, a_bufs))]
    return grads, a_bufs, chips_start(pairs, name=f"reduce_chips_start_{tag}")


def kernel(x, positions, e_norm_mix, e_w_in, e_q_norm, e_w_uq, e_kv_norm, e_w_ukv, e_v_norm, e_sgu_w, e_sgu_b, e_mla_out_norm, e_sgu_out_norm, e_w_out, o_norm_mix, o_w_in, o_conv_w, o_w_out, mlp_norm, mlp_w1, mlp_w2, final_norm, loss_target, m_e_norm_mix, m_e_w_in, m_e_q_norm, m_e_w_uq, m_e_kv_norm, m_e_w_ukv, m_e_v_norm, m_e_sgu_w, m_e_sgu_b, m_e_mla_out_norm, m_e_sgu_out_norm, m_e_w_out, m_o_norm_mix, m_o_w_in, m_o_conv_w, m_o_w_out, m_mlp_norm, m_mlp_w1, m_mlp_w2, m_final_norm, v_e_norm_mix, v_e_w_in, v_e_q_norm, v_e_w_uq, v_e_kv_norm, v_e_w_ukv, v_e_v_norm, v_e_sgu_w, v_e_sgu_b, v_e_mla_out_norm, v_e_sgu_out_norm, v_e_w_out, v_o_norm_mix, v_o_w_in, v_o_conv_w, v_o_w_out, v_mlp_norm, v_mlp_w1, v_mlp_w2, v_final_norm):
    T, D = x.shape[1], x.shape[2]
    d_shard = o_norm_mix.shape[1]
    x0 = x[0]
    target = loss_target[0]
    me = 4 * lax.axis_index("x") + 2 * lax.axis_index("y") + lax.axis_index("c")

    bf = lambda s: s.astype(BF)
    gather_groups = [[bf(jnp.transpose(e_w_in[0])), bf(e_w_uq[0]), bf(e_w_ukv[0])], [bf(e_w_out[0]), bf(mlp_w1[0])],
                     [bf(mlp_w2[0]), bf(o_w_in[0])], [bf(o_w_out[0]), bf(mlp_w1[1])], [bf(mlp_w2[1])]]
    small_rows = jnp.concatenate([o_norm_mix, o_conv_w[0], jnp.zeros((4, d_shard), F32)], axis=0)
    small_flat = all_gather_vmem(small_rows, name="gather_small")
    started, start_token = gather_start(gather_groups[:1], small_flat, name="gather_start0")
    started_rest, rest_token = gather_start(gather_groups[1:], start_token, name="gather_start1")
    started += started_rest

    def gathered(gi, after):
        srcs, lands = gather_wait(started[gi], after, name=f"gather_wait{gi}")
        return gather_finish(srcs, lands, name=f"gather_finish{gi}")

    small_g = small_flat.reshape(N_DEV, 8, d_shard)
    o_norm_full = small_g[:, 0, :].reshape(1, D)
    conv_w_full = jnp.transpose(small_g[:, 1:4, :], (1, 0, 2)).reshape(3, D)
    w_tril = jnp.tril(e_sgu_w[0])
    w_tril_b = w_tril.astype(BF)
    w_tril_tb = jnp.swapaxes(w_tril, 1, 2).astype(BF)
    b_full = jnp.repeat(e_sgu_b[0].T, CH, axis=1)
    v_gain = e_v_norm[0].reshape(1, SGU_OUT)
    cos_t, sin_t = _rope_tables(positions[0])
    mlp_gain = [mlp_norm[0:1], mlp_norm[1:2]]
    final_gain = final_norm.reshape(1, D)

    h0 = rms_fwd(x0, e_norm_mix, name="e_norm", deps=[rest_token])
    g_w_in_t, g_w_uq, w_ukv = gathered(
        0, [h0, cos_t, sin_t, w_tril_b, w_tril_tb, b_full, o_norm_full, conv_w_full])
    w_in_t = _pack_w_in_t(g_w_in_t)
    w_uq = jnp.concatenate([g_w_uq[..., :QK_NOPE], _rope_slab(g_w_uq[..., QK_NOPE:])], axis=-1)
    proj = mm(h0, w_in_t, tb=True, name="e_in", out=((T, w_in_t.shape[0]), F32), bn=_tile(w_in_t.shape[0], 640))
    qn, kvn, krope = mla_prep(proj, e_q_norm, e_kv_norm, cos_t, sin_t, name="mla_prep")
    bm = _tile(T, MM_TILE)

    def q_epi(acc, cos_ref, sin_ref):
        return (jnp.concatenate([acc[:, :QK_NOPE], _rope_fwd(acc[:, QK_NOPE:], cos_ref[...], sin_ref[...])], axis=-1),)

    q = mm(qn, w_uq, name="mla_q", out=((T, HEADS * HEAD_PAD), BF), bm=bm, bn=HEAD_PAD, epi=q_epi,
           epi_ins=[(cos_t, (bm, LANES), lambda i, j, k: (i, 0)), (sin_t, (bm, LANES), lambda i, j, k: (i, 0))])

    def kv_epi(acc, kr_ref):
        return jnp.concatenate([acc[:, :QK_NOPE].astype(BF), kr_ref[...]], axis=-1), acc[:, QK_NOPE:]

    k, v = mm(kvn, w_ukv, name="mla_kv", bm=bm, bn=HEAD_PAD, epi=kv_epi,
              outs=[((T, HEADS * HEAD_PAD), BF, HEAD_PAD), ((T, MLA_OUT), BF, V_HEAD)],
              epi_ins=[(krope, (bm, LANES), lambda i, j, k: (i, 0))])
    attn, attn_lse = attn_fwd(q, k, v, name="attn_fwd")
    mixed = mix_fwd(attn, proj, e_mla_out_norm, e_sgu_out_norm, v_gain, w_tril_b, b_full, name="mix_fwd")
    bn = _tile(D, MM_TILE)
    g_w_out_e, w1_0 = gathered(1, [mixed])
    w_out_e = g_w_out_e.reshape(-1, D)
    x1 = mm(mixed, w_out_e, name="e_out", out=((T, D), F32), bm=bm, bn=bn,
            epi=lambda acc, r: (acc + r[...],), epi_ins=[(x0, (bm, bn), lambda i, j, k: (i, j))])
    hn0, a0, act0 = _mlp_up(x1, mlp_gain[0], w1_0, 0)
    g_w2_0, g_w_in_o = gathered(2, [act0])
    w2_0 = g_w2_0.reshape(-1, D)
    x2 = _mlp_down(x1, act0, w2_0, 0)
    ho = rms_fwd(x2, o_norm_full, name="o_norm")
    proj_o = mm(ho, g_w_in_o, name="o_in", out=((T, 3 * D), F32))
    gated = conv_fwd(proj_o, conv_w_full, name="conv_fwd")
    g_w_out_o, w1_1 = gathered(3, [gated])
    w_out_o = g_w_out_o.reshape(-1, D)
    x3 = mm(gated, w_out_o, name="o_out", out=((T, D), F32), bm=bm, bn=bn,
            epi=lambda acc, r: (acc + r[...],), epi_ins=[(x2, (bm, bn), lambda i, j, k: (i, j))])
    hn1, a1, act1 = _mlp_up(x3, mlp_gain[1], w1_1, 1)
    (g_w2_1,) = gathered(4, [act1])
    w2_1 = g_w2_1.reshape(-1, D)
    x4 = _mlp_down(x3, act1, w2_1, 1)
    w1, w2 = [w1_0, w1_1], [w2_0, w2_1]

    dx4, dx4b, d_final, loss_part = loss_bwd([x4], final_gain, target, name="loss_bwd")

    hosted = dict(job_index=device_index())
    dhid1, dw1_1, dw2_1 = _mlp_bwd_weights(w1[1], w2[1], (hn1, a1, act1), dx4b, 1)
    sib_r0 = _reduce_begin([dw1_1, dw2_1], "r0")
    dhn1 = mm(dhid1, w1[1], tb=True, name="mlp1_dhn", out=((T, D), F32), deps=[sib_r0[-1]])
    grads_r0, a_r0 = sibling_wait(sib_r0, dhn1, name="reduce_sibling_wait_r0")
    dx3, dx3b, d_mlp1 = rms_bwd(x3, mlp_gain[1], dhn1, dres=dx4, name="mlp1_norm_bwd")

    dgated, ((pair_r0a,),) = mm(dx3b, w_out_o, tb=True, name="o_out_dx", out=((T, D), F32),
                                jobs=[pair_job(grads_r0[0], a_r0[0])], **hosted)
    dw_out_o, ((pair_r0b,),) = mm(gated, dx3b, ta=True, name="o_out_dw", out=((D, D), BF),
                                  jobs=[pair_job(grads_r0[1], a_r0[1])], **hosted)
    st_r0 = chips_start([pair_r0a, pair_r0b], name="reduce_chips_start_r0")
    dproj_o, dconv_full = conv_bwd(dgated, proj_o, conv_w_full, name="conv_bwd", deps=[st_r0[-1]])
    dw_in_o = mm(ho, dproj_o, ta=True, name="o_in_dw", out=(g_w_in_o.shape, BF))
    sib_r1 = _reduce_begin([dw_out_o.reshape(g_w_out_o.shape), dw_in_o], "r1")
    dho = mm(dproj_o, g_w_in_o, tb=True, name="o_in_dx", out=((T, D), F32), deps=[sib_r1[-1]])
    grads_r1, a_r1 = sibling_wait(sib_r1, dho, name="reduce_sibling_wait_r1")
    dx2, dx2b, d_onorm_full = rms_bwd(x2, o_norm_full, dho, dres=dx3, name="o_norm_bwd")

    d_ff = a0.shape[1]
    bm_h, bn_h = T, _tile(d_ff, MM_NARROW_TILE)
    dhid0, ((pair_r1a,), (pair_r1b,)) = mm(
        dx2b, w2[0], tb=True, name="mlp0_dhid", out=((T, d_ff), BF), bm=bm_h, bn=bn_h,
        epi=lambda acc, a_ref: (2.0 * a_ref[...].astype(F32) * acc,),
        epi_ins=[(a0, (bm_h, bn_h), lambda i, j, k: (i, j))],
        jobs=[pair_job(grads_r1[0], a_r1[0]), pair_job(grads_r1[1], a_r1[1])], **hosted)
    st_r1 = chips_start([pair_r1a, pair_r1b], name="reduce_chips_start_r1")
    dw2_0 = mm(act0, dx2b, ta=True, name="mlp0_dw2", out=((d_ff, D), BF), bm=_tile(d_ff, MM_TILE), bn=D,
               deps=[st_r1[-1]])
    b_r0 = chips_wait(st_r0, dw2_0, name="reduce_chips_wait_r0")
    dw1_0, (r_w1, r_w2) = mm(
        hn0, dhid0, ta=True, name="mlp0_dw1", out=(w1[0].shape, BF),
        jobs=[adam_job(grads_r0[0], a_r0[0], b_r0[0], mlp_w1, m_mlp_w1, v_mlp_w1, 1, None),
              adam_job(grads_r0[1], a_r0[1], b_r0[1], mlp_w2, m_mlp_w2, v_mlp_w2, 1, None)], **hosted)
    sib_r2 = _reduce_begin([dw1_0, dw2_0.reshape(N_DEV, d_ff // N_DEV, D)], "r2")
    dhn0 = mm(dhid0, w1[0], tb=True, name="mlp0_dhn", out=((T, D), F32), deps=[sib_r2[-1]])
    grads_r2, a_r2 = sibling_wait(sib_r2, dhn0, name="reduce_sibling_wait_r2")
    dx1, dx1b, d_mlp0 = rms_bwd(x1, mlp_gain[0], dhn0, dres=dx2, name="mlp0_norm_bwd")

    dmixed, ((pair_r2a,),) = mm(dx1b, w_out_e, tb=True, name="e_out_dx", out=((T, MLA_OUT + SGU_OUT), F32),
                                jobs=[pair_job(grads_r2[0], a_r2[0])], **hosted)
    dw_out_e, ((pair_r2b,),) = mm(mixed, dx1b, ta=True, name="e_out_dw", out=(w_out_e.shape, BF),
                                  jobs=[pair_job(grads_r2[1], a_r2[1])], **hosted)
    st_r2 = chips_start([pair_r2a, pair_r2b], name="reduce_chips_start_r2")
    (dattn, duv, d_mla_out, d_sgu_out, d_vgain, d_sgu_w, d_b_full) = mix_bwd(
        dmixed, attn, proj, e_mla_out_norm, e_sgu_out_norm, v_gain, w_tril_b, w_tril_tb, b_full, name="mix_bwd",
        deps=[st_r2[-1]])
    b_r1 = chips_wait(st_r1, dattn, name="reduce_chips_wait_r1")
    dq, dk, dv = attn_bwd(q, k, v, attn, attn_lse, dattn, name="attn_bwd")
    dq_lin, dkv_lin, dkr = mla_bwd_prep(dq, dk, dv, cos_t, sin_t, name="mla_bwd_prep")
    dw_uq_pad = mm(qn, dq_lin, ta=True, name="mla_q_dw", out=(w_uq.shape, BF))
    dw_ukv = mm(kvn, dkv_lin, ta=True, name="mla_kv_dw", out=(w_ukv.shape, BF))
    dw_uq = jnp.concatenate([dw_uq_pad[..., :QK_NOPE], _rope_unslab(dw_uq_pad[..., QK_NOPE:])], axis=-1)
    sib_r2b = _reduce_begin([dw_out_e.reshape(g_w_out_e.shape), dw_uq, dw_ukv], "r2b")
    dqn = mm(dq_lin, w_uq, tb=True, name="mla_q_dx", out=((T, Q_LORA), F32), deps=[sib_r2b[-1]])
    dkvn = mm(dkv_lin, w_ukv, tb=True, name="mla_kv_dx", out=((T, KV_LORA), F32), deps=[sib_r2b[-1]])
    grads_r2b, a_r2b, st_r2b = _reduce_continue(sib_r2b, dkvn, "r2b", hosted["job_index"])
    dcq, d_qnorm = rms_bwd(proj, e_q_norm, dqn, col_block=0, want_f32=False, name="q_norm_bwd", deps=[st_r2b[-1]])
    dckv, d_kvnorm = rms_bwd(proj, e_kv_norm, dkvn, col_block=1, want_f32=False, name="kv_norm_bwd")
    dproj = jnp.concatenate([dcq, dckv, duv, dkr], axis=-1)
    dw_in_t_pad, (r_w_out_o, r_w_in_o) = mm(
        dproj, h0, ta=True, name="e_in_dw", out=(w_in_t.shape, BF), bm=_tile(w_in_t.shape[0], 640),
        jobs=[adam_job(grads_r1[0], a_r1[0], b_r1[0], o_w_out, m_o_w_out, v_o_w_out, 0, None),
              adam_job(grads_r1[1], a_r1[1], b_r1[1], o_w_in, m_o_w_in, v_o_w_in, 0, None)], **hosted)
    dw_in_t = _unpack_w_in_t_grad(dw_in_t_pad)
    sib_r3 = _reduce_begin([dw_in_t], "r3")
    dh0 = mm(dproj, w_in_t, name="e_in_dx", out=((T, D), F32), deps=[sib_r3[-1]])
    grads_r3, a_r3, st_r3 = _reduce_continue(sib_r3, dh0, "r3", hosted["job_index"])
    tok_r3 = st_r3[-1]
    grad_x, d_enorm = rms_bwd(x0, e_norm_mix, dh0, dres=dx1, want_bf=False, name="e_norm_bwd", deps=[tok_r3])
    b_r2 = chips_wait(st_r2, grad_x, name="reduce_chips_wait_r2")

    def finish(grads, a_bufs, b_bufs, t, w, m, v, layer=0, prev=None, tag="", deps=()):
        return run_job(adam_job(grads[t], a_bufs[t], b_bufs[t], w, m, v, layer, prev), index=hosted["job_index"],
                       name=f"adam_{tag}", deps=deps)

    r_w1 = finish(grads_r2, a_r2, b_r2, 0, mlp_w1, m_mlp_w1, v_mlp_w1, 0, r_w1, tag="w1_l0", deps=[tok_r3])
    r_w2 = finish(grads_r2, a_r2, b_r2, 1, mlp_w2, m_mlp_w2, v_mlp_w2, 0, r_w2, tag="w2_l0", deps=[r_w1[1]])
    b_r2b = chips_wait(st_r2b, r_w2[1], name="reduce_chips_wait_r2b")
    r_w_out_e = finish(grads_r2b, a_r2b, b_r2b, 0, e_w_out, m_e_w_out, v_e_w_out, tag="e_w_out")
    r_w_uq = finish(grads_r2b, a_r2b, b_r2b, 1, e_w_uq, m_e_w_uq, v_e_w_uq, tag="e_w_uq")
    r_w_ukv = finish(grads_r2b, a_r2b, b_r2b, 2, e_w_ukv, m_e_w_ukv, v_e_w_ukv, tag="e_w_ukv")
    b_r3 = chips_wait(st_r3, r_w_out_e[1], name="reduce_chips_wait_r3")
    g_w_in_t = reduce_sum(grads_r3[0], a_r3[0], b_r3[0], name="sum_e_w_in")
    w_in_upd_t = adam_rows(g_w_in_t, jnp.transpose(e_w_in[0]), jnp.transpose(m_e_w_in[0]), jnp.transpose(v_e_w_in[0]),
                           name="adam_e_w_in")
    r_w_in = [jnp.transpose(t)[None] for t in (g_w_in_t, *w_in_upd_t)]

    d_sgu_b = jnp.transpose(d_b_full[:, ::CH])
    d_sgu_w_tril = jnp.tril(d_sgu_w)
    rep = [("e_norm_mix", e_norm_mix, m_e_norm_mix, v_e_norm_mix, d_enorm),
           ("e_q_norm", e_q_norm, m_e_q_norm, v_e_q_norm, d_qnorm),
           ("e_kv_norm", e_kv_norm, m_e_kv_norm, v_e_kv_norm, d_kvnorm),
           ("e_v_norm", e_v_norm, m_e_v_norm, v_e_v_norm, d_vgain),
           ("e_sgu_w", e_sgu_w, m_e_sgu_w, v_e_sgu_w, d_sgu_w_tril),
           ("e_sgu_b", e_sgu_b, m_e_sgu_b, v_e_sgu_b, d_sgu_b),
           ("e_mla_out_norm", e_mla_out_norm, m_e_mla_out_norm, v_e_mla_out_norm, d_mla_out),
           ("e_sgu_out_norm", e_sgu_out_norm, m_e_sgu_out_norm, v_e_sgu_out_norm, d_sgu_out),
           ("mlp_norm", mlp_norm, m_mlp_norm, v_mlp_norm, jnp.concatenate([d_mlp0, d_mlp1], axis=0)),
           ("final_norm", final_norm, m_final_norm, v_final_norm, d_final)]
    sizes = [int(np.prod(r[1].shape)) for r in rep]
    n_rep = sum(sizes)
    n_all = n_rep + 4 * D + 1
    width = -(-n_all // (8 * LANES)) * LANES
    pad = 8 * width - n_all
    flat = jnp.concatenate([r[4].reshape(-1) for r in rep]
                           + [d_onorm_full.reshape(-1), dconv_full.reshape(-1), loss_part[0, :1],
                              jnp.zeros((pad,), F32)])
    summed = sum_rows8(all_gather_vmem(flat.reshape(8, width), name="gather_small_grads", after=b_r3[0]), 8,
                       name="sum_small_grads").reshape(-1)

    loss = summed[n_rep + 4 * D]

    def pack_rep(i):
        return jnp.concatenate([r[i].reshape(-1) for r in rep]).reshape(n_rep // LANES, LANES)

    g_rep = summed[:n_rep].reshape(n_rep // LANES, LANES)
    d_rep, nm_rep, nv_rep = adam_flat(g_rep, pack_rep(1), pack_rep(2), pack_rep(3), name="adam_replicated")

    def unpack_rep(flat2d):
        out, off = {}, 0
        f = flat2d.reshape(-1)
        for r, n in zip(rep, sizes):
            out[r[0]] = f[off:off + n].reshape(r[1].shape)
            off += n
        return out

    small = {"grad": unpack_rep(g_rep), "delta": unpack_rep(d_rep), "new_m": unpack_rep(nm_rep),
             "new_v": unpack_rep(nv_rep)}
    g_onorm = lax.dynamic_slice(summed[n_rep:n_rep + D].reshape(1, D), (0, me * d_shard), (1, d_shard))
    g_conv = lax.dynamic_slice(summed[n_rep + D:n_rep + 4 * D].reshape(3, D), (0, me * d_shard), (3, d_shard))

    def pack_sharded(norm_part, conv_part):
        return jnp.concatenate([norm_part, conv_part, jnp.zeros((4, d_shard), F32)], axis=0)

    g_sh = pack_sharded(g_onorm, g_conv)
    d_sh, nm_sh, nv_sh = adam_flat(g_sh, pack_sharded(o_norm_mix, o_conv_w[0]), pack_sharded(m_o_norm_mix, m_o_conv_w[0]),
                                   pack_sharded(v_o_norm_mix, v_o_conv_w[0]), name="adam_sharded_small")
    for kind, arr in (("grad", g_sh), ("delta", d_sh), ("new_m", nm_sh), ("new_v", nv_sh)):
        small[kind]["o_norm_mix"] = arr[0:1]
        small[kind]["o_conv_w"] = arr[1:4][None]

    big = {"e_w_in": r_w_in, "e_w_uq": r_w_uq, "e_w_ukv": r_w_ukv, "e_w_out": r_w_out_e, "o_w_in": r_w_in_o,
           "o_w_out": r_w_out_o, "mlp_w1": r_w1, "mlp_w2": r_w2}
    order = ["e_norm_mix", "e_w_in", "e_q_norm", "e_w_uq", "e_kv_norm", "e_w_ukv", "e_v_norm", "e_sgu_w", "e_sgu_b",
             "e_mla_out_norm", "e_sgu_out_norm", "e_w_out", "o_norm_mix", "o_w_in", "o_conv_w", "o_w_out", "mlp_norm",
             "mlp_w1", "mlp_w2", "final_norm"]
    result = [loss, grad_x[None]]
    for ki, kind in enumerate(("grad", "delta", "new_m", "new_v")):
        for nm in order:
            result.append(big[nm][ki] if nm in big else small[kind][nm])
    return tuple(result)
```

```python
import numpy as np
import jax
import jax.numpy as jnp
from jax import lax
from jax.experimental import pallas as pl
from jax.experimental.pallas import tpu as pltpu

BF = jnp.bfloat16
F32 = jnp.float32
MESH = pl.DeviceIdType.MESH
N_DEV = 8

EPS = 1e-6
HEADS = 8
Q_LORA = 512
KV_LORA = 512
QK_NOPE = 128
QK_ROPE = 64
HALF_ROPE = QK_ROPE // 2
V_HEAD = 128
HEAD_PAD = 256
ROPE_BASE = 10000.0
GROUPS = 8
CH = 128
CHUNK = 128
SGU_OUT = GROUPS * CH
MLA_OUT = HEADS * V_HEAD
ATTN_SCALE = float((QK_NOPE + QK_ROPE) ** -0.5)

ADAM_LR = 0.001
ADAM_B1 = 0.9
ADAM_B2 = 0.999
ADAM_EPS = 1e-08
ADAM_WD = 0.01
ADAM_STEP = 10
ADAM_C1 = 1.0 - ADAM_B1 ** ADAM_STEP
ADAM_C2 = 1.0 - ADAM_B2 ** ADAM_STEP

V7X_VMEM_BYTES = 64 * 2 ** 20
VMEM_LIMIT_CAP = V7X_VMEM_BYTES - 6 * 2 ** 20
LANES = 128
ROW_TILE = 256
ATTN_TILE = 512
STREAM_BLOCK_ELEMS = 512 * 1024
MM_TILE = 1024
MM_K_TILE = 2048
MM_K_BLOCK_MAX = 3072


def _padded_bytes(block, dtype):
    dims = [d for d in block if d is not None]
    if len(dims) >= 1:
        dims[-1] = -(-dims[-1] // LANES) * LANES
    if len(dims) >= 2:
        dims[-2] = -(-dims[-2] // 16) * 16
    return int(np.prod(dims)) * jnp.dtype(dtype).itemsize


def _pcall(body, *, name, grid, ins, outs, scratch=(), semantics=None, aliases=None, prefetch=None, deps=()):
    any_spec = pl.BlockSpec(memory_space=pl.ANY)
    if deps:
        n_lead = len(ins) + (1 if prefetch is not None else 0)
        n_deps = len(deps)
        inner = body

        def body(*refs):
            inner(*refs[:n_lead], *refs[n_lead + n_deps:])

        ins = list(ins) + [(d, None, None) for d in deps]
    in_specs = [any_spec if b is None else pl.BlockSpec(b, m) for _, b, m in ins]
    out_specs = [any_spec if b is None else pl.BlockSpec(b, m) for _, _, b, m in outs]
    out_shape = [pltpu.HBM(s, d) for s, d, _, _ in outs]
    est = 0
    for a, b, _ in ins:
        if b is not None:
            est += 2 * _padded_bytes(b, a.dtype)
    for _, d, b, _ in outs:
        if b is not None:
            est += 2 * _padded_bytes(b, d)
    for s in scratch:
        if hasattr(s, "shape") and hasattr(s, "dtype"):
            est += _padded_bytes(s.shape, s.dtype)
    limit = int(min(VMEM_LIMIT_CAP, est + 16 * 2 ** 20))
    params = pltpu.CompilerParams(
        dimension_semantics=semantics or ("arbitrary",) * len(grid), vmem_limit_bytes=limit)
    args = [pltpu.with_memory_space_constraint(a, pltpu.HBM) for a, _, _ in ins]
    if prefetch is not None:
        grid_spec = pltpu.PrefetchScalarGridSpec(
            num_scalar_prefetch=1, grid=grid, in_specs=in_specs, out_specs=out_specs, scratch_shapes=list(scratch))
        call = pl.pallas_call(body, out_shape=out_shape, grid_spec=grid_spec, name=name, compiler_params=params,
                              input_output_aliases=aliases or {})
        return call(prefetch, *args)
    call = pl.pallas_call(body, out_shape=out_shape, grid=grid, in_specs=in_specs, out_specs=out_specs,
                          scratch_shapes=list(scratch), name=name, compiler_params=params,
                          input_output_aliases=aliases or {})
    return call(*args)


def _tile(dim, pref, quantum=LANES):
    if dim <= pref:
        return dim
    t = (pref // quantum) * quantum
    while t >= quantum:
        if dim % t == 0:
            return t
        t -= quantum
    return dim


def _vshape(arr_shape):
    if len(arr_shape) == 2:
        return tuple(arr_shape)
    s, r, c = arr_shape
    return (r, s * c)


def _vblock(arr_shape, br, bc, rc):
    if len(arr_shape) == 2:
        return (br, bc), (lambda *g: rc(*g))
    _, _, c = arr_shape
    assert c % bc == 0, (arr_shape, bc)
    per = c // bc

    def imap(*g):
        ri, ci = rc(*g)
        return (ci // per, ri, ci % per)

    return (None, br, bc), imap


def _shard_width(*shapes):
    w = None
    for s in shapes:
        if len(s) == 3:
            w = s[2] if w is None else int(np.gcd(w, s[2]))
    return w


def mm(a, b, *, name, ta=False, tb=False, out=None, outs=None, epi=None, epi_ins=(), bm=None, bn=None, bk=None,
       deps=(), jobs=(), job_index=None):
    av, bv = _vshape(a.shape), _vshape(b.shape)
    M, K = (av[1], av[0]) if ta else av
    K2, N = (bv[1], bv[0]) if tb else bv
    assert K == K2, (a.shape, b.shape, ta, tb)
    if outs is None:
        outs = [(out[0], out[1], None)]
    a_sw = _shard_width(a.shape)
    b_sw = _shard_width(b.shape)
    o_sw = _shard_width(*[o[0] for o in outs])
    m_lim = a_sw if (ta and a_sw) else None
    k_lim = [w for w in ((a_sw if not ta else None), (b_sw if tb else None)) if w]
    n_lim = [w for w in ((b_sw if not tb else None), o_sw) if w]
    if bm is None:
        bm = _tile(M, min([MM_TILE] + ([m_lim] if m_lim else [])))
    if bn is None:
        bn = _tile(N, min([MM_TILE] + n_lim))
    k_shards = 0
    if tb and len(b.shape) == 3 and bk is None and not (a_sw and not ta):
        k_shards = 1
        while 2 * k_shards <= b.shape[0] and 2 * k_shards * b_sw <= MM_K_BLOCK_MAX:
            k_shards *= 2
        bk = k_shards * b_sw
    if bk is None:
        bk = K if (K <= 4096 and not k_lim) else _tile(K, min([MM_K_TILE] + k_lim))
    assert M % bm == 0 and N % bn == 0 and K % bk == 0, (name, M, N, K, bm, bn, bk)
    nk = K // bk
    grid = (M // bm, N // bn, nk)
    if ta:
        a_blk, a_map = _vblock(a.shape, bk, bm, lambda i, j, k: (k, i))
    else:
        a_blk, a_map = _vblock(a.shape, bm, bk, lambda i, j, k: (i, k))
    if k_shards:
        b_blk, b_map = (k_shards, bn, b_sw), (lambda i, j, k: (k, j, 0))
    elif tb:
        b_blk, b_map = _vblock(b.shape, bn, bk, lambda i, j, k: (j, k))
    else:
        b_blk, b_map = _vblock(b.shape, bk, bn, lambda i, j, k: (k, j))
    dn = (((0 if ta else 1,), (1 if tb else 0,)), ((), ()))
    ins = [(a, a_blk, a_map), (b, b_blk, b_map)] + list(epi_ins)
    out_list = []
    for shape, dtype, cols in outs:
        cols = cols or bn
        blk, imap = _vblock(shape, bm, cols, lambda i, j, k: (i, j))
        out_list.append((shape, dtype, blk, imap))
    n_e, n_o = len(epi_ins), len(out_list)

    n_steps = grid[0] * grid[1] * nk
    built = [job(n_steps) for job in jobs]
    aliases = {}
    job_slices = []
    if built:
        def lin(i, j, k):
            return (i * grid[1] + j) * nk + k

        ins = [(arr, blk, None if blk is None else (lambda i, j, k, s, f=f: f(i, j, k))) for arr, blk, f in ins]
        out_list = [(sh, dt, blk, (lambda i, j, k, s, f=f: f(i, j, k))) for sh, dt, blk, f in out_list]
        n_main_in, n_main_out = len(ins), len(out_list)
        for jb in built:
            i0, o0 = len(ins), len(out_list)
            ins += [(arr, blk, None if blk is None else (lambda i, j, k, s, f=f: f(lin(i, j, k), s)))
                    for arr, blk, f in jb["ins"]]
            out_list += [(sh, dt, blk, (lambda i, j, k, s, f=f: f(lin(i, j, k), s))) for sh, dt, blk, f in jb["outs"]]
            aliases.update({1 + i0 + ai: o0 + ao for ai, ao in jb["aliases"].items()})
            job_slices.append((i0, len(jb["ins"]), o0, len(jb["outs"])))
    n_in_total = len(ins)

    def body(*refs):
        if built:
            refs = refs[1:]
        a_ref, b_ref = refs[0], refs[1]
        e_refs = refs[2:2 + n_e]
        o_refs = refs[n_in_total:n_in_total + n_o]
        for jb, (i0, ni, o0, no) in zip(built, job_slices):
            jb["fn"](refs[i0:i0 + ni], refs[n_in_total + o0:n_in_total + o0 + no])

        def finish(acc):
            res = epi(acc, *e_refs) if epi is not None else (acc,)
            for o_ref, r in zip(o_refs, res):
                o_ref[...] = r.astype(o_ref.dtype)

        x = a_ref[...].astype(BF)
        y = b_ref[...].astype(BF)
        if k_shards:
            p = None
            for s in range(k_shards):
                part = lax.dot_general(x[:, s * b_sw:(s + 1) * b_sw], y[s], dn, preferred_element_type=F32)
                p = part if p is None else p + part
        else:
            p = lax.dot_general(x, y, dn, preferred_element_type=F32)
        if nk == 1:
            finish(p)
        else:
            acc_ref = refs[-1]
            k = pl.program_id(2)

            @pl.when(k == 0)
            def _():
                acc_ref[...] = p

            @pl.when(k > 0)
            def _():
                acc_ref[...] += p

            @pl.when(k == nk - 1)
            def _():
                finish(acc_ref[...])

    scratch = [pltpu.VMEM((bm, bn), F32)] if nk > 1 else []
    res = _pcall(body, name=name, grid=grid, ins=ins, outs=out_list, scratch=scratch, deps=deps,
                 semantics=("parallel", "parallel", "arbitrary"), prefetch=job_index if built else None, aliases=aliases)
    main = res[0] if n_o == 1 else res[:n_o]
    if not built:
        return main
    return main, [res[o0:o0 + no] for _, _, o0, no in job_slices]


_GELU_K = float(np.sqrt(2.0 / np.pi))
_GELU_C = 0.044715


def _gelu(x):
    t = jnp.tanh(_GELU_K * (x + _GELU_C * (x * x * x)))
    return 0.5 * x * (1.0 + t)


def _gelu_grad(x):
    t = jnp.tanh(_GELU_K * (x + _GELU_C * (x * x * x)))
    return 0.5 * (1.0 + t) + 0.5 * x * (1.0 - t * t) * (_GELU_K * (1.0 + 3.0 * _GELU_C * (x * x)))


def _rstd(x):
    return lax.rsqrt(jnp.mean(x * x, axis=-1, keepdims=True) + EPS)


def _rms_bwd(x, gain, dy):
    r = _rstd(x)
    xh = x * r
    gdy = dy * gain
    dx = r * (gdy - xh * jnp.mean(gdy * xh, axis=-1, keepdims=True))
    return dx, dy * xh


def _rope_fwd(x, cos_t, sin_t):
    return x * cos_t + pltpu.roll(x, 2 * HALF_ROPE, 1) * sin_t


def _rope_bwd(dy, cos_t, sin_t):
    return dy * cos_t + pltpu.roll(dy * sin_t, 2 * HALF_ROPE, 1)


def _acc_rows(ref, val, first):
    s = jnp.sum(val, axis=0, keepdims=True)

    @pl.when(first)
    def _():
        ref[...] = s

    @pl.when(jnp.logical_not(first))
    def _():
        ref[...] += s


def rms_fwd(x, gain, *, name, col_block=0, width=None, deps=()):
    T = x.shape[0]
    width = width or x.shape[1]
    tm = _tile(T, ROW_TILE, 8)

    def body(x_ref, g_ref, o_ref):
        v = x_ref[...]
        o_ref[...] = (v * _rstd(v) * g_ref[...]).astype(BF)

    return _pcall(body, name=name, grid=(T // tm,),
                  ins=[(x, (tm, width), lambda i: (i, col_block)), (gain, (1, width), lambda i: (0, 0))],
                  outs=[((T, width), BF, (tm, width), lambda i: (i, 0))], semantics=("parallel",), deps=deps)[0]


def rms_bwd(x, gain, dy, *, name, col_block=0, dres=None, want_f32=True, want_bf=True, deps=()):
    T, width = dy.shape
    tm = _tile(T, ROW_TILE, 8)
    has_res = dres is not None

    def body(*refs):
        x_ref, g_ref, dy_ref = refs[:3]
        pos = 3
        res_ref = None
        if has_res:
            res_ref = refs[pos]
            pos += 1
        outs = refs[pos:]
        dx, dg_rows = _rms_bwd(x_ref[...], g_ref[...], dy_ref[...])
        if has_res:
            dx = dx + res_ref[...]
        o = 0
        if want_f32:
            outs[o][...] = dx
            o += 1
        if want_bf:
            outs[o][...] = dx.astype(BF)
            o += 1
        _acc_rows(outs[o], dg_rows, pl.program_id(0) == 0)

    ins = [(x, (tm, width), lambda i: (i, col_block)), (gain, (1, width), lambda i: (0, 0)),
           (dy, (tm, width), lambda i: (i, 0))]
    if has_res:
        ins.append((dres, (tm, width), lambda i: (i, 0)))
    outs = []
    if want_f32:
        outs.append(((T, width), F32, (tm, width), lambda i: (i, 0)))
    if want_bf:
        outs.append(((T, width), BF, (tm, width), lambda i: (i, 0)))
    outs.append(((1, width), F32, (1, width), lambda i: (0, 0)))
    return _pcall(body, name=name, grid=(T // tm,), ins=ins, outs=outs, deps=deps)


def mla_prep(proj, q_norm, kv_norm, cos_t, sin_t, *, name):
    T = proj.shape[0]
    tm = _tile(T, ROW_TILE, 8)
    kr_block = (proj.shape[1] - LANES) // LANES

    def body(cq_ref, ckv_ref, kr_ref, qg_ref, kg_ref, cos_ref, sin_ref, qn_ref, kvn_ref, krope_ref):
        cq = cq_ref[...]
        qn_ref[...] = (cq * _rstd(cq) * qg_ref[...]).astype(BF)
        ckv = ckv_ref[...]
        kvn_ref[...] = (ckv * _rstd(ckv) * kg_ref[...]).astype(BF)
        krope_ref[...] = _rope_fwd(kr_ref[...], cos_ref[...], sin_ref[...]).astype(BF)

    return _pcall(
        body, name=name, grid=(T // tm,),
        ins=[(proj, (tm, Q_LORA), lambda i: (i, 0)), (proj, (tm, KV_LORA), lambda i: (i, 1)),
             (proj, (tm, LANES), lambda i: (i, kr_block)),
             (q_norm, (1, Q_LORA), lambda i: (0, 0)), (kv_norm, (1, KV_LORA), lambda i: (0, 0)),
             (cos_t, (tm, LANES), lambda i: (i, 0)), (sin_t, (tm, LANES), lambda i: (i, 0))],
        outs=[((T, Q_LORA), BF, (tm, Q_LORA), lambda i: (i, 0)), ((T, KV_LORA), BF, (tm, KV_LORA), lambda i: (i, 0)),
              ((T, LANES), BF, (tm, LANES), lambda i: (i, 0))],
        semantics=("parallel",))


def _attn_scores(q, k_blk, diagonal):
    s = lax.dot_general(q, k_blk, (((1,), (1,)), ((), ())), preferred_element_type=F32) * ATTN_SCALE
    if diagonal:
        row = lax.broadcasted_iota(jnp.int32, s.shape, 0)
        col = lax.broadcasted_iota(jnp.int32, s.shape, 1)
        s = jnp.where(col <= row, s, -jnp.inf)
    return s


def attn_fwd(q, k, v, *, name):
    T = q.shape[0]
    tq = _tile(T, ATTN_TILE, 8)

    def body(q_ref, k_ref, v_ref, o_ref, lse_ref):
        i = pl.program_id(1)
        qv = q_ref[...]

        def block(kb, carry, diagonal):
            m, l, acc = carry
            start = pl.multiple_of(kb * tq, tq)
            s = _attn_scores(qv, k_ref[pl.ds(start, tq), :], diagonal)
            m_new = jnp.maximum(m, jnp.max(s, axis=-1, keepdims=True))
            alpha = jnp.exp(m - m_new)
            p = jnp.exp(s - m_new)
            l = alpha * l + jnp.sum(p, axis=-1, keepdims=True)
            acc = alpha * acc + jnp.dot(p.astype(BF), v_ref[pl.ds(start, tq), :], preferred_element_type=F32)
            return m_new, l, acc

        init = (jnp.full((tq, 1), -jnp.inf, F32), jnp.zeros((tq, 1), F32), jnp.zeros((tq, V_HEAD), F32))
        carry = lax.fori_loop(0, i, lambda kb, c: block(kb, c, False), init)
        m, l, acc = block(i, carry, True)
        o_ref[...] = acc / l
        lse_ref[...] = jnp.broadcast_to(m + jnp.log(l), (tq, V_HEAD))

    return _pcall(
        body, name=name, grid=(HEADS, T // tq),
        ins=[(q, (tq, HEAD_PAD), lambda h, i: (i, h)), (k, (T, HEAD_PAD), lambda h, i: (0, h)),
             (v, (T, V_HEAD), lambda h, i: (0, h))],
        outs=[((T, MLA_OUT), F32, (tq, V_HEAD), lambda h, i: (i, h)),
              ((T, MLA_OUT), F32, (tq, V_HEAD), lambda h, i: (i, h))], semantics=("parallel", "parallel"))


def attn_bwd(q, k, v, o, lse, do, *, name):
    T = q.shape[0]
    tq = _tile(T, ATTN_TILE, 8)

    def body(q_ref, k_ref, v_ref, o_ref, lse_ref, do_ref, dq_ref, dk_ref, dv_ref):
        i = pl.program_id(1)

        @pl.when(i == 0)
        def _():
            dk_ref[...] = jnp.zeros_like(dk_ref)
            dv_ref[...] = jnp.zeros_like(dv_ref)

        qv = q_ref[...]
        do_t = do_ref[...]
        lse_v = lse_ref[:, 0:1]
        delta = jnp.sum(do_t.astype(F32) * o_ref[...], axis=-1, keepdims=True)

        def block(kb, dq, diagonal):
            start = pl.multiple_of(kb * tq, tq)
            k_blk = k_ref[pl.ds(start, tq), :]
            v_blk = v_ref[pl.ds(start, tq), :]
            p = jnp.exp(_attn_scores(qv, k_blk, diagonal) - lse_v)
            dp = lax.dot_general(do_t, v_blk, (((1,), (1,)), ((), ())), preferred_element_type=F32)
            ds = (p * (dp - delta) * ATTN_SCALE).astype(BF)
            dk_ref[pl.ds(start, tq), :] += lax.dot_general(ds, qv, (((0,), (0,)), ((), ())), preferred_element_type=F32)
            dv_ref[pl.ds(start, tq), :] += lax.dot_general(p.astype(BF), do_t, (((0,), (0,)), ((), ())),
                                                          preferred_element_type=F32)
            return dq + jnp.dot(ds, k_blk, preferred_element_type=F32)

        dq = lax.fori_loop(0, i, lambda kb, c: block(kb, c, False), jnp.zeros((tq, HEAD_PAD), F32))
        dq_ref[...] = block(i, dq, True)

    return _pcall(
        body, name=name, grid=(HEADS, T // tq),
        ins=[(q, (tq, HEAD_PAD), lambda h, i: (i, h)), (k, (T, HEAD_PAD), lambda h, i: (0, h)),
             (v, (T, V_HEAD), lambda h, i: (0, h)), (o, (tq, V_HEAD), lambda h, i: (i, h)),
             (lse, (tq, V_HEAD), lambda h, i: (i, h)), (do, (tq, V_HEAD), lambda h, i: (i, h))],
        outs=[((T, HEADS * HEAD_PAD), F32, (tq, HEAD_PAD), lambda h, i: (i, h)),
              ((T, HEADS * HEAD_PAD), F32, (T, HEAD_PAD), lambda h, i: (0, h)),
              ((T, MLA_OUT), F32, (T, V_HEAD), lambda h, i: (0, h))],
        semantics=("parallel", "arbitrary"))


def mla_bwd_prep(dq, dk, dv, cos_t, sin_t, *, name):
    T = dq.shape[0]
    tm = _tile(T, ROW_TILE, 8)

    def body(dq_ref, dk_ref, dv_ref, cos_ref, sin_ref, dql_ref, dkvl_ref, dkr_ref):
        cos_v, sin_v = cos_ref[...], sin_ref[...]
        kr = jnp.zeros((tm, LANES), F32)
        for h in range(HEADS):
            lo = h * HEAD_PAD
            dql_ref[:, lo:lo + QK_NOPE] = dq_ref[:, lo:lo + QK_NOPE].astype(BF)
            dql_ref[:, lo + QK_NOPE:lo + HEAD_PAD] = _rope_bwd(
                dq_ref[:, lo + QK_NOPE:lo + HEAD_PAD], cos_v, sin_v).astype(BF)
            dkvl_ref[:, lo:lo + QK_NOPE] = dk_ref[:, lo:lo + QK_NOPE].astype(BF)
            dkvl_ref[:, lo + QK_NOPE:lo + HEAD_PAD] = dv_ref[:, h * V_HEAD:(h + 1) * V_HEAD].astype(BF)
            kr = kr + dk_ref[:, lo + QK_NOPE:lo + HEAD_PAD]
        dkr_ref[...] = _rope_bwd(kr, cos_v, sin_v).astype(BF)

    W = HEADS * HEAD_PAD
    return _pcall(
        body, name=name, grid=(T // tm,),
        ins=[(dq, (tm, W), lambda i: (i, 0)), (dk, (tm, W), lambda i: (i, 0)), (dv, (tm, MLA_OUT), lambda i: (i, 0)),
             (cos_t, (tm, LANES), lambda i: (i, 0)), (sin_t, (tm, LANES), lambda i: (i, 0))],
        outs=[((T, W), BF, (tm, W), lambda i: (i, 0)), ((T, W), BF, (tm, W), lambda i: (i, 0)),
              ((T, LANES), BF, (tm, LANES), lambda i: (i, 0))],
        semantics=("parallel",))


def _group_norm_stats(vg):
    mu = jnp.mean(vg, axis=-1, keepdims=True)
    d = vg - mu
    r = lax.rsqrt(jnp.mean(d * d, axis=-1, keepdims=True) + EPS)
    return d * r, r


def mix_fwd(a, proj, g_mla, g_sgu, v_gain, w_tril, b_full, *, name):
    T = a.shape[0]
    tm = _tile(T, ROW_TILE, CHUNK)
    n_chunk = tm // CHUNK

    def body(a_ref, u_ref, v_ref, gm_ref, gs_ref, vg_ref, w_ref, b_ref, o_ref, s_scr):
        av = a_ref[...]
        o_ref[:, :MLA_OUT] = (av * _rstd(av) * gm_ref[...]).astype(BF)
        for g in range(GROUPS):
            sl = slice(g * CH, (g + 1) * CH)
            vhat, _ = _group_norm_stats(_gelu(v_ref[:, sl]))
            vn = (vhat * vg_ref[:, sl]).astype(BF)
            u = _gelu(u_ref[:, sl])
            for ci in range(n_chunk):
                rs = slice(ci * CHUNK, (ci + 1) * CHUNK)
                y = jnp.dot(w_ref[g], vn[rs], preferred_element_type=F32) + b_ref[:, sl]
                s_scr[rs, sl] = u[rs] * y
        s = s_scr[...]
        o_ref[:, MLA_OUT:] = (s * _rstd(s) * gs_ref[...]).astype(BF)

    return _pcall(
        body, name=name, grid=(T // tm,),
        ins=[(a, (tm, MLA_OUT), lambda i: (i, 0)), (proj, (tm, SGU_OUT), lambda i: (i, 1)),
             (proj, (tm, SGU_OUT), lambda i: (i, 2)), (g_mla, (1, MLA_OUT), lambda i: (0, 0)),
             (g_sgu, (1, SGU_OUT), lambda i: (0, 0)), (v_gain, (1, SGU_OUT), lambda i: (0, 0)),
             (w_tril, (GROUPS, CHUNK, CHUNK), lambda i: (0, 0, 0)), (b_full, (CHUNK, SGU_OUT), lambda i: (0, 0))],
        outs=[((T, MLA_OUT + SGU_OUT), BF, (tm, MLA_OUT + SGU_OUT), lambda i: (i, 0))],
        scratch=[pltpu.VMEM((tm, SGU_OUT), F32)], semantics=("parallel",))[0]


def mix_bwd(dmixed, a, proj, g_mla, g_sgu, v_gain, w_tril, w_tril_t, b_full, *, name, deps=()):
    T = a.shape[0]
    tm = _tile(T, ROW_TILE, CHUNK)
    n_chunk = tm // CHUNK

    def body(dm_a_ref, dm_s_ref, a_ref, u_ref, v_ref, gm_ref, gs_ref, vg_ref, w_ref, wt_ref, b_ref,
             da_ref, duv_ref, dgm_ref, dgs_ref, dvg_ref, dw_ref, db_ref, s_scr, y_scr):
        first = pl.program_id(0) == 0
        da, dgm_rows = _rms_bwd(a_ref[...], gm_ref[...], dm_a_ref[...])
        da_ref[...] = da.astype(BF)
        _acc_rows(dgm_ref, dgm_rows, first)

        for g in range(GROUPS):
            sl = slice(g * CH, (g + 1) * CH)
            vhat, _ = _group_norm_stats(_gelu(v_ref[:, sl]))
            vn = (vhat * vg_ref[:, sl]).astype(BF)
            u = _gelu(u_ref[:, sl])
            for ci in range(n_chunk):
                rs = slice(ci * CHUNK, (ci + 1) * CHUNK)
                y = jnp.dot(w_ref[g], vn[rs], preferred_element_type=F32) + b_ref[:, sl]
                y_scr[rs, sl] = y
                s_scr[rs, sl] = u[rs] * y
        ds, dgs_rows = _rms_bwd(s_scr[...], gs_ref[...], dm_s_ref[...])
        _acc_rows(dgs_ref, dgs_rows, first)
        s_scr[...] = ds

        @pl.when(first)
        def _():
            dw_ref[...] = jnp.zeros_like(dw_ref)
            db_ref[...] = jnp.zeros_like(db_ref)

        for g in range(GROUPS):
            sl = slice(g * CH, (g + 1) * CH)
            upre = u_ref[:, sl]
            vpre = v_ref[:, sl]
            u = _gelu(upre)
            vhat, r = _group_norm_stats(_gelu(vpre))
            gain = vg_ref[:, sl]
            vn = (vhat * gain).astype(BF)
            dsg = s_scr[:, sl]
            duv_ref[:, sl] = (dsg * y_scr[:, sl] * _gelu_grad(upre)).astype(BF)
            dy = dsg * u
            dyb = dy.astype(BF)
            dvn_parts = []
            for ci in range(n_chunk):
                rs = slice(ci * CHUNK, (ci + 1) * CHUNK)
                dvn_parts.append(jnp.dot(wt_ref[g], dyb[rs], preferred_element_type=F32))
                dw_ref[g] += lax.dot_general(dyb[rs], vn[rs], (((1,), (1,)), ((), ())), preferred_element_type=F32)
                db_ref[:, sl] += jnp.broadcast_to(jnp.sum(dy[rs], axis=-1, keepdims=True), (CHUNK, CH))
            dvn = dvn_parts[0] if n_chunk == 1 else jnp.concatenate(dvn_parts, axis=0)
            _acc_rows(dvg_ref.at[:, sl], dvn * vhat, first)
            dvh = dvn * gain
            dvg = r * (dvh - jnp.mean(dvh, axis=-1, keepdims=True)
                       - vhat * jnp.mean(dvh * vhat, axis=-1, keepdims=True))
            duv_ref[:, SGU_OUT + g * CH:SGU_OUT + (g + 1) * CH] = (dvg * _gelu_grad(vpre)).astype(BF)

    return _pcall(
        body, name=name, grid=(T // tm,),
        ins=[(dmixed, (tm, MLA_OUT), lambda i: (i, 0)), (dmixed, (tm, SGU_OUT), lambda i: (i, 1)),
             (a, (tm, MLA_OUT), lambda i: (i, 0)), (proj, (tm, SGU_OUT), lambda i: (i, 1)),
             (proj, (tm, SGU_OUT), lambda i: (i, 2)), (g_mla, (1, MLA_OUT), lambda i: (0, 0)),
             (g_sgu, (1, SGU_OUT), lambda i: (0, 0)), (v_gain, (1, SGU_OUT), lambda i: (0, 0)),
             (w_tril, (GROUPS, CHUNK, CHUNK), lambda i: (0, 0, 0)), (w_tril_t, (GROUPS, CHUNK, CHUNK), lambda i: (0, 0, 0)),
             (b_full, (CHUNK, SGU_OUT), lambda i: (0, 0))],
        outs=[((T, MLA_OUT), BF, (tm, MLA_OUT), lambda i: (i, 0)),
              ((T, 2 * SGU_OUT), BF, (tm, 2 * SGU_OUT), lambda i: (i, 0)),
              ((1, MLA_OUT), F32, (1, MLA_OUT), lambda i: (0, 0)), ((1, SGU_OUT), F32, (1, SGU_OUT), lambda i: (0, 0)),
              ((1, SGU_OUT), F32, (1, SGU_OUT), lambda i: (0, 0)),
              ((GROUPS, CHUNK, CHUNK), F32, (GROUPS, CHUNK, CHUNK), lambda i: (0, 0, 0)),
              ((CHUNK, SGU_OUT), F32, (CHUNK, SGU_OUT), lambda i: (0, 0))],
        scratch=[pltpu.VMEM((tm, SGU_OUT), F32), pltpu.VMEM((tm, SGU_OUT), F32)], deps=deps)


def _shift_down(z, n, row):
    return jnp.where(row >= n, pltpu.roll(z, n, 0), 0.0)


def _shift_up(z, n, row, T):
    return jnp.where(row < T - n, pltpu.roll(z, T - n, 0), 0.0)


def conv_fwd(proj, conv_w, *, name):
    T, D3 = proj.shape
    D = D3 // 3
    tn = _tile(D, 256)
    nj = D // tn

    def body(b_ref, c_ref, x_ref, w_ref, o_ref):
        row = lax.broadcasted_iota(jnp.int32, (T, tn), 0)
        z = c_ref[...] * x_ref[...]
        zc = w_ref[2:3, :] * z + w_ref[1:2, :] * _shift_down(z, 1, row) + w_ref[0:1, :] * _shift_down(z, 2, row)
        o_ref[...] = (b_ref[...] * zc).astype(BF)

    return _pcall(
        body, name=name, grid=(nj,),
        ins=[(proj, (T, tn), lambda j: (0, j)), (proj, (T, tn), lambda j: (0, nj + j)),
             (proj, (T, tn), lambda j: (0, 2 * nj + j)), (conv_w, (3, tn), lambda j: (0, j))],
        outs=[((T, D), BF, (T, tn), lambda j: (0, j))], semantics=("parallel",))[0]


def conv_bwd(dg, proj, conv_w, *, name, deps=()):
    T, D3 = proj.shape
    D = D3 // 3
    tn = _tile(D, 256)
    nj = D // tn

    def body(dg_ref, b_ref, c_ref, x_ref, w_ref, dp_ref, dw_ref, dc_scr, dx_scr):
        part = pl.program_id(1)

        @pl.when(part == 0)
        def _():
            row = lax.broadcasted_iota(jnp.int32, (T, tn), 0)
            c, x = c_ref[...], x_ref[...]
            z = c * x
            z1 = _shift_down(z, 1, row)
            z2 = _shift_down(z, 2, row)
            dgv = dg_ref[...]
            zc = w_ref[2:3, :] * z + w_ref[1:2, :] * z1 + w_ref[0:1, :] * z2
            dp_ref[...] = (dgv * zc).astype(BF)
            dzc = dgv * b_ref[...]
            dw_ref[0:1, :] = jnp.sum(dzc * z2, axis=0, keepdims=True)
            dw_ref[1:2, :] = jnp.sum(dzc * z1, axis=0, keepdims=True)
            dw_ref[2:3, :] = jnp.sum(dzc * z, axis=0, keepdims=True)
            dz = (w_ref[2:3, :] * dzc + w_ref[1:2, :] * _shift_up(dzc, 1, row, T)
                  + w_ref[0:1, :] * _shift_up(dzc, 2, row, T))
            dc_scr[...] = (dz * x).astype(BF)
            dx_scr[...] = (dz * c).astype(BF)

        @pl.when(part == 1)
        def _():
            dp_ref[...] = dc_scr[...]

        @pl.when(part == 2)
        def _():
            dp_ref[...] = dx_scr[...]

    return _pcall(
        body, name=name, grid=(nj, 3),
        ins=[(dg, (T, tn), lambda j, p: (0, j)), (proj, (T, tn), lambda j, p: (0, j)),
             (proj, (T, tn), lambda j, p: (0, nj + j)), (proj, (T, tn), lambda j, p: (0, 2 * nj + j)),
             (conv_w, (3, tn), lambda j, p: (0, j))],
        outs=[((T, D3), BF, (T, tn), lambda j, p: (0, p * nj + j)), ((3, D), F32, (3, tn), lambda j, p: (0, j))],
        scratch=[pltpu.VMEM((T, tn), BF), pltpu.VMEM((T, tn), BF)], semantics=("parallel", "arbitrary"), deps=deps)


def loss_bwd(x_parts, gain, target, *, name):
    T, D = target.shape
    tm = _tile(T, ROW_TILE, 8)
    n_x = len(x_parts)

    def body(*refs):
        x_refs = refs[:n_x]
        g_ref, t_ref, dx_ref, dxb_ref, dg_ref, loss_ref = refs[n_x:]
        first = pl.program_id(0) == 0
        xv = jnp.concatenate([r[...] for r in x_refs], axis=-1) if n_x > 1 else x_refs[0][...]
        r = _rstd(xv)
        xh = xv * r
        gain_v = g_ref[...]
        err = xh * gain_v - t_ref[...]
        part = 0.5 * jnp.sum(jnp.mean(err * err, axis=-1, keepdims=True), axis=0, keepdims=True)
        _acc_rows(loss_ref, jnp.broadcast_to(part, (1, LANES)), first)
        dy = err * (1.0 / D)
        gdy = dy * gain_v
        dx = r * (gdy - xh * jnp.mean(gdy * xh, axis=-1, keepdims=True))
        dx_ref[...] = dx
        dxb_ref[...] = dx.astype(BF)
        _acc_rows(dg_ref, dy * xh, first)

    return _pcall(
        body, name=name, grid=(T // tm,),
        ins=[(p, (tm, D // n_x), lambda i: (i, 0)) for p in x_parts]
        + [(gain, (1, D), lambda i: (0, 0)), (target, (tm, D), lambda i: (i, 0))],
        outs=[((T, D), F32, (tm, D), lambda i: (i, 0)), ((T, D), BF, (tm, D), lambda i: (i, 0)),
              ((1, D), F32, (1, D), lambda i: (0, 0)), ((1, LANES), F32, (1, LANES), lambda i: (0, 0))])


def _adamw(g, w, m, v):
    m = ADAM_B1 * m + (1.0 - ADAM_B1) * g
    v = ADAM_B2 * v + (1.0 - ADAM_B2) * (g * g)
    m_hat = m / ADAM_C1
    v_hat = v / ADAM_C2
    delta = -ADAM_LR * (m_hat / (jnp.sqrt(v_hat) + ADAM_EPS) + ADAM_WD * w)
    return delta, m, v


def adam_flat(g, w, m, v, *, name):
    def body(g_ref, w_ref, m_ref, v_ref, d_ref, nm_ref, nv_ref):
        d, nm, nv = _adamw(g_ref[...], w_ref[...], m_ref[...], v_ref[...])
        d_ref[...] = d
        nm_ref[...] = nm
        nv_ref[...] = nv

    blk = g.shape
    zero = lambda: (0, 0)
    return _pcall(body, name=name, grid=(),
                  ins=[(t, blk, zero) for t in (g, w, m, v)],
                  outs=[(blk, F32, blk, zero)] * 3)


def _chip_slots():
    x, y, c = lax.axis_index("x"), lax.axis_index("y"), lax.axis_index("c")
    chips = [(1 - x, y), (x, 1 - y), (1 - x, 1 - y)]
    return x, y, c, chips


def device_index():
    x, y, c, chips = _chip_slots()
    return jnp.stack([4 * x + 2 * y + c, 2 * x + y] + [4 * cx + 2 * cy + c for cx, cy in chips]
                     + [2 * cx + cy for cx, cy in chips]).astype(jnp.int32)


def _job_rows(R, C, n_steps):
    if n_steps is None:
        n_steps = max(1, R * C // STREAM_BLOCK_ELEMS)
    n_blk = max([d for d in range(1, n_steps + 1) if R % d == 0 and (R // d) % 16 == 0] or [1])
    return R // n_blk, n_blk


def run_job(job, *, index, name, deps=()):
    jb = job(None)
    n_in = len(jb["ins"])

    def body(idx_ref, *refs):
        jb["fn"](refs[:n_in], refs[n_in:n_in + len(jb["outs"])])

    return _pcall(body, name=name, grid=(jb["n_blk"],), ins=jb["ins"], outs=jb["outs"], prefetch=index,
                  aliases={1 + a: o for a, o in jb["aliases"].items()}, semantics=("parallel",), deps=deps)


def adam_job(gs, a_buf, b_buf, w, m, v, layer, prev):
    L, R, C = w.shape

    def build(n_steps):
        tr, n_blk = _job_rows(R, C, n_steps)
        blk = (None, tr, C)
        row = lambda t: jnp.minimum(t, n_blk - 1)
        ins = [(gs, blk, lambda t, s: (s[0], row(t), 0)), (a_buf, blk, lambda t, s: (s[1], row(t), 0))]
        ins += [(b_buf, blk, lambda t, s, j=j: (j, row(t), 0)) for j in range(3)]
        ins += [(p, blk, lambda t, s: (layer, row(t), 0)) for p in (w, m, v)]
        ins += [(p, None, None) for p in (prev or [])]

        def fn(i, o):
            g = ((((i[0][...].astype(F32) + i[1][...].astype(F32)) + i[2][...].astype(F32))
                  + i[3][...].astype(F32)) + i[4][...].astype(F32))
            d, nm, nv = _adamw(g, i[5][...], i[6][...], i[7][...])
            o[0][...] = g
            o[1][...] = d
            o[2][...] = nm
            o[3][...] = nv

        return dict(ins=ins, outs=[((L, R, C), F32, blk, lambda t, s: (layer, row(t), 0))] * 4, fn=fn,
                    aliases={8 + o: o for o in range(4)} if prev else {}, n_blk=n_blk)

    return build


def pair_job(gs, a_buf):
    _, R, C = gs.shape

    def build(n_steps):
        tr, n_blk = _job_rows(R, C, n_steps)
        blk = (None, tr, C)
        row = lambda t: jnp.minimum(t, n_blk - 1)
        ins = [(gs, blk, lambda t, s, j=j: (s[2 + j], row(t), 0)) for j in range(3)]
        ins += [(a_buf, blk, lambda t, s, j=j: (s[5 + j], row(t), 0)) for j in range(3)]

        def fn(i, o):
            for j in range(3):
                o[0][j] = (i[j][...].astype(F32) + i[3 + j][...].astype(F32)).astype(BF)

        return dict(ins=ins, outs=[((3, R, C), BF, (3, tr, C), lambda t, s: (0, row(t), 0))], fn=fn, aliases={},
                    n_blk=n_blk)

    return build


def reduce_sum(gs, a_buf, b_buf, *, name):
    _, R, C = gs.shape
    tr = _tile(R, 256, 16)
    x, y, c, _ = _chip_slots()
    idx = jnp.stack([4 * x + 2 * y + c, 2 * x + y]).astype(jnp.int32)

    def body(idx_ref, g_ref, a_ref, b0_ref, b1_ref, b2_ref, o_ref):
        o_ref[...] = ((((g_ref[...].astype(F32) + a_ref[...].astype(F32)) + b0_ref[...].astype(F32))
                       + b1_ref[...].astype(F32)) + b2_ref[...].astype(F32))

    blk3 = (None, tr, C)
    return _pcall(body, name=name, grid=(R // tr,),
                  ins=[(gs, blk3, lambda i, s: (s[0], i, 0)), (a_buf, blk3, lambda i, s: (s[1], i, 0)),
                       (b_buf, blk3, lambda i, s: (0, i, 0)), (b_buf, blk3, lambda i, s: (1, i, 0)),
                       (b_buf, blk3, lambda i, s: (2, i, 0))],
                  outs=[((R, C), F32, (tr, C), lambda i, s: (i, 0))], prefetch=idx, semantics=("parallel",))[0]


def adam_rows(g, w, m, v, *, name):
    R, C = g.shape
    tr = _tile(R, 256, 8)

    def body(g_ref, w_ref, m_ref, v_ref, d_ref, nm_ref, nv_ref):
        d, nm, nv = _adamw(g_ref[...], w_ref[...], m_ref[...], v_ref[...])
        d_ref[...] = d
        nm_ref[...] = nm
        nv_ref[...] = nv

    spec = ((tr, C), lambda i: (i, 0))
    return _pcall(body, name=name, grid=(R // tr,), ins=[(t, *spec) for t in (g, w, m, v)],
                  outs=[((R, C), F32, *spec)] * 3, semantics=("parallel",))


def sum_rows8(gathered, rows, *, name):
    W = gathered.shape[1]

    def body(g_ref, o_ref):
        acc = g_ref[0:rows, :]
        for d in range(1, N_DEV):
            acc = acc + g_ref[d * rows:(d + 1) * rows, :]
        o_ref[...] = acc

    return _pcall(body, name=name, grid=(), ins=[(gathered, gathered.shape, lambda: (0, 0))],
                  outs=[((rows, W), F32, (rows, W), lambda: (0, 0))])[0]


HBM_SPEC = pl.BlockSpec(memory_space=pltpu.HBM)
SEM_SPEC = pl.BlockSpec(memory_space=pltpu.SEMAPHORE)
ANY_SPEC = pl.BlockSpec(memory_space=pl.ANY)
DATAFLOW = pltpu.SideEffectType.DATAFLOW_SIDE_EFFECTING


def _in_hbm(v):
    return pltpu.with_memory_space_constraint(v, pltpu.HBM)


def _slot(p):
    return 4 * p[0] + 2 * p[1] + p[2]


def _gather_peers():
    x, y, c, chips = _chip_slots()
    return (x, y, c), [(x, y, 1 - c)] + [(*chip, c) for chip in chips]


def gather_start(groups, after, *, name):
    flat = [s for g in groups for s in g]
    n, n_g = len(flat), len(groups)
    where = [(gi, ti) for gi, g in enumerate(groups) for ti in range(len(g))]

    def body(*refs):
        src, land = refs[:n], refs[n:2 * n]
        sems = refs[2 * n + 1:2 * n + 1 + 2 * n_g]
        me, peers = _gather_peers()
        for t in range(n):
            gi, ti = where[t]
            for k, to in enumerate(peers):
                pltpu.make_async_remote_copy(
                    src_ref=src[t], dst_ref=land[t].at[_slot(me)], send_sem=sems[2 * gi].at[4 * ti + k],
                    recv_sem=sems[2 * gi + 1].at[4 * ti + k], device_id=to, device_id_type=MESH).start()
        refs[-1][...] = jnp.zeros_like(refs[-1])

    out_shape = []
    for g in groups:
        out_shape += [pltpu.SemaphoreType.DMA((4 * len(g),)), pltpu.SemaphoreType.DMA((4 * len(g),))]
    out_shape += [pltpu.HBM(s.shape, s.dtype) for s in flat]
    out_shape += [pltpu.HBM((N_DEV,) + s.shape, s.dtype) for s in flat]
    out_shape += [jax.ShapeDtypeStruct((8, LANES), F32)]
    aliases = {t: 2 * n_g + t for t in range(n)}
    aliases.update({n + t: 2 * n_g + n + t for t in range(n)})
    res = pl.pallas_call(
        body, name=name, out_shape=out_shape, in_specs=[HBM_SPEC] * (2 * n) + [ANY_SPEC],
        out_specs=[SEM_SPEC] * (2 * n_g) + [HBM_SPEC] * (2 * n) + [pl.BlockSpec(memory_space=pltpu.VMEM)],
        input_output_aliases=aliases, compiler_params=pltpu.CompilerParams(has_side_effects=DATAFLOW),
    )(*[_in_hbm(s) for s in flat], *[_in_hbm(lax.empty((N_DEV,) + s.shape, s.dtype)) for s in flat], after)
    out, off = [], 0
    for gi, g in enumerate(groups):
        k = len(g)
        out.append((res[2 * gi], res[2 * gi + 1], res[2 * n_g + off:2 * n_g + off + k],
                    res[2 * n_g + n + off:2 * n_g + n + off + k]))
        off += k
    return out, res[-1]


def gather_wait(started, after, *, name):
    send_sems, recv_sems, srcs, lands = started
    n = len(srcs)
    after = list(after)

    def body(*refs):
        src, land = refs[:n], refs[n:2 * n]
        send, recv = refs[2 * n], refs[2 * n + 1]
        _, peers = _gather_peers()
        for t in range(n):
            for k, frm in enumerate(peers):
                cp = pltpu.make_async_remote_copy(
                    src_ref=src[t], dst_ref=land[t].at[_slot(frm)], send_sem=send.at[4 * t + k],
                    recv_sem=recv.at[4 * t + k],
                    device_id=frm, device_id_type=MESH)
                cp.wait_send()
                cp.wait_recv()

    res = pl.pallas_call(
        body, name=name,
        out_shape=[pltpu.HBM(s.shape, s.dtype) for s in srcs] + [pltpu.HBM(l.shape, l.dtype) for l in lands],
        in_specs=[HBM_SPEC] * (2 * n) + [SEM_SPEC, SEM_SPEC] + [ANY_SPEC] * len(after),
        out_specs=[HBM_SPEC] * (2 * n), input_output_aliases={t: t for t in range(2 * n)},
        compiler_params=pltpu.CompilerParams(has_side_effects=DATAFLOW),
    )(*srcs, *lands, send_sems, recv_sems, *after)
    return res[:n], res[n:]


def place_own(src, land, *, name):
    R, C = src.shape
    tr = _tile(R, 512, 16)
    x, y, c, _ = _chip_slots()
    idx = jnp.stack([4 * x + 2 * y + c]).astype(jnp.int32)

    def body(idx_ref, s_ref, land_ref, o_ref):
        o_ref[...] = s_ref[...]

    return _pcall(body, name=name, grid=(R // tr,),
                  ins=[(src, (tr, C), lambda i, s: (i, 0)), (land, None, None)],
                  outs=[(land.shape, land.dtype, (None, tr, C), lambda i, s: (s[0], i, 0))],
                  prefetch=idx, aliases={2: 0}, semantics=("parallel",))[0]


def gather_finish(srcs, lands, *, name):
    n = len(srcs)

    def body(*refs):
        land = refs[n:2 * n]
        send_sems, recv_sems = refs[2 * n:]
        x, y, c, chips = _chip_slots()
        me, sibling = (x, y, c), (x, y, 1 - c)

        def copy(t, j, block, to):
            return pltpu.make_async_remote_copy(
                src_ref=land[t].at[_slot(block)], dst_ref=land[t].at[_slot(block)], send_sem=send_sems.at[t, j],
                recv_sem=recv_sems.at[t, j], device_id=to, device_id_type=MESH)

        sends = [copy(t, j, (*chip, c), sibling) for t in range(n) for j, chip in enumerate(chips)]
        for cp in sends:
            cp.start()
        for t in range(n):
            for j, chip in enumerate(chips):
                copy(t, j, (*chip, 1 - c), me).wait_recv()
        for cp in sends:
            cp.wait_send()

    passed = pl.pallas_call(
        body, name=name, out_shape=[jax.ShapeDtypeStruct(l.shape, l.dtype) for l in lands],
        in_specs=[ANY_SPEC] * n, out_specs=[ANY_SPEC] * n,
        input_output_aliases={t: t for t in range(n)},
        scratch_shapes=[pltpu.SemaphoreType.DMA((n, 3)), pltpu.SemaphoreType.DMA((n, 3))],
    )(*lands)
    return [place_own(s, l, name=f"{name}_own{t}") for t, (s, l) in enumerate(zip(srcs, passed))]


def chips_start(pairs, *, name):
    n = len(pairs)

    def body(*refs):
        src, land = refs[:n], refs[n:2 * n]
        send, recv = refs[2 * n], refs[2 * n + 1]
        token = refs[-1]
        x, y, c, chips = _chip_slots()
        for t in range(n):
            for j, chip in enumerate(chips):
                pltpu.make_async_remote_copy(
                    src_ref=src[t].at[j], dst_ref=land[t].at[j], send_sem=send.at[3 * t + j],
                    recv_sem=recv.at[3 * t + j], device_id=(*chip, c), device_id_type=MESH).start()
        token[...] = jnp.zeros_like(token)

    res = pl.pallas_call(
        body, name=name,
        out_shape=[pltpu.SemaphoreType.DMA((3 * n,)), pltpu.SemaphoreType.DMA((3 * n,))]
        + [pltpu.HBM(p.shape, p.dtype) for p in pairs] * 2 + [jax.ShapeDtypeStruct((8, LANES), F32)],
        in_specs=[HBM_SPEC] * (2 * n),
        out_specs=[SEM_SPEC, SEM_SPEC] + [HBM_SPEC] * (2 * n) + [pl.BlockSpec(memory_space=pltpu.VMEM)],
        input_output_aliases={t: 2 + t for t in range(2 * n)},
        compiler_params=pltpu.CompilerParams(has_side_effects=DATAFLOW),
    )(*[_in_hbm(p) for p in pairs], *[_in_hbm(lax.empty(p.shape, p.dtype)) for p in pairs])
    return res[0], res[1], res[2:2 + n], res[2 + n:2 + 2 * n], res[-1]


def chips_wait(started, after, *, name):
    send_sems, recv_sems, srcs, lands, _ = started
    n = len(srcs)

    def body(*refs):
        src, land = refs[:n], refs[n:2 * n]
        send, recv = refs[2 * n], refs[2 * n + 1]
        x, y, c, chips = _chip_slots()
        for t in range(n):
            for j, chip in enumerate(chips):
                cp = pltpu.make_async_remote_copy(
                    src_ref=src[t].at[j], dst_ref=land[t].at[j], send_sem=send.at[3 * t + j],
                    recv_sem=recv.at[3 * t + j], device_id=(*chip, c), device_id_type=MESH)
                cp.wait_send()
                cp.wait_recv()

    res = pl.pallas_call(
        body, name=name, out_shape=[pltpu.HBM(s.shape, s.dtype) for s in srcs] * 2,
        in_specs=[HBM_SPEC] * (2 * n) + [SEM_SPEC, SEM_SPEC, ANY_SPEC], out_specs=[HBM_SPEC] * (2 * n),
        input_output_aliases={t: t for t in range(2 * n)},
        compiler_params=pltpu.CompilerParams(has_side_effects=DATAFLOW),
    )(*srcs, *lands, send_sems, recv_sems, after)
    return res[n:]


def _sibling_copies(src, land, send, recv, n):
    x, y, c, _ = _chip_slots()
    return [pltpu.make_async_remote_copy(
        src_ref=src[t].at[4 * (q // 2) + 2 * (q % 2) + (1 - c)], dst_ref=land[t].at[q], send_sem=send.at[4 * t + q],
        recv_sem=recv.at[4 * t + q], device_id=(x, y, 1 - c), device_id_type=MESH)
        for t in range(n) for q in range(4)]


def sibling_start(gs, *, name):
    n = len(gs)

    def body(*refs):
        for cp in _sibling_copies(refs[:n], refs[n:2 * n], refs[2 * n], refs[2 * n + 1], n):
            cp.start()
        refs[-1][...] = jnp.zeros_like(refs[-1])

    lands = [lax.empty((4,) + g.shape[1:], g.dtype) for g in gs]
    res = pl.pallas_call(
        body, name=name,
        out_shape=[pltpu.SemaphoreType.DMA((4 * n,)), pltpu.SemaphoreType.DMA((4 * n,))]
        + [pltpu.HBM(g.shape, g.dtype) for g in gs] + [pltpu.HBM(l.shape, l.dtype) for l in lands]
        + [jax.ShapeDtypeStruct((8, LANES), F32)],
        in_specs=[HBM_SPEC] * (2 * n),
        out_specs=[SEM_SPEC, SEM_SPEC] + [HBM_SPEC] * (2 * n) + [pl.BlockSpec(memory_space=pltpu.VMEM)],
        input_output_aliases={t: 2 + t for t in range(2 * n)},
        compiler_params=pltpu.CompilerParams(has_side_effects=DATAFLOW),
    )(*[_in_hbm(g) for g in gs], *[_in_hbm(l) for l in lands])
    return res[0], res[1], res[2:2 + n], res[2 + n:2 + 2 * n], res[-1]


def sibling_wait(started, after, *, name):
    send_sems, recv_sems, srcs, lands, _ = started
    n = len(srcs)

    def body(*refs):
        for cp in _sibling_copies(refs[:n], refs[n:2 * n], refs[2 * n], refs[2 * n + 1], n):
            cp.wait_send()
            cp.wait_recv()

    res = pl.pallas_call(
        body, name=name,
        out_shape=[pltpu.HBM(s.shape, s.dtype) for s in srcs] + [pltpu.HBM(l.shape, l.dtype) for l in lands],
        in_specs=[HBM_SPEC] * (2 * n) + [SEM_SPEC, SEM_SPEC, ANY_SPEC], out_specs=[HBM_SPEC] * (2 * n),
        input_output_aliases={t: t for t in range(2 * n)},
        compiler_params=pltpu.CompilerParams(has_side_effects=DATAFLOW),
    )(*srcs, *lands, send_sems, recv_sems, after)
    return res[:n], res[n:]


def all_gather_vmem(x_shard, *, name, after=None):
    m_per, n = x_shard.shape
    n_after = 0 if after is None else 1

    def body(x_ref, *rest):
        out_ref, send_sems, recv_sems, local_sem = rest[n_after:]
        x, y, c, _ = _chip_slots()
        me = (x, y, c)
        peers = [tuple(1 - v if (k + 1) >> b & 1 else v for b, v in enumerate(me)) for k in range(N_DEV - 1)]

        def rows(p):
            return out_ref.at[pl.ds(_slot(p) * m_per, m_per), :]

        def copy(k, block, to, src=None):
            return pltpu.make_async_remote_copy(
                src_ref=rows(block) if src is None else src, dst_ref=rows(block),
                send_sem=send_sems.at[k], recv_sem=recv_sems.at[k], device_id=to, device_id_type=MESH)

        mine = pltpu.make_async_copy(x_ref, rows(me), local_sem)
        mine.start()
        sends = [copy(k, me, peer, src=x_ref) for k, peer in enumerate(peers)]
        for cp in sends:
            cp.start()
        for k, peer in enumerate(peers):
            copy(k, peer, me).wait_recv()
        for cp in sends:
            cp.wait_send()
        mine.wait()

    vmem = pl.BlockSpec(memory_space=pltpu.VMEM)
    return pl.pallas_call(
        body, name=name, out_shape=jax.ShapeDtypeStruct((N_DEV * m_per, n), x_shard.dtype),
        in_specs=[vmem] + [ANY_SPEC] * n_after, out_specs=vmem,
        scratch_shapes=[pltpu.SemaphoreType.DMA((7,)), pltpu.SemaphoreType.DMA((7,)), pltpu.SemaphoreType.DMA],
        compiler_params=pltpu.CompilerParams(vmem_limit_bytes=int(min(
            VMEM_LIMIT_CAP, 2 * (N_DEV + 1) * m_per * n * x_shard.dtype.itemsize + 16 * 2 ** 20))),
    )(x_shard, *([] if after is None else [after]))


def _rope_slab(cols):
    z = jnp.zeros(cols.shape[:-1] + (HALF_ROPE,), cols.dtype)
    return jnp.concatenate([cols[..., :HALF_ROPE], z, cols[..., HALF_ROPE:], z], axis=-1)


def _rope_unslab(slab):
    return jnp.concatenate([slab[..., :HALF_ROPE], slab[..., 2 * HALF_ROPE:3 * HALF_ROPE]], axis=-1)


def _pack_w_in_t(wt_g):
    s, c, d = wt_g.shape
    w = wt_g.reshape(s * c, d)
    c2, c3 = Q_LORA + KV_LORA, Q_LORA + KV_LORA + QK_ROPE
    r = w[c2:c3]
    z = jnp.zeros((HALF_ROPE, d), w.dtype)
    return jnp.concatenate([w[:c2], w[c3:], r[:HALF_ROPE], z, r[HALF_ROPE:], z], axis=0)


def _unpack_w_in_t_grad(dwt):
    d = dwt.shape[1]
    c2 = Q_LORA + KV_LORA
    uv = 2 * SGU_OUT
    slab = dwt[c2 + uv:]
    g = jnp.concatenate([dwt[:c2], slab[:HALF_ROPE], slab[2 * HALF_ROPE:3 * HALF_ROPE], dwt[c2:c2 + uv]], axis=0)
    return g.reshape(N_DEV, g.shape[0] // N_DEV, d)


def _rope_tables(positions):
    inv_freq = ROPE_BASE ** (-jnp.arange(0, QK_ROPE, 2, dtype=F32) / QK_ROPE)
    ang = positions.astype(F32)[:, None] * inv_freq
    cos, sin = jnp.cos(ang), jnp.sin(ang)
    z = jnp.zeros_like(cos)
    return jnp.concatenate([cos, z, cos, z], axis=-1), jnp.concatenate([-sin, z, sin, z], axis=-1)


def _mlp_up(x, gain, w1, tag):
    hn = rms_fwd(x, gain, name=f"mlp{tag}_norm")

    def act_epi(acc):
        a = jnp.maximum(acc, 0.0)
        return a, a * a

    T = x.shape[0]
    F = w1.shape[0] * w1.shape[2]
    a, act = mm(hn, w1, name=f"mlp{tag}_up", outs=[((T, F), BF, None), ((T, F), BF, None)], epi=act_epi)
    return hn, a, act


def _mlp_down(x, act, w2, tag, part=0):
    n = w2.shape[1]
    bm = _tile(x.shape[0], MM_TILE)
    bn = _tile(n, MM_TILE)
    per = n // bn
    return mm(act, w2, name=f"mlp{tag}_down{part}", out=((x.shape[0], n), F32), bm=bm, bn=bn,
              epi=lambda acc, r: (acc + r[...],), epi_ins=[(x, (bm, bn), lambda i, j, k: (i, part * per + j))])


def _mlp_bwd_weights(w1, w2, saved, dxb, tag):
    hn, a, act = saved
    T, D = dxb.shape
    F = a.shape[1]
    bm = _tile(T, MM_TILE)
    bn = _tile(F, min(MM_TILE, w1.shape[2]))
    dhid = mm(dxb, w2, tb=True, name=f"mlp{tag}_dhid", out=((T, F), BF), bm=bm, bn=bn,
              epi=lambda acc, a_ref: (2.0 * a_ref[...].astype(F32) * acc,),
              epi_ins=[(a, (bm, bn), lambda i, j, k: (i, j))])
    dw2 = mm(act, dxb, ta=True, name=f"mlp{tag}_dw2", out=((F, D), BF))
    dw1 = mm(hn, dhid, ta=True, name=f"mlp{tag}_dw1", out=(w1.shape, BF))
    return dhid, dw1, dw2.reshape(N_DEV, F // N_DEV, D)


def _reduce_begin(grads, tag):
    return sibling_start(grads, name=f"reduce_sibling_start_{tag}")


def _reduce_continue(sib, after, tag, index):
    grads, a_bufs = sibling_wait(sib, after, name=f"reduce_sibling_wait_{tag}")
    pairs = [run_job(pair_job(g, a), index=index, name=f"pair_sum_{tag}{t}")[0]
             for t, (g, a) in enumerate(zip(grads, a_bufs))]
    return grads, a_bufs, chips_start(pairs, name=f"reduce_chips_start_{tag}")


def kernel(x, positions, e_norm_mix, e_w_in, e_q_norm, e_w_uq, e_kv_norm, e_w_ukv, e_v_norm, e_sgu_w, e_sgu_b, e_mla_out_norm, e_sgu_out_norm, e_w_out, o_norm_mix, o_w_in, o_conv_w, o_w_out, mlp_norm, mlp_w1, mlp_w2, final_norm, loss_target, m_e_norm_mix, m_e_w_in, m_e_q_norm, m_e_w_uq, m_e_kv_norm, m_e_w_ukv, m_e_v_norm, m_e_sgu_w, m_e_sgu_b, m_e_mla_out_norm, m_e_sgu_out_norm, m_e_w_out, m_o_norm_mix, m_o_w_in, m_o_conv_w, m_o_w_out, m_mlp_norm, m_mlp_w1, m_mlp_w2, m_final_norm, v_e_norm_mix, v_e_w_in, v_e_q_norm, v_e_w_uq, v_e_kv_norm, v_e_w_ukv, v_e_v_norm, v_e_sgu_w, v_e_sgu_b, v_e_mla_out_norm, v_e_sgu_out_norm, v_e_w_out, v_o_norm_mix, v_o_w_in, v_o_conv_w, v_o_w_out, v_mlp_norm, v_mlp_w1, v_mlp_w2, v_final_norm):
    T, D = x.shape[1], x.shape[2]
    d_shard = o_norm_mix.shape[1]
    x0 = x[0]
    target = loss_target[0]
    me = 4 * lax.axis_index("x") + 2 * lax.axis_index("y") + lax.axis_index("c")

    bf = lambda s: s.astype(BF)
    gather_groups = [[bf(jnp.transpose(e_w_in[0])), bf(e_w_uq[0]), bf(e_w_ukv[0])], [bf(e_w_out[0]), bf(mlp_w1[0])],
                     [bf(mlp_w2[0]), bf(o_w_in[0])], [bf(o_w_out[0]), bf(mlp_w1[1])], [bf(mlp_w2[1])]]
    small_rows = jnp.concatenate([o_norm_mix, o_conv_w[0], jnp.zeros((4, d_shard), F32)], axis=0)
    small_flat = all_gather_vmem(small_rows, name="gather_small")
    started, start_token = gather_start(gather_groups[:1], small_flat, name="gather_start0")
    started_rest, rest_token = gather_start(gather_groups[1:], start_token, name="gather_start1")
    started += started_rest

    def gathered(gi, after):
        srcs, lands = gather_wait(started[gi], after, name=f"gather_wait{gi}")
        return gather_finish(srcs, lands, name=f"gather_finish{gi}")

    small_g = small_flat.reshape(N_DEV, 8, d_shard)
    o_norm_full = small_g[:, 0, :].reshape(1, D)
    conv_w_full = jnp.transpose(small_g[:, 1:4, :], (1, 0, 2)).reshape(3, D)
    w_tril = jnp.tril(e_sgu_w[0])
    w_tril_b = w_tril.astype(BF)
    w_tril_tb = jnp.swapaxes(w_tril, 1, 2).astype(BF)
    b_full = jnp.repeat(e_sgu_b[0].T, CH, axis=1)
    v_gain = e_v_norm[0].reshape(1, SGU_OUT)
    cos_t, sin_t = _rope_tables(positions[0])
    mlp_gain = [mlp_norm[0:1], mlp_norm[1:2]]
    final_gain = final_norm.reshape(1, D)

    h0 = rms_fwd(x0, e_norm_mix, name="e_norm", deps=[rest_token])
    g_w_in_t, g_w_uq, w_ukv = gathered(
        0, [h0, cos_t, sin_t, w_tril_b, w_tril_tb, b_full, o_norm_full, conv_w_full])
    w_in_t = _pack_w_in_t(g_w_in_t)
    w_uq = jnp.concatenate([g_w_uq[..., :QK_NOPE], _rope_slab(g_w_uq[..., QK_NOPE:])], axis=-1)
    proj = mm(h0, w_in_t, tb=True, name="e_in", out=((T, w_in_t.shape[0]), F32), bn=_tile(w_in_t.shape[0], 640))
    qn, kvn, krope = mla_prep(proj, e_q_norm, e_kv_norm, cos_t, sin_t, name="mla_prep")
    bm = _tile(T, MM_TILE)

    def q_epi(acc, cos_ref, sin_ref):
        return (jnp.concatenate([acc[:, :QK_NOPE], _rope_fwd(acc[:, QK_NOPE:], cos_ref[...], sin_ref[...])], axis=-1),)

    q = mm(qn, w_uq, name="mla_q", out=((T, HEADS * HEAD_PAD), BF), bm=bm, bn=HEAD_PAD, epi=q_epi,
           epi_ins=[(cos_t, (bm, LANES), lambda i, j, k: (i, 0)), (sin_t, (bm, LANES), lambda i, j, k: (i, 0))])

    def kv_epi(acc, kr_ref):
        return jnp.concatenate([acc[:, :QK_NOPE].astype(BF), kr_ref[...]], axis=-1), acc[:, QK_NOPE:]

    k, v = mm(kvn, w_ukv, name="mla_kv", bm=bm, bn=HEAD_PAD, epi=kv_epi,
              outs=[((T, HEADS * HEAD_PAD), BF, HEAD_PAD), ((T, MLA_OUT), BF, V_HEAD)],
              epi_ins=[(krope, (bm, LANES), lambda i, j, k: (i, 0))])
    attn, attn_lse = attn_fwd(q, k, v, name="attn_fwd")
    mixed = mix_fwd(attn, proj, e_mla_out_norm, e_sgu_out_norm, v_gain, w_tril_b, b_full, name="mix_fwd")
    bn = _tile(D, MM_TILE)
    g_w_out_e, w1_0 = gathered(1, [mixed])
    w_out_e = g_w_out_e.reshape(-1, D)
    x1 = mm(mixed, w_out_e, name="e_out", out=((T, D), F32), bm=bm, bn=bn,
            epi=lambda acc, r: (acc + r[...],), epi_ins=[(x0, (bm, bn), lambda i, j, k: (i, j))])
    hn0, a0, act0 = _mlp_up(x1, mlp_gain[0], w1_0, 0)
    g_w2_0, g_w_in_o = gathered(2, [act0])
    w2_0 = g_w2_0.reshape(-1, D)
    x2 = _mlp_down(x1, act0, w2_0, 0)
    ho = rms_fwd(x2, o_norm_full, name="o_norm")
    proj_o = mm(ho, g_w_in_o, name="o_in", out=((T, 3 * D), F32))
    gated = conv_fwd(proj_o, conv_w_full, name="conv_fwd")
    g_w_out_o, w1_1 = gathered(3, [gated])
    w_out_o = g_w_out_o.reshape(-1, D)
    x3 = mm(gated, w_out_o, name="o_out", out=((T, D), F32), bm=bm, bn=bn,
            epi=lambda acc, r: (acc + r[...],), epi_ins=[(x2, (bm, bn), lambda i, j, k: (i, j))])
    hn1, a1, act1 = _mlp_up(x3, mlp_gain[1], w1_1, 1)
    (g_w2_1,) = gathered(4, [act1])
    w2_1 = g_w2_1.reshape(-1, D)
    x4 = _mlp_down(x3, act1, w2_1, 1)
    w1, w2 = [w1_0, w1_1], [w2_0, w2_1]

    dx4, dx4b, d_final, loss_part = loss_bwd([x4], final_gain, target, name="loss_bwd")

    hosted = dict(job_index=device_index())
    dhid1, dw1_1, dw2_1 = _mlp_bwd_weights(w1[1], w2[1], (hn1, a1, act1), dx4b, 1)
    sib_r0 = _reduce_begin([dw1_1, dw2_1], "r0")
    dhn1 = mm(dhid1, w1[1], tb=True, name="mlp1_dhn", out=((T, D), F32), deps=[sib_r0[-1]])
    grads_r0, a_r0 = sibling_wait(sib_r0, dhn1, name="reduce_sibling_wait_r0")
    dx3, dx3b, d_mlp1 = rms_bwd(x3, mlp_gain[1], dhn1, dres=dx4, name="mlp1_norm_bwd")

    dgated, ((pair_r0a,),) = mm(dx3b, w_out_o, tb=True, name="o_out_dx", out=((T, D), F32),
                                jobs=[pair_job(grads_r0[0], a_r0[0])], **hosted)
    dw_out_o, ((pair_r0b,),) = mm(gated, dx3b, ta=True, name="o_out_dw", out=((D, D), BF),
                                  jobs=[pair_job(grads_r0[1], a_r0[1])], **hosted)
    st_r0 = chips_start([pair_r0a, pair_r0b], name="reduce_chips_start_r0")
    dproj_o, dconv_full = conv_bwd(dgated, proj_o, conv_w_full, name="conv_bwd", deps=[st_r0[-1]])
    dw_in_o = mm(ho, dproj_o, ta=True, name="o_in_dw", out=(g_w_in_o.shape, BF))
    sib_r1 = _reduce_begin([dw_out_o.reshape(g_w_out_o.shape), dw_in_o], "r1")
    dho = mm(dproj_o, g_w_in_o, tb=True, name="o_in_dx", out=((T, D), F32), deps=[sib_r1[-1]])
    grads_r1, a_r1 = sibling_wait(sib_r1, dho, name="reduce_sibling_wait_r1")
    dx2, dx2b, d_onorm_full = rms_bwd(x2, o_norm_full, dho, dres=dx3, name="o_norm_bwd")

    d_ff = a0.shape[1]
    bm_h, bn_h = _tile(T, MM_TILE), _tile(d_ff, min(MM_TILE, w1[0].shape[2]))
    dhid0, ((pair_r1a,), (pair_r1b,)) = mm(
        dx2b, w2[0], tb=True, name="mlp0_dhid", out=((T, d_ff), BF), bm=bm_h, bn=bn_h,
        epi=lambda acc, a_ref: (2.0 * a_ref[...].astype(F32) * acc,),
        epi_ins=[(a0, (bm_h, bn_h), lambda i, j, k: (i, j))],
        jobs=[pair_job(grads_r1[0], a_r1[0]), pair_job(grads_r1[1], a_r1[1])], **hosted)
    st_r1 = chips_start([pair_r1a, pair_r1b], name="reduce_chips_start_r1")
    dw2_0 = mm(act0, dx2b, ta=True, name="mlp0_dw2", out=((d_ff, D), BF), deps=[st_r1[-1]])
    b_r0 = chips_wait(st_r0, dw2_0, name="reduce_chips_wait_r0")
    dw1_0, (r_w1, r_w2) = mm(
        hn0, dhid0, ta=True, name="mlp0_dw1", out=(w1[0].shape, BF),
        jobs=[adam_job(grads_r0[0], a_r0[0], b_r0[0], mlp_w1, m_mlp_w1, v_mlp_w1, 1, None),
              adam_job(grads_r0[1], a_r0[1], b_r0[1], mlp_w2, m_mlp_w2, v_mlp_w2, 1, None)], **hosted)
    sib_r2 = _reduce_begin([dw1_0, dw2_0.reshape(N_DEV, d_ff // N_DEV, D)], "r2")
    dhn0 = mm(dhid0, w1[0], tb=True, name="mlp0_dhn", out=((T, D), F32), deps=[sib_r2[-1]])
    grads_r2, a_r2 = sibling_wait(sib_r2, dhn0, name="reduce_sibling_wait_r2")
    dx1, dx1b, d_mlp0 = rms_bwd(x1, mlp_gain[0], dhn0, dres=dx2, name="mlp0_norm_bwd")

    dmixed, ((pair_r2a,),) = mm(dx1b, w_out_e, tb=True, name="e_out_dx", out=((T, MLA_OUT + SGU_OUT), F32),
                                jobs=[pair_job(grads_r2[0], a_r2[0])], **hosted)
    dw_out_e, ((pair_r2b,),) = mm(mixed, dx1b, ta=True, name="e_out_dw", out=(w_out_e.shape, BF),
                                  jobs=[pair_job(grads_r2[1], a_r2[1])], **hosted)
    st_r2 = chips_start([pair_r2a, pair_r2b], name="reduce_chips_start_r2")
    (dattn, duv, d_mla_out, d_sgu_out, d_vgain, d_sgu_w, d_b_full) = mix_bwd(
        dmixed, attn, proj, e_mla_out_norm, e_sgu_out_norm, v_gain, w_tril_b, w_tril_tb, b_full, name="mix_bwd",
        deps=[st_r2[-1]])
    b_r1 = chips_wait(st_r1, dattn, name="reduce_chips_wait_r1")
    dq, dk, dv = attn_bwd(q, k, v, attn, attn_lse, dattn, name="attn_bwd")
    dq_lin, dkv_lin, dkr = mla_bwd_prep(dq, dk, dv, cos_t, sin_t, name="mla_bwd_prep")
    dw_uq_pad = mm(qn, dq_lin, ta=True, name="mla_q_dw", out=(w_uq.shape, BF))
    dw_ukv = mm(kvn, dkv_lin, ta=True, name="mla_kv_dw", out=(w_ukv.shape, BF))
    dw_uq = jnp.concatenate([dw_uq_pad[..., :QK_NOPE], _rope_unslab(dw_uq_pad[..., QK_NOPE:])], axis=-1)
    sib_r2b = _reduce_begin([dw_out_e.reshape(g_w_out_e.shape), dw_uq, dw_ukv], "r2b")
    dqn = mm(dq_lin, w_uq, tb=True, name="mla_q_dx", out=((T, Q_LORA), F32), deps=[sib_r2b[-1]])
    dkvn = mm(dkv_lin, w_ukv, tb=True, name="mla_kv_dx", out=((T, KV_LORA), F32), deps=[sib_r2b[-1]])
    grads_r2b, a_r2b, st_r2b = _reduce_continue(sib_r2b, dkvn, "r2b", hosted["job_index"])
    dcq, d_qnorm = rms_bwd(proj, e_q_norm, dqn, col_block=0, want_f32=False, name="q_norm_bwd", deps=[st_r2b[-1]])
    dckv, d_kvnorm = rms_bwd(proj, e_kv_norm, dkvn, col_block=1, want_f32=False, name="kv_norm_bwd")
    dproj = jnp.concatenate([dcq, dckv, duv, dkr], axis=-1)
    dw_in_t_pad, (r_w_out_o, r_w_in_o) = mm(
        dproj, h0, ta=True, name="e_in_dw", out=(w_in_t.shape, BF), bm=_tile(w_in_t.shape[0], 640),
        jobs=[adam_job(grads_r1[0], a_r1[0], b_r1[0], o_w_out, m_o_w_out, v_o_w_out, 0, None),
              adam_job(grads_r1[1], a_r1[1], b_r1[1], o_w_in, m_o_w_in, v_o_w_in, 0, None)], **hosted)
    dw_in_t = _unpack_w_in_t_grad(dw_in_t_pad)
    sib_r3 = _reduce_begin([dw_in_t], "r3")
    dh0 = mm(dproj, w_in_t, name="e_in_dx", out=((T, D), F32), deps=[sib_r3[-1]])
    grads_r3, a_r3, st_r3 = _reduce_continue(sib_r3, dh0, "r3", hosted["job_index"])
    tok_r3 = st_r3[-1]
    grad_x, d_enorm = rms_bwd(x0, e_norm_mix, dh0, dres=dx1, want_bf=False, name="e_norm_bwd", deps=[tok_r3])
    b_r2 = chips_wait(st_r2, grad_x, name="reduce_chips_wait_r2")

    def finish(grads, a_bufs, b_bufs, t, w, m, v, layer=0, prev=None, tag="", deps=()):
        return run_job(adam_job(grads[t], a_bufs[t], b_bufs[t], w, m, v, layer, prev), index=hosted["job_index"],
                       name=f"adam_{tag}", deps=deps)

    r_w1 = finish(grads_r2, a_r2, b_r2, 0, mlp_w1, m_mlp_w1, v_mlp_w1, 0, r_w1, tag="w1_l0", deps=[tok_r3])
    r_w2 = finish(grads_r2, a_r2, b_r2, 1, mlp_w2, m_mlp_w2, v_mlp_w2, 0, r_w2, tag="w2_l0", deps=[r_w1[1]])
    b_r2b = chips_wait(st_r2b, r_w2[1], name="reduce_chips_wait_r2b")
    r_w_out_e = finish(grads_r2b, a_r2b, b_r2b, 0, e_w_out, m_e_w_out, v_e_w_out, tag="e_w_out")
    r_w_uq = finish(grads_r2b, a_r2b, b_r2b, 1, e_w_uq, m_e_w_uq, v_e_w_uq, tag="e_w_uq")
    r_w_ukv = finish(grads_r2b, a_r2b, b_r2b, 2, e_w_ukv, m_e_w_ukv, v_e_w_ukv, tag="e_w_ukv")
    b_r3 = chips_wait(st_r3, r_w_out_e[1], name="reduce_chips_wait_r3")
    g_w_in_t = reduce_sum(grads_r3[0], a_r3[0], b_r3[0], name="sum_e_w_in")
    w_in_upd_t = adam_rows(g_w_in_t, jnp.transpose(e_w_in[0]), jnp.transpose(m_e_w_in[0]), jnp.transpose(v_e_w_in[0]),
                           name="adam_e_w_in")
    r_w_in = [jnp.transpose(t)[None] for t in (g_w_in_t, *w_in_upd_t)]

    d_sgu_b = jnp.transpose(d_b_full[:, ::CH])
    d_sgu_w_tril = jnp.tril(d_sgu_w)
    rep = [("e_norm_mix", e_norm_mix, m_e_norm_mix, v_e_norm_mix, d_enorm),
           ("e_q_norm", e_q_norm, m_e_q_norm, v_e_q_norm, d_qnorm),
           ("e_kv_norm", e_kv_norm, m_e_kv_norm, v_e_kv_norm, d_kvnorm),
           ("e_v_norm", e_v_norm, m_e_v_norm, v_e_v_norm, d_vgain),
           ("e_sgu_w", e_sgu_w, m_e_sgu_w, v_e_sgu_w, d_sgu_w_tril),
           ("e_sgu_b", e_sgu_b, m_e_sgu_b, v_e_sgu_b, d_sgu_b),
           ("e_mla_out_norm", e_mla_out_norm, m_e_mla_out_norm, v_e_mla_out_norm, d_mla_out),
           ("e_sgu_out_norm", e_sgu_out_norm, m_e_sgu_out_norm, v_e_sgu_out_norm, d_sgu_out),
           ("mlp_norm", mlp_norm, m_mlp_norm, v_mlp_norm, jnp.concatenate([d_mlp0, d_mlp1], axis=0)),
           ("final_norm", final_norm, m_final_norm, v_final_norm, d_final)]
    sizes = [int(np.prod(r[1].shape)) for r in rep]
    n_rep = sum(sizes)
    n_all = n_rep + 4 * D + 1
    width = -(-n_all // (8 * LANES)) * LANES
    pad = 8 * width - n_all
    flat = jnp.concatenate([r[4].reshape(-1) for r in rep]
                           + [d_onorm_full.reshape(-1), dconv_full.reshape(-1), loss_part[0, :1],
                              jnp.zeros((pad,), F32)])
    summed = sum_rows8(all_gather_vmem(flat.reshape(8, width), name="gather_small_grads", after=b_r3[0]), 8,
                       name="sum_small_grads").reshape(-1)

    loss = summed[n_rep + 4 * D]

    def pack_rep(i):
        return jnp.concatenate([r[i].reshape(-1) for r in rep]).reshape(n_rep // LANES, LANES)

    g_rep = summed[:n_rep].reshape(n_rep // LANES, LANES)
    d_rep, nm_rep, nv_rep = adam_flat(g_rep, pack_rep(1), pack_rep(2), pack_rep(3), name="adam_replicated")

    def unpack_rep(flat2d):
        out, off = {}, 0
        f = flat2d.reshape(-1)
        for r, n in zip(rep, sizes):
            out[r[0]] = f[off:off + n].reshape(r[1].shape)
            off += n
        return out

    small = {"grad": unpack_rep(g_rep), "delta": unpack_rep(d_rep), "new_m": unpack_rep(nm_rep),
             "new_v": unpack_rep(nv_rep)}
    g_onorm = lax.dynamic_slice(summed[n_rep:n_rep + D].reshape(1, D), (0, me * d_shard), (1, d_shard))
    g_conv = lax.dynamic_slice(summed[n_rep + D:n_rep + 4 * D].reshape(3, D), (0, me * d_shard), (3, d_shard))

    def pack_sharded(norm_part, conv_part):
        return jnp.concatenate([norm_part, conv_part, jnp.zeros((4, d_shard), F32)], axis=0)

    g_sh = pack_sharded(g_onorm, g_conv)
    d_sh, nm_sh, nv_sh = adam_flat(g_sh, pack_sharded(o_norm_mix, o_conv_w[0]), pack_sharded(m_o_norm_mix, m_o_conv_w[0]),
                                   pack_sharded(v_o_norm_mix, v_o_conv_w[0]), name="adam_sharded_small")
    for kind, arr in (("grad", g_sh), ("delta", d_sh), ("new_m", nm_sh), ("new_v", nv_sh)):
        small[kind]["o_norm_mix"] = arr[0:1]
        small[kind]["o_conv_w"] = arr[1:4][None]

    big = {"e_w_in": r_w_in, "e_w_uq": r_w_uq, "e_w_ukv": r_w_ukv, "e_w_out": r_w_out_e, "o_w_in": r_w_in_o,
           "o_w_out": r_w_out_o, "mlp_w1": r_w1, "mlp_w2": r_w2}
    order = ["e_norm_mix", "e_w_in", "e_q_norm", "e_w_uq", "e_kv_norm", "e_w_ukv", "e_v_norm", "e_sgu_w", "e_sgu_b",
             "e_mla_out_norm", "e_sgu_out_norm", "e_w_out", "o_norm_mix", "o_w_in", "o_conv_w", "o_w_out", "mlp_norm",
             "mlp_w1", "mlp_w2", "final_norm"]
    result = [loss, grad_x[None]]
    for ki, kind in enumerate(("grad", "delta", "new_m", "new_v")):
        for nm in order:
            result.append(big[nm][ki] if nm in big else small[kind][nm])
    return tuple(result)
```

```python
import numpy as np
import jax
import jax.numpy as jnp
from jax import lax
from jax.experimental import pallas as pl
from jax.experimental.pallas import tpu as pltpu

BF = jnp.bfloat16
F32 = jnp.float32
MESH = pl.DeviceIdType.MESH
N_DEV = 8

EPS = 1e-6
HEADS = 8
Q_LORA = 512
KV_LORA = 512
QK_NOPE = 128
QK_ROPE = 64
HALF_ROPE = QK_ROPE // 2
V_HEAD = 128
HEAD_PAD = 256
ROPE_BASE = 10000.0
GROUPS = 8
CH = 128
CHUNK = 128
SGU_OUT = GROUPS * CH
MLA_OUT = HEADS * V_HEAD
ATTN_SCALE = float((QK_NOPE + QK_ROPE) ** -0.5)

ADAM_LR = 0.001
ADAM_B1 = 0.9
ADAM_B2 = 0.999
ADAM_EPS = 1e-08
ADAM_WD = 0.01
ADAM_STEP = 10
ADAM_C1 = 1.0 - ADAM_B1 ** ADAM_STEP
ADAM_C2 = 1.0 - ADAM_B2 ** ADAM_STEP

V7X_VMEM_BYTES = 64 * 2 ** 20
VMEM_LIMIT_CAP = V7X_VMEM_BYTES - 6 * 2 ** 20
LANES = 128
ROW_TILE = 256
ATTN_TILE = 512
STREAM_BLOCK_ELEMS = 512 * 1024
MM_TILE = 1024
MM_K_TILE = 2048
MM_K_BLOCK_MAX = 3072


def _padded_bytes(block, dtype):
    dims = [d for d in block if d is not None]
    if len(dims) >= 1:
        dims[-1] = -(-dims[-1] // LANES) * LANES
    if len(dims) >= 2:
        dims[-2] = -(-dims[-2] // 16) * 16
    return int(np.prod(dims)) * jnp.dtype(dtype).itemsize


def _pcall(body, *, name, grid, ins, outs, scratch=(), semantics=None, aliases=None, prefetch=None, deps=()):
    any_spec = pl.BlockSpec(memory_space=pl.ANY)
    if deps:
        n_lead = len(ins) + (1 if prefetch is not None else 0)
        n_deps = len(deps)
        inner = body

        def body(*refs):
            inner(*refs[:n_lead], *refs[n_lead + n_deps:])

        ins = list(ins) + [(d, None, None) for d in deps]
    in_specs = [any_spec if b is None else pl.BlockSpec(b, m) for _, b, m in ins]
    out_specs = [any_spec if b is None else pl.BlockSpec(b, m) for _, _, b, m in outs]
    out_shape = [pltpu.HBM(s, d) for s, d, _, _ in outs]
    est = 0
    for a, b, _ in ins:
        if b is not None:
            est += 2 * _padded_bytes(b, a.dtype)
    for _, d, b, _ in outs:
        if b is not None:
            est += 2 * _padded_bytes(b, d)
    for s in scratch:
        if hasattr(s, "shape") and hasattr(s, "dtype"):
            est += _padded_bytes(s.shape, s.dtype)
    limit = int(min(VMEM_LIMIT_CAP, est + 16 * 2 ** 20))
    params = pltpu.CompilerParams(
        dimension_semantics=semantics or ("arbitrary",) * len(grid), vmem_limit_bytes=limit)
    args = [pltpu.with_memory_space_constraint(a, pltpu.HBM) for a, _, _ in ins]
    if prefetch is not None:
        grid_spec = pltpu.PrefetchScalarGridSpec(
            num_scalar_prefetch=1, grid=grid, in_specs=in_specs, out_specs=out_specs, scratch_shapes=list(scratch))
        call = pl.pallas_call(body, out_shape=out_shape, grid_spec=grid_spec, name=name, compiler_params=params,
                              input_output_aliases=aliases or {})
        return call(prefetch, *args)
    call = pl.pallas_call(body, out_shape=out_shape, grid=grid, in_specs=in_specs, out_specs=out_specs,
                          scratch_shapes=list(scratch), name=name, compiler_params=params,
                          input_output_aliases=aliases or {})
    return call(*args)


def _tile(dim, pref, quantum=LANES):
    if dim <= pref:
        return dim
    t = (pref // quantum) * quantum
    while t >= quantum:
        if dim % t == 0:
            return t
        t -= quantum
    return dim


def _vshape(arr_shape):
    if len(arr_shape) == 2:
        return tuple(arr_shape)
    s, r, c = arr_shape
    return (r, s * c)


def _vblock(arr_shape, br, bc, rc):
    if len(arr_shape) == 2:
        return (br, bc), (lambda *g: rc(*g))
    _, _, c = arr_shape
    assert c % bc == 0, (arr_shape, bc)
    per = c // bc

    def imap(*g):
        ri, ci = rc(*g)
        return (ci // per, ri, ci % per)

    return (None, br, bc), imap


def _shard_width(*shapes):
    w = None
    for s in shapes:
        if len(s) == 3:
            w = s[2] if w is None else int(np.gcd(w, s[2]))
    return w


def mm(a, b, *, name, ta=False, tb=False, out=None, outs=None, epi=None, epi_ins=(), bm=None, bn=None, bk=None,
       deps=(), jobs=(), job_index=None):
    av, bv = _vshape(a.shape), _vshape(b.shape)
    M, K = (av[1], av[0]) if ta else av
    K2, N = (bv[1], bv[0]) if tb else bv
    assert K == K2, (a.shape, b.shape, ta, tb)
    if outs is None:
        outs = [(out[0], out[1], None)]
    a_sw = _shard_width(a.shape)
    b_sw = _shard_width(b.shape)
    o_sw = _shard_width(*[o[0] for o in outs])
    m_lim = a_sw if (ta and a_sw) else None
    k_lim = [w for w in ((a_sw if not ta else None), (b_sw if tb else None)) if w]
    n_lim = [w for w in ((b_sw if not tb else None), o_sw) if w]
    if bm is None:
        bm = _tile(M, min([MM_TILE] + ([m_lim] if m_lim else [])))
    if bn is None:
        bn = _tile(N, min([MM_TILE] + n_lim))
    k_shards = 0
    if tb and len(b.shape) == 3 and bk is None and not (a_sw and not ta):
        k_shards = 1
        while 2 * k_shards <= b.shape[0] and 2 * k_shards * b_sw <= MM_K_BLOCK_MAX:
            k_shards *= 2
        bk = k_shards * b_sw
    if bk is None:
        bk = K if (K <= 4096 and not k_lim) else _tile(K, min([MM_K_TILE] + k_lim))
    assert M % bm == 0 and N % bn == 0 and K % bk == 0, (name, M, N, K, bm, bn, bk)
    nk = K // bk
    grid = (M // bm, N // bn, nk)
    if ta:
        a_blk, a_map = _vblock(a.shape, bk, bm, lambda i, j, k: (k, i))
    else:
        a_blk, a_map = _vblock(a.shape, bm, bk, lambda i, j, k: (i, k))
    if k_shards:
        b_blk, b_map = (k_shards, bn, b_sw), (lambda i, j, k: (k, j, 0))
    elif tb:
        b_blk, b_map = _vblock(b.shape, bn, bk, lambda i, j, k: (j, k))
    else:
        b_blk, b_map = _vblock(b.shape, bk, bn, lambda i, j, k: (k, j))
    dn = (((0 if ta else 1,), (1 if tb else 0,)), ((), ()))
    ins = [(a, a_blk, a_map), (b, b_blk, b_map)] + list(epi_ins)
    out_list = []
    for shape, dtype, cols in outs:
        cols = cols or bn
        blk, imap = _vblock(shape, bm, cols, lambda i, j, k: (i, j))
        out_list.append((shape, dtype, blk, imap))
    n_e, n_o = len(epi_ins), len(out_list)

    n_steps = grid[0] * grid[1] * nk
    built = [job(n_steps) for job in jobs]
    aliases = {}
    job_slices = []
    if built:
        def lin(i, j, k):
            return (i * grid[1] + j) * nk + k

        ins = [(arr, blk, None if blk is None else (lambda i, j, k, s, f=f: f(i, j, k))) for arr, blk, f in ins]
        out_list = [(sh, dt, blk, (lambda i, j, k, s, f=f: f(i, j, k))) for sh, dt, blk, f in out_list]
        n_main_in, n_main_out = len(ins), len(out_list)
        for jb in built:
            i0, o0 = len(ins), len(out_list)
            ins += [(arr, blk, None if blk is None else (lambda i, j, k, s, f=f: f(lin(i, j, k), s)))
                    for arr, blk, f in jb["ins"]]
            out_list += [(sh, dt, blk, (lambda i, j, k, s, f=f: f(lin(i, j, k), s))) for sh, dt, blk, f in jb["outs"]]
            aliases.update({1 + i0 + ai: o0 + ao for ai, ao in jb["aliases"].items()})
            job_slices.append((i0, len(jb["ins"]), o0, len(jb["outs"])))
    n_in_total = len(ins)

    def body(*refs):
        if built:
            refs = refs[1:]
        a_ref, b_ref = refs[0], refs[1]
        e_refs = refs[2:2 + n_e]
        o_refs = refs[n_in_total:n_in_total + n_o]
        for jb, (i0, ni, o0, no) in zip(built, job_slices):
            jb["fn"](refs[i0:i0 + ni], refs[n_in_total + o0:n_in_total + o0 + no])

        def finish(acc):
            res = epi(acc, *e_refs) if epi is not None else (acc,)
            for o_ref, r in zip(o_refs, res):
                o_ref[...] = r.astype(o_ref.dtype)

        x = a_ref[...].astype(BF)
        y = b_ref[...].astype(BF)
        if k_shards:
            p = None
            for s in range(k_shards):
                part = lax.dot_general(x[:, s * b_sw:(s + 1) * b_sw], y[s], dn, preferred_element_type=F32)
                p = part if p is None else p + part
        else:
            p = lax.dot_general(x, y, dn, preferred_element_type=F32)
        if nk == 1:
            finish(p)
        else:
            acc_ref = refs[-1]
            k = pl.program_id(2)

            @pl.when(k == 0)
            def _():
                acc_ref[...] = p

            @pl.when(k > 0)
            def _():
                acc_ref[...] += p

            @pl.when(k == nk - 1)
            def _():
                finish(acc_ref[...])

    scratch = [pltpu.VMEM((bm, bn), F32)] if nk > 1 else []
    res = _pcall(body, name=name, grid=grid, ins=ins, outs=out_list, scratch=scratch, deps=deps,
                 semantics=("parallel", "parallel", "arbitrary"), prefetch=job_index if built else None, aliases=aliases)
    main = res[0] if n_o == 1 else res[:n_o]
    if not built:
        return main
    return main, [res[o0:o0 + no] for _, _, o0, no in job_slices]


_GELU_K = float(np.sqrt(2.0 / np.pi))
_GELU_C = 0.044715


def _gelu(x):
    t = jnp.tanh(_GELU_K * (x + _GELU_C * (x * x * x)))
    return 0.5 * x * (1.0 + t)


def _gelu_grad(x):
    t = jnp.tanh(_GELU_K * (x + _GELU_C * (x * x * x)))
    return 0.5 * (1.0 + t) + 0.5 * x * (1.0 - t * t) * (_GELU_K * (1.0 + 3.0 * _GELU_C * (x * x)))


def _rstd(x):
    return lax.rsqrt(jnp.mean(x * x, axis=-1, keepdims=True) + EPS)


def _rms_bwd(x, gain, dy):
    r = _rstd(x)
    xh = x * r
    gdy = dy * gain
    dx = r * (gdy - xh * jnp.mean(gdy * xh, axis=-1, keepdims=True))
    return dx, dy * xh


def _rope_fwd(x, cos_t, sin_t):
    return x * cos_t + pltpu.roll(x, 2 * HALF_ROPE, 1) * sin_t


def _rope_bwd(dy, cos_t, sin_t):
    return dy * cos_t + pltpu.roll(dy * sin_t, 2 * HALF_ROPE, 1)


def _acc_rows(ref, val, first):
    s = jnp.sum(val, axis=0, keepdims=True)

    @pl.when(first)
    def _():
        ref[...] = s

    @pl.when(jnp.logical_not(first))
    def _():
        ref[...] += s


def rms_fwd(x, gain, *, name, col_block=0, width=None, deps=()):
    T = x.shape[0]
    width = width or x.shape[1]
    tm = _tile(T, ROW_TILE, 8)

    def body(x_ref, g_ref, o_ref):
        v = x_ref[...]
        o_ref[...] = (v * _rstd(v) * g_ref[...]).astype(BF)

    return _pcall(body, name=name, grid=(T // tm,),
                  ins=[(x, (tm, width), lambda i: (i, col_block)), (gain, (1, width), lambda i: (0, 0))],
                  outs=[((T, width), BF, (tm, width), lambda i: (i, 0))], semantics=("parallel",), deps=deps)[0]


def rms_bwd(x, gain, dy, *, name, col_block=0, dres=None, want_f32=True, want_bf=True, into=None, deps=()):
    T, width = dy.shape
    tm = _tile(T, ROW_TILE, 8)
    has_res = dres is not None

    def body(*refs):
        x_ref, g_ref, dy_ref = refs[:3]
        pos = 3
        res_ref = None
        if has_res:
            res_ref = refs[pos]
            pos += 1
        if into is not None:
            pos += 1
        outs = refs[pos:]
        dx, dg_rows = _rms_bwd(x_ref[...], g_ref[...], dy_ref[...])
        if has_res:
            dx = dx + res_ref[...]
        o = 0
        if want_f32:
            outs[o][...] = dx
            o += 1
        if want_bf:
            outs[o][...] = dx.astype(BF)
            o += 1
        _acc_rows(outs[o], dg_rows, pl.program_id(0) == 0)

    ins = [(x, (tm, width), lambda i: (i, col_block)), (gain, (1, width), lambda i: (0, 0)),
           (dy, (tm, width), lambda i: (i, 0))]
    if has_res:
        ins.append((dres, (tm, width), lambda i: (i, 0)))
    outs = []
    aliases = {}
    if want_f32:
        outs.append(((T, width), F32, (tm, width), lambda i: (i, 0)))
    if want_bf and into is not None:
        ins.append((into, None, None))
        aliases[len(ins) - 1] = len(outs)
        outs.append((into.shape, BF, (tm, width), lambda i: (i, col_block)))
    elif want_bf:
        outs.append(((T, width), BF, (tm, width), lambda i: (i, 0)))
    outs.append(((1, width), F32, (1, width), lambda i: (0, 0)))
    return _pcall(body, name=name, grid=(T // tm,), ins=ins, outs=outs, aliases=aliases, deps=deps)


def mla_prep(proj, q_norm, kv_norm, cos_t, sin_t, *, name):
    T = proj.shape[0]
    tm = _tile(T, ROW_TILE, 8)
    kr_block = (proj.shape[1] - LANES) // LANES

    def body(cq_ref, ckv_ref, kr_ref, qg_ref, kg_ref, cos_ref, sin_ref, qn_ref, kvn_ref, krope_ref):
        cq = cq_ref[...]
        qn_ref[...] = (cq * _rstd(cq) * qg_ref[...]).astype(BF)
        ckv = ckv_ref[...]
        kvn_ref[...] = (ckv * _rstd(ckv) * kg_ref[...]).astype(BF)
        krope_ref[...] = _rope_fwd(kr_ref[...], cos_ref[...], sin_ref[...]).astype(BF)

    return _pcall(
        body, name=name, grid=(T // tm,),
        ins=[(proj, (tm, Q_LORA), lambda i: (i, 0)), (proj, (tm, KV_LORA), lambda i: (i, 1)),
             (proj, (tm, LANES), lambda i: (i, kr_block)),
             (q_norm, (1, Q_LORA), lambda i: (0, 0)), (kv_norm, (1, KV_LORA), lambda i: (0, 0)),
             (cos_t, (tm, LANES), lambda i: (i, 0)), (sin_t, (tm, LANES), lambda i: (i, 0))],
        outs=[((T, Q_LORA), BF, (tm, Q_LORA), lambda i: (i, 0)), ((T, KV_LORA), BF, (tm, KV_LORA), lambda i: (i, 0)),
              ((T, LANES), BF, (tm, LANES), lambda i: (i, 0))],
        semantics=("parallel",))


def _attn_scores(q, k_blk, diagonal):
    s = lax.dot_general(q, k_blk, (((1,), (1,)), ((), ())), preferred_element_type=F32) * ATTN_SCALE
    if diagonal:
        row = lax.broadcasted_iota(jnp.int32, s.shape, 0)
        col = lax.broadcasted_iota(jnp.int32, s.shape, 1)
        s = jnp.where(col <= row, s, -jnp.inf)
    return s


def attn_fwd(q, k, v, *, name):
    T = q.shape[0]
    tq = _tile(T, ATTN_TILE, 8)

    def body(q_ref, k_ref, v_ref, o_ref, lse_ref):
        i = pl.program_id(1)
        qv = q_ref[...]

        def block(kb, carry, diagonal):
            m, l, acc = carry
            start = pl.multiple_of(kb * tq, tq)
            s = _attn_scores(qv, k_ref[pl.ds(start, tq), :], diagonal)
            m_new = jnp.maximum(m, jnp.max(s, axis=-1, keepdims=True))
            alpha = jnp.exp(m - m_new)
            p = jnp.exp(s - m_new)
            l = alpha * l + jnp.sum(p, axis=-1, keepdims=True)
            acc = alpha * acc + jnp.dot(p.astype(BF), v_ref[pl.ds(start, tq), :], preferred_element_type=F32)
            return m_new, l, acc

        init = (jnp.full((tq, 1), -jnp.inf, F32), jnp.zeros((tq, 1), F32), jnp.zeros((tq, V_HEAD), F32))
        carry = lax.fori_loop(0, i, lambda kb, c: block(kb, c, False), init)
        m, l, acc = block(i, carry, True)
        o_ref[...] = acc / l
        lse_ref[...] = jnp.broadcast_to(m + jnp.log(l), (tq, V_HEAD))

    return _pcall(
        body, name=name, grid=(HEADS, T // tq),
        ins=[(q, (tq, HEAD_PAD), lambda h, i: (i, h)), (k, (T, HEAD_PAD), lambda h, i: (0, h)),
             (v, (T, V_HEAD), lambda h, i: (0, h))],
        outs=[((T, MLA_OUT), F32, (tq, V_HEAD), lambda h, i: (i, h)),
              ((T, MLA_OUT), F32, (tq, V_HEAD), lambda h, i: (i, h))], semantics=("parallel", "parallel"))


def attn_bwd(q, k, v, o, lse, do, *, name):
    T = q.shape[0]
    tq = _tile(T, ATTN_TILE, 8)

    def body(q_ref, k_ref, v_ref, o_ref, lse_ref, do_ref, dq_ref, dk_ref, dv_ref):
        i = pl.program_id(1)

        @pl.when(i == 0)
        def _():
            dk_ref[...] = jnp.zeros_like(dk_ref)
            dv_ref[...] = jnp.zeros_like(dv_ref)

        qv = q_ref[...]
        do_t = do_ref[...]
        lse_v = lse_ref[:, 0:1]
        delta = jnp.sum(do_t.astype(F32) * o_ref[...], axis=-1, keepdims=True)

        def block(kb, dq, diagonal):
            start = pl.multiple_of(kb * tq, tq)
            k_blk = k_ref[pl.ds(start, tq), :]
            v_blk = v_ref[pl.ds(start, tq), :]
            p = jnp.exp(_attn_scores(qv, k_blk, diagonal) - lse_v)
            dp = lax.dot_general(do_t, v_blk, (((1,), (1,)), ((), ())), preferred_element_type=F32)
            ds = (p * (dp - delta) * ATTN_SCALE).astype(BF)
            dk_ref[pl.ds(start, tq), :] += lax.dot_general(ds, qv, (((0,), (0,)), ((), ())), preferred_element_type=F32)
            dv_ref[pl.ds(start, tq), :] += lax.dot_general(p.astype(BF), do_t, (((0,), (0,)), ((), ())),
                                                          preferred_element_type=F32)
            return dq + jnp.dot(ds, k_blk, preferred_element_type=F32)

        dq = lax.fori_loop(0, i, lambda kb, c: block(kb, c, False), jnp.zeros((tq, HEAD_PAD), F32))
        dq_ref[...] = block(i, dq, True)

    return _pcall(
        body, name=name, grid=(HEADS, T // tq),
        ins=[(q, (tq, HEAD_PAD), lambda h, i: (i, h)), (k, (T, HEAD_PAD), lambda h, i: (0, h)),
             (v, (T, V_HEAD), lambda h, i: (0, h)), (o, (tq, V_HEAD), lambda h, i: (i, h)),
             (lse, (tq, V_HEAD), lambda h, i: (i, h)), (do, (tq, V_HEAD), lambda h, i: (i, h))],
        outs=[((T, HEADS * HEAD_PAD), F32, (tq, HEAD_PAD), lambda h, i: (i, h)),
              ((T, HEADS * HEAD_PAD), F32, (T, HEAD_PAD), lambda h, i: (0, h)),
              ((T, MLA_OUT), F32, (T, V_HEAD), lambda h, i: (0, h))],
        semantics=("parallel", "arbitrary"))


def mla_bwd_prep(dq, dk, dv, cos_t, sin_t, dproj, *, name):
    T = dq.shape[0]
    tm = _tile(T, ROW_TILE, 8)
    kr_block = (dproj.shape[1] - LANES) // LANES

    def body(dq_ref, dk_ref, dv_ref, cos_ref, sin_ref, dproj_in, dql_ref, dkvl_ref, dkr_ref):
        cos_v, sin_v = cos_ref[...], sin_ref[...]
        kr = jnp.zeros((tm, LANES), F32)
        for h in range(HEADS):
            lo = h * HEAD_PAD
            dql_ref[:, lo:lo + QK_NOPE] = dq_ref[:, lo:lo + QK_NOPE].astype(BF)
            dql_ref[:, lo + QK_NOPE:lo + HEAD_PAD] = _rope_bwd(
                dq_ref[:, lo + QK_NOPE:lo + HEAD_PAD], cos_v, sin_v).astype(BF)
            dkvl_ref[:, lo:lo + QK_NOPE] = dk_ref[:, lo:lo + QK_NOPE].astype(BF)
            dkvl_ref[:, lo + QK_NOPE:lo + HEAD_PAD] = dv_ref[:, h * V_HEAD:(h + 1) * V_HEAD].astype(BF)
            kr = kr + dk_ref[:, lo + QK_NOPE:lo + HEAD_PAD]
        dkr_ref[...] = _rope_bwd(kr, cos_v, sin_v).astype(BF)

    W = HEADS * HEAD_PAD
    return _pcall(
        body, name=name, grid=(T // tm,),
        ins=[(dq, (tm, W), lambda i: (i, 0)), (dk, (tm, W), lambda i: (i, 0)), (dv, (tm, MLA_OUT), lambda i: (i, 0)),
             (cos_t, (tm, LANES), lambda i: (i, 0)), (sin_t, (tm, LANES), lambda i: (i, 0)), (dproj, None, None)],
        outs=[((T, W), BF, (tm, W), lambda i: (i, 0)), ((T, W), BF, (tm, W), lambda i: (i, 0)),
              (dproj.shape, BF, (tm, LANES), lambda i: (i, kr_block))],
        aliases={5: 2}, semantics=("parallel",))


def _group_norm_stats(vg):
    mu = jnp.mean(vg, axis=-1, keepdims=True)
    d = vg - mu
    r = lax.rsqrt(jnp.mean(d * d, axis=-1, keepdims=True) + EPS)
    return d * r, r


def mix_fwd(a, proj, g_mla, g_sgu, v_gain, w_tril, b_full, *, name):
    T = a.shape[0]
    tm = _tile(T, ROW_TILE, CHUNK)
    n_chunk = tm // CHUNK

    def body(a_ref, u_ref, v_ref, gm_ref, gs_ref, vg_ref, w_ref, b_ref, o_ref, s_scr):
        av = a_ref[...]
        o_ref[:, :MLA_OUT] = (av * _rstd(av) * gm_ref[...]).astype(BF)
        for g in range(GROUPS):
            sl = slice(g * CH, (g + 1) * CH)
            vhat, _ = _group_norm_stats(_gelu(v_ref[:, sl]))
            vn = (vhat * vg_ref[:, sl]).astype(BF)
            u = _gelu(u_ref[:, sl])
            for ci in range(n_chunk):
                rs = slice(ci * CHUNK, (ci + 1) * CHUNK)
                y = jnp.dot(w_ref[g], vn[rs], preferred_element_type=F32) + b_ref[:, sl]
                s_scr[rs, sl] = u[rs] * y
        s = s_scr[...]
        o_ref[:, MLA_OUT:] = (s * _rstd(s) * gs_ref[...]).astype(BF)

    return _pcall(
        body, name=name, grid=(T // tm,),
        ins=[(a, (tm, MLA_OUT), lambda i: (i, 0)), (proj, (tm, SGU_OUT), lambda i: (i, 1)),
             (proj, (tm, SGU_OUT), lambda i: (i, 2)), (g_mla, (1, MLA_OUT), lambda i: (0, 0)),
             (g_sgu, (1, SGU_OUT), lambda i: (0, 0)), (v_gain, (1, SGU_OUT), lambda i: (0, 0)),
             (w_tril, (GROUPS, CHUNK, CHUNK), lambda i: (0, 0, 0)), (b_full, (CHUNK, SGU_OUT), lambda i: (0, 0))],
        outs=[((T, MLA_OUT + SGU_OUT), BF, (tm, MLA_OUT + SGU_OUT), lambda i: (i, 0))],
        scratch=[pltpu.VMEM((tm, SGU_OUT), F32)], semantics=("parallel",))[0]


def mix_bwd(dmixed, a, proj, g_mla, g_sgu, v_gain, w_tril, w_tril_t, b_full, *, name, deps=()):
    T = a.shape[0]
    tm = _tile(T, ROW_TILE, CHUNK)
    n_chunk = tm // CHUNK
    uv0 = Q_LORA + KV_LORA

    def body(dm_a_ref, dm_s_ref, a_ref, u_ref, v_ref, gm_ref, gs_ref, vg_ref, w_ref, wt_ref, b_ref,
             da_ref, duv_ref, dgm_ref, dgs_ref, dvg_ref, dw_ref, db_ref, s_scr, y_scr):
        first = pl.program_id(0) == 0
        duv_ref[:, :uv0] = jnp.zeros((tm, uv0), BF)
        duv_ref[:, uv0 + 2 * SGU_OUT:] = jnp.zeros((tm, duv_ref.shape[1] - uv0 - 2 * SGU_OUT), BF)
        da, dgm_rows = _rms_bwd(a_ref[...], gm_ref[...], dm_a_ref[...])
        da_ref[...] = da.astype(BF)
        _acc_rows(dgm_ref, dgm_rows, first)

        for g in range(GROUPS):
            sl = slice(g * CH, (g + 1) * CH)
            vhat, _ = _group_norm_stats(_gelu(v_ref[:, sl]))
            vn = (vhat * vg_ref[:, sl]).astype(BF)
            u = _gelu(u_ref[:, sl])
            for ci in range(n_chunk):
                rs = slice(ci * CHUNK, (ci + 1) * CHUNK)
                y = jnp.dot(w_ref[g], vn[rs], preferred_element_type=F32) + b_ref[:, sl]
                y_scr[rs, sl] = y
                s_scr[rs, sl] = u[rs] * y
        ds, dgs_rows = _rms_bwd(s_scr[...], gs_ref[...], dm_s_ref[...])
        _acc_rows(dgs_ref, dgs_rows, first)
        s_scr[...] = ds

        @pl.when(first)
        def _():
            dw_ref[...] = jnp.zeros_like(dw_ref)
            db_ref[...] = jnp.zeros_like(db_ref)

        for g in range(GROUPS):
            sl = slice(g * CH, (g + 1) * CH)
            upre = u_ref[:, sl]
            vpre = v_ref[:, sl]
            u = _gelu(upre)
            vhat, r = _group_norm_stats(_gelu(vpre))
            gain = vg_ref[:, sl]
            vn = (vhat * gain).astype(BF)
            dsg = s_scr[:, sl]
            duv_ref[:, uv0 + g * CH:uv0 + (g + 1) * CH] = (dsg * y_scr[:, sl] * _gelu_grad(upre)).astype(BF)
            dy = dsg * u
            dyb = dy.astype(BF)
            dvn_parts = []
            for ci in range(n_chunk):
                rs = slice(ci * CHUNK, (ci + 1) * CHUNK)
                dvn_parts.append(jnp.dot(wt_ref[g], dyb[rs], preferred_element_type=F32))
                dw_ref[g] += lax.dot_general(dyb[rs], vn[rs], (((1,), (1,)), ((), ())), preferred_element_type=F32)
                db_ref[:, sl] += jnp.broadcast_to(jnp.sum(dy[rs], axis=-1, keepdims=True), (CHUNK, CH))
            dvn = dvn_parts[0] if n_chunk == 1 else jnp.concatenate(dvn_parts, axis=0)
            _acc_rows(dvg_ref.at[:, sl], dvn * vhat, first)
            dvh = dvn * gain
            dvg = r * (dvh - jnp.mean(dvh, axis=-1, keepdims=True)
                       - vhat * jnp.mean(dvh * vhat, axis=-1, keepdims=True))
            duv_ref[:, uv0 + SGU_OUT + g * CH:uv0 + SGU_OUT + (g + 1) * CH] = (dvg * _gelu_grad(vpre)).astype(BF)

    return _pcall(
        body, name=name, grid=(T // tm,),
        ins=[(dmixed, (tm, MLA_OUT), lambda i: (i, 0)), (dmixed, (tm, SGU_OUT), lambda i: (i, 1)),
             (a, (tm, MLA_OUT), lambda i: (i, 0)), (proj, (tm, SGU_OUT), lambda i: (i, 1)),
             (proj, (tm, SGU_OUT), lambda i: (i, 2)), (g_mla, (1, MLA_OUT), lambda i: (0, 0)),
             (g_sgu, (1, SGU_OUT), lambda i: (0, 0)), (v_gain, (1, SGU_OUT), lambda i: (0, 0)),
             (w_tril, (GROUPS, CHUNK, CHUNK), lambda i: (0, 0, 0)), (w_tril_t, (GROUPS, CHUNK, CHUNK), lambda i: (0, 0, 0)),
             (b_full, (CHUNK, SGU_OUT), lambda i: (0, 0))],
        outs=[((T, MLA_OUT), BF, (tm, MLA_OUT), lambda i: (i, 0)),
              ((T, proj.shape[1]), BF, (tm, proj.shape[1]), lambda i: (i, 0)),
              ((1, MLA_OUT), F32, (1, MLA_OUT), lambda i: (0, 0)), ((1, SGU_OUT), F32, (1, SGU_OUT), lambda i: (0, 0)),
              ((1, SGU_OUT), F32, (1, SGU_OUT), lambda i: (0, 0)),
              ((GROUPS, CHUNK, CHUNK), F32, (GROUPS, CHUNK, CHUNK), lambda i: (0, 0, 0)),
              ((CHUNK, SGU_OUT), F32, (CHUNK, SGU_OUT), lambda i: (0, 0))],
        scratch=[pltpu.VMEM((tm, SGU_OUT), F32), pltpu.VMEM((tm, SGU_OUT), F32)], deps=deps)


def _shift_down(z, n, row):
    return jnp.where(row >= n, pltpu.roll(z, n, 0), 0.0)


def _shift_up(z, n, row, T):
    return jnp.where(row < T - n, pltpu.roll(z, T - n, 0), 0.0)


def conv_fwd(proj, conv_w, *, name):
    T, D3 = proj.shape
    D = D3 // 3
    tn = _tile(D, 256)
    nj = D // tn

    def body(b_ref, c_ref, x_ref, w_ref, o_ref):
        row = lax.broadcasted_iota(jnp.int32, (T, tn), 0)
        z = c_ref[...] * x_ref[...]
        zc = w_ref[2:3, :] * z + w_ref[1:2, :] * _shift_down(z, 1, row) + w_ref[0:1, :] * _shift_down(z, 2, row)
        o_ref[...] = (b_ref[...] * zc).astype(BF)

    return _pcall(
        body, name=name, grid=(nj,),
        ins=[(proj, (T, tn), lambda j: (0, j)), (proj, (T, tn), lambda j: (0, nj + j)),
             (proj, (T, tn), lambda j: (0, 2 * nj + j)), (conv_w, (3, tn), lambda j: (0, j))],
        outs=[((T, D), BF, (T, tn), lambda j: (0, j))], semantics=("parallel",))[0]


def conv_bwd(dg, proj, conv_w, *, name, deps=()):
    T, D3 = proj.shape
    D = D3 // 3
    tn = _tile(D, 256)
    nj = D // tn

    def body(dg_ref, b_ref, c_ref, x_ref, w_ref, dp_ref, dw_ref, dc_scr, dx_scr):
        part = pl.program_id(1)

        @pl.when(part == 0)
        def _():
            row = lax.broadcasted_iota(jnp.int32, (T, tn), 0)
            c, x = c_ref[...], x_ref[...]
            z = c * x
            z1 = _shift_down(z, 1, row)
            z2 = _shift_down(z, 2, row)
            dgv = dg_ref[...]
            zc = w_ref[2:3, :] * z + w_ref[1:2, :] * z1 + w_ref[0:1, :] * z2
            dp_ref[...] = (dgv * zc).astype(BF)
            dzc = dgv * b_ref[...]
            dw_ref[0:1, :] = jnp.sum(dzc * z2, axis=0, keepdims=True)
            dw_ref[1:2, :] = jnp.sum(dzc * z1, axis=0, keepdims=True)
            dw_ref[2:3, :] = jnp.sum(dzc * z, axis=0, keepdims=True)
            dz = (w_ref[2:3, :] * dzc + w_ref[1:2, :] * _shift_up(dzc, 1, row, T)
                  + w_ref[0:1, :] * _shift_up(dzc, 2, row, T))
            dc_scr[...] = (dz * x).astype(BF)
            dx_scr[...] = (dz * c).astype(BF)

        @pl.when(part == 1)
        def _():
            dp_ref[...] = dc_scr[...]

        @pl.when(part == 2)
        def _():
            dp_ref[...] = dx_scr[...]

    return _pcall(
        body, name=name, grid=(nj, 3),
        ins=[(dg, (T, tn), lambda j, p: (0, j)), (proj, (T, tn), lambda j, p: (0, j)),
             (proj, (T, tn), lambda j, p: (0, nj + j)), (proj, (T, tn), lambda j, p: (0, 2 * nj + j)),
             (conv_w, (3, tn), lambda j, p: (0, j))],
        outs=[((T, D3), BF, (T, tn), lambda j, p: (0, p * nj + j)), ((3, D), F32, (3, tn), lambda j, p: (0, j))],
        scratch=[pltpu.VMEM((T, tn), BF), pltpu.VMEM((T, tn), BF)], semantics=("parallel", "arbitrary"), deps=deps)


def loss_bwd(x_parts, gain, target, *, name):
    T, D = target.shape
    tm = _tile(T, ROW_TILE, 8)
    n_x = len(x_parts)

    def body(*refs):
        x_refs = refs[:n_x]
        g_ref, t_ref, dx_ref, dxb_ref, dg_ref, loss_ref = refs[n_x:]
        first = pl.program_id(0) == 0
        xv = jnp.concatenate([r[...] for r in x_refs], axis=-1) if n_x > 1 else x_refs[0][...]
        r = _rstd(xv)
        xh = xv * r
        gain_v = g_ref[...]
        err = xh * gain_v - t_ref[...]
        part = 0.5 * jnp.sum(jnp.mean(err * err, axis=-1, keepdims=True), axis=0, keepdims=True)
        _acc_rows(loss_ref, jnp.broadcast_to(part, (1, LANES)), first)
        dy = err * (1.0 / D)
        gdy = dy * gain_v
        dx = r * (gdy - xh * jnp.mean(gdy * xh, axis=-1, keepdims=True))
        dx_ref[...] = dx
        dxb_ref[...] = dx.astype(BF)
        _acc_rows(dg_ref, dy * xh, first)

    return _pcall(
        body, name=name, grid=(T // tm,),
        ins=[(p, (tm, D // n_x), lambda i: (i, 0)) for p in x_parts]
        + [(gain, (1, D), lambda i: (0, 0)), (target, (tm, D), lambda i: (i, 0))],
        outs=[((T, D), F32, (tm, D), lambda i: (i, 0)), ((T, D), BF, (tm, D), lambda i: (i, 0)),
              ((1, D), F32, (1, D), lambda i: (0, 0)), ((1, LANES), F32, (1, LANES), lambda i: (0, 0))])


def _adamw(g, w, m, v):
    m = ADAM_B1 * m + (1.0 - ADAM_B1) * g
    v = ADAM_B2 * v + (1.0 - ADAM_B2) * (g * g)
    m_hat = m / ADAM_C1
    v_hat = v / ADAM_C2
    delta = -ADAM_LR * (m_hat / (jnp.sqrt(v_hat) + ADAM_EPS) + ADAM_WD * w)
    return delta, m, v


def adam_flat(g, w, m, v, *, name):
    def body(g_ref, w_ref, m_ref, v_ref, d_ref, nm_ref, nv_ref):
        d, nm, nv = _adamw(g_ref[...], w_ref[...], m_ref[...], v_ref[...])
        d_ref[...] = d
        nm_ref[...] = nm
        nv_ref[...] = nv

    blk = g.shape
    zero = lambda: (0, 0)
    return _pcall(body, name=name, grid=(),
                  ins=[(t, blk, zero) for t in (g, w, m, v)],
                  outs=[(blk, F32, blk, zero)] * 3)


def _chip_slots():
    x, y, c = lax.axis_index("x"), lax.axis_index("y"), lax.axis_index("c")
    chips = [(1 - x, y), (x, 1 - y), (1 - x, 1 - y)]
    return x, y, c, chips


def device_index():
    x, y, c, chips = _chip_slots()
    return jnp.stack([4 * x + 2 * y + c, 2 * x + y] + [4 * cx + 2 * cy + c for cx, cy in chips]
                     + [2 * cx + cy for cx, cy in chips]).astype(jnp.int32)


def _job_rows(R, C, n_steps):
    if n_steps is None:
        n_steps = max(1, R * C // STREAM_BLOCK_ELEMS)
    n_blk = max([d for d in range(1, n_steps + 1) if R % d == 0 and (R // d) % 16 == 0] or [1])
    return R // n_blk, n_blk


def run_job(job, *, index, name, deps=()):
    jb = job(None)
    n_in = len(jb["ins"])

    def body(idx_ref, *refs):
        jb["fn"](refs[:n_in], refs[n_in:n_in + len(jb["outs"])])

    return _pcall(body, name=name, grid=(jb["n_blk"],), ins=jb["ins"], outs=jb["outs"], prefetch=index,
                  aliases={1 + a: o for a, o in jb["aliases"].items()}, semantics=("parallel",), deps=deps)


def adam_job(gs, a_buf, b_buf, w, m, v, layer, prev):
    L, R, C = w.shape

    def build(n_steps):
        tr, n_blk = _job_rows(R, C, n_steps)
        blk = (None, tr, C)
        row = lambda t: jnp.minimum(t, n_blk - 1)
        ins = [(gs, blk, lambda t, s: (s[0], row(t), 0)), (a_buf, blk, lambda t, s: (s[1], row(t), 0))]
        ins += [(b_buf, blk, lambda t, s, j=j: (j, row(t), 0)) for j in range(3)]
        ins += [(p, blk, lambda t, s: (layer, row(t), 0)) for p in (w, m, v)]
        ins += [(p, None, None) for p in (prev or [])]

        def fn(i, o):
            g = ((((i[0][...].astype(F32) + i[1][...].astype(F32)) + i[2][...].astype(F32))
                  + i[3][...].astype(F32)) + i[4][...].astype(F32))
            d, nm, nv = _adamw(g, i[5][...], i[6][...], i[7][...])
            o[0][...] = g
            o[1][...] = d
            o[2][...] = nm
            o[3][...] = nv

        return dict(ins=ins, outs=[((L, R, C), F32, blk, lambda t, s: (layer, row(t), 0))] * 4, fn=fn,
                    aliases={8 + o: o for o in range(4)} if prev else {}, n_blk=n_blk)

    return build


def pair_job(gs, a_buf):
    _, R, C = gs.shape

    def build(n_steps):
        tr, n_blk = _job_rows(R, C, n_steps)
        blk = (None, tr, C)
        row = lambda t: jnp.minimum(t, n_blk - 1)
        ins = [(gs, blk, lambda t, s, j=j: (s[2 + j], row(t), 0)) for j in range(3)]
        ins += [(a_buf, blk, lambda t, s, j=j: (s[5 + j], row(t), 0)) for j in range(3)]

        def fn(i, o):
            for j in range(3):
                o[0][j] = (i[j][...].astype(F32) + i[3 + j][...].astype(F32)).astype(BF)

        return dict(ins=ins, outs=[((3, R, C), BF, (3, tr, C), lambda t, s: (0, row(t), 0))], fn=fn, aliases={},
                    n_blk=n_blk)

    return build


def reduce_sum(gs, a_buf, b_buf, *, name):
    _, R, C = gs.shape
    tr = _tile(R, 256, 16)
    x, y, c, _ = _chip_slots()
    idx = jnp.stack([4 * x + 2 * y + c, 2 * x + y]).astype(jnp.int32)

    def body(idx_ref, g_ref, a_ref, b0_ref, b1_ref, b2_ref, o_ref):
        o_ref[...] = ((((g_ref[...].astype(F32) + a_ref[...].astype(F32)) + b0_ref[...].astype(F32))
                       + b1_ref[...].astype(F32)) + b2_ref[...].astype(F32))

    blk3 = (None, tr, C)
    return _pcall(body, name=name, grid=(R // tr,),
                  ins=[(gs, blk3, lambda i, s: (s[0], i, 0)), (a_buf, blk3, lambda i, s: (s[1], i, 0)),
                       (b_buf, blk3, lambda i, s: (0, i, 0)), (b_buf, blk3, lambda i, s: (1, i, 0)),
                       (b_buf, blk3, lambda i, s: (2, i, 0))],
                  outs=[((R, C), F32, (tr, C), lambda i, s: (i, 0))], prefetch=idx, semantics=("parallel",))[0]


def adam_rows(g, w, m, v, *, name):
    R, C = g.shape
    tr = _tile(R, 256, 8)

    def body(g_ref, w_ref, m_ref, v_ref, d_ref, nm_ref, nv_ref):
        d, nm, nv = _adamw(g_ref[...], w_ref[...], m_ref[...], v_ref[...])
        d_ref[...] = d
        nm_ref[...] = nm
        nv_ref[...] = nv

    spec = ((tr, C), lambda i: (i, 0))
    return _pcall(body, name=name, grid=(R // tr,), ins=[(t, *spec) for t in (g, w, m, v)],
                  outs=[((R, C), F32, *spec)] * 3, semantics=("parallel",))


def sum_rows8(gathered, rows, *, name):
    W = gathered.shape[1]

    def body(g_ref, o_ref):
        acc = g_ref[0:rows, :]
        for d in range(1, N_DEV):
            acc = acc + g_ref[d * rows:(d + 1) * rows, :]
        o_ref[...] = acc

    return _pcall(body, name=name, grid=(), ins=[(gathered, gathered.shape, lambda: (0, 0))],
                  outs=[((rows, W), F32, (rows, W), lambda: (0, 0))])[0]


HBM_SPEC = pl.BlockSpec(memory_space=pltpu.HBM)
SEM_SPEC = pl.BlockSpec(memory_space=pltpu.SEMAPHORE)
ANY_SPEC = pl.BlockSpec(memory_space=pl.ANY)
DATAFLOW = pltpu.SideEffectType.DATAFLOW_SIDE_EFFECTING


def _in_hbm(v):
    return pltpu.with_memory_space_constraint(v, pltpu.HBM)


def _slot(p):
    return 4 * p[0] + 2 * p[1] + p[2]


def _gather_peers():
    x, y, c, chips = _chip_slots()
    return (x, y, c), [(x, y, 1 - c)] + [(*chip, c) for chip in chips]


def gather_start(groups, after, *, name):
    flat = [s for g in groups for s in g]
    n, n_g = len(flat), len(groups)
    where = [(gi, ti) for gi, g in enumerate(groups) for ti in range(len(g))]

    def body(*refs):
        src, land = refs[:n], refs[n:2 * n]
        sems = refs[2 * n + 1:2 * n + 1 + 2 * n_g]
        me, peers = _gather_peers()
        for t in range(n):
            gi, ti = where[t]
            for k, to in enumerate(peers):
                pltpu.make_async_remote_copy(
                    src_ref=src[t], dst_ref=land[t].at[_slot(me)], send_sem=sems[2 * gi].at[4 * ti + k],
                    recv_sem=sems[2 * gi + 1].at[4 * ti + k], device_id=to, device_id_type=MESH).start()
        refs[-1][...] = jnp.zeros_like(refs[-1])

    out_shape = []
    for g in groups:
        out_shape += [pltpu.SemaphoreType.DMA((4 * len(g),)), pltpu.SemaphoreType.DMA((4 * len(g),))]
    out_shape += [pltpu.HBM(s.shape, s.dtype) for s in flat]
    out_shape += [pltpu.HBM((N_DEV,) + s.shape, s.dtype) for s in flat]
    out_shape += [jax.ShapeDtypeStruct((8, LANES), F32)]
    aliases = {t: 2 * n_g + t for t in range(n)}
    aliases.update({n + t: 2 * n_g + n + t for t in range(n)})
    res = pl.pallas_call(
        body, name=name, out_shape=out_shape, in_specs=[HBM_SPEC] * (2 * n) + [ANY_SPEC],
        out_specs=[SEM_SPEC] * (2 * n_g) + [HBM_SPEC] * (2 * n) + [pl.BlockSpec(memory_space=pltpu.VMEM)],
        input_output_aliases=aliases, compiler_params=pltpu.CompilerParams(has_side_effects=DATAFLOW),
    )(*[_in_hbm(s) for s in flat], *[_in_hbm(lax.empty((N_DEV,) + s.shape, s.dtype)) for s in flat], after)
    out, off = [], 0
    for gi, g in enumerate(groups):
        k = len(g)
        out.append((res[2 * gi], res[2 * gi + 1], res[2 * n_g + off:2 * n_g + off + k],
                    res[2 * n_g + n + off:2 * n_g + n + off + k]))
        off += k
    return out, res[-1]


def gather_wait(started, after, *, name):
    send_sems, recv_sems, srcs, lands = started
    n = len(srcs)
    after = list(after)

    def body(*refs):
        src, land = refs[:n], refs[n:2 * n]
        send, recv = refs[2 * n], refs[2 * n + 1]
        _, peers = _gather_peers()
        for t in range(n):
            for k, frm in enumerate(peers):
                cp = pltpu.make_async_remote_copy(
                    src_ref=src[t], dst_ref=land[t].at[_slot(frm)], send_sem=send.at[4 * t + k],
                    recv_sem=recv.at[4 * t + k],
                    device_id=frm, device_id_type=MESH)
                cp.wait_send()
                cp.wait_recv()

    res = pl.pallas_call(
        body, name=name,
        out_shape=[pltpu.HBM(s.shape, s.dtype) for s in srcs] + [pltpu.HBM(l.shape, l.dtype) for l in lands],
        in_specs=[HBM_SPEC] * (2 * n) + [SEM_SPEC, SEM_SPEC] + [ANY_SPEC] * len(after),
        out_specs=[HBM_SPEC] * (2 * n), input_output_aliases={t: t for t in range(2 * n)},
        compiler_params=pltpu.CompilerParams(has_side_effects=DATAFLOW),
    )(*srcs, *lands, send_sems, recv_sems, *after)
    return res[:n], res[n:]


def place_own(src, land, *, name):
    R, C = src.shape
    tr = _tile(R, 512, 16)
    x, y, c, _ = _chip_slots()
    idx = jnp.stack([4 * x + 2 * y + c]).astype(jnp.int32)

    def body(idx_ref, s_ref, land_ref, o_ref):
        o_ref[...] = s_ref[...]

    return _pcall(body, name=name, grid=(R // tr,),
                  ins=[(src, (tr, C), lambda i, s: (i, 0)), (land, None, None)],
                  outs=[(land.shape, land.dtype, (None, tr, C), lambda i, s: (s[0], i, 0))],
                  prefetch=idx, aliases={2: 0}, semantics=("parallel",))[0]


def gather_finish(srcs, lands, *, name):
    n = len(srcs)

    def body(*refs):
        land = refs[n:2 * n]
        send_sems, recv_sems = refs[2 * n:]
        x, y, c, chips = _chip_slots()
        me, sibling = (x, y, c), (x, y, 1 - c)

        def copy(t, j, block, to):
            return pltpu.make_async_remote_copy(
                src_ref=land[t].at[_slot(block)], dst_ref=land[t].at[_slot(block)], send_sem=send_sems.at[t, j],
                recv_sem=recv_sems.at[t, j], device_id=to, device_id_type=MESH)

        sends = [copy(t, j, (*chip, c), sibling) for t in range(n) for j, chip in enumerate(chips)]
        for cp in sends:
            cp.start()
        for t in range(n):
            for j, chip in enumerate(chips):
                copy(t, j, (*chip, 1 - c), me).wait_recv()
        for cp in sends:
            cp.wait_send()

    passed = pl.pallas_call(
        body, name=name, out_shape=[jax.ShapeDtypeStruct(l.shape, l.dtype) for l in lands],
        in_specs=[ANY_SPEC] * n, out_specs=[ANY_SPEC] * n,
        input_output_aliases={t: t for t in range(n)},
        scratch_shapes=[pltpu.SemaphoreType.DMA((n, 3)), pltpu.SemaphoreType.DMA((n, 3))],
    )(*lands)
    return [place_own(s, l, name=f"{name}_own{t}") for t, (s, l) in enumerate(zip(srcs, passed))]


def chips_start(pairs, *, name):
    n = len(pairs)

    def body(*refs):
        src, land = refs[:n], refs[n:2 * n]
        send, recv = refs[2 * n], refs[2 * n + 1]
        token = refs[-1]
        x, y, c, chips = _chip_slots()
        for t in range(n):
            for j, chip in enumerate(chips):
                pltpu.make_async_remote_copy(
                    src_ref=src[t].at[j], dst_ref=land[t].at[j], send_sem=send.at[3 * t + j],
                    recv_sem=recv.at[3 * t + j], device_id=(*chip, c), device_id_type=MESH).start()
        token[...] = jnp.zeros_like(token)

    res = pl.pallas_call(
        body, name=name,
        out_shape=[pltpu.SemaphoreType.DMA((3 * n,)), pltpu.SemaphoreType.DMA((3 * n,))]
        + [pltpu.HBM(p.shape, p.dtype) for p in pairs] * 2 + [jax.ShapeDtypeStruct((8, LANES), F32)],
        in_specs=[HBM_SPEC] * (2 * n),
        out_specs=[SEM_SPEC, SEM_SPEC] + [HBM_SPEC] * (2 * n) + [pl.BlockSpec(memory_space=pltpu.VMEM)],
        input_output_aliases={t: 2 + t for t in range(2 * n)},
        compiler_params=pltpu.CompilerParams(has_side_effects=DATAFLOW),
    )(*[_in_hbm(p) for p in pairs], *[_in_hbm(lax.empty(p.shape, p.dtype)) for p in pairs])
    return res[0], res[1], res[2:2 + n], res[2 + n:2 + 2 * n], res[-1]


def chips_wait(started, after, *, name):
    send_sems, recv_sems, srcs, lands, _ = started
    n = len(srcs)

    def body(*refs):
        src, land = refs[:n], refs[n:2 * n]
        send, recv = refs[2 * n], refs[2 * n + 1]
        x, y, c, chips = _chip_slots()
        for t in range(n):
            for j, chip in enumerate(chips):
                cp = pltpu.make_async_remote_copy(
                    src_ref=src[t].at[j], dst_ref=land[t].at[j], send_sem=send.at[3 * t + j],
                    recv_sem=recv.at[3 * t + j], device_id=(*chip, c), device_id_type=MESH)
                cp.wait_send()
                cp.wait_recv()

    res = pl.pallas_call(
        body, name=name, out_shape=[pltpu.HBM(s.shape, s.dtype) for s in srcs] * 2,
        in_specs=[HBM_SPEC] * (2 * n) + [SEM_SPEC, SEM_SPEC, ANY_SPEC], out_specs=[HBM_SPEC] * (2 * n),
        input_output_aliases={t: t for t in range(2 * n)},
        compiler_params=pltpu.CompilerParams(has_side_effects=DATAFLOW),
    )(*srcs, *lands, send_sems, recv_sems, after)
    return res[n:]


def _sibling_copies(src, land, send, recv, n):
    x, y, c, _ = _chip_slots()
    return [pltpu.make_async_remote_copy(
        src_ref=src[t].at[4 * (q // 2) + 2 * (q % 2) + (1 - c)], dst_ref=land[t].at[q], send_sem=send.at[4 * t + q],
        recv_sem=recv.at[4 * t + q], device_id=(x, y, 1 - c), device_id_type=MESH)
        for t in range(n) for q in range(4)]


def sibling_start(gs, *, name):
    n = len(gs)

    def body(*refs):
        for cp in _sibling_copies(refs[:n], refs[n:2 * n], refs[2 * n], refs[2 * n + 1], n):
            cp.start()
        refs[-1][...] = jnp.zeros_like(refs[-1])

    lands = [lax.empty((4,) + g.shape[1:], g.dtype) for g in gs]
    res = pl.pallas_call(
        body, name=name,
        out_shape=[pltpu.SemaphoreType.DMA((4 * n,)), pltpu.SemaphoreType.DMA((4 * n,))]
        + [pltpu.HBM(g.shape, g.dtype) for g in gs] + [pltpu.HBM(l.shape, l.dtype) for l in lands]
        + [jax.ShapeDtypeStruct((8, LANES), F32)],
        in_specs=[HBM_SPEC] * (2 * n),
        out_specs=[SEM_SPEC, SEM_SPEC] + [HBM_SPEC] * (2 * n) + [pl.BlockSpec(memory_space=pltpu.VMEM)],
        input_output_aliases={t: 2 + t for t in range(2 * n)},
        compiler_params=pltpu.CompilerParams(has_side_effects=DATAFLOW),
    )(*[_in_hbm(g) for g in gs], *[_in_hbm(l) for l in lands])
    return res[0], res[1], res[2:2 + n], res[2 + n:2 + 2 * n], res[-1]


def sibling_wait(started, after, *, name):
    send_sems, recv_sems, srcs, lands, _ = started
    n = len(srcs)

    def body(*refs):
        for cp in _sibling_copies(refs[:n], refs[n:2 * n], refs[2 * n], refs[2 * n + 1], n):
            cp.wait_send()
            cp.wait_recv()

    res = pl.pallas_call(
        body, name=name,
        out_shape=[pltpu.HBM(s.shape, s.dtype) for s in srcs] + [pltpu.HBM(l.shape, l.dtype) for l in lands],
        in_specs=[HBM_SPEC] * (2 * n) + [SEM_SPEC, SEM_SPEC, ANY_SPEC], out_specs=[HBM_SPEC] * (2 * n),
        input_output_aliases={t: t for t in range(2 * n)},
        compiler_params=pltpu.CompilerParams(has_side_effects=DATAFLOW),
    )(*srcs, *lands, send_sems, recv_sems, after)
    return res[:n], res[n:]


def all_gather_vmem(x_shard, *, name, after=None):
    m_per, n = x_shard.shape
    n_after = 0 if after is None else 1

    def body(x_ref, *rest):
        out_ref, send_sems, recv_sems, local_sem = rest[n_after:]
        x, y, c, chips = _chip_slots()
        me, sibling = (x, y, c), (x, y, 1 - c)

        def rows(px, py, pc):
            return out_ref.at[pl.ds((4 * px + 2 * py + pc) * m_per, m_per), :]

        def copy(k, block, to, src=None):
            return pltpu.make_async_remote_copy(
                src_ref=rows(*block) if src is None else src, dst_ref=rows(*block),
                send_sem=send_sems.at[k], recv_sem=recv_sems.at[k], device_id=to, device_id_type=MESH)

        mine = pltpu.make_async_copy(x_ref, rows(*me), local_sem)
        mine.start()
        first = [copy(0, me, sibling, src=x_ref)]
        first += [copy(1 + j, me, (*chip, c), src=x_ref) for j, chip in enumerate(chips)]
        for cp in first:
            cp.start()
        passed = [copy(4 + j, (*chip, c), sibling) for j, chip in enumerate(chips)]
        for j, chip in enumerate(chips):
            copy(1 + j, (*chip, c), me).wait_recv()
            passed[j].start()
        copy(0, sibling, me).wait_recv()
        for j, chip in enumerate(chips):
            copy(4 + j, (*chip, 1 - c), me).wait_recv()
        for cp in first + passed:
            cp.wait_send()
        mine.wait()

    vmem = pl.BlockSpec(memory_space=pltpu.VMEM)
    return pl.pallas_call(
        body, name=name, out_shape=jax.ShapeDtypeStruct((N_DEV * m_per, n), x_shard.dtype),
        in_specs=[vmem] + [ANY_SPEC] * n_after, out_specs=vmem,
        scratch_shapes=[pltpu.SemaphoreType.DMA((7,)), pltpu.SemaphoreType.DMA((7,)), pltpu.SemaphoreType.DMA],
        compiler_params=pltpu.CompilerParams(vmem_limit_bytes=int(min(
            VMEM_LIMIT_CAP, 2 * (N_DEV + 1) * m_per * n * x_shard.dtype.itemsize + 16 * 2 ** 20))),
    )(x_shard, *([] if after is None else [after]))


def _rope_slab(cols):
    z = jnp.zeros(cols.shape[:-1] + (HALF_ROPE,), cols.dtype)
    return jnp.concatenate([cols[..., :HALF_ROPE], z, cols[..., HALF_ROPE:], z], axis=-1)


def _rope_unslab(slab):
    return jnp.concatenate([slab[..., :HALF_ROPE], slab[..., 2 * HALF_ROPE:3 * HALF_ROPE]], axis=-1)


def _pack_w_in_t(wt_g):
    s, c, d = wt_g.shape
    w = wt_g.reshape(s * c, d)
    c2, c3 = Q_LORA + KV_LORA, Q_LORA + KV_LORA + QK_ROPE
    r = w[c2:c3]
    z = jnp.zeros((HALF_ROPE, d), w.dtype)
    return jnp.concatenate([w[:c2], w[c3:], r[:HALF_ROPE], z, r[HALF_ROPE:], z], axis=0)


def _unpack_w_in_t_grad(dwt):
    d = dwt.shape[1]
    c2 = Q_LORA + KV_LORA
    uv = 2 * SGU_OUT
    slab = dwt[c2 + uv:]
    g = jnp.concatenate([dwt[:c2], slab[:HALF_ROPE], slab[2 * HALF_ROPE:3 * HALF_ROPE], dwt[c2:c2 + uv]], axis=0)
    return g.reshape(N_DEV, g.shape[0] // N_DEV, d)


def _rope_tables(positions):
    inv_freq = ROPE_BASE ** (-jnp.arange(0, QK_ROPE, 2, dtype=F32) / QK_ROPE)
    ang = positions.astype(F32)[:, None] * inv_freq
    cos, sin = jnp.cos(ang), jnp.sin(ang)
    z = jnp.zeros_like(cos)
    return jnp.concatenate([cos, z, cos, z], axis=-1), jnp.concatenate([-sin, z, sin, z], axis=-1)


def _mlp_up(x, gain, w1, tag):
    hn = rms_fwd(x, gain, name=f"mlp{tag}_norm")

    def act_epi(acc):
        a = jnp.maximum(acc, 0.0)
        return a, a * a

    T = x.shape[0]
    F = w1.shape[0] * w1.shape[2]
    a, act = mm(hn, w1, name=f"mlp{tag}_up", outs=[((T, F), BF, None), ((T, F), BF, None)], epi=act_epi)
    return hn, a, act


def _mlp_down(x, act, w2, tag, part=0):
    n = w2.shape[1]
    bm = _tile(x.shape[0], MM_TILE)
    bn = _tile(n, MM_TILE)
    per = n // bn
    return mm(act, w2, name=f"mlp{tag}_down{part}", out=((x.shape[0], n), F32), bm=bm, bn=bn,
              epi=lambda acc, r: (acc + r[...],), epi_ins=[(x, (bm, bn), lambda i, j, k: (i, part * per + j))])


def _mlp_bwd_weights(w1, w2, saved, dxb, tag):
    hn, a, act = saved
    T, D = dxb.shape
    F = a.shape[1]
    bm = _tile(T, MM_TILE)
    bn = _tile(F, min(MM_TILE, w1.shape[2]))
    dhid = mm(dxb, w2, tb=True, name=f"mlp{tag}_dhid", out=((T, F), BF), bm=bm, bn=bn,
              epi=lambda acc, a_ref: (2.0 * a_ref[...].astype(F32) * acc,),
              epi_ins=[(a, (bm, bn), lambda i, j, k: (i, j))])
    dw2 = mm(act, dxb, ta=True, name=f"mlp{tag}_dw2", out=((F, D), BF))
    dw1 = mm(hn, dhid, ta=True, name=f"mlp{tag}_dw1", out=(w1.shape, BF))
    return dhid, dw1, dw2.reshape(N_DEV, F // N_DEV, D)


def _reduce_begin(grads, tag):
    return sibling_start(grads, name=f"reduce_sibling_start_{tag}")


def _reduce_continue(sib, after, tag, index):
    grads, a_bufs = sibling_wait(sib, after, name=f"reduce_sibling_wait_{tag}")
    pairs = [run_job(pair_job(g, a), index=index, name=f"pair_sum_{tag}{t}")[0]
             for t, (g, a) in enumerate(zip(grads, a_bufs))]
    return grads, a_bufs, chips_start(pairs, name=f"reduce_chips_start_{tag}")


def kernel(x, positions, e_norm_mix, e_w_in, e_q_norm, e_w_uq, e_kv_norm, e_w_ukv, e_v_norm, e_sgu_w, e_sgu_b, e_mla_out_norm, e_sgu_out_norm, e_w_out, o_norm_mix, o_w_in, o_conv_w, o_w_out, mlp_norm, mlp_w1, mlp_w2, final_norm, loss_target, m_e_norm_mix, m_e_w_in, m_e_q_norm, m_e_w_uq, m_e_kv_norm, m_e_w_ukv, m_e_v_norm, m_e_sgu_w, m_e_sgu_b, m_e_mla_out_norm, m_e_sgu_out_norm, m_e_w_out, m_o_norm_mix, m_o_w_in, m_o_conv_w, m_o_w_out, m_mlp_norm, m_mlp_w1, m_mlp_w2, m_final_norm, v_e_norm_mix, v_e_w_in, v_e_q_norm, v_e_w_uq, v_e_kv_norm, v_e_w_ukv, v_e_v_norm, v_e_sgu_w, v_e_sgu_b, v_e_mla_out_norm, v_e_sgu_out_norm, v_e_w_out, v_o_norm_mix, v_o_w_in, v_o_conv_w, v_o_w_out, v_mlp_norm, v_mlp_w1, v_mlp_w2, v_final_norm):
    T, D = x.shape[1], x.shape[2]
    d_shard = o_norm_mix.shape[1]
    x0 = x[0]
    target = loss_target[0]
    me = 4 * lax.axis_index("x") + 2 * lax.axis_index("y") + lax.axis_index("c")

    bf = lambda s: s.astype(BF)
    gather_groups = [[bf(jnp.transpose(e_w_in[0])), bf(e_w_uq[0]), bf(e_w_ukv[0])], [bf(e_w_out[0]), bf(mlp_w1[0])],
                     [bf(mlp_w2[0]), bf(o_w_in[0])], [bf(o_w_out[0]), bf(mlp_w1[1])], [bf(mlp_w2[1])]]
    small_rows = jnp.concatenate([o_norm_mix, o_conv_w[0], jnp.zeros((4, d_shard), F32)], axis=0)
    small_flat = all_gather_vmem(small_rows, name="gather_small")
    started, start_token = gather_start(gather_groups[:1], small_flat, name="gather_start0")
    started_rest, rest_token = gather_start(gather_groups[1:], start_token, name="gather_start1")
    started += started_rest

    def gathered(gi, after):
        srcs, lands = gather_wait(started[gi], after, name=f"gather_wait{gi}")
        return gather_finish(srcs, lands, name=f"gather_finish{gi}")

    small_g = small_flat.reshape(N_DEV, 8, d_shard)
    o_norm_full = small_g[:, 0, :].reshape(1, D)
    conv_w_full = jnp.transpose(small_g[:, 1:4, :], (1, 0, 2)).reshape(3, D)
    w_tril = jnp.tril(e_sgu_w[0])
    w_tril_b = w_tril.astype(BF)
    w_tril_tb = jnp.swapaxes(w_tril, 1, 2).astype(BF)
    b_full = jnp.repeat(e_sgu_b[0].T, CH, axis=1)
    v_gain = e_v_norm[0].reshape(1, SGU_OUT)
    cos_t, sin_t = _rope_tables(positions[0])
    mlp_gain = [mlp_norm[0:1], mlp_norm[1:2]]
    final_gain = final_norm.reshape(1, D)

    h0 = rms_fwd(x0, e_norm_mix, name="e_norm", deps=[rest_token])
    g_w_in_t, g_w_uq, w_ukv = gathered(
        0, [h0, cos_t, sin_t, w_tril_b, w_tril_tb, b_full, o_norm_full, conv_w_full])
    w_in_t = _pack_w_in_t(g_w_in_t)
    w_uq = jnp.concatenate([g_w_uq[..., :QK_NOPE], _rope_slab(g_w_uq[..., QK_NOPE:])], axis=-1)
    proj = mm(h0, w_in_t, tb=True, name="e_in", out=((T, w_in_t.shape[0]), F32), bn=_tile(w_in_t.shape[0], 640))
    qn, kvn, krope = mla_prep(proj, e_q_norm, e_kv_norm, cos_t, sin_t, name="mla_prep")
    bm = _tile(T, MM_TILE)

    def q_epi(acc, cos_ref, sin_ref):
        return (jnp.concatenate([acc[:, :QK_NOPE], _rope_fwd(acc[:, QK_NOPE:], cos_ref[...], sin_ref[...])], axis=-1),)

    q = mm(qn, w_uq, name="mla_q", out=((T, HEADS * HEAD_PAD), BF), bm=bm, bn=HEAD_PAD, epi=q_epi,
           epi_ins=[(cos_t, (bm, LANES), lambda i, j, k: (i, 0)), (sin_t, (bm, LANES), lambda i, j, k: (i, 0))])

    def kv_epi(acc, kr_ref):
        return jnp.concatenate([acc[:, :QK_NOPE].astype(BF), kr_ref[...]], axis=-1), acc[:, QK_NOPE:]

    k, v = mm(kvn, w_ukv, name="mla_kv", bm=bm, bn=HEAD_PAD, epi=kv_epi,
              outs=[((T, HEADS * HEAD_PAD), BF, HEAD_PAD), ((T, MLA_OUT), BF, V_HEAD)],
              epi_ins=[(krope, (bm, LANES), lambda i, j, k: (i, 0))])
    attn, attn_lse = attn_fwd(q, k, v, name="attn_fwd")
    mixed = mix_fwd(attn, proj, e_mla_out_norm, e_sgu_out_norm, v_gain, w_tril_b, b_full, name="mix_fwd")
    bn = _tile(D, MM_TILE)
    g_w_out_e, w1_0 = gathered(1, [mixed])
    w_out_e = g_w_out_e.reshape(-1, D)
    x1 = mm(mixed, w_out_e, name="e_out", out=((T, D), F32), bm=bm, bn=bn,
            epi=lambda acc, r: (acc + r[...],), epi_ins=[(x0, (bm, bn), lambda i, j, k: (i, j))])
    hn0, a0, act0 = _mlp_up(x1, mlp_gain[0], w1_0, 0)
    g_w2_0, g_w_in_o = gathered(2, [act0])
    w2_0 = g_w2_0.reshape(-1, D)
    x2 = _mlp_down(x1, act0, w2_0, 0)
    ho = rms_fwd(x2, o_norm_full, name="o_norm")
    proj_o = mm(ho, g_w_in_o, name="o_in", out=((T, 3 * D), F32))
    gated = conv_fwd(proj_o, conv_w_full, name="conv_fwd")
    g_w_out_o, w1_1 = gathered(3, [gated])
    w_out_o = g_w_out_o.reshape(-1, D)
    x3 = mm(gated, w_out_o, name="o_out", out=((T, D), F32), bm=bm, bn=bn,
            epi=lambda acc, r: (acc + r[...],), epi_ins=[(x2, (bm, bn), lambda i, j, k: (i, j))])
    hn1, a1, act1 = _mlp_up(x3, mlp_gain[1], w1_1, 1)
    (g_w2_1,) = gathered(4, [act1])
    w2_1 = g_w2_1.reshape(-1, D)
    x4 = _mlp_down(x3, act1, w2_1, 1)
    w1, w2 = [w1_0, w1_1], [w2_0, w2_1]

    dx4, dx4b, d_final, loss_part = loss_bwd([x4], final_gain, target, name="loss_bwd")

    hosted = dict(job_index=device_index())
    dhid1, dw1_1, dw2_1 = _mlp_bwd_weights(w1[1], w2[1], (hn1, a1, act1), dx4b, 1)
    sib_r0 = _reduce_begin([dw1_1, dw2_1], "r0")
    dhn1 = mm(dhid1, w1[1], tb=True, name="mlp1_dhn", out=((T, D), F32), deps=[sib_r0[-1]])
    grads_r0, a_r0 = sibling_wait(sib_r0, dhn1, name="reduce_sibling_wait_r0")
    dx3, dx3b, d_mlp1 = rms_bwd(x3, mlp_gain[1], dhn1, dres=dx4, name="mlp1_norm_bwd")

    dgated, ((pair_r0a,),) = mm(dx3b, w_out_o, tb=True, name="o_out_dx", out=((T, D), F32),
                                jobs=[pair_job(grads_r0[0], a_r0[0])], **hosted)
    dw_out_o, ((pair_r0b,),) = mm(gated, dx3b, ta=True, name="o_out_dw", out=((D, D), BF),
                                  jobs=[pair_job(grads_r0[1], a_r0[1])], **hosted)
    st_r0 = chips_start([pair_r0a, pair_r0b], name="reduce_chips_start_r0")
    dproj_o, dconv_full = conv_bwd(dgated, proj_o, conv_w_full, name="conv_bwd", deps=[st_r0[-1]])
    dw_in_o = mm(ho, dproj_o, ta=True, name="o_in_dw", out=(g_w_in_o.shape, BF))
    sib_r1 = _reduce_begin([dw_out_o.reshape(g_w_out_o.shape), dw_in_o], "r1")
    dho = mm(dproj_o, g_w_in_o, tb=True, name="o_in_dx", out=((T, D), F32), deps=[sib_r1[-1]])
    grads_r1, a_r1 = sibling_wait(sib_r1, dho, name="reduce_sibling_wait_r1")
    dx2, dx2b, d_onorm_full = rms_bwd(x2, o_norm_full, dho, dres=dx3, name="o_norm_bwd")

    d_ff = a0.shape[1]
    bm_h, bn_h = _tile(T, MM_TILE), _tile(d_ff, min(MM_TILE, w1[0].shape[2]))
    dhid0, ((pair_r1a,), (pair_r1b,)) = mm(
        dx2b, w2[0], tb=True, name="mlp0_dhid", out=((T, d_ff), BF), bm=bm_h, bn=bn_h,
        epi=lambda acc, a_ref: (2.0 * a_ref[...].astype(F32) * acc,),
        epi_ins=[(a0, (bm_h, bn_h), lambda i, j, k: (i, j))],
        jobs=[pair_job(grads_r1[0], a_r1[0]), pair_job(grads_r1[1], a_r1[1])], **hosted)
    st_r1 = chips_start([pair_r1a, pair_r1b], name="reduce_chips_start_r1")
    dw2_0 = mm(act0, dx2b, ta=True, name="mlp0_dw2", out=((d_ff, D), BF), deps=[st_r1[-1]])
    b_r0 = chips_wait(st_r0, dw2_0, name="reduce_chips_wait_r0")
    dw1_0, (r_w1, r_w2) = mm(
        hn0, dhid0, ta=True, name="mlp0_dw1", out=(w1[0].shape, BF),
        jobs=[adam_job(grads_r0[0], a_r0[0], b_r0[0], mlp_w1, m_mlp_w1, v_mlp_w1, 1, None),
              adam_job(grads_r0[1], a_r0[1], b_r0[1], mlp_w2, m_mlp_w2, v_mlp_w2, 1, None)], **hosted)
    sib_r2 = _reduce_begin([dw1_0, dw2_0.reshape(N_DEV, d_ff // N_DEV, D)], "r2")
    dhn0 = mm(dhid0, w1[0], tb=True, name="mlp0_dhn", out=((T, D), F32), deps=[sib_r2[-1]])
    grads_r2, a_r2 = sibling_wait(sib_r2, dhn0, name="reduce_sibling_wait_r2")
    dx1, dx1b, d_mlp0 = rms_bwd(x1, mlp_gain[0], dhn0, dres=dx2, name="mlp0_norm_bwd")

    dmixed, ((pair_r2a,),) = mm(dx1b, w_out_e, tb=True, name="e_out_dx", out=((T, MLA_OUT + SGU_OUT), F32),
                                jobs=[pair_job(grads_r2[0], a_r2[0])], **hosted)
    dw_out_e, ((pair_r2b,),) = mm(mixed, dx1b, ta=True, name="e_out_dw", out=(w_out_e.shape, BF),
                                  jobs=[pair_job(grads_r2[1], a_r2[1])], **hosted)
    st_r2 = chips_start([pair_r2a, pair_r2b], name="reduce_chips_start_r2")
    (dattn, dproj, d_mla_out, d_sgu_out, d_vgain, d_sgu_w, d_b_full) = mix_bwd(
        dmixed, attn, proj, e_mla_out_norm, e_sgu_out_norm, v_gain, w_tril_b, w_tril_tb, b_full, name="mix_bwd",
        deps=[st_r2[-1]])
    b_r1 = chips_wait(st_r1, dattn, name="reduce_chips_wait_r1")
    dq, dk, dv = attn_bwd(q, k, v, attn, attn_lse, dattn, name="attn_bwd")
    dq_lin, dkv_lin, dproj = mla_bwd_prep(dq, dk, dv, cos_t, sin_t, dproj, name="mla_bwd_prep")
    dw_uq_pad = mm(qn, dq_lin, ta=True, name="mla_q_dw", out=(w_uq.shape, BF))
    dw_ukv = mm(kvn, dkv_lin, ta=True, name="mla_kv_dw", out=(w_ukv.shape, BF))
    dw_uq = jnp.concatenate([dw_uq_pad[..., :QK_NOPE], _rope_unslab(dw_uq_pad[..., QK_NOPE:])], axis=-1)
    sib_r2b = _reduce_begin([dw_out_e.reshape(g_w_out_e.shape), dw_uq, dw_ukv], "r2b")
    dqn = mm(dq_lin, w_uq, tb=True, name="mla_q_dx", out=((T, Q_LORA), F32), deps=[sib_r2b[-1]])
    dkvn = mm(dkv_lin, w_ukv, tb=True, name="mla_kv_dx", out=((T, KV_LORA), F32), deps=[sib_r2b[-1]])
    grads_r2b, a_r2b, st_r2b = _reduce_continue(sib_r2b, dkvn, "r2b", hosted["job_index"])
    dproj, d_qnorm = rms_bwd(proj, e_q_norm, dqn, col_block=0, want_f32=False, into=dproj, name="q_norm_bwd",
                             deps=[st_r2b[-1]])
    dproj, d_kvnorm = rms_bwd(proj, e_kv_norm, dkvn, col_block=1, want_f32=False, into=dproj, name="kv_norm_bwd")
    dw_in_t_pad, (r_w_out_o, r_w_in_o) = mm(
        dproj, h0, ta=True, name="e_in_dw", out=(w_in_t.shape, BF), bm=_tile(w_in_t.shape[0], 640),
        jobs=[adam_job(grads_r1[0], a_r1[0], b_r1[0], o_w_out, m_o_w_out, v_o_w_out, 0, None),
              adam_job(grads_r1[1], a_r1[1], b_r1[1], o_w_in, m_o_w_in, v_o_w_in, 0, None)], **hosted)
    dw_in_t = _unpack_w_in_t_grad(dw_in_t_pad)
    sib_r3 = _reduce_begin([dw_in_t], "r3")
    dh0 = mm(dproj, w_in_t, name="e_in_dx", out=((T, D), F32), deps=[sib_r3[-1]])
    grads_r3, a_r3, st_r3 = _reduce_continue(sib_r3, dh0, "r3", hosted["job_index"])
    tok_r3 = st_r3[-1]
    grad_x, d_enorm = rms_bwd(x0, e_norm_mix, dh0, dres=dx1, want_bf=False, name="e_norm_bwd", deps=[tok_r3])
    b_r2 = chips_wait(st_r2, grad_x, name="reduce_chips_wait_r2")

    def finish(grads, a_bufs, b_bufs, t, w, m, v, layer=0, prev=None, tag="", deps=()):
        return run_job(adam_job(grads[t], a_bufs[t], b_bufs[t], w, m, v, layer, prev), index=hosted["job_index"],
                       name=f"adam_{tag}", deps=deps)

    r_w1 = finish(grads_r2, a_r2, b_r2, 0, mlp_w1, m_mlp_w1, v_mlp_w1, 0, r_w1, tag="w1_l0", deps=[tok_r3])
    r_w2 = finish(grads_r2, a_r2, b_r2, 1, mlp_w2, m_mlp_w2, v_mlp_w2, 0, r_w2, tag="w2_l0", deps=[r_w1[1]])
    b_r2b = chips_wait(st_r2b, r_w2[1], name="reduce_chips_wait_r2b")
    r_w_out_e = finish(grads_r2b, a_r2b, b_r2b, 0, e_w_out, m_e_w_out, v_e_w_out, tag="e_w_out")
    r_w_uq = finish(grads_r2b, a_r2b, b_r2b, 1, e_w_uq, m_e_w_uq, v_e_w_uq, tag="e_w_uq")
    r_w_ukv = finish(grads_r2b, a_r2b, b_r2b, 2, e_w_ukv, m_e_w_ukv, v_e_w_ukv, tag="e_w_ukv")
    b_r3 = chips_wait(st_r3, r_w_out_e[1], name="reduce_chips_wait_r3")
    g_w_in_t = reduce_sum(grads_r3[0], a_r3[0], b_r3[0], name="sum_e_w_in")
    w_in_upd_t = adam_rows(g_w_in_t, jnp.transpose(e_w_in[0]), jnp.transpose(m_e_w_in[0]), jnp.transpose(v_e_w_in[0]),
                           name="adam_e_w_in")
    r_w_in = [jnp.transpose(t)[None] for t in (g_w_in_t, *w_in_upd_t)]

    d_sgu_b = jnp.transpose(d_b_full[:, ::CH])
    d_sgu_w_tril = jnp.tril(d_sgu_w)
    rep = [("e_norm_mix", e_norm_mix, m_e_norm_mix, v_e_norm_mix, d_enorm),
           ("e_q_norm", e_q_norm, m_e_q_norm, v_e_q_norm, d_qnorm),
           ("e_kv_norm", e_kv_norm, m_e_kv_norm, v_e_kv_norm, d_kvnorm),
           ("e_v_norm", e_v_norm, m_e_v_norm, v_e_v_norm, d_vgain),
           ("e_sgu_w", e_sgu_w, m_e_sgu_w, v_e_sgu_w, d_sgu_w_tril),
           ("e_sgu_b", e_sgu_b, m_e_sgu_b, v_e_sgu_b, d_sgu_b),
           ("e_mla_out_norm", e_mla_out_norm, m_e_mla_out_norm, v_e_mla_out_norm, d_mla_out),
           ("e_sgu_out_norm", e_sgu_out_norm, m_e_sgu_out_norm, v_e_sgu_out_norm, d_sgu_out),
           ("mlp_norm", mlp_norm, m_mlp_norm, v_mlp_norm, jnp.concatenate([d_mlp0, d_mlp1], axis=0)),
           ("final_norm", final_norm, m_final_norm, v_final_norm, d_final)]
    sizes = [int(np.prod(r[1].shape)) for r in rep]
    n_rep = sum(sizes)
    n_all = n_rep + 4 * D + 1
    width = -(-n_all // (8 * LANES)) * LANES
    pad = 8 * width - n_all
    flat = jnp.concatenate([r[4].reshape(-1) for r in rep]
                           + [d_onorm_full.reshape(-1), dconv_full.reshape(-1), loss_part[0, :1],
                              jnp.zeros((pad,), F32)])
    summed = sum_rows8(all_gather_vmem(flat.reshape(8, width), name="gather_small_grads", after=b_r3[0]), 8,
                       name="sum_small_grads").reshape(-1)

    loss = summed[n_rep + 4 * D]

    def pack_rep(i):
        return jnp.concatenate([r[i].reshape(-1) for r in rep]).reshape(n_rep // LANES, LANES)

    g_rep = summed[:n_rep].reshape(n_rep // LANES, LANES)
    d_rep, nm_rep, nv_rep = adam_flat(g_rep, pack_rep(1), pack_rep(2), pack_rep(3), name="adam_replicated")

    def unpack_rep(flat2d):
        out, off = {}, 0
        f = flat2d.reshape(-1)
        for r, n in zip(rep, sizes):
            out[r[0]] = f[off:off + n].reshape(r[1].shape)
            off += n
        return out

    small = {"grad": unpack_rep(g_rep), "delta": unpack_rep(d_rep), "new_m": unpack_rep(nm_rep),
             "new_v": unpack_rep(nv_rep)}
    g_onorm = lax.dynamic_slice(summed[n_rep:n_rep + D].reshape(1, D), (0, me * d_shard), (1, d_shard))
    g_conv = lax.dynamic_slice(summed[n_rep + D:n_rep + 4 * D].reshape(3, D), (0, me * d_shard), (3, d_shard))

    def pack_sharded(norm_part, conv_part):
        return jnp.concatenate([norm_part, conv_part, jnp.zeros((4, d_shard), F32)], axis=0)

    g_sh = pack_sharded(g_onorm, g_conv)
    d_sh, nm_sh, nv_sh = adam_flat(g_sh, pack_sharded(o_norm_mix, o_conv_w[0]), pack_sharded(m_o_norm_mix, m_o_conv_w[0]),
                                   pack_sharded(v_o_norm_mix, v_o_conv_w[0]), name="adam_sharded_small")
    for kind, arr in (("grad", g_sh), ("delta", d_sh), ("new_m", nm_sh), ("new_v", nv_sh)):
        small[kind]["o_norm_mix"] = arr[0:1]
        small[kind]["o_conv_w"] = arr[1:4][None]

    big = {"e_w_in": r_w_in, "e_w_uq": r_w_uq, "e_w_ukv": r_w_ukv, "e_w_out": r_w_out_e, "o_w_in": r_w_in_o,
           "o_w_out": r_w_out_o, "mlp_w1": r_w1, "mlp_w2": r_w2}
    order = ["e_norm_mix", "e_w_in", "e_q_norm", "e_w_uq", "e_kv_norm", "e_w_ukv", "e_v_norm", "e_sgu_w", "e_sgu_b",
             "e_mla_out_norm", "e_sgu_out_norm", "e_w_out", "o_norm_mix", "o_w_in", "o_conv_w", "o_w_out", "mlp_norm",
             "mlp_w1", "mlp_w2", "final_norm"]
    result = [loss, grad_x[None]]
    for ki, kind in enumerate(("grad", "delta", "new_m", "new_v")):
        for nm in order:
            result.append(big[nm][ki] if nm in big else small[kind][nm])
    return tuple(result)
```

```python
import numpy as np
import jax
import jax.numpy as jnp
from jax import lax
from jax.experimental import pallas as pl
from jax.experimental.pallas import tpu as pltpu

BF = jnp.bfloat16
F32 = jnp.float32
MESH = pl.DeviceIdType.MESH
N_DEV = 8

EPS = 1e-6
HEADS = 8
Q_LORA = 512
KV_LORA = 512
QK_NOPE = 128
QK_ROPE = 64
HALF_ROPE = QK_ROPE // 2
V_HEAD = 128
HEAD_PAD = 256
ROPE_BASE = 10000.0
GROUPS = 8
CH = 128
CHUNK = 128
SGU_OUT = GROUPS * CH
MLA_OUT = HEADS * V_HEAD
ATTN_SCALE = float((QK_NOPE + QK_ROPE) ** -0.5)

ADAM_LR = 0.001
ADAM_B1 = 0.9
ADAM_B2 = 0.999
ADAM_EPS = 1e-08
ADAM_WD = 0.01
ADAM_STEP = 10
ADAM_C1 = 1.0 - ADAM_B1 ** ADAM_STEP
ADAM_C2 = 1.0 - ADAM_B2 ** ADAM_STEP

V7X_VMEM_BYTES = 64 * 2 ** 20
VMEM_LIMIT_CAP = V7X_VMEM_BYTES - 6 * 2 ** 20
LANES = 128
ROW_TILE = 256
ATTN_TILE = 512
STREAM_BLOCK_ELEMS = 512 * 1024
MM_TILE = 1024
MM_K_TILE = 2048
MM_K_BLOCK_MAX = 3072


def _padded_bytes(block, dtype):
    dims = [d for d in block if d is not None]
    if len(dims) >= 1:
        dims[-1] = -(-dims[-1] // LANES) * LANES
    if len(dims) >= 2:
        dims[-2] = -(-dims[-2] // 16) * 16
    return int(np.prod(dims)) * jnp.dtype(dtype).itemsize


def _pcall(body, *, name, grid, ins, outs, scratch=(), semantics=None, aliases=None, prefetch=None, deps=()):
    any_spec = pl.BlockSpec(memory_space=pl.ANY)
    if deps:
        n_lead = len(ins) + (1 if prefetch is not None else 0)
        n_deps = len(deps)
        inner = body

        def body(*refs):
            inner(*refs[:n_lead], *refs[n_lead + n_deps:])

        ins = list(ins) + [(d, None, None) for d in deps]
    in_specs = [any_spec if b is None else pl.BlockSpec(b, m) for _, b, m in ins]
    out_specs = [any_spec if b is None else pl.BlockSpec(b, m) for _, _, b, m in outs]
    out_shape = [pltpu.HBM(s, d) for s, d, _, _ in outs]
    est = 0
    for a, b, _ in ins:
        if b is not None:
            est += 2 * _padded_bytes(b, a.dtype)
    for _, d, b, _ in outs:
        if b is not None:
            est += 2 * _padded_bytes(b, d)
    for s in scratch:
        if hasattr(s, "shape") and hasattr(s, "dtype"):
            est += _padded_bytes(s.shape, s.dtype)
    limit = int(min(VMEM_LIMIT_CAP, est + 16 * 2 ** 20))
    params = pltpu.CompilerParams(
        dimension_semantics=semantics or ("arbitrary",) * len(grid), vmem_limit_bytes=limit)
    args = [pltpu.with_memory_space_constraint(a, pltpu.HBM) for a, _, _ in ins]
    if prefetch is not None:
        grid_spec = pltpu.PrefetchScalarGridSpec(
            num_scalar_prefetch=1, grid=grid, in_specs=in_specs, out_specs=out_specs, scratch_shapes=list(scratch))
        call = pl.pallas_call(body, out_shape=out_shape, grid_spec=grid_spec, name=name, compiler_params=params,
                              input_output_aliases=aliases or {})
        return call(prefetch, *args)
    call = pl.pallas_call(body, out_shape=out_shape, grid=grid, in_specs=in_specs, out_specs=out_specs,
                          scratch_shapes=list(scratch), name=name, compiler_params=params,
                          input_output_aliases=aliases or {})
    return call(*args)


def _tile(dim, pref, quantum=LANES):
    if dim <= pref:
        return dim
    t = (pref // quantum) * quantum
    while t >= quantum:
        if dim % t == 0:
            return t
        t -= quantum
    return dim


def _vshape(arr_shape):
    if len(arr_shape) == 2:
        return tuple(arr_shape)
    s, r, c = arr_shape
    return (r, s * c)


def _vblock(arr_shape, br, bc, rc):
    if len(arr_shape) == 2:
        return (br, bc), (lambda *g: rc(*g))
    _, _, c = arr_shape
    assert c % bc == 0, (arr_shape, bc)
    per = c // bc

    def imap(*g):
        ri, ci = rc(*g)
        return (ci // per, ri, ci % per)

    return (None, br, bc), imap


def _shard_width(*shapes):
    w = None
    for s in shapes:
        if len(s) == 3:
            w = s[2] if w is None else int(np.gcd(w, s[2]))
    return w


def mm(a, b, *, name, ta=False, tb=False, out=None, outs=None, epi=None, epi_ins=(), bm=None, bn=None, bk=None,
       deps=(), jobs=(), job_index=None):
    av, bv = _vshape(a.shape), _vshape(b.shape)
    M, K = (av[1], av[0]) if ta else av
    K2, N = (bv[1], bv[0]) if tb else bv
    assert K == K2, (a.shape, b.shape, ta, tb)
    if outs is None:
        outs = [(out[0], out[1], None)]
    a_sw = _shard_width(a.shape)
    b_sw = _shard_width(b.shape)
    o_sw = _shard_width(*[o[0] for o in outs])
    m_lim = a_sw if (ta and a_sw) else None
    k_lim = [w for w in ((a_sw if not ta else None), (b_sw if tb else None)) if w]
    n_lim = [w for w in ((b_sw if not tb else None), o_sw) if w]
    if bm is None:
        bm = _tile(M, min([MM_TILE] + ([m_lim] if m_lim else [])))
    if bn is None:
        bn = _tile(N, min([MM_TILE] + n_lim))
    k_shards = 0
    if tb and len(b.shape) == 3 and bk is None and not (a_sw and not ta):
        k_shards = 1
        while 2 * k_shards <= b.shape[0] and 2 * k_shards * b_sw <= MM_K_BLOCK_MAX:
            k_shards *= 2
        bk = k_shards * b_sw
    if bk is None:
        bk = K if (K <= 4096 and not k_lim) else _tile(K, min([MM_K_TILE] + k_lim))
    assert M % bm == 0 and N % bn == 0 and K % bk == 0, (name, M, N, K, bm, bn, bk)
    nk = K // bk
    grid = (M // bm, N // bn, nk)
    if ta:
        a_blk, a_map = _vblock(a.shape, bk, bm, lambda i, j, k: (k, i))
    else:
        a_blk, a_map = _vblock(a.shape, bm, bk, lambda i, j, k: (i, k))
    if k_shards:
        b_blk, b_map = (k_shards, bn, b_sw), (lambda i, j, k: (k, j, 0))
    elif tb:
        b_blk, b_map = _vblock(b.shape, bn, bk, lambda i, j, k: (j, k))
    else:
        b_blk, b_map = _vblock(b.shape, bk, bn, lambda i, j, k: (k, j))
    dn = (((0 if ta else 1,), (1 if tb else 0,)), ((), ()))
    ins = [(a, a_blk, a_map), (b, b_blk, b_map)] + list(epi_ins)
    out_list = []
    for shape, dtype, cols in outs:
        cols = cols or bn
        blk, imap = _vblock(shape, bm, cols, lambda i, j, k: (i, j))
        out_list.append((shape, dtype, blk, imap))
    n_e, n_o = len(epi_ins), len(out_list)

    n_steps = grid[0] * grid[1] * nk
    built = [job(n_steps) for job in jobs]
    aliases = {}
    job_slices = []
    if built:
        def lin(i, j, k):
            return (i * grid[1] + j) * nk + k

        ins = [(arr, blk, None if blk is None else (lambda i, j, k, s, f=f: f(i, j, k))) for arr, blk, f in ins]
        out_list = [(sh, dt, blk, (lambda i, j, k, s, f=f: f(i, j, k))) for sh, dt, blk, f in out_list]
        n_main_in, n_main_out = len(ins), len(out_list)
        for jb in built:
            i0, o0 = len(ins), len(out_list)
            ins += [(arr, blk, None if blk is None else (lambda i, j, k, s, f=f: f(lin(i, j, k), s)))
                    for arr, blk, f in jb["ins"]]
            out_list += [(sh, dt, blk, (lambda i, j, k, s, f=f: f(lin(i, j, k), s))) for sh, dt, blk, f in jb["outs"]]
            aliases.update({1 + i0 + ai: o0 + ao for ai, ao in jb["aliases"].items()})
            job_slices.append((i0, len(jb["ins"]), o0, len(jb["outs"])))
    n_in_total = len(ins)

    def body(*refs):
        if built:
            refs = refs[1:]
        a_ref, b_ref = refs[0], refs[1]
        e_refs = refs[2:2 + n_e]
        o_refs = refs[n_in_total:n_in_total + n_o]
        for jb, (i0, ni, o0, no) in zip(built, job_slices):
            jb["fn"](refs[i0:i0 + ni], refs[n_in_total + o0:n_in_total + o0 + no])

        def finish(acc):
            res = epi(acc, *e_refs) if epi is not None else (acc,)
            for o_ref, r in zip(o_refs, res):
                o_ref[...] = r.astype(o_ref.dtype)

        x = a_ref[...].astype(BF)
        y = b_ref[...].astype(BF)
        if k_shards:
            p = None
            for s in range(k_shards):
                part = lax.dot_general(x[:, s * b_sw:(s + 1) * b_sw], y[s], dn, preferred_element_type=F32)
                p = part if p is None else p + part
        else:
            p = lax.dot_general(x, y, dn, preferred_element_type=F32)
        if nk == 1:
            finish(p)
        else:
            acc_ref = refs[-1]
            k = pl.program_id(2)

            @pl.when(k == 0)
            def _():
                acc_ref[...] = p

            @pl.when(k > 0)
            def _():
                acc_ref[...] += p

            @pl.when(k == nk - 1)
            def _():
                finish(acc_ref[...])

    scratch = [pltpu.VMEM((bm, bn), F32)] if nk > 1 else []
    res = _pcall(body, name=name, grid=grid, ins=ins, outs=out_list, scratch=scratch, deps=deps,
                 semantics=("parallel", "parallel", "arbitrary"), prefetch=job_index if built else None, aliases=aliases)
    main = res[0] if n_o == 1 else res[:n_o]
    if not built:
        return main
    return main, [res[o0:o0 + no] for _, _, o0, no in job_slices]


_GELU_K = float(np.sqrt(2.0 / np.pi))
_GELU_C = 0.044715


def _gelu(x):
    t = jnp.tanh(_GELU_K * (x + _GELU_C * (x * x * x)))
    return 0.5 * x * (1.0 + t)


def _gelu_grad(x):
    t = jnp.tanh(_GELU_K * (x + _GELU_C * (x * x * x)))
    return 0.5 * (1.0 + t) + 0.5 * x * (1.0 - t * t) * (_GELU_K * (1.0 + 3.0 * _GELU_C * (x * x)))


def _rstd(x):
    return lax.rsqrt(jnp.mean(x * x, axis=-1, keepdims=True) + EPS)


def _rms_bwd(x, gain, dy):
    r = _rstd(x)
    xh = x * r
    gdy = dy * gain
    dx = r * (gdy - xh * jnp.mean(gdy * xh, axis=-1, keepdims=True))
    return dx, dy * xh


def _rope_fwd(x, cos_t, sin_t):
    return x * cos_t + pltpu.roll(x, 2 * HALF_ROPE, 1) * sin_t


def _rope_bwd(dy, cos_t, sin_t):
    return dy * cos_t + pltpu.roll(dy * sin_t, 2 * HALF_ROPE, 1)


def _acc_rows(ref, val, first):
    s = jnp.sum(val, axis=0, keepdims=True)

    @pl.when(first)
    def _():
        ref[...] = s

    @pl.when(jnp.logical_not(first))
    def _():
        ref[...] += s


def rms_fwd(x, gain, *, name, col_block=0, width=None, deps=()):
    T = x.shape[0]
    width = width or x.shape[1]
    tm = _tile(T, ROW_TILE, 8)

    def body(x_ref, g_ref, o_ref):
        v = x_ref[...]
        o_ref[...] = (v * _rstd(v) * g_ref[...]).astype(BF)

    return _pcall(body, name=name, grid=(T // tm,),
                  ins=[(x, (tm, width), lambda i: (i, col_block)), (gain, (1, width), lambda i: (0, 0))],
                  outs=[((T, width), BF, (tm, width), lambda i: (i, 0))], semantics=("parallel",), deps=deps)[0]


def rms_bwd(x, gain, dy, *, name, col_block=0, dres=None, want_f32=True, want_bf=True, into=None, deps=()):
    T, width = dy.shape
    tm = _tile(T, ROW_TILE, 8)
    has_res = dres is not None

    def body(*refs):
        x_ref, g_ref, dy_ref = refs[:3]
        pos = 3
        res_ref = None
        if has_res:
            res_ref = refs[pos]
            pos += 1
        if into is not None:
            pos += 1
        outs = refs[pos:]
        dx, dg_rows = _rms_bwd(x_ref[...], g_ref[...], dy_ref[...])
        if has_res:
            dx = dx + res_ref[...]
        o = 0
        if want_f32:
            outs[o][...] = dx
            o += 1
        if want_bf:
            outs[o][...] = dx.astype(BF)
            o += 1
        _acc_rows(outs[o], dg_rows, pl.program_id(0) == 0)

    ins = [(x, (tm, width), lambda i: (i, col_block)), (gain, (1, width), lambda i: (0, 0)),
           (dy, (tm, width), lambda i: (i, 0))]
    if has_res:
        ins.append((dres, (tm, width), lambda i: (i, 0)))
    outs = []
    aliases = {}
    if want_f32:
        outs.append(((T, width), F32, (tm, width), lambda i: (i, 0)))
    if want_bf and into is not None:
        ins.append((into, None, None))
        aliases[len(ins) - 1] = len(outs)
        outs.append((into.shape, BF, (tm, width), lambda i: (i, col_block)))
    elif want_bf:
        outs.append(((T, width), BF, (tm, width), lambda i: (i, 0)))
    outs.append(((1, width), F32, (1, width), lambda i: (0, 0)))
    return _pcall(body, name=name, grid=(T // tm,), ins=ins, outs=outs, aliases=aliases, deps=deps)


def mla_prep(proj, q_norm, kv_norm, cos_t, sin_t, *, name):
    T = proj.shape[0]
    tm = _tile(T, ROW_TILE, 8)
    kr_block = (proj.shape[1] - LANES) // LANES

    def body(cq_ref, ckv_ref, kr_ref, qg_ref, kg_ref, cos_ref, sin_ref, qn_ref, kvn_ref, krope_ref):
        cq = cq_ref[...]
        qn_ref[...] = (cq * _rstd(cq) * qg_ref[...]).astype(BF)
        ckv = ckv_ref[...]
        kvn_ref[...] = (ckv * _rstd(ckv) * kg_ref[...]).astype(BF)
        krope_ref[...] = _rope_fwd(kr_ref[...], cos_ref[...], sin_ref[...]).astype(BF)

    return _pcall(
        body, name=name, grid=(T // tm,),
        ins=[(proj, (tm, Q_LORA), lambda i: (i, 0)), (proj, (tm, KV_LORA), lambda i: (i, 1)),
             (proj, (tm, LANES), lambda i: (i, kr_block)),
             (q_norm, (1, Q_LORA), lambda i: (0, 0)), (kv_norm, (1, KV_LORA), lambda i: (0, 0)),
             (cos_t, (tm, LANES), lambda i: (i, 0)), (sin_t, (tm, LANES), lambda i: (i, 0))],
        outs=[((T, Q_LORA), BF, (tm, Q_LORA), lambda i: (i, 0)), ((T, KV_LORA), BF, (tm, KV_LORA), lambda i: (i, 0)),
              ((T, LANES), BF, (tm, LANES), lambda i: (i, 0))],
        semantics=("parallel",))


def _attn_scores(q, k_blk, diagonal):
    s = lax.dot_general(q, k_blk, (((1,), (1,)), ((), ())), preferred_element_type=F32) * ATTN_SCALE
    if diagonal:
        row = lax.broadcasted_iota(jnp.int32, s.shape, 0)
        col = lax.broadcasted_iota(jnp.int32, s.shape, 1)
        s = jnp.where(col <= row, s, -jnp.inf)
    return s


def attn_fwd(q, k, v, *, name):
    T = q.shape[0]
    tq = _tile(T, ATTN_TILE, 8)

    def body(q_ref, k_ref, v_ref, o_ref, lse_ref):
        i = pl.program_id(1)
        qv = q_ref[...]

        def block(kb, carry, diagonal):
            m, l, acc = carry
            start = pl.multiple_of(kb * tq, tq)
            s = _attn_scores(qv, k_ref[pl.ds(start, tq), :], diagonal)
            m_new = jnp.maximum(m, jnp.max(s, axis=-1, keepdims=True))
            alpha = jnp.exp(m - m_new)
            p = jnp.exp(s - m_new)
            l = alpha * l + jnp.sum(p, axis=-1, keepdims=True)
            acc = alpha * acc + jnp.dot(p.astype(BF), v_ref[pl.ds(start, tq), :], preferred_element_type=F32)
            return m_new, l, acc

        init = (jnp.full((tq, 1), -jnp.inf, F32), jnp.zeros((tq, 1), F32), jnp.zeros((tq, V_HEAD), F32))
        carry = lax.fori_loop(0, i, lambda kb, c: block(kb, c, False), init)
        m, l, acc = block(i, carry, True)
        o_ref[...] = acc / l
        lse_ref[...] = jnp.broadcast_to(m + jnp.log(l), (tq, V_HEAD))

    return _pcall(
        body, name=name, grid=(HEADS, T // tq),
        ins=[(q, (tq, HEAD_PAD), lambda h, i: (i, h)), (k, (T, HEAD_PAD), lambda h, i: (0, h)),
             (v, (T, V_HEAD), lambda h, i: (0, h))],
        outs=[((T, MLA_OUT), F32, (tq, V_HEAD), lambda h, i: (i, h)),
              ((T, MLA_OUT), F32, (tq, V_HEAD), lambda h, i: (i, h))], semantics=("parallel", "parallel"))


def attn_bwd(q, k, v, o, lse, do, *, name):
    T = q.shape[0]
    tq = _tile(T, ATTN_TILE, 8)

    def body(q_ref, k_ref, v_ref, o_ref, lse_ref, do_ref, dq_ref, dk_ref, dv_ref):
        i = pl.program_id(1)

        @pl.when(i == 0)
        def _():
            dk_ref[...] = jnp.zeros_like(dk_ref)
            dv_ref[...] = jnp.zeros_like(dv_ref)

        qv = q_ref[...]
        do_t = do_ref[...]
        lse_v = lse_ref[:, 0:1]
        delta = jnp.sum(do_t.astype(F32) * o_ref[...], axis=-1, keepdims=True)

        def block(kb, dq, diagonal):
            start = pl.multiple_of(kb * tq, tq)
            k_blk = k_ref[pl.ds(start, tq), :]
            v_blk = v_ref[pl.ds(start, tq), :]
            p = jnp.exp(_attn_scores(qv, k_blk, diagonal) - lse_v)
            dp = lax.dot_general(do_t, v_blk, (((1,), (1,)), ((), ())), preferred_element_type=F32)
            ds = (p * (dp - delta) * ATTN_SCALE).astype(BF)
            dk_ref[pl.ds(start, tq), :] += lax.dot_general(ds, qv, (((0,), (0,)), ((), ())), preferred_element_type=F32)
            dv_ref[pl.ds(start, tq), :] += lax.dot_general(p.astype(BF), do_t, (((0,), (0,)), ((), ())),
                                                          preferred_element_type=F32)
            return dq + jnp.dot(ds, k_blk, preferred_element_type=F32)

        dq = lax.fori_loop(0, i, lambda kb, c: block(kb, c, False), jnp.zeros((tq, HEAD_PAD), F32))
        dq_ref[...] = block(i, dq, True)

    return _pcall(
        body, name=name, grid=(HEADS, T // tq),
        ins=[(q, (tq, HEAD_PAD), lambda h, i: (i, h)), (k, (T, HEAD_PAD), lambda h, i: (0, h)),
             (v, (T, V_HEAD), lambda h, i: (0, h)), (o, (tq, V_HEAD), lambda h, i: (i, h)),
             (lse, (tq, V_HEAD), lambda h, i: (i, h)), (do, (tq, V_HEAD), lambda h, i: (i, h))],
        outs=[((T, HEADS * HEAD_PAD), F32, (tq, HEAD_PAD), lambda h, i: (i, h)),
              ((T, HEADS * HEAD_PAD), F32, (T, HEAD_PAD), lambda h, i: (0, h)),
              ((T, MLA_OUT), F32, (T, V_HEAD), lambda h, i: (0, h))],
        semantics=("parallel", "arbitrary"))


def mla_bwd_prep(dq, dk, dv, cos_t, sin_t, dproj, *, name):
    T = dq.shape[0]
    tm = _tile(T, ROW_TILE, 8)
    kr_block = (dproj.shape[1] - LANES) // LANES

    def body(dq_ref, dk_ref, dv_ref, cos_ref, sin_ref, dproj_in, dql_ref, dkvl_ref, dkr_ref):
        cos_v, sin_v = cos_ref[...], sin_ref[...]
        kr = jnp.zeros((tm, LANES), F32)
        for h in range(HEADS):
            lo = h * HEAD_PAD
            dql_ref[:, lo:lo + QK_NOPE] = dq_ref[:, lo:lo + QK_NOPE].astype(BF)
            dql_ref[:, lo + QK_NOPE:lo + HEAD_PAD] = _rope_bwd(
                dq_ref[:, lo + QK_NOPE:lo + HEAD_PAD], cos_v, sin_v).astype(BF)
            dkvl_ref[:, lo:lo + QK_NOPE] = dk_ref[:, lo:lo + QK_NOPE].astype(BF)
            dkvl_ref[:, lo + QK_NOPE:lo + HEAD_PAD] = dv_ref[:, h * V_HEAD:(h + 1) * V_HEAD].astype(BF)
            kr = kr + dk_ref[:, lo + QK_NOPE:lo + HEAD_PAD]
        dkr_ref[...] = _rope_bwd(kr, cos_v, sin_v).astype(BF)

    W = HEADS * HEAD_PAD
    return _pcall(
        body, name=name, grid=(T // tm,),
        ins=[(dq, (tm, W), lambda i: (i, 0)), (dk, (tm, W), lambda i: (i, 0)), (dv, (tm, MLA_OUT), lambda i: (i, 0)),
             (cos_t, (tm, LANES), lambda i: (i, 0)), (sin_t, (tm, LANES), lambda i: (i, 0)), (dproj, None, None)],
        outs=[((T, W), BF, (tm, W), lambda i: (i, 0)), ((T, W), BF, (tm, W), lambda i: (i, 0)),
              (dproj.shape, BF, (tm, LANES), lambda i: (i, kr_block))],
        aliases={5: 2}, semantics=("parallel",))


def _group_norm_stats(vg):
    mu = jnp.mean(vg, axis=-1, keepdims=True)
    d = vg - mu
    r = lax.rsqrt(jnp.mean(d * d, axis=-1, keepdims=True) + EPS)
    return d * r, r


def mix_fwd(a, proj, g_mla, g_sgu, v_gain, w_tril, b_full, *, name):
    T = a.shape[0]
    tm = _tile(T, ROW_TILE, CHUNK)
    n_chunk = tm // CHUNK

    def body(a_ref, u_ref, v_ref, gm_ref, gs_ref, vg_ref, w_ref, b_ref, o_ref, s_scr):
        av = a_ref[...]
        o_ref[:, :MLA_OUT] = (av * _rstd(av) * gm_ref[...]).astype(BF)
        for g in range(GROUPS):
            sl = slice(g * CH, (g + 1) * CH)
            vhat, _ = _group_norm_stats(_gelu(v_ref[:, sl]))
            vn = (vhat * vg_ref[:, sl]).astype(BF)
            u = _gelu(u_ref[:, sl])
            for ci in range(n_chunk):
                rs = slice(ci * CHUNK, (ci + 1) * CHUNK)
                y = jnp.dot(w_ref[g], vn[rs], preferred_element_type=F32) + b_ref[:, sl]
                s_scr[rs, sl] = u[rs] * y
        s = s_scr[...]
        o_ref[:, MLA_OUT:] = (s * _rstd(s) * gs_ref[...]).astype(BF)

    return _pcall(
        body, name=name, grid=(T // tm,),
        ins=[(a, (tm, MLA_OUT), lambda i: (i, 0)), (proj, (tm, SGU_OUT), lambda i: (i, 1)),
             (proj, (tm, SGU_OUT), lambda i: (i, 2)), (g_mla, (1, MLA_OUT), lambda i: (0, 0)),
             (g_sgu, (1, SGU_OUT), lambda i: (0, 0)), (v_gain, (1, SGU_OUT), lambda i: (0, 0)),
             (w_tril, (GROUPS, CHUNK, CHUNK), lambda i: (0, 0, 0)), (b_full, (CHUNK, SGU_OUT), lambda i: (0, 0))],
        outs=[((T, MLA_OUT + SGU_OUT), BF, (tm, MLA_OUT + SGU_OUT), lambda i: (i, 0))],
        scratch=[pltpu.VMEM((tm, SGU_OUT), F32)], semantics=("parallel",))[0]


def mix_bwd(dmixed, a, proj, g_mla, g_sgu, v_gain, w_tril, w_tril_t, b_full, *, name, deps=()):
    T = a.shape[0]
    tm = _tile(T, ROW_TILE, CHUNK)
    n_chunk = tm // CHUNK
    uv0 = Q_LORA + KV_LORA

    def body(dm_a_ref, dm_s_ref, a_ref, u_ref, v_ref, gm_ref, gs_ref, vg_ref, w_ref, wt_ref, b_ref,
             da_ref, duv_ref, dgm_ref, dgs_ref, dvg_ref, dw_ref, db_ref, s_scr, y_scr):
        first = pl.program_id(0) == 0
        duv_ref[:, :uv0] = jnp.zeros((tm, uv0), BF)
        duv_ref[:, uv0 + 2 * SGU_OUT:] = jnp.zeros((tm, duv_ref.shape[1] - uv0 - 2 * SGU_OUT), BF)
        da, dgm_rows = _rms_bwd(a_ref[...], gm_ref[...], dm_a_ref[...])
        da_ref[...] = da.astype(BF)
        _acc_rows(dgm_ref, dgm_rows, first)

        for g in range(GROUPS):
            sl = slice(g * CH, (g + 1) * CH)
            vhat, _ = _group_norm_stats(_gelu(v_ref[:, sl]))
            vn = (vhat * vg_ref[:, sl]).astype(BF)
            u = _gelu(u_ref[:, sl])
            for ci in range(n_chunk):
                rs = slice(ci * CHUNK, (ci + 1) * CHUNK)
                y = jnp.dot(w_ref[g], vn[rs], preferred_element_type=F32) + b_ref[:, sl]
                y_scr[rs, sl] = y
                s_scr[rs, sl] = u[rs] * y
        ds, dgs_rows = _rms_bwd(s_scr[...], gs_ref[...], dm_s_ref[...])
        _acc_rows(dgs_ref, dgs_rows, first)
        s_scr[...] = ds

        @pl.when(first)
        def _():
            dw_ref[...] = jnp.zeros_like(dw_ref)
            db_ref[...] = jnp.zeros_like(db_ref)

        for g in range(GROUPS):
            sl = slice(g * CH, (g + 1) * CH)
            upre = u_ref[:, sl]
            vpre = v_ref[:, sl]
            u = _gelu(upre)
            vhat, r = _group_norm_stats(_gelu(vpre))
            gain = vg_ref[:, sl]
            vn = (vhat * gain).astype(BF)
            dsg = s_scr[:, sl]
            duv_ref[:, uv0 + g * CH:uv0 + (g + 1) * CH] = (dsg * y_scr[:, sl] * _gelu_grad(upre)).astype(BF)
            dy = dsg * u
            dyb = dy.astype(BF)
            dvn_parts = []
            for ci in range(n_chunk):
                rs = slice(ci * CHUNK, (ci + 1) * CHUNK)
                dvn_parts.append(jnp.dot(wt_ref[g], dyb[rs], preferred_element_type=F32))
                dw_ref[g] += lax.dot_general(dyb[rs], vn[rs], (((1,), (1,)), ((), ())), preferred_element_type=F32)
                db_ref[:, sl] += jnp.broadcast_to(jnp.sum(dy[rs], axis=-1, keepdims=True), (CHUNK, CH))
            dvn = dvn_parts[0] if n_chunk == 1 else jnp.concatenate(dvn_parts, axis=0)
            _acc_rows(dvg_ref.at[:, sl], dvn * vhat, first)
            dvh = dvn * gain
            dvg = r * (dvh - jnp.mean(dvh, axis=-1, keepdims=True)
                       - vhat * jnp.mean(dvh * vhat, axis=-1, keepdims=True))
            duv_ref[:, uv0 + SGU_OUT + g * CH:uv0 + SGU_OUT + (g + 1) * CH] = (dvg * _gelu_grad(vpre)).astype(BF)

    return _pcall(
        body, name=name, grid=(T // tm,),
        ins=[(dmixed, (tm, MLA_OUT), lambda i: (i, 0)), (dmixed, (tm, SGU_OUT), lambda i: (i, 1)),
             (a, (tm, MLA_OUT), lambda i: (i, 0)), (proj, (tm, SGU_OUT), lambda i: (i, 1)),
             (proj, (tm, SGU_OUT), lambda i: (i, 2)), (g_mla, (1, MLA_OUT), lambda i: (0, 0)),
             (g_sgu, (1, SGU_OUT), lambda i: (0, 0)), (v_gain, (1, SGU_OUT), lambda i: (0, 0)),
             (w_tril, (GROUPS, CHUNK, CHUNK), lambda i: (0, 0, 0)), (w_tril_t, (GROUPS, CHUNK, CHUNK), lambda i: (0, 0, 0)),
             (b_full, (CHUNK, SGU_OUT), lambda i: (0, 0))],
        outs=[((T, MLA_OUT), BF, (tm, MLA_OUT), lambda i: (i, 0)),
              ((T, proj.shape[1]), BF, (tm, proj.shape[1]), lambda i: (i, 0)),
              ((1, MLA_OUT), F32, (1, MLA_OUT), lambda i: (0, 0)), ((1, SGU_OUT), F32, (1, SGU_OUT), lambda i: (0, 0)),
              ((1, SGU_OUT), F32, (1, SGU_OUT), lambda i: (0, 0)),
              ((GROUPS, CHUNK, CHUNK), F32, (GROUPS, CHUNK, CHUNK), lambda i: (0, 0, 0)),
              ((CHUNK, SGU_OUT), F32, (CHUNK, SGU_OUT), lambda i: (0, 0))],
        scratch=[pltpu.VMEM((tm, SGU_OUT), F32), pltpu.VMEM((tm, SGU_OUT), F32)], deps=deps)


def _shift_down(z, n, row):
    return jnp.where(row >= n, pltpu.roll(z, n, 0), 0.0)


def _shift_up(z, n, row, T):
    return jnp.where(row < T - n, pltpu.roll(z, T - n, 0), 0.0)


def conv_fwd(proj, conv_w, *, name):
    T, D3 = proj.shape
    D = D3 // 3
    tn = _tile(D, 256)
    nj = D // tn

    def body(b_ref, c_ref, x_ref, w_ref, o_ref):
        row = lax.broadcasted_iota(jnp.int32, (T, tn), 0)
        z = c_ref[...] * x_ref[...]
        zc = w_ref[2:3, :] * z + w_ref[1:2, :] * _shift_down(z, 1, row) + w_ref[0:1, :] * _shift_down(z, 2, row)
        o_ref[...] = (b_ref[...] * zc).astype(BF)

    return _pcall(
        body, name=name, grid=(nj,),
        ins=[(proj, (T, tn), lambda j: (0, j)), (proj, (T, tn), lambda j: (0, nj + j)),
             (proj, (T, tn), lambda j: (0, 2 * nj + j)), (conv_w, (3, tn), lambda j: (0, j))],
        outs=[((T, D), BF, (T, tn), lambda j: (0, j))], semantics=("parallel",))[0]


def conv_bwd(dg, proj, conv_w, *, name, deps=()):
    T, D3 = proj.shape
    D = D3 // 3
    tn = _tile(D, 256)
    nj = D // tn

    def body(dg_ref, b_ref, c_ref, x_ref, w_ref, dp_ref, dw_ref, dc_scr, dx_scr):
        part = pl.program_id(1)

        @pl.when(part == 0)
        def _():
            row = lax.broadcasted_iota(jnp.int32, (T, tn), 0)
            c, x = c_ref[...], x_ref[...]
            z = c * x
            z1 = _shift_down(z, 1, row)
            z2 = _shift_down(z, 2, row)
            dgv = dg_ref[...]
            zc = w_ref[2:3, :] * z + w_ref[1:2, :] * z1 + w_ref[0:1, :] * z2
            dp_ref[...] = (dgv * zc).astype(BF)
            dzc = dgv * b_ref[...]
            dw_ref[0:1, :] = jnp.sum(dzc * z2, axis=0, keepdims=True)
            dw_ref[1:2, :] = jnp.sum(dzc * z1, axis=0, keepdims=True)
            dw_ref[2:3, :] = jnp.sum(dzc * z, axis=0, keepdims=True)
            dz = (w_ref[2:3, :] * dzc + w_ref[1:2, :] * _shift_up(dzc, 1, row, T)
                  + w_ref[0:1, :] * _shift_up(dzc, 2, row, T))
            dc_scr[...] = (dz * x).astype(BF)
            dx_scr[...] = (dz * c).astype(BF)

        @pl.when(part == 1)
        def _():
            dp_ref[...] = dc_scr[...]

        @pl.when(part == 2)
        def _():
            dp_ref[...] = dx_scr[...]

    return _pcall(
        body, name=name, grid=(nj, 3),
        ins=[(dg, (T, tn), lambda j, p: (0, j)), (proj, (T, tn), lambda j, p: (0, j)),
             (proj, (T, tn), lambda j, p: (0, nj + j)), (proj, (T, tn), lambda j, p: (0, 2 * nj + j)),
             (conv_w, (3, tn), lambda j, p: (0, j))],
        outs=[((T, D3), BF, (T, tn), lambda j, p: (0, p * nj + j)), ((3, D), F32, (3, tn), lambda j, p: (0, j))],
        scratch=[pltpu.VMEM((T, tn), BF), pltpu.VMEM((T, tn), BF)], semantics=("parallel", "arbitrary"), deps=deps)


def loss_bwd(x_parts, gain, target, *, name):
    T, D = target.shape
    tm = _tile(T, ROW_TILE, 8)
    n_x = len(x_parts)

    def body(*refs):
        x_refs = refs[:n_x]
        g_ref, t_ref, dx_ref, dxb_ref, dg_ref, loss_ref = refs[n_x:]
        first = pl.program_id(0) == 0
        xv = jnp.concatenate([r[...] for r in x_refs], axis=-1) if n_x > 1 else x_refs[0][...]
        r = _rstd(xv)
        xh = xv * r
        gain_v = g_ref[...]
        err = xh * gain_v - t_ref[...]
        part = 0.5 * jnp.sum(jnp.mean(err * err, axis=-1, keepdims=True), axis=0, keepdims=True)
        _acc_rows(loss_ref, jnp.broadcast_to(part, (1, LANES)), first)
        dy = err * (1.0 / D)
        gdy = dy * gain_v
        dx = r * (gdy - xh * jnp.mean(gdy * xh, axis=-1, keepdims=True))
        dx_ref[...] = dx
        dxb_ref[...] = dx.astype(BF)
        _acc_rows(dg_ref, dy * xh, first)

    return _pcall(
        body, name=name, grid=(T // tm,),
        ins=[(p, (tm, D // n_x), lambda i: (i, 0)) for p in x_parts]
        + [(gain, (1, D), lambda i: (0, 0)), (target, (tm, D), lambda i: (i, 0))],
        outs=[((T, D), F32, (tm, D), lambda i: (i, 0)), ((T, D), BF, (tm, D), lambda i: (i, 0)),
              ((1, D), F32, (1, D), lambda i: (0, 0)), ((1, LANES), F32, (1, LANES), lambda i: (0, 0))])


def _adamw(g, w, m, v):
    m = ADAM_B1 * m + (1.0 - ADAM_B1) * g
    v = ADAM_B2 * v + (1.0 - ADAM_B2) * (g * g)
    m_hat = m / ADAM_C1
    v_hat = v / ADAM_C2
    delta = -ADAM_LR * (m_hat / (jnp.sqrt(v_hat) + ADAM_EPS) + ADAM_WD * w)
    return delta, m, v


def adam_flat(g, w, m, v, *, name):
    def body(g_ref, w_ref, m_ref, v_ref, d_ref, nm_ref, nv_ref):
        d, nm, nv = _adamw(g_ref[...], w_ref[...], m_ref[...], v_ref[...])
        d_ref[...] = d
        nm_ref[...] = nm
        nv_ref[...] = nv

    blk = g.shape
    zero = lambda: (0, 0)
    return _pcall(body, name=name, grid=(),
                  ins=[(t, blk, zero) for t in (g, w, m, v)],
                  outs=[(blk, F32, blk, zero)] * 3)


def _chip_slots():
    x, y, c = lax.axis_index("x"), lax.axis_index("y"), lax.axis_index("c")
    chips = [(1 - x, y), (x, 1 - y), (1 - x, 1 - y)]
    return x, y, c, chips


def device_index():
    x, y, c, chips = _chip_slots()
    return jnp.stack([4 * x + 2 * y + c, 2 * x + y] + [4 * cx + 2 * cy + c for cx, cy in chips]
                     + [2 * cx + cy for cx, cy in chips]).astype(jnp.int32)


def _job_rows(R, C, n_steps):
    if n_steps is None:
        n_steps = max(1, R * C // STREAM_BLOCK_ELEMS)
    n_blk = max([d for d in range(1, n_steps + 1) if R % d == 0 and (R // d) % 16 == 0] or [1])
    return R // n_blk, n_blk


def run_job(job, *, index, name, deps=()):
    jb = job(None)
    n_in = len(jb["ins"])

    def body(idx_ref, *refs):
        jb["fn"](refs[:n_in], refs[n_in:n_in + len(jb["outs"])])

    return _pcall(body, name=name, grid=(jb["n_blk"],), ins=jb["ins"], outs=jb["outs"], prefetch=index,
                  aliases={1 + a: o for a, o in jb["aliases"].items()}, semantics=("parallel",), deps=deps)


def adam_job(gs, a_buf, b_buf, w, m, v, layer, prev):
    L, R, C = w.shape

    def build(n_steps):
        tr, n_blk = _job_rows(R, C, n_steps)
        blk = (None, tr, C)
        row = lambda t: jnp.minimum(t, n_blk - 1)
        ins = [(gs, blk, lambda t, s: (s[0], row(t), 0)), (a_buf, blk, lambda t, s: (s[1], row(t), 0))]
        ins += [(b_buf, blk, lambda t, s, j=j: (j, row(t), 0)) for j in range(3)]
        ins += [(p, blk, lambda t, s: (layer, row(t), 0)) for p in (w, m, v)]
        ins += [(p, None, None) for p in (prev or [])]

        def fn(i, o):
            g = ((((i[0][...].astype(F32) + i[1][...].astype(F32)) + i[2][...].astype(F32))
                  + i[3][...].astype(F32)) + i[4][...].astype(F32))
            d, nm, nv = _adamw(g, i[5][...], i[6][...], i[7][...])
            o[0][...] = g
            o[1][...] = d
            o[2][...] = nm
            o[3][...] = nv

        return dict(ins=ins, outs=[((L, R, C), F32, blk, lambda t, s: (layer, row(t), 0))] * 4, fn=fn,
                    aliases={8 + o: o for o in range(4)} if prev else {}, n_blk=n_blk)

    return build


def pair_job(gs, a_buf):
    _, R, C = gs.shape

    def build(n_steps):
        tr, n_blk = _job_rows(R, C, n_steps)
        blk = (None, tr, C)
        row = lambda t: jnp.minimum(t, n_blk - 1)
        ins = [(gs, blk, lambda t, s, j=j: (s[2 + j], row(t), 0)) for j in range(3)]
        ins += [(a_buf, blk, lambda t, s, j=j: (s[5 + j], row(t), 0)) for j in range(3)]

        def fn(i, o):
            for j in range(3):
                o[0][j] = (i[j][...].astype(F32) + i[3 + j][...].astype(F32)).astype(BF)

        return dict(ins=ins, outs=[((3, R, C), BF, (3, tr, C), lambda t, s: (0, row(t), 0))], fn=fn, aliases={},
                    n_blk=n_blk)

    return build


def reduce_sum(gs, a_buf, b_buf, *, name):
    _, R, C = gs.shape
    tr = _tile(R, 256, 16)
    x, y, c, _ = _chip_slots()
    idx = jnp.stack([4 * x + 2 * y + c, 2 * x + y]).astype(jnp.int32)

    def body(idx_ref, g_ref, a_ref, b0_ref, b1_ref, b2_ref, o_ref):
        o_ref[...] = ((((g_ref[...].astype(F32) + a_ref[...].astype(F32)) + b0_ref[...].astype(F32))
                       + b1_ref[...].astype(F32)) + b2_ref[...].astype(F32))

    blk3 = (None, tr, C)
    return _pcall(body, name=name, grid=(R // tr,),
                  ins=[(gs, blk3, lambda i, s: (s[0], i, 0)), (a_buf, blk3, lambda i, s: (s[1], i, 0)),
                       (b_buf, blk3, lambda i, s: (0, i, 0)), (b_buf, blk3, lambda i, s: (1, i, 0)),
                       (b_buf, blk3, lambda i, s: (2, i, 0))],
                  outs=[((R, C), F32, (tr, C), lambda i, s: (i, 0))], prefetch=idx, semantics=("parallel",))[0]


def adam_rows(g, w, m, v, *, name):
    R, C = g.shape
    tr = _tile(R, 256, 8)

    def body(g_ref, w_ref, m_ref, v_ref, d_ref, nm_ref, nv_ref):
        d, nm, nv = _adamw(g_ref[...], w_ref[...], m_ref[...], v_ref[...])
        d_ref[...] = d
        nm_ref[...] = nm
        nv_ref[...] = nv

    spec = ((tr, C), lambda i: (i, 0))
    return _pcall(body, name=name, grid=(R // tr,), ins=[(t, *spec) for t in (g, w, m, v)],
                  outs=[((R, C), F32, *spec)] * 3, semantics=("parallel",))


def sum_rows8(gathered, rows, *, name):
    W = gathered.shape[1]

    def body(g_ref, o_ref):
        acc = g_ref[0:rows, :]
        for d in range(1, N_DEV):
            acc = acc + g_ref[d * rows:(d + 1) * rows, :]
        o_ref[...] = acc

    return _pcall(body, name=name, grid=(), ins=[(gathered, gathered.shape, lambda: (0, 0))],
                  outs=[((rows, W), F32, (rows, W), lambda: (0, 0))])[0]


HBM_SPEC = pl.BlockSpec(memory_space=pltpu.HBM)
SEM_SPEC = pl.BlockSpec(memory_space=pltpu.SEMAPHORE)
ANY_SPEC = pl.BlockSpec(memory_space=pl.ANY)
DATAFLOW = pltpu.SideEffectType.DATAFLOW_SIDE_EFFECTING


def _in_hbm(v):
    return pltpu.with_memory_space_constraint(v, pltpu.HBM)


def _slot(p):
    return 4 * p[0] + 2 * p[1] + p[2]


def _gather_peers():
    x, y, c, chips = _chip_slots()
    return (x, y, c), [(x, y, 1 - c)] + [(*chip, c) for chip in chips]


def gather_start(groups, after, *, name):
    flat = [s for g in groups for s in g]
    n, n_g = len(flat), len(groups)
    where = [(gi, ti) for gi, g in enumerate(groups) for ti in range(len(g))]

    def body(*refs):
        src, land = refs[:n], refs[n:2 * n]
        sems = refs[2 * n + 1:2 * n + 1 + 2 * n_g]
        me, peers = _gather_peers()
        for t in range(n):
            gi, ti = where[t]
            for k, to in enumerate(peers):
                pltpu.make_async_remote_copy(
                    src_ref=src[t], dst_ref=land[t].at[_slot(me)], send_sem=sems[2 * gi].at[4 * ti + k],
                    recv_sem=sems[2 * gi + 1].at[4 * ti + k], device_id=to, device_id_type=MESH).start()
        refs[-1][...] = jnp.zeros_like(refs[-1])

    out_shape = []
    for g in groups:
        out_shape += [pltpu.SemaphoreType.DMA((4 * len(g),)), pltpu.SemaphoreType.DMA((4 * len(g),))]
    out_shape += [pltpu.HBM(s.shape, s.dtype) for s in flat]
    out_shape += [pltpu.HBM((N_DEV,) + s.shape, s.dtype) for s in flat]
    out_shape += [jax.ShapeDtypeStruct((8, LANES), F32)]
    aliases = {t: 2 * n_g + t for t in range(n)}
    aliases.update({n + t: 2 * n_g + n + t for t in range(n)})
    res = pl.pallas_call(
        body, name=name, out_shape=out_shape, in_specs=[HBM_SPEC] * (2 * n) + [ANY_SPEC],
        out_specs=[SEM_SPEC] * (2 * n_g) + [HBM_SPEC] * (2 * n) + [pl.BlockSpec(memory_space=pltpu.VMEM)],
        input_output_aliases=aliases, compiler_params=pltpu.CompilerParams(has_side_effects=DATAFLOW),
    )(*[_in_hbm(s) for s in flat], *[_in_hbm(lax.empty((N_DEV,) + s.shape, s.dtype)) for s in flat], after)
    out, off = [], 0
    for gi, g in enumerate(groups):
        k = len(g)
        out.append((res[2 * gi], res[2 * gi + 1], res[2 * n_g + off:2 * n_g + off + k],
                    res[2 * n_g + n + off:2 * n_g + n + off + k]))
        off += k
    return out, res[-1]


def gather_wait(started, after, *, name):
    send_sems, recv_sems, srcs, lands = started
    n = len(srcs)
    after = list(after)

    def body(*refs):
        src, land = refs[:n], refs[n:2 * n]
        send, recv = refs[2 * n], refs[2 * n + 1]
        _, peers = _gather_peers()
        for t in range(n):
            for k, frm in enumerate(peers):
                cp = pltpu.make_async_remote_copy(
                    src_ref=src[t], dst_ref=land[t].at[_slot(frm)], send_sem=send.at[4 * t + k],
                    recv_sem=recv.at[4 * t + k],
                    device_id=frm, device_id_type=MESH)
                cp.wait_send()
                cp.wait_recv()

    res = pl.pallas_call(
        body, name=name,
        out_shape=[pltpu.HBM(s.shape, s.dtype) for s in srcs] + [pltpu.HBM(l.shape, l.dtype) for l in lands],
        in_specs=[HBM_SPEC] * (2 * n) + [SEM_SPEC, SEM_SPEC] + [ANY_SPEC] * len(after),
        out_specs=[HBM_SPEC] * (2 * n), input_output_aliases={t: t for t in range(2 * n)},
        compiler_params=pltpu.CompilerParams(has_side_effects=DATAFLOW),
    )(*srcs, *lands, send_sems, recv_sems, *after)
    return res[:n], res[n:]


def place_own(src, land, *, name):
    R, C = src.shape
    tr = _tile(R, 512, 16)
    x, y, c, _ = _chip_slots()
    idx = jnp.stack([4 * x + 2 * y + c]).astype(jnp.int32)

    def body(idx_ref, s_ref, land_ref, o_ref):
        o_ref[...] = s_ref[...]

    return _pcall(body, name=name, grid=(R // tr,),
                  ins=[(src, (tr, C), lambda i, s: (i, 0)), (land, None, None)],
                  outs=[(land.shape, land.dtype, (None, tr, C), lambda i, s: (s[0], i, 0))],
                  prefetch=idx, aliases={2: 0}, semantics=("parallel",))[0]


def gather_finish(srcs, lands, *, name):
    n = len(srcs)

    def body(*refs):
        land = refs[n:2 * n]
        send_sems, recv_sems = refs[2 * n:]
        x, y, c, chips = _chip_slots()
        me, sibling = (x, y, c), (x, y, 1 - c)

        def copy(t, j, block, to):
            return pltpu.make_async_remote_copy(
                src_ref=land[t].at[_slot(block)], dst_ref=land[t].at[_slot(block)], send_sem=send_sems.at[t, j],
                recv_sem=recv_sems.at[t, j], device_id=to, device_id_type=MESH)

        sends = [copy(t, j, (*chip, c), sibling) for t in range(n) for j, chip in enumerate(chips)]
        for cp in sends:
            cp.start()
        for t in range(n):
            for j, chip in enumerate(chips):
                copy(t, j, (*chip, 1 - c), me).wait_recv()
        for cp in sends:
            cp.wait_send()

    passed = pl.pallas_call(
        body, name=name, out_shape=[jax.ShapeDtypeStruct(l.shape, l.dtype) for l in lands],
        in_specs=[ANY_SPEC] * n, out_specs=[ANY_SPEC] * n,
        input_output_aliases={t: t for t in range(n)},
        scratch_shapes=[pltpu.SemaphoreType.DMA((n, 3)), pltpu.SemaphoreType.DMA((n, 3))],
    )(*lands)
    return [place_own(s, l, name=f"{name}_own{t}") for t, (s, l) in enumerate(zip(srcs, passed))]


def chips_start(pairs, *, name):
    n = len(pairs)

    def body(*refs):
        src, land = refs[:n], refs[n:2 * n]
        send, recv = refs[2 * n], refs[2 * n + 1]
        token = refs[-1]
        x, y, c, chips = _chip_slots()
        for t in range(n):
            for j, chip in enumerate(chips):
                pltpu.make_async_remote_copy(
                    src_ref=src[t].at[j], dst_ref=land[t].at[j], send_sem=send.at[3 * t + j],
                    recv_sem=recv.at[3 * t + j], device_id=(*chip, c), device_id_type=MESH).start()
        token[...] = jnp.zeros_like(token)

    res = pl.pallas_call(
        body, name=name,
        out_shape=[pltpu.SemaphoreType.DMA((3 * n,)), pltpu.SemaphoreType.DMA((3 * n,))]
        + [pltpu.HBM(p.shape, p.dtype) for p in pairs] * 2 + [jax.ShapeDtypeStruct((8, LANES), F32)],
        in_specs=[HBM_SPEC] * (2 * n),
        out_specs=[SEM_SPEC, SEM_SPEC] + [HBM_SPEC] * (2 * n) + [pl.BlockSpec(memory_space=pltpu.VMEM)],
        input_output_aliases={t: 2 + t for t in range(2 * n)},
        compiler_params=pltpu.CompilerParams(has_side_effects=DATAFLOW),
    )(*[_in_hbm(p) for p in pairs], *[_in_hbm(lax.empty(p.shape, p.dtype)) for p in pairs])
    return res[0], res[1], res[2:2 + n], res[2 + n:2 + 2 * n], res[-1]


def chips_wait(started, after, *, name):
    send_sems, recv_sems, srcs, lands, _ = started
    n = len(srcs)

    def body(*refs):
        src, land = refs[:n], refs[n:2 * n]
        send, recv = refs[2 * n], refs[2 * n + 1]
        x, y, c, chips = _chip_slots()
        for t in range(n):
            for j, chip in enumerate(chips):
                cp = pltpu.make_async_remote_copy(
                    src_ref=src[t].at[j], dst_ref=land[t].at[j], send_sem=send.at[3 * t + j],
                    recv_sem=recv.at[3 * t + j], device_id=(*chip, c), device_id_type=MESH)
                cp.wait_send()
                cp.wait_recv()

    res = pl.pallas_call(
        body, name=name, out_shape=[pltpu.HBM(s.shape, s.dtype) for s in srcs] * 2,
        in_specs=[HBM_SPEC] * (2 * n) + [SEM_SPEC, SEM_SPEC, ANY_SPEC], out_specs=[HBM_SPEC] * (2 * n),
        input_output_aliases={t: t for t in range(2 * n)},
        compiler_params=pltpu.CompilerParams(has_side_effects=DATAFLOW),
    )(*srcs, *lands, send_sems, recv_sems, after)
    return res[n:]


def _sibling_copies(src, land, send, recv, n):
    x, y, c, _ = _chip_slots()
    return [pltpu.make_async_remote_copy(
        src_ref=src[t].at[4 * (q // 2) + 2 * (q % 2) + (1 - c)], dst_ref=land[t].at[q], send_sem=send.at[4 * t + q],
        recv_sem=recv.at[4 * t + q], device_id=(x, y, 1 - c), device_id_type=MESH)
        for t in range(n) for q in range(4)]


def sibling_start(gs, *, name):
    n = len(gs)

    def body(*refs):
        for cp in _sibling_copies(refs[:n], refs[n:2 * n], refs[2 * n], refs[2 * n + 1], n):
            cp.start()
        refs[-1][...] = jnp.zeros_like(refs[-1])

    lands = [lax.empty((4,) + g.shape[1:], g.dtype) for g in gs]
    res = pl.pallas_call(
        body, name=name,
        out_shape=[pltpu.SemaphoreType.DMA((4 * n,)), pltpu.SemaphoreType.DMA((4 * n,))]
        + [pltpu.HBM(g.shape, g.dtype) for g in gs] + [pltpu.HBM(l.shape, l.dtype) for l in lands]
        + [jax.ShapeDtypeStruct((8, LANES), F32)],
        in_specs=[HBM_SPEC] * (2 * n),
        out_specs=[SEM_SPEC, SEM_SPEC] + [HBM_SPEC] * (2 * n) + [pl.BlockSpec(memory_space=pltpu.VMEM)],
        input_output_aliases={t: 2 + t for t in range(2 * n)},
        compiler_params=pltpu.CompilerParams(has_side_effects=DATAFLOW),
    )(*[_in_hbm(g) for g in gs], *[_in_hbm(l) for l in lands])
    return res[0], res[1], res[2:2 + n], res[2 + n:2 + 2 * n], res[-1]


def sibling_wait(started, after, *, name):
    send_sems, recv_sems, srcs, lands, _ = started
    n = len(srcs)

    def body(*refs):
        for cp in _sibling_copies(refs[:n], refs[n:2 * n], refs[2 * n], refs[2 * n + 1], n):
            cp.wait_send()
            cp.wait_recv()

    res = pl.pallas_call(
        body, name=name,
        out_shape=[pltpu.HBM(s.shape, s.dtype) for s in srcs] + [pltpu.HBM(l.shape, l.dtype) for l in lands],
        in_specs=[HBM_SPEC] * (2 * n) + [SEM_SPEC, SEM_SPEC, ANY_SPEC], out_specs=[HBM_SPEC] * (2 * n),
        input_output_aliases={t: t for t in range(2 * n)},
        compiler_params=pltpu.CompilerParams(has_side_effects=DATAFLOW),
    )(*srcs, *lands, send_sems, recv_sems, after)
    return res[:n], res[n:]


def all_gather_vmem(x_shard, *, name, after=None):
    m_per, n = x_shard.shape
    n_after = 0 if after is None else 1

    def body(x_ref, *rest):
        out_ref, send_sems, recv_sems, local_sem = rest[n_after:]
        x, y, c, chips = _chip_slots()
        me, sibling = (x, y, c), (x, y, 1 - c)

        def rows(px, py, pc):
            return out_ref.at[pl.ds((4 * px + 2 * py + pc) * m_per, m_per), :]

        def copy(k, block, to, src=None):
            return pltpu.make_async_remote_copy(
                src_ref=rows(*block) if src is None else src, dst_ref=rows(*block),
                send_sem=send_sems.at[k], recv_sem=recv_sems.at[k], device_id=to, device_id_type=MESH)

        mine = pltpu.make_async_copy(x_ref, rows(*me), local_sem)
        mine.start()
        first = [copy(0, me, sibling, src=x_ref)]
        first += [copy(1 + j, me, (*chip, c), src=x_ref) for j, chip in enumerate(chips)]
        for cp in first:
            cp.start()
        passed = [copy(4 + j, (*chip, c), sibling) for j, chip in enumerate(chips)]
        for j, chip in enumerate(chips):
            copy(1 + j, (*chip, c), me).wait_recv()
            passed[j].start()
        copy(0, sibling, me).wait_recv()
        for j, chip in enumerate(chips):
            copy(4 + j, (*chip, 1 - c), me).wait_recv()
        for cp in first + passed:
            cp.wait_send()
        mine.wait()

    vmem = pl.BlockSpec(memory_space=pltpu.VMEM)
    return pl.pallas_call(
        body, name=name, out_shape=jax.ShapeDtypeStruct((N_DEV * m_per, n), x_shard.dtype),
        in_specs=[vmem] + [ANY_SPEC] * n_after, out_specs=vmem,
        scratch_shapes=[pltpu.SemaphoreType.DMA((7,)), pltpu.SemaphoreType.DMA((7,)), pltpu.SemaphoreType.DMA],
        compiler_params=pltpu.CompilerParams(vmem_limit_bytes=int(min(
            VMEM_LIMIT_CAP, 2 * (N_DEV + 1) * m_per * n * x_shard.dtype.itemsize + 16 * 2 ** 20))),
    )(x_shard, *([] if after is None else [after]))


def _rope_slab(cols):
    z = jnp.zeros(cols.shape[:-1] + (HALF_ROPE,), cols.dtype)
    return jnp.concatenate([cols[..., :HALF_ROPE], z, cols[..., HALF_ROPE:], z], axis=-1)


def _rope_unslab(slab):
    return jnp.concatenate([slab[..., :HALF_ROPE], slab[..., 2 * HALF_ROPE:3 * HALF_ROPE]], axis=-1)


def _pack_w_in_t(wt_g):
    s, c, d = wt_g.shape
    w = wt_g.reshape(s * c, d)
    c2, c3 = Q_LORA + KV_LORA, Q_LORA + KV_LORA + QK_ROPE
    r = w[c2:c3]
    z = jnp.zeros((HALF_ROPE, d), w.dtype)
    return jnp.concatenate([w[:c2], w[c3:], r[:HALF_ROPE], z, r[HALF_ROPE:], z], axis=0)


def unpack_w_in_t_grad(dwt, *, name):
    n_rows, d = dwt.shape
    rb = QK_ROPE
    head = (Q_LORA + KV_LORA) // rb
    n_out = (n_rows - rb) // rb
    slab_block = (n_rows - 2 * rb) // (2 * rb)

    def body(in_ref, slab_ref, o_ref):
        r = pl.program_id(0)

        @pl.when(r == head)
        def _():
            o_ref[:HALF_ROPE, :] = slab_ref[:HALF_ROPE, :]
            o_ref[HALF_ROPE:, :] = slab_ref[2 * HALF_ROPE:3 * HALF_ROPE, :]

        @pl.when(r != head)
        def _():
            o_ref[...] = in_ref[...]

    out = _pcall(body, name=name, grid=(n_out,),
                 ins=[(dwt, (rb, d), lambda r: (jnp.where(r > head, r - 1, jnp.minimum(r, head - 1)), 0)),
                      (dwt, (2 * rb, d), lambda r: (slab_block, 0))],
                 outs=[((n_out * rb, d), dwt.dtype, (rb, d), lambda r: (r, 0))], semantics=("parallel",))[0]
    return out.reshape(N_DEV, n_out * rb // N_DEV, d)


def _rope_tables(positions):
    inv_freq = ROPE_BASE ** (-jnp.arange(0, QK_ROPE, 2, dtype=F32) / QK_ROPE)
    ang = positions.astype(F32)[:, None] * inv_freq
    cos, sin = jnp.cos(ang), jnp.sin(ang)
    z = jnp.zeros_like(cos)
    return jnp.concatenate([cos, z, cos, z], axis=-1), jnp.concatenate([-sin, z, sin, z], axis=-1)


def _mlp_up(x, gain, w1, tag):
    hn = rms_fwd(x, gain, name=f"mlp{tag}_norm")

    def act_epi(acc):
        a = jnp.maximum(acc, 0.0)
        return a, a * a

    T = x.shape[0]
    F = w1.shape[0] * w1.shape[2]
    a, act = mm(hn, w1, name=f"mlp{tag}_up", outs=[((T, F), BF, None), ((T, F), BF, None)], epi=act_epi)
    return hn, a, act


def _mlp_down(x, act, w2, tag, part=0):
    n = w2.shape[1]
    bm = _tile(x.shape[0], MM_TILE)
    bn = _tile(n, MM_TILE)
    per = n // bn
    return mm(act, w2, name=f"mlp{tag}_down{part}", out=((x.shape[0], n), F32), bm=bm, bn=bn,
              epi=lambda acc, r: (acc + r[...],), epi_ins=[(x, (bm, bn), lambda i, j, k: (i, part * per + j))])


def _mlp_bwd_weights(w1, w2, saved, dxb, tag):
    hn, a, act = saved
    T, D = dxb.shape
    F = a.shape[1]
    bm = _tile(T, MM_TILE)
    bn = _tile(F, min(MM_TILE, w1.shape[2]))
    dhid = mm(dxb, w2, tb=True, name=f"mlp{tag}_dhid", out=((T, F), BF), bm=bm, bn=bn,
              epi=lambda acc, a_ref: (2.0 * a_ref[...].astype(F32) * acc,),
              epi_ins=[(a, (bm, bn), lambda i, j, k: (i, j))])
    dw2 = mm(act, dxb, ta=True, name=f"mlp{tag}_dw2", out=((F, D), BF))
    dw1 = mm(hn, dhid, ta=True, name=f"mlp{tag}_dw1", out=(w1.shape, BF))
    return dhid, dw1, dw2.reshape(N_DEV, F // N_DEV, D)


def _reduce_begin(grads, tag):
    return sibling_start(grads, name=f"reduce_sibling_start_{tag}")


def _reduce_continue(sib, after, tag, index):
    grads, a_bufs = sibling_wait(sib, after, name=f"reduce_sibling_wait_{tag}")
    pairs = [run_job(pair_job(g, a), index=index, name=f"pair_sum_{tag}{t}")[0]
             for t, (g, a) in enumerate(zip(grads, a_bufs))]
    return grads, a_bufs, chips_start(pairs, name=f"reduce_chips_start_{tag}")


def kernel(x, positions, e_norm_mix, e_w_in, e_q_norm, e_w_uq, e_kv_norm, e_w_ukv, e_v_norm, e_sgu_w, e_sgu_b, e_mla_out_norm, e_sgu_out_norm, e_w_out, o_norm_mix, o_w_in, o_conv_w, o_w_out, mlp_norm, mlp_w1, mlp_w2, final_norm, loss_target, m_e_norm_mix, m_e_w_in, m_e_q_norm, m_e_w_uq, m_e_kv_norm, m_e_w_ukv, m_e_v_norm, m_e_sgu_w, m_e_sgu_b, m_e_mla_out_norm, m_e_sgu_out_norm, m_e_w_out, m_o_norm_mix, m_o_w_in, m_o_conv_w, m_o_w_out, m_mlp_norm, m_mlp_w1, m_mlp_w2, m_final_norm, v_e_norm_mix, v_e_w_in, v_e_q_norm, v_e_w_uq, v_e_kv_norm, v_e_w_ukv, v_e_v_norm, v_e_sgu_w, v_e_sgu_b, v_e_mla_out_norm, v_e_sgu_out_norm, v_e_w_out, v_o_norm_mix, v_o_w_in, v_o_conv_w, v_o_w_out, v_mlp_norm, v_mlp_w1, v_mlp_w2, v_final_norm):
    T, D = x.shape[1], x.shape[2]
    d_shard = o_norm_mix.shape[1]
    x0 = x[0]
    target = loss_target[0]
    me = 4 * lax.axis_index("x") + 2 * lax.axis_index("y") + lax.axis_index("c")

    bf = lambda s: s.astype(BF)
    gather_groups = [[bf(jnp.transpose(e_w_in[0])), bf(e_w_uq[0]), bf(e_w_ukv[0])], [bf(e_w_out[0]), bf(mlp_w1[0])],
                     [bf(mlp_w2[0]), bf(o_w_in[0])], [bf(o_w_out[0]), bf(mlp_w1[1])], [bf(mlp_w2[1])]]
    small_rows = jnp.concatenate([o_norm_mix, o_conv_w[0], jnp.zeros((4, d_shard), F32)], axis=0)
    small_flat = all_gather_vmem(small_rows, name="gather_small")
    started, start_token = gather_start(gather_groups[:1], small_flat, name="gather_start0")
    started_rest, rest_token = gather_start(gather_groups[1:], start_token, name="gather_start1")
    started += started_rest

    def gathered(gi, after):
        srcs, lands = gather_wait(started[gi], after, name=f"gather_wait{gi}")
        return gather_finish(srcs, lands, name=f"gather_finish{gi}")

    small_g = small_flat.reshape(N_DEV, 8, d_shard)
    o_norm_full = small_g[:, 0, :].reshape(1, D)
    conv_w_full = jnp.transpose(small_g[:, 1:4, :], (1, 0, 2)).reshape(3, D)
    w_tril = jnp.tril(e_sgu_w[0])
    w_tril_b = w_tril.astype(BF)
    w_tril_tb = jnp.swapaxes(w_tril, 1, 2).astype(BF)
    b_full = jnp.repeat(e_sgu_b[0].T, CH, axis=1)
    v_gain = e_v_norm[0].reshape(1, SGU_OUT)
    cos_t, sin_t = _rope_tables(positions[0])
    mlp_gain = [mlp_norm[0:1], mlp_norm[1:2]]
    final_gain = final_norm.reshape(1, D)

    h0 = rms_fwd(x0, e_norm_mix, name="e_norm", deps=[rest_token])
    g_w_in_t, g_w_uq, w_ukv = gathered(
        0, [h0, cos_t, sin_t, w_tril_b, w_tril_tb, b_full, o_norm_full, conv_w_full])
    w_in_t = _pack_w_in_t(g_w_in_t)
    w_uq = jnp.concatenate([g_w_uq[..., :QK_NOPE], _rope_slab(g_w_uq[..., QK_NOPE:])], axis=-1)
    proj = mm(h0, w_in_t, tb=True, name="e_in", out=((T, w_in_t.shape[0]), F32), bn=_tile(w_in_t.shape[0], 640))
    qn, kvn, krope = mla_prep(proj, e_q_norm, e_kv_norm, cos_t, sin_t, name="mla_prep")
    bm = _tile(T, MM_TILE)

    def q_epi(acc, cos_ref, sin_ref):
        return (jnp.concatenate([acc[:, :QK_NOPE], _rope_fwd(acc[:, QK_NOPE:], cos_ref[...], sin_ref[...])], axis=-1),)

    q = mm(qn, w_uq, name="mla_q", out=((T, HEADS * HEAD_PAD), BF), bm=bm, bn=HEAD_PAD, epi=q_epi,
           epi_ins=[(cos_t, (bm, LANES), lambda i, j, k: (i, 0)), (sin_t, (bm, LANES), lambda i, j, k: (i, 0))])

    def kv_epi(acc, kr_ref):
        return jnp.concatenate([acc[:, :QK_NOPE].astype(BF), kr_ref[...]], axis=-1), acc[:, QK_NOPE:]

    k, v = mm(kvn, w_ukv, name="mla_kv", bm=bm, bn=HEAD_PAD, epi=kv_epi,
              outs=[((T, HEADS * HEAD_PAD), BF, HEAD_PAD), ((T, MLA_OUT), BF, V_HEAD)],
              epi_ins=[(krope, (bm, LANES), lambda i, j, k: (i, 0))])
    attn, attn_lse = attn_fwd(q, k, v, name="attn_fwd")
    mixed = mix_fwd(attn, proj, e_mla_out_norm, e_sgu_out_norm, v_gain, w_tril_b, b_full, name="mix_fwd")
    bn = _tile(D, MM_TILE)
    g_w_out_e, w1_0 = gathered(1, [mixed])
    w_out_e = g_w_out_e.reshape(-1, D)
    x1 = mm(mixed, w_out_e, name="e_out", out=((T, D), F32), bm=bm, bn=bn,
            epi=lambda acc, r: (acc + r[...],), epi_ins=[(x0, (bm, bn), lambda i, j, k: (i, j))])
    hn0, a0, act0 = _mlp_up(x1, mlp_gain[0], w1_0, 0)
    g_w2_0, g_w_in_o = gathered(2, [act0])
    w2_0 = g_w2_0.reshape(-1, D)
    x2 = _mlp_down(x1, act0, w2_0, 0)
    ho = rms_fwd(x2, o_norm_full, name="o_norm")
    proj_o = mm(ho, g_w_in_o, name="o_in", out=((T, 3 * D), F32))
    gated = conv_fwd(proj_o, conv_w_full, name="conv_fwd")
    g_w_out_o, w1_1 = gathered(3, [gated])
    w_out_o = g_w_out_o.reshape(-1, D)
    x3 = mm(gated, w_out_o, name="o_out", out=((T, D), F32), bm=bm, bn=bn,
            epi=lambda acc, r: (acc + r[...],), epi_ins=[(x2, (bm, bn), lambda i, j, k: (i, j))])
    hn1, a1, act1 = _mlp_up(x3, mlp_gain[1], w1_1, 1)
    (g_w2_1,) = gathered(4, [act1])
    w2_1 = g_w2_1.reshape(-1, D)
    x4 = _mlp_down(x3, act1, w2_1, 1)
    w1, w2 = [w1_0, w1_1], [w2_0, w2_1]

    dx4, dx4b, d_final, loss_part = loss_bwd([x4], final_gain, target, name="loss_bwd")

    hosted = dict(job_index=device_index())
    dhid1, dw1_1, dw2_1 = _mlp_bwd_weights(w1[1], w2[1], (hn1, a1, act1), dx4b, 1)
    sib_r0 = _reduce_begin([dw1_1, dw2_1], "r0")
    dhn1 = mm(dhid1, w1[1], tb=True, name="mlp1_dhn", out=((T, D), F32), deps=[sib_r0[-1]])
    grads_r0, a_r0 = sibling_wait(sib_r0, dhn1, name="reduce_sibling_wait_r0")
    dx3, dx3b, d_mlp1 = rms_bwd(x3, mlp_gain[1], dhn1, dres=dx4, name="mlp1_norm_bwd")

    dgated, ((pair_r0a,),) = mm(dx3b, w_out_o, tb=True, name="o_out_dx", out=((T, D), F32),
                                jobs=[pair_job(grads_r0[0], a_r0[0])], **hosted)
    dw_out_o, ((pair_r0b,),) = mm(gated, dx3b, ta=True, name="o_out_dw", out=((D, D), BF),
                                  jobs=[pair_job(grads_r0[1], a_r0[1])], **hosted)
    st_r0 = chips_start([pair_r0a, pair_r0b], name="reduce_chips_start_r0")
    dproj_o, dconv_full = conv_bwd(dgated, proj_o, conv_w_full, name="conv_bwd", deps=[st_r0[-1]])
    dw_in_o = mm(ho, dproj_o, ta=True, name="o_in_dw", out=(g_w_in_o.shape, BF))
    sib_r1 = _reduce_begin([dw_out_o.reshape(g_w_out_o.shape), dw_in_o], "r1")
    dho = mm(dproj_o, g_w_in_o, tb=True, name="o_in_dx", out=((T, D), F32), deps=[sib_r1[-1]])
    grads_r1, a_r1 = sibling_wait(sib_r1, dho, name="reduce_sibling_wait_r1")
    dx2, dx2b, d_onorm_full = rms_bwd(x2, o_norm_full, dho, dres=dx3, name="o_norm_bwd")

    d_ff = a0.shape[1]
    bm_h, bn_h = _tile(T, MM_TILE), _tile(d_ff, min(MM_TILE, w1[0].shape[2]))
    dhid0, ((pair_r1a,), (pair_r1b,)) = mm(
        dx2b, w2[0], tb=True, name="mlp0_dhid", out=((T, d_ff), BF), bm=bm_h, bn=bn_h,
        epi=lambda acc, a_ref: (2.0 * a_ref[...].astype(F32) * acc,),
        epi_ins=[(a0, (bm_h, bn_h), lambda i, j, k: (i, j))],
        jobs=[pair_job(grads_r1[0], a_r1[0]), pair_job(grads_r1[1], a_r1[1])], **hosted)
    st_r1 = chips_start([pair_r1a, pair_r1b], name="reduce_chips_start_r1")
    dw2_0 = mm(act0, dx2b, ta=True, name="mlp0_dw2", out=((d_ff, D), BF), deps=[st_r1[-1]])
    b_r0 = chips_wait(st_r0, dw2_0, name="reduce_chips_wait_r0")
    dw1_0, (r_w1, r_w2) = mm(
        hn0, dhid0, ta=True, name="mlp0_dw1", out=(w1[0].shape, BF),
        jobs=[adam_job(grads_r0[0], a_r0[0], b_r0[0], mlp_w1, m_mlp_w1, v_mlp_w1, 1, None),
              adam_job(grads_r0[1], a_r0[1], b_r0[1], mlp_w2, m_mlp_w2, v_mlp_w2, 1, None)], **hosted)
    sib_r2 = _reduce_begin([dw1_0, dw2_0.reshape(N_DEV, d_ff // N_DEV, D)], "r2")
    dhn0 = mm(dhid0, w1[0], tb=True, name="mlp0_dhn", out=((T, D), F32), deps=[sib_r2[-1]])
    grads_r2, a_r2 = sibling_wait(sib_r2, dhn0, name="reduce_sibling_wait_r2")
    dx1, dx1b, d_mlp0 = rms_bwd(x1, mlp_gain[0], dhn0, dres=dx2, name="mlp0_norm_bwd")

    dmixed, ((pair_r2a,),) = mm(dx1b, w_out_e, tb=True, name="e_out_dx", out=((T, MLA_OUT + SGU_OUT), F32),
                                jobs=[pair_job(grads_r2[0], a_r2[0])], **hosted)
    dw_out_e, ((pair_r2b,),) = mm(mixed, dx1b, ta=True, name="e_out_dw", out=(w_out_e.shape, BF),
                                  jobs=[pair_job(grads_r2[1], a_r2[1])], **hosted)
    st_r2 = chips_start([pair_r2a, pair_r2b], name="reduce_chips_start_r2")
    (dattn, dproj, d_mla_out, d_sgu_out, d_vgain, d_sgu_w, d_b_full) = mix_bwd(
        dmixed, attn, proj, e_mla_out_norm, e_sgu_out_norm, v_gain, w_tril_b, w_tril_tb, b_full, name="mix_bwd",
        deps=[st_r2[-1]])
    b_r1 = chips_wait(st_r1, dattn, name="reduce_chips_wait_r1")
    dq, dk, dv = attn_bwd(q, k, v, attn, attn_lse, dattn, name="attn_bwd")
    dq_lin, dkv_lin, dproj = mla_bwd_prep(dq, dk, dv, cos_t, sin_t, dproj, name="mla_bwd_prep")
    dw_uq_pad = mm(qn, dq_lin, ta=True, name="mla_q_dw", out=(w_uq.shape, BF))
    dw_ukv = mm(kvn, dkv_lin, ta=True, name="mla_kv_dw", out=(w_ukv.shape, BF))
    dw_uq = jnp.concatenate([dw_uq_pad[..., :QK_NOPE], _rope_unslab(dw_uq_pad[..., QK_NOPE:])], axis=-1)
    sib_r2b = _reduce_begin([dw_out_e.reshape(g_w_out_e.shape), dw_uq, dw_ukv], "r2b")
    dqn = mm(dq_lin, w_uq, tb=True, name="mla_q_dx", out=((T, Q_LORA), F32), deps=[sib_r2b[-1]])
    dkvn = mm(dkv_lin, w_ukv, tb=True, name="mla_kv_dx", out=((T, KV_LORA), F32), deps=[sib_r2b[-1]])
    grads_r2b, a_r2b, st_r2b = _reduce_continue(sib_r2b, dkvn, "r2b", hosted["job_index"])
    dproj, d_qnorm = rms_bwd(proj, e_q_norm, dqn, col_block=0, want_f32=False, into=dproj, name="q_norm_bwd",
                             deps=[st_r2b[-1]])
    dproj, d_kvnorm = rms_bwd(proj, e_kv_norm, dkvn, col_block=1, want_f32=False, into=dproj, name="kv_norm_bwd")
    dw_in_t_pad, (r_w_out_o, r_w_in_o) = mm(
        dproj, h0, ta=True, name="e_in_dw", out=(w_in_t.shape, BF), bm=_tile(w_in_t.shape[0], 640),
        jobs=[adam_job(grads_r1[0], a_r1[0], b_r1[0], o_w_out, m_o_w_out, v_o_w_out, 0, None),
              adam_job(grads_r1[1], a_r1[1], b_r1[1], o_w_in, m_o_w_in, v_o_w_in, 0, None)], **hosted)
    dw_in_t = unpack_w_in_t_grad(dw_in_t_pad, name="e_in_dw_unpack")
    sib_r3 = _reduce_begin([dw_in_t], "r3")
    dh0 = mm(dproj, w_in_t, name="e_in_dx", out=((T, D), F32), deps=[sib_r3[-1]])
    grads_r3, a_r3, st_r3 = _reduce_continue(sib_r3, dh0, "r3", hosted["job_index"])
    tok_r3 = st_r3[-1]
    grad_x, d_enorm = rms_bwd(x0, e_norm_mix, dh0, dres=dx1, want_bf=False, name="e_norm_bwd", deps=[tok_r3])
    b_r2 = chips_wait(st_r2, grad_x, name="reduce_chips_wait_r2")

    def finish(grads, a_bufs, b_bufs, t, w, m, v, layer=0, prev=None, tag="", deps=()):
        return run_job(adam_job(grads[t], a_bufs[t], b_bufs[t], w, m, v, layer, prev), index=hosted["job_index"],
                       name=f"adam_{tag}", deps=deps)

    r_w1 = finish(grads_r2, a_r2, b_r2, 0, mlp_w1, m_mlp_w1, v_mlp_w1, 0, r_w1, tag="w1_l0", deps=[tok_r3])
    r_w2 = finish(grads_r2, a_r2, b_r2, 1, mlp_w2, m_mlp_w2, v_mlp_w2, 0, r_w2, tag="w2_l0", deps=[r_w1[1]])
    b_r2b = chips_wait(st_r2b, r_w2[1], name="reduce_chips_wait_r2b")
    r_w_out_e = finish(grads_r2b, a_r2b, b_r2b, 0, e_w_out, m_e_w_out, v_e_w_out, tag="e_w_out")
    r_w_uq = finish(grads_r2b, a_r2b, b_r2b, 1, e_w_uq, m_e_w_uq, v_e_w_uq, tag="e_w_uq")
    r_w_ukv = finish(grads_r2b, a_r2b, b_r2b, 2, e_w_ukv, m_e_w_ukv, v_e_w_ukv, tag="e_w_ukv")
    b_r3 = chips_wait(st_r3, r_w_out_e[1], name="reduce_chips_wait_r3")
    g_w_in_t = reduce_sum(grads_r3[0], a_r3[0], b_r3[0], name="sum_e_w_in")
    w_in_upd_t = adam_rows(g_w_in_t, jnp.transpose(e_w_in[0]), jnp.transpose(m_e_w_in[0]), jnp.transpose(v_e_w_in[0]),
                           name="adam_e_w_in")
    r_w_in = [jnp.transpose(t)[None] for t in (g_w_in_t, *w_in_upd_t)]

    d_sgu_b = jnp.transpose(d_b_full[:, ::CH])
    d_sgu_w_tril = jnp.tril(d_sgu_w)
    rep = [("e_norm_mix", e_norm_mix, m_e_norm_mix, v_e_norm_mix, d_enorm),
           ("e_q_norm", e_q_norm, m_e_q_norm, v_e_q_norm, d_qnorm),
           ("e_kv_norm", e_kv_norm, m_e_kv_norm, v_e_kv_norm, d_kvnorm),
           ("e_v_norm", e_v_norm, m_e_v_norm, v_e_v_norm, d_vgain),
           ("e_sgu_w", e_sgu_w, m_e_sgu_w, v_e_sgu_w, d_sgu_w_tril),
           ("e_sgu_b", e_sgu_b, m_e_sgu_b, v_e_sgu_b, d_sgu_b),
           ("e_mla_out_norm", e_mla_out_norm, m_e_mla_out_norm, v_e_mla_out_norm, d_mla_out),
           ("e_sgu_out_norm", e_sgu_out_norm, m_e_sgu_out_norm, v_e_sgu_out_norm, d_sgu_out),
           ("mlp_norm", mlp_norm, m_mlp_norm, v_mlp_norm, jnp.concatenate([d_mlp0, d_mlp1], axis=0)),
           ("final_norm", final_norm, m_final_norm, v_final_norm, d_final)]
    sizes = [int(np.prod(r[1].shape)) for r in rep]
    n_rep = sum(sizes)
    n_all = n_rep + 4 * D + 1
    width = -(-n_all // (8 * LANES)) * LANES
    pad = 8 * width - n_all
    flat = jnp.concatenate([r[4].reshape(-1) for r in rep]
                           + [d_onorm_full.reshape(-1), dconv_full.reshape(-1), loss_part[0, :1],
                              jnp.zeros((pad,), F32)])
    summed = sum_rows8(all_gather_vmem(flat.reshape(8, width), name="gather_small_grads", after=b_r3[0]), 8,
                       name="sum_small_grads").reshape(-1)

    loss = summed[n_rep + 4 * D]

    def pack_rep(i):
        return jnp.concatenate([r[i].reshape(-1) for r in rep]).reshape(n_rep // LANES, LANES)

    g_rep = summed[:n_rep].reshape(n_rep // LANES, LANES)
    d_rep, nm_rep, nv_rep = adam_flat(g_rep, pack_rep(1), pack_rep(2), pack_rep(3), name="adam_replicated")

    def unpack_rep(flat2d):
        out, off = {}, 0
        f = flat2d.reshape(-1)
        for r, n in zip(rep, sizes):
            out[r[0]] = f[off:off + n].reshape(r[1].shape)
            off += n
        return out

    small = {"grad": unpack_rep(g_rep), "delta": unpack_rep(d_rep), "new_m": unpack_rep(nm_rep),
             "new_v": unpack_rep(nv_rep)}
    g_onorm = lax.dynamic_slice(summed[n_rep:n_rep + D].reshape(1, D), (0, me * d_shard), (1, d_shard))
    g_conv = lax.dynamic_slice(summed[n_rep + D:n_rep + 4 * D].reshape(3, D), (0, me * d_shard), (3, d_shard))

    def pack_sharded(norm_part, conv_part):
        return jnp.concatenate([norm_part, conv_part, jnp.zeros((4, d_shard), F32)], axis=0)

    g_sh = pack_sharded(g_onorm, g_conv)
    d_sh, nm_sh, nv_sh = adam_flat(g_sh, pack_sharded(o_norm_mix, o_conv_w[0]), pack_sharded(m_o_norm_mix, m_o_conv_w[0]),
                                   pack_sharded(v_o_norm_mix, v_o_conv_w[0]), name="adam_sharded_small")
    for kind, arr in (("grad", g_sh), ("delta", d_sh), ("new_m", nm_sh), ("new_v", nv_sh)):
        small[kind]["o_norm_mix"] = arr[0:1]
        small[kind]["o_conv_w"] = arr[1:4][None]

    big = {"e_w_in": r_w_in, "e_w_uq": r_w_uq, "e_w_ukv": r_w_ukv, "e_w_out": r_w_out_e, "o_w_in": r_w_in_o,
           "o_w_out": r_w_out_o, "mlp_w1": r_w1, "mlp_w2": r_w2}
    order = ["e_norm_mix", "e_w_in", "e_q_norm", "e_w_uq", "e_kv_norm", "e_w_ukv", "e_v_norm", "e_sgu_w", "e_sgu_b",
             "e_mla_out_norm", "e_sgu_out_norm", "e_w_out", "o_norm_mix", "o_w_in", "o_conv_w", "o_w_out", "mlp_norm",
             "mlp_w1", "mlp_w2", "final_norm"]
    result = [loss, grad_x[None]]
    for ki, kind in enumerate(("grad", "delta", "new_m", "new_v")):
        for nm in order:
            result.append(big[nm][ki] if nm in big else small[kind][nm])
    return tuple(result)
```

```python
import numpy as np
import jax
import jax.numpy as jnp
from jax import lax
from jax.experimental import pallas as pl
from jax.experimental.pallas import tpu as pltpu

BF = jnp.bfloat16
F32 = jnp.float32
MESH = pl.DeviceIdType.MESH
N_DEV = 8

EPS = 1e-6
HEADS = 8
Q_LORA = 512
KV_LORA = 512
QK_NOPE = 128
QK_ROPE = 64
HALF_ROPE = QK_ROPE // 2
V_HEAD = 128
HEAD_PAD = 256
ROPE_BASE = 10000.0
GROUPS = 8
CH = 128
CHUNK = 128
SGU_OUT = GROUPS * CH
MLA_OUT = HEADS * V_HEAD
ATTN_SCALE = float((QK_NOPE + QK_ROPE) ** -0.5)

ADAM_LR = 0.001
ADAM_B1 = 0.9
ADAM_B2 = 0.999
ADAM_EPS = 1e-08
ADAM_WD = 0.01
ADAM_STEP = 10
ADAM_C1 = 1.0 - ADAM_B1 ** ADAM_STEP
ADAM_C2 = 1.0 - ADAM_B2 ** ADAM_STEP

V7X_VMEM_BYTES = 64 * 2 ** 20
VMEM_LIMIT_CAP = V7X_VMEM_BYTES - 6 * 2 ** 20
LANES = 128
ROW_TILE = 256
ATTN_TILE = 512
STREAM_BLOCK_ELEMS = 512 * 1024
MM_TILE = 1024
MM_K_TILE = 2048
MM_K_BLOCK_MAX = 3072


def _padded_bytes(block, dtype):
    dims = [d for d in block if d is not None]
    if len(dims) >= 1:
        dims[-1] = -(-dims[-1] // LANES) * LANES
    if len(dims) >= 2:
        dims[-2] = -(-dims[-2] // 16) * 16
    return int(np.prod(dims)) * jnp.dtype(dtype).itemsize


def _pcall(body, *, name, grid, ins, outs, scratch=(), semantics=None, aliases=None, prefetch=None, deps=()):
    any_spec = pl.BlockSpec(memory_space=pl.ANY)
    if deps:
        n_lead = len(ins) + (1 if prefetch is not None else 0)
        n_deps = len(deps)
        inner = body

        def body(*refs):
            inner(*refs[:n_lead], *refs[n_lead + n_deps:])

        ins = list(ins) + [(d, None, None) for d in deps]
    in_specs = [any_spec if b is None else pl.BlockSpec(b, m) for _, b, m in ins]
    out_specs = [any_spec if b is None else pl.BlockSpec(b, m) for _, _, b, m in outs]
    out_shape = [pltpu.HBM(s, d) for s, d, _, _ in outs]
    est = 0
    for a, b, _ in ins:
        if b is not None:
            est += 2 * _padded_bytes(b, a.dtype)
    for _, d, b, _ in outs:
        if b is not None:
            est += 2 * _padded_bytes(b, d)
    for s in scratch:
        if hasattr(s, "shape") and hasattr(s, "dtype"):
            est += _padded_bytes(s.shape, s.dtype)
    limit = int(min(VMEM_LIMIT_CAP, est + 16 * 2 ** 20))
    params = pltpu.CompilerParams(
        dimension_semantics=semantics or ("arbitrary",) * len(grid), vmem_limit_bytes=limit)
    args = [pltpu.with_memory_space_constraint(a, pltpu.HBM) for a, _, _ in ins]
    if prefetch is not None:
        grid_spec = pltpu.PrefetchScalarGridSpec(
            num_scalar_prefetch=1, grid=grid, in_specs=in_specs, out_specs=out_specs, scratch_shapes=list(scratch))
        call = pl.pallas_call(body, out_shape=out_shape, grid_spec=grid_spec, name=name, compiler_params=params,
                              input_output_aliases=aliases or {})
        return call(prefetch, *args)
    call = pl.pallas_call(body, out_shape=out_shape, grid=grid, in_specs=in_specs, out_specs=out_specs,
                          scratch_shapes=list(scratch), name=name, compiler_params=params,
                          input_output_aliases=aliases or {})
    return call(*args)


def _tile(dim, pref, quantum=LANES):
    if dim <= pref:
        return dim
    t = (pref // quantum) * quantum
    while t >= quantum:
        if dim % t == 0:
            return t
        t -= quantum
    return dim


def _vshape(arr_shape):
    if len(arr_shape) == 2:
        return tuple(arr_shape)
    s, r, c = arr_shape
    return (r, s * c)


def _vblock(arr_shape, br, bc, rc):
    if len(arr_shape) == 2:
        return (br, bc), (lambda *g: rc(*g))
    _, _, c = arr_shape
    assert c % bc == 0, (arr_shape, bc)
    per = c // bc

    def imap(*g):
        ri, ci = rc(*g)
        return (ci // per, ri, ci % per)

    return (None, br, bc), imap


def _shard_width(*shapes):
    w = None
    for s in shapes:
        if len(s) == 3:
            w = s[2] if w is None else int(np.gcd(w, s[2]))
    return w


def mm(a, b, *, name, ta=False, tb=False, out=None, outs=None, epi=None, epi_ins=(), bm=None, bn=None, bk=None,
       deps=(), jobs=(), job_index=None):
    av, bv = _vshape(a.shape), _vshape(b.shape)
    M, K = (av[1], av[0]) if ta else av
    K2, N = (bv[1], bv[0]) if tb else bv
    assert K == K2, (a.shape, b.shape, ta, tb)
    if outs is None:
        outs = [(out[0], out[1], None)]
    a_sw = _shard_width(a.shape)
    b_sw = _shard_width(b.shape)
    o_sw = _shard_width(*[o[0] for o in outs])
    m_lim = a_sw if (ta and a_sw) else None
    k_lim = [w for w in ((a_sw if not ta else None), (b_sw if tb else None)) if w]
    n_lim = [w for w in ((b_sw if not tb else None), o_sw) if w]
    if bm is None:
        bm = _tile(M, min([MM_TILE] + ([m_lim] if m_lim else [])))
    if bn is None:
        bn = _tile(N, min([MM_TILE] + n_lim))
    k_shards = 0
    if tb and len(b.shape) == 3 and bk is None and not (a_sw and not ta):
        k_shards = 1
        while 2 * k_shards <= b.shape[0] and 2 * k_shards * b_sw <= MM_K_BLOCK_MAX:
            k_shards *= 2
        bk = k_shards * b_sw
    if bk is None:
        bk = K if (K <= 4096 and not k_lim) else _tile(K, min([MM_K_TILE] + k_lim))
    assert M % bm == 0 and N % bn == 0 and K % bk == 0, (name, M, N, K, bm, bn, bk)
    nk = K // bk
    grid = (M // bm, N // bn, nk)
    if ta:
        a_blk, a_map = _vblock(a.shape, bk, bm, lambda i, j, k: (k, i))
    else:
        a_blk, a_map = _vblock(a.shape, bm, bk, lambda i, j, k: (i, k))
    if k_shards:
        b_blk, b_map = (k_shards, bn, b_sw), (lambda i, j, k: (k, j, 0))
    elif tb:
        b_blk, b_map = _vblock(b.shape, bn, bk, lambda i, j, k: (j, k))
    else:
        b_blk, b_map = _vblock(b.shape, bk, bn, lambda i, j, k: (k, j))
    dn = (((0 if ta else 1,), (1 if tb else 0,)), ((), ()))
    ins = [(a, a_blk, a_map), (b, b_blk, b_map)] + list(epi_ins)
    out_list = []
    for shape, dtype, cols in outs:
        cols = cols or bn
        blk, imap = _vblock(shape, bm, cols, lambda i, j, k: (i, j))
        out_list.append((shape, dtype, blk, imap))
    n_e, n_o = len(epi_ins), len(out_list)

    n_steps = grid[0] * grid[1] * nk
    built = [job(n_steps) for job in jobs]
    aliases = {}
    job_slices = []
    if built:
        def lin(i, j, k):
            return (i * grid[1] + j) * nk + k

        ins = [(arr, blk, None if blk is None else (lambda i, j, k, s, f=f: f(i, j, k))) for arr, blk, f in ins]
        out_list = [(sh, dt, blk, (lambda i, j, k, s, f=f: f(i, j, k))) for sh, dt, blk, f in out_list]
        n_main_in, n_main_out = len(ins), len(out_list)
        for jb in built:
            i0, o0 = len(ins), len(out_list)
            ins += [(arr, blk, None if blk is None else (lambda i, j, k, s, f=f: f(lin(i, j, k), s)))
                    for arr, blk, f in jb["ins"]]
            out_list += [(sh, dt, blk, (lambda i, j, k, s, f=f: f(lin(i, j, k), s))) for sh, dt, blk, f in jb["outs"]]
            aliases.update({1 + i0 + ai: o0 + ao for ai, ao in jb["aliases"].items()})
            job_slices.append((i0, len(jb["ins"]), o0, len(jb["outs"])))
    n_in_total = len(ins)

    def body(*refs):
        if built:
            refs = refs[1:]
        a_ref, b_ref = refs[0], refs[1]
        e_refs = refs[2:2 + n_e]
        o_refs = refs[n_in_total:n_in_total + n_o]
        for jb, (i0, ni, o0, no) in zip(built, job_slices):
            jb["fn"](refs[i0:i0 + ni], refs[n_in_total + o0:n_in_total + o0 + no])

        def finish(acc):
            res = epi(acc, *e_refs) if epi is not None else (acc,)
            for o_ref, r in zip(o_refs, res):
                o_ref[...] = r.astype(o_ref.dtype)

        x = a_ref[...].astype(BF)
        y = b_ref[...].astype(BF)
        if k_shards:
            p = None
            for s in range(k_shards):
                part = lax.dot_general(x[:, s * b_sw:(s + 1) * b_sw], y[s], dn, preferred_element_type=F32)
                p = part if p is None else p + part
        else:
            p = lax.dot_general(x, y, dn, preferred_element_type=F32)
        if nk == 1:
            finish(p)
        else:
            acc_ref = refs[-1]
            k = pl.program_id(2)

            @pl.when(k == 0)
            def _():
                acc_ref[...] = p

            @pl.when(k > 0)
            def _():
                acc_ref[...] += p

            @pl.when(k == nk - 1)
            def _():
                finish(acc_ref[...])

    scratch = [pltpu.VMEM((bm, bn), F32)] if nk > 1 else []
    res = _pcall(body, name=name, grid=grid, ins=ins, outs=out_list, scratch=scratch, deps=deps,
                 semantics=("parallel", "parallel", "arbitrary"), prefetch=job_index if built else None, aliases=aliases)
    main = res[0] if n_o == 1 else res[:n_o]
    if not built:
        return main
    return main, [res[o0:o0 + no] for _, _, o0, no in job_slices]


_GELU_K = float(np.sqrt(2.0 / np.pi))
_GELU_C = 0.044715


def _gelu(x):
    t = jnp.tanh(_GELU_K * (x + _GELU_C * (x * x * x)))
    return 0.5 * x * (1.0 + t)


def _gelu_grad(x):
    t = jnp.tanh(_GELU_K * (x + _GELU_C * (x * x * x)))
    return 0.5 * (1.0 + t) + 0.5 * x * (1.0 - t * t) * (_GELU_K * (1.0 + 3.0 * _GELU_C * (x * x)))


def _rstd(x):
    return lax.rsqrt(jnp.mean(x * x, axis=-1, keepdims=True) + EPS)


def _rms_bwd(x, gain, dy):
    r = _rstd(x)
    xh = x * r
    gdy = dy * gain
    dx = r * (gdy - xh * jnp.mean(gdy * xh, axis=-1, keepdims=True))
    return dx, dy * xh


def _rope_fwd(x, cos_t, sin_t):
    return x * cos_t + pltpu.roll(x, 2 * HALF_ROPE, 1) * sin_t


def _rope_bwd(dy, cos_t, sin_t):
    return dy * cos_t + pltpu.roll(dy * sin_t, 2 * HALF_ROPE, 1)


def _acc_rows(ref, val, first):
    s = jnp.sum(val, axis=0, keepdims=True)

    @pl.when(first)
    def _():
        ref[...] = s

    @pl.when(jnp.logical_not(first))
    def _():
        ref[...] += s


def rms_fwd(x, gain, *, name, col_block=0, width=None, deps=()):
    T = x.shape[0]
    width = width or x.shape[1]
    tm = _tile(T, ROW_TILE, 8)

    def body(x_ref, g_ref, o_ref):
        v = x_ref[...]
        o_ref[...] = (v * _rstd(v) * g_ref[...]).astype(BF)

    return _pcall(body, name=name, grid=(T // tm,),
                  ins=[(x, (tm, width), lambda i: (i, col_block)), (gain, (1, width), lambda i: (0, 0))],
                  outs=[((T, width), BF, (tm, width), lambda i: (i, 0))], semantics=("parallel",), deps=deps)[0]


def rms_bwd(x, gain, dy, *, name, col_block=0, dres=None, want_f32=True, want_bf=True, into=None, deps=()):
    T, width = dy.shape
    tm = _tile(T, ROW_TILE, 8)
    has_res = dres is not None

    def body(*refs):
        x_ref, g_ref, dy_ref = refs[:3]
        pos = 3
        res_ref = None
        if has_res:
            res_ref = refs[pos]
            pos += 1
        if into is not None:
            pos += 1
        outs = refs[pos:]
        dx, dg_rows = _rms_bwd(x_ref[...], g_ref[...], dy_ref[...])
        if has_res:
            dx = dx + res_ref[...]
        o = 0
        if want_f32:
            outs[o][...] = dx
            o += 1
        if want_bf:
            outs[o][...] = dx.astype(BF)
            o += 1
        _acc_rows(outs[o], dg_rows, pl.program_id(0) == 0)

    ins = [(x, (tm, width), lambda i: (i, col_block)), (gain, (1, width), lambda i: (0, 0)),
           (dy, (tm, width), lambda i: (i, 0))]
    if has_res:
        ins.append((dres, (tm, width), lambda i: (i, 0)))
    outs = []
    aliases = {}
    if want_f32:
        outs.append(((T, width), F32, (tm, width), lambda i: (i, 0)))
    if want_bf and into is not None:
        ins.append((into, None, None))
        aliases[len(ins) - 1] = len(outs)
        outs.append((into.shape, BF, (tm, width), lambda i: (i, col_block)))
    elif want_bf:
        outs.append(((T, width), BF, (tm, width), lambda i: (i, 0)))
    outs.append(((1, width), F32, (1, width), lambda i: (0, 0)))
    return _pcall(body, name=name, grid=(T // tm,), ins=ins, outs=outs, aliases=aliases, deps=deps)


def mla_prep(proj, q_norm, kv_norm, cos_t, sin_t, *, name):
    T = proj.shape[0]
    tm = _tile(T, ROW_TILE, 8)
    kr_block = (proj.shape[1] - LANES) // LANES

    def body(cq_ref, ckv_ref, kr_ref, qg_ref, kg_ref, cos_ref, sin_ref, qn_ref, kvn_ref, krope_ref):
        cq = cq_ref[...]
        qn_ref[...] = (cq * _rstd(cq) * qg_ref[...]).astype(BF)
        ckv = ckv_ref[...]
        kvn_ref[...] = (ckv * _rstd(ckv) * kg_ref[...]).astype(BF)
        krope_ref[...] = _rope_fwd(kr_ref[...], cos_ref[...], sin_ref[...]).astype(BF)

    return _pcall(
        body, name=name, grid=(T // tm,),
        ins=[(proj, (tm, Q_LORA), lambda i: (i, 0)), (proj, (tm, KV_LORA), lambda i: (i, 1)),
             (proj, (tm, LANES), lambda i: (i, kr_block)),
             (q_norm, (1, Q_LORA), lambda i: (0, 0)), (kv_norm, (1, KV_LORA), lambda i: (0, 0)),
             (cos_t, (tm, LANES), lambda i: (i, 0)), (sin_t, (tm, LANES), lambda i: (i, 0))],
        outs=[((T, Q_LORA), BF, (tm, Q_LORA), lambda i: (i, 0)), ((T, KV_LORA), BF, (tm, KV_LORA), lambda i: (i, 0)),
              ((T, LANES), BF, (tm, LANES), lambda i: (i, 0))],
        semantics=("parallel",))


def _attn_scores(q, k_blk, diagonal):
    s = lax.dot_general(q, k_blk, (((1,), (1,)), ((), ())), preferred_element_type=F32) * ATTN_SCALE
    if diagonal:
        row = lax.broadcasted_iota(jnp.int32, s.shape, 0)
        col = lax.broadcasted_iota(jnp.int32, s.shape, 1)
        s = jnp.where(col <= row, s, -jnp.inf)
    return s


def attn_fwd(q, k, v, *, name):
    T = q.shape[0]
    tq = _tile(T, ATTN_TILE, 8)

    def body(q_ref, k_ref, v_ref, o_ref, lse_ref):
        i = pl.program_id(1)
        qv = q_ref[...]

        def block(kb, carry, diagonal):
            m, l, acc = carry
            start = pl.multiple_of(kb * tq, tq)
            s = _attn_scores(qv, k_ref[pl.ds(start, tq), :], diagonal)
            m_new = jnp.maximum(m, jnp.max(s, axis=-1, keepdims=True))
            alpha = jnp.exp(m - m_new)
            p = jnp.exp(s - m_new)
            l = alpha * l + jnp.sum(p, axis=-1, keepdims=True)
            acc = alpha * acc + jnp.dot(p.astype(BF), v_ref[pl.ds(start, tq), :], preferred_element_type=F32)
            return m_new, l, acc

        init = (jnp.full((tq, 1), -jnp.inf, F32), jnp.zeros((tq, 1), F32), jnp.zeros((tq, V_HEAD), F32))
        carry = lax.fori_loop(0, i, lambda kb, c: block(kb, c, False), init)
        m, l, acc = block(i, carry, True)
        o_ref[...] = acc / l
        lse_ref[...] = jnp.broadcast_to(m + jnp.log(l), (tq, V_HEAD))

    return _pcall(
        body, name=name, grid=(HEADS, T // tq),
        ins=[(q, (tq, HEAD_PAD), lambda h, i: (i, h)), (k, (T, HEAD_PAD), lambda h, i: (0, h)),
             (v, (T, V_HEAD), lambda h, i: (0, h))],
        outs=[((T, MLA_OUT), F32, (tq, V_HEAD), lambda h, i: (i, h)),
              ((T, MLA_OUT), F32, (tq, V_HEAD), lambda h, i: (i, h))], semantics=("parallel", "parallel"))


def attn_bwd(q, k, v, o, lse, do, *, name):
    T = q.shape[0]
    tq = _tile(T, ATTN_TILE, 8)

    def body(q_ref, k_ref, v_ref, o_ref, lse_ref, do_ref, dq_ref, dk_ref, dv_ref):
        i = pl.program_id(1)

        @pl.when(i == 0)
        def _():
            dk_ref[...] = jnp.zeros_like(dk_ref)
            dv_ref[...] = jnp.zeros_like(dv_ref)

        qv = q_ref[...]
        do_t = do_ref[...]
        lse_v = lse_ref[:, 0:1]
        delta = jnp.sum(do_t.astype(F32) * o_ref[...], axis=-1, keepdims=True)

        def block(kb, dq, diagonal):
            start = pl.multiple_of(kb * tq, tq)
            k_blk = k_ref[pl.ds(start, tq), :]
            v_blk = v_ref[pl.ds(start, tq), :]
            p = jnp.exp(_attn_scores(qv, k_blk, diagonal) - lse_v)
            dp = lax.dot_general(do_t, v_blk, (((1,), (1,)), ((), ())), preferred_element_type=F32)
            ds = (p * (dp - delta) * ATTN_SCALE).astype(BF)
            dk_ref[pl.ds(start, tq), :] += lax.dot_general(ds, qv, (((0,), (0,)), ((), ())), preferred_element_type=F32)
            dv_ref[pl.ds(start, tq), :] += lax.dot_general(p.astype(BF), do_t, (((0,), (0,)), ((), ())),
                                                          preferred_element_type=F32)
            return dq + jnp.dot(ds, k_blk, preferred_element_type=F32)

        dq = lax.fori_loop(0, i, lambda kb, c: block(kb, c, False), jnp.zeros((tq, HEAD_PAD), F32))
        dq_ref[...] = block(i, dq, True)

    return _pcall(
        body, name=name, grid=(HEADS, T // tq),
        ins=[(q, (tq, HEAD_PAD), lambda h, i: (i, h)), (k, (T, HEAD_PAD), lambda h, i: (0, h)),
             (v, (T, V_HEAD), lambda h, i: (0, h)), (o, (tq, V_HEAD), lambda h, i: (i, h)),
             (lse, (tq, V_HEAD), lambda h, i: (i, h)), (do, (tq, V_HEAD), lambda h, i: (i, h))],
        outs=[((T, HEADS * HEAD_PAD), F32, (tq, HEAD_PAD), lambda h, i: (i, h)),
              ((T, HEADS * HEAD_PAD), F32, (T, HEAD_PAD), lambda h, i: (0, h)),
              ((T, MLA_OUT), F32, (T, V_HEAD), lambda h, i: (0, h))],
        semantics=("parallel", "arbitrary"))


def mla_bwd_prep(dq, dk, dv, cos_t, sin_t, dproj, *, name):
    T = dq.shape[0]
    tm = _tile(T, ROW_TILE, 8)
    kr_block = (dproj.shape[1] - LANES) // LANES

    def body(dq_ref, dk_ref, dv_ref, cos_ref, sin_ref, dproj_in, dql_ref, dkvl_ref, dkr_ref):
        cos_v, sin_v = cos_ref[...], sin_ref[...]
        kr = jnp.zeros((tm, LANES), F32)
        for h in range(HEADS):
            lo = h * HEAD_PAD
            dql_ref[:, lo:lo + QK_NOPE] = dq_ref[:, lo:lo + QK_NOPE].astype(BF)
            dql_ref[:, lo + QK_NOPE:lo + HEAD_PAD] = _rope_bwd(
                dq_ref[:, lo + QK_NOPE:lo + HEAD_PAD], cos_v, sin_v).astype(BF)
            dkvl_ref[:, lo:lo + QK_NOPE] = dk_ref[:, lo:lo + QK_NOPE].astype(BF)
            dkvl_ref[:, lo + QK_NOPE:lo + HEAD_PAD] = dv_ref[:, h * V_HEAD:(h + 1) * V_HEAD].astype(BF)
            kr = kr + dk_ref[:, lo + QK_NOPE:lo + HEAD_PAD]
        dkr_ref[...] = _rope_bwd(kr, cos_v, sin_v).astype(BF)

    W = HEADS * HEAD_PAD
    return _pcall(
        body, name=name, grid=(T // tm,),
        ins=[(dq, (tm, W), lambda i: (i, 0)), (dk, (tm, W), lambda i: (i, 0)), (dv, (tm, MLA_OUT), lambda i: (i, 0)),
             (cos_t, (tm, LANES), lambda i: (i, 0)), (sin_t, (tm, LANES), lambda i: (i, 0)), (dproj, None, None)],
        outs=[((T, W), BF, (tm, W), lambda i: (i, 0)), ((T, W), BF, (tm, W), lambda i: (i, 0)),
              (dproj.shape, BF, (tm, LANES), lambda i: (i, kr_block))],
        aliases={5: 2}, semantics=("parallel",))


def _group_norm_stats(vg):
    mu = jnp.mean(vg, axis=-1, keepdims=True)
    d = vg - mu
    r = lax.rsqrt(jnp.mean(d * d, axis=-1, keepdims=True) + EPS)
    return d * r, r


def mix_fwd(a, proj, g_mla, g_sgu, v_gain, w_tril, b_full, *, name):
    T = a.shape[0]
    tm = _tile(T, ROW_TILE, CHUNK)
    n_chunk = tm // CHUNK

    def body(a_ref, u_ref, v_ref, gm_ref, gs_ref, vg_ref, w_ref, b_ref, o_ref, s_scr):
        av = a_ref[...]
        o_ref[:, :MLA_OUT] = (av * _rstd(av) * gm_ref[...]).astype(BF)
        for g in range(GROUPS):
            sl = slice(g * CH, (g + 1) * CH)
            vhat, _ = _group_norm_stats(_gelu(v_ref[:, sl]))
            vn = (vhat * vg_ref[:, sl]).astype(BF)
            u = _gelu(u_ref[:, sl])
            for ci in range(n_chunk):
                rs = slice(ci * CHUNK, (ci + 1) * CHUNK)
                y = jnp.dot(w_ref[g], vn[rs], preferred_element_type=F32) + b_ref[:, sl]
                s_scr[rs, sl] = u[rs] * y
        s = s_scr[...]
        o_ref[:, MLA_OUT:] = (s * _rstd(s) * gs_ref[...]).astype(BF)

    return _pcall(
        body, name=name, grid=(T // tm,),
        ins=[(a, (tm, MLA_OUT), lambda i: (i, 0)), (proj, (tm, SGU_OUT), lambda i: (i, 1)),
             (proj, (tm, SGU_OUT), lambda i: (i, 2)), (g_mla, (1, MLA_OUT), lambda i: (0, 0)),
             (g_sgu, (1, SGU_OUT), lambda i: (0, 0)), (v_gain, (1, SGU_OUT), lambda i: (0, 0)),
             (w_tril, (GROUPS, CHUNK, CHUNK), lambda i: (0, 0, 0)), (b_full, (CHUNK, SGU_OUT), lambda i: (0, 0))],
        outs=[((T, MLA_OUT + SGU_OUT), BF, (tm, MLA_OUT + SGU_OUT), lambda i: (i, 0))],
        scratch=[pltpu.VMEM((tm, SGU_OUT), F32)], semantics=("parallel",))[0]


def mix_bwd(dmixed, a, proj, g_mla, g_sgu, v_gain, w_tril, w_tril_t, b_full, *, name, deps=()):
    T = a.shape[0]
    tm = _tile(T, ROW_TILE, CHUNK)
    n_chunk = tm // CHUNK
    uv0 = Q_LORA + KV_LORA

    def body(dm_a_ref, dm_s_ref, a_ref, u_ref, v_ref, gm_ref, gs_ref, vg_ref, w_ref, wt_ref, b_ref,
             da_ref, duv_ref, dgm_ref, dgs_ref, dvg_ref, dw_ref, db_ref, s_scr, y_scr):
        first = pl.program_id(0) == 0
        duv_ref[:, :uv0] = jnp.zeros((tm, uv0), BF)
        duv_ref[:, uv0 + 2 * SGU_OUT:] = jnp.zeros((tm, duv_ref.shape[1] - uv0 - 2 * SGU_OUT), BF)
        da, dgm_rows = _rms_bwd(a_ref[...], gm_ref[...], dm_a_ref[...])
        da_ref[...] = da.astype(BF)
        _acc_rows(dgm_ref, dgm_rows, first)

        for g in range(GROUPS):
            sl = slice(g * CH, (g + 1) * CH)
            vhat, _ = _group_norm_stats(_gelu(v_ref[:, sl]))
            vn = (vhat * vg_ref[:, sl]).astype(BF)
            u = _gelu(u_ref[:, sl])
            for ci in range(n_chunk):
                rs = slice(ci * CHUNK, (ci + 1) * CHUNK)
                y = jnp.dot(w_ref[g], vn[rs], preferred_element_type=F32) + b_ref[:, sl]
                y_scr[rs, sl] = y
                s_scr[rs, sl] = u[rs] * y
        ds, dgs_rows = _rms_bwd(s_scr[...], gs_ref[...], dm_s_ref[...])
        _acc_rows(dgs_ref, dgs_rows, first)
        s_scr[...] = ds

        @pl.when(first)
        def _():
            dw_ref[...] = jnp.zeros_like(dw_ref)
            db_ref[...] = jnp.zeros_like(db_ref)

        for g in range(GROUPS):
            sl = slice(g * CH, (g + 1) * CH)
            upre = u_ref[:, sl]
            vpre = v_ref[:, sl]
            u = _gelu(upre)
            vhat, r = _group_norm_stats(_gelu(vpre))
            gain = vg_ref[:, sl]
            vn = (vhat * gain).astype(BF)
            dsg = s_scr[:, sl]
            duv_ref[:, uv0 + g * CH:uv0 + (g + 1) * CH] = (dsg * y_scr[:, sl] * _gelu_grad(upre)).astype(BF)
            dy = dsg * u
            dyb = dy.astype(BF)
            dvn_parts = []
            for ci in range(n_chunk):
                rs = slice(ci * CHUNK, (ci + 1) * CHUNK)
                dvn_parts.append(jnp.dot(wt_ref[g], dyb[rs], preferred_element_type=F32))
                dw_ref[g] += lax.dot_general(dyb[rs], vn[rs], (((1,), (1,)), ((), ())), preferred_element_type=F32)
                db_ref[:, sl] += jnp.broadcast_to(jnp.sum(dy[rs], axis=-1, keepdims=True), (CHUNK, CH))
            dvn = dvn_parts[0] if n_chunk == 1 else jnp.concatenate(dvn_parts, axis=0)
            _acc_rows(dvg_ref.at[:, sl], dvn * vhat, first)
            dvh = dvn * gain
            dvg = r * (dvh - jnp.mean(dvh, axis=-1, keepdims=True)
                       - vhat * jnp.mean(dvh * vhat, axis=-1, keepdims=True))
            duv_ref[:, uv0 + SGU_OUT + g * CH:uv0 + SGU_OUT + (g + 1) * CH] = (dvg * _gelu_grad(vpre)).astype(BF)

    return _pcall(
        body, name=name, grid=(T // tm,),
        ins=[(dmixed, (tm, MLA_OUT), lambda i: (i, 0)), (dmixed, (tm, SGU_OUT), lambda i: (i, 1)),
             (a, (tm, MLA_OUT), lambda i: (i, 0)), (proj, (tm, SGU_OUT), lambda i: (i, 1)),
             (proj, (tm, SGU_OUT), lambda i: (i, 2)), (g_mla, (1, MLA_OUT), lambda i: (0, 0)),
             (g_sgu, (1, SGU_OUT), lambda i: (0, 0)), (v_gain, (1, SGU_OUT), lambda i: (0, 0)),
             (w_tril, (GROUPS, CHUNK, CHUNK), lambda i: (0, 0, 0)), (w_tril_t, (GROUPS, CHUNK, CHUNK), lambda i: (0, 0, 0)),
             (b_full, (CHUNK, SGU_OUT), lambda i: (0, 0))],
        outs=[((T, MLA_OUT), BF, (tm, MLA_OUT), lambda i: (i, 0)),
              ((T, proj.shape[1]), BF, (tm, proj.shape[1]), lambda i: (i, 0)),
              ((1, MLA_OUT), F32, (1, MLA_OUT), lambda i: (0, 0)), ((1, SGU_OUT), F32, (1, SGU_OUT), lambda i: (0, 0)),
              ((1, SGU_OUT), F32, (1, SGU_OUT), lambda i: (0, 0)),
              ((GROUPS, CHUNK, CHUNK), F32, (GROUPS, CHUNK, CHUNK), lambda i: (0, 0, 0)),
              ((CHUNK, SGU_OUT), F32, (CHUNK, SGU_OUT), lambda i: (0, 0))],
        scratch=[pltpu.VMEM((tm, SGU_OUT), F32), pltpu.VMEM((tm, SGU_OUT), F32)], deps=deps)


def _shift_down(z, n, row):
    return jnp.where(row >= n, pltpu.roll(z, n, 0), 0.0)


def _shift_up(z, n, row, T):
    return jnp.where(row < T - n, pltpu.roll(z, T - n, 0), 0.0)


def conv_fwd(proj, conv_w, *, name):
    T, D3 = proj.shape
    D = D3 // 3
    tn = _tile(D, 256)
    nj = D // tn

    def body(b_ref, c_ref, x_ref, w_ref, o_ref):
        row = lax.broadcasted_iota(jnp.int32, (T, tn), 0)
        z = c_ref[...] * x_ref[...]
        zc = w_ref[2:3, :] * z + w_ref[1:2, :] * _shift_down(z, 1, row) + w_ref[0:1, :] * _shift_down(z, 2, row)
        o_ref[...] = (b_ref[...] * zc).astype(BF)

    return _pcall(
        body, name=name, grid=(nj,),
        ins=[(proj, (T, tn), lambda j: (0, j)), (proj, (T, tn), lambda j: (0, nj + j)),
             (proj, (T, tn), lambda j: (0, 2 * nj + j)), (conv_w, (3, tn), lambda j: (0, j))],
        outs=[((T, D), BF, (T, tn), lambda j: (0, j))], semantics=("parallel",))[0]


def conv_bwd(dg, proj, conv_w, *, name, deps=()):
    T, D3 = proj.shape
    D = D3 // 3
    tn = _tile(D, 256)
    nj = D // tn

    def body(dg_ref, b_ref, c_ref, x_ref, w_ref, dp_ref, dw_ref, dc_scr, dx_scr):
        part = pl.program_id(1)

        @pl.when(part == 0)
        def _():
            row = lax.broadcasted_iota(jnp.int32, (T, tn), 0)
            c, x = c_ref[...], x_ref[...]
            z = c * x
            z1 = _shift_down(z, 1, row)
            z2 = _shift_down(z, 2, row)
            dgv = dg_ref[...]
            zc = w_ref[2:3, :] * z + w_ref[1:2, :] * z1 + w_ref[0:1, :] * z2
            dp_ref[...] = (dgv * zc).astype(BF)
            dzc = dgv * b_ref[...]
            dw_ref[0:1, :] = jnp.sum(dzc * z2, axis=0, keepdims=True)
            dw_ref[1:2, :] = jnp.sum(dzc * z1, axis=0, keepdims=True)
            dw_ref[2:3, :] = jnp.sum(dzc * z, axis=0, keepdims=True)
            dz = (w_ref[2:3, :] * dzc + w_ref[1:2, :] * _shift_up(dzc, 1, row, T)
                  + w_ref[0:1, :] * _shift_up(dzc, 2, row, T))
            dc_scr[...] = (dz * x).astype(BF)
            dx_scr[...] = (dz * c).astype(BF)

        @pl.when(part == 1)
        def _():
            dp_ref[...] = dc_scr[...]

        @pl.when(part == 2)
        def _():
            dp_ref[...] = dx_scr[...]

    return _pcall(
        body, name=name, grid=(nj, 3),
        ins=[(dg, (T, tn), lambda j, p: (0, j)), (proj, (T, tn), lambda j, p: (0, j)),
             (proj, (T, tn), lambda j, p: (0, nj + j)), (proj, (T, tn), lambda j, p: (0, 2 * nj + j)),
             (conv_w, (3, tn), lambda j, p: (0, j))],
        outs=[((T, D3), BF, (T, tn), lambda j, p: (0, p * nj + j)), ((3, D), F32, (3, tn), lambda j, p: (0, j))],
        scratch=[pltpu.VMEM((T, tn), BF), pltpu.VMEM((T, tn), BF)], semantics=("parallel", "arbitrary"), deps=deps)


def loss_bwd(x_parts, gain, target, *, name):
    T, D = target.shape
    tm = _tile(T, ROW_TILE, 8)
    n_x = len(x_parts)

    def body(*refs):
        x_refs = refs[:n_x]
        g_ref, t_ref, dx_ref, dxb_ref, dg_ref, loss_ref = refs[n_x:]
        first = pl.program_id(0) == 0
        xv = jnp.concatenate([r[...] for r in x_refs], axis=-1) if n_x > 1 else x_refs[0][...]
        r = _rstd(xv)
        xh = xv * r
        gain_v = g_ref[...]
        err = xh * gain_v - t_ref[...]
        part = 0.5 * jnp.sum(jnp.mean(err * err, axis=-1, keepdims=True), axis=0, keepdims=True)
        _acc_rows(loss_ref, jnp.broadcast_to(part, (1, LANES)), first)
        dy = err * (1.0 / D)
        gdy = dy * gain_v
        dx = r * (gdy - xh * jnp.mean(gdy * xh, axis=-1, keepdims=True))
        dx_ref[...] = dx
        dxb_ref[...] = dx.astype(BF)
        _acc_rows(dg_ref, dy * xh, first)

    return _pcall(
        body, name=name, grid=(T // tm,),
        ins=[(p, (tm, D // n_x), lambda i: (i, 0)) for p in x_parts]
        + [(gain, (1, D), lambda i: (0, 0)), (target, (tm, D), lambda i: (i, 0))],
        outs=[((T, D), F32, (tm, D), lambda i: (i, 0)), ((T, D), BF, (tm, D), lambda i: (i, 0)),
              ((1, D), F32, (1, D), lambda i: (0, 0)), ((1, LANES), F32, (1, LANES), lambda i: (0, 0))])


def _adamw(g, w, m, v):
    m = ADAM_B1 * m + (1.0 - ADAM_B1) * g
    v = ADAM_B2 * v + (1.0 - ADAM_B2) * (g * g)
    m_hat = m / ADAM_C1
    v_hat = v / ADAM_C2
    delta = -ADAM_LR * (m_hat / (jnp.sqrt(v_hat) + ADAM_EPS) + ADAM_WD * w)
    return delta, m, v


def adam_flat(g, w, m, v, *, name):
    def body(g_ref, w_ref, m_ref, v_ref, d_ref, nm_ref, nv_ref):
        d, nm, nv = _adamw(g_ref[...], w_ref[...], m_ref[...], v_ref[...])
        d_ref[...] = d
        nm_ref[...] = nm
        nv_ref[...] = nv

    blk = g.shape
    zero = lambda: (0, 0)
    return _pcall(body, name=name, grid=(),
                  ins=[(t, blk, zero) for t in (g, w, m, v)],
                  outs=[(blk, F32, blk, zero)] * 3)


def _chip_slots():
    x, y, c = lax.axis_index("x"), lax.axis_index("y"), lax.axis_index("c")
    chips = [(1 - x, y), (x, 1 - y), (1 - x, 1 - y)]
    return x, y, c, chips


def device_index():
    x, y, c, chips = _chip_slots()
    return jnp.stack([4 * x + 2 * y + c, 2 * x + y] + [4 * cx + 2 * cy + c for cx, cy in chips]
                     + [2 * cx + cy for cx, cy in chips]).astype(jnp.int32)


def _job_rows(R, C, n_steps):
    if n_steps is None:
        n_steps = max(1, R * C // STREAM_BLOCK_ELEMS)
    n_blk = max([d for d in range(1, n_steps + 1) if R % d == 0 and (R // d) % 16 == 0] or [1])
    return R // n_blk, n_blk


def run_job(job, *, index, name, deps=()):
    jb = job(None)
    n_in = len(jb["ins"])

    def body(idx_ref, *refs):
        jb["fn"](refs[:n_in], refs[n_in:n_in + len(jb["outs"])])

    return _pcall(body, name=name, grid=(jb["n_blk"],), ins=jb["ins"], outs=jb["outs"], prefetch=index,
                  aliases={1 + a: o for a, o in jb["aliases"].items()}, semantics=("parallel",), deps=deps)


def adam_job(gs, a_buf, b_buf, w, m, v, layer, prev):
    L, R, C = w.shape

    def build(n_steps):
        tr, n_blk = _job_rows(R, C, n_steps)
        blk = (None, tr, C)
        row = lambda t: jnp.minimum(t, n_blk - 1)
        ins = [(gs, blk, lambda t, s: (s[0], row(t), 0)), (a_buf, blk, lambda t, s: (s[1], row(t), 0))]
        ins += [(b_buf, blk, lambda t, s, j=j: (j, row(t), 0)) for j in range(3)]
        ins += [(p, blk, lambda t, s: (layer, row(t), 0)) for p in (w, m, v)]
        ins += [(p, None, None) for p in (prev or [])]

        def fn(i, o):
            g = ((((i[0][...].astype(F32) + i[1][...].astype(F32)) + i[2][...].astype(F32))
                  + i[3][...].astype(F32)) + i[4][...].astype(F32))
            d, nm, nv = _adamw(g, i[5][...], i[6][...], i[7][...])
            o[0][...] = g
            o[1][...] = d
            o[2][...] = nm
            o[3][...] = nv

        return dict(ins=ins, outs=[((L, R, C), F32, blk, lambda t, s: (layer, row(t), 0))] * 4, fn=fn,
                    aliases={8 + o: o for o in range(4)} if prev else {}, n_blk=n_blk)

    return build


def pair_job(gs, a_buf):
    _, R, C = gs.shape

    def build(n_steps):
        tr, n_blk = _job_rows(R, C, n_steps)
        blk = (None, tr, C)
        row = lambda t: jnp.minimum(t, n_blk - 1)
        ins = [(gs, blk, lambda t, s, j=j: (s[2 + j], row(t), 0)) for j in range(3)]
        ins += [(a_buf, blk, lambda t, s, j=j: (s[5 + j], row(t), 0)) for j in range(3)]

        def fn(i, o):
            for j in range(3):
                o[0][j] = (i[j][...].astype(F32) + i[3 + j][...].astype(F32)).astype(BF)

        return dict(ins=ins, outs=[((3, R, C), BF, (3, tr, C), lambda t, s: (0, row(t), 0))], fn=fn, aliases={},
                    n_blk=n_blk)

    return build


def reduce_sum(gs, a_buf, b_buf, *, name):
    _, R, C = gs.shape
    tr = _tile(R, 256, 16)
    x, y, c, _ = _chip_slots()
    idx = jnp.stack([4 * x + 2 * y + c, 2 * x + y]).astype(jnp.int32)

    def body(idx_ref, g_ref, a_ref, b0_ref, b1_ref, b2_ref, o_ref):
        o_ref[...] = ((((g_ref[...].astype(F32) + a_ref[...].astype(F32)) + b0_ref[...].astype(F32))
                       + b1_ref[...].astype(F32)) + b2_ref[...].astype(F32))

    blk3 = (None, tr, C)
    return _pcall(body, name=name, grid=(R // tr,),
                  ins=[(gs, blk3, lambda i, s: (s[0], i, 0)), (a_buf, blk3, lambda i, s: (s[1], i, 0)),
                       (b_buf, blk3, lambda i, s: (0, i, 0)), (b_buf, blk3, lambda i, s: (1, i, 0)),
                       (b_buf, blk3, lambda i, s: (2, i, 0))],
                  outs=[((R, C), F32, (tr, C), lambda i, s: (i, 0))], prefetch=idx, semantics=("parallel",))[0]


def adam_rows(g, w, m, v, *, name):
    R, C = g.shape
    tr = _tile(R, 256, 8)

    def body(g_ref, w_ref, m_ref, v_ref, d_ref, nm_ref, nv_ref):
        d, nm, nv = _adamw(g_ref[...], w_ref[...], m_ref[...], v_ref[...])
        d_ref[...] = d
        nm_ref[...] = nm
        nv_ref[...] = nv

    spec = ((tr, C), lambda i: (i, 0))
    return _pcall(body, name=name, grid=(R // tr,), ins=[(t, *spec) for t in (g, w, m, v)],
                  outs=[((R, C), F32, *spec)] * 3, semantics=("parallel",))


def sum_rows8(gathered, rows, *, name):
    W = gathered.shape[1]

    def body(g_ref, o_ref):
        acc = g_ref[0:rows, :]
        for d in range(1, N_DEV):
            acc = acc + g_ref[d * rows:(d + 1) * rows, :]
        o_ref[...] = acc

    return _pcall(body, name=name, grid=(), ins=[(gathered, gathered.shape, lambda: (0, 0))],
                  outs=[((rows, W), F32, (rows, W), lambda: (0, 0))])[0]


HBM_SPEC = pl.BlockSpec(memory_space=pltpu.HBM)
SEM_SPEC = pl.BlockSpec(memory_space=pltpu.SEMAPHORE)
ANY_SPEC = pl.BlockSpec(memory_space=pl.ANY)
DATAFLOW = pltpu.SideEffectType.DATAFLOW_SIDE_EFFECTING


def _in_hbm(v):
    return pltpu.with_memory_space_constraint(v, pltpu.HBM)


def _slot(p):
    return 4 * p[0] + 2 * p[1] + p[2]


def _gather_peers():
    x, y, c, chips = _chip_slots()
    return (x, y, c), [(x, y, 1 - c)] + [(*chip, c) for chip in chips]


def gather_start(groups, after, *, name):
    flat = [s for g in groups for s in g]
    n, n_g = len(flat), len(groups)
    where = [(gi, ti) for gi, g in enumerate(groups) for ti in range(len(g))]

    def body(*refs):
        src, land = refs[:n], refs[n:2 * n]
        sems = refs[2 * n + 1:2 * n + 1 + 2 * n_g]
        me, peers = _gather_peers()
        for t in range(n):
            gi, ti = where[t]
            for k, to in enumerate(peers):
                pltpu.make_async_remote_copy(
                    src_ref=src[t], dst_ref=land[t].at[_slot(me)], send_sem=sems[2 * gi].at[4 * ti + k],
                    recv_sem=sems[2 * gi + 1].at[4 * ti + k], device_id=to, device_id_type=MESH).start()
        refs[-1][...] = jnp.zeros_like(refs[-1])

    out_shape = []
    for g in groups:
        out_shape += [pltpu.SemaphoreType.DMA((4 * len(g),)), pltpu.SemaphoreType.DMA((4 * len(g),))]
    out_shape += [pltpu.HBM(s.shape, s.dtype) for s in flat]
    out_shape += [pltpu.HBM((N_DEV,) + s.shape, s.dtype) for s in flat]
    out_shape += [jax.ShapeDtypeStruct((8, LANES), F32)]
    aliases = {t: 2 * n_g + t for t in range(n)}
    aliases.update({n + t: 2 * n_g + n + t for t in range(n)})
    res = pl.pallas_call(
        body, name=name, out_shape=out_shape, in_specs=[HBM_SPEC] * (2 * n) + [ANY_SPEC],
        out_specs=[SEM_SPEC] * (2 * n_g) + [HBM_SPEC] * (2 * n) + [pl.BlockSpec(memory_space=pltpu.VMEM)],
        input_output_aliases=aliases, compiler_params=pltpu.CompilerParams(has_side_effects=DATAFLOW),
    )(*[_in_hbm(s) for s in flat], *[_in_hbm(lax.empty((N_DEV,) + s.shape, s.dtype)) for s in flat], after)
    out, off = [], 0
    for gi, g in enumerate(groups):
        k = len(g)
        out.append((res[2 * gi], res[2 * gi + 1], res[2 * n_g + off:2 * n_g + off + k],
                    res[2 * n_g + n + off:2 * n_g + n + off + k]))
        off += k
    return out, res[-1]


def gather_wait(started, after, *, name):
    send_sems, recv_sems, srcs, lands = started
    n = len(srcs)
    after = list(after)

    def body(*refs):
        src, land = refs[:n], refs[n:2 * n]
        send, recv = refs[2 * n], refs[2 * n + 1]
        _, peers = _gather_peers()
        for t in range(n):
            for k, frm in enumerate(peers):
                cp = pltpu.make_async_remote_copy(
                    src_ref=src[t], dst_ref=land[t].at[_slot(frm)], send_sem=send.at[4 * t + k],
                    recv_sem=recv.at[4 * t + k],
                    device_id=frm, device_id_type=MESH)
                cp.wait_send()
                cp.wait_recv()

    res = pl.pallas_call(
        body, name=name,
        out_shape=[pltpu.HBM(s.shape, s.dtype) for s in srcs] + [pltpu.HBM(l.shape, l.dtype) for l in lands],
        in_specs=[HBM_SPEC] * (2 * n) + [SEM_SPEC, SEM_SPEC] + [ANY_SPEC] * len(after),
        out_specs=[HBM_SPEC] * (2 * n), input_output_aliases={t: t for t in range(2 * n)},
        compiler_params=pltpu.CompilerParams(has_side_effects=DATAFLOW),
    )(*srcs, *lands, send_sems, recv_sems, *after)
    return res[:n], res[n:]


def place_own(src, land, *, name):
    R, C = src.shape
    tr = _tile(R, 512, 16)
    x, y, c, _ = _chip_slots()
    idx = jnp.stack([4 * x + 2 * y + c]).astype(jnp.int32)

    def body(idx_ref, s_ref, land_ref, o_ref):
        o_ref[...] = s_ref[...]

    return _pcall(body, name=name, grid=(R // tr,),
                  ins=[(src, (tr, C), lambda i, s: (i, 0)), (land, None, None)],
                  outs=[(land.shape, land.dtype, (None, tr, C), lambda i, s: (s[0], i, 0))],
                  prefetch=idx, aliases={2: 0}, semantics=("parallel",))[0]


def gather_finish(srcs, lands, *, name):
    n = len(srcs)

    def body(*refs):
        land = refs[n:2 * n]
        send_sems, recv_sems = refs[2 * n:]
        x, y, c, chips = _chip_slots()
        me, sibling = (x, y, c), (x, y, 1 - c)

        def copy(t, j, block, to):
            return pltpu.make_async_remote_copy(
                src_ref=land[t].at[_slot(block)], dst_ref=land[t].at[_slot(block)], send_sem=send_sems.at[t, j],
                recv_sem=recv_sems.at[t, j], device_id=to, device_id_type=MESH)

        sends = [copy(t, j, (*chip, c), sibling) for t in range(n) for j, chip in enumerate(chips)]
        for cp in sends:
            cp.start()
        for t in range(n):
            for j, chip in enumerate(chips):
                copy(t, j, (*chip, 1 - c), me).wait_recv()
        for cp in sends:
            cp.wait_send()

    passed = pl.pallas_call(
        body, name=name, out_shape=[jax.ShapeDtypeStruct(l.shape, l.dtype) for l in lands],
        in_specs=[ANY_SPEC] * n, out_specs=[ANY_SPEC] * n,
        input_output_aliases={t: t for t in range(n)},
        scratch_shapes=[pltpu.SemaphoreType.DMA((n, 3)), pltpu.SemaphoreType.DMA((n, 3))],
    )(*lands)
    return [place_own(s, l, name=f"{name}_own{t}") for t, (s, l) in enumerate(zip(srcs, passed))]


def chips_start(pairs, *, name):
    n = len(pairs)

    def body(*refs):
        src, land = refs[:n], refs[n:2 * n]
        send, recv = refs[2 * n], refs[2 * n + 1]
        token = refs[-1]
        x, y, c, chips = _chip_slots()
        for t in range(n):
            for j, chip in enumerate(chips):
                pltpu.make_async_remote_copy(
                    src_ref=src[t].at[j], dst_ref=land[t].at[j], send_sem=send.at[3 * t + j],
                    recv_sem=recv.at[3 * t + j], device_id=(*chip, c), device_id_type=MESH).start()
        token[...] = jnp.zeros_like(token)

    res = pl.pallas_call(
        body, name=name,
        out_shape=[pltpu.SemaphoreType.DMA((3 * n,)), pltpu.SemaphoreType.DMA((3 * n,))]
        + [pltpu.HBM(p.shape, p.dtype) for p in pairs] * 2 + [jax.ShapeDtypeStruct((8, LANES), F32)],
        in_specs=[HBM_SPEC] * (2 * n),
        out_specs=[SEM_SPEC, SEM_SPEC] + [HBM_SPEC] * (2 * n) + [pl.BlockSpec(memory_space=pltpu.VMEM)],
        input_output_aliases={t: 2 + t for t in range(2 * n)},
        compiler_params=pltpu.CompilerParams(has_side_effects=DATAFLOW),
    )(*[_in_hbm(p) for p in pairs], *[_in_hbm(lax.empty(p.shape, p.dtype)) for p in pairs])
    return res[0], res[1], res[2:2 + n], res[2 + n:2 + 2 * n], res[-1]


def chips_wait(started, after, *, name):
    send_sems, recv_sems, srcs, lands, _ = started
    n = len(srcs)

    def body(*refs):
        src, land = refs[:n], refs[n:2 * n]
        send, recv = refs[2 * n], refs[2 * n + 1]
        x, y, c, chips = _chip_slots()
        for t in range(n):
            for j, chip in enumerate(chips):
                cp = pltpu.make_async_remote_copy(
                    src_ref=src[t].at[j], dst_ref=land[t].at[j], send_sem=send.at[3 * t + j],
                    recv_sem=recv.at[3 * t + j], device_id=(*chip, c), device_id_type=MESH)
                cp.wait_send()
                cp.wait_recv()

    res = pl.pallas_call(
        body, name=name, out_shape=[pltpu.HBM(s.shape, s.dtype) for s in srcs] * 2,
        in_specs=[HBM_SPEC] * (2 * n) + [SEM_SPEC, SEM_SPEC, ANY_SPEC], out_specs=[HBM_SPEC] * (2 * n),
        input_output_aliases={t: t for t in range(2 * n)},
        compiler_params=pltpu.CompilerParams(has_side_effects=DATAFLOW),
    )(*srcs, *lands, send_sems, recv_sems, after)
    return res[n:]


def _sibling_copies(src, land, send, recv, n):
    x, y, c, _ = _chip_slots()
    return [pltpu.make_async_remote_copy(
        src_ref=src[t].at[4 * (q // 2) + 2 * (q % 2) + (1 - c)], dst_ref=land[t].at[q], send_sem=send.at[4 * t + q],
        recv_sem=recv.at[4 * t + q], device_id=(x, y, 1 - c), device_id_type=MESH)
        for t in range(n) for q in range(4)]


def sibling_start(gs, *, name):
    n = len(gs)

    def body(*refs):
        for cp in _sibling_copies(refs[:n], refs[n:2 * n], refs[2 * n], refs[2 * n + 1], n):
            cp.start()
        refs[-1][...] = jnp.zeros_like(refs[-1])

    lands = [lax.empty((4,) + g.shape[1:], g.dtype) for g in gs]
    res = pl.pallas_call(
        body, name=name,
        out_shape=[pltpu.SemaphoreType.DMA((4 * n,)), pltpu.SemaphoreType.DMA((4 * n,))]
        + [pltpu.HBM(g.shape, g.dtype) for g in gs] + [pltpu.HBM(l.shape, l.dtype) for l in lands]
        + [jax.ShapeDtypeStruct((8, LANES), F32)],
        in_specs=[HBM_SPEC] * (2 * n),
        out_specs=[SEM_SPEC, SEM_SPEC] + [HBM_SPEC] * (2 * n) + [pl.BlockSpec(memory_space=pltpu.VMEM)],
        input_output_aliases={t: 2 + t for t in range(2 * n)},
        compiler_params=pltpu.CompilerParams(has_side_effects=DATAFLOW),
    )(*[_in_hbm(g) for g in gs], *[_in_hbm(l) for l in lands])
    return res[0], res[1], res[2:2 + n], res[2 + n:2 + 2 * n], res[-1]


def sibling_wait(started, after, *, name):
    send_sems, recv_sems, srcs, lands, _ = started
    n = len(srcs)

    def body(*refs):
        for cp in _sibling_copies(refs[:n], refs[n:2 * n], refs[2 * n], refs[2 * n + 1], n):
            cp.wait_send()
            cp.wait_recv()

    res = pl.pallas_call(
        body, name=name,
        out_shape=[pltpu.HBM(s.shape, s.dtype) for s in srcs] + [pltpu.HBM(l.shape, l.dtype) for l in lands],
        in_specs=[HBM_SPEC] * (2 * n) + [SEM_SPEC, SEM_SPEC, ANY_SPEC], out_specs=[HBM_SPEC] * (2 * n),
        input_output_aliases={t: t for t in range(2 * n)},
        compiler_params=pltpu.CompilerParams(has_side_effects=DATAFLOW),
    )(*srcs, *lands, send_sems, recv_sems, after)
    return res[:n], res[n:]


def all_gather_vmem(x_shard, *, name, after=None):
    m_per, n = x_shard.shape
    n_after = 0 if after is None else 1

    def body(x_ref, *rest):
        out_ref, send_sems, recv_sems, local_sem = rest[n_after:]
        x, y, c, chips = _chip_slots()
        me, sibling = (x, y, c), (x, y, 1 - c)

        def rows(px, py, pc):
            return out_ref.at[pl.ds((4 * px + 2 * py + pc) * m_per, m_per), :]

        def copy(k, block, to, src=None):
            return pltpu.make_async_remote_copy(
                src_ref=rows(*block) if src is None else src, dst_ref=rows(*block),
                send_sem=send_sems.at[k], recv_sem=recv_sems.at[k], device_id=to, device_id_type=MESH)

        mine = pltpu.make_async_copy(x_ref, rows(*me), local_sem)
        mine.start()
        first = [copy(0, me, sibling, src=x_ref)]
        first += [copy(1 + j, me, (*chip, c), src=x_ref) for j, chip in enumerate(chips)]
        for cp in first:
            cp.start()
        passed = [copy(4 + j, (*chip, c), sibling) for j, chip in enumerate(chips)]
        for j, chip in enumerate(chips):
            copy(1 + j, (*chip, c), me).wait_recv()
            passed[j].start()
        copy(0, sibling, me).wait_recv()
        for j, chip in enumerate(chips):
            copy(4 + j, (*chip, 1 - c), me).wait_recv()
        for cp in first + passed:
            cp.wait_send()
        mine.wait()

    vmem = pl.BlockSpec(memory_space=pltpu.VMEM)
    return pl.pallas_call(
        body, name=name, out_shape=jax.ShapeDtypeStruct((N_DEV * m_per, n), x_shard.dtype),
        in_specs=[vmem] + [ANY_SPEC] * n_after, out_specs=vmem,
        scratch_shapes=[pltpu.SemaphoreType.DMA((7,)), pltpu.SemaphoreType.DMA((7,)), pltpu.SemaphoreType.DMA],
        compiler_params=pltpu.CompilerParams(vmem_limit_bytes=int(min(
            VMEM_LIMIT_CAP, 2 * (N_DEV + 1) * m_per * n * x_shard.dtype.itemsize + 16 * 2 ** 20))),
    )(x_shard, *([] if after is None else [after]))


def _rope_slab(cols):
    z = jnp.zeros(cols.shape[:-1] + (HALF_ROPE,), cols.dtype)
    return jnp.concatenate([cols[..., :HALF_ROPE], z, cols[..., HALF_ROPE:], z], axis=-1)


def _rope_unslab(slab):
    return jnp.concatenate([slab[..., :HALF_ROPE], slab[..., 2 * HALF_ROPE:3 * HALF_ROPE]], axis=-1)


def _pack_w_in_t(wt_g):
    s, c, d = wt_g.shape
    w = wt_g.reshape(s * c, d)
    c2, c3 = Q_LORA + KV_LORA, Q_LORA + KV_LORA + QK_ROPE
    r = w[c2:c3]
    z = jnp.zeros((HALF_ROPE, d), w.dtype)
    return jnp.concatenate([w[:c2], w[c3:], r[:HALF_ROPE], z, r[HALF_ROPE:], z], axis=0)


def unpack_w_in_t_grad(dwt, *, name):
    n_rows, d = dwt.shape
    kr = QK_ROPE
    n_out_rows = n_rows - kr
    blk = n_out_rows // 7
    assert blk * 7 == n_out_rows and blk % kr == 0 and n_rows % (2 * kr) == 0
    kr_row = Q_LORA + KV_LORA
    k_mix = kr_row // blk
    off = kr_row - k_mix * blk
    slab_block = (n_rows - 2 * kr) // (2 * kr)

    def body(prev_ref, in_ref, slab_ref, o_ref):
        k = pl.program_id(0)

        @pl.when(k < k_mix)
        def _():
            o_ref[...] = in_ref[...]

        @pl.when(k == k_mix)
        def _():
            o_ref[:off, :] = in_ref[:off, :]
            o_ref[off:off + HALF_ROPE, :] = slab_ref[:HALF_ROPE, :]
            o_ref[off + HALF_ROPE:off + kr, :] = slab_ref[2 * HALF_ROPE:3 * HALF_ROPE, :]
            o_ref[off + kr:, :] = in_ref[off:blk - kr, :]

        @pl.when(k > k_mix)
        def _():
            o_ref[:kr, :] = prev_ref[blk - kr:, :]
            o_ref[kr:, :] = in_ref[:blk - kr, :]

    out = _pcall(body, name=name, grid=(7,),
                 ins=[(dwt, (blk, d), lambda k: (jnp.maximum(k - 1, 0), 0)), (dwt, (blk, d), lambda k: (k, 0)),
                      (dwt, (2 * kr, d), lambda k: (slab_block, 0))],
                 outs=[((n_out_rows, d), dwt.dtype, (blk, d), lambda k: (k, 0))], semantics=("parallel",))[0]
    return out.reshape(N_DEV, n_out_rows // N_DEV, d)


def _rope_tables(positions):
    inv_freq = ROPE_BASE ** (-jnp.arange(0, QK_ROPE, 2, dtype=F32) / QK_ROPE)
    ang = positions.astype(F32)[:, None] * inv_freq
    cos, sin = jnp.cos(ang), jnp.sin(ang)
    z = jnp.zeros_like(cos)
    return jnp.concatenate([cos, z, cos, z], axis=-1), jnp.concatenate([-sin, z, sin, z], axis=-1)


def _mlp_up(x, gain, w1, tag):
    hn = rms_fwd(x, gain, name=f"mlp{tag}_norm")

    def act_epi(acc):
        a = jnp.maximum(acc, 0.0)
        return a, a * a

    T = x.shape[0]
    F = w1.shape[0] * w1.shape[2]
    a, act = mm(hn, w1, name=f"mlp{tag}_up", outs=[((T, F), BF, None), ((T, F), BF, None)], epi=act_epi)
    return hn, a, act


def _mlp_down(x, act, w2, tag, part=0):
    n = w2.shape[1]
    bm = _tile(x.shape[0], MM_TILE)
    bn = _tile(n, MM_TILE)
    per = n // bn
    return mm(act, w2, name=f"mlp{tag}_down{part}", out=((x.shape[0], n), F32), bm=bm, bn=bn,
              epi=lambda acc, r: (acc + r[...],), epi_ins=[(x, (bm, bn), lambda i, j, k: (i, part * per + j))])


def _mlp_bwd_weights(w1, w2, saved, dxb, tag):
    hn, a, act = saved
    T, D = dxb.shape
    F = a.shape[1]
    bm = _tile(T, MM_TILE)
    bn = _tile(F, min(MM_TILE, w1.shape[2]))
    dhid = mm(dxb, w2, tb=True, name=f"mlp{tag}_dhid", out=((T, F), BF), bm=bm, bn=bn,
              epi=lambda acc, a_ref: (2.0 * a_ref[...].astype(F32) * acc,),
              epi_ins=[(a, (bm, bn), lambda i, j, k: (i, j))])
    dw2 = mm(act, dxb, ta=True, name=f"mlp{tag}_dw2", out=((F, D), BF))
    dw1 = mm(hn, dhid, ta=True, name=f"mlp{tag}_dw1", out=(w1.shape, BF))
    return dhid, dw1, dw2.reshape(N_DEV, F // N_DEV, D)


def _reduce_begin(grads, tag):
    return sibling_start(grads, name=f"reduce_sibling_start_{tag}")


def _reduce_continue(sib, after, tag, index):
    grads, a_bufs = sibling_wait(sib, after, name=f"reduce_sibling_wait_{tag}")
    pairs = [run_job(pair_job(g, a), index=index, name=f"pair_sum_{tag}{t}")[0]
             for t, (g, a) in enumerate(zip(grads, a_bufs))]
    return grads, a_bufs, chips_start(pairs, name=f"reduce_chips_start_{tag}")


def kernel(x, positions, e_norm_mix, e_w_in, e_q_norm, e_w_uq, e_kv_norm, e_w_ukv, e_v_norm, e_sgu_w, e_sgu_b, e_mla_out_norm, e_sgu_out_norm, e_w_out, o_norm_mix, o_w_in, o_conv_w, o_w_out, mlp_norm, mlp_w1, mlp_w2, final_norm, loss_target, m_e_norm_mix, m_e_w_in, m_e_q_norm, m_e_w_uq, m_e_kv_norm, m_e_w_ukv, m_e_v_norm, m_e_sgu_w, m_e_sgu_b, m_e_mla_out_norm, m_e_sgu_out_norm, m_e_w_out, m_o_norm_mix, m_o_w_in, m_o_conv_w, m_o_w_out, m_mlp_norm, m_mlp_w1, m_mlp_w2, m_final_norm, v_e_norm_mix, v_e_w_in, v_e_q_norm, v_e_w_uq, v_e_kv_norm, v_e_w_ukv, v_e_v_norm, v_e_sgu_w, v_e_sgu_b, v_e_mla_out_norm, v_e_sgu_out_norm, v_e_w_out, v_o_norm_mix, v_o_w_in, v_o_conv_w, v_o_w_out, v_mlp_norm, v_mlp_w1, v_mlp_w2, v_final_norm):
    T, D = x.shape[1], x.shape[2]
    d_shard = o_norm_mix.shape[1]
    x0 = x[0]
    target = loss_target[0]
    me = 4 * lax.axis_index("x") + 2 * lax.axis_index("y") + lax.axis_index("c")

    bf = lambda s: s.astype(BF)
    gather_groups = [[bf(jnp.transpose(e_w_in[0])), bf(e_w_uq[0]), bf(e_w_ukv[0])], [bf(e_w_out[0]), bf(mlp_w1[0])],
                     [bf(mlp_w2[0]), bf(o_w_in[0])], [bf(o_w_out[0]), bf(mlp_w1[1])], [bf(mlp_w2[1])]]
    small_rows = jnp.concatenate([o_norm_mix, o_conv_w[0], jnp.zeros((4, d_shard), F32)], axis=0)
    small_flat = all_gather_vmem(small_rows, name="gather_small")
    started, start_token = gather_start(gather_groups[:1], small_flat, name="gather_start0")
    started_rest, rest_token = gather_start(gather_groups[1:], start_token, name="gather_start1")
    started += started_rest

    def gathered(gi, after):
        srcs, lands = gather_wait(started[gi], after, name=f"gather_wait{gi}")
        return gather_finish(srcs, lands, name=f"gather_finish{gi}")

    small_g = small_flat.reshape(N_DEV, 8, d_shard)
    o_norm_full = small_g[:, 0, :].reshape(1, D)
    conv_w_full = jnp.transpose(small_g[:, 1:4, :], (1, 0, 2)).reshape(3, D)
    w_tril = jnp.tril(e_sgu_w[0])
    w_tril_b = w_tril.astype(BF)
    w_tril_tb = jnp.swapaxes(w_tril, 1, 2).astype(BF)
    b_full = jnp.repeat(e_sgu_b[0].T, CH, axis=1)
    v_gain = e_v_norm[0].reshape(1, SGU_OUT)
    cos_t, sin_t = _rope_tables(positions[0])
    mlp_gain = [mlp_norm[0:1], mlp_norm[1:2]]
    final_gain = final_norm.reshape(1, D)

    h0 = rms_fwd(x0, e_norm_mix, name="e_norm", deps=[rest_token])
    g_w_in_t, g_w_uq, w_ukv = gathered(
        0, [h0, cos_t, sin_t, w_tril_b, w_tril_tb, b_full, o_norm_full, conv_w_full])
    w_in_t = _pack_w_in_t(g_w_in_t)
    w_uq = jnp.concatenate([g_w_uq[..., :QK_NOPE], _rope_slab(g_w_uq[..., QK_NOPE:])], axis=-1)
    proj = mm(h0, w_in_t, tb=True, name="e_in", out=((T, w_in_t.shape[0]), F32), bn=_tile(w_in_t.shape[0], 640))
    qn, kvn, krope = mla_prep(proj, e_q_norm, e_kv_norm, cos_t, sin_t, name="mla_prep")
    bm = _tile(T, MM_TILE)

    def q_epi(acc, cos_ref, sin_ref):
        return (jnp.concatenate([acc[:, :QK_NOPE], _rope_fwd(acc[:, QK_NOPE:], cos_ref[...], sin_ref[...])], axis=-1),)

    q = mm(qn, w_uq, name="mla_q", out=((T, HEADS * HEAD_PAD), BF), bm=bm, bn=HEAD_PAD, epi=q_epi,
           epi_ins=[(cos_t, (bm, LANES), lambda i, j, k: (i, 0)), (sin_t, (bm, LANES), lambda i, j, k: (i, 0))])

    def kv_epi(acc, kr_ref):
        return jnp.concatenate([acc[:, :QK_NOPE].astype(BF), kr_ref[...]], axis=-1), acc[:, QK_NOPE:]

    k, v = mm(kvn, w_ukv, name="mla_kv", bm=bm, bn=HEAD_PAD, epi=kv_epi,
              outs=[((T, HEADS * HEAD_PAD), BF, HEAD_PAD), ((T, MLA_OUT), BF, V_HEAD)],
              epi_ins=[(krope, (bm, LANES), lambda i, j, k: (i, 0))])
    attn, attn_lse = attn_fwd(q, k, v, name="attn_fwd")
    mixed = mix_fwd(attn, proj, e_mla_out_norm, e_sgu_out_norm, v_gain, w_tril_b, b_full, name="mix_fwd")
    bn = _tile(D, MM_TILE)
    g_w_out_e, w1_0 = gathered(1, [mixed])
    w_out_e = g_w_out_e.reshape(-1, D)
    x1 = mm(mixed, w_out_e, name="e_out", out=((T, D), F32), bm=bm, bn=bn,
            epi=lambda acc, r: (acc + r[...],), epi_ins=[(x0, (bm, bn), lambda i, j, k: (i, j))])
    hn0, a0, act0 = _mlp_up(x1, mlp_gain[0], w1_0, 0)
    g_w2_0, g_w_in_o = gathered(2, [act0])
    w2_0 = g_w2_0.reshape(-1, D)
    x2 = _mlp_down(x1, act0, w2_0, 0)
    ho = rms_fwd(x2, o_norm_full, name="o_norm")
    proj_o = mm(ho, g_w_in_o, name="o_in", out=((T, 3 * D), F32))
    gated = conv_fwd(proj_o, conv_w_full, name="conv_fwd")
    g_w_out_o, w1_1 = gathered(3, [gated])
    w_out_o = g_w_out_o.reshape(-1, D)
    x3 = mm(gated, w_out_o, name="o_out", out=((T, D), F32), bm=bm, bn=bn,
            epi=lambda acc, r: (acc + r[...],), epi_ins=[(x2, (bm, bn), lambda i, j, k: (i, j))])
    hn1, a1, act1 = _mlp_up(x3, mlp_gain[1], w1_1, 1)
    (g_w2_1,) = gathered(4, [act1])
    w2_1 = g_w2_1.reshape(-1, D)
    x4 = _mlp_down(x3, act1, w2_1, 1)
    w1, w2 = [w1_0, w1_1], [w2_0, w2_1]

    dx4, dx4b, d_final, loss_part = loss_bwd([x4], final_gain, target, name="loss_bwd")

    hosted = dict(job_index=device_index())
    dhid1, dw1_1, dw2_1 = _mlp_bwd_weights(w1[1], w2[1], (hn1, a1, act1), dx4b, 1)
    sib_r0 = _reduce_begin([dw1_1, dw2_1], "r0")
    dhn1 = mm(dhid1, w1[1], tb=True, name="mlp1_dhn", out=((T, D), F32), deps=[sib_r0[-1]])
    grads_r0, a_r0 = sibling_wait(sib_r0, dhn1, name="reduce_sibling_wait_r0")
    dx3, dx3b, d_mlp1 = rms_bwd(x3, mlp_gain[1], dhn1, dres=dx4, name="mlp1_norm_bwd")

    dgated, ((pair_r0a,),) = mm(dx3b, w_out_o, tb=True, name="o_out_dx", out=((T, D), F32),
                                jobs=[pair_job(grads_r0[0], a_r0[0])], **hosted)
    dw_out_o, ((pair_r0b,),) = mm(gated, dx3b, ta=True, name="o_out_dw", out=((D, D), BF),
                                  jobs=[pair_job(grads_r0[1], a_r0[1])], **hosted)
    st_r0 = chips_start([pair_r0a, pair_r0b], name="reduce_chips_start_r0")
    dproj_o, dconv_full = conv_bwd(dgated, proj_o, conv_w_full, name="conv_bwd", deps=[st_r0[-1]])
    dw_in_o = mm(ho, dproj_o, ta=True, name="o_in_dw", out=(g_w_in_o.shape, BF))
    sib_r1 = _reduce_begin([dw_out_o.reshape(g_w_out_o.shape), dw_in_o], "r1")
    dho = mm(dproj_o, g_w_in_o, tb=True, name="o_in_dx", out=((T, D), F32), deps=[sib_r1[-1]])
    grads_r1, a_r1 = sibling_wait(sib_r1, dho, name="reduce_sibling_wait_r1")
    dx2, dx2b, d_onorm_full = rms_bwd(x2, o_norm_full, dho, dres=dx3, name="o_norm_bwd")

    d_ff = a0.shape[1]
    bm_h, bn_h = _tile(T, MM_TILE), _tile(d_ff, min(MM_TILE, w1[0].shape[2]))
    dhid0, ((pair_r1a,), (pair_r1b,)) = mm(
        dx2b, w2[0], tb=True, name="mlp0_dhid", out=((T, d_ff), BF), bm=bm_h, bn=bn_h,
        epi=lambda acc, a_ref: (2.0 * a_ref[...].astype(F32) * acc,),
        epi_ins=[(a0, (bm_h, bn_h), lambda i, j, k: (i, j))],
        jobs=[pair_job(grads_r1[0], a_r1[0]), pair_job(grads_r1[1], a_r1[1])], **hosted)
    st_r1 = chips_start([pair_r1a, pair_r1b], name="reduce_chips_start_r1")
    dw2_0 = mm(act0, dx2b, ta=True, name="mlp0_dw2", out=((d_ff, D), BF), deps=[st_r1[-1]])
    b_r0 = chips_wait(st_r0, dw2_0, name="reduce_chips_wait_r0")
    dw1_0, (r_w1, r_w2) = mm(
        hn0, dhid0, ta=True, name="mlp0_dw1", out=(w1[0].shape, BF),
        jobs=[adam_job(grads_r0[0], a_r0[0], b_r0[0], mlp_w1, m_mlp_w1, v_mlp_w1, 1, None),
              adam_job(grads_r0[1], a_r0[1], b_r0[1], mlp_w2, m_mlp_w2, v_mlp_w2, 1, None)], **hosted)
    sib_r2 = _reduce_begin([dw1_0, dw2_0.reshape(N_DEV, d_ff // N_DEV, D)], "r2")
    dhn0 = mm(dhid0, w1[0], tb=True, name="mlp0_dhn", out=((T, D), F32), deps=[sib_r2[-1]])
    grads_r2, a_r2 = sibling_wait(sib_r2, dhn0, name="reduce_sibling_wait_r2")
    dx1, dx1b, d_mlp0 = rms_bwd(x1, mlp_gain[0], dhn0, dres=dx2, name="mlp0_norm_bwd")

    dmixed, ((pair_r2a,),) = mm(dx1b, w_out_e, tb=True, name="e_out_dx", out=((T, MLA_OUT + SGU_OUT), F32),
                                jobs=[pair_job(grads_r2[0], a_r2[0])], **hosted)
    dw_out_e, ((pair_r2b,),) = mm(mixed, dx1b, ta=True, name="e_out_dw", out=(w_out_e.shape, BF),
                                  jobs=[pair_job(grads_r2[1], a_r2[1])], **hosted)
    st_r2 = chips_start([pair_r2a, pair_r2b], name="reduce_chips_start_r2")
    (dattn, dproj, d_mla_out, d_sgu_out, d_vgain, d_sgu_w, d_b_full) = mix_bwd(
        dmixed, attn, proj, e_mla_out_norm, e_sgu_out_norm, v_gain, w_tril_b, w_tril_tb, b_full, name="mix_bwd",
        deps=[st_r2[-1]])
    b_r1 = chips_wait(st_r1, dattn, name="reduce_chips_wait_r1")
    dq, dk, dv = attn_bwd(q, k, v, attn, attn_lse, dattn, name="attn_bwd")
    dq_lin, dkv_lin, dproj = mla_bwd_prep(dq, dk, dv, cos_t, sin_t, dproj, name="mla_bwd_prep")
    dw_uq_pad = mm(qn, dq_lin, ta=True, name="mla_q_dw", out=(w_uq.shape, BF))
    dw_ukv = mm(kvn, dkv_lin, ta=True, name="mla_kv_dw", out=(w_ukv.shape, BF))
    dw_uq = jnp.concatenate([dw_uq_pad[..., :QK_NOPE], _rope_unslab(dw_uq_pad[..., QK_NOPE:])], axis=-1)
    sib_r2b = _reduce_begin([dw_out_e.reshape(g_w_out_e.shape), dw_uq, dw_ukv], "r2b")
    dqn = mm(dq_lin, w_uq, tb=True, name="mla_q_dx", out=((T, Q_LORA), F32), deps=[sib_r2b[-1]])
    dkvn = mm(dkv_lin, w_ukv, tb=True, name="mla_kv_dx", out=((T, KV_LORA), F32), deps=[sib_r2b[-1]])
    grads_r2b, a_r2b, st_r2b = _reduce_continue(sib_r2b, dkvn, "r2b", hosted["job_index"])
    dproj, d_qnorm = rms_bwd(proj, e_q_norm, dqn, col_block=0, want_f32=False, into=dproj, name="q_norm_bwd",
                             deps=[st_r2b[-1]])
    dproj, d_kvnorm = rms_bwd(proj, e_kv_norm, dkvn, col_block=1, want_f32=False, into=dproj, name="kv_norm_bwd")
    dw_in_t_pad, (r_w_out_o, r_w_in_o) = mm(
        dproj, h0, ta=True, name="e_in_dw", out=(w_in_t.shape, BF), bm=_tile(w_in_t.shape[0], 640),
        jobs=[adam_job(grads_r1[0], a_r1[0], b_r1[0], o_w_out, m_o_w_out, v_o_w_out, 0, None),
              adam_job(grads_r1[1], a_r1[1], b_r1[1], o_w_in, m_o_w_in, v_o_w_in, 0, None)], **hosted)
    dw_in_t = unpack_w_in_t_grad(dw_in_t_pad, name="e_in_dw_unpack")
    sib_r3 = _reduce_begin([dw_in_t], "r3")
    dh0 = mm(dproj, w_in_t, name="e_in_dx", out=((T, D), F32), deps=[sib_r3[-1]])
    grads_r3, a_r3, st_r3 = _reduce_continue(sib_r3, dh0, "r3", hosted["job_index"])
    tok_r3 = st_r3[-1]
    grad_x, d_enorm = rms_bwd(x0, e_norm_mix, dh0, dres=dx1, want_bf=False, name="e_norm_bwd", deps=[tok_r3])
    b_r2 = chips_wait(st_r2, grad_x, name="reduce_chips_wait_r2")

    def finish(grads, a_bufs, b_bufs, t, w, m, v, layer=0, prev=None, tag="", deps=()):
        return run_job(adam_job(grads[t], a_bufs[t], b_bufs[t], w, m, v, layer, prev), index=hosted["job_index"],
                       name=f"adam_{tag}", deps=deps)

    r_w1 = finish(grads_r2, a_r2, b_r2, 0, mlp_w1, m_mlp_w1, v_mlp_w1, 0, r_w1, tag="w1_l0", deps=[tok_r3])
    r_w2 = finish(grads_r2, a_r2, b_r2, 1, mlp_w2, m_mlp_w2, v_mlp_w2, 0, r_w2, tag="w2_l0", deps=[r_w1[1]])
    b_r2b = chips_wait(st_r2b, r_w2[1], name="reduce_chips_wait_r2b")
    r_w_out_e = finish(grads_r2b, a_r2b, b_r2b, 0, e_w_out, m_e_w_out, v_e_w_out, tag="e_w_out")
    r_w_uq = finish(grads_r2b, a_r2b, b_r2b, 1, e_w_uq, m_e_w_uq, v_e_w_uq, tag="e_w_uq")
    r_w_ukv = finish(grads_r2b, a_r2b, b_r2b, 2, e_w_ukv, m_e_w_ukv, v_e_w_ukv, tag="e_w_ukv")
    b_r3 = chips_wait(st_r3, r_w_out_e[1], name="reduce_chips_wait_r3")
    g_w_in_t = reduce_sum(grads_r3[0], a_r3[0], b_r3[0], name="sum_e_w_in")
    w_in_upd_t = adam_rows(g_w_in_t, jnp.transpose(e_w_in[0]), jnp.transpose(m_e_w_in[0]), jnp.transpose(v_e_w_in[0]),
                           name="adam_e_w_in")
    r_w_in = [jnp.transpose(t)[None] for t in (g_w_in_t, *w_in_upd_t)]

    d_sgu_b = jnp.transpose(d_b_full[:, ::CH])
    d_sgu_w_tril = jnp.tril(d_sgu_w)
    rep = [("e_norm_mix", e_norm_mix, m_e_norm_mix, v_e_norm_mix, d_enorm),
           ("e_q_norm", e_q_norm, m_e_q_norm, v_e_q_norm, d_qnorm),
           ("e_kv_norm", e_kv_norm, m_e_kv_norm, v_e_kv_norm, d_kvnorm),
           ("e_v_norm", e_v_norm, m_e_v_norm, v_e_v_norm, d_vgain),
           ("e_sgu_w", e_sgu_w, m_e_sgu_w, v_e_sgu_w, d_sgu_w_tril),
           ("e_sgu_b", e_sgu_b, m_e_sgu_b, v_e_sgu_b, d_sgu_b),
           ("e_mla_out_norm", e_mla_out_norm, m_e_mla_out_norm, v_e_mla_out_norm, d_mla_out),
           ("e_sgu_out_norm", e_sgu_out_norm, m_e_sgu_out_norm, v_e_sgu_out_norm, d_sgu_out),
           ("mlp_norm", mlp_norm, m_mlp_norm, v_mlp_norm, jnp.concatenate([d_mlp0, d_mlp1], axis=0)),
           ("final_norm", final_norm, m_final_norm, v_final_norm, d_final)]
    sizes = [int(np.prod(r[1].shape)) for r in rep]
    n_rep = sum(sizes)
    n_all = n_rep + 4 * D + 1
    width = -(-n_all // (8 * LANES)) * LANES
    pad = 8 * width - n_all
    flat = jnp.concatenate([r[4].reshape(-1) for r in rep]
                           + [d_onorm_full.reshape(-1), dconv_full.reshape(-1), loss_part[0, :1],
                              jnp.zeros((pad,), F32)])
    summed = sum_rows8(all_gather_vmem(flat.reshape(8, width), name="gather_small_grads", after=b_r3[0]), 8,
                       name="sum_small_grads").reshape(-1)

    loss = summed[n_rep + 4 * D]

    def pack_rep(i):
        return jnp.concatenate([r[i].reshape(-1) for r in rep]).reshape(n_rep // LANES, LANES)

    g_rep = summed[:n_rep].reshape(n_rep // LANES, LANES)
    d_rep, nm_rep, nv_rep = adam_flat(g_rep, pack_rep(1), pack_rep(2), pack_rep(3), name="adam_replicated")

    def unpack_rep(flat2d):
        out, off = {}, 0
        f = flat2d.reshape(-1)
        for r, n in zip(rep, sizes):
            out[r[0]] = f[off:off + n].reshape(r[1].shape)
            off += n
        return out

    small = {"grad": unpack_rep(g_rep), "delta": unpack_rep(d_rep), "new_m": unpack_rep(nm_rep),
             "new_v": unpack_rep(nv_rep)}
    g_onorm = lax.dynamic_slice(summed[n_rep:n_rep + D].reshape(1, D), (0, me * d_shard), (1, d_shard))
    g_conv = lax.dynamic_slice(summed[n_rep + D:n_rep + 4 * D].reshape(3, D), (0, me * d_shard), (3, d_shard))

    def pack_sharded(norm_part, conv_part):
        return jnp.concatenate([norm_part, conv_part, jnp.zeros((4, d_shard), F32)], axis=0)

    g_sh = pack_sharded(g_onorm, g_conv)
    d_sh, nm_sh, nv_sh = adam_flat(g_sh, pack_sharded(o_norm_mix, o_conv_w[0]), pack_sharded(m_o_norm_mix, m_o_conv_w[0]),
                                   pack_sharded(v_o_norm_mix, v_o_conv_w[0]), name="adam_sharded_small")
    for kind, arr in (("grad", g_sh), ("delta", d_sh), ("new_m", nm_sh), ("new_v", nv_sh)):
        small[kind]["o_norm_mix"] = arr[0:1]
        small[kind]["o_conv_w"] = arr[1:4][None]

    big = {"e_w_in": r_w_in, "e_w_uq": r_w_uq, "e_w_ukv": r_w_ukv, "e_w_out": r_w_out_e, "o_w_in": r_w_in_o,
           "o_w_out": r_w_out_o, "mlp_w1": r_w1, "mlp_w2": r_w2}
    order = ["e_norm_mix", "e_w_in", "e_q_norm", "e_w_uq", "e_kv_norm", "e_w_ukv", "e_v_norm", "e_sgu_w", "e_sgu_b",
             "e_mla_out_norm", "e_sgu_out_norm", "e_w_out", "o_norm_mix", "o_w_in", "o_conv_w", "o_w_out", "mlp_norm",
             "mlp_w1", "mlp_w2", "final_norm"]
    result = [loss, grad_x[None]]
    for ki, kind in enumerate(("grad", "delta", "new_m", "new_v")):
        for nm in order:
            result.append(big[nm][ki] if nm in big else small[kind][nm])
    return tuple(result)
```

```python
import numpy as np
import jax
import jax.numpy as jnp
from jax import lax
from jax.experimental import pallas as pl
from jax.experimental.pallas import tpu as pltpu

BF = jnp.bfloat16
F32 = jnp.float32
MESH = pl.DeviceIdType.MESH
N_DEV = 8

EPS = 1e-6
HEADS = 8
Q_LORA = 512
KV_LORA = 512
QK_NOPE = 128
QK_ROPE = 64
HALF_ROPE = QK_ROPE // 2
V_HEAD = 128
HEAD_PAD = 256
ROPE_BASE = 10000.0
GROUPS = 8
CH = 128
CHUNK = 128
SGU_OUT = GROUPS * CH
MLA_OUT = HEADS * V_HEAD
ATTN_SCALE = float((QK_NOPE + QK_ROPE) ** -0.5)

ADAM_LR = 0.001
ADAM_B1 = 0.9
ADAM_B2 = 0.999
ADAM_EPS = 1e-08
ADAM_WD = 0.01
ADAM_STEP = 10
ADAM_C1 = 1.0 - ADAM_B1 ** ADAM_STEP
ADAM_C2 = 1.0 - ADAM_B2 ** ADAM_STEP

V7X_VMEM_BYTES = 64 * 2 ** 20
VMEM_LIMIT_CAP = V7X_VMEM_BYTES - 6 * 2 ** 20
LANES = 128
ROW_TILE = 256
ATTN_TILE = 512
STREAM_BLOCK_ELEMS = 512 * 1024
MM_TILE = 1024
MM_K_TILE = 2048
MM_K_BLOCK_MAX = 3072


def _padded_bytes(block, dtype):
    dims = [d for d in block if d is not None]
    if len(dims) >= 1:
        dims[-1] = -(-dims[-1] // LANES) * LANES
    if len(dims) >= 2:
        dims[-2] = -(-dims[-2] // 16) * 16
    return int(np.prod(dims)) * jnp.dtype(dtype).itemsize


def _pcall(body, *, name, grid, ins, outs, scratch=(), semantics=None, aliases=None, prefetch=None, deps=()):
    any_spec = pl.BlockSpec(memory_space=pl.ANY)
    if deps:
        n_lead = len(ins) + (1 if prefetch is not None else 0)
        n_deps = len(deps)
        inner = body

        def body(*refs):
            inner(*refs[:n_lead], *refs[n_lead + n_deps:])

        ins = list(ins) + [(d, None, None) for d in deps]
    in_specs = [any_spec if b is None else pl.BlockSpec(b, m) for _, b, m in ins]
    out_specs = [any_spec if b is None else pl.BlockSpec(b, m) for _, _, b, m in outs]
    out_shape = [pltpu.HBM(s, d) for s, d, _, _ in outs]
    est = 0
    for a, b, _ in ins:
        if b is not None:
            est += 2 * _padded_bytes(b, a.dtype)
    for _, d, b, _ in outs:
        if b is not None:
            est += 2 * _padded_bytes(b, d)
    for s in scratch:
        if hasattr(s, "shape") and hasattr(s, "dtype"):
            est += _padded_bytes(s.shape, s.dtype)
    limit = int(min(VMEM_LIMIT_CAP, est + 16 * 2 ** 20))
    params = pltpu.CompilerParams(
        dimension_semantics=semantics or ("arbitrary",) * len(grid), vmem_limit_bytes=limit)
    args = [pltpu.with_memory_space_constraint(a, pltpu.HBM) for a, _, _ in ins]
    if prefetch is not None:
        grid_spec = pltpu.PrefetchScalarGridSpec(
            num_scalar_prefetch=1, grid=grid, in_specs=in_specs, out_specs=out_specs, scratch_shapes=list(scratch))
        call = pl.pallas_call(body, out_shape=out_shape, grid_spec=grid_spec, name=name, compiler_params=params,
                              input_output_aliases=aliases or {})
        return call(prefetch, *args)
    call = pl.pallas_call(body, out_shape=out_shape, grid=grid, in_specs=in_specs, out_specs=out_specs,
                          scratch_shapes=list(scratch), name=name, compiler_params=params,
                          input_output_aliases=aliases or {})
    return call(*args)


def _tile(dim, pref, quantum=LANES):
    if dim <= pref:
        return dim
    t = (pref // quantum) * quantum
    while t >= quantum:
        if dim % t == 0:
            return t
        t -= quantum
    return dim


def _vshape(arr_shape):
    if len(arr_shape) == 2:
        return tuple(arr_shape)
    s, r, c = arr_shape
    return (r, s * c)


def _vblock(arr_shape, br, bc, rc):
    if len(arr_shape) == 2:
        return (br, bc), (lambda *g: rc(*g))
    _, _, c = arr_shape
    assert c % bc == 0, (arr_shape, bc)
    per = c // bc

    def imap(*g):
        ri, ci = rc(*g)
        return (ci // per, ri, ci % per)

    return (None, br, bc), imap


def _shard_width(*shapes):
    w = None
    for s in shapes:
        if len(s) == 3:
            w = s[2] if w is None else int(np.gcd(w, s[2]))
    return w


def mm(a, b, *, name, ta=False, tb=False, out=None, outs=None, epi=None, epi_ins=(), bm=None, bn=None, bk=None,
       deps=(), jobs=(), job_index=None):
    av, bv = _vshape(a.shape), _vshape(b.shape)
    M, K = (av[1], av[0]) if ta else av
    K2, N = (bv[1], bv[0]) if tb else bv
    assert K == K2, (a.shape, b.shape, ta, tb)
    if outs is None:
        outs = [(out[0], out[1], None)]
    a_sw = _shard_width(a.shape)
    b_sw = _shard_width(b.shape)
    o_sw = _shard_width(*[o[0] for o in outs])
    m_lim = a_sw if (ta and a_sw) else None
    k_lim = [w for w in ((a_sw if not ta else None), (b_sw if tb else None)) if w]
    n_lim = [w for w in ((b_sw if not tb else None), o_sw) if w]
    if bm is None:
        bm = _tile(M, min([MM_TILE] + ([m_lim] if m_lim else [])))
    if bn is None:
        bn = _tile(N, min([MM_TILE] + n_lim))
    k_shards = 0
    if tb and len(b.shape) == 3 and bk is None and not (a_sw and not ta):
        k_shards = 1
        while 2 * k_shards <= b.shape[0] and 2 * k_shards * b_sw <= MM_K_BLOCK_MAX:
            k_shards *= 2
        bk = k_shards * b_sw
    if bk is None:
        bk = K if (K <= 4096 and not k_lim) else _tile(K, min([MM_K_TILE] + k_lim))
    assert M % bm == 0 and N % bn == 0 and K % bk == 0, (name, M, N, K, bm, bn, bk)
    nk = K // bk
    grid = (M // bm, N // bn, nk)
    if ta:
        a_blk, a_map = _vblock(a.shape, bk, bm, lambda i, j, k: (k, i))
    else:
        a_blk, a_map = _vblock(a.shape, bm, bk, lambda i, j, k: (i, k))
    if k_shards:
        b_blk, b_map = (k_shards, bn, b_sw), (lambda i, j, k: (k, j, 0))
    elif tb:
        b_blk, b_map = _vblock(b.shape, bn, bk, lambda i, j, k: (j, k))
    else:
        b_blk, b_map = _vblock(b.shape, bk, bn, lambda i, j, k: (k, j))
    dn = (((0 if ta else 1,), (1 if tb else 0,)), ((), ()))
    ins = [(a, a_blk, a_map), (b, b_blk, b_map)] + list(epi_ins)
    out_list = []
    for shape, dtype, cols in outs:
        cols = cols or bn
        blk, imap = _vblock(shape, bm, cols, lambda i, j, k: (i, j))
        out_list.append((shape, dtype, blk, imap))
    n_e, n_o = len(epi_ins), len(out_list)

    n_steps = grid[0] * grid[1] * nk
    built = [job(n_steps) for job in jobs]
    aliases = {}
    job_slices = []
    if built:
        def lin(i, j, k):
            return (i * grid[1] + j) * nk + k

        ins = [(arr, blk, None if blk is None else (lambda i, j, k, s, f=f: f(i, j, k))) for arr, blk, f in ins]
        out_list = [(sh, dt, blk, (lambda i, j, k, s, f=f: f(i, j, k))) for sh, dt, blk, f in out_list]
        n_main_in, n_main_out = len(ins), len(out_list)
        for jb in built:
            i0, o0 = len(ins), len(out_list)
            ins += [(arr, blk, None if blk is None else (lambda i, j, k, s, f=f: f(lin(i, j, k), s)))
                    for arr, blk, f in jb["ins"]]
            out_list += [(sh, dt, blk, (lambda i, j, k, s, f=f: f(lin(i, j, k), s))) for sh, dt, blk, f in jb["outs"]]
            aliases.update({1 + i0 + ai: o0 + ao for ai, ao in jb["aliases"].items()})
            job_slices.append((i0, len(jb["ins"]), o0, len(jb["outs"])))
    n_in_total = len(ins)

    def body(*refs):
        if built:
            refs = refs[1:]
        a_ref, b_ref = refs[0], refs[1]
        e_refs = refs[2:2 + n_e]
        o_refs = refs[n_in_total:n_in_total + n_o]
        for jb, (i0, ni, o0, no) in zip(built, job_slices):
            jb["fn"](refs[i0:i0 + ni], refs[n_in_total + o0:n_in_total + o0 + no])

        def finish(acc):
            res = epi(acc, *e_refs) if epi is not None else (acc,)
            for o_ref, r in zip(o_refs, res):
                o_ref[...] = r.astype(o_ref.dtype)

        x = a_ref[...].astype(BF)
        y = b_ref[...].astype(BF)
        if k_shards:
            p = None
            for s in range(k_shards):
                part = lax.dot_general(x[:, s * b_sw:(s + 1) * b_sw], y[s], dn, preferred_element_type=F32)
                p = part if p is None else p + part
        else:
            p = lax.dot_general(x, y, dn, preferred_element_type=F32)
        if nk == 1:
            finish(p)
        else:
            acc_ref = refs[-1]
            k = pl.program_id(2)

            @pl.when(k == 0)
            def _():
                acc_ref[...] = p

            @pl.when(k > 0)
            def _():
                acc_ref[...] += p

            @pl.when(k == nk - 1)
            def _():
                finish(acc_ref[...])

    scratch = [pltpu.VMEM((bm, bn), F32)] if nk > 1 else []
    res = _pcall(body, name=name, grid=grid, ins=ins, outs=out_list, scratch=scratch, deps=deps,
                 semantics=("parallel", "parallel", "arbitrary"), prefetch=job_index if built else None, aliases=aliases)
    main = res[0] if n_o == 1 else res[:n_o]
    if not built:
        return main
    return main, [res[o0:o0 + no] for _, _, o0, no in job_slices]


_GELU_K = float(np.sqrt(2.0 / np.pi))
_GELU_C = 0.044715


def _gelu(x):
    t = jnp.tanh(_GELU_K * (x + _GELU_C * (x * x * x)))
    return 0.5 * x * (1.0 + t)


def _gelu_grad(x):
    t = jnp.tanh(_GELU_K * (x + _GELU_C * (x * x * x)))
    return 0.5 * (1.0 + t) + 0.5 * x * (1.0 - t * t) * (_GELU_K * (1.0 + 3.0 * _GELU_C * (x * x)))


def _rstd(x):
    return lax.rsqrt(jnp.mean(x * x, axis=-1, keepdims=True) + EPS)


def _rms_bwd(x, gain, dy):
    r = _rstd(x)
    xh = x * r
    gdy = dy * gain
    dx = r * (gdy - xh * jnp.mean(gdy * xh, axis=-1, keepdims=True))
    return dx, dy * xh


def _rope_fwd(x, cos_t, sin_t):
    return x * cos_t + pltpu.roll(x, 2 * HALF_ROPE, 1) * sin_t


def _rope_bwd(dy, cos_t, sin_t):
    return dy * cos_t + pltpu.roll(dy * sin_t, 2 * HALF_ROPE, 1)


def _acc_rows(ref, val, first):
    s = jnp.sum(val, axis=0, keepdims=True)

    @pl.when(first)
    def _():
        ref[...] = s

    @pl.when(jnp.logical_not(first))
    def _():
        ref[...] += s


def rms_fwd(x, gain, *, name, col_block=0, width=None, deps=()):
    T = x.shape[0]
    width = width or x.shape[1]
    tm = _tile(T, ROW_TILE, 8)

    def body(x_ref, g_ref, o_ref):
        v = x_ref[...]
        o_ref[...] = (v * _rstd(v) * g_ref[...]).astype(BF)

    return _pcall(body, name=name, grid=(T // tm,),
                  ins=[(x, (tm, width), lambda i: (i, col_block)), (gain, (1, width), lambda i: (0, 0))],
                  outs=[((T, width), BF, (tm, width), lambda i: (i, 0))], semantics=("parallel",), deps=deps)[0]


def rms_bwd(x, gain, dy, *, name, col_block=0, dres=None, want_f32=True, want_bf=True, into=None, deps=()):
    T, width = dy.shape
    tm = _tile(T, ROW_TILE, 8)
    has_res = dres is not None

    def body(*refs):
        x_ref, g_ref, dy_ref = refs[:3]
        pos = 3
        res_ref = None
        if has_res:
            res_ref = refs[pos]
            pos += 1
        if into is not None:
            pos += 1
        outs = refs[pos:]
        dx, dg_rows = _rms_bwd(x_ref[...], g_ref[...], dy_ref[...])
        if has_res:
            dx = dx + res_ref[...]
        o = 0
        if want_f32:
            outs[o][...] = dx
            o += 1
        if want_bf:
            outs[o][...] = dx.astype(BF)
            o += 1
        _acc_rows(outs[o], dg_rows, pl.program_id(0) == 0)

    ins = [(x, (tm, width), lambda i: (i, col_block)), (gain, (1, width), lambda i: (0, 0)),
           (dy, (tm, width), lambda i: (i, 0))]
    if has_res:
        ins.append((dres, (tm, width), lambda i: (i, 0)))
    outs = []
    aliases = {}
    if want_f32:
        outs.append(((T, width), F32, (tm, width), lambda i: (i, 0)))
    if want_bf and into is not None:
        ins.append((into, None, None))
        aliases[len(ins) - 1] = len(outs)
        outs.append((into.shape, BF, (tm, width), lambda i: (i, col_block)))
    elif want_bf:
        outs.append(((T, width), BF, (tm, width), lambda i: (i, 0)))
    outs.append(((1, width), F32, (1, width), lambda i: (0, 0)))
    return _pcall(body, name=name, grid=(T // tm,), ins=ins, outs=outs, aliases=aliases, deps=deps)


def mla_prep(proj, q_norm, kv_norm, cos_t, sin_t, *, name):
    T = proj.shape[0]
    tm = _tile(T, ROW_TILE, 8)
    kr_block = (proj.shape[1] - LANES) // LANES

    def body(cq_ref, ckv_ref, kr_ref, qg_ref, kg_ref, cos_ref, sin_ref, qn_ref, kvn_ref, krope_ref):
        cq = cq_ref[...]
        qn_ref[...] = (cq * _rstd(cq) * qg_ref[...]).astype(BF)
        ckv = ckv_ref[...]
        kvn_ref[...] = (ckv * _rstd(ckv) * kg_ref[...]).astype(BF)
        krope_ref[...] = _rope_fwd(kr_ref[...], cos_ref[...], sin_ref[...]).astype(BF)

    return _pcall(
        body, name=name, grid=(T // tm,),
        ins=[(proj, (tm, Q_LORA), lambda i: (i, 0)), (proj, (tm, KV_LORA), lambda i: (i, 1)),
             (proj, (tm, LANES), lambda i: (i, kr_block)),
             (q_norm, (1, Q_LORA), lambda i: (0, 0)), (kv_norm, (1, KV_LORA), lambda i: (0, 0)),
             (cos_t, (tm, LANES), lambda i: (i, 0)), (sin_t, (tm, LANES), lambda i: (i, 0))],
        outs=[((T, Q_LORA), BF, (tm, Q_LORA), lambda i: (i, 0)), ((T, KV_LORA), BF, (tm, KV_LORA), lambda i: (i, 0)),
              ((T, LANES), BF, (tm, LANES), lambda i: (i, 0))],
        semantics=("parallel",))


def _attn_scores(q, k_blk, diagonal):
    s = lax.dot_general(q, k_blk, (((1,), (1,)), ((), ())), preferred_element_type=F32) * ATTN_SCALE
    if diagonal:
        row = lax.broadcasted_iota(jnp.int32, s.shape, 0)
        col = lax.broadcasted_iota(jnp.int32, s.shape, 1)
        s = jnp.where(col <= row, s, -jnp.inf)
    return s


def attn_fwd(q, k, v, *, name):
    T = q.shape[0]
    tq = _tile(T, ATTN_TILE, 8)

    def body(q_ref, k_ref, v_ref, o_ref, lse_ref):
        i = pl.program_id(1)
        qv = q_ref[...]

        def block(kb, carry, diagonal):
            m, l, acc = carry
            start = pl.multiple_of(kb * tq, tq)
            s = _attn_scores(qv, k_ref[pl.ds(start, tq), :], diagonal)
            m_new = jnp.maximum(m, jnp.max(s, axis=-1, keepdims=True))
            alpha = jnp.exp(m - m_new)
            p = jnp.exp(s - m_new)
            l = alpha * l + jnp.sum(p, axis=-1, keepdims=True)
            acc = alpha * acc + jnp.dot(p.astype(BF), v_ref[pl.ds(start, tq), :], preferred_element_type=F32)
            return m_new, l, acc

        init = (jnp.full((tq, 1), -jnp.inf, F32), jnp.zeros((tq, 1), F32), jnp.zeros((tq, V_HEAD), F32))
        carry = lax.fori_loop(0, i, lambda kb, c: block(kb, c, False), init)
        m, l, acc = block(i, carry, True)
        o_ref[...] = acc / l
        lse_ref[...] = jnp.broadcast_to(m + jnp.log(l), (tq, V_HEAD))

    return _pcall(
        body, name=name, grid=(HEADS, T // tq),
        ins=[(q, (tq, HEAD_PAD), lambda h, i: (i, h)), (k, (T, HEAD_PAD), lambda h, i: (0, h)),
             (v, (T, V_HEAD), lambda h, i: (0, h))],
        outs=[((T, MLA_OUT), F32, (tq, V_HEAD), lambda h, i: (i, h)),
              ((T, MLA_OUT), F32, (tq, V_HEAD), lambda h, i: (i, h))], semantics=("parallel", "parallel"))


def attn_bwd(q, k, v, o, lse, do, *, name):
    T = q.shape[0]
    tq = _tile(T, ATTN_TILE, 8)

    def body(q_ref, k_ref, v_ref, o_ref, lse_ref, do_ref, dq_ref, dk_ref, dv_ref):
        i = pl.program_id(1)

        @pl.when(i == 0)
        def _():
            dk_ref[...] = jnp.zeros_like(dk_ref)
            dv_ref[...] = jnp.zeros_like(dv_ref)

        qv = q_ref[...]
        do_t = do_ref[...]
        lse_v = lse_ref[:, 0:1]
        delta = jnp.sum(do_t.astype(F32) * o_ref[...], axis=-1, keepdims=True)

        def block(kb, dq, diagonal):
            start = pl.multiple_of(kb * tq, tq)
            k_blk = k_ref[pl.ds(start, tq), :]
            v_blk = v_ref[pl.ds(start, tq), :]
            p = jnp.exp(_attn_scores(qv, k_blk, diagonal) - lse_v)
            dp = lax.dot_general(do_t, v_blk, (((1,), (1,)), ((), ())), preferred_element_type=F32)
            ds = (p * (dp - delta) * ATTN_SCALE).astype(BF)
            dk_ref[pl.ds(start, tq), :] += lax.dot_general(ds, qv, (((0,), (0,)), ((), ())), preferred_element_type=F32)
            dv_ref[pl.ds(start, tq), :] += lax.dot_general(p.astype(BF), do_t, (((0,), (0,)), ((), ())),
                                                          preferred_element_type=F32)
            return dq + jnp.dot(ds, k_blk, preferred_element_type=F32)

        dq = lax.fori_loop(0, i, lambda kb, c: block(kb, c, False), jnp.zeros((tq, HEAD_PAD), F32))
        dq_ref[...] = block(i, dq, True)

    return _pcall(
        body, name=name, grid=(HEADS, T // tq),
        ins=[(q, (tq, HEAD_PAD), lambda h, i: (i, h)), (k, (T, HEAD_PAD), lambda h, i: (0, h)),
             (v, (T, V_HEAD), lambda h, i: (0, h)), (o, (tq, V_HEAD), lambda h, i: (i, h)),
             (lse, (tq, V_HEAD), lambda h, i: (i, h)), (do, (tq, V_HEAD), lambda h, i: (i, h))],
        outs=[((T, HEADS * HEAD_PAD), F32, (tq, HEAD_PAD), lambda h, i: (i, h)),
              ((T, HEADS * HEAD_PAD), F32, (T, HEAD_PAD), lambda h, i: (0, h)),
              ((T, MLA_OUT), F32, (T, V_HEAD), lambda h, i: (0, h))],
        semantics=("parallel", "arbitrary"))


def mla_bwd_prep(dq, dk, dv, cos_t, sin_t, dproj, *, name):
    T = dq.shape[0]
    tm = _tile(T, ROW_TILE, 8)
    kr_block = (dproj.shape[1] - LANES) // LANES

    def body(dq_ref, dk_ref, dv_ref, cos_ref, sin_ref, dproj_in, dql_ref, dkvl_ref, dkr_ref):
        cos_v, sin_v = cos_ref[...], sin_ref[...]
        kr = jnp.zeros((tm, LANES), F32)
        for h in range(HEADS):
            lo = h * HEAD_PAD
            dql_ref[:, lo:lo + QK_NOPE] = dq_ref[:, lo:lo + QK_NOPE].astype(BF)
            dql_ref[:, lo + QK_NOPE:lo + HEAD_PAD] = _rope_bwd(
                dq_ref[:, lo + QK_NOPE:lo + HEAD_PAD], cos_v, sin_v).astype(BF)
            dkvl_ref[:, lo:lo + QK_NOPE] = dk_ref[:, lo:lo + QK_NOPE].astype(BF)
            dkvl_ref[:, lo + QK_NOPE:lo + HEAD_PAD] = dv_ref[:, h * V_HEAD:(h + 1) * V_HEAD].astype(BF)
            kr = kr + dk_ref[:, lo + QK_NOPE:lo + HEAD_PAD]
        dkr_ref[...] = _rope_bwd(kr, cos_v, sin_v).astype(BF)

    W = HEADS * HEAD_PAD
    return _pcall(
        body, name=name, grid=(T // tm,),
        ins=[(dq, (tm, W), lambda i: (i, 0)), (dk, (tm, W), lambda i: (i, 0)), (dv, (tm, MLA_OUT), lambda i: (i, 0)),
             (cos_t, (tm, LANES), lambda i: (i, 0)), (sin_t, (tm, LANES), lambda i: (i, 0)), (dproj, None, None)],
        outs=[((T, W), BF, (tm, W), lambda i: (i, 0)), ((T, W), BF, (tm, W), lambda i: (i, 0)),
              (dproj.shape, BF, (tm, LANES), lambda i: (i, kr_block))],
        aliases={5: 2}, semantics=("parallel",))


def _group_norm_stats(vg):
    mu = jnp.mean(vg, axis=-1, keepdims=True)
    d = vg - mu
    r = lax.rsqrt(jnp.mean(d * d, axis=-1, keepdims=True) + EPS)
    return d * r, r


def mix_fwd(a, proj, g_mla, g_sgu, v_gain, w_tril, b_full, *, name):
    T = a.shape[0]
    tm = _tile(T, ROW_TILE, CHUNK)
    n_chunk = tm // CHUNK

    def body(a_ref, u_ref, v_ref, gm_ref, gs_ref, vg_ref, w_ref, b_ref, o_ref, s_scr):
        av = a_ref[...]
        o_ref[:, :MLA_OUT] = (av * _rstd(av) * gm_ref[...]).astype(BF)
        for g in range(GROUPS):
            sl = slice(g * CH, (g + 1) * CH)
            vhat, _ = _group_norm_stats(_gelu(v_ref[:, sl]))
            vn = (vhat * vg_ref[:, sl]).astype(BF)
            u = _gelu(u_ref[:, sl])
            for ci in range(n_chunk):
                rs = slice(ci * CHUNK, (ci + 1) * CHUNK)
                y = jnp.dot(w_ref[g], vn[rs], preferred_element_type=F32) + b_ref[:, sl]
                s_scr[rs, sl] = u[rs] * y
        s = s_scr[...]
        o_ref[:, MLA_OUT:] = (s * _rstd(s) * gs_ref[...]).astype(BF)

    return _pcall(
        body, name=name, grid=(T // tm,),
        ins=[(a, (tm, MLA_OUT), lambda i: (i, 0)), (proj, (tm, SGU_OUT), lambda i: (i, 1)),
             (proj, (tm, SGU_OUT), lambda i: (i, 2)), (g_mla, (1, MLA_OUT), lambda i: (0, 0)),
             (g_sgu, (1, SGU_OUT), lambda i: (0, 0)), (v_gain, (1, SGU_OUT), lambda i: (0, 0)),
             (w_tril, (GROUPS, CHUNK, CHUNK), lambda i: (0, 0, 0)), (b_full, (CHUNK, SGU_OUT), lambda i: (0, 0))],
        outs=[((T, MLA_OUT + SGU_OUT), BF, (tm, MLA_OUT + SGU_OUT), lambda i: (i, 0))],
        scratch=[pltpu.VMEM((tm, SGU_OUT), F32)], semantics=("parallel",))[0]


def mix_bwd(dmixed, a, proj, g_mla, g_sgu, v_gain, w_tril, w_tril_t, b_full, *, name, deps=()):
    T = a.shape[0]
    tm = _tile(T, ROW_TILE, CHUNK)
    n_chunk = tm // CHUNK
    uv0 = Q_LORA + KV_LORA

    def body(dm_a_ref, dm_s_ref, a_ref, u_ref, v_ref, gm_ref, gs_ref, vg_ref, w_ref, wt_ref, b_ref,
             da_ref, duv_ref, dgm_ref, dgs_ref, dvg_ref, dw_ref, db_ref, s_scr, y_scr):
        first = pl.program_id(0) == 0
        duv_ref[:, :uv0] = jnp.zeros((tm, uv0), BF)
        duv_ref[:, uv0 + 2 * SGU_OUT:] = jnp.zeros((tm, duv_ref.shape[1] - uv0 - 2 * SGU_OUT), BF)
        da, dgm_rows = _rms_bwd(a_ref[...], gm_ref[...], dm_a_ref[...])
        da_ref[...] = da.astype(BF)
        _acc_rows(dgm_ref, dgm_rows, first)

        for g in range(GROUPS):
            sl = slice(g * CH, (g + 1) * CH)
            vhat, _ = _group_norm_stats(_gelu(v_ref[:, sl]))
            vn = (vhat * vg_ref[:, sl]).astype(BF)
            u = _gelu(u_ref[:, sl])
            for ci in range(n_chunk):
                rs = slice(ci * CHUNK, (ci + 1) * CHUNK)
                y = jnp.dot(w_ref[g], vn[rs], preferred_element_type=F32) + b_ref[:, sl]
                y_scr[rs, sl] = y
                s_scr[rs, sl] = u[rs] * y
        ds, dgs_rows = _rms_bwd(s_scr[...], gs_ref[...], dm_s_ref[...])
        _acc_rows(dgs_ref, dgs_rows, first)
        s_scr[...] = ds

        @pl.when(first)
        def _():
            dw_ref[...] = jnp.zeros_like(dw_ref)
            db_ref[...] = jnp.zeros_like(db_ref)

        for g in range(GROUPS):
            sl = slice(g * CH, (g + 1) * CH)
            upre = u_ref[:, sl]
            vpre = v_ref[:, sl]
            u = _gelu(upre)
            vhat, r = _group_norm_stats(_gelu(vpre))
            gain = vg_ref[:, sl]
            vn = (vhat * gain).astype(BF)
            dsg = s_scr[:, sl]
            duv_ref[:, uv0 + g * CH:uv0 + (g + 1) * CH] = (dsg * y_scr[:, sl] * _gelu_grad(upre)).astype(BF)
            dy = dsg * u
            dyb = dy.astype(BF)
            dvn_parts = []
            for ci in range(n_chunk):
                rs = slice(ci * CHUNK, (ci + 1) * CHUNK)
                dvn_parts.append(jnp.dot(wt_ref[g], dyb[rs], preferred_element_type=F32))
                dw_ref[g] += lax.dot_general(dyb[rs], vn[rs], (((1,), (1,)), ((), ())), preferred_element_type=F32)
                db_ref[:, sl] += jnp.broadcast_to(jnp.sum(dy[rs], axis=-1, keepdims=True), (CHUNK, CH))
            dvn = dvn_parts[0] if n_chunk == 1 else jnp.concatenate(dvn_parts, axis=0)
            _acc_rows(dvg_ref.at[:, sl], dvn * vhat, first)
            dvh = dvn * gain
            dvg = r * (dvh - jnp.mean(dvh, axis=-1, keepdims=True)
                       - vhat * jnp.mean(dvh * vhat, axis=-1, keepdims=True))
            duv_ref[:, uv0 + SGU_OUT + g * CH:uv0 + SGU_OUT + (g + 1) * CH] = (dvg * _gelu_grad(vpre)).astype(BF)

    return _pcall(
        body, name=name, grid=(T // tm,),
        ins=[(dmixed, (tm, MLA_OUT), lambda i: (i, 0)), (dmixed, (tm, SGU_OUT), lambda i: (i, 1)),
             (a, (tm, MLA_OUT), lambda i: (i, 0)), (proj, (tm, SGU_OUT), lambda i: (i, 1)),
             (proj, (tm, SGU_OUT), lambda i: (i, 2)), (g_mla, (1, MLA_OUT), lambda i: (0, 0)),
             (g_sgu, (1, SGU_OUT), lambda i: (0, 0)), (v_gain, (1, SGU_OUT), lambda i: (0, 0)),
             (w_tril, (GROUPS, CHUNK, CHUNK), lambda i: (0, 0, 0)), (w_tril_t, (GROUPS, CHUNK, CHUNK), lambda i: (0, 0, 0)),
             (b_full, (CHUNK, SGU_OUT), lambda i: (0, 0))],
        outs=[((T, MLA_OUT), BF, (tm, MLA_OUT), lambda i: (i, 0)),
              ((T, proj.shape[1]), BF, (tm, proj.shape[1]), lambda i: (i, 0)),
              ((1, MLA_OUT), F32, (1, MLA_OUT), lambda i: (0, 0)), ((1, SGU_OUT), F32, (1, SGU_OUT), lambda i: (0, 0)),
              ((1, SGU_OUT), F32, (1, SGU_OUT), lambda i: (0, 0)),
              ((GROUPS, CHUNK, CHUNK), F32, (GROUPS, CHUNK, CHUNK), lambda i: (0, 0, 0)),
              ((CHUNK, SGU_OUT), F32, (CHUNK, SGU_OUT), lambda i: (0, 0))],
        scratch=[pltpu.VMEM((tm, SGU_OUT), F32), pltpu.VMEM((tm, SGU_OUT), F32)], deps=deps)


def _shift_down(z, n, row):
    return jnp.where(row >= n, pltpu.roll(z, n, 0), 0.0)


def _shift_up(z, n, row, T):
    return jnp.where(row < T - n, pltpu.roll(z, T - n, 0), 0.0)


def conv_fwd(proj, conv_w, *, name):
    T, D3 = proj.shape
    D = D3 // 3
    tn = _tile(D, 256)
    nj = D // tn

    def body(b_ref, c_ref, x_ref, w_ref, o_ref):
        row = lax.broadcasted_iota(jnp.int32, (T, tn), 0)
        z = c_ref[...] * x_ref[...]
        zc = w_ref[2:3, :] * z + w_ref[1:2, :] * _shift_down(z, 1, row) + w_ref[0:1, :] * _shift_down(z, 2, row)
        o_ref[...] = (b_ref[...] * zc).astype(BF)

    return _pcall(
        body, name=name, grid=(nj,),
        ins=[(proj, (T, tn), lambda j: (0, j)), (proj, (T, tn), lambda j: (0, nj + j)),
             (proj, (T, tn), lambda j: (0, 2 * nj + j)), (conv_w, (3, tn), lambda j: (0, j))],
        outs=[((T, D), BF, (T, tn), lambda j: (0, j))], semantics=("parallel",))[0]


def conv_bwd(dg, proj, conv_w, *, name, deps=()):
    T, D3 = proj.shape
    D = D3 // 3
    tn = _tile(D, 256)
    nj = D // tn

    def body(dg_ref, b_ref, c_ref, x_ref, w_ref, dp_ref, dw_ref, dc_scr, dx_scr):
        part = pl.program_id(1)

        @pl.when(part == 0)
        def _():
            row = lax.broadcasted_iota(jnp.int32, (T, tn), 0)
            c, x = c_ref[...], x_ref[...]
            z = c * x
            z1 = _shift_down(z, 1, row)
            z2 = _shift_down(z, 2, row)
            dgv = dg_ref[...]
            zc = w_ref[2:3, :] * z + w_ref[1:2, :] * z1 + w_ref[0:1, :] * z2
            dp_ref[...] = (dgv * zc).astype(BF)
            dzc = dgv * b_ref[...]
            dw_ref[0:1, :] = jnp.sum(dzc * z2, axis=0, keepdims=True)
            dw_ref[1:2, :] = jnp.sum(dzc * z1, axis=0, keepdims=True)
            dw_ref[2:3, :] = jnp.sum(dzc * z, axis=0, keepdims=True)
            dz = (w_ref[2:3, :] * dzc + w_ref[1:2, :] * _shift_up(dzc, 1, row, T)
                  + w_ref[0:1, :] * _shift_up(dzc, 2, row, T))
            dc_scr[...] = (dz * x).astype(BF)
            dx_scr[...] = (dz * c).astype(BF)

        @pl.when(part == 1)
        def _():
            dp_ref[...] = dc_scr[...]

        @pl.when(part == 2)
        def _():
            dp_ref[...] = dx_scr[...]

    return _pcall(
        body, name=name, grid=(nj, 3),
        ins=[(dg, (T, tn), lambda j, p: (0, j)), (proj, (T, tn), lambda j, p: (0, j)),
             (proj, (T, tn), lambda j, p: (0, nj + j)), (proj, (T, tn), lambda j, p: (0, 2 * nj + j)),
             (conv_w, (3, tn), lambda j, p: (0, j))],
        outs=[((T, D3), BF, (T, tn), lambda j, p: (0, p * nj + j)), ((3, D), F32, (3, tn), lambda j, p: (0, j))],
        scratch=[pltpu.VMEM((T, tn), BF), pltpu.VMEM((T, tn), BF)], semantics=("parallel", "arbitrary"), deps=deps)


def loss_bwd(x_parts, gain, target, *, name):
    T, D = target.shape
    tm = _tile(T, ROW_TILE, 8)
    n_x = len(x_parts)

    def body(*refs):
        x_refs = refs[:n_x]
        g_ref, t_ref, dx_ref, dxb_ref, dg_ref, loss_ref = refs[n_x:]
        first = pl.program_id(0) == 0
        xv = jnp.concatenate([r[...] for r in x_refs], axis=-1) if n_x > 1 else x_refs[0][...]
        r = _rstd(xv)
        xh = xv * r
        gain_v = g_ref[...]
        err = xh * gain_v - t_ref[...]
        part = 0.5 * jnp.sum(jnp.mean(err * err, axis=-1, keepdims=True), axis=0, keepdims=True)
        _acc_rows(loss_ref, jnp.broadcast_to(part, (1, LANES)), first)
        dy = err * (1.0 / D)
        gdy = dy * gain_v
        dx = r * (gdy - xh * jnp.mean(gdy * xh, axis=-1, keepdims=True))
        dx_ref[...] = dx
        dxb_ref[...] = dx.astype(BF)
        _acc_rows(dg_ref, dy * xh, first)

    return _pcall(
        body, name=name, grid=(T // tm,),
        ins=[(p, (tm, D // n_x), lambda i: (i, 0)) for p in x_parts]
        + [(gain, (1, D), lambda i: (0, 0)), (target, (tm, D), lambda i: (i, 0))],
        outs=[((T, D), F32, (tm, D), lambda i: (i, 0)), ((T, D), BF, (tm, D), lambda i: (i, 0)),
              ((1, D), F32, (1, D), lambda i: (0, 0)), ((1, LANES), F32, (1, LANES), lambda i: (0, 0))])


def _adamw(g, w, m, v):
    m = ADAM_B1 * m + (1.0 - ADAM_B1) * g
    v = ADAM_B2 * v + (1.0 - ADAM_B2) * (g * g)
    m_hat = m / ADAM_C1
    v_hat = v / ADAM_C2
    delta = -ADAM_LR * (m_hat / (jnp.sqrt(v_hat) + ADAM_EPS) + ADAM_WD * w)
    return delta, m, v


def adam_flat(g, w, m, v, *, name):
    def body(g_ref, w_ref, m_ref, v_ref, d_ref, nm_ref, nv_ref):
        d, nm, nv = _adamw(g_ref[...], w_ref[...], m_ref[...], v_ref[...])
        d_ref[...] = d
        nm_ref[...] = nm
        nv_ref[...] = nv

    blk = g.shape
    zero = lambda: (0, 0)
    return _pcall(body, name=name, grid=(),
                  ins=[(t, blk, zero) for t in (g, w, m, v)],
                  outs=[(blk, F32, blk, zero)] * 3)


def _chip_slots():
    x, y, c = lax.axis_index("x"), lax.axis_index("y"), lax.axis_index("c")
    chips = [(1 - x, y), (x, 1 - y), (1 - x, 1 - y)]
    return x, y, c, chips


def device_index():
    x, y, c, chips = _chip_slots()
    return jnp.stack([4 * x + 2 * y + c, 2 * x + y] + [4 * cx + 2 * cy + c for cx, cy in chips]
                     + [2 * cx + cy for cx, cy in chips]).astype(jnp.int32)


def _job_rows(R, C, n_steps):
    if n_steps is None:
        n_steps = max(1, R * C // STREAM_BLOCK_ELEMS)
    n_blk = max([d for d in range(1, n_steps + 1) if R % d == 0 and (R // d) % 16 == 0] or [1])
    return R // n_blk, n_blk


def run_job(job, *, index, name, deps=()):
    jb = job(None)
    n_in = len(jb["ins"])

    def body(idx_ref, *refs):
        jb["fn"](refs[:n_in], refs[n_in:n_in + len(jb["outs"])])

    return _pcall(body, name=name, grid=(jb["n_blk"],), ins=jb["ins"], outs=jb["outs"], prefetch=index,
                  aliases={1 + a: o for a, o in jb["aliases"].items()}, semantics=("parallel",), deps=deps)


def adam_job(gs, a_buf, b_buf, w, m, v, layer, prev):
    L, R, C = w.shape

    def build(n_steps):
        tr, n_blk = _job_rows(R, C, n_steps)
        blk = (None, tr, C)
        row = lambda t: jnp.minimum(t, n_blk - 1)
        ins = [(gs, blk, lambda t, s: (s[0], row(t), 0)), (a_buf, blk, lambda t, s: (s[1], row(t), 0))]
        ins += [(b_buf, blk, lambda t, s, j=j: (j, row(t), 0)) for j in range(3)]
        ins += [(p, blk, lambda t, s: (layer, row(t), 0)) for p in (w, m, v)]
        ins += [(p, None, None) for p in (prev or [])]

        def fn(i, o):
            g = ((((i[0][...].astype(F32) + i[1][...].astype(F32)) + i[2][...].astype(F32))
                  + i[3][...].astype(F32)) + i[4][...].astype(F32))
            d, nm, nv = _adamw(g, i[5][...], i[6][...], i[7][...])
            o[0][...] = g
            o[1][...] = d
            o[2][...] = nm
            o[3][...] = nv

        return dict(ins=ins, outs=[((L, R, C), F32, blk, lambda t, s: (layer, row(t), 0))] * 4, fn=fn,
                    aliases={8 + o: o for o in range(4)} if prev else {}, n_blk=n_blk)

    return build


def pair_job(gs, a_buf):
    _, R, C = gs.shape

    def build(n_steps):
        tr, n_blk = _job_rows(R, C, n_steps)
        blk = (None, tr, C)
        row = lambda t: jnp.minimum(t, n_blk - 1)
        ins = [(gs, blk, lambda t, s, j=j: (s[2 + j], row(t), 0)) for j in range(3)]
        ins += [(a_buf, blk, lambda t, s, j=j: (s[5 + j], row(t), 0)) for j in range(3)]

        def fn(i, o):
            for j in range(3):
                o[0][j] = (i[j][...].astype(F32) + i[3 + j][...].astype(F32)).astype(BF)

        return dict(ins=ins, outs=[((3, R, C), BF, (3, tr, C), lambda t, s: (0, row(t), 0))], fn=fn, aliases={},
                    n_blk=n_blk)

    return build


def reduce_sum(gs, a_buf, b_buf, *, name):
    _, R, C = gs.shape
    tr = _tile(R, 256, 16)
    x, y, c, _ = _chip_slots()
    idx = jnp.stack([4 * x + 2 * y + c, 2 * x + y]).astype(jnp.int32)

    def body(idx_ref, g_ref, a_ref, b0_ref, b1_ref, b2_ref, o_ref):
        o_ref[...] = ((((g_ref[...].astype(F32) + a_ref[...].astype(F32)) + b0_ref[...].astype(F32))
                       + b1_ref[...].astype(F32)) + b2_ref[...].astype(F32))

    blk3 = (None, tr, C)
    return _pcall(body, name=name, grid=(R // tr,),
                  ins=[(gs, blk3, lambda i, s: (s[0], i, 0)), (a_buf, blk3, lambda i, s: (s[1], i, 0)),
                       (b_buf, blk3, lambda i, s: (0, i, 0)), (b_buf, blk3, lambda i, s: (1, i, 0)),
                       (b_buf, blk3, lambda i, s: (2, i, 0))],
                  outs=[((R, C), F32, (tr, C), lambda i, s: (i, 0))], prefetch=idx, semantics=("parallel",))[0]


def adam_rows(g, w, m, v, *, name):
    R, C = g.shape
    tr = _tile(R, 256, 8)

    def body(g_ref, w_ref, m_ref, v_ref, d_ref, nm_ref, nv_ref):
        d, nm, nv = _adamw(g_ref[...], w_ref[...], m_ref[...], v_ref[...])
        d_ref[...] = d
        nm_ref[...] = nm
        nv_ref[...] = nv

    spec = ((tr, C), lambda i: (i, 0))
    return _pcall(body, name=name, grid=(R // tr,), ins=[(t, *spec) for t in (g, w, m, v)],
                  outs=[((R, C), F32, *spec)] * 3, semantics=("parallel",))


def sum_rows8(gathered, rows, *, name):
    W = gathered.shape[1]

    def body(g_ref, o_ref):
        acc = g_ref[0:rows, :]
        for d in range(1, N_DEV):
            acc = acc + g_ref[d * rows:(d + 1) * rows, :]
        o_ref[...] = acc

    return _pcall(body, name=name, grid=(), ins=[(gathered, gathered.shape, lambda: (0, 0))],
                  outs=[((rows, W), F32, (rows, W), lambda: (0, 0))])[0]


HBM_SPEC = pl.BlockSpec(memory_space=pltpu.HBM)
SEM_SPEC = pl.BlockSpec(memory_space=pltpu.SEMAPHORE)
ANY_SPEC = pl.BlockSpec(memory_space=pl.ANY)
DATAFLOW = pltpu.SideEffectType.DATAFLOW_SIDE_EFFECTING


def _in_hbm(v):
    return pltpu.with_memory_space_constraint(v, pltpu.HBM)


def _slot(p):
    return 4 * p[0] + 2 * p[1] + p[2]


def _gather_peers():
    x, y, c, chips = _chip_slots()
    return (x, y, c), [(x, y, 1 - c)] + [(*chip, c) for chip in chips]


def gather_start(groups, after, *, name):
    flat = [s for g in groups for s in g]
    n, n_g = len(flat), len(groups)
    where = [(gi, ti) for gi, g in enumerate(groups) for ti in range(len(g))]

    def body(*refs):
        src, land = refs[:n], refs[n:2 * n]
        sems = refs[2 * n + 1:2 * n + 1 + 2 * n_g]
        me, peers = _gather_peers()
        for t in range(n):
            gi, ti = where[t]
            for k, to in enumerate(peers):
                pltpu.make_async_remote_copy(
                    src_ref=src[t], dst_ref=land[t].at[_slot(me)], send_sem=sems[2 * gi].at[4 * ti + k],
                    recv_sem=sems[2 * gi + 1].at[4 * ti + k], device_id=to, device_id_type=MESH).start()
        refs[-1][...] = jnp.zeros_like(refs[-1])

    out_shape = []
    for g in groups:
        out_shape += [pltpu.SemaphoreType.DMA((4 * len(g),)), pltpu.SemaphoreType.DMA((4 * len(g),))]
    out_shape += [pltpu.HBM(s.shape, s.dtype) for s in flat]
    out_shape += [pltpu.HBM((N_DEV,) + s.shape, s.dtype) for s in flat]
    out_shape += [jax.ShapeDtypeStruct((8, LANES), F32)]
    aliases = {t: 2 * n_g + t for t in range(n)}
    aliases.update({n + t: 2 * n_g + n + t for t in range(n)})
    res = pl.pallas_call(
        body, name=name, out_shape=out_shape, in_specs=[HBM_SPEC] * (2 * n) + [ANY_SPEC],
        out_specs=[SEM_SPEC] * (2 * n_g) + [HBM_SPEC] * (2 * n) + [pl.BlockSpec(memory_space=pltpu.VMEM)],
        input_output_aliases=aliases, compiler_params=pltpu.CompilerParams(has_side_effects=DATAFLOW),
    )(*[_in_hbm(s) for s in flat], *[_in_hbm(lax.empty((N_DEV,) + s.shape, s.dtype)) for s in flat], after)
    out, off = [], 0
    for gi, g in enumerate(groups):
        k = len(g)
        out.append((res[2 * gi], res[2 * gi + 1], res[2 * n_g + off:2 * n_g + off + k],
                    res[2 * n_g + n + off:2 * n_g + n + off + k]))
        off += k
    return out, res[-1]


def gather_wait(started, after, *, name):
    send_sems, recv_sems, srcs, lands = started
    n = len(srcs)
    after = list(after)

    def body(*refs):
        src, land = refs[:n], refs[n:2 * n]
        send, recv = refs[2 * n], refs[2 * n + 1]
        _, peers = _gather_peers()
        for t in range(n):
            for k, frm in enumerate(peers):
                cp = pltpu.make_async_remote_copy(
                    src_ref=src[t], dst_ref=land[t].at[_slot(frm)], send_sem=send.at[4 * t + k],
                    recv_sem=recv.at[4 * t + k],
                    device_id=frm, device_id_type=MESH)
                cp.wait_send()
                cp.wait_recv()

    res = pl.pallas_call(
        body, name=name,
        out_shape=[pltpu.HBM(s.shape, s.dtype) for s in srcs] + [pltpu.HBM(l.shape, l.dtype) for l in lands],
        in_specs=[HBM_SPEC] * (2 * n) + [SEM_SPEC, SEM_SPEC] + [ANY_SPEC] * len(after),
        out_specs=[HBM_SPEC] * (2 * n), input_output_aliases={t: t for t in range(2 * n)},
        compiler_params=pltpu.CompilerParams(has_side_effects=DATAFLOW),
    )(*srcs, *lands, send_sems, recv_sems, *after)
    return res[:n], res[n:]


def place_own(src, land, *, name):
    R, C = src.shape
    tr = _tile(R, 512, 16)
    x, y, c, _ = _chip_slots()
    idx = jnp.stack([4 * x + 2 * y + c]).astype(jnp.int32)

    def body(idx_ref, s_ref, land_ref, o_ref):
        o_ref[...] = s_ref[...]

    return _pcall(body, name=name, grid=(R // tr,),
                  ins=[(src, (tr, C), lambda i, s: (i, 0)), (land, None, None)],
                  outs=[(land.shape, land.dtype, (None, tr, C), lambda i, s: (s[0], i, 0))],
                  prefetch=idx, aliases={2: 0}, semantics=("parallel",))[0]


def gather_finish(srcs, lands, *, name):
    n = len(srcs)

    def body(*refs):
        land = refs[n:2 * n]
        send_sems, recv_sems = refs[2 * n:]
        x, y, c, chips = _chip_slots()
        me, sibling = (x, y, c), (x, y, 1 - c)

        def copy(t, j, block, to):
            return pltpu.make_async_remote_copy(
                src_ref=land[t].at[_slot(block)], dst_ref=land[t].at[_slot(block)], send_sem=send_sems.at[t, j],
                recv_sem=recv_sems.at[t, j], device_id=to, device_id_type=MESH)

        sends = [copy(t, j, (*chip, c), sibling) for t in range(n) for j, chip in enumerate(chips)]
        for cp in sends:
            cp.start()
        for t in range(n):
            for j, chip in enumerate(chips):
                copy(t, j, (*chip, 1 - c), me).wait_recv()
        for cp in sends:
            cp.wait_send()

    passed = pl.pallas_call(
        body, name=name, out_shape=[jax.ShapeDtypeStruct(l.shape, l.dtype) for l in lands],
        in_specs=[ANY_SPEC] * n, out_specs=[ANY_SPEC] * n,
        input_output_aliases={t: t for t in range(n)},
        scratch_shapes=[pltpu.SemaphoreType.DMA((n, 3)), pltpu.SemaphoreType.DMA((n, 3))],
    )(*lands)
    return [place_own(s, l, name=f"{name}_own{t}") for t, (s, l) in enumerate(zip(srcs, passed))]


def chips_start(pairs, *, name):
    n = len(pairs)

    def body(*refs):
        src, land = refs[:n], refs[n:2 * n]
        send, recv = refs[2 * n], refs[2 * n + 1]
        token = refs[-1]
        x, y, c, chips = _chip_slots()
        for t in range(n):
            for j, chip in enumerate(chips):
                pltpu.make_async_remote_copy(
                    src_ref=src[t].at[j], dst_ref=land[t].at[j], send_sem=send.at[3 * t + j],
                    recv_sem=recv.at[3 * t + j], device_id=(*chip, c), device_id_type=MESH).start()
        token[...] = jnp.zeros_like(token)

    res = pl.pallas_call(
        body, name=name,
        out_shape=[pltpu.SemaphoreType.DMA((3 * n,)), pltpu.SemaphoreType.DMA((3 * n,))]
        + [pltpu.HBM(p.shape, p.dtype) for p in pairs] * 2 + [jax.ShapeDtypeStruct((8, LANES), F32)],
        in_specs=[HBM_SPEC] * (2 * n),
        out_specs=[SEM_SPEC, SEM_SPEC] + [HBM_SPEC] * (2 * n) + [pl.BlockSpec(memory_space=pltpu.VMEM)],
        input_output_aliases={t: 2 + t for t in range(2 * n)},
        compiler_params=pltpu.CompilerParams(has_side_effects=DATAFLOW),
    )(*[_in_hbm(p) for p in pairs], *[_in_hbm(lax.empty(p.shape, p.dtype)) for p in pairs])
    return res[0], res[1], res[2:2 + n], res[2 + n:2 + 2 * n], res[-1]


def chips_wait(started, after, *, name):
    send_sems, recv_sems, srcs, lands, _ = started
    n = len(srcs)

    def body(*refs):
        src, land = refs[:n], refs[n:2 * n]
        send, recv = refs[2 * n], refs[2 * n + 1]
        x, y, c, chips = _chip_slots()
        for t in range(n):
            for j, chip in enumerate(chips):
                cp = pltpu.make_async_remote_copy(
                    src_ref=src[t].at[j], dst_ref=land[t].at[j], send_sem=send.at[3 * t + j],
                    recv_sem=recv.at[3 * t + j], device_id=(*chip, c), device_id_type=MESH)
                cp.wait_send()
                cp.wait_recv()

    res = pl.pallas_call(
        body, name=name, out_shape=[pltpu.HBM(s.shape, s.dtype) for s in srcs] * 2,
        in_specs=[HBM_SPEC] * (2 * n) + [SEM_SPEC, SEM_SPEC, ANY_SPEC], out_specs=[HBM_SPEC] * (2 * n),
        input_output_aliases={t: t for t in range(2 * n)},
        compiler_params=pltpu.CompilerParams(has_side_effects=DATAFLOW),
    )(*srcs, *lands, send_sems, recv_sems, after)
    return res[n:]


def _sibling_copies(src, land, send, recv, n):
    x, y, c, _ = _chip_slots()
    return [pltpu.make_async_remote_copy(
        src_ref=src[t].at[4 * (q // 2) + 2 * (q % 2) + (1 - c)], dst_ref=land[t].at[q], send_sem=send.at[4 * t + q],
        recv_sem=recv.at[4 * t + q], device_id=(x, y, 1 - c), device_id_type=MESH)
        for t in range(n) for q in range(4)]


def sibling_start(gs, *, name):
    n = len(gs)

    def body(*refs):
        for cp in _sibling_copies(refs[:n], refs[n:2 * n], refs[2 * n], refs[2 * n + 1], n):
            cp.start()
        refs[-1][...] = jnp.zeros_like(refs[-1])

    lands = [lax.empty((4,) + g.shape[1:], g.dtype) for g in gs]
    res = pl.pallas_call(
        body, name=name,
        out_shape=[pltpu.SemaphoreType.DMA((4 * n,)), pltpu.SemaphoreType.DMA((4 * n,))]
        + [pltpu.HBM(g.shape, g.dtype) for g in gs] + [pltpu.HBM(l.shape, l.dtype) for l in lands]
        + [jax.ShapeDtypeStruct((8, LANES), F32)],
        in_specs=[HBM_SPEC] * (2 * n),
        out_specs=[SEM_SPEC, SEM_SPEC] + [HBM_SPEC] * (2 * n) + [pl.BlockSpec(memory_space=pltpu.VMEM)],
        input_output_aliases={t: 2 + t for t in range(2 * n)},
        compiler_params=pltpu.CompilerParams(has_side_effects=DATAFLOW),
    )(*[_in_hbm(g) for g in gs], *[_in_hbm(l) for l in lands])
    return res[0], res[1], res[2:2 + n], res[2 + n:2 + 2 * n], res[-1]


def sibling_wait(started, after, *, name):
    send_sems, recv_sems, srcs, lands, _ = started
    n = len(srcs)

    def body(*refs):
        for cp in _sibling_copies(refs[:n], refs[n:2 * n], refs[2 * n], refs[2 * n + 1], n):
            cp.wait_send()
            cp.wait_recv()

    res = pl.pallas_call(
        body, name=name,
        out_shape=[pltpu.HBM(s.shape, s.dtype) for s in srcs] + [pltpu.HBM(l.shape, l.dtype) for l in lands],
        in_specs=[HBM_SPEC] * (2 * n) + [SEM_SPEC, SEM_SPEC, ANY_SPEC], out_specs=[HBM_SPEC] * (2 * n),
        input_output_aliases={t: t for t in range(2 * n)},
        compiler_params=pltpu.CompilerParams(has_side_effects=DATAFLOW),
    )(*srcs, *lands, send_sems, recv_sems, after)
    return res[:n], res[n:]


def all_gather_vmem(x_shard, *, name, after=None):
    m_per, n = x_shard.shape
    n_after = 0 if after is None else 1

    def body(x_ref, *rest):
        out_ref, send_sems, recv_sems, local_sem = rest[n_after:]
        x, y, c, chips = _chip_slots()
        me, sibling = (x, y, c), (x, y, 1 - c)

        def rows(px, py, pc):
            return out_ref.at[pl.ds((4 * px + 2 * py + pc) * m_per, m_per), :]

        def copy(k, block, to, src=None):
            return pltpu.make_async_remote_copy(
                src_ref=rows(*block) if src is None else src, dst_ref=rows(*block),
                send_sem=send_sems.at[k], recv_sem=recv_sems.at[k], device_id=to, device_id_type=MESH)

        mine = pltpu.make_async_copy(x_ref, rows(*me), local_sem)
        mine.start()
        first = [copy(0, me, sibling, src=x_ref)]
        first += [copy(1 + j, me, (*chip, c), src=x_ref) for j, chip in enumerate(chips)]
        for cp in first:
            cp.start()
        passed = [copy(4 + j, (*chip, c), sibling) for j, chip in enumerate(chips)]
        for j, chip in enumerate(chips):
            copy(1 + j, (*chip, c), me).wait_recv()
            passed[j].start()
        copy(0, sibling, me).wait_recv()
        for j, chip in enumerate(chips):
            copy(4 + j, (*chip, 1 - c), me).wait_recv()
        for cp in first + passed:
            cp.wait_send()
        mine.wait()

    vmem = pl.BlockSpec(memory_space=pltpu.VMEM)
    return pl.pallas_call(
        body, name=name, out_shape=jax.ShapeDtypeStruct((N_DEV * m_per, n), x_shard.dtype),
        in_specs=[vmem] + [ANY_SPEC] * n_after, out_specs=vmem,
        scratch_shapes=[pltpu.SemaphoreType.DMA((7,)), pltpu.SemaphoreType.DMA((7,)), pltpu.SemaphoreType.DMA],
        compiler_params=pltpu.CompilerParams(vmem_limit_bytes=int(min(
            VMEM_LIMIT_CAP, 2 * (N_DEV + 1) * m_per * n * x_shard.dtype.itemsize + 16 * 2 ** 20))),
    )(x_shard, *([] if after is None else [after]))


def _rope_slab(cols):
    z = jnp.zeros(cols.shape[:-1] + (HALF_ROPE,), cols.dtype)
    return jnp.concatenate([cols[..., :HALF_ROPE], z, cols[..., HALF_ROPE:], z], axis=-1)


def _rope_unslab(slab):
    return jnp.concatenate([slab[..., :HALF_ROPE], slab[..., 2 * HALF_ROPE:3 * HALF_ROPE]], axis=-1)


def _pack_w_in_t(wt_g):
    s, c, d = wt_g.shape
    w = wt_g.reshape(s * c, d)
    c2, c3 = Q_LORA + KV_LORA, Q_LORA + KV_LORA + QK_ROPE
    r = w[c2:c3]
    z = jnp.zeros((HALF_ROPE, d), w.dtype)
    return jnp.concatenate([w[:c2], w[c3:], r[:HALF_ROPE], z, r[HALF_ROPE:], z], axis=0)


def unpack_w_in_t_grad(dwt, *, name):
    n_rows, d = dwt.shape
    kr = QK_ROPE
    n_out_rows = n_rows - kr
    blk = n_out_rows // 7
    assert blk * 7 == n_out_rows and blk % kr == 0 and n_rows % (2 * kr) == 0
    kr_row = Q_LORA + KV_LORA
    k_mix = kr_row // blk
    off = kr_row - k_mix * blk
    slab_block = (n_rows - 2 * kr) // (2 * kr)

    def body(prev_ref, in_ref, slab_ref, o_ref):
        k = pl.program_id(0)

        @pl.when(k < k_mix)
        def _():
            o_ref[...] = in_ref[...]

        @pl.when(k == k_mix)
        def _():
            o_ref[:off, :] = in_ref[:off, :]
            o_ref[off:off + HALF_ROPE, :] = slab_ref[:HALF_ROPE, :]
            o_ref[off + HALF_ROPE:off + kr, :] = slab_ref[2 * HALF_ROPE:3 * HALF_ROPE, :]
            o_ref[off + kr:, :] = in_ref[off:blk - kr, :]

        @pl.when(k > k_mix)
        def _():
            o_ref[:kr, :] = prev_ref[blk - kr:, :]
            o_ref[kr:, :] = in_ref[:blk - kr, :]

    out = _pcall(body, name=name, grid=(7,),
                 ins=[(dwt, (blk, d), lambda k: (jnp.maximum(k - 1, 0), 0)), (dwt, (blk, d), lambda k: (k, 0)),
                      (dwt, (2 * kr, d), lambda k: (slab_block, 0))],
                 outs=[((n_out_rows, d), dwt.dtype, (blk, d), lambda k: (k, 0))], semantics=("parallel",))[0]
    return out.reshape(N_DEV, n_out_rows // N_DEV, d)


def _rope_tables(positions):
    inv_freq = ROPE_BASE ** (-jnp.arange(0, QK_ROPE, 2, dtype=F32) / QK_ROPE)
    ang = positions.astype(F32)[:, None] * inv_freq
    cos, sin = jnp.cos(ang), jnp.sin(ang)
    z = jnp.zeros_like(cos)
    return jnp.concatenate([cos, z, cos, z], axis=-1), jnp.concatenate([-sin, z, sin, z], axis=-1)


def _mlp_up(x, gain, w1, tag):
    hn = rms_fwd(x, gain, name=f"mlp{tag}_norm")

    def act_epi(acc):
        a = jnp.maximum(acc, 0.0)
        return a, a * a

    T = x.shape[0]
    F = w1.shape[0] * w1.shape[2]
    a, act = mm(hn, w1, name=f"mlp{tag}_up", outs=[((T, F), BF, None), ((T, F), BF, None)], epi=act_epi)
    return hn, a, act


def _mlp_down(x, act, w2, tag, part=0):
    n = w2.shape[1]
    bm = _tile(x.shape[0], MM_TILE)
    bn = _tile(n, MM_TILE)
    per = n // bn
    return mm(act, w2, name=f"mlp{tag}_down{part}", out=((x.shape[0], n), F32), bm=bm, bn=bn,
              epi=lambda acc, r: (acc + r[...],), epi_ins=[(x, (bm, bn), lambda i, j, k: (i, part * per + j))])


def _mlp_bwd_weights(w1, w2, saved, dxb, tag):
    hn, a, act = saved
    T, D = dxb.shape
    F = a.shape[1]
    bm = _tile(T, MM_TILE)
    bn = _tile(F, min(MM_TILE, w1.shape[2]))
    dhid = mm(dxb, w2, tb=True, name=f"mlp{tag}_dhid", out=((T, F), BF), bm=bm, bn=bn,
              epi=lambda acc, a_ref: (2.0 * a_ref[...].astype(F32) * acc,),
              epi_ins=[(a, (bm, bn), lambda i, j, k: (i, j))])
    dw2 = mm(act, dxb, ta=True, name=f"mlp{tag}_dw2", out=((F, D), BF))
    dw1 = mm(hn, dhid, ta=True, name=f"mlp{tag}_dw1", out=(w1.shape, BF))
    return dhid, dw1, dw2.reshape(N_DEV, F // N_DEV, D)


def _reduce_begin(grads, tag):
    return sibling_start(grads, name=f"reduce_sibling_start_{tag}")


def _reduce_continue(sib, after, tag, index):
    grads, a_bufs = sibling_wait(sib, after, name=f"reduce_sibling_wait_{tag}")
    pairs = [run_job(pair_job(g, a), index=index, name=f"pair_sum_{tag}{t}")[0]
             for t, (g, a) in enumerate(zip(grads, a_bufs))]
    return grads, a_bufs, chips_start(pairs, name=f"reduce_chips_start_{tag}")


def kernel(x, positions, e_norm_mix, e_w_in, e_q_norm, e_w_uq, e_kv_norm, e_w_ukv, e_v_norm, e_sgu_w, e_sgu_b, e_mla_out_norm, e_sgu_out_norm, e_w_out, o_norm_mix, o_w_in, o_conv_w, o_w_out, mlp_norm, mlp_w1, mlp_w2, final_norm, loss_target, m_e_norm_mix, m_e_w_in, m_e_q_norm, m_e_w_uq, m_e_kv_norm, m_e_w_ukv, m_e_v_norm, m_e_sgu_w, m_e_sgu_b, m_e_mla_out_norm, m_e_sgu_out_norm, m_e_w_out, m_o_norm_mix, m_o_w_in, m_o_conv_w, m_o_w_out, m_mlp_norm, m_mlp_w1, m_mlp_w2, m_final_norm, v_e_norm_mix, v_e_w_in, v_e_q_norm, v_e_w_uq, v_e_kv_norm, v_e_w_ukv, v_e_v_norm, v_e_sgu_w, v_e_sgu_b, v_e_mla_out_norm, v_e_sgu_out_norm, v_e_w_out, v_o_norm_mix, v_o_w_in, v_o_conv_w, v_o_w_out, v_mlp_norm, v_mlp_w1, v_mlp_w2, v_final_norm):
    T, D = x.shape[1], x.shape[2]
    d_shard = o_norm_mix.shape[1]
    x0 = x[0]
    target = loss_target[0]
    me = 4 * lax.axis_index("x") + 2 * lax.axis_index("y") + lax.axis_index("c")

    bf = lambda s: s.astype(BF)
    gather_groups = [[bf(jnp.transpose(e_w_in[0])), bf(e_w_uq[0]), bf(e_w_ukv[0])], [bf(e_w_out[0]), bf(mlp_w1[0])],
                     [bf(mlp_w2[0]), bf(o_w_in[0])], [bf(o_w_out[0]), bf(mlp_w1[1])], [bf(mlp_w2[1])]]
    small_rows = jnp.concatenate([o_norm_mix, o_conv_w[0], jnp.zeros((4, d_shard), F32)], axis=0)
    small_flat = all_gather_vmem(small_rows, name="gather_small")
    started, start_token = gather_start(gather_groups[:1], small_flat, name="gather_start0")
    started_rest, rest_token = gather_start(gather_groups[1:], start_token, name="gather_start1")
    started += started_rest

    def gathered(gi, after):
        srcs, lands = gather_wait(started[gi], after, name=f"gather_wait{gi}")
        return gather_finish(srcs, lands, name=f"gather_finish{gi}")

    small_g = small_flat.reshape(N_DEV, 8, d_shard)
    o_norm_full = small_g[:, 0, :].reshape(1, D)
    conv_w_full = jnp.transpose(small_g[:, 1:4, :], (1, 0, 2)).reshape(3, D)
    w_tril = jnp.tril(e_sgu_w[0])
    w_tril_b = w_tril.astype(BF)
    w_tril_tb = jnp.swapaxes(w_tril, 1, 2).astype(BF)
    b_full = jnp.repeat(e_sgu_b[0].T, CH, axis=1)
    v_gain = e_v_norm[0].reshape(1, SGU_OUT)
    cos_t, sin_t = _rope_tables(positions[0])
    mlp_gain = [mlp_norm[0:1], mlp_norm[1:2]]
    final_gain = final_norm.reshape(1, D)

    h0 = rms_fwd(x0, e_norm_mix, name="e_norm", deps=[rest_token])
    g_w_in_t, g_w_uq, w_ukv = gathered(
        0, [h0, cos_t, sin_t, w_tril_b, w_tril_tb, b_full, o_norm_full, conv_w_full])
    w_in_t = _pack_w_in_t(g_w_in_t)
    w_uq = jnp.concatenate([g_w_uq[..., :QK_NOPE], _rope_slab(g_w_uq[..., QK_NOPE:])], axis=-1)
    proj = mm(h0, w_in_t, tb=True, name="e_in", out=((T, w_in_t.shape[0]), F32), bn=_tile(w_in_t.shape[0], 640))
    qn, kvn, krope = mla_prep(proj, e_q_norm, e_kv_norm, cos_t, sin_t, name="mla_prep")
    bm = _tile(T, MM_TILE)

    def q_epi(acc, cos_ref, sin_ref):
        return (jnp.concatenate([acc[:, :QK_NOPE], _rope_fwd(acc[:, QK_NOPE:], cos_ref[...], sin_ref[...])], axis=-1),)

    q = mm(qn, w_uq, name="mla_q", out=((T, HEADS * HEAD_PAD), BF), bm=bm, bn=HEAD_PAD, epi=q_epi,
           epi_ins=[(cos_t, (bm, LANES), lambda i, j, k: (i, 0)), (sin_t, (bm, LANES), lambda i, j, k: (i, 0))])

    def kv_epi(acc, kr_ref):
        return jnp.concatenate([acc[:, :QK_NOPE].astype(BF), kr_ref[...]], axis=-1), acc[:, QK_NOPE:]

    k, v = mm(kvn, w_ukv, name="mla_kv", bm=bm, bn=HEAD_PAD, epi=kv_epi,
              outs=[((T, HEADS * HEAD_PAD), BF, HEAD_PAD), ((T, MLA_OUT), BF, V_HEAD)],
              epi_ins=[(krope, (bm, LANES), lambda i, j, k: (i, 0))])
    attn, attn_lse = attn_fwd(q, k, v, name="attn_fwd")
    mixed = mix_fwd(attn, proj, e_mla_out_norm, e_sgu_out_norm, v_gain, w_tril_b, b_full, name="mix_fwd")
    bn = _tile(D, MM_TILE)
    g_w_out_e, w1_0 = gathered(1, [mixed])
    w_out_e = g_w_out_e.reshape(-1, D)
    x1 = mm(mixed, w_out_e, name="e_out", out=((T, D), F32), bm=bm, bn=bn,
            epi=lambda acc, r: (acc + r[...],), epi_ins=[(x0, (bm, bn), lambda i, j, k: (i, j))])
    hn0, a0, act0 = _mlp_up(x1, mlp_gain[0], w1_0, 0)
    g_w2_0, g_w_in_o = gathered(2, [act0])
    w2_0 = g_w2_0.reshape(-1, D)
    x2 = _mlp_down(x1, act0, w2_0, 0)
    ho = rms_fwd(x2, o_norm_full, name="o_norm")
    proj_o = mm(ho, g_w_in_o, name="o_in", out=((T, 3 * D), F32))
    gated = conv_fwd(proj_o, conv_w_full, name="conv_fwd")
    g_w_out_o, w1_1 = gathered(3, [gated])
    w_out_o = g_w_out_o.reshape(-1, D)
    x3 = mm(gated, w_out_o, name="o_out", out=((T, D), F32), bm=bm, bn=bn,
            epi=lambda acc, r: (acc + r[...],), epi_ins=[(x2, (bm, bn), lambda i, j, k: (i, j))])
    hn1, a1, act1 = _mlp_up(x3, mlp_gain[1], w1_1, 1)
    (g_w2_1,) = gathered(4, [act1])
    w2_1 = g_w2_1.reshape(-1, D)
    x4 = _mlp_down(x3, act1, w2_1, 1)
    w1, w2 = [w1_0, w1_1], [w2_0, w2_1]

    dx4, dx4b, d_final, loss_part = loss_bwd([x4], final_gain, target, name="loss_bwd")

    hosted = dict(job_index=device_index())
    dhid1, dw1_1, dw2_1 = _mlp_bwd_weights(w1[1], w2[1], (hn1, a1, act1), dx4b, 1)
    sib_r0 = _reduce_begin([dw1_1, dw2_1], "r0")
    dhn1 = mm(dhid1, w1[1], tb=True, name="mlp1_dhn", out=((T, D), F32), deps=[sib_r0[-1]])
    grads_r0, a_r0 = sibling_wait(sib_r0, dhn1, name="reduce_sibling_wait_r0")
    dx3, dx3b, d_mlp1 = rms_bwd(x3, mlp_gain[1], dhn1, dres=dx4, name="mlp1_norm_bwd")

    dgated, ((pair_r0a,),) = mm(dx3b, w_out_o, tb=True, name="o_out_dx", out=((T, D), F32),
                                jobs=[pair_job(grads_r0[0], a_r0[0])], **hosted)
    dw_out_o, ((pair_r0b,),) = mm(gated, dx3b, ta=True, name="o_out_dw", out=((D, D), BF),
                                  jobs=[pair_job(grads_r0[1], a_r0[1])], **hosted)
    st_r0 = chips_start([pair_r0a, pair_r0b], name="reduce_chips_start_r0")
    dproj_o, dconv_full = conv_bwd(dgated, proj_o, conv_w_full, name="conv_bwd", deps=[st_r0[-1]])
    dw_in_o = mm(ho, dproj_o, ta=True, name="o_in_dw", out=(g_w_in_o.shape, BF))
    sib_r1 = _reduce_begin([dw_out_o.reshape(g_w_out_o.shape), dw_in_o], "r1")
    dho = mm(dproj_o, g_w_in_o, tb=True, name="o_in_dx", out=((T, D), F32), deps=[sib_r1[-1]])
    grads_r1, a_r1 = sibling_wait(sib_r1, dho, name="reduce_sibling_wait_r1")
    dx2, dx2b, d_onorm_full = rms_bwd(x2, o_norm_full, dho, dres=dx3, name="o_norm_bwd")

    d_ff = a0.shape[1]
    bm_h, bn_h = _tile(T, MM_TILE), _tile(d_ff, min(MM_TILE, w1[0].shape[2]))
    dhid0, ((pair_r1a,), (pair_r1b,)) = mm(
        dx2b, w2[0], tb=True, name="mlp0_dhid", out=((T, d_ff), BF), bm=bm_h, bn=bn_h,
        epi=lambda acc, a_ref: (2.0 * a_ref[...].astype(F32) * acc,),
        epi_ins=[(a0, (bm_h, bn_h), lambda i, j, k: (i, j))],
        jobs=[pair_job(grads_r1[0], a_r1[0]), pair_job(grads_r1[1], a_r1[1])], **hosted)
    st_r1 = chips_start([pair_r1a, pair_r1b], name="reduce_chips_start_r1")
    dw2_0 = mm(act0, dx2b, ta=True, name="mlp0_dw2", out=((d_ff, D), BF), deps=[st_r1[-1]])
    b_r0 = chips_wait(st_r0, dw2_0, name="reduce_chips_wait_r0")
    dw1_0, (r_w1, r_w2) = mm(
        hn0, dhid0, ta=True, name="mlp0_dw1", out=(w1[0].shape, BF),
        jobs=[adam_job(grads_r0[0], a_r0[0], b_r0[0], mlp_w1, m_mlp_w1, v_mlp_w1, 1, None),
              adam_job(grads_r0[1], a_r0[1], b_r0[1], mlp_w2, m_mlp_w2, v_mlp_w2, 1, None)], **hosted)
    sib_r2 = _reduce_begin([dw1_0, dw2_0.reshape(N_DEV, d_ff // N_DEV, D)], "r2")
    dhn0 = mm(dhid0, w1[0], tb=True, name="mlp0_dhn", out=((T, D), F32), deps=[sib_r2[-1]])
    grads_r2, a_r2 = sibling_wait(sib_r2, dhn0, name="reduce_sibling_wait_r2")
    dx1, dx1b, d_mlp0 = rms_bwd(x1, mlp_gain[0], dhn0, dres=dx2, name="mlp0_norm_bwd")

    dmixed, ((pair_r2a,),) = mm(dx1b, w_out_e, tb=True, name="e_out_dx", out=((T, MLA_OUT + SGU_OUT), F32),
                                jobs=[pair_job(grads_r2[0], a_r2[0])], **hosted)
    dw_out_e, ((pair_r2b,),) = mm(mixed, dx1b, ta=True, name="e_out_dw", out=(w_out_e.shape, BF),
                                  jobs=[pair_job(grads_r2[1], a_r2[1])], **hosted)
    st_r2 = chips_start([pair_r2a, pair_r2b], name="reduce_chips_start_r2")
    (dattn, dproj, d_mla_out, d_sgu_out, d_vgain, d_sgu_w, d_b_full) = mix_bwd(
        dmixed, attn, proj, e_mla_out_norm, e_sgu_out_norm, v_gain, w_tril_b, w_tril_tb, b_full, name="mix_bwd",
        deps=[st_r2[-1]])
    b_r1 = chips_wait(st_r1, dattn, name="reduce_chips_wait_r1")
    dq, dk, dv = attn_bwd(q, k, v, attn, attn_lse, dattn, name="attn_bwd")
    dq_lin, dkv_lin, dproj = mla_bwd_prep(dq, dk, dv, cos_t, sin_t, dproj, name="mla_bwd_prep")
    dw_uq_pad = mm(qn, dq_lin, ta=True, name="mla_q_dw", out=(w_uq.shape, BF))
    dw_ukv = mm(kvn, dkv_lin, ta=True, name="mla_kv_dw", out=(w_ukv.shape, BF))
    dw_uq = jnp.concatenate([dw_uq_pad[..., :QK_NOPE], _rope_unslab(dw_uq_pad[..., QK_NOPE:])], axis=-1)
    sib_r2b = _reduce_begin([dw_out_e.reshape(g_w_out_e.shape), dw_uq, dw_ukv], "r2b")
    dqn = mm(dq_lin, w_uq, tb=True, name="mla_q_dx", out=((T, Q_LORA), F32), deps=[sib_r2b[-1]])
    dkvn = mm(dkv_lin, w_ukv, tb=True, name="mla_kv_dx", out=((T, KV_LORA), F32), deps=[sib_r2b[-1]])
    grads_r2b, a_r2b, st_r2b = _reduce_continue(sib_r2b, dkvn, "r2b", hosted["job_index"])
    dproj, d_qnorm = rms_bwd(proj, e_q_norm, dqn, col_block=0, want_f32=False, into=dproj, name="q_norm_bwd",
                             deps=[st_r2b[-1]])
    dproj, d_kvnorm = rms_bwd(proj, e_kv_norm, dkvn, col_block=1, want_f32=False, into=dproj, name="kv_norm_bwd")
    dw_in_t_pad, (r_w_out_o, r_w_in_o) = mm(
        dproj, h0, ta=True, name="e_in_dw", out=(w_in_t.shape, BF), bm=_tile(w_in_t.shape[0], 640),
        jobs=[adam_job(grads_r1[0], a_r1[0], b_r1[0], o_w_out, m_o_w_out, v_o_w_out, 0, None),
              adam_job(grads_r1[1], a_r1[1], b_r1[1], o_w_in, m_o_w_in, v_o_w_in, 0, None)], **hosted)
    dw_in_t = unpack_w_in_t_grad(dw_in_t_pad, name="e_in_dw_unpack")
    sib_r3 = _reduce_begin([dw_in_t], "r3")
    dh0 = mm(dproj, w_in_t, name="e_in_dx", out=((T, D), F32), deps=[sib_r3[-1]])
    grads_r3, a_r3, st_r3 = _reduce_continue(sib_r3, dh0, "r3", hosted["job_index"])
    tok_r3 = st_r3[-1]
    grad_x, d_enorm = rms_bwd(x0, e_norm_mix, dh0, dres=dx1, want_bf=False, name="e_norm_bwd", deps=[tok_r3])
    b_r2 = chips_wait(st_r2, grad_x, name="reduce_chips_wait_r2")

    d_sgu_b = jnp.transpose(d_b_full[:, ::CH])
    d_sgu_w_tril = jnp.tril(d_sgu_w)
    rep = [("e_norm_mix", e_norm_mix, m_e_norm_mix, v_e_norm_mix, d_enorm),
           ("e_q_norm", e_q_norm, m_e_q_norm, v_e_q_norm, d_qnorm),
           ("e_kv_norm", e_kv_norm, m_e_kv_norm, v_e_kv_norm, d_kvnorm),
           ("e_v_norm", e_v_norm, m_e_v_norm, v_e_v_norm, d_vgain),
           ("e_sgu_w", e_sgu_w, m_e_sgu_w, v_e_sgu_w, d_sgu_w_tril),
           ("e_sgu_b", e_sgu_b, m_e_sgu_b, v_e_sgu_b, d_sgu_b),
           ("e_mla_out_norm", e_mla_out_norm, m_e_mla_out_norm, v_e_mla_out_norm, d_mla_out),
           ("e_sgu_out_norm", e_sgu_out_norm, m_e_sgu_out_norm, v_e_sgu_out_norm, d_sgu_out),
           ("mlp_norm", mlp_norm, m_mlp_norm, v_mlp_norm, jnp.concatenate([d_mlp0, d_mlp1], axis=0)),
           ("final_norm", final_norm, m_final_norm, v_final_norm, d_final)]
    sizes = [int(np.prod(r[1].shape)) for r in rep]
    n_rep = sum(sizes)
    n_all = n_rep + 4 * D + 1
    width = -(-n_all // (8 * LANES)) * LANES
    pad = 8 * width - n_all
    flat = jnp.concatenate([r[4].reshape(-1) for r in rep]
                           + [d_onorm_full.reshape(-1), dconv_full.reshape(-1), loss_part[0, :1],
                              jnp.zeros((pad,), F32)])
    small_started, small_token = gather_start([[flat.reshape(8, width)]], b_r2[0], name="gather_small_grads_start")

    def finish(grads, a_bufs, b_bufs, t, w, m, v, layer=0, prev=None, tag="", deps=()):
        return run_job(adam_job(grads[t], a_bufs[t], b_bufs[t], w, m, v, layer, prev), index=hosted["job_index"],
                       name=f"adam_{tag}", deps=deps)

    r_w1 = finish(grads_r2, a_r2, b_r2, 0, mlp_w1, m_mlp_w1, v_mlp_w1, 0, r_w1, tag="w1_l0", deps=[tok_r3, small_token])
    r_w2 = finish(grads_r2, a_r2, b_r2, 1, mlp_w2, m_mlp_w2, v_mlp_w2, 0, r_w2, tag="w2_l0", deps=[r_w1[1]])
    b_r2b = chips_wait(st_r2b, r_w2[1], name="reduce_chips_wait_r2b")
    r_w_out_e = finish(grads_r2b, a_r2b, b_r2b, 0, e_w_out, m_e_w_out, v_e_w_out, tag="e_w_out")
    r_w_uq = finish(grads_r2b, a_r2b, b_r2b, 1, e_w_uq, m_e_w_uq, v_e_w_uq, tag="e_w_uq")
    r_w_ukv = finish(grads_r2b, a_r2b, b_r2b, 2, e_w_ukv, m_e_w_ukv, v_e_w_ukv, tag="e_w_ukv")
    b_r3 = chips_wait(st_r3, r_w_out_e[1], name="reduce_chips_wait_r3")
    g_w_in_t = reduce_sum(grads_r3[0], a_r3[0], b_r3[0], name="sum_e_w_in")
    w_in_upd_t = adam_rows(g_w_in_t, jnp.transpose(e_w_in[0]), jnp.transpose(m_e_w_in[0]), jnp.transpose(v_e_w_in[0]),
                           name="adam_e_w_in")
    r_w_in = [jnp.transpose(t)[None] for t in (g_w_in_t, *w_in_upd_t)]

    small_srcs, small_lands = gather_wait(small_started[0], [r_w2[1]], name="gather_small_grads_wait")
    small_all = gather_finish(small_srcs, small_lands, name="gather_small_grads_finish")[0]
    summed = sum_rows8(small_all.reshape(N_DEV * 8, width), 8, name="sum_small_grads").reshape(-1)

    loss = summed[n_rep + 4 * D]

    def pack_rep(i):
        return jnp.concatenate([r[i].reshape(-1) for r in rep]).reshape(n_rep // LANES, LANES)

    g_rep = summed[:n_rep].reshape(n_rep // LANES, LANES)
    d_rep, nm_rep, nv_rep = adam_flat(g_rep, pack_rep(1), pack_rep(2), pack_rep(3), name="adam_replicated")

    def unpack_rep(flat2d):
        out, off = {}, 0
        f = flat2d.reshape(-1)
        for r, n in zip(rep, sizes):
            out[r[0]] = f[off:off + n].reshape(r[1].shape)
            off += n
        return out

    small = {"grad": unpack_rep(g_rep), "delta": unpack_rep(d_rep), "new_m": unpack_rep(nm_rep),
             "new_v": unpack_rep(nv_rep)}
    g_onorm = lax.dynamic_slice(summed[n_rep:n_rep + D].reshape(1, D), (0, me * d_shard), (1, d_shard))
    g_conv = lax.dynamic_slice(summed[n_rep + D:n_rep + 4 * D].reshape(3, D), (0, me * d_shard), (3, d_shard))

    def pack_sharded(norm_part, conv_part):
        return jnp.concatenate([norm_part, conv_part, jnp.zeros((4, d_shard), F32)], axis=0)

    g_sh = pack_sharded(g_onorm, g_conv)
    d_sh, nm_sh, nv_sh = adam_flat(g_sh, pack_sharded(o_norm_mix, o_conv_w[0]), pack_sharded(m_o_norm_mix, m_o_conv_w[0]),
                                   pack_sharded(v_o_norm_mix, v_o_conv_w[0]), name="adam_sharded_small")
    for kind, arr in (("grad", g_sh), ("delta", d_sh), ("new_m", nm_sh), ("new_v", nv_sh)):
        small[kind]["o_norm_mix"] = arr[0:1]
        small[kind]["o_conv_w"] = arr[1:4][None]

    big = {"e_w_in": r_w_in, "e_w_uq": r_w_uq, "e_w_ukv": r_w_ukv, "e_w_out": r_w_out_e, "o_w_in": r_w_in_o,
           "o_w_out": r_w_out_o, "mlp_w1": r_w1, "mlp_w2": r_w2}
    order = ["e_norm_mix", "e_w_in", "e_q_norm", "e_w_uq", "e_kv_norm", "e_w_ukv", "e_v_norm", "e_sgu_w", "e_sgu_b",
             "e_mla_out_norm", "e_sgu_out_norm", "e_w_out", "o_norm_mix", "o_w_in", "o_conv_w", "o_w_out", "mlp_norm",
             "mlp_w1", "mlp_w2", "final_norm"]
    result = [loss, grad_x[None]]
    for ki, kind in enumerate(("grad", "delta", "new_m", "new_v")):
        for nm in order:
            result.append(big[nm][ki] if nm in big else small[kind][nm])
    return tuple(result)
```

```python
import numpy as np
import jax
import jax.numpy as jnp
from jax import lax
from jax.experimental import pallas as pl
from jax.experimental.pallas import tpu as pltpu

BF = jnp.bfloat16
F32 = jnp.float32
MESH = pl.DeviceIdType.MESH
N_DEV = 8

EPS = 1e-6
HEADS = 8
Q_LORA = 512
KV_LORA = 512
QK_NOPE = 128
QK_ROPE = 64
HALF_ROPE = QK_ROPE // 2
V_HEAD = 128
HEAD_PAD = 256
ROPE_BASE = 10000.0
GROUPS = 8
CH = 128
CHUNK = 128
SGU_OUT = GROUPS * CH
MLA_OUT = HEADS * V_HEAD
ATTN_SCALE = float((QK_NOPE + QK_ROPE) ** -0.5)

ADAM_LR = 0.001
ADAM_B1 = 0.9
ADAM_B2 = 0.999
ADAM_EPS = 1e-08
ADAM_WD = 0.01
ADAM_STEP = 10
ADAM_C1 = 1.0 - ADAM_B1 ** ADAM_STEP
ADAM_C2 = 1.0 - ADAM_B2 ** ADAM_STEP

V7X_VMEM_BYTES = 64 * 2 ** 20
VMEM_LIMIT_CAP = V7X_VMEM_BYTES - 6 * 2 ** 20
LANES = 128
ROW_TILE = 256
ATTN_TILE = 512
STREAM_BLOCK_ELEMS = 512 * 1024
MM_TILE = 1024
MM_K_TILE = 2048
MM_K_BLOCK_MAX = 3072


def _padded_bytes(block, dtype):
    dims = [d for d in block if d is not None]
    if len(dims) >= 1:
        dims[-1] = -(-dims[-1] // LANES) * LANES
    if len(dims) >= 2:
        dims[-2] = -(-dims[-2] // 16) * 16
    return int(np.prod(dims)) * jnp.dtype(dtype).itemsize


def _pcall(body, *, name, grid, ins, outs, scratch=(), semantics=None, aliases=None, prefetch=None, deps=()):
    any_spec = pl.BlockSpec(memory_space=pl.ANY)
    if deps:
        n_lead = len(ins) + (1 if prefetch is not None else 0)
        n_deps = len(deps)
        inner = body

        def body(*refs):
            inner(*refs[:n_lead], *refs[n_lead + n_deps:])

        ins = list(ins) + [(d, None, None) for d in deps]
    in_specs = [any_spec if b is None else pl.BlockSpec(b, m) for _, b, m in ins]
    out_specs = [any_spec if b is None else pl.BlockSpec(b, m) for _, _, b, m in outs]
    out_shape = [pltpu.HBM(s, d) for s, d, _, _ in outs]
    est = 0
    for a, b, _ in ins:
        if b is not None:
            est += 2 * _padded_bytes(b, a.dtype)
    for _, d, b, _ in outs:
        if b is not None:
            est += 2 * _padded_bytes(b, d)
    for s in scratch:
        if hasattr(s, "shape") and hasattr(s, "dtype"):
            est += _padded_bytes(s.shape, s.dtype)
    limit = int(min(VMEM_LIMIT_CAP, est + 16 * 2 ** 20))
    params = pltpu.CompilerParams(
        dimension_semantics=semantics or ("arbitrary",) * len(grid), vmem_limit_bytes=limit)
    args = [pltpu.with_memory_space_constraint(a, pltpu.HBM) for a, _, _ in ins]
    if prefetch is not None:
        grid_spec = pltpu.PrefetchScalarGridSpec(
            num_scalar_prefetch=1, grid=grid, in_specs=in_specs, out_specs=out_specs, scratch_shapes=list(scratch))
        call = pl.pallas_call(body, out_shape=out_shape, grid_spec=grid_spec, name=name, compiler_params=params,
                              input_output_aliases=aliases or {})
        return call(prefetch, *args)
    call = pl.pallas_call(body, out_shape=out_shape, grid=grid, in_specs=in_specs, out_specs=out_specs,
                          scratch_shapes=list(scratch), name=name, compiler_params=params,
                          input_output_aliases=aliases or {})
    return call(*args)


def _tile(dim, pref, quantum=LANES):
    if dim <= pref:
        return dim
    t = (pref // quantum) * quantum
    while t >= quantum:
        if dim % t == 0:
            return t
        t -= quantum
    return dim


def _vshape(arr_shape):
    if len(arr_shape) == 2:
        return tuple(arr_shape)
    s, r, c = arr_shape
    return (r, s * c)


def _vblock(arr_shape, br, bc, rc):
    if len(arr_shape) == 2:
        return (br, bc), (lambda *g: rc(*g))
    _, _, c = arr_shape
    assert c % bc == 0, (arr_shape, bc)
    per = c // bc

    def imap(*g):
        ri, ci = rc(*g)
        return (ci // per, ri, ci % per)

    return (None, br, bc), imap


def _shard_width(*shapes):
    w = None
    for s in shapes:
        if len(s) == 3:
            w = s[2] if w is None else int(np.gcd(w, s[2]))
    return w


def mm(a, b, *, name, ta=False, tb=False, out=None, outs=None, epi=None, epi_ins=(), bm=None, bn=None, bk=None,
       deps=(), jobs=(), job_index=None):
    av, bv = _vshape(a.shape), _vshape(b.shape)
    M, K = (av[1], av[0]) if ta else av
    K2, N = (bv[1], bv[0]) if tb else bv
    assert K == K2, (a.shape, b.shape, ta, tb)
    if outs is None:
        outs = [(out[0], out[1], None)]
    a_sw = _shard_width(a.shape)
    b_sw = _shard_width(b.shape)
    o_sw = _shard_width(*[o[0] for o in outs])
    m_lim = a_sw if (ta and a_sw) else None
    k_lim = [w for w in ((a_sw if not ta else None), (b_sw if tb else None)) if w]
    n_lim = [w for w in ((b_sw if not tb else None), o_sw) if w]
    if bm is None:
        bm = _tile(M, min([MM_TILE] + ([m_lim] if m_lim else [])))
    if bn is None:
        bn = _tile(N, min([MM_TILE] + n_lim))
    k_shards = 0
    if tb and len(b.shape) == 3 and bk is None and not (a_sw and not ta):
        k_shards = 1
        while 2 * k_shards <= b.shape[0] and 2 * k_shards * b_sw <= MM_K_BLOCK_MAX:
            k_shards *= 2
        bk = k_shards * b_sw
    if bk is None:
        bk = K if (K <= 4096 and not k_lim) else _tile(K, min([MM_K_TILE] + k_lim))
    assert M % bm == 0 and N % bn == 0 and K % bk == 0, (name, M, N, K, bm, bn, bk)
    nk = K // bk
    grid = (M // bm, N // bn, nk)
    if ta:
        a_blk, a_map = _vblock(a.shape, bk, bm, lambda i, j, k: (k, i))
    else:
        a_blk, a_map = _vblock(a.shape, bm, bk, lambda i, j, k: (i, k))
    if k_shards:
        b_blk, b_map = (k_shards, bn, b_sw), (lambda i, j, k: (k, j, 0))
    elif tb:
        b_blk, b_map = _vblock(b.shape, bn, bk, lambda i, j, k: (j, k))
    else:
        b_blk, b_map = _vblock(b.shape, bk, bn, lambda i, j, k: (k, j))
    dn = (((0 if ta else 1,), (1 if tb else 0,)), ((), ()))
    ins = [(a, a_blk, a_map), (b, b_blk, b_map)] + list(epi_ins)
    out_list = []
    for shape, dtype, cols in outs:
        cols = cols or bn
        blk, imap = _vblock(shape, bm, cols, lambda i, j, k: (i, j))
        out_list.append((shape, dtype, blk, imap))
    n_e, n_o = len(epi_ins), len(out_list)

    n_steps = grid[0] * grid[1] * nk
    built = [job(n_steps) for job in jobs]
    aliases = {}
    job_slices = []
    if built:
        def lin(i, j, k):
            return (i * grid[1] + j) * nk + k

        ins = [(arr, blk, None if blk is None else (lambda i, j, k, s, f=f: f(i, j, k))) for arr, blk, f in ins]
        out_list = [(sh, dt, blk, (lambda i, j, k, s, f=f: f(i, j, k))) for sh, dt, blk, f in out_list]
        n_main_in, n_main_out = len(ins), len(out_list)
        for jb in built:
            i0, o0 = len(ins), len(out_list)
            ins += [(arr, blk, None if blk is None else (lambda i, j, k, s, f=f: f(lin(i, j, k), s)))
                    for arr, blk, f in jb["ins"]]
            out_list += [(sh, dt, blk, (lambda i, j, k, s, f=f: f(lin(i, j, k), s))) for sh, dt, blk, f in jb["outs"]]
            aliases.update({1 + i0 + ai: o0 + ao for ai, ao in jb["aliases"].items()})
            job_slices.append((i0, len(jb["ins"]), o0, len(jb["outs"])))
    n_in_total = len(ins)

    def body(*refs):
        if built:
            refs = refs[1:]
        a_ref, b_ref = refs[0], refs[1]
        e_refs = refs[2:2 + n_e]
        o_refs = refs[n_in_total:n_in_total + n_o]
        for jb, (i0, ni, o0, no) in zip(built, job_slices):
            jb["fn"](refs[i0:i0 + ni], refs[n_in_total + o0:n_in_total + o0 + no])

        def finish(acc):
            res = epi(acc, *e_refs) if epi is not None else (acc,)
            for o_ref, r in zip(o_refs, res):
                o_ref[...] = r.astype(o_ref.dtype)

        x = a_ref[...].astype(BF)
        y = b_ref[...].astype(BF)
        if k_shards:
            p = None
            for s in range(k_shards):
                part = lax.dot_general(x[:, s * b_sw:(s + 1) * b_sw], y[s], dn, preferred_element_type=F32)
                p = part if p is None else p + part
        else:
            p = lax.dot_general(x, y, dn, preferred_element_type=F32)
        if nk == 1:
            finish(p)
        else:
            acc_ref = refs[-1]
            k = pl.program_id(2)

            @pl.when(k == 0)
            def _():
                acc_ref[...] = p

            @pl.when(k > 0)
            def _():
                acc_ref[...] += p

            @pl.when(k == nk - 1)
            def _():
                finish(acc_ref[...])

    scratch = [pltpu.VMEM((bm, bn), F32)] if nk > 1 else []
    res = _pcall(body, name=name, grid=grid, ins=ins, outs=out_list, scratch=scratch, deps=deps,
                 semantics=("parallel", "parallel", "arbitrary"), prefetch=job_index if built else None, aliases=aliases)
    main = res[0] if n_o == 1 else res[:n_o]
    if not built:
        return main
    return main, [res[o0:o0 + no] for _, _, o0, no in job_slices]


_GELU_K = float(np.sqrt(2.0 / np.pi))
_GELU_C = 0.044715


def _gelu(x):
    t = jnp.tanh(_GELU_K * (x + _GELU_C * (x * x * x)))
    return 0.5 * x * (1.0 + t)


def _gelu_grad(x):
    t = jnp.tanh(_GELU_K * (x + _GELU_C * (x * x * x)))
    return 0.5 * (1.0 + t) + 0.5 * x * (1.0 - t * t) * (_GELU_K * (1.0 + 3.0 * _GELU_C * (x * x)))


def _rstd(x):
    return lax.rsqrt(jnp.mean(x * x, axis=-1, keepdims=True) + EPS)


def _rms_bwd(x, gain, dy):
    r = _rstd(x)
    xh = x * r
    gdy = dy * gain
    dx = r * (gdy - xh * jnp.mean(gdy * xh, axis=-1, keepdims=True))
    return dx, dy * xh


def _rope_fwd(x, cos_t, sin_t):
    return x * cos_t + pltpu.roll(x, 2 * HALF_ROPE, 1) * sin_t


def _rope_bwd(dy, cos_t, sin_t):
    return dy * cos_t + pltpu.roll(dy * sin_t, 2 * HALF_ROPE, 1)


def _acc_rows(ref, val, first):
    s = jnp.sum(val, axis=0, keepdims=True)

    @pl.when(first)
    def _():
        ref[...] = s

    @pl.when(jnp.logical_not(first))
    def _():
        ref[...] += s


def rms_fwd(x, gain, *, name, col_block=0, width=None, deps=()):
    T = x.shape[0]
    width = width or x.shape[1]
    tm = _tile(T, ROW_TILE, 8)

    def body(x_ref, g_ref, o_ref):
        v = x_ref[...]
        o_ref[...] = (v * _rstd(v) * g_ref[...]).astype(BF)

    return _pcall(body, name=name, grid=(T // tm,),
                  ins=[(x, (tm, width), lambda i: (i, col_block)), (gain, (1, width), lambda i: (0, 0))],
                  outs=[((T, width), BF, (tm, width), lambda i: (i, 0))], semantics=("parallel",), deps=deps)[0]


def rms_bwd(x, gain, dy, *, name, col_block=0, dres=None, want_f32=True, want_bf=True, into=None, deps=()):
    T, width = dy.shape
    tm = _tile(T, ROW_TILE, 8)
    has_res = dres is not None

    def body(*refs):
        x_ref, g_ref, dy_ref = refs[:3]
        pos = 3
        res_ref = None
        if has_res:
            res_ref = refs[pos]
            pos += 1
        if into is not None:
            pos += 1
        outs = refs[pos:]
        dx, dg_rows = _rms_bwd(x_ref[...], g_ref[...], dy_ref[...])
        if has_res:
            dx = dx + res_ref[...]
        o = 0
        if want_f32:
            outs[o][...] = dx
            o += 1
        if want_bf:
            outs[o][...] = dx.astype(BF)
            o += 1
        _acc_rows(outs[o], dg_rows, pl.program_id(0) == 0)

    ins = [(x, (tm, width), lambda i: (i, col_block)), (gain, (1, width), lambda i: (0, 0)),
           (dy, (tm, width), lambda i: (i, 0))]
    if has_res:
        ins.append((dres, (tm, width), lambda i: (i, 0)))
    outs = []
    aliases = {}
    if want_f32:
        outs.append(((T, width), F32, (tm, width), lambda i: (i, 0)))
    if want_bf and into is not None:
        ins.append((into, None, None))
        aliases[len(ins) - 1] = len(outs)
        outs.append((into.shape, BF, (tm, width), lambda i: (i, col_block)))
    elif want_bf:
        outs.append(((T, width), BF, (tm, width), lambda i: (i, 0)))
    outs.append(((1, width), F32, (1, width), lambda i: (0, 0)))
    return _pcall(body, name=name, grid=(T // tm,), ins=ins, outs=outs, aliases=aliases, deps=deps)


def mla_prep(proj, q_norm, kv_norm, cos_t, sin_t, *, name):
    T = proj.shape[0]
    tm = _tile(T, ROW_TILE, 8)
    kr_block = (proj.shape[1] - LANES) // LANES

    def body(cq_ref, ckv_ref, kr_ref, qg_ref, kg_ref, cos_ref, sin_ref, qn_ref, kvn_ref, krope_ref):
        cq = cq_ref[...]
        qn_ref[...] = (cq * _rstd(cq) * qg_ref[...]).astype(BF)
        ckv = ckv_ref[...]
        kvn_ref[...] = (ckv * _rstd(ckv) * kg_ref[...]).astype(BF)
        krope_ref[...] = _rope_fwd(kr_ref[...], cos_ref[...], sin_ref[...]).astype(BF)

    return _pcall(
        body, name=name, grid=(T // tm,),
        ins=[(proj, (tm, Q_LORA), lambda i: (i, 0)), (proj, (tm, KV_LORA), lambda i: (i, 1)),
             (proj, (tm, LANES), lambda i: (i, kr_block)),
             (q_norm, (1, Q_LORA), lambda i: (0, 0)), (kv_norm, (1, KV_LORA), lambda i: (0, 0)),
             (cos_t, (tm, LANES), lambda i: (i, 0)), (sin_t, (tm, LANES), lambda i: (i, 0))],
        outs=[((T, Q_LORA), BF, (tm, Q_LORA), lambda i: (i, 0)), ((T, KV_LORA), BF, (tm, KV_LORA), lambda i: (i, 0)),
              ((T, LANES), BF, (tm, LANES), lambda i: (i, 0))],
        semantics=("parallel",))


def _attn_scores(q, k_blk, diagonal):
    s = lax.dot_general(q, k_blk, (((1,), (1,)), ((), ())), preferred_element_type=F32) * ATTN_SCALE
    if diagonal:
        row = lax.broadcasted_iota(jnp.int32, s.shape, 0)
        col = lax.broadcasted_iota(jnp.int32, s.shape, 1)
        s = jnp.where(col <= row, s, -jnp.inf)
    return s


def attn_fwd(q, k, v, *, name):
    T = q.shape[0]
    tq = _tile(T, ATTN_TILE, 8)

    def body(q_ref, k_ref, v_ref, o_ref, lse_ref):
        i = pl.program_id(1)
        qv = q_ref[...]

        def block(kb, carry, diagonal):
            m, l, acc = carry
            start = pl.multiple_of(kb * tq, tq)
            s = _attn_scores(qv, k_ref[pl.ds(start, tq), :], diagonal)
            m_new = jnp.maximum(m, jnp.max(s, axis=-1, keepdims=True))
            alpha = jnp.exp(m - m_new)
            p = jnp.exp(s - m_new)
            l = alpha * l + jnp.sum(p, axis=-1, keepdims=True)
            acc = alpha * acc + jnp.dot(p.astype(BF), v_ref[pl.ds(start, tq), :], preferred_element_type=F32)
            return m_new, l, acc

        init = (jnp.full((tq, 1), -jnp.inf, F32), jnp.zeros((tq, 1), F32), jnp.zeros((tq, V_HEAD), F32))
        carry = lax.fori_loop(0, i, lambda kb, c: block(kb, c, False), init)
        m, l, acc = block(i, carry, True)
        o_ref[...] = acc / l
        lse_ref[...] = jnp.broadcast_to(m + jnp.log(l), (tq, V_HEAD))

    return _pcall(
        body, name=name, grid=(HEADS, T // tq),
        ins=[(q, (tq, HEAD_PAD), lambda h, i: (i, h)), (k, (T, HEAD_PAD), lambda h, i: (0, h)),
             (v, (T, V_HEAD), lambda h, i: (0, h))],
        outs=[((T, MLA_OUT), F32, (tq, V_HEAD), lambda h, i: (i, h)),
              ((T, MLA_OUT), F32, (tq, V_HEAD), lambda h, i: (i, h))], semantics=("parallel", "parallel"))


def attn_bwd(q, k, v, o, lse, do, *, name):
    T = q.shape[0]
    tq = _tile(T, ATTN_TILE, 8)

    def body(q_ref, k_ref, v_ref, o_ref, lse_ref, do_ref, dq_ref, dk_ref, dv_ref):
        i = pl.program_id(1)

        @pl.when(i == 0)
        def _():
            dk_ref[...] = jnp.zeros_like(dk_ref)
            dv_ref[...] = jnp.zeros_like(dv_ref)

        qv = q_ref[...]
        do_t = do_ref[...]
        lse_v = lse_ref[:, 0:1]
        delta = jnp.sum(do_t.astype(F32) * o_ref[...], axis=-1, keepdims=True)

        def block(kb, dq, diagonal):
            start = pl.multiple_of(kb * tq, tq)
            k_blk = k_ref[pl.ds(start, tq), :]
            v_blk = v_ref[pl.ds(start, tq), :]
            p = jnp.exp(_attn_scores(qv, k_blk, diagonal) - lse_v)
            dp = lax.dot_general(do_t, v_blk, (((1,), (1,)), ((), ())), preferred_element_type=F32)
            ds = (p * (dp - delta) * ATTN_SCALE).astype(BF)
            dk_ref[pl.ds(start, tq), :] += lax.dot_general(ds, qv, (((0,), (0,)), ((), ())), preferred_element_type=F32)
            dv_ref[pl.ds(start, tq), :] += lax.dot_general(p.astype(BF), do_t, (((0,), (0,)), ((), ())),
                                                          preferred_element_type=F32)
            return dq + jnp.dot(ds, k_blk, preferred_element_type=F32)

        dq = lax.fori_loop(0, i, lambda kb, c: block(kb, c, False), jnp.zeros((tq, HEAD_PAD), F32))
        dq_ref[...] = block(i, dq, True)

    return _pcall(
        body, name=name, grid=(HEADS, T // tq),
        ins=[(q, (tq, HEAD_PAD), lambda h, i: (i, h)), (k, (T, HEAD_PAD), lambda h, i: (0, h)),
             (v, (T, V_HEAD), lambda h, i: (0, h)), (o, (tq, V_HEAD), lambda h, i: (i, h)),
             (lse, (tq, V_HEAD), lambda h, i: (i, h)), (do, (tq, V_HEAD), lambda h, i: (i, h))],
        outs=[((T, HEADS * HEAD_PAD), F32, (tq, HEAD_PAD), lambda h, i: (i, h)),
              ((T, HEADS * HEAD_PAD), F32, (T, HEAD_PAD), lambda h, i: (0, h)),
              ((T, MLA_OUT), F32, (T, V_HEAD), lambda h, i: (0, h))],
        semantics=("parallel", "arbitrary"))


def mla_bwd_prep(dq, dk, dv, cos_t, sin_t, dproj, *, name):
    T = dq.shape[0]
    tm = _tile(T, ROW_TILE, 8)
    kr_block = (dproj.shape[1] - LANES) // LANES

    def body(dq_ref, dk_ref, dv_ref, cos_ref, sin_ref, dproj_in, dql_ref, dkvl_ref, dkr_ref):
        cos_v, sin_v = cos_ref[...], sin_ref[...]
        kr = jnp.zeros((tm, LANES), F32)
        for h in range(HEADS):
            lo = h * HEAD_PAD
            dql_ref[:, lo:lo + QK_NOPE] = dq_ref[:, lo:lo + QK_NOPE].astype(BF)
            dql_ref[:, lo + QK_NOPE:lo + HEAD_PAD] = _rope_bwd(
                dq_ref[:, lo + QK_NOPE:lo + HEAD_PAD], cos_v, sin_v).astype(BF)
            dkvl_ref[:, lo:lo + QK_NOPE] = dk_ref[:, lo:lo + QK_NOPE].astype(BF)
            dkvl_ref[:, lo + QK_NOPE:lo + HEAD_PAD] = dv_ref[:, h * V_HEAD:(h + 1) * V_HEAD].astype(BF)
            kr = kr + dk_ref[:, lo + QK_NOPE:lo + HEAD_PAD]
        dkr_ref[...] = _rope_bwd(kr, cos_v, sin_v).astype(BF)

    W = HEADS * HEAD_PAD
    return _pcall(
        body, name=name, grid=(T // tm,),
        ins=[(dq, (tm, W), lambda i: (i, 0)), (dk, (tm, W), lambda i: (i, 0)), (dv, (tm, MLA_OUT), lambda i: (i, 0)),
             (cos_t, (tm, LANES), lambda i: (i, 0)), (sin_t, (tm, LANES), lambda i: (i, 0)), (dproj, None, None)],
        outs=[((T, W), BF, (tm, W), lambda i: (i, 0)), ((T, W), BF, (tm, W), lambda i: (i, 0)),
              (dproj.shape, BF, (tm, LANES), lambda i: (i, kr_block))],
        aliases={5: 2}, semantics=("parallel",))


def _group_norm_stats(vg):
    mu = jnp.mean(vg, axis=-1, keepdims=True)
    d = vg - mu
    r = lax.rsqrt(jnp.mean(d * d, axis=-1, keepdims=True) + EPS)
    return d * r, r


def mix_fwd(a, proj, g_mla, g_sgu, v_gain, w_tril, b_full, *, name):
    T = a.shape[0]
    tm = _tile(T, ROW_TILE, CHUNK)
    n_chunk = tm // CHUNK

    def body(a_ref, u_ref, v_ref, gm_ref, gs_ref, vg_ref, w_ref, b_ref, o_ref, s_scr):
        av = a_ref[...]
        o_ref[:, :MLA_OUT] = (av * _rstd(av) * gm_ref[...]).astype(BF)
        for g in range(GROUPS):
            sl = slice(g * CH, (g + 1) * CH)
            vhat, _ = _group_norm_stats(_gelu(v_ref[:, sl]))
            vn = (vhat * vg_ref[:, sl]).astype(BF)
            u = _gelu(u_ref[:, sl])
            for ci in range(n_chunk):
                rs = slice(ci * CHUNK, (ci + 1) * CHUNK)
                y = jnp.dot(w_ref[g], vn[rs], preferred_element_type=F32) + b_ref[:, sl]
                s_scr[rs, sl] = u[rs] * y
        s = s_scr[...]
        o_ref[:, MLA_OUT:] = (s * _rstd(s) * gs_ref[...]).astype(BF)

    return _pcall(
        body, name=name, grid=(T // tm,),
        ins=[(a, (tm, MLA_OUT), lambda i: (i, 0)), (proj, (tm, SGU_OUT), lambda i: (i, 1)),
             (proj, (tm, SGU_OUT), lambda i: (i, 2)), (g_mla, (1, MLA_OUT), lambda i: (0, 0)),
             (g_sgu, (1, SGU_OUT), lambda i: (0, 0)), (v_gain, (1, SGU_OUT), lambda i: (0, 0)),
             (w_tril, (GROUPS, CHUNK, CHUNK), lambda i: (0, 0, 0)), (b_full, (CHUNK, SGU_OUT), lambda i: (0, 0))],
        outs=[((T, MLA_OUT + SGU_OUT), BF, (tm, MLA_OUT + SGU_OUT), lambda i: (i, 0))],
        scratch=[pltpu.VMEM((tm, SGU_OUT), F32)], semantics=("parallel",))[0]


def mix_bwd(dmixed, a, proj, g_mla, g_sgu, v_gain, w_tril, w_tril_t, b_full, *, name, deps=()):
    T = a.shape[0]
    tm = _tile(T, ROW_TILE, CHUNK)
    n_chunk = tm // CHUNK
    uv0 = Q_LORA + KV_LORA

    def body(dm_a_ref, dm_s_ref, a_ref, u_ref, v_ref, gm_ref, gs_ref, vg_ref, w_ref, wt_ref, b_ref,
             da_ref, duv_ref, dgm_ref, dgs_ref, dvg_ref, dw_ref, db_ref, s_scr, y_scr):
        first = pl.program_id(0) == 0
        duv_ref[:, :uv0] = jnp.zeros((tm, uv0), BF)
        duv_ref[:, uv0 + 2 * SGU_OUT:] = jnp.zeros((tm, duv_ref.shape[1] - uv0 - 2 * SGU_OUT), BF)
        da, dgm_rows = _rms_bwd(a_ref[...], gm_ref[...], dm_a_ref[...])
        da_ref[...] = da.astype(BF)
        _acc_rows(dgm_ref, dgm_rows, first)

        for g in range(GROUPS):
            sl = slice(g * CH, (g + 1) * CH)
            vhat, _ = _group_norm_stats(_gelu(v_ref[:, sl]))
            vn = (vhat * vg_ref[:, sl]).astype(BF)
            u = _gelu(u_ref[:, sl])
            for ci in range(n_chunk):
                rs = slice(ci * CHUNK, (ci + 1) * CHUNK)
                y = jnp.dot(w_ref[g], vn[rs], preferred_element_type=F32) + b_ref[:, sl]
                y_scr[rs, sl] = y
                s_scr[rs, sl] = u[rs] * y
        ds, dgs_rows = _rms_bwd(s_scr[...], gs_ref[...], dm_s_ref[...])
        _acc_rows(dgs_ref, dgs_rows, first)
        s_scr[...] = ds

        @pl.when(first)
        def _():
            dw_ref[...] = jnp.zeros_like(dw_ref)
            db_ref[...] = jnp.zeros_like(db_ref)

        for g in range(GROUPS):
            sl = slice(g * CH, (g + 1) * CH)
            upre = u_ref[:, sl]
            vpre = v_ref[:, sl]
            u = _gelu(upre)
            vhat, r = _group_norm_stats(_gelu(vpre))
            gain = vg_ref[:, sl]
            vn = (vhat * gain).astype(BF)
            dsg = s_scr[:, sl]
            duv_ref[:, uv0 + g * CH:uv0 + (g + 1) * CH] = (dsg * y_scr[:, sl] * _gelu_grad(upre)).astype(BF)
            dy = dsg * u
            dyb = dy.astype(BF)
            dvn_parts = []
            for ci in range(n_chunk):
                rs = slice(ci * CHUNK, (ci + 1) * CHUNK)
                dvn_parts.append(jnp.dot(wt_ref[g], dyb[rs], preferred_element_type=F32))
                dw_ref[g] += lax.dot_general(dyb[rs], vn[rs], (((1,), (1,)), ((), ())), preferred_element_type=F32)
                db_ref[:, sl] += jnp.broadcast_to(jnp.sum(dy[rs], axis=-1, keepdims=True), (CHUNK, CH))
            dvn = dvn_parts[0] if n_chunk == 1 else jnp.concatenate(dvn_parts, axis=0)
            _acc_rows(dvg_ref.at[:, sl], dvn * vhat, first)
            dvh = dvn * gain
            dvg = r * (dvh - jnp.mean(dvh, axis=-1, keepdims=True)
                       - vhat * jnp.mean(dvh * vhat, axis=-1, keepdims=True))
            duv_ref[:, uv0 + SGU_OUT + g * CH:uv0 + SGU_OUT + (g + 1) * CH] = (dvg * _gelu_grad(vpre)).astype(BF)

    return _pcall(
        body, name=name, grid=(T // tm,),
        ins=[(dmixed, (tm, MLA_OUT), lambda i: (i, 0)), (dmixed, (tm, SGU_OUT), lambda i: (i, 1)),
             (a, (tm, MLA_OUT), lambda i: (i, 0)), (proj, (tm, SGU_OUT), lambda i: (i, 1)),
             (proj, (tm, SGU_OUT), lambda i: (i, 2)), (g_mla, (1, MLA_OUT), lambda i: (0, 0)),
             (g_sgu, (1, SGU_OUT), lambda i: (0, 0)), (v_gain, (1, SGU_OUT), lambda i: (0, 0)),
             (w_tril, (GROUPS, CHUNK, CHUNK), lambda i: (0, 0, 0)), (w_tril_t, (GROUPS, CHUNK, CHUNK), lambda i: (0, 0, 0)),
             (b_full, (CHUNK, SGU_OUT), lambda i: (0, 0))],
        outs=[((T, MLA_OUT), BF, (tm, MLA_OUT), lambda i: (i, 0)),
              ((T, proj.shape[1]), BF, (tm, proj.shape[1]), lambda i: (i, 0)),
              ((1, MLA_OUT), F32, (1, MLA_OUT), lambda i: (0, 0)), ((1, SGU_OUT), F32, (1, SGU_OUT), lambda i: (0, 0)),
              ((1, SGU_OUT), F32, (1, SGU_OUT), lambda i: (0, 0)),
              ((GROUPS, CHUNK, CHUNK), F32, (GROUPS, CHUNK, CHUNK), lambda i: (0, 0, 0)),
              ((CHUNK, SGU_OUT), F32, (CHUNK, SGU_OUT), lambda i: (0, 0))],
        scratch=[pltpu.VMEM((tm, SGU_OUT), F32), pltpu.VMEM((tm, SGU_OUT), F32)], deps=deps)


def _shift_down(z, n, row):
    return jnp.where(row >= n, pltpu.roll(z, n, 0), 0.0)


def _shift_up(z, n, row, T):
    return jnp.where(row < T - n, pltpu.roll(z, T - n, 0), 0.0)


def conv_fwd(proj, conv_w, *, name):
    T, D3 = proj.shape
    D = D3 // 3
    tn = _tile(D, 256)
    nj = D // tn

    def body(b_ref, c_ref, x_ref, w_ref, o_ref):
        row = lax.broadcasted_iota(jnp.int32, (T, tn), 0)
        z = c_ref[...] * x_ref[...]
        zc = w_ref[2:3, :] * z + w_ref[1:2, :] * _shift_down(z, 1, row) + w_ref[0:1, :] * _shift_down(z, 2, row)
        o_ref[...] = (b_ref[...] * zc).astype(BF)

    return _pcall(
        body, name=name, grid=(nj,),
        ins=[(proj, (T, tn), lambda j: (0, j)), (proj, (T, tn), lambda j: (0, nj + j)),
             (proj, (T, tn), lambda j: (0, 2 * nj + j)), (conv_w, (3, tn), lambda j: (0, j))],
        outs=[((T, D), BF, (T, tn), lambda j: (0, j))], semantics=("parallel",))[0]


def conv_bwd(dg, proj, conv_w, *, name, deps=()):
    T, D3 = proj.shape
    D = D3 // 3
    tn = _tile(D, 256)
    nj = D // tn

    def body(dg_ref, b_ref, c_ref, x_ref, w_ref, dp_ref, dw_ref, dc_scr, dx_scr):
        part = pl.program_id(1)

        @pl.when(part == 0)
        def _():
            row = lax.broadcasted_iota(jnp.int32, (T, tn), 0)
            c, x = c_ref[...], x_ref[...]
            z = c * x
            z1 = _shift_down(z, 1, row)
            z2 = _shift_down(z, 2, row)
            dgv = dg_ref[...]
            zc = w_ref[2:3, :] * z + w_ref[1:2, :] * z1 + w_ref[0:1, :] * z2
            dp_ref[...] = (dgv * zc).astype(BF)
            dzc = dgv * b_ref[...]
            dw_ref[0:1, :] = jnp.sum(dzc * z2, axis=0, keepdims=True)
            dw_ref[1:2, :] = jnp.sum(dzc * z1, axis=0, keepdims=True)
            dw_ref[2:3, :] = jnp.sum(dzc * z, axis=0, keepdims=True)
            dz = (w_ref[2:3, :] * dzc + w_ref[1:2, :] * _shift_up(dzc, 1, row, T)
                  + w_ref[0:1, :] * _shift_up(dzc, 2, row, T))
            dc_scr[...] = (dz * x).astype(BF)
            dx_scr[...] = (dz * c).astype(BF)

        @pl.when(part == 1)
        def _():
            dp_ref[...] = dc_scr[...]

        @pl.when(part == 2)
        def _():
            dp_ref[...] = dx_scr[...]

    return _pcall(
        body, name=name, grid=(nj, 3),
        ins=[(dg, (T, tn), lambda j, p: (0, j)), (proj, (T, tn), lambda j, p: (0, j)),
             (proj, (T, tn), lambda j, p: (0, nj + j)), (proj, (T, tn), lambda j, p: (0, 2 * nj + j)),
             (conv_w, (3, tn), lambda j, p: (0, j))],
        outs=[((T, D3), BF, (T, tn), lambda j, p: (0, p * nj + j)), ((3, D), F32, (3, tn), lambda j, p: (0, j))],
        scratch=[pltpu.VMEM((T, tn), BF), pltpu.VMEM((T, tn), BF)], semantics=("parallel", "arbitrary"), deps=deps)


def loss_bwd(x_parts, gain, target, *, name):
    T, D = target.shape
    tm = _tile(T, ROW_TILE, 8)
    n_x = len(x_parts)

    def body(*refs):
        x_refs = refs[:n_x]
        g_ref, t_ref, dx_ref, dxb_ref, dg_ref, loss_ref = refs[n_x:]
        first = pl.program_id(0) == 0
        xv = jnp.concatenate([r[...] for r in x_refs], axis=-1) if n_x > 1 else x_refs[0][...]
        r = _rstd(xv)
        xh = xv * r
        gain_v = g_ref[...]
        err = xh * gain_v - t_ref[...]
        part = 0.5 * jnp.sum(jnp.mean(err * err, axis=-1, keepdims=True), axis=0, keepdims=True)
        _acc_rows(loss_ref, jnp.broadcast_to(part, (1, LANES)), first)
        dy = err * (1.0 / D)
        gdy = dy * gain_v
        dx = r * (gdy - xh * jnp.mean(gdy * xh, axis=-1, keepdims=True))
        dx_ref[...] = dx
        dxb_ref[...] = dx.astype(BF)
        _acc_rows(dg_ref, dy * xh, first)

    return _pcall(
        body, name=name, grid=(T // tm,),
        ins=[(p, (tm, D // n_x), lambda i: (i, 0)) for p in x_parts]
        + [(gain, (1, D), lambda i: (0, 0)), (target, (tm, D), lambda i: (i, 0))],
        outs=[((T, D), F32, (tm, D), lambda i: (i, 0)), ((T, D), BF, (tm, D), lambda i: (i, 0)),
              ((1, D), F32, (1, D), lambda i: (0, 0)), ((1, LANES), F32, (1, LANES), lambda i: (0, 0))])


def _adamw(g, w, m, v):
    m = ADAM_B1 * m + (1.0 - ADAM_B1) * g
    v = ADAM_B2 * v + (1.0 - ADAM_B2) * (g * g)
    m_hat = m / ADAM_C1
    v_hat = v / ADAM_C2
    delta = -ADAM_LR * (m_hat / (jnp.sqrt(v_hat) + ADAM_EPS) + ADAM_WD * w)
    return delta, m, v


def adam_flat(g, w, m, v, *, name):
    def body(g_ref, w_ref, m_ref, v_ref, d_ref, nm_ref, nv_ref):
        d, nm, nv = _adamw(g_ref[...], w_ref[...], m_ref[...], v_ref[...])
        d_ref[...] = d
        nm_ref[...] = nm
        nv_ref[...] = nv

    blk = g.shape
    zero = lambda: (0, 0)
    return _pcall(body, name=name, grid=(),
                  ins=[(t, blk, zero) for t in (g, w, m, v)],
                  outs=[(blk, F32, blk, zero)] * 3)


def _chip_slots():
    x, y, c = lax.axis_index("x"), lax.axis_index("y"), lax.axis_index("c")
    chips = [(1 - x, y), (x, 1 - y), (1 - x, 1 - y)]
    return x, y, c, chips


def device_index():
    x, y, c, chips = _chip_slots()
    return jnp.stack([4 * x + 2 * y + c, 2 * x + y] + [4 * cx + 2 * cy + c for cx, cy in chips]
                     + [2 * cx + cy for cx, cy in chips]).astype(jnp.int32)


def _job_rows(R, C, n_steps):
    if n_steps is None:
        n_steps = max(1, R * C // STREAM_BLOCK_ELEMS)
    n_blk = max([d for d in range(1, n_steps + 1) if R % d == 0 and (R // d) % 16 == 0] or [1])
    return R // n_blk, n_blk


def run_job(job, *, index, name, deps=()):
    jb = job(None)
    n_in = len(jb["ins"])

    def body(idx_ref, *refs):
        jb["fn"](refs[:n_in], refs[n_in:n_in + len(jb["outs"])])

    return _pcall(body, name=name, grid=(jb["n_blk"],), ins=jb["ins"], outs=jb["outs"], prefetch=index,
                  aliases={1 + a: o for a, o in jb["aliases"].items()}, semantics=("parallel",), deps=deps)


def adam_job(gs, a_buf, b_buf, w, m, v, layer, prev):
    L, R, C = w.shape

    def build(n_steps):
        tr, n_blk = _job_rows(R, C, n_steps)
        blk = (None, tr, C)
        row = lambda t: jnp.minimum(t, n_blk - 1)
        ins = [(gs, blk, lambda t, s: (s[0], row(t), 0)), (a_buf, blk, lambda t, s: (s[1], row(t), 0))]
        ins += [(b_buf, blk, lambda t, s, j=j: (j, row(t), 0)) for j in range(3)]
        ins += [(p, blk, lambda t, s: (layer, row(t), 0)) for p in (w, m, v)]
        ins += [(p, None, None) for p in (prev or [])]

        def fn(i, o):
            g = ((((i[0][...].astype(F32) + i[1][...].astype(F32)) + i[2][...].astype(F32))
                  + i[3][...].astype(F32)) + i[4][...].astype(F32))
            d, nm, nv = _adamw(g, i[5][...], i[6][...], i[7][...])
            o[0][...] = g
            o[1][...] = d
            o[2][...] = nm
            o[3][...] = nv

        return dict(ins=ins, outs=[((L, R, C), F32, blk, lambda t, s: (layer, row(t), 0))] * 4, fn=fn,
                    aliases={8 + o: o for o in range(4)} if prev else {}, n_blk=n_blk)

    return build


def pair_job(gs, a_buf):
    _, R, C = gs.shape

    def build(n_steps):
        tr, n_blk = _job_rows(R, C, n_steps)
        blk = (None, tr, C)
        row = lambda t: jnp.minimum(t, n_blk - 1)
        ins = [(gs, blk, lambda t, s, j=j: (s[2 + j], row(t), 0)) for j in range(3)]
        ins += [(a_buf, blk, lambda t, s, j=j: (s[5 + j], row(t), 0)) for j in range(3)]

        def fn(i, o):
            for j in range(3):
                o[0][j] = (i[j][...].astype(F32) + i[3 + j][...].astype(F32)).astype(BF)

        return dict(ins=ins, outs=[((3, R, C), BF, (3, tr, C), lambda t, s: (0, row(t), 0))], fn=fn, aliases={},
                    n_blk=n_blk)

    return build


def reduce_sum(gs, a_buf, b_buf, *, name):
    _, R, C = gs.shape
    tr = _tile(R, 256, 16)
    x, y, c, _ = _chip_slots()
    idx = jnp.stack([4 * x + 2 * y + c, 2 * x + y]).astype(jnp.int32)

    def body(idx_ref, g_ref, a_ref, b0_ref, b1_ref, b2_ref, o_ref):
        o_ref[...] = ((((g_ref[...].astype(F32) + a_ref[...].astype(F32)) + b0_ref[...].astype(F32))
                       + b1_ref[...].astype(F32)) + b2_ref[...].astype(F32))

    blk3 = (None, tr, C)
    return _pcall(body, name=name, grid=(R // tr,),
                  ins=[(gs, blk3, lambda i, s: (s[0], i, 0)), (a_buf, blk3, lambda i, s: (s[1], i, 0)),
                       (b_buf, blk3, lambda i, s: (0, i, 0)), (b_buf, blk3, lambda i, s: (1, i, 0)),
                       (b_buf, blk3, lambda i, s: (2, i, 0))],
                  outs=[((R, C), F32, (tr, C), lambda i, s: (i, 0))], prefetch=idx, semantics=("parallel",))[0]


def adam_rows(g, w, m, v, *, name):
    R, C = g.shape
    tr = _tile(R, 256, 8)

    def body(g_ref, w_ref, m_ref, v_ref, d_ref, nm_ref, nv_ref):
        d, nm, nv = _adamw(g_ref[...], w_ref[...], m_ref[...], v_ref[...])
        d_ref[...] = d
        nm_ref[...] = nm
        nv_ref[...] = nv

    spec = ((tr, C), lambda i: (i, 0))
    return _pcall(body, name=name, grid=(R // tr,), ins=[(t, *spec) for t in (g, w, m, v)],
                  outs=[((R, C), F32, *spec)] * 3, semantics=("parallel",))


def sum_rows8(gathered, rows, *, name):
    W = gathered.shape[1]

    def body(g_ref, o_ref):
        acc = g_ref[0:rows, :]
        for d in range(1, N_DEV):
            acc = acc + g_ref[d * rows:(d + 1) * rows, :]
        o_ref[...] = acc

    return _pcall(body, name=name, grid=(), ins=[(gathered, gathered.shape, lambda: (0, 0))],
                  outs=[((rows, W), F32, (rows, W), lambda: (0, 0))])[0]


HBM_SPEC = pl.BlockSpec(memory_space=pltpu.HBM)
SEM_SPEC = pl.BlockSpec(memory_space=pltpu.SEMAPHORE)
ANY_SPEC = pl.BlockSpec(memory_space=pl.ANY)
DATAFLOW = pltpu.SideEffectType.DATAFLOW_SIDE_EFFECTING


def _in_hbm(v):
    return pltpu.with_memory_space_constraint(v, pltpu.HBM)


def _slot(p):
    return 4 * p[0] + 2 * p[1] + p[2]


def _gather_peers():
    x, y, c, chips = _chip_slots()
    return (x, y, c), [(x, y, 1 - c)] + [(*chip, c) for chip in chips]


def gather_start(groups, after, *, name):
    flat = [s for g in groups for s in g]
    n, n_g = len(flat), len(groups)
    where = [(gi, ti) for gi, g in enumerate(groups) for ti in range(len(g))]

    def body(*refs):
        src, land = refs[:n], refs[n:2 * n]
        sems = refs[2 * n + 1:2 * n + 1 + 2 * n_g]
        me, peers = _gather_peers()
        for t in range(n):
            gi, ti = where[t]
            for k, to in enumerate(peers):
                pltpu.make_async_remote_copy(
                    src_ref=src[t], dst_ref=land[t].at[_slot(me)], send_sem=sems[2 * gi].at[4 * ti + k],
                    recv_sem=sems[2 * gi + 1].at[4 * ti + k], device_id=to, device_id_type=MESH).start()
        refs[-1][...] = jnp.zeros_like(refs[-1])

    out_shape = []
    for g in groups:
        out_shape += [pltpu.SemaphoreType.DMA((4 * len(g),)), pltpu.SemaphoreType.DMA((4 * len(g),))]
    out_shape += [pltpu.HBM(s.shape, s.dtype) for s in flat]
    out_shape += [pltpu.HBM((N_DEV,) + s.shape, s.dtype) for s in flat]
    out_shape += [jax.ShapeDtypeStruct((8, LANES), F32)]
    aliases = {t: 2 * n_g + t for t in range(n)}
    aliases.update({n + t: 2 * n_g + n + t for t in range(n)})
    res = pl.pallas_call(
        body, name=name, out_shape=out_shape, in_specs=[HBM_SPEC] * (2 * n) + [ANY_SPEC],
        out_specs=[SEM_SPEC] * (2 * n_g) + [HBM_SPEC] * (2 * n) + [pl.BlockSpec(memory_space=pltpu.VMEM)],
        input_output_aliases=aliases, compiler_params=pltpu.CompilerParams(has_side_effects=DATAFLOW),
    )(*[_in_hbm(s) for s in flat], *[_in_hbm(lax.empty((N_DEV,) + s.shape, s.dtype)) for s in flat], after)
    out, off = [], 0
    for gi, g in enumerate(groups):
        k = len(g)
        out.append((res[2 * gi], res[2 * gi + 1], res[2 * n_g + off:2 * n_g + off + k],
                    res[2 * n_g + n + off:2 * n_g + n + off + k]))
        off += k
    return out, res[-1]


def gather_wait(started, after, *, name):
    send_sems, recv_sems, srcs, lands = started
    n = len(srcs)
    after = list(after)

    def body(*refs):
        src, land = refs[:n], refs[n:2 * n]
        send, recv = refs[2 * n], refs[2 * n + 1]
        _, peers = _gather_peers()
        for t in range(n):
            for k, frm in enumerate(peers):
                cp = pltpu.make_async_remote_copy(
                    src_ref=src[t], dst_ref=land[t].at[_slot(frm)], send_sem=send.at[4 * t + k],
                    recv_sem=recv.at[4 * t + k],
                    device_id=frm, device_id_type=MESH)
                cp.wait_send()
                cp.wait_recv()

    res = pl.pallas_call(
        body, name=name,
        out_shape=[pltpu.HBM(s.shape, s.dtype) for s in srcs] + [pltpu.HBM(l.shape, l.dtype) for l in lands],
        in_specs=[HBM_SPEC] * (2 * n) + [SEM_SPEC, SEM_SPEC] + [ANY_SPEC] * len(after),
        out_specs=[HBM_SPEC] * (2 * n), input_output_aliases={t: t for t in range(2 * n)},
        compiler_params=pltpu.CompilerParams(has_side_effects=DATAFLOW),
    )(*srcs, *lands, send_sems, recv_sems, *after)
    return res[:n], res[n:]


def place_own(src, land, *, name):
    R, C = src.shape
    tr = _tile(R, 512, 16)
    x, y, c, _ = _chip_slots()
    idx = jnp.stack([4 * x + 2 * y + c]).astype(jnp.int32)

    def body(idx_ref, s_ref, land_ref, o_ref):
        o_ref[...] = s_ref[...]

    return _pcall(body, name=name, grid=(R // tr,),
                  ins=[(src, (tr, C), lambda i, s: (i, 0)), (land, None, None)],
                  outs=[(land.shape, land.dtype, (None, tr, C), lambda i, s: (s[0], i, 0))],
                  prefetch=idx, aliases={2: 0}, semantics=("parallel",))[0]


def gather_finish(srcs, lands, *, name):
    n = len(srcs)

    def body(*refs):
        land = refs[n:2 * n]
        send_sems, recv_sems = refs[2 * n:]
        x, y, c, chips = _chip_slots()
        me, sibling = (x, y, c), (x, y, 1 - c)

        def copy(t, j, block, to):
            return pltpu.make_async_remote_copy(
                src_ref=land[t].at[_slot(block)], dst_ref=land[t].at[_slot(block)], send_sem=send_sems.at[t, j],
                recv_sem=recv_sems.at[t, j], device_id=to, device_id_type=MESH)

        sends = [copy(t, j, (*chip, c), sibling) for t in range(n) for j, chip in enumerate(chips)]
        for cp in sends:
            cp.start()
        for t in range(n):
            for j, chip in enumerate(chips):
                copy(t, j, (*chip, 1 - c), me).wait_recv()
        for cp in sends:
            cp.wait_send()

    passed = pl.pallas_call(
        body, name=name, out_shape=[jax.ShapeDtypeStruct(l.shape, l.dtype) for l in lands],
        in_specs=[ANY_SPEC] * n, out_specs=[ANY_SPEC] * n,
        input_output_aliases={t: t for t in range(n)},
        scratch_shapes=[pltpu.SemaphoreType.DMA((n, 3)), pltpu.SemaphoreType.DMA((n, 3))],
    )(*lands)
    return [place_own(s, l, name=f"{name}_own{t}") for t, (s, l) in enumerate(zip(srcs, passed))]


def chips_start(pairs, *, name):
    n = len(pairs)

    def body(*refs):
        src, land = refs[:n], refs[n:2 * n]
        send, recv = refs[2 * n], refs[2 * n + 1]
        token = refs[-1]
        x, y, c, chips = _chip_slots()
        for t in range(n):
            for j, chip in enumerate(chips):
                pltpu.make_async_remote_copy(
                    src_ref=src[t].at[j], dst_ref=land[t].at[j], send_sem=send.at[3 * t + j],
                    recv_sem=recv.at[3 * t + j], device_id=(*chip, c), device_id_type=MESH).start()
        token[...] = jnp.zeros_like(token)

    res = pl.pallas_call(
        body, name=name,
        out_shape=[pltpu.SemaphoreType.DMA((3 * n,)), pltpu.SemaphoreType.DMA((3 * n,))]
        + [pltpu.HBM(p.shape, p.dtype) for p in pairs] * 2 + [jax.ShapeDtypeStruct((8, LANES), F32)],
        in_specs=[HBM_SPEC] * (2 * n),
        out_specs=[SEM_SPEC, SEM_SPEC] + [HBM_SPEC] * (2 * n) + [pl.BlockSpec(memory_space=pltpu.VMEM)],
        input_output_aliases={t: 2 + t for t in range(2 * n)},
        compiler_params=pltpu.CompilerParams(has_side_effects=DATAFLOW),
    )(*[_in_hbm(p) for p in pairs], *[_in_hbm(lax.empty(p.shape, p.dtype)) for p in pairs])
    return res[0], res[1], res[2:2 + n], res[2 + n:2 + 2 * n], res[-1]


def chips_wait(started, after, *, name):
    send_sems, recv_sems, srcs, lands, _ = started
    n = len(srcs)

    def body(*refs):
        src, land = refs[:n], refs[n:2 * n]
        send, recv = refs[2 * n], refs[2 * n + 1]
        x, y, c, chips = _chip_slots()
        for t in range(n):
            for j, chip in enumerate(chips):
                cp = pltpu.make_async_remote_copy(
                    src_ref=src[t].at[j], dst_ref=land[t].at[j], send_sem=send.at[3 * t + j],
                    recv_sem=recv.at[3 * t + j], device_id=(*chip, c), device_id_type=MESH)
                cp.wait_send()
                cp.wait_recv()

    res = pl.pallas_call(
        body, name=name, out_shape=[pltpu.HBM(s.shape, s.dtype) for s in srcs] * 2,
        in_specs=[HBM_SPEC] * (2 * n) + [SEM_SPEC, SEM_SPEC, ANY_SPEC], out_specs=[HBM_SPEC] * (2 * n),
        input_output_aliases={t: t for t in range(2 * n)},
        compiler_params=pltpu.CompilerParams(has_side_effects=DATAFLOW),
    )(*srcs, *lands, send_sems, recv_sems, after)
    return res[n:]


def _sibling_copies(src, land, send, recv, n):
    x, y, c, _ = _chip_slots()
    return [pltpu.make_async_remote_copy(
        src_ref=src[t].at[4 * (q // 2) + 2 * (q % 2) + (1 - c)], dst_ref=land[t].at[q], send_sem=send.at[4 * t + q],
        recv_sem=recv.at[4 * t + q], device_id=(x, y, 1 - c), device_id_type=MESH)
        for t in range(n) for q in range(4)]


def sibling_start(gs, *, name):
    n = len(gs)

    def body(*refs):
        for cp in _sibling_copies(refs[:n], refs[n:2 * n], refs[2 * n], refs[2 * n + 1], n):
            cp.start()
        refs[-1][...] = jnp.zeros_like(refs[-1])

    lands = [lax.empty((4,) + g.shape[1:], g.dtype) for g in gs]
    res = pl.pallas_call(
        body, name=name,
        out_shape=[pltpu.SemaphoreType.DMA((4 * n,)), pltpu.SemaphoreType.DMA((4 * n,))]
        + [pltpu.HBM(g.shape, g.dtype) for g in gs] + [pltpu.HBM(l.shape, l.dtype) for l in lands]
        + [jax.ShapeDtypeStruct((8, LANES), F32)],
        in_specs=[HBM_SPEC] * (2 * n),
        out_specs=[SEM_SPEC, SEM_SPEC] + [HBM_SPEC] * (2 * n) + [pl.BlockSpec(memory_space=pltpu.VMEM)],
        input_output_aliases={t: 2 + t for t in range(2 * n)},
        compiler_params=pltpu.CompilerParams(has_side_effects=DATAFLOW),
    )(*[_in_hbm(g) for g in gs], *[_in_hbm(l) for l in lands])
    return res[0], res[1], res[2:2 + n], res[2 + n:2 + 2 * n], res[-1]


def sibling_wait(started, after, *, name):
    send_sems, recv_sems, srcs, lands, _ = started
    n = len(srcs)

    def body(*refs):
        for cp in _sibling_copies(refs[:n], refs[n:2 * n], refs[2 * n], refs[2 * n + 1], n):
            cp.wait_send()
            cp.wait_recv()

    res = pl.pallas_call(
        body, name=name,
        out_shape=[pltpu.HBM(s.shape, s.dtype) for s in srcs] + [pltpu.HBM(l.shape, l.dtype) for l in lands],
        in_specs=[HBM_SPEC] * (2 * n) + [SEM_SPEC, SEM_SPEC, ANY_SPEC], out_specs=[HBM_SPEC] * (2 * n),
        input_output_aliases={t: t for t in range(2 * n)},
        compiler_params=pltpu.CompilerParams(has_side_effects=DATAFLOW),
    )(*srcs, *lands, send_sems, recv_sems, after)
    return res[:n], res[n:]


def _rope_slab(cols):
    z = jnp.zeros(cols.shape[:-1] + (HALF_ROPE,), cols.dtype)
    return jnp.concatenate([cols[..., :HALF_ROPE], z, cols[..., HALF_ROPE:], z], axis=-1)


def _rope_unslab(slab):
    return jnp.concatenate([slab[..., :HALF_ROPE], slab[..., 2 * HALF_ROPE:3 * HALF_ROPE]], axis=-1)


def _pack_w_in_t(wt_g):
    s, c, d = wt_g.shape
    w = wt_g.reshape(s * c, d)
    c2, c3 = Q_LORA + KV_LORA, Q_LORA + KV_LORA + QK_ROPE
    r = w[c2:c3]
    z = jnp.zeros((HALF_ROPE, d), w.dtype)
    return jnp.concatenate([w[:c2], w[c3:], r[:HALF_ROPE], z, r[HALF_ROPE:], z], axis=0)


def unpack_w_in_t_grad(dwt, *, name):
    n_rows, d = dwt.shape
    kr = QK_ROPE
    n_out_rows = n_rows - kr
    blk = n_out_rows // 7
    assert blk * 7 == n_out_rows and blk % kr == 0 and n_rows % (2 * kr) == 0
    kr_row = Q_LORA + KV_LORA
    k_mix = kr_row // blk
    off = kr_row - k_mix * blk
    slab_block = (n_rows - 2 * kr) // (2 * kr)

    def body(prev_ref, in_ref, slab_ref, o_ref):
        k = pl.program_id(0)

        @pl.when(k < k_mix)
        def _():
            o_ref[...] = in_ref[...]

        @pl.when(k == k_mix)
        def _():
            o_ref[:off, :] = in_ref[:off, :]
            o_ref[off:off + HALF_ROPE, :] = slab_ref[:HALF_ROPE, :]
            o_ref[off + HALF_ROPE:off + kr, :] = slab_ref[2 * HALF_ROPE:3 * HALF_ROPE, :]
            o_ref[off + kr:, :] = in_ref[off:blk - kr, :]

        @pl.when(k > k_mix)
        def _():
            o_ref[:kr, :] = prev_ref[blk - kr:, :]
            o_ref[kr:, :] = in_ref[:blk - kr, :]

    out = _pcall(body, name=name, grid=(7,),
                 ins=[(dwt, (blk, d), lambda k: (jnp.maximum(k - 1, 0), 0)), (dwt, (blk, d), lambda k: (k, 0)),
                      (dwt, (2 * kr, d), lambda k: (slab_block, 0))],
                 outs=[((n_out_rows, d), dwt.dtype, (blk, d), lambda k: (k, 0))], semantics=("parallel",))[0]
    return out.reshape(N_DEV, n_out_rows // N_DEV, d)


def _rope_tables(positions):
    inv_freq = ROPE_BASE ** (-jnp.arange(0, QK_ROPE, 2, dtype=F32) / QK_ROPE)
    ang = positions.astype(F32)[:, None] * inv_freq
    cos, sin = jnp.cos(ang), jnp.sin(ang)
    z = jnp.zeros_like(cos)
    return jnp.concatenate([cos, z, cos, z], axis=-1), jnp.concatenate([-sin, z, sin, z], axis=-1)


def _mlp_up(x, gain, w1, tag):
    hn = rms_fwd(x, gain, name=f"mlp{tag}_norm")

    def act_epi(acc):
        a = jnp.maximum(acc, 0.0)
        return a, a * a

    T = x.shape[0]
    F = w1.shape[0] * w1.shape[2]
    a, act = mm(hn, w1, name=f"mlp{tag}_up", outs=[((T, F), BF, None), ((T, F), BF, None)], epi=act_epi)
    return hn, a, act


def _mlp_down(x, act, w2, tag, part=0):
    n = w2.shape[1]
    bm = _tile(x.shape[0], MM_TILE)
    bn = _tile(n, MM_TILE)
    per = n // bn
    return mm(act, w2, name=f"mlp{tag}_down{part}", out=((x.shape[0], n), F32), bm=bm, bn=bn,
              epi=lambda acc, r: (acc + r[...],), epi_ins=[(x, (bm, bn), lambda i, j, k: (i, part * per + j))])


def _mlp_bwd_weights(w1, w2, saved, dxb, tag):
    hn, a, act = saved
    T, D = dxb.shape
    F = a.shape[1]
    bm = _tile(T, MM_TILE)
    bn = _tile(F, min(MM_TILE, w1.shape[2]))
    dhid = mm(dxb, w2, tb=True, name=f"mlp{tag}_dhid", out=((T, F), BF), bm=bm, bn=bn,
              epi=lambda acc, a_ref: (2.0 * a_ref[...].astype(F32) * acc,),
              epi_ins=[(a, (bm, bn), lambda i, j, k: (i, j))])
    dw2 = mm(act, dxb, ta=True, name=f"mlp{tag}_dw2", out=((F, D), BF))
    dw1 = mm(hn, dhid, ta=True, name=f"mlp{tag}_dw1", out=(w1.shape, BF))
    return dhid, dw1, dw2.reshape(N_DEV, F // N_DEV, D)


def _reduce_begin(grads, tag):
    return sibling_start(grads, name=f"reduce_sibling_start_{tag}")


def _reduce_continue(sib, after, tag, index):
    grads, a_bufs = sibling_wait(sib, after, name=f"reduce_sibling_wait_{tag}")
    pairs = [run_job(pair_job(g, a), index=index, name=f"pair_sum_{tag}{t}")[0]
             for t, (g, a) in enumerate(zip(grads, a_bufs))]
    return grads, a_bufs, chips_start(pairs, name=f"reduce_chips_start_{tag}")


def kernel(x, positions, e_norm_mix, e_w_in, e_q_norm, e_w_uq, e_kv_norm, e_w_ukv, e_v_norm, e_sgu_w, e_sgu_b, e_mla_out_norm, e_sgu_out_norm, e_w_out, o_norm_mix, o_w_in, o_conv_w, o_w_out, mlp_norm, mlp_w1, mlp_w2, final_norm, loss_target, m_e_norm_mix, m_e_w_in, m_e_q_norm, m_e_w_uq, m_e_kv_norm, m_e_w_ukv, m_e_v_norm, m_e_sgu_w, m_e_sgu_b, m_e_mla_out_norm, m_e_sgu_out_norm, m_e_w_out, m_o_norm_mix, m_o_w_in, m_o_conv_w, m_o_w_out, m_mlp_norm, m_mlp_w1, m_mlp_w2, m_final_norm, v_e_norm_mix, v_e_w_in, v_e_q_norm, v_e_w_uq, v_e_kv_norm, v_e_w_ukv, v_e_v_norm, v_e_sgu_w, v_e_sgu_b, v_e_mla_out_norm, v_e_sgu_out_norm, v_e_w_out, v_o_norm_mix, v_o_w_in, v_o_conv_w, v_o_w_out, v_mlp_norm, v_mlp_w1, v_mlp_w2, v_final_norm):
    T, D = x.shape[1], x.shape[2]
    d_shard = o_norm_mix.shape[1]
    x0 = x[0]
    target = loss_target[0]
    me = 4 * lax.axis_index("x") + 2 * lax.axis_index("y") + lax.axis_index("c")

    bf = lambda s: s.astype(BF)
    gather_groups = [[bf(jnp.transpose(e_w_in[0])), bf(e_w_uq[0]), bf(e_w_ukv[0])], [bf(e_w_out[0]), bf(mlp_w1[0])],
                     [bf(mlp_w2[0]), bf(o_w_in[0])], [bf(o_w_out[0]), bf(mlp_w1[1])], [bf(mlp_w2[1])]]
    small_rows = jnp.concatenate([o_norm_mix, o_conv_w[0], jnp.zeros((4, d_shard), F32)], axis=0)
    gather_groups[0].insert(0, small_rows)
    started, start_token = gather_start(gather_groups[:1], x0, name="gather_start0")
    started_rest, rest_token = gather_start(gather_groups[1:], start_token, name="gather_start1")
    started += started_rest

    def gathered(gi, after):
        srcs, lands = gather_wait(started[gi], after, name=f"gather_wait{gi}")
        return gather_finish(srcs, lands, name=f"gather_finish{gi}")

    w_tril = jnp.tril(e_sgu_w[0])
    w_tril_b = w_tril.astype(BF)
    w_tril_tb = jnp.swapaxes(w_tril, 1, 2).astype(BF)
    b_full = jnp.repeat(e_sgu_b[0].T, CH, axis=1)
    v_gain = e_v_norm[0].reshape(1, SGU_OUT)
    cos_t, sin_t = _rope_tables(positions[0])
    mlp_gain = [mlp_norm[0:1], mlp_norm[1:2]]
    final_gain = final_norm.reshape(1, D)

    h0 = rms_fwd(x0, e_norm_mix, name="e_norm", deps=[rest_token])
    small_g, g_w_in_t, g_w_uq, w_ukv = gathered(0, [h0, cos_t, sin_t, w_tril_b, w_tril_tb, b_full])
    o_norm_full = small_g[:, 0, :].reshape(1, D)
    conv_w_full = jnp.transpose(small_g[:, 1:4, :], (1, 0, 2)).reshape(3, D)
    w_in_t = _pack_w_in_t(g_w_in_t)
    w_uq = jnp.concatenate([g_w_uq[..., :QK_NOPE], _rope_slab(g_w_uq[..., QK_NOPE:])], axis=-1)
    proj = mm(h0, w_in_t, tb=True, name="e_in", out=((T, w_in_t.shape[0]), F32), bn=_tile(w_in_t.shape[0], 640))
    qn, kvn, krope = mla_prep(proj, e_q_norm, e_kv_norm, cos_t, sin_t, name="mla_prep")
    bm = _tile(T, MM_TILE)

    def q_epi(acc, cos_ref, sin_ref):
        return (jnp.concatenate([acc[:, :QK_NOPE], _rope_fwd(acc[:, QK_NOPE:], cos_ref[...], sin_ref[...])], axis=-1),)

    q = mm(qn, w_uq, name="mla_q", out=((T, HEADS * HEAD_PAD), BF), bm=bm, bn=HEAD_PAD, epi=q_epi,
           epi_ins=[(cos_t, (bm, LANES), lambda i, j, k: (i, 0)), (sin_t, (bm, LANES), lambda i, j, k: (i, 0))])

    def kv_epi(acc, kr_ref):
        return jnp.concatenate([acc[:, :QK_NOPE].astype(BF), kr_ref[...]], axis=-1), acc[:, QK_NOPE:]

    k, v = mm(kvn, w_ukv, name="mla_kv", bm=bm, bn=HEAD_PAD, epi=kv_epi,
              outs=[((T, HEADS * HEAD_PAD), BF, HEAD_PAD), ((T, MLA_OUT), BF, V_HEAD)],
              epi_ins=[(krope, (bm, LANES), lambda i, j, k: (i, 0))])
    attn, attn_lse = attn_fwd(q, k, v, name="attn_fwd")
    mixed = mix_fwd(attn, proj, e_mla_out_norm, e_sgu_out_norm, v_gain, w_tril_b, b_full, name="mix_fwd")
    bn = _tile(D, MM_TILE)
    g_w_out_e, w1_0 = gathered(1, [mixed])
    w_out_e = g_w_out_e.reshape(-1, D)
    x1 = mm(mixed, w_out_e, name="e_out", out=((T, D), F32), bm=bm, bn=bn,
            epi=lambda acc, r: (acc + r[...],), epi_ins=[(x0, (bm, bn), lambda i, j, k: (i, j))])
    hn0, a0, act0 = _mlp_up(x1, mlp_gain[0], w1_0, 0)
    g_w2_0, g_w_in_o = gathered(2, [act0])
    w2_0 = g_w2_0.reshape(-1, D)
    x2 = _mlp_down(x1, act0, w2_0, 0)
    ho = rms_fwd(x2, o_norm_full, name="o_norm")
    proj_o = mm(ho, g_w_in_o, name="o_in", out=((T, 3 * D), F32))
    gated = conv_fwd(proj_o, conv_w_full, name="conv_fwd")
    g_w_out_o, w1_1 = gathered(3, [gated])
    w_out_o = g_w_out_o.reshape(-1, D)
    x3 = mm(gated, w_out_o, name="o_out", out=((T, D), F32), bm=bm, bn=bn,
            epi=lambda acc, r: (acc + r[...],), epi_ins=[(x2, (bm, bn), lambda i, j, k: (i, j))])
    hn1, a1, act1 = _mlp_up(x3, mlp_gain[1], w1_1, 1)
    (g_w2_1,) = gathered(4, [act1])
    w2_1 = g_w2_1.reshape(-1, D)
    x4 = _mlp_down(x3, act1, w2_1, 1)
    w1, w2 = [w1_0, w1_1], [w2_0, w2_1]

    dx4, dx4b, d_final, loss_part = loss_bwd([x4], final_gain, target, name="loss_bwd")

    hosted = dict(job_index=device_index())
    dhid1, dw1_1, dw2_1 = _mlp_bwd_weights(w1[1], w2[1], (hn1, a1, act1), dx4b, 1)
    sib_r0 = _reduce_begin([dw1_1, dw2_1], "r0")
    dhn1 = mm(dhid1, w1[1], tb=True, name="mlp1_dhn", out=((T, D), F32), deps=[sib_r0[-1]])
    grads_r0, a_r0 = sibling_wait(sib_r0, dhn1, name="reduce_sibling_wait_r0")
    dx3, dx3b, d_mlp1 = rms_bwd(x3, mlp_gain[1], dhn1, dres=dx4, name="mlp1_norm_bwd")

    dgated, ((pair_r0a,),) = mm(dx3b, w_out_o, tb=True, name="o_out_dx", out=((T, D), F32),
                                jobs=[pair_job(grads_r0[0], a_r0[0])], **hosted)
    dw_out_o, ((pair_r0b,),) = mm(gated, dx3b, ta=True, name="o_out_dw", out=((D, D), BF),
                                  jobs=[pair_job(grads_r0[1], a_r0[1])], **hosted)
    st_r0 = chips_start([pair_r0a, pair_r0b], name="reduce_chips_start_r0")
    dproj_o, dconv_full = conv_bwd(dgated, proj_o, conv_w_full, name="conv_bwd", deps=[st_r0[-1]])
    dw_in_o = mm(ho, dproj_o, ta=True, name="o_in_dw", out=(g_w_in_o.shape, BF))
    sib_r1 = _reduce_begin([dw_out_o.reshape(g_w_out_o.shape), dw_in_o], "r1")
    dho = mm(dproj_o, g_w_in_o, tb=True, name="o_in_dx", out=((T, D), F32), deps=[sib_r1[-1]])
    grads_r1, a_r1 = sibling_wait(sib_r1, dho, name="reduce_sibling_wait_r1")
    dx2, dx2b, d_onorm_full = rms_bwd(x2, o_norm_full, dho, dres=dx3, name="o_norm_bwd")

    d_ff = a0.shape[1]
    bm_h, bn_h = _tile(T, MM_TILE), _tile(d_ff, min(MM_TILE, w1[0].shape[2]))
    dhid0, ((pair_r1a,), (pair_r1b,)) = mm(
        dx2b, w2[0], tb=True, name="mlp0_dhid", out=((T, d_ff), BF), bm=bm_h, bn=bn_h,
        epi=lambda acc, a_ref: (2.0 * a_ref[...].astype(F32) * acc,),
        epi_ins=[(a0, (bm_h, bn_h), lambda i, j, k: (i, j))],
        jobs=[pair_job(grads_r1[0], a_r1[0]), pair_job(grads_r1[1], a_r1[1])], **hosted)
    st_r1 = chips_start([pair_r1a, pair_r1b], name="reduce_chips_start_r1")
    dw2_0 = mm(act0, dx2b, ta=True, name="mlp0_dw2", out=((d_ff, D), BF), deps=[st_r1[-1]])
    b_r0 = chips_wait(st_r0, dw2_0, name="reduce_chips_wait_r0")
    dw1_0, (r_w1, r_w2) = mm(
        hn0, dhid0, ta=True, name="mlp0_dw1", out=(w1[0].shape, BF),
        jobs=[adam_job(grads_r0[0], a_r0[0], b_r0[0], mlp_w1, m_mlp_w1, v_mlp_w1, 1, None),
              adam_job(grads_r0[1], a_r0[1], b_r0[1], mlp_w2, m_mlp_w2, v_mlp_w2, 1, None)], **hosted)
    sib_r2 = _reduce_begin([dw1_0, dw2_0.reshape(N_DEV, d_ff // N_DEV, D)], "r2")
    dhn0 = mm(dhid0, w1[0], tb=True, name="mlp0_dhn", out=((T, D), F32), deps=[sib_r2[-1]])
    grads_r2, a_r2 = sibling_wait(sib_r2, dhn0, name="reduce_sibling_wait_r2")
    dx1, dx1b, d_mlp0 = rms_bwd(x1, mlp_gain[0], dhn0, dres=dx2, name="mlp0_norm_bwd")

    dmixed, ((pair_r2a,),) = mm(dx1b, w_out_e, tb=True, name="e_out_dx", out=((T, MLA_OUT + SGU_OUT), F32),
                                jobs=[pair_job(grads_r2[0], a_r2[0])], **hosted)
    dw_out_e, ((pair_r2b,),) = mm(mixed, dx1b, ta=True, name="e_out_dw", out=(w_out_e.shape, BF),
                                  jobs=[pair_job(grads_r2[1], a_r2[1])], **hosted)
    st_r2 = chips_start([pair_r2a, pair_r2b], name="reduce_chips_start_r2")
    (dattn, dproj, d_mla_out, d_sgu_out, d_vgain, d_sgu_w, d_b_full) = mix_bwd(
        dmixed, attn, proj, e_mla_out_norm, e_sgu_out_norm, v_gain, w_tril_b, w_tril_tb, b_full, name="mix_bwd",
        deps=[st_r2[-1]])
    b_r1 = chips_wait(st_r1, dattn, name="reduce_chips_wait_r1")
    dq, dk, dv = attn_bwd(q, k, v, attn, attn_lse, dattn, name="attn_bwd")
    dq_lin, dkv_lin, dproj = mla_bwd_prep(dq, dk, dv, cos_t, sin_t, dproj, name="mla_bwd_prep")
    dw_uq_pad = mm(qn, dq_lin, ta=True, name="mla_q_dw", out=(w_uq.shape, BF))
    dw_ukv = mm(kvn, dkv_lin, ta=True, name="mla_kv_dw", out=(w_ukv.shape, BF))
    dw_uq = jnp.concatenate([dw_uq_pad[..., :QK_NOPE], _rope_unslab(dw_uq_pad[..., QK_NOPE:])], axis=-1)
    sib_r2b = _reduce_begin([dw_out_e.reshape(g_w_out_e.shape), dw_uq, dw_ukv], "r2b")
    dqn = mm(dq_lin, w_uq, tb=True, name="mla_q_dx", out=((T, Q_LORA), F32), deps=[sib_r2b[-1]])
    dkvn = mm(dkv_lin, w_ukv, tb=True, name="mla_kv_dx", out=((T, KV_LORA), F32), deps=[sib_r2b[-1]])
    grads_r2b, a_r2b, st_r2b = _reduce_continue(sib_r2b, dkvn, "r2b", hosted["job_index"])
    dproj, d_qnorm = rms_bwd(proj, e_q_norm, dqn, col_block=0, want_f32=False, into=dproj, name="q_norm_bwd",
                             deps=[st_r2b[-1]])
    dproj, d_kvnorm = rms_bwd(proj, e_kv_norm, dkvn, col_block=1, want_f32=False, into=dproj, name="kv_norm_bwd")
    dw_in_t_pad, (r_w_out_o, r_w_in_o) = mm(
        dproj, h0, ta=True, name="e_in_dw", out=(w_in_t.shape, BF), bm=_tile(w_in_t.shape[0], 640),
        jobs=[adam_job(grads_r1[0], a_r1[0], b_r1[0], o_w_out, m_o_w_out, v_o_w_out, 0, None),
              adam_job(grads_r1[1], a_r1[1], b_r1[1], o_w_in, m_o_w_in, v_o_w_in, 0, None)], **hosted)
    dw_in_t = unpack_w_in_t_grad(dw_in_t_pad, name="e_in_dw_unpack")
    sib_r3 = _reduce_begin([dw_in_t], "r3")
    dh0 = mm(dproj, w_in_t, name="e_in_dx", out=((T, D), F32), deps=[sib_r3[-1]])
    grads_r3, a_r3, st_r3 = _reduce_continue(sib_r3, dh0, "r3", hosted["job_index"])
    tok_r3 = st_r3[-1]
    grad_x, d_enorm = rms_bwd(x0, e_norm_mix, dh0, dres=dx1, want_bf=False, name="e_norm_bwd", deps=[tok_r3])
    b_r2 = chips_wait(st_r2, grad_x, name="reduce_chips_wait_r2")

    d_sgu_b = jnp.transpose(d_b_full[:, ::CH])
    d_sgu_w_tril = jnp.tril(d_sgu_w)
    rep = [("e_norm_mix", e_norm_mix, m_e_norm_mix, v_e_norm_mix, d_enorm),
           ("e_q_norm", e_q_norm, m_e_q_norm, v_e_q_norm, d_qnorm),
           ("e_kv_norm", e_kv_norm, m_e_kv_norm, v_e_kv_norm, d_kvnorm),
           ("e_v_norm", e_v_norm, m_e_v_norm, v_e_v_norm, d_vgain),
           ("e_sgu_w", e_sgu_w, m_e_sgu_w, v_e_sgu_w, d_sgu_w_tril),
           ("e_sgu_b", e_sgu_b, m_e_sgu_b, v_e_sgu_b, d_sgu_b),
           ("e_mla_out_norm", e_mla_out_norm, m_e_mla_out_norm, v_e_mla_out_norm, d_mla_out),
           ("e_sgu_out_norm", e_sgu_out_norm, m_e_sgu_out_norm, v_e_sgu_out_norm, d_sgu_out),
           ("mlp_norm", mlp_norm, m_mlp_norm, v_mlp_norm, jnp.concatenate([d_mlp0, d_mlp1], axis=0)),
           ("final_norm", final_norm, m_final_norm, v_final_norm, d_final)]
    sizes = [int(np.prod(r[1].shape)) for r in rep]
    n_rep = sum(sizes)
    n_all = n_rep + 4 * D + 1
    width = -(-n_all // (8 * LANES)) * LANES
    pad = 8 * width - n_all
    flat = jnp.concatenate([r[4].reshape(-1) for r in rep]
                           + [d_onorm_full.reshape(-1), dconv_full.reshape(-1), loss_part[0, :1],
                              jnp.zeros((pad,), F32)])
    small_started, small_token = gather_start([[flat.reshape(8, width)]], b_r2[0], name="gather_small_grads_start")

    def finish(grads, a_bufs, b_bufs, t, w, m, v, layer=0, prev=None, tag="", deps=()):
        return run_job(adam_job(grads[t], a_bufs[t], b_bufs[t], w, m, v, layer, prev), index=hosted["job_index"],
                       name=f"adam_{tag}", deps=deps)

    r_w1 = finish(grads_r2, a_r2, b_r2, 0, mlp_w1, m_mlp_w1, v_mlp_w1, 0, r_w1, tag="w1_l0", deps=[tok_r3, small_token])
    r_w2 = finish(grads_r2, a_r2, b_r2, 1, mlp_w2, m_mlp_w2, v_mlp_w2, 0, r_w2, tag="w2_l0", deps=[r_w1[1]])
    b_r2b = chips_wait(st_r2b, r_w2[1], name="reduce_chips_wait_r2b")
    r_w_out_e = finish(grads_r2b, a_r2b, b_r2b, 0, e_w_out, m_e_w_out, v_e_w_out, tag="e_w_out")
    r_w_uq = finish(grads_r2b, a_r2b, b_r2b, 1, e_w_uq, m_e_w_uq, v_e_w_uq, tag="e_w_uq")
    r_w_ukv = finish(grads_r2b, a_r2b, b_r2b, 2, e_w_ukv, m_e_w_ukv, v_e_w_ukv, tag="e_w_ukv")
    b_r3 = chips_wait(st_r3, r_w_out_e[1], name="reduce_chips_wait_r3")
    g_w_in_t = reduce_sum(grads_r3[0], a_r3[0], b_r3[0], name="sum_e_w_in")
    w_in_upd_t = adam_rows(g_w_in_t, jnp.transpose(e_w_in[0]), jnp.transpose(m_e_w_in[0]), jnp.transpose(v_e_w_in[0]),
                           name="adam_e_w_in")
    r_w_in = [jnp.transpose(t)[None] for t in (g_w_in_t, *w_in_upd_t)]

    small_srcs, small_lands = gather_wait(small_started[0], [r_w2[1]], name="gather_small_grads_wait")
    small_all = gather_finish(small_srcs, small_lands, name="gather_small_grads_finish")[0]
    summed = sum_rows8(small_all.reshape(N_DEV * 8, width), 8, name="sum_small_grads").reshape(-1)

    loss = summed[n_rep + 4 * D]

    def pack_rep(i):
        return jnp.concatenate([r[i].reshape(-1) for r in rep]).reshape(n_rep // LANES, LANES)

    g_rep = summed[:n_rep].reshape(n_rep // LANES, LANES)
    d_rep, nm_rep, nv_rep = adam_flat(g_rep, pack_rep(1), pack_rep(2), pack_rep(3), name="adam_replicated")

    def unpack_rep(flat2d):
        out, off = {}, 0
        f = flat2d.reshape(-1)
        for r, n in zip(rep, sizes):
            out[r[0]] = f[off:off + n].reshape(r[1].shape)
            off += n
        return out

    small = {"grad": unpack_rep(g_rep), "delta": unpack_rep(d_rep), "new_m": unpack_rep(nm_rep),
             "new_v": unpack_rep(nv_rep)}
    g_onorm = lax.dynamic_slice(summed[n_rep:n_rep + D].reshape(1, D), (0, me * d_shard), (1, d_shard))
    g_conv = lax.dynamic_slice(summed[n_rep + D:n_rep + 4 * D].reshape(3, D), (0, me * d_shard), (3, d_shard))

    def pack_sharded(norm_part, conv_part):
        return jnp.concatenate([norm_part, conv_part, jnp.zeros((4, d_shard), F32)], axis=0)

    g_sh = pack_sharded(g_onorm, g_conv)
    d_sh, nm_sh, nv_sh = adam_flat(g_sh, pack_sharded(o_norm_mix, o_conv_w[0]), pack_sharded(m_o_norm_mix, m_o_conv_w[0]),
                                   pack_sharded(v_o_norm_mix, v_o_conv_w[0]), name="adam_sharded_small")
    for kind, arr in (("grad", g_sh), ("delta", d_sh), ("new_m", nm_sh), ("new_v", nv_sh)):
        small[kind]["o_norm_mix"] = arr[0:1]
        small[kind]["o_conv_w"] = arr[1:4][None]

    big = {"e_w_in": r_w_in, "e_w_uq": r_w_uq, "e_w_ukv": r_w_ukv, "e_w_out": r_w_out_e, "o_w_in": r_w_in_o,
           "o_w_out": r_w_out_o, "mlp_w1": r_w1, "mlp_w2": r_w2}
    order = ["e_norm_mix", "e_w_in", "e_q_norm", "e_w_uq", "e_kv_norm", "e_w_ukv", "e_v_norm", "e_sgu_w", "e_sgu_b",
             "e_mla_out_norm", "e_sgu_out_norm", "e_w_out", "o_norm_mix", "o_w_in", "o_conv_w", "o_w_out", "mlp_norm",
             "mlp_w1", "mlp_w2", "final_norm"]
    result = [loss, grad_x[None]]
    for ki, kind in enumerate(("grad", "delta", "new_m", "new_v")):
        for nm in order:
            result.append(big[nm][ki] if nm in big else small[kind][nm])
    return tuple(result)
```

```python
import numpy as np
import jax
import jax.numpy as jnp
from jax import lax
from jax.experimental import pallas as pl
from jax.experimental.pallas import tpu as pltpu

BF = jnp.bfloat16
F32 = jnp.float32
MESH = pl.DeviceIdType.MESH
N_DEV = 8

EPS = 1e-6
HEADS = 8
Q_LORA = 512
KV_LORA = 512
QK_NOPE = 128
QK_ROPE = 64
HALF_ROPE = QK_ROPE // 2
V_HEAD = 128
HEAD_PAD = 256
ROPE_BASE = 10000.0
GROUPS = 8
CH = 128
CHUNK = 128
SGU_OUT = GROUPS * CH
MLA_OUT = HEADS * V_HEAD
ATTN_SCALE = float((QK_NOPE + QK_ROPE) ** -0.5)

ADAM_LR = 0.001
ADAM_B1 = 0.9
ADAM_B2 = 0.999
ADAM_EPS = 1e-08
ADAM_WD = 0.01
ADAM_STEP = 10
ADAM_C1 = 1.0 - ADAM_B1 ** ADAM_STEP
ADAM_C2 = 1.0 - ADAM_B2 ** ADAM_STEP

V7X_VMEM_BYTES = 64 * 2 ** 20
VMEM_LIMIT_CAP = V7X_VMEM_BYTES - 6 * 2 ** 20
LANES = 128
ROW_TILE = 256
ATTN_TILE = 512
STREAM_BLOCK_ELEMS = 512 * 1024
MM_TILE = 1024
MM_K_TILE = 2048
MM_K_BLOCK_MAX = 3072


def _padded_bytes(block, dtype):
    dims = [d for d in block if d is not None]
    if len(dims) >= 1:
        dims[-1] = -(-dims[-1] // LANES) * LANES
    if len(dims) >= 2:
        dims[-2] = -(-dims[-2] // 16) * 16
    return int(np.prod(dims)) * jnp.dtype(dtype).itemsize


def _pcall(body, *, name, grid, ins, outs, scratch=(), semantics=None, aliases=None, prefetch=None, deps=()):
    any_spec = pl.BlockSpec(memory_space=pl.ANY)
    if deps:
        n_lead = len(ins) + (1 if prefetch is not None else 0)
        n_deps = len(deps)
        inner = body

        def body(*refs):
            inner(*refs[:n_lead], *refs[n_lead + n_deps:])

        ins = list(ins) + [(d, None, None) for d in deps]
    in_specs = [any_spec if b is None else pl.BlockSpec(b, m) for _, b, m in ins]
    out_specs = [any_spec if b is None else pl.BlockSpec(b, m) for _, _, b, m in outs]
    out_shape = [pltpu.HBM(s, d) for s, d, _, _ in outs]
    est = 0
    for a, b, _ in ins:
        if b is not None:
            est += 2 * _padded_bytes(b, a.dtype)
    for _, d, b, _ in outs:
        if b is not None:
            est += 2 * _padded_bytes(b, d)
    for s in scratch:
        if hasattr(s, "shape") and hasattr(s, "dtype"):
            est += _padded_bytes(s.shape, s.dtype)
    limit = int(min(VMEM_LIMIT_CAP, est + 16 * 2 ** 20))
    params = pltpu.CompilerParams(
        dimension_semantics=semantics or ("arbitrary",) * len(grid), vmem_limit_bytes=limit)
    args = [pltpu.with_memory_space_constraint(a, pltpu.HBM) for a, _, _ in ins]
    if prefetch is not None:
        grid_spec = pltpu.PrefetchScalarGridSpec(
            num_scalar_prefetch=1, grid=grid, in_specs=in_specs, out_specs=out_specs, scratch_shapes=list(scratch))
        call = pl.pallas_call(body, out_shape=out_shape, grid_spec=grid_spec, name=name, compiler_params=params,
                              input_output_aliases=aliases or {})
        return call(prefetch, *args)
    call = pl.pallas_call(body, out_shape=out_shape, grid=grid, in_specs=in_specs, out_specs=out_specs,
                          scratch_shapes=list(scratch), name=name, compiler_params=params,
                          input_output_aliases=aliases or {})
    return call(*args)


def _tile(dim, pref, quantum=LANES):
    if dim <= pref:
        return dim
    t = (pref // quantum) * quantum
    while t >= quantum:
        if dim % t == 0:
            return t
        t -= quantum
    return dim


def _vshape(arr_shape):
    if len(arr_shape) == 2:
        return tuple(arr_shape)
    s, r, c = arr_shape
    return (r, s * c)


def _vblock(arr_shape, br, bc, rc):
    if len(arr_shape) == 2:
        return (br, bc), (lambda *g: rc(*g))
    _, _, c = arr_shape
    assert c % bc == 0, (arr_shape, bc)
    per = c // bc

    def imap(*g):
        ri, ci = rc(*g)
        return (ci // per, ri, ci % per)

    return (None, br, bc), imap


def _shard_width(*shapes):
    w = None
    for s in shapes:
        if len(s) == 3:
            w = s[2] if w is None else int(np.gcd(w, s[2]))
    return w


def mm(a, b, *, name, ta=False, tb=False, out=None, outs=None, epi=None, epi_ins=(), bm=None, bn=None, bk=None,
       deps=(), jobs=(), job_index=None):
    av, bv = _vshape(a.shape), _vshape(b.shape)
    M, K = (av[1], av[0]) if ta else av
    K2, N = (bv[1], bv[0]) if tb else bv
    assert K == K2, (a.shape, b.shape, ta, tb)
    if outs is None:
        outs = [(out[0], out[1], None)]
    a_sw = _shard_width(a.shape)
    b_sw = _shard_width(b.shape)
    o_sw = _shard_width(*[o[0] for o in outs])
    m_lim = a_sw if (ta and a_sw) else None
    k_lim = [w for w in ((a_sw if not ta else None), (b_sw if tb else None)) if w]
    n_lim = [w for w in ((b_sw if not tb else None), o_sw) if w]
    if bm is None:
        bm = _tile(M, min([MM_TILE] + ([m_lim] if m_lim else [])))
    if bn is None:
        bn = _tile(N, min([MM_TILE] + n_lim))
    k_shards = 0
    if tb and len(b.shape) == 3 and bk is None and not (a_sw and not ta):
        k_shards = 1
        while 2 * k_shards <= b.shape[0] and 2 * k_shards * b_sw <= MM_K_BLOCK_MAX:
            k_shards *= 2
        bk = k_shards * b_sw
    if bk is None:
        bk = K if (K <= 4096 and not k_lim) else _tile(K, min([MM_K_TILE] + k_lim))
    assert M % bm == 0 and N % bn == 0 and K % bk == 0, (name, M, N, K, bm, bn, bk)
    nk = K // bk
    grid = (M // bm, N // bn, nk)
    if ta:
        a_blk, a_map = _vblock(a.shape, bk, bm, lambda i, j, k: (k, i))
    else:
        a_blk, a_map = _vblock(a.shape, bm, bk, lambda i, j, k: (i, k))
    if k_shards:
        b_blk, b_map = (k_shards, bn, b_sw), (lambda i, j, k: (k, j, 0))
    elif tb:
        b_blk, b_map = _vblock(b.shape, bn, bk, lambda i, j, k: (j, k))
    else:
        b_blk, b_map = _vblock(b.shape, bk, bn, lambda i, j, k: (k, j))
    dn = (((0 if ta else 1,), (1 if tb else 0,)), ((), ()))
    ins = [(a, a_blk, a_map), (b, b_blk, b_map)] + list(epi_ins)
    out_list = []
    for shape, dtype, cols in outs:
        cols = cols or bn
        blk, imap = _vblock(shape, bm, cols, lambda i, j, k: (i, j))
        out_list.append((shape, dtype, blk, imap))
    n_e, n_o = len(epi_ins), len(out_list)

    n_steps = grid[0] * grid[1] * nk
    built = [job(n_steps) for job in jobs]
    aliases = {}
    job_slices = []
    if built:
        def lin(i, j, k):
            return (i * grid[1] + j) * nk + k

        ins = [(arr, blk, None if blk is None else (lambda i, j, k, s, f=f: f(i, j, k))) for arr, blk, f in ins]
        out_list = [(sh, dt, blk, (lambda i, j, k, s, f=f: f(i, j, k))) for sh, dt, blk, f in out_list]
        n_main_in, n_main_out = len(ins), len(out_list)
        for jb in built:
            i0, o0 = len(ins), len(out_list)
            ins += [(arr, blk, None if blk is None else (lambda i, j, k, s, f=f: f(lin(i, j, k), s)))
                    for arr, blk, f in jb["ins"]]
            out_list += [(sh, dt, blk, (lambda i, j, k, s, f=f: f(lin(i, j, k), s))) for sh, dt, blk, f in jb["outs"]]
            aliases.update({1 + i0 + ai: o0 + ao for ai, ao in jb["aliases"].items()})
            job_slices.append((i0, len(jb["ins"]), o0, len(jb["outs"])))
    n_in_total = len(ins)

    def body(*refs):
        if built:
            refs = refs[1:]
        a_ref, b_ref = refs[0], refs[1]
        e_refs = refs[2:2 + n_e]
        o_refs = refs[n_in_total:n_in_total + n_o]
        for jb, (i0, ni, o0, no) in zip(built, job_slices):
            jb["fn"](refs[i0:i0 + ni], refs[n_in_total + o0:n_in_total + o0 + no])

        def finish(acc):
            res = epi(acc, *e_refs) if epi is not None else (acc,)
            for o_ref, r in zip(o_refs, res):
                o_ref[...] = r.astype(o_ref.dtype)

        x = a_ref[...].astype(BF)
        y = b_ref[...].astype(BF)
        if k_shards:
            p = None
            for s in range(k_shards):
                part = lax.dot_general(x[:, s * b_sw:(s + 1) * b_sw], y[s], dn, preferred_element_type=F32)
                p = part if p is None else p + part
        else:
            p = lax.dot_general(x, y, dn, preferred_element_type=F32)
        if nk == 1:
            finish(p)
        else:
            acc_ref = refs[-1]
            k = pl.program_id(2)

            @pl.when(k == 0)
            def _():
                acc_ref[...] = p

            @pl.when(k > 0)
            def _():
                acc_ref[...] += p

            @pl.when(k == nk - 1)
            def _():
                finish(acc_ref[...])

    scratch = [pltpu.VMEM((bm, bn), F32)] if nk > 1 else []
    res = _pcall(body, name=name, grid=grid, ins=ins, outs=out_list, scratch=scratch, deps=deps,
                 semantics=("parallel", "parallel", "arbitrary"), prefetch=job_index if built else None, aliases=aliases)
    main = res[0] if n_o == 1 else res[:n_o]
    if not built:
        return main
    return main, [res[o0:o0 + no] for _, _, o0, no in job_slices]


_GELU_K = float(np.sqrt(2.0 / np.pi))
_GELU_C = 0.044715


def _gelu(x):
    t = jnp.tanh(_GELU_K * (x + _GELU_C * (x * x * x)))
    return 0.5 * x * (1.0 + t)


def _gelu_grad(x):
    t = jnp.tanh(_GELU_K * (x + _GELU_C * (x * x * x)))
    return 0.5 * (1.0 + t) + 0.5 * x * (1.0 - t * t) * (_GELU_K * (1.0 + 3.0 * _GELU_C * (x * x)))


def _rstd(x):
    return lax.rsqrt(jnp.mean(x * x, axis=-1, keepdims=True) + EPS)


def _rms_bwd(x, gain, dy):
    r = _rstd(x)
    xh = x * r
    gdy = dy * gain
    dx = r * (gdy - xh * jnp.mean(gdy * xh, axis=-1, keepdims=True))
    return dx, dy * xh


def _rope_fwd(x, cos_t, sin_t):
    return x * cos_t + pltpu.roll(x, 2 * HALF_ROPE, 1) * sin_t


def _rope_bwd(dy, cos_t, sin_t):
    return dy * cos_t + pltpu.roll(dy * sin_t, 2 * HALF_ROPE, 1)


def _acc_rows(ref, val, first):
    s = jnp.sum(val, axis=0, keepdims=True)

    @pl.when(first)
    def _():
        ref[...] = s

    @pl.when(jnp.logical_not(first))
    def _():
        ref[...] += s


def rms_fwd(x, gain, *, name, col_block=0, width=None, deps=()):
    T = x.shape[0]
    width = width or x.shape[1]
    tm = _tile(T, ROW_TILE, 8)

    def body(x_ref, g_ref, o_ref):
        v = x_ref[...]
        o_ref[...] = (v * _rstd(v) * g_ref[...]).astype(BF)

    return _pcall(body, name=name, grid=(T // tm,),
                  ins=[(x, (tm, width), lambda i: (i, col_block)), (gain, (1, width), lambda i: (0, 0))],
                  outs=[((T, width), BF, (tm, width), lambda i: (i, 0))], semantics=("parallel",), deps=deps)[0]


def rms_bwd(x, gain, dy, *, name, col_block=0, dres=None, want_f32=True, want_bf=True, into=None, deps=()):
    T, width = dy.shape
    tm = _tile(T, ROW_TILE, 8)
    has_res = dres is not None

    def body(*refs):
        x_ref, g_ref, dy_ref = refs[:3]
        pos = 3
        res_ref = None
        if has_res:
            res_ref = refs[pos]
            pos += 1
        if into is not None:
            pos += 1
        outs = refs[pos:]
        dx, dg_rows = _rms_bwd(x_ref[...], g_ref[...], dy_ref[...])
        if has_res:
            dx = dx + res_ref[...]
        o = 0
        if want_f32:
            outs[o][...] = dx
            o += 1
        if want_bf:
            outs[o][...] = dx.astype(BF)
            o += 1
        _acc_rows(outs[o], dg_rows, pl.program_id(0) == 0)

    ins = [(x, (tm, width), lambda i: (i, col_block)), (gain, (1, width), lambda i: (0, 0)),
           (dy, (tm, width), lambda i: (i, 0))]
    if has_res:
        ins.append((dres, (tm, width), lambda i: (i, 0)))
    outs = []
    aliases = {}
    if want_f32:
        outs.append(((T, width), F32, (tm, width), lambda i: (i, 0)))
    if want_bf and into is not None:
        ins.append((into, None, None))
        aliases[len(ins) - 1] = len(outs)
        outs.append((into.shape, BF, (tm, width), lambda i: (i, col_block)))
    elif want_bf:
        outs.append(((T, width), BF, (tm, width), lambda i: (i, 0)))
    outs.append(((1, width), F32, (1, width), lambda i: (0, 0)))
    return _pcall(body, name=name, grid=(T // tm,), ins=ins, outs=outs, aliases=aliases, deps=deps)


def mla_prep(proj, q_norm, kv_norm, cos_t, sin_t, *, name):
    T = proj.shape[0]
    tm = _tile(T, ROW_TILE, 8)
    kr_block = (proj.shape[1] - LANES) // LANES

    def body(cq_ref, ckv_ref, kr_ref, qg_ref, kg_ref, cos_ref, sin_ref, qn_ref, kvn_ref, krope_ref):
        cq = cq_ref[...]
        qn_ref[...] = (cq * _rstd(cq) * qg_ref[...]).astype(BF)
        ckv = ckv_ref[...]
        kvn_ref[...] = (ckv * _rstd(ckv) * kg_ref[...]).astype(BF)
        krope_ref[...] = _rope_fwd(kr_ref[...], cos_ref[...], sin_ref[...]).astype(BF)

    return _pcall(
        body, name=name, grid=(T // tm,),
        ins=[(proj, (tm, Q_LORA), lambda i: (i, 0)), (proj, (tm, KV_LORA), lambda i: (i, 1)),
             (proj, (tm, LANES), lambda i: (i, kr_block)),
             (q_norm, (1, Q_LORA), lambda i: (0, 0)), (kv_norm, (1, KV_LORA), lambda i: (0, 0)),
             (cos_t, (tm, LANES), lambda i: (i, 0)), (sin_t, (tm, LANES), lambda i: (i, 0))],
        outs=[((T, Q_LORA), BF, (tm, Q_LORA), lambda i: (i, 0)), ((T, KV_LORA), BF, (tm, KV_LORA), lambda i: (i, 0)),
              ((T, LANES), BF, (tm, LANES), lambda i: (i, 0))],
        semantics=("parallel",))


def _attn_scores(q, k_blk, diagonal):
    s = lax.dot_general(q, k_blk, (((1,), (1,)), ((), ())), preferred_element_type=F32) * ATTN_SCALE
    if diagonal:
        row = lax.broadcasted_iota(jnp.int32, s.shape, 0)
        col = lax.broadcasted_iota(jnp.int32, s.shape, 1)
        s = jnp.where(col <= row, s, -jnp.inf)
    return s


def attn_fwd(q, k, v, *, name):
    T = q.shape[0]
    tq = _tile(T, ATTN_TILE, 8)

    def body(q_ref, k_ref, v_ref, o_ref, lse_ref):
        i = pl.program_id(1)
        qv = q_ref[...]

        def block(kb, carry, diagonal):
            m, l, acc = carry
            start = pl.multiple_of(kb * tq, tq)
            s = _attn_scores(qv, k_ref[pl.ds(start, tq), :], diagonal)
            m_new = jnp.maximum(m, jnp.max(s, axis=-1, keepdims=True))
            alpha = jnp.exp(m - m_new)
            p = jnp.exp(s - m_new)
            l = alpha * l + jnp.sum(p, axis=-1, keepdims=True)
            acc = alpha * acc + jnp.dot(p.astype(BF), v_ref[pl.ds(start, tq), :], preferred_element_type=F32)
            return m_new, l, acc

        init = (jnp.full((tq, 1), -jnp.inf, F32), jnp.zeros((tq, 1), F32), jnp.zeros((tq, V_HEAD), F32))
        carry = lax.fori_loop(0, i, lambda kb, c: block(kb, c, False), init)
        m, l, acc = block(i, carry, True)
        o_ref[...] = acc / l
        lse_ref[...] = jnp.broadcast_to(m + jnp.log(l), (tq, V_HEAD))

    return _pcall(
        body, name=name, grid=(HEADS, T // tq),
        ins=[(q, (tq, HEAD_PAD), lambda h, i: (i, h)), (k, (T, HEAD_PAD), lambda h, i: (0, h)),
             (v, (T, V_HEAD), lambda h, i: (0, h))],
        outs=[((T, MLA_OUT), F32, (tq, V_HEAD), lambda h, i: (i, h)),
              ((T, MLA_OUT), F32, (tq, V_HEAD), lambda h, i: (i, h))], semantics=("parallel", "parallel"))


def attn_bwd(q, k, v, o, lse, do, *, name):
    T = q.shape[0]
    tq = _tile(T, ATTN_TILE, 8)

    def body(q_ref, k_ref, v_ref, o_ref, lse_ref, do_ref, dq_ref, dk_ref, dv_ref):
        i = pl.program_id(1)

        @pl.when(i == 0)
        def _():
            dk_ref[...] = jnp.zeros_like(dk_ref)
            dv_ref[...] = jnp.zeros_like(dv_ref)

        qv = q_ref[...]
        do_t = do_ref[...]
        lse_v = lse_ref[:, 0:1]
        delta = jnp.sum(do_t.astype(F32) * o_ref[...], axis=-1, keepdims=True)

        def block(kb, dq, diagonal):
            start = pl.multiple_of(kb * tq, tq)
            k_blk = k_ref[pl.ds(start, tq), :]
            v_blk = v_ref[pl.ds(start, tq), :]
            p = jnp.exp(_attn_scores(qv, k_blk, diagonal) - lse_v)
            dp = lax.dot_general(do_t, v_blk, (((1,), (1,)), ((), ())), preferred_element_type=F32)
            ds = (p * (dp - delta) * ATTN_SCALE).astype(BF)
            dk_ref[pl.ds(start, tq), :] += lax.dot_general(ds, qv, (((0,), (0,)), ((), ())), preferred_element_type=F32)
            dv_ref[pl.ds(start, tq), :] += lax.dot_general(p.astype(BF), do_t, (((0,), (0,)), ((), ())),
                                                          preferred_element_type=F32)
            return dq + jnp.dot(ds, k_blk, preferred_element_type=F32)

        dq = lax.fori_loop(0, i, lambda kb, c: block(kb, c, False), jnp.zeros((tq, HEAD_PAD), F32))
        dq_ref[...] = block(i, dq, True)

    return _pcall(
        body, name=name, grid=(HEADS, T // tq),
        ins=[(q, (tq, HEAD_PAD), lambda h, i: (i, h)), (k, (T, HEAD_PAD), lambda h, i: (0, h)),
             (v, (T, V_HEAD), lambda h, i: (0, h)), (o, (tq, V_HEAD), lambda h, i: (i, h)),
             (lse, (tq, V_HEAD), lambda h, i: (i, h)), (do, (tq, V_HEAD), lambda h, i: (i, h))],
        outs=[((T, HEADS * HEAD_PAD), F32, (tq, HEAD_PAD), lambda h, i: (i, h)),
              ((T, HEADS * HEAD_PAD), F32, (T, HEAD_PAD), lambda h, i: (0, h)),
              ((T, MLA_OUT), F32, (T, V_HEAD), lambda h, i: (0, h))],
        semantics=("parallel", "arbitrary"))


def mla_bwd_prep(dq, dk, dv, cos_t, sin_t, dproj, *, name):
    T = dq.shape[0]
    tm = _tile(T, ROW_TILE, 8)
    kr_block = (dproj.shape[1] - LANES) // LANES

    def body(dq_ref, dk_ref, dv_ref, cos_ref, sin_ref, dproj_in, dql_ref, dkvl_ref, dkr_ref):
        cos_v, sin_v = cos_ref[...], sin_ref[...]
        kr = jnp.zeros((tm, LANES), F32)
        for h in range(HEADS):
            lo = h * HEAD_PAD
            dql_ref[:, lo:lo + QK_NOPE] = dq_ref[:, lo:lo + QK_NOPE].astype(BF)
            dql_ref[:, lo + QK_NOPE:lo + HEAD_PAD] = _rope_bwd(
                dq_ref[:, lo + QK_NOPE:lo + HEAD_PAD], cos_v, sin_v).astype(BF)
            dkvl_ref[:, lo:lo + QK_NOPE] = dk_ref[:, lo:lo + QK_NOPE].astype(BF)
            dkvl_ref[:, lo + QK_NOPE:lo + HEAD_PAD] = dv_ref[:, h * V_HEAD:(h + 1) * V_HEAD].astype(BF)
            kr = kr + dk_ref[:, lo + QK_NOPE:lo + HEAD_PAD]
        dkr_ref[...] = _rope_bwd(kr, cos_v, sin_v).astype(BF)

    W = HEADS * HEAD_PAD
    return _pcall(
        body, name=name, grid=(T // tm,),
        ins=[(dq, (tm, W), lambda i: (i, 0)), (dk, (tm, W), lambda i: (i, 0)), (dv, (tm, MLA_OUT), lambda i: (i, 0)),
             (cos_t, (tm, LANES), lambda i: (i, 0)), (sin_t, (tm, LANES), lambda i: (i, 0)), (dproj, None, None)],
        outs=[((T, W), BF, (tm, W), lambda i: (i, 0)), ((T, W), BF, (tm, W), lambda i: (i, 0)),
              (dproj.shape, BF, (tm, LANES), lambda i: (i, kr_block))],
        aliases={5: 2}, semantics=("parallel",))


def _group_norm_stats(vg):
    mu = jnp.mean(vg, axis=-1, keepdims=True)
    d = vg - mu
    r = lax.rsqrt(jnp.mean(d * d, axis=-1, keepdims=True) + EPS)
    return d * r, r


def mix_fwd(a, proj, g_mla, g_sgu, v_gain, w_tril, b_full, *, name):
    T = a.shape[0]
    tm = _tile(T, ROW_TILE, CHUNK)
    n_chunk = tm // CHUNK

    def body(a_ref, u_ref, v_ref, gm_ref, gs_ref, vg_ref, w_ref, b_ref, o_ref, s_scr):
        av = a_ref[...]
        o_ref[:, :MLA_OUT] = (av * _rstd(av) * gm_ref[...]).astype(BF)
        for g in range(GROUPS):
            sl = slice(g * CH, (g + 1) * CH)
            vhat, _ = _group_norm_stats(_gelu(v_ref[:, sl]))
            vn = (vhat * vg_ref[:, sl]).astype(BF)
            u = _gelu(u_ref[:, sl])
            for ci in range(n_chunk):
                rs = slice(ci * CHUNK, (ci + 1) * CHUNK)
                y = jnp.dot(w_ref[g], vn[rs], preferred_element_type=F32) + b_ref[:, sl]
                s_scr[rs, sl] = u[rs] * y
        s = s_scr[...]
        o_ref[:, MLA_OUT:] = (s * _rstd(s) * gs_ref[...]).astype(BF)

    return _pcall(
        body, name=name, grid=(T // tm,),
        ins=[(a, (tm, MLA_OUT), lambda i: (i, 0)), (proj, (tm, SGU_OUT), lambda i: (i, 1)),
             (proj, (tm, SGU_OUT), lambda i: (i, 2)), (g_mla, (1, MLA_OUT), lambda i: (0, 0)),
             (g_sgu, (1, SGU_OUT), lambda i: (0, 0)), (v_gain, (1, SGU_OUT), lambda i: (0, 0)),
             (w_tril, (GROUPS, CHUNK, CHUNK), lambda i: (0, 0, 0)), (b_full, (CHUNK, SGU_OUT), lambda i: (0, 0))],
        outs=[((T, MLA_OUT + SGU_OUT), BF, (tm, MLA_OUT + SGU_OUT), lambda i: (i, 0))],
        scratch=[pltpu.VMEM((tm, SGU_OUT), F32)], semantics=("parallel",))[0]


def mix_bwd(dmixed, a, proj, g_mla, g_sgu, v_gain, w_tril, w_tril_t, b_full, *, name, deps=()):
    T = a.shape[0]
    tm = _tile(T, ROW_TILE, CHUNK)
    n_chunk = tm // CHUNK
    uv0 = Q_LORA + KV_LORA

    def body(dm_a_ref, dm_s_ref, a_ref, u_ref, v_ref, gm_ref, gs_ref, vg_ref, w_ref, wt_ref, b_ref,
             da_ref, duv_ref, dgm_ref, dgs_ref, dvg_ref, dw_ref, db_ref, s_scr, y_scr):
        first = pl.program_id(0) == 0
        duv_ref[:, :uv0] = jnp.zeros((tm, uv0), BF)
        duv_ref[:, uv0 + 2 * SGU_OUT:] = jnp.zeros((tm, duv_ref.shape[1] - uv0 - 2 * SGU_OUT), BF)
        da, dgm_rows = _rms_bwd(a_ref[...], gm_ref[...], dm_a_ref[...])
        da_ref[...] = da.astype(BF)
        _acc_rows(dgm_ref, dgm_rows, first)

        for g in range(GROUPS):
            sl = slice(g * CH, (g + 1) * CH)
            vhat, _ = _group_norm_stats(_gelu(v_ref[:, sl]))
            vn = (vhat * vg_ref[:, sl]).astype(BF)
            u = _gelu(u_ref[:, sl])
            for ci in range(n_chunk):
                rs = slice(ci * CHUNK, (ci + 1) * CHUNK)
                y = jnp.dot(w_ref[g], vn[rs], preferred_element_type=F32) + b_ref[:, sl]
                y_scr[rs, sl] = y
                s_scr[rs, sl] = u[rs] * y
        ds, dgs_rows = _rms_bwd(s_scr[...], gs_ref[...], dm_s_ref[...])
        _acc_rows(dgs_ref, dgs_rows, first)
        s_scr[...] = ds

        @pl.when(first)
        def _():
            dw_ref[...] = jnp.zeros_like(dw_ref)
            db_ref[...] = jnp.zeros_like(db_ref)

        for g in range(GROUPS):
            sl = slice(g * CH, (g + 1) * CH)
            upre = u_ref[:, sl]
            vpre = v_ref[:, sl]
            u = _gelu(upre)
            vhat, r = _group_norm_stats(_gelu(vpre))
            gain = vg_ref[:, sl]
            vn = (vhat * gain).astype(BF)
            dsg = s_scr[:, sl]
            duv_ref[:, uv0 + g * CH:uv0 + (g + 1) * CH] = (dsg * y_scr[:, sl] * _gelu_grad(upre)).astype(BF)
            dy = dsg * u
            dyb = dy.astype(BF)
            dvn_parts = []
            for ci in range(n_chunk):
                rs = slice(ci * CHUNK, (ci + 1) * CHUNK)
                dvn_parts.append(jnp.dot(wt_ref[g], dyb[rs], preferred_element_type=F32))
                dw_ref[g] += lax.dot_general(dyb[rs], vn[rs], (((1,), (1,)), ((), ())), preferred_element_type=F32)
                db_ref[:, sl] += jnp.broadcast_to(jnp.sum(dy[rs], axis=-1, keepdims=True), (CHUNK, CH))
            dvn = dvn_parts[0] if n_chunk == 1 else jnp.concatenate(dvn_parts, axis=0)
            _acc_rows(dvg_ref.at[:, sl], dvn * vhat, first)
            dvh = dvn * gain
            dvg = r * (dvh - jnp.mean(dvh, axis=-1, keepdims=True)
                       - vhat * jnp.mean(dvh * vhat, axis=-1, keepdims=True))
            duv_ref[:, uv0 + SGU_OUT + g * CH:uv0 + SGU_OUT + (g + 1) * CH] = (dvg * _gelu_grad(vpre)).astype(BF)

    return _pcall(
        body, name=name, grid=(T // tm,),
        ins=[(dmixed, (tm, MLA_OUT), lambda i: (i, 0)), (dmixed, (tm, SGU_OUT), lambda i: (i, 1)),
             (a, (tm, MLA_OUT), lambda i: (i, 0)), (proj, (tm, SGU_OUT), lambda i: (i, 1)),
             (proj, (tm, SGU_OUT), lambda i: (i, 2)), (g_mla, (1, MLA_OUT), lambda i: (0, 0)),
             (g_sgu, (1, SGU_OUT), lambda i: (0, 0)), (v_gain, (1, SGU_OUT), lambda i: (0, 0)),
             (w_tril, (GROUPS, CHUNK, CHUNK), lambda i: (0, 0, 0)), (w_tril_t, (GROUPS, CHUNK, CHUNK), lambda i: (0, 0, 0)),
             (b_full, (CHUNK, SGU_OUT), lambda i: (0, 0))],
        outs=[((T, MLA_OUT), BF, (tm, MLA_OUT), lambda i: (i, 0)),
              ((T, proj.shape[1]), BF, (tm, proj.shape[1]), lambda i: (i, 0)),
              ((1, MLA_OUT), F32, (1, MLA_OUT), lambda i: (0, 0)), ((1, SGU_OUT), F32, (1, SGU_OUT), lambda i: (0, 0)),
              ((1, SGU_OUT), F32, (1, SGU_OUT), lambda i: (0, 0)),
              ((GROUPS, CHUNK, CHUNK), F32, (GROUPS, CHUNK, CHUNK), lambda i: (0, 0, 0)),
              ((CHUNK, SGU_OUT), F32, (CHUNK, SGU_OUT), lambda i: (0, 0))],
        scratch=[pltpu.VMEM((tm, SGU_OUT), F32), pltpu.VMEM((tm, SGU_OUT), F32)], deps=deps)


def _shift_down(z, n, row):
    return jnp.where(row >= n, pltpu.roll(z, n, 0), 0.0)


def _shift_up(z, n, row, T):
    return jnp.where(row < T - n, pltpu.roll(z, T - n, 0), 0.0)


def conv_fwd(proj, conv_w, *, name):
    T, D3 = proj.shape
    D = D3 // 3
    tn = _tile(D, 256)
    nj = D // tn

    def body(b_ref, c_ref, x_ref, w_ref, o_ref):
        row = lax.broadcasted_iota(jnp.int32, (T, tn), 0)
        z = c_ref[...] * x_ref[...]
        zc = w_ref[2:3, :] * z + w_ref[1:2, :] * _shift_down(z, 1, row) + w_ref[0:1, :] * _shift_down(z, 2, row)
        o_ref[...] = (b_ref[...] * zc).astype(BF)

    return _pcall(
        body, name=name, grid=(nj,),
        ins=[(proj, (T, tn), lambda j: (0, j)), (proj, (T, tn), lambda j: (0, nj + j)),
             (proj, (T, tn), lambda j: (0, 2 * nj + j)), (conv_w, (3, tn), lambda j: (0, j))],
        outs=[((T, D), BF, (T, tn), lambda j: (0, j))], semantics=("parallel",))[0]


def conv_bwd(dg, proj, conv_w, *, name, deps=()):
    T, D3 = proj.shape
    D = D3 // 3
    tn = _tile(D, 256)
    nj = D // tn

    def body(dg_ref, b_ref, c_ref, x_ref, w_ref, dp_ref, dw_ref, dc_scr, dx_scr):
        part = pl.program_id(1)

        @pl.when(part == 0)
        def _():
            row = lax.broadcasted_iota(jnp.int32, (T, tn), 0)
            c, x = c_ref[...], x_ref[...]
            z = c * x
            z1 = _shift_down(z, 1, row)
            z2 = _shift_down(z, 2, row)
            dgv = dg_ref[...]
            zc = w_ref[2:3, :] * z + w_ref[1:2, :] * z1 + w_ref[0:1, :] * z2
            dp_ref[...] = (dgv * zc).astype(BF)
            dzc = dgv * b_ref[...]
            dw_ref[0:1, :] = jnp.sum(dzc * z2, axis=0, keepdims=True)
            dw_ref[1:2, :] = jnp.sum(dzc * z1, axis=0, keepdims=True)
            dw_ref[2:3, :] = jnp.sum(dzc * z, axis=0, keepdims=True)
            dz = (w_ref[2:3, :] * dzc + w_ref[1:2, :] * _shift_up(dzc, 1, row, T)
                  + w_ref[0:1, :] * _shift_up(dzc, 2, row, T))
            dc_scr[...] = (dz * x).astype(BF)
            dx_scr[...] = (dz * c).astype(BF)

        @pl.when(part == 1)
        def _():
            dp_ref[...] = dc_scr[...]

        @pl.when(part == 2)
        def _():
            dp_ref[...] = dx_scr[...]

    return _pcall(
        body, name=name, grid=(nj, 3),
        ins=[(dg, (T, tn), lambda j, p: (0, j)), (proj, (T, tn), lambda j, p: (0, j)),
             (proj, (T, tn), lambda j, p: (0, nj + j)), (proj, (T, tn), lambda j, p: (0, 2 * nj + j)),
             (conv_w, (3, tn), lambda j, p: (0, j))],
        outs=[((T, D3), BF, (T, tn), lambda j, p: (0, p * nj + j)), ((3, D), F32, (3, tn), lambda j, p: (0, j))],
        scratch=[pltpu.VMEM((T, tn), BF), pltpu.VMEM((T, tn), BF)], semantics=("parallel", "arbitrary"), deps=deps)


def loss_bwd(x_parts, gain, target, *, name):
    T, D = target.shape
    tm = _tile(T, ROW_TILE, 8)
    n_x = len(x_parts)

    def body(*refs):
        x_refs = refs[:n_x]
        g_ref, t_ref, dx_ref, dxb_ref, dg_ref, loss_ref = refs[n_x:]
        first = pl.program_id(0) == 0
        xv = jnp.concatenate([r[...] for r in x_refs], axis=-1) if n_x > 1 else x_refs[0][...]
        r = _rstd(xv)
        xh = xv * r
        gain_v = g_ref[...]
        err = xh * gain_v - t_ref[...]
        part = 0.5 * jnp.sum(jnp.mean(err * err, axis=-1, keepdims=True), axis=0, keepdims=True)
        _acc_rows(loss_ref, jnp.broadcast_to(part, (1, LANES)), first)
        dy = err * (1.0 / D)
        gdy = dy * gain_v
        dx = r * (gdy - xh * jnp.mean(gdy * xh, axis=-1, keepdims=True))
        dx_ref[...] = dx
        dxb_ref[...] = dx.astype(BF)
        _acc_rows(dg_ref, dy * xh, first)

    return _pcall(
        body, name=name, grid=(T // tm,),
        ins=[(p, (tm, D // n_x), lambda i: (i, 0)) for p in x_parts]
        + [(gain, (1, D), lambda i: (0, 0)), (target, (tm, D), lambda i: (i, 0))],
        outs=[((T, D), F32, (tm, D), lambda i: (i, 0)), ((T, D), BF, (tm, D), lambda i: (i, 0)),
              ((1, D), F32, (1, D), lambda i: (0, 0)), ((1, LANES), F32, (1, LANES), lambda i: (0, 0))])


def _adamw(g, w, m, v):
    m = ADAM_B1 * m + (1.0 - ADAM_B1) * g
    v = ADAM_B2 * v + (1.0 - ADAM_B2) * (g * g)
    m_hat = m / ADAM_C1
    v_hat = v / ADAM_C2
    delta = -ADAM_LR * (m_hat / (jnp.sqrt(v_hat) + ADAM_EPS) + ADAM_WD * w)
    return delta, m, v


def adam_flat(g, w, m, v, *, name):
    def body(g_ref, w_ref, m_ref, v_ref, d_ref, nm_ref, nv_ref):
        d, nm, nv = _adamw(g_ref[...], w_ref[...], m_ref[...], v_ref[...])
        d_ref[...] = d
        nm_ref[...] = nm
        nv_ref[...] = nv

    blk = g.shape
    zero = lambda: (0, 0)
    return _pcall(body, name=name, grid=(),
                  ins=[(t, blk, zero) for t in (g, w, m, v)],
                  outs=[(blk, F32, blk, zero)] * 3)


def _chip_slots():
    x, y, c = lax.axis_index("x"), lax.axis_index("y"), lax.axis_index("c")
    chips = [(1 - x, y), (x, 1 - y), (1 - x, 1 - y)]
    return x, y, c, chips


def device_index():
    x, y, c, chips = _chip_slots()
    return jnp.stack([4 * x + 2 * y + c, 2 * x + y] + [4 * cx + 2 * cy + c for cx, cy in chips]
                     + [2 * cx + cy for cx, cy in chips]).astype(jnp.int32)


def _job_rows(R, C, n_steps):
    if n_steps is None:
        n_steps = max(1, R * C // STREAM_BLOCK_ELEMS)
    n_blk = max([d for d in range(1, n_steps + 1) if R % d == 0 and (R // d) % 16 == 0] or [1])
    return R // n_blk, n_blk


def run_job(job, *, index, name, deps=()):
    jb = job(None)
    n_in = len(jb["ins"])

    def body(idx_ref, *refs):
        jb["fn"](refs[:n_in], refs[n_in:n_in + len(jb["outs"])])

    return _pcall(body, name=name, grid=(jb["n_blk"],), ins=jb["ins"], outs=jb["outs"], prefetch=index,
                  aliases={1 + a: o for a, o in jb["aliases"].items()}, semantics=("parallel",), deps=deps)


def adam_job(gs, a_buf, b_buf, w, m, v, layer, prev):
    L, R, C = w.shape

    def build(n_steps):
        tr, n_blk = _job_rows(R, C, n_steps)
        blk = (None, tr, C)
        row = lambda t: jnp.minimum(t, n_blk - 1)
        ins = [(gs, blk, lambda t, s: (s[0], row(t), 0)), (a_buf, blk, lambda t, s: (s[1], row(t), 0))]
        ins += [(b_buf, blk, lambda t, s, j=j: (j, row(t), 0)) for j in range(3)]
        ins += [(p, blk, lambda t, s: (layer, row(t), 0)) for p in (w, m, v)]
        ins += [(p, None, None) for p in (prev or [])]

        def fn(i, o):
            g = ((((i[0][...].astype(F32) + i[1][...].astype(F32)) + i[2][...].astype(F32))
                  + i[3][...].astype(F32)) + i[4][...].astype(F32))
            d, nm, nv = _adamw(g, i[5][...], i[6][...], i[7][...])
            o[0][...] = g
            o[1][...] = d
            o[2][...] = nm
            o[3][...] = nv

        return dict(ins=ins, outs=[((L, R, C), F32, blk, lambda t, s: (layer, row(t), 0))] * 4, fn=fn,
                    aliases={8 + o: o for o in range(4)} if prev else {}, n_blk=n_blk)

    return build


def pair_job(gs, a_buf):
    _, R, C = gs.shape

    def build(n_steps):
        tr, n_blk = _job_rows(R, C, n_steps)
        blk = (None, tr, C)
        row = lambda t: jnp.minimum(t, n_blk - 1)
        ins = [(gs, blk, lambda t, s, j=j: (s[2 + j], row(t), 0)) for j in range(3)]
        ins += [(a_buf, blk, lambda t, s, j=j: (s[5 + j], row(t), 0)) for j in range(3)]

        def fn(i, o):
            for j in range(3):
                o[0][j] = (i[j][...].astype(F32) + i[3 + j][...].astype(F32)).astype(BF)

        return dict(ins=ins, outs=[((3, R, C), BF, (3, tr, C), lambda t, s: (0, row(t), 0))], fn=fn, aliases={},
                    n_blk=n_blk)

    return build


def reduce_sum(gs, a_buf, b_buf, *, name):
    _, R, C = gs.shape
    tr = _tile(R, 256, 16)
    x, y, c, _ = _chip_slots()
    idx = jnp.stack([4 * x + 2 * y + c, 2 * x + y]).astype(jnp.int32)

    def body(idx_ref, g_ref, a_ref, b0_ref, b1_ref, b2_ref, o_ref):
        o_ref[...] = ((((g_ref[...].astype(F32) + a_ref[...].astype(F32)) + b0_ref[...].astype(F32))
                       + b1_ref[...].astype(F32)) + b2_ref[...].astype(F32))

    blk3 = (None, tr, C)
    return _pcall(body, name=name, grid=(R // tr,),
                  ins=[(gs, blk3, lambda i, s: (s[0], i, 0)), (a_buf, blk3, lambda i, s: (s[1], i, 0)),
                       (b_buf, blk3, lambda i, s: (0, i, 0)), (b_buf, blk3, lambda i, s: (1, i, 0)),
                       (b_buf, blk3, lambda i, s: (2, i, 0))],
                  outs=[((R, C), F32, (tr, C), lambda i, s: (i, 0))], prefetch=idx, semantics=("parallel",))[0]


def adam_rows(g, w, m, v, *, name):
    R, C = g.shape
    tr = _tile(R, 256, 8)

    def body(g_ref, w_ref, m_ref, v_ref, d_ref, nm_ref, nv_ref):
        d, nm, nv = _adamw(g_ref[...], w_ref[...], m_ref[...], v_ref[...])
        d_ref[...] = d
        nm_ref[...] = nm
        nv_ref[...] = nv

    spec = ((tr, C), lambda i: (i, 0))
    return _pcall(body, name=name, grid=(R // tr,), ins=[(t, *spec) for t in (g, w, m, v)],
                  outs=[((R, C), F32, *spec)] * 3, semantics=("parallel",))


def sum_rows8(gathered, rows, *, name):
    W = gathered.shape[1]

    def body(g_ref, o_ref):
        acc = g_ref[0:rows, :]
        for d in range(1, N_DEV):
            acc = acc + g_ref[d * rows:(d + 1) * rows, :]
        o_ref[...] = acc

    return _pcall(body, name=name, grid=(), ins=[(gathered, gathered.shape, lambda: (0, 0))],
                  outs=[((rows, W), F32, (rows, W), lambda: (0, 0))])[0]


HBM_SPEC = pl.BlockSpec(memory_space=pltpu.HBM)
SEM_SPEC = pl.BlockSpec(memory_space=pltpu.SEMAPHORE)
ANY_SPEC = pl.BlockSpec(memory_space=pl.ANY)
DATAFLOW = pltpu.SideEffectType.DATAFLOW_SIDE_EFFECTING


def _in_hbm(v):
    return pltpu.with_memory_space_constraint(v, pltpu.HBM)


def _slot(p):
    return 4 * p[0] + 2 * p[1] + p[2]


def _gather_peers():
    x, y, c, chips = _chip_slots()
    return (x, y, c), [(x, y, 1 - c)] + [(*chip, c) for chip in chips]


def gather_start(groups, after, *, name):
    flat = [s for g in groups for s in g]
    n, n_g = len(flat), len(groups)
    where = [(gi, ti) for gi, g in enumerate(groups) for ti in range(len(g))]

    def body(*refs):
        src, land = refs[:n], refs[n:2 * n]
        sems = refs[2 * n + 1:2 * n + 1 + 2 * n_g]
        me, peers = _gather_peers()
        for t in range(n):
            gi, ti = where[t]
            for k, to in enumerate(peers):
                pltpu.make_async_remote_copy(
                    src_ref=src[t], dst_ref=land[t].at[_slot(me)], send_sem=sems[2 * gi].at[4 * ti + k],
                    recv_sem=sems[2 * gi + 1].at[4 * ti + k], device_id=to, device_id_type=MESH).start()
        refs[-1][...] = jnp.zeros_like(refs[-1])

    out_shape = []
    for g in groups:
        out_shape += [pltpu.SemaphoreType.DMA((4 * len(g),)), pltpu.SemaphoreType.DMA((4 * len(g),))]
    out_shape += [pltpu.HBM(s.shape, s.dtype) for s in flat]
    out_shape += [pltpu.HBM((N_DEV,) + s.shape, s.dtype) for s in flat]
    out_shape += [jax.ShapeDtypeStruct((8, LANES), F32)]
    aliases = {t: 2 * n_g + t for t in range(n)}
    aliases.update({n + t: 2 * n_g + n + t for t in range(n)})
    res = pl.pallas_call(
        body, name=name, out_shape=out_shape, in_specs=[HBM_SPEC] * (2 * n) + [ANY_SPEC],
        out_specs=[SEM_SPEC] * (2 * n_g) + [HBM_SPEC] * (2 * n) + [pl.BlockSpec(memory_space=pltpu.VMEM)],
        input_output_aliases=aliases, compiler_params=pltpu.CompilerParams(has_side_effects=DATAFLOW),
    )(*[_in_hbm(s) for s in flat], *[_in_hbm(lax.empty((N_DEV,) + s.shape, s.dtype)) for s in flat], after)
    out, off = [], 0
    for gi, g in enumerate(groups):
        k = len(g)
        out.append((res[2 * gi], res[2 * gi + 1], res[2 * n_g + off:2 * n_g + off + k],
                    res[2 * n_g + n + off:2 * n_g + n + off + k]))
        off += k
    return out, res[-1]


def gather_wait(started, after, *, name):
    send_sems, recv_sems, srcs, lands = started
    n = len(srcs)
    after = list(after)

    def body(*refs):
        src, land = refs[:n], refs[n:2 * n]
        send, recv = refs[2 * n], refs[2 * n + 1]
        _, peers = _gather_peers()
        for t in range(n):
            for k, frm in enumerate(peers):
                cp = pltpu.make_async_remote_copy(
                    src_ref=src[t], dst_ref=land[t].at[_slot(frm)], send_sem=send.at[4 * t + k],
                    recv_sem=recv.at[4 * t + k],
                    device_id=frm, device_id_type=MESH)
                cp.wait_send()
                cp.wait_recv()

    res = pl.pallas_call(
        body, name=name,
        out_shape=[pltpu.HBM(s.shape, s.dtype) for s in srcs] + [pltpu.HBM(l.shape, l.dtype) for l in lands],
        in_specs=[HBM_SPEC] * (2 * n) + [SEM_SPEC, SEM_SPEC] + [ANY_SPEC] * len(after),
        out_specs=[HBM_SPEC] * (2 * n), input_output_aliases={t: t for t in range(2 * n)},
        compiler_params=pltpu.CompilerParams(has_side_effects=DATAFLOW),
    )(*srcs, *lands, send_sems, recv_sems, *after)
    return res[:n], res[n:]


def place_own(src, land, *, name):
    R, C = src.shape
    tr = _tile(R, 512, 16)
    x, y, c, _ = _chip_slots()
    idx = jnp.stack([4 * x + 2 * y + c]).astype(jnp.int32)

    def body(idx_ref, s_ref, land_ref, o_ref):
        o_ref[...] = s_ref[...]

    return _pcall(body, name=name, grid=(R // tr,),
                  ins=[(src, (tr, C), lambda i, s: (i, 0)), (land, None, None)],
                  outs=[(land.shape, land.dtype, (None, tr, C), lambda i, s: (s[0], i, 0))],
                  prefetch=idx, aliases={2: 0}, semantics=("parallel",))[0]


def gather_finish(srcs, lands, *, name):
    n = len(srcs)

    def body(*refs):
        land = refs[n:2 * n]
        send_sems, recv_sems = refs[2 * n:]
        x, y, c, chips = _chip_slots()
        me, sibling = (x, y, c), (x, y, 1 - c)

        def copy(t, j, block, to):
            return pltpu.make_async_remote_copy(
                src_ref=land[t].at[_slot(block)], dst_ref=land[t].at[_slot(block)], send_sem=send_sems.at[t, j],
                recv_sem=recv_sems.at[t, j], device_id=to, device_id_type=MESH)

        sends = [copy(t, j, (*chip, c), sibling) for t in range(n) for j, chip in enumerate(chips)]
        for cp in sends:
            cp.start()
        for t in range(n):
            for j, chip in enumerate(chips):
                copy(t, j, (*chip, 1 - c), me).wait_recv()
        for cp in sends:
            cp.wait_send()

    passed = pl.pallas_call(
        body, name=name, out_shape=[jax.ShapeDtypeStruct(l.shape, l.dtype) for l in lands],
        in_specs=[ANY_SPEC] * n, out_specs=[ANY_SPEC] * n,
        input_output_aliases={t: t for t in range(n)},
        scratch_shapes=[pltpu.SemaphoreType.DMA((n, 3)), pltpu.SemaphoreType.DMA((n, 3))],
    )(*lands)
    return [place_own(s, l, name=f"{name}_own{t}") for t, (s, l) in enumerate(zip(srcs, passed))]


def chips_start(pairs, *, name):
    n = len(pairs)

    def body(*refs):
        src, land = refs[:n], refs[n:2 * n]
        send, recv = refs[2 * n], refs[2 * n + 1]
        token = refs[-1]
        x, y, c, chips = _chip_slots()
        for t in range(n):
            for j, chip in enumerate(chips):
                pltpu.make_async_remote_copy(
                    src_ref=src[t].at[j], dst_ref=land[t].at[j], send_sem=send.at[3 * t + j],
                    recv_sem=recv.at[3 * t + j], device_id=(*chip, c), device_id_type=MESH).start()
        token[...] = jnp.zeros_like(token)

    res = pl.pallas_call(
        body, name=name,
        out_shape=[pltpu.SemaphoreType.DMA((3 * n,)), pltpu.SemaphoreType.DMA((3 * n,))]
        + [pltpu.HBM(p.shape, p.dtype) for p in pairs] * 2 + [jax.ShapeDtypeStruct((8, LANES), F32)],
        in_specs=[HBM_SPEC] * (2 * n),
        out_specs=[SEM_SPEC, SEM_SPEC] + [HBM_SPEC] * (2 * n) + [pl.BlockSpec(memory_space=pltpu.VMEM)],
        input_output_aliases={t: 2 + t for t in range(2 * n)},
        compiler_params=pltpu.CompilerParams(has_side_effects=DATAFLOW),
    )(*[_in_hbm(p) for p in pairs], *[_in_hbm(lax.empty(p.shape, p.dtype)) for p in pairs])
    return res[0], res[1], res[2:2 + n], res[2 + n:2 + 2 * n], res[-1]


def chips_wait(started, after, *, name):
    send_sems, recv_sems, srcs, lands, _ = started
    n = len(srcs)

    def body(*refs):
        src, land = refs[:n], refs[n:2 * n]
        send, recv = refs[2 * n], refs[2 * n + 1]
        x, y, c, chips = _chip_slots()
        for t in range(n):
            for j, chip in enumerate(chips):
                cp = pltpu.make_async_remote_copy(
                    src_ref=src[t].at[j], dst_ref=land[t].at[j], send_sem=send.at[3 * t + j],
                    recv_sem=recv.at[3 * t + j], device_id=(*chip, c), device_id_type=MESH)
                cp.wait_send()
                cp.wait_recv()

    res = pl.pallas_call(
        body, name=name, out_shape=[pltpu.HBM(s.shape, s.dtype) for s in srcs] * 2,
        in_specs=[HBM_SPEC] * (2 * n) + [SEM_SPEC, SEM_SPEC, ANY_SPEC], out_specs=[HBM_SPEC] * (2 * n),
        input_output_aliases={t: t for t in range(2 * n)},
        compiler_params=pltpu.CompilerParams(has_side_effects=DATAFLOW),
    )(*srcs, *lands, send_sems, recv_sems, after)
    return res[n:]


def _sibling_copies(src, land, send, recv, n):
    x, y, c, _ = _chip_slots()
    return [pltpu.make_async_remote_copy(
        src_ref=src[t].at[4 * (q // 2) + 2 * (q % 2) + (1 - c)], dst_ref=land[t].at[q], send_sem=send.at[4 * t + q],
        recv_sem=recv.at[4 * t + q], device_id=(x, y, 1 - c), device_id_type=MESH)
        for t in range(n) for q in range(4)]


def sibling_start(gs, *, name):
    n = len(gs)

    def body(*refs):
        for cp in _sibling_copies(refs[:n], refs[n:2 * n], refs[2 * n], refs[2 * n + 1], n):
            cp.start()
        refs[-1][...] = jnp.zeros_like(refs[-1])

    lands = [lax.empty((4,) + g.shape[1:], g.dtype) for g in gs]
    res = pl.pallas_call(
        body, name=name,
        out_shape=[pltpu.SemaphoreType.DMA((4 * n,)), pltpu.SemaphoreType.DMA((4 * n,))]
        + [pltpu.HBM(g.shape, g.dtype) for g in gs] + [pltpu.HBM(l.shape, l.dtype) for l in lands]
        + [jax.ShapeDtypeStruct((8, LANES), F32)],
        in_specs=[HBM_SPEC] * (2 * n),
        out_specs=[SEM_SPEC, SEM_SPEC] + [HBM_SPEC] * (2 * n) + [pl.BlockSpec(memory_space=pltpu.VMEM)],
        input_output_aliases={t: 2 + t for t in range(2 * n)},
        compiler_params=pltpu.CompilerParams(has_side_effects=DATAFLOW),
    )(*[_in_hbm(g) for g in gs], *[_in_hbm(l) for l in lands])
    return res[0], res[1], res[2:2 + n], res[2 + n:2 + 2 * n], res[-1]


def sibling_wait(started, after, *, name):
    send_sems, recv_sems, srcs, lands, _ = started
    n = len(srcs)

    def body(*refs):
        for cp in _sibling_copies(refs[:n], refs[n:2 * n], refs[2 * n], refs[2 * n + 1], n):
            cp.wait_send()
            cp.wait_recv()

    res = pl.pallas_call(
        body, name=name,
        out_shape=[pltpu.HBM(s.shape, s.dtype) for s in srcs] + [pltpu.HBM(l.shape, l.dtype) for l in lands],
        in_specs=[HBM_SPEC] * (2 * n) + [SEM_SPEC, SEM_SPEC, ANY_SPEC], out_specs=[HBM_SPEC] * (2 * n),
        input_output_aliases={t: t for t in range(2 * n)},
        compiler_params=pltpu.CompilerParams(has_side_effects=DATAFLOW),
    )(*srcs, *lands, send_sems, recv_sems, after)
    return res[:n], res[n:]


def _rope_slab(cols):
    z = jnp.zeros(cols.shape[:-1] + (HALF_ROPE,), cols.dtype)
    return jnp.concatenate([cols[..., :HALF_ROPE], z, cols[..., HALF_ROPE:], z], axis=-1)


def _rope_unslab(slab):
    return jnp.concatenate([slab[..., :HALF_ROPE], slab[..., 2 * HALF_ROPE:3 * HALF_ROPE]], axis=-1)


def _pack_w_in_t(wt_g):
    s, c, d = wt_g.shape
    w = wt_g.reshape(s * c, d)
    c2, c3 = Q_LORA + KV_LORA, Q_LORA + KV_LORA + QK_ROPE
    r = w[c2:c3]
    z = jnp.zeros((HALF_ROPE, d), w.dtype)
    return jnp.concatenate([w[:c2], w[c3:], r[:HALF_ROPE], z, r[HALF_ROPE:], z], axis=0)


def unpack_w_in_t_grad(dwt, *, name):
    n_rows, d = dwt.shape
    kr = QK_ROPE
    n_out_rows = n_rows - kr
    blk = n_out_rows // 7
    assert blk * 7 == n_out_rows and blk % kr == 0 and n_rows % (2 * kr) == 0
    kr_row = Q_LORA + KV_LORA
    k_mix = kr_row // blk
    off = kr_row - k_mix * blk
    slab_block = (n_rows - 2 * kr) // (2 * kr)

    def body(prev_ref, in_ref, slab_ref, o_ref):
        k = pl.program_id(0)

        @pl.when(k < k_mix)
        def _():
            o_ref[...] = in_ref[...]

        @pl.when(k == k_mix)
        def _():
            o_ref[:off, :] = in_ref[:off, :]
            o_ref[off:off + HALF_ROPE, :] = slab_ref[:HALF_ROPE, :]
            o_ref[off + HALF_ROPE:off + kr, :] = slab_ref[2 * HALF_ROPE:3 * HALF_ROPE, :]
            o_ref[off + kr:, :] = in_ref[off:blk - kr, :]

        @pl.when(k > k_mix)
        def _():
            o_ref[:kr, :] = prev_ref[blk - kr:, :]
            o_ref[kr:, :] = in_ref[:blk - kr, :]

    out = _pcall(body, name=name, grid=(7,),
                 ins=[(dwt, (blk, d), lambda k: (jnp.maximum(k - 1, 0), 0)), (dwt, (blk, d), lambda k: (k, 0)),
                      (dwt, (2 * kr, d), lambda k: (slab_block, 0))],
                 outs=[((n_out_rows, d), dwt.dtype, (blk, d), lambda k: (k, 0))], semantics=("parallel",))[0]
    return out.reshape(N_DEV, n_out_rows // N_DEV, d)


def _rope_tables(positions):
    inv_freq = ROPE_BASE ** (-jnp.arange(0, QK_ROPE, 2, dtype=F32) / QK_ROPE)
    ang = positions.astype(F32)[:, None] * inv_freq
    cos, sin = jnp.cos(ang), jnp.sin(ang)
    z = jnp.zeros_like(cos)
    return jnp.concatenate([cos, z, cos, z], axis=-1), jnp.concatenate([-sin, z, sin, z], axis=-1)


def _mlp_up(x, gain, w1, tag):
    hn = rms_fwd(x, gain, name=f"mlp{tag}_norm")

    def act_epi(acc):
        a = jnp.maximum(acc, 0.0)
        return a, a * a

    T = x.shape[0]
    F = w1.shape[0] * w1.shape[2]
    a, act = mm(hn, w1, name=f"mlp{tag}_up", outs=[((T, F), BF, None), ((T, F), BF, None)], epi=act_epi)
    return hn, a, act


def _mlp_down(x, act, w2, tag, part=0):
    n = w2.shape[1]
    bm = _tile(x.shape[0], MM_TILE)
    bn = _tile(n, MM_TILE)
    per = n // bn
    return mm(act, w2, name=f"mlp{tag}_down{part}", out=((x.shape[0], n), F32), bm=bm, bn=bn,
              epi=lambda acc, r: (acc + r[...],), epi_ins=[(x, (bm, bn), lambda i, j, k: (i, part * per + j))])


def _mlp_bwd_weights(w1, w2, saved, dxb, tag):
    hn, a, act = saved
    T, D = dxb.shape
    F = a.shape[1]
    bm = _tile(T, MM_TILE)
    bn = _tile(F, min(MM_TILE, w1.shape[2]))
    dhid = mm(dxb, w2, tb=True, name=f"mlp{tag}_dhid", out=((T, F), BF), bm=bm, bn=bn,
              epi=lambda acc, a_ref: (2.0 * a_ref[...].astype(F32) * acc,),
              epi_ins=[(a, (bm, bn), lambda i, j, k: (i, j))])
    dw2 = mm(act, dxb, ta=True, name=f"mlp{tag}_dw2", out=((F, D), BF))
    dw1 = mm(hn, dhid, ta=True, name=f"mlp{tag}_dw1", out=(w1.shape, BF))
    return dhid, dw1, dw2.reshape(N_DEV, F // N_DEV, D)


def _reduce_begin(grads, tag):
    return sibling_start(grads, name=f"reduce_sibling_start_{tag}")


def _reduce_continue(sib, after, tag, index):
    grads, a_bufs = sibling_wait(sib, after, name=f"reduce_sibling_wait_{tag}")
    pairs = [run_job(pair_job(g, a), index=index, name=f"pair_sum_{tag}{t}")[0]
             for t, (g, a) in enumerate(zip(grads, a_bufs))]
    return grads, a_bufs, chips_start(pairs, name=f"reduce_chips_start_{tag}")


def kernel(x, positions, e_norm_mix, e_w_in, e_q_norm, e_w_uq, e_kv_norm, e_w_ukv, e_v_norm, e_sgu_w, e_sgu_b, e_mla_out_norm, e_sgu_out_norm, e_w_out, o_norm_mix, o_w_in, o_conv_w, o_w_out, mlp_norm, mlp_w1, mlp_w2, final_norm, loss_target, m_e_norm_mix, m_e_w_in, m_e_q_norm, m_e_w_uq, m_e_kv_norm, m_e_w_ukv, m_e_v_norm, m_e_sgu_w, m_e_sgu_b, m_e_mla_out_norm, m_e_sgu_out_norm, m_e_w_out, m_o_norm_mix, m_o_w_in, m_o_conv_w, m_o_w_out, m_mlp_norm, m_mlp_w1, m_mlp_w2, m_final_norm, v_e_norm_mix, v_e_w_in, v_e_q_norm, v_e_w_uq, v_e_kv_norm, v_e_w_ukv, v_e_v_norm, v_e_sgu_w, v_e_sgu_b, v_e_mla_out_norm, v_e_sgu_out_norm, v_e_w_out, v_o_norm_mix, v_o_w_in, v_o_conv_w, v_o_w_out, v_mlp_norm, v_mlp_w1, v_mlp_w2, v_final_norm):
    T, D = x.shape[1], x.shape[2]
    d_shard = o_norm_mix.shape[1]
    x0 = x[0]
    target = loss_target[0]
    me = 4 * lax.axis_index("x") + 2 * lax.axis_index("y") + lax.axis_index("c")

    bf = lambda s: s.astype(BF)
    gather_groups = [[bf(jnp.transpose(e_w_in[0])), bf(e_w_uq[0]), bf(e_w_ukv[0])], [bf(e_w_out[0]), bf(mlp_w1[0])],
                     [bf(mlp_w2[0]), bf(o_w_in[0])], [bf(o_w_out[0]), bf(mlp_w1[1])], [bf(mlp_w2[1])]]
    small_rows = jnp.concatenate([o_norm_mix, o_conv_w[0], jnp.zeros((4, d_shard), F32)], axis=0)
    gather_groups[0].insert(0, small_rows)
    started, start_token = gather_start(gather_groups[:1], x0, name="gather_start0")
    started_rest, rest_token = gather_start(gather_groups[1:], start_token, name="gather_start1")
    started += started_rest

    def gathered(gi, after):
        srcs, lands = gather_wait(started[gi], after, name=f"gather_wait{gi}")
        return gather_finish(srcs, lands, name=f"gather_finish{gi}")

    w_tril = jnp.tril(e_sgu_w[0])
    w_tril_b = w_tril.astype(BF)
    w_tril_tb = jnp.swapaxes(w_tril, 1, 2).astype(BF)
    b_full = jnp.repeat(e_sgu_b[0].T, CH, axis=1)
    v_gain = e_v_norm[0].reshape(1, SGU_OUT)
    cos_t, sin_t = _rope_tables(positions[0])
    mlp_gain = [mlp_norm[0:1], mlp_norm[1:2]]
    final_gain = final_norm.reshape(1, D)

    h0 = rms_fwd(x0, e_norm_mix, name="e_norm", deps=[rest_token])
    small_g, g_w_in_t, g_w_uq, w_ukv = gathered(0, [h0, cos_t, sin_t, w_tril_b, w_tril_tb, b_full])
    o_norm_full = small_g[:, 0, :].reshape(1, D)
    conv_w_full = jnp.transpose(small_g[:, 1:4, :], (1, 0, 2)).reshape(3, D)
    w_in_t = _pack_w_in_t(g_w_in_t)
    w_uq = jnp.concatenate([g_w_uq[..., :QK_NOPE], _rope_slab(g_w_uq[..., QK_NOPE:])], axis=-1)
    proj = mm(h0, w_in_t, tb=True, name="e_in", out=((T, w_in_t.shape[0]), F32), bn=_tile(w_in_t.shape[0], 640))
    qn, kvn, krope = mla_prep(proj, e_q_norm, e_kv_norm, cos_t, sin_t, name="mla_prep")
    bm = _tile(T, MM_TILE)

    def q_epi(acc, cos_ref, sin_ref):
        return (jnp.concatenate([acc[:, :QK_NOPE], _rope_fwd(acc[:, QK_NOPE:], cos_ref[...], sin_ref[...])], axis=-1),)

    q = mm(qn, w_uq, name="mla_q", out=((T, HEADS * HEAD_PAD), BF), bm=bm, bn=HEAD_PAD, epi=q_epi,
           epi_ins=[(cos_t, (bm, LANES), lambda i, j, k: (i, 0)), (sin_t, (bm, LANES), lambda i, j, k: (i, 0))])

    def kv_epi(acc, kr_ref):
        return jnp.concatenate([acc[:, :QK_NOPE].astype(BF), kr_ref[...]], axis=-1), acc[:, QK_NOPE:]

    k, v = mm(kvn, w_ukv, name="mla_kv", bm=bm, bn=HEAD_PAD, epi=kv_epi,
              outs=[((T, HEADS * HEAD_PAD), BF, HEAD_PAD), ((T, MLA_OUT), BF, V_HEAD)],
              epi_ins=[(krope, (bm, LANES), lambda i, j, k: (i, 0))])
    attn, attn_lse = attn_fwd(q, k, v, name="attn_fwd")
    mixed = mix_fwd(attn, proj, e_mla_out_norm, e_sgu_out_norm, v_gain, w_tril_b, b_full, name="mix_fwd")
    bn = _tile(D, MM_TILE)
    g_w_out_e, w1_0 = gathered(1, [mixed])
    w_out_e = g_w_out_e.reshape(-1, D)
    x1 = mm(mixed, w_out_e, name="e_out", out=((T, D), F32), bm=bm, bn=bn,
            epi=lambda acc, r: (acc + r[...],), epi_ins=[(x0, (bm, bn), lambda i, j, k: (i, j))])
    hn0, a0, act0 = _mlp_up(x1, mlp_gain[0], w1_0, 0)
    g_w2_0, g_w_in_o = gathered(2, [act0])
    w2_0 = g_w2_0.reshape(-1, D)
    x2 = _mlp_down(x1, act0, w2_0, 0)
    ho = rms_fwd(x2, o_norm_full, name="o_norm")
    proj_o = mm(ho, g_w_in_o, name="o_in", out=((T, 3 * D), F32))
    gated = conv_fwd(proj_o, conv_w_full, name="conv_fwd")
    g_w_out_o, w1_1 = gathered(3, [gated])
    w_out_o = g_w_out_o.reshape(-1, D)
    x3 = mm(gated, w_out_o, name="o_out", out=((T, D), F32), bm=bm, bn=bn,
            epi=lambda acc, r: (acc + r[...],), epi_ins=[(x2, (bm, bn), lambda i, j, k: (i, j))])
    hn1, a1, act1 = _mlp_up(x3, mlp_gain[1], w1_1, 1)
    (g_w2_1,) = gathered(4, [act1])
    w2_1 = g_w2_1.reshape(-1, D)
    x4 = _mlp_down(x3, act1, w2_1, 1)
    w1, w2 = [w1_0, w1_1], [w2_0, w2_1]

    dx4, dx4b, d_final, loss_part = loss_bwd([x4], final_gain, target, name="loss_bwd")

    hosted = dict(job_index=device_index())
    dhid1, dw1_1, dw2_1 = _mlp_bwd_weights(w1[1], w2[1], (hn1, a1, act1), dx4b, 1)
    sib_r0 = _reduce_begin([dw1_1, dw2_1], "r0")
    dhn1 = mm(dhid1, w1[1], tb=True, name="mlp1_dhn", out=((T, D), F32), deps=[sib_r0[-1]])
    grads_r0, a_r0 = sibling_wait(sib_r0, dhn1, name="reduce_sibling_wait_r0")
    dx3, dx3b, d_mlp1 = rms_bwd(x3, mlp_gain[1], dhn1, dres=dx4, name="mlp1_norm_bwd")

    dgated, ((pair_r0a,),) = mm(dx3b, w_out_o, tb=True, name="o_out_dx", out=((T, D), F32),
                                jobs=[pair_job(grads_r0[0], a_r0[0])], **hosted)
    dw_out_o, ((pair_r0b,),) = mm(gated, dx3b, ta=True, name="o_out_dw", out=((D, D), BF),
                                  jobs=[pair_job(grads_r0[1], a_r0[1])], **hosted)
    st_r0 = chips_start([pair_r0a, pair_r0b], name="reduce_chips_start_r0")
    dproj_o, dconv_full = conv_bwd(dgated, proj_o, conv_w_full, name="conv_bwd", deps=[st_r0[-1]])
    dw_in_o = mm(ho, dproj_o, ta=True, name="o_in_dw", out=(g_w_in_o.shape, BF))
    sib_r1 = _reduce_begin([dw_out_o.reshape(g_w_out_o.shape), dw_in_o], "r1")
    dho = mm(dproj_o, g_w_in_o, tb=True, name="o_in_dx", out=((T, D), F32), deps=[sib_r1[-1]])
    grads_r1, a_r1 = sibling_wait(sib_r1, dho, name="reduce_sibling_wait_r1")
    dx2, dx2b, d_onorm_full = rms_bwd(x2, o_norm_full, dho, dres=dx3, name="o_norm_bwd")

    d_ff = a0.shape[1]
    bm_h, bn_h = _tile(T, MM_TILE), _tile(d_ff, min(MM_TILE, w1[0].shape[2]))
    dhid0, ((pair_r1a,), (pair_r1b,)) = mm(
        dx2b, w2[0], tb=True, name="mlp0_dhid", out=((T, d_ff), BF), bm=bm_h, bn=bn_h,
        epi=lambda acc, a_ref: (2.0 * a_ref[...].astype(F32) * acc,),
        epi_ins=[(a0, (bm_h, bn_h), lambda i, j, k: (i, j))],
        jobs=[pair_job(grads_r1[0], a_r1[0]), pair_job(grads_r1[1], a_r1[1])], **hosted)
    st_r1 = chips_start([pair_r1a, pair_r1b], name="reduce_chips_start_r1")
    dw2_0 = mm(act0, dx2b, ta=True, name="mlp0_dw2", out=((d_ff, D), BF), deps=[st_r1[-1]])
    b_r0 = chips_wait(st_r0, dw2_0, name="reduce_chips_wait_r0")
    dw1_0, (r_w1,) = mm(
        hn0, dhid0, ta=True, name="mlp0_dw1", out=(w1[0].shape, BF),
        jobs=[adam_job(grads_r0[0], a_r0[0], b_r0[0], mlp_w1, m_mlp_w1, v_mlp_w1, 1, None)], **hosted)
    sib_r2 = _reduce_begin([dw1_0, dw2_0.reshape(N_DEV, d_ff // N_DEV, D)], "r2")
    dhn0, (r_w2,) = mm(
        dhid0, w1[0], tb=True, name="mlp0_dhn", out=((T, D), F32), deps=[sib_r2[-1]],
        jobs=[adam_job(grads_r0[1], a_r0[1], b_r0[1], mlp_w2, m_mlp_w2, v_mlp_w2, 1, None)], **hosted)
    grads_r2, a_r2 = sibling_wait(sib_r2, dhn0, name="reduce_sibling_wait_r2")
    dx1, dx1b, d_mlp0 = rms_bwd(x1, mlp_gain[0], dhn0, dres=dx2, name="mlp0_norm_bwd")

    dmixed, ((pair_r2a,),) = mm(dx1b, w_out_e, tb=True, name="e_out_dx", out=((T, MLA_OUT + SGU_OUT), F32),
                                jobs=[pair_job(grads_r2[0], a_r2[0])], **hosted)
    dw_out_e, ((pair_r2b,),) = mm(mixed, dx1b, ta=True, name="e_out_dw", out=(w_out_e.shape, BF),
                                  jobs=[pair_job(grads_r2[1], a_r2[1])], **hosted)
    st_r2 = chips_start([pair_r2a, pair_r2b], name="reduce_chips_start_r2")
    (dattn, dproj, d_mla_out, d_sgu_out, d_vgain, d_sgu_w, d_b_full) = mix_bwd(
        dmixed, attn, proj, e_mla_out_norm, e_sgu_out_norm, v_gain, w_tril_b, w_tril_tb, b_full, name="mix_bwd",
        deps=[st_r2[-1]])
    b_r1 = chips_wait(st_r1, dattn, name="reduce_chips_wait_r1")
    dq, dk, dv = attn_bwd(q, k, v, attn, attn_lse, dattn, name="attn_bwd")
    dq_lin, dkv_lin, dproj = mla_bwd_prep(dq, dk, dv, cos_t, sin_t, dproj, name="mla_bwd_prep")
    dw_uq_pad = mm(qn, dq_lin, ta=True, name="mla_q_dw", out=(w_uq.shape, BF))
    dw_ukv = mm(kvn, dkv_lin, ta=True, name="mla_kv_dw", out=(w_ukv.shape, BF))
    dw_uq = jnp.concatenate([dw_uq_pad[..., :QK_NOPE], _rope_unslab(dw_uq_pad[..., QK_NOPE:])], axis=-1)
    sib_r2b = _reduce_begin([dw_out_e.reshape(g_w_out_e.shape), dw_uq, dw_ukv], "r2b")
    dqn = mm(dq_lin, w_uq, tb=True, name="mla_q_dx", out=((T, Q_LORA), F32), deps=[sib_r2b[-1]])
    dkvn = mm(dkv_lin, w_ukv, tb=True, name="mla_kv_dx", out=((T, KV_LORA), F32), deps=[sib_r2b[-1]])
    grads_r2b, a_r2b, st_r2b = _reduce_continue(sib_r2b, dkvn, "r2b", hosted["job_index"])
    dproj, d_qnorm = rms_bwd(proj, e_q_norm, dqn, col_block=0, want_f32=False, into=dproj, name="q_norm_bwd",
                             deps=[st_r2b[-1]])
    dproj, d_kvnorm = rms_bwd(proj, e_kv_norm, dkvn, col_block=1, want_f32=False, into=dproj, name="kv_norm_bwd")
    dw_in_t_pad, (r_w_out_o, r_w_in_o) = mm(
        dproj, h0, ta=True, name="e_in_dw", out=(w_in_t.shape, BF), bm=_tile(w_in_t.shape[0], 640),
        jobs=[adam_job(grads_r1[0], a_r1[0], b_r1[0], o_w_out, m_o_w_out, v_o_w_out, 0, None),
              adam_job(grads_r1[1], a_r1[1], b_r1[1], o_w_in, m_o_w_in, v_o_w_in, 0, None)], **hosted)
    dw_in_t = unpack_w_in_t_grad(dw_in_t_pad, name="e_in_dw_unpack")
    sib_r3 = _reduce_begin([dw_in_t], "r3")
    b_r2b = chips_wait(st_r2b, sib_r3[-1], name="reduce_chips_wait_r2b")
    dh0, (r_w_out_e, r_w_uq, r_w_ukv) = mm(
        dproj, w_in_t, name="e_in_dx", out=((T, D), F32),
        jobs=[adam_job(grads_r2b[0], a_r2b[0], b_r2b[0], e_w_out, m_e_w_out, v_e_w_out, 0, None),
              adam_job(grads_r2b[1], a_r2b[1], b_r2b[1], e_w_uq, m_e_w_uq, v_e_w_uq, 0, None),
              adam_job(grads_r2b[2], a_r2b[2], b_r2b[2], e_w_ukv, m_e_w_ukv, v_e_w_ukv, 0, None)], **hosted)
    grads_r3, a_r3, st_r3 = _reduce_continue(sib_r3, dh0, "r3", hosted["job_index"])
    tok_r3 = st_r3[-1]
    grad_x, d_enorm = rms_bwd(x0, e_norm_mix, dh0, dres=dx1, want_bf=False, name="e_norm_bwd", deps=[tok_r3])
    b_r2 = chips_wait(st_r2, grad_x, name="reduce_chips_wait_r2")

    d_sgu_b = jnp.transpose(d_b_full[:, ::CH])
    d_sgu_w_tril = jnp.tril(d_sgu_w)
    rep = [("e_norm_mix", e_norm_mix, m_e_norm_mix, v_e_norm_mix, d_enorm),
           ("e_q_norm", e_q_norm, m_e_q_norm, v_e_q_norm, d_qnorm),
           ("e_kv_norm", e_kv_norm, m_e_kv_norm, v_e_kv_norm, d_kvnorm),
           ("e_v_norm", e_v_norm, m_e_v_norm, v_e_v_norm, d_vgain),
           ("e_sgu_w", e_sgu_w, m_e_sgu_w, v_e_sgu_w, d_sgu_w_tril),
           ("e_sgu_b", e_sgu_b, m_e_sgu_b, v_e_sgu_b, d_sgu_b),
           ("e_mla_out_norm", e_mla_out_norm, m_e_mla_out_norm, v_e_mla_out_norm, d_mla_out),
           ("e_sgu_out_norm", e_sgu_out_norm, m_e_sgu_out_norm, v_e_sgu_out_norm, d_sgu_out),
           ("mlp_norm", mlp_norm, m_mlp_norm, v_mlp_norm, jnp.concatenate([d_mlp0, d_mlp1], axis=0)),
           ("final_norm", final_norm, m_final_norm, v_final_norm, d_final)]
    sizes = [int(np.prod(r[1].shape)) for r in rep]
    n_rep = sum(sizes)
    n_all = n_rep + 4 * D + 1
    width = -(-n_all // (8 * LANES)) * LANES
    pad = 8 * width - n_all
    flat = jnp.concatenate([r[4].reshape(-1) for r in rep]
                           + [d_onorm_full.reshape(-1), dconv_full.reshape(-1), loss_part[0, :1],
                              jnp.zeros((pad,), F32)])
    small_started, small_token = gather_start([[flat.reshape(8, width)]], b_r2[0], name="gather_small_grads_start")

    def finish(grads, a_bufs, b_bufs, t, w, m, v, layer=0, prev=None, tag="", deps=()):
        return run_job(adam_job(grads[t], a_bufs[t], b_bufs[t], w, m, v, layer, prev), index=hosted["job_index"],
                       name=f"adam_{tag}", deps=deps)

    r_w1 = finish(grads_r2, a_r2, b_r2, 0, mlp_w1, m_mlp_w1, v_mlp_w1, 0, r_w1, tag="w1_l0", deps=[tok_r3, small_token])
    r_w2 = finish(grads_r2, a_r2, b_r2, 1, mlp_w2, m_mlp_w2, v_mlp_w2, 0, r_w2, tag="w2_l0", deps=[r_w1[1]])
    b_r3 = chips_wait(st_r3, r_w2[1], name="reduce_chips_wait_r3")
    g_w_in_t = reduce_sum(grads_r3[0], a_r3[0], b_r3[0], name="sum_e_w_in")
    w_in_upd_t = adam_rows(g_w_in_t, jnp.transpose(e_w_in[0]), jnp.transpose(m_e_w_in[0]), jnp.transpose(v_e_w_in[0]),
                           name="adam_e_w_in")
    r_w_in = [jnp.transpose(t)[None] for t in (g_w_in_t, *w_in_upd_t)]

    small_srcs, small_lands = gather_wait(small_started[0], [r_w2[1]], name="gather_small_grads_wait")
    small_all = gather_finish(small_srcs, small_lands, name="gather_small_grads_finish")[0]
    summed = sum_rows8(small_all.reshape(N_DEV * 8, width), 8, name="sum_small_grads").reshape(-1)

    loss = summed[n_rep + 4 * D]

    def pack_rep(i):
        return jnp.concatenate([r[i].reshape(-1) for r in rep]).reshape(n_rep // LANES, LANES)

    g_rep = summed[:n_rep].reshape(n_rep // LANES, LANES)
    d_rep, nm_rep, nv_rep = adam_flat(g_rep, pack_rep(1), pack_rep(2), pack_rep(3), name="adam_replicated")

    def unpack_rep(flat2d):
        out, off = {}, 0
        f = flat2d.reshape(-1)
        for r, n in zip(rep, sizes):
            out[r[0]] = f[off:off + n].reshape(r[1].shape)
            off += n
        return out

    small = {"grad": unpack_rep(g_rep), "delta": unpack_rep(d_rep), "new_m": unpack_rep(nm_rep),
             "new_v": unpack_rep(nv_rep)}
    g_onorm = lax.dynamic_slice(summed[n_rep:n_rep + D].reshape(1, D), (0, me * d_shard), (1, d_shard))
    g_conv = lax.dynamic_slice(summed[n_rep + D:n_rep + 4 * D].reshape(3, D), (0, me * d_shard), (3, d_shard))

    def pack_sharded(norm_part, conv_part):
        return jnp.concatenate([norm_part, conv_part, jnp.zeros((4, d_shard), F32)], axis=0)

    g_sh = pack_sharded(g_onorm, g_conv)
    d_sh, nm_sh, nv_sh = adam_flat(g_sh, pack_sharded(o_norm_mix, o_conv_w[0]), pack_sharded(m_o_norm_mix, m_o_conv_w[0]),
                                   pack_sharded(v_o_norm_mix, v_o_conv_w[0]), name="adam_sharded_small")
    for kind, arr in (("grad", g_sh), ("delta", d_sh), ("new_m", nm_sh), ("new_v", nv_sh)):
        small[kind]["o_norm_mix"] = arr[0:1]
        small[kind]["o_conv_w"] = arr[1:4][None]

    big = {"e_w_in": r_w_in, "e_w_uq": r_w_uq, "e_w_ukv": r_w_ukv, "e_w_out": r_w_out_e, "o_w_in": r_w_in_o,
           "o_w_out": r_w_out_o, "mlp_w1": r_w1, "mlp_w2": r_w2}
    order = ["e_norm_mix", "e_w_in", "e_q_norm", "e_w_uq", "e_kv_norm", "e_w_ukv", "e_v_norm", "e_sgu_w", "e_sgu_b",
             "e_mla_out_norm", "e_sgu_out_norm", "e_w_out", "o_norm_mix", "o_w_in", "o_conv_w", "o_w_out", "mlp_norm",
             "mlp_w1", "mlp_w2", "final_norm"]
    result = [loss, grad_x[None]]
    for ki, kind in enumerate(("grad", "delta", "new_m", "new_v")):
        for nm in order:
            result.append(big[nm][ki] if nm in big else small[kind][nm])
    return tuple(result)
```

```python
import numpy as np
import jax
import jax.numpy as jnp
from jax import lax
from jax.experimental import pallas as pl
from jax.experimental.pallas import tpu as pltpu

BF = jnp.bfloat16
F32 = jnp.float32
MESH = pl.DeviceIdType.MESH
N_DEV = 8

EPS = 1e-6
HEADS = 8
Q_LORA = 512
KV_LORA = 512
QK_NOPE = 128
QK_ROPE = 64
HALF_ROPE = QK_ROPE // 2
V_HEAD = 128
HEAD_PAD = 256
ROPE_BASE = 10000.0
GROUPS = 8
CH = 128
CHUNK = 128
SGU_OUT = GROUPS * CH
MLA_OUT = HEADS * V_HEAD
ATTN_SCALE = float((QK_NOPE + QK_ROPE) ** -0.5)

ADAM_LR = 0.001
ADAM_B1 = 0.9
ADAM_B2 = 0.999
ADAM_EPS = 1e-08
ADAM_WD = 0.01
ADAM_STEP = 10
ADAM_C1 = 1.0 - ADAM_B1 ** ADAM_STEP
ADAM_C2 = 1.0 - ADAM_B2 ** ADAM_STEP

V7X_VMEM_BYTES = 64 * 2 ** 20
VMEM_LIMIT_CAP = V7X_VMEM_BYTES - 6 * 2 ** 20
LANES = 128
ROW_TILE = 256
ATTN_TILE = 256
STREAM_BLOCK_ELEMS = 512 * 1024
MM_TILE = 1024
MM_K_TILE = 2048
MM_K_BLOCK_MAX = 3072


def _padded_bytes(block, dtype):
    dims = [d for d in block if d is not None]
    if len(dims) >= 1:
        dims[-1] = -(-dims[-1] // LANES) * LANES
    if len(dims) >= 2:
        dims[-2] = -(-dims[-2] // 16) * 16
    return int(np.prod(dims)) * jnp.dtype(dtype).itemsize


def _pcall(body, *, name, grid, ins, outs, scratch=(), semantics=None, aliases=None, prefetch=None, deps=()):
    any_spec = pl.BlockSpec(memory_space=pl.ANY)
    if deps:
        n_lead = len(ins) + (1 if prefetch is not None else 0)
        n_deps = len(deps)
        inner = body

        def body(*refs):
            inner(*refs[:n_lead], *refs[n_lead + n_deps:])

        ins = list(ins) + [(d, None, None) for d in deps]
    in_specs = [any_spec if b is None else pl.BlockSpec(b, m) for _, b, m in ins]
    out_specs = [any_spec if b is None else pl.BlockSpec(b, m) for _, _, b, m in outs]
    out_shape = [pltpu.HBM(s, d) for s, d, _, _ in outs]
    est = 0
    for a, b, _ in ins:
        if b is not None:
            est += 2 * _padded_bytes(b, a.dtype)
    for _, d, b, _ in outs:
        if b is not None:
            est += 2 * _padded_bytes(b, d)
    for s in scratch:
        if hasattr(s, "shape") and hasattr(s, "dtype"):
            est += _padded_bytes(s.shape, s.dtype)
    limit = int(min(VMEM_LIMIT_CAP, est + 16 * 2 ** 20))
    params = pltpu.CompilerParams(
        dimension_semantics=semantics or ("arbitrary",) * len(grid), vmem_limit_bytes=limit)
    args = [pltpu.with_memory_space_constraint(a, pltpu.HBM) for a, _, _ in ins]
    if prefetch is not None:
        grid_spec = pltpu.PrefetchScalarGridSpec(
            num_scalar_prefetch=1, grid=grid, in_specs=in_specs, out_specs=out_specs, scratch_shapes=list(scratch))
        call = pl.pallas_call(body, out_shape=out_shape, grid_spec=grid_spec, name=name, compiler_params=params,
                              input_output_aliases=aliases or {})
        return call(prefetch, *args)
    call = pl.pallas_call(body, out_shape=out_shape, grid=grid, in_specs=in_specs, out_specs=out_specs,
                          scratch_shapes=list(scratch), name=name, compiler_params=params,
                          input_output_aliases=aliases or {})
    return call(*args)


def _tile(dim, pref, quantum=LANES):
    if dim <= pref:
        return dim
    t = (pref // quantum) * quantum
    while t >= quantum:
        if dim % t == 0:
            return t
        t -= quantum
    return dim


def _vshape(arr_shape):
    if len(arr_shape) == 2:
        return tuple(arr_shape)
    s, r, c = arr_shape
    return (r, s * c)


def _vblock(arr_shape, br, bc, rc):
    if len(arr_shape) == 2:
        return (br, bc), (lambda *g: rc(*g))
    _, _, c = arr_shape
    assert c % bc == 0, (arr_shape, bc)
    per = c // bc

    def imap(*g):
        ri, ci = rc(*g)
        return (ci // per, ri, ci % per)

    return (None, br, bc), imap


def _shard_width(*shapes):
    w = None
    for s in shapes:
        if len(s) == 3:
            w = s[2] if w is None else int(np.gcd(w, s[2]))
    return w


def mm(a, b, *, name, ta=False, tb=False, out=None, outs=None, epi=None, epi_ins=(), bm=None, bn=None, bk=None,
       deps=(), jobs=(), job_index=None):
    av, bv = _vshape(a.shape), _vshape(b.shape)
    M, K = (av[1], av[0]) if ta else av
    K2, N = (bv[1], bv[0]) if tb else bv
    assert K == K2, (a.shape, b.shape, ta, tb)
    if outs is None:
        outs = [(out[0], out[1], None)]
    a_sw = _shard_width(a.shape)
    b_sw = _shard_width(b.shape)
    o_sw = _shard_width(*[o[0] for o in outs])
    m_lim = a_sw if (ta and a_sw) else None
    k_lim = [w for w in ((a_sw if not ta else None), (b_sw if tb else None)) if w]
    n_lim = [w for w in ((b_sw if not tb else None), o_sw) if w]
    if bm is None:
        bm = _tile(M, min([MM_TILE] + ([m_lim] if m_lim else [])))
    if bn is None:
        bn = _tile(N, min([MM_TILE] + n_lim))
    k_shards = 0
    if tb and len(b.shape) == 3 and bk is None and not (a_sw and not ta):
        k_shards = 1
        while 2 * k_shards <= b.shape[0] and 2 * k_shards * b_sw <= MM_K_BLOCK_MAX:
            k_shards *= 2
        bk = k_shards * b_sw
    if bk is None:
        bk = K if (K <= 4096 and not k_lim) else _tile(K, min([MM_K_TILE] + k_lim))
    assert M % bm == 0 and N % bn == 0 and K % bk == 0, (name, M, N, K, bm, bn, bk)
    nk = K // bk
    grid = (M // bm, N // bn, nk)
    if ta:
        a_blk, a_map = _vblock(a.shape, bk, bm, lambda i, j, k: (k, i))
    else:
        a_blk, a_map = _vblock(a.shape, bm, bk, lambda i, j, k: (i, k))
    if k_shards:
        b_blk, b_map = (k_shards, bn, b_sw), (lambda i, j, k: (k, j, 0))
    elif tb:
        b_blk, b_map = _vblock(b.shape, bn, bk, lambda i, j, k: (j, k))
    else:
        b_blk, b_map = _vblock(b.shape, bk, bn, lambda i, j, k: (k, j))
    dn = (((0 if ta else 1,), (1 if tb else 0,)), ((), ()))
    ins = [(a, a_blk, a_map), (b, b_blk, b_map)] + list(epi_ins)
    out_list = []
    for shape, dtype, cols in outs:
        cols = cols or bn
        blk, imap = _vblock(shape, bm, cols, lambda i, j, k: (i, j))
        out_list.append((shape, dtype, blk, imap))
    n_e, n_o = len(epi_ins), len(out_list)

    n_steps = grid[0] * grid[1] * nk
    built = [job(n_steps) for job in jobs]
    aliases = {}
    job_slices = []
    if built:
        def lin(i, j, k):
            return (i * grid[1] + j) * nk + k

        ins = [(arr, blk, None if blk is None else (lambda i, j, k, s, f=f: f(i, j, k))) for arr, blk, f in ins]
        out_list = [(sh, dt, blk, (lambda i, j, k, s, f=f: f(i, j, k))) for sh, dt, blk, f in out_list]
        n_main_in, n_main_out = len(ins), len(out_list)
        for jb in built:
            i0, o0 = len(ins), len(out_list)
            ins += [(arr, blk, None if blk is None else (lambda i, j, k, s, f=f: f(lin(i, j, k), s)))
                    for arr, blk, f in jb["ins"]]
            out_list += [(sh, dt, blk, (lambda i, j, k, s, f=f: f(lin(i, j, k), s))) for sh, dt, blk, f in jb["outs"]]
            aliases.update({1 + i0 + ai: o0 + ao for ai, ao in jb["aliases"].items()})
            job_slices.append((i0, len(jb["ins"]), o0, len(jb["outs"])))
    n_in_total = len(ins)

    def body(*refs):
        if built:
            refs = refs[1:]
        a_ref, b_ref = refs[0], refs[1]
        e_refs = refs[2:2 + n_e]
        o_refs = refs[n_in_total:n_in_total + n_o]
        for jb, (i0, ni, o0, no) in zip(built, job_slices):
            jb["fn"](refs[i0:i0 + ni], refs[n_in_total + o0:n_in_total + o0 + no])

        def finish(acc):
            res = epi(acc, *e_refs) if epi is not None else (acc,)
            for o_ref, r in zip(o_refs, res):
                o_ref[...] = r.astype(o_ref.dtype)

        x = a_ref[...].astype(BF)
        y = b_ref[...].astype(BF)
        if k_shards:
            p = None
            for s in range(k_shards):
                part = lax.dot_general(x[:, s * b_sw:(s + 1) * b_sw], y[s], dn, preferred_element_type=F32)
                p = part if p is None else p + part
        else:
            p = lax.dot_general(x, y, dn, preferred_element_type=F32)
        if nk == 1:
            finish(p)
        else:
            acc_ref = refs[-1]
            k = pl.program_id(2)

            @pl.when(k == 0)
            def _():
                acc_ref[...] = p

            @pl.when(k > 0)
            def _():
                acc_ref[...] += p

            @pl.when(k == nk - 1)
            def _():
                finish(acc_ref[...])

    scratch = [pltpu.VMEM((bm, bn), F32)] if nk > 1 else []
    res = _pcall(body, name=name, grid=grid, ins=ins, outs=out_list, scratch=scratch, deps=deps,
                 semantics=("parallel", "parallel", "arbitrary"), prefetch=job_index if built else None, aliases=aliases)
    main = res[0] if n_o == 1 else res[:n_o]
    if not built:
        return main
    return main, [res[o0:o0 + no] for _, _, o0, no in job_slices]


_GELU_K = float(np.sqrt(2.0 / np.pi))
_GELU_C = 0.044715


def _gelu(x):
    t = jnp.tanh(_GELU_K * (x + _GELU_C * (x * x * x)))
    return 0.5 * x * (1.0 + t)


def _gelu_grad(x):
    t = jnp.tanh(_GELU_K * (x + _GELU_C * (x * x * x)))
    return 0.5 * (1.0 + t) + 0.5 * x * (1.0 - t * t) * (_GELU_K * (1.0 + 3.0 * _GELU_C * (x * x)))


def _rstd(x):
    return lax.rsqrt(jnp.mean(x * x, axis=-1, keepdims=True) + EPS)


def _rms_bwd(x, gain, dy):
    r = _rstd(x)
    xh = x * r
    gdy = dy * gain
    dx = r * (gdy - xh * jnp.mean(gdy * xh, axis=-1, keepdims=True))
    return dx, dy * xh


def _rope_fwd(x, cos_t, sin_t):
    return x * cos_t + pltpu.roll(x, 2 * HALF_ROPE, 1) * sin_t


def _rope_bwd(dy, cos_t, sin_t):
    return dy * cos_t + pltpu.roll(dy * sin_t, 2 * HALF_ROPE, 1)


def _acc_rows(ref, val, first):
    s = jnp.sum(val, axis=0, keepdims=True)

    @pl.when(first)
    def _():
        ref[...] = s

    @pl.when(jnp.logical_not(first))
    def _():
        ref[...] += s


def rms_fwd(x, gain, *, name, col_block=0, width=None, deps=()):
    T = x.shape[0]
    width = width or x.shape[1]
    tm = _tile(T, ROW_TILE, 8)

    def body(x_ref, g_ref, o_ref):
        v = x_ref[...]
        o_ref[...] = (v * _rstd(v) * g_ref[...]).astype(BF)

    return _pcall(body, name=name, grid=(T // tm,),
                  ins=[(x, (tm, width), lambda i: (i, col_block)), (gain, (1, width), lambda i: (0, 0))],
                  outs=[((T, width), BF, (tm, width), lambda i: (i, 0))], semantics=("parallel",), deps=deps)[0]


def rms_bwd(x, gain, dy, *, name, col_block=0, dres=None, want_f32=True, want_bf=True, into=None, deps=()):
    T, width = dy.shape
    tm = _tile(T, ROW_TILE, 8)
    has_res = dres is not None

    def body(*refs):
        x_ref, g_ref, dy_ref = refs[:3]
        pos = 3
        res_ref = None
        if has_res:
            res_ref = refs[pos]
            pos += 1
        if into is not None:
            pos += 1
        outs = refs[pos:]
        dx, dg_rows = _rms_bwd(x_ref[...], g_ref[...], dy_ref[...])
        if has_res:
            dx = dx + res_ref[...]
        o = 0
        if want_f32:
            outs[o][...] = dx
            o += 1
        if want_bf:
            outs[o][...] = dx.astype(BF)
            o += 1
        _acc_rows(outs[o], dg_rows, pl.program_id(0) == 0)

    ins = [(x, (tm, width), lambda i: (i, col_block)), (gain, (1, width), lambda i: (0, 0)),
           (dy, (tm, width), lambda i: (i, 0))]
    if has_res:
        ins.append((dres, (tm, width), lambda i: (i, 0)))
    outs = []
    aliases = {}
    if want_f32:
        outs.append(((T, width), F32, (tm, width), lambda i: (i, 0)))
    if want_bf and into is not None:
        ins.append((into, None, None))
        aliases[len(ins) - 1] = len(outs)
        outs.append((into.shape, BF, (tm, width), lambda i: (i, col_block)))
    elif want_bf:
        outs.append(((T, width), BF, (tm, width), lambda i: (i, 0)))
    outs.append(((1, width), F32, (1, width), lambda i: (0, 0)))
    return _pcall(body, name=name, grid=(T // tm,), ins=ins, outs=outs, aliases=aliases, deps=deps)


def mla_prep(proj, q_norm, kv_norm, cos_t, sin_t, *, name):
    T = proj.shape[0]
    tm = _tile(T, ROW_TILE, 8)
    kr_block = (proj.shape[1] - LANES) // LANES

    def body(cq_ref, ckv_ref, kr_ref, qg_ref, kg_ref, cos_ref, sin_ref, qn_ref, kvn_ref, krope_ref):
        cq = cq_ref[...]
        qn_ref[...] = (cq * _rstd(cq) * qg_ref[...]).astype(BF)
        ckv = ckv_ref[...]
        kvn_ref[...] = (ckv * _rstd(ckv) * kg_ref[...]).astype(BF)
        krope_ref[...] = _rope_fwd(kr_ref[...], cos_ref[...], sin_ref[...]).astype(BF)

    return _pcall(
        body, name=name, grid=(T // tm,),
        ins=[(proj, (tm, Q_LORA), lambda i: (i, 0)), (proj, (tm, KV_LORA), lambda i: (i, 1)),
             (proj, (tm, LANES), lambda i: (i, kr_block)),
             (q_norm, (1, Q_LORA), lambda i: (0, 0)), (kv_norm, (1, KV_LORA), lambda i: (0, 0)),
             (cos_t, (tm, LANES), lambda i: (i, 0)), (sin_t, (tm, LANES), lambda i: (i, 0))],
        outs=[((T, Q_LORA), BF, (tm, Q_LORA), lambda i: (i, 0)), ((T, KV_LORA), BF, (tm, KV_LORA), lambda i: (i, 0)),
              ((T, LANES), BF, (tm, LANES), lambda i: (i, 0))],
        semantics=("parallel",))


def _attn_scores(q, k_blk, diagonal):
    s = lax.dot_general(q, k_blk, (((1,), (1,)), ((), ())), preferred_element_type=F32) * ATTN_SCALE
    if diagonal:
        row = lax.broadcasted_iota(jnp.int32, s.shape, 0)
        col = lax.broadcasted_iota(jnp.int32, s.shape, 1)
        s = jnp.where(col <= row, s, -jnp.inf)
    return s


def attn_fwd(q, k, v, *, name):
    T = q.shape[0]
    tq = _tile(T, ATTN_TILE, 8)

    def body(q_ref, k_ref, v_ref, o_ref, lse_ref):
        i = pl.program_id(1)
        qv = q_ref[...]

        def block(kb, carry, diagonal):
            m, l, acc = carry
            start = pl.multiple_of(kb * tq, tq)
            s = _attn_scores(qv, k_ref[pl.ds(start, tq), :], diagonal)
            m_new = jnp.maximum(m, jnp.max(s, axis=-1, keepdims=True))
            alpha = jnp.exp(m - m_new)
            p = jnp.exp(s - m_new)
            l = alpha * l + jnp.sum(p, axis=-1, keepdims=True)
            acc = alpha * acc + jnp.dot(p.astype(BF), v_ref[pl.ds(start, tq), :], preferred_element_type=F32)
            return m_new, l, acc

        init = (jnp.full((tq, 1), -jnp.inf, F32), jnp.zeros((tq, 1), F32), jnp.zeros((tq, V_HEAD), F32))
        carry = lax.fori_loop(0, i, lambda kb, c: block(kb, c, False), init)
        m, l, acc = block(i, carry, True)
        o_ref[...] = acc / l
        lse_ref[...] = jnp.broadcast_to(m + jnp.log(l), (tq, V_HEAD))

    return _pcall(
        body, name=name, grid=(HEADS, T // tq),
        ins=[(q, (tq, HEAD_PAD), lambda h, i: (i, h)), (k, (T, HEAD_PAD), lambda h, i: (0, h)),
             (v, (T, V_HEAD), lambda h, i: (0, h))],
        outs=[((T, MLA_OUT), F32, (tq, V_HEAD), lambda h, i: (i, h)),
              ((T, MLA_OUT), F32, (tq, V_HEAD), lambda h, i: (i, h))], semantics=("parallel", "parallel"))


def attn_bwd(q, k, v, o, lse, do, *, name):
    T = q.shape[0]
    tq = _tile(T, ATTN_TILE, 8)

    def body(q_ref, k_ref, v_ref, o_ref, lse_ref, do_ref, dq_ref, dk_ref, dv_ref):
        i = pl.program_id(1)

        @pl.when(i == 0)
        def _():
            dk_ref[...] = jnp.zeros_like(dk_ref)
            dv_ref[...] = jnp.zeros_like(dv_ref)

        qv = q_ref[...]
        do_t = do_ref[...]
        lse_v = lse_ref[:, 0:1]
        delta = jnp.sum(do_t.astype(F32) * o_ref[...], axis=-1, keepdims=True)

        def block(kb, dq, diagonal):
            start = pl.multiple_of(kb * tq, tq)
            k_blk = k_ref[pl.ds(start, tq), :]
            v_blk = v_ref[pl.ds(start, tq), :]
            p = jnp.exp(_attn_scores(qv, k_blk, diagonal) - lse_v)
            dp = lax.dot_general(do_t, v_blk, (((1,), (1,)), ((), ())), preferred_element_type=F32)
            ds = (p * (dp - delta) * ATTN_SCALE).astype(BF)
            dk_ref[pl.ds(start, tq), :] += lax.dot_general(ds, qv, (((0,), (0,)), ((), ())), preferred_element_type=F32)
            dv_ref[pl.ds(start, tq), :] += lax.dot_general(p.astype(BF), do_t, (((0,), (0,)), ((), ())),
                                                          preferred_element_type=F32)
            return dq + jnp.dot(ds, k_blk, preferred_element_type=F32)

        dq = lax.fori_loop(0, i, lambda kb, c: block(kb, c, False), jnp.zeros((tq, HEAD_PAD), F32))
        dq_ref[...] = block(i, dq, True)

    return _pcall(
        body, name=name, grid=(HEADS, T // tq),
        ins=[(q, (tq, HEAD_PAD), lambda h, i: (i, h)), (k, (T, HEAD_PAD), lambda h, i: (0, h)),
             (v, (T, V_HEAD), lambda h, i: (0, h)), (o, (tq, V_HEAD), lambda h, i: (i, h)),
             (lse, (tq, V_HEAD), lambda h, i: (i, h)), (do, (tq, V_HEAD), lambda h, i: (i, h))],
        outs=[((T, HEADS * HEAD_PAD), F32, (tq, HEAD_PAD), lambda h, i: (i, h)),
              ((T, HEADS * HEAD_PAD), F32, (T, HEAD_PAD), lambda h, i: (0, h)),
              ((T, MLA_OUT), F32, (T, V_HEAD), lambda h, i: (0, h))],
        semantics=("parallel", "arbitrary"))


def mla_bwd_prep(dq, dk, dv, cos_t, sin_t, dproj, *, name):
    T = dq.shape[0]
    tm = _tile(T, ROW_TILE, 8)
    kr_block = (dproj.shape[1] - LANES) // LANES

    def body(dq_ref, dk_ref, dv_ref, cos_ref, sin_ref, dproj_in, dql_ref, dkvl_ref, dkr_ref):
        cos_v, sin_v = cos_ref[...], sin_ref[...]
        kr = jnp.zeros((tm, LANES), F32)
        for h in range(HEADS):
            lo = h * HEAD_PAD
            dql_ref[:, lo:lo + QK_NOPE] = dq_ref[:, lo:lo + QK_NOPE].astype(BF)
            dql_ref[:, lo + QK_NOPE:lo + HEAD_PAD] = _rope_bwd(
                dq_ref[:, lo + QK_NOPE:lo + HEAD_PAD], cos_v, sin_v).astype(BF)
            dkvl_ref[:, lo:lo + QK_NOPE] = dk_ref[:, lo:lo + QK_NOPE].astype(BF)
            dkvl_ref[:, lo + QK_NOPE:lo + HEAD_PAD] = dv_ref[:, h * V_HEAD:(h + 1) * V_HEAD].astype(BF)
            kr = kr + dk_ref[:, lo + QK_NOPE:lo + HEAD_PAD]
        dkr_ref[...] = _rope_bwd(kr, cos_v, sin_v).astype(BF)

    W = HEADS * HEAD_PAD
    return _pcall(
        body, name=name, grid=(T // tm,),
        ins=[(dq, (tm, W), lambda i: (i, 0)), (dk, (tm, W), lambda i: (i, 0)), (dv, (tm, MLA_OUT), lambda i: (i, 0)),
             (cos_t, (tm, LANES), lambda i: (i, 0)), (sin_t, (tm, LANES), lambda i: (i, 0)), (dproj, None, None)],
        outs=[((T, W), BF, (tm, W), lambda i: (i, 0)), ((T, W), BF, (tm, W), lambda i: (i, 0)),
              (dproj.shape, BF, (tm, LANES), lambda i: (i, kr_block))],
        aliases={5: 2}, semantics=("parallel",))


def _group_norm_stats(vg):
    mu = jnp.mean(vg, axis=-1, keepdims=True)
    d = vg - mu
    r = lax.rsqrt(jnp.mean(d * d, axis=-1, keepdims=True) + EPS)
    return d * r, r


def mix_fwd(a, proj, g_mla, g_sgu, v_gain, w_tril, b_full, *, name):
    T = a.shape[0]
    tm = _tile(T, ROW_TILE, CHUNK)
    n_chunk = tm // CHUNK

    def body(a_ref, u_ref, v_ref, gm_ref, gs_ref, vg_ref, w_ref, b_ref, o_ref, s_scr):
        av = a_ref[...]
        o_ref[:, :MLA_OUT] = (av * _rstd(av) * gm_ref[...]).astype(BF)
        for g in range(GROUPS):
            sl = slice(g * CH, (g + 1) * CH)
            vhat, _ = _group_norm_stats(_gelu(v_ref[:, sl]))
            vn = (vhat * vg_ref[:, sl]).astype(BF)
            u = _gelu(u_ref[:, sl])
            for ci in range(n_chunk):
                rs = slice(ci * CHUNK, (ci + 1) * CHUNK)
                y = jnp.dot(w_ref[g], vn[rs], preferred_element_type=F32) + b_ref[:, sl]
                s_scr[rs, sl] = u[rs] * y
        s = s_scr[...]
        o_ref[:, MLA_OUT:] = (s * _rstd(s) * gs_ref[...]).astype(BF)

    return _pcall(
        body, name=name, grid=(T // tm,),
        ins=[(a, (tm, MLA_OUT), lambda i: (i, 0)), (proj, (tm, SGU_OUT), lambda i: (i, 1)),
             (proj, (tm, SGU_OUT), lambda i: (i, 2)), (g_mla, (1, MLA_OUT), lambda i: (0, 0)),
             (g_sgu, (1, SGU_OUT), lambda i: (0, 0)), (v_gain, (1, SGU_OUT), lambda i: (0, 0)),
             (w_tril, (GROUPS, CHUNK, CHUNK), lambda i: (0, 0, 0)), (b_full, (CHUNK, SGU_OUT), lambda i: (0, 0))],
        outs=[((T, MLA_OUT + SGU_OUT), BF, (tm, MLA_OUT + SGU_OUT), lambda i: (i, 0))],
        scratch=[pltpu.VMEM((tm, SGU_OUT), F32)], semantics=("parallel",))[0]


def mix_bwd(dmixed, a, proj, g_mla, g_sgu, v_gain, w_tril, w_tril_t, b_full, *, name, deps=()):
    T = a.shape[0]
    tm = _tile(T, ROW_TILE, CHUNK)
    n_chunk = tm // CHUNK
    uv0 = Q_LORA + KV_LORA

    def body(dm_a_ref, dm_s_ref, a_ref, u_ref, v_ref, gm_ref, gs_ref, vg_ref, w_ref, wt_ref, b_ref,
             da_ref, duv_ref, dgm_ref, dgs_ref, dvg_ref, dw_ref, db_ref, s_scr, y_scr):
        first = pl.program_id(0) == 0
        duv_ref[:, :uv0] = jnp.zeros((tm, uv0), BF)
        duv_ref[:, uv0 + 2 * SGU_OUT:] = jnp.zeros((tm, duv_ref.shape[1] - uv0 - 2 * SGU_OUT), BF)
        da, dgm_rows = _rms_bwd(a_ref[...], gm_ref[...], dm_a_ref[...])
        da_ref[...] = da.astype(BF)
        _acc_rows(dgm_ref, dgm_rows, first)

        for g in range(GROUPS):
            sl = slice(g * CH, (g + 1) * CH)
            vhat, _ = _group_norm_stats(_gelu(v_ref[:, sl]))
            vn = (vhat * vg_ref[:, sl]).astype(BF)
            u = _gelu(u_ref[:, sl])
            for ci in range(n_chunk):
                rs = slice(ci * CHUNK, (ci + 1) * CHUNK)
                y = jnp.dot(w_ref[g], vn[rs], preferred_element_type=F32) + b_ref[:, sl]
                y_scr[rs, sl] = y
                s_scr[rs, sl] = u[rs] * y
        ds, dgs_rows = _rms_bwd(s_scr[...], gs_ref[...], dm_s_ref[...])
        _acc_rows(dgs_ref, dgs_rows, first)
        s_scr[...] = ds

        @pl.when(first)
        def _():
            dw_ref[...] = jnp.zeros_like(dw_ref)
            db_ref[...] = jnp.zeros_like(db_ref)

        for g in range(GROUPS):
            sl = slice(g * CH, (g + 1) * CH)
            upre = u_ref[:, sl]
            vpre = v_ref[:, sl]
            u = _gelu(upre)
            vhat, r = _group_norm_stats(_gelu(vpre))
            gain = vg_ref[:, sl]
            vn = (vhat * gain).astype(BF)
            dsg = s_scr[:, sl]
            duv_ref[:, uv0 + g * CH:uv0 + (g + 1) * CH] = (dsg * y_scr[:, sl] * _gelu_grad(upre)).astype(BF)
            dy = dsg * u
            dyb = dy.astype(BF)
            dvn_parts = []
            for ci in range(n_chunk):
                rs = slice(ci * CHUNK, (ci + 1) * CHUNK)
                dvn_parts.append(jnp.dot(wt_ref[g], dyb[rs], preferred_element_type=F32))
                dw_ref[g] += lax.dot_general(dyb[rs], vn[rs], (((1,), (1,)), ((), ())), preferred_element_type=F32)
                db_ref[:, sl] += jnp.broadcast_to(jnp.sum(dy[rs], axis=-1, keepdims=True), (CHUNK, CH))
            dvn = dvn_parts[0] if n_chunk == 1 else jnp.concatenate(dvn_parts, axis=0)
            _acc_rows(dvg_ref.at[:, sl], dvn * vhat, first)
            dvh = dvn * gain
            dvg = r * (dvh - jnp.mean(dvh, axis=-1, keepdims=True)
                       - vhat * jnp.mean(dvh * vhat, axis=-1, keepdims=True))
            duv_ref[:, uv0 + SGU_OUT + g * CH:uv0 + SGU_OUT + (g + 1) * CH] = (dvg * _gelu_grad(vpre)).astype(BF)

    return _pcall(
        body, name=name, grid=(T // tm,),
        ins=[(dmixed, (tm, MLA_OUT), lambda i: (i, 0)), (dmixed, (tm, SGU_OUT), lambda i: (i, 1)),
             (a, (tm, MLA_OUT), lambda i: (i, 0)), (proj, (tm, SGU_OUT), lambda i: (i, 1)),
             (proj, (tm, SGU_OUT), lambda i: (i, 2)), (g_mla, (1, MLA_OUT), lambda i: (0, 0)),
             (g_sgu, (1, SGU_OUT), lambda i: (0, 0)), (v_gain, (1, SGU_OUT), lambda i: (0, 0)),
             (w_tril, (GROUPS, CHUNK, CHUNK), lambda i: (0, 0, 0)), (w_tril_t, (GROUPS, CHUNK, CHUNK), lambda i: (0, 0, 0)),
             (b_full, (CHUNK, SGU_OUT), lambda i: (0, 0))],
        outs=[((T, MLA_OUT), BF, (tm, MLA_OUT), lambda i: (i, 0)),
              ((T, proj.shape[1]), BF, (tm, proj.shape[1]), lambda i: (i, 0)),
              ((1, MLA_OUT), F32, (1, MLA_OUT), lambda i: (0, 0)), ((1, SGU_OUT), F32, (1, SGU_OUT), lambda i: (0, 0)),
              ((1, SGU_OUT), F32, (1, SGU_OUT), lambda i: (0, 0)),
              ((GROUPS, CHUNK, CHUNK), F32, (GROUPS, CHUNK, CHUNK), lambda i: (0, 0, 0)),
              ((CHUNK, SGU_OUT), F32, (CHUNK, SGU_OUT), lambda i: (0, 0))],
        scratch=[pltpu.VMEM((tm, SGU_OUT), F32), pltpu.VMEM((tm, SGU_OUT), F32)], deps=deps)


def _shift_down(z, n, row):
    return jnp.where(row >= n, pltpu.roll(z, n, 0), 0.0)


def _shift_up(z, n, row, T):
    return jnp.where(row < T - n, pltpu.roll(z, T - n, 0), 0.0)


def conv_fwd(proj, conv_w, *, name):
    T, D3 = proj.shape
    D = D3 // 3
    tn = _tile(D, 256)
    nj = D // tn

    def body(b_ref, c_ref, x_ref, w_ref, o_ref):
        row = lax.broadcasted_iota(jnp.int32, (T, tn), 0)
        z = c_ref[...] * x_ref[...]
        zc = w_ref[2:3, :] * z + w_ref[1:2, :] * _shift_down(z, 1, row) + w_ref[0:1, :] * _shift_down(z, 2, row)
        o_ref[...] = (b_ref[...] * zc).astype(BF)

    return _pcall(
        body, name=name, grid=(nj,),
        ins=[(proj, (T, tn), lambda j: (0, j)), (proj, (T, tn), lambda j: (0, nj + j)),
             (proj, (T, tn), lambda j: (0, 2 * nj + j)), (conv_w, (3, tn), lambda j: (0, j))],
        outs=[((T, D), BF, (T, tn), lambda j: (0, j))], semantics=("parallel",))[0]


def conv_bwd(dg, proj, conv_w, *, name, deps=()):
    T, D3 = proj.shape
    D = D3 // 3
    tn = _tile(D, 256)
    nj = D // tn

    def body(dg_ref, b_ref, c_ref, x_ref, w_ref, dp_ref, dw_ref, dc_scr, dx_scr):
        part = pl.program_id(1)

        @pl.when(part == 0)
        def _():
            row = lax.broadcasted_iota(jnp.int32, (T, tn), 0)
            c, x = c_ref[...], x_ref[...]
            z = c * x
            z1 = _shift_down(z, 1, row)
            z2 = _shift_down(z, 2, row)
            dgv = dg_ref[...]
            zc = w_ref[2:3, :] * z + w_ref[1:2, :] * z1 + w_ref[0:1, :] * z2
            dp_ref[...] = (dgv * zc).astype(BF)
            dzc = dgv * b_ref[...]
            dw_ref[0:1, :] = jnp.sum(dzc * z2, axis=0, keepdims=True)
            dw_ref[1:2, :] = jnp.sum(dzc * z1, axis=0, keepdims=True)
            dw_ref[2:3, :] = jnp.sum(dzc * z, axis=0, keepdims=True)
            dz = (w_ref[2:3, :] * dzc + w_ref[1:2, :] * _shift_up(dzc, 1, row, T)
                  + w_ref[0:1, :] * _shift_up(dzc, 2, row, T))
            dc_scr[...] = (dz * x).astype(BF)
            dx_scr[...] = (dz * c).astype(BF)

        @pl.when(part == 1)
        def _():
            dp_ref[...] = dc_scr[...]

        @pl.when(part == 2)
        def _():
            dp_ref[...] = dx_scr[...]

    return _pcall(
        body, name=name, grid=(nj, 3),
        ins=[(dg, (T, tn), lambda j, p: (0, j)), (proj, (T, tn), lambda j, p: (0, j)),
             (proj, (T, tn), lambda j, p: (0, nj + j)), (proj, (T, tn), lambda j, p: (0, 2 * nj + j)),
             (conv_w, (3, tn), lambda j, p: (0, j))],
        outs=[((T, D3), BF, (T, tn), lambda j, p: (0, p * nj + j)), ((3, D), F32, (3, tn), lambda j, p: (0, j))],
        scratch=[pltpu.VMEM((T, tn), BF), pltpu.VMEM((T, tn), BF)], semantics=("parallel", "arbitrary"), deps=deps)


def loss_bwd(x_parts, gain, target, *, name):
    T, D = target.shape
    tm = _tile(T, ROW_TILE, 8)
    n_x = len(x_parts)

    def body(*refs):
        x_refs = refs[:n_x]
        g_ref, t_ref, dx_ref, dxb_ref, dg_ref, loss_ref = refs[n_x:]
        first = pl.program_id(0) == 0
        xv = jnp.concatenate([r[...] for r in x_refs], axis=-1) if n_x > 1 else x_refs[0][...]
        r = _rstd(xv)
        xh = xv * r
        gain_v = g_ref[...]
        err = xh * gain_v - t_ref[...]
        part = 0.5 * jnp.sum(jnp.mean(err * err, axis=-1, keepdims=True), axis=0, keepdims=True)
        _acc_rows(loss_ref, jnp.broadcast_to(part, (1, LANES)), first)
        dy = err * (1.0 / D)
        gdy = dy * gain_v
        dx = r * (gdy - xh * jnp.mean(gdy * xh, axis=-1, keepdims=True))
        dx_ref[...] = dx
        dxb_ref[...] = dx.astype(BF)
        _acc_rows(dg_ref, dy * xh, first)

    return _pcall(
        body, name=name, grid=(T // tm,),
        ins=[(p, (tm, D // n_x), lambda i: (i, 0)) for p in x_parts]
        + [(gain, (1, D), lambda i: (0, 0)), (target, (tm, D), lambda i: (i, 0))],
        outs=[((T, D), F32, (tm, D), lambda i: (i, 0)), ((T, D), BF, (tm, D), lambda i: (i, 0)),
              ((1, D), F32, (1, D), lambda i: (0, 0)), ((1, LANES), F32, (1, LANES), lambda i: (0, 0))])


def _adamw(g, w, m, v):
    m = ADAM_B1 * m + (1.0 - ADAM_B1) * g
    v = ADAM_B2 * v + (1.0 - ADAM_B2) * (g * g)
    m_hat = m / ADAM_C1
    v_hat = v / ADAM_C2
    delta = -ADAM_LR * (m_hat / (jnp.sqrt(v_hat) + ADAM_EPS) + ADAM_WD * w)
    return delta, m, v


def adam_flat(g, w, m, v, *, name):
    def body(g_ref, w_ref, m_ref, v_ref, d_ref, nm_ref, nv_ref):
        d, nm, nv = _adamw(g_ref[...], w_ref[...], m_ref[...], v_ref[...])
        d_ref[...] = d
        nm_ref[...] = nm
        nv_ref[...] = nv

    blk = g.shape
    zero = lambda: (0, 0)
    return _pcall(body, name=name, grid=(),
                  ins=[(t, blk, zero) for t in (g, w, m, v)],
                  outs=[(blk, F32, blk, zero)] * 3)


def _chip_slots():
    x, y, c = lax.axis_index("x"), lax.axis_index("y"), lax.axis_index("c")
    chips = [(1 - x, y), (x, 1 - y), (1 - x, 1 - y)]
    return x, y, c, chips


def device_index():
    x, y, c, chips = _chip_slots()
    return jnp.stack([4 * x + 2 * y + c, 2 * x + y] + [4 * cx + 2 * cy + c for cx, cy in chips]
                     + [2 * cx + cy for cx, cy in chips]).astype(jnp.int32)


def _job_rows(R, C, n_steps):
    if n_steps is None:
        n_steps = max(1, R * C // STREAM_BLOCK_ELEMS)
    n_blk = max([d for d in range(1, n_steps + 1) if R % d == 0 and (R // d) % 16 == 0] or [1])
    return R // n_blk, n_blk


def run_job(job, *, index, name, deps=()):
    jb = job(None)
    n_in = len(jb["ins"])

    def body(idx_ref, *refs):
        jb["fn"](refs[:n_in], refs[n_in:n_in + len(jb["outs"])])

    return _pcall(body, name=name, grid=(jb["n_blk"],), ins=jb["ins"], outs=jb["outs"], prefetch=index,
                  aliases={1 + a: o for a, o in jb["aliases"].items()}, semantics=("parallel",), deps=deps)


def adam_job(gs, a_buf, b_buf, w, m, v, layer, prev):
    L, R, C = w.shape

    def build(n_steps):
        tr, n_blk = _job_rows(R, C, n_steps)
        blk = (None, tr, C)
        row = lambda t: jnp.minimum(t, n_blk - 1)
        ins = [(gs, blk, lambda t, s: (s[0], row(t), 0)), (a_buf, blk, lambda t, s: (s[1], row(t), 0))]
        ins += [(b_buf, blk, lambda t, s, j=j: (j, row(t), 0)) for j in range(3)]
        ins += [(p, blk, lambda t, s: (layer, row(t), 0)) for p in (w, m, v)]
        ins += [(p, None, None) for p in (prev or [])]

        def fn(i, o):
            g = ((((i[0][...].astype(F32) + i[1][...].astype(F32)) + i[2][...].astype(F32))
                  + i[3][...].astype(F32)) + i[4][...].astype(F32))
            d, nm, nv = _adamw(g, i[5][...], i[6][...], i[7][...])
            o[0][...] = g
            o[1][...] = d
            o[2][...] = nm
            o[3][...] = nv

        return dict(ins=ins, outs=[((L, R, C), F32, blk, lambda t, s: (layer, row(t), 0))] * 4, fn=fn,
                    aliases={8 + o: o for o in range(4)} if prev else {}, n_blk=n_blk)

    return build


def pair_job(gs, a_buf):
    _, R, C = gs.shape

    def build(n_steps):
        tr, n_blk = _job_rows(R, C, n_steps)
        blk = (None, tr, C)
        row = lambda t: jnp.minimum(t, n_blk - 1)
        ins = [(gs, blk, lambda t, s, j=j: (s[2 + j], row(t), 0)) for j in range(3)]
        ins += [(a_buf, blk, lambda t, s, j=j: (s[5 + j], row(t), 0)) for j in range(3)]

        def fn(i, o):
            for j in range(3):
                o[0][j] = (i[j][...].astype(F32) + i[3 + j][...].astype(F32)).astype(BF)

        return dict(ins=ins, outs=[((3, R, C), BF, (3, tr, C), lambda t, s: (0, row(t), 0))], fn=fn, aliases={},
                    n_blk=n_blk)

    return build


def reduce_sum(gs, a_buf, b_buf, *, name):
    _, R, C = gs.shape
    tr = _tile(R, 256, 16)
    x, y, c, _ = _chip_slots()
    idx = jnp.stack([4 * x + 2 * y + c, 2 * x + y]).astype(jnp.int32)

    def body(idx_ref, g_ref, a_ref, b0_ref, b1_ref, b2_ref, o_ref):
        o_ref[...] = ((((g_ref[...].astype(F32) + a_ref[...].astype(F32)) + b0_ref[...].astype(F32))
                       + b1_ref[...].astype(F32)) + b2_ref[...].astype(F32))

    blk3 = (None, tr, C)
    return _pcall(body, name=name, grid=(R // tr,),
                  ins=[(gs, blk3, lambda i, s: (s[0], i, 0)), (a_buf, blk3, lambda i, s: (s[1], i, 0)),
                       (b_buf, blk3, lambda i, s: (0, i, 0)), (b_buf, blk3, lambda i, s: (1, i, 0)),
                       (b_buf, blk3, lambda i, s: (2, i, 0))],
                  outs=[((R, C), F32, (tr, C), lambda i, s: (i, 0))], prefetch=idx, semantics=("parallel",))[0]


def adam_rows(g, w, m, v, *, name):
    R, C = g.shape
    tr = _tile(R, 256, 8)

    def body(g_ref, w_ref, m_ref, v_ref, d_ref, nm_ref, nv_ref):
        d, nm, nv = _adamw(g_ref[...], w_ref[...], m_ref[...], v_ref[...])
        d_ref[...] = d
        nm_ref[...] = nm
        nv_ref[...] = nv

    spec = ((tr, C), lambda i: (i, 0))
    return _pcall(body, name=name, grid=(R // tr,), ins=[(t, *spec) for t in (g, w, m, v)],
                  outs=[((R, C), F32, *spec)] * 3, semantics=("parallel",))


def sum_rows8(gathered, rows, *, name):
    W = gathered.shape[1]

    def body(g_ref, o_ref):
        acc = g_ref[0:rows, :]
        for d in range(1, N_DEV):
            acc = acc + g_ref[d * rows:(d + 1) * rows, :]
        o_ref[...] = acc

    return _pcall(body, name=name, grid=(), ins=[(gathered, gathered.shape, lambda: (0, 0))],
                  outs=[((rows, W), F32, (rows, W), lambda: (0, 0))])[0]


HBM_SPEC = pl.BlockSpec(memory_space=pltpu.HBM)
SEM_SPEC = pl.BlockSpec(memory_space=pltpu.SEMAPHORE)
ANY_SPEC = pl.BlockSpec(memory_space=pl.ANY)
DATAFLOW = pltpu.SideEffectType.DATAFLOW_SIDE_EFFECTING


def _in_hbm(v):
    return pltpu.with_memory_space_constraint(v, pltpu.HBM)


def _slot(p):
    return 4 * p[0] + 2 * p[1] + p[2]


def _gather_peers():
    x, y, c, chips = _chip_slots()
    return (x, y, c), [(x, y, 1 - c)] + [(*chip, c) for chip in chips]


def gather_start(groups, after, *, name):
    flat = [s for g in groups for s in g]
    n, n_g = len(flat), len(groups)
    where = [(gi, ti) for gi, g in enumerate(groups) for ti in range(len(g))]

    def body(*refs):
        src, land = refs[:n], refs[n:2 * n]
        sems = refs[2 * n + 1:2 * n + 1 + 2 * n_g]
        me, peers = _gather_peers()
        for t in range(n):
            gi, ti = where[t]
            for k, to in enumerate(peers):
                pltpu.make_async_remote_copy(
                    src_ref=src[t], dst_ref=land[t].at[_slot(me)], send_sem=sems[2 * gi].at[4 * ti + k],
                    recv_sem=sems[2 * gi + 1].at[4 * ti + k], device_id=to, device_id_type=MESH).start()
        refs[-1][...] = jnp.zeros_like(refs[-1])

    out_shape = []
    for g in groups:
        out_shape += [pltpu.SemaphoreType.DMA((4 * len(g),)), pltpu.SemaphoreType.DMA((4 * len(g),))]
    out_shape += [pltpu.HBM(s.shape, s.dtype) for s in flat]
    out_shape += [pltpu.HBM((N_DEV,) + s.shape, s.dtype) for s in flat]
    out_shape += [jax.ShapeDtypeStruct((8, LANES), F32)]
    aliases = {t: 2 * n_g + t for t in range(n)}
    aliases.update({n + t: 2 * n_g + n + t for t in range(n)})
    res = pl.pallas_call(
        body, name=name, out_shape=out_shape, in_specs=[HBM_SPEC] * (2 * n) + [ANY_SPEC],
        out_specs=[SEM_SPEC] * (2 * n_g) + [HBM_SPEC] * (2 * n) + [pl.BlockSpec(memory_space=pltpu.VMEM)],
        input_output_aliases=aliases, compiler_params=pltpu.CompilerParams(has_side_effects=DATAFLOW),
    )(*[_in_hbm(s) for s in flat], *[_in_hbm(lax.empty((N_DEV,) + s.shape, s.dtype)) for s in flat], after)
    out, off = [], 0
    for gi, g in enumerate(groups):
        k = len(g)
        out.append((res[2 * gi], res[2 * gi + 1], res[2 * n_g + off:2 * n_g + off + k],
                    res[2 * n_g + n + off:2 * n_g + n + off + k]))
        off += k
    return out, res[-1]


def gather_wait(started, after, *, name):
    send_sems, recv_sems, srcs, lands = started
    n = len(srcs)
    after = list(after)

    def body(*refs):
        src, land = refs[:n], refs[n:2 * n]
        send, recv = refs[2 * n], refs[2 * n + 1]
        _, peers = _gather_peers()
        for t in range(n):
            for k, frm in enumerate(peers):
                cp = pltpu.make_async_remote_copy(
                    src_ref=src[t], dst_ref=land[t].at[_slot(frm)], send_sem=send.at[4 * t + k],
                    recv_sem=recv.at[4 * t + k],
                    device_id=frm, device_id_type=MESH)
                cp.wait_send()
                cp.wait_recv()

    res = pl.pallas_call(
        body, name=name,
        out_shape=[pltpu.HBM(s.shape, s.dtype) for s in srcs] + [pltpu.HBM(l.shape, l.dtype) for l in lands],
        in_specs=[HBM_SPEC] * (2 * n) + [SEM_SPEC, SEM_SPEC] + [ANY_SPEC] * len(after),
        out_specs=[HBM_SPEC] * (2 * n), input_output_aliases={t: t for t in range(2 * n)},
        compiler_params=pltpu.CompilerParams(has_side_effects=DATAFLOW),
    )(*srcs, *lands, send_sems, recv_sems, *after)
    return res[:n], res[n:]


def place_own(src, land, *, name):
    R, C = src.shape
    tr = _tile(R, 512, 16)
    x, y, c, _ = _chip_slots()
    idx = jnp.stack([4 * x + 2 * y + c]).astype(jnp.int32)

    def body(idx_ref, s_ref, land_ref, o_ref):
        o_ref[...] = s_ref[...]

    return _pcall(body, name=name, grid=(R // tr,),
                  ins=[(src, (tr, C), lambda i, s: (i, 0)), (land, None, None)],
                  outs=[(land.shape, land.dtype, (None, tr, C), lambda i, s: (s[0], i, 0))],
                  prefetch=idx, aliases={2: 0}, semantics=("parallel",))[0]


def gather_finish(srcs, lands, *, name):
    n = len(srcs)

    def body(*refs):
        land = refs[n:2 * n]
        send_sems, recv_sems = refs[2 * n:]
        x, y, c, chips = _chip_slots()
        me, sibling = (x, y, c), (x, y, 1 - c)

        def copy(t, j, block, to):
            return pltpu.make_async_remote_copy(
                src_ref=land[t].at[_slot(block)], dst_ref=land[t].at[_slot(block)], send_sem=send_sems.at[t, j],
                recv_sem=recv_sems.at[t, j], device_id=to, device_id_type=MESH)

        sends = [copy(t, j, (*chip, c), sibling) for t in range(n) for j, chip in enumerate(chips)]
        for cp in sends:
            cp.start()
        for t in range(n):
            for j, chip in enumerate(chips):
                copy(t, j, (*chip, 1 - c), me).wait_recv()
        for cp in sends:
            cp.wait_send()

    passed = pl.pallas_call(
        body, name=name, out_shape=[jax.ShapeDtypeStruct(l.shape, l.dtype) for l in lands],
        in_specs=[ANY_SPEC] * n, out_specs=[ANY_SPEC] * n,
        input_output_aliases={t: t for t in range(n)},
        scratch_shapes=[pltpu.SemaphoreType.DMA((n, 3)), pltpu.SemaphoreType.DMA((n, 3))],
    )(*lands)
    return [place_own(s, l, name=f"{name}_own{t}") for t, (s, l) in enumerate(zip(srcs, passed))]


def chips_start(pairs, *, name):
    n = len(pairs)

    def body(*refs):
        src, land = refs[:n], refs[n:2 * n]
        send, recv = refs[2 * n], refs[2 * n + 1]
        token = refs[-1]
        x, y, c, chips = _chip_slots()
        for t in range(n):
            for j, chip in enumerate(chips):
                pltpu.make_async_remote_copy(
                    src_ref=src[t].at[j], dst_ref=land[t].at[j], send_sem=send.at[3 * t + j],
                    recv_sem=recv.at[3 * t + j], device_id=(*chip, c), device_id_type=MESH).start()
        token[...] = jnp.zeros_like(token)

    res = pl.pallas_call(
        body, name=name,
        out_shape=[pltpu.SemaphoreType.DMA((3 * n,)), pltpu.SemaphoreType.DMA((3 * n,))]
        + [pltpu.HBM(p.shape, p.dtype) for p in pairs] * 2 + [jax.ShapeDtypeStruct((8, LANES), F32)],
        in_specs=[HBM_SPEC] * (2 * n),
        out_specs=[SEM_SPEC, SEM_SPEC] + [HBM_SPEC] * (2 * n) + [pl.BlockSpec(memory_space=pltpu.VMEM)],
        input_output_aliases={t: 2 + t for t in range(2 * n)},
        compiler_params=pltpu.CompilerParams(has_side_effects=DATAFLOW),
    )(*[_in_hbm(p) for p in pairs], *[_in_hbm(lax.empty(p.shape, p.dtype)) for p in pairs])
    return res[0], res[1], res[2:2 + n], res[2 + n:2 + 2 * n], res[-1]


def chips_wait(started, after, *, name):
    send_sems, recv_sems, srcs, lands, _ = started
    n = len(srcs)

    def body(*refs):
        src, land = refs[:n], refs[n:2 * n]
        send, recv = refs[2 * n], refs[2 * n + 1]
        x, y, c, chips = _chip_slots()
        for t in range(n):
            for j, chip in enumerate(chips):
                cp = pltpu.make_async_remote_copy(
                    src_ref=src[t].at[j], dst_ref=land[t].at[j], send_sem=send.at[3 * t + j],
                    recv_sem=recv.at[3 * t + j], device_id=(*chip, c), device_id_type=MESH)
                cp.wait_send()
                cp.wait_recv()

    res = pl.pallas_call(
        body, name=name, out_shape=[pltpu.HBM(s.shape, s.dtype) for s in srcs] * 2,
        in_specs=[HBM_SPEC] * (2 * n) + [SEM_SPEC, SEM_SPEC, ANY_SPEC], out_specs=[HBM_SPEC] * (2 * n),
        input_output_aliases={t: t for t in range(2 * n)},
        compiler_params=pltpu.CompilerParams(has_side_effects=DATAFLOW),
    )(*srcs, *lands, send_sems, recv_sems, after)
    return res[n:]


def _sibling_copies(src, land, send, recv, n):
    x, y, c, _ = _chip_slots()
    return [pltpu.make_async_remote_copy(
        src_ref=src[t].at[4 * (q // 2) + 2 * (q % 2) + (1 - c)], dst_ref=land[t].at[q], send_sem=send.at[4 * t + q],
        recv_sem=recv.at[4 * t + q], device_id=(x, y, 1 - c), device_id_type=MESH)
        for t in range(n) for q in range(4)]


def sibling_start(gs, *, name):
    n = len(gs)

    def body(*refs):
        for cp in _sibling_copies(refs[:n], refs[n:2 * n], refs[2 * n], refs[2 * n + 1], n):
            cp.start()
        refs[-1][...] = jnp.zeros_like(refs[-1])

    lands = [lax.empty((4,) + g.shape[1:], g.dtype) for g in gs]
    res = pl.pallas_call(
        body, name=name,
        out_shape=[pltpu.SemaphoreType.DMA((4 * n,)), pltpu.SemaphoreType.DMA((4 * n,))]
        + [pltpu.HBM(g.shape, g.dtype) for g in gs] + [pltpu.HBM(l.shape, l.dtype) for l in lands]
        + [jax.ShapeDtypeStruct((8, LANES), F32)],
        in_specs=[HBM_SPEC] * (2 * n),
        out_specs=[SEM_SPEC, SEM_SPEC] + [HBM_SPEC] * (2 * n) + [pl.BlockSpec(memory_space=pltpu.VMEM)],
        input_output_aliases={t: 2 + t for t in range(2 * n)},
        compiler_params=pltpu.CompilerParams(has_side_effects=DATAFLOW),
    )(*[_in_hbm(g) for g in gs], *[_in_hbm(l) for l in lands])
    return res[0], res[1], res[2:2 + n], res[2 + n:2 + 2 * n], res[-1]


def sibling_wait(started, after, *, name):
    send_sems, recv_sems, srcs, lands, _ = started
    n = len(srcs)

    def body(*refs):
        for cp in _sibling_copies(refs[:n], refs[n:2 * n], refs[2 * n], refs[2 * n + 1], n):
            cp.wait_send()
            cp.wait_recv()

    res = pl.pallas_call(
        body, name=name,
        out_shape=[pltpu.HBM(s.shape, s.dtype) for s in srcs] + [pltpu.HBM(l.shape, l.dtype) for l in lands],
        in_specs=[HBM_SPEC] * (2 * n) + [SEM_SPEC, SEM_SPEC, ANY_SPEC], out_specs=[HBM_SPEC] * (2 * n),
        input_output_aliases={t: t for t in range(2 * n)},
        compiler_params=pltpu.CompilerParams(has_side_effects=DATAFLOW),
    )(*srcs, *lands, send_sems, recv_sems, after)
    return res[:n], res[n:]


def _rope_slab(cols):
    z = jnp.zeros(cols.shape[:-1] + (HALF_ROPE,), cols.dtype)
    return jnp.concatenate([cols[..., :HALF_ROPE], z, cols[..., HALF_ROPE:], z], axis=-1)


def _rope_unslab(slab):
    return jnp.concatenate([slab[..., :HALF_ROPE], slab[..., 2 * HALF_ROPE:3 * HALF_ROPE]], axis=-1)


def _pack_w_in_t(wt_g):
    s, c, d = wt_g.shape
    w = wt_g.reshape(s * c, d)
    c2, c3 = Q_LORA + KV_LORA, Q_LORA + KV_LORA + QK_ROPE
    r = w[c2:c3]
    z = jnp.zeros((HALF_ROPE, d), w.dtype)
    return jnp.concatenate([w[:c2], w[c3:], r[:HALF_ROPE], z, r[HALF_ROPE:], z], axis=0)


def unpack_w_in_t_grad(dwt, *, name):
    n_rows, d = dwt.shape
    kr = QK_ROPE
    n_out_rows = n_rows - kr
    blk = n_out_rows // 7
    assert blk * 7 == n_out_rows and blk % kr == 0 and n_rows % (2 * kr) == 0
    kr_row = Q_LORA + KV_LORA
    k_mix = kr_row // blk
    off = kr_row - k_mix * blk
    slab_block = (n_rows - 2 * kr) // (2 * kr)

    def body(prev_ref, in_ref, slab_ref, o_ref):
        k = pl.program_id(0)

        @pl.when(k < k_mix)
        def _():
            o_ref[...] = in_ref[...]

        @pl.when(k == k_mix)
        def _():
            o_ref[:off, :] = in_ref[:off, :]
            o_ref[off:off + HALF_ROPE, :] = slab_ref[:HALF_ROPE, :]
            o_ref[off + HALF_ROPE:off + kr, :] = slab_ref[2 * HALF_ROPE:3 * HALF_ROPE, :]
            o_ref[off + kr:, :] = in_ref[off:blk - kr, :]

        @pl.when(k > k_mix)
        def _():
            o_ref[:kr, :] = prev_ref[blk - kr:, :]
            o_ref[kr:, :] = in_ref[:blk - kr, :]

    out = _pcall(body, name=name, grid=(7,),
                 ins=[(dwt, (blk, d), lambda k: (jnp.maximum(k - 1, 0), 0)), (dwt, (blk, d), lambda k: (k, 0)),
                      (dwt, (2 * kr, d), lambda k: (slab_block, 0))],
                 outs=[((n_out_rows, d), dwt.dtype, (blk, d), lambda k: (k, 0))], semantics=("parallel",))[0]
    return out.reshape(N_DEV, n_out_rows // N_DEV, d)


def _rope_tables(positions):
    inv_freq = ROPE_BASE ** (-jnp.arange(0, QK_ROPE, 2, dtype=F32) / QK_ROPE)
    ang = positions.astype(F32)[:, None] * inv_freq
    cos, sin = jnp.cos(ang), jnp.sin(ang)
    z = jnp.zeros_like(cos)
    return jnp.concatenate([cos, z, cos, z], axis=-1), jnp.concatenate([-sin, z, sin, z], axis=-1)


def _mlp_up(x, gain, w1, tag):
    hn = rms_fwd(x, gain, name=f"mlp{tag}_norm")

    def act_epi(acc):
        a = jnp.maximum(acc, 0.0)
        return a, a * a

    T = x.shape[0]
    F = w1.shape[0] * w1.shape[2]
    a, act = mm(hn, w1, name=f"mlp{tag}_up", outs=[((T, F), BF, None), ((T, F), BF, None)], epi=act_epi)
    return hn, a, act


def _mlp_down(x, act, w2, tag, part=0):
    n = w2.shape[1]
    bm = _tile(x.shape[0], MM_TILE)
    bn = _tile(n, MM_TILE)
    per = n // bn
    return mm(act, w2, name=f"mlp{tag}_down{part}", out=((x.shape[0], n), F32), bm=bm, bn=bn,
              epi=lambda acc, r: (acc + r[...],), epi_ins=[(x, (bm, bn), lambda i, j, k: (i, part * per + j))])


def _mlp_bwd_weights(w1, w2, saved, dxb, tag):
    hn, a, act = saved
    T, D = dxb.shape
    F = a.shape[1]
    bm = _tile(T, MM_TILE)
    bn = _tile(F, min(MM_TILE, w1.shape[2]))
    dhid = mm(dxb, w2, tb=True, name=f"mlp{tag}_dhid", out=((T, F), BF), bm=bm, bn=bn,
              epi=lambda acc, a_ref: (2.0 * a_ref[...].astype(F32) * acc,),
              epi_ins=[(a, (bm, bn), lambda i, j, k: (i, j))])
    dw2 = mm(act, dxb, ta=True, name=f"mlp{tag}_dw2", out=((F, D), BF))
    dw1 = mm(hn, dhid, ta=True, name=f"mlp{tag}_dw1", out=(w1.shape, BF))
    return dhid, dw1, dw2.reshape(N_DEV, F // N_DEV, D)


def _reduce_begin(grads, tag):
    return sibling_start(grads, name=f"reduce_sibling_start_{tag}")


def _reduce_continue(sib, after, tag, index):
    grads, a_bufs = sibling_wait(sib, after, name=f"reduce_sibling_wait_{tag}")
    pairs = [run_job(pair_job(g, a), index=index, name=f"pair_sum_{tag}{t}")[0]
             for t, (g, a) in enumerate(zip(grads, a_bufs))]
    return grads, a_bufs, chips_start(pairs, name=f"reduce_chips_start_{tag}")


def kernel(x, positions, e_norm_mix, e_w_in, e_q_norm, e_w_uq, e_kv_norm, e_w_ukv, e_v_norm, e_sgu_w, e_sgu_b, e_mla_out_norm, e_sgu_out_norm, e_w_out, o_norm_mix, o_w_in, o_conv_w, o_w_out, mlp_norm, mlp_w1, mlp_w2, final_norm, loss_target, m_e_norm_mix, m_e_w_in, m_e_q_norm, m_e_w_uq, m_e_kv_norm, m_e_w_ukv, m_e_v_norm, m_e_sgu_w, m_e_sgu_b, m_e_mla_out_norm, m_e_sgu_out_norm, m_e_w_out, m_o_norm_mix, m_o_w_in, m_o_conv_w, m_o_w_out, m_mlp_norm, m_mlp_w1, m_mlp_w2, m_final_norm, v_e_norm_mix, v_e_w_in, v_e_q_norm, v_e_w_uq, v_e_kv_norm, v_e_w_ukv, v_e_v_norm, v_e_sgu_w, v_e_sgu_b, v_e_mla_out_norm, v_e_sgu_out_norm, v_e_w_out, v_o_norm_mix, v_o_w_in, v_o_conv_w, v_o_w_out, v_mlp_norm, v_mlp_w1, v_mlp_w2, v_final_norm):
    T, D = x.shape[1], x.shape[2]
    d_shard = o_norm_mix.shape[1]
    x0 = x[0]
    target = loss_target[0]
    me = 4 * lax.axis_index("x") + 2 * lax.axis_index("y") + lax.axis_index("c")

    bf = lambda s: s.astype(BF)
    gather_groups = [[bf(jnp.transpose(e_w_in[0])), bf(e_w_uq[0]), bf(e_w_ukv[0])], [bf(e_w_out[0]), bf(mlp_w1[0])],
                     [bf(mlp_w2[0]), bf(o_w_in[0])], [bf(o_w_out[0]), bf(mlp_w1[1])], [bf(mlp_w2[1])]]
    small_rows = jnp.concatenate([o_norm_mix, o_conv_w[0], jnp.zeros((4, d_shard), F32)], axis=0)
    gather_groups[0].insert(0, small_rows)
    started, start_token = gather_start(gather_groups[:1], x0, name="gather_start0")
    started_rest, rest_token = gather_start(gather_groups[1:], start_token, name="gather_start1")
    started += started_rest

    def gathered(gi, after):
        srcs, lands = gather_wait(started[gi], after, name=f"gather_wait{gi}")
        return gather_finish(srcs, lands, name=f"gather_finish{gi}")

    w_tril = jnp.tril(e_sgu_w[0])
    w_tril_b = w_tril.astype(BF)
    w_tril_tb = jnp.swapaxes(w_tril, 1, 2).astype(BF)
    b_full = jnp.repeat(e_sgu_b[0].T, CH, axis=1)
    v_gain = e_v_norm[0].reshape(1, SGU_OUT)
    cos_t, sin_t = _rope_tables(positions[0])
    mlp_gain = [mlp_norm[0:1], mlp_norm[1:2]]
    final_gain = final_norm.reshape(1, D)

    h0 = rms_fwd(x0, e_norm_mix, name="e_norm", deps=[rest_token])
    small_g, g_w_in_t, g_w_uq, w_ukv = gathered(0, [h0, cos_t, sin_t, w_tril_b, w_tril_tb, b_full])
    o_norm_full = small_g[:, 0, :].reshape(1, D)
    conv_w_full = jnp.transpose(small_g[:, 1:4, :], (1, 0, 2)).reshape(3, D)
    w_in_t = _pack_w_in_t(g_w_in_t)
    w_uq = jnp.concatenate([g_w_uq[..., :QK_NOPE], _rope_slab(g_w_uq[..., QK_NOPE:])], axis=-1)
    proj = mm(h0, w_in_t, tb=True, name="e_in", out=((T, w_in_t.shape[0]), F32), bn=_tile(w_in_t.shape[0], 640))
    qn, kvn, krope = mla_prep(proj, e_q_norm, e_kv_norm, cos_t, sin_t, name="mla_prep")
    bm = _tile(T, MM_TILE)

    def q_epi(acc, cos_ref, sin_ref):
        return (jnp.concatenate([acc[:, :QK_NOPE], _rope_fwd(acc[:, QK_NOPE:], cos_ref[...], sin_ref[...])], axis=-1),)

    q = mm(qn, w_uq, name="mla_q", out=((T, HEADS * HEAD_PAD), BF), bm=bm, bn=HEAD_PAD, epi=q_epi,
           epi_ins=[(cos_t, (bm, LANES), lambda i, j, k: (i, 0)), (sin_t, (bm, LANES), lambda i, j, k: (i, 0))])

    def kv_epi(acc, kr_ref):
        return jnp.concatenate([acc[:, :QK_NOPE].astype(BF), kr_ref[...]], axis=-1), acc[:, QK_NOPE:]

    k, v = mm(kvn, w_ukv, name="mla_kv", bm=bm, bn=HEAD_PAD, epi=kv_epi,
              outs=[((T, HEADS * HEAD_PAD), BF, HEAD_PAD), ((T, MLA_OUT), BF, V_HEAD)],
              epi_ins=[(krope, (bm, LANES), lambda i, j, k: (i, 0))])
    attn, attn_lse = attn_fwd(q, k, v, name="attn_fwd")
    mixed = mix_fwd(attn, proj, e_mla_out_norm, e_sgu_out_norm, v_gain, w_tril_b, b_full, name="mix_fwd")
    bn = _tile(D, MM_TILE)
    g_w_out_e, w1_0 = gathered(1, [mixed])
    w_out_e = g_w_out_e.reshape(-1, D)
    x1 = mm(mixed, w_out_e, name="e_out", out=((T, D), F32), bm=bm, bn=bn,
            epi=lambda acc, r: (acc + r[...],), epi_ins=[(x0, (bm, bn), lambda i, j, k: (i, j))])
    hn0, a0, act0 = _mlp_up(x1, mlp_gain[0], w1_0, 0)
    g_w2_0, g_w_in_o = gathered(2, [act0])
    w2_0 = g_w2_0.reshape(-1, D)
    x2 = _mlp_down(x1, act0, w2_0, 0)
    ho = rms_fwd(x2, o_norm_full, name="o_norm")
    proj_o = mm(ho, g_w_in_o, name="o_in", out=((T, 3 * D), F32))
    gated = conv_fwd(proj_o, conv_w_full, name="conv_fwd")
    g_w_out_o, w1_1 = gathered(3, [gated])
    w_out_o = g_w_out_o.reshape(-1, D)
    x3 = mm(gated, w_out_o, name="o_out", out=((T, D), F32), bm=bm, bn=bn,
            epi=lambda acc, r: (acc + r[...],), epi_ins=[(x2, (bm, bn), lambda i, j, k: (i, j))])
    hn1, a1, act1 = _mlp_up(x3, mlp_gain[1], w1_1, 1)
    (g_w2_1,) = gathered(4, [act1])
    w2_1 = g_w2_1.reshape(-1, D)
    x4 = _mlp_down(x3, act1, w2_1, 1)
    w1, w2 = [w1_0, w1_1], [w2_0, w2_1]

    dx4, dx4b, d_final, loss_part = loss_bwd([x4], final_gain, target, name="loss_bwd")

    hosted = dict(job_index=device_index())
    dhid1, dw1_1, dw2_1 = _mlp_bwd_weights(w1[1], w2[1], (hn1, a1, act1), dx4b, 1)
    sib_r0 = _reduce_begin([dw1_1, dw2_1], "r0")
    dhn1 = mm(dhid1, w1[1], tb=True, name="mlp1_dhn", out=((T, D), F32), deps=[sib_r0[-1]])
    grads_r0, a_r0 = sibling_wait(sib_r0, dhn1, name="reduce_sibling_wait_r0")
    dx3, dx3b, d_mlp1 = rms_bwd(x3, mlp_gain[1], dhn1, dres=dx4, name="mlp1_norm_bwd")

    dgated, ((pair_r0a,),) = mm(dx3b, w_out_o, tb=True, name="o_out_dx", out=((T, D), F32),
                                jobs=[pair_job(grads_r0[0], a_r0[0])], **hosted)
    dw_out_o, ((pair_r0b,),) = mm(gated, dx3b, ta=True, name="o_out_dw", out=((D, D), BF),
                                  jobs=[pair_job(grads_r0[1], a_r0[1])], **hosted)
    st_r0 = chips_start([pair_r0a, pair_r0b], name="reduce_chips_start_r0")
    dproj_o, dconv_full = conv_bwd(dgated, proj_o, conv_w_full, name="conv_bwd", deps=[st_r0[-1]])
    dw_in_o = mm(ho, dproj_o, ta=True, name="o_in_dw", out=(g_w_in_o.shape, BF))
    sib_r1 = _reduce_begin([dw_out_o.reshape(g_w_out_o.shape), dw_in_o], "r1")
    dho = mm(dproj_o, g_w_in_o, tb=True, name="o_in_dx", out=((T, D), F32), deps=[sib_r1[-1]])
    grads_r1, a_r1 = sibling_wait(sib_r1, dho, name="reduce_sibling_wait_r1")
    dx2, dx2b, d_onorm_full = rms_bwd(x2, o_norm_full, dho, dres=dx3, name="o_norm_bwd")

    d_ff = a0.shape[1]
    bm_h, bn_h = _tile(T, MM_TILE), _tile(d_ff, min(MM_TILE, w1[0].shape[2]))
    dhid0, ((pair_r1a,), (pair_r1b,)) = mm(
        dx2b, w2[0], tb=True, name="mlp0_dhid", out=((T, d_ff), BF), bm=bm_h, bn=bn_h,
        epi=lambda acc, a_ref: (2.0 * a_ref[...].astype(F32) * acc,),
        epi_ins=[(a0, (bm_h, bn_h), lambda i, j, k: (i, j))],
        jobs=[pair_job(grads_r1[0], a_r1[0]), pair_job(grads_r1[1], a_r1[1])], **hosted)
    st_r1 = chips_start([pair_r1a, pair_r1b], name="reduce_chips_start_r1")
    dw2_0 = mm(act0, dx2b, ta=True, name="mlp0_dw2", out=((d_ff, D), BF), deps=[st_r1[-1]])
    b_r0 = chips_wait(st_r0, dw2_0, name="reduce_chips_wait_r0")
    dw1_0, (r_w1, r_w2) = mm(
        hn0, dhid0, ta=True, name="mlp0_dw1", out=(w1[0].shape, BF),
        jobs=[adam_job(grads_r0[0], a_r0[0], b_r0[0], mlp_w1, m_mlp_w1, v_mlp_w1, 1, None),
              adam_job(grads_r0[1], a_r0[1], b_r0[1], mlp_w2, m_mlp_w2, v_mlp_w2, 1, None)], **hosted)
    sib_r2 = _reduce_begin([dw1_0, dw2_0.reshape(N_DEV, d_ff // N_DEV, D)], "r2")
    dhn0 = mm(dhid0, w1[0], tb=True, name="mlp0_dhn", out=((T, D), F32), deps=[sib_r2[-1]])
    grads_r2, a_r2 = sibling_wait(sib_r2, dhn0, name="reduce_sibling_wait_r2")
    dx1, dx1b, d_mlp0 = rms_bwd(x1, mlp_gain[0], dhn0, dres=dx2, name="mlp0_norm_bwd")

    dmixed, ((pair_r2a,),) = mm(dx1b, w_out_e, tb=True, name="e_out_dx", out=((T, MLA_OUT + SGU_OUT), F32),
                                jobs=[pair_job(grads_r2[0], a_r2[0])], **hosted)
    dw_out_e, ((pair_r2b,),) = mm(mixed, dx1b, ta=True, name="e_out_dw", out=(w_out_e.shape, BF),
                                  jobs=[pair_job(grads_r2[1], a_r2[1])], **hosted)
    st_r2 = chips_start([pair_r2a, pair_r2b], name="reduce_chips_start_r2")
    (dattn, dproj, d_mla_out, d_sgu_out, d_vgain, d_sgu_w, d_b_full) = mix_bwd(
        dmixed, attn, proj, e_mla_out_norm, e_sgu_out_norm, v_gain, w_tril_b, w_tril_tb, b_full, name="mix_bwd",
        deps=[st_r2[-1]])
    b_r1 = chips_wait(st_r1, dattn, name="reduce_chips_wait_r1")
    dq, dk, dv = attn_bwd(q, k, v, attn, attn_lse, dattn, name="attn_bwd")
    dq_lin, dkv_lin, dproj = mla_bwd_prep(dq, dk, dv, cos_t, sin_t, dproj, name="mla_bwd_prep")
    dw_uq_pad = mm(qn, dq_lin, ta=True, name="mla_q_dw", out=(w_uq.shape, BF))
    dw_ukv = mm(kvn, dkv_lin, ta=True, name="mla_kv_dw", out=(w_ukv.shape, BF))
    dw_uq = jnp.concatenate([dw_uq_pad[..., :QK_NOPE], _rope_unslab(dw_uq_pad[..., QK_NOPE:])], axis=-1)
    sib_r2b = _reduce_begin([dw_out_e.reshape(g_w_out_e.shape), dw_uq, dw_ukv], "r2b")
    dqn = mm(dq_lin, w_uq, tb=True, name="mla_q_dx", out=((T, Q_LORA), F32), deps=[sib_r2b[-1]])
    dkvn = mm(dkv_lin, w_ukv, tb=True, name="mla_kv_dx", out=((T, KV_LORA), F32), deps=[sib_r2b[-1]])
    grads_r2b, a_r2b, st_r2b = _reduce_continue(sib_r2b, dkvn, "r2b", hosted["job_index"])
    dproj, d_qnorm = rms_bwd(proj, e_q_norm, dqn, col_block=0, want_f32=False, into=dproj, name="q_norm_bwd",
                             deps=[st_r2b[-1]])
    dproj, d_kvnorm = rms_bwd(proj, e_kv_norm, dkvn, col_block=1, want_f32=False, into=dproj, name="kv_norm_bwd")
    dw_in_t_pad, (r_w_out_o, r_w_in_o) = mm(
        dproj, h0, ta=True, name="e_in_dw", out=(w_in_t.shape, BF), bm=_tile(w_in_t.shape[0], 640),
        jobs=[adam_job(grads_r1[0], a_r1[0], b_r1[0], o_w_out, m_o_w_out, v_o_w_out, 0, None),
              adam_job(grads_r1[1], a_r1[1], b_r1[1], o_w_in, m_o_w_in, v_o_w_in, 0, None)], **hosted)
    dw_in_t = unpack_w_in_t_grad(dw_in_t_pad, name="e_in_dw_unpack")
    sib_r3 = _reduce_begin([dw_in_t], "r3")
    dh0 = mm(dproj, w_in_t, name="e_in_dx", out=((T, D), F32), deps=[sib_r3[-1]])
    grads_r3, a_r3, st_r3 = _reduce_continue(sib_r3, dh0, "r3", hosted["job_index"])
    tok_r3 = st_r3[-1]
    grad_x, d_enorm = rms_bwd(x0, e_norm_mix, dh0, dres=dx1, want_bf=False, name="e_norm_bwd", deps=[tok_r3])
    b_r2 = chips_wait(st_r2, grad_x, name="reduce_chips_wait_r2")

    d_sgu_b = jnp.transpose(d_b_full[:, ::CH])
    d_sgu_w_tril = jnp.tril(d_sgu_w)
    rep = [("e_norm_mix", e_norm_mix, m_e_norm_mix, v_e_norm_mix, d_enorm),
           ("e_q_norm", e_q_norm, m_e_q_norm, v_e_q_norm, d_qnorm),
           ("e_kv_norm", e_kv_norm, m_e_kv_norm, v_e_kv_norm, d_kvnorm),
           ("e_v_norm", e_v_norm, m_e_v_norm, v_e_v_norm, d_vgain),
           ("e_sgu_w", e_sgu_w, m_e_sgu_w, v_e_sgu_w, d_sgu_w_tril),
           ("e_sgu_b", e_sgu_b, m_e_sgu_b, v_e_sgu_b, d_sgu_b),
           ("e_mla_out_norm", e_mla_out_norm, m_e_mla_out_norm, v_e_mla_out_norm, d_mla_out),
           ("e_sgu_out_norm", e_sgu_out_norm, m_e_sgu_out_norm, v_e_sgu_out_norm, d_sgu_out),
           ("mlp_norm", mlp_norm, m_mlp_norm, v_mlp_norm, jnp.concatenate([d_mlp0, d_mlp1], axis=0)),
           ("final_norm", final_norm, m_final_norm, v_final_norm, d_final)]
    sizes = [int(np.prod(r[1].shape)) for r in rep]
    n_rep = sum(sizes)
    n_all = n_rep + 4 * D + 1
    width = -(-n_all // (8 * LANES)) * LANES
    pad = 8 * width - n_all
    flat = jnp.concatenate([r[4].reshape(-1) for r in rep]
                           + [d_onorm_full.reshape(-1), dconv_full.reshape(-1), loss_part[0, :1],
                              jnp.zeros((pad,), F32)])
    small_started, small_token = gather_start([[flat.reshape(8, width)]], b_r2[0], name="gather_small_grads_start")

    def finish(grads, a_bufs, b_bufs, t, w, m, v, layer=0, prev=None, tag="", deps=()):
        return run_job(adam_job(grads[t], a_bufs[t], b_bufs[t], w, m, v, layer, prev), index=hosted["job_index"],
                       name=f"adam_{tag}", deps=deps)

    r_w1 = finish(grads_r2, a_r2, b_r2, 0, mlp_w1, m_mlp_w1, v_mlp_w1, 0, r_w1, tag="w1_l0", deps=[tok_r3, small_token])
    r_w2 = finish(grads_r2, a_r2, b_r2, 1, mlp_w2, m_mlp_w2, v_mlp_w2, 0, r_w2, tag="w2_l0", deps=[r_w1[1]])
    b_r2b = chips_wait(st_r2b, r_w2[1], name="reduce_chips_wait_r2b")
    r_w_out_e = finish(grads_r2b, a_r2b, b_r2b, 0, e_w_out, m_e_w_out, v_e_w_out, tag="e_w_out")
    r_w_uq = finish(grads_r2b, a_r2b, b_r2b, 1, e_w_uq, m_e_w_uq, v_e_w_uq, tag="e_w_uq")
    r_w_ukv = finish(grads_r2b, a_r2b, b_r2b, 2, e_w_ukv, m_e_w_ukv, v_e_w_ukv, tag="e_w_ukv")
    b_r3 = chips_wait(st_r3, r_w_out_e[1], name="reduce_chips_wait_r3")
    g_w_in_t = reduce_sum(grads_r3[0], a_r3[0], b_r3[0], name="sum_e_w_in")
    w_in_upd_t = adam_rows(g_w_in_t, jnp.transpose(e_w_in[0]), jnp.transpose(m_e_w_in[0]), jnp.transpose(v_e_w_in[0]),
                           name="adam_e_w_in")
    r_w_in = [jnp.transpose(t)[None] for t in (g_w_in_t, *w_in_upd_t)]

    small_srcs, small_lands = gather_wait(small_started[0], [r_w2[1]], name="gather_small_grads_wait")
    small_all = gather_finish(small_srcs, small_lands, name="gather_small_grads_finish")[0]
    summed = sum_rows8(small_all.reshape(N_DEV * 8, width), 8, name="sum_small_grads").reshape(-1)

    loss = summed[n_rep + 4 * D]

    def pack_rep(i):
        return jnp.concatenate([r[i].reshape(-1) for r in rep]).reshape(n_rep // LANES, LANES)

    g_rep = summed[:n_rep].reshape(n_rep // LANES, LANES)
    d_rep, nm_rep, nv_rep = adam_flat(g_rep, pack_rep(1), pack_rep(2), pack_rep(3), name="adam_replicated")

    def unpack_rep(flat2d):
        out, off = {}, 0
        f = flat2d.reshape(-1)
        for r, n in zip(rep, sizes):
            out[r[0]] = f[off:off + n].reshape(r[1].shape)
            off += n
        return out

    small = {"grad": unpack_rep(g_rep), "delta": unpack_rep(d_rep), "new_m": unpack_rep(nm_rep),
             "new_v": unpack_rep(nv_rep)}
    g_onorm = lax.dynamic_slice(summed[n_rep:n_rep + D].reshape(1, D), (0, me * d_shard), (1, d_shard))
    g_conv = lax.dynamic_slice(summed[n_rep + D:n_rep + 4 * D].reshape(3, D), (0, me * d_shard), (3, d_shard))

    def pack_sharded(norm_part, conv_part):
        return jnp.concatenate([norm_part, conv_part, jnp.zeros((4, d_shard), F32)], axis=0)

    g_sh = pack_sharded(g_onorm, g_conv)
    d_sh, nm_sh, nv_sh = adam_flat(g_sh, pack_sharded(o_norm_mix, o_conv_w[0]), pack_sharded(m_o_norm_mix, m_o_conv_w[0]),
                                   pack_sharded(v_o_norm_mix, v_o_conv_w[0]), name="adam_sharded_small")
    for kind, arr in (("grad", g_sh), ("delta", d_sh), ("new_m", nm_sh), ("new_v", nv_sh)):
        small[kind]["o_norm_mix"] = arr[0:1]
        small[kind]["o_conv_w"] = arr[1:4][None]

    big = {"e_w_in": r_w_in, "e_w_uq": r_w_uq, "e_w_ukv": r_w_ukv, "e_w_out": r_w_out_e, "o_w_in": r_w_in_o,
           "o_w_out": r_w_out_o, "mlp_w1": r_w1, "mlp_w2": r_w2}
    order = ["e_norm_mix", "e_w_in", "e_q_norm", "e_w_uq", "e_kv_norm", "e_w_ukv", "e_v_norm", "e_sgu_w", "e_sgu_b",
             "e_mla_out_norm", "e_sgu_out_norm", "e_w_out", "o_norm_mix", "o_w_in", "o_conv_w", "o_w_out", "mlp_norm",
             "mlp_w1", "mlp_w2", "final_norm"]
    result = [loss, grad_x[None]]
    for ki, kind in enumerate(("grad", "delta", "new_m", "new_v")):
        for nm in order:
            result.append(big[nm][ki] if nm in big else small[kind][nm])
    return tuple(result)
```

```python
import numpy as np
import jax
import jax.numpy as jnp
from jax import lax
from jax.experimental import pallas as pl
from jax.experimental.pallas import tpu as pltpu

BF = jnp.bfloat16
F32 = jnp.float32
MESH = pl.DeviceIdType.MESH
N_DEV = 8

EPS = 1e-6
HEADS = 8
Q_LORA = 512
KV_LORA = 512
QK_NOPE = 128
QK_ROPE = 64
HALF_ROPE = QK_ROPE // 2
V_HEAD = 128
HEAD_PAD = 256
ROPE_BASE = 10000.0
GROUPS = 8
CH = 128
CHUNK = 128
SGU_OUT = GROUPS * CH
MLA_OUT = HEADS * V_HEAD
ATTN_SCALE = float((QK_NOPE + QK_ROPE) ** -0.5)

ADAM_LR = 0.001
ADAM_B1 = 0.9
ADAM_B2 = 0.999
ADAM_EPS = 1e-08
ADAM_WD = 0.01
ADAM_STEP = 10
ADAM_C1 = 1.0 - ADAM_B1 ** ADAM_STEP
ADAM_C2 = 1.0 - ADAM_B2 ** ADAM_STEP

V7X_VMEM_BYTES = 64 * 2 ** 20
VMEM_LIMIT_CAP = V7X_VMEM_BYTES - 6 * 2 ** 20
LANES = 128
ROW_TILE = 256
ATTN_TILE = 1024
STREAM_BLOCK_ELEMS = 512 * 1024
MM_TILE = 1024
MM_K_TILE = 2048
MM_K_BLOCK_MAX = 3072


def _padded_bytes(block, dtype):
    dims = [d for d in block if d is not None]
    if len(dims) >= 1:
        dims[-1] = -(-dims[-1] // LANES) * LANES
    if len(dims) >= 2:
        dims[-2] = -(-dims[-2] // 16) * 16
    return int(np.prod(dims)) * jnp.dtype(dtype).itemsize


def _pcall(body, *, name, grid, ins, outs, scratch=(), semantics=None, aliases=None, prefetch=None, deps=()):
    any_spec = pl.BlockSpec(memory_space=pl.ANY)
    if deps:
        n_lead = len(ins) + (1 if prefetch is not None else 0)
        n_deps = len(deps)
        inner = body

        def body(*refs):
            inner(*refs[:n_lead], *refs[n_lead + n_deps:])

        ins = list(ins) + [(d, None, None) for d in deps]
    in_specs = [any_spec if b is None else pl.BlockSpec(b, m) for _, b, m in ins]
    out_specs = [any_spec if b is None else pl.BlockSpec(b, m) for _, _, b, m in outs]
    out_shape = [pltpu.HBM(s, d) for s, d, _, _ in outs]
    est = 0
    for a, b, _ in ins:
        if b is not None:
            est += 2 * _padded_bytes(b, a.dtype)
    for _, d, b, _ in outs:
        if b is not None:
            est += 2 * _padded_bytes(b, d)
    for s in scratch:
        if hasattr(s, "shape") and hasattr(s, "dtype"):
            est += _padded_bytes(s.shape, s.dtype)
    limit = int(min(VMEM_LIMIT_CAP, est + 16 * 2 ** 20))
    params = pltpu.CompilerParams(
        dimension_semantics=semantics or ("arbitrary",) * len(grid), vmem_limit_bytes=limit)
    args = [pltpu.with_memory_space_constraint(a, pltpu.HBM) for a, _, _ in ins]
    if prefetch is not None:
        grid_spec = pltpu.PrefetchScalarGridSpec(
            num_scalar_prefetch=1, grid=grid, in_specs=in_specs, out_specs=out_specs, scratch_shapes=list(scratch))
        call = pl.pallas_call(body, out_shape=out_shape, grid_spec=grid_spec, name=name, compiler_params=params,
                              input_output_aliases=aliases or {})
        return call(prefetch, *args)
    call = pl.pallas_call(body, out_shape=out_shape, grid=grid, in_specs=in_specs, out_specs=out_specs,
                          scratch_shapes=list(scratch), name=name, compiler_params=params,
                          input_output_aliases=aliases or {})
    return call(*args)


def _tile(dim, pref, quantum=LANES):
    if dim <= pref:
        return dim
    t = (pref // quantum) * quantum
    while t >= quantum:
        if dim % t == 0:
            return t
        t -= quantum
    return dim


def _vshape(arr_shape):
    if len(arr_shape) == 2:
        return tuple(arr_shape)
    s, r, c = arr_shape
    return (r, s * c)


def _vblock(arr_shape, br, bc, rc):
    if len(arr_shape) == 2:
        return (br, bc), (lambda *g: rc(*g))
    _, _, c = arr_shape
    assert c % bc == 0, (arr_shape, bc)
    per = c // bc

    def imap(*g):
        ri, ci = rc(*g)
        return (ci // per, ri, ci % per)

    return (None, br, bc), imap


def _shard_width(*shapes):
    w = None
    for s in shapes:
        if len(s) == 3:
            w = s[2] if w is None else int(np.gcd(w, s[2]))
    return w


def mm(a, b, *, name, ta=False, tb=False, out=None, outs=None, epi=None, epi_ins=(), bm=None, bn=None, bk=None,
       deps=(), jobs=(), job_index=None):
    av, bv = _vshape(a.shape), _vshape(b.shape)
    M, K = (av[1], av[0]) if ta else av
    K2, N = (bv[1], bv[0]) if tb else bv
    assert K == K2, (a.shape, b.shape, ta, tb)
    if outs is None:
        outs = [(out[0], out[1], None)]
    a_sw = _shard_width(a.shape)
    b_sw = _shard_width(b.shape)
    o_sw = _shard_width(*[o[0] for o in outs])
    m_lim = a_sw if (ta and a_sw) else None
    k_lim = [w for w in ((a_sw if not ta else None), (b_sw if tb else None)) if w]
    n_lim = [w for w in ((b_sw if not tb else None), o_sw) if w]
    if bm is None:
        bm = _tile(M, min([MM_TILE] + ([m_lim] if m_lim else [])))
    if bn is None:
        bn = _tile(N, min([MM_TILE] + n_lim))
    k_shards = 0
    if tb and len(b.shape) == 3 and bk is None and not (a_sw and not ta):
        k_shards = 1
        while 2 * k_shards <= b.shape[0] and 2 * k_shards * b_sw <= MM_K_BLOCK_MAX:
            k_shards *= 2
        bk = k_shards * b_sw
    if bk is None:
        bk = K if (K <= 4096 and not k_lim) else _tile(K, min([MM_K_TILE] + k_lim))
    assert M % bm == 0 and N % bn == 0 and K % bk == 0, (name, M, N, K, bm, bn, bk)
    nk = K // bk
    grid = (M // bm, N // bn, nk)
    if ta:
        a_blk, a_map = _vblock(a.shape, bk, bm, lambda i, j, k: (k, i))
    else:
        a_blk, a_map = _vblock(a.shape, bm, bk, lambda i, j, k: (i, k))
    if k_shards:
        b_blk, b_map = (k_shards, bn, b_sw), (lambda i, j, k: (k, j, 0))
    elif tb:
        b_blk, b_map = _vblock(b.shape, bn, bk, lambda i, j, k: (j, k))
    else:
        b_blk, b_map = _vblock(b.shape, bk, bn, lambda i, j, k: (k, j))
    dn = (((0 if ta else 1,), (1 if tb else 0,)), ((), ()))
    ins = [(a, a_blk, a_map), (b, b_blk, b_map)] + list(epi_ins)
    out_list = []
    for shape, dtype, cols in outs:
        cols = cols or bn
        blk, imap = _vblock(shape, bm, cols, lambda i, j, k: (i, j))
        out_list.append((shape, dtype, blk, imap))
    n_e, n_o = len(epi_ins), len(out_list)

    n_steps = grid[0] * grid[1] * nk
    built = [job(n_steps) for job in jobs]
    aliases = {}
    job_slices = []
    if built:
        def lin(i, j, k):
            return (i * grid[1] + j) * nk + k

        ins = [(arr, blk, None if blk is None else (lambda i, j, k, s, f=f: f(i, j, k))) for arr, blk, f in ins]
        out_list = [(sh, dt, blk, (lambda i, j, k, s, f=f: f(i, j, k))) for sh, dt, blk, f in out_list]
        n_main_in, n_main_out = len(ins), len(out_list)
        for jb in built:
            i0, o0 = len(ins), len(out_list)
            ins += [(arr, blk, None if blk is None else (lambda i, j, k, s, f=f: f(lin(i, j, k), s)))
                    for arr, blk, f in jb["ins"]]
            out_list += [(sh, dt, blk, (lambda i, j, k, s, f=f: f(lin(i, j, k), s))) for sh, dt, blk, f in jb["outs"]]
            aliases.update({1 + i0 + ai: o0 + ao for ai, ao in jb["aliases"].items()})
            job_slices.append((i0, len(jb["ins"]), o0, len(jb["outs"])))
    n_in_total = len(ins)

    def body(*refs):
        if built:
            refs = refs[1:]
        a_ref, b_ref = refs[0], refs[1]
        e_refs = refs[2:2 + n_e]
        o_refs = refs[n_in_total:n_in_total + n_o]
        for jb, (i0, ni, o0, no) in zip(built, job_slices):
            jb["fn"](refs[i0:i0 + ni], refs[n_in_total + o0:n_in_total + o0 + no])

        def finish(acc):
            res = epi(acc, *e_refs) if epi is not None else (acc,)
            for o_ref, r in zip(o_refs, res):
                o_ref[...] = r.astype(o_ref.dtype)

        x = a_ref[...].astype(BF)
        y = b_ref[...].astype(BF)
        if k_shards:
            p = None
            for s in range(k_shards):
                part = lax.dot_general(x[:, s * b_sw:(s + 1) * b_sw], y[s], dn, preferred_element_type=F32)
                p = part if p is None else p + part
        else:
            p = lax.dot_general(x, y, dn, preferred_element_type=F32)
        if nk == 1:
            finish(p)
        else:
            acc_ref = refs[-1]
            k = pl.program_id(2)

            @pl.when(k == 0)
            def _():
                acc_ref[...] = p

            @pl.when(k > 0)
            def _():
                acc_ref[...] += p

            @pl.when(k == nk - 1)
            def _():
                finish(acc_ref[...])

    scratch = [pltpu.VMEM((bm, bn), F32)] if nk > 1 else []
    res = _pcall(body, name=name, grid=grid, ins=ins, outs=out_list, scratch=scratch, deps=deps,
                 semantics=("parallel", "parallel", "arbitrary"), prefetch=job_index if built else None, aliases=aliases)
    main = res[0] if n_o == 1 else res[:n_o]
    if not built:
        return main
    return main, [res[o0:o0 + no] for _, _, o0, no in job_slices]


_GELU_K = float(np.sqrt(2.0 / np.pi))
_GELU_C = 0.044715


def _gelu(x):
    t = jnp.tanh(_GELU_K * (x + _GELU_C * (x * x * x)))
    return 0.5 * x * (1.0 + t)


def _gelu_grad(x):
    t = jnp.tanh(_GELU_K * (x + _GELU_C * (x * x * x)))
    return 0.5 * (1.0 + t) + 0.5 * x * (1.0 - t * t) * (_GELU_K * (1.0 + 3.0 * _GELU_C * (x * x)))


def _rstd(x):
    return lax.rsqrt(jnp.mean(x * x, axis=-1, keepdims=True) + EPS)


def _rms_bwd(x, gain, dy):
    r = _rstd(x)
    xh = x * r
    gdy = dy * gain
    dx = r * (gdy - xh * jnp.mean(gdy * xh, axis=-1, keepdims=True))
    return dx, dy * xh


def _rope_fwd(x, cos_t, sin_t):
    return x * cos_t + pltpu.roll(x, 2 * HALF_ROPE, 1) * sin_t


def _rope_bwd(dy, cos_t, sin_t):
    return dy * cos_t + pltpu.roll(dy * sin_t, 2 * HALF_ROPE, 1)


def _acc_rows(ref, val, first):
    s = jnp.sum(val, axis=0, keepdims=True)

    @pl.when(first)
    def _():
        ref[...] = s

    @pl.when(jnp.logical_not(first))
    def _():
        ref[...] += s


def rms_fwd(x, gain, *, name, col_block=0, width=None, deps=()):
    T = x.shape[0]
    width = width or x.shape[1]
    tm = _tile(T, ROW_TILE, 8)

    def body(x_ref, g_ref, o_ref):
        v = x_ref[...]
        o_ref[...] = (v * _rstd(v) * g_ref[...]).astype(BF)

    return _pcall(body, name=name, grid=(T // tm,),
                  ins=[(x, (tm, width), lambda i: (i, col_block)), (gain, (1, width), lambda i: (0, 0))],
                  outs=[((T, width), BF, (tm, width), lambda i: (i, 0))], semantics=("parallel",), deps=deps)[0]


def rms_bwd(x, gain, dy, *, name, col_block=0, dres=None, want_f32=True, want_bf=True, into=None, deps=()):
    T, width = dy.shape
    tm = _tile(T, ROW_TILE, 8)
    has_res = dres is not None

    def body(*refs):
        x_ref, g_ref, dy_ref = refs[:3]
        pos = 3
        res_ref = None
        if has_res:
            res_ref = refs[pos]
            pos += 1
        if into is not None:
            pos += 1
        outs = refs[pos:]
        dx, dg_rows = _rms_bwd(x_ref[...], g_ref[...], dy_ref[...])
        if has_res:
            dx = dx + res_ref[...]
        o = 0
        if want_f32:
            outs[o][...] = dx
            o += 1
        if want_bf:
            outs[o][...] = dx.astype(BF)
            o += 1
        _acc_rows(outs[o], dg_rows, pl.program_id(0) == 0)

    ins = [(x, (tm, width), lambda i: (i, col_block)), (gain, (1, width), lambda i: (0, 0)),
           (dy, (tm, width), lambda i: (i, 0))]
    if has_res:
        ins.append((dres, (tm, width), lambda i: (i, 0)))
    outs = []
    aliases = {}
    if want_f32:
        outs.append(((T, width), F32, (tm, width), lambda i: (i, 0)))
    if want_bf and into is not None:
        ins.append((into, None, None))
        aliases[len(ins) - 1] = len(outs)
        outs.append((into.shape, BF, (tm, width), lambda i: (i, col_block)))
    elif want_bf:
        outs.append(((T, width), BF, (tm, width), lambda i: (i, 0)))
    outs.append(((1, width), F32, (1, width), lambda i: (0, 0)))
    return _pcall(body, name=name, grid=(T // tm,), ins=ins, outs=outs, aliases=aliases, deps=deps)


def mla_prep(proj, q_norm, kv_norm, cos_t, sin_t, *, name):
    T = proj.shape[0]
    tm = _tile(T, ROW_TILE, 8)
    kr_block = (proj.shape[1] - LANES) // LANES

    def body(cq_ref, ckv_ref, kr_ref, qg_ref, kg_ref, cos_ref, sin_ref, qn_ref, kvn_ref, krope_ref):
        cq = cq_ref[...]
        qn_ref[...] = (cq * _rstd(cq) * qg_ref[...]).astype(BF)
        ckv = ckv_ref[...]
        kvn_ref[...] = (ckv * _rstd(ckv) * kg_ref[...]).astype(BF)
        krope_ref[...] = _rope_fwd(kr_ref[...], cos_ref[...], sin_ref[...]).astype(BF)

    return _pcall(
        body, name=name, grid=(T // tm,),
        ins=[(proj, (tm, Q_LORA), lambda i: (i, 0)), (proj, (tm, KV_LORA), lambda i: (i, 1)),
             (proj, (tm, LANES), lambda i: (i, kr_block)),
             (q_norm, (1, Q_LORA), lambda i: (0, 0)), (kv_norm, (1, KV_LORA), lambda i: (0, 0)),
             (cos_t, (tm, LANES), lambda i: (i, 0)), (sin_t, (tm, LANES), lambda i: (i, 0))],
        outs=[((T, Q_LORA), BF, (tm, Q_LORA), lambda i: (i, 0)), ((T, KV_LORA), BF, (tm, KV_LORA), lambda i: (i, 0)),
              ((T, LANES), BF, (tm, LANES), lambda i: (i, 0))],
        semantics=("parallel",))


def _attn_scores(q, k_blk, diagonal):
    s = lax.dot_general(q, k_blk, (((1,), (1,)), ((), ())), preferred_element_type=F32) * ATTN_SCALE
    if diagonal:
        row = lax.broadcasted_iota(jnp.int32, s.shape, 0)
        col = lax.broadcasted_iota(jnp.int32, s.shape, 1)
        s = jnp.where(col <= row, s, -jnp.inf)
    return s


def attn_fwd(q, k, v, *, name):
    T = q.shape[0]
    tq = _tile(T, ATTN_TILE, 8)

    def body(q_ref, k_ref, v_ref, o_ref, lse_ref):
        i = pl.program_id(1)
        qv = q_ref[...]

        def block(kb, carry, diagonal):
            m, l, acc = carry
            start = pl.multiple_of(kb * tq, tq)
            s = _attn_scores(qv, k_ref[pl.ds(start, tq), :], diagonal)
            m_new = jnp.maximum(m, jnp.max(s, axis=-1, keepdims=True))
            alpha = jnp.exp(m - m_new)
            p = jnp.exp(s - m_new)
            l = alpha * l + jnp.sum(p, axis=-1, keepdims=True)
            acc = alpha * acc + jnp.dot(p.astype(BF), v_ref[pl.ds(start, tq), :], preferred_element_type=F32)
            return m_new, l, acc

        init = (jnp.full((tq, 1), -jnp.inf, F32), jnp.zeros((tq, 1), F32), jnp.zeros((tq, V_HEAD), F32))
        carry = lax.fori_loop(0, i, lambda kb, c: block(kb, c, False), init)
        m, l, acc = block(i, carry, True)
        o_ref[...] = acc / l
        lse_ref[...] = jnp.broadcast_to(m + jnp.log(l), (tq, V_HEAD))

    return _pcall(
        body, name=name, grid=(HEADS, T // tq),
        ins=[(q, (tq, HEAD_PAD), lambda h, i: (i, h)), (k, (T, HEAD_PAD), lambda h, i: (0, h)),
             (v, (T, V_HEAD), lambda h, i: (0, h))],
        outs=[((T, MLA_OUT), F32, (tq, V_HEAD), lambda h, i: (i, h)),
              ((T, MLA_OUT), F32, (tq, V_HEAD), lambda h, i: (i, h))], semantics=("parallel", "parallel"))


def attn_bwd(q, k, v, o, lse, do, *, name):
    T = q.shape[0]
    tq = _tile(T, ATTN_TILE, 8)

    def body(q_ref, k_ref, v_ref, o_ref, lse_ref, do_ref, dq_ref, dk_ref, dv_ref):
        i = pl.program_id(1)

        @pl.when(i == 0)
        def _():
            dk_ref[...] = jnp.zeros_like(dk_ref)
            dv_ref[...] = jnp.zeros_like(dv_ref)

        qv = q_ref[...]
        do_t = do_ref[...]
        lse_v = lse_ref[:, 0:1]
        delta = jnp.sum(do_t.astype(F32) * o_ref[...], axis=-1, keepdims=True)

        def block(kb, dq, diagonal):
            start = pl.multiple_of(kb * tq, tq)
            k_blk = k_ref[pl.ds(start, tq), :]
            v_blk = v_ref[pl.ds(start, tq), :]
            p = jnp.exp(_attn_scores(qv, k_blk, diagonal) - lse_v)
            dp = lax.dot_general(do_t, v_blk, (((1,), (1,)), ((), ())), preferred_element_type=F32)
            ds = (p * (dp - delta) * ATTN_SCALE).astype(BF)
            dk_ref[pl.ds(start, tq), :] += lax.dot_general(ds, qv, (((0,), (0,)), ((), ())), preferred_element_type=F32)
            dv_ref[pl.ds(start, tq), :] += lax.dot_general(p.astype(BF), do_t, (((0,), (0,)), ((), ())),
                                                          preferred_element_type=F32)
            return dq + jnp.dot(ds, k_blk, preferred_element_type=F32)

        dq = lax.fori_loop(0, i, lambda kb, c: block(kb, c, False), jnp.zeros((tq, HEAD_PAD), F32))
        dq_ref[...] = block(i, dq, True)

    return _pcall(
        body, name=name, grid=(HEADS, T // tq),
        ins=[(q, (tq, HEAD_PAD), lambda h, i: (i, h)), (k, (T, HEAD_PAD), lambda h, i: (0, h)),
             (v, (T, V_HEAD), lambda h, i: (0, h)), (o, (tq, V_HEAD), lambda h, i: (i, h)),
             (lse, (tq, V_HEAD), lambda h, i: (i, h)), (do, (tq, V_HEAD), lambda h, i: (i, h))],
        outs=[((T, HEADS * HEAD_PAD), F32, (tq, HEAD_PAD), lambda h, i: (i, h)),
              ((T, HEADS * HEAD_PAD), F32, (T, HEAD_PAD), lambda h, i: (0, h)),
              ((T, MLA_OUT), F32, (T, V_HEAD), lambda h, i: (0, h))],
        semantics=("parallel", "arbitrary"))


def mla_bwd_prep(dq, dk, dv, cos_t, sin_t, dproj, *, name):
    T = dq.shape[0]
    tm = _tile(T, ROW_TILE, 8)
    kr_block = (dproj.shape[1] - LANES) // LANES

    def body(dq_ref, dk_ref, dv_ref, cos_ref, sin_ref, dproj_in, dql_ref, dkvl_ref, dkr_ref):
        cos_v, sin_v = cos_ref[...], sin_ref[...]
        kr = jnp.zeros((tm, LANES), F32)
        for h in range(HEADS):
            lo = h * HEAD_PAD
            dql_ref[:, lo:lo + QK_NOPE] = dq_ref[:, lo:lo + QK_NOPE].astype(BF)
            dql_ref[:, lo + QK_NOPE:lo + HEAD_PAD] = _rope_bwd(
                dq_ref[:, lo + QK_NOPE:lo + HEAD_PAD], cos_v, sin_v).astype(BF)
            dkvl_ref[:, lo:lo + QK_NOPE] = dk_ref[:, lo:lo + QK_NOPE].astype(BF)
            dkvl_ref[:, lo + QK_NOPE:lo + HEAD_PAD] = dv_ref[:, h * V_HEAD:(h + 1) * V_HEAD].astype(BF)
            kr = kr + dk_ref[:, lo + QK_NOPE:lo + HEAD_PAD]
        dkr_ref[...] = _rope_bwd(kr, cos_v, sin_v).astype(BF)

    W = HEADS * HEAD_PAD
    return _pcall(
        body, name=name, grid=(T // tm,),
        ins=[(dq, (tm, W), lambda i: (i, 0)), (dk, (tm, W), lambda i: (i, 0)), (dv, (tm, MLA_OUT), lambda i: (i, 0)),
             (cos_t, (tm, LANES), lambda i: (i, 0)), (sin_t, (tm, LANES), lambda i: (i, 0)), (dproj, None, None)],
        outs=[((T, W), BF, (tm, W), lambda i: (i, 0)), ((T, W), BF, (tm, W), lambda i: (i, 0)),
              (dproj.shape, BF, (tm, LANES), lambda i: (i, kr_block))],
        aliases={5: 2}, semantics=("parallel",))


def _group_norm_stats(vg):
    mu = jnp.mean(vg, axis=-1, keepdims=True)
    d = vg - mu
    r = lax.rsqrt(jnp.mean(d * d, axis=-1, keepdims=True) + EPS)
    return d * r, r


def mix_fwd(a, proj, g_mla, g_sgu, v_gain, w_tril, b_full, *, name):
    T = a.shape[0]
    tm = _tile(T, ROW_TILE, CHUNK)
    n_chunk = tm // CHUNK

    def body(a_ref, u_ref, v_ref, gm_ref, gs_ref, vg_ref, w_ref, b_ref, o_ref, s_scr):
        av = a_ref[...]
        o_ref[:, :MLA_OUT] = (av * _rstd(av) * gm_ref[...]).astype(BF)
        for g in range(GROUPS):
            sl = slice(g * CH, (g + 1) * CH)
            vhat, _ = _group_norm_stats(_gelu(v_ref[:, sl]))
            vn = (vhat * vg_ref[:, sl]).astype(BF)
            u = _gelu(u_ref[:, sl])
            for ci in range(n_chunk):
                rs = slice(ci * CHUNK, (ci + 1) * CHUNK)
                y = jnp.dot(w_ref[g], vn[rs], preferred_element_type=F32) + b_ref[:, sl]
                s_scr[rs, sl] = u[rs] * y
        s = s_scr[...]
        o_ref[:, MLA_OUT:] = (s * _rstd(s) * gs_ref[...]).astype(BF)

    return _pcall(
        body, name=name, grid=(T // tm,),
        ins=[(a, (tm, MLA_OUT), lambda i: (i, 0)), (proj, (tm, SGU_OUT), lambda i: (i, 1)),
             (proj, (tm, SGU_OUT), lambda i: (i, 2)), (g_mla, (1, MLA_OUT), lambda i: (0, 0)),
             (g_sgu, (1, SGU_OUT), lambda i: (0, 0)), (v_gain, (1, SGU_OUT), lambda i: (0, 0)),
             (w_tril, (GROUPS, CHUNK, CHUNK), lambda i: (0, 0, 0)), (b_full, (CHUNK, SGU_OUT), lambda i: (0, 0))],
        outs=[((T, MLA_OUT + SGU_OUT), BF, (tm, MLA_OUT + SGU_OUT), lambda i: (i, 0))],
        scratch=[pltpu.VMEM((tm, SGU_OUT), F32)], semantics=("parallel",))[0]


def mix_bwd(dmixed, a, proj, g_mla, g_sgu, v_gain, w_tril, w_tril_t, b_full, *, name, deps=()):
    T = a.shape[0]
    tm = _tile(T, ROW_TILE, CHUNK)
    n_chunk = tm // CHUNK
    uv0 = Q_LORA + KV_LORA

    def body(dm_a_ref, dm_s_ref, a_ref, u_ref, v_ref, gm_ref, gs_ref, vg_ref, w_ref, wt_ref, b_ref,
             da_ref, duv_ref, dgm_ref, dgs_ref, dvg_ref, dw_ref, db_ref, s_scr, y_scr):
        first = pl.program_id(0) == 0
        duv_ref[:, :uv0] = jnp.zeros((tm, uv0), BF)
        duv_ref[:, uv0 + 2 * SGU_OUT:] = jnp.zeros((tm, duv_ref.shape[1] - uv0 - 2 * SGU_OUT), BF)
        da, dgm_rows = _rms_bwd(a_ref[...], gm_ref[...], dm_a_ref[...])
        da_ref[...] = da.astype(BF)
        _acc_rows(dgm_ref, dgm_rows, first)

        for g in range(GROUPS):
            sl = slice(g * CH, (g + 1) * CH)
            vhat, _ = _group_norm_stats(_gelu(v_ref[:, sl]))
            vn = (vhat * vg_ref[:, sl]).astype(BF)
            u = _gelu(u_ref[:, sl])
            for ci in range(n_chunk):
                rs = slice(ci * CHUNK, (ci + 1) * CHUNK)
                y = jnp.dot(w_ref[g], vn[rs], preferred_element_type=F32) + b_ref[:, sl]
                y_scr[rs, sl] = y
                s_scr[rs, sl] = u[rs] * y
        ds, dgs_rows = _rms_bwd(s_scr[...], gs_ref[...], dm_s_ref[...])
        _acc_rows(dgs_ref, dgs_rows, first)
        s_scr[...] = ds

        @pl.when(first)
        def _():
            dw_ref[...] = jnp.zeros_like(dw_ref)
            db_ref[...] = jnp.zeros_like(db_ref)

        for g in range(GROUPS):
            sl = slice(g * CH, (g + 1) * CH)
            upre = u_ref[:, sl]
            vpre = v_ref[:, sl]
            u = _gelu(upre)
            vhat, r = _group_norm_stats(_gelu(vpre))
            gain = vg_ref[:, sl]
            vn = (vhat * gain).astype(BF)
            dsg = s_scr[:, sl]
            duv_ref[:, uv0 + g * CH:uv0 + (g + 1) * CH] = (dsg * y_scr[:, sl] * _gelu_grad(upre)).astype(BF)
            dy = dsg * u
            dyb = dy.astype(BF)
            dvn_parts = []
            for ci in range(n_chunk):
                rs = slice(ci * CHUNK, (ci + 1) * CHUNK)
                dvn_parts.append(jnp.dot(wt_ref[g], dyb[rs], preferred_element_type=F32))
                dw_ref[g] += lax.dot_general(dyb[rs], vn[rs], (((1,), (1,)), ((), ())), preferred_element_type=F32)
                db_ref[:, sl] += jnp.broadcast_to(jnp.sum(dy[rs], axis=-1, keepdims=True), (CHUNK, CH))
            dvn = dvn_parts[0] if n_chunk == 1 else jnp.concatenate(dvn_parts, axis=0)
            _acc_rows(dvg_ref.at[:, sl], dvn * vhat, first)
            dvh = dvn * gain
            dvg = r * (dvh - jnp.mean(dvh, axis=-1, keepdims=True)
                       - vhat * jnp.mean(dvh * vhat, axis=-1, keepdims=True))
            duv_ref[:, uv0 + SGU_OUT + g * CH:uv0 + SGU_OUT + (g + 1) * CH] = (dvg * _gelu_grad(vpre)).astype(BF)

    return _pcall(
        body, name=name, grid=(T // tm,),
        ins=[(dmixed, (tm, MLA_OUT), lambda i: (i, 0)), (dmixed, (tm, SGU_OUT), lambda i: (i, 1)),
             (a, (tm, MLA_OUT), lambda i: (i, 0)), (proj, (tm, SGU_OUT), lambda i: (i, 1)),
             (proj, (tm, SGU_OUT), lambda i: (i, 2)), (g_mla, (1, MLA_OUT), lambda i: (0, 0)),
             (g_sgu, (1, SGU_OUT), lambda i: (0, 0)), (v_gain, (1, SGU_OUT), lambda i: (0, 0)),
             (w_tril, (GROUPS, CHUNK, CHUNK), lambda i: (0, 0, 0)), (w_tril_t, (GROUPS, CHUNK, CHUNK), lambda i: (0, 0, 0)),
             (b_full, (CHUNK, SGU_OUT), lambda i: (0, 0))],
        outs=[((T, MLA_OUT), BF, (tm, MLA_OUT), lambda i: (i, 0)),
              ((T, proj.shape[1]), BF, (tm, proj.shape[1]), lambda i: (i, 0)),
              ((1, MLA_OUT), F32, (1, MLA_OUT), lambda i: (0, 0)), ((1, SGU_OUT), F32, (1, SGU_OUT), lambda i: (0, 0)),
              ((1, SGU_OUT), F32, (1, SGU_OUT), lambda i: (0, 0)),
              ((GROUPS, CHUNK, CHUNK), F32, (GROUPS, CHUNK, CHUNK), lambda i: (0, 0, 0)),
              ((CHUNK, SGU_OUT), F32, (CHUNK, SGU_OUT), lambda i: (0, 0))],
        scratch=[pltpu.VMEM((tm, SGU_OUT), F32), pltpu.VMEM((tm, SGU_OUT), F32)], deps=deps)


def _shift_down(z, n, row):
    return jnp.where(row >= n, pltpu.roll(z, n, 0), 0.0)


def _shift_up(z, n, row, T):
    return jnp.where(row < T - n, pltpu.roll(z, T - n, 0), 0.0)


def conv_fwd(proj, conv_w, *, name):
    T, D3 = proj.shape
    D = D3 // 3
    tn = _tile(D, 256)
    nj = D // tn

    def body(b_ref, c_ref, x_ref, w_ref, o_ref):
        row = lax.broadcasted_iota(jnp.int32, (T, tn), 0)
        z = c_ref[...] * x_ref[...]
        zc = w_ref[2:3, :] * z + w_ref[1:2, :] * _shift_down(z, 1, row) + w_ref[0:1, :] * _shift_down(z, 2, row)
        o_ref[...] = (b_ref[...] * zc).astype(BF)

    return _pcall(
        body, name=name, grid=(nj,),
        ins=[(proj, (T, tn), lambda j: (0, j)), (proj, (T, tn), lambda j: (0, nj + j)),
             (proj, (T, tn), lambda j: (0, 2 * nj + j)), (conv_w, (3, tn), lambda j: (0, j))],
        outs=[((T, D), BF, (T, tn), lambda j: (0, j))], semantics=("parallel",))[0]


def conv_bwd(dg, proj, conv_w, *, name, deps=()):
    T, D3 = proj.shape
    D = D3 // 3
    tn = _tile(D, 256)
    nj = D // tn

    def body(dg_ref, b_ref, c_ref, x_ref, w_ref, dp_ref, dw_ref, dc_scr, dx_scr):
        part = pl.program_id(1)

        @pl.when(part == 0)
        def _():
            row = lax.broadcasted_iota(jnp.int32, (T, tn), 0)
            c, x = c_ref[...], x_ref[...]
            z = c * x
            z1 = _shift_down(z, 1, row)
            z2 = _shift_down(z, 2, row)
            dgv = dg_ref[...]
            zc = w_ref[2:3, :] * z + w_ref[1:2, :] * z1 + w_ref[0:1, :] * z2
            dp_ref[...] = (dgv * zc).astype(BF)
            dzc = dgv * b_ref[...]
            dw_ref[0:1, :] = jnp.sum(dzc * z2, axis=0, keepdims=True)
            dw_ref[1:2, :] = jnp.sum(dzc * z1, axis=0, keepdims=True)
            dw_ref[2:3, :] = jnp.sum(dzc * z, axis=0, keepdims=True)
            dz = (w_ref[2:3, :] * dzc + w_ref[1:2, :] * _shift_up(dzc, 1, row, T)
                  + w_ref[0:1, :] * _shift_up(dzc, 2, row, T))
            dc_scr[...] = (dz * x).astype(BF)
            dx_scr[...] = (dz * c).astype(BF)

        @pl.when(part == 1)
        def _():
            dp_ref[...] = dc_scr[...]

        @pl.when(part == 2)
        def _():
            dp_ref[...] = dx_scr[...]

    return _pcall(
        body, name=name, grid=(nj, 3),
        ins=[(dg, (T, tn), lambda j, p: (0, j)), (proj, (T, tn), lambda j, p: (0, j)),
             (proj, (T, tn), lambda j, p: (0, nj + j)), (proj, (T, tn), lambda j, p: (0, 2 * nj + j)),
             (conv_w, (3, tn), lambda j, p: (0, j))],
        outs=[((T, D3), BF, (T, tn), lambda j, p: (0, p * nj + j)), ((3, D), F32, (3, tn), lambda j, p: (0, j))],
        scratch=[pltpu.VMEM((T, tn), BF), pltpu.VMEM((T, tn), BF)], semantics=("parallel", "arbitrary"), deps=deps)


def loss_bwd(x_parts, gain, target, *, name):
    T, D = target.shape
    tm = _tile(T, ROW_TILE, 8)
    n_x = len(x_parts)

    def body(*refs):
        x_refs = refs[:n_x]
        g_ref, t_ref, dx_ref, dxb_ref, dg_ref, loss_ref = refs[n_x:]
        first = pl.program_id(0) == 0
        xv = jnp.concatenate([r[...] for r in x_refs], axis=-1) if n_x > 1 else x_refs[0][...]
        r = _rstd(xv)
        xh = xv * r
        gain_v = g_ref[...]
        err = xh * gain_v - t_ref[...]
        part = 0.5 * jnp.sum(jnp.mean(err * err, axis=-1, keepdims=True), axis=0, keepdims=True)
        _acc_rows(loss_ref, jnp.broadcast_to(part, (1, LANES)), first)
        dy = err * (1.0 / D)
        gdy = dy * gain_v
        dx = r * (gdy - xh * jnp.mean(gdy * xh, axis=-1, keepdims=True))
        dx_ref[...] = dx
        dxb_ref[...] = dx.astype(BF)
        _acc_rows(dg_ref, dy * xh, first)

    return _pcall(
        body, name=name, grid=(T // tm,),
        ins=[(p, (tm, D // n_x), lambda i: (i, 0)) for p in x_parts]
        + [(gain, (1, D), lambda i: (0, 0)), (target, (tm, D), lambda i: (i, 0))],
        outs=[((T, D), F32, (tm, D), lambda i: (i, 0)), ((T, D), BF, (tm, D), lambda i: (i, 0)),
              ((1, D), F32, (1, D), lambda i: (0, 0)), ((1, LANES), F32, (1, LANES), lambda i: (0, 0))])


def _adamw(g, w, m, v):
    m = ADAM_B1 * m + (1.0 - ADAM_B1) * g
    v = ADAM_B2 * v + (1.0 - ADAM_B2) * (g * g)
    m_hat = m / ADAM_C1
    v_hat = v / ADAM_C2
    delta = -ADAM_LR * (m_hat / (jnp.sqrt(v_hat) + ADAM_EPS) + ADAM_WD * w)
    return delta, m, v


def adam_flat(g, w, m, v, *, name):
    def body(g_ref, w_ref, m_ref, v_ref, d_ref, nm_ref, nv_ref):
        d, nm, nv = _adamw(g_ref[...], w_ref[...], m_ref[...], v_ref[...])
        d_ref[...] = d
        nm_ref[...] = nm
        nv_ref[...] = nv

    blk = g.shape
    zero = lambda: (0, 0)
    return _pcall(body, name=name, grid=(),
                  ins=[(t, blk, zero) for t in (g, w, m, v)],
                  outs=[(blk, F32, blk, zero)] * 3)


def _chip_slots():
    x, y, c = lax.axis_index("x"), lax.axis_index("y"), lax.axis_index("c")
    chips = [(1 - x, y), (x, 1 - y), (1 - x, 1 - y)]
    return x, y, c, chips


def device_index():
    x, y, c, chips = _chip_slots()
    return jnp.stack([4 * x + 2 * y + c, 2 * x + y] + [4 * cx + 2 * cy + c for cx, cy in chips]
                     + [2 * cx + cy for cx, cy in chips]).astype(jnp.int32)


def _job_rows(R, C, n_steps):
    if n_steps is None:
        n_steps = max(1, R * C // STREAM_BLOCK_ELEMS)
    n_blk = max([d for d in range(1, n_steps + 1) if R % d == 0 and (R // d) % 16 == 0] or [1])
    return R // n_blk, n_blk


def run_job(job, *, index, name, deps=()):
    jb = job(None)
    n_in = len(jb["ins"])

    def body(idx_ref, *refs):
        jb["fn"](refs[:n_in], refs[n_in:n_in + len(jb["outs"])])

    return _pcall(body, name=name, grid=(jb["n_blk"],), ins=jb["ins"], outs=jb["outs"], prefetch=index,
                  aliases={1 + a: o for a, o in jb["aliases"].items()}, semantics=("parallel",), deps=deps)


def adam_job(gs, a_buf, b_buf, w, m, v, layer, prev):
    L, R, C = w.shape

    def build(n_steps):
        tr, n_blk = _job_rows(R, C, n_steps)
        blk = (None, tr, C)
        row = lambda t: jnp.minimum(t, n_blk - 1)
        ins = [(gs, blk, lambda t, s: (s[0], row(t), 0)), (a_buf, blk, lambda t, s: (s[1], row(t), 0))]
        ins += [(b_buf, blk, lambda t, s, j=j: (j, row(t), 0)) for j in range(3)]
        ins += [(p, blk, lambda t, s: (layer, row(t), 0)) for p in (w, m, v)]
        ins += [(p, None, None) for p in (prev or [])]

        def fn(i, o):
            g = ((((i[0][...].astype(F32) + i[1][...].astype(F32)) + i[2][...].astype(F32))
                  + i[3][...].astype(F32)) + i[4][...].astype(F32))
            d, nm, nv = _adamw(g, i[5][...], i[6][...], i[7][...])
            o[0][...] = g
            o[1][...] = d
            o[2][...] = nm
            o[3][...] = nv

        return dict(ins=ins, outs=[((L, R, C), F32, blk, lambda t, s: (layer, row(t), 0))] * 4, fn=fn,
                    aliases={8 + o: o for o in range(4)} if prev else {}, n_blk=n_blk)

    return build


def pair_job(gs, a_buf):
    _, R, C = gs.shape

    def build(n_steps):
        tr, n_blk = _job_rows(R, C, n_steps)
        blk = (None, tr, C)
        row = lambda t: jnp.minimum(t, n_blk - 1)
        ins = [(gs, blk, lambda t, s, j=j: (s[2 + j], row(t), 0)) for j in range(3)]
        ins += [(a_buf, blk, lambda t, s, j=j: (s[5 + j], row(t), 0)) for j in range(3)]

        def fn(i, o):
            for j in range(3):
                o[0][j] = (i[j][...].astype(F32) + i[3 + j][...].astype(F32)).astype(BF)

        return dict(ins=ins, outs=[((3, R, C), BF, (3, tr, C), lambda t, s: (0, row(t), 0))], fn=fn, aliases={},
                    n_blk=n_blk)

    return build


def reduce_sum(gs, a_buf, b_buf, *, name):
    _, R, C = gs.shape
    tr = _tile(R, 256, 16)
    x, y, c, _ = _chip_slots()
    idx = jnp.stack([4 * x + 2 * y + c, 2 * x + y]).astype(jnp.int32)

    def body(idx_ref, g_ref, a_ref, b0_ref, b1_ref, b2_ref, o_ref):
        o_ref[...] = ((((g_ref[...].astype(F32) + a_ref[...].astype(F32)) + b0_ref[...].astype(F32))
                       + b1_ref[...].astype(F32)) + b2_ref[...].astype(F32))

    blk3 = (None, tr, C)
    return _pcall(body, name=name, grid=(R // tr,),
                  ins=[(gs, blk3, lambda i, s: (s[0], i, 0)), (a_buf, blk3, lambda i, s: (s[1], i, 0)),
                       (b_buf, blk3, lambda i, s: (0, i, 0)), (b_buf, blk3, lambda i, s: (1, i, 0)),
                       (b_buf, blk3, lambda i, s: (2, i, 0))],
                  outs=[((R, C), F32, (tr, C), lambda i, s: (i, 0))], prefetch=idx, semantics=("parallel",))[0]


def adam_rows(g, w, m, v, *, name):
    R, C = g.shape
    tr = _tile(R, 256, 8)

    def body(g_ref, w_ref, m_ref, v_ref, d_ref, nm_ref, nv_ref):
        d, nm, nv = _adamw(g_ref[...], w_ref[...], m_ref[...], v_ref[...])
        d_ref[...] = d
        nm_ref[...] = nm
        nv_ref[...] = nv

    spec = ((tr, C), lambda i: (i, 0))
    return _pcall(body, name=name, grid=(R // tr,), ins=[(t, *spec) for t in (g, w, m, v)],
                  outs=[((R, C), F32, *spec)] * 3, semantics=("parallel",))


def sum_rows8(gathered, rows, *, name):
    W = gathered.shape[1]

    def body(g_ref, o_ref):
        acc = g_ref[0:rows, :]
        for d in range(1, N_DEV):
            acc = acc + g_ref[d * rows:(d + 1) * rows, :]
        o_ref[...] = acc

    return _pcall(body, name=name, grid=(), ins=[(gathered, gathered.shape, lambda: (0, 0))],
                  outs=[((rows, W), F32, (rows, W), lambda: (0, 0))])[0]


HBM_SPEC = pl.BlockSpec(memory_space=pltpu.HBM)
SEM_SPEC = pl.BlockSpec(memory_space=pltpu.SEMAPHORE)
ANY_SPEC = pl.BlockSpec(memory_space=pl.ANY)
DATAFLOW = pltpu.SideEffectType.DATAFLOW_SIDE_EFFECTING


def _in_hbm(v):
    return pltpu.with_memory_space_constraint(v, pltpu.HBM)


def _slot(p):
    return 4 * p[0] + 2 * p[1] + p[2]


def _gather_peers():
    x, y, c, chips = _chip_slots()
    return (x, y, c), [(x, y, 1 - c)] + [(*chip, c) for chip in chips]


def gather_start(groups, after, *, name):
    flat = [s for g in groups for s in g]
    n, n_g = len(flat), len(groups)
    where = [(gi, ti) for gi, g in enumerate(groups) for ti in range(len(g))]

    def body(*refs):
        src, land = refs[:n], refs[n:2 * n]
        sems = refs[2 * n + 1:2 * n + 1 + 2 * n_g]
        me, peers = _gather_peers()
        for t in range(n):
            gi, ti = where[t]
            for k, to in enumerate(peers):
                pltpu.make_async_remote_copy(
                    src_ref=src[t], dst_ref=land[t].at[_slot(me)], send_sem=sems[2 * gi].at[4 * ti + k],
                    recv_sem=sems[2 * gi + 1].at[4 * ti + k], device_id=to, device_id_type=MESH).start()
        refs[-1][...] = jnp.zeros_like(refs[-1])

    out_shape = []
    for g in groups:
        out_shape += [pltpu.SemaphoreType.DMA((4 * len(g),)), pltpu.SemaphoreType.DMA((4 * len(g),))]
    out_shape += [pltpu.HBM(s.shape, s.dtype) for s in flat]
    out_shape += [pltpu.HBM((N_DEV,) + s.shape, s.dtype) for s in flat]
    out_shape += [jax.ShapeDtypeStruct((8, LANES), F32)]
    aliases = {t: 2 * n_g + t for t in range(n)}
    aliases.update({n + t: 2 * n_g + n + t for t in range(n)})
    res = pl.pallas_call(
        body, name=name, out_shape=out_shape, in_specs=[HBM_SPEC] * (2 * n) + [ANY_SPEC],
        out_specs=[SEM_SPEC] * (2 * n_g) + [HBM_SPEC] * (2 * n) + [pl.BlockSpec(memory_space=pltpu.VMEM)],
        input_output_aliases=aliases, compiler_params=pltpu.CompilerParams(has_side_effects=DATAFLOW),
    )(*[_in_hbm(s) for s in flat], *[_in_hbm(lax.empty((N_DEV,) + s.shape, s.dtype)) for s in flat], after)
    out, off = [], 0
    for gi, g in enumerate(groups):
        k = len(g)
        out.append((res[2 * gi], res[2 * gi + 1], res[2 * n_g + off:2 * n_g + off + k],
                    res[2 * n_g + n + off:2 * n_g + n + off + k]))
        off += k
    return out, res[-1]


def gather_wait(started, after, *, name):
    send_sems, recv_sems, srcs, lands = started
    n = len(srcs)
    after = list(after)

    def body(*refs):
        src, land = refs[:n], refs[n:2 * n]
        send, recv = refs[2 * n], refs[2 * n + 1]
        _, peers = _gather_peers()
        for t in range(n):
            for k, frm in enumerate(peers):
                cp = pltpu.make_async_remote_copy(
                    src_ref=src[t], dst_ref=land[t].at[_slot(frm)], send_sem=send.at[4 * t + k],
                    recv_sem=recv.at[4 * t + k],
                    device_id=frm, device_id_type=MESH)
                cp.wait_send()
                cp.wait_recv()

    res = pl.pallas_call(
        body, name=name,
        out_shape=[pltpu.HBM(s.shape, s.dtype) for s in srcs] + [pltpu.HBM(l.shape, l.dtype) for l in lands],
        in_specs=[HBM_SPEC] * (2 * n) + [SEM_SPEC, SEM_SPEC] + [ANY_SPEC] * len(after),
        out_specs=[HBM_SPEC] * (2 * n), input_output_aliases={t: t for t in range(2 * n)},
        compiler_params=pltpu.CompilerParams(has_side_effects=DATAFLOW),
    )(*srcs, *lands, send_sems, recv_sems, *after)
    return res[:n], res[n:]


def place_own(src, land, *, name):
    R, C = src.shape
    tr = _tile(R, 512, 16)
    x, y, c, _ = _chip_slots()
    idx = jnp.stack([4 * x + 2 * y + c]).astype(jnp.int32)

    def body(idx_ref, s_ref, land_ref, o_ref):
        o_ref[...] = s_ref[...]

    return _pcall(body, name=name, grid=(R // tr,),
                  ins=[(src, (tr, C), lambda i, s: (i, 0)), (land, None, None)],
                  outs=[(land.shape, land.dtype, (None, tr, C), lambda i, s: (s[0], i, 0))],
                  prefetch=idx, aliases={2: 0}, semantics=("parallel",))[0]


def gather_finish(srcs, lands, *, name):
    n = len(srcs)

    def body(*refs):
        land = refs[n:2 * n]
        send_sems, recv_sems = refs[2 * n:]
        x, y, c, chips = _chip_slots()
        me, sibling = (x, y, c), (x, y, 1 - c)

        def copy(t, j, block, to):
            return pltpu.make_async_remote_copy(
                src_ref=land[t].at[_slot(block)], dst_ref=land[t].at[_slot(block)], send_sem=send_sems.at[t, j],
                recv_sem=recv_sems.at[t, j], device_id=to, device_id_type=MESH)

        sends = [copy(t, j, (*chip, c), sibling) for t in range(n) for j, chip in enumerate(chips)]
        for cp in sends:
            cp.start()
        for t in range(n):
            for j, chip in enumerate(chips):
                copy(t, j, (*chip, 1 - c), me).wait_recv()
        for cp in sends:
            cp.wait_send()

    passed = pl.pallas_call(
        body, name=name, out_shape=[jax.ShapeDtypeStruct(l.shape, l.dtype) for l in lands],
        in_specs=[ANY_SPEC] * n, out_specs=[ANY_SPEC] * n,
        input_output_aliases={t: t for t in range(n)},
        scratch_shapes=[pltpu.SemaphoreType.DMA((n, 3)), pltpu.SemaphoreType.DMA((n, 3))],
    )(*lands)
    return [place_own(s, l, name=f"{name}_own{t}") for t, (s, l) in enumerate(zip(srcs, passed))]


def chips_start(pairs, *, name):
    n = len(pairs)

    def body(*refs):
        src, land = refs[:n], refs[n:2 * n]
        send, recv = refs[2 * n], refs[2 * n + 1]
        token = refs[-1]
        x, y, c, chips = _chip_slots()
        for t in range(n):
            for j, chip in enumerate(chips):
                pltpu.make_async_remote_copy(
                    src_ref=src[t].at[j], dst_ref=land[t].at[j], send_sem=send.at[3 * t + j],
                    recv_sem=recv.at[3 * t + j], device_id=(*chip, c), device_id_type=MESH).start()
        token[...] = jnp.zeros_like(token)

    res = pl.pallas_call(
        body, name=name,
        out_shape=[pltpu.SemaphoreType.DMA((3 * n,)), pltpu.SemaphoreType.DMA((3 * n,))]
        + [pltpu.HBM(p.shape, p.dtype) for p in pairs] * 2 + [jax.ShapeDtypeStruct((8, LANES), F32)],
        in_specs=[HBM_SPEC] * (2 * n),
        out_specs=[SEM_SPEC, SEM_SPEC] + [HBM_SPEC] * (2 * n) + [pl.BlockSpec(memory_space=pltpu.VMEM)],
        input_output_aliases={t: 2 + t for t in range(2 * n)},
        compiler_params=pltpu.CompilerParams(has_side_effects=DATAFLOW),
    )(*[_in_hbm(p) for p in pairs], *[_in_hbm(lax.empty(p.shape, p.dtype)) for p in pairs])
    return res[0], res[1], res[2:2 + n], res[2 + n:2 + 2 * n], res[-1]


def chips_wait(started, after, *, name):
    send_sems, recv_sems, srcs, lands, _ = started
    n = len(srcs)

    def body(*refs):
        src, land = refs[:n], refs[n:2 * n]
        send, recv = refs[2 * n], refs[2 * n + 1]
        x, y, c, chips = _chip_slots()
        for t in range(n):
            for j, chip in enumerate(chips):
                cp = pltpu.make_async_remote_copy(
                    src_ref=src[t].at[j], dst_ref=land[t].at[j], send_sem=send.at[3 * t + j],
                    recv_sem=recv.at[3 * t + j], device_id=(*chip, c), device_id_type=MESH)
                cp.wait_send()
                cp.wait_recv()

    res = pl.pallas_call(
        body, name=name, out_shape=[pltpu.HBM(s.shape, s.dtype) for s in srcs] * 2,
        in_specs=[HBM_SPEC] * (2 * n) + [SEM_SPEC, SEM_SPEC, ANY_SPEC], out_specs=[HBM_SPEC] * (2 * n),
        input_output_aliases={t: t for t in range(2 * n)},
        compiler_params=pltpu.CompilerParams(has_side_effects=DATAFLOW),
    )(*srcs, *lands, send_sems, recv_sems, after)
    return res[n:]


def _sibling_copies(src, land, send, recv, n):
    x, y, c, _ = _chip_slots()
    return [pltpu.make_async_remote_copy(
        src_ref=src[t].at[4 * (q // 2) + 2 * (q % 2) + (1 - c)], dst_ref=land[t].at[q], send_sem=send.at[4 * t + q],
        recv_sem=recv.at[4 * t + q], device_id=(x, y, 1 - c), device_id_type=MESH)
        for t in range(n) for q in range(4)]


def sibling_start(gs, *, name):
    n = len(gs)

    def body(*refs):
        for cp in _sibling_copies(refs[:n], refs[n:2 * n], refs[2 * n], refs[2 * n + 1], n):
            cp.start()
        refs[-1][...] = jnp.zeros_like(refs[-1])

    lands = [lax.empty((4,) + g.shape[1:], g.dtype) for g in gs]
    res = pl.pallas_call(
        body, name=name,
        out_shape=[pltpu.SemaphoreType.DMA((4 * n,)), pltpu.SemaphoreType.DMA((4 * n,))]
        + [pltpu.HBM(g.shape, g.dtype) for g in gs] + [pltpu.HBM(l.shape, l.dtype) for l in lands]
        + [jax.ShapeDtypeStruct((8, LANES), F32)],
        in_specs=[HBM_SPEC] * (2 * n),
        out_specs=[SEM_SPEC, SEM_SPEC] + [HBM_SPEC] * (2 * n) + [pl.BlockSpec(memory_space=pltpu.VMEM)],
        input_output_aliases={t: 2 + t for t in range(2 * n)},
        compiler_params=pltpu.CompilerParams(has_side_effects=DATAFLOW),
    )(*[_in_hbm(g) for g in gs], *[_in_hbm(l) for l in lands])
    return res[0], res[1], res[2:2 + n], res[2 + n:2 + 2 * n], res[-1]


def sibling_wait(started, after, *, name):
    send_sems, recv_sems, srcs, lands, _ = started
    n = len(srcs)

    def body(*refs):
        for cp in _sibling_copies(refs[:n], refs[n:2 * n], refs[2 * n], refs[2 * n + 1], n):
            cp.wait_send()
            cp.wait_recv()

    res = pl.pallas_call(
        body, name=name,
        out_shape=[pltpu.HBM(s.shape, s.dtype) for s in srcs] + [pltpu.HBM(l.shape, l.dtype) for l in lands],
        in_specs=[HBM_SPEC] * (2 * n) + [SEM_SPEC, SEM_SPEC, ANY_SPEC], out_specs=[HBM_SPEC] * (2 * n),
        input_output_aliases={t: t for t in range(2 * n)},
        compiler_params=pltpu.CompilerParams(has_side_effects=DATAFLOW),
    )(*srcs, *lands, send_sems, recv_sems, after)
    return res[:n], res[n:]


def _rope_slab(cols):
    z = jnp.zeros(cols.shape[:-1] + (HALF_ROPE,), cols.dtype)
    return jnp.concatenate([cols[..., :HALF_ROPE], z, cols[..., HALF_ROPE:], z], axis=-1)


def _rope_unslab(slab):
    return jnp.concatenate([slab[..., :HALF_ROPE], slab[..., 2 * HALF_ROPE:3 * HALF_ROPE]], axis=-1)


def _pack_w_in_t(wt_g):
    s, c, d = wt_g.shape
    w = wt_g.reshape(s * c, d)
    c2, c3 = Q_LORA + KV_LORA, Q_LORA + KV_LORA + QK_ROPE
    r = w[c2:c3]
    z = jnp.zeros((HALF_ROPE, d), w.dtype)
    return jnp.concatenate([w[:c2], w[c3:], r[:HALF_ROPE], z, r[HALF_ROPE:], z], axis=0)


def unpack_w_in_t_grad(dwt, *, name):
    n_rows, d = dwt.shape
    kr = QK_ROPE
    n_out_rows = n_rows - kr
    blk = n_out_rows // 7
    assert blk * 7 == n_out_rows and blk % kr == 0 and n_rows % (2 * kr) == 0
    kr_row = Q_LORA + KV_LORA
    k_mix = kr_row // blk
    off = kr_row - k_mix * blk
    slab_block = (n_rows - 2 * kr) // (2 * kr)

    def body(prev_ref, in_ref, slab_ref, o_ref):
        k = pl.program_id(0)

        @pl.when(k < k_mix)
        def _():
            o_ref[...] = in_ref[...]

        @pl.when(k == k_mix)
        def _():
            o_ref[:off, :] = in_ref[:off, :]
            o_ref[off:off + HALF_ROPE, :] = slab_ref[:HALF_ROPE, :]
            o_ref[off + HALF_ROPE:off + kr, :] = slab_ref[2 * HALF_ROPE:3 * HALF_ROPE, :]
            o_ref[off + kr:, :] = in_ref[off:blk - kr, :]

        @pl.when(k > k_mix)
        def _():
            o_ref[:kr, :] = prev_ref[blk - kr:, :]
            o_ref[kr:, :] = in_ref[:blk - kr, :]

    out = _pcall(body, name=name, grid=(7,),
                 ins=[(dwt, (blk, d), lambda k: (jnp.maximum(k - 1, 0), 0)), (dwt, (blk, d), lambda k: (k, 0)),
                      (dwt, (2 * kr, d), lambda k: (slab_block, 0))],
                 outs=[((n_out_rows, d), dwt.dtype, (blk, d), lambda k: (k, 0))], semantics=("parallel",))[0]
    return out.reshape(N_DEV, n_out_rows // N_DEV, d)


def _rope_tables(positions):
    inv_freq = ROPE_BASE ** (-jnp.arange(0, QK_ROPE, 2, dtype=F32) / QK_ROPE)
    ang = positions.astype(F32)[:, None] * inv_freq
    cos, sin = jnp.cos(ang), jnp.sin(ang)
    z = jnp.zeros_like(cos)
    return jnp.concatenate([cos, z, cos, z], axis=-1), jnp.concatenate([-sin, z, sin, z], axis=-1)


def _mlp_up(x, gain, w1, tag):
    hn = rms_fwd(x, gain, name=f"mlp{tag}_norm")

    def act_epi(acc):
        a = jnp.maximum(acc, 0.0)
        return a, a * a

    T = x.shape[0]
    F = w1.shape[0] * w1.shape[2]
    a, act = mm(hn, w1, name=f"mlp{tag}_up", outs=[((T, F), BF, None), ((T, F), BF, None)], epi=act_epi)
    return hn, a, act


def _mlp_down(x, act, w2, tag, part=0):
    n = w2.shape[1]
    bm = _tile(x.shape[0], MM_TILE)
    bn = _tile(n, MM_TILE)
    per = n // bn
    return mm(act, w2, name=f"mlp{tag}_down{part}", out=((x.shape[0], n), F32), bm=bm, bn=bn,
              epi=lambda acc, r: (acc + r[...],), epi_ins=[(x, (bm, bn), lambda i, j, k: (i, part * per + j))])


def _mlp_bwd_weights(w1, w2, saved, dxb, tag):
    hn, a, act = saved
    T, D = dxb.shape
    F = a.shape[1]
    bm = _tile(T, MM_TILE)
    bn = _tile(F, min(MM_TILE, w1.shape[2]))
    dhid = mm(dxb, w2, tb=True, name=f"mlp{tag}_dhid", out=((T, F), BF), bm=bm, bn=bn,
              epi=lambda acc, a_ref: (2.0 * a_ref[...].astype(F32) * acc,),
              epi_ins=[(a, (bm, bn), lambda i, j, k: (i, j))])
    dw2 = mm(act, dxb, ta=True, name=f"mlp{tag}_dw2", out=((F, D), BF))
    dw1 = mm(hn, dhid, ta=True, name=f"mlp{tag}_dw1", out=(w1.shape, BF))
    return dhid, dw1, dw2.reshape(N_DEV, F // N_DEV, D)


def _reduce_begin(grads, tag):
    return sibling_start(grads, name=f"reduce_sibling_start_{tag}")


def _reduce_continue(sib, after, tag, index):
    grads, a_bufs = sibling_wait(sib, after, name=f"reduce_sibling_wait_{tag}")
    pairs = [run_job(pair_job(g, a), index=index, name=f"pair_sum_{tag}{t}")[0]
             for t, (g, a) in enumerate(zip(grads, a_bufs))]
    return grads, a_bufs, chips_start(pairs, name=f"reduce_chips_start_{tag}")


def kernel(x, positions, e_norm_mix, e_w_in, e_q_norm, e_w_uq, e_kv_norm, e_w_ukv, e_v_norm, e_sgu_w, e_sgu_b, e_mla_out_norm, e_sgu_out_norm, e_w_out, o_norm_mix, o_w_in, o_conv_w, o_w_out, mlp_norm, mlp_w1, mlp_w2, final_norm, loss_target, m_e_norm_mix, m_e_w_in, m_e_q_norm, m_e_w_uq, m_e_kv_norm, m_e_w_ukv, m_e_v_norm, m_e_sgu_w, m_e_sgu_b, m_e_mla_out_norm, m_e_sgu_out_norm, m_e_w_out, m_o_norm_mix, m_o_w_in, m_o_conv_w, m_o_w_out, m_mlp_norm, m_mlp_w1, m_mlp_w2, m_final_norm, v_e_norm_mix, v_e_w_in, v_e_q_norm, v_e_w_uq, v_e_kv_norm, v_e_w_ukv, v_e_v_norm, v_e_sgu_w, v_e_sgu_b, v_e_mla_out_norm, v_e_sgu_out_norm, v_e_w_out, v_o_norm_mix, v_o_w_in, v_o_conv_w, v_o_w_out, v_mlp_norm, v_mlp_w1, v_mlp_w2, v_final_norm):
    T, D = x.shape[1], x.shape[2]
    d_shard = o_norm_mix.shape[1]
    x0 = x[0]
    target = loss_target[0]
    me = 4 * lax.axis_index("x") + 2 * lax.axis_index("y") + lax.axis_index("c")

    bf = lambda s: s.astype(BF)
    gather_groups = [[bf(jnp.transpose(e_w_in[0])), bf(e_w_uq[0]), bf(e_w_ukv[0])], [bf(e_w_out[0]), bf(mlp_w1[0])],
                     [bf(mlp_w2[0]), bf(o_w_in[0])], [bf(o_w_out[0]), bf(mlp_w1[1])], [bf(mlp_w2[1])]]
    small_rows = jnp.concatenate([o_norm_mix, o_conv_w[0], jnp.zeros((4, d_shard), F32)], axis=0)
    gather_groups[0].insert(0, small_rows)
    started, start_token = gather_start(gather_groups[:1], x0, name="gather_start0")
    started_rest, rest_token = gather_start(gather_groups[1:], start_token, name="gather_start1")
    started += started_rest

    def gathered(gi, after):
        srcs, lands = gather_wait(started[gi], after, name=f"gather_wait{gi}")
        return gather_finish(srcs, lands, name=f"gather_finish{gi}")

    w_tril = jnp.tril(e_sgu_w[0])
    w_tril_b = w_tril.astype(BF)
    w_tril_tb = jnp.swapaxes(w_tril, 1, 2).astype(BF)
    b_full = jnp.repeat(e_sgu_b[0].T, CH, axis=1)
    v_gain = e_v_norm[0].reshape(1, SGU_OUT)
    cos_t, sin_t = _rope_tables(positions[0])
    mlp_gain = [mlp_norm[0:1], mlp_norm[1:2]]
    final_gain = final_norm.reshape(1, D)

    h0 = rms_fwd(x0, e_norm_mix, name="e_norm", deps=[rest_token])
    small_g, g_w_in_t, g_w_uq, w_ukv = gathered(0, [h0, cos_t, sin_t, w_tril_b, w_tril_tb, b_full])
    o_norm_full = small_g[:, 0, :].reshape(1, D)
    conv_w_full = jnp.transpose(small_g[:, 1:4, :], (1, 0, 2)).reshape(3, D)
    w_in_t = _pack_w_in_t(g_w_in_t)
    w_uq = jnp.concatenate([g_w_uq[..., :QK_NOPE], _rope_slab(g_w_uq[..., QK_NOPE:])], axis=-1)
    proj = mm(h0, w_in_t, tb=True, name="e_in", out=((T, w_in_t.shape[0]), F32), bn=_tile(w_in_t.shape[0], 640))
    qn, kvn, krope = mla_prep(proj, e_q_norm, e_kv_norm, cos_t, sin_t, name="mla_prep")
    bm = _tile(T, MM_TILE)

    def q_epi(acc, cos_ref, sin_ref):
        return (jnp.concatenate([acc[:, :QK_NOPE], _rope_fwd(acc[:, QK_NOPE:], cos_ref[...], sin_ref[...])], axis=-1),)

    q = mm(qn, w_uq, name="mla_q", out=((T, HEADS * HEAD_PAD), BF), bm=bm, bn=HEAD_PAD, epi=q_epi,
           epi_ins=[(cos_t, (bm, LANES), lambda i, j, k: (i, 0)), (sin_t, (bm, LANES), lambda i, j, k: (i, 0))])

    def kv_epi(acc, kr_ref):
        return jnp.concatenate([acc[:, :QK_NOPE].astype(BF), kr_ref[...]], axis=-1), acc[:, QK_NOPE:]

    k, v = mm(kvn, w_ukv, name="mla_kv", bm=bm, bn=HEAD_PAD, epi=kv_epi,
              outs=[((T, HEADS * HEAD_PAD), BF, HEAD_PAD), ((T, MLA_OUT), BF, V_HEAD)],
              epi_ins=[(krope, (bm, LANES), lambda i, j, k: (i, 0))])
    attn, attn_lse = attn_fwd(q, k, v, name="attn_fwd")
    mixed = mix_fwd(attn, proj, e_mla_out_norm, e_sgu_out_norm, v_gain, w_tril_b, b_full, name="mix_fwd")
    bn = _tile(D, MM_TILE)
    g_w_out_e, w1_0 = gathered(1, [mixed])
    w_out_e = g_w_out_e.reshape(-1, D)
    x1 = mm(mixed, w_out_e, name="e_out", out=((T, D), F32), bm=bm, bn=bn,
            epi=lambda acc, r: (acc + r[...],), epi_ins=[(x0, (bm, bn), lambda i, j, k: (i, j))])
    hn0, a0, act0 = _mlp_up(x1, mlp_gain[0], w1_0, 0)
    g_w2_0, g_w_in_o = gathered(2, [act0])
    w2_0 = g_w2_0.reshape(-1, D)
    x2 = _mlp_down(x1, act0, w2_0, 0)
    ho = rms_fwd(x2, o_norm_full, name="o_norm")
    proj_o = mm(ho, g_w_in_o, name="o_in", out=((T, 3 * D), F32))
    gated = conv_fwd(proj_o, conv_w_full, name="conv_fwd")
    g_w_out_o, w1_1 = gathered(3, [gated])
    w_out_o = g_w_out_o.reshape(-1, D)
    x3 = mm(gated, w_out_o, name="o_out", out=((T, D), F32), bm=bm, bn=bn,
            epi=lambda acc, r: (acc + r[...],), epi_ins=[(x2, (bm, bn), lambda i, j, k: (i, j))])
    hn1, a1, act1 = _mlp_up(x3, mlp_gain[1], w1_1, 1)
    (g_w2_1,) = gathered(4, [act1])
    w2_1 = g_w2_1.reshape(-1, D)
    x4 = _mlp_down(x3, act1, w2_1, 1)
    w1, w2 = [w1_0, w1_1], [w2_0, w2_1]

    dx4, dx4b, d_final, loss_part = loss_bwd([x4], final_gain, target, name="loss_bwd")

    hosted = dict(job_index=device_index())
    dhid1, dw1_1, dw2_1 = _mlp_bwd_weights(w1[1], w2[1], (hn1, a1, act1), dx4b, 1)
    sib_r0 = _reduce_begin([dw1_1, dw2_1], "r0")
    dhn1 = mm(dhid1, w1[1], tb=True, name="mlp1_dhn", out=((T, D), F32), deps=[sib_r0[-1]])
    grads_r0, a_r0 = sibling_wait(sib_r0, dhn1, name="reduce_sibling_wait_r0")
    dx3, dx3b, d_mlp1 = rms_bwd(x3, mlp_gain[1], dhn1, dres=dx4, name="mlp1_norm_bwd")

    dgated, ((pair_r0a,),) = mm(dx3b, w_out_o, tb=True, name="o_out_dx", out=((T, D), F32),
                                jobs=[pair_job(grads_r0[0], a_r0[0])], **hosted)
    dw_out_o, ((pair_r0b,),) = mm(gated, dx3b, ta=True, name="o_out_dw", out=((D, D), BF),
                                  jobs=[pair_job(grads_r0[1], a_r0[1])], **hosted)
    st_r0 = chips_start([pair_r0a, pair_r0b], name="reduce_chips_start_r0")
    dproj_o, dconv_full = conv_bwd(dgated, proj_o, conv_w_full, name="conv_bwd", deps=[st_r0[-1]])
    dw_in_o = mm(ho, dproj_o, ta=True, name="o_in_dw", out=(g_w_in_o.shape, BF))
    sib_r1 = _reduce_begin([dw_out_o.reshape(g_w_out_o.shape), dw_in_o], "r1")
    dho = mm(dproj_o, g_w_in_o, tb=True, name="o_in_dx", out=((T, D), F32), deps=[sib_r1[-1]])
    grads_r1, a_r1 = sibling_wait(sib_r1, dho, name="reduce_sibling_wait_r1")
    dx2, dx2b, d_onorm_full = rms_bwd(x2, o_norm_full, dho, dres=dx3, name="o_norm_bwd")

    d_ff = a0.shape[1]
    bm_h, bn_h = _tile(T, MM_TILE), _tile(d_ff, min(MM_TILE, w1[0].shape[2]))
    dhid0, ((pair_r1a,), (pair_r1b,)) = mm(
        dx2b, w2[0], tb=True, name="mlp0_dhid", out=((T, d_ff), BF), bm=bm_h, bn=bn_h,
        epi=lambda acc, a_ref: (2.0 * a_ref[...].astype(F32) * acc,),
        epi_ins=[(a0, (bm_h, bn_h), lambda i, j, k: (i, j))],
        jobs=[pair_job(grads_r1[0], a_r1[0]), pair_job(grads_r1[1], a_r1[1])], **hosted)
    st_r1 = chips_start([pair_r1a, pair_r1b], name="reduce_chips_start_r1")
    dw2_0 = mm(act0, dx2b, ta=True, name="mlp0_dw2", out=((d_ff, D), BF), deps=[st_r1[-1]])
    b_r0 = chips_wait(st_r0, dw2_0, name="reduce_chips_wait_r0")
    dw1_0, (r_w1, r_w2) = mm(
        hn0, dhid0, ta=True, name="mlp0_dw1", out=(w1[0].shape, BF),
        jobs=[adam_job(grads_r0[0], a_r0[0], b_r0[0], mlp_w1, m_mlp_w1, v_mlp_w1, 1, None),
              adam_job(grads_r0[1], a_r0[1], b_r0[1], mlp_w2, m_mlp_w2, v_mlp_w2, 1, None)], **hosted)
    sib_r2 = _reduce_begin([dw1_0, dw2_0.reshape(N_DEV, d_ff // N_DEV, D)], "r2")
    dhn0 = mm(dhid0, w1[0], tb=True, name="mlp0_dhn", out=((T, D), F32), deps=[sib_r2[-1]])
    grads_r2, a_r2 = sibling_wait(sib_r2, dhn0, name="reduce_sibling_wait_r2")
    dx1, dx1b, d_mlp0 = rms_bwd(x1, mlp_gain[0], dhn0, dres=dx2, name="mlp0_norm_bwd")

    dmixed, ((pair_r2a,),) = mm(dx1b, w_out_e, tb=True, name="e_out_dx", out=((T, MLA_OUT + SGU_OUT), F32),
                                jobs=[pair_job(grads_r2[0], a_r2[0])], **hosted)
    dw_out_e, ((pair_r2b,),) = mm(mixed, dx1b, ta=True, name="e_out_dw", out=(w_out_e.shape, BF),
                                  jobs=[pair_job(grads_r2[1], a_r2[1])], **hosted)
    st_r2 = chips_start([pair_r2a, pair_r2b], name="reduce_chips_start_r2")
    (dattn, dproj, d_mla_out, d_sgu_out, d_vgain, d_sgu_w, d_b_full) = mix_bwd(
        dmixed, attn, proj, e_mla_out_norm, e_sgu_out_norm, v_gain, w_tril_b, w_tril_tb, b_full, name="mix_bwd",
        deps=[st_r2[-1]])
    b_r1 = chips_wait(st_r1, dattn, name="reduce_chips_wait_r1")
    dq, dk, dv = attn_bwd(q, k, v, attn, attn_lse, dattn, name="attn_bwd")
    dq_lin, dkv_lin, dproj = mla_bwd_prep(dq, dk, dv, cos_t, sin_t, dproj, name="mla_bwd_prep")
    dw_uq_pad = mm(qn, dq_lin, ta=True, name="mla_q_dw", out=(w_uq.shape, BF))
    dw_ukv = mm(kvn, dkv_lin, ta=True, name="mla_kv_dw", out=(w_ukv.shape, BF))
    dw_uq = jnp.concatenate([dw_uq_pad[..., :QK_NOPE], _rope_unslab(dw_uq_pad[..., QK_NOPE:])], axis=-1)
    sib_r2b = _reduce_begin([dw_out_e.reshape(g_w_out_e.shape), dw_uq, dw_ukv], "r2b")
    dqn = mm(dq_lin, w_uq, tb=True, name="mla_q_dx", out=((T, Q_LORA), F32), deps=[sib_r2b[-1]])
    dkvn = mm(dkv_lin, w_ukv, tb=True, name="mla_kv_dx", out=((T, KV_LORA), F32), deps=[sib_r2b[-1]])
    grads_r2b, a_r2b, st_r2b = _reduce_continue(sib_r2b, dkvn, "r2b", hosted["job_index"])
    dproj, d_qnorm = rms_bwd(proj, e_q_norm, dqn, col_block=0, want_f32=False, into=dproj, name="q_norm_bwd",
                             deps=[st_r2b[-1]])
    dproj, d_kvnorm = rms_bwd(proj, e_kv_norm, dkvn, col_block=1, want_f32=False, into=dproj, name="kv_norm_bwd")
    dw_in_t_pad, (r_w_out_o, r_w_in_o) = mm(
        dproj, h0, ta=True, name="e_in_dw", out=(w_in_t.shape, BF), bm=_tile(w_in_t.shape[0], 640),
        jobs=[adam_job(grads_r1[0], a_r1[0], b_r1[0], o_w_out, m_o_w_out, v_o_w_out, 0, None),
              adam_job(grads_r1[1], a_r1[1], b_r1[1], o_w_in, m_o_w_in, v_o_w_in, 0, None)], **hosted)
    dw_in_t = unpack_w_in_t_grad(dw_in_t_pad, name="e_in_dw_unpack")
    sib_r3 = _reduce_begin([dw_in_t], "r3")
    dh0 = mm(dproj, w_in_t, name="e_in_dx", out=((T, D), F32), deps=[sib_r3[-1]])
    grads_r3, a_r3, st_r3 = _reduce_continue(sib_r3, dh0, "r3", hosted["job_index"])
    tok_r3 = st_r3[-1]
    grad_x, d_enorm = rms_bwd(x0, e_norm_mix, dh0, dres=dx1, want_bf=False, name="e_norm_bwd", deps=[tok_r3])
    b_r2 = chips_wait(st_r2, grad_x, name="reduce_chips_wait_r2")

    d_sgu_b = jnp.transpose(d_b_full[:, ::CH])
    d_sgu_w_tril = jnp.tril(d_sgu_w)
    rep = [("e_norm_mix", e_norm_mix, m_e_norm_mix, v_e_norm_mix, d_enorm),
           ("e_q_norm", e_q_norm, m_e_q_norm, v_e_q_norm, d_qnorm),
           ("e_kv_norm", e_kv_norm, m_e_kv_norm, v_e_kv_norm, d_kvnorm),
           ("e_v_norm", e_v_norm, m_e_v_norm, v_e_v_norm, d_vgain),
           ("e_sgu_w", e_sgu_w, m_e_sgu_w, v_e_sgu_w, d_sgu_w_tril),
           ("e_sgu_b", e_sgu_b, m_e_sgu_b, v_e_sgu_b, d_sgu_b),
           ("e_mla_out_norm", e_mla_out_norm, m_e_mla_out_norm, v_e_mla_out_norm, d_mla_out),
           ("e_sgu_out_norm", e_sgu_out_norm, m_e_sgu_out_norm, v_e_sgu_out_norm, d_sgu_out),
           ("mlp_norm", mlp_norm, m_mlp_norm, v_mlp_norm, jnp.concatenate([d_mlp0, d_mlp1], axis=0)),
           ("final_norm", final_norm, m_final_norm, v_final_norm, d_final)]
    sizes = [int(np.prod(r[1].shape)) for r in rep]
    n_rep = sum(sizes)
    n_all = n_rep + 4 * D + 1
    width = -(-n_all // (8 * LANES)) * LANES
    pad = 8 * width - n_all
    flat = jnp.concatenate([r[4].reshape(-1) for r in rep]
                           + [d_onorm_full.reshape(-1), dconv_full.reshape(-1), loss_part[0, :1],
                              jnp.zeros((pad,), F32)])
    small_started, small_token = gather_start([[flat.reshape(8, width)]], b_r2[0], name="gather_small_grads_start")

    def finish(grads, a_bufs, b_bufs, t, w, m, v, layer=0, prev=None, tag="", deps=()):
        return run_job(adam_job(grads[t], a_bufs[t], b_bufs[t], w, m, v, layer, prev), index=hosted["job_index"],
                       name=f"adam_{tag}", deps=deps)

    r_w1 = finish(grads_r2, a_r2, b_r2, 0, mlp_w1, m_mlp_w1, v_mlp_w1, 0, r_w1, tag="w1_l0", deps=[tok_r3, small_token])
    r_w2 = finish(grads_r2, a_r2, b_r2, 1, mlp_w2, m_mlp_w2, v_mlp_w2, 0, r_w2, tag="w2_l0", deps=[r_w1[1]])
    b_r2b = chips_wait(st_r2b, r_w2[1], name="reduce_chips_wait_r2b")
    r_w_out_e = finish(grads_r2b, a_r2b, b_r2b, 0, e_w_out, m_e_w_out, v_e_w_out, tag="e_w_out")
    r_w_uq = finish(grads_r2b, a_r2b, b_r2b, 1, e_w_uq, m_e_w_uq, v_e_w_uq, tag="e_w_uq")
    r_w_ukv = finish(grads_r2b, a_r2b, b_r2b, 2, e_w_ukv, m_e_w_ukv, v_e_w_ukv, tag="e_w_ukv")
    b_r3 = chips_wait(st_r3, r_w_out_e[1], name="reduce_chips_wait_r3")
    g_w_in_t = reduce_sum(grads_r3[0], a_r3[0], b_r3[0], name="sum_e_w_in")
    w_in_upd_t = adam_rows(g_w_in_t, jnp.transpose(e_w_in[0]), jnp.transpose(m_e_w_in[0]), jnp.transpose(v_e_w_in[0]),
                           name="adam_e_w_in")
    r_w_in = [jnp.transpose(t)[None] for t in (g_w_in_t, *w_in_upd_t)]

    small_srcs, small_lands = gather_wait(small_started[0], [r_w2[1]], name="gather_small_grads_wait")
    small_all = gather_finish(small_srcs, small_lands, name="gather_small_grads_finish")[0]
    summed = sum_rows8(small_all.reshape(N_DEV * 8, width), 8, name="sum_small_grads").reshape(-1)

    loss = summed[n_rep + 4 * D]

    def pack_rep(i):
        return jnp.concatenate([r[i].reshape(-1) for r in rep]).reshape(n_rep // LANES, LANES)

    g_rep = summed[:n_rep].reshape(n_rep // LANES, LANES)
    d_rep, nm_rep, nv_rep = adam_flat(g_rep, pack_rep(1), pack_rep(2), pack_rep(3), name="adam_replicated")

    def unpack_rep(flat2d):
        out, off = {}, 0
        f = flat2d.reshape(-1)
        for r, n in zip(rep, sizes):
            out[r[0]] = f[off:off + n].reshape(r[1].shape)
            off += n
        return out

    small = {"grad": unpack_rep(g_rep), "delta": unpack_rep(d_rep), "new_m": unpack_rep(nm_rep),
             "new_v": unpack_rep(nv_rep)}
    g_onorm = lax.dynamic_slice(summed[n_rep:n_rep + D].reshape(1, D), (0, me * d_shard), (1, d_shard))
    g_conv = lax.dynamic_slice(summed[n_rep + D:n_rep + 4 * D].reshape(3, D), (0, me * d_shard), (3, d_shard))

    def pack_sharded(norm_part, conv_part):
        return jnp.concatenate([norm_part, conv_part, jnp.zeros((4, d_shard), F32)], axis=0)

    g_sh = pack_sharded(g_onorm, g_conv)
    d_sh, nm_sh, nv_sh = adam_flat(g_sh, pack_sharded(o_norm_mix, o_conv_w[0]), pack_sharded(m_o_norm_mix, m_o_conv_w[0]),
                                   pack_sharded(v_o_norm_mix, v_o_conv_w[0]), name="adam_sharded_small")
    for kind, arr in (("grad", g_sh), ("delta", d_sh), ("new_m", nm_sh), ("new_v", nv_sh)):
        small[kind]["o_norm_mix"] = arr[0:1]
        small[kind]["o_conv_w"] = arr[1:4][None]

    big = {"e_w_in": r_w_in, "e_w_uq": r_w_uq, "e_w_ukv": r_w_ukv, "e_w_out": r_w_out_e, "o_w_in": r_w_in_o,
           "o_w_out": r_w_out_o, "mlp_w1": r_w1, "mlp_w2": r_w2}
    order = ["e_norm_mix", "e_w_in", "e_q_norm", "e_w_uq", "e_kv_norm", "e_w_ukv", "e_v_norm", "e_sgu_w", "e_sgu_b",
             "e_mla_out_norm", "e_sgu_out_norm", "e_w_out", "o_norm_mix", "o_w_in", "o_conv_w", "o_w_out", "mlp_norm",
             "mlp_w1", "mlp_w2", "final_norm"]
    result = [loss, grad_x[None]]
    for ki, kind in enumerate(("grad", "delta", "new_m", "new_v")):
        for nm in order:
            result.append(big[nm][ki] if nm in big else small[kind][nm])
    return tuple(result)
```

```python
import numpy as np
import jax
import jax.numpy as jnp
from jax import lax
from jax.experimental import pallas as pl
from jax.experimental.pallas import tpu as pltpu

BF = jnp.bfloat16
F32 = jnp.float32
MESH = pl.DeviceIdType.MESH
N_DEV = 8

EPS = 1e-6
HEADS = 8
Q_LORA = 512
KV_LORA = 512
QK_NOPE = 128
QK_ROPE = 64
HALF_ROPE = QK_ROPE // 2
V_HEAD = 128
HEAD_PAD = 256
ROPE_BASE = 10000.0
GROUPS = 8
CH = 128
CHUNK = 128
SGU_OUT = GROUPS * CH
MLA_OUT = HEADS * V_HEAD
ATTN_SCALE = float((QK_NOPE + QK_ROPE) ** -0.5)

ADAM_LR = 0.001
ADAM_B1 = 0.9
ADAM_B2 = 0.999
ADAM_EPS = 1e-08
ADAM_WD = 0.01
ADAM_STEP = 10
ADAM_C1 = 1.0 - ADAM_B1 ** ADAM_STEP
ADAM_C2 = 1.0 - ADAM_B2 ** ADAM_STEP

V7X_VMEM_BYTES = 64 * 2 ** 20
VMEM_LIMIT_CAP = V7X_VMEM_BYTES - 6 * 2 ** 20
LANES = 128
ROW_TILE = 256
ATTN_TILE = 512
STREAM_BLOCK_ELEMS = 512 * 1024
MM_TILE = 1024
MM_K_TILE = 2048
MM_K_BLOCK_MAX = 4096


def _padded_bytes(block, dtype):
    dims = [d for d in block if d is not None]
    if len(dims) >= 1:
        dims[-1] = -(-dims[-1] // LANES) * LANES
    if len(dims) >= 2:
        dims[-2] = -(-dims[-2] // 16) * 16
    return int(np.prod(dims)) * jnp.dtype(dtype).itemsize


def _pcall(body, *, name, grid, ins, outs, scratch=(), semantics=None, aliases=None, prefetch=None, deps=()):
    any_spec = pl.BlockSpec(memory_space=pl.ANY)
    if deps:
        n_lead = len(ins) + (1 if prefetch is not None else 0)
        n_deps = len(deps)
        inner = body

        def body(*refs):
            inner(*refs[:n_lead], *refs[n_lead + n_deps:])

        ins = list(ins) + [(d, None, None) for d in deps]
    in_specs = [any_spec if b is None else pl.BlockSpec(b, m) for _, b, m in ins]
    out_specs = [any_spec if b is None else pl.BlockSpec(b, m) for _, _, b, m in outs]
    out_shape = [pltpu.HBM(s, d) for s, d, _, _ in outs]
    est = 0
    for a, b, _ in ins:
        if b is not None:
            est += 2 * _padded_bytes(b, a.dtype)
    for _, d, b, _ in outs:
        if b is not None:
            est += 2 * _padded_bytes(b, d)
    for s in scratch:
        if hasattr(s, "shape") and hasattr(s, "dtype"):
            est += _padded_bytes(s.shape, s.dtype)
    limit = int(min(VMEM_LIMIT_CAP, est + 16 * 2 ** 20))
    params = pltpu.CompilerParams(
        dimension_semantics=semantics or ("arbitrary",) * len(grid), vmem_limit_bytes=limit)
    args = [pltpu.with_memory_space_constraint(a, pltpu.HBM) for a, _, _ in ins]
    if prefetch is not None:
        grid_spec = pltpu.PrefetchScalarGridSpec(
            num_scalar_prefetch=1, grid=grid, in_specs=in_specs, out_specs=out_specs, scratch_shapes=list(scratch))
        call = pl.pallas_call(body, out_shape=out_shape, grid_spec=grid_spec, name=name, compiler_params=params,
                              input_output_aliases=aliases or {})
        return call(prefetch, *args)
    call = pl.pallas_call(body, out_shape=out_shape, grid=grid, in_specs=in_specs, out_specs=out_specs,
                          scratch_shapes=list(scratch), name=name, compiler_params=params,
                          input_output_aliases=aliases or {})
    return call(*args)


def _tile(dim, pref, quantum=LANES):
    if dim <= pref:
        return dim
    t = (pref // quantum) * quantum
    while t >= quantum:
        if dim % t == 0:
            return t
        t -= quantum
    return dim


def _vshape(arr_shape):
    if len(arr_shape) == 2:
        return tuple(arr_shape)
    s, r, c = arr_shape
    return (r, s * c)


def _vblock(arr_shape, br, bc, rc):
    if len(arr_shape) == 2:
        return (br, bc), (lambda *g: rc(*g))
    _, _, c = arr_shape
    assert c % bc == 0, (arr_shape, bc)
    per = c // bc

    def imap(*g):
        ri, ci = rc(*g)
        return (ci // per, ri, ci % per)

    return (None, br, bc), imap


def _shard_width(*shapes):
    w = None
    for s in shapes:
        if len(s) == 3:
            w = s[2] if w is None else int(np.gcd(w, s[2]))
    return w


def mm(a, b, *, name, ta=False, tb=False, out=None, outs=None, epi=None, epi_ins=(), bm=None, bn=None, bk=None,
       deps=(), jobs=(), job_index=None):
    av, bv = _vshape(a.shape), _vshape(b.shape)
    M, K = (av[1], av[0]) if ta else av
    K2, N = (bv[1], bv[0]) if tb else bv
    assert K == K2, (a.shape, b.shape, ta, tb)
    if outs is None:
        outs = [(out[0], out[1], None)]
    a_sw = _shard_width(a.shape)
    b_sw = _shard_width(b.shape)
    o_sw = _shard_width(*[o[0] for o in outs])
    m_lim = a_sw if (ta and a_sw) else None
    k_lim = [w for w in ((a_sw if not ta else None), (b_sw if tb else None)) if w]
    n_lim = [w for w in ((b_sw if not tb else None), o_sw) if w]
    if bm is None:
        bm = _tile(M, min([MM_TILE] + ([m_lim] if m_lim else [])))
    if bn is None:
        bn = _tile(N, min([MM_TILE] + n_lim))
    k_shards = 0
    if tb and len(b.shape) == 3 and bk is None and not (a_sw and not ta):
        k_shards = 1
        while 2 * k_shards <= b.shape[0] and 2 * k_shards * b_sw <= MM_K_BLOCK_MAX:
            k_shards *= 2
        bk = k_shards * b_sw
    if bk is None:
        bk = K if (K <= 4096 and not k_lim) else _tile(K, min([MM_K_TILE] + k_lim))
    assert M % bm == 0 and N % bn == 0 and K % bk == 0, (name, M, N, K, bm, bn, bk)
    nk = K // bk
    grid = (M // bm, N // bn, nk)
    if ta:
        a_blk, a_map = _vblock(a.shape, bk, bm, lambda i, j, k: (k, i))
    else:
        a_blk, a_map = _vblock(a.shape, bm, bk, lambda i, j, k: (i, k))
    if k_shards:
        b_blk, b_map = (k_shards, bn, b_sw), (lambda i, j, k: (k, j, 0))
    elif tb:
        b_blk, b_map = _vblock(b.shape, bn, bk, lambda i, j, k: (j, k))
    else:
        b_blk, b_map = _vblock(b.shape, bk, bn, lambda i, j, k: (k, j))
    dn = (((0 if ta else 1,), (1 if tb else 0,)), ((), ()))
    ins = [(a, a_blk, a_map), (b, b_blk, b_map)] + list(epi_ins)
    out_list = []
    for shape, dtype, cols in outs:
        cols = cols or bn
        blk, imap = _vblock(shape, bm, cols, lambda i, j, k: (i, j))
        out_list.append((shape, dtype, blk, imap))
    n_e, n_o = len(epi_ins), len(out_list)

    n_steps = grid[0] * grid[1] * nk
    built = [job(n_steps) for job in jobs]
    aliases = {}
    job_slices = []
    if built:
        def lin(i, j, k):
            return (i * grid[1] + j) * nk + k

        ins = [(arr, blk, None if blk is None else (lambda i, j, k, s, f=f: f(i, j, k))) for arr, blk, f in ins]
        out_list = [(sh, dt, blk, (lambda i, j, k, s, f=f: f(i, j, k))) for sh, dt, blk, f in out_list]
        n_main_in, n_main_out = len(ins), len(out_list)
        for jb in built:
            i0, o0 = len(ins), len(out_list)
            ins += [(arr, blk, None if blk is None else (lambda i, j, k, s, f=f: f(lin(i, j, k), s)))
                    for arr, blk, f in jb["ins"]]
            out_list += [(sh, dt, blk, (lambda i, j, k, s, f=f: f(lin(i, j, k), s))) for sh, dt, blk, f in jb["outs"]]
            aliases.update({1 + i0 + ai: o0 + ao for ai, ao in jb["aliases"].items()})
            job_slices.append((i0, len(jb["ins"]), o0, len(jb["outs"])))
    n_in_total = len(ins)

    def body(*refs):
        if built:
            refs = refs[1:]
        a_ref, b_ref = refs[0], refs[1]
        e_refs = refs[2:2 + n_e]
        o_refs = refs[n_in_total:n_in_total + n_o]
        for jb, (i0, ni, o0, no) in zip(built, job_slices):
            jb["fn"](refs[i0:i0 + ni], refs[n_in_total + o0:n_in_total + o0 + no])

        def finish(acc):
            res = epi(acc, *e_refs) if epi is not None else (acc,)
            for o_ref, r in zip(o_refs, res):
                o_ref[...] = r.astype(o_ref.dtype)

        x = a_ref[...].astype(BF)
        y = b_ref[...].astype(BF)
        if k_shards:
            p = None
            for s in range(k_shards):
                part = lax.dot_general(x[:, s * b_sw:(s + 1) * b_sw], y[s], dn, preferred_element_type=F32)
                p = part if p is None else p + part
        else:
            p = lax.dot_general(x, y, dn, preferred_element_type=F32)
        if nk == 1:
            finish(p)
        else:
            acc_ref = refs[-1]
            k = pl.program_id(2)

            @pl.when(k == 0)
            def _():
                acc_ref[...] = p

            @pl.when(k > 0)
            def _():
                acc_ref[...] += p

            @pl.when(k == nk - 1)
            def _():
                finish(acc_ref[...])

    scratch = [pltpu.VMEM((bm, bn), F32)] if nk > 1 else []
    res = _pcall(body, name=name, grid=grid, ins=ins, outs=out_list, scratch=scratch, deps=deps,
                 semantics=("parallel", "parallel", "arbitrary"), prefetch=job_index if built else None, aliases=aliases)
    main = res[0] if n_o == 1 else res[:n_o]
    if not built:
        return main
    return main, [res[o0:o0 + no] for _, _, o0, no in job_slices]


_GELU_K = float(np.sqrt(2.0 / np.pi))
_GELU_C = 0.044715


def _gelu(x):
    t = jnp.tanh(_GELU_K * (x + _GELU_C * (x * x * x)))
    return 0.5 * x * (1.0 + t)


def _gelu_grad(x):
    t = jnp.tanh(_GELU_K * (x + _GELU_C * (x * x * x)))
    return 0.5 * (1.0 + t) + 0.5 * x * (1.0 - t * t) * (_GELU_K * (1.0 + 3.0 * _GELU_C * (x * x)))


def _rstd(x):
    return lax.rsqrt(jnp.mean(x * x, axis=-1, keepdims=True) + EPS)


def _rms_bwd(x, gain, dy):
    r = _rstd(x)
    xh = x * r
    gdy = dy * gain
    dx = r * (gdy - xh * jnp.mean(gdy * xh, axis=-1, keepdims=True))
    return dx, dy * xh


def _rope_fwd(x, cos_t, sin_t):
    return x * cos_t + pltpu.roll(x, 2 * HALF_ROPE, 1) * sin_t


def _rope_bwd(dy, cos_t, sin_t):
    return dy * cos_t + pltpu.roll(dy * sin_t, 2 * HALF_ROPE, 1)


def _acc_rows(ref, val, first):
    s = jnp.sum(val, axis=0, keepdims=True)

    @pl.when(first)
    def _():
        ref[...] = s

    @pl.when(jnp.logical_not(first))
    def _():
        ref[...] += s


def rms_fwd(x, gain, *, name, col_block=0, width=None, deps=()):
    T = x.shape[0]
    width = width or x.shape[1]
    tm = _tile(T, ROW_TILE, 8)

    def body(x_ref, g_ref, o_ref):
        v = x_ref[...]
        o_ref[...] = (v * _rstd(v) * g_ref[...]).astype(BF)

    return _pcall(body, name=name, grid=(T // tm,),
                  ins=[(x, (tm, width), lambda i: (i, col_block)), (gain, (1, width), lambda i: (0, 0))],
                  outs=[((T, width), BF, (tm, width), lambda i: (i, 0))], semantics=("parallel",), deps=deps)[0]


def rms_bwd(x, gain, dy, *, name, col_block=0, dres=None, want_f32=True, want_bf=True, into=None, deps=()):
    T, width = dy.shape
    tm = _tile(T, ROW_TILE, 8)
    has_res = dres is not None

    def body(*refs):
        x_ref, g_ref, dy_ref = refs[:3]
        pos = 3
        res_ref = None
        if has_res:
            res_ref = refs[pos]
            pos += 1
        if into is not None:
            pos += 1
        outs = refs[pos:]
        dx, dg_rows = _rms_bwd(x_ref[...], g_ref[...], dy_ref[...])
        if has_res:
            dx = dx + res_ref[...]
        o = 0
        if want_f32:
            outs[o][...] = dx
            o += 1
        if want_bf:
            outs[o][...] = dx.astype(BF)
            o += 1
        _acc_rows(outs[o], dg_rows, pl.program_id(0) == 0)

    ins = [(x, (tm, width), lambda i: (i, col_block)), (gain, (1, width), lambda i: (0, 0)),
           (dy, (tm, width), lambda i: (i, 0))]
    if has_res:
        ins.append((dres, (tm, width), lambda i: (i, 0)))
    outs = []
    aliases = {}
    if want_f32:
        outs.append(((T, width), F32, (tm, width), lambda i: (i, 0)))
    if want_bf and into is not None:
        ins.append((into, None, None))
        aliases[len(ins) - 1] = len(outs)
        outs.append((into.shape, BF, (tm, width), lambda i: (i, col_block)))
    elif want_bf:
        outs.append(((T, width), BF, (tm, width), lambda i: (i, 0)))
    outs.append(((1, width), F32, (1, width), lambda i: (0, 0)))
    return _pcall(body, name=name, grid=(T // tm,), ins=ins, outs=outs, aliases=aliases, deps=deps)


def mla_prep(proj, q_norm, kv_norm, cos_t, sin_t, *, name):
    T = proj.shape[0]
    tm = _tile(T, ROW_TILE, 8)
    kr_block = (proj.shape[1] - LANES) // LANES

    def body(cq_ref, ckv_ref, kr_ref, qg_ref, kg_ref, cos_ref, sin_ref, qn_ref, kvn_ref, krope_ref):
        cq = cq_ref[...]
        qn_ref[...] = (cq * _rstd(cq) * qg_ref[...]).astype(BF)
        ckv = ckv_ref[...]
        kvn_ref[...] = (ckv * _rstd(ckv) * kg_ref[...]).astype(BF)
        krope_ref[...] = _rope_fwd(kr_ref[...], cos_ref[...], sin_ref[...]).astype(BF)

    return _pcall(
        body, name=name, grid=(T // tm,),
        ins=[(proj, (tm, Q_LORA), lambda i: (i, 0)), (proj, (tm, KV_LORA), lambda i: (i, 1)),
             (proj, (tm, LANES), lambda i: (i, kr_block)),
             (q_norm, (1, Q_LORA), lambda i: (0, 0)), (kv_norm, (1, KV_LORA), lambda i: (0, 0)),
             (cos_t, (tm, LANES), lambda i: (i, 0)), (sin_t, (tm, LANES), lambda i: (i, 0))],
        outs=[((T, Q_LORA), BF, (tm, Q_LORA), lambda i: (i, 0)), ((T, KV_LORA), BF, (tm, KV_LORA), lambda i: (i, 0)),
              ((T, LANES), BF, (tm, LANES), lambda i: (i, 0))],
        semantics=("parallel",))


def _attn_scores(q, k_blk, diagonal):
    s = lax.dot_general(q, k_blk, (((1,), (1,)), ((), ())), preferred_element_type=F32) * ATTN_SCALE
    if diagonal:
        row = lax.broadcasted_iota(jnp.int32, s.shape, 0)
        col = lax.broadcasted_iota(jnp.int32, s.shape, 1)
        s = jnp.where(col <= row, s, -jnp.inf)
    return s


def attn_fwd(q, k, v, *, name):
    T = q.shape[0]
    tq = _tile(T, ATTN_TILE, 8)

    def body(q_ref, k_ref, v_ref, o_ref, lse_ref):
        i = pl.program_id(1)
        qv = q_ref[...]

        def block(kb, carry, diagonal):
            m, l, acc = carry
            start = pl.multiple_of(kb * tq, tq)
            s = _attn_scores(qv, k_ref[pl.ds(start, tq), :], diagonal)
            m_new = jnp.maximum(m, jnp.max(s, axis=-1, keepdims=True))
            alpha = jnp.exp(m - m_new)
            p = jnp.exp(s - m_new)
            l = alpha * l + jnp.sum(p, axis=-1, keepdims=True)
            acc = alpha * acc + jnp.dot(p.astype(BF), v_ref[pl.ds(start, tq), :], preferred_element_type=F32)
            return m_new, l, acc

        init = (jnp.full((tq, 1), -jnp.inf, F32), jnp.zeros((tq, 1), F32), jnp.zeros((tq, V_HEAD), F32))
        carry = lax.fori_loop(0, i, lambda kb, c: block(kb, c, False), init)
        m, l, acc = block(i, carry, True)
        o_ref[...] = acc / l
        lse_ref[...] = jnp.broadcast_to(m + jnp.log(l), (tq, V_HEAD))

    return _pcall(
        body, name=name, grid=(HEADS, T // tq),
        ins=[(q, (tq, HEAD_PAD), lambda h, i: (i, h)), (k, (T, HEAD_PAD), lambda h, i: (0, h)),
             (v, (T, V_HEAD), lambda h, i: (0, h))],
        outs=[((T, MLA_OUT), F32, (tq, V_HEAD), lambda h, i: (i, h)),
              ((T, MLA_OUT), F32, (tq, V_HEAD), lambda h, i: (i, h))], semantics=("parallel", "parallel"))


def attn_bwd(q, k, v, o, lse, do, *, name):
    T = q.shape[0]
    tq = _tile(T, ATTN_TILE, 8)

    def body(q_ref, k_ref, v_ref, o_ref, lse_ref, do_ref, dq_ref, dk_ref, dv_ref):
        i = pl.program_id(1)

        @pl.when(i == 0)
        def _():
            dk_ref[...] = jnp.zeros_like(dk_ref)
            dv_ref[...] = jnp.zeros_like(dv_ref)

        qv = q_ref[...]
        do_t = do_ref[...]
        lse_v = lse_ref[:, 0:1]
        delta = jnp.sum(do_t.astype(F32) * o_ref[...], axis=-1, keepdims=True)

        def block(kb, dq, diagonal):
            start = pl.multiple_of(kb * tq, tq)
            k_blk = k_ref[pl.ds(start, tq), :]
            v_blk = v_ref[pl.ds(start, tq), :]
            p = jnp.exp(_attn_scores(qv, k_blk, diagonal) - lse_v)
            dp = lax.dot_general(do_t, v_blk, (((1,), (1,)), ((), ())), preferred_element_type=F32)
            ds = (p * (dp - delta) * ATTN_SCALE).astype(BF)
            dk_ref[pl.ds(start, tq), :] += lax.dot_general(ds, qv, (((0,), (0,)), ((), ())), preferred_element_type=F32)
            dv_ref[pl.ds(start, tq), :] += lax.dot_general(p.astype(BF), do_t, (((0,), (0,)), ((), ())),
                                                          preferred_element_type=F32)
            return dq + jnp.dot(ds, k_blk, preferred_element_type=F32)

        dq = lax.fori_loop(0, i, lambda kb, c: block(kb, c, False), jnp.zeros((tq, HEAD_PAD), F32))
        dq_ref[...] = block(i, dq, True)

    return _pcall(
        body, name=name, grid=(HEADS, T // tq),
        ins=[(q, (tq, HEAD_PAD), lambda h, i: (i, h)), (k, (T, HEAD_PAD), lambda h, i: (0, h)),
             (v, (T, V_HEAD), lambda h, i: (0, h)), (o, (tq, V_HEAD), lambda h, i: (i, h)),
             (lse, (tq, V_HEAD), lambda h, i: (i, h)), (do, (tq, V_HEAD), lambda h, i: (i, h))],
        outs=[((T, HEADS * HEAD_PAD), F32, (tq, HEAD_PAD), lambda h, i: (i, h)),
              ((T, HEADS * HEAD_PAD), F32, (T, HEAD_PAD), lambda h, i: (0, h)),
              ((T, MLA_OUT), F32, (T, V_HEAD), lambda h, i: (0, h))],
        semantics=("parallel", "arbitrary"))


def mla_bwd_prep(dq, dk, dv, cos_t, sin_t, dproj, *, name):
    T = dq.shape[0]
    tm = _tile(T, ROW_TILE, 8)
    kr_block = (dproj.shape[1] - LANES) // LANES

    def body(dq_ref, dk_ref, dv_ref, cos_ref, sin_ref, dproj_in, dql_ref, dkvl_ref, dkr_ref):
        cos_v, sin_v = cos_ref[...], sin_ref[...]
        kr = jnp.zeros((tm, LANES), F32)
        for h in range(HEADS):
            lo = h * HEAD_PAD
            dql_ref[:, lo:lo + QK_NOPE] = dq_ref[:, lo:lo + QK_NOPE].astype(BF)
            dql_ref[:, lo + QK_NOPE:lo + HEAD_PAD] = _rope_bwd(
                dq_ref[:, lo + QK_NOPE:lo + HEAD_PAD], cos_v, sin_v).astype(BF)
            dkvl_ref[:, lo:lo + QK_NOPE] = dk_ref[:, lo:lo + QK_NOPE].astype(BF)
            dkvl_ref[:, lo + QK_NOPE:lo + HEAD_PAD] = dv_ref[:, h * V_HEAD:(h + 1) * V_HEAD].astype(BF)
            kr = kr + dk_ref[:, lo + QK_NOPE:lo + HEAD_PAD]
        dkr_ref[...] = _rope_bwd(kr, cos_v, sin_v).astype(BF)

    W = HEADS * HEAD_PAD
    return _pcall(
        body, name=name, grid=(T // tm,),
        ins=[(dq, (tm, W), lambda i: (i, 0)), (dk, (tm, W), lambda i: (i, 0)), (dv, (tm, MLA_OUT), lambda i: (i, 0)),
             (cos_t, (tm, LANES), lambda i: (i, 0)), (sin_t, (tm, LANES), lambda i: (i, 0)), (dproj, None, None)],
        outs=[((T, W), BF, (tm, W), lambda i: (i, 0)), ((T, W), BF, (tm, W), lambda i: (i, 0)),
              (dproj.shape, BF, (tm, LANES), lambda i: (i, kr_block))],
        aliases={5: 2}, semantics=("parallel",))


def _group_norm_stats(vg):
    mu = jnp.mean(vg, axis=-1, keepdims=True)
    d = vg - mu
    r = lax.rsqrt(jnp.mean(d * d, axis=-1, keepdims=True) + EPS)
    return d * r, r


def mix_fwd(a, proj, g_mla, g_sgu, v_gain, w_tril, b_full, *, name):
    T = a.shape[0]
    tm = _tile(T, ROW_TILE, CHUNK)
    n_chunk = tm // CHUNK

    def body(a_ref, u_ref, v_ref, gm_ref, gs_ref, vg_ref, w_ref, b_ref, o_ref, s_scr):
        av = a_ref[...]
        o_ref[:, :MLA_OUT] = (av * _rstd(av) * gm_ref[...]).astype(BF)
        for g in range(GROUPS):
            sl = slice(g * CH, (g + 1) * CH)
            vhat, _ = _group_norm_stats(_gelu(v_ref[:, sl]))
            vn = (vhat * vg_ref[:, sl]).astype(BF)
            u = _gelu(u_ref[:, sl])
            for ci in range(n_chunk):
                rs = slice(ci * CHUNK, (ci + 1) * CHUNK)
                y = jnp.dot(w_ref[g], vn[rs], preferred_element_type=F32) + b_ref[:, sl]
                s_scr[rs, sl] = u[rs] * y
        s = s_scr[...]
        o_ref[:, MLA_OUT:] = (s * _rstd(s) * gs_ref[...]).astype(BF)

    return _pcall(
        body, name=name, grid=(T // tm,),
        ins=[(a, (tm, MLA_OUT), lambda i: (i, 0)), (proj, (tm, SGU_OUT), lambda i: (i, 1)),
             (proj, (tm, SGU_OUT), lambda i: (i, 2)), (g_mla, (1, MLA_OUT), lambda i: (0, 0)),
             (g_sgu, (1, SGU_OUT), lambda i: (0, 0)), (v_gain, (1, SGU_OUT), lambda i: (0, 0)),
             (w_tril, (GROUPS, CHUNK, CHUNK), lambda i: (0, 0, 0)), (b_full, (CHUNK, SGU_OUT), lambda i: (0, 0))],
        outs=[((T, MLA_OUT + SGU_OUT), BF, (tm, MLA_OUT + SGU_OUT), lambda i: (i, 0))],
        scratch=[pltpu.VMEM((tm, SGU_OUT), F32)], semantics=("parallel",))[0]


def mix_bwd(dmixed, a, proj, g_mla, g_sgu, v_gain, w_tril, w_tril_t, b_full, *, name, deps=()):
    T = a.shape[0]
    tm = _tile(T, ROW_TILE, CHUNK)
    n_chunk = tm // CHUNK
    uv0 = Q_LORA + KV_LORA

    def body(dm_a_ref, dm_s_ref, a_ref, u_ref, v_ref, gm_ref, gs_ref, vg_ref, w_ref, wt_ref, b_ref,
             da_ref, duv_ref, dgm_ref, dgs_ref, dvg_ref, dw_ref, db_ref, s_scr, y_scr):
        first = pl.program_id(0) == 0
        duv_ref[:, :uv0] = jnp.zeros((tm, uv0), BF)
        duv_ref[:, uv0 + 2 * SGU_OUT:] = jnp.zeros((tm, duv_ref.shape[1] - uv0 - 2 * SGU_OUT), BF)
        da, dgm_rows = _rms_bwd(a_ref[...], gm_ref[...], dm_a_ref[...])
        da_ref[...] = da.astype(BF)
        _acc_rows(dgm_ref, dgm_rows, first)

        for g in range(GROUPS):
            sl = slice(g * CH, (g + 1) * CH)
            vhat, _ = _group_norm_stats(_gelu(v_ref[:, sl]))
            vn = (vhat * vg_ref[:, sl]).astype(BF)
            u = _gelu(u_ref[:, sl])
            for ci in range(n_chunk):
                rs = slice(ci * CHUNK, (ci + 1) * CHUNK)
                y = jnp.dot(w_ref[g], vn[rs], preferred_element_type=F32) + b_ref[:, sl]
                y_scr[rs, sl] = y
                s_scr[rs, sl] = u[rs] * y
        ds, dgs_rows = _rms_bwd(s_scr[...], gs_ref[...], dm_s_ref[...])
        _acc_rows(dgs_ref, dgs_rows, first)
        s_scr[...] = ds

        @pl.when(first)
        def _():
            dw_ref[...] = jnp.zeros_like(dw_ref)
            db_ref[...] = jnp.zeros_like(db_ref)

        for g in range(GROUPS):
            sl = slice(g * CH, (g + 1) * CH)
            upre = u_ref[:, sl]
            vpre = v_ref[:, sl]
            u = _gelu(upre)
            vhat, r = _group_norm_stats(_gelu(vpre))
            gain = vg_ref[:, sl]
            vn = (vhat * gain).astype(BF)
            dsg = s_scr[:, sl]
            duv_ref[:, uv0 + g * CH:uv0 + (g + 1) * CH] = (dsg * y_scr[:, sl] * _gelu_grad(upre)).astype(BF)
            dy = dsg * u
            dyb = dy.astype(BF)
            dvn_parts = []
            for ci in range(n_chunk):
                rs = slice(ci * CHUNK, (ci + 1) * CHUNK)
                dvn_parts.append(jnp.dot(wt_ref[g], dyb[rs], preferred_element_type=F32))
                dw_ref[g] += lax.dot_general(dyb[rs], vn[rs], (((1,), (1,)), ((), ())), preferred_element_type=F32)
                db_ref[:, sl] += jnp.broadcast_to(jnp.sum(dy[rs], axis=-1, keepdims=True), (CHUNK, CH))
            dvn = dvn_parts[0] if n_chunk == 1 else jnp.concatenate(dvn_parts, axis=0)
            _acc_rows(dvg_ref.at[:, sl], dvn * vhat, first)
            dvh = dvn * gain
            dvg = r * (dvh - jnp.mean(dvh, axis=-1, keepdims=True)
                       - vhat * jnp.mean(dvh * vhat, axis=-1, keepdims=True))
            duv_ref[:, uv0 + SGU_OUT + g * CH:uv0 + SGU_OUT + (g + 1) * CH] = (dvg * _gelu_grad(vpre)).astype(BF)

    return _pcall(
        body, name=name, grid=(T // tm,),
        ins=[(dmixed, (tm, MLA_OUT), lambda i: (i, 0)), (dmixed, (tm, SGU_OUT), lambda i: (i, 1)),
             (a, (tm, MLA_OUT), lambda i: (i, 0)), (proj, (tm, SGU_OUT), lambda i: (i, 1)),
             (proj, (tm, SGU_OUT), lambda i: (i, 2)), (g_mla, (1, MLA_OUT), lambda i: (0, 0)),
             (g_sgu, (1, SGU_OUT), lambda i: (0, 0)), (v_gain, (1, SGU_OUT), lambda i: (0, 0)),
             (w_tril, (GROUPS, CHUNK, CHUNK), lambda i: (0, 0, 0)), (w_tril_t, (GROUPS, CHUNK, CHUNK), lambda i: (0, 0, 0)),
             (b_full, (CHUNK, SGU_OUT), lambda i: (0, 0))],
        outs=[((T, MLA_OUT), BF, (tm, MLA_OUT), lambda i: (i, 0)),
              ((T, proj.shape[1]), BF, (tm, proj.shape[1]), lambda i: (i, 0)),
              ((1, MLA_OUT), F32, (1, MLA_OUT), lambda i: (0, 0)), ((1, SGU_OUT), F32, (1, SGU_OUT), lambda i: (0, 0)),
              ((1, SGU_OUT), F32, (1, SGU_OUT), lambda i: (0, 0)),
              ((GROUPS, CHUNK, CHUNK), F32, (GROUPS, CHUNK, CHUNK), lambda i: (0, 0, 0)),
              ((CHUNK, SGU_OUT), F32, (CHUNK, SGU_OUT), lambda i: (0, 0))],
        scratch=[pltpu.VMEM((tm, SGU_OUT), F32), pltpu.VMEM((tm, SGU_OUT), F32)], deps=deps)


def _shift_down(z, n, row):
    return jnp.where(row >= n, pltpu.roll(z, n, 0), 0.0)


def _shift_up(z, n, row, T):
    return jnp.where(row < T - n, pltpu.roll(z, T - n, 0), 0.0)


def conv_fwd(proj, conv_w, *, name):
    T, D3 = proj.shape
    D = D3 // 3
    tn = _tile(D, 256)
    nj = D // tn

    def body(b_ref, c_ref, x_ref, w_ref, o_ref):
        row = lax.broadcasted_iota(jnp.int32, (T, tn), 0)
        z = c_ref[...] * x_ref[...]
        zc = w_ref[2:3, :] * z + w_ref[1:2, :] * _shift_down(z, 1, row) + w_ref[0:1, :] * _shift_down(z, 2, row)
        o_ref[...] = (b_ref[...] * zc).astype(BF)

    return _pcall(
        body, name=name, grid=(nj,),
        ins=[(proj, (T, tn), lambda j: (0, j)), (proj, (T, tn), lambda j: (0, nj + j)),
             (proj, (T, tn), lambda j: (0, 2 * nj + j)), (conv_w, (3, tn), lambda j: (0, j))],
        outs=[((T, D), BF, (T, tn), lambda j: (0, j))], semantics=("parallel",))[0]


def conv_bwd(dg, proj, conv_w, *, name, deps=()):
    T, D3 = proj.shape
    D = D3 // 3
    tn = _tile(D, 256)
    nj = D // tn

    def body(dg_ref, b_ref, c_ref, x_ref, w_ref, dp_ref, dw_ref, dc_scr, dx_scr):
        part = pl.program_id(1)

        @pl.when(part == 0)
        def _():
            row = lax.broadcasted_iota(jnp.int32, (T, tn), 0)
            c, x = c_ref[...], x_ref[...]
            z = c * x
            z1 = _shift_down(z, 1, row)
            z2 = _shift_down(z, 2, row)
            dgv = dg_ref[...]
            zc = w_ref[2:3, :] * z + w_ref[1:2, :] * z1 + w_ref[0:1, :] * z2
            dp_ref[...] = (dgv * zc).astype(BF)
            dzc = dgv * b_ref[...]
            dw_ref[0:1, :] = jnp.sum(dzc * z2, axis=0, keepdims=True)
            dw_ref[1:2, :] = jnp.sum(dzc * z1, axis=0, keepdims=True)
            dw_ref[2:3, :] = jnp.sum(dzc * z, axis=0, keepdims=True)
            dz = (w_ref[2:3, :] * dzc + w_ref[1:2, :] * _shift_up(dzc, 1, row, T)
                  + w_ref[0:1, :] * _shift_up(dzc, 2, row, T))
            dc_scr[...] = (dz * x).astype(BF)
            dx_scr[...] = (dz * c).astype(BF)

        @pl.when(part == 1)
        def _():
            dp_ref[...] = dc_scr[...]

        @pl.when(part == 2)
        def _():
            dp_ref[...] = dx_scr[...]

    return _pcall(
        body, name=name, grid=(nj, 3),
        ins=[(dg, (T, tn), lambda j, p: (0, j)), (proj, (T, tn), lambda j, p: (0, j)),
             (proj, (T, tn), lambda j, p: (0, nj + j)), (proj, (T, tn), lambda j, p: (0, 2 * nj + j)),
             (conv_w, (3, tn), lambda j, p: (0, j))],
        outs=[((T, D3), BF, (T, tn), lambda j, p: (0, p * nj + j)), ((3, D), F32, (3, tn), lambda j, p: (0, j))],
        scratch=[pltpu.VMEM((T, tn), BF), pltpu.VMEM((T, tn), BF)], semantics=("parallel", "arbitrary"), deps=deps)


def loss_bwd(x_parts, gain, target, *, name):
    T, D = target.shape
    tm = _tile(T, ROW_TILE, 8)
    n_x = len(x_parts)

    def body(*refs):
        x_refs = refs[:n_x]
        g_ref, t_ref, dx_ref, dxb_ref, dg_ref, loss_ref = refs[n_x:]
        first = pl.program_id(0) == 0
        xv = jnp.concatenate([r[...] for r in x_refs], axis=-1) if n_x > 1 else x_refs[0][...]
        r = _rstd(xv)
        xh = xv * r
        gain_v = g_ref[...]
        err = xh * gain_v - t_ref[...]
        part = 0.5 * jnp.sum(jnp.mean(err * err, axis=-1, keepdims=True), axis=0, keepdims=True)
        _acc_rows(loss_ref, jnp.broadcast_to(part, (1, LANES)), first)
        dy = err * (1.0 / D)
        gdy = dy * gain_v
        dx = r * (gdy - xh * jnp.mean(gdy * xh, axis=-1, keepdims=True))
        dx_ref[...] = dx
        dxb_ref[...] = dx.astype(BF)
        _acc_rows(dg_ref, dy * xh, first)

    return _pcall(
        body, name=name, grid=(T // tm,),
        ins=[(p, (tm, D // n_x), lambda i: (i, 0)) for p in x_parts]
        + [(gain, (1, D), lambda i: (0, 0)), (target, (tm, D), lambda i: (i, 0))],
        outs=[((T, D), F32, (tm, D), lambda i: (i, 0)), ((T, D), BF, (tm, D), lambda i: (i, 0)),
              ((1, D), F32, (1, D), lambda i: (0, 0)), ((1, LANES), F32, (1, LANES), lambda i: (0, 0))])


def _adamw(g, w, m, v):
    m = ADAM_B1 * m + (1.0 - ADAM_B1) * g
    v = ADAM_B2 * v + (1.0 - ADAM_B2) * (g * g)
    m_hat = m / ADAM_C1
    v_hat = v / ADAM_C2
    delta = -ADAM_LR * (m_hat / (jnp.sqrt(v_hat) + ADAM_EPS) + ADAM_WD * w)
    return delta, m, v


def adam_flat(g, w, m, v, *, name):
    def body(g_ref, w_ref, m_ref, v_ref, d_ref, nm_ref, nv_ref):
        d, nm, nv = _adamw(g_ref[...], w_ref[...], m_ref[...], v_ref[...])
        d_ref[...] = d
        nm_ref[...] = nm
        nv_ref[...] = nv

    blk = g.shape
    zero = lambda: (0, 0)
    return _pcall(body, name=name, grid=(),
                  ins=[(t, blk, zero) for t in (g, w, m, v)],
                  outs=[(blk, F32, blk, zero)] * 3)


def _chip_slots():
    x, y, c = lax.axis_index("x"), lax.axis_index("y"), lax.axis_index("c")
    chips = [(1 - x, y), (x, 1 - y), (1 - x, 1 - y)]
    return x, y, c, chips


def device_index():
    x, y, c, chips = _chip_slots()
    return jnp.stack([4 * x + 2 * y + c, 2 * x + y] + [4 * cx + 2 * cy + c for cx, cy in chips]
                     + [2 * cx + cy for cx, cy in chips]).astype(jnp.int32)


def _job_rows(R, C, n_steps):
    if n_steps is None:
        n_steps = max(1, R * C // STREAM_BLOCK_ELEMS)
    n_blk = max([d for d in range(1, n_steps + 1) if R % d == 0 and (R // d) % 16 == 0] or [1])
    return R // n_blk, n_blk


def run_job(job, *, index, name, deps=()):
    jb = job(None)
    n_in = len(jb["ins"])

    def body(idx_ref, *refs):
        jb["fn"](refs[:n_in], refs[n_in:n_in + len(jb["outs"])])

    return _pcall(body, name=name, grid=(jb["n_blk"],), ins=jb["ins"], outs=jb["outs"], prefetch=index,
                  aliases={1 + a: o for a, o in jb["aliases"].items()}, semantics=("parallel",), deps=deps)


def adam_job(gs, a_buf, b_buf, w, m, v, layer, prev):
    L, R, C = w.shape

    def build(n_steps):
        tr, n_blk = _job_rows(R, C, n_steps)
        blk = (None, tr, C)
        row = lambda t: jnp.minimum(t, n_blk - 1)
        ins = [(gs, blk, lambda t, s: (s[0], row(t), 0)), (a_buf, blk, lambda t, s: (s[1], row(t), 0))]
        ins += [(b_buf, blk, lambda t, s, j=j: (j, row(t), 0)) for j in range(3)]
        ins += [(p, blk, lambda t, s: (layer, row(t), 0)) for p in (w, m, v)]
        ins += [(p, None, None) for p in (prev or [])]

        def fn(i, o):
            g = ((((i[0][...].astype(F32) + i[1][...].astype(F32)) + i[2][...].astype(F32))
                  + i[3][...].astype(F32)) + i[4][...].astype(F32))
            d, nm, nv = _adamw(g, i[5][...], i[6][...], i[7][...])
            o[0][...] = g
            o[1][...] = d
            o[2][...] = nm
            o[3][...] = nv

        return dict(ins=ins, outs=[((L, R, C), F32, blk, lambda t, s: (layer, row(t), 0))] * 4, fn=fn,
                    aliases={8 + o: o for o in range(4)} if prev else {}, n_blk=n_blk)

    return build


def pair_job(gs, a_buf):
    _, R, C = gs.shape

    def build(n_steps):
        tr, n_blk = _job_rows(R, C, n_steps)
        blk = (None, tr, C)
        row = lambda t: jnp.minimum(t, n_blk - 1)
        ins = [(gs, blk, lambda t, s, j=j: (s[2 + j], row(t), 0)) for j in range(3)]
        ins += [(a_buf, blk, lambda t, s, j=j: (s[5 + j], row(t), 0)) for j in range(3)]

        def fn(i, o):
            for j in range(3):
                o[0][j] = (i[j][...].astype(F32) + i[3 + j][...].astype(F32)).astype(BF)

        return dict(ins=ins, outs=[((3, R, C), BF, (3, tr, C), lambda t, s: (0, row(t), 0))], fn=fn, aliases={},
                    n_blk=n_blk)

    return build


def reduce_sum(gs, a_buf, b_buf, *, name):
    _, R, C = gs.shape
    tr = _tile(R, 256, 16)
    x, y, c, _ = _chip_slots()
    idx = jnp.stack([4 * x + 2 * y + c, 2 * x + y]).astype(jnp.int32)

    def body(idx_ref, g_ref, a_ref, b0_ref, b1_ref, b2_ref, o_ref):
        o_ref[...] = ((((g_ref[...].astype(F32) + a_ref[...].astype(F32)) + b0_ref[...].astype(F32))
                       + b1_ref[...].astype(F32)) + b2_ref[...].astype(F32))

    blk3 = (None, tr, C)
    return _pcall(body, name=name, grid=(R // tr,),
                  ins=[(gs, blk3, lambda i, s: (s[0], i, 0)), (a_buf, blk3, lambda i, s: (s[1], i, 0)),
                       (b_buf, blk3, lambda i, s: (0, i, 0)), (b_buf, blk3, lambda i, s: (1, i, 0)),
                       (b_buf, blk3, lambda i, s: (2, i, 0))],
                  outs=[((R, C), F32, (tr, C), lambda i, s: (i, 0))], prefetch=idx, semantics=("parallel",))[0]


def adam_rows(g, w, m, v, *, name):
    R, C = g.shape
    tr = _tile(R, 256, 8)

    def body(g_ref, w_ref, m_ref, v_ref, d_ref, nm_ref, nv_ref):
        d, nm, nv = _adamw(g_ref[...], w_ref[...], m_ref[...], v_ref[...])
        d_ref[...] = d
        nm_ref[...] = nm
        nv_ref[...] = nv

    spec = ((tr, C), lambda i: (i, 0))
    return _pcall(body, name=name, grid=(R // tr,), ins=[(t, *spec) for t in (g, w, m, v)],
                  outs=[((R, C), F32, *spec)] * 3, semantics=("parallel",))


def sum_rows8(gathered, rows, *, name):
    W = gathered.shape[1]

    def body(g_ref, o_ref):
        acc = g_ref[0:rows, :]
        for d in range(1, N_DEV):
            acc = acc + g_ref[d * rows:(d + 1) * rows, :]
        o_ref[...] = acc

    return _pcall(body, name=name, grid=(), ins=[(gathered, gathered.shape, lambda: (0, 0))],
                  outs=[((rows, W), F32, (rows, W), lambda: (0, 0))])[0]


HBM_SPEC = pl.BlockSpec(memory_space=pltpu.HBM)
SEM_SPEC = pl.BlockSpec(memory_space=pltpu.SEMAPHORE)
ANY_SPEC = pl.BlockSpec(memory_space=pl.ANY)
DATAFLOW = pltpu.SideEffectType.DATAFLOW_SIDE_EFFECTING


def _in_hbm(v):
    return pltpu.with_memory_space_constraint(v, pltpu.HBM)


def _slot(p):
    return 4 * p[0] + 2 * p[1] + p[2]


def _gather_peers():
    x, y, c, chips = _chip_slots()
    return (x, y, c), [(x, y, 1 - c)] + [(*chip, c) for chip in chips]


def gather_start(groups, after, *, name):
    flat = [s for g in groups for s in g]
    n, n_g = len(flat), len(groups)
    where = [(gi, ti) for gi, g in enumerate(groups) for ti in range(len(g))]

    def body(*refs):
        src, land = refs[:n], refs[n:2 * n]
        sems = refs[2 * n + 1:2 * n + 1 + 2 * n_g]
        me, peers = _gather_peers()
        for t in range(n):
            gi, ti = where[t]
            for k, to in enumerate(peers):
                pltpu.make_async_remote_copy(
                    src_ref=src[t], dst_ref=land[t].at[_slot(me)], send_sem=sems[2 * gi].at[4 * ti + k],
                    recv_sem=sems[2 * gi + 1].at[4 * ti + k], device_id=to, device_id_type=MESH).start()
        refs[-1][...] = jnp.zeros_like(refs[-1])

    out_shape = []
    for g in groups:
        out_shape += [pltpu.SemaphoreType.DMA((4 * len(g),)), pltpu.SemaphoreType.DMA((4 * len(g),))]
    out_shape += [pltpu.HBM(s.shape, s.dtype) for s in flat]
    out_shape += [pltpu.HBM((N_DEV,) + s.shape, s.dtype) for s in flat]
    out_shape += [jax.ShapeDtypeStruct((8, LANES), F32)]
    aliases = {t: 2 * n_g + t for t in range(n)}
    aliases.update({n + t: 2 * n_g + n + t for t in range(n)})
    res = pl.pallas_call(
        body, name=name, out_shape=out_shape, in_specs=[HBM_SPEC] * (2 * n) + [ANY_SPEC],
        out_specs=[SEM_SPEC] * (2 * n_g) + [HBM_SPEC] * (2 * n) + [pl.BlockSpec(memory_space=pltpu.VMEM)],
        input_output_aliases=aliases, compiler_params=pltpu.CompilerParams(has_side_effects=DATAFLOW),
    )(*[_in_hbm(s) for s in flat], *[_in_hbm(lax.empty((N_DEV,) + s.shape, s.dtype)) for s in flat], after)
    out, off = [], 0
    for gi, g in enumerate(groups):
        k = len(g)
        out.append((res[2 * gi], res[2 * gi + 1], res[2 * n_g + off:2 * n_g + off + k],
                    res[2 * n_g + n + off:2 * n_g + n + off + k]))
        off += k
    return out, res[-1]


def gather_wait(started, after, *, name):
    send_sems, recv_sems, srcs, lands = started
    n = len(srcs)
    after = list(after)

    def body(*refs):
        src, land = refs[:n], refs[n:2 * n]
        send, recv = refs[2 * n], refs[2 * n + 1]
        _, peers = _gather_peers()
        for t in range(n):
            for k, frm in enumerate(peers):
                cp = pltpu.make_async_remote_copy(
                    src_ref=src[t], dst_ref=land[t].at[_slot(frm)], send_sem=send.at[4 * t + k],
                    recv_sem=recv.at[4 * t + k],
                    device_id=frm, device_id_type=MESH)
                cp.wait_send()
                cp.wait_recv()

    res = pl.pallas_call(
        body, name=name,
        out_shape=[pltpu.HBM(s.shape, s.dtype) for s in srcs] + [pltpu.HBM(l.shape, l.dtype) for l in lands],
        in_specs=[HBM_SPEC] * (2 * n) + [SEM_SPEC, SEM_SPEC] + [ANY_SPEC] * len(after),
        out_specs=[HBM_SPEC] * (2 * n), input_output_aliases={t: t for t in range(2 * n)},
        compiler_params=pltpu.CompilerParams(has_side_effects=DATAFLOW),
    )(*srcs, *lands, send_sems, recv_sems, *after)
    return res[:n], res[n:]


def place_own(src, land, *, name):
    R, C = src.shape
    tr = _tile(R, 512, 16)
    x, y, c, _ = _chip_slots()
    idx = jnp.stack([4 * x + 2 * y + c]).astype(jnp.int32)

    def body(idx_ref, s_ref, land_ref, o_ref):
        o_ref[...] = s_ref[...]

    return _pcall(body, name=name, grid=(R // tr,),
                  ins=[(src, (tr, C), lambda i, s: (i, 0)), (land, None, None)],
                  outs=[(land.shape, land.dtype, (None, tr, C), lambda i, s: (s[0], i, 0))],
                  prefetch=idx, aliases={2: 0}, semantics=("parallel",))[0]


def gather_finish(srcs, lands, *, name):
    n = len(srcs)

    def body(*refs):
        land = refs[n:2 * n]
        send_sems, recv_sems = refs[2 * n:]
        x, y, c, chips = _chip_slots()
        me, sibling = (x, y, c), (x, y, 1 - c)

        def copy(t, j, block, to):
            return pltpu.make_async_remote_copy(
                src_ref=land[t].at[_slot(block)], dst_ref=land[t].at[_slot(block)], send_sem=send_sems.at[t, j],
                recv_sem=recv_sems.at[t, j], device_id=to, device_id_type=MESH)

        sends = [copy(t, j, (*chip, c), sibling) for t in range(n) for j, chip in enumerate(chips)]
        for cp in sends:
            cp.start()
        for t in range(n):
            for j, chip in enumerate(chips):
                copy(t, j, (*chip, 1 - c), me).wait_recv()
        for cp in sends:
            cp.wait_send()

    passed = pl.pallas_call(
        body, name=name, out_shape=[jax.ShapeDtypeStruct(l.shape, l.dtype) for l in lands],
        in_specs=[ANY_SPEC] * n, out_specs=[ANY_SPEC] * n,
        input_output_aliases={t: t for t in range(n)},
        scratch_shapes=[pltpu.SemaphoreType.DMA((n, 3)), pltpu.SemaphoreType.DMA((n, 3))],
    )(*lands)
    return [place_own(s, l, name=f"{name}_own{t}") for t, (s, l) in enumerate(zip(srcs, passed))]


def chips_start(pairs, *, name):
    n = len(pairs)

    def body(*refs):
        src, land = refs[:n], refs[n:2 * n]
        send, recv = refs[2 * n], refs[2 * n + 1]
        token = refs[-1]
        x, y, c, chips = _chip_slots()
        for t in range(n):
            for j, chip in enumerate(chips):
                pltpu.make_async_remote_copy(
                    src_ref=src[t].at[j], dst_ref=land[t].at[j], send_sem=send.at[3 * t + j],
                    recv_sem=recv.at[3 * t + j], device_id=(*chip, c), device_id_type=MESH).start()
        token[...] = jnp.zeros_like(token)

    res = pl.pallas_call(
        body, name=name,
        out_shape=[pltpu.SemaphoreType.DMA((3 * n,)), pltpu.SemaphoreType.DMA((3 * n,))]
        + [pltpu.HBM(p.shape, p.dtype) for p in pairs] * 2 + [jax.ShapeDtypeStruct((8, LANES), F32)],
        in_specs=[HBM_SPEC] * (2 * n),
        out_specs=[SEM_SPEC, SEM_SPEC] + [HBM_SPEC] * (2 * n) + [pl.BlockSpec(memory_space=pltpu.VMEM)],
        input_output_aliases={t: 2 + t for t in range(2 * n)},
        compiler_params=pltpu.CompilerParams(has_side_effects=DATAFLOW),
    )(*[_in_hbm(p) for p in pairs], *[_in_hbm(lax.empty(p.shape, p.dtype)) for p in pairs])
    return res[0], res[1], res[2:2 + n], res[2 + n:2 + 2 * n], res[-1]


def chips_wait(started, after, *, name):
    send_sems, recv_sems, srcs, lands, _ = started
    n = len(srcs)

    def body(*refs):
        src, land = refs[:n], refs[n:2 * n]
        send, recv = refs[2 * n], refs[2 * n + 1]
        x, y, c, chips = _chip_slots()
        for t in range(n):
            for j, chip in enumerate(chips):
                cp = pltpu.make_async_remote_copy(
                    src_ref=src[t].at[j], dst_ref=land[t].at[j], send_sem=send.at[3 * t + j],
                    recv_sem=recv.at[3 * t + j], device_id=(*chip, c), device_id_type=MESH)
                cp.wait_send()
                cp.wait_recv()

    res = pl.pallas_call(
        body, name=name, out_shape=[pltpu.HBM(s.shape, s.dtype) for s in srcs] * 2,
        in_specs=[HBM_SPEC] * (2 * n) + [SEM_SPEC, SEM_SPEC, ANY_SPEC], out_specs=[HBM_SPEC] * (2 * n),
        input_output_aliases={t: t for t in range(2 * n)},
        compiler_params=pltpu.CompilerParams(has_side_effects=DATAFLOW),
    )(*srcs, *lands, send_sems, recv_sems, after)
    return res[n:]


def _sibling_copies(src, land, send, recv, n):
    x, y, c, _ = _chip_slots()
    return [pltpu.make_async_remote_copy(
        src_ref=src[t].at[4 * (q // 2) + 2 * (q % 2) + (1 - c)], dst_ref=land[t].at[q], send_sem=send.at[4 * t + q],
        recv_sem=recv.at[4 * t + q], device_id=(x, y, 1 - c), device_id_type=MESH)
        for t in range(n) for q in range(4)]


def sibling_start(gs, *, name):
    n = len(gs)

    def body(*refs):
        for cp in _sibling_copies(refs[:n], refs[n:2 * n], refs[2 * n], refs[2 * n + 1], n):
            cp.start()
        refs[-1][...] = jnp.zeros_like(refs[-1])

    lands = [lax.empty((4,) + g.shape[1:], g.dtype) for g in gs]
    res = pl.pallas_call(
        body, name=name,
        out_shape=[pltpu.SemaphoreType.DMA((4 * n,)), pltpu.SemaphoreType.DMA((4 * n,))]
        + [pltpu.HBM(g.shape, g.dtype) for g in gs] + [pltpu.HBM(l.shape, l.dtype) for l in lands]
        + [jax.ShapeDtypeStruct((8, LANES), F32)],
        in_specs=[HBM_SPEC] * (2 * n),
        out_specs=[SEM_SPEC, SEM_SPEC] + [HBM_SPEC] * (2 * n) + [pl.BlockSpec(memory_space=pltpu.VMEM)],
        input_output_aliases={t: 2 + t for t in range(2 * n)},
        compiler_params=pltpu.CompilerParams(has_side_effects=DATAFLOW),
    )(*[_in_hbm(g) for g in gs], *[_in_hbm(l) for l in lands])
    return res[0], res[1], res[2:2 + n], res[2 + n:2 + 2 * n], res[-1]


def sibling_wait(started, after, *, name):
    send_sems, recv_sems, srcs, lands, _ = started
    n = len(srcs)

    def body(*refs):
        for cp in _sibling_copies(refs[:n], refs[n:2 * n], refs[2 * n], refs[2 * n + 1], n):
            cp.wait_send()
            cp.wait_recv()

    res = pl.pallas_call(
        body, name=name,
        out_shape=[pltpu.HBM(s.shape, s.dtype) for s in srcs] + [pltpu.HBM(l.shape, l.dtype) for l in lands],
        in_specs=[HBM_SPEC] * (2 * n) + [SEM_SPEC, SEM_SPEC, ANY_SPEC], out_specs=[HBM_SPEC] * (2 * n),
        input_output_aliases={t: t for t in range(2 * n)},
        compiler_params=pltpu.CompilerParams(has_side_effects=DATAFLOW),
    )(*srcs, *lands, send_sems, recv_sems, after)
    return res[:n], res[n:]


def _rope_slab(cols):
    z = jnp.zeros(cols.shape[:-1] + (HALF_ROPE,), cols.dtype)
    return jnp.concatenate([cols[..., :HALF_ROPE], z, cols[..., HALF_ROPE:], z], axis=-1)


def _rope_unslab(slab):
    return jnp.concatenate([slab[..., :HALF_ROPE], slab[..., 2 * HALF_ROPE:3 * HALF_ROPE]], axis=-1)


def _pack_w_in_t(wt_g):
    s, c, d = wt_g.shape
    w = wt_g.reshape(s * c, d)
    c2, c3 = Q_LORA + KV_LORA, Q_LORA + KV_LORA + QK_ROPE
    r = w[c2:c3]
    z = jnp.zeros((HALF_ROPE, d), w.dtype)
    return jnp.concatenate([w[:c2], w[c3:], r[:HALF_ROPE], z, r[HALF_ROPE:], z], axis=0)


def unpack_w_in_t_grad(dwt, *, name):
    n_rows, d = dwt.shape
    kr = QK_ROPE
    n_out_rows = n_rows - kr
    blk = n_out_rows // 7
    assert blk * 7 == n_out_rows and blk % kr == 0 and n_rows % (2 * kr) == 0
    kr_row = Q_LORA + KV_LORA
    k_mix = kr_row // blk
    off = kr_row - k_mix * blk
    slab_block = (n_rows - 2 * kr) // (2 * kr)

    def body(prev_ref, in_ref, slab_ref, o_ref):
        k = pl.program_id(0)

        @pl.when(k < k_mix)
        def _():
            o_ref[...] = in_ref[...]

        @pl.when(k == k_mix)
        def _():
            o_ref[:off, :] = in_ref[:off, :]
            o_ref[off:off + HALF_ROPE, :] = slab_ref[:HALF_ROPE, :]
            o_ref[off + HALF_ROPE:off + kr, :] = slab_ref[2 * HALF_ROPE:3 * HALF_ROPE, :]
            o_ref[off + kr:, :] = in_ref[off:blk - kr, :]

        @pl.when(k > k_mix)
        def _():
            o_ref[:kr, :] = prev_ref[blk - kr:, :]
            o_ref[kr:, :] = in_ref[:blk - kr, :]

    out = _pcall(body, name=name, grid=(7,),
                 ins=[(dwt, (blk, d), lambda k: (jnp.maximum(k - 1, 0), 0)), (dwt, (blk, d), lambda k: (k, 0)),
                      (dwt, (2 * kr, d), lambda k: (slab_block, 0))],
                 outs=[((n_out_rows, d), dwt.dtype, (blk, d), lambda k: (k, 0))], semantics=("parallel",))[0]
    return out.reshape(N_DEV, n_out_rows // N_DEV, d)


def _rope_tables(positions):
    inv_freq = ROPE_BASE ** (-jnp.arange(0, QK_ROPE, 2, dtype=F32) / QK_ROPE)
    ang = positions.astype(F32)[:, None] * inv_freq
    cos, sin = jnp.cos(ang), jnp.sin(ang)
    z = jnp.zeros_like(cos)
    return jnp.concatenate([cos, z, cos, z], axis=-1), jnp.concatenate([-sin, z, sin, z], axis=-1)


def _mlp_up(x, gain, w1, tag):
    hn = rms_fwd(x, gain, name=f"mlp{tag}_norm")

    def act_epi(acc):
        a = jnp.maximum(acc, 0.0)
        return a, a * a

    T = x.shape[0]
    F = w1.shape[0] * w1.shape[2]
    a, act = mm(hn, w1, name=f"mlp{tag}_up", outs=[((T, F), BF, None), ((T, F), BF, None)], epi=act_epi)
    return hn, a, act


def _mlp_down(x, act, w2, tag, part=0):
    n = w2.shape[1]
    bm = _tile(x.shape[0], MM_TILE)
    bn = _tile(n, MM_TILE)
    per = n // bn
    return mm(act, w2, name=f"mlp{tag}_down{part}", out=((x.shape[0], n), F32), bm=bm, bn=bn,
              epi=lambda acc, r: (acc + r[...],), epi_ins=[(x, (bm, bn), lambda i, j, k: (i, part * per + j))])


def _mlp_bwd_weights(w1, w2, saved, dxb, tag):
    hn, a, act = saved
    T, D = dxb.shape
    F = a.shape[1]
    bm = _tile(T, MM_TILE)
    bn = _tile(F, min(MM_TILE, w1.shape[2]))
    dhid = mm(dxb, w2, tb=True, name=f"mlp{tag}_dhid", out=((T, F), BF), bm=bm, bn=bn,
              epi=lambda acc, a_ref: (2.0 * a_ref[...].astype(F32) * acc,),
              epi_ins=[(a, (bm, bn), lambda i, j, k: (i, j))])
    dw2 = mm(act, dxb, ta=True, name=f"mlp{tag}_dw2", out=((F, D), BF))
    dw1 = mm(hn, dhid, ta=True, name=f"mlp{tag}_dw1", out=(w1.shape, BF))
    return dhid, dw1, dw2.reshape(N_DEV, F // N_DEV, D)


def _reduce_begin(grads, tag):
    return sibling_start(grads, name=f"reduce_sibling_start_{tag}")


def _reduce_continue(sib, after, tag, index):
    grads, a_bufs = sibling_wait(sib, after, name=f"reduce_sibling_wait_{tag}")
    pairs = [run_job(pair_job(g, a), index=index, name=f"pair_sum_{tag}{t}")[0]
             for t, (g, a) in enumerate(zip(grads, a_bufs))]
    return grads, a_bufs, chips_start(pairs, name=f"reduce_chips_start_{tag}")


def kernel(x, positions, e_norm_mix, e_w_in, e_q_norm, e_w_uq, e_kv_norm, e_w_ukv, e_v_norm, e_sgu_w, e_sgu_b, e_mla_out_norm, e_sgu_out_norm, e_w_out, o_norm_mix, o_w_in, o_conv_w, o_w_out, mlp_norm, mlp_w1, mlp_w2, final_norm, loss_target, m_e_norm_mix, m_e_w_in, m_e_q_norm, m_e_w_uq, m_e_kv_norm, m_e_w_ukv, m_e_v_norm, m_e_sgu_w, m_e_sgu_b, m_e_mla_out_norm, m_e_sgu_out_norm, m_e_w_out, m_o_norm_mix, m_o_w_in, m_o_conv_w, m_o_w_out, m_mlp_norm, m_mlp_w1, m_mlp_w2, m_final_norm, v_e_norm_mix, v_e_w_in, v_e_q_norm, v_e_w_uq, v_e_kv_norm, v_e_w_ukv, v_e_v_norm, v_e_sgu_w, v_e_sgu_b, v_e_mla_out_norm, v_e_sgu_out_norm, v_e_w_out, v_o_norm_mix, v_o_w_in, v_o_conv_w, v_o_w_out, v_mlp_norm, v_mlp_w1, v_mlp_w2, v_final_norm):
    T, D = x.shape[1], x.shape[2]
    d_shard = o_norm_mix.shape[1]
    x0 = x[0]
    target = loss_target[0]
    me = 4 * lax.axis_index("x") + 2 * lax.axis_index("y") + lax.axis_index("c")

    bf = lambda s: s.astype(BF)
    gather_groups = [[bf(jnp.transpose(e_w_in[0])), bf(e_w_uq[0]), bf(e_w_ukv[0])], [bf(e_w_out[0]), bf(mlp_w1[0])],
                     [bf(mlp_w2[0]), bf(o_w_in[0])], [bf(o_w_out[0]), bf(mlp_w1[1])], [bf(mlp_w2[1])]]
    small_rows = jnp.concatenate([o_norm_mix, o_conv_w[0], jnp.zeros((4, d_shard), F32)], axis=0)
    gather_groups[0].insert(0, small_rows)
    started, start_token = gather_start(gather_groups[:1], x0, name="gather_start0")
    started_rest, rest_token = gather_start(gather_groups[1:], start_token, name="gather_start1")
    started += started_rest

    def gathered(gi, after):
        srcs, lands = gather_wait(started[gi], after, name=f"gather_wait{gi}")
        return gather_finish(srcs, lands, name=f"gather_finish{gi}")

    w_tril = jnp.tril(e_sgu_w[0])
    w_tril_b = w_tril.astype(BF)
    w_tril_tb = jnp.swapaxes(w_tril, 1, 2).astype(BF)
    b_full = jnp.repeat(e_sgu_b[0].T, CH, axis=1)
    v_gain = e_v_norm[0].reshape(1, SGU_OUT)
    cos_t, sin_t = _rope_tables(positions[0])
    mlp_gain = [mlp_norm[0:1], mlp_norm[1:2]]
    final_gain = final_norm.reshape(1, D)

    h0 = rms_fwd(x0, e_norm_mix, name="e_norm", deps=[rest_token])
    small_g, g_w_in_t, g_w_uq, w_ukv = gathered(0, [h0, cos_t, sin_t, w_tril_b, w_tril_tb, b_full])
    o_norm_full = small_g[:, 0, :].reshape(1, D)
    conv_w_full = jnp.transpose(small_g[:, 1:4, :], (1, 0, 2)).reshape(3, D)
    w_in_t = _pack_w_in_t(g_w_in_t)
    w_uq = jnp.concatenate([g_w_uq[..., :QK_NOPE], _rope_slab(g_w_uq[..., QK_NOPE:])], axis=-1)
    proj = mm(h0, w_in_t, tb=True, name="e_in", out=((T, w_in_t.shape[0]), F32), bn=_tile(w_in_t.shape[0], 640))
    qn, kvn, krope = mla_prep(proj, e_q_norm, e_kv_norm, cos_t, sin_t, name="mla_prep")
    bm = _tile(T, MM_TILE)

    def q_epi(acc, cos_ref, sin_ref):
        return (jnp.concatenate([acc[:, :QK_NOPE], _rope_fwd(acc[:, QK_NOPE:], cos_ref[...], sin_ref[...])], axis=-1),)

    q = mm(qn, w_uq, name="mla_q", out=((T, HEADS * HEAD_PAD), BF), bm=bm, bn=HEAD_PAD, epi=q_epi,
           epi_ins=[(cos_t, (bm, LANES), lambda i, j, k: (i, 0)), (sin_t, (bm, LANES), lambda i, j, k: (i, 0))])

    def kv_epi(acc, kr_ref):
        return jnp.concatenate([acc[:, :QK_NOPE].astype(BF), kr_ref[...]], axis=-1), acc[:, QK_NOPE:]

    k, v = mm(kvn, w_ukv, name="mla_kv", bm=bm, bn=HEAD_PAD, epi=kv_epi,
              outs=[((T, HEADS * HEAD_PAD), BF, HEAD_PAD), ((T, MLA_OUT), BF, V_HEAD)],
              epi_ins=[(krope, (bm, LANES), lambda i, j, k: (i, 0))])
    attn, attn_lse = attn_fwd(q, k, v, name="attn_fwd")
    mixed = mix_fwd(attn, proj, e_mla_out_norm, e_sgu_out_norm, v_gain, w_tril_b, b_full, name="mix_fwd")
    bn = _tile(D, MM_TILE)
    g_w_out_e, w1_0 = gathered(1, [mixed])
    w_out_e = g_w_out_e.reshape(-1, D)
    x1 = mm(mixed, w_out_e, name="e_out", out=((T, D), F32), bm=bm, bn=bn,
            epi=lambda acc, r: (acc + r[...],), epi_ins=[(x0, (bm, bn), lambda i, j, k: (i, j))])
    hn0, a0, act0 = _mlp_up(x1, mlp_gain[0], w1_0, 0)
    g_w2_0, g_w_in_o = gathered(2, [act0])
    w2_0 = g_w2_0.reshape(-1, D)
    x2 = _mlp_down(x1, act0, w2_0, 0)
    ho = rms_fwd(x2, o_norm_full, name="o_norm")
    proj_o = mm(ho, g_w_in_o, name="o_in", out=((T, 3 * D), F32))
    gated = conv_fwd(proj_o, conv_w_full, name="conv_fwd")
    g_w_out_o, w1_1 = gathered(3, [gated])
    w_out_o = g_w_out_o.reshape(-1, D)
    x3 = mm(gated, w_out_o, name="o_out", out=((T, D), F32), bm=bm, bn=bn,
            epi=lambda acc, r: (acc + r[...],), epi_ins=[(x2, (bm, bn), lambda i, j, k: (i, j))])
    hn1, a1, act1 = _mlp_up(x3, mlp_gain[1], w1_1, 1)
    (g_w2_1,) = gathered(4, [act1])
    w2_1 = g_w2_1.reshape(-1, D)
    x4 = _mlp_down(x3, act1, w2_1, 1)
    w1, w2 = [w1_0, w1_1], [w2_0, w2_1]

    dx4, dx4b, d_final, loss_part = loss_bwd([x4], final_gain, target, name="loss_bwd")

    hosted = dict(job_index=device_index())
    dhid1, dw1_1, dw2_1 = _mlp_bwd_weights(w1[1], w2[1], (hn1, a1, act1), dx4b, 1)
    sib_r0 = _reduce_begin([dw1_1, dw2_1], "r0")
    dhn1 = mm(dhid1, w1[1], tb=True, name="mlp1_dhn", out=((T, D), F32), deps=[sib_r0[-1]])
    grads_r0, a_r0 = sibling_wait(sib_r0, dhn1, name="reduce_sibling_wait_r0")
    dx3, dx3b, d_mlp1 = rms_bwd(x3, mlp_gain[1], dhn1, dres=dx4, name="mlp1_norm_bwd")

    dgated, ((pair_r0a,),) = mm(dx3b, w_out_o, tb=True, name="o_out_dx", out=((T, D), F32),
                                jobs=[pair_job(grads_r0[0], a_r0[0])], **hosted)
    dw_out_o, ((pair_r0b,),) = mm(gated, dx3b, ta=True, name="o_out_dw", out=((D, D), BF),
                                  jobs=[pair_job(grads_r0[1], a_r0[1])], **hosted)
    st_r0 = chips_start([pair_r0a, pair_r0b], name="reduce_chips_start_r0")
    dproj_o, dconv_full = conv_bwd(dgated, proj_o, conv_w_full, name="conv_bwd", deps=[st_r0[-1]])
    dw_in_o = mm(ho, dproj_o, ta=True, name="o_in_dw", out=(g_w_in_o.shape, BF))
    sib_r1 = _reduce_begin([dw_out_o.reshape(g_w_out_o.shape), dw_in_o], "r1")
    dho = mm(dproj_o, g_w_in_o, tb=True, name="o_in_dx", out=((T, D), F32), deps=[sib_r1[-1]])
    grads_r1, a_r1 = sibling_wait(sib_r1, dho, name="reduce_sibling_wait_r1")
    dx2, dx2b, d_onorm_full = rms_bwd(x2, o_norm_full, dho, dres=dx3, name="o_norm_bwd")

    d_ff = a0.shape[1]
    bm_h, bn_h = _tile(T, MM_TILE), _tile(d_ff, min(MM_TILE, w1[0].shape[2]))
    dhid0, ((pair_r1a,), (pair_r1b,)) = mm(
        dx2b, w2[0], tb=True, name="mlp0_dhid", out=((T, d_ff), BF), bm=bm_h, bn=bn_h,
        epi=lambda acc, a_ref: (2.0 * a_ref[...].astype(F32) * acc,),
        epi_ins=[(a0, (bm_h, bn_h), lambda i, j, k: (i, j))],
        jobs=[pair_job(grads_r1[0], a_r1[0]), pair_job(grads_r1[1], a_r1[1])], **hosted)
    st_r1 = chips_start([pair_r1a, pair_r1b], name="reduce_chips_start_r1")
    dw2_0 = mm(act0, dx2b, ta=True, name="mlp0_dw2", out=((d_ff, D), BF), deps=[st_r1[-1]])
    b_r0 = chips_wait(st_r0, dw2_0, name="reduce_chips_wait_r0")
    dw1_0, (r_w1, r_w2) = mm(
        hn0, dhid0, ta=True, name="mlp0_dw1", out=(w1[0].shape, BF),
        jobs=[adam_job(grads_r0[0], a_r0[0], b_r0[0], mlp_w1, m_mlp_w1, v_mlp_w1, 1, None),
              adam_job(grads_r0[1], a_r0[1], b_r0[1], mlp_w2, m_mlp_w2, v_mlp_w2, 1, None)], **hosted)
    sib_r2 = _reduce_begin([dw1_0, dw2_0.reshape(N_DEV, d_ff // N_DEV, D)], "r2")
    dhn0 = mm(dhid0, w1[0], tb=True, name="mlp0_dhn", out=((T, D), F32), deps=[sib_r2[-1]])
    grads_r2, a_r2 = sibling_wait(sib_r2, dhn0, name="reduce_sibling_wait_r2")
    dx1, dx1b, d_mlp0 = rms_bwd(x1, mlp_gain[0], dhn0, dres=dx2, name="mlp0_norm_bwd")

    dmixed, ((pair_r2a,),) = mm(dx1b, w_out_e, tb=True, name="e_out_dx", out=((T, MLA_OUT + SGU_OUT), F32),
                                jobs=[pair_job(grads_r2[0], a_r2[0])], **hosted)
    dw_out_e, ((pair_r2b,),) = mm(mixed, dx1b, ta=True, name="e_out_dw", out=(w_out_e.shape, BF),
                                  jobs=[pair_job(grads_r2[1], a_r2[1])], **hosted)
    st_r2 = chips_start([pair_r2a, pair_r2b], name="reduce_chips_start_r2")
    (dattn, dproj, d_mla_out, d_sgu_out, d_vgain, d_sgu_w, d_b_full) = mix_bwd(
        dmixed, attn, proj, e_mla_out_norm, e_sgu_out_norm, v_gain, w_tril_b, w_tril_tb, b_full, name="mix_bwd",
        deps=[st_r2[-1]])
    b_r1 = chips_wait(st_r1, dattn, name="reduce_chips_wait_r1")
    dq, dk, dv = attn_bwd(q, k, v, attn, attn_lse, dattn, name="attn_bwd")
    dq_lin, dkv_lin, dproj = mla_bwd_prep(dq, dk, dv, cos_t, sin_t, dproj, name="mla_bwd_prep")
    dw_uq_pad = mm(qn, dq_lin, ta=True, name="mla_q_dw", out=(w_uq.shape, BF))
    dw_ukv = mm(kvn, dkv_lin, ta=True, name="mla_kv_dw", out=(w_ukv.shape, BF))
    dw_uq = jnp.concatenate([dw_uq_pad[..., :QK_NOPE], _rope_unslab(dw_uq_pad[..., QK_NOPE:])], axis=-1)
    sib_r2b = _reduce_begin([dw_out_e.reshape(g_w_out_e.shape), dw_uq, dw_ukv], "r2b")
    dqn = mm(dq_lin, w_uq, tb=True, name="mla_q_dx", out=((T, Q_LORA), F32), deps=[sib_r2b[-1]])
    dkvn = mm(dkv_lin, w_ukv, tb=True, name="mla_kv_dx", out=((T, KV_LORA), F32), deps=[sib_r2b[-1]])
    grads_r2b, a_r2b, st_r2b = _reduce_continue(sib_r2b, dkvn, "r2b", hosted["job_index"])
    dproj, d_qnorm = rms_bwd(proj, e_q_norm, dqn, col_block=0, want_f32=False, into=dproj, name="q_norm_bwd",
                             deps=[st_r2b[-1]])
    dproj, d_kvnorm = rms_bwd(proj, e_kv_norm, dkvn, col_block=1, want_f32=False, into=dproj, name="kv_norm_bwd")
    dw_in_t_pad, (r_w_out_o, r_w_in_o) = mm(
        dproj, h0, ta=True, name="e_in_dw", out=(w_in_t.shape, BF), bm=_tile(w_in_t.shape[0], 640),
        jobs=[adam_job(grads_r1[0], a_r1[0], b_r1[0], o_w_out, m_o_w_out, v_o_w_out, 0, None),
              adam_job(grads_r1[1], a_r1[1], b_r1[1], o_w_in, m_o_w_in, v_o_w_in, 0, None)], **hosted)
    dw_in_t = unpack_w_in_t_grad(dw_in_t_pad, name="e_in_dw_unpack")
    sib_r3 = _reduce_begin([dw_in_t], "r3")
    dh0 = mm(dproj, w_in_t, name="e_in_dx", out=((T, D), F32), deps=[sib_r3[-1]])
    grads_r3, a_r3, st_r3 = _reduce_continue(sib_r3, dh0, "r3", hosted["job_index"])
    tok_r3 = st_r3[-1]
    grad_x, d_enorm = rms_bwd(x0, e_norm_mix, dh0, dres=dx1, want_bf=False, name="e_norm_bwd", deps=[tok_r3])
    b_r2 = chips_wait(st_r2, grad_x, name="reduce_chips_wait_r2")

    d_sgu_b = jnp.transpose(d_b_full[:, ::CH])
    d_sgu_w_tril = jnp.tril(d_sgu_w)
    rep = [("e_norm_mix", e_norm_mix, m_e_norm_mix, v_e_norm_mix, d_enorm),
           ("e_q_norm", e_q_norm, m_e_q_norm, v_e_q_norm, d_qnorm),
           ("e_kv_norm", e_kv_norm, m_e_kv_norm, v_e_kv_norm, d_kvnorm),
           ("e_v_norm", e_v_norm, m_e_v_norm, v_e_v_norm, d_vgain),
           ("e_sgu_w", e_sgu_w, m_e_sgu_w, v_e_sgu_w, d_sgu_w_tril),
           ("e_sgu_b", e_sgu_b, m_e_sgu_b, v_e_sgu_b, d_sgu_b),
           ("e_mla_out_norm", e_mla_out_norm, m_e_mla_out_norm, v_e_mla_out_norm, d_mla_out),
           ("e_sgu_out_norm", e_sgu_out_norm, m_e_sgu_out_norm, v_e_sgu_out_norm, d_sgu_out),
           ("mlp_norm", mlp_norm, m_mlp_norm, v_mlp_norm, jnp.concatenate([d_mlp0, d_mlp1], axis=0)),
           ("final_norm", final_norm, m_final_norm, v_final_norm, d_final)]
    sizes = [int(np.prod(r[1].shape)) for r in rep]
    n_rep = sum(sizes)
    n_all = n_rep + 4 * D + 1
    width = -(-n_all // (8 * LANES)) * LANES
    pad = 8 * width - n_all
    flat = jnp.concatenate([r[4].reshape(-1) for r in rep]
                           + [d_onorm_full.reshape(-1), dconv_full.reshape(-1), loss_part[0, :1],
                              jnp.zeros((pad,), F32)])
    small_started, small_token = gather_start([[flat.reshape(8, width)]], b_r2[0], name="gather_small_grads_start")

    def finish(grads, a_bufs, b_bufs, t, w, m, v, layer=0, prev=None, tag="", deps=()):
        return run_job(adam_job(grads[t], a_bufs[t], b_bufs[t], w, m, v, layer, prev), index=hosted["job_index"],
                       name=f"adam_{tag}", deps=deps)

    r_w1 = finish(grads_r2, a_r2, b_r2, 0, mlp_w1, m_mlp_w1, v_mlp_w1, 0, r_w1, tag="w1_l0", deps=[tok_r3, small_token])
    r_w2 = finish(grads_r2, a_r2, b_r2, 1, mlp_w2, m_mlp_w2, v_mlp_w2, 0, r_w2, tag="w2_l0", deps=[r_w1[1]])
    b_r2b = chips_wait(st_r2b, r_w2[1], name="reduce_chips_wait_r2b")
    r_w_out_e = finish(grads_r2b, a_r2b, b_r2b, 0, e_w_out, m_e_w_out, v_e_w_out, tag="e_w_out")
    r_w_uq = finish(grads_r2b, a_r2b, b_r2b, 1, e_w_uq, m_e_w_uq, v_e_w_uq, tag="e_w_uq")
    r_w_ukv = finish(grads_r2b, a_r2b, b_r2b, 2, e_w_ukv, m_e_w_ukv, v_e_w_ukv, tag="e_w_ukv")
    b_r3 = chips_wait(st_r3, r_w_out_e[1], name="reduce_chips_wait_r3")
    g_w_in_t = reduce_sum(grads_r3[0], a_r3[0], b_r3[0], name="sum_e_w_in")
    w_in_upd_t = adam_rows(g_w_in_t, jnp.transpose(e_w_in[0]), jnp.transpose(m_e_w_in[0]), jnp.transpose(v_e_w_in[0]),
                           name="adam_e_w_in")
    r_w_in = [jnp.transpose(t)[None] for t in (g_w_in_t, *w_in_upd_t)]

    small_srcs, small_lands = gather_wait(small_started[0], [r_w2[1]], name="gather_small_grads_wait")
    small_all = gather_finish(small_srcs, small_lands, name="gather_small_grads_finish")[0]
    summed = sum_rows8(small_all.reshape(N_DEV * 8, width), 8, name="sum_small_grads").reshape(-1)

    loss = summed[n_rep + 4 * D]

    def pack_rep(i):
        return jnp.concatenate([r[i].reshape(-1) for r in rep]).reshape(n_rep // LANES, LANES)

    g_rep = summed[:n_rep].reshape(n_rep // LANES, LANES)
    d_rep, nm_rep, nv_rep = adam_flat(g_rep, pack_rep(1), pack_rep(2), pack_rep(3), name="adam_replicated")

    def unpack_rep(flat2d):
        out, off = {}, 0
        f = flat2d.reshape(-1)
        for r, n in zip(rep, sizes):
            out[r[0]] = f[off:off + n].reshape(r[1].shape)
            off += n
        return out

    small = {"grad": unpack_rep(g_rep), "delta": unpack_rep(d_rep), "new_m": unpack_rep(nm_rep),
             "new_v": unpack_rep(nv_rep)}
    g_onorm = lax.dynamic_slice(summed[n_rep:n_rep + D].reshape(1, D), (0, me * d_shard), (1, d_shard))
    g_conv = lax.dynamic_slice(summed[n_rep + D:n_rep + 4 * D].reshape(3, D), (0, me * d_shard), (3, d_shard))

    def pack_sharded(norm_part, conv_part):
        return jnp.concatenate([norm_part, conv_part, jnp.zeros((4, d_shard), F32)], axis=0)

    g_sh = pack_sharded(g_onorm, g_conv)
    d_sh, nm_sh, nv_sh = adam_flat(g_sh, pack_sharded(o_norm_mix, o_conv_w[0]), pack_sharded(m_o_norm_mix, m_o_conv_w[0]),
                                   pack_sharded(v_o_norm_mix, v_o_conv_w[0]), name="adam_sharded_small")
    for kind, arr in (("grad", g_sh), ("delta", d_sh), ("new_m", nm_sh), ("new_v", nv_sh)):
        small[kind]["o_norm_mix"] = arr[0:1]
        small[kind]["o_conv_w"] = arr[1:4][None]

    big = {"e_w_in": r_w_in, "e_w_uq": r_w_uq, "e_w_ukv": r_w_ukv, "e_w_out": r_w_out_e, "o_w_in": r_w_in_o,
           "o_w_out": r_w_out_o, "mlp_w1": r_w1, "mlp_w2": r_w2}
    order = ["e_norm_mix", "e_w_in", "e_q_norm", "e_w_uq", "e_kv_norm", "e_w_ukv", "e_v_norm", "e_sgu_w", "e_sgu_b",
             "e_mla_out_norm", "e_sgu_out_norm", "e_w_out", "o_norm_mix", "o_w_in", "o_conv_w", "o_w_out", "mlp_norm",
             "mlp_w1", "mlp_w2", "final_norm"]
    result = [loss, grad_x[None]]
    for ki, kind in enumerate(("grad", "delta", "new_m", "new_v")):
        for nm in order:
            result.append(big[nm][ki] if nm in big else small[kind][nm])
    return tuple(result)
```

```python
import numpy as np
import jax
import jax.numpy as jnp
from jax import lax
from jax.experimental import pallas as pl
from jax.experimental.pallas import tpu as pltpu

BF = jnp.bfloat16
F32 = jnp.float32
MESH = pl.DeviceIdType.MESH
N_DEV = 8

EPS = 1e-6
HEADS = 8
Q_LORA = 512
KV_LORA = 512
QK_NOPE = 128
QK_ROPE = 64
HALF_ROPE = QK_ROPE // 2
V_HEAD = 128
HEAD_PAD = 256
ROPE_BASE = 10000.0
GROUPS = 8
CH = 128
CHUNK = 128
SGU_OUT = GROUPS * CH
MLA_OUT = HEADS * V_HEAD
ATTN_SCALE = float((QK_NOPE + QK_ROPE) ** -0.5)

ADAM_LR = 0.001
ADAM_B1 = 0.9
ADAM_B2 = 0.999
ADAM_EPS = 1e-08
ADAM_WD = 0.01
ADAM_STEP = 10
ADAM_C1 = 1.0 - ADAM_B1 ** ADAM_STEP
ADAM_C2 = 1.0 - ADAM_B2 ** ADAM_STEP

V7X_VMEM_BYTES = 64 * 2 ** 20
VMEM_LIMIT_CAP = V7X_VMEM_BYTES - 6 * 2 ** 20
LANES = 128
ROW_TILE = 512
ATTN_TILE = 512
STREAM_BLOCK_ELEMS = 512 * 1024
MM_TILE = 1024
MM_K_TILE = 2048
MM_K_BLOCK_MAX = 4096


def _padded_bytes(block, dtype):
    dims = [d for d in block if d is not None]
    if len(dims) >= 1:
        dims[-1] = -(-dims[-1] // LANES) * LANES
    if len(dims) >= 2:
        dims[-2] = -(-dims[-2] // 16) * 16
    return int(np.prod(dims)) * jnp.dtype(dtype).itemsize


def _pcall(body, *, name, grid, ins, outs, scratch=(), semantics=None, aliases=None, prefetch=None, deps=()):
    any_spec = pl.BlockSpec(memory_space=pl.ANY)
    if deps:
        n_lead = len(ins) + (1 if prefetch is not None else 0)
        n_deps = len(deps)
        inner = body

        def body(*refs):
            inner(*refs[:n_lead], *refs[n_lead + n_deps:])

        ins = list(ins) + [(d, None, None) for d in deps]
    in_specs = [any_spec if b is None else pl.BlockSpec(b, m) for _, b, m in ins]
    out_specs = [any_spec if b is None else pl.BlockSpec(b, m) for _, _, b, m in outs]
    out_shape = [pltpu.HBM(s, d) for s, d, _, _ in outs]
    est = 0
    for a, b, _ in ins:
        if b is not None:
            est += 2 * _padded_bytes(b, a.dtype)
    for _, d, b, _ in outs:
        if b is not None:
            est += 2 * _padded_bytes(b, d)
    for s in scratch:
        if hasattr(s, "shape") and hasattr(s, "dtype"):
            est += _padded_bytes(s.shape, s.dtype)
    limit = int(min(VMEM_LIMIT_CAP, est + 16 * 2 ** 20))
    params = pltpu.CompilerParams(
        dimension_semantics=semantics or ("arbitrary",) * len(grid), vmem_limit_bytes=limit)
    args = [pltpu.with_memory_space_constraint(a, pltpu.HBM) for a, _, _ in ins]
    if prefetch is not None:
        grid_spec = pltpu.PrefetchScalarGridSpec(
            num_scalar_prefetch=1, grid=grid, in_specs=in_specs, out_specs=out_specs, scratch_shapes=list(scratch))
        call = pl.pallas_call(body, out_shape=out_shape, grid_spec=grid_spec, name=name, compiler_params=params,
                              input_output_aliases=aliases or {})
        return call(prefetch, *args)
    call = pl.pallas_call(body, out_shape=out_shape, grid=grid, in_specs=in_specs, out_specs=out_specs,
                          scratch_shapes=list(scratch), name=name, compiler_params=params,
                          input_output_aliases=aliases or {})
    return call(*args)


def _tile(dim, pref, quantum=LANES):
    if dim <= pref:
        return dim
    t = (pref // quantum) * quantum
    while t >= quantum:
        if dim % t == 0:
            return t
        t -= quantum
    return dim


def _vshape(arr_shape):
    if len(arr_shape) == 2:
        return tuple(arr_shape)
    s, r, c = arr_shape
    return (r, s * c)


def _vblock(arr_shape, br, bc, rc):
    if len(arr_shape) == 2:
        return (br, bc), (lambda *g: rc(*g))
    _, _, c = arr_shape
    assert c % bc == 0, (arr_shape, bc)
    per = c // bc

    def imap(*g):
        ri, ci = rc(*g)
        return (ci // per, ri, ci % per)

    return (None, br, bc), imap


def _shard_width(*shapes):
    w = None
    for s in shapes:
        if len(s) == 3:
            w = s[2] if w is None else int(np.gcd(w, s[2]))
    return w


def mm(a, b, *, name, ta=False, tb=False, out=None, outs=None, epi=None, epi_ins=(), bm=None, bn=None, bk=None,
       deps=(), jobs=(), job_index=None):
    av, bv = _vshape(a.shape), _vshape(b.shape)
    M, K = (av[1], av[0]) if ta else av
    K2, N = (bv[1], bv[0]) if tb else bv
    assert K == K2, (a.shape, b.shape, ta, tb)
    if outs is None:
        outs = [(out[0], out[1], None)]
    a_sw = _shard_width(a.shape)
    b_sw = _shard_width(b.shape)
    o_sw = _shard_width(*[o[0] for o in outs])
    m_lim = a_sw if (ta and a_sw) else None
    k_lim = [w for w in ((a_sw if not ta else None), (b_sw if tb else None)) if w]
    n_lim = [w for w in ((b_sw if not tb else None), o_sw) if w]
    if bm is None:
        bm = _tile(M, min([MM_TILE] + ([m_lim] if m_lim else [])))
    if bn is None:
        bn = _tile(N, min([MM_TILE] + n_lim))
    k_shards = 0
    if tb and len(b.shape) == 3 and bk is None and not (a_sw and not ta):
        k_shards = 1
        while 2 * k_shards <= b.shape[0] and 2 * k_shards * b_sw <= MM_K_BLOCK_MAX:
            k_shards *= 2
        bk = k_shards * b_sw
    if bk is None:
        bk = K if (K <= 4096 and not k_lim) else _tile(K, min([MM_K_TILE] + k_lim))
    assert M % bm == 0 and N % bn == 0 and K % bk == 0, (name, M, N, K, bm, bn, bk)
    nk = K // bk
    grid = (M // bm, N // bn, nk)
    if ta:
        a_blk, a_map = _vblock(a.shape, bk, bm, lambda i, j, k: (k, i))
    else:
        a_blk, a_map = _vblock(a.shape, bm, bk, lambda i, j, k: (i, k))
    if k_shards:
        b_blk, b_map = (k_shards, bn, b_sw), (lambda i, j, k: (k, j, 0))
    elif tb:
        b_blk, b_map = _vblock(b.shape, bn, bk, lambda i, j, k: (j, k))
    else:
        b_blk, b_map = _vblock(b.shape, bk, bn, lambda i, j, k: (k, j))
    dn = (((0 if ta else 1,), (1 if tb else 0,)), ((), ()))
    ins = [(a, a_blk, a_map), (b, b_blk, b_map)] + list(epi_ins)
    out_list = []
    for shape, dtype, cols in outs:
        cols = cols or bn
        blk, imap = _vblock(shape, bm, cols, lambda i, j, k: (i, j))
        out_list.append((shape, dtype, blk, imap))
    n_e, n_o = len(epi_ins), len(out_list)

    n_steps = grid[0] * grid[1] * nk
    built = [job(n_steps) for job in jobs]
    aliases = {}
    job_slices = []
    if built:
        def lin(i, j, k):
            return (i * grid[1] + j) * nk + k

        ins = [(arr, blk, None if blk is None else (lambda i, j, k, s, f=f: f(i, j, k))) for arr, blk, f in ins]
        out_list = [(sh, dt, blk, (lambda i, j, k, s, f=f: f(i, j, k))) for sh, dt, blk, f in out_list]
        n_main_in, n_main_out = len(ins), len(out_list)
        for jb in built:
            i0, o0 = len(ins), len(out_list)
            ins += [(arr, blk, None if blk is None else (lambda i, j, k, s, f=f: f(lin(i, j, k), s)))
                    for arr, blk, f in jb["ins"]]
            out_list += [(sh, dt, blk, (lambda i, j, k, s, f=f: f(lin(i, j, k), s))) for sh, dt, blk, f in jb["outs"]]
            aliases.update({1 + i0 + ai: o0 + ao for ai, ao in jb["aliases"].items()})
            job_slices.append((i0, len(jb["ins"]), o0, len(jb["outs"])))
    n_in_total = len(ins)

    def body(*refs):
        if built:
            refs = refs[1:]
        a_ref, b_ref = refs[0], refs[1]
        e_refs = refs[2:2 + n_e]
        o_refs = refs[n_in_total:n_in_total + n_o]
        for jb, (i0, ni, o0, no) in zip(built, job_slices):
            jb["fn"](refs[i0:i0 + ni], refs[n_in_total + o0:n_in_total + o0 + no])

        def finish(acc):
            res = epi(acc, *e_refs) if epi is not None else (acc,)
            for o_ref, r in zip(o_refs, res):
                o_ref[...] = r.astype(o_ref.dtype)

        x = a_ref[...].astype(BF)
        y = b_ref[...].astype(BF)
        if k_shards:
            p = None
            for s in range(k_shards):
                part = lax.dot_general(x[:, s * b_sw:(s + 1) * b_sw], y[s], dn, preferred_element_type=F32)
                p = part if p is None else p + part
        else:
            p = lax.dot_general(x, y, dn, preferred_element_type=F32)
        if nk == 1:
            finish(p)
        else:
            acc_ref = refs[-1]
            k = pl.program_id(2)

            @pl.when(k == 0)
            def _():
                acc_ref[...] = p

            @pl.when(k > 0)
            def _():
                acc_ref[...] += p

            @pl.when(k == nk - 1)
            def _():
                finish(acc_ref[...])

    scratch = [pltpu.VMEM((bm, bn), F32)] if nk > 1 else []
    res = _pcall(body, name=name, grid=grid, ins=ins, outs=out_list, scratch=scratch, deps=deps,
                 semantics=("parallel", "parallel", "arbitrary"), prefetch=job_index if built else None, aliases=aliases)
    main = res[0] if n_o == 1 else res[:n_o]
    if not built:
        return main
    return main, [res[o0:o0 + no] for _, _, o0, no in job_slices]


_GELU_K = float(np.sqrt(2.0 / np.pi))
_GELU_C = 0.044715


def _gelu(x):
    t = jnp.tanh(_GELU_K * (x + _GELU_C * (x * x * x)))
    return 0.5 * x * (1.0 + t)


def _gelu_grad(x):
    t = jnp.tanh(_GELU_K * (x + _GELU_C * (x * x * x)))
    return 0.5 * (1.0 + t) + 0.5 * x * (1.0 - t * t) * (_GELU_K * (1.0 + 3.0 * _GELU_C * (x * x)))


def _rstd(x):
    return lax.rsqrt(jnp.mean(x * x, axis=-1, keepdims=True) + EPS)


def _rms_bwd(x, gain, dy):
    r = _rstd(x)
    xh = x * r
    gdy = dy * gain
    dx = r * (gdy - xh * jnp.mean(gdy * xh, axis=-1, keepdims=True))
    return dx, dy * xh


def _rope_fwd(x, cos_t, sin_t):
    return x * cos_t + pltpu.roll(x, 2 * HALF_ROPE, 1) * sin_t


def _rope_bwd(dy, cos_t, sin_t):
    return dy * cos_t + pltpu.roll(dy * sin_t, 2 * HALF_ROPE, 1)


def _acc_rows(ref, val, first):
    s = jnp.sum(val, axis=0, keepdims=True)

    @pl.when(first)
    def _():
        ref[...] = s

    @pl.when(jnp.logical_not(first))
    def _():
        ref[...] += s


def rms_fwd(x, gain, *, name, col_block=0, width=None, deps=()):
    T = x.shape[0]
    width = width or x.shape[1]
    tm = _tile(T, ROW_TILE, 8)

    def body(x_ref, g_ref, o_ref):
        v = x_ref[...]
        o_ref[...] = (v * _rstd(v) * g_ref[...]).astype(BF)

    return _pcall(body, name=name, grid=(T // tm,),
                  ins=[(x, (tm, width), lambda i: (i, col_block)), (gain, (1, width), lambda i: (0, 0))],
                  outs=[((T, width), BF, (tm, width), lambda i: (i, 0))], semantics=("parallel",), deps=deps)[0]


def rms_bwd(x, gain, dy, *, name, col_block=0, dres=None, want_f32=True, want_bf=True, into=None, deps=()):
    T, width = dy.shape
    tm = _tile(T, ROW_TILE, 8)
    has_res = dres is not None

    def body(*refs):
        x_ref, g_ref, dy_ref = refs[:3]
        pos = 3
        res_ref = None
        if has_res:
            res_ref = refs[pos]
            pos += 1
        if into is not None:
            pos += 1
        outs = refs[pos:]
        dx, dg_rows = _rms_bwd(x_ref[...], g_ref[...], dy_ref[...])
        if has_res:
            dx = dx + res_ref[...]
        o = 0
        if want_f32:
            outs[o][...] = dx
            o += 1
        if want_bf:
            outs[o][...] = dx.astype(BF)
            o += 1
        _acc_rows(outs[o], dg_rows, pl.program_id(0) == 0)

    ins = [(x, (tm, width), lambda i: (i, col_block)), (gain, (1, width), lambda i: (0, 0)),
           (dy, (tm, width), lambda i: (i, 0))]
    if has_res:
        ins.append((dres, (tm, width), lambda i: (i, 0)))
    outs = []
    aliases = {}
    if want_f32:
        outs.append(((T, width), F32, (tm, width), lambda i: (i, 0)))
    if want_bf and into is not None:
        ins.append((into, None, None))
        aliases[len(ins) - 1] = len(outs)
        outs.append((into.shape, BF, (tm, width), lambda i: (i, col_block)))
    elif want_bf:
        outs.append(((T, width), BF, (tm, width), lambda i: (i, 0)))
    outs.append(((1, width), F32, (1, width), lambda i: (0, 0)))
    return _pcall(body, name=name, grid=(T // tm,), ins=ins, outs=outs, aliases=aliases, deps=deps)


def mla_prep(proj, q_norm, kv_norm, cos_t, sin_t, *, name):
    T = proj.shape[0]
    tm = _tile(T, ROW_TILE, 8)
    kr_block = (proj.shape[1] - LANES) // LANES

    def body(cq_ref, ckv_ref, kr_ref, qg_ref, kg_ref, cos_ref, sin_ref, qn_ref, kvn_ref, krope_ref):
        cq = cq_ref[...]
        qn_ref[...] = (cq * _rstd(cq) * qg_ref[...]).astype(BF)
        ckv = ckv_ref[...]
        kvn_ref[...] = (ckv * _rstd(ckv) * kg_ref[...]).astype(BF)
        krope_ref[...] = _rope_fwd(kr_ref[...], cos_ref[...], sin_ref[...]).astype(BF)

    return _pcall(
        body, name=name, grid=(T // tm,),
        ins=[(proj, (tm, Q_LORA), lambda i: (i, 0)), (proj, (tm, KV_LORA), lambda i: (i, 1)),
             (proj, (tm, LANES), lambda i: (i, kr_block)),
             (q_norm, (1, Q_LORA), lambda i: (0, 0)), (kv_norm, (1, KV_LORA), lambda i: (0, 0)),
             (cos_t, (tm, LANES), lambda i: (i, 0)), (sin_t, (tm, LANES), lambda i: (i, 0))],
        outs=[((T, Q_LORA), BF, (tm, Q_LORA), lambda i: (i, 0)), ((T, KV_LORA), BF, (tm, KV_LORA), lambda i: (i, 0)),
              ((T, LANES), BF, (tm, LANES), lambda i: (i, 0))],
        semantics=("parallel",))


def _attn_scores(q, k_blk, diagonal):
    s = lax.dot_general(q, k_blk, (((1,), (1,)), ((), ())), preferred_element_type=F32) * ATTN_SCALE
    if diagonal:
        row = lax.broadcasted_iota(jnp.int32, s.shape, 0)
        col = lax.broadcasted_iota(jnp.int32, s.shape, 1)
        s = jnp.where(col <= row, s, -jnp.inf)
    return s


def attn_fwd(q, k, v, *, name):
    T = q.shape[0]
    tq = _tile(T, ATTN_TILE, 8)

    def body(q_ref, k_ref, v_ref, o_ref, lse_ref):
        i = pl.program_id(1)
        qv = q_ref[...]

        def block(kb, carry, diagonal):
            m, l, acc = carry
            start = pl.multiple_of(kb * tq, tq)
            s = _attn_scores(qv, k_ref[pl.ds(start, tq), :], diagonal)
            m_new = jnp.maximum(m, jnp.max(s, axis=-1, keepdims=True))
            alpha = jnp.exp(m - m_new)
            p = jnp.exp(s - m_new)
            l = alpha * l + jnp.sum(p, axis=-1, keepdims=True)
            acc = alpha * acc + jnp.dot(p.astype(BF), v_ref[pl.ds(start, tq), :], preferred_element_type=F32)
            return m_new, l, acc

        init = (jnp.full((tq, 1), -jnp.inf, F32), jnp.zeros((tq, 1), F32), jnp.zeros((tq, V_HEAD), F32))
        carry = lax.fori_loop(0, i, lambda kb, c: block(kb, c, False), init)
        m, l, acc = block(i, carry, True)
        o_ref[...] = acc / l
        lse_ref[...] = jnp.broadcast_to(m + jnp.log(l), (tq, V_HEAD))

    return _pcall(
        body, name=name, grid=(HEADS, T // tq),
        ins=[(q, (tq, HEAD_PAD), lambda h, i: (i, h)), (k, (T, HEAD_PAD), lambda h, i: (0, h)),
             (v, (T, V_HEAD), lambda h, i: (0, h))],
        outs=[((T, MLA_OUT), F32, (tq, V_HEAD), lambda h, i: (i, h)),
              ((T, MLA_OUT), F32, (tq, V_HEAD), lambda h, i: (i, h))], semantics=("parallel", "parallel"))


def attn_bwd(q, k, v, o, lse, do, *, name):
    T = q.shape[0]
    tq = _tile(T, ATTN_TILE, 8)

    def body(q_ref, k_ref, v_ref, o_ref, lse_ref, do_ref, dq_ref, dk_ref, dv_ref):
        i = pl.program_id(1)

        @pl.when(i == 0)
        def _():
            dk_ref[...] = jnp.zeros_like(dk_ref)
            dv_ref[...] = jnp.zeros_like(dv_ref)

        qv = q_ref[...]
        do_t = do_ref[...]
        lse_v = lse_ref[:, 0:1]
        delta = jnp.sum(do_t.astype(F32) * o_ref[...], axis=-1, keepdims=True)

        def block(kb, dq, diagonal):
            start = pl.multiple_of(kb * tq, tq)
            k_blk = k_ref[pl.ds(start, tq), :]
            v_blk = v_ref[pl.ds(start, tq), :]
            p = jnp.exp(_attn_scores(qv, k_blk, diagonal) - lse_v)
            dp = lax.dot_general(do_t, v_blk, (((1,), (1,)), ((), ())), preferred_element_type=F32)
            ds = (p * (dp - delta) * ATTN_SCALE).astype(BF)
            dk_ref[pl.ds(start, tq), :] += lax.dot_general(ds, qv, (((0,), (0,)), ((), ())), preferred_element_type=F32)
            dv_ref[pl.ds(start, tq), :] += lax.dot_general(p.astype(BF), do_t, (((0,), (0,)), ((), ())),
                                                          preferred_element_type=F32)
            return dq + jnp.dot(ds, k_blk, preferred_element_type=F32)

        dq = lax.fori_loop(0, i, lambda kb, c: block(kb, c, False), jnp.zeros((tq, HEAD_PAD), F32))
        dq_ref[...] = block(i, dq, True)

    return _pcall(
        body, name=name, grid=(HEADS, T // tq),
        ins=[(q, (tq, HEAD_PAD), lambda h, i: (i, h)), (k, (T, HEAD_PAD), lambda h, i: (0, h)),
             (v, (T, V_HEAD), lambda h, i: (0, h)), (o, (tq, V_HEAD), lambda h, i: (i, h)),
             (lse, (tq, V_HEAD), lambda h, i: (i, h)), (do, (tq, V_HEAD), lambda h, i: (i, h))],
        outs=[((T, HEADS * HEAD_PAD), F32, (tq, HEAD_PAD), lambda h, i: (i, h)),
              ((T, HEADS * HEAD_PAD), F32, (T, HEAD_PAD), lambda h, i: (0, h)),
              ((T, MLA_OUT), F32, (T, V_HEAD), lambda h, i: (0, h))],
        semantics=("parallel", "arbitrary"))


def mla_bwd_prep(dq, dk, dv, cos_t, sin_t, dproj, *, name):
    T = dq.shape[0]
    tm = _tile(T, ROW_TILE, 8)
    kr_block = (dproj.shape[1] - LANES) // LANES

    def body(dq_ref, dk_ref, dv_ref, cos_ref, sin_ref, dproj_in, dql_ref, dkvl_ref, dkr_ref):
        cos_v, sin_v = cos_ref[...], sin_ref[...]
        kr = jnp.zeros((tm, LANES), F32)
        for h in range(HEADS):
            lo = h * HEAD_PAD
            dql_ref[:, lo:lo + QK_NOPE] = dq_ref[:, lo:lo + QK_NOPE].astype(BF)
            dql_ref[:, lo + QK_NOPE:lo + HEAD_PAD] = _rope_bwd(
                dq_ref[:, lo + QK_NOPE:lo + HEAD_PAD], cos_v, sin_v).astype(BF)
            dkvl_ref[:, lo:lo + QK_NOPE] = dk_ref[:, lo:lo + QK_NOPE].astype(BF)
            dkvl_ref[:, lo + QK_NOPE:lo + HEAD_PAD] = dv_ref[:, h * V_HEAD:(h + 1) * V_HEAD].astype(BF)
            kr = kr + dk_ref[:, lo + QK_NOPE:lo + HEAD_PAD]
        dkr_ref[...] = _rope_bwd(kr, cos_v, sin_v).astype(BF)

    W = HEADS * HEAD_PAD
    return _pcall(
        body, name=name, grid=(T // tm,),
        ins=[(dq, (tm, W), lambda i: (i, 0)), (dk, (tm, W), lambda i: (i, 0)), (dv, (tm, MLA_OUT), lambda i: (i, 0)),
             (cos_t, (tm, LANES), lambda i: (i, 0)), (sin_t, (tm, LANES), lambda i: (i, 0)), (dproj, None, None)],
        outs=[((T, W), BF, (tm, W), lambda i: (i, 0)), ((T, W), BF, (tm, W), lambda i: (i, 0)),
              (dproj.shape, BF, (tm, LANES), lambda i: (i, kr_block))],
        aliases={5: 2}, semantics=("parallel",))


def _group_norm_stats(vg):
    mu = jnp.mean(vg, axis=-1, keepdims=True)
    d = vg - mu
    r = lax.rsqrt(jnp.mean(d * d, axis=-1, keepdims=True) + EPS)
    return d * r, r


def mix_fwd(a, proj, g_mla, g_sgu, v_gain, w_tril, b_full, *, name):
    T = a.shape[0]
    tm = _tile(T, ROW_TILE, CHUNK)
    n_chunk = tm // CHUNK

    def body(a_ref, u_ref, v_ref, gm_ref, gs_ref, vg_ref, w_ref, b_ref, o_ref, s_scr):
        av = a_ref[...]
        o_ref[:, :MLA_OUT] = (av * _rstd(av) * gm_ref[...]).astype(BF)
        for g in range(GROUPS):
            sl = slice(g * CH, (g + 1) * CH)
            vhat, _ = _group_norm_stats(_gelu(v_ref[:, sl]))
            vn = (vhat * vg_ref[:, sl]).astype(BF)
            u = _gelu(u_ref[:, sl])
            for ci in range(n_chunk):
                rs = slice(ci * CHUNK, (ci + 1) * CHUNK)
                y = jnp.dot(w_ref[g], vn[rs], preferred_element_type=F32) + b_ref[:, sl]
                s_scr[rs, sl] = u[rs] * y
        s = s_scr[...]
        o_ref[:, MLA_OUT:] = (s * _rstd(s) * gs_ref[...]).astype(BF)

    return _pcall(
        body, name=name, grid=(T // tm,),
        ins=[(a, (tm, MLA_OUT), lambda i: (i, 0)), (proj, (tm, SGU_OUT), lambda i: (i, 1)),
             (proj, (tm, SGU_OUT), lambda i: (i, 2)), (g_mla, (1, MLA_OUT), lambda i: (0, 0)),
             (g_sgu, (1, SGU_OUT), lambda i: (0, 0)), (v_gain, (1, SGU_OUT), lambda i: (0, 0)),
             (w_tril, (GROUPS, CHUNK, CHUNK), lambda i: (0, 0, 0)), (b_full, (CHUNK, SGU_OUT), lambda i: (0, 0))],
        outs=[((T, MLA_OUT + SGU_OUT), BF, (tm, MLA_OUT + SGU_OUT), lambda i: (i, 0))],
        scratch=[pltpu.VMEM((tm, SGU_OUT), F32)], semantics=("parallel",))[0]


def mix_bwd(dmixed, a, proj, g_mla, g_sgu, v_gain, w_tril, w_tril_t, b_full, *, name, deps=()):
    T = a.shape[0]
    tm = _tile(T, ROW_TILE, CHUNK)
    n_chunk = tm // CHUNK
    uv0 = Q_LORA + KV_LORA

    def body(dm_a_ref, dm_s_ref, a_ref, u_ref, v_ref, gm_ref, gs_ref, vg_ref, w_ref, wt_ref, b_ref,
             da_ref, duv_ref, dgm_ref, dgs_ref, dvg_ref, dw_ref, db_ref, s_scr, y_scr):
        first = pl.program_id(0) == 0
        duv_ref[:, :uv0] = jnp.zeros((tm, uv0), BF)
        duv_ref[:, uv0 + 2 * SGU_OUT:] = jnp.zeros((tm, duv_ref.shape[1] - uv0 - 2 * SGU_OUT), BF)
        da, dgm_rows = _rms_bwd(a_ref[...], gm_ref[...], dm_a_ref[...])
        da_ref[...] = da.astype(BF)
        _acc_rows(dgm_ref, dgm_rows, first)

        for g in range(GROUPS):
            sl = slice(g * CH, (g + 1) * CH)
            vhat, _ = _group_norm_stats(_gelu(v_ref[:, sl]))
            vn = (vhat * vg_ref[:, sl]).astype(BF)
            u = _gelu(u_ref[:, sl])
            for ci in range(n_chunk):
                rs = slice(ci * CHUNK, (ci + 1) * CHUNK)
                y = jnp.dot(w_ref[g], vn[rs], preferred_element_type=F32) + b_ref[:, sl]
                y_scr[rs, sl] = y
                s_scr[rs, sl] = u[rs] * y
        ds, dgs_rows = _rms_bwd(s_scr[...], gs_ref[...], dm_s_ref[...])
        _acc_rows(dgs_ref, dgs_rows, first)
        s_scr[...] = ds

        @pl.when(first)
        def _():
            dw_ref[...] = jnp.zeros_like(dw_ref)
            db_ref[...] = jnp.zeros_like(db_ref)

        for g in range(GROUPS):
            sl = slice(g * CH, (g + 1) * CH)
            upre = u_ref[:, sl]
            vpre = v_ref[:, sl]
            u = _gelu(upre)
            vhat, r = _group_norm_stats(_gelu(vpre))
            gain = vg_ref[:, sl]
            vn = (vhat * gain).astype(BF)
            dsg = s_scr[:, sl]
            duv_ref[:, uv0 + g * CH:uv0 + (g + 1) * CH] = (dsg * y_scr[:, sl] * _gelu_grad(upre)).astype(BF)
            dy = dsg * u
            dyb = dy.astype(BF)
            dvn_parts = []
            for ci in range(n_chunk):
                rs = slice(ci * CHUNK, (ci + 1) * CHUNK)
                dvn_parts.append(jnp.dot(wt_ref[g], dyb[rs], preferred_element_type=F32))
                dw_ref[g] += lax.dot_general(dyb[rs], vn[rs], (((1,), (1,)), ((), ())), preferred_element_type=F32)
                db_ref[:, sl] += jnp.broadcast_to(jnp.sum(dy[rs], axis=-1, keepdims=True), (CHUNK, CH))
            dvn = dvn_parts[0] if n_chunk == 1 else jnp.concatenate(dvn_parts, axis=0)
            _acc_rows(dvg_ref.at[:, sl], dvn * vhat, first)
            dvh = dvn * gain
            dvg = r * (dvh - jnp.mean(dvh, axis=-1, keepdims=True)
                       - vhat * jnp.mean(dvh * vhat, axis=-1, keepdims=True))
            duv_ref[:, uv0 + SGU_OUT + g * CH:uv0 + SGU_OUT + (g + 1) * CH] = (dvg * _gelu_grad(vpre)).astype(BF)

    return _pcall(
        body, name=name, grid=(T // tm,),
        ins=[(dmixed, (tm, MLA_OUT), lambda i: (i, 0)), (dmixed, (tm, SGU_OUT), lambda i: (i, 1)),
             (a, (tm, MLA_OUT), lambda i: (i, 0)), (proj, (tm, SGU_OUT), lambda i: (i, 1)),
             (proj, (tm, SGU_OUT), lambda i: (i, 2)), (g_mla, (1, MLA_OUT), lambda i: (0, 0)),
             (g_sgu, (1, SGU_OUT), lambda i: (0, 0)), (v_gain, (1, SGU_OUT), lambda i: (0, 0)),
             (w_tril, (GROUPS, CHUNK, CHUNK), lambda i: (0, 0, 0)), (w_tril_t, (GROUPS, CHUNK, CHUNK), lambda i: (0, 0, 0)),
             (b_full, (CHUNK, SGU_OUT), lambda i: (0, 0))],
        outs=[((T, MLA_OUT), BF, (tm, MLA_OUT), lambda i: (i, 0)),
              ((T, proj.shape[1]), BF, (tm, proj.shape[1]), lambda i: (i, 0)),
              ((1, MLA_OUT), F32, (1, MLA_OUT), lambda i: (0, 0)), ((1, SGU_OUT), F32, (1, SGU_OUT), lambda i: (0, 0)),
              ((1, SGU_OUT), F32, (1, SGU_OUT), lambda i: (0, 0)),
              ((GROUPS, CHUNK, CHUNK), F32, (GROUPS, CHUNK, CHUNK), lambda i: (0, 0, 0)),
              ((CHUNK, SGU_OUT), F32, (CHUNK, SGU_OUT), lambda i: (0, 0))],
        scratch=[pltpu.VMEM((tm, SGU_OUT), F32), pltpu.VMEM((tm, SGU_OUT), F32)], deps=deps)


def _shift_down(z, n, row):
    return jnp.where(row >= n, pltpu.roll(z, n, 0), 0.0)


def _shift_up(z, n, row, T):
    return jnp.where(row < T - n, pltpu.roll(z, T - n, 0), 0.0)


def conv_fwd(proj, conv_w, *, name):
    T, D3 = proj.shape
    D = D3 // 3
    tn = _tile(D, 256)
    nj = D // tn

    def body(b_ref, c_ref, x_ref, w_ref, o_ref):
        row = lax.broadcasted_iota(jnp.int32, (T, tn), 0)
        z = c_ref[...] * x_ref[...]
        zc = w_ref[2:3, :] * z + w_ref[1:2, :] * _shift_down(z, 1, row) + w_ref[0:1, :] * _shift_down(z, 2, row)
        o_ref[...] = (b_ref[...] * zc).astype(BF)

    return _pcall(
        body, name=name, grid=(nj,),
        ins=[(proj, (T, tn), lambda j: (0, j)), (proj, (T, tn), lambda j: (0, nj + j)),
             (proj, (T, tn), lambda j: (0, 2 * nj + j)), (conv_w, (3, tn), lambda j: (0, j))],
        outs=[((T, D), BF, (T, tn), lambda j: (0, j))], semantics=("parallel",))[0]


def conv_bwd(dg, proj, conv_w, *, name, deps=()):
    T, D3 = proj.shape
    D = D3 // 3
    tn = _tile(D, 256)
    nj = D // tn

    def body(dg_ref, b_ref, c_ref, x_ref, w_ref, dp_ref, dw_ref, dc_scr, dx_scr):
        part = pl.program_id(1)

        @pl.when(part == 0)
        def _():
            row = lax.broadcasted_iota(jnp.int32, (T, tn), 0)
            c, x = c_ref[...], x_ref[...]
            z = c * x
            z1 = _shift_down(z, 1, row)
            z2 = _shift_down(z, 2, row)
            dgv = dg_ref[...]
            zc = w_ref[2:3, :] * z + w_ref[1:2, :] * z1 + w_ref[0:1, :] * z2
            dp_ref[...] = (dgv * zc).astype(BF)
            dzc = dgv * b_ref[...]
            dw_ref[0:1, :] = jnp.sum(dzc * z2, axis=0, keepdims=True)
            dw_ref[1:2, :] = jnp.sum(dzc * z1, axis=0, keepdims=True)
            dw_ref[2:3, :] = jnp.sum(dzc * z, axis=0, keepdims=True)
            dz = (w_ref[2:3, :] * dzc + w_ref[1:2, :] * _shift_up(dzc, 1, row, T)
                  + w_ref[0:1, :] * _shift_up(dzc, 2, row, T))
            dc_scr[...] = (dz * x).astype(BF)
            dx_scr[...] = (dz * c).astype(BF)

        @pl.when(part == 1)
        def _():
            dp_ref[...] = dc_scr[...]

        @pl.when(part == 2)
        def _():
            dp_ref[...] = dx_scr[...]

    return _pcall(
        body, name=name, grid=(nj, 3),
        ins=[(dg, (T, tn), lambda j, p: (0, j)), (proj, (T, tn), lambda j, p: (0, j)),
             (proj, (T, tn), lambda j, p: (0, nj + j)), (proj, (T, tn), lambda j, p: (0, 2 * nj + j)),
             (conv_w, (3, tn), lambda j, p: (0, j))],
        outs=[((T, D3), BF, (T, tn), lambda j, p: (0, p * nj + j)), ((3, D), F32, (3, tn), lambda j, p: (0, j))],
        scratch=[pltpu.VMEM((T, tn), BF), pltpu.VMEM((T, tn), BF)], semantics=("parallel", "arbitrary"), deps=deps)


def loss_bwd(x_parts, gain, target, *, name):
    T, D = target.shape
    tm = _tile(T, ROW_TILE, 8)
    n_x = len(x_parts)

    def body(*refs):
        x_refs = refs[:n_x]
        g_ref, t_ref, dx_ref, dxb_ref, dg_ref, loss_ref = refs[n_x:]
        first = pl.program_id(0) == 0
        xv = jnp.concatenate([r[...] for r in x_refs], axis=-1) if n_x > 1 else x_refs[0][...]
        r = _rstd(xv)
        xh = xv * r
        gain_v = g_ref[...]
        err = xh * gain_v - t_ref[...]
        part = 0.5 * jnp.sum(jnp.mean(err * err, axis=-1, keepdims=True), axis=0, keepdims=True)
        _acc_rows(loss_ref, jnp.broadcast_to(part, (1, LANES)), first)
        dy = err * (1.0 / D)
        gdy = dy * gain_v
        dx = r * (gdy - xh * jnp.mean(gdy * xh, axis=-1, keepdims=True))
        dx_ref[...] = dx
        dxb_ref[...] = dx.astype(BF)
        _acc_rows(dg_ref, dy * xh, first)

    return _pcall(
        body, name=name, grid=(T // tm,),
        ins=[(p, (tm, D // n_x), lambda i: (i, 0)) for p in x_parts]
        + [(gain, (1, D), lambda i: (0, 0)), (target, (tm, D), lambda i: (i, 0))],
        outs=[((T, D), F32, (tm, D), lambda i: (i, 0)), ((T, D), BF, (tm, D), lambda i: (i, 0)),
              ((1, D), F32, (1, D), lambda i: (0, 0)), ((1, LANES), F32, (1, LANES), lambda i: (0, 0))])


def _adamw(g, w, m, v):
    m = ADAM_B1 * m + (1.0 - ADAM_B1) * g
    v = ADAM_B2 * v + (1.0 - ADAM_B2) * (g * g)
    m_hat = m / ADAM_C1
    v_hat = v / ADAM_C2
    delta = -ADAM_LR * (m_hat / (jnp.sqrt(v_hat) + ADAM_EPS) + ADAM_WD * w)
    return delta, m, v


def adam_flat(g, w, m, v, *, name):
    def body(g_ref, w_ref, m_ref, v_ref, d_ref, nm_ref, nv_ref):
        d, nm, nv = _adamw(g_ref[...], w_ref[...], m_ref[...], v_ref[...])
        d_ref[...] = d
        nm_ref[...] = nm
        nv_ref[...] = nv

    blk = g.shape
    zero = lambda: (0, 0)
    return _pcall(body, name=name, grid=(),
                  ins=[(t, blk, zero) for t in (g, w, m, v)],
                  outs=[(blk, F32, blk, zero)] * 3)


def _chip_slots():
    x, y, c = lax.axis_index("x"), lax.axis_index("y"), lax.axis_index("c")
    chips = [(1 - x, y), (x, 1 - y), (1 - x, 1 - y)]
    return x, y, c, chips


def device_index():
    x, y, c, chips = _chip_slots()
    return jnp.stack([4 * x + 2 * y + c, 2 * x + y] + [4 * cx + 2 * cy + c for cx, cy in chips]
                     + [2 * cx + cy for cx, cy in chips]).astype(jnp.int32)


def _job_rows(R, C, n_steps):
    if n_steps is None:
        n_steps = max(1, R * C // STREAM_BLOCK_ELEMS)
    n_blk = max([d for d in range(1, n_steps + 1) if R % d == 0 and (R // d) % 16 == 0] or [1])
    return R // n_blk, n_blk


def run_job(job, *, index, name, deps=()):
    jb = job(None)
    n_in = len(jb["ins"])

    def body(idx_ref, *refs):
        jb["fn"](refs[:n_in], refs[n_in:n_in + len(jb["outs"])])

    return _pcall(body, name=name, grid=(jb["n_blk"],), ins=jb["ins"], outs=jb["outs"], prefetch=index,
                  aliases={1 + a: o for a, o in jb["aliases"].items()}, semantics=("parallel",), deps=deps)


def adam_job(gs, a_buf, b_buf, w, m, v, layer, prev):
    L, R, C = w.shape

    def build(n_steps):
        tr, n_blk = _job_rows(R, C, n_steps)
        blk = (None, tr, C)
        row = lambda t: jnp.minimum(t, n_blk - 1)
        ins = [(gs, blk, lambda t, s: (s[0], row(t), 0)), (a_buf, blk, lambda t, s: (s[1], row(t), 0))]
        ins += [(b_buf, blk, lambda t, s, j=j: (j, row(t), 0)) for j in range(3)]
        ins += [(p, blk, lambda t, s: (layer, row(t), 0)) for p in (w, m, v)]
        ins += [(p, None, None) for p in (prev or [])]

        def fn(i, o):
            g = ((((i[0][...].astype(F32) + i[1][...].astype(F32)) + i[2][...].astype(F32))
                  + i[3][...].astype(F32)) + i[4][...].astype(F32))
            d, nm, nv = _adamw(g, i[5][...], i[6][...], i[7][...])
            o[0][...] = g
            o[1][...] = d
            o[2][...] = nm
            o[3][...] = nv

        return dict(ins=ins, outs=[((L, R, C), F32, blk, lambda t, s: (layer, row(t), 0))] * 4, fn=fn,
                    aliases={8 + o: o for o in range(4)} if prev else {}, n_blk=n_blk)

    return build


def pair_job(gs, a_buf):
    _, R, C = gs.shape

    def build(n_steps):
        tr, n_blk = _job_rows(R, C, n_steps)
        blk = (None, tr, C)
        row = lambda t: jnp.minimum(t, n_blk - 1)
        ins = [(gs, blk, lambda t, s, j=j: (s[2 + j], row(t), 0)) for j in range(3)]
        ins += [(a_buf, blk, lambda t, s, j=j: (s[5 + j], row(t), 0)) for j in range(3)]

        def fn(i, o):
            for j in range(3):
                o[0][j] = (i[j][...].astype(F32) + i[3 + j][...].astype(F32)).astype(BF)

        return dict(ins=ins, outs=[((3, R, C), BF, (3, tr, C), lambda t, s: (0, row(t), 0))], fn=fn, aliases={},
                    n_blk=n_blk)

    return build


def reduce_sum(gs, a_buf, b_buf, *, name):
    _, R, C = gs.shape
    tr = _tile(R, 256, 16)
    x, y, c, _ = _chip_slots()
    idx = jnp.stack([4 * x + 2 * y + c, 2 * x + y]).astype(jnp.int32)

    def body(idx_ref, g_ref, a_ref, b0_ref, b1_ref, b2_ref, o_ref):
        o_ref[...] = ((((g_ref[...].astype(F32) + a_ref[...].astype(F32)) + b0_ref[...].astype(F32))
                       + b1_ref[...].astype(F32)) + b2_ref[...].astype(F32))

    blk3 = (None, tr, C)
    return _pcall(body, name=name, grid=(R // tr,),
                  ins=[(gs, blk3, lambda i, s: (s[0], i, 0)), (a_buf, blk3, lambda i, s: (s[1], i, 0)),
                       (b_buf, blk3, lambda i, s: (0, i, 0)), (b_buf, blk3, lambda i, s: (1, i, 0)),
                       (b_buf, blk3, lambda i, s: (2, i, 0))],
                  outs=[((R, C), F32, (tr, C), lambda i, s: (i, 0))], prefetch=idx, semantics=("parallel",))[0]


def adam_rows(g, w, m, v, *, name):
    R, C = g.shape
    tr = _tile(R, 256, 8)

    def body(g_ref, w_ref, m_ref, v_ref, d_ref, nm_ref, nv_ref):
        d, nm, nv = _adamw(g_ref[...], w_ref[...], m_ref[...], v_ref[...])
        d_ref[...] = d
        nm_ref[...] = nm
        nv_ref[...] = nv

    spec = ((tr, C), lambda i: (i, 0))
    return _pcall(body, name=name, grid=(R // tr,), ins=[(t, *spec) for t in (g, w, m, v)],
                  outs=[((R, C), F32, *spec)] * 3, semantics=("parallel",))


def sum_rows8(gathered, rows, *, name):
    W = gathered.shape[1]

    def body(g_ref, o_ref):
        acc = g_ref[0:rows, :]
        for d in range(1, N_DEV):
            acc = acc + g_ref[d * rows:(d + 1) * rows, :]
        o_ref[...] = acc

    return _pcall(body, name=name, grid=(), ins=[(gathered, gathered.shape, lambda: (0, 0))],
                  outs=[((rows, W), F32, (rows, W), lambda: (0, 0))])[0]


HBM_SPEC = pl.BlockSpec(memory_space=pltpu.HBM)
SEM_SPEC = pl.BlockSpec(memory_space=pltpu.SEMAPHORE)
ANY_SPEC = pl.BlockSpec(memory_space=pl.ANY)
DATAFLOW = pltpu.SideEffectType.DATAFLOW_SIDE_EFFECTING


def _in_hbm(v):
    return pltpu.with_memory_space_constraint(v, pltpu.HBM)


def _slot(p):
    return 4 * p[0] + 2 * p[1] + p[2]


def _gather_peers():
    x, y, c, chips = _chip_slots()
    return (x, y, c), [(x, y, 1 - c)] + [(*chip, c) for chip in chips]


def gather_start(groups, after, *, name):
    flat = [s for g in groups for s in g]
    n, n_g = len(flat), len(groups)
    where = [(gi, ti) for gi, g in enumerate(groups) for ti in range(len(g))]

    def body(*refs):
        src, land = refs[:n], refs[n:2 * n]
        sems = refs[2 * n + 1:2 * n + 1 + 2 * n_g]
        me, peers = _gather_peers()
        for t in range(n):
            gi, ti = where[t]
            for k, to in enumerate(peers):
                pltpu.make_async_remote_copy(
                    src_ref=src[t], dst_ref=land[t].at[_slot(me)], send_sem=sems[2 * gi].at[4 * ti + k],
                    recv_sem=sems[2 * gi + 1].at[4 * ti + k], device_id=to, device_id_type=MESH).start()
        refs[-1][...] = jnp.zeros_like(refs[-1])

    out_shape = []
    for g in groups:
        out_shape += [pltpu.SemaphoreType.DMA((4 * len(g),)), pltpu.SemaphoreType.DMA((4 * len(g),))]
    out_shape += [pltpu.HBM(s.shape, s.dtype) for s in flat]
    out_shape += [pltpu.HBM((N_DEV,) + s.shape, s.dtype) for s in flat]
    out_shape += [jax.ShapeDtypeStruct((8, LANES), F32)]
    aliases = {t: 2 * n_g + t for t in range(n)}
    aliases.update({n + t: 2 * n_g + n + t for t in range(n)})
    res = pl.pallas_call(
        body, name=name, out_shape=out_shape, in_specs=[HBM_SPEC] * (2 * n) + [ANY_SPEC],
        out_specs=[SEM_SPEC] * (2 * n_g) + [HBM_SPEC] * (2 * n) + [pl.BlockSpec(memory_space=pltpu.VMEM)],
        input_output_aliases=aliases, compiler_params=pltpu.CompilerParams(has_side_effects=DATAFLOW),
    )(*[_in_hbm(s) for s in flat], *[_in_hbm(lax.empty((N_DEV,) + s.shape, s.dtype)) for s in flat], after)
    out, off = [], 0
    for gi, g in enumerate(groups):
        k = len(g)
        out.append((res[2 * gi], res[2 * gi + 1], res[2 * n_g + off:2 * n_g + off + k],
                    res[2 * n_g + n + off:2 * n_g + n + off + k]))
        off += k
    return out, res[-1]


def gather_wait(started, after, *, name):
    send_sems, recv_sems, srcs, lands = started
    n = len(srcs)
    after = list(after)

    def body(*refs):
        src, land = refs[:n], refs[n:2 * n]
        send, recv = refs[2 * n], refs[2 * n + 1]
        _, peers = _gather_peers()
        for t in range(n):
            for k, frm in enumerate(peers):
                cp = pltpu.make_async_remote_copy(
                    src_ref=src[t], dst_ref=land[t].at[_slot(frm)], send_sem=send.at[4 * t + k],
                    recv_sem=recv.at[4 * t + k],
                    device_id=frm, device_id_type=MESH)
                cp.wait_send()
                cp.wait_recv()

    res = pl.pallas_call(
        body, name=name,
        out_shape=[pltpu.HBM(s.shape, s.dtype) for s in srcs] + [pltpu.HBM(l.shape, l.dtype) for l in lands],
        in_specs=[HBM_SPEC] * (2 * n) + [SEM_SPEC, SEM_SPEC] + [ANY_SPEC] * len(after),
        out_specs=[HBM_SPEC] * (2 * n), input_output_aliases={t: t for t in range(2 * n)},
        compiler_params=pltpu.CompilerParams(has_side_effects=DATAFLOW),
    )(*srcs, *lands, send_sems, recv_sems, *after)
    return res[:n], res[n:]


def place_own(src, land, *, name):
    R, C = src.shape
    tr = _tile(R, 512, 16)
    x, y, c, _ = _chip_slots()
    idx = jnp.stack([4 * x + 2 * y + c]).astype(jnp.int32)

    def body(idx_ref, s_ref, land_ref, o_ref):
        o_ref[...] = s_ref[...]

    return _pcall(body, name=name, grid=(R // tr,),
                  ins=[(src, (tr, C), lambda i, s: (i, 0)), (land, None, None)],
                  outs=[(land.shape, land.dtype, (None, tr, C), lambda i, s: (s[0], i, 0))],
                  prefetch=idx, aliases={2: 0}, semantics=("parallel",))[0]


def gather_finish(srcs, lands, *, name):
    n = len(srcs)

    def body(*refs):
        land = refs[n:2 * n]
        send_sems, recv_sems = refs[2 * n:]
        x, y, c, chips = _chip_slots()
        me, sibling = (x, y, c), (x, y, 1 - c)

        def copy(t, j, block, to):
            return pltpu.make_async_remote_copy(
                src_ref=land[t].at[_slot(block)], dst_ref=land[t].at[_slot(block)], send_sem=send_sems.at[t, j],
                recv_sem=recv_sems.at[t, j], device_id=to, device_id_type=MESH)

        sends = [copy(t, j, (*chip, c), sibling) for t in range(n) for j, chip in enumerate(chips)]
        for cp in sends:
            cp.start()
        for t in range(n):
            for j, chip in enumerate(chips):
                copy(t, j, (*chip, 1 - c), me).wait_recv()
        for cp in sends:
            cp.wait_send()

    passed = pl.pallas_call(
        body, name=name, out_shape=[jax.ShapeDtypeStruct(l.shape, l.dtype) for l in lands],
        in_specs=[ANY_SPEC] * n, out_specs=[ANY_SPEC] * n,
        input_output_aliases={t: t for t in range(n)},
        scratch_shapes=[pltpu.SemaphoreType.DMA((n, 3)), pltpu.SemaphoreType.DMA((n, 3))],
    )(*lands)
    return [place_own(s, l, name=f"{name}_own{t}") for t, (s, l) in enumerate(zip(srcs, passed))]


def chips_start(pairs, *, name):
    n = len(pairs)

    def body(*refs):
        src, land = refs[:n], refs[n:2 * n]
        send, recv = refs[2 * n], refs[2 * n + 1]
        token = refs[-1]
        x, y, c, chips = _chip_slots()
        for t in range(n):
            for j, chip in enumerate(chips):
                pltpu.make_async_remote_copy(
                    src_ref=src[t].at[j], dst_ref=land[t].at[j], send_sem=send.at[3 * t + j],
                    recv_sem=recv.at[3 * t + j], device_id=(*chip, c), device_id_type=MESH).start()
        token[...] = jnp.zeros_like(token)

    res = pl.pallas_call(
        body, name=name,
        out_shape=[pltpu.SemaphoreType.DMA((3 * n,)), pltpu.SemaphoreType.DMA((3 * n,))]
        + [pltpu.HBM(p.shape, p.dtype) for p in pairs] * 2 + [jax.ShapeDtypeStruct((8, LANES), F32)],
        in_specs=[HBM_SPEC] * (2 * n),
        out_specs=[SEM_SPEC, SEM_SPEC] + [HBM_SPEC] * (2 * n) + [pl.BlockSpec(memory_space=pltpu.VMEM)],
        input_output_aliases={t: 2 + t for t in range(2 * n)},
        compiler_params=pltpu.CompilerParams(has_side_effects=DATAFLOW),
    )(*[_in_hbm(p) for p in pairs], *[_in_hbm(lax.empty(p.shape, p.dtype)) for p in pairs])
    return res[0], res[1], res[2:2 + n], res[2 + n:2 + 2 * n], res[-1]


def chips_wait(started, after, *, name):
    send_sems, recv_sems, srcs, lands, _ = started
    n = len(srcs)

    def body(*refs):
        src, land = refs[:n], refs[n:2 * n]
        send, recv = refs[2 * n], refs[2 * n + 1]
        x, y, c, chips = _chip_slots()
        for t in range(n):
            for j, chip in enumerate(chips):
                cp = pltpu.make_async_remote_copy(
                    src_ref=src[t].at[j], dst_ref=land[t].at[j], send_sem=send.at[3 * t + j],
                    recv_sem=recv.at[3 * t + j], device_id=(*chip, c), device_id_type=MESH)
                cp.wait_send()
                cp.wait_recv()

    res = pl.pallas_call(
        body, name=name, out_shape=[pltpu.HBM(s.shape, s.dtype) for s in srcs] * 2,
        in_specs=[HBM_SPEC] * (2 * n) + [SEM_SPEC, SEM_SPEC, ANY_SPEC], out_specs=[HBM_SPEC] * (2 * n),
        input_output_aliases={t: t for t in range(2 * n)},
        compiler_params=pltpu.CompilerParams(has_side_effects=DATAFLOW),
    )(*srcs, *lands, send_sems, recv_sems, after)
    return res[n:]


def _sibling_copies(src, land, send, recv, n):
    x, y, c, _ = _chip_slots()
    return [pltpu.make_async_remote_copy(
        src_ref=src[t].at[4 * (q // 2) + 2 * (q % 2) + (1 - c)], dst_ref=land[t].at[q], send_sem=send.at[4 * t + q],
        recv_sem=recv.at[4 * t + q], device_id=(x, y, 1 - c), device_id_type=MESH)
        for t in range(n) for q in range(4)]


def sibling_start(gs, *, name):
    n = len(gs)

    def body(*refs):
        for cp in _sibling_copies(refs[:n], refs[n:2 * n], refs[2 * n], refs[2 * n + 1], n):
            cp.start()
        refs[-1][...] = jnp.zeros_like(refs[-1])

    lands = [lax.empty((4,) + g.shape[1:], g.dtype) for g in gs]
    res = pl.pallas_call(
        body, name=name,
        out_shape=[pltpu.SemaphoreType.DMA((4 * n,)), pltpu.SemaphoreType.DMA((4 * n,))]
        + [pltpu.HBM(g.shape, g.dtype) for g in gs] + [pltpu.HBM(l.shape, l.dtype) for l in lands]
        + [jax.ShapeDtypeStruct((8, LANES), F32)],
        in_specs=[HBM_SPEC] * (2 * n),
        out_specs=[SEM_SPEC, SEM_SPEC] + [HBM_SPEC] * (2 * n) + [pl.BlockSpec(memory_space=pltpu.VMEM)],
        input_output_aliases={t: 2 + t for t in range(2 * n)},
        compiler_params=pltpu.CompilerParams(has_side_effects=DATAFLOW),
    )(*[_in_hbm(g) for g in gs], *[_in_hbm(l) for l in lands])
    return res[0], res[1], res[2:2 + n], res[2 + n:2 + 2 * n], res[-1]


def sibling_wait(started, after, *, name):
    send_sems, recv_sems, srcs, lands, _ = started
    n = len(srcs)

    def body(*refs):
        for cp in _sibling_copies(refs[:n], refs[n:2 * n], refs[2 * n], refs[2 * n + 1], n):
            cp.wait_send()
            cp.wait_recv()

    res = pl.pallas_call(
        body, name=name,
        out_shape=[pltpu.HBM(s.shape, s.dtype) for s in srcs] + [pltpu.HBM(l.shape, l.dtype) for l in lands],
        in_specs=[HBM_SPEC] * (2 * n) + [SEM_SPEC, SEM_SPEC, ANY_SPEC], out_specs=[HBM_SPEC] * (2 * n),
        input_output_aliases={t: t for t in range(2 * n)},
        compiler_params=pltpu.CompilerParams(has_side_effects=DATAFLOW),
    )(*srcs, *lands, send_sems, recv_sems, after)
    return res[:n], res[n:]


def _rope_slab(cols):
    z = jnp.zeros(cols.shape[:-1] + (HALF_ROPE,), cols.dtype)
    return jnp.concatenate([cols[..., :HALF_ROPE], z, cols[..., HALF_ROPE:], z], axis=-1)


def _rope_unslab(slab):
    return jnp.concatenate([slab[..., :HALF_ROPE], slab[..., 2 * HALF_ROPE:3 * HALF_ROPE]], axis=-1)


def _pack_w_in_t(wt_g):
    s, c, d = wt_g.shape
    w = wt_g.reshape(s * c, d)
    c2, c3 = Q_LORA + KV_LORA, Q_LORA + KV_LORA + QK_ROPE
    r = w[c2:c3]
    z = jnp.zeros((HALF_ROPE, d), w.dtype)
    return jnp.concatenate([w[:c2], w[c3:], r[:HALF_ROPE], z, r[HALF_ROPE:], z], axis=0)


def unpack_w_in_t_grad(dwt, *, name):
    n_rows, d = dwt.shape
    kr = QK_ROPE
    n_out_rows = n_rows - kr
    blk = n_out_rows // 7
    assert blk * 7 == n_out_rows and blk % kr == 0 and n_rows % (2 * kr) == 0
    kr_row = Q_LORA + KV_LORA
    k_mix = kr_row // blk
    off = kr_row - k_mix * blk
    slab_block = (n_rows - 2 * kr) // (2 * kr)

    def body(prev_ref, in_ref, slab_ref, o_ref):
        k = pl.program_id(0)

        @pl.when(k < k_mix)
        def _():
            o_ref[...] = in_ref[...]

        @pl.when(k == k_mix)
        def _():
            o_ref[:off, :] = in_ref[:off, :]
            o_ref[off:off + HALF_ROPE, :] = slab_ref[:HALF_ROPE, :]
            o_ref[off + HALF_ROPE:off + kr, :] = slab_ref[2 * HALF_ROPE:3 * HALF_ROPE, :]
            o_ref[off + kr:, :] = in_ref[off:blk - kr, :]

        @pl.when(k > k_mix)
        def _():
            o_ref[:kr, :] = prev_ref[blk - kr:, :]
            o_ref[kr:, :] = in_ref[:blk - kr, :]

    out = _pcall(body, name=name, grid=(7,),
                 ins=[(dwt, (blk, d), lambda k: (jnp.maximum(k - 1, 0), 0)), (dwt, (blk, d), lambda k: (k, 0)),
                      (dwt, (2 * kr, d), lambda k: (slab_block, 0))],
                 outs=[((n_out_rows, d), dwt.dtype, (blk, d), lambda k: (k, 0))], semantics=("parallel",))[0]
    return out.reshape(N_DEV, n_out_rows // N_DEV, d)


def _rope_tables(positions):
    inv_freq = ROPE_BASE ** (-jnp.arange(0, QK_ROPE, 2, dtype=F32) / QK_ROPE)
    ang = positions.astype(F32)[:, None] * inv_freq
    cos, sin = jnp.cos(ang), jnp.sin(ang)
    z = jnp.zeros_like(cos)
    return jnp.concatenate([cos, z, cos, z], axis=-1), jnp.concatenate([-sin, z, sin, z], axis=-1)


def _mlp_up(x, gain, w1, tag):
    hn = rms_fwd(x, gain, name=f"mlp{tag}_norm")

    def act_epi(acc):
        a = jnp.maximum(acc, 0.0)
        return a, a * a

    T = x.shape[0]
    F = w1.shape[0] * w1.shape[2]
    a, act = mm(hn, w1, name=f"mlp{tag}_up", outs=[((T, F), BF, None), ((T, F), BF, None)], epi=act_epi)
    return hn, a, act


def _mlp_down(x, act, w2, tag, part=0):
    n = w2.shape[1]
    bm = _tile(x.shape[0], MM_TILE)
    bn = _tile(n, MM_TILE)
    per = n // bn
    return mm(act, w2, name=f"mlp{tag}_down{part}", out=((x.shape[0], n), F32), bm=bm, bn=bn,
              epi=lambda acc, r: (acc + r[...],), epi_ins=[(x, (bm, bn), lambda i, j, k: (i, part * per + j))])


def _mlp_bwd_weights(w1, w2, saved, dxb, tag):
    hn, a, act = saved
    T, D = dxb.shape
    F = a.shape[1]
    bm = _tile(T, MM_TILE)
    bn = _tile(F, min(MM_TILE, w1.shape[2]))
    dhid = mm(dxb, w2, tb=True, name=f"mlp{tag}_dhid", out=((T, F), BF), bm=bm, bn=bn,
              epi=lambda acc, a_ref: (2.0 * a_ref[...].astype(F32) * acc,),
              epi_ins=[(a, (bm, bn), lambda i, j, k: (i, j))])
    dw2 = mm(act, dxb, ta=True, name=f"mlp{tag}_dw2", out=((F, D), BF))
    dw1 = mm(hn, dhid, ta=True, name=f"mlp{tag}_dw1", out=(w1.shape, BF))
    return dhid, dw1, dw2.reshape(N_DEV, F // N_DEV, D)


def _reduce_begin(grads, tag):
    return sibling_start(grads, name=f"reduce_sibling_start_{tag}")


def _reduce_continue(sib, after, tag, index):
    grads, a_bufs = sibling_wait(sib, after, name=f"reduce_sibling_wait_{tag}")
    pairs = [run_job(pair_job(g, a), index=index, name=f"pair_sum_{tag}{t}")[0]
             for t, (g, a) in enumerate(zip(grads, a_bufs))]
    return grads, a_bufs, chips_start(pairs, name=f"reduce_chips_start_{tag}")


def kernel(x, positions, e_norm_mix, e_w_in, e_q_norm, e_w_uq, e_kv_norm, e_w_ukv, e_v_norm, e_sgu_w, e_sgu_b, e_mla_out_norm, e_sgu_out_norm, e_w_out, o_norm_mix, o_w_in, o_conv_w, o_w_out, mlp_norm, mlp_w1, mlp_w2, final_norm, loss_target, m_e_norm_mix, m_e_w_in, m_e_q_norm, m_e_w_uq, m_e_kv_norm, m_e_w_ukv, m_e_v_norm, m_e_sgu_w, m_e_sgu_b, m_e_mla_out_norm, m_e_sgu_out_norm, m_e_w_out, m_o_norm_mix, m_o_w_in, m_o_conv_w, m_o_w_out, m_mlp_norm, m_mlp_w1, m_mlp_w2, m_final_norm, v_e_norm_mix, v_e_w_in, v_e_q_norm, v_e_w_uq, v_e_kv_norm, v_e_w_ukv, v_e_v_norm, v_e_sgu_w, v_e_sgu_b, v_e_mla_out_norm, v_e_sgu_out_norm, v_e_w_out, v_o_norm_mix, v_o_w_in, v_o_conv_w, v_o_w_out, v_mlp_norm, v_mlp_w1, v_mlp_w2, v_final_norm):
    T, D = x.shape[1], x.shape[2]
    d_shard = o_norm_mix.shape[1]
    x0 = x[0]
    target = loss_target[0]
    me = 4 * lax.axis_index("x") + 2 * lax.axis_index("y") + lax.axis_index("c")

    bf = lambda s: s.astype(BF)
    gather_groups = [[bf(jnp.transpose(e_w_in[0])), bf(e_w_uq[0]), bf(e_w_ukv[0])], [bf(e_w_out[0]), bf(mlp_w1[0])],
                     [bf(mlp_w2[0]), bf(o_w_in[0])], [bf(o_w_out[0]), bf(mlp_w1[1])], [bf(mlp_w2[1])]]
    small_rows = jnp.concatenate([o_norm_mix, o_conv_w[0], jnp.zeros((4, d_shard), F32)], axis=0)
    gather_groups[0].insert(0, small_rows)
    started, start_token = gather_start(gather_groups[:1], x0, name="gather_start0")
    started_rest, rest_token = gather_start(gather_groups[1:], start_token, name="gather_start1")
    started += started_rest

    def gathered(gi, after):
        srcs, lands = gather_wait(started[gi], after, name=f"gather_wait{gi}")
        return gather_finish(srcs, lands, name=f"gather_finish{gi}")

    w_tril = jnp.tril(e_sgu_w[0])
    w_tril_b = w_tril.astype(BF)
    w_tril_tb = jnp.swapaxes(w_tril, 1, 2).astype(BF)
    b_full = jnp.repeat(e_sgu_b[0].T, CH, axis=1)
    v_gain = e_v_norm[0].reshape(1, SGU_OUT)
    cos_t, sin_t = _rope_tables(positions[0])
    mlp_gain = [mlp_norm[0:1], mlp_norm[1:2]]
    final_gain = final_norm.reshape(1, D)

    h0 = rms_fwd(x0, e_norm_mix, name="e_norm", deps=[rest_token])
    small_g, g_w_in_t, g_w_uq, w_ukv = gathered(0, [h0, cos_t, sin_t, w_tril_b, w_tril_tb, b_full])
    o_norm_full = small_g[:, 0, :].reshape(1, D)
    conv_w_full = jnp.transpose(small_g[:, 1:4, :], (1, 0, 2)).reshape(3, D)
    w_in_t = _pack_w_in_t(g_w_in_t)
    w_uq = jnp.concatenate([g_w_uq[..., :QK_NOPE], _rope_slab(g_w_uq[..., QK_NOPE:])], axis=-1)
    proj = mm(h0, w_in_t, tb=True, name="e_in", out=((T, w_in_t.shape[0]), F32), bn=_tile(w_in_t.shape[0], 640))
    qn, kvn, krope = mla_prep(proj, e_q_norm, e_kv_norm, cos_t, sin_t, name="mla_prep")
    bm = _tile(T, MM_TILE)

    def q_epi(acc, cos_ref, sin_ref):
        return (jnp.concatenate([acc[:, :QK_NOPE], _rope_fwd(acc[:, QK_NOPE:], cos_ref[...], sin_ref[...])], axis=-1),)

    q = mm(qn, w_uq, name="mla_q", out=((T, HEADS * HEAD_PAD), BF), bm=bm, bn=HEAD_PAD, epi=q_epi,
           epi_ins=[(cos_t, (bm, LANES), lambda i, j, k: (i, 0)), (sin_t, (bm, LANES), lambda i, j, k: (i, 0))])

    def kv_epi(acc, kr_ref):
        return jnp.concatenate([acc[:, :QK_NOPE].astype(BF), kr_ref[...]], axis=-1), acc[:, QK_NOPE:]

    k, v = mm(kvn, w_ukv, name="mla_kv", bm=bm, bn=HEAD_PAD, epi=kv_epi,
              outs=[((T, HEADS * HEAD_PAD), BF, HEAD_PAD), ((T, MLA_OUT), BF, V_HEAD)],
              epi_ins=[(krope, (bm, LANES), lambda i, j, k: (i, 0))])
    attn, attn_lse = attn_fwd(q, k, v, name="attn_fwd")
    mixed = mix_fwd(attn, proj, e_mla_out_norm, e_sgu_out_norm, v_gain, w_tril_b, b_full, name="mix_fwd")
    bn = _tile(D, MM_TILE)
    g_w_out_e, w1_0 = gathered(1, [mixed])
    w_out_e = g_w_out_e.reshape(-1, D)
    x1 = mm(mixed, w_out_e, name="e_out", out=((T, D), F32), bm=bm, bn=bn,
            epi=lambda acc, r: (acc + r[...],), epi_ins=[(x0, (bm, bn), lambda i, j, k: (i, j))])
    hn0, a0, act0 = _mlp_up(x1, mlp_gain[0], w1_0, 0)
    g_w2_0, g_w_in_o = gathered(2, [act0])
    w2_0 = g_w2_0.reshape(-1, D)
    x2 = _mlp_down(x1, act0, w2_0, 0)
    ho = rms_fwd(x2, o_norm_full, name="o_norm")
    proj_o = mm(ho, g_w_in_o, name="o_in", out=((T, 3 * D), F32))
    gated = conv_fwd(proj_o, conv_w_full, name="conv_fwd")
    g_w_out_o, w1_1 = gathered(3, [gated])
    w_out_o = g_w_out_o.reshape(-1, D)
    x3 = mm(gated, w_out_o, name="o_out", out=((T, D), F32), bm=bm, bn=bn,
            epi=lambda acc, r: (acc + r[...],), epi_ins=[(x2, (bm, bn), lambda i, j, k: (i, j))])
    hn1, a1, act1 = _mlp_up(x3, mlp_gain[1], w1_1, 1)
    (g_w2_1,) = gathered(4, [act1])
    w2_1 = g_w2_1.reshape(-1, D)
    x4 = _mlp_down(x3, act1, w2_1, 1)
    w1, w2 = [w1_0, w1_1], [w2_0, w2_1]

    dx4, dx4b, d_final, loss_part = loss_bwd([x4], final_gain, target, name="loss_bwd")

    hosted = dict(job_index=device_index())
    dhid1, dw1_1, dw2_1 = _mlp_bwd_weights(w1[1], w2[1], (hn1, a1, act1), dx4b, 1)
    sib_r0 = _reduce_begin([dw1_1, dw2_1], "r0")
    dhn1 = mm(dhid1, w1[1], tb=True, name="mlp1_dhn", out=((T, D), F32), deps=[sib_r0[-1]])
    grads_r0, a_r0 = sibling_wait(sib_r0, dhn1, name="reduce_sibling_wait_r0")
    dx3, dx3b, d_mlp1 = rms_bwd(x3, mlp_gain[1], dhn1, dres=dx4, name="mlp1_norm_bwd")

    dgated, ((pair_r0a,),) = mm(dx3b, w_out_o, tb=True, name="o_out_dx", out=((T, D), F32),
                                jobs=[pair_job(grads_r0[0], a_r0[0])], **hosted)
    dw_out_o, ((pair_r0b,),) = mm(gated, dx3b, ta=True, name="o_out_dw", out=((D, D), BF),
                                  jobs=[pair_job(grads_r0[1], a_r0[1])], **hosted)
    st_r0 = chips_start([pair_r0a, pair_r0b], name="reduce_chips_start_r0")
    dproj_o, dconv_full = conv_bwd(dgated, proj_o, conv_w_full, name="conv_bwd", deps=[st_r0[-1]])
    dw_in_o = mm(ho, dproj_o, ta=True, name="o_in_dw", out=(g_w_in_o.shape, BF))
    sib_r1 = _reduce_begin([dw_out_o.reshape(g_w_out_o.shape), dw_in_o], "r1")
    dho = mm(dproj_o, g_w_in_o, tb=True, name="o_in_dx", out=((T, D), F32), deps=[sib_r1[-1]])
    grads_r1, a_r1 = sibling_wait(sib_r1, dho, name="reduce_sibling_wait_r1")
    dx2, dx2b, d_onorm_full = rms_bwd(x2, o_norm_full, dho, dres=dx3, name="o_norm_bwd")

    d_ff = a0.shape[1]
    bm_h, bn_h = _tile(T, MM_TILE), _tile(d_ff, min(MM_TILE, w1[0].shape[2]))
    dhid0, ((pair_r1a,), (pair_r1b,)) = mm(
        dx2b, w2[0], tb=True, name="mlp0_dhid", out=((T, d_ff), BF), bm=bm_h, bn=bn_h,
        epi=lambda acc, a_ref: (2.0 * a_ref[...].astype(F32) * acc,),
        epi_ins=[(a0, (bm_h, bn_h), lambda i, j, k: (i, j))],
        jobs=[pair_job(grads_r1[0], a_r1[0]), pair_job(grads_r1[1], a_r1[1])], **hosted)
    st_r1 = chips_start([pair_r1a, pair_r1b], name="reduce_chips_start_r1")
    dw2_0 = mm(act0, dx2b, ta=True, name="mlp0_dw2", out=((d_ff, D), BF), deps=[st_r1[-1]])
    b_r0 = chips_wait(st_r0, dw2_0, name="reduce_chips_wait_r0")
    dw1_0, (r_w1, r_w2) = mm(
        hn0, dhid0, ta=True, name="mlp0_dw1", out=(w1[0].shape, BF),
        jobs=[adam_job(grads_r0[0], a_r0[0], b_r0[0], mlp_w1, m_mlp_w1, v_mlp_w1, 1, None),
              adam_job(grads_r0[1], a_r0[1], b_r0[1], mlp_w2, m_mlp_w2, v_mlp_w2, 1, None)], **hosted)
    sib_r2 = _reduce_begin([dw1_0, dw2_0.reshape(N_DEV, d_ff // N_DEV, D)], "r2")
    dhn0 = mm(dhid0, w1[0], tb=True, name="mlp0_dhn", out=((T, D), F32), deps=[sib_r2[-1]])
    grads_r2, a_r2 = sibling_wait(sib_r2, dhn0, name="reduce_sibling_wait_r2")
    dx1, dx1b, d_mlp0 = rms_bwd(x1, mlp_gain[0], dhn0, dres=dx2, name="mlp0_norm_bwd")

    dmixed, ((pair_r2a,),) = mm(dx1b, w_out_e, tb=True, name="e_out_dx", out=((T, MLA_OUT + SGU_OUT), F32),
                                jobs=[pair_job(grads_r2[0], a_r2[0])], **hosted)
    dw_out_e, ((pair_r2b,),) = mm(mixed, dx1b, ta=True, name="e_out_dw", out=(w_out_e.shape, BF),
                                  jobs=[pair_job(grads_r2[1], a_r2[1])], **hosted)
    st_r2 = chips_start([pair_r2a, pair_r2b], name="reduce_chips_start_r2")
    (dattn, dproj, d_mla_out, d_sgu_out, d_vgain, d_sgu_w, d_b_full) = mix_bwd(
        dmixed, attn, proj, e_mla_out_norm, e_sgu_out_norm, v_gain, w_tril_b, w_tril_tb, b_full, name="mix_bwd",
        deps=[st_r2[-1]])
    b_r1 = chips_wait(st_r1, dattn, name="reduce_chips_wait_r1")
    dq, dk, dv = attn_bwd(q, k, v, attn, attn_lse, dattn, name="attn_bwd")
    dq_lin, dkv_lin, dproj = mla_bwd_prep(dq, dk, dv, cos_t, sin_t, dproj, name="mla_bwd_prep")
    dw_uq_pad = mm(qn, dq_lin, ta=True, name="mla_q_dw", out=(w_uq.shape, BF))
    dw_ukv = mm(kvn, dkv_lin, ta=True, name="mla_kv_dw", out=(w_ukv.shape, BF))
    dw_uq = jnp.concatenate([dw_uq_pad[..., :QK_NOPE], _rope_unslab(dw_uq_pad[..., QK_NOPE:])], axis=-1)
    sib_r2b = _reduce_begin([dw_out_e.reshape(g_w_out_e.shape), dw_uq, dw_ukv], "r2b")
    dqn = mm(dq_lin, w_uq, tb=True, name="mla_q_dx", out=((T, Q_LORA), F32), deps=[sib_r2b[-1]])
    dkvn = mm(dkv_lin, w_ukv, tb=True, name="mla_kv_dx", out=((T, KV_LORA), F32), deps=[sib_r2b[-1]])
    grads_r2b, a_r2b, st_r2b = _reduce_continue(sib_r2b, dkvn, "r2b", hosted["job_index"])
    dproj, d_qnorm = rms_bwd(proj, e_q_norm, dqn, col_block=0, want_f32=False, into=dproj, name="q_norm_bwd",
                             deps=[st_r2b[-1]])
    dproj, d_kvnorm = rms_bwd(proj, e_kv_norm, dkvn, col_block=1, want_f32=False, into=dproj, name="kv_norm_bwd")
    dw_in_t_pad, (r_w_out_o, r_w_in_o) = mm(
        dproj, h0, ta=True, name="e_in_dw", out=(w_in_t.shape, BF), bm=_tile(w_in_t.shape[0], 640),
        jobs=[adam_job(grads_r1[0], a_r1[0], b_r1[0], o_w_out, m_o_w_out, v_o_w_out, 0, None),
              adam_job(grads_r1[1], a_r1[1], b_r1[1], o_w_in, m_o_w_in, v_o_w_in, 0, None)], **hosted)
    dw_in_t = unpack_w_in_t_grad(dw_in_t_pad, name="e_in_dw_unpack")
    sib_r3 = _reduce_begin([dw_in_t], "r3")
    dh0 = mm(dproj, w_in_t, name="e_in_dx", out=((T, D), F32), deps=[sib_r3[-1]])
    grads_r3, a_r3, st_r3 = _reduce_continue(sib_r3, dh0, "r3", hosted["job_index"])
    tok_r3 = st_r3[-1]
    grad_x, d_enorm = rms_bwd(x0, e_norm_mix, dh0, dres=dx1, want_bf=False, name="e_norm_bwd", deps=[tok_r3])
    b_r2 = chips_wait(st_r2, grad_x, name="reduce_chips_wait_r2")

    d_sgu_b = jnp.transpose(d_b_full[:, ::CH])
    d_sgu_w_tril = jnp.tril(d_sgu_w)
    rep = [("e_norm_mix", e_norm_mix, m_e_norm_mix, v_e_norm_mix, d_enorm),
           ("e_q_norm", e_q_norm, m_e_q_norm, v_e_q_norm, d_qnorm),
           ("e_kv_norm", e_kv_norm, m_e_kv_norm, v_e_kv_norm, d_kvnorm),
           ("e_v_norm", e_v_norm, m_e_v_norm, v_e_v_norm, d_vgain),
           ("e_sgu_w", e_sgu_w, m_e_sgu_w, v_e_sgu_w, d_sgu_w_tril),
           ("e_sgu_b", e_sgu_b, m_e_sgu_b, v_e_sgu_b, d_sgu_b),
           ("e_mla_out_norm", e_mla_out_norm, m_e_mla_out_norm, v_e_mla_out_norm, d_mla_out),
           ("e_sgu_out_norm", e_sgu_out_norm, m_e_sgu_out_norm, v_e_sgu_out_norm, d_sgu_out),
           ("mlp_norm", mlp_norm, m_mlp_norm, v_mlp_norm, jnp.concatenate([d_mlp0, d_mlp1], axis=0)),
           ("final_norm", final_norm, m_final_norm, v_final_norm, d_final)]
    sizes = [int(np.prod(r[1].shape)) for r in rep]
    n_rep = sum(sizes)
    n_all = n_rep + 4 * D + 1
    width = -(-n_all // (8 * LANES)) * LANES
    pad = 8 * width - n_all
    flat = jnp.concatenate([r[4].reshape(-1) for r in rep]
                           + [d_onorm_full.reshape(-1), dconv_full.reshape(-1), loss_part[0, :1],
                              jnp.zeros((pad,), F32)])
    small_started, small_token = gather_start([[flat.reshape(8, width)]], b_r2[0], name="gather_small_grads_start")

    def finish(grads, a_bufs, b_bufs, t, w, m, v, layer=0, prev=None, tag="", deps=()):
        return run_job(adam_job(grads[t], a_bufs[t], b_bufs[t], w, m, v, layer, prev), index=hosted["job_index"],
                       name=f"adam_{tag}", deps=deps)

    r_w1 = finish(grads_r2, a_r2, b_r2, 0, mlp_w1, m_mlp_w1, v_mlp_w1, 0, r_w1, tag="w1_l0", deps=[tok_r3, small_token])
    r_w2 = finish(grads_r2, a_r2, b_r2, 1, mlp_w2, m_mlp_w2, v_mlp_w2, 0, r_w2, tag="w2_l0", deps=[r_w1[1]])
    b_r2b = chips_wait(st_r2b, r_w2[1], name="reduce_chips_wait_r2b")
    r_w_out_e = finish(grads_r2b, a_r2b, b_r2b, 0, e_w_out, m_e_w_out, v_e_w_out, tag="e_w_out")
    r_w_uq = finish(grads_r2b, a_r2b, b_r2b, 1, e_w_uq, m_e_w_uq, v_e_w_uq, tag="e_w_uq")
    r_w_ukv = finish(grads_r2b, a_r2b, b_r2b, 2, e_w_ukv, m_e_w_ukv, v_e_w_ukv, tag="e_w_ukv")
    b_r3 = chips_wait(st_r3, r_w_out_e[1], name="reduce_chips_wait_r3")
    g_w_in_t = reduce_sum(grads_r3[0], a_r3[0], b_r3[0], name="sum_e_w_in")
    w_in_upd_t = adam_rows(g_w_in_t, jnp.transpose(e_w_in[0]), jnp.transpose(m_e_w_in[0]), jnp.transpose(v_e_w_in[0]),
                           name="adam_e_w_in")
    r_w_in = [jnp.transpose(t)[None] for t in (g_w_in_t, *w_in_upd_t)]

    small_srcs, small_lands = gather_wait(small_started[0], [r_w2[1]], name="gather_small_grads_wait")
    small_all = gather_finish(small_srcs, small_lands, name="gather_small_grads_finish")[0]
    summed = sum_rows8(small_all.reshape(N_DEV * 8, width), 8, name="sum_small_grads").reshape(-1)

    loss = summed[n_rep + 4 * D]

    def pack_rep(i):
        return jnp.concatenate([r[i].reshape(-1) for r in rep]).reshape(n_rep // LANES, LANES)

    g_rep = summed[:n_rep].reshape(n_rep // LANES, LANES)
    d_rep, nm_rep, nv_rep = adam_flat(g_rep, pack_rep(1), pack_rep(2), pack_rep(3), name="adam_replicated")

    def unpack_rep(flat2d):
        out, off = {}, 0
        f = flat2d.reshape(-1)
        for r, n in zip(rep, sizes):
            out[r[0]] = f[off:off + n].reshape(r[1].shape)
            off += n
        return out

    small = {"grad": unpack_rep(g_rep), "delta": unpack_rep(d_rep), "new_m": unpack_rep(nm_rep),
             "new_v": unpack_rep(nv_rep)}
    g_onorm = lax.dynamic_slice(summed[n_rep:n_rep + D].reshape(1, D), (0, me * d_shard), (1, d_shard))
    g_conv = lax.dynamic_slice(summed[n_rep + D:n_rep + 4 * D].reshape(3, D), (0, me * d_shard), (3, d_shard))

    def pack_sharded(norm_part, conv_part):
        return jnp.concatenate([norm_part, conv_part, jnp.zeros((4, d_shard), F32)], axis=0)

    g_sh = pack_sharded(g_onorm, g_conv)
    d_sh, nm_sh, nv_sh = adam_flat(g_sh, pack_sharded(o_norm_mix, o_conv_w[0]), pack_sharded(m_o_norm_mix, m_o_conv_w[0]),
                                   pack_sharded(v_o_norm_mix, v_o_conv_w[0]), name="adam_sharded_small")
    for kind, arr in (("grad", g_sh), ("delta", d_sh), ("new_m", nm_sh), ("new_v", nv_sh)):
        small[kind]["o_norm_mix"] = arr[0:1]
        small[kind]["o_conv_w"] = arr[1:4][None]

    big = {"e_w_in": r_w_in, "e_w_uq": r_w_uq, "e_w_ukv": r_w_ukv, "e_w_out": r_w_out_e, "o_w_in": r_w_in_o,
           "o_w_out": r_w_out_o, "mlp_w1": r_w1, "mlp_w2": r_w2}
    order = ["e_norm_mix", "e_w_in", "e_q_norm", "e_w_uq", "e_kv_norm", "e_w_ukv", "e_v_norm", "e_sgu_w", "e_sgu_b",
             "e_mla_out_norm", "e_sgu_out_norm", "e_w_out", "o_norm_mix", "o_w_in", "o_conv_w", "o_w_out", "mlp_norm",
             "mlp_w1", "mlp_w2", "final_norm"]
    result = [loss, grad_x[None]]
    for ki, kind in enumerate(("grad", "delta", "new_m", "new_v")):
        for nm in order:
            result.append(big[nm][ki] if nm in big else small[kind][nm])
    return tuple(result)
```

```python
import numpy as np
import jax
import jax.numpy as jnp
from jax import lax
from jax.experimental import pallas as pl
from jax.experimental.pallas import tpu as pltpu

BF = jnp.bfloat16
F32 = jnp.float32
MESH = pl.DeviceIdType.MESH
N_DEV = 8

EPS = 1e-6
HEADS = 8
Q_LORA = 512
KV_LORA = 512
QK_NOPE = 128
QK_ROPE = 64
HALF_ROPE = QK_ROPE // 2
V_HEAD = 128
HEAD_PAD = 256
ROPE_BASE = 10000.0
GROUPS = 8
CH = 128
CHUNK = 128
SGU_OUT = GROUPS * CH
MLA_OUT = HEADS * V_HEAD
ATTN_SCALE = float((QK_NOPE + QK_ROPE) ** -0.5)

ADAM_LR = 0.001
ADAM_B1 = 0.9
ADAM_B2 = 0.999
ADAM_EPS = 1e-08
ADAM_WD = 0.01
ADAM_STEP = 10
ADAM_C1 = 1.0 - ADAM_B1 ** ADAM_STEP
ADAM_C2 = 1.0 - ADAM_B2 ** ADAM_STEP

V7X_VMEM_BYTES = 64 * 2 ** 20
VMEM_LIMIT_CAP = V7X_VMEM_BYTES - 6 * 2 ** 20
LANES = 128
ROW_TILE = 256
ATTN_TILE = 512
STREAM_BLOCK_ELEMS = 512 * 1024
MM_TILE = 1024
MM_K_TILE = 2048
MM_K_BLOCK_MAX = 4096


def _padded_bytes(block, dtype):
    dims = [d for d in block if d is not None]
    if len(dims) >= 1:
        dims[-1] = -(-dims[-1] // LANES) * LANES
    if len(dims) >= 2:
        dims[-2] = -(-dims[-2] // 16) * 16
    return int(np.prod(dims)) * jnp.dtype(dtype).itemsize


def _pcall(body, *, name, grid, ins, outs, scratch=(), semantics=None, aliases=None, prefetch=None, deps=()):
    any_spec = pl.BlockSpec(memory_space=pl.ANY)
    if deps:
        n_lead = len(ins) + (1 if prefetch is not None else 0)
        n_deps = len(deps)
        inner = body

        def body(*refs):
            inner(*refs[:n_lead], *refs[n_lead + n_deps:])

        ins = list(ins) + [(d, None, None) for d in deps]
    in_specs = [any_spec if b is None else pl.BlockSpec(b, m) for _, b, m in ins]
    out_specs = [any_spec if b is None else pl.BlockSpec(b, m) for _, _, b, m in outs]
    out_shape = [pltpu.HBM(s, d) for s, d, _, _ in outs]
    est = 0
    for a, b, _ in ins:
        if b is not None:
            est += 2 * _padded_bytes(b, a.dtype)
    for _, d, b, _ in outs:
        if b is not None:
            est += 2 * _padded_bytes(b, d)
    for s in scratch:
        if hasattr(s, "shape") and hasattr(s, "dtype"):
            est += _padded_bytes(s.shape, s.dtype)
    limit = int(min(VMEM_LIMIT_CAP, est + 16 * 2 ** 20))
    params = pltpu.CompilerParams(
        dimension_semantics=semantics or ("arbitrary",) * len(grid), vmem_limit_bytes=limit)
    args = [pltpu.with_memory_space_constraint(a, pltpu.HBM) for a, _, _ in ins]
    if prefetch is not None:
        grid_spec = pltpu.PrefetchScalarGridSpec(
            num_scalar_prefetch=1, grid=grid, in_specs=in_specs, out_specs=out_specs, scratch_shapes=list(scratch))
        call = pl.pallas_call(body, out_shape=out_shape, grid_spec=grid_spec, name=name, compiler_params=params,
                              input_output_aliases=aliases or {})
        return call(prefetch, *args)
    call = pl.pallas_call(body, out_shape=out_shape, grid=grid, in_specs=in_specs, out_specs=out_specs,
                          scratch_shapes=list(scratch), name=name, compiler_params=params,
                          input_output_aliases=aliases or {})
    return call(*args)


def _tile(dim, pref, quantum=LANES):
    if dim <= pref:
        return dim
    t = (pref // quantum) * quantum
    while t >= quantum:
        if dim % t == 0:
            return t
        t -= quantum
    return dim


def _vshape(arr_shape):
    if len(arr_shape) == 2:
        return tuple(arr_shape)
    s, r, c = arr_shape
    return (r, s * c)


def _vblock(arr_shape, br, bc, rc):
    if len(arr_shape) == 2:
        return (br, bc), (lambda *g: rc(*g))
    _, _, c = arr_shape
    assert c % bc == 0, (arr_shape, bc)
    per = c // bc

    def imap(*g):
        ri, ci = rc(*g)
        return (ci // per, ri, ci % per)

    return (None, br, bc), imap


def _shard_width(*shapes):
    w = None
    for s in shapes:
        if len(s) == 3:
            w = s[2] if w is None else int(np.gcd(w, s[2]))
    return w


def mm(a, b, *, name, ta=False, tb=False, out=None, outs=None, epi=None, epi_ins=(), bm=None, bn=None, bk=None,
       deps=(), jobs=(), job_index=None):
    av, bv = _vshape(a.shape), _vshape(b.shape)
    M, K = (av[1], av[0]) if ta else av
    K2, N = (bv[1], bv[0]) if tb else bv
    assert K == K2, (a.shape, b.shape, ta, tb)
    if outs is None:
        outs = [(out[0], out[1], None)]
    a_sw = _shard_width(a.shape)
    b_sw = _shard_width(b.shape)
    o_sw = _shard_width(*[o[0] for o in outs])
    m_lim = a_sw if (ta and a_sw) else None
    k_lim = [w for w in ((a_sw if not ta else None), (b_sw if tb else None)) if w]
    n_lim = [w for w in ((b_sw if not tb else None), o_sw) if w]
    if bm is None:
        bm = _tile(M, min([MM_TILE] + ([m_lim] if m_lim else [])))
    if bn is None:
        bn = _tile(N, min([MM_TILE] + n_lim))
    k_shards = 0
    if tb and len(b.shape) == 3 and bk is None and not (a_sw and not ta):
        k_shards = 1
        while 2 * k_shards <= b.shape[0] and 2 * k_shards * b_sw <= MM_K_BLOCK_MAX:
            k_shards *= 2
        bk = k_shards * b_sw
    if bk is None:
        bk = K if (K <= 4096 and not k_lim) else _tile(K, min([MM_K_TILE] + k_lim))
    assert M % bm == 0 and N % bn == 0 and K % bk == 0, (name, M, N, K, bm, bn, bk)
    nk = K // bk
    grid = (M // bm, N // bn, nk)
    if ta:
        a_blk, a_map = _vblock(a.shape, bk, bm, lambda i, j, k: (k, i))
    else:
        a_blk, a_map = _vblock(a.shape, bm, bk, lambda i, j, k: (i, k))
    if k_shards:
        b_blk, b_map = (k_shards, bn, b_sw), (lambda i, j, k: (k, j, 0))
    elif tb:
        b_blk, b_map = _vblock(b.shape, bn, bk, lambda i, j, k: (j, k))
    else:
        b_blk, b_map = _vblock(b.shape, bk, bn, lambda i, j, k: (k, j))
    dn = (((0 if ta else 1,), (1 if tb else 0,)), ((), ()))
    ins = [(a, a_blk, a_map), (b, b_blk, b_map)] + list(epi_ins)
    out_list = []
    for shape, dtype, cols in outs:
        cols = cols or bn
        blk, imap = _vblock(shape, bm, cols, lambda i, j, k: (i, j))
        out_list.append((shape, dtype, blk, imap))
    n_e, n_o = len(epi_ins), len(out_list)

    n_steps = grid[0] * grid[1] * nk
    built = [job(n_steps) for job in jobs]
    aliases = {}
    job_slices = []
    if built:
        def lin(i, j, k):
            return (i * grid[1] + j) * nk + k

        ins = [(arr, blk, None if blk is None else (lambda i, j, k, s, f=f: f(i, j, k))) for arr, blk, f in ins]
        out_list = [(sh, dt, blk, (lambda i, j, k, s, f=f: f(i, j, k))) for sh, dt, blk, f in out_list]
        n_main_in, n_main_out = len(ins), len(out_list)
        for jb in built:
            i0, o0 = len(ins), len(out_list)
            ins += [(arr, blk, None if blk is None else (lambda i, j, k, s, f=f: f(lin(i, j, k), s)))
                    for arr, blk, f in jb["ins"]]
            out_list += [(sh, dt, blk, (lambda i, j, k, s, f=f: f(lin(i, j, k), s))) for sh, dt, blk, f in jb["outs"]]
            aliases.update({1 + i0 + ai: o0 + ao for ai, ao in jb["aliases"].items()})
            job_slices.append((i0, len(jb["ins"]), o0, len(jb["outs"])))
    n_in_total = len(ins)

    def body(*refs):
        if built:
            refs = refs[1:]
        a_ref, b_ref = refs[0], refs[1]
        e_refs = refs[2:2 + n_e]
        o_refs = refs[n_in_total:n_in_total + n_o]
        for jb, (i0, ni, o0, no) in zip(built, job_slices):
            jb["fn"](refs[i0:i0 + ni], refs[n_in_total + o0:n_in_total + o0 + no])

        def finish(acc):
            res = epi(acc, *e_refs) if epi is not None else (acc,)
            for o_ref, r in zip(o_refs, res):
                o_ref[...] = r.astype(o_ref.dtype)

        x = a_ref[...].astype(BF)
        y = b_ref[...].astype(BF)
        if k_shards:
            p = None
            for s in range(k_shards):
                part = lax.dot_general(x[:, s * b_sw:(s + 1) * b_sw], y[s], dn, preferred_element_type=F32)
                p = part if p is None else p + part
        else:
            p = lax.dot_general(x, y, dn, preferred_element_type=F32)
        if nk == 1:
            finish(p)
        else:
            acc_ref = refs[-1]
            k = pl.program_id(2)

            @pl.when(k == 0)
            def _():
                acc_ref[...] = p

            @pl.when(k > 0)
            def _():
                acc_ref[...] += p

            @pl.when(k == nk - 1)
            def _():
                finish(acc_ref[...])

    scratch = [pltpu.VMEM((bm, bn), F32)] if nk > 1 else []
    res = _pcall(body, name=name, grid=grid, ins=ins, outs=out_list, scratch=scratch, deps=deps,
                 semantics=("parallel", "parallel", "arbitrary"), prefetch=job_index if built else None, aliases=aliases)
    main = res[0] if n_o == 1 else res[:n_o]
    if not built:
        return main
    return main, [res[o0:o0 + no] for _, _, o0, no in job_slices]


_GELU_K = float(np.sqrt(2.0 / np.pi))
_GELU_C = 0.044715


def _gelu(x):
    t = jnp.tanh(_GELU_K * (x + _GELU_C * (x * x * x)))
    return 0.5 * x * (1.0 + t)


def _gelu_grad(x):
    t = jnp.tanh(_GELU_K * (x + _GELU_C * (x * x * x)))
    return 0.5 * (1.0 + t) + 0.5 * x * (1.0 - t * t) * (_GELU_K * (1.0 + 3.0 * _GELU_C * (x * x)))


def _rstd(x):
    return lax.rsqrt(jnp.mean(x * x, axis=-1, keepdims=True) + EPS)


def _rms_bwd(x, gain, dy):
    r = _rstd(x)
    xh = x * r
    gdy = dy * gain
    dx = r * (gdy - xh * jnp.mean(gdy * xh, axis=-1, keepdims=True))
    return dx, dy * xh


def _rope_fwd(x, cos_t, sin_t):
    return x * cos_t + pltpu.roll(x, 2 * HALF_ROPE, 1) * sin_t


def _rope_bwd(dy, cos_t, sin_t):
    return dy * cos_t + pltpu.roll(dy * sin_t, 2 * HALF_ROPE, 1)


def _acc_rows(ref, val, first):
    s = jnp.sum(val, axis=0, keepdims=True)

    @pl.when(first)
    def _():
        ref[...] = s

    @pl.when(jnp.logical_not(first))
    def _():
        ref[...] += s


def rms_fwd(x, gain, *, name, col_block=0, width=None, deps=()):
    T = x.shape[0]
    width = width or x.shape[1]
    tm = _tile(T, ROW_TILE, 8)

    def body(x_ref, g_ref, o_ref):
        v = x_ref[...]
        o_ref[...] = (v * _rstd(v) * g_ref[...]).astype(BF)

    return _pcall(body, name=name, grid=(T // tm,),
                  ins=[(x, (tm, width), lambda i: (i, col_block)), (gain, (1, width), lambda i: (0, 0))],
                  outs=[((T, width), BF, (tm, width), lambda i: (i, 0))], semantics=("parallel",), deps=deps)[0]


def rms_bwd(x, gain, dy, *, name, col_block=0, dres=None, want_f32=True, want_bf=True, into=None, deps=()):
    T, width = dy.shape
    tm = _tile(T, ROW_TILE, 8)
    has_res = dres is not None

    def body(*refs):
        x_ref, g_ref, dy_ref = refs[:3]
        pos = 3
        res_ref = None
        if has_res:
            res_ref = refs[pos]
            pos += 1
        if into is not None:
            pos += 1
        outs = refs[pos:]
        dx, dg_rows = _rms_bwd(x_ref[...], g_ref[...], dy_ref[...])
        if has_res:
            dx = dx + res_ref[...]
        o = 0
        if want_f32:
            outs[o][...] = dx
            o += 1
        if want_bf:
            outs[o][...] = dx.astype(BF)
            o += 1
        _acc_rows(outs[o], dg_rows, pl.program_id(0) == 0)

    ins = [(x, (tm, width), lambda i: (i, col_block)), (gain, (1, width), lambda i: (0, 0)),
           (dy, (tm, width), lambda i: (i, 0))]
    if has_res:
        ins.append((dres, (tm, width), lambda i: (i, 0)))
    outs = []
    aliases = {}
    if want_f32:
        outs.append(((T, width), F32, (tm, width), lambda i: (i, 0)))
    if want_bf and into is not None:
        ins.append((into, None, None))
        aliases[len(ins) - 1] = len(outs)
        outs.append((into.shape, BF, (tm, width), lambda i: (i, col_block)))
    elif want_bf:
        outs.append(((T, width), BF, (tm, width), lambda i: (i, 0)))
    outs.append(((1, width), F32, (1, width), lambda i: (0, 0)))
    return _pcall(body, name=name, grid=(T // tm,), ins=ins, outs=outs, aliases=aliases, deps=deps)


def mla_prep(proj, q_norm, kv_norm, cos_t, sin_t, *, name):
    T = proj.shape[0]
    tm = _tile(T, ROW_TILE, 8)
    kr_block = (proj.shape[1] - LANES) // LANES

    def body(cq_ref, ckv_ref, kr_ref, qg_ref, kg_ref, cos_ref, sin_ref, qn_ref, kvn_ref, krope_ref):
        cq = cq_ref[...]
        qn_ref[...] = (cq * _rstd(cq) * qg_ref[...]).astype(BF)
        ckv = ckv_ref[...]
        kvn_ref[...] = (ckv * _rstd(ckv) * kg_ref[...]).astype(BF)
        krope_ref[...] = _rope_fwd(kr_ref[...], cos_ref[...], sin_ref[...]).astype(BF)

    return _pcall(
        body, name=name, grid=(T // tm,),
        ins=[(proj, (tm, Q_LORA), lambda i: (i, 0)), (proj, (tm, KV_LORA), lambda i: (i, 1)),
             (proj, (tm, LANES), lambda i: (i, kr_block)),
             (q_norm, (1, Q_LORA), lambda i: (0, 0)), (kv_norm, (1, KV_LORA), lambda i: (0, 0)),
             (cos_t, (tm, LANES), lambda i: (i, 0)), (sin_t, (tm, LANES), lambda i: (i, 0))],
        outs=[((T, Q_LORA), BF, (tm, Q_LORA), lambda i: (i, 0)), ((T, KV_LORA), BF, (tm, KV_LORA), lambda i: (i, 0)),
              ((T, LANES), BF, (tm, LANES), lambda i: (i, 0))],
        semantics=("parallel",))


def _attn_scores(q, k_blk, diagonal):
    s = lax.dot_general(q, k_blk, (((1,), (1,)), ((), ())), preferred_element_type=F32) * ATTN_SCALE
    if diagonal:
        row = lax.broadcasted_iota(jnp.int32, s.shape, 0)
        col = lax.broadcasted_iota(jnp.int32, s.shape, 1)
        s = jnp.where(col <= row, s, -jnp.inf)
    return s


def attn_fwd(q, k, v, *, name):
    T = q.shape[0]
    tq = _tile(T, ATTN_TILE, 8)

    def body(q_ref, k_ref, v_ref, o_ref, lse_ref):
        i = pl.program_id(1)
        qv = q_ref[...]

        def block(kb, carry, diagonal):
            m, l, acc = carry
            start = pl.multiple_of(kb * tq, tq)
            s = _attn_scores(qv, k_ref[pl.ds(start, tq), :], diagonal)
            m_new = jnp.maximum(m, jnp.max(s, axis=-1, keepdims=True))
            alpha = jnp.exp(m - m_new)
            p = jnp.exp(s - m_new)
            l = alpha * l + jnp.sum(p, axis=-1, keepdims=True)
            acc = alpha * acc + jnp.dot(p.astype(BF), v_ref[pl.ds(start, tq), :], preferred_element_type=F32)
            return m_new, l, acc

        init = (jnp.full((tq, 1), -jnp.inf, F32), jnp.zeros((tq, 1), F32), jnp.zeros((tq, V_HEAD), F32))
        carry = lax.fori_loop(0, i, lambda kb, c: block(kb, c, False), init)
        m, l, acc = block(i, carry, True)
        o_ref[...] = acc / l
        lse_ref[...] = jnp.broadcast_to(m + jnp.log(l), (tq, V_HEAD))

    return _pcall(
        body, name=name, grid=(HEADS, T // tq),
        ins=[(q, (tq, HEAD_PAD), lambda h, i: (i, h)), (k, (T, HEAD_PAD), lambda h, i: (0, h)),
             (v, (T, V_HEAD), lambda h, i: (0, h))],
        outs=[((T, MLA_OUT), F32, (tq, V_HEAD), lambda h, i: (i, h)),
              ((T, MLA_OUT), F32, (tq, V_HEAD), lambda h, i: (i, h))], semantics=("parallel", "parallel"))


def attn_bwd(q, k, v, o, lse, do, *, name):
    T = q.shape[0]
    tq = _tile(T, ATTN_TILE, 8)

    def body(q_ref, k_ref, v_ref, o_ref, lse_ref, do_ref, dq_ref, dk_ref, dv_ref):
        i = pl.program_id(1)

        @pl.when(i == 0)
        def _():
            dk_ref[...] = jnp.zeros_like(dk_ref)
            dv_ref[...] = jnp.zeros_like(dv_ref)

        qv = q_ref[...]
        do_t = do_ref[...]
        lse_v = lse_ref[:, 0:1]
        delta = jnp.sum(do_t.astype(F32) * o_ref[...], axis=-1, keepdims=True)

        def block(kb, dq, diagonal):
            start = pl.multiple_of(kb * tq, tq)
            k_blk = k_ref[pl.ds(start, tq), :]
            v_blk = v_ref[pl.ds(start, tq), :]
            p = jnp.exp(_attn_scores(qv, k_blk, diagonal) - lse_v)
            dp = lax.dot_general(do_t, v_blk, (((1,), (1,)), ((), ())), preferred_element_type=F32)
            ds = (p * (dp - delta) * ATTN_SCALE).astype(BF)
            dk_ref[pl.ds(start, tq), :] += lax.dot_general(ds, qv, (((0,), (0,)), ((), ())), preferred_element_type=F32)
            dv_ref[pl.ds(start, tq), :] += lax.dot_general(p.astype(BF), do_t, (((0,), (0,)), ((), ())),
                                                          preferred_element_type=F32)
            return dq + jnp.dot(ds, k_blk, preferred_element_type=F32)

        dq = lax.fori_loop(0, i, lambda kb, c: block(kb, c, False), jnp.zeros((tq, HEAD_PAD), F32))
        dq_ref[...] = block(i, dq, True)

    return _pcall(
        body, name=name, grid=(HEADS, T // tq),
        ins=[(q, (tq, HEAD_PAD), lambda h, i: (i, h)), (k, (T, HEAD_PAD), lambda h, i: (0, h)),
             (v, (T, V_HEAD), lambda h, i: (0, h)), (o, (tq, V_HEAD), lambda h, i: (i, h)),
             (lse, (tq, V_HEAD), lambda h, i: (i, h)), (do, (tq, V_HEAD), lambda h, i: (i, h))],
        outs=[((T, HEADS * HEAD_PAD), F32, (tq, HEAD_PAD), lambda h, i: (i, h)),
              ((T, HEADS * HEAD_PAD), F32, (T, HEAD_PAD), lambda h, i: (0, h)),
              ((T, MLA_OUT), F32, (T, V_HEAD), lambda h, i: (0, h))],
        semantics=("parallel", "arbitrary"))


def mla_bwd_prep(dq, dk, dv, cos_t, sin_t, dproj, *, name):
    T = dq.shape[0]
    tm = _tile(T, ROW_TILE, 8)
    kr_block = (dproj.shape[1] - LANES) // LANES

    def body(dq_ref, dk_ref, dv_ref, cos_ref, sin_ref, dproj_in, dql_ref, dkvl_ref, dkr_ref):
        cos_v, sin_v = cos_ref[...], sin_ref[...]
        kr = jnp.zeros((tm, LANES), F32)
        for h in range(HEADS):
            lo = h * HEAD_PAD
            dql_ref[:, lo:lo + QK_NOPE] = dq_ref[:, lo:lo + QK_NOPE].astype(BF)
            dql_ref[:, lo + QK_NOPE:lo + HEAD_PAD] = _rope_bwd(
                dq_ref[:, lo + QK_NOPE:lo + HEAD_PAD], cos_v, sin_v).astype(BF)
            dkvl_ref[:, lo:lo + QK_NOPE] = dk_ref[:, lo:lo + QK_NOPE].astype(BF)
            dkvl_ref[:, lo + QK_NOPE:lo + HEAD_PAD] = dv_ref[:, h * V_HEAD:(h + 1) * V_HEAD].astype(BF)
            kr = kr + dk_ref[:, lo + QK_NOPE:lo + HEAD_PAD]
        dkr_ref[...] = _rope_bwd(kr, cos_v, sin_v).astype(BF)

    W = HEADS * HEAD_PAD
    return _pcall(
        body, name=name, grid=(T // tm,),
        ins=[(dq, (tm, W), lambda i: (i, 0)), (dk, (tm, W), lambda i: (i, 0)), (dv, (tm, MLA_OUT), lambda i: (i, 0)),
             (cos_t, (tm, LANES), lambda i: (i, 0)), (sin_t, (tm, LANES), lambda i: (i, 0)), (dproj, None, None)],
        outs=[((T, W), BF, (tm, W), lambda i: (i, 0)), ((T, W), BF, (tm, W), lambda i: (i, 0)),
              (dproj.shape, BF, (tm, LANES), lambda i: (i, kr_block))],
        aliases={5: 2}, semantics=("parallel",))


def _group_norm_stats(vg):
    mu = jnp.mean(vg, axis=-1, keepdims=True)
    d = vg - mu
    r = lax.rsqrt(jnp.mean(d * d, axis=-1, keepdims=True) + EPS)
    return d * r, r


def mix_fwd(a, proj, g_mla, g_sgu, v_gain, w_tril, b_full, *, name):
    T = a.shape[0]
    tm = _tile(T, ROW_TILE, CHUNK)
    n_chunk = tm // CHUNK

    def body(a_ref, u_ref, v_ref, gm_ref, gs_ref, vg_ref, w_ref, b_ref, o_ref, s_scr):
        av = a_ref[...]
        o_ref[:, :MLA_OUT] = (av * _rstd(av) * gm_ref[...]).astype(BF)
        for g in range(GROUPS):
            sl = slice(g * CH, (g + 1) * CH)
            vhat, _ = _group_norm_stats(_gelu(v_ref[:, sl]))
            vn = (vhat * vg_ref[:, sl]).astype(BF)
            u = _gelu(u_ref[:, sl])
            for ci in range(n_chunk):
                rs = slice(ci * CHUNK, (ci + 1) * CHUNK)
                y = jnp.dot(w_ref[g], vn[rs], preferred_element_type=F32) + b_ref[:, sl]
                s_scr[rs, sl] = u[rs] * y
        s = s_scr[...]
        o_ref[:, MLA_OUT:] = (s * _rstd(s) * gs_ref[...]).astype(BF)

    return _pcall(
        body, name=name, grid=(T // tm,),
        ins=[(a, (tm, MLA_OUT), lambda i: (i, 0)), (proj, (tm, SGU_OUT), lambda i: (i, 1)),
             (proj, (tm, SGU_OUT), lambda i: (i, 2)), (g_mla, (1, MLA_OUT), lambda i: (0, 0)),
             (g_sgu, (1, SGU_OUT), lambda i: (0, 0)), (v_gain, (1, SGU_OUT), lambda i: (0, 0)),
             (w_tril, (GROUPS, CHUNK, CHUNK), lambda i: (0, 0, 0)), (b_full, (CHUNK, SGU_OUT), lambda i: (0, 0))],
        outs=[((T, MLA_OUT + SGU_OUT), BF, (tm, MLA_OUT + SGU_OUT), lambda i: (i, 0))],
        scratch=[pltpu.VMEM((tm, SGU_OUT), F32)], semantics=("parallel",))[0]


def mix_bwd(dmixed, a, proj, g_mla, g_sgu, v_gain, w_tril, w_tril_t, b_full, *, name, deps=()):
    T = a.shape[0]
    tm = _tile(T, ROW_TILE, CHUNK)
    n_chunk = tm // CHUNK
    uv0 = Q_LORA + KV_LORA

    def body(dm_a_ref, dm_s_ref, a_ref, u_ref, v_ref, gm_ref, gs_ref, vg_ref, w_ref, wt_ref, b_ref,
             da_ref, duv_ref, dgm_ref, dgs_ref, dvg_ref, dw_ref, db_ref, s_scr, y_scr):
        first = pl.program_id(0) == 0
        duv_ref[:, :uv0] = jnp.zeros((tm, uv0), BF)
        duv_ref[:, uv0 + 2 * SGU_OUT:] = jnp.zeros((tm, duv_ref.shape[1] - uv0 - 2 * SGU_OUT), BF)
        da, dgm_rows = _rms_bwd(a_ref[...], gm_ref[...], dm_a_ref[...])
        da_ref[...] = da.astype(BF)
        _acc_rows(dgm_ref, dgm_rows, first)

        for g in range(GROUPS):
            sl = slice(g * CH, (g + 1) * CH)
            vhat, _ = _group_norm_stats(_gelu(v_ref[:, sl]))
            vn = (vhat * vg_ref[:, sl]).astype(BF)
            u = _gelu(u_ref[:, sl])
            for ci in range(n_chunk):
                rs = slice(ci * CHUNK, (ci + 1) * CHUNK)
                y = jnp.dot(w_ref[g], vn[rs], preferred_element_type=F32) + b_ref[:, sl]
                y_scr[rs, sl] = y
                s_scr[rs, sl] = u[rs] * y
        ds, dgs_rows = _rms_bwd(s_scr[...], gs_ref[...], dm_s_ref[...])
        _acc_rows(dgs_ref, dgs_rows, first)
        s_scr[...] = ds

        @pl.when(first)
        def _():
            dw_ref[...] = jnp.zeros_like(dw_ref)
            db_ref[...] = jnp.zeros_like(db_ref)

        for g in range(GROUPS):
            sl = slice(g * CH, (g + 1) * CH)
            upre = u_ref[:, sl]
            vpre = v_ref[:, sl]
            u = _gelu(upre)
            vhat, r = _group_norm_stats(_gelu(vpre))
            gain = vg_ref[:, sl]
            vn = (vhat * gain).astype(BF)
            dsg = s_scr[:, sl]
            duv_ref[:, uv0 + g * CH:uv0 + (g + 1) * CH] = (dsg * y_scr[:, sl] * _gelu_grad(upre)).astype(BF)
            dy = dsg * u
            dyb = dy.astype(BF)
            dvn_parts = []
            for ci in range(n_chunk):
                rs = slice(ci * CHUNK, (ci + 1) * CHUNK)
                dvn_parts.append(jnp.dot(wt_ref[g], dyb[rs], preferred_element_type=F32))
                dw_ref[g] += lax.dot_general(dyb[rs], vn[rs], (((1,), (1,)), ((), ())), preferred_element_type=F32)
                db_ref[:, sl] += jnp.broadcast_to(jnp.sum(dy[rs], axis=-1, keepdims=True), (CHUNK, CH))
            dvn = dvn_parts[0] if n_chunk == 1 else jnp.concatenate(dvn_parts, axis=0)
            _acc_rows(dvg_ref.at[:, sl], dvn * vhat, first)
            dvh = dvn * gain
            dvg = r * (dvh - jnp.mean(dvh, axis=-1, keepdims=True)
                       - vhat * jnp.mean(dvh * vhat, axis=-1, keepdims=True))
            duv_ref[:, uv0 + SGU_OUT + g * CH:uv0 + SGU_OUT + (g + 1) * CH] = (dvg * _gelu_grad(vpre)).astype(BF)

    return _pcall(
        body, name=name, grid=(T // tm,),
        ins=[(dmixed, (tm, MLA_OUT), lambda i: (i, 0)), (dmixed, (tm, SGU_OUT), lambda i: (i, 1)),
             (a, (tm, MLA_OUT), lambda i: (i, 0)), (proj, (tm, SGU_OUT), lambda i: (i, 1)),
             (proj, (tm, SGU_OUT), lambda i: (i, 2)), (g_mla, (1, MLA_OUT), lambda i: (0, 0)),
             (g_sgu, (1, SGU_OUT), lambda i: (0, 0)), (v_gain, (1, SGU_OUT), lambda i: (0, 0)),
             (w_tril, (GROUPS, CHUNK, CHUNK), lambda i: (0, 0, 0)), (w_tril_t, (GROUPS, CHUNK, CHUNK), lambda i: (0, 0, 0)),
             (b_full, (CHUNK, SGU_OUT), lambda i: (0, 0))],
        outs=[((T, MLA_OUT), BF, (tm, MLA_OUT), lambda i: (i, 0)),
              ((T, proj.shape[1]), BF, (tm, proj.shape[1]), lambda i: (i, 0)),
              ((1, MLA_OUT), F32, (1, MLA_OUT), lambda i: (0, 0)), ((1, SGU_OUT), F32, (1, SGU_OUT), lambda i: (0, 0)),
              ((1, SGU_OUT), F32, (1, SGU_OUT), lambda i: (0, 0)),
              ((GROUPS, CHUNK, CHUNK), F32, (GROUPS, CHUNK, CHUNK), lambda i: (0, 0, 0)),
              ((CHUNK, SGU_OUT), F32, (CHUNK, SGU_OUT), lambda i: (0, 0))],
        scratch=[pltpu.VMEM((tm, SGU_OUT), F32), pltpu.VMEM((tm, SGU_OUT), F32)], deps=deps)


def _shift_down(z, n, row):
    return jnp.where(row >= n, pltpu.roll(z, n, 0), 0.0)


def _shift_up(z, n, row, T):
    return jnp.where(row < T - n, pltpu.roll(z, T - n, 0), 0.0)


def conv_fwd(proj, conv_w, *, name):
    T, D3 = proj.shape
    D = D3 // 3
    tn = _tile(D, 256)
    nj = D // tn

    def body(b_ref, c_ref, x_ref, w_ref, o_ref):
        row = lax.broadcasted_iota(jnp.int32, (T, tn), 0)
        z = c_ref[...] * x_ref[...]
        zc = w_ref[2:3, :] * z + w_ref[1:2, :] * _shift_down(z, 1, row) + w_ref[0:1, :] * _shift_down(z, 2, row)
        o_ref[...] = (b_ref[...] * zc).astype(BF)

    return _pcall(
        body, name=name, grid=(nj,),
        ins=[(proj, (T, tn), lambda j: (0, j)), (proj, (T, tn), lambda j: (0, nj + j)),
             (proj, (T, tn), lambda j: (0, 2 * nj + j)), (conv_w, (3, tn), lambda j: (0, j))],
        outs=[((T, D), BF, (T, tn), lambda j: (0, j))], semantics=("parallel",))[0]


def conv_bwd(dg, proj, conv_w, *, name, deps=()):
    T, D3 = proj.shape
    D = D3 // 3
    tn = _tile(D, 256)
    nj = D // tn

    def body(dg_ref, b_ref, c_ref, x_ref, w_ref, dp_ref, dw_ref, dc_scr, dx_scr):
        part = pl.program_id(1)

        @pl.when(part == 0)
        def _():
            row = lax.broadcasted_iota(jnp.int32, (T, tn), 0)
            c, x = c_ref[...], x_ref[...]
            z = c * x
            z1 = _shift_down(z, 1, row)
            z2 = _shift_down(z, 2, row)
            dgv = dg_ref[...]
            zc = w_ref[2:3, :] * z + w_ref[1:2, :] * z1 + w_ref[0:1, :] * z2
            dp_ref[...] = (dgv * zc).astype(BF)
            dzc = dgv * b_ref[...]
            dw_ref[0:1, :] = jnp.sum(dzc * z2, axis=0, keepdims=True)
            dw_ref[1:2, :] = jnp.sum(dzc * z1, axis=0, keepdims=True)
            dw_ref[2:3, :] = jnp.sum(dzc * z, axis=0, keepdims=True)
            dz = (w_ref[2:3, :] * dzc + w_ref[1:2, :] * _shift_up(dzc, 1, row, T)
                  + w_ref[0:1, :] * _shift_up(dzc, 2, row, T))
            dc_scr[...] = (dz * x).astype(BF)
            dx_scr[...] = (dz * c).astype(BF)

        @pl.when(part == 1)
        def _():
            dp_ref[...] = dc_scr[...]

        @pl.when(part == 2)
        def _():
            dp_ref[...] = dx_scr[...]

    return _pcall(
        body, name=name, grid=(nj, 3),
        ins=[(dg, (T, tn), lambda j, p: (0, j)), (proj, (T, tn), lambda j, p: (0, j)),
             (proj, (T, tn), lambda j, p: (0, nj + j)), (proj, (T, tn), lambda j, p: (0, 2 * nj + j)),
             (conv_w, (3, tn), lambda j, p: (0, j))],
        outs=[((T, D3), BF, (T, tn), lambda j, p: (0, p * nj + j)), ((3, D), F32, (3, tn), lambda j, p: (0, j))],
        scratch=[pltpu.VMEM((T, tn), BF), pltpu.VMEM((T, tn), BF)], semantics=("parallel", "arbitrary"), deps=deps)


def loss_bwd(x_parts, gain, target, *, name):
    T, D = target.shape
    tm = _tile(T, ROW_TILE, 8)
    n_x = len(x_parts)

    def body(*refs):
        x_refs = refs[:n_x]
        g_ref, t_ref, dx_ref, dxb_ref, dg_ref, loss_ref = refs[n_x:]
        first = pl.program_id(0) == 0
        xv = jnp.concatenate([r[...] for r in x_refs], axis=-1) if n_x > 1 else x_refs[0][...]
        r = _rstd(xv)
        xh = xv * r
        gain_v = g_ref[...]
        err = xh * gain_v - t_ref[...]
        part = 0.5 * jnp.sum(jnp.mean(err * err, axis=-1, keepdims=True), axis=0, keepdims=True)
        _acc_rows(loss_ref, jnp.broadcast_to(part, (1, LANES)), first)
        dy = err * (1.0 / D)
        gdy = dy * gain_v
        dx = r * (gdy - xh * jnp.mean(gdy * xh, axis=-1, keepdims=True))
        dx_ref[...] = dx
        dxb_ref[...] = dx.astype(BF)
        _acc_rows(dg_ref, dy * xh, first)

    return _pcall(
        body, name=name, grid=(T // tm,),
        ins=[(p, (tm, D // n_x), lambda i: (i, 0)) for p in x_parts]
        + [(gain, (1, D), lambda i: (0, 0)), (target, (tm, D), lambda i: (i, 0))],
        outs=[((T, D), F32, (tm, D), lambda i: (i, 0)), ((T, D), BF, (tm, D), lambda i: (i, 0)),
              ((1, D), F32, (1, D), lambda i: (0, 0)), ((1, LANES), F32, (1, LANES), lambda i: (0, 0))])


def _adamw(g, w, m, v):
    m = ADAM_B1 * m + (1.0 - ADAM_B1) * g
    v = ADAM_B2 * v + (1.0 - ADAM_B2) * (g * g)
    m_hat = m / ADAM_C1
    v_hat = v / ADAM_C2
    delta = -ADAM_LR * (m_hat / (jnp.sqrt(v_hat) + ADAM_EPS) + ADAM_WD * w)
    return delta, m, v


def adam_flat(g, w, m, v, *, name):
    def body(g_ref, w_ref, m_ref, v_ref, d_ref, nm_ref, nv_ref):
        d, nm, nv = _adamw(g_ref[...], w_ref[...], m_ref[...], v_ref[...])
        d_ref[...] = d
        nm_ref[...] = nm
        nv_ref[...] = nv

    blk = g.shape
    zero = lambda: (0, 0)
    return _pcall(body, name=name, grid=(),
                  ins=[(t, blk, zero) for t in (g, w, m, v)],
                  outs=[(blk, F32, blk, zero)] * 3)


def _chip_slots():
    x, y, c = lax.axis_index("x"), lax.axis_index("y"), lax.axis_index("c")
    chips = [(1 - x, y), (x, 1 - y), (1 - x, 1 - y)]
    return x, y, c, chips


def device_index():
    x, y, c, chips = _chip_slots()
    return jnp.stack([4 * x + 2 * y + c, 2 * x + y] + [4 * cx + 2 * cy + c for cx, cy in chips]
                     + [2 * cx + cy for cx, cy in chips]).astype(jnp.int32)


def _job_rows(R, C, n_steps):
    if n_steps is None:
        n_steps = max(1, R * C // STREAM_BLOCK_ELEMS)
    n_blk = max([d for d in range(1, n_steps + 1) if R % d == 0 and (R // d) % 16 == 0] or [1])
    return R // n_blk, n_blk


def run_job(job, *, index, name, deps=()):
    jb = job(None)
    n_in = len(jb["ins"])

    def body(idx_ref, *refs):
        jb["fn"](refs[:n_in], refs[n_in:n_in + len(jb["outs"])])

    return _pcall(body, name=name, grid=(jb["n_blk"],), ins=jb["ins"], outs=jb["outs"], prefetch=index,
                  aliases={1 + a: o for a, o in jb["aliases"].items()}, semantics=("parallel",), deps=deps)


def adam_job(gs, a_buf, b_buf, w, m, v, layer, prev):
    L, R, C = w.shape

    def build(n_steps):
        tr, n_blk = _job_rows(R, C, n_steps)
        blk = (None, tr, C)
        row = lambda t: jnp.minimum(t, n_blk - 1)
        ins = [(gs, blk, lambda t, s: (s[0], row(t), 0)), (a_buf, blk, lambda t, s: (s[1], row(t), 0))]
        ins += [(b_buf, blk, lambda t, s, j=j: (j, row(t), 0)) for j in range(3)]
        ins += [(p, blk, lambda t, s: (layer, row(t), 0)) for p in (w, m, v)]
        ins += [(p, None, None) for p in (prev or [])]

        def fn(i, o):
            g = ((((i[0][...].astype(F32) + i[1][...].astype(F32)) + i[2][...].astype(F32))
                  + i[3][...].astype(F32)) + i[4][...].astype(F32))
            d, nm, nv = _adamw(g, i[5][...], i[6][...], i[7][...])
            o[0][...] = g
            o[1][...] = d
            o[2][...] = nm
            o[3][...] = nv

        return dict(ins=ins, outs=[((L, R, C), F32, blk, lambda t, s: (layer, row(t), 0))] * 4, fn=fn,
                    aliases={8 + o: o for o in range(4)} if prev else {}, n_blk=n_blk)

    return build


def pair_job(gs, a_buf):
    _, R, C = gs.shape

    def build(n_steps):
        tr, n_blk = _job_rows(R, C, n_steps)
        blk = (None, tr, C)
        row = lambda t: jnp.minimum(t, n_blk - 1)
        ins = [(gs, blk, lambda t, s, j=j: (s[2 + j], row(t), 0)) for j in range(3)]
        ins += [(a_buf, blk, lambda t, s, j=j: (s[5 + j], row(t), 0)) for j in range(3)]

        def fn(i, o):
            for j in range(3):
                o[0][j] = (i[j][...].astype(F32) + i[3 + j][...].astype(F32)).astype(BF)

        return dict(ins=ins, outs=[((3, R, C), BF, (3, tr, C), lambda t, s: (0, row(t), 0))], fn=fn, aliases={},
                    n_blk=n_blk)

    return build


def reduce_sum(gs, a_buf, b_buf, *, name):
    _, R, C = gs.shape
    tr = _tile(R, 256, 16)
    x, y, c, _ = _chip_slots()
    idx = jnp.stack([4 * x + 2 * y + c, 2 * x + y]).astype(jnp.int32)

    def body(idx_ref, g_ref, a_ref, b0_ref, b1_ref, b2_ref, o_ref):
        o_ref[...] = ((((g_ref[...].astype(F32) + a_ref[...].astype(F32)) + b0_ref[...].astype(F32))
                       + b1_ref[...].astype(F32)) + b2_ref[...].astype(F32))

    blk3 = (None, tr, C)
    return _pcall(body, name=name, grid=(R // tr,),
                  ins=[(gs, blk3, lambda i, s: (s[0], i, 0)), (a_buf, blk3, lambda i, s: (s[1], i, 0)),
                       (b_buf, blk3, lambda i, s: (0, i, 0)), (b_buf, blk3, lambda i, s: (1, i, 0)),
                       (b_buf, blk3, lambda i, s: (2, i, 0))],
                  outs=[((R, C), F32, (tr, C), lambda i, s: (i, 0))], prefetch=idx, semantics=("parallel",))[0]


def adam_rows(g, w, m, v, *, name):
    R, C = g.shape
    tr = _tile(R, 256, 8)

    def body(g_ref, w_ref, m_ref, v_ref, d_ref, nm_ref, nv_ref):
        d, nm, nv = _adamw(g_ref[...], w_ref[...], m_ref[...], v_ref[...])
        d_ref[...] = d
        nm_ref[...] = nm
        nv_ref[...] = nv

    spec = ((tr, C), lambda i: (i, 0))
    return _pcall(body, name=name, grid=(R // tr,), ins=[(t, *spec) for t in (g, w, m, v)],
                  outs=[((R, C), F32, *spec)] * 3, semantics=("parallel",))


def sum_rows8(gathered, rows, *, name):
    W = gathered.shape[1]

    def body(g_ref, o_ref):
        acc = g_ref[0:rows, :]
        for d in range(1, N_DEV):
            acc = acc + g_ref[d * rows:(d + 1) * rows, :]
        o_ref[...] = acc

    return _pcall(body, name=name, grid=(), ins=[(gathered, gathered.shape, lambda: (0, 0))],
                  outs=[((rows, W), F32, (rows, W), lambda: (0, 0))])[0]


HBM_SPEC = pl.BlockSpec(memory_space=pltpu.HBM)
SEM_SPEC = pl.BlockSpec(memory_space=pltpu.SEMAPHORE)
ANY_SPEC = pl.BlockSpec(memory_space=pl.ANY)
DATAFLOW = pltpu.SideEffectType.DATAFLOW_SIDE_EFFECTING


def _in_hbm(v):
    return pltpu.with_memory_space_constraint(v, pltpu.HBM)


def _slot(p):
    return 4 * p[0] + 2 * p[1] + p[2]


def _gather_peers():
    x, y, c, chips = _chip_slots()
    return (x, y, c), [(x, y, 1 - c)] + [(*chip, c) for chip in chips]


def gather_start(groups, after, *, name):
    flat = [s for g in groups for s in g]
    n, n_g = len(flat), len(groups)
    where = [(gi, ti) for gi, g in enumerate(groups) for ti in range(len(g))]

    def body(*refs):
        src, land = refs[:n], refs[n:2 * n]
        sems = refs[2 * n + 1:2 * n + 1 + 2 * n_g]
        me, peers = _gather_peers()
        for t in range(n):
            gi, ti = where[t]
            for k, to in enumerate(peers):
                pltpu.make_async_remote_copy(
                    src_ref=src[t], dst_ref=land[t].at[_slot(me)], send_sem=sems[2 * gi].at[4 * ti + k],
                    recv_sem=sems[2 * gi + 1].at[4 * ti + k], device_id=to, device_id_type=MESH).start()
        refs[-1][...] = jnp.zeros_like(refs[-1])

    out_shape = []
    for g in groups:
        out_shape += [pltpu.SemaphoreType.DMA((4 * len(g),)), pltpu.SemaphoreType.DMA((4 * len(g),))]
    out_shape += [pltpu.HBM(s.shape, s.dtype) for s in flat]
    out_shape += [pltpu.HBM((N_DEV,) + s.shape, s.dtype) for s in flat]
    out_shape += [jax.ShapeDtypeStruct((8, LANES), F32)]
    aliases = {t: 2 * n_g + t for t in range(n)}
    aliases.update({n + t: 2 * n_g + n + t for t in range(n)})
    res = pl.pallas_call(
        body, name=name, out_shape=out_shape, in_specs=[HBM_SPEC] * (2 * n) + [ANY_SPEC],
        out_specs=[SEM_SPEC] * (2 * n_g) + [HBM_SPEC] * (2 * n) + [pl.BlockSpec(memory_space=pltpu.VMEM)],
        input_output_aliases=aliases, compiler_params=pltpu.CompilerParams(has_side_effects=DATAFLOW),
    )(*[_in_hbm(s) for s in flat], *[_in_hbm(lax.empty((N_DEV,) + s.shape, s.dtype)) for s in flat], after)
    out, off = [], 0
    for gi, g in enumerate(groups):
        k = len(g)
        out.append((res[2 * gi], res[2 * gi + 1], res[2 * n_g + off:2 * n_g + off + k],
                    res[2 * n_g + n + off:2 * n_g + n + off + k]))
        off += k
    return out, res[-1]


def gather_wait(started, after, *, name):
    send_sems, recv_sems, srcs, lands = started
    n = len(srcs)
    after = list(after)

    def body(*refs):
        src, land = refs[:n], refs[n:2 * n]
        send, recv = refs[2 * n], refs[2 * n + 1]
        _, peers = _gather_peers()
        for t in range(n):
            for k, frm in enumerate(peers):
                cp = pltpu.make_async_remote_copy(
                    src_ref=src[t], dst_ref=land[t].at[_slot(frm)], send_sem=send.at[4 * t + k],
                    recv_sem=recv.at[4 * t + k],
                    device_id=frm, device_id_type=MESH)
                cp.wait_send()
                cp.wait_recv()

    res = pl.pallas_call(
        body, name=name,
        out_shape=[pltpu.HBM(s.shape, s.dtype) for s in srcs] + [pltpu.HBM(l.shape, l.dtype) for l in lands],
        in_specs=[HBM_SPEC] * (2 * n) + [SEM_SPEC, SEM_SPEC] + [ANY_SPEC] * len(after),
        out_specs=[HBM_SPEC] * (2 * n), input_output_aliases={t: t for t in range(2 * n)},
        compiler_params=pltpu.CompilerParams(has_side_effects=DATAFLOW),
    )(*srcs, *lands, send_sems, recv_sems, *after)
    return res[:n], res[n:]


def place_own(src, land, *, name):
    R, C = src.shape
    tr = _tile(R, 512, 16)
    x, y, c, _ = _chip_slots()
    idx = jnp.stack([4 * x + 2 * y + c]).astype(jnp.int32)

    def body(idx_ref, s_ref, land_ref, o_ref):
        o_ref[...] = s_ref[...]

    return _pcall(body, name=name, grid=(R // tr,),
                  ins=[(src, (tr, C), lambda i, s: (i, 0)), (land, None, None)],
                  outs=[(land.shape, land.dtype, (None, tr, C), lambda i, s: (s[0], i, 0))],
                  prefetch=idx, aliases={2: 0}, semantics=("parallel",))[0]


def gather_finish(srcs, lands, *, name):
    n = len(srcs)

    def body(*refs):
        land = refs[n:2 * n]
        send_sems, recv_sems = refs[2 * n:]
        x, y, c, chips = _chip_slots()
        me, sibling = (x, y, c), (x, y, 1 - c)

        def copy(t, j, block, to):
            return pltpu.make_async_remote_copy(
                src_ref=land[t].at[_slot(block)], dst_ref=land[t].at[_slot(block)], send_sem=send_sems.at[t, j],
                recv_sem=recv_sems.at[t, j], device_id=to, device_id_type=MESH)

        sends = [copy(t, j, (*chip, c), sibling) for t in range(n) for j, chip in enumerate(chips)]
        for cp in sends:
            cp.start()
        for t in range(n):
            for j, chip in enumerate(chips):
                copy(t, j, (*chip, 1 - c), me).wait_recv()
        for cp in sends:
            cp.wait_send()

    passed = pl.pallas_call(
        body, name=name, out_shape=[jax.ShapeDtypeStruct(l.shape, l.dtype) for l in lands],
        in_specs=[ANY_SPEC] * n, out_specs=[ANY_SPEC] * n,
        input_output_aliases={t: t for t in range(n)},
        scratch_shapes=[pltpu.SemaphoreType.DMA((n, 3)), pltpu.SemaphoreType.DMA((n, 3))],
    )(*lands)
    return [place_own(s, l, name=f"{name}_own{t}") for t, (s, l) in enumerate(zip(srcs, passed))]


def chips_start(pairs, *, name):
    n = len(pairs)

    def body(*refs):
        src, land = refs[:n], refs[n:2 * n]
        send, recv = refs[2 * n], refs[2 * n + 1]
        token = refs[-1]
        x, y, c, chips = _chip_slots()
        for t in range(n):
            for j, chip in enumerate(chips):
                pltpu.make_async_remote_copy(
                    src_ref=src[t].at[j], dst_ref=land[t].at[j], send_sem=send.at[3 * t + j],
                    recv_sem=recv.at[3 * t + j], device_id=(*chip, c), device_id_type=MESH).start()
        token[...] = jnp.zeros_like(token)

    res = pl.pallas_call(
        body, name=name,
        out_shape=[pltpu.SemaphoreType.DMA((3 * n,)), pltpu.SemaphoreType.DMA((3 * n,))]
        + [pltpu.HBM(p.shape, p.dtype) for p in pairs] * 2 + [jax.ShapeDtypeStruct((8, LANES), F32)],
        in_specs=[HBM_SPEC] * (2 * n),
        out_specs=[SEM_SPEC, SEM_SPEC] + [HBM_SPEC] * (2 * n) + [pl.BlockSpec(memory_space=pltpu.VMEM)],
        input_output_aliases={t: 2 + t for t in range(2 * n)},
        compiler_params=pltpu.CompilerParams(has_side_effects=DATAFLOW),
    )(*[_in_hbm(p) for p in pairs], *[_in_hbm(lax.empty(p.shape, p.dtype)) for p in pairs])
    return res[0], res[1], res[2:2 + n], res[2 + n:2 + 2 * n], res[-1]


def chips_wait(started, after, *, name):
    send_sems, recv_sems, srcs, lands, _ = started
    n = len(srcs)

    def body(*refs):
        src, land = refs[:n], refs[n:2 * n]
        send, recv = refs[2 * n], refs[2 * n + 1]
        x, y, c, chips = _chip_slots()
        for t in range(n):
            for j, chip in enumerate(chips):
                cp = pltpu.make_async_remote_copy(
                    src_ref=src[t].at[j], dst_ref=land[t].at[j], send_sem=send.at[3 * t + j],
                    recv_sem=recv.at[3 * t + j], device_id=(*chip, c), device_id_type=MESH)
                cp.wait_send()
                cp.wait_recv()

    res = pl.pallas_call(
        body, name=name, out_shape=[pltpu.HBM(s.shape, s.dtype) for s in srcs] * 2,
        in_specs=[HBM_SPEC] * (2 * n) + [SEM_SPEC, SEM_SPEC, ANY_SPEC], out_specs=[HBM_SPEC] * (2 * n),
        input_output_aliases={t: t for t in range(2 * n)},
        compiler_params=pltpu.CompilerParams(has_side_effects=DATAFLOW),
    )(*srcs, *lands, send_sems, recv_sems, after)
    return res[n:]


def _sibling_copies(src, land, send, recv, n):
    x, y, c, _ = _chip_slots()
    return [pltpu.make_async_remote_copy(
        src_ref=src[t].at[4 * (q // 2) + 2 * (q % 2) + (1 - c)], dst_ref=land[t].at[q], send_sem=send.at[4 * t + q],
        recv_sem=recv.at[4 * t + q], device_id=(x, y, 1 - c), device_id_type=MESH)
        for t in range(n) for q in range(4)]


def sibling_start(gs, *, name):
    n = len(gs)

    def body(*refs):
        for cp in _sibling_copies(refs[:n], refs[n:2 * n], refs[2 * n], refs[2 * n + 1], n):
            cp.start()
        refs[-1][...] = jnp.zeros_like(refs[-1])

    lands = [lax.empty((4,) + g.shape[1:], g.dtype) for g in gs]
    res = pl.pallas_call(
        body, name=name,
        out_shape=[pltpu.SemaphoreType.DMA((4 * n,)), pltpu.SemaphoreType.DMA((4 * n,))]
        + [pltpu.HBM(g.shape, g.dtype) for g in gs] + [pltpu.HBM(l.shape, l.dtype) for l in lands]
        + [jax.ShapeDtypeStruct((8, LANES), F32)],
        in_specs=[HBM_SPEC] * (2 * n),
        out_specs=[SEM_SPEC, SEM_SPEC] + [HBM_SPEC] * (2 * n) + [pl.BlockSpec(memory_space=pltpu.VMEM)],
        input_output_aliases={t: 2 + t for t in range(2 * n)},
        compiler_params=pltpu.CompilerParams(has_side_effects=DATAFLOW),
    )(*[_in_hbm(g) for g in gs], *[_in_hbm(l) for l in lands])
    return res[0], res[1], res[2:2 + n], res[2 + n:2 + 2 * n], res[-1]


def sibling_wait(started, after, *, name):
    send_sems, recv_sems, srcs, lands, _ = started
    n = len(srcs)

    def body(*refs):
        for cp in _sibling_copies(refs[:n], refs[n:2 * n], refs[2 * n], refs[2 * n + 1], n):
            cp.wait_send()
            cp.wait_recv()

    res = pl.pallas_call(
        body, name=name,
        out_shape=[pltpu.HBM(s.shape, s.dtype) for s in srcs] + [pltpu.HBM(l.shape, l.dtype) for l in lands],
        in_specs=[HBM_SPEC] * (2 * n) + [SEM_SPEC, SEM_SPEC, ANY_SPEC], out_specs=[HBM_SPEC] * (2 * n),
        input_output_aliases={t: t for t in range(2 * n)},
        compiler_params=pltpu.CompilerParams(has_side_effects=DATAFLOW),
    )(*srcs, *lands, send_sems, recv_sems, after)
    return res[:n], res[n:]


def _rope_slab(cols):
    z = jnp.zeros(cols.shape[:-1] + (HALF_ROPE,), cols.dtype)
    return jnp.concatenate([cols[..., :HALF_ROPE], z, cols[..., HALF_ROPE:], z], axis=-1)


def _rope_unslab(slab):
    return jnp.concatenate([slab[..., :HALF_ROPE], slab[..., 2 * HALF_ROPE:3 * HALF_ROPE]], axis=-1)


def _pack_w_in_t(wt_g):
    s, c, d = wt_g.shape
    w = wt_g.reshape(s * c, d)
    c2, c3 = Q_LORA + KV_LORA, Q_LORA + KV_LORA + QK_ROPE
    r = w[c2:c3]
    z = jnp.zeros((HALF_ROPE, d), w.dtype)
    return jnp.concatenate([w[:c2], w[c3:], r[:HALF_ROPE], z, r[HALF_ROPE:], z], axis=0)


def unpack_w_in_t_grad(dwt, *, name):
    n_rows, d = dwt.shape
    kr = QK_ROPE
    n_out_rows = n_rows - kr
    blk = n_out_rows // 7
    assert blk * 7 == n_out_rows and blk % kr == 0 and n_rows % (2 * kr) == 0
    kr_row = Q_LORA + KV_LORA
    k_mix = kr_row // blk
    off = kr_row - k_mix * blk
    slab_block = (n_rows - 2 * kr) // (2 * kr)

    def body(prev_ref, in_ref, slab_ref, o_ref):
        k = pl.program_id(0)

        @pl.when(k < k_mix)
        def _():
            o_ref[...] = in_ref[...]

        @pl.when(k == k_mix)
        def _():
            o_ref[:off, :] = in_ref[:off, :]
            o_ref[off:off + HALF_ROPE, :] = slab_ref[:HALF_ROPE, :]
            o_ref[off + HALF_ROPE:off + kr, :] = slab_ref[2 * HALF_ROPE:3 * HALF_ROPE, :]
            o_ref[off + kr:, :] = in_ref[off:blk - kr, :]

        @pl.when(k > k_mix)
        def _():
            o_ref[:kr, :] = prev_ref[blk - kr:, :]
            o_ref[kr:, :] = in_ref[:blk - kr, :]

    out = _pcall(body, name=name, grid=(7,),
                 ins=[(dwt, (blk, d), lambda k: (jnp.maximum(k - 1, 0), 0)), (dwt, (blk, d), lambda k: (k, 0)),
                      (dwt, (2 * kr, d), lambda k: (slab_block, 0))],
                 outs=[((n_out_rows, d), dwt.dtype, (blk, d), lambda k: (k, 0))], semantics=("parallel",))[0]
    return out.reshape(N_DEV, n_out_rows // N_DEV, d)


def _rope_tables(positions):
    inv_freq = ROPE_BASE ** (-jnp.arange(0, QK_ROPE, 2, dtype=F32) / QK_ROPE)
    ang = positions.astype(F32)[:, None] * inv_freq
    cos, sin = jnp.cos(ang), jnp.sin(ang)
    z = jnp.zeros_like(cos)
    return jnp.concatenate([cos, z, cos, z], axis=-1), jnp.concatenate([-sin, z, sin, z], axis=-1)


def _mlp_up(x, gain, w1, tag):
    hn = rms_fwd(x, gain, name=f"mlp{tag}_norm")

    def act_epi(acc):
        a = jnp.maximum(acc, 0.0)
        return a, a * a

    T = x.shape[0]
    F = w1.shape[0] * w1.shape[2]
    a, act = mm(hn, w1, name=f"mlp{tag}_up", outs=[((T, F), BF, None), ((T, F), BF, None)], epi=act_epi)
    return hn, a, act


def _mlp_down(x, act, w2, tag, part=0):
    n = w2.shape[1]
    bm = _tile(x.shape[0], MM_TILE)
    bn = _tile(n, MM_TILE)
    per = n // bn
    return mm(act, w2, name=f"mlp{tag}_down{part}", out=((x.shape[0], n), F32), bm=bm, bn=bn,
              epi=lambda acc, r: (acc + r[...],), epi_ins=[(x, (bm, bn), lambda i, j, k: (i, part * per + j))])


def _mlp_bwd_weights(w1, w2, saved, dxb, tag):
    hn, a, act = saved
    T, D = dxb.shape
    F = a.shape[1]
    bm = _tile(T, MM_TILE)
    bn = _tile(F, min(MM_TILE, w1.shape[2]))
    dhid = mm(dxb, w2, tb=True, name=f"mlp{tag}_dhid", out=((T, F), BF), bm=bm, bn=bn,
              epi=lambda acc, a_ref: (2.0 * a_ref[...].astype(F32) * acc,),
              epi_ins=[(a, (bm, bn), lambda i, j, k: (i, j))])
    dw2 = mm(act, dxb, ta=True, name=f"mlp{tag}_dw2", out=((F, D), BF))
    dw1 = mm(hn, dhid, ta=True, name=f"mlp{tag}_dw1", out=(w1.shape, BF))
    return dhid, dw1, dw2.reshape(N_DEV, F // N_DEV, D)


def _reduce_begin(grads, tag):
    return sibling_start(grads, name=f"reduce_sibling_start_{tag}")


def _reduce_continue(sib, after, tag, index):
    grads, a_bufs = sibling_wait(sib, after, name=f"reduce_sibling_wait_{tag}")
    pairs = [run_job(pair_job(g, a), index=index, name=f"pair_sum_{tag}{t}")[0]
             for t, (g, a) in enumerate(zip(grads, a_bufs))]
    return grads, a_bufs, chips_start(pairs, name=f"reduce_chips_start_{tag}")


def kernel(x, positions, e_norm_mix, e_w_in, e_q_norm, e_w_uq, e_kv_norm, e_w_ukv, e_v_norm, e_sgu_w, e_sgu_b, e_mla_out_norm, e_sgu_out_norm, e_w_out, o_norm_mix, o_w_in, o_conv_w, o_w_out, mlp_norm, mlp_w1, mlp_w2, final_norm, loss_target, m_e_norm_mix, m_e_w_in, m_e_q_norm, m_e_w_uq, m_e_kv_norm, m_e_w_ukv, m_e_v_norm, m_e_sgu_w, m_e_sgu_b, m_e_mla_out_norm, m_e_sgu_out_norm, m_e_w_out, m_o_norm_mix, m_o_w_in, m_o_conv_w, m_o_w_out, m_mlp_norm, m_mlp_w1, m_mlp_w2, m_final_norm, v_e_norm_mix, v_e_w_in, v_e_q_norm, v_e_w_uq, v_e_kv_norm, v_e_w_ukv, v_e_v_norm, v_e_sgu_w, v_e_sgu_b, v_e_mla_out_norm, v_e_sgu_out_norm, v_e_w_out, v_o_norm_mix, v_o_w_in, v_o_conv_w, v_o_w_out, v_mlp_norm, v_mlp_w1, v_mlp_w2, v_final_norm):
    T, D = x.shape[1], x.shape[2]
    d_shard = o_norm_mix.shape[1]
    x0 = x[0]
    target = loss_target[0]
    me = 4 * lax.axis_index("x") + 2 * lax.axis_index("y") + lax.axis_index("c")

    bf = lambda s: s.astype(BF)
    gather_groups = [[bf(jnp.transpose(e_w_in[0])), bf(e_w_uq[0]), bf(e_w_ukv[0])], [bf(e_w_out[0]), bf(mlp_w1[0])],
                     [bf(mlp_w2[0]), bf(o_w_in[0])], [bf(o_w_out[0]), bf(mlp_w1[1])], [bf(mlp_w2[1])]]
    small_rows = jnp.concatenate([o_norm_mix, o_conv_w[0], jnp.zeros((4, d_shard), F32)], axis=0)
    gather_groups[0].insert(0, small_rows)
    started, start_token = gather_start(gather_groups[:1], x0, name="gather_start0")
    started_rest, rest_token = gather_start(gather_groups[1:], start_token, name="gather_start1")
    started += started_rest

    def gathered(gi, after):
        srcs, lands = gather_wait(started[gi], after, name=f"gather_wait{gi}")
        return gather_finish(srcs, lands, name=f"gather_finish{gi}")

    w_tril = jnp.tril(e_sgu_w[0])
    w_tril_b = w_tril.astype(BF)
    w_tril_tb = jnp.swapaxes(w_tril, 1, 2).astype(BF)
    b_full = jnp.repeat(e_sgu_b[0].T, CH, axis=1)
    v_gain = e_v_norm[0].reshape(1, SGU_OUT)
    cos_t, sin_t = _rope_tables(positions[0])
    mlp_gain = [mlp_norm[0:1], mlp_norm[1:2]]
    final_gain = final_norm.reshape(1, D)

    h0 = rms_fwd(x0, e_norm_mix, name="e_norm", deps=[rest_token])
    small_g, g_w_in_t, g_w_uq, w_ukv = gathered(0, [h0, cos_t, sin_t, w_tril_b, w_tril_tb, b_full])
    o_norm_full = small_g[:, 0, :].reshape(1, D)
    conv_w_full = jnp.transpose(small_g[:, 1:4, :], (1, 0, 2)).reshape(3, D)
    w_in_t = _pack_w_in_t(g_w_in_t)
    w_uq = jnp.concatenate([g_w_uq[..., :QK_NOPE], _rope_slab(g_w_uq[..., QK_NOPE:])], axis=-1)
    proj = mm(h0, w_in_t, tb=True, name="e_in", out=((T, w_in_t.shape[0]), F32), bn=_tile(w_in_t.shape[0], 640))
    qn, kvn, krope = mla_prep(proj, e_q_norm, e_kv_norm, cos_t, sin_t, name="mla_prep")
    bm = _tile(T, MM_TILE)

    def q_epi(acc, cos_ref, sin_ref):
        return (jnp.concatenate([acc[:, :QK_NOPE], _rope_fwd(acc[:, QK_NOPE:], cos_ref[...], sin_ref[...])], axis=-1),)

    q = mm(qn, w_uq, name="mla_q", out=((T, HEADS * HEAD_PAD), BF), bm=bm, bn=HEAD_PAD, epi=q_epi,
           epi_ins=[(cos_t, (bm, LANES), lambda i, j, k: (i, 0)), (sin_t, (bm, LANES), lambda i, j, k: (i, 0))])

    def kv_epi(acc, kr_ref):
        return jnp.concatenate([acc[:, :QK_NOPE].astype(BF), kr_ref[...]], axis=-1), acc[:, QK_NOPE:]

    k, v = mm(kvn, w_ukv, name="mla_kv", bm=bm, bn=HEAD_PAD, epi=kv_epi,
              outs=[((T, HEADS * HEAD_PAD), BF, HEAD_PAD), ((T, MLA_OUT), BF, V_HEAD)],
              epi_ins=[(krope, (bm, LANES), lambda i, j, k: (i, 0))])
    attn, attn_lse = attn_fwd(q, k, v, name="attn_fwd")
    mixed = mix_fwd(attn, proj, e_mla_out_norm, e_sgu_out_norm, v_gain, w_tril_b, b_full, name="mix_fwd")
    bn = _tile(D, MM_TILE)
    g_w_out_e, w1_0 = gathered(1, [mixed])
    w_out_e = g_w_out_e.reshape(-1, D)
    x1 = mm(mixed, w_out_e, name="e_out", out=((T, D), F32), bm=bm, bn=bn,
            epi=lambda acc, r: (acc + r[...],), epi_ins=[(x0, (bm, bn), lambda i, j, k: (i, j))])
    hn0, a0, act0 = _mlp_up(x1, mlp_gain[0], w1_0, 0)
    g_w2_0, g_w_in_o = gathered(2, [act0])
    w2_0 = g_w2_0.reshape(-1, D)
    x2 = _mlp_down(x1, act0, w2_0, 0)
    ho = rms_fwd(x2, o_norm_full, name="o_norm")
    proj_o = mm(ho, g_w_in_o, name="o_in", out=((T, 3 * D), F32))
    gated = conv_fwd(proj_o, conv_w_full, name="conv_fwd")
    g_w_out_o, w1_1 = gathered(3, [gated])
    w_out_o = g_w_out_o.reshape(-1, D)
    x3 = mm(gated, w_out_o, name="o_out", out=((T, D), F32), bm=bm, bn=bn,
            epi=lambda acc, r: (acc + r[...],), epi_ins=[(x2, (bm, bn), lambda i, j, k: (i, j))])
    hn1, a1, act1 = _mlp_up(x3, mlp_gain[1], w1_1, 1)
    (g_w2_1,) = gathered(4, [act1])
    w2_1 = g_w2_1.reshape(-1, D)
    x4 = _mlp_down(x3, act1, w2_1, 1)
    w1, w2 = [w1_0, w1_1], [w2_0, w2_1]

    dx4, dx4b, d_final, loss_part = loss_bwd([x4], final_gain, target, name="loss_bwd")

    hosted = dict(job_index=device_index())
    dhid1, dw1_1, dw2_1 = _mlp_bwd_weights(w1[1], w2[1], (hn1, a1, act1), dx4b, 1)
    sib_r0 = _reduce_begin([dw1_1, dw2_1], "r0")
    dhn1 = mm(dhid1, w1[1], tb=True, name="mlp1_dhn", out=((T, D), F32), deps=[sib_r0[-1]])
    grads_r0, a_r0 = sibling_wait(sib_r0, dhn1, name="reduce_sibling_wait_r0")
    dx3, dx3b, d_mlp1 = rms_bwd(x3, mlp_gain[1], dhn1, dres=dx4, name="mlp1_norm_bwd")

    dgated, ((pair_r0a,),) = mm(dx3b, w_out_o, tb=True, name="o_out_dx", out=((T, D), F32),
                                jobs=[pair_job(grads_r0[0], a_r0[0])], **hosted)
    dw_out_o, ((pair_r0b,),) = mm(gated, dx3b, ta=True, name="o_out_dw", out=((D, D), BF),
                                  jobs=[pair_job(grads_r0[1], a_r0[1])], **hosted)
    st_r0 = chips_start([pair_r0a, pair_r0b], name="reduce_chips_start_r0")
    dproj_o, dconv_full = conv_bwd(dgated, proj_o, conv_w_full, name="conv_bwd", deps=[st_r0[-1]])
    dw_in_o = mm(ho, dproj_o, ta=True, name="o_in_dw", out=(g_w_in_o.shape, BF))
    sib_r1 = _reduce_begin([dw_out_o.reshape(g_w_out_o.shape), dw_in_o], "r1")
    dho = mm(dproj_o, g_w_in_o, tb=True, name="o_in_dx", out=((T, D), F32), deps=[sib_r1[-1]])
    grads_r1, a_r1 = sibling_wait(sib_r1, dho, name="reduce_sibling_wait_r1")
    dx2, dx2b, d_onorm_full = rms_bwd(x2, o_norm_full, dho, dres=dx3, name="o_norm_bwd")

    d_ff = a0.shape[1]
    bm_h, bn_h = _tile(T, MM_TILE), _tile(d_ff, min(MM_TILE, w1[0].shape[2]))
    dhid0, ((pair_r1a,), (pair_r1b,)) = mm(
        dx2b, w2[0], tb=True, name="mlp0_dhid", out=((T, d_ff), BF), bm=bm_h, bn=bn_h,
        epi=lambda acc, a_ref: (2.0 * a_ref[...].astype(F32) * acc,),
        epi_ins=[(a0, (bm_h, bn_h), lambda i, j, k: (i, j))],
        jobs=[pair_job(grads_r1[0], a_r1[0]), pair_job(grads_r1[1], a_r1[1])], **hosted)
    st_r1 = chips_start([pair_r1a, pair_r1b], name="reduce_chips_start_r1")
    dw2_0 = mm(act0, dx2b, ta=True, name="mlp0_dw2", out=((d_ff, D), BF), deps=[st_r1[-1]])
    b_r0 = chips_wait(st_r0, dw2_0, name="reduce_chips_wait_r0")
    dw1_0, (r_w1, r_w2) = mm(
        hn0, dhid0, ta=True, name="mlp0_dw1", out=(w1[0].shape, BF),
        jobs=[adam_job(grads_r0[0], a_r0[0], b_r0[0], mlp_w1, m_mlp_w1, v_mlp_w1, 1, None),
              adam_job(grads_r0[1], a_r0[1], b_r0[1], mlp_w2, m_mlp_w2, v_mlp_w2, 1, None)], **hosted)
    sib_r2 = _reduce_begin([dw1_0, dw2_0.reshape(N_DEV, d_ff // N_DEV, D)], "r2")
    dhn0 = mm(dhid0, w1[0], tb=True, name="mlp0_dhn", out=((T, D), F32), deps=[sib_r2[-1]])
    grads_r2, a_r2 = sibling_wait(sib_r2, dhn0, name="reduce_sibling_wait_r2")
    dx1, dx1b, d_mlp0 = rms_bwd(x1, mlp_gain[0], dhn0, dres=dx2, name="mlp0_norm_bwd")

    dmixed, ((pair_r2a,),) = mm(dx1b, w_out_e, tb=True, name="e_out_dx", out=((T, MLA_OUT + SGU_OUT), F32),
                                jobs=[pair_job(grads_r2[0], a_r2[0])], **hosted)
    dw_out_e, ((pair_r2b,),) = mm(mixed, dx1b, ta=True, name="e_out_dw", out=(w_out_e.shape, BF),
                                  jobs=[pair_job(grads_r2[1], a_r2[1])], **hosted)
    st_r2 = chips_start([pair_r2a, pair_r2b], name="reduce_chips_start_r2")
    (dattn, dproj, d_mla_out, d_sgu_out, d_vgain, d_sgu_w, d_b_full) = mix_bwd(
        dmixed, attn, proj, e_mla_out_norm, e_sgu_out_norm, v_gain, w_tril_b, w_tril_tb, b_full, name="mix_bwd",
        deps=[st_r2[-1]])
    b_r1 = chips_wait(st_r1, dattn, name="reduce_chips_wait_r1")
    dq, dk, dv = attn_bwd(q, k, v, attn, attn_lse, dattn, name="attn_bwd")
    dq_lin, dkv_lin, dproj = mla_bwd_prep(dq, dk, dv, cos_t, sin_t, dproj, name="mla_bwd_prep")
    dw_uq_pad = mm(qn, dq_lin, ta=True, name="mla_q_dw", out=(w_uq.shape, BF))
    dw_ukv = mm(kvn, dkv_lin, ta=True, name="mla_kv_dw", out=(w_ukv.shape, BF))
    dw_uq = jnp.concatenate([dw_uq_pad[..., :QK_NOPE], _rope_unslab(dw_uq_pad[..., QK_NOPE:])], axis=-1)
    sib_r2b = _reduce_begin([dw_out_e.reshape(g_w_out_e.shape), dw_uq, dw_ukv], "r2b")
    dqn = mm(dq_lin, w_uq, tb=True, name="mla_q_dx", out=((T, Q_LORA), F32), deps=[sib_r2b[-1]])
    dkvn = mm(dkv_lin, w_ukv, tb=True, name="mla_kv_dx", out=((T, KV_LORA), F32), deps=[sib_r2b[-1]])
    grads_r2b, a_r2b, st_r2b = _reduce_continue(sib_r2b, dkvn, "r2b", hosted["job_index"])
    dproj, d_qnorm = rms_bwd(proj, e_q_norm, dqn, col_block=0, want_f32=False, into=dproj, name="q_norm_bwd",
                             deps=[st_r2b[-1]])
    dproj, d_kvnorm = rms_bwd(proj, e_kv_norm, dkvn, col_block=1, want_f32=False, into=dproj, name="kv_norm_bwd")
    dw_in_t_pad, (r_w_out_o, r_w_in_o) = mm(
        dproj, h0, ta=True, name="e_in_dw", out=(w_in_t.shape, BF), bm=_tile(w_in_t.shape[0], 640),
        jobs=[adam_job(grads_r1[0], a_r1[0], b_r1[0], o_w_out, m_o_w_out, v_o_w_out, 0, None),
              adam_job(grads_r1[1], a_r1[1], b_r1[1], o_w_in, m_o_w_in, v_o_w_in, 0, None)], **hosted)
    dw_in_t = unpack_w_in_t_grad(dw_in_t_pad, name="e_in_dw_unpack")
    sib_r3 = _reduce_begin([dw_in_t], "r3")
    dh0 = mm(dproj, w_in_t, name="e_in_dx", out=((T, D), F32), deps=[sib_r3[-1]])
    grads_r3, a_r3, st_r3 = _reduce_continue(sib_r3, dh0, "r3", hosted["job_index"])
    tok_r3 = st_r3[-1]
    grad_x, d_enorm = rms_bwd(x0, e_norm_mix, dh0, dres=dx1, want_bf=False, name="e_norm_bwd", deps=[tok_r3])
    b_r2 = chips_wait(st_r2, grad_x, name="reduce_chips_wait_r2")

    d_sgu_b = jnp.transpose(d_b_full[:, ::CH])
    d_sgu_w_tril = jnp.tril(d_sgu_w)
    rep = [("e_norm_mix", e_norm_mix, m_e_norm_mix, v_e_norm_mix, d_enorm),
           ("e_q_norm", e_q_norm, m_e_q_norm, v_e_q_norm, d_qnorm),
           ("e_kv_norm", e_kv_norm, m_e_kv_norm, v_e_kv_norm, d_kvnorm),
           ("e_v_norm", e_v_norm, m_e_v_norm, v_e_v_norm, d_vgain),
           ("e_sgu_w", e_sgu_w, m_e_sgu_w, v_e_sgu_w, d_sgu_w_tril),
           ("e_sgu_b", e_sgu_b, m_e_sgu_b, v_e_sgu_b, d_sgu_b),
           ("e_mla_out_norm", e_mla_out_norm, m_e_mla_out_norm, v_e_mla_out_norm, d_mla_out),
           ("e_sgu_out_norm", e_sgu_out_norm, m_e_sgu_out_norm, v_e_sgu_out_norm, d_sgu_out),
           ("mlp_norm", mlp_norm, m_mlp_norm, v_mlp_norm, jnp.concatenate([d_mlp0, d_mlp1], axis=0)),
           ("final_norm", final_norm, m_final_norm, v_final_norm, d_final)]
    sizes = [int(np.prod(r[1].shape)) for r in rep]
    n_rep = sum(sizes)
    n_all = n_rep + 4 * D + 1
    width = -(-n_all // (8 * LANES)) * LANES
    pad = 8 * width - n_all
    flat = jnp.concatenate([r[4].reshape(-1) for r in rep]
                           + [d_onorm_full.reshape(-1), dconv_full.reshape(-1), loss_part[0, :1],
                              jnp.zeros((pad,), F32)])
    small_started, small_token = gather_start([[flat.reshape(8, width)]], b_r2[0], name="gather_small_grads_start")

    def finish(grads, a_bufs, b_bufs, t, w, m, v, layer=0, prev=None, tag="", deps=()):
        return run_job(adam_job(grads[t], a_bufs[t], b_bufs[t], w, m, v, layer, prev), index=hosted["job_index"],
                       name=f"adam_{tag}", deps=deps)

    r_w1 = finish(grads_r2, a_r2, b_r2, 0, mlp_w1, m_mlp_w1, v_mlp_w1, 0, r_w1, tag="w1_l0", deps=[tok_r3, small_token])
    r_w2 = finish(grads_r2, a_r2, b_r2, 1, mlp_w2, m_mlp_w2, v_mlp_w2, 0, r_w2, tag="w2_l0", deps=[r_w1[1]])
    b_r2b = chips_wait(st_r2b, r_w2[1], name="reduce_chips_wait_r2b")
    r_w_out_e = finish(grads_r2b, a_r2b, b_r2b, 0, e_w_out, m_e_w_out, v_e_w_out, tag="e_w_out")
    r_w_uq = finish(grads_r2b, a_r2b, b_r2b, 1, e_w_uq, m_e_w_uq, v_e_w_uq, tag="e_w_uq")
    r_w_ukv = finish(grads_r2b, a_r2b, b_r2b, 2, e_w_ukv, m_e_w_ukv, v_e_w_ukv, tag="e_w_ukv")
    b_r3 = chips_wait(st_r3, r_w_out_e[1], name="reduce_chips_wait_r3")
    g_w_in_t = reduce_sum(grads_r3[0], a_r3[0], b_r3[0], name="sum_e_w_in")
    w_in_upd_t = adam_rows(g_w_in_t, jnp.transpose(e_w_in[0]), jnp.transpose(m_e_w_in[0]), jnp.transpose(v_e_w_in[0]),
                           name="adam_e_w_in")
    r_w_in = [jnp.transpose(t)[None] for t in (g_w_in_t, *w_in_upd_t)]

    small_srcs, small_lands = gather_wait(small_started[0], [w_in_upd_t[0]], name="gather_small_grads_wait")
    small_all = gather_finish(small_srcs, small_lands, name="gather_small_grads_finish")[0]
    summed = sum_rows8(small_all.reshape(N_DEV * 8, width), 8, name="sum_small_grads").reshape(-1)

    loss = summed[n_rep + 4 * D]

    def pack_rep(i):
        return jnp.concatenate([r[i].reshape(-1) for r in rep]).reshape(n_rep // LANES, LANES)

    g_rep = summed[:n_rep].reshape(n_rep // LANES, LANES)
    d_rep, nm_rep, nv_rep = adam_flat(g_rep, pack_rep(1), pack_rep(2), pack_rep(3), name="adam_replicated")

    def unpack_rep(flat2d):
        out, off = {}, 0
        f = flat2d.reshape(-1)
        for r, n in zip(rep, sizes):
            out[r[0]] = f[off:off + n].reshape(r[1].shape)
            off += n
        return out

    small = {"grad": unpack_rep(g_rep), "delta": unpack_rep(d_rep), "new_m": unpack_rep(nm_rep),
             "new_v": unpack_rep(nv_rep)}
    g_onorm = lax.dynamic_slice(summed[n_rep:n_rep + D].reshape(1, D), (0, me * d_shard), (1, d_shard))
    g_conv = lax.dynamic_slice(summed[n_rep + D:n_rep + 4 * D].reshape(3, D), (0, me * d_shard), (3, d_shard))

    def pack_sharded(norm_part, conv_part):
        return jnp.concatenate([norm_part, conv_part, jnp.zeros((4, d_shard), F32)], axis=0)

    g_sh = pack_sharded(g_onorm, g_conv)
    d_sh, nm_sh, nv_sh = adam_flat(g_sh, pack_sharded(o_norm_mix, o_conv_w[0]), pack_sharded(m_o_norm_mix, m_o_conv_w[0]),
                                   pack_sharded(v_o_norm_mix, v_o_conv_w[0]), name="adam_sharded_small")
    for kind, arr in (("grad", g_sh), ("delta", d_sh), ("new_m", nm_sh), ("new_v", nv_sh)):
        small[kind]["o_norm_mix"] = arr[0:1]
        small[kind]["o_conv_w"] = arr[1:4][None]

    big = {"e_w_in": r_w_in, "e_w_uq": r_w_uq, "e_w_ukv": r_w_ukv, "e_w_out": r_w_out_e, "o_w_in": r_w_in_o,
           "o_w_out": r_w_out_o, "mlp_w1": r_w1, "mlp_w2": r_w2}
    order = ["e_norm_mix", "e_w_in", "e_q_norm", "e_w_uq", "e_kv_norm", "e_w_ukv", "e_v_norm", "e_sgu_w", "e_sgu_b",
             "e_mla_out_norm", "e_sgu_out_norm", "e_w_out", "o_norm_mix", "o_w_in", "o_conv_w", "o_w_out", "mlp_norm",
             "mlp_w1", "mlp_w2", "final_norm"]
    result = [loss, grad_x[None]]
    for ki, kind in enumerate(("grad", "delta", "new_m", "new_v")):
        for nm in order:
            result.append(big[nm][ki] if nm in big else small[kind][nm])
    return tuple(result)
```

```python
import numpy as np
import jax
import jax.numpy as jnp
from jax import lax
from jax.experimental import pallas as pl
from jax.experimental.pallas import tpu as pltpu

BF = jnp.bfloat16
F32 = jnp.float32
MESH = pl.DeviceIdType.MESH
N_DEV = 8

EPS = 1e-6
HEADS = 8
Q_LORA = 512
KV_LORA = 512
QK_NOPE = 128
QK_ROPE = 64
HALF_ROPE = QK_ROPE // 2
V_HEAD = 128
HEAD_PAD = 256
ROPE_BASE = 10000.0
GROUPS = 8
CH = 128
CHUNK = 128
SGU_OUT = GROUPS * CH
MLA_OUT = HEADS * V_HEAD
ATTN_SCALE = float((QK_NOPE + QK_ROPE) ** -0.5)

ADAM_LR = 0.001
ADAM_B1 = 0.9
ADAM_B2 = 0.999
ADAM_EPS = 1e-08
ADAM_WD = 0.01
ADAM_STEP = 10
ADAM_C1 = 1.0 - ADAM_B1 ** ADAM_STEP
ADAM_C2 = 1.0 - ADAM_B2 ** ADAM_STEP

V7X_VMEM_BYTES = 64 * 2 ** 20
VMEM_LIMIT_CAP = V7X_VMEM_BYTES - 6 * 2 ** 20
LANES = 128
ROW_TILE = 256
ATTN_TILE = 1024
STREAM_BLOCK_ELEMS = 512 * 1024
MM_TILE = 1024
MM_K_TILE = 2048
MM_K_BLOCK_MAX = 4096


def _padded_bytes(block, dtype):
    dims = [d for d in block if d is not None]
    if len(dims) >= 1:
        dims[-1] = -(-dims[-1] // LANES) * LANES
    if len(dims) >= 2:
        dims[-2] = -(-dims[-2] // 16) * 16
    return int(np.prod(dims)) * jnp.dtype(dtype).itemsize


def _pcall(body, *, name, grid, ins, outs, scratch=(), semantics=None, aliases=None, prefetch=None, deps=()):
    any_spec = pl.BlockSpec(memory_space=pl.ANY)
    if deps:
        n_lead = len(ins) + (1 if prefetch is not None else 0)
        n_deps = len(deps)
        inner = body

        def body(*refs):
            inner(*refs[:n_lead], *refs[n_lead + n_deps:])

        ins = list(ins) + [(d, None, None) for d in deps]
    in_specs = [any_spec if b is None else pl.BlockSpec(b, m) for _, b, m in ins]
    out_specs = [any_spec if b is None else pl.BlockSpec(b, m) for _, _, b, m in outs]
    out_shape = [pltpu.HBM(s, d) for s, d, _, _ in outs]
    est = 0
    for a, b, _ in ins:
        if b is not None:
            est += 2 * _padded_bytes(b, a.dtype)
    for _, d, b, _ in outs:
        if b is not None:
            est += 2 * _padded_bytes(b, d)
    for s in scratch:
        if hasattr(s, "shape") and hasattr(s, "dtype"):
            est += _padded_bytes(s.shape, s.dtype)
    limit = int(min(VMEM_LIMIT_CAP, est + 16 * 2 ** 20))
    params = pltpu.CompilerParams(
        dimension_semantics=semantics or ("arbitrary",) * len(grid), vmem_limit_bytes=limit)
    args = [pltpu.with_memory_space_constraint(a, pltpu.HBM) for a, _, _ in ins]
    if prefetch is not None:
        grid_spec = pltpu.PrefetchScalarGridSpec(
            num_scalar_prefetch=1, grid=grid, in_specs=in_specs, out_specs=out_specs, scratch_shapes=list(scratch))
        call = pl.pallas_call(body, out_shape=out_shape, grid_spec=grid_spec, name=name, compiler_params=params,
                              input_output_aliases=aliases or {})
        return call(prefetch, *args)
    call = pl.pallas_call(body, out_shape=out_shape, grid=grid, in_specs=in_specs, out_specs=out_specs,
                          scratch_shapes=list(scratch), name=name, compiler_params=params,
                          input_output_aliases=aliases or {})
    return call(*args)


def _tile(dim, pref, quantum=LANES):
    if dim <= pref:
        return dim
    t = (pref // quantum) * quantum
    while t >= quantum:
        if dim % t == 0:
            return t
        t -= quantum
    return dim


def _vshape(arr_shape):
    if len(arr_shape) == 2:
        return tuple(arr_shape)
    s, r, c = arr_shape
    return (r, s * c)


def _vblock(arr_shape, br, bc, rc):
    if len(arr_shape) == 2:
        return (br, bc), (lambda *g: rc(*g))
    _, _, c = arr_shape
    assert c % bc == 0, (arr_shape, bc)
    per = c // bc

    def imap(*g):
        ri, ci = rc(*g)
        return (ci // per, ri, ci % per)

    return (None, br, bc), imap


def _shard_width(*shapes):
    w = None
    for s in shapes:
        if len(s) == 3:
            w = s[2] if w is None else int(np.gcd(w, s[2]))
    return w


def mm(a, b, *, name, ta=False, tb=False, out=None, outs=None, epi=None, epi_ins=(), bm=None, bn=None, bk=None,
       deps=(), jobs=(), job_index=None):
    av, bv = _vshape(a.shape), _vshape(b.shape)
    M, K = (av[1], av[0]) if ta else av
    K2, N = (bv[1], bv[0]) if tb else bv
    assert K == K2, (a.shape, b.shape, ta, tb)
    if outs is None:
        outs = [(out[0], out[1], None)]
    a_sw = _shard_width(a.shape)
    b_sw = _shard_width(b.shape)
    o_sw = _shard_width(*[o[0] for o in outs])
    m_lim = a_sw if (ta and a_sw) else None
    k_lim = [w for w in ((a_sw if not ta else None), (b_sw if tb else None)) if w]
    n_lim = [w for w in ((b_sw if not tb else None), o_sw) if w]
    if bm is None:
        bm = _tile(M, min([MM_TILE] + ([m_lim] if m_lim else [])))
    if bn is None:
        bn = _tile(N, min([MM_TILE] + n_lim))
    k_shards = 0
    if tb and len(b.shape) == 3 and bk is None and not (a_sw and not ta):
        k_shards = 1
        while 2 * k_shards <= b.shape[0] and 2 * k_shards * b_sw <= MM_K_BLOCK_MAX:
            k_shards *= 2
        bk = k_shards * b_sw
    if bk is None:
        bk = K if (K <= 4096 and not k_lim) else _tile(K, min([MM_K_TILE] + k_lim))
    assert M % bm == 0 and N % bn == 0 and K % bk == 0, (name, M, N, K, bm, bn, bk)
    nk = K // bk
    grid = (M // bm, N // bn, nk)
    if ta:
        a_blk, a_map = _vblock(a.shape, bk, bm, lambda i, j, k: (k, i))
    else:
        a_blk, a_map = _vblock(a.shape, bm, bk, lambda i, j, k: (i, k))
    if k_shards:
        b_blk, b_map = (k_shards, bn, b_sw), (lambda i, j, k: (k, j, 0))
    elif tb:
        b_blk, b_map = _vblock(b.shape, bn, bk, lambda i, j, k: (j, k))
    else:
        b_blk, b_map = _vblock(b.shape, bk, bn, lambda i, j, k: (k, j))
    dn = (((0 if ta else 1,), (1 if tb else 0,)), ((), ()))
    ins = [(a, a_blk, a_map), (b, b_blk, b_map)] + list(epi_ins)
    out_list = []
    for shape, dtype, cols in outs:
        cols = cols or bn
        blk, imap = _vblock(shape, bm, cols, lambda i, j, k: (i, j))
        out_list.append((shape, dtype, blk, imap))
    n_e, n_o = len(epi_ins), len(out_list)

    n_steps = grid[0] * grid[1] * nk
    built = [job(n_steps) for job in jobs]
    aliases = {}
    job_slices = []
    if built:
        def lin(i, j, k):
            return (i * grid[1] + j) * nk + k

        ins = [(arr, blk, None if blk is None else (lambda i, j, k, s, f=f: f(i, j, k))) for arr, blk, f in ins]
        out_list = [(sh, dt, blk, (lambda i, j, k, s, f=f: f(i, j, k))) for sh, dt, blk, f in out_list]
        n_main_in, n_main_out = len(ins), len(out_list)
        for jb in built:
            i0, o0 = len(ins), len(out_list)
            ins += [(arr, blk, None if blk is None else (lambda i, j, k, s, f=f: f(lin(i, j, k), s)))
                    for arr, blk, f in jb["ins"]]
            out_list += [(sh, dt, blk, (lambda i, j, k, s, f=f: f(lin(i, j, k), s))) for sh, dt, blk, f in jb["outs"]]
            aliases.update({1 + i0 + ai: o0 + ao for ai, ao in jb["aliases"].items()})
            job_slices.append((i0, len(jb["ins"]), o0, len(jb["outs"])))
    n_in_total = len(ins)

    def body(*refs):
        if built:
            refs = refs[1:]
        a_ref, b_ref = refs[0], refs[1]
        e_refs = refs[2:2 + n_e]
        o_refs = refs[n_in_total:n_in_total + n_o]
        for jb, (i0, ni, o0, no) in zip(built, job_slices):
            jb["fn"](refs[i0:i0 + ni], refs[n_in_total + o0:n_in_total + o0 + no])

        def finish(acc):
            res = epi(acc, *e_refs) if epi is not None else (acc,)
            for o_ref, r in zip(o_refs, res):
                o_ref[...] = r.astype(o_ref.dtype)

        x = a_ref[...].astype(BF)
        y = b_ref[...].astype(BF)
        if k_shards:
            p = None
            for s in range(k_shards):
                part = lax.dot_general(x[:, s * b_sw:(s + 1) * b_sw], y[s], dn, preferred_element_type=F32)
                p = part if p is None else p + part
        else:
            p = lax.dot_general(x, y, dn, preferred_element_type=F32)
        if nk == 1:
            finish(p)
        else:
            acc_ref = refs[-1]
            k = pl.program_id(2)

            @pl.when(k == 0)
            def _():
                acc_ref[...] = p

            @pl.when(k > 0)
            def _():
                acc_ref[...] += p

            @pl.when(k == nk - 1)
            def _():
                finish(acc_ref[...])

    scratch = [pltpu.VMEM((bm, bn), F32)] if nk > 1 else []
    res = _pcall(body, name=name, grid=grid, ins=ins, outs=out_list, scratch=scratch, deps=deps,
                 semantics=("parallel", "parallel", "arbitrary"), prefetch=job_index if built else None, aliases=aliases)
    main = res[0] if n_o == 1 else res[:n_o]
    if not built:
        return main
    return main, [res[o0:o0 + no] for _, _, o0, no in job_slices]


_GELU_K = float(np.sqrt(2.0 / np.pi))
_GELU_C = 0.044715


def _gelu(x):
    t = jnp.tanh(_GELU_K * (x + _GELU_C * (x * x * x)))
    return 0.5 * x * (1.0 + t)


def _gelu_grad(x):
    t = jnp.tanh(_GELU_K * (x + _GELU_C * (x * x * x)))
    return 0.5 * (1.0 + t) + 0.5 * x * (1.0 - t * t) * (_GELU_K * (1.0 + 3.0 * _GELU_C * (x * x)))


def _rstd(x):
    return lax.rsqrt(jnp.mean(x * x, axis=-1, keepdims=True) + EPS)


def _rms_bwd(x, gain, dy):
    r = _rstd(x)
    xh = x * r
    gdy = dy * gain
    dx = r * (gdy - xh * jnp.mean(gdy * xh, axis=-1, keepdims=True))
    return dx, dy * xh


def _rope_fwd(x, cos_t, sin_t):
    return x * cos_t + pltpu.roll(x, 2 * HALF_ROPE, 1) * sin_t


def _rope_bwd(dy, cos_t, sin_t):
    return dy * cos_t + pltpu.roll(dy * sin_t, 2 * HALF_ROPE, 1)


def _acc_rows(ref, val, first):
    s = jnp.sum(val, axis=0, keepdims=True)

    @pl.when(first)
    def _():
        ref[...] = s

    @pl.when(jnp.logical_not(first))
    def _():
        ref[...] += s


def rms_fwd(x, gain, *, name, col_block=0, width=None, deps=()):
    T = x.shape[0]
    width = width or x.shape[1]
    tm = _tile(T, ROW_TILE, 8)

    def body(x_ref, g_ref, o_ref):
        v = x_ref[...]
        o_ref[...] = (v * _rstd(v) * g_ref[...]).astype(BF)

    return _pcall(body, name=name, grid=(T // tm,),
                  ins=[(x, (tm, width), lambda i: (i, col_block)), (gain, (1, width), lambda i: (0, 0))],
                  outs=[((T, width), BF, (tm, width), lambda i: (i, 0))], semantics=("parallel",), deps=deps)[0]


def rms_bwd(x, gain, dy, *, name, col_block=0, dres=None, want_f32=True, want_bf=True, into=None, deps=()):
    T, width = dy.shape
    tm = _tile(T, ROW_TILE, 8)
    has_res = dres is not None

    def body(*refs):
        x_ref, g_ref, dy_ref = refs[:3]
        pos = 3
        res_ref = None
        if has_res:
            res_ref = refs[pos]
            pos += 1
        if into is not None:
            pos += 1
        outs = refs[pos:]
        dx, dg_rows = _rms_bwd(x_ref[...], g_ref[...], dy_ref[...])
        if has_res:
            dx = dx + res_ref[...]
        o = 0
        if want_f32:
            outs[o][...] = dx
            o += 1
        if want_bf:
            outs[o][...] = dx.astype(BF)
            o += 1
        _acc_rows(outs[o], dg_rows, pl.program_id(0) == 0)

    ins = [(x, (tm, width), lambda i: (i, col_block)), (gain, (1, width), lambda i: (0, 0)),
           (dy, (tm, width), lambda i: (i, 0))]
    if has_res:
        ins.append((dres, (tm, width), lambda i: (i, 0)))
    outs = []
    aliases = {}
    if want_f32:
        outs.append(((T, width), F32, (tm, width), lambda i: (i, 0)))
    if want_bf and into is not None:
        ins.append((into, None, None))
        aliases[len(ins) - 1] = len(outs)
        outs.append((into.shape, BF, (tm, width), lambda i: (i, col_block)))
    elif want_bf:
        outs.append(((T, width), BF, (tm, width), lambda i: (i, 0)))
    outs.append(((1, width), F32, (1, width), lambda i: (0, 0)))
    return _pcall(body, name=name, grid=(T // tm,), ins=ins, outs=outs, aliases=aliases, deps=deps)


def mla_prep(proj, q_norm, kv_norm, cos_t, sin_t, *, name):
    T = proj.shape[0]
    tm = _tile(T, ROW_TILE, 8)
    kr_block = (proj.shape[1] - LANES) // LANES

    def body(cq_ref, ckv_ref, kr_ref, qg_ref, kg_ref, cos_ref, sin_ref, qn_ref, kvn_ref, krope_ref):
        cq = cq_ref[...]
        qn_ref[...] = (cq * _rstd(cq) * qg_ref[...]).astype(BF)
        ckv = ckv_ref[...]
        kvn_ref[...] = (ckv * _rstd(ckv) * kg_ref[...]).astype(BF)
        krope_ref[...] = _rope_fwd(kr_ref[...], cos_ref[...], sin_ref[...]).astype(BF)

    return _pcall(
        body, name=name, grid=(T // tm,),
        ins=[(proj, (tm, Q_LORA), lambda i: (i, 0)), (proj, (tm, KV_LORA), lambda i: (i, 1)),
             (proj, (tm, LANES), lambda i: (i, kr_block)),
             (q_norm, (1, Q_LORA), lambda i: (0, 0)), (kv_norm, (1, KV_LORA), lambda i: (0, 0)),
             (cos_t, (tm, LANES), lambda i: (i, 0)), (sin_t, (tm, LANES), lambda i: (i, 0))],
        outs=[((T, Q_LORA), BF, (tm, Q_LORA), lambda i: (i, 0)), ((T, KV_LORA), BF, (tm, KV_LORA), lambda i: (i, 0)),
              ((T, LANES), BF, (tm, LANES), lambda i: (i, 0))],
        semantics=("parallel",))


def _attn_scores(q, k_blk, diagonal):
    s = lax.dot_general(q, k_blk, (((1,), (1,)), ((), ())), preferred_element_type=F32) * ATTN_SCALE
    if diagonal:
        row = lax.broadcasted_iota(jnp.int32, s.shape, 0)
        col = lax.broadcasted_iota(jnp.int32, s.shape, 1)
        s = jnp.where(col <= row, s, -jnp.inf)
    return s


def attn_fwd(q, k, v, *, name):
    T = q.shape[0]
    tq = _tile(T, ATTN_TILE, 8)

    def body(q_ref, k_ref, v_ref, o_ref, lse_ref):
        i = pl.program_id(1)
        qv = q_ref[...]

        def block(kb, carry, diagonal):
            m, l, acc = carry
            start = pl.multiple_of(kb * tq, tq)
            s = _attn_scores(qv, k_ref[pl.ds(start, tq), :], diagonal)
            m_new = jnp.maximum(m, jnp.max(s, axis=-1, keepdims=True))
            alpha = jnp.exp(m - m_new)
            p = jnp.exp(s - m_new)
            l = alpha * l + jnp.sum(p, axis=-1, keepdims=True)
            acc = alpha * acc + jnp.dot(p.astype(BF), v_ref[pl.ds(start, tq), :], preferred_element_type=F32)
            return m_new, l, acc

        init = (jnp.full((tq, 1), -jnp.inf, F32), jnp.zeros((tq, 1), F32), jnp.zeros((tq, V_HEAD), F32))
        carry = lax.fori_loop(0, i, lambda kb, c: block(kb, c, False), init)
        m, l, acc = block(i, carry, True)
        o_ref[...] = acc / l
        lse_ref[...] = jnp.broadcast_to(m + jnp.log(l), (tq, V_HEAD))

    return _pcall(
        body, name=name, grid=(HEADS, T // tq),
        ins=[(q, (tq, HEAD_PAD), lambda h, i: (i, h)), (k, (T, HEAD_PAD), lambda h, i: (0, h)),
             (v, (T, V_HEAD), lambda h, i: (0, h))],
        outs=[((T, MLA_OUT), F32, (tq, V_HEAD), lambda h, i: (i, h)),
              ((T, MLA_OUT), F32, (tq, V_HEAD), lambda h, i: (i, h))], semantics=("parallel", "parallel"))


def attn_bwd(q, k, v, o, lse, do, *, name):
    T = q.shape[0]
    tq = _tile(T, ATTN_TILE, 8)

    def body(q_ref, k_ref, v_ref, o_ref, lse_ref, do_ref, dq_ref, dk_ref, dv_ref):
        i = pl.program_id(1)

        @pl.when(i == 0)
        def _():
            dk_ref[...] = jnp.zeros_like(dk_ref)
            dv_ref[...] = jnp.zeros_like(dv_ref)

        qv = q_ref[...]
        do_t = do_ref[...]
        lse_v = lse_ref[:, 0:1]
        delta = jnp.sum(do_t.astype(F32) * o_ref[...], axis=-1, keepdims=True)

        def block(kb, dq, diagonal):
            start = pl.multiple_of(kb * tq, tq)
            k_blk = k_ref[pl.ds(start, tq), :]
            v_blk = v_ref[pl.ds(start, tq), :]
            p = jnp.exp(_attn_scores(qv, k_blk, diagonal) - lse_v)
            dp = lax.dot_general(do_t, v_blk, (((1,), (1,)), ((), ())), preferred_element_type=F32)
            ds = (p * (dp - delta) * ATTN_SCALE).astype(BF)
            dk_ref[pl.ds(start, tq), :] += lax.dot_general(ds, qv, (((0,), (0,)), ((), ())), preferred_element_type=F32)
            dv_ref[pl.ds(start, tq), :] += lax.dot_general(p.astype(BF), do_t, (((0,), (0,)), ((), ())),
                                                          preferred_element_type=F32)
            return dq + jnp.dot(ds, k_blk, preferred_element_type=F32)

        dq = lax.fori_loop(0, i, lambda kb, c: block(kb, c, False), jnp.zeros((tq, HEAD_PAD), F32))
        dq_ref[...] = block(i, dq, True)

    return _pcall(
        body, name=name, grid=(HEADS, T // tq),
        ins=[(q, (tq, HEAD_PAD), lambda h, i: (i, h)), (k, (T, HEAD_PAD), lambda h, i: (0, h)),
             (v, (T, V_HEAD), lambda h, i: (0, h)), (o, (tq, V_HEAD), lambda h, i: (i, h)),
             (lse, (tq, V_HEAD), lambda h, i: (i, h)), (do, (tq, V_HEAD), lambda h, i: (i, h))],
        outs=[((T, HEADS * HEAD_PAD), F32, (tq, HEAD_PAD), lambda h, i: (i, h)),
              ((T, HEADS * HEAD_PAD), F32, (T, HEAD_PAD), lambda h, i: (0, h)),
              ((T, MLA_OUT), F32, (T, V_HEAD), lambda h, i: (0, h))],
        semantics=("parallel", "arbitrary"))


def mla_bwd_prep(dq, dk, dv, cos_t, sin_t, dproj, *, name):
    T = dq.shape[0]
    tm = _tile(T, ROW_TILE, 8)
    kr_block = (dproj.shape[1] - LANES) // LANES

    def body(dq_ref, dk_ref, dv_ref, cos_ref, sin_ref, dproj_in, dql_ref, dkvl_ref, dkr_ref):
        cos_v, sin_v = cos_ref[...], sin_ref[...]
        kr = jnp.zeros((tm, LANES), F32)
        for h in range(HEADS):
            lo = h * HEAD_PAD
            dql_ref[:, lo:lo + QK_NOPE] = dq_ref[:, lo:lo + QK_NOPE].astype(BF)
            dql_ref[:, lo + QK_NOPE:lo + HEAD_PAD] = _rope_bwd(
                dq_ref[:, lo + QK_NOPE:lo + HEAD_PAD], cos_v, sin_v).astype(BF)
            dkvl_ref[:, lo:lo + QK_NOPE] = dk_ref[:, lo:lo + QK_NOPE].astype(BF)
            dkvl_ref[:, lo + QK_NOPE:lo + HEAD_PAD] = dv_ref[:, h * V_HEAD:(h + 1) * V_HEAD].astype(BF)
            kr = kr + dk_ref[:, lo + QK_NOPE:lo + HEAD_PAD]
        dkr_ref[...] = _rope_bwd(kr, cos_v, sin_v).astype(BF)

    W = HEADS * HEAD_PAD
    return _pcall(
        body, name=name, grid=(T // tm,),
        ins=[(dq, (tm, W), lambda i: (i, 0)), (dk, (tm, W), lambda i: (i, 0)), (dv, (tm, MLA_OUT), lambda i: (i, 0)),
             (cos_t, (tm, LANES), lambda i: (i, 0)), (sin_t, (tm, LANES), lambda i: (i, 0)), (dproj, None, None)],
        outs=[((T, W), BF, (tm, W), lambda i: (i, 0)), ((T, W), BF, (tm, W), lambda i: (i, 0)),
              (dproj.shape, BF, (tm, LANES), lambda i: (i, kr_block))],
        aliases={5: 2}, semantics=("parallel",))


def _group_norm_stats(vg):
    mu = jnp.mean(vg, axis=-1, keepdims=True)
    d = vg - mu
    r = lax.rsqrt(jnp.mean(d * d, axis=-1, keepdims=True) + EPS)
    return d * r, r


def mix_fwd(a, proj, g_mla, g_sgu, v_gain, w_tril, b_full, *, name):
    T = a.shape[0]
    tm = _tile(T, ROW_TILE, CHUNK)
    n_chunk = tm // CHUNK

    def body(a_ref, u_ref, v_ref, gm_ref, gs_ref, vg_ref, w_ref, b_ref, o_ref, s_scr):
        av = a_ref[...]
        o_ref[:, :MLA_OUT] = (av * _rstd(av) * gm_ref[...]).astype(BF)
        for g in range(GROUPS):
            sl = slice(g * CH, (g + 1) * CH)
            vhat, _ = _group_norm_stats(_gelu(v_ref[:, sl]))
            vn = (vhat * vg_ref[:, sl]).astype(BF)
            u = _gelu(u_ref[:, sl])
            for ci in range(n_chunk):
                rs = slice(ci * CHUNK, (ci + 1) * CHUNK)
                y = jnp.dot(w_ref[g], vn[rs], preferred_element_type=F32) + b_ref[:, sl]
                s_scr[rs, sl] = u[rs] * y
        s = s_scr[...]
        o_ref[:, MLA_OUT:] = (s * _rstd(s) * gs_ref[...]).astype(BF)

    return _pcall(
        body, name=name, grid=(T // tm,),
        ins=[(a, (tm, MLA_OUT), lambda i: (i, 0)), (proj, (tm, SGU_OUT), lambda i: (i, 1)),
             (proj, (tm, SGU_OUT), lambda i: (i, 2)), (g_mla, (1, MLA_OUT), lambda i: (0, 0)),
             (g_sgu, (1, SGU_OUT), lambda i: (0, 0)), (v_gain, (1, SGU_OUT), lambda i: (0, 0)),
             (w_tril, (GROUPS, CHUNK, CHUNK), lambda i: (0, 0, 0)), (b_full, (CHUNK, SGU_OUT), lambda i: (0, 0))],
        outs=[((T, MLA_OUT + SGU_OUT), BF, (tm, MLA_OUT + SGU_OUT), lambda i: (i, 0))],
        scratch=[pltpu.VMEM((tm, SGU_OUT), F32)], semantics=("parallel",))[0]


def mix_bwd(dmixed, a, proj, g_mla, g_sgu, v_gain, w_tril, w_tril_t, b_full, *, name, deps=()):
    T = a.shape[0]
    tm = _tile(T, ROW_TILE, CHUNK)
    n_chunk = tm // CHUNK
    uv0 = Q_LORA + KV_LORA

    def body(dm_a_ref, dm_s_ref, a_ref, u_ref, v_ref, gm_ref, gs_ref, vg_ref, w_ref, wt_ref, b_ref,
             da_ref, duv_ref, dgm_ref, dgs_ref, dvg_ref, dw_ref, db_ref, s_scr, y_scr):
        first = pl.program_id(0) == 0
        duv_ref[:, :uv0] = jnp.zeros((tm, uv0), BF)
        duv_ref[:, uv0 + 2 * SGU_OUT:] = jnp.zeros((tm, duv_ref.shape[1] - uv0 - 2 * SGU_OUT), BF)
        da, dgm_rows = _rms_bwd(a_ref[...], gm_ref[...], dm_a_ref[...])
        da_ref[...] = da.astype(BF)
        _acc_rows(dgm_ref, dgm_rows, first)

        for g in range(GROUPS):
            sl = slice(g * CH, (g + 1) * CH)
            vhat, _ = _group_norm_stats(_gelu(v_ref[:, sl]))
            vn = (vhat * vg_ref[:, sl]).astype(BF)
            u = _gelu(u_ref[:, sl])
            for ci in range(n_chunk):
                rs = slice(ci * CHUNK, (ci + 1) * CHUNK)
                y = jnp.dot(w_ref[g], vn[rs], preferred_element_type=F32) + b_ref[:, sl]
                y_scr[rs, sl] = y
                s_scr[rs, sl] = u[rs] * y
        ds, dgs_rows = _rms_bwd(s_scr[...], gs_ref[...], dm_s_ref[...])
        _acc_rows(dgs_ref, dgs_rows, first)
        s_scr[...] = ds

        @pl.when(first)
        def _():
            dw_ref[...] = jnp.zeros_like(dw_ref)
            db_ref[...] = jnp.zeros_like(db_ref)

        for g in range(GROUPS):
            sl = slice(g * CH, (g + 1) * CH)
            upre = u_ref[:, sl]
            vpre = v_ref[:, sl]
            u = _gelu(upre)
            vhat, r = _group_norm_stats(_gelu(vpre))
            gain = vg_ref[:, sl]
            vn = (vhat * gain).astype(BF)
            dsg = s_scr[:, sl]
            duv_ref[:, uv0 + g * CH:uv0 + (g + 1) * CH] = (dsg * y_scr[:, sl] * _gelu_grad(upre)).astype(BF)
            dy = dsg * u
            dyb = dy.astype(BF)
            dvn_parts = []
            for ci in range(n_chunk):
                rs = slice(ci * CHUNK, (ci + 1) * CHUNK)
                dvn_parts.append(jnp.dot(wt_ref[g], dyb[rs], preferred_element_type=F32))
                dw_ref[g] += lax.dot_general(dyb[rs], vn[rs], (((1,), (1,)), ((), ())), preferred_element_type=F32)
                db_ref[:, sl] += jnp.broadcast_to(jnp.sum(dy[rs], axis=-1, keepdims=True), (CHUNK, CH))
            dvn = dvn_parts[0] if n_chunk == 1 else jnp.concatenate(dvn_parts, axis=0)
            _acc_rows(dvg_ref.at[:, sl], dvn * vhat, first)
            dvh = dvn * gain
            dvg = r * (dvh - jnp.mean(dvh, axis=-1, keepdims=True)
                       - vhat * jnp.mean(dvh * vhat, axis=-1, keepdims=True))
            duv_ref[:, uv0 + SGU_OUT + g * CH:uv0 + SGU_OUT + (g + 1) * CH] = (dvg * _gelu_grad(vpre)).astype(BF)

    return _pcall(
        body, name=name, grid=(T // tm,),
        ins=[(dmixed, (tm, MLA_OUT), lambda i: (i, 0)), (dmixed, (tm, SGU_OUT), lambda i: (i, 1)),
             (a, (tm, MLA_OUT), lambda i: (i, 0)), (proj, (tm, SGU_OUT), lambda i: (i, 1)),
             (proj, (tm, SGU_OUT), lambda i: (i, 2)), (g_mla, (1, MLA_OUT), lambda i: (0, 0)),
             (g_sgu, (1, SGU_OUT), lambda i: (0, 0)), (v_gain, (1, SGU_OUT), lambda i: (0, 0)),
             (w_tril, (GROUPS, CHUNK, CHUNK), lambda i: (0, 0, 0)), (w_tril_t, (GROUPS, CHUNK, CHUNK), lambda i: (0, 0, 0)),
             (b_full, (CHUNK, SGU_OUT), lambda i: (0, 0))],
        outs=[((T, MLA_OUT), BF, (tm, MLA_OUT), lambda i: (i, 0)),
              ((T, proj.shape[1]), BF, (tm, proj.shape[1]), lambda i: (i, 0)),
              ((1, MLA_OUT), F32, (1, MLA_OUT), lambda i: (0, 0)), ((1, SGU_OUT), F32, (1, SGU_OUT), lambda i: (0, 0)),
              ((1, SGU_OUT), F32, (1, SGU_OUT), lambda i: (0, 0)),
              ((GROUPS, CHUNK, CHUNK), F32, (GROUPS, CHUNK, CHUNK), lambda i: (0, 0, 0)),
              ((CHUNK, SGU_OUT), F32, (CHUNK, SGU_OUT), lambda i: (0, 0))],
        scratch=[pltpu.VMEM((tm, SGU_OUT), F32), pltpu.VMEM((tm, SGU_OUT), F32)], deps=deps)


def _shift_down(z, n, row):
    return jnp.where(row >= n, pltpu.roll(z, n, 0), 0.0)


def _shift_up(z, n, row, T):
    return jnp.where(row < T - n, pltpu.roll(z, T - n, 0), 0.0)


def conv_fwd(proj, conv_w, *, name):
    T, D3 = proj.shape
    D = D3 // 3
    tn = _tile(D, 256)
    nj = D // tn

    def body(b_ref, c_ref, x_ref, w_ref, o_ref):
        row = lax.broadcasted_iota(jnp.int32, (T, tn), 0)
        z = c_ref[...] * x_ref[...]
        zc = w_ref[2:3, :] * z + w_ref[1:2, :] * _shift_down(z, 1, row) + w_ref[0:1, :] * _shift_down(z, 2, row)
        o_ref[...] = (b_ref[...] * zc).astype(BF)

    return _pcall(
        body, name=name, grid=(nj,),
        ins=[(proj, (T, tn), lambda j: (0, j)), (proj, (T, tn), lambda j: (0, nj + j)),
             (proj, (T, tn), lambda j: (0, 2 * nj + j)), (conv_w, (3, tn), lambda j: (0, j))],
        outs=[((T, D), BF, (T, tn), lambda j: (0, j))], semantics=("parallel",))[0]


def conv_bwd(dg, proj, conv_w, *, name, deps=()):
    T, D3 = proj.shape
    D = D3 // 3
    tn = _tile(D, 256)
    nj = D // tn

    def body(dg_ref, b_ref, c_ref, x_ref, w_ref, dp_ref, dw_ref, dc_scr, dx_scr):
        part = pl.program_id(1)

        @pl.when(part == 0)
        def _():
            row = lax.broadcasted_iota(jnp.int32, (T, tn), 0)
            c, x = c_ref[...], x_ref[...]
            z = c * x
            z1 = _shift_down(z, 1, row)
            z2 = _shift_down(z, 2, row)
            dgv = dg_ref[...]
            zc = w_ref[2:3, :] * z + w_ref[1:2, :] * z1 + w_ref[0:1, :] * z2
            dp_ref[...] = (dgv * zc).astype(BF)
            dzc = dgv * b_ref[...]
            dw_ref[0:1, :] = jnp.sum(dzc * z2, axis=0, keepdims=True)
            dw_ref[1:2, :] = jnp.sum(dzc * z1, axis=0, keepdims=True)
            dw_ref[2:3, :] = jnp.sum(dzc * z, axis=0, keepdims=True)
            dz = (w_ref[2:3, :] * dzc + w_ref[1:2, :] * _shift_up(dzc, 1, row, T)
                  + w_ref[0:1, :] * _shift_up(dzc, 2, row, T))
            dc_scr[...] = (dz * x).astype(BF)
            dx_scr[...] = (dz * c).astype(BF)

        @pl.when(part == 1)
        def _():
            dp_ref[...] = dc_scr[...]

        @pl.when(part == 2)
        def _():
            dp_ref[...] = dx_scr[...]

    return _pcall(
        body, name=name, grid=(nj, 3),
        ins=[(dg, (T, tn), lambda j, p: (0, j)), (proj, (T, tn), lambda j, p: (0, j)),
             (proj, (T, tn), lambda j, p: (0, nj + j)), (proj, (T, tn), lambda j, p: (0, 2 * nj + j)),
             (conv_w, (3, tn), lambda j, p: (0, j))],
        outs=[((T, D3), BF, (T, tn), lambda j, p: (0, p * nj + j)), ((3, D), F32, (3, tn), lambda j, p: (0, j))],
        scratch=[pltpu.VMEM((T, tn), BF), pltpu.VMEM((T, tn), BF)], semantics=("parallel", "arbitrary"), deps=deps)


def loss_bwd(x_parts, gain, target, *, name):
    T, D = target.shape
    tm = _tile(T, ROW_TILE, 8)
    n_x = len(x_parts)

    def body(*refs):
        x_refs = refs[:n_x]
        g_ref, t_ref, dx_ref, dxb_ref, dg_ref, loss_ref = refs[n_x:]
        first = pl.program_id(0) == 0
        xv = jnp.concatenate([r[...] for r in x_refs], axis=-1) if n_x > 1 else x_refs[0][...]
        r = _rstd(xv)
        xh = xv * r
        gain_v = g_ref[...]
        err = xh * gain_v - t_ref[...]
        part = 0.5 * jnp.sum(jnp.mean(err * err, axis=-1, keepdims=True), axis=0, keepdims=True)
        _acc_rows(loss_ref, jnp.broadcast_to(part, (1, LANES)), first)
        dy = err * (1.0 / D)
        gdy = dy * gain_v
        dx = r * (gdy - xh * jnp.mean(gdy * xh, axis=-1, keepdims=True))
        dx_ref[...] = dx
        dxb_ref[...] = dx.astype(BF)
        _acc_rows(dg_ref, dy * xh, first)

    return _pcall(
        body, name=name, grid=(T // tm,),
        ins=[(p, (tm, D // n_x), lambda i: (i, 0)) for p in x_parts]
        + [(gain, (1, D), lambda i: (0, 0)), (target, (tm, D), lambda i: (i, 0))],
        outs=[((T, D), F32, (tm, D), lambda i: (i, 0)), ((T, D), BF, (tm, D), lambda i: (i, 0)),
              ((1, D), F32, (1, D), lambda i: (0, 0)), ((1, LANES), F32, (1, LANES), lambda i: (0, 0))])


def _adamw(g, w, m, v):
    m = ADAM_B1 * m + (1.0 - ADAM_B1) * g
    v = ADAM_B2 * v + (1.0 - ADAM_B2) * (g * g)
    m_hat = m / ADAM_C1
    v_hat = v / ADAM_C2
    delta = -ADAM_LR * (m_hat / (jnp.sqrt(v_hat) + ADAM_EPS) + ADAM_WD * w)
    return delta, m, v


def adam_flat(g, w, m, v, *, name):
    def body(g_ref, w_ref, m_ref, v_ref, d_ref, nm_ref, nv_ref):
        d, nm, nv = _adamw(g_ref[...], w_ref[...], m_ref[...], v_ref[...])
        d_ref[...] = d
        nm_ref[...] = nm
        nv_ref[...] = nv

    blk = g.shape
    zero = lambda: (0, 0)
    return _pcall(body, name=name, grid=(),
                  ins=[(t, blk, zero) for t in (g, w, m, v)],
                  outs=[(blk, F32, blk, zero)] * 3)


def _chip_slots():
    x, y, c = lax.axis_index("x"), lax.axis_index("y"), lax.axis_index("c")
    chips = [(1 - x, y), (x, 1 - y), (1 - x, 1 - y)]
    return x, y, c, chips


def device_index():
    x, y, c, chips = _chip_slots()
    return jnp.stack([4 * x + 2 * y + c, 2 * x + y] + [4 * cx + 2 * cy + c for cx, cy in chips]
                     + [2 * cx + cy for cx, cy in chips]).astype(jnp.int32)


def _job_rows(R, C, n_steps):
    if n_steps is None:
        n_steps = max(1, R * C // STREAM_BLOCK_ELEMS)
    n_blk = max([d for d in range(1, n_steps + 1) if R % d == 0 and (R // d) % 16 == 0] or [1])
    return R // n_blk, n_blk


def run_job(job, *, index, name, deps=()):
    jb = job(None)
    n_in = len(jb["ins"])

    def body(idx_ref, *refs):
        jb["fn"](refs[:n_in], refs[n_in:n_in + len(jb["outs"])])

    return _pcall(body, name=name, grid=(jb["n_blk"],), ins=jb["ins"], outs=jb["outs"], prefetch=index,
                  aliases={1 + a: o for a, o in jb["aliases"].items()}, semantics=("parallel",), deps=deps)


def adam_job(gs, a_buf, b_buf, w, m, v, layer, prev):
    L, R, C = w.shape

    def build(n_steps):
        tr, n_blk = _job_rows(R, C, n_steps)
        blk = (None, tr, C)
        row = lambda t: jnp.minimum(t, n_blk - 1)
        ins = [(gs, blk, lambda t, s: (s[0], row(t), 0)), (a_buf, blk, lambda t, s: (s[1], row(t), 0))]
        ins += [(b_buf, blk, lambda t, s, j=j: (j, row(t), 0)) for j in range(3)]
        ins += [(p, blk, lambda t, s: (layer, row(t), 0)) for p in (w, m, v)]
        ins += [(p, None, None) for p in (prev or [])]

        def fn(i, o):
            g = ((((i[0][...].astype(F32) + i[1][...].astype(F32)) + i[2][...].astype(F32))
                  + i[3][...].astype(F32)) + i[4][...].astype(F32))
            d, nm, nv = _adamw(g, i[5][...], i[6][...], i[7][...])
            o[0][...] = g
            o[1][...] = d
            o[2][...] = nm
            o[3][...] = nv

        return dict(ins=ins, outs=[((L, R, C), F32, blk, lambda t, s: (layer, row(t), 0))] * 4, fn=fn,
                    aliases={8 + o: o for o in range(4)} if prev else {}, n_blk=n_blk)

    return build


def pair_job(gs, a_buf):
    _, R, C = gs.shape

    def build(n_steps):
        tr, n_blk = _job_rows(R, C, n_steps)
        blk = (None, tr, C)
        row = lambda t: jnp.minimum(t, n_blk - 1)
        ins = [(gs, blk, lambda t, s, j=j: (s[2 + j], row(t), 0)) for j in range(3)]
        ins += [(a_buf, blk, lambda t, s, j=j: (s[5 + j], row(t), 0)) for j in range(3)]

        def fn(i, o):
            for j in range(3):
                o[0][j] = (i[j][...].astype(F32) + i[3 + j][...].astype(F32)).astype(BF)

        return dict(ins=ins, outs=[((3, R, C), BF, (3, tr, C), lambda t, s: (0, row(t), 0))], fn=fn, aliases={},
                    n_blk=n_blk)

    return build


def reduce_sum(gs, a_buf, b_buf, *, name):
    _, R, C = gs.shape
    tr = _tile(R, 256, 16)
    x, y, c, _ = _chip_slots()
    idx = jnp.stack([4 * x + 2 * y + c, 2 * x + y]).astype(jnp.int32)

    def body(idx_ref, g_ref, a_ref, b0_ref, b1_ref, b2_ref, o_ref):
        o_ref[...] = ((((g_ref[...].astype(F32) + a_ref[...].astype(F32)) + b0_ref[...].astype(F32))
                       + b1_ref[...].astype(F32)) + b2_ref[...].astype(F32))

    blk3 = (None, tr, C)
    return _pcall(body, name=name, grid=(R // tr,),
                  ins=[(gs, blk3, lambda i, s: (s[0], i, 0)), (a_buf, blk3, lambda i, s: (s[1], i, 0)),
                       (b_buf, blk3, lambda i, s: (0, i, 0)), (b_buf, blk3, lambda i, s: (1, i, 0)),
                       (b_buf, blk3, lambda i, s: (2, i, 0))],
                  outs=[((R, C), F32, (tr, C), lambda i, s: (i, 0))], prefetch=idx, semantics=("parallel",))[0]


def adam_rows(g, w, m, v, *, name):
    R, C = g.shape
    tr = _tile(R, 256, 8)

    def body(g_ref, w_ref, m_ref, v_ref, d_ref, nm_ref, nv_ref):
        d, nm, nv = _adamw(g_ref[...], w_ref[...], m_ref[...], v_ref[...])
        d_ref[...] = d
        nm_ref[...] = nm
        nv_ref[...] = nv

    spec = ((tr, C), lambda i: (i, 0))
    return _pcall(body, name=name, grid=(R // tr,), ins=[(t, *spec) for t in (g, w, m, v)],
                  outs=[((R, C), F32, *spec)] * 3, semantics=("parallel",))


def sum_rows8(gathered, rows, *, name):
    W = gathered.shape[1]

    def body(g_ref, o_ref):
        acc = g_ref[0:rows, :]
        for d in range(1, N_DEV):
            acc = acc + g_ref[d * rows:(d + 1) * rows, :]
        o_ref[...] = acc

    return _pcall(body, name=name, grid=(), ins=[(gathered, gathered.shape, lambda: (0, 0))],
                  outs=[((rows, W), F32, (rows, W), lambda: (0, 0))])[0]


HBM_SPEC = pl.BlockSpec(memory_space=pltpu.HBM)
SEM_SPEC = pl.BlockSpec(memory_space=pltpu.SEMAPHORE)
ANY_SPEC = pl.BlockSpec(memory_space=pl.ANY)
DATAFLOW = pltpu.SideEffectType.DATAFLOW_SIDE_EFFECTING


def _in_hbm(v):
    return pltpu.with_memory_space_constraint(v, pltpu.HBM)


def _slot(p):
    return 4 * p[0] + 2 * p[1] + p[2]


def _gather_peers():
    x, y, c, chips = _chip_slots()
    return (x, y, c), [(x, y, 1 - c)] + [(*chip, c) for chip in chips]


def gather_start(groups, after, *, name):
    flat = [s for g in groups for s in g]
    n, n_g = len(flat), len(groups)
    where = [(gi, ti) for gi, g in enumerate(groups) for ti in range(len(g))]

    def body(*refs):
        src, land = refs[:n], refs[n:2 * n]
        sems = refs[2 * n + 1:2 * n + 1 + 2 * n_g]
        me, peers = _gather_peers()
        for t in range(n):
            gi, ti = where[t]
            for k, to in enumerate(peers):
                pltpu.make_async_remote_copy(
                    src_ref=src[t], dst_ref=land[t].at[_slot(me)], send_sem=sems[2 * gi].at[4 * ti + k],
                    recv_sem=sems[2 * gi + 1].at[4 * ti + k], device_id=to, device_id_type=MESH).start()
        refs[-1][...] = jnp.zeros_like(refs[-1])

    out_shape = []
    for g in groups:
        out_shape += [pltpu.SemaphoreType.DMA((4 * len(g),)), pltpu.SemaphoreType.DMA((4 * len(g),))]
    out_shape += [pltpu.HBM(s.shape, s.dtype) for s in flat]
    out_shape += [pltpu.HBM((N_DEV,) + s.shape, s.dtype) for s in flat]
    out_shape += [jax.ShapeDtypeStruct((8, LANES), F32)]
    aliases = {t: 2 * n_g + t for t in range(n)}
    aliases.update({n + t: 2 * n_g + n + t for t in range(n)})
    res = pl.pallas_call(
        body, name=name, out_shape=out_shape, in_specs=[HBM_SPEC] * (2 * n) + [ANY_SPEC],
        out_specs=[SEM_SPEC] * (2 * n_g) + [HBM_SPEC] * (2 * n) + [pl.BlockSpec(memory_space=pltpu.VMEM)],
        input_output_aliases=aliases, compiler_params=pltpu.CompilerParams(has_side_effects=DATAFLOW),
    )(*[_in_hbm(s) for s in flat], *[_in_hbm(lax.empty((N_DEV,) + s.shape, s.dtype)) for s in flat], after)
    out, off = [], 0
    for gi, g in enumerate(groups):
        k = len(g)
        out.append((res[2 * gi], res[2 * gi + 1], res[2 * n_g + off:2 * n_g + off + k],
                    res[2 * n_g + n + off:2 * n_g + n + off + k]))
        off += k
    return out, res[-1]


def gather_wait(started, after, *, name):
    send_sems, recv_sems, srcs, lands = started
    n = len(srcs)
    after = list(after)

    def body(*refs):
        src, land = refs[:n], refs[n:2 * n]
        send, recv = refs[2 * n], refs[2 * n + 1]
        _, peers = _gather_peers()
        for t in range(n):
            for k, frm in enumerate(peers):
                cp = pltpu.make_async_remote_copy(
                    src_ref=src[t], dst_ref=land[t].at[_slot(frm)], send_sem=send.at[4 * t + k],
                    recv_sem=recv.at[4 * t + k],
                    device_id=frm, device_id_type=MESH)
                cp.wait_send()
                cp.wait_recv()

    res = pl.pallas_call(
        body, name=name,
        out_shape=[pltpu.HBM(s.shape, s.dtype) for s in srcs] + [pltpu.HBM(l.shape, l.dtype) for l in lands],
        in_specs=[HBM_SPEC] * (2 * n) + [SEM_SPEC, SEM_SPEC] + [ANY_SPEC] * len(after),
        out_specs=[HBM_SPEC] * (2 * n), input_output_aliases={t: t for t in range(2 * n)},
        compiler_params=pltpu.CompilerParams(has_side_effects=DATAFLOW),
    )(*srcs, *lands, send_sems, recv_sems, *after)
    return res[:n], res[n:]


def place_own(src, land, *, name):
    R, C = src.shape
    tr = _tile(R, 512, 16)
    x, y, c, _ = _chip_slots()
    idx = jnp.stack([4 * x + 2 * y + c]).astype(jnp.int32)

    def body(idx_ref, s_ref, land_ref, o_ref):
        o_ref[...] = s_ref[...]

    return _pcall(body, name=name, grid=(R // tr,),
                  ins=[(src, (tr, C), lambda i, s: (i, 0)), (land, None, None)],
                  outs=[(land.shape, land.dtype, (None, tr, C), lambda i, s: (s[0], i, 0))],
                  prefetch=idx, aliases={2: 0}, semantics=("parallel",))[0]


def gather_finish(srcs, lands, *, name):
    n = len(srcs)

    def body(*refs):
        land = refs[n:2 * n]
        send_sems, recv_sems = refs[2 * n:]
        x, y, c, chips = _chip_slots()
        me, sibling = (x, y, c), (x, y, 1 - c)

        def copy(t, j, block, to):
            return pltpu.make_async_remote_copy(
                src_ref=land[t].at[_slot(block)], dst_ref=land[t].at[_slot(block)], send_sem=send_sems.at[t, j],
                recv_sem=recv_sems.at[t, j], device_id=to, device_id_type=MESH)

        sends = [copy(t, j, (*chip, c), sibling) for t in range(n) for j, chip in enumerate(chips)]
        for cp in sends:
            cp.start()
        for t in range(n):
            for j, chip in enumerate(chips):
                copy(t, j, (*chip, 1 - c), me).wait_recv()
        for cp in sends:
            cp.wait_send()

    passed = pl.pallas_call(
        body, name=name, out_shape=[jax.ShapeDtypeStruct(l.shape, l.dtype) for l in lands],
        in_specs=[ANY_SPEC] * n, out_specs=[ANY_SPEC] * n,
        input_output_aliases={t: t for t in range(n)},
        scratch_shapes=[pltpu.SemaphoreType.DMA((n, 3)), pltpu.SemaphoreType.DMA((n, 3))],
    )(*lands)
    return [place_own(s, l, name=f"{name}_own{t}") for t, (s, l) in enumerate(zip(srcs, passed))]


def chips_start(pairs, *, name):
    n = len(pairs)

    def body(*refs):
        src, land = refs[:n], refs[n:2 * n]
        send, recv = refs[2 * n], refs[2 * n + 1]
        token = refs[-1]
        x, y, c, chips = _chip_slots()
        for t in range(n):
            for j, chip in enumerate(chips):
                pltpu.make_async_remote_copy(
                    src_ref=src[t].at[j], dst_ref=land[t].at[j], send_sem=send.at[3 * t + j],
                    recv_sem=recv.at[3 * t + j], device_id=(*chip, c), device_id_type=MESH).start()
        token[...] = jnp.zeros_like(token)

    res = pl.pallas_call(
        body, name=name,
        out_shape=[pltpu.SemaphoreType.DMA((3 * n,)), pltpu.SemaphoreType.DMA((3 * n,))]
        + [pltpu.HBM(p.shape, p.dtype) for p in pairs] * 2 + [jax.ShapeDtypeStruct((8, LANES), F32)],
        in_specs=[HBM_SPEC] * (2 * n),
        out_specs=[SEM_SPEC, SEM_SPEC] + [HBM_SPEC] * (2 * n) + [pl.BlockSpec(memory_space=pltpu.VMEM)],
        input_output_aliases={t: 2 + t for t in range(2 * n)},
        compiler_params=pltpu.CompilerParams(has_side_effects=DATAFLOW),
    )(*[_in_hbm(p) for p in pairs], *[_in_hbm(lax.empty(p.shape, p.dtype)) for p in pairs])
    return res[0], res[1], res[2:2 + n], res[2 + n:2 + 2 * n], res[-1]


def chips_wait(started, after, *, name):
    send_sems, recv_sems, srcs, lands, _ = started
    n = len(srcs)

    def body(*refs):
        src, land = refs[:n], refs[n:2 * n]
        send, recv = refs[2 * n], refs[2 * n + 1]
        x, y, c, chips = _chip_slots()
        for t in range(n):
            for j, chip in enumerate(chips):
                cp = pltpu.make_async_remote_copy(
                    src_ref=src[t].at[j], dst_ref=land[t].at[j], send_sem=send.at[3 * t + j],
                    recv_sem=recv.at[3 * t + j], device_id=(*chip, c), device_id_type=MESH)
                cp.wait_send()
                cp.wait_recv()

    res = pl.pallas_call(
        body, name=name, out_shape=[pltpu.HBM(s.shape, s.dtype) for s in srcs] * 2,
        in_specs=[HBM_SPEC] * (2 * n) + [SEM_SPEC, SEM_SPEC, ANY_SPEC], out_specs=[HBM_SPEC] * (2 * n),
        input_output_aliases={t: t for t in range(2 * n)},
        compiler_params=pltpu.CompilerParams(has_side_effects=DATAFLOW),
    )(*srcs, *lands, send_sems, recv_sems, after)
    return res[n:]


def _sibling_copies(src, land, send, recv, n):
    x, y, c, _ = _chip_slots()
    return [pltpu.make_async_remote_copy(
        src_ref=src[t].at[4 * (q // 2) + 2 * (q % 2) + (1 - c)], dst_ref=land[t].at[q], send_sem=send.at[4 * t + q],
        recv_sem=recv.at[4 * t + q], device_id=(x, y, 1 - c), device_id_type=MESH)
        for t in range(n) for q in range(4)]


def sibling_start(gs, *, name):
    n = len(gs)

    def body(*refs):
        for cp in _sibling_copies(refs[:n], refs[n:2 * n], refs[2 * n], refs[2 * n + 1], n):
            cp.start()
        refs[-1][...] = jnp.zeros_like(refs[-1])

    lands = [lax.empty((4,) + g.shape[1:], g.dtype) for g in gs]
    res = pl.pallas_call(
        body, name=name,
        out_shape=[pltpu.SemaphoreType.DMA((4 * n,)), pltpu.SemaphoreType.DMA((4 * n,))]
        + [pltpu.HBM(g.shape, g.dtype) for g in gs] + [pltpu.HBM(l.shape, l.dtype) for l in lands]
        + [jax.ShapeDtypeStruct((8, LANES), F32)],
        in_specs=[HBM_SPEC] * (2 * n),
        out_specs=[SEM_SPEC, SEM_SPEC] + [HBM_SPEC] * (2 * n) + [pl.BlockSpec(memory_space=pltpu.VMEM)],
        input_output_aliases={t: 2 + t for t in range(2 * n)},
        compiler_params=pltpu.CompilerParams(has_side_effects=DATAFLOW),
    )(*[_in_hbm(g) for g in gs], *[_in_hbm(l) for l in lands])
    return res[0], res[1], res[2:2 + n], res[2 + n:2 + 2 * n], res[-1]


def sibling_wait(started, after, *, name):
    send_sems, recv_sems, srcs, lands, _ = started
    n = len(srcs)

    def body(*refs):
        for cp in _sibling_copies(refs[:n], refs[n:2 * n], refs[2 * n], refs[2 * n + 1], n):
            cp.wait_send()
            cp.wait_recv()

    res = pl.pallas_call(
        body, name=name,
        out_shape=[pltpu.HBM(s.shape, s.dtype) for s in srcs] + [pltpu.HBM(l.shape, l.dtype) for l in lands],
        in_specs=[HBM_SPEC] * (2 * n) + [SEM_SPEC, SEM_SPEC, ANY_SPEC], out_specs=[HBM_SPEC] * (2 * n),
        input_output_aliases={t: t for t in range(2 * n)},
        compiler_params=pltpu.CompilerParams(has_side_effects=DATAFLOW),
    )(*srcs, *lands, send_sems, recv_sems, after)
    return res[:n], res[n:]


def _rope_slab(cols):
    z = jnp.zeros(cols.shape[:-1] + (HALF_ROPE,), cols.dtype)
    return jnp.concatenate([cols[..., :HALF_ROPE], z, cols[..., HALF_ROPE:], z], axis=-1)


def _rope_unslab(slab):
    return jnp.concatenate([slab[..., :HALF_ROPE], slab[..., 2 * HALF_ROPE:3 * HALF_ROPE]], axis=-1)


def _pack_w_in_t(wt_g):
    s, c, d = wt_g.shape
    w = wt_g.reshape(s * c, d)
    c2, c3 = Q_LORA + KV_LORA, Q_LORA + KV_LORA + QK_ROPE
    r = w[c2:c3]
    z = jnp.zeros((HALF_ROPE, d), w.dtype)
    return jnp.concatenate([w[:c2], w[c3:], r[:HALF_ROPE], z, r[HALF_ROPE:], z], axis=0)


def unpack_w_in_t_grad(dwt, *, name):
    n_rows, d = dwt.shape
    kr = QK_ROPE
    n_out_rows = n_rows - kr
    blk = n_out_rows // 7
    assert blk * 7 == n_out_rows and blk % kr == 0 and n_rows % (2 * kr) == 0
    kr_row = Q_LORA + KV_LORA
    k_mix = kr_row // blk
    off = kr_row - k_mix * blk
    slab_block = (n_rows - 2 * kr) // (2 * kr)

    def body(prev_ref, in_ref, slab_ref, o_ref):
        k = pl.program_id(0)

        @pl.when(k < k_mix)
        def _():
            o_ref[...] = in_ref[...]

        @pl.when(k == k_mix)
        def _():
            o_ref[:off, :] = in_ref[:off, :]
            o_ref[off:off + HALF_ROPE, :] = slab_ref[:HALF_ROPE, :]
            o_ref[off + HALF_ROPE:off + kr, :] = slab_ref[2 * HALF_ROPE:3 * HALF_ROPE, :]
            o_ref[off + kr:, :] = in_ref[off:blk - kr, :]

        @pl.when(k > k_mix)
        def _():
            o_ref[:kr, :] = prev_ref[blk - kr:, :]
            o_ref[kr:, :] = in_ref[:blk - kr, :]

    out = _pcall(body, name=name, grid=(7,),
                 ins=[(dwt, (blk, d), lambda k: (jnp.maximum(k - 1, 0), 0)), (dwt, (blk, d), lambda k: (k, 0)),
                      (dwt, (2 * kr, d), lambda k: (slab_block, 0))],
                 outs=[((n_out_rows, d), dwt.dtype, (blk, d), lambda k: (k, 0))], semantics=("parallel",))[0]
    return out.reshape(N_DEV, n_out_rows // N_DEV, d)


def _rope_tables(positions):
    inv_freq = ROPE_BASE ** (-jnp.arange(0, QK_ROPE, 2, dtype=F32) / QK_ROPE)
    ang = positions.astype(F32)[:, None] * inv_freq
    cos, sin = jnp.cos(ang), jnp.sin(ang)
    z = jnp.zeros_like(cos)
    return jnp.concatenate([cos, z, cos, z], axis=-1), jnp.concatenate([-sin, z, sin, z], axis=-1)


def _mlp_up(x, gain, w1, tag):
    hn = rms_fwd(x, gain, name=f"mlp{tag}_norm")

    def act_epi(acc):
        a = jnp.maximum(acc, 0.0)
        return a, a * a

    T = x.shape[0]
    F = w1.shape[0] * w1.shape[2]
    a, act = mm(hn, w1, name=f"mlp{tag}_up", outs=[((T, F), BF, None), ((T, F), BF, None)], epi=act_epi)
    return hn, a, act


def _mlp_down(x, act, w2, tag, part=0):
    n = w2.shape[1]
    bm = _tile(x.shape[0], MM_TILE)
    bn = _tile(n, MM_TILE)
    per = n // bn
    return mm(act, w2, name=f"mlp{tag}_down{part}", out=((x.shape[0], n), F32), bm=bm, bn=bn,
              epi=lambda acc, r: (acc + r[...],), epi_ins=[(x, (bm, bn), lambda i, j, k: (i, part * per + j))])


def _mlp_bwd_weights(w1, w2, saved, dxb, tag):
    hn, a, act = saved
    T, D = dxb.shape
    F = a.shape[1]
    bm = _tile(T, MM_TILE)
    bn = _tile(F, min(MM_TILE, w1.shape[2]))
    dhid = mm(dxb, w2, tb=True, name=f"mlp{tag}_dhid", out=((T, F), BF), bm=bm, bn=bn,
              epi=lambda acc, a_ref: (2.0 * a_ref[...].astype(F32) * acc,),
              epi_ins=[(a, (bm, bn), lambda i, j, k: (i, j))])
    dw2 = mm(act, dxb, ta=True, name=f"mlp{tag}_dw2", out=((F, D), BF))
    dw1 = mm(hn, dhid, ta=True, name=f"mlp{tag}_dw1", out=(w1.shape, BF))
    return dhid, dw1, dw2.reshape(N_DEV, F // N_DEV, D)


def _reduce_begin(grads, tag):
    return sibling_start(grads, name=f"reduce_sibling_start_{tag}")


def _reduce_continue(sib, after, tag, index):
    grads, a_bufs = sibling_wait(sib, after, name=f"reduce_sibling_wait_{tag}")
    pairs = [run_job(pair_job(g, a), index=index, name=f"pair_sum_{tag}{t}")[0]
             for t, (g, a) in enumerate(zip(grads, a_bufs))]
    return grads, a_bufs, chips_start(pairs, name=f"reduce_chips_start_{tag}")


def kernel(x, positions, e_norm_mix, e_w_in, e_q_norm, e_w_uq, e_kv_norm, e_w_ukv, e_v_norm, e_sgu_w, e_sgu_b, e_mla_out_norm, e_sgu_out_norm, e_w_out, o_norm_mix, o_w_in, o_conv_w, o_w_out, mlp_norm, mlp_w1, mlp_w2, final_norm, loss_target, m_e_norm_mix, m_e_w_in, m_e_q_norm, m_e_w_uq, m_e_kv_norm, m_e_w_ukv, m_e_v_norm, m_e_sgu_w, m_e_sgu_b, m_e_mla_out_norm, m_e_sgu_out_norm, m_e_w_out, m_o_norm_mix, m_o_w_in, m_o_conv_w, m_o_w_out, m_mlp_norm, m_mlp_w1, m_mlp_w2, m_final_norm, v_e_norm_mix, v_e_w_in, v_e_q_norm, v_e_w_uq, v_e_kv_norm, v_e_w_ukv, v_e_v_norm, v_e_sgu_w, v_e_sgu_b, v_e_mla_out_norm, v_e_sgu_out_norm, v_e_w_out, v_o_norm_mix, v_o_w_in, v_o_conv_w, v_o_w_out, v_mlp_norm, v_mlp_w1, v_mlp_w2, v_final_norm):
    T, D = x.shape[1], x.shape[2]
    d_shard = o_norm_mix.shape[1]
    x0 = x[0]
    target = loss_target[0]
    me = 4 * lax.axis_index("x") + 2 * lax.axis_index("y") + lax.axis_index("c")

    bf = lambda s: s.astype(BF)
    gather_groups = [[bf(jnp.transpose(e_w_in[0])), bf(e_w_uq[0]), bf(e_w_ukv[0])], [bf(e_w_out[0]), bf(mlp_w1[0])],
                     [bf(mlp_w2[0]), bf(o_w_in[0])], [bf(o_w_out[0]), bf(mlp_w1[1])], [bf(mlp_w2[1])]]
    small_rows = jnp.concatenate([o_norm_mix, o_conv_w[0], jnp.zeros((4, d_shard), F32)], axis=0)
    gather_groups[0].insert(0, small_rows)
    started, start_token = gather_start(gather_groups[:1], x0, name="gather_start0")
    started_rest, rest_token = gather_start(gather_groups[1:], start_token, name="gather_start1")
    started += started_rest

    def gathered(gi, after):
        srcs, lands = gather_wait(started[gi], after, name=f"gather_wait{gi}")
        return gather_finish(srcs, lands, name=f"gather_finish{gi}")

    w_tril = jnp.tril(e_sgu_w[0])
    w_tril_b = w_tril.astype(BF)
    w_tril_tb = jnp.swapaxes(w_tril, 1, 2).astype(BF)
    b_full = jnp.repeat(e_sgu_b[0].T, CH, axis=1)
    v_gain = e_v_norm[0].reshape(1, SGU_OUT)
    cos_t, sin_t = _rope_tables(positions[0])
    mlp_gain = [mlp_norm[0:1], mlp_norm[1:2]]
    final_gain = final_norm.reshape(1, D)

    h0 = rms_fwd(x0, e_norm_mix, name="e_norm", deps=[rest_token])
    small_g, g_w_in_t, g_w_uq, w_ukv = gathered(0, [h0, cos_t, sin_t, w_tril_b, w_tril_tb, b_full])
    o_norm_full = small_g[:, 0, :].reshape(1, D)
    conv_w_full = jnp.transpose(small_g[:, 1:4, :], (1, 0, 2)).reshape(3, D)
    w_in_t = _pack_w_in_t(g_w_in_t)
    w_uq = jnp.concatenate([g_w_uq[..., :QK_NOPE], _rope_slab(g_w_uq[..., QK_NOPE:])], axis=-1)
    proj = mm(h0, w_in_t, tb=True, name="e_in", out=((T, w_in_t.shape[0]), F32), bn=_tile(w_in_t.shape[0], 640))
    qn, kvn, krope = mla_prep(proj, e_q_norm, e_kv_norm, cos_t, sin_t, name="mla_prep")
    bm = _tile(T, MM_TILE)

    def q_epi(acc, cos_ref, sin_ref):
        return (jnp.concatenate([acc[:, :QK_NOPE], _rope_fwd(acc[:, QK_NOPE:], cos_ref[...], sin_ref[...])], axis=-1),)

    q = mm(qn, w_uq, name="mla_q", out=((T, HEADS * HEAD_PAD), BF), bm=bm, bn=HEAD_PAD, epi=q_epi,
           epi_ins=[(cos_t, (bm, LANES), lambda i, j, k: (i, 0)), (sin_t, (bm, LANES), lambda i, j, k: (i, 0))])

    def kv_epi(acc, kr_ref):
        return jnp.concatenate([acc[:, :QK_NOPE].astype(BF), kr_ref[...]], axis=-1), acc[:, QK_NOPE:]

    k, v = mm(kvn, w_ukv, name="mla_kv", bm=bm, bn=HEAD_PAD, epi=kv_epi,
              outs=[((T, HEADS * HEAD_PAD), BF, HEAD_PAD), ((T, MLA_OUT), BF, V_HEAD)],
              epi_ins=[(krope, (bm, LANES), lambda i, j, k: (i, 0))])
    attn, attn_lse = attn_fwd(q, k, v, name="attn_fwd")
    mixed = mix_fwd(attn, proj, e_mla_out_norm, e_sgu_out_norm, v_gain, w_tril_b, b_full, name="mix_fwd")
    bn = _tile(D, MM_TILE)
    g_w_out_e, w1_0 = gathered(1, [mixed])
    w_out_e = g_w_out_e.reshape(-1, D)
    x1 = mm(mixed, w_out_e, name="e_out", out=((T, D), F32), bm=bm, bn=bn,
            epi=lambda acc, r: (acc + r[...],), epi_ins=[(x0, (bm, bn), lambda i, j, k: (i, j))])
    hn0, a0, act0 = _mlp_up(x1, mlp_gain[0], w1_0, 0)
    g_w2_0, g_w_in_o = gathered(2, [act0])
    w2_0 = g_w2_0.reshape(-1, D)
    x2 = _mlp_down(x1, act0, w2_0, 0)
    ho = rms_fwd(x2, o_norm_full, name="o_norm")
    proj_o = mm(ho, g_w_in_o, name="o_in", out=((T, 3 * D), F32))
    gated = conv_fwd(proj_o, conv_w_full, name="conv_fwd")
    g_w_out_o, w1_1 = gathered(3, [gated])
    w_out_o = g_w_out_o.reshape(-1, D)
    x3 = mm(gated, w_out_o, name="o_out", out=((T, D), F32), bm=bm, bn=bn,
            epi=lambda acc, r: (acc + r[...],), epi_ins=[(x2, (bm, bn), lambda i, j, k: (i, j))])
    hn1, a1, act1 = _mlp_up(x3, mlp_gain[1], w1_1, 1)
    (g_w2_1,) = gathered(4, [act1])
    w2_1 = g_w2_1.reshape(-1, D)
    x4 = _mlp_down(x3, act1, w2_1, 1)
    w1, w2 = [w1_0, w1_1], [w2_0, w2_1]

    dx4, dx4b, d_final, loss_part = loss_bwd([x4], final_gain, target, name="loss_bwd")

    hosted = dict(job_index=device_index())
    dhid1, dw1_1, dw2_1 = _mlp_bwd_weights(w1[1], w2[1], (hn1, a1, act1), dx4b, 1)
    sib_r0 = _reduce_begin([dw1_1, dw2_1], "r0")
    dhn1 = mm(dhid1, w1[1], tb=True, name="mlp1_dhn", out=((T, D), F32), deps=[sib_r0[-1]])
    grads_r0, a_r0 = sibling_wait(sib_r0, dhn1, name="reduce_sibling_wait_r0")
    dx3, dx3b, d_mlp1 = rms_bwd(x3, mlp_gain[1], dhn1, dres=dx4, name="mlp1_norm_bwd")

    dgated, ((pair_r0a,),) = mm(dx3b, w_out_o, tb=True, name="o_out_dx", out=((T, D), F32),
                                jobs=[pair_job(grads_r0[0], a_r0[0])], **hosted)
    dw_out_o, ((pair_r0b,),) = mm(gated, dx3b, ta=True, name="o_out_dw", out=((D, D), BF),
                                  jobs=[pair_job(grads_r0[1], a_r0[1])], **hosted)
    st_r0 = chips_start([pair_r0a, pair_r0b], name="reduce_chips_start_r0")
    dproj_o, dconv_full = conv_bwd(dgated, proj_o, conv_w_full, name="conv_bwd", deps=[st_r0[-1]])
    dw_in_o = mm(ho, dproj_o, ta=True, name="o_in_dw", out=(g_w_in_o.shape, BF))
    sib_r1 = _reduce_begin([dw_out_o.reshape(g_w_out_o.shape), dw_in_o], "r1")
    dho = mm(dproj_o, g_w_in_o, tb=True, name="o_in_dx", out=((T, D), F32), deps=[sib_r1[-1]])
    grads_r1, a_r1 = sibling_wait(sib_r1, dho, name="reduce_sibling_wait_r1")
    dx2, dx2b, d_onorm_full = rms_bwd(x2, o_norm_full, dho, dres=dx3, name="o_norm_bwd")

    d_ff = a0.shape[1]
    bm_h, bn_h = _tile(T, MM_TILE), _tile(d_ff, min(MM_TILE, w1[0].shape[2]))
    dhid0, ((pair_r1a,), (pair_r1b,)) = mm(
        dx2b, w2[0], tb=True, name="mlp0_dhid", out=((T, d_ff), BF), bm=bm_h, bn=bn_h,
        epi=lambda acc, a_ref: (2.0 * a_ref[...].astype(F32) * acc,),
        epi_ins=[(a0, (bm_h, bn_h), lambda i, j, k: (i, j))],
        jobs=[pair_job(grads_r1[0], a_r1[0]), pair_job(grads_r1[1], a_r1[1])], **hosted)
    st_r1 = chips_start([pair_r1a, pair_r1b], name="reduce_chips_start_r1")
    dw2_0 = mm(act0, dx2b, ta=True, name="mlp0_dw2", out=((d_ff, D), BF), deps=[st_r1[-1]])
    b_r0 = chips_wait(st_r0, dw2_0, name="reduce_chips_wait_r0")
    dw1_0, (r_w1, r_w2) = mm(
        hn0, dhid0, ta=True, name="mlp0_dw1", out=(w1[0].shape, BF),
        jobs=[adam_job(grads_r0[0], a_r0[0], b_r0[0], mlp_w1, m_mlp_w1, v_mlp_w1, 1, None),
              adam_job(grads_r0[1], a_r0[1], b_r0[1], mlp_w2, m_mlp_w2, v_mlp_w2, 1, None)], **hosted)
    sib_r2 = _reduce_begin([dw1_0, dw2_0.reshape(N_DEV, d_ff // N_DEV, D)], "r2")
    dhn0 = mm(dhid0, w1[0], tb=True, name="mlp0_dhn", out=((T, D), F32), deps=[sib_r2[-1]])
    grads_r2, a_r2 = sibling_wait(sib_r2, dhn0, name="reduce_sibling_wait_r2")
    dx1, dx1b, d_mlp0 = rms_bwd(x1, mlp_gain[0], dhn0, dres=dx2, name="mlp0_norm_bwd")

    dmixed, ((pair_r2a,),) = mm(dx1b, w_out_e, tb=True, name="e_out_dx", out=((T, MLA_OUT + SGU_OUT), F32),
                                jobs=[pair_job(grads_r2[0], a_r2[0])], **hosted)
    dw_out_e, ((pair_r2b,),) = mm(mixed, dx1b, ta=True, name="e_out_dw", out=(w_out_e.shape, BF),
                                  jobs=[pair_job(grads_r2[1], a_r2[1])], **hosted)
    st_r2 = chips_start([pair_r2a, pair_r2b], name="reduce_chips_start_r2")
    (dattn, dproj, d_mla_out, d_sgu_out, d_vgain, d_sgu_w, d_b_full) = mix_bwd(
        dmixed, attn, proj, e_mla_out_norm, e_sgu_out_norm, v_gain, w_tril_b, w_tril_tb, b_full, name="mix_bwd",
        deps=[st_r2[-1]])
    b_r1 = chips_wait(st_r1, dattn, name="reduce_chips_wait_r1")
    dq, dk, dv = attn_bwd(q, k, v, attn, attn_lse, dattn, name="attn_bwd")
    dq_lin, dkv_lin, dproj = mla_bwd_prep(dq, dk, dv, cos_t, sin_t, dproj, name="mla_bwd_prep")
    dw_uq_pad = mm(qn, dq_lin, ta=True, name="mla_q_dw", out=(w_uq.shape, BF))
    dw_ukv = mm(kvn, dkv_lin, ta=True, name="mla_kv_dw", out=(w_ukv.shape, BF))
    dw_uq = jnp.concatenate([dw_uq_pad[..., :QK_NOPE], _rope_unslab(dw_uq_pad[..., QK_NOPE:])], axis=-1)
    sib_r2b = _reduce_begin([dw_out_e.reshape(g_w_out_e.shape), dw_uq, dw_ukv], "r2b")
    dqn = mm(dq_lin, w_uq, tb=True, name="mla_q_dx", out=((T, Q_LORA), F32), deps=[sib_r2b[-1]])
    dkvn = mm(dkv_lin, w_ukv, tb=True, name="mla_kv_dx", out=((T, KV_LORA), F32), deps=[sib_r2b[-1]])
    grads_r2b, a_r2b, st_r2b = _reduce_continue(sib_r2b, dkvn, "r2b", hosted["job_index"])
    dproj, d_qnorm = rms_bwd(proj, e_q_norm, dqn, col_block=0, want_f32=False, into=dproj, name="q_norm_bwd",
                             deps=[st_r2b[-1]])
    dproj, d_kvnorm = rms_bwd(proj, e_kv_norm, dkvn, col_block=1, want_f32=False, into=dproj, name="kv_norm_bwd")
    dw_in_t_pad, (r_w_out_o, r_w_in_o) = mm(
        dproj, h0, ta=True, name="e_in_dw", out=(w_in_t.shape, BF), bm=_tile(w_in_t.shape[0], 640),
        jobs=[adam_job(grads_r1[0], a_r1[0], b_r1[0], o_w_out, m_o_w_out, v_o_w_out, 0, None),
              adam_job(grads_r1[1], a_r1[1], b_r1[1], o_w_in, m_o_w_in, v_o_w_in, 0, None)], **hosted)
    dw_in_t = unpack_w_in_t_grad(dw_in_t_pad, name="e_in_dw_unpack")
    sib_r3 = _reduce_begin([dw_in_t], "r3")
    dh0 = mm(dproj, w_in_t, name="e_in_dx", out=((T, D), F32), deps=[sib_r3[-1]])
    grads_r3, a_r3, st_r3 = _reduce_continue(sib_r3, dh0, "r3", hosted["job_index"])
    tok_r3 = st_r3[-1]
    grad_x, d_enorm = rms_bwd(x0, e_norm_mix, dh0, dres=dx1, want_bf=False, name="e_norm_bwd", deps=[tok_r3])
    b_r2 = chips_wait(st_r2, grad_x, name="reduce_chips_wait_r2")

    d_sgu_b = jnp.transpose(d_b_full[:, ::CH])
    d_sgu_w_tril = jnp.tril(d_sgu_w)
    rep = [("e_norm_mix", e_norm_mix, m_e_norm_mix, v_e_norm_mix, d_enorm),
           ("e_q_norm", e_q_norm, m_e_q_norm, v_e_q_norm, d_qnorm),
           ("e_kv_norm", e_kv_norm, m_e_kv_norm, v_e_kv_norm, d_kvnorm),
           ("e_v_norm", e_v_norm, m_e_v_norm, v_e_v_norm, d_vgain),
           ("e_sgu_w", e_sgu_w, m_e_sgu_w, v_e_sgu_w, d_sgu_w_tril),
           ("e_sgu_b", e_sgu_b, m_e_sgu_b, v_e_sgu_b, d_sgu_b),
           ("e_mla_out_norm", e_mla_out_norm, m_e_mla_out_norm, v_e_mla_out_norm, d_mla_out),
           ("e_sgu_out_norm", e_sgu_out_norm, m_e_sgu_out_norm, v_e_sgu_out_norm, d_sgu_out),
           ("mlp_norm", mlp_norm, m_mlp_norm, v_mlp_norm, jnp.concatenate([d_mlp0, d_mlp1], axis=0)),
           ("final_norm", final_norm, m_final_norm, v_final_norm, d_final)]
    sizes = [int(np.prod(r[1].shape)) for r in rep]
    n_rep = sum(sizes)
    n_all = n_rep + 4 * D + 1
    width = -(-n_all // (8 * LANES)) * LANES
    pad = 8 * width - n_all
    flat = jnp.concatenate([r[4].reshape(-1) for r in rep]
                           + [d_onorm_full.reshape(-1), dconv_full.reshape(-1), loss_part[0, :1],
                              jnp.zeros((pad,), F32)])
    small_started, small_token = gather_start([[flat.reshape(8, width)]], b_r2[0], name="gather_small_grads_start")

    def finish(grads, a_bufs, b_bufs, t, w, m, v, layer=0, prev=None, tag="", deps=()):
        return run_job(adam_job(grads[t], a_bufs[t], b_bufs[t], w, m, v, layer, prev), index=hosted["job_index"],
                       name=f"adam_{tag}", deps=deps)

    r_w1 = finish(grads_r2, a_r2, b_r2, 0, mlp_w1, m_mlp_w1, v_mlp_w1, 0, r_w1, tag="w1_l0", deps=[tok_r3, small_token])
    r_w2 = finish(grads_r2, a_r2, b_r2, 1, mlp_w2, m_mlp_w2, v_mlp_w2, 0, r_w2, tag="w2_l0", deps=[r_w1[1]])
    b_r2b = chips_wait(st_r2b, r_w2[1], name="reduce_chips_wait_r2b")
    r_w_out_e = finish(grads_r2b, a_r2b, b_r2b, 0, e_w_out, m_e_w_out, v_e_w_out, tag="e_w_out")
    r_w_uq = finish(grads_r2b, a_r2b, b_r2b, 1, e_w_uq, m_e_w_uq, v_e_w_uq, tag="e_w_uq")
    r_w_ukv = finish(grads_r2b, a_r2b, b_r2b, 2, e_w_ukv, m_e_w_ukv, v_e_w_ukv, tag="e_w_ukv")
    b_r3 = chips_wait(st_r3, r_w_out_e[1], name="reduce_chips_wait_r3")
    g_w_in_t = reduce_sum(grads_r3[0], a_r3[0], b_r3[0], name="sum_e_w_in")
    w_in_upd_t = adam_rows(g_w_in_t, jnp.transpose(e_w_in[0]), jnp.transpose(m_e_w_in[0]), jnp.transpose(v_e_w_in[0]),
                           name="adam_e_w_in")
    r_w_in = [jnp.transpose(t)[None] for t in (g_w_in_t, *w_in_upd_t)]

    small_srcs, small_lands = gather_wait(small_started[0], [r_w2[1]], name="gather_small_grads_wait")
    small_all = gather_finish(small_srcs, small_lands, name="gather_small_grads_finish")[0]
    summed = sum_rows8(small_all.reshape(N_DEV * 8, width), 8, name="sum_small_grads").reshape(-1)

    loss = summed[n_rep + 4 * D]

    def pack_rep(i):
        return jnp.concatenate([r[i].reshape(-1) for r in rep]).reshape(n_rep // LANES, LANES)

    g_rep = summed[:n_rep].reshape(n_rep // LANES, LANES)
    d_rep, nm_rep, nv_rep = adam_flat(g_rep, pack_rep(1), pack_rep(2), pack_rep(3), name="adam_replicated")

    def unpack_rep(flat2d):
        out, off = {}, 0
        f = flat2d.reshape(-1)
        for r, n in zip(rep, sizes):
            out[r[0]] = f[off:off + n].reshape(r[1].shape)
            off += n
        return out

    small = {"grad": unpack_rep(g_rep), "delta": unpack_rep(d_rep), "new_m": unpack_rep(nm_rep),
             "new_v": unpack_rep(nv_rep)}
    g_onorm = lax.dynamic_slice(summed[n_rep:n_rep + D].reshape(1, D), (0, me * d_shard), (1, d_shard))
    g_conv = lax.dynamic_slice(summed[n_rep + D:n_rep + 4 * D].reshape(3, D), (0, me * d_shard), (3, d_shard))

    def pack_sharded(norm_part, conv_part):
        return jnp.concatenate([norm_part, conv_part, jnp.zeros((4, d_shard), F32)], axis=0)

    g_sh = pack_sharded(g_onorm, g_conv)
    d_sh, nm_sh, nv_sh = adam_flat(g_sh, pack_sharded(o_norm_mix, o_conv_w[0]), pack_sharded(m_o_norm_mix, m_o_conv_w[0]),
                                   pack_sharded(v_o_norm_mix, v_o_conv_w[0]), name="adam_sharded_small")
    for kind, arr in (("grad", g_sh), ("delta", d_sh), ("new_m", nm_sh), ("new_v", nv_sh)):
        small[kind]["o_norm_mix"] = arr[0:1]
        small[kind]["o_conv_w"] = arr[1:4][None]

    big = {"e_w_in": r_w_in, "e_w_uq": r_w_uq, "e_w_ukv": r_w_ukv, "e_w_out": r_w_out_e, "o_w_in": r_w_in_o,
           "o_w_out": r_w_out_o, "mlp_w1": r_w1, "mlp_w2": r_w2}
    order = ["e_norm_mix", "e_w_in", "e_q_norm", "e_w_uq", "e_kv_norm", "e_w_ukv", "e_v_norm", "e_sgu_w", "e_sgu_b",
             "e_mla_out_norm", "e_sgu_out_norm", "e_w_out", "o_norm_mix", "o_w_in", "o_conv_w", "o_w_out", "mlp_norm",
             "mlp_w1", "mlp_w2", "final_norm"]
    result = [loss, grad_x[None]]
    for ki, kind in enumerate(("grad", "delta", "new_m", "new_v")):
        for nm in order:
            result.append(big[nm][ki] if nm in big else small[kind][nm])
    return tuple(result)
```

```python
import numpy as np
import jax
import jax.numpy as jnp
from jax import lax
from jax.experimental import pallas as pl
from jax.experimental.pallas import tpu as pltpu

BF = jnp.bfloat16
F32 = jnp.float32
MESH = pl.DeviceIdType.MESH
N_DEV = 8

EPS = 1e-6
HEADS = 8
Q_LORA = 512
KV_LORA = 512
QK_NOPE = 128
QK_ROPE = 64
HALF_ROPE = QK_ROPE // 2
V_HEAD = 128
HEAD_PAD = 256
ROPE_BASE = 10000.0
GROUPS = 8
CH = 128
CHUNK = 128
SGU_OUT = GROUPS * CH
MLA_OUT = HEADS * V_HEAD
ATTN_SCALE = float((QK_NOPE + QK_ROPE) ** -0.5)

ADAM_LR = 0.001
ADAM_B1 = 0.9
ADAM_B2 = 0.999
ADAM_EPS = 1e-08
ADAM_WD = 0.01
ADAM_STEP = 10
ADAM_C1 = 1.0 - ADAM_B1 ** ADAM_STEP
ADAM_C2 = 1.0 - ADAM_B2 ** ADAM_STEP

V7X_VMEM_BYTES = 64 * 2 ** 20
VMEM_LIMIT_CAP = V7X_VMEM_BYTES - 6 * 2 ** 20
LANES = 128
ROW_TILE = 256
ATTN_TILE = 1024
STREAM_BLOCK_ELEMS = 256 * 1024
MM_TILE = 1024
MM_K_TILE = 2048
MM_K_BLOCK_MAX = 4096


def _padded_bytes(block, dtype):
    dims = [d for d in block if d is not None]
    if len(dims) >= 1:
        dims[-1] = -(-dims[-1] // LANES) * LANES
    if len(dims) >= 2:
        dims[-2] = -(-dims[-2] // 16) * 16
    return int(np.prod(dims)) * jnp.dtype(dtype).itemsize


def _pcall(body, *, name, grid, ins, outs, scratch=(), semantics=None, aliases=None, prefetch=None, deps=()):
    any_spec = pl.BlockSpec(memory_space=pl.ANY)
    if deps:
        n_lead = len(ins) + (1 if prefetch is not None else 0)
        n_deps = len(deps)
        inner = body

        def body(*refs):
            inner(*refs[:n_lead], *refs[n_lead + n_deps:])

        ins = list(ins) + [(d, None, None) for d in deps]
    in_specs = [any_spec if b is None else pl.BlockSpec(b, m) for _, b, m in ins]
    out_specs = [any_spec if b is None else pl.BlockSpec(b, m) for _, _, b, m in outs]
    out_shape = [pltpu.HBM(s, d) for s, d, _, _ in outs]
    est = 0
    for a, b, _ in ins:
        if b is not None:
            est += 2 * _padded_bytes(b, a.dtype)
    for _, d, b, _ in outs:
        if b is not None:
            est += 2 * _padded_bytes(b, d)
    for s in scratch:
        if hasattr(s, "shape") and hasattr(s, "dtype"):
            est += _padded_bytes(s.shape, s.dtype)
    limit = int(min(VMEM_LIMIT_CAP, est + 16 * 2 ** 20))
    params = pltpu.CompilerParams(
        dimension_semantics=semantics or ("arbitrary",) * len(grid), vmem_limit_bytes=limit)
    args = [pltpu.with_memory_space_constraint(a, pltpu.HBM) for a, _, _ in ins]
    if prefetch is not None:
        grid_spec = pltpu.PrefetchScalarGridSpec(
            num_scalar_prefetch=1, grid=grid, in_specs=in_specs, out_specs=out_specs, scratch_shapes=list(scratch))
        call = pl.pallas_call(body, out_shape=out_shape, grid_spec=grid_spec, name=name, compiler_params=params,
                              input_output_aliases=aliases or {})
        return call(prefetch, *args)
    call = pl.pallas_call(body, out_shape=out_shape, grid=grid, in_specs=in_specs, out_specs=out_specs,
                          scratch_shapes=list(scratch), name=name, compiler_params=params,
                          input_output_aliases=aliases or {})
    return call(*args)


def _tile(dim, pref, quantum=LANES):
    if dim <= pref:
        return dim
    t = (pref // quantum) * quantum
    while t >= quantum:
        if dim % t == 0:
            return t
        t -= quantum
    return dim


def _vshape(arr_shape):
    if len(arr_shape) == 2:
        return tuple(arr_shape)
    s, r, c = arr_shape
    return (r, s * c)


def _vblock(arr_shape, br, bc, rc):
    if len(arr_shape) == 2:
        return (br, bc), (lambda *g: rc(*g))
    _, _, c = arr_shape
    assert c % bc == 0, (arr_shape, bc)
    per = c // bc

    def imap(*g):
        ri, ci = rc(*g)
        return (ci // per, ri, ci % per)

    return (None, br, bc), imap


def _shard_width(*shapes):
    w = None
    for s in shapes:
        if len(s) == 3:
            w = s[2] if w is None else int(np.gcd(w, s[2]))
    return w


def mm(a, b, *, name, ta=False, tb=False, out=None, outs=None, epi=None, epi_ins=(), bm=None, bn=None, bk=None,
       deps=(), jobs=(), job_index=None):
    av, bv = _vshape(a.shape), _vshape(b.shape)
    M, K = (av[1], av[0]) if ta else av
    K2, N = (bv[1], bv[0]) if tb else bv
    assert K == K2, (a.shape, b.shape, ta, tb)
    if outs is None:
        outs = [(out[0], out[1], None)]
    a_sw = _shard_width(a.shape)
    b_sw = _shard_width(b.shape)
    o_sw = _shard_width(*[o[0] for o in outs])
    m_lim = a_sw if (ta and a_sw) else None
    k_lim = [w for w in ((a_sw if not ta else None), (b_sw if tb else None)) if w]
    n_lim = [w for w in ((b_sw if not tb else None), o_sw) if w]
    if bm is None:
        bm = _tile(M, min([MM_TILE] + ([m_lim] if m_lim else [])))
    if bn is None:
        bn = _tile(N, min([MM_TILE] + n_lim))
    k_shards = 0
    if tb and len(b.shape) == 3 and bk is None and not (a_sw and not ta):
        k_shards = 1
        while 2 * k_shards <= b.shape[0] and 2 * k_shards * b_sw <= MM_K_BLOCK_MAX:
            k_shards *= 2
        bk = k_shards * b_sw
    if bk is None:
        bk = K if (K <= 4096 and not k_lim) else _tile(K, min([MM_K_TILE] + k_lim))
    assert M % bm == 0 and N % bn == 0 and K % bk == 0, (name, M, N, K, bm, bn, bk)
    nk = K // bk
    grid = (M // bm, N // bn, nk)
    if ta:
        a_blk, a_map = _vblock(a.shape, bk, bm, lambda i, j, k: (k, i))
    else:
        a_blk, a_map = _vblock(a.shape, bm, bk, lambda i, j, k: (i, k))
    if k_shards:
        b_blk, b_map = (k_shards, bn, b_sw), (lambda i, j, k: (k, j, 0))
    elif tb:
        b_blk, b_map = _vblock(b.shape, bn, bk, lambda i, j, k: (j, k))
    else:
        b_blk, b_map = _vblock(b.shape, bk, bn, lambda i, j, k: (k, j))
    dn = (((0 if ta else 1,), (1 if tb else 0,)), ((), ()))
    ins = [(a, a_blk, a_map), (b, b_blk, b_map)] + list(epi_ins)
    out_list = []
    for shape, dtype, cols in outs:
        cols = cols or bn
        blk, imap = _vblock(shape, bm, cols, lambda i, j, k: (i, j))
        out_list.append((shape, dtype, blk, imap))
    n_e, n_o = len(epi_ins), len(out_list)

    n_steps = grid[0] * grid[1] * nk
    built = [job(n_steps) for job in jobs]
    aliases = {}
    job_slices = []
    if built:
        def lin(i, j, k):
            return (i * grid[1] + j) * nk + k

        ins = [(arr, blk, None if blk is None else (lambda i, j, k, s, f=f: f(i, j, k))) for arr, blk, f in ins]
        out_list = [(sh, dt, blk, (lambda i, j, k, s, f=f: f(i, j, k))) for sh, dt, blk, f in out_list]
        n_main_in, n_main_out = len(ins), len(out_list)
        for jb in built:
            i0, o0 = len(ins), len(out_list)
            ins += [(arr, blk, None if blk is None else (lambda i, j, k, s, f=f: f(lin(i, j, k), s)))
                    for arr, blk, f in jb["ins"]]
            out_list += [(sh, dt, blk, (lambda i, j, k, s, f=f: f(lin(i, j, k), s))) for sh, dt, blk, f in jb["outs"]]
            aliases.update({1 + i0 + ai: o0 + ao for ai, ao in jb["aliases"].items()})
            job_slices.append((i0, len(jb["ins"]), o0, len(jb["outs"])))
    n_in_total = len(ins)

    def body(*refs):
        if built:
            refs = refs[1:]
        a_ref, b_ref = refs[0], refs[1]
        e_refs = refs[2:2 + n_e]
        o_refs = refs[n_in_total:n_in_total + n_o]
        for jb, (i0, ni, o0, no) in zip(built, job_slices):
            jb["fn"](refs[i0:i0 + ni], refs[n_in_total + o0:n_in_total + o0 + no])

        def finish(acc):
            res = epi(acc, *e_refs) if epi is not None else (acc,)
            for o_ref, r in zip(o_refs, res):
                o_ref[...] = r.astype(o_ref.dtype)

        x = a_ref[...].astype(BF)
        y = b_ref[...].astype(BF)
        if k_shards:
            p = None
            for s in range(k_shards):
                part = lax.dot_general(x[:, s * b_sw:(s + 1) * b_sw], y[s], dn, preferred_element_type=F32)
                p = part if p is None else p + part
        else:
            p = lax.dot_general(x, y, dn, preferred_element_type=F32)
        if nk == 1:
            finish(p)
        else:
            acc_ref = refs[-1]
            k = pl.program_id(2)

            @pl.when(k == 0)
            def _():
                acc_ref[...] = p

            @pl.when(k > 0)
            def _():
                acc_ref[...] += p

            @pl.when(k == nk - 1)
            def _():
                finish(acc_ref[...])

    scratch = [pltpu.VMEM((bm, bn), F32)] if nk > 1 else []
    res = _pcall(body, name=name, grid=grid, ins=ins, outs=out_list, scratch=scratch, deps=deps,
                 semantics=("parallel", "parallel", "arbitrary"), prefetch=job_index if built else None, aliases=aliases)
    main = res[0] if n_o == 1 else res[:n_o]
    if not built:
        return main
    return main, [res[o0:o0 + no] for _, _, o0, no in job_slices]


_GELU_K = float(np.sqrt(2.0 / np.pi))
_GELU_C = 0.044715


def _gelu(x):
    t = jnp.tanh(_GELU_K * (x + _GELU_C * (x * x * x)))
    return 0.5 * x * (1.0 + t)


def _gelu_grad(x):
    t = jnp.tanh(_GELU_K * (x + _GELU_C * (x * x * x)))
    return 0.5 * (1.0 + t) + 0.5 * x * (1.0 - t * t) * (_GELU_K * (1.0 + 3.0 * _GELU_C * (x * x)))


def _rstd(x):
    return lax.rsqrt(jnp.mean(x * x, axis=-1, keepdims=True) + EPS)


def _rms_bwd(x, gain, dy):
    r = _rstd(x)
    xh = x * r
    gdy = dy * gain
    dx = r * (gdy - xh * jnp.mean(gdy * xh, axis=-1, keepdims=True))
    return dx, dy * xh


def _rope_fwd(x, cos_t, sin_t):
    return x * cos_t + pltpu.roll(x, 2 * HALF_ROPE, 1) * sin_t


def _rope_bwd(dy, cos_t, sin_t):
    return dy * cos_t + pltpu.roll(dy * sin_t, 2 * HALF_ROPE, 1)


def _acc_rows(ref, val, first):
    s = jnp.sum(val, axis=0, keepdims=True)

    @pl.when(first)
    def _():
        ref[...] = s

    @pl.when(jnp.logical_not(first))
    def _():
        ref[...] += s


def rms_fwd(x, gain, *, name, col_block=0, width=None, deps=()):
    T = x.shape[0]
    width = width or x.shape[1]
    tm = _tile(T, ROW_TILE, 8)

    def body(x_ref, g_ref, o_ref):
        v = x_ref[...]
        o_ref[...] = (v * _rstd(v) * g_ref[...]).astype(BF)

    return _pcall(body, name=name, grid=(T // tm,),
                  ins=[(x, (tm, width), lambda i: (i, col_block)), (gain, (1, width), lambda i: (0, 0))],
                  outs=[((T, width), BF, (tm, width), lambda i: (i, 0))], semantics=("parallel",), deps=deps)[0]


def rms_bwd(x, gain, dy, *, name, col_block=0, dres=None, want_f32=True, want_bf=True, into=None, deps=()):
    T, width = dy.shape
    tm = _tile(T, ROW_TILE, 8)
    has_res = dres is not None

    def body(*refs):
        x_ref, g_ref, dy_ref = refs[:3]
        pos = 3
        res_ref = None
        if has_res:
            res_ref = refs[pos]
            pos += 1
        if into is not None:
            pos += 1
        outs = refs[pos:]
        dx, dg_rows = _rms_bwd(x_ref[...], g_ref[...], dy_ref[...])
        if has_res:
            dx = dx + res_ref[...]
        o = 0
        if want_f32:
            outs[o][...] = dx
            o += 1
        if want_bf:
            outs[o][...] = dx.astype(BF)
            o += 1
        _acc_rows(outs[o], dg_rows, pl.program_id(0) == 0)

    ins = [(x, (tm, width), lambda i: (i, col_block)), (gain, (1, width), lambda i: (0, 0)),
           (dy, (tm, width), lambda i: (i, 0))]
    if has_res:
        ins.append((dres, (tm, width), lambda i: (i, 0)))
    outs = []
    aliases = {}
    if want_f32:
        outs.append(((T, width), F32, (tm, width), lambda i: (i, 0)))
    if want_bf and into is not None:
        ins.append((into, None, None))
        aliases[len(ins) - 1] = len(outs)
        outs.append((into.shape, BF, (tm, width), lambda i: (i, col_block)))
    elif want_bf:
        outs.append(((T, width), BF, (tm, width), lambda i: (i, 0)))
    outs.append(((1, width), F32, (1, width), lambda i: (0, 0)))
    return _pcall(body, name=name, grid=(T // tm,), ins=ins, outs=outs, aliases=aliases, deps=deps)


def mla_prep(proj, q_norm, kv_norm, cos_t, sin_t, *, name):
    T = proj.shape[0]
    tm = _tile(T, ROW_TILE, 8)
    kr_block = (proj.shape[1] - LANES) // LANES

    def body(cq_ref, ckv_ref, kr_ref, qg_ref, kg_ref, cos_ref, sin_ref, qn_ref, kvn_ref, krope_ref):
        cq = cq_ref[...]
        qn_ref[...] = (cq * _rstd(cq) * qg_ref[...]).astype(BF)
        ckv = ckv_ref[...]
        kvn_ref[...] = (ckv * _rstd(ckv) * kg_ref[...]).astype(BF)
        krope_ref[...] = _rope_fwd(kr_ref[...], cos_ref[...], sin_ref[...]).astype(BF)

    return _pcall(
        body, name=name, grid=(T // tm,),
        ins=[(proj, (tm, Q_LORA), lambda i: (i, 0)), (proj, (tm, KV_LORA), lambda i: (i, 1)),
             (proj, (tm, LANES), lambda i: (i, kr_block)),
             (q_norm, (1, Q_LORA), lambda i: (0, 0)), (kv_norm, (1, KV_LORA), lambda i: (0, 0)),
             (cos_t, (tm, LANES), lambda i: (i, 0)), (sin_t, (tm, LANES), lambda i: (i, 0))],
        outs=[((T, Q_LORA), BF, (tm, Q_LORA), lambda i: (i, 0)), ((T, KV_LORA), BF, (tm, KV_LORA), lambda i: (i, 0)),
              ((T, LANES), BF, (tm, LANES), lambda i: (i, 0))],
        semantics=("parallel",))


def _attn_scores(q, k_blk, diagonal):
    s = lax.dot_general(q, k_blk, (((1,), (1,)), ((), ())), preferred_element_type=F32) * ATTN_SCALE
    if diagonal:
        row = lax.broadcasted_iota(jnp.int32, s.shape, 0)
        col = lax.broadcasted_iota(jnp.int32, s.shape, 1)
        s = jnp.where(col <= row, s, -jnp.inf)
    return s


def attn_fwd(q, k, v, *, name):
    T = q.shape[0]
    tq = _tile(T, ATTN_TILE, 8)

    def body(q_ref, k_ref, v_ref, o_ref, lse_ref):
        i = pl.program_id(1)
        qv = q_ref[...]

        def block(kb, carry, diagonal):
            m, l, acc = carry
            start = pl.multiple_of(kb * tq, tq)
            s = _attn_scores(qv, k_ref[pl.ds(start, tq), :], diagonal)
            m_new = jnp.maximum(m, jnp.max(s, axis=-1, keepdims=True))
            alpha = jnp.exp(m - m_new)
            p = jnp.exp(s - m_new)
            l = alpha * l + jnp.sum(p, axis=-1, keepdims=True)
            acc = alpha * acc + jnp.dot(p.astype(BF), v_ref[pl.ds(start, tq), :], preferred_element_type=F32)
            return m_new, l, acc

        init = (jnp.full((tq, 1), -jnp.inf, F32), jnp.zeros((tq, 1), F32), jnp.zeros((tq, V_HEAD), F32))
        carry = lax.fori_loop(0, i, lambda kb, c: block(kb, c, False), init)
        m, l, acc = block(i, carry, True)
        o_ref[...] = acc / l
        lse_ref[...] = jnp.broadcast_to(m + jnp.log(l), (tq, V_HEAD))

    return _pcall(
        body, name=name, grid=(HEADS, T // tq),
        ins=[(q, (tq, HEAD_PAD), lambda h, i: (i, h)), (k, (T, HEAD_PAD), lambda h, i: (0, h)),
             (v, (T, V_HEAD), lambda h, i: (0, h))],
        outs=[((T, MLA_OUT), F32, (tq, V_HEAD), lambda h, i: (i, h)),
              ((T, MLA_OUT), F32, (tq, V_HEAD), lambda h, i: (i, h))], semantics=("parallel", "parallel"))


def attn_bwd(q, k, v, o, lse, do, *, name):
    T = q.shape[0]
    tq = _tile(T, ATTN_TILE, 8)

    def body(q_ref, k_ref, v_ref, o_ref, lse_ref, do_ref, dq_ref, dk_ref, dv_ref):
        i = pl.program_id(1)

        @pl.when(i == 0)
        def _():
            dk_ref[...] = jnp.zeros_like(dk_ref)
            dv_ref[...] = jnp.zeros_like(dv_ref)

        qv = q_ref[...]
        do_t = do_ref[...]
        lse_v = lse_ref[:, 0:1]
        delta = jnp.sum(do_t.astype(F32) * o_ref[...], axis=-1, keepdims=True)

        def block(kb, dq, diagonal):
            start = pl.multiple_of(kb * tq, tq)
            k_blk = k_ref[pl.ds(start, tq), :]
            v_blk = v_ref[pl.ds(start, tq), :]
            p = jnp.exp(_attn_scores(qv, k_blk, diagonal) - lse_v)
            dp = lax.dot_general(do_t, v_blk, (((1,), (1,)), ((), ())), preferred_element_type=F32)
            ds = (p * (dp - delta) * ATTN_SCALE).astype(BF)
            dk_ref[pl.ds(start, tq), :] += lax.dot_general(ds, qv, (((0,), (0,)), ((), ())), preferred_element_type=F32)
            dv_ref[pl.ds(start, tq), :] += lax.dot_general(p.astype(BF), do_t, (((0,), (0,)), ((), ())),
                                                          preferred_element_type=F32)
            return dq + jnp.dot(ds, k_blk, preferred_element_type=F32)

        dq = lax.fori_loop(0, i, lambda kb, c: block(kb, c, False), jnp.zeros((tq, HEAD_PAD), F32))
        dq_ref[...] = block(i, dq, True)

    return _pcall(
        body, name=name, grid=(HEADS, T // tq),
        ins=[(q, (tq, HEAD_PAD), lambda h, i: (i, h)), (k, (T, HEAD_PAD), lambda h, i: (0, h)),
             (v, (T, V_HEAD), lambda h, i: (0, h)), (o, (tq, V_HEAD), lambda h, i: (i, h)),
             (lse, (tq, V_HEAD), lambda h, i: (i, h)), (do, (tq, V_HEAD), lambda h, i: (i, h))],
        outs=[((T, HEADS * HEAD_PAD), F32, (tq, HEAD_PAD), lambda h, i: (i, h)),
              ((T, HEADS * HEAD_PAD), F32, (T, HEAD_PAD), lambda h, i: (0, h)),
              ((T, MLA_OUT), F32, (T, V_HEAD), lambda h, i: (0, h))],
        semantics=("parallel", "arbitrary"))


def mla_bwd_prep(dq, dk, dv, cos_t, sin_t, dproj, *, name):
    T = dq.shape[0]
    tm = _tile(T, ROW_TILE, 8)
    kr_block = (dproj.shape[1] - LANES) // LANES

    def body(dq_ref, dk_ref, dv_ref, cos_ref, sin_ref, dproj_in, dql_ref, dkvl_ref, dkr_ref):
        cos_v, sin_v = cos_ref[...], sin_ref[...]
        kr = jnp.zeros((tm, LANES), F32)
        for h in range(HEADS):
            lo = h * HEAD_PAD
            dql_ref[:, lo:lo + QK_NOPE] = dq_ref[:, lo:lo + QK_NOPE].astype(BF)
            dql_ref[:, lo + QK_NOPE:lo + HEAD_PAD] = _rope_bwd(
                dq_ref[:, lo + QK_NOPE:lo + HEAD_PAD], cos_v, sin_v).astype(BF)
            dkvl_ref[:, lo:lo + QK_NOPE] = dk_ref[:, lo:lo + QK_NOPE].astype(BF)
            dkvl_ref[:, lo + QK_NOPE:lo + HEAD_PAD] = dv_ref[:, h * V_HEAD:(h + 1) * V_HEAD].astype(BF)
            kr = kr + dk_ref[:, lo + QK_NOPE:lo + HEAD_PAD]
        dkr_ref[...] = _rope_bwd(kr, cos_v, sin_v).astype(BF)

    W = HEADS * HEAD_PAD
    return _pcall(
        body, name=name, grid=(T // tm,),
        ins=[(dq, (tm, W), lambda i: (i, 0)), (dk, (tm, W), lambda i: (i, 0)), (dv, (tm, MLA_OUT), lambda i: (i, 0)),
             (cos_t, (tm, LANES), lambda i: (i, 0)), (sin_t, (tm, LANES), lambda i: (i, 0)), (dproj, None, None)],
        outs=[((T, W), BF, (tm, W), lambda i: (i, 0)), ((T, W), BF, (tm, W), lambda i: (i, 0)),
              (dproj.shape, BF, (tm, LANES), lambda i: (i, kr_block))],
        aliases={5: 2}, semantics=("parallel",))


def _group_norm_stats(vg):
    mu = jnp.mean(vg, axis=-1, keepdims=True)
    d = vg - mu
    r = lax.rsqrt(jnp.mean(d * d, axis=-1, keepdims=True) + EPS)
    return d * r, r


def mix_fwd(a, proj, g_mla, g_sgu, v_gain, w_tril, b_full, *, name):
    T = a.shape[0]
    tm = _tile(T, ROW_TILE, CHUNK)
    n_chunk = tm // CHUNK

    def body(a_ref, u_ref, v_ref, gm_ref, gs_ref, vg_ref, w_ref, b_ref, o_ref, s_scr):
        av = a_ref[...]
        o_ref[:, :MLA_OUT] = (av * _rstd(av) * gm_ref[...]).astype(BF)
        for g in range(GROUPS):
            sl = slice(g * CH, (g + 1) * CH)
            vhat, _ = _group_norm_stats(_gelu(v_ref[:, sl]))
            vn = (vhat * vg_ref[:, sl]).astype(BF)
            u = _gelu(u_ref[:, sl])
            for ci in range(n_chunk):
                rs = slice(ci * CHUNK, (ci + 1) * CHUNK)
                y = jnp.dot(w_ref[g], vn[rs], preferred_element_type=F32) + b_ref[:, sl]
                s_scr[rs, sl] = u[rs] * y
        s = s_scr[...]
        o_ref[:, MLA_OUT:] = (s * _rstd(s) * gs_ref[...]).astype(BF)

    return _pcall(
        body, name=name, grid=(T // tm,),
        ins=[(a, (tm, MLA_OUT), lambda i: (i, 0)), (proj, (tm, SGU_OUT), lambda i: (i, 1)),
             (proj, (tm, SGU_OUT), lambda i: (i, 2)), (g_mla, (1, MLA_OUT), lambda i: (0, 0)),
             (g_sgu, (1, SGU_OUT), lambda i: (0, 0)), (v_gain, (1, SGU_OUT), lambda i: (0, 0)),
             (w_tril, (GROUPS, CHUNK, CHUNK), lambda i: (0, 0, 0)), (b_full, (CHUNK, SGU_OUT), lambda i: (0, 0))],
        outs=[((T, MLA_OUT + SGU_OUT), BF, (tm, MLA_OUT + SGU_OUT), lambda i: (i, 0))],
        scratch=[pltpu.VMEM((tm, SGU_OUT), F32)], semantics=("parallel",))[0]


def mix_bwd(dmixed, a, proj, g_mla, g_sgu, v_gain, w_tril, w_tril_t, b_full, *, name, deps=()):
    T = a.shape[0]
    tm = _tile(T, ROW_TILE, CHUNK)
    n_chunk = tm // CHUNK
    uv0 = Q_LORA + KV_LORA

    def body(dm_a_ref, dm_s_ref, a_ref, u_ref, v_ref, gm_ref, gs_ref, vg_ref, w_ref, wt_ref, b_ref,
             da_ref, duv_ref, dgm_ref, dgs_ref, dvg_ref, dw_ref, db_ref, s_scr, y_scr):
        first = pl.program_id(0) == 0
        duv_ref[:, :uv0] = jnp.zeros((tm, uv0), BF)
        duv_ref[:, uv0 + 2 * SGU_OUT:] = jnp.zeros((tm, duv_ref.shape[1] - uv0 - 2 * SGU_OUT), BF)
        da, dgm_rows = _rms_bwd(a_ref[...], gm_ref[...], dm_a_ref[...])
        da_ref[...] = da.astype(BF)
        _acc_rows(dgm_ref, dgm_rows, first)

        for g in range(GROUPS):
            sl = slice(g * CH, (g + 1) * CH)
            vhat, _ = _group_norm_stats(_gelu(v_ref[:, sl]))
            vn = (vhat * vg_ref[:, sl]).astype(BF)
            u = _gelu(u_ref[:, sl])
            for ci in range(n_chunk):
                rs = slice(ci * CHUNK, (ci + 1) * CHUNK)
                y = jnp.dot(w_ref[g], vn[rs], preferred_element_type=F32) + b_ref[:, sl]
                y_scr[rs, sl] = y
                s_scr[rs, sl] = u[rs] * y
        ds, dgs_rows = _rms_bwd(s_scr[...], gs_ref[...], dm_s_ref[...])
        _acc_rows(dgs_ref, dgs_rows, first)
        s_scr[...] = ds

        @pl.when(first)
        def _():
            dw_ref[...] = jnp.zeros_like(dw_ref)
            db_ref[...] = jnp.zeros_like(db_ref)

        for g in range(GROUPS):
            sl = slice(g * CH, (g + 1) * CH)
            upre = u_ref[:, sl]
            vpre = v_ref[:, sl]
            u = _gelu(upre)
            vhat, r = _group_norm_stats(_gelu(vpre))
            gain = vg_ref[:, sl]
            vn = (vhat * gain).astype(BF)
            dsg = s_scr[:, sl]
            duv_ref[:, uv0 + g * CH:uv0 + (g + 1) * CH] = (dsg * y_scr[:, sl] * _gelu_grad(upre)).astype(BF)
            dy = dsg * u
            dyb = dy.astype(BF)
            dvn_parts = []
            for ci in range(n_chunk):
                rs = slice(ci * CHUNK, (ci + 1) * CHUNK)
                dvn_parts.append(jnp.dot(wt_ref[g], dyb[rs], preferred_element_type=F32))
                dw_ref[g] += lax.dot_general(dyb[rs], vn[rs], (((1,), (1,)), ((), ())), preferred_element_type=F32)
                db_ref[:, sl] += jnp.broadcast_to(jnp.sum(dy[rs], axis=-1, keepdims=True), (CHUNK, CH))
            dvn = dvn_parts[0] if n_chunk == 1 else jnp.concatenate(dvn_parts, axis=0)
            _acc_rows(dvg_ref.at[:, sl], dvn * vhat, first)
            dvh = dvn * gain
            dvg = r * (dvh - jnp.mean(dvh, axis=-1, keepdims=True)
                       - vhat * jnp.mean(dvh * vhat, axis=-1, keepdims=True))
            duv_ref[:, uv0 + SGU_OUT + g * CH:uv0 + SGU_OUT + (g + 1) * CH] = (dvg * _gelu_grad(vpre)).astype(BF)

    return _pcall(
        body, name=name, grid=(T // tm,),
        ins=[(dmixed, (tm, MLA_OUT), lambda i: (i, 0)), (dmixed, (tm, SGU_OUT), lambda i: (i, 1)),
             (a, (tm, MLA_OUT), lambda i: (i, 0)), (proj, (tm, SGU_OUT), lambda i: (i, 1)),
             (proj, (tm, SGU_OUT), lambda i: (i, 2)), (g_mla, (1, MLA_OUT), lambda i: (0, 0)),
             (g_sgu, (1, SGU_OUT), lambda i: (0, 0)), (v_gain, (1, SGU_OUT), lambda i: (0, 0)),
             (w_tril, (GROUPS, CHUNK, CHUNK), lambda i: (0, 0, 0)), (w_tril_t, (GROUPS, CHUNK, CHUNK), lambda i: (0, 0, 0)),
             (b_full, (CHUNK, SGU_OUT), lambda i: (0, 0))],
        outs=[((T, MLA_OUT), BF, (tm, MLA_OUT), lambda i: (i, 0)),
              ((T, proj.shape[1]), BF, (tm, proj.shape[1]), lambda i: (i, 0)),
              ((1, MLA_OUT), F32, (1, MLA_OUT), lambda i: (0, 0)), ((1, SGU_OUT), F32, (1, SGU_OUT), lambda i: (0, 0)),
              ((1, SGU_OUT), F32, (1, SGU_OUT), lambda i: (0, 0)),
              ((GROUPS, CHUNK, CHUNK), F32, (GROUPS, CHUNK, CHUNK), lambda i: (0, 0, 0)),
              ((CHUNK, SGU_OUT), F32, (CHUNK, SGU_OUT), lambda i: (0, 0))],
        scratch=[pltpu.VMEM((tm, SGU_OUT), F32), pltpu.VMEM((tm, SGU_OUT), F32)], deps=deps)


def _shift_down(z, n, row):
    return jnp.where(row >= n, pltpu.roll(z, n, 0), 0.0)


def _shift_up(z, n, row, T):
    return jnp.where(row < T - n, pltpu.roll(z, T - n, 0), 0.0)


def conv_fwd(proj, conv_w, *, name):
    T, D3 = proj.shape
    D = D3 // 3
    tn = _tile(D, 256)
    nj = D // tn

    def body(b_ref, c_ref, x_ref, w_ref, o_ref):
        row = lax.broadcasted_iota(jnp.int32, (T, tn), 0)
        z = c_ref[...] * x_ref[...]
        zc = w_ref[2:3, :] * z + w_ref[1:2, :] * _shift_down(z, 1, row) + w_ref[0:1, :] * _shift_down(z, 2, row)
        o_ref[...] = (b_ref[...] * zc).astype(BF)

    return _pcall(
        body, name=name, grid=(nj,),
        ins=[(proj, (T, tn), lambda j: (0, j)), (proj, (T, tn), lambda j: (0, nj + j)),
             (proj, (T, tn), lambda j: (0, 2 * nj + j)), (conv_w, (3, tn), lambda j: (0, j))],
        outs=[((T, D), BF, (T, tn), lambda j: (0, j))], semantics=("parallel",))[0]


def conv_bwd(dg, proj, conv_w, *, name, deps=()):
    T, D3 = proj.shape
    D = D3 // 3
    tn = _tile(D, 256)
    nj = D // tn

    def body(dg_ref, b_ref, c_ref, x_ref, w_ref, dp_ref, dw_ref, dc_scr, dx_scr):
        part = pl.program_id(1)

        @pl.when(part == 0)
        def _():
            row = lax.broadcasted_iota(jnp.int32, (T, tn), 0)
            c, x = c_ref[...], x_ref[...]
            z = c * x
            z1 = _shift_down(z, 1, row)
            z2 = _shift_down(z, 2, row)
            dgv = dg_ref[...]
            zc = w_ref[2:3, :] * z + w_ref[1:2, :] * z1 + w_ref[0:1, :] * z2
            dp_ref[...] = (dgv * zc).astype(BF)
            dzc = dgv * b_ref[...]
            dw_ref[0:1, :] = jnp.sum(dzc * z2, axis=0, keepdims=True)
            dw_ref[1:2, :] = jnp.sum(dzc * z1, axis=0, keepdims=True)
            dw_ref[2:3, :] = jnp.sum(dzc * z, axis=0, keepdims=True)
            dz = (w_ref[2:3, :] * dzc + w_ref[1:2, :] * _shift_up(dzc, 1, row, T)
                  + w_ref[0:1, :] * _shift_up(dzc, 2, row, T))
            dc_scr[...] = (dz * x).astype(BF)
            dx_scr[...] = (dz * c).astype(BF)

        @pl.when(part == 1)
        def _():
            dp_ref[...] = dc_scr[...]

        @pl.when(part == 2)
        def _():
            dp_ref[...] = dx_scr[...]

    return _pcall(
        body, name=name, grid=(nj, 3),
        ins=[(dg, (T, tn), lambda j, p: (0, j)), (proj, (T, tn), lambda j, p: (0, j)),
             (proj, (T, tn), lambda j, p: (0, nj + j)), (proj, (T, tn), lambda j, p: (0, 2 * nj + j)),
             (conv_w, (3, tn), lambda j, p: (0, j))],
        outs=[((T, D3), BF, (T, tn), lambda j, p: (0, p * nj + j)), ((3, D), F32, (3, tn), lambda j, p: (0, j))],
        scratch=[pltpu.VMEM((T, tn), BF), pltpu.VMEM((T, tn), BF)], semantics=("parallel", "arbitrary"), deps=deps)


def loss_bwd(x_parts, gain, target, *, name):
    T, D = target.shape
    tm = _tile(T, ROW_TILE, 8)
    n_x = len(x_parts)

    def body(*refs):
        x_refs = refs[:n_x]
        g_ref, t_ref, dx_ref, dxb_ref, dg_ref, loss_ref = refs[n_x:]
        first = pl.program_id(0) == 0
        xv = jnp.concatenate([r[...] for r in x_refs], axis=-1) if n_x > 1 else x_refs[0][...]
        r = _rstd(xv)
        xh = xv * r
        gain_v = g_ref[...]
        err = xh * gain_v - t_ref[...]
        part = 0.5 * jnp.sum(jnp.mean(err * err, axis=-1, keepdims=True), axis=0, keepdims=True)
        _acc_rows(loss_ref, jnp.broadcast_to(part, (1, LANES)), first)
        dy = err * (1.0 / D)
        gdy = dy * gain_v
        dx = r * (gdy - xh * jnp.mean(gdy * xh, axis=-1, keepdims=True))
        dx_ref[...] = dx
        dxb_ref[...] = dx.astype(BF)
        _acc_rows(dg_ref, dy * xh, first)

    return _pcall(
        body, name=name, grid=(T // tm,),
        ins=[(p, (tm, D // n_x), lambda i: (i, 0)) for p in x_parts]
        + [(gain, (1, D), lambda i: (0, 0)), (target, (tm, D), lambda i: (i, 0))],
        outs=[((T, D), F32, (tm, D), lambda i: (i, 0)), ((T, D), BF, (tm, D), lambda i: (i, 0)),
              ((1, D), F32, (1, D), lambda i: (0, 0)), ((1, LANES), F32, (1, LANES), lambda i: (0, 0))])


def _adamw(g, w, m, v):
    m = ADAM_B1 * m + (1.0 - ADAM_B1) * g
    v = ADAM_B2 * v + (1.0 - ADAM_B2) * (g * g)
    m_hat = m / ADAM_C1
    v_hat = v / ADAM_C2
    delta = -ADAM_LR * (m_hat / (jnp.sqrt(v_hat) + ADAM_EPS) + ADAM_WD * w)
    return delta, m, v


def adam_flat(g, w, m, v, *, name):
    def body(g_ref, w_ref, m_ref, v_ref, d_ref, nm_ref, nv_ref):
        d, nm, nv = _adamw(g_ref[...], w_ref[...], m_ref[...], v_ref[...])
        d_ref[...] = d
        nm_ref[...] = nm
        nv_ref[...] = nv

    blk = g.shape
    zero = lambda: (0, 0)
    return _pcall(body, name=name, grid=(),
                  ins=[(t, blk, zero) for t in (g, w, m, v)],
                  outs=[(blk, F32, blk, zero)] * 3)


def _chip_slots():
    x, y, c = lax.axis_index("x"), lax.axis_index("y"), lax.axis_index("c")
    chips = [(1 - x, y), (x, 1 - y), (1 - x, 1 - y)]
    return x, y, c, chips


def device_index():
    x, y, c, chips = _chip_slots()
    return jnp.stack([4 * x + 2 * y + c, 2 * x + y] + [4 * cx + 2 * cy + c for cx, cy in chips]
                     + [2 * cx + cy for cx, cy in chips]).astype(jnp.int32)


def _job_rows(R, C, n_steps):
    if n_steps is None:
        n_steps = max(1, R * C // STREAM_BLOCK_ELEMS)
    n_blk = max([d for d in range(1, n_steps + 1) if R % d == 0 and (R // d) % 16 == 0] or [1])
    return R // n_blk, n_blk


def run_job(job, *, index, name, deps=()):
    jb = job(None)
    n_in = len(jb["ins"])

    def body(idx_ref, *refs):
        jb["fn"](refs[:n_in], refs[n_in:n_in + len(jb["outs"])])

    return _pcall(body, name=name, grid=(jb["n_blk"],), ins=jb["ins"], outs=jb["outs"], prefetch=index,
                  aliases={1 + a: o for a, o in jb["aliases"].items()}, semantics=("parallel",), deps=deps)


def adam_job(gs, a_buf, b_buf, w, m, v, layer, prev):
    L, R, C = w.shape

    def build(n_steps):
        tr, n_blk = _job_rows(R, C, n_steps)
        blk = (None, tr, C)
        row = lambda t: jnp.minimum(t, n_blk - 1)
        ins = [(gs, blk, lambda t, s: (s[0], row(t), 0)), (a_buf, blk, lambda t, s: (s[1], row(t), 0))]
        ins += [(b_buf, blk, lambda t, s, j=j: (j, row(t), 0)) for j in range(3)]
        ins += [(p, blk, lambda t, s: (layer, row(t), 0)) for p in (w, m, v)]
        ins += [(p, None, None) for p in (prev or [])]

        def fn(i, o):
            g = ((((i[0][...].astype(F32) + i[1][...].astype(F32)) + i[2][...].astype(F32))
                  + i[3][...].astype(F32)) + i[4][...].astype(F32))
            d, nm, nv = _adamw(g, i[5][...], i[6][...], i[7][...])
            o[0][...] = g
            o[1][...] = d
            o[2][...] = nm
            o[3][...] = nv

        return dict(ins=ins, outs=[((L, R, C), F32, blk, lambda t, s: (layer, row(t), 0))] * 4, fn=fn,
                    aliases={8 + o: o for o in range(4)} if prev else {}, n_blk=n_blk)

    return build


def pair_job(gs, a_buf):
    _, R, C = gs.shape

    def build(n_steps):
        tr, n_blk = _job_rows(R, C, n_steps)
        blk = (None, tr, C)
        row = lambda t: jnp.minimum(t, n_blk - 1)
        ins = [(gs, blk, lambda t, s, j=j: (s[2 + j], row(t), 0)) for j in range(3)]
        ins += [(a_buf, blk, lambda t, s, j=j: (s[5 + j], row(t), 0)) for j in range(3)]

        def fn(i, o):
            for j in range(3):
                o[0][j] = (i[j][...].astype(F32) + i[3 + j][...].astype(F32)).astype(BF)

        return dict(ins=ins, outs=[((3, R, C), BF, (3, tr, C), lambda t, s: (0, row(t), 0))], fn=fn, aliases={},
                    n_blk=n_blk)

    return build


def reduce_sum(gs, a_buf, b_buf, *, name):
    _, R, C = gs.shape
    tr = _tile(R, 256, 16)
    x, y, c, _ = _chip_slots()
    idx = jnp.stack([4 * x + 2 * y + c, 2 * x + y]).astype(jnp.int32)

    def body(idx_ref, g_ref, a_ref, b0_ref, b1_ref, b2_ref, o_ref):
        o_ref[...] = ((((g_ref[...].astype(F32) + a_ref[...].astype(F32)) + b0_ref[...].astype(F32))
                       + b1_ref[...].astype(F32)) + b2_ref[...].astype(F32))

    blk3 = (None, tr, C)
    return _pcall(body, name=name, grid=(R // tr,),
                  ins=[(gs, blk3, lambda i, s: (s[0], i, 0)), (a_buf, blk3, lambda i, s: (s[1], i, 0)),
                       (b_buf, blk3, lambda i, s: (0, i, 0)), (b_buf, blk3, lambda i, s: (1, i, 0)),
                       (b_buf, blk3, lambda i, s: (2, i, 0))],
                  outs=[((R, C), F32, (tr, C), lambda i, s: (i, 0))], prefetch=idx, semantics=("parallel",))[0]


def adam_rows(g, w, m, v, *, name):
    R, C = g.shape
    tr = _tile(R, 256, 8)

    def body(g_ref, w_ref, m_ref, v_ref, d_ref, nm_ref, nv_ref):
        d, nm, nv = _adamw(g_ref[...], w_ref[...], m_ref[...], v_ref[...])
        d_ref[...] = d
        nm_ref[...] = nm
        nv_ref[...] = nv

    spec = ((tr, C), lambda i: (i, 0))
    return _pcall(body, name=name, grid=(R // tr,), ins=[(t, *spec) for t in (g, w, m, v)],
                  outs=[((R, C), F32, *spec)] * 3, semantics=("parallel",))


def sum_rows8(gathered, rows, *, name):
    W = gathered.shape[1]

    def body(g_ref, o_ref):
        acc = g_ref[0:rows, :]
        for d in range(1, N_DEV):
            acc = acc + g_ref[d * rows:(d + 1) * rows, :]
        o_ref[...] = acc

    return _pcall(body, name=name, grid=(), ins=[(gathered, gathered.shape, lambda: (0, 0))],
                  outs=[((rows, W), F32, (rows, W), lambda: (0, 0))])[0]


HBM_SPEC = pl.BlockSpec(memory_space=pltpu.HBM)
SEM_SPEC = pl.BlockSpec(memory_space=pltpu.SEMAPHORE)
ANY_SPEC = pl.BlockSpec(memory_space=pl.ANY)
DATAFLOW = pltpu.SideEffectType.DATAFLOW_SIDE_EFFECTING


def _in_hbm(v):
    return pltpu.with_memory_space_constraint(v, pltpu.HBM)


def _slot(p):
    return 4 * p[0] + 2 * p[1] + p[2]


def _gather_peers():
    x, y, c, chips = _chip_slots()
    return (x, y, c), [(x, y, 1 - c)] + [(*chip, c) for chip in chips]


def gather_start(groups, after, *, name):
    flat = [s for g in groups for s in g]
    n, n_g = len(flat), len(groups)
    where = [(gi, ti) for gi, g in enumerate(groups) for ti in range(len(g))]

    def body(*refs):
        src, land = refs[:n], refs[n:2 * n]
        sems = refs[2 * n + 1:2 * n + 1 + 2 * n_g]
        me, peers = _gather_peers()
        for t in range(n):
            gi, ti = where[t]
            for k, to in enumerate(peers):
                pltpu.make_async_remote_copy(
                    src_ref=src[t], dst_ref=land[t].at[_slot(me)], send_sem=sems[2 * gi].at[4 * ti + k],
                    recv_sem=sems[2 * gi + 1].at[4 * ti + k], device_id=to, device_id_type=MESH).start()
        refs[-1][...] = jnp.zeros_like(refs[-1])

    out_shape = []
    for g in groups:
        out_shape += [pltpu.SemaphoreType.DMA((4 * len(g),)), pltpu.SemaphoreType.DMA((4 * len(g),))]
    out_shape += [pltpu.HBM(s.shape, s.dtype) for s in flat]
    out_shape += [pltpu.HBM((N_DEV,) + s.shape, s.dtype) for s in flat]
    out_shape += [jax.ShapeDtypeStruct((8, LANES), F32)]
    aliases = {t: 2 * n_g + t for t in range(n)}
    aliases.update({n + t: 2 * n_g + n + t for t in range(n)})
    res = pl.pallas_call(
        body, name=name, out_shape=out_shape, in_specs=[HBM_SPEC] * (2 * n) + [ANY_SPEC],
        out_specs=[SEM_SPEC] * (2 * n_g) + [HBM_SPEC] * (2 * n) + [pl.BlockSpec(memory_space=pltpu.VMEM)],
        input_output_aliases=aliases, compiler_params=pltpu.CompilerParams(has_side_effects=DATAFLOW),
    )(*[_in_hbm(s) for s in flat], *[_in_hbm(lax.empty((N_DEV,) + s.shape, s.dtype)) for s in flat], after)
    out, off = [], 0
    for gi, g in enumerate(groups):
        k = len(g)
        out.append((res[2 * gi], res[2 * gi + 1], res[2 * n_g + off:2 * n_g + off + k],
                    res[2 * n_g + n + off:2 * n_g + n + off + k]))
        off += k
    return out, res[-1]


def gather_wait(started, after, *, name):
    send_sems, recv_sems, srcs, lands = started
    n = len(srcs)
    after = list(after)

    def body(*refs):
        src, land = refs[:n], refs[n:2 * n]
        send, recv = refs[2 * n], refs[2 * n + 1]
        _, peers = _gather_peers()
        for t in range(n):
            for k, frm in enumerate(peers):
                cp = pltpu.make_async_remote_copy(
                    src_ref=src[t], dst_ref=land[t].at[_slot(frm)], send_sem=send.at[4 * t + k],
                    recv_sem=recv.at[4 * t + k],
                    device_id=frm, device_id_type=MESH)
                cp.wait_send()
                cp.wait_recv()

    res = pl.pallas_call(
        body, name=name,
        out_shape=[pltpu.HBM(s.shape, s.dtype) for s in srcs] + [pltpu.HBM(l.shape, l.dtype) for l in lands],
        in_specs=[HBM_SPEC] * (2 * n) + [SEM_SPEC, SEM_SPEC] + [ANY_SPEC] * len(after),
        out_specs=[HBM_SPEC] * (2 * n), input_output_aliases={t: t for t in range(2 * n)},
        compiler_params=pltpu.CompilerParams(has_side_effects=DATAFLOW),
    )(*srcs, *lands, send_sems, recv_sems, *after)
    return res[:n], res[n:]


def place_own(src, land, *, name):
    R, C = src.shape
    tr = _tile(R, 512, 16)
    x, y, c, _ = _chip_slots()
    idx = jnp.stack([4 * x + 2 * y + c]).astype(jnp.int32)

    def body(idx_ref, s_ref, land_ref, o_ref):
        o_ref[...] = s_ref[...]

    return _pcall(body, name=name, grid=(R // tr,),
                  ins=[(src, (tr, C), lambda i, s: (i, 0)), (land, None, None)],
                  outs=[(land.shape, land.dtype, (None, tr, C), lambda i, s: (s[0], i, 0))],
                  prefetch=idx, aliases={2: 0}, semantics=("parallel",))[0]


def gather_finish(srcs, lands, *, name):
    n = len(srcs)

    def body(*refs):
        land = refs[n:2 * n]
        send_sems, recv_sems = refs[2 * n:]
        x, y, c, chips = _chip_slots()
        me, sibling = (x, y, c), (x, y, 1 - c)

        def copy(t, j, block, to):
            return pltpu.make_async_remote_copy(
                src_ref=land[t].at[_slot(block)], dst_ref=land[t].at[_slot(block)], send_sem=send_sems.at[t, j],
                recv_sem=recv_sems.at[t, j], device_id=to, device_id_type=MESH)

        sends = [copy(t, j, (*chip, c), sibling) for t in range(n) for j, chip in enumerate(chips)]
        for cp in sends:
            cp.start()
        for t in range(n):
            for j, chip in enumerate(chips):
                copy(t, j, (*chip, 1 - c), me).wait_recv()
        for cp in sends:
            cp.wait_send()

    passed = pl.pallas_call(
        body, name=name, out_shape=[jax.ShapeDtypeStruct(l.shape, l.dtype) for l in lands],
        in_specs=[ANY_SPEC] * n, out_specs=[ANY_SPEC] * n,
        input_output_aliases={t: t for t in range(n)},
        scratch_shapes=[pltpu.SemaphoreType.DMA((n, 3)), pltpu.SemaphoreType.DMA((n, 3))],
    )(*lands)
    return [place_own(s, l, name=f"{name}_own{t}") for t, (s, l) in enumerate(zip(srcs, passed))]


def chips_start(pairs, *, name):
    n = len(pairs)

    def body(*refs):
        src, land = refs[:n], refs[n:2 * n]
        send, recv = refs[2 * n], refs[2 * n + 1]
        token = refs[-1]
        x, y, c, chips = _chip_slots()
        for t in range(n):
            for j, chip in enumerate(chips):
                pltpu.make_async_remote_copy(
                    src_ref=src[t].at[j], dst_ref=land[t].at[j], send_sem=send.at[3 * t + j],
                    recv_sem=recv.at[3 * t + j], device_id=(*chip, c), device_id_type=MESH).start()
        token[...] = jnp.zeros_like(token)

    res = pl.pallas_call(
        body, name=name,
        out_shape=[pltpu.SemaphoreType.DMA((3 * n,)), pltpu.SemaphoreType.DMA((3 * n,))]
        + [pltpu.HBM(p.shape, p.dtype) for p in pairs] * 2 + [jax.ShapeDtypeStruct((8, LANES), F32)],
        in_specs=[HBM_SPEC] * (2 * n),
        out_specs=[SEM_SPEC, SEM_SPEC] + [HBM_SPEC] * (2 * n) + [pl.BlockSpec(memory_space=pltpu.VMEM)],
        input_output_aliases={t: 2 + t for t in range(2 * n)},
        compiler_params=pltpu.CompilerParams(has_side_effects=DATAFLOW),
    )(*[_in_hbm(p) for p in pairs], *[_in_hbm(lax.empty(p.shape, p.dtype)) for p in pairs])
    return res[0], res[1], res[2:2 + n], res[2 + n:2 + 2 * n], res[-1]


def chips_wait(started, after, *, name):
    send_sems, recv_sems, srcs, lands, _ = started
    n = len(srcs)

    def body(*refs):
        src, land = refs[:n], refs[n:2 * n]
        send, recv = refs[2 * n], refs[2 * n + 1]
        x, y, c, chips = _chip_slots()
        for t in range(n):
            for j, chip in enumerate(chips):
                cp = pltpu.make_async_remote_copy(
                    src_ref=src[t].at[j], dst_ref=land[t].at[j], send_sem=send.at[3 * t + j],
                    recv_sem=recv.at[3 * t + j], device_id=(*chip, c), device_id_type=MESH)
                cp.wait_send()
                cp.wait_recv()

    res = pl.pallas_call(
        body, name=name, out_shape=[pltpu.HBM(s.shape, s.dtype) for s in srcs] * 2,
        in_specs=[HBM_SPEC] * (2 * n) + [SEM_SPEC, SEM_SPEC, ANY_SPEC], out_specs=[HBM_SPEC] * (2 * n),
        input_output_aliases={t: t for t in range(2 * n)},
        compiler_params=pltpu.CompilerParams(has_side_effects=DATAFLOW),
    )(*srcs, *lands, send_sems, recv_sems, after)
    return res[n:]


def _sibling_copies(src, land, send, recv, n):
    x, y, c, _ = _chip_slots()
    return [pltpu.make_async_remote_copy(
        src_ref=src[t].at[4 * (q // 2) + 2 * (q % 2) + (1 - c)], dst_ref=land[t].at[q], send_sem=send.at[4 * t + q],
        recv_sem=recv.at[4 * t + q], device_id=(x, y, 1 - c), device_id_type=MESH)
        for t in range(n) for q in range(4)]


def sibling_start(gs, *, name):
    n = len(gs)

    def body(*refs):
        for cp in _sibling_copies(refs[:n], refs[n:2 * n], refs[2 * n], refs[2 * n + 1], n):
            cp.start()
        refs[-1][...] = jnp.zeros_like(refs[-1])

    lands = [lax.empty((4,) + g.shape[1:], g.dtype) for g in gs]
    res = pl.pallas_call(
        body, name=name,
        out_shape=[pltpu.SemaphoreType.DMA((4 * n,)), pltpu.SemaphoreType.DMA((4 * n,))]
        + [pltpu.HBM(g.shape, g.dtype) for g in gs] + [pltpu.HBM(l.shape, l.dtype) for l in lands]
        + [jax.ShapeDtypeStruct((8, LANES), F32)],
        in_specs=[HBM_SPEC] * (2 * n),
        out_specs=[SEM_SPEC, SEM_SPEC] + [HBM_SPEC] * (2 * n) + [pl.BlockSpec(memory_space=pltpu.VMEM)],
        input_output_aliases={t: 2 + t for t in range(2 * n)},
        compiler_params=pltpu.CompilerParams(has_side_effects=DATAFLOW),
    )(*[_in_hbm(g) for g in gs], *[_in_hbm(l) for l in lands])
    return res[0], res[1], res[2:2 + n], res[2 + n:2 + 2 * n], res[-1]


def sibling_wait(started, after, *, name):
    send_sems, recv_sems, srcs, lands, _ = started
    n = len(srcs)

    def body(*refs):
        for cp in _sibling_copies(refs[:n], refs[n:2 * n], refs[2 * n], refs[2 * n + 1], n):
            cp.wait_send()
            cp.wait_recv()

    res = pl.pallas_call(
        body, name=name,
        out_shape=[pltpu.HBM(s.shape, s.dtype) for s in srcs] + [pltpu.HBM(l.shape, l.dtype) for l in lands],
        in_specs=[HBM_SPEC] * (2 * n) + [SEM_SPEC, SEM_SPEC, ANY_SPEC], out_specs=[HBM_SPEC] * (2 * n),
        input_output_aliases={t: t for t in range(2 * n)},
        compiler_params=pltpu.CompilerParams(has_side_effects=DATAFLOW),
    )(*srcs, *lands, send_sems, recv_sems, after)
    return res[:n], res[n:]


def _rope_slab(cols):
    z = jnp.zeros(cols.shape[:-1] + (HALF_ROPE,), cols.dtype)
    return jnp.concatenate([cols[..., :HALF_ROPE], z, cols[..., HALF_ROPE:], z], axis=-1)


def _rope_unslab(slab):
    return jnp.concatenate([slab[..., :HALF_ROPE], slab[..., 2 * HALF_ROPE:3 * HALF_ROPE]], axis=-1)


def _pack_w_in_t(wt_g):
    s, c, d = wt_g.shape
    w = wt_g.reshape(s * c, d)
    c2, c3 = Q_LORA + KV_LORA, Q_LORA + KV_LORA + QK_ROPE
    r = w[c2:c3]
    z = jnp.zeros((HALF_ROPE, d), w.dtype)
    return jnp.concatenate([w[:c2], w[c3:], r[:HALF_ROPE], z, r[HALF_ROPE:], z], axis=0)


def unpack_w_in_t_grad(dwt, *, name):
    n_rows, d = dwt.shape
    kr = QK_ROPE
    n_out_rows = n_rows - kr
    blk = n_out_rows // 7
    assert blk * 7 == n_out_rows and blk % kr == 0 and n_rows % (2 * kr) == 0
    kr_row = Q_LORA + KV_LORA
    k_mix = kr_row // blk
    off = kr_row - k_mix * blk
    slab_block = (n_rows - 2 * kr) // (2 * kr)

    def body(prev_ref, in_ref, slab_ref, o_ref):
        k = pl.program_id(0)

        @pl.when(k < k_mix)
        def _():
            o_ref[...] = in_ref[...]

        @pl.when(k == k_mix)
        def _():
            o_ref[:off, :] = in_ref[:off, :]
            o_ref[off:off + HALF_ROPE, :] = slab_ref[:HALF_ROPE, :]
            o_ref[off + HALF_ROPE:off + kr, :] = slab_ref[2 * HALF_ROPE:3 * HALF_ROPE, :]
            o_ref[off + kr:, :] = in_ref[off:blk - kr, :]

        @pl.when(k > k_mix)
        def _():
            o_ref[:kr, :] = prev_ref[blk - kr:, :]
            o_ref[kr:, :] = in_ref[:blk - kr, :]

    out = _pcall(body, name=name, grid=(7,),
                 ins=[(dwt, (blk, d), lambda k: (jnp.maximum(k - 1, 0), 0)), (dwt, (blk, d), lambda k: (k, 0)),
                      (dwt, (2 * kr, d), lambda k: (slab_block, 0))],
                 outs=[((n_out_rows, d), dwt.dtype, (blk, d), lambda k: (k, 0))], semantics=("parallel",))[0]
    return out.reshape(N_DEV, n_out_rows // N_DEV, d)


def _rope_tables(positions):
    inv_freq = ROPE_BASE ** (-jnp.arange(0, QK_ROPE, 2, dtype=F32) / QK_ROPE)
    ang = positions.astype(F32)[:, None] * inv_freq
    cos, sin = jnp.cos(ang), jnp.sin(ang)
    z = jnp.zeros_like(cos)
    return jnp.concatenate([cos, z, cos, z], axis=-1), jnp.concatenate([-sin, z, sin, z], axis=-1)


def _mlp_up(x, gain, w1, tag):
    hn = rms_fwd(x, gain, name=f"mlp{tag}_norm")

    def act_epi(acc):
        a = jnp.maximum(acc, 0.0)
        return a, a * a

    T = x.shape[0]
    F = w1.shape[0] * w1.shape[2]
    a, act = mm(hn, w1, name=f"mlp{tag}_up", outs=[((T, F), BF, None), ((T, F), BF, None)], epi=act_epi)
    return hn, a, act


def _mlp_down(x, act, w2, tag, part=0):
    n = w2.shape[1]
    bm = _tile(x.shape[0], MM_TILE)
    bn = _tile(n, MM_TILE)
    per = n // bn
    return mm(act, w2, name=f"mlp{tag}_down{part}", out=((x.shape[0], n), F32), bm=bm, bn=bn,
              epi=lambda acc, r: (acc + r[...],), epi_ins=[(x, (bm, bn), lambda i, j, k: (i, part * per + j))])


def _mlp_bwd_weights(w1, w2, saved, dxb, tag):
    hn, a, act = saved
    T, D = dxb.shape
    F = a.shape[1]
    bm = _tile(T, MM_TILE)
    bn = _tile(F, min(MM_TILE, w1.shape[2]))
    dhid = mm(dxb, w2, tb=True, name=f"mlp{tag}_dhid", out=((T, F), BF), bm=bm, bn=bn,
              epi=lambda acc, a_ref: (2.0 * a_ref[...].astype(F32) * acc,),
              epi_ins=[(a, (bm, bn), lambda i, j, k: (i, j))])
    dw2 = mm(act, dxb, ta=True, name=f"mlp{tag}_dw2", out=((F, D), BF))
    dw1 = mm(hn, dhid, ta=True, name=f"mlp{tag}_dw1", out=(w1.shape, BF))
    return dhid, dw1, dw2.reshape(N_DEV, F // N_DEV, D)


def _reduce_begin(grads, tag):
    return sibling_start(grads, name=f"reduce_sibling_start_{tag}")


def _reduce_continue(sib, after, tag, index):
    grads, a_bufs = sibling_wait(sib, after, name=f"reduce_sibling_wait_{tag}")
    pairs = [run_job(pair_job(g, a), index=index, name=f"pair_sum_{tag}{t}")[0]
             for t, (g, a) in enumerate(zip(grads, a_bufs))]
    return grads, a_bufs, chips_start(pairs, name=f"reduce_chips_start_{tag}")


def kernel(x, positions, e_norm_mix, e_w_in, e_q_norm, e_w_uq, e_kv_norm, e_w_ukv, e_v_norm, e_sgu_w, e_sgu_b, e_mla_out_norm, e_sgu_out_norm, e_w_out, o_norm_mix, o_w_in, o_conv_w, o_w_out, mlp_norm, mlp_w1, mlp_w2, final_norm, loss_target, m_e_norm_mix, m_e_w_in, m_e_q_norm, m_e_w_uq, m_e_kv_norm, m_e_w_ukv, m_e_v_norm, m_e_sgu_w, m_e_sgu_b, m_e_mla_out_norm, m_e_sgu_out_norm, m_e_w_out, m_o_norm_mix, m_o_w_in, m_o_conv_w, m_o_w_out, m_mlp_norm, m_mlp_w1, m_mlp_w2, m_final_norm, v_e_norm_mix, v_e_w_in, v_e_q_norm, v_e_w_uq, v_e_kv_norm, v_e_w_ukv, v_e_v_norm, v_e_sgu_w, v_e_sgu_b, v_e_mla_out_norm, v_e_sgu_out_norm, v_e_w_out, v_o_norm_mix, v_o_w_in, v_o_conv_w, v_o_w_out, v_mlp_norm, v_mlp_w1, v_mlp_w2, v_final_norm):
    T, D = x.shape[1], x.shape[2]
    d_shard = o_norm_mix.shape[1]
    x0 = x[0]
    target = loss_target[0]
    me = 4 * lax.axis_index("x") + 2 * lax.axis_index("y") + lax.axis_index("c")

    bf = lambda s: s.astype(BF)
    gather_groups = [[bf(jnp.transpose(e_w_in[0])), bf(e_w_uq[0]), bf(e_w_ukv[0])], [bf(e_w_out[0]), bf(mlp_w1[0])],
                     [bf(mlp_w2[0]), bf(o_w_in[0])], [bf(o_w_out[0]), bf(mlp_w1[1])], [bf(mlp_w2[1])]]
    small_rows = jnp.concatenate([o_norm_mix, o_conv_w[0], jnp.zeros((4, d_shard), F32)], axis=0)
    gather_groups[0].insert(0, small_rows)
    started, start_token = gather_start(gather_groups[:1], x0, name="gather_start0")
    started_rest, rest_token = gather_start(gather_groups[1:], start_token, name="gather_start1")
    started += started_rest

    def gathered(gi, after):
        srcs, lands = gather_wait(started[gi], after, name=f"gather_wait{gi}")
        return gather_finish(srcs, lands, name=f"gather_finish{gi}")

    w_tril = jnp.tril(e_sgu_w[0])
    w_tril_b = w_tril.astype(BF)
    w_tril_tb = jnp.swapaxes(w_tril, 1, 2).astype(BF)
    b_full = jnp.repeat(e_sgu_b[0].T, CH, axis=1)
    v_gain = e_v_norm[0].reshape(1, SGU_OUT)
    cos_t, sin_t = _rope_tables(positions[0])
    mlp_gain = [mlp_norm[0:1], mlp_norm[1:2]]
    final_gain = final_norm.reshape(1, D)

    h0 = rms_fwd(x0, e_norm_mix, name="e_norm", deps=[rest_token])
    small_g, g_w_in_t, g_w_uq, w_ukv = gathered(0, [h0, cos_t, sin_t, w_tril_b, w_tril_tb, b_full])
    o_norm_full = small_g[:, 0, :].reshape(1, D)
    conv_w_full = jnp.transpose(small_g[:, 1:4, :], (1, 0, 2)).reshape(3, D)
    w_in_t = _pack_w_in_t(g_w_in_t)
    w_uq = jnp.concatenate([g_w_uq[..., :QK_NOPE], _rope_slab(g_w_uq[..., QK_NOPE:])], axis=-1)
    proj = mm(h0, w_in_t, tb=True, name="e_in", out=((T, w_in_t.shape[0]), F32), bn=_tile(w_in_t.shape[0], 640))
    qn, kvn, krope = mla_prep(proj, e_q_norm, e_kv_norm, cos_t, sin_t, name="mla_prep")
    bm = _tile(T, MM_TILE)

    def q_epi(acc, cos_ref, sin_ref):
        return (jnp.concatenate([acc[:, :QK_NOPE], _rope_fwd(acc[:, QK_NOPE:], cos_ref[...], sin_ref[...])], axis=-1),)

    q = mm(qn, w_uq, name="mla_q", out=((T, HEADS * HEAD_PAD), BF), bm=bm, bn=HEAD_PAD, epi=q_epi,
           epi_ins=[(cos_t, (bm, LANES), lambda i, j, k: (i, 0)), (sin_t, (bm, LANES), lambda i, j, k: (i, 0))])

    def kv_epi(acc, kr_ref):
        return jnp.concatenate([acc[:, :QK_NOPE].astype(BF), kr_ref[...]], axis=-1), acc[:, QK_NOPE:]

    k, v = mm(kvn, w_ukv, name="mla_kv", bm=bm, bn=HEAD_PAD, epi=kv_epi,
              outs=[((T, HEADS * HEAD_PAD), BF, HEAD_PAD), ((T, MLA_OUT), BF, V_HEAD)],
              epi_ins=[(krope, (bm, LANES), lambda i, j, k: (i, 0))])
    attn, attn_lse = attn_fwd(q, k, v, name="attn_fwd")
    mixed = mix_fwd(attn, proj, e_mla_out_norm, e_sgu_out_norm, v_gain, w_tril_b, b_full, name="mix_fwd")
    bn = _tile(D, MM_TILE)
    g_w_out_e, w1_0 = gathered(1, [mixed])
    w_out_e = g_w_out_e.reshape(-1, D)
    x1 = mm(mixed, w_out_e, name="e_out", out=((T, D), F32), bm=bm, bn=bn,
            epi=lambda acc, r: (acc + r[...],), epi_ins=[(x0, (bm, bn), lambda i, j, k: (i, j))])
    hn0, a0, act0 = _mlp_up(x1, mlp_gain[0], w1_0, 0)
    g_w2_0, g_w_in_o = gathered(2, [act0])
    w2_0 = g_w2_0.reshape(-1, D)
    x2 = _mlp_down(x1, act0, w2_0, 0)
    ho = rms_fwd(x2, o_norm_full, name="o_norm")
    proj_o = mm(ho, g_w_in_o, name="o_in", out=((T, 3 * D), F32))
    gated = conv_fwd(proj_o, conv_w_full, name="conv_fwd")
    g_w_out_o, w1_1 = gathered(3, [gated])
    w_out_o = g_w_out_o.reshape(-1, D)
    x3 = mm(gated, w_out_o, name="o_out", out=((T, D), F32), bm=bm, bn=bn,
            epi=lambda acc, r: (acc + r[...],), epi_ins=[(x2, (bm, bn), lambda i, j, k: (i, j))])
    hn1, a1, act1 = _mlp_up(x3, mlp_gain[1], w1_1, 1)
    (g_w2_1,) = gathered(4, [act1])
    w2_1 = g_w2_1.reshape(-1, D)
    x4 = _mlp_down(x3, act1, w2_1, 1)
    w1, w2 = [w1_0, w1_1], [w2_0, w2_1]

    dx4, dx4b, d_final, loss_part = loss_bwd([x4], final_gain, target, name="loss_bwd")

    hosted = dict(job_index=device_index())
    dhid1, dw1_1, dw2_1 = _mlp_bwd_weights(w1[1], w2[1], (hn1, a1, act1), dx4b, 1)
    sib_r0 = _reduce_begin([dw1_1, dw2_1], "r0")
    dhn1 = mm(dhid1, w1[1], tb=True, name="mlp1_dhn", out=((T, D), F32), deps=[sib_r0[-1]])
    grads_r0, a_r0 = sibling_wait(sib_r0, dhn1, name="reduce_sibling_wait_r0")
    dx3, dx3b, d_mlp1 = rms_bwd(x3, mlp_gain[1], dhn1, dres=dx4, name="mlp1_norm_bwd")

    dgated, ((pair_r0a,),) = mm(dx3b, w_out_o, tb=True, name="o_out_dx", out=((T, D), F32),
                                jobs=[pair_job(grads_r0[0], a_r0[0])], **hosted)
    dw_out_o, ((pair_r0b,),) = mm(gated, dx3b, ta=True, name="o_out_dw", out=((D, D), BF),
                                  jobs=[pair_job(grads_r0[1], a_r0[1])], **hosted)
    st_r0 = chips_start([pair_r0a, pair_r0b], name="reduce_chips_start_r0")
    dproj_o, dconv_full = conv_bwd(dgated, proj_o, conv_w_full, name="conv_bwd", deps=[st_r0[-1]])
    dw_in_o = mm(ho, dproj_o, ta=True, name="o_in_dw", out=(g_w_in_o.shape, BF))
    sib_r1 = _reduce_begin([dw_out_o.reshape(g_w_out_o.shape), dw_in_o], "r1")
    dho = mm(dproj_o, g_w_in_o, tb=True, name="o_in_dx", out=((T, D), F32), deps=[sib_r1[-1]])
    grads_r1, a_r1 = sibling_wait(sib_r1, dho, name="reduce_sibling_wait_r1")
    dx2, dx2b, d_onorm_full = rms_bwd(x2, o_norm_full, dho, dres=dx3, name="o_norm_bwd")

    d_ff = a0.shape[1]
    bm_h, bn_h = _tile(T, MM_TILE), _tile(d_ff, min(MM_TILE, w1[0].shape[2]))
    dhid0, ((pair_r1a,), (pair_r1b,)) = mm(
        dx2b, w2[0], tb=True, name="mlp0_dhid", out=((T, d_ff), BF), bm=bm_h, bn=bn_h,
        epi=lambda acc, a_ref: (2.0 * a_ref[...].astype(F32) * acc,),
        epi_ins=[(a0, (bm_h, bn_h), lambda i, j, k: (i, j))],
        jobs=[pair_job(grads_r1[0], a_r1[0]), pair_job(grads_r1[1], a_r1[1])], **hosted)
    st_r1 = chips_start([pair_r1a, pair_r1b], name="reduce_chips_start_r1")
    dw2_0 = mm(act0, dx2b, ta=True, name="mlp0_dw2", out=((d_ff, D), BF), deps=[st_r1[-1]])
    b_r0 = chips_wait(st_r0, dw2_0, name="reduce_chips_wait_r0")
    dw1_0, (r_w1, r_w2) = mm(
        hn0, dhid0, ta=True, name="mlp0_dw1", out=(w1[0].shape, BF),
        jobs=[adam_job(grads_r0[0], a_r0[0], b_r0[0], mlp_w1, m_mlp_w1, v_mlp_w1, 1, None),
              adam_job(grads_r0[1], a_r0[1], b_r0[1], mlp_w2, m_mlp_w2, v_mlp_w2, 1, None)], **hosted)
    sib_r2 = _reduce_begin([dw1_0, dw2_0.reshape(N_DEV, d_ff // N_DEV, D)], "r2")
    dhn0 = mm(dhid0, w1[0], tb=True, name="mlp0_dhn", out=((T, D), F32), deps=[sib_r2[-1]])
    grads_r2, a_r2 = sibling_wait(sib_r2, dhn0, name="reduce_sibling_wait_r2")
    dx1, dx1b, d_mlp0 = rms_bwd(x1, mlp_gain[0], dhn0, dres=dx2, name="mlp0_norm_bwd")

    dmixed, ((pair_r2a,),) = mm(dx1b, w_out_e, tb=True, name="e_out_dx", out=((T, MLA_OUT + SGU_OUT), F32),
                                jobs=[pair_job(grads_r2[0], a_r2[0])], **hosted)
    dw_out_e, ((pair_r2b,),) = mm(mixed, dx1b, ta=True, name="e_out_dw", out=(w_out_e.shape, BF),
                                  jobs=[pair_job(grads_r2[1], a_r2[1])], **hosted)
    st_r2 = chips_start([pair_r2a, pair_r2b], name="reduce_chips_start_r2")
    (dattn, dproj, d_mla_out, d_sgu_out, d_vgain, d_sgu_w, d_b_full) = mix_bwd(
        dmixed, attn, proj, e_mla_out_norm, e_sgu_out_norm, v_gain, w_tril_b, w_tril_tb, b_full, name="mix_bwd",
        deps=[st_r2[-1]])
    b_r1 = chips_wait(st_r1, dattn, name="reduce_chips_wait_r1")
    dq, dk, dv = attn_bwd(q, k, v, attn, attn_lse, dattn, name="attn_bwd")
    dq_lin, dkv_lin, dproj = mla_bwd_prep(dq, dk, dv, cos_t, sin_t, dproj, name="mla_bwd_prep")
    dw_uq_pad = mm(qn, dq_lin, ta=True, name="mla_q_dw", out=(w_uq.shape, BF))
    dw_ukv = mm(kvn, dkv_lin, ta=True, name="mla_kv_dw", out=(w_ukv.shape, BF))
    dw_uq = jnp.concatenate([dw_uq_pad[..., :QK_NOPE], _rope_unslab(dw_uq_pad[..., QK_NOPE:])], axis=-1)
    sib_r2b = _reduce_begin([dw_out_e.reshape(g_w_out_e.shape), dw_uq, dw_ukv], "r2b")
    dqn = mm(dq_lin, w_uq, tb=True, name="mla_q_dx", out=((T, Q_LORA), F32), deps=[sib_r2b[-1]])
    dkvn = mm(dkv_lin, w_ukv, tb=True, name="mla_kv_dx", out=((T, KV_LORA), F32), deps=[sib_r2b[-1]])
    grads_r2b, a_r2b, st_r2b = _reduce_continue(sib_r2b, dkvn, "r2b", hosted["job_index"])
    dproj, d_qnorm = rms_bwd(proj, e_q_norm, dqn, col_block=0, want_f32=False, into=dproj, name="q_norm_bwd",
                             deps=[st_r2b[-1]])
    dproj, d_kvnorm = rms_bwd(proj, e_kv_norm, dkvn, col_block=1, want_f32=False, into=dproj, name="kv_norm_bwd")
    dw_in_t_pad, (r_w_out_o, r_w_in_o) = mm(
        dproj, h0, ta=True, name="e_in_dw", out=(w_in_t.shape, BF), bm=_tile(w_in_t.shape[0], 640),
        jobs=[adam_job(grads_r1[0], a_r1[0], b_r1[0], o_w_out, m_o_w_out, v_o_w_out, 0, None),
              adam_job(grads_r1[1], a_r1[1], b_r1[1], o_w_in, m_o_w_in, v_o_w_in, 0, None)], **hosted)
    dw_in_t = unpack_w_in_t_grad(dw_in_t_pad, name="e_in_dw_unpack")
    sib_r3 = _reduce_begin([dw_in_t], "r3")
    dh0 = mm(dproj, w_in_t, name="e_in_dx", out=((T, D), F32), deps=[sib_r3[-1]])
    grads_r3, a_r3, st_r3 = _reduce_continue(sib_r3, dh0, "r3", hosted["job_index"])
    tok_r3 = st_r3[-1]
    grad_x, d_enorm = rms_bwd(x0, e_norm_mix, dh0, dres=dx1, want_bf=False, name="e_norm_bwd", deps=[tok_r3])
    b_r2 = chips_wait(st_r2, grad_x, name="reduce_chips_wait_r2")

    d_sgu_b = jnp.transpose(d_b_full[:, ::CH])
    d_sgu_w_tril = jnp.tril(d_sgu_w)
    rep = [("e_norm_mix", e_norm_mix, m_e_norm_mix, v_e_norm_mix, d_enorm),
           ("e_q_norm", e_q_norm, m_e_q_norm, v_e_q_norm, d_qnorm),
           ("e_kv_norm", e_kv_norm, m_e_kv_norm, v_e_kv_norm, d_kvnorm),
           ("e_v_norm", e_v_norm, m_e_v_norm, v_e_v_norm, d_vgain),
           ("e_sgu_w", e_sgu_w, m_e_sgu_w, v_e_sgu_w, d_sgu_w_tril),
           ("e_sgu_b", e_sgu_b, m_e_sgu_b, v_e_sgu_b, d_sgu_b),
           ("e_mla_out_norm", e_mla_out_norm, m_e_mla_out_norm, v_e_mla_out_norm, d_mla_out),
           ("e_sgu_out_norm", e_sgu_out_norm, m_e_sgu_out_norm, v_e_sgu_out_norm, d_sgu_out),
           ("mlp_norm", mlp_norm, m_mlp_norm, v_mlp_norm, jnp.concatenate([d_mlp0, d_mlp1], axis=0)),
           ("final_norm", final_norm, m_final_norm, v_final_norm, d_final)]
    sizes = [int(np.prod(r[1].shape)) for r in rep]
    n_rep = sum(sizes)
    n_all = n_rep + 4 * D + 1
    width = -(-n_all // (8 * LANES)) * LANES
    pad = 8 * width - n_all
    flat = jnp.concatenate([r[4].reshape(-1) for r in rep]
                           + [d_onorm_full.reshape(-1), dconv_full.reshape(-1), loss_part[0, :1],
                              jnp.zeros((pad,), F32)])
    small_started, small_token = gather_start([[flat.reshape(8, width)]], b_r2[0], name="gather_small_grads_start")

    def finish(grads, a_bufs, b_bufs, t, w, m, v, layer=0, prev=None, tag="", deps=()):
        return run_job(adam_job(grads[t], a_bufs[t], b_bufs[t], w, m, v, layer, prev), index=hosted["job_index"],
                       name=f"adam_{tag}", deps=deps)

    r_w1 = finish(grads_r2, a_r2, b_r2, 0, mlp_w1, m_mlp_w1, v_mlp_w1, 0, r_w1, tag="w1_l0", deps=[tok_r3, small_token])
    r_w2 = finish(grads_r2, a_r2, b_r2, 1, mlp_w2, m_mlp_w2, v_mlp_w2, 0, r_w2, tag="w2_l0", deps=[r_w1[1]])
    b_r2b = chips_wait(st_r2b, r_w2[1], name="reduce_chips_wait_r2b")
    r_w_out_e = finish(grads_r2b, a_r2b, b_r2b, 0, e_w_out, m_e_w_out, v_e_w_out, tag="e_w_out")
    r_w_uq = finish(grads_r2b, a_r2b, b_r2b, 1, e_w_uq, m_e_w_uq, v_e_w_uq, tag="e_w_uq")
    r_w_ukv = finish(grads_r2b, a_r2b, b_r2b, 2, e_w_ukv, m_e_w_ukv, v_e_w_ukv, tag="e_w_ukv")
    b_r3 = chips_wait(st_r3, r_w_out_e[1], name="reduce_chips_wait_r3")
    g_w_in_t = reduce_sum(grads_r3[0], a_r3[0], b_r3[0], name="sum_e_w_in")
    w_in_upd_t = adam_rows(g_w_in_t, jnp.transpose(e_w_in[0]), jnp.transpose(m_e_w_in[0]), jnp.transpose(v_e_w_in[0]),
                           name="adam_e_w_in")
    r_w_in = [jnp.transpose(t)[None] for t in (g_w_in_t, *w_in_upd_t)]

    small_srcs, small_lands = gather_wait(small_started[0], [r_w2[1]], name="gather_small_grads_wait")
    small_all = gather_finish(small_srcs, small_lands, name="gather_small_grads_finish")[0]
    summed = sum_rows8(small_all.reshape(N_DEV * 8, width), 8, name="sum_small_grads").reshape(-1)

    loss = summed[n_rep + 4 * D]

    def pack_rep(i):
        return jnp.concatenate([r[i].reshape(-1) for r in rep]).reshape(n_rep // LANES, LANES)

    g_rep = summed[:n_rep].reshape(n_rep // LANES, LANES)
    d_rep, nm_rep, nv_rep = adam_flat(g_rep, pack_rep(1), pack_rep(2), pack_rep(3), name="adam_replicated")

    def unpack_rep(flat2d):
        out, off = {}, 0
        f = flat2d.reshape(-1)
        for r, n in zip(rep, sizes):
            out[r[0]] = f[off:off + n].reshape(r[1].shape)
            off += n
        return out

    small = {"grad": unpack_rep(g_rep), "delta": unpack_rep(d_rep), "new_m": unpack_rep(nm_rep),
             "new_v": unpack_rep(nv_rep)}
    g_onorm = lax.dynamic_slice(summed[n_rep:n_rep + D].reshape(1, D), (0, me * d_shard), (1, d_shard))
    g_conv = lax.dynamic_slice(summed[n_rep + D:n_rep + 4 * D].reshape(3, D), (0, me * d_shard), (3, d_shard))

    def pack_sharded(norm_part, conv_part):
        return jnp.concatenate([norm_part, conv_part, jnp.zeros((4, d_shard), F32)], axis=0)

    g_sh = pack_sharded(g_onorm, g_conv)
    d_sh, nm_sh, nv_sh = adam_flat(g_sh, pack_sharded(o_norm_mix, o_conv_w[0]), pack_sharded(m_o_norm_mix, m_o_conv_w[0]),
                                   pack_sharded(v_o_norm_mix, v_o_conv_w[0]), name="adam_sharded_small")
    for kind, arr in (("grad", g_sh), ("delta", d_sh), ("new_m", nm_sh), ("new_v", nv_sh)):
        small[kind]["o_norm_mix"] = arr[0:1]
        small[kind]["o_conv_w"] = arr[1:4][None]

    big = {"e_w_in": r_w_in, "e_w_uq": r_w_uq, "e_w_ukv": r_w_ukv, "e_w_out": r_w_out_e, "o_w_in": r_w_in_o,
           "o_w_out": r_w_out_o, "mlp_w1": r_w1, "mlp_w2": r_w2}
    order = ["e_norm_mix", "e_w_in", "e_q_norm", "e_w_uq", "e_kv_norm", "e_w_ukv", "e_v_norm", "e_sgu_w", "e_sgu_b",
             "e_mla_out_norm", "e_sgu_out_norm", "e_w_out", "o_norm_mix", "o_w_in", "o_conv_w", "o_w_out", "mlp_norm",
             "mlp_w1", "mlp_w2", "final_norm"]
    result = [loss, grad_x[None]]
    for ki, kind in enumerate(("grad", "delta", "new_m", "new_v")):
        for nm in order:
            result.append(big[nm][ki] if nm in big else small[kind][nm])
    return tuple(result)
```

```python
import numpy as np
import jax
import jax.numpy as jnp
from jax import lax
from jax.experimental import pallas as pl
from jax.experimental.pallas import tpu as pltpu

BF = jnp.bfloat16
F32 = jnp.float32
MESH = pl.DeviceIdType.MESH
N_DEV = 8

EPS = 1e-6
HEADS = 8
Q_LORA = 512
KV_LORA = 512
QK_NOPE = 128
QK_ROPE = 64
HALF_ROPE = QK_ROPE // 2
V_HEAD = 128
HEAD_PAD = 256
ROPE_BASE = 10000.0
GROUPS = 8
CH = 128
CHUNK = 128
SGU_OUT = GROUPS * CH
MLA_OUT = HEADS * V_HEAD
ATTN_SCALE = float((QK_NOPE + QK_ROPE) ** -0.5)

ADAM_LR = 0.001
ADAM_B1 = 0.9
ADAM_B2 = 0.999
ADAM_EPS = 1e-08
ADAM_WD = 0.01
ADAM_STEP = 10
ADAM_C1 = 1.0 - ADAM_B1 ** ADAM_STEP
ADAM_C2 = 1.0 - ADAM_B2 ** ADAM_STEP

V7X_VMEM_BYTES = 64 * 2 ** 20
VMEM_LIMIT_CAP = V7X_VMEM_BYTES - 6 * 2 ** 20
LANES = 128
ROW_TILE = 256
ATTN_TILE = 1024
STREAM_BLOCK_ELEMS = 128 * 1024
MM_TILE = 1024
MM_K_TILE = 2048
MM_K_BLOCK_MAX = 4096


def _padded_bytes(block, dtype):
    dims = [d for d in block if d is not None]
    if len(dims) >= 1:
        dims[-1] = -(-dims[-1] // LANES) * LANES
    if len(dims) >= 2:
        dims[-2] = -(-dims[-2] // 16) * 16
    return int(np.prod(dims)) * jnp.dtype(dtype).itemsize


def _pcall(body, *, name, grid, ins, outs, scratch=(), semantics=None, aliases=None, prefetch=None, deps=()):
    any_spec = pl.BlockSpec(memory_space=pl.ANY)
    if deps:
        n_lead = len(ins) + (1 if prefetch is not None else 0)
        n_deps = len(deps)
        inner = body

        def body(*refs):
            inner(*refs[:n_lead], *refs[n_lead + n_deps:])

        ins = list(ins) + [(d, None, None) for d in deps]
    in_specs = [any_spec if b is None else pl.BlockSpec(b, m) for _, b, m in ins]
    out_specs = [any_spec if b is None else pl.BlockSpec(b, m) for _, _, b, m in outs]
    out_shape = [pltpu.HBM(s, d) for s, d, _, _ in outs]
    est = 0
    for a, b, _ in ins:
        if b is not None:
            est += 2 * _padded_bytes(b, a.dtype)
    for _, d, b, _ in outs:
        if b is not None:
            est += 2 * _padded_bytes(b, d)
    for s in scratch:
        if hasattr(s, "shape") and hasattr(s, "dtype"):
            est += _padded_bytes(s.shape, s.dtype)
    limit = int(min(VMEM_LIMIT_CAP, est + 16 * 2 ** 20))
    params = pltpu.CompilerParams(
        dimension_semantics=semantics or ("arbitrary",) * len(grid), vmem_limit_bytes=limit)
    args = [pltpu.with_memory_space_constraint(a, pltpu.HBM) for a, _, _ in ins]
    if prefetch is not None:
        grid_spec = pltpu.PrefetchScalarGridSpec(
            num_scalar_prefetch=1, grid=grid, in_specs=in_specs, out_specs=out_specs, scratch_shapes=list(scratch))
        call = pl.pallas_call(body, out_shape=out_shape, grid_spec=grid_spec, name=name, compiler_params=params,
                              input_output_aliases=aliases or {})
        return call(prefetch, *args)
    call = pl.pallas_call(body, out_shape=out_shape, grid=grid, in_specs=in_specs, out_specs=out_specs,
                          scratch_shapes=list(scratch), name=name, compiler_params=params,
                          input_output_aliases=aliases or {})
    return call(*args)


def _tile(dim, pref, quantum=LANES):
    if dim <= pref:
        return dim
    t = (pref // quantum) * quantum
    while t >= quantum:
        if dim % t == 0:
            return t
        t -= quantum
    return dim


def _vshape(arr_shape):
    if len(arr_shape) == 2:
        return tuple(arr_shape)
    s, r, c = arr_shape
    return (r, s * c)


def _vblock(arr_shape, br, bc, rc):
    if len(arr_shape) == 2:
        return (br, bc), (lambda *g: rc(*g))
    _, _, c = arr_shape
    assert c % bc == 0, (arr_shape, bc)
    per = c // bc

    def imap(*g):
        ri, ci = rc(*g)
        return (ci // per, ri, ci % per)

    return (None, br, bc), imap


def _shard_width(*shapes):
    w = None
    for s in shapes:
        if len(s) == 3:
            w = s[2] if w is None else int(np.gcd(w, s[2]))
    return w


def mm(a, b, *, name, ta=False, tb=False, out=None, outs=None, epi=None, epi_ins=(), bm=None, bn=None, bk=None,
       deps=(), jobs=(), job_index=None):
    av, bv = _vshape(a.shape), _vshape(b.shape)
    M, K = (av[1], av[0]) if ta else av
    K2, N = (bv[1], bv[0]) if tb else bv
    assert K == K2, (a.shape, b.shape, ta, tb)
    if outs is None:
        outs = [(out[0], out[1], None)]
    a_sw = _shard_width(a.shape)
    b_sw = _shard_width(b.shape)
    o_sw = _shard_width(*[o[0] for o in outs])
    m_lim = a_sw if (ta and a_sw) else None
    k_lim = [w for w in ((a_sw if not ta else None), (b_sw if tb else None)) if w]
    n_lim = [w for w in ((b_sw if not tb else None), o_sw) if w]
    if bm is None:
        bm = _tile(M, min([MM_TILE] + ([m_lim] if m_lim else [])))
    if bn is None:
        bn = _tile(N, min([MM_TILE] + n_lim))
    k_shards = 0
    if tb and len(b.shape) == 3 and bk is None and not (a_sw and not ta):
        k_shards = 1
        while 2 * k_shards <= b.shape[0] and 2 * k_shards * b_sw <= MM_K_BLOCK_MAX:
            k_shards *= 2
        bk = k_shards * b_sw
    if bk is None:
        bk = K if (K <= 4096 and not k_lim) else _tile(K, min([MM_K_TILE] + k_lim))
    assert M % bm == 0 and N % bn == 0 and K % bk == 0, (name, M, N, K, bm, bn, bk)
    nk = K // bk
    grid = (M // bm, N // bn, nk)
    if ta:
        a_blk, a_map = _vblock(a.shape, bk, bm, lambda i, j, k: (k, i))
    else:
        a_blk, a_map = _vblock(a.shape, bm, bk, lambda i, j, k: (i, k))
    if k_shards:
        b_blk, b_map = (k_shards, bn, b_sw), (lambda i, j, k: (k, j, 0))
    elif tb:
        b_blk, b_map = _vblock(b.shape, bn, bk, lambda i, j, k: (j, k))
    else:
        b_blk, b_map = _vblock(b.shape, bk, bn, lambda i, j, k: (k, j))
    dn = (((0 if ta else 1,), (1 if tb else 0,)), ((), ()))
    ins = [(a, a_blk, a_map), (b, b_blk, b_map)] + list(epi_ins)
    out_list = []
    for shape, dtype, cols in outs:
        cols = cols or bn
        blk, imap = _vblock(shape, bm, cols, lambda i, j, k: (i, j))
        out_list.append((shape, dtype, blk, imap))
    n_e, n_o = len(epi_ins), len(out_list)

    n_steps = grid[0] * grid[1] * nk
    built = [job(n_steps) for job in jobs]
    aliases = {}
    job_slices = []
    if built:
        def lin(i, j, k):
            return (i * grid[1] + j) * nk + k

        ins = [(arr, blk, None if blk is None else (lambda i, j, k, s, f=f: f(i, j, k))) for arr, blk, f in ins]
        out_list = [(sh, dt, blk, (lambda i, j, k, s, f=f: f(i, j, k))) for sh, dt, blk, f in out_list]
        n_main_in, n_main_out = len(ins), len(out_list)
        for jb in built:
            i0, o0 = len(ins), len(out_list)
            ins += [(arr, blk, None if blk is None else (lambda i, j, k, s, f=f: f(lin(i, j, k), s)))
                    for arr, blk, f in jb["ins"]]
            out_list += [(sh, dt, blk, (lambda i, j, k, s, f=f: f(lin(i, j, k), s))) for sh, dt, blk, f in jb["outs"]]
            aliases.update({1 + i0 + ai: o0 + ao for ai, ao in jb["aliases"].items()})
            job_slices.append((i0, len(jb["ins"]), o0, len(jb["outs"])))
    n_in_total = len(ins)

    def body(*refs):
        if built:
            refs = refs[1:]
        a_ref, b_ref = refs[0], refs[1]
        e_refs = refs[2:2 + n_e]
        o_refs = refs[n_in_total:n_in_total + n_o]
        for jb, (i0, ni, o0, no) in zip(built, job_slices):
            jb["fn"](refs[i0:i0 + ni], refs[n_in_total + o0:n_in_total + o0 + no])

        def finish(acc):
            res = epi(acc, *e_refs) if epi is not None else (acc,)
            for o_ref, r in zip(o_refs, res):
                o_ref[...] = r.astype(o_ref.dtype)

        x = a_ref[...].astype(BF)
        y = b_ref[...].astype(BF)
        if k_shards:
            p = None
            for s in range(k_shards):
                part = lax.dot_general(x[:, s * b_sw:(s + 1) * b_sw], y[s], dn, preferred_element_type=F32)
                p = part if p is None else p + part
        else:
            p = lax.dot_general(x, y, dn, preferred_element_type=F32)
        if nk == 1:
            finish(p)
        else:
            acc_ref = refs[-1]
            k = pl.program_id(2)

            @pl.when(k == 0)
            def _():
                acc_ref[...] = p

            @pl.when(k > 0)
            def _():
                acc_ref[...] += p

            @pl.when(k == nk - 1)
            def _():
                finish(acc_ref[...])

    scratch = [pltpu.VMEM((bm, bn), F32)] if nk > 1 else []
    res = _pcall(body, name=name, grid=grid, ins=ins, outs=out_list, scratch=scratch, deps=deps,
                 semantics=("parallel", "parallel", "arbitrary"), prefetch=job_index if built else None, aliases=aliases)
    main = res[0] if n_o == 1 else res[:n_o]
    if not built:
        return main
    return main, [res[o0:o0 + no] for _, _, o0, no in job_slices]


_GELU_K = float(np.sqrt(2.0 / np.pi))
_GELU_C = 0.044715


def _gelu(x):
    t = jnp.tanh(_GELU_K * (x + _GELU_C * (x * x * x)))
    return 0.5 * x * (1.0 + t)


def _gelu_grad(x):
    t = jnp.tanh(_GELU_K * (x + _GELU_C * (x * x * x)))
    return 0.5 * (1.0 + t) + 0.5 * x * (1.0 - t * t) * (_GELU_K * (1.0 + 3.0 * _GELU_C * (x * x)))


def _rstd(x):
    return lax.rsqrt(jnp.mean(x * x, axis=-1, keepdims=True) + EPS)


def _rms_bwd(x, gain, dy):
    r = _rstd(x)
    xh = x * r
    gdy = dy * gain
    dx = r * (gdy - xh * jnp.mean(gdy * xh, axis=-1, keepdims=True))
    return dx, dy * xh


def _rope_fwd(x, cos_t, sin_t):
    return x * cos_t + pltpu.roll(x, 2 * HALF_ROPE, 1) * sin_t


def _rope_bwd(dy, cos_t, sin_t):
    return dy * cos_t + pltpu.roll(dy * sin_t, 2 * HALF_ROPE, 1)


def _acc_rows(ref, val, first):
    s = jnp.sum(val, axis=0, keepdims=True)

    @pl.when(first)
    def _():
        ref[...] = s

    @pl.when(jnp.logical_not(first))
    def _():
        ref[...] += s


def rms_fwd(x, gain, *, name, col_block=0, width=None, deps=()):
    T = x.shape[0]
    width = width or x.shape[1]
    tm = _tile(T, ROW_TILE, 8)

    def body(x_ref, g_ref, o_ref):
        v = x_ref[...]
        o_ref[...] = (v * _rstd(v) * g_ref[...]).astype(BF)

    return _pcall(body, name=name, grid=(T // tm,),
                  ins=[(x, (tm, width), lambda i: (i, col_block)), (gain, (1, width), lambda i: (0, 0))],
                  outs=[((T, width), BF, (tm, width), lambda i: (i, 0))], semantics=("parallel",), deps=deps)[0]


def rms_bwd(x, gain, dy, *, name, col_block=0, dres=None, want_f32=True, want_bf=True, into=None, deps=()):
    T, width = dy.shape
    tm = _tile(T, ROW_TILE, 8)
    has_res = dres is not None

    def body(*refs):
        x_ref, g_ref, dy_ref = refs[:3]
        pos = 3
        res_ref = None
        if has_res:
            res_ref = refs[pos]
            pos += 1
        if into is not None:
            pos += 1
        outs = refs[pos:]
        dx, dg_rows = _rms_bwd(x_ref[...], g_ref[...], dy_ref[...])
        if has_res:
            dx = dx + res_ref[...]
        o = 0
        if want_f32:
            outs[o][...] = dx
            o += 1
        if want_bf:
            outs[o][...] = dx.astype(BF)
            o += 1
        _acc_rows(outs[o], dg_rows, pl.program_id(0) == 0)

    ins = [(x, (tm, width), lambda i: (i, col_block)), (gain, (1, width), lambda i: (0, 0)),
           (dy, (tm, width), lambda i: (i, 0))]
    if has_res:
        ins.append((dres, (tm, width), lambda i: (i, 0)))
    outs = []
    aliases = {}
    if want_f32:
        outs.append(((T, width), F32, (tm, width), lambda i: (i, 0)))
    if want_bf and into is not None:
        ins.append((into, None, None))
        aliases[len(ins) - 1] = len(outs)
        outs.append((into.shape, BF, (tm, width), lambda i: (i, col_block)))
    elif want_bf:
        outs.append(((T, width), BF, (tm, width), lambda i: (i, 0)))
    outs.append(((1, width), F32, (1, width), lambda i: (0, 0)))
    return _pcall(body, name=name, grid=(T // tm,), ins=ins, outs=outs, aliases=aliases, deps=deps)


def mla_prep(proj, q_norm, kv_norm, cos_t, sin_t, *, name):
    T = proj.shape[0]
    tm = _tile(T, ROW_TILE, 8)
    kr_block = (proj.shape[1] - LANES) // LANES

    def body(cq_ref, ckv_ref, kr_ref, qg_ref, kg_ref, cos_ref, sin_ref, qn_ref, kvn_ref, krope_ref):
        cq = cq_ref[...]
        qn_ref[...] = (cq * _rstd(cq) * qg_ref[...]).astype(BF)
        ckv = ckv_ref[...]
        kvn_ref[...] = (ckv * _rstd(ckv) * kg_ref[...]).astype(BF)
        krope_ref[...] = _rope_fwd(kr_ref[...], cos_ref[...], sin_ref[...]).astype(BF)

    return _pcall(
        body, name=name, grid=(T // tm,),
        ins=[(proj, (tm, Q_LORA), lambda i: (i, 0)), (proj, (tm, KV_LORA), lambda i: (i, 1)),
             (proj, (tm, LANES), lambda i: (i, kr_block)),
             (q_norm, (1, Q_LORA), lambda i: (0, 0)), (kv_norm, (1, KV_LORA), lambda i: (0, 0)),
             (cos_t, (tm, LANES), lambda i: (i, 0)), (sin_t, (tm, LANES), lambda i: (i, 0))],
        outs=[((T, Q_LORA), BF, (tm, Q_LORA), lambda i: (i, 0)), ((T, KV_LORA), BF, (tm, KV_LORA), lambda i: (i, 0)),
              ((T, LANES), BF, (tm, LANES), lambda i: (i, 0))],
        semantics=("parallel",))


def _attn_scores(q, k_blk, diagonal):
    s = lax.dot_general(q, k_blk, (((1,), (1,)), ((), ())), preferred_element_type=F32) * ATTN_SCALE
    if diagonal:
        row = lax.broadcasted_iota(jnp.int32, s.shape, 0)
        col = lax.broadcasted_iota(jnp.int32, s.shape, 1)
        s = jnp.where(col <= row, s, -jnp.inf)
    return s


def attn_fwd(q, k, v, *, name):
    T = q.shape[0]
    tq = _tile(T, ATTN_TILE, 8)

    def body(q_ref, k_ref, v_ref, o_ref, lse_ref):
        i = pl.program_id(1)
        qv = q_ref[...]

        def block(kb, carry, diagonal):
            m, l, acc = carry
            start = pl.multiple_of(kb * tq, tq)
            s = _attn_scores(qv, k_ref[pl.ds(start, tq), :], diagonal)
            m_new = jnp.maximum(m, jnp.max(s, axis=-1, keepdims=True))
            alpha = jnp.exp(m - m_new)
            p = jnp.exp(s - m_new)
            l = alpha * l + jnp.sum(p, axis=-1, keepdims=True)
            acc = alpha * acc + jnp.dot(p.astype(BF), v_ref[pl.ds(start, tq), :], preferred_element_type=F32)
            return m_new, l, acc

        init = (jnp.full((tq, 1), -jnp.inf, F32), jnp.zeros((tq, 1), F32), jnp.zeros((tq, V_HEAD), F32))
        carry = lax.fori_loop(0, i, lambda kb, c: block(kb, c, False), init)
        m, l, acc = block(i, carry, True)
        o_ref[...] = acc / l
        lse_ref[...] = jnp.broadcast_to(m + jnp.log(l), (tq, V_HEAD))

    return _pcall(
        body, name=name, grid=(HEADS, T // tq),
        ins=[(q, (tq, HEAD_PAD), lambda h, i: (i, h)), (k, (T, HEAD_PAD), lambda h, i: (0, h)),
             (v, (T, V_HEAD), lambda h, i: (0, h))],
        outs=[((T, MLA_OUT), F32, (tq, V_HEAD), lambda h, i: (i, h)),
              ((T, MLA_OUT), F32, (tq, V_HEAD), lambda h, i: (i, h))], semantics=("parallel", "parallel"))


def attn_bwd(q, k, v, o, lse, do, *, name):
    T = q.shape[0]
    tq = _tile(T, ATTN_TILE, 8)

    def body(q_ref, k_ref, v_ref, o_ref, lse_ref, do_ref, dq_ref, dk_ref, dv_ref):
        i = pl.program_id(1)

        @pl.when(i == 0)
        def _():
            dk_ref[...] = jnp.zeros_like(dk_ref)
            dv_ref[...] = jnp.zeros_like(dv_ref)

        qv = q_ref[...]
        do_t = do_ref[...]
        lse_v = lse_ref[:, 0:1]
        delta = jnp.sum(do_t.astype(F32) * o_ref[...], axis=-1, keepdims=True)

        def block(kb, dq, diagonal):
            start = pl.multiple_of(kb * tq, tq)
            k_blk = k_ref[pl.ds(start, tq), :]
            v_blk = v_ref[pl.ds(start, tq), :]
            p = jnp.exp(_attn_scores(qv, k_blk, diagonal) - lse_v)
            dp = lax.dot_general(do_t, v_blk, (((1,), (1,)), ((), ())), preferred_element_type=F32)
            ds = (p * (dp - delta) * ATTN_SCALE).astype(BF)
            dk_ref[pl.ds(start, tq), :] += lax.dot_general(ds, qv, (((0,), (0,)), ((), ())), preferred_element_type=F32)
            dv_ref[pl.ds(start, tq), :] += lax.dot_general(p.astype(BF), do_t, (((0,), (0,)), ((), ())),
                                                          preferred_element_type=F32)
            return dq + jnp.dot(ds, k_blk, preferred_element_type=F32)

        dq = lax.fori_loop(0, i, lambda kb, c: block(kb, c, False), jnp.zeros((tq, HEAD_PAD), F32))
        dq_ref[...] = block(i, dq, True)

    return _pcall(
        body, name=name, grid=(HEADS, T // tq),
        ins=[(q, (tq, HEAD_PAD), lambda h, i: (i, h)), (k, (T, HEAD_PAD), lambda h, i: (0, h)),
             (v, (T, V_HEAD), lambda h, i: (0, h)), (o, (tq, V_HEAD), lambda h, i: (i, h)),
             (lse, (tq, V_HEAD), lambda h, i: (i, h)), (do, (tq, V_HEAD), lambda h, i: (i, h))],
        outs=[((T, HEADS * HEAD_PAD), F32, (tq, HEAD_PAD), lambda h, i: (i, h)),
              ((T, HEADS * HEAD_PAD), F32, (T, HEAD_PAD), lambda h, i: (0, h)),
              ((T, MLA_OUT), F32, (T, V_HEAD), lambda h, i: (0, h))],
        semantics=("parallel", "arbitrary"))


def mla_bwd_prep(dq, dk, dv, cos_t, sin_t, dproj, *, name):
    T = dq.shape[0]
    tm = _tile(T, ROW_TILE, 8)
    kr_block = (dproj.shape[1] - LANES) // LANES

    def body(dq_ref, dk_ref, dv_ref, cos_ref, sin_ref, dproj_in, dql_ref, dkvl_ref, dkr_ref):
        cos_v, sin_v = cos_ref[...], sin_ref[...]
        kr = jnp.zeros((tm, LANES), F32)
        for h in range(HEADS):
            lo = h * HEAD_PAD
            dql_ref[:, lo:lo + QK_NOPE] = dq_ref[:, lo:lo + QK_NOPE].astype(BF)
            dql_ref[:, lo + QK_NOPE:lo + HEAD_PAD] = _rope_bwd(
                dq_ref[:, lo + QK_NOPE:lo + HEAD_PAD], cos_v, sin_v).astype(BF)
            dkvl_ref[:, lo:lo + QK_NOPE] = dk_ref[:, lo:lo + QK_NOPE].astype(BF)
            dkvl_ref[:, lo + QK_NOPE:lo + HEAD_PAD] = dv_ref[:, h * V_HEAD:(h + 1) * V_HEAD].astype(BF)
            kr = kr + dk_ref[:, lo + QK_NOPE:lo + HEAD_PAD]
        dkr_ref[...] = _rope_bwd(kr, cos_v, sin_v).astype(BF)

    W = HEADS * HEAD_PAD
    return _pcall(
        body, name=name, grid=(T // tm,),
        ins=[(dq, (tm, W), lambda i: (i, 0)), (dk, (tm, W), lambda i: (i, 0)), (dv, (tm, MLA_OUT), lambda i: (i, 0)),
             (cos_t, (tm, LANES), lambda i: (i, 0)), (sin_t, (tm, LANES), lambda i: (i, 0)), (dproj, None, None)],
        outs=[((T, W), BF, (tm, W), lambda i: (i, 0)), ((T, W), BF, (tm, W), lambda i: (i, 0)),
              (dproj.shape, BF, (tm, LANES), lambda i: (i, kr_block))],
        aliases={5: 2}, semantics=("parallel",))


def _group_norm_stats(vg):
    mu = jnp.mean(vg, axis=-1, keepdims=True)
    d = vg - mu
    r = lax.rsqrt(jnp.mean(d * d, axis=-1, keepdims=True) + EPS)
    return d * r, r


def mix_fwd(a, proj, g_mla, g_sgu, v_gain, w_tril, b_full, *, name):
    T = a.shape[0]
    tm = _tile(T, ROW_TILE, CHUNK)
    n_chunk = tm // CHUNK

    def body(a_ref, u_ref, v_ref, gm_ref, gs_ref, vg_ref, w_ref, b_ref, o_ref, s_scr):
        av = a_ref[...]
        o_ref[:, :MLA_OUT] = (av * _rstd(av) * gm_ref[...]).astype(BF)
        for g in range(GROUPS):
            sl = slice(g * CH, (g + 1) * CH)
            vhat, _ = _group_norm_stats(_gelu(v_ref[:, sl]))
            vn = (vhat * vg_ref[:, sl]).astype(BF)
            u = _gelu(u_ref[:, sl])
            for ci in range(n_chunk):
                rs = slice(ci * CHUNK, (ci + 1) * CHUNK)
                y = jnp.dot(w_ref[g], vn[rs], preferred_element_type=F32) + b_ref[:, sl]
                s_scr[rs, sl] = u[rs] * y
        s = s_scr[...]
        o_ref[:, MLA_OUT:] = (s * _rstd(s) * gs_ref[...]).astype(BF)

    return _pcall(
        body, name=name, grid=(T // tm,),
        ins=[(a, (tm, MLA_OUT), lambda i: (i, 0)), (proj, (tm, SGU_OUT), lambda i: (i, 1)),
             (proj, (tm, SGU_OUT), lambda i: (i, 2)), (g_mla, (1, MLA_OUT), lambda i: (0, 0)),
             (g_sgu, (1, SGU_OUT), lambda i: (0, 0)), (v_gain, (1, SGU_OUT), lambda i: (0, 0)),
             (w_tril, (GROUPS, CHUNK, CHUNK), lambda i: (0, 0, 0)), (b_full, (CHUNK, SGU_OUT), lambda i: (0, 0))],
        outs=[((T, MLA_OUT + SGU_OUT), BF, (tm, MLA_OUT + SGU_OUT), lambda i: (i, 0))],
        scratch=[pltpu.VMEM((tm, SGU_OUT), F32)], semantics=("parallel",))[0]


def mix_bwd(dmixed, a, proj, g_mla, g_sgu, v_gain, w_tril, w_tril_t, b_full, *, name, deps=()):
    T = a.shape[0]
    tm = _tile(T, ROW_TILE, CHUNK)
    n_chunk = tm // CHUNK
    uv0 = Q_LORA + KV_LORA

    def body(dm_a_ref, dm_s_ref, a_ref, u_ref, v_ref, gm_ref, gs_ref, vg_ref, w_ref, wt_ref, b_ref,
             da_ref, duv_ref, dgm_ref, dgs_ref, dvg_ref, dw_ref, db_ref, s_scr, y_scr):
        first = pl.program_id(0) == 0
        duv_ref[:, :uv0] = jnp.zeros((tm, uv0), BF)
        duv_ref[:, uv0 + 2 * SGU_OUT:] = jnp.zeros((tm, duv_ref.shape[1] - uv0 - 2 * SGU_OUT), BF)
        da, dgm_rows = _rms_bwd(a_ref[...], gm_ref[...], dm_a_ref[...])
        da_ref[...] = da.astype(BF)
        _acc_rows(dgm_ref, dgm_rows, first)

        for g in range(GROUPS):
            sl = slice(g * CH, (g + 1) * CH)
            vhat, _ = _group_norm_stats(_gelu(v_ref[:, sl]))
            vn = (vhat * vg_ref[:, sl]).astype(BF)
            u = _gelu(u_ref[:, sl])
            for ci in range(n_chunk):
                rs = slice(ci * CHUNK, (ci + 1) * CHUNK)
                y = jnp.dot(w_ref[g], vn[rs], preferred_element_type=F32) + b_ref[:, sl]
                y_scr[rs, sl] = y
                s_scr[rs, sl] = u[rs] * y
        ds, dgs_rows = _rms_bwd(s_scr[...], gs_ref[...], dm_s_ref[...])
        _acc_rows(dgs_ref, dgs_rows, first)
        s_scr[...] = ds

        @pl.when(first)
        def _():
            dw_ref[...] = jnp.zeros_like(dw_ref)
            db_ref[...] = jnp.zeros_like(db_ref)

        for g in range(GROUPS):
            sl = slice(g * CH, (g + 1) * CH)
            upre = u_ref[:, sl]
            vpre = v_ref[:, sl]
            u = _gelu(upre)
            vhat, r = _group_norm_stats(_gelu(vpre))
            gain = vg_ref[:, sl]
            vn = (vhat * gain).astype(BF)
            dsg = s_scr[:, sl]
            duv_ref[:, uv0 + g * CH:uv0 + (g + 1) * CH] = (dsg * y_scr[:, sl] * _gelu_grad(upre)).astype(BF)
            dy = dsg * u
            dyb = dy.astype(BF)
            dvn_parts = []
            for ci in range(n_chunk):
                rs = slice(ci * CHUNK, (ci + 1) * CHUNK)
                dvn_parts.append(jnp.dot(wt_ref[g], dyb[rs], preferred_element_type=F32))
                dw_ref[g] += lax.dot_general(dyb[rs], vn[rs], (((1,), (1,)), ((), ())), preferred_element_type=F32)
                db_ref[:, sl] += jnp.broadcast_to(jnp.sum(dy[rs], axis=-1, keepdims=True), (CHUNK, CH))
            dvn = dvn_parts[0] if n_chunk == 1 else jnp.concatenate(dvn_parts, axis=0)
            _acc_rows(dvg_ref.at[:, sl], dvn * vhat, first)
            dvh = dvn * gain
            dvg = r * (dvh - jnp.mean(dvh, axis=-1, keepdims=True)
                       - vhat * jnp.mean(dvh * vhat, axis=-1, keepdims=True))
            duv_ref[:, uv0 + SGU_OUT + g * CH:uv0 + SGU_OUT + (g + 1) * CH] = (dvg * _gelu_grad(vpre)).astype(BF)

    return _pcall(
        body, name=name, grid=(T // tm,),
        ins=[(dmixed, (tm, MLA_OUT), lambda i: (i, 0)), (dmixed, (tm, SGU_OUT), lambda i: (i, 1)),
             (a, (tm, MLA_OUT), lambda i: (i, 0)), (proj, (tm, SGU_OUT), lambda i: (i, 1)),
             (proj, (tm, SGU_OUT), lambda i: (i, 2)), (g_mla, (1, MLA_OUT), lambda i: (0, 0)),
             (g_sgu, (1, SGU_OUT), lambda i: (0, 0)), (v_gain, (1, SGU_OUT), lambda i: (0, 0)),
             (w_tril, (GROUPS, CHUNK, CHUNK), lambda i: (0, 0, 0)), (w_tril_t, (GROUPS, CHUNK, CHUNK), lambda i: (0, 0, 0)),
             (b_full, (CHUNK, SGU_OUT), lambda i: (0, 0))],
        outs=[((T, MLA_OUT), BF, (tm, MLA_OUT), lambda i: (i, 0)),
              ((T, proj.shape[1]), BF, (tm, proj.shape[1]), lambda i: (i, 0)),
              ((1, MLA_OUT), F32, (1, MLA_OUT), lambda i: (0, 0)), ((1, SGU_OUT), F32, (1, SGU_OUT), lambda i: (0, 0)),
              ((1, SGU_OUT), F32, (1, SGU_OUT), lambda i: (0, 0)),
              ((GROUPS, CHUNK, CHUNK), F32, (GROUPS, CHUNK, CHUNK), lambda i: (0, 0, 0)),
              ((CHUNK, SGU_OUT), F32, (CHUNK, SGU_OUT), lambda i: (0, 0))],
        scratch=[pltpu.VMEM((tm, SGU_OUT), F32), pltpu.VMEM((tm, SGU_OUT), F32)], deps=deps)


def _shift_down(z, n, row):
    return jnp.where(row >= n, pltpu.roll(z, n, 0), 0.0)


def _shift_up(z, n, row, T):
    return jnp.where(row < T - n, pltpu.roll(z, T - n, 0), 0.0)


def conv_fwd(proj, conv_w, *, name):
    T, D3 = proj.shape
    D = D3 // 3
    tn = _tile(D, 256)
    nj = D // tn

    def body(b_ref, c_ref, x_ref, w_ref, o_ref):
        row = lax.broadcasted_iota(jnp.int32, (T, tn), 0)
        z = c_ref[...] * x_ref[...]
        zc = w_ref[2:3, :] * z + w_ref[1:2, :] * _shift_down(z, 1, row) + w_ref[0:1, :] * _shift_down(z, 2, row)
        o_ref[...] = (b_ref[...] * zc).astype(BF)

    return _pcall(
        body, name=name, grid=(nj,),
        ins=[(proj, (T, tn), lambda j: (0, j)), (proj, (T, tn), lambda j: (0, nj + j)),
             (proj, (T, tn), lambda j: (0, 2 * nj + j)), (conv_w, (3, tn), lambda j: (0, j))],
        outs=[((T, D), BF, (T, tn), lambda j: (0, j))], semantics=("parallel",))[0]


def conv_bwd(dg, proj, conv_w, *, name, deps=()):
    T, D3 = proj.shape
    D = D3 // 3
    tn = _tile(D, 256)
    nj = D // tn

    def body(dg_ref, b_ref, c_ref, x_ref, w_ref, dp_ref, dw_ref, dc_scr, dx_scr):
        part = pl.program_id(1)

        @pl.when(part == 0)
        def _():
            row = lax.broadcasted_iota(jnp.int32, (T, tn), 0)
            c, x = c_ref[...], x_ref[...]
            z = c * x
            z1 = _shift_down(z, 1, row)
            z2 = _shift_down(z, 2, row)
            dgv = dg_ref[...]
            zc = w_ref[2:3, :] * z + w_ref[1:2, :] * z1 + w_ref[0:1, :] * z2
            dp_ref[...] = (dgv * zc).astype(BF)
            dzc = dgv * b_ref[...]
            dw_ref[0:1, :] = jnp.sum(dzc * z2, axis=0, keepdims=True)
            dw_ref[1:2, :] = jnp.sum(dzc * z1, axis=0, keepdims=True)
            dw_ref[2:3, :] = jnp.sum(dzc * z, axis=0, keepdims=True)
            dz = (w_ref[2:3, :] * dzc + w_ref[1:2, :] * _shift_up(dzc, 1, row, T)
                  + w_ref[0:1, :] * _shift_up(dzc, 2, row, T))
            dc_scr[...] = (dz * x).astype(BF)
            dx_scr[...] = (dz * c).astype(BF)

        @pl.when(part == 1)
        def _():
            dp_ref[...] = dc_scr[...]

        @pl.when(part == 2)
        def _():
            dp_ref[...] = dx_scr[...]

    return _pcall(
        body, name=name, grid=(nj, 3),
        ins=[(dg, (T, tn), lambda j, p: (0, j)), (proj, (T, tn), lambda j, p: (0, j)),
             (proj, (T, tn), lambda j, p: (0, nj + j)), (proj, (T, tn), lambda j, p: (0, 2 * nj + j)),
             (conv_w, (3, tn), lambda j, p: (0, j))],
        outs=[((T, D3), BF, (T, tn), lambda j, p: (0, p * nj + j)), ((3, D), F32, (3, tn), lambda j, p: (0, j))],
        scratch=[pltpu.VMEM((T, tn), BF), pltpu.VMEM((T, tn), BF)], semantics=("parallel", "arbitrary"), deps=deps)


def loss_bwd(x_parts, gain, target, *, name):
    T, D = target.shape
    tm = _tile(T, ROW_TILE, 8)
    n_x = len(x_parts)

    def body(*refs):
        x_refs = refs[:n_x]
        g_ref, t_ref, dx_ref, dxb_ref, dg_ref, loss_ref = refs[n_x:]
        first = pl.program_id(0) == 0
        xv = jnp.concatenate([r[...] for r in x_refs], axis=-1) if n_x > 1 else x_refs[0][...]
        r = _rstd(xv)
        xh = xv * r
        gain_v = g_ref[...]
        err = xh * gain_v - t_ref[...]
        part = 0.5 * jnp.sum(jnp.mean(err * err, axis=-1, keepdims=True), axis=0, keepdims=True)
        _acc_rows(loss_ref, jnp.broadcast_to(part, (1, LANES)), first)
        dy = err * (1.0 / D)
        gdy = dy * gain_v
        dx = r * (gdy - xh * jnp.mean(gdy * xh, axis=-1, keepdims=True))
        dx_ref[...] = dx
        dxb_ref[...] = dx.astype(BF)
        _acc_rows(dg_ref, dy * xh, first)

    return _pcall(
        body, name=name, grid=(T // tm,),
        ins=[(p, (tm, D // n_x), lambda i: (i, 0)) for p in x_parts]
        + [(gain, (1, D), lambda i: (0, 0)), (target, (tm, D), lambda i: (i, 0))],
        outs=[((T, D), F32, (tm, D), lambda i: (i, 0)), ((T, D), BF, (tm, D), lambda i: (i, 0)),
              ((1, D), F32, (1, D), lambda i: (0, 0)), ((1, LANES), F32, (1, LANES), lambda i: (0, 0))])


def _adamw(g, w, m, v):
    m = ADAM_B1 * m + (1.0 - ADAM_B1) * g
    v = ADAM_B2 * v + (1.0 - ADAM_B2) * (g * g)
    m_hat = m / ADAM_C1
    v_hat = v / ADAM_C2
    delta = -ADAM_LR * (m_hat / (jnp.sqrt(v_hat) + ADAM_EPS) + ADAM_WD * w)
    return delta, m, v


def adam_flat(g, w, m, v, *, name):
    def body(g_ref, w_ref, m_ref, v_ref, d_ref, nm_ref, nv_ref):
        d, nm, nv = _adamw(g_ref[...], w_ref[...], m_ref[...], v_ref[...])
        d_ref[...] = d
        nm_ref[...] = nm
        nv_ref[...] = nv

    blk = g.shape
    zero = lambda: (0, 0)
    return _pcall(body, name=name, grid=(),
                  ins=[(t, blk, zero) for t in (g, w, m, v)],
                  outs=[(blk, F32, blk, zero)] * 3)


def _chip_slots():
    x, y, c = lax.axis_index("x"), lax.axis_index("y"), lax.axis_index("c")
    chips = [(1 - x, y), (x, 1 - y), (1 - x, 1 - y)]
    return x, y, c, chips


def device_index():
    x, y, c, chips = _chip_slots()
    return jnp.stack([4 * x + 2 * y + c, 2 * x + y] + [4 * cx + 2 * cy + c for cx, cy in chips]
                     + [2 * cx + cy for cx, cy in chips]).astype(jnp.int32)


def _job_rows(R, C, n_steps):
    if n_steps is None:
        n_steps = max(1, R * C // STREAM_BLOCK_ELEMS)
    n_blk = max([d for d in range(1, n_steps + 1) if R % d == 0 and (R // d) % 16 == 0] or [1])
    return R // n_blk, n_blk


def run_job(job, *, index, name, deps=()):
    jb = job(None)
    n_in = len(jb["ins"])

    def body(idx_ref, *refs):
        jb["fn"](refs[:n_in], refs[n_in:n_in + len(jb["outs"])])

    return _pcall(body, name=name, grid=(jb["n_blk"],), ins=jb["ins"], outs=jb["outs"], prefetch=index,
                  aliases={1 + a: o for a, o in jb["aliases"].items()}, semantics=("parallel",), deps=deps)


def adam_job(gs, a_buf, b_buf, w, m, v, layer, prev):
    L, R, C = w.shape

    def build(n_steps):
        tr, n_blk = _job_rows(R, C, n_steps)
        blk = (None, tr, C)
        row = lambda t: jnp.minimum(t, n_blk - 1)
        ins = [(gs, blk, lambda t, s: (s[0], row(t), 0)), (a_buf, blk, lambda t, s: (s[1], row(t), 0))]
        ins += [(b_buf, blk, lambda t, s, j=j: (j, row(t), 0)) for j in range(3)]
        ins += [(p, blk, lambda t, s: (layer, row(t), 0)) for p in (w, m, v)]
        ins += [(p, None, None) for p in (prev or [])]

        def fn(i, o):
            g = ((((i[0][...].astype(F32) + i[1][...].astype(F32)) + i[2][...].astype(F32))
                  + i[3][...].astype(F32)) + i[4][...].astype(F32))
            d, nm, nv = _adamw(g, i[5][...], i[6][...], i[7][...])
            o[0][...] = g
            o[1][...] = d
            o[2][...] = nm
            o[3][...] = nv

        return dict(ins=ins, outs=[((L, R, C), F32, blk, lambda t, s: (layer, row(t), 0))] * 4, fn=fn,
                    aliases={8 + o: o for o in range(4)} if prev else {}, n_blk=n_blk)

    return build


def pair_job(gs, a_buf):
    _, R, C = gs.shape

    def build(n_steps):
        tr, n_blk = _job_rows(R, C, n_steps)
        blk = (None, tr, C)
        row = lambda t: jnp.minimum(t, n_blk - 1)
        ins = [(gs, blk, lambda t, s, j=j: (s[2 + j], row(t), 0)) for j in range(3)]
        ins += [(a_buf, blk, lambda t, s, j=j: (s[5 + j], row(t), 0)) for j in range(3)]

        def fn(i, o):
            for j in range(3):
                o[0][j] = (i[j][...].astype(F32) + i[3 + j][...].astype(F32)).astype(BF)

        return dict(ins=ins, outs=[((3, R, C), BF, (3, tr, C), lambda t, s: (0, row(t), 0))], fn=fn, aliases={},
                    n_blk=n_blk)

    return build


def reduce_sum(gs, a_buf, b_buf, *, name):
    _, R, C = gs.shape
    tr = _tile(R, 256, 16)
    x, y, c, _ = _chip_slots()
    idx = jnp.stack([4 * x + 2 * y + c, 2 * x + y]).astype(jnp.int32)

    def body(idx_ref, g_ref, a_ref, b0_ref, b1_ref, b2_ref, o_ref):
        o_ref[...] = ((((g_ref[...].astype(F32) + a_ref[...].astype(F32)) + b0_ref[...].astype(F32))
                       + b1_ref[...].astype(F32)) + b2_ref[...].astype(F32))

    blk3 = (None, tr, C)
    return _pcall(body, name=name, grid=(R // tr,),
                  ins=[(gs, blk3, lambda i, s: (s[0], i, 0)), (a_buf, blk3, lambda i, s: (s[1], i, 0)),
                       (b_buf, blk3, lambda i, s: (0, i, 0)), (b_buf, blk3, lambda i, s: (1, i, 0)),
                       (b_buf, blk3, lambda i, s: (2, i, 0))],
                  outs=[((R, C), F32, (tr, C), lambda i, s: (i, 0))], prefetch=idx, semantics=("parallel",))[0]


def adam_rows(g, w, m, v, *, name):
    R, C = g.shape
    tr = _tile(R, 256, 8)

    def body(g_ref, w_ref, m_ref, v_ref, d_ref, nm_ref, nv_ref):
        d, nm, nv = _adamw(g_ref[...], w_ref[...], m_ref[...], v_ref[...])
        d_ref[...] = d
        nm_ref[...] = nm
        nv_ref[...] = nv

    spec = ((tr, C), lambda i: (i, 0))
    return _pcall(body, name=name, grid=(R // tr,), ins=[(t, *spec) for t in (g, w, m, v)],
                  outs=[((R, C), F32, *spec)] * 3, semantics=("parallel",))


def sum_rows8(gathered, rows, *, name):
    W = gathered.shape[1]

    def body(g_ref, o_ref):
        acc = g_ref[0:rows, :]
        for d in range(1, N_DEV):
            acc = acc + g_ref[d * rows:(d + 1) * rows, :]
        o_ref[...] = acc

    return _pcall(body, name=name, grid=(), ins=[(gathered, gathered.shape, lambda: (0, 0))],
                  outs=[((rows, W), F32, (rows, W), lambda: (0, 0))])[0]


HBM_SPEC = pl.BlockSpec(memory_space=pltpu.HBM)
SEM_SPEC = pl.BlockSpec(memory_space=pltpu.SEMAPHORE)
ANY_SPEC = pl.BlockSpec(memory_space=pl.ANY)
DATAFLOW = pltpu.SideEffectType.DATAFLOW_SIDE_EFFECTING


def _in_hbm(v):
    return pltpu.with_memory_space_constraint(v, pltpu.HBM)


def _slot(p):
    return 4 * p[0] + 2 * p[1] + p[2]


def _gather_peers():
    x, y, c, chips = _chip_slots()
    return (x, y, c), [(x, y, 1 - c)] + [(*chip, c) for chip in chips]


def gather_start(groups, after, *, name):
    flat = [s for g in groups for s in g]
    n, n_g = len(flat), len(groups)
    where = [(gi, ti) for gi, g in enumerate(groups) for ti in range(len(g))]

    def body(*refs):
        src, land = refs[:n], refs[n:2 * n]
        sems = refs[2 * n + 1:2 * n + 1 + 2 * n_g]
        me, peers = _gather_peers()
        for t in range(n):
            gi, ti = where[t]
            for k, to in enumerate(peers):
                pltpu.make_async_remote_copy(
                    src_ref=src[t], dst_ref=land[t].at[_slot(me)], send_sem=sems[2 * gi].at[4 * ti + k],
                    recv_sem=sems[2 * gi + 1].at[4 * ti + k], device_id=to, device_id_type=MESH).start()
        refs[-1][...] = jnp.zeros_like(refs[-1])

    out_shape = []
    for g in groups:
        out_shape += [pltpu.SemaphoreType.DMA((4 * len(g),)), pltpu.SemaphoreType.DMA((4 * len(g),))]
    out_shape += [pltpu.HBM(s.shape, s.dtype) for s in flat]
    out_shape += [pltpu.HBM((N_DEV,) + s.shape, s.dtype) for s in flat]
    out_shape += [jax.ShapeDtypeStruct((8, LANES), F32)]
    aliases = {t: 2 * n_g + t for t in range(n)}
    aliases.update({n + t: 2 * n_g + n + t for t in range(n)})
    res = pl.pallas_call(
        body, name=name, out_shape=out_shape, in_specs=[HBM_SPEC] * (2 * n) + [ANY_SPEC],
        out_specs=[SEM_SPEC] * (2 * n_g) + [HBM_SPEC] * (2 * n) + [pl.BlockSpec(memory_space=pltpu.VMEM)],
        input_output_aliases=aliases, compiler_params=pltpu.CompilerParams(has_side_effects=DATAFLOW),
    )(*[_in_hbm(s) for s in flat], *[_in_hbm(lax.empty((N_DEV,) + s.shape, s.dtype)) for s in flat], after)
    out, off = [], 0
    for gi, g in enumerate(groups):
        k = len(g)
        out.append((res[2 * gi], res[2 * gi + 1], res[2 * n_g + off:2 * n_g + off + k],
                    res[2 * n_g + n + off:2 * n_g + n + off + k]))
        off += k
    return out, res[-1]


def gather_wait(started, after, *, name):
    send_sems, recv_sems, srcs, lands = started
    n = len(srcs)
    after = list(after)

    def body(*refs):
        src, land = refs[:n], refs[n:2 * n]
        send, recv = refs[2 * n], refs[2 * n + 1]
        _, peers = _gather_peers()
        for t in range(n):
            for k, frm in enumerate(peers):
                cp = pltpu.make_async_remote_copy(
                    src_ref=src[t], dst_ref=land[t].at[_slot(frm)], send_sem=send.at[4 * t + k],
                    recv_sem=recv.at[4 * t + k],
                    device_id=frm, device_id_type=MESH)
                cp.wait_send()
                cp.wait_recv()

    res = pl.pallas_call(
        body, name=name,
        out_shape=[pltpu.HBM(s.shape, s.dtype) for s in srcs] + [pltpu.HBM(l.shape, l.dtype) for l in lands],
        in_specs=[HBM_SPEC] * (2 * n) + [SEM_SPEC, SEM_SPEC] + [ANY_SPEC] * len(after),
        out_specs=[HBM_SPEC] * (2 * n), input_output_aliases={t: t for t in range(2 * n)},
        compiler_params=pltpu.CompilerParams(has_side_effects=DATAFLOW),
    )(*srcs, *lands, send_sems, recv_sems, *after)
    return res[:n], res[n:]


def place_own(src, land, *, name):
    R, C = src.shape
    tr = _tile(R, 512, 16)
    x, y, c, _ = _chip_slots()
    idx = jnp.stack([4 * x + 2 * y + c]).astype(jnp.int32)

    def body(idx_ref, s_ref, land_ref, o_ref):
        o_ref[...] = s_ref[...]

    return _pcall(body, name=name, grid=(R // tr,),
                  ins=[(src, (tr, C), lambda i, s: (i, 0)), (land, None, None)],
                  outs=[(land.shape, land.dtype, (None, tr, C), lambda i, s: (s[0], i, 0))],
                  prefetch=idx, aliases={2: 0}, semantics=("parallel",))[0]


def gather_finish(srcs, lands, *, name):
    n = len(srcs)

    def body(*refs):
        land = refs[n:2 * n]
        send_sems, recv_sems = refs[2 * n:]
        x, y, c, chips = _chip_slots()
        me, sibling = (x, y, c), (x, y, 1 - c)

        def copy(t, j, block, to):
            return pltpu.make_async_remote_copy(
                src_ref=land[t].at[_slot(block)], dst_ref=land[t].at[_slot(block)], send_sem=send_sems.at[t, j],
                recv_sem=recv_sems.at[t, j], device_id=to, device_id_type=MESH)

        sends = [copy(t, j, (*chip, c), sibling) for t in range(n) for j, chip in enumerate(chips)]
        for cp in sends:
            cp.start()
        for t in range(n):
            for j, chip in enumerate(chips):
                copy(t, j, (*chip, 1 - c), me).wait_recv()
        for cp in sends:
            cp.wait_send()

    passed = pl.pallas_call(
        body, name=name, out_shape=[jax.ShapeDtypeStruct(l.shape, l.dtype) for l in lands],
        in_specs=[ANY_SPEC] * n, out_specs=[ANY_SPEC] * n,
        input_output_aliases={t: t for t in range(n)},
        scratch_shapes=[pltpu.SemaphoreType.DMA((n, 3)), pltpu.SemaphoreType.DMA((n, 3))],
    )(*lands)
    return [place_own(s, l, name=f"{name}_own{t}") for t, (s, l) in enumerate(zip(srcs, passed))]


def chips_start(pairs, *, name):
    n = len(pairs)

    def body(*refs):
        src, land = refs[:n], refs[n:2 * n]
        send, recv = refs[2 * n], refs[2 * n + 1]
        token = refs[-1]
        x, y, c, chips = _chip_slots()
        for t in range(n):
            for j, chip in enumerate(chips):
                pltpu.make_async_remote_copy(
                    src_ref=src[t].at[j], dst_ref=land[t].at[j], send_sem=send.at[3 * t + j],
                    recv_sem=recv.at[3 * t + j], device_id=(*chip, c), device_id_type=MESH).start()
        token[...] = jnp.zeros_like(token)

    res = pl.pallas_call(
        body, name=name,
        out_shape=[pltpu.SemaphoreType.DMA((3 * n,)), pltpu.SemaphoreType.DMA((3 * n,))]
        + [pltpu.HBM(p.shape, p.dtype) for p in pairs] * 2 + [jax.ShapeDtypeStruct((8, LANES), F32)],
        in_specs=[HBM_SPEC] * (2 * n),
        out_specs=[SEM_SPEC, SEM_SPEC] + [HBM_SPEC] * (2 * n) + [pl.BlockSpec(memory_space=pltpu.VMEM)],
        input_output_aliases={t: 2 + t for t in range(2 * n)},
        compiler_params=pltpu.CompilerParams(has_side_effects=DATAFLOW),
    )(*[_in_hbm(p) for p in pairs], *[_in_hbm(lax.empty(p.shape, p.dtype)) for p in pairs])
    return res[0], res[1], res[2:2 + n], res[2 + n:2 + 2 * n], res[-1]


def chips_wait(started, after, *, name):
    send_sems, recv_sems, srcs, lands, _ = started
    n = len(srcs)

    def body(*refs):
        src, land = refs[:n], refs[n:2 * n]
        send, recv = refs[2 * n], refs[2 * n + 1]
        x, y, c, chips = _chip_slots()
        for t in range(n):
            for j, chip in enumerate(chips):
                cp = pltpu.make_async_remote_copy(
                    src_ref=src[t].at[j], dst_ref=land[t].at[j], send_sem=send.at[3 * t + j],
                    recv_sem=recv.at[3 * t + j], device_id=(*chip, c), device_id_type=MESH)
                cp.wait_send()
                cp.wait_recv()

    res = pl.pallas_call(
        body, name=name, out_shape=[pltpu.HBM(s.shape, s.dtype) for s in srcs] * 2,
        in_specs=[HBM_SPEC] * (2 * n) + [SEM_SPEC, SEM_SPEC, ANY_SPEC], out_specs=[HBM_SPEC] * (2 * n),
        input_output_aliases={t: t for t in range(2 * n)},
        compiler_params=pltpu.CompilerParams(has_side_effects=DATAFLOW),
    )(*srcs, *lands, send_sems, recv_sems, after)
    return res[n:]


def _sibling_copies(src, land, send, recv, n):
    x, y, c, _ = _chip_slots()
    return [pltpu.make_async_remote_copy(
        src_ref=src[t].at[4 * (q // 2) + 2 * (q % 2) + (1 - c)], dst_ref=land[t].at[q], send_sem=send.at[4 * t + q],
        recv_sem=recv.at[4 * t + q], device_id=(x, y, 1 - c), device_id_type=MESH)
        for t in range(n) for q in range(4)]


def sibling_start(gs, *, name):
    n = len(gs)

    def body(*refs):
        for cp in _sibling_copies(refs[:n], refs[n:2 * n], refs[2 * n], refs[2 * n + 1], n):
            cp.start()
        refs[-1][...] = jnp.zeros_like(refs[-1])

    lands = [lax.empty((4,) + g.shape[1:], g.dtype) for g in gs]
    res = pl.pallas_call(
        body, name=name,
        out_shape=[pltpu.SemaphoreType.DMA((4 * n,)), pltpu.SemaphoreType.DMA((4 * n,))]
        + [pltpu.HBM(g.shape, g.dtype) for g in gs] + [pltpu.HBM(l.shape, l.dtype) for l in lands]
        + [jax.ShapeDtypeStruct((8, LANES), F32)],
        in_specs=[HBM_SPEC] * (2 * n),
        out_specs=[SEM_SPEC, SEM_SPEC] + [HBM_SPEC] * (2 * n) + [pl.BlockSpec(memory_space=pltpu.VMEM)],
        input_output_aliases={t: 2 + t for t in range(2 * n)},
        compiler_params=pltpu.CompilerParams(has_side_effects=DATAFLOW),
    )(*[_in_hbm(g) for g in gs], *[_in_hbm(l) for l in lands])
    return res[0], res[1], res[2:2 + n], res[2 + n:2 + 2 * n], res[-1]


def sibling_wait(started, after, *, name):
    send_sems, recv_sems, srcs, lands, _ = started
    n = len(srcs)

    def body(*refs):
        for cp in _sibling_copies(refs[:n], refs[n:2 * n], refs[2 * n], refs[2 * n + 1], n):
            cp.wait_send()
            cp.wait_recv()

    res = pl.pallas_call(
        body, name=name,
        out_shape=[pltpu.HBM(s.shape, s.dtype) for s in srcs] + [pltpu.HBM(l.shape, l.dtype) for l in lands],
        in_specs=[HBM_SPEC] * (2 * n) + [SEM_SPEC, SEM_SPEC, ANY_SPEC], out_specs=[HBM_SPEC] * (2 * n),
        input_output_aliases={t: t for t in range(2 * n)},
        compiler_params=pltpu.CompilerParams(has_side_effects=DATAFLOW),
    )(*srcs, *lands, send_sems, recv_sems, after)
    return res[:n], res[n:]


def _rope_slab(cols):
    z = jnp.zeros(cols.shape[:-1] + (HALF_ROPE,), cols.dtype)
    return jnp.concatenate([cols[..., :HALF_ROPE], z, cols[..., HALF_ROPE:], z], axis=-1)


def _rope_unslab(slab):
    return jnp.concatenate([slab[..., :HALF_ROPE], slab[..., 2 * HALF_ROPE:3 * HALF_ROPE]], axis=-1)


def _pack_w_in_t(wt_g):
    s, c, d = wt_g.shape
    w = wt_g.reshape(s * c, d)
    c2, c3 = Q_LORA + KV_LORA, Q_LORA + KV_LORA + QK_ROPE
    r = w[c2:c3]
    z = jnp.zeros((HALF_ROPE, d), w.dtype)
    return jnp.concatenate([w[:c2], w[c3:], r[:HALF_ROPE], z, r[HALF_ROPE:], z], axis=0)


def unpack_w_in_t_grad(dwt, *, name):
    n_rows, d = dwt.shape
    kr = QK_ROPE
    n_out_rows = n_rows - kr
    blk = n_out_rows // 7
    assert blk * 7 == n_out_rows and blk % kr == 0 and n_rows % (2 * kr) == 0
    kr_row = Q_LORA + KV_LORA
    k_mix = kr_row // blk
    off = kr_row - k_mix * blk
    slab_block = (n_rows - 2 * kr) // (2 * kr)

    def body(prev_ref, in_ref, slab_ref, o_ref):
        k = pl.program_id(0)

        @pl.when(k < k_mix)
        def _():
            o_ref[...] = in_ref[...]

        @pl.when(k == k_mix)
        def _():
            o_ref[:off, :] = in_ref[:off, :]
            o_ref[off:off + HALF_ROPE, :] = slab_ref[:HALF_ROPE, :]
            o_ref[off + HALF_ROPE:off + kr, :] = slab_ref[2 * HALF_ROPE:3 * HALF_ROPE, :]
            o_ref[off + kr:, :] = in_ref[off:blk - kr, :]

        @pl.when(k > k_mix)
        def _():
            o_ref[:kr, :] = prev_ref[blk - kr:, :]
            o_ref[kr:, :] = in_ref[:blk - kr, :]

    out = _pcall(body, name=name, grid=(7,),
                 ins=[(dwt, (blk, d), lambda k: (jnp.maximum(k - 1, 0), 0)), (dwt, (blk, d), lambda k: (k, 0)),
                      (dwt, (2 * kr, d), lambda k: (slab_block, 0))],
                 outs=[((n_out_rows, d), dwt.dtype, (blk, d), lambda k: (k, 0))], semantics=("parallel",))[0]
    return out.reshape(N_DEV, n_out_rows // N_DEV, d)


def _rope_tables(positions):
    inv_freq = ROPE_BASE ** (-jnp.arange(0, QK_ROPE, 2, dtype=F32) / QK_ROPE)
    ang = positions.astype(F32)[:, None] * inv_freq
    cos, sin = jnp.cos(ang), jnp.sin(ang)
    z = jnp.zeros_like(cos)
    return jnp.concatenate([cos, z, cos, z], axis=-1), jnp.concatenate([-sin, z, sin, z], axis=-1)


def _mlp_up(x, gain, w1, tag):
    hn = rms_fwd(x, gain, name=f"mlp{tag}_norm")

    def act_epi(acc):
        a = jnp.maximum(acc, 0.0)
        return a, a * a

    T = x.shape[0]
    F = w1.shape[0] * w1.shape[2]
    a, act = mm(hn, w1, name=f"mlp{tag}_up", outs=[((T, F), BF, None), ((T, F), BF, None)], epi=act_epi)
    return hn, a, act


def _mlp_down(x, act, w2, tag, part=0):
    n = w2.shape[1]
    bm = _tile(x.shape[0], MM_TILE)
    bn = _tile(n, MM_TILE)
    per = n // bn
    return mm(act, w2, name=f"mlp{tag}_down{part}", out=((x.shape[0], n), F32), bm=bm, bn=bn,
              epi=lambda acc, r: (acc + r[...],), epi_ins=[(x, (bm, bn), lambda i, j, k: (i, part * per + j))])


def _mlp_bwd_weights(w1, w2, saved, dxb, tag):
    hn, a, act = saved
    T, D = dxb.shape
    F = a.shape[1]
    bm = _tile(T, MM_TILE)
    bn = _tile(F, min(MM_TILE, w1.shape[2]))
    dhid = mm(dxb, w2, tb=True, name=f"mlp{tag}_dhid", out=((T, F), BF), bm=bm, bn=bn,
              epi=lambda acc, a_ref: (2.0 * a_ref[...].astype(F32) * acc,),
              epi_ins=[(a, (bm, bn), lambda i, j, k: (i, j))])
    dw2 = mm(act, dxb, ta=True, name=f"mlp{tag}_dw2", out=((F, D), BF))
    dw1 = mm(hn, dhid, ta=True, name=f"mlp{tag}_dw1", out=(w1.shape, BF))
    return dhid, dw1, dw2.reshape(N_DEV, F // N_DEV, D)


def _reduce_begin(grads, tag):
    return sibling_start(grads, name=f"reduce_sibling_start_{tag}")


def _reduce_continue(sib, after, tag, index):
    grads, a_bufs = sibling_wait(sib, after, name=f"reduce_sibling_wait_{tag}")
    pairs = [run_job(pair_job(g, a), index=index, name=f"pair_sum_{tag}{t}")[0]
             for t, (g, a) in enumerate(zip(grads, a_bufs))]
    return grads, a_bufs, chips_start(pairs, name=f"reduce_chips_start_{tag}")


def kernel(x, positions, e_norm_mix, e_w_in, e_q_norm, e_w_uq, e_kv_norm, e_w_ukv, e_v_norm, e_sgu_w, e_sgu_b, e_mla_out_norm, e_sgu_out_norm, e_w_out, o_norm_mix, o_w_in, o_conv_w, o_w_out, mlp_norm, mlp_w1, mlp_w2, final_norm, loss_target, m_e_norm_mix, m_e_w_in, m_e_q_norm, m_e_w_uq, m_e_kv_norm, m_e_w_ukv, m_e_v_norm, m_e_sgu_w, m_e_sgu_b, m_e_mla_out_norm, m_e_sgu_out_norm, m_e_w_out, m_o_norm_mix, m_o_w_in, m_o_conv_w, m_o_w_out, m_mlp_norm, m_mlp_w1, m_mlp_w2, m_final_norm, v_e_norm_mix, v_e_w_in, v_e_q_norm, v_e_w_uq, v_e_kv_norm, v_e_w_ukv, v_e_v_norm, v_e_sgu_w, v_e_sgu_b, v_e_mla_out_norm, v_e_sgu_out_norm, v_e_w_out, v_o_norm_mix, v_o_w_in, v_o_conv_w, v_o_w_out, v_mlp_norm, v_mlp_w1, v_mlp_w2, v_final_norm):
    T, D = x.shape[1], x.shape[2]
    d_shard = o_norm_mix.shape[1]
    x0 = x[0]
    target = loss_target[0]
    me = 4 * lax.axis_index("x") + 2 * lax.axis_index("y") + lax.axis_index("c")

    bf = lambda s: s.astype(BF)
    gather_groups = [[bf(jnp.transpose(e_w_in[0])), bf(e_w_uq[0]), bf(e_w_ukv[0])], [bf(e_w_out[0]), bf(mlp_w1[0])],
                     [bf(mlp_w2[0]), bf(o_w_in[0])], [bf(o_w_out[0]), bf(mlp_w1[1])], [bf(mlp_w2[1])]]
    small_rows = jnp.concatenate([o_norm_mix, o_conv_w[0], jnp.zeros((4, d_shard), F32)], axis=0)
    gather_groups[0].insert(0, small_rows)
    started, start_token = gather_start(gather_groups[:1], x0, name="gather_start0")
    started_rest, rest_token = gather_start(gather_groups[1:], start_token, name="gather_start1")
    started += started_rest

    def gathered(gi, after):
        srcs, lands = gather_wait(started[gi], after, name=f"gather_wait{gi}")
        return gather_finish(srcs, lands, name=f"gather_finish{gi}")

    w_tril = jnp.tril(e_sgu_w[0])
    w_tril_b = w_tril.astype(BF)
    w_tril_tb = jnp.swapaxes(w_tril, 1, 2).astype(BF)
    b_full = jnp.repeat(e_sgu_b[0].T, CH, axis=1)
    v_gain = e_v_norm[0].reshape(1, SGU_OUT)
    cos_t, sin_t = _rope_tables(positions[0])
    mlp_gain = [mlp_norm[0:1], mlp_norm[1:2]]
    final_gain = final_norm.reshape(1, D)

    h0 = rms_fwd(x0, e_norm_mix, name="e_norm", deps=[rest_token])
    small_g, g_w_in_t, g_w_uq, w_ukv = gathered(0, [h0, cos_t, sin_t, w_tril_b, w_tril_tb, b_full])
    o_norm_full = small_g[:, 0, :].reshape(1, D)
    conv_w_full = jnp.transpose(small_g[:, 1:4, :], (1, 0, 2)).reshape(3, D)
    w_in_t = _pack_w_in_t(g_w_in_t)
    w_uq = jnp.concatenate([g_w_uq[..., :QK_NOPE], _rope_slab(g_w_uq[..., QK_NOPE:])], axis=-1)
    proj = mm(h0, w_in_t, tb=True, name="e_in", out=((T, w_in_t.shape[0]), F32), bn=_tile(w_in_t.shape[0], 640))
    qn, kvn, krope = mla_prep(proj, e_q_norm, e_kv_norm, cos_t, sin_t, name="mla_prep")
    bm = _tile(T, MM_TILE)

    def q_epi(acc, cos_ref, sin_ref):
        return (jnp.concatenate([acc[:, :QK_NOPE], _rope_fwd(acc[:, QK_NOPE:], cos_ref[...], sin_ref[...])], axis=-1),)

    q = mm(qn, w_uq, name="mla_q", out=((T, HEADS * HEAD_PAD), BF), bm=bm, bn=HEAD_PAD, epi=q_epi,
           epi_ins=[(cos_t, (bm, LANES), lambda i, j, k: (i, 0)), (sin_t, (bm, LANES), lambda i, j, k: (i, 0))])

    def kv_epi(acc, kr_ref):
        return jnp.concatenate([acc[:, :QK_NOPE].astype(BF), kr_ref[...]], axis=-1), acc[:, QK_NOPE:]

    k, v = mm(kvn, w_ukv, name="mla_kv", bm=bm, bn=HEAD_PAD, epi=kv_epi,
              outs=[((T, HEADS * HEAD_PAD), BF, HEAD_PAD), ((T, MLA_OUT), BF, V_HEAD)],
              epi_ins=[(krope, (bm, LANES), lambda i, j, k: (i, 0))])
    attn, attn_lse = attn_fwd(q, k, v, name="attn_fwd")
    mixed = mix_fwd(attn, proj, e_mla_out_norm, e_sgu_out_norm, v_gain, w_tril_b, b_full, name="mix_fwd")
    bn = _tile(D, MM_TILE)
    g_w_out_e, w1_0 = gathered(1, [mixed])
    w_out_e = g_w_out_e.reshape(-1, D)
    x1 = mm(mixed, w_out_e, name="e_out", out=((T, D), F32), bm=bm, bn=bn,
            epi=lambda acc, r: (acc + r[...],), epi_ins=[(x0, (bm, bn), lambda i, j, k: (i, j))])
    hn0, a0, act0 = _mlp_up(x1, mlp_gain[0], w1_0, 0)
    g_w2_0, g_w_in_o = gathered(2, [act0])
    w2_0 = g_w2_0.reshape(-1, D)
    x2 = _mlp_down(x1, act0, w2_0, 0)
    ho = rms_fwd(x2, o_norm_full, name="o_norm")
    proj_o = mm(ho, g_w_in_o, name="o_in", out=((T, 3 * D), F32))
    gated = conv_fwd(proj_o, conv_w_full, name="conv_fwd")
    g_w_out_o, w1_1 = gathered(3, [gated])
    w_out_o = g_w_out_o.reshape(-1, D)
    x3 = mm(gated, w_out_o, name="o_out", out=((T, D), F32), bm=bm, bn=bn,
            epi=lambda acc, r: (acc + r[...],), epi_ins=[(x2, (bm, bn), lambda i, j, k: (i, j))])
    hn1, a1, act1 = _mlp_up(x3, mlp_gain[1], w1_1, 1)
    (g_w2_1,) = gathered(4, [act1])
    w2_1 = g_w2_1.reshape(-1, D)
    x4 = _mlp_down(x3, act1, w2_1, 1)
    w1, w2 = [w1_0, w1_1], [w2_0, w2_1]

    dx4, dx4b, d_final, loss_part = loss_bwd([x4], final_gain, target, name="loss_bwd")

    hosted = dict(job_index=device_index())
    dhid1, dw1_1, dw2_1 = _mlp_bwd_weights(w1[1], w2[1], (hn1, a1, act1), dx4b, 1)
    sib_r0 = _reduce_begin([dw1_1, dw2_1], "r0")
    dhn1 = mm(dhid1, w1[1], tb=True, name="mlp1_dhn", out=((T, D), F32), deps=[sib_r0[-1]])
    grads_r0, a_r0 = sibling_wait(sib_r0, dhn1, name="reduce_sibling_wait_r0")
    dx3, dx3b, d_mlp1 = rms_bwd(x3, mlp_gain[1], dhn1, dres=dx4, name="mlp1_norm_bwd")

    dgated, ((pair_r0a,),) = mm(dx3b, w_out_o, tb=True, name="o_out_dx", out=((T, D), F32),
                                jobs=[pair_job(grads_r0[0], a_r0[0])], **hosted)
    dw_out_o, ((pair_r0b,),) = mm(gated, dx3b, ta=True, name="o_out_dw", out=((D, D), BF),
                                  jobs=[pair_job(grads_r0[1], a_r0[1])], **hosted)
    st_r0 = chips_start([pair_r0a, pair_r0b], name="reduce_chips_start_r0")
    dproj_o, dconv_full = conv_bwd(dgated, proj_o, conv_w_full, name="conv_bwd", deps=[st_r0[-1]])
    dw_in_o = mm(ho, dproj_o, ta=True, name="o_in_dw", out=(g_w_in_o.shape, BF))
    sib_r1 = _reduce_begin([dw_out_o.reshape(g_w_out_o.shape), dw_in_o], "r1")
    dho = mm(dproj_o, g_w_in_o, tb=True, name="o_in_dx", out=((T, D), F32), deps=[sib_r1[-1]])
    grads_r1, a_r1 = sibling_wait(sib_r1, dho, name="reduce_sibling_wait_r1")
    dx2, dx2b, d_onorm_full = rms_bwd(x2, o_norm_full, dho, dres=dx3, name="o_norm_bwd")

    d_ff = a0.shape[1]
    bm_h, bn_h = _tile(T, MM_TILE), _tile(d_ff, min(MM_TILE, w1[0].shape[2]))
    dhid0, ((pair_r1a,), (pair_r1b,)) = mm(
        dx2b, w2[0], tb=True, name="mlp0_dhid", out=((T, d_ff), BF), bm=bm_h, bn=bn_h,
        epi=lambda acc, a_ref: (2.0 * a_ref[...].astype(F32) * acc,),
        epi_ins=[(a0, (bm_h, bn_h), lambda i, j, k: (i, j))],
        jobs=[pair_job(grads_r1[0], a_r1[0]), pair_job(grads_r1[1], a_r1[1])], **hosted)
    st_r1 = chips_start([pair_r1a, pair_r1b], name="reduce_chips_start_r1")
    dw2_0 = mm(act0, dx2b, ta=True, name="mlp0_dw2", out=((d_ff, D), BF), deps=[st_r1[-1]])
    b_r0 = chips_wait(st_r0, dw2_0, name="reduce_chips_wait_r0")
    dw1_0, (r_w1, r_w2) = mm(
        hn0, dhid0, ta=True, name="mlp0_dw1", out=(w1[0].shape, BF),
        jobs=[adam_job(grads_r0[0], a_r0[0], b_r0[0], mlp_w1, m_mlp_w1, v_mlp_w1, 1, None),
              adam_job(grads_r0[1], a_r0[1], b_r0[1], mlp_w2, m_mlp_w2, v_mlp_w2, 1, None)], **hosted)
    sib_r2 = _reduce_begin([dw1_0, dw2_0.reshape(N_DEV, d_ff // N_DEV, D)], "r2")
    dhn0 = mm(dhid0, w1[0], tb=True, name="mlp0_dhn", out=((T, D), F32), deps=[sib_r2[-1]])
    grads_r2, a_r2 = sibling_wait(sib_r2, dhn0, name="reduce_sibling_wait_r2")
    dx1, dx1b, d_mlp0 = rms_bwd(x1, mlp_gain[0], dhn0, dres=dx2, name="mlp0_norm_bwd")

    dmixed, ((pair_r2a,),) = mm(dx1b, w_out_e, tb=True, name="e_out_dx", out=((T, MLA_OUT + SGU_OUT), F32),
                                jobs=[pair_job(grads_r2[0], a_r2[0])], **hosted)
    dw_out_e, ((pair_r2b,),) = mm(mixed, dx1b, ta=True, name="e_out_dw", out=(w_out_e.shape, BF),
                                  jobs=[pair_job(grads_r2[1], a_r2[1])], **hosted)
    st_r2 = chips_start([pair_r2a, pair_r2b], name="reduce_chips_start_r2")
    (dattn, dproj, d_mla_out, d_sgu_out, d_vgain, d_sgu_w, d_b_full) = mix_bwd(
        dmixed, attn, proj, e_mla_out_norm, e_sgu_out_norm, v_gain, w_tril_b, w_tril_tb, b_full, name="mix_bwd",
        deps=[st_r2[-1]])
    b_r1 = chips_wait(st_r1, dattn, name="reduce_chips_wait_r1")
    dq, dk, dv = attn_bwd(q, k, v, attn, attn_lse, dattn, name="attn_bwd")
    dq_lin, dkv_lin, dproj = mla_bwd_prep(dq, dk, dv, cos_t, sin_t, dproj, name="mla_bwd_prep")
    dw_uq_pad = mm(qn, dq_lin, ta=True, name="mla_q_dw", out=(w_uq.shape, BF))
    dw_ukv = mm(kvn, dkv_lin, ta=True, name="mla_kv_dw", out=(w_ukv.shape, BF))
    dw_uq = jnp.concatenate([dw_uq_pad[..., :QK_NOPE], _rope_unslab(dw_uq_pad[..., QK_NOPE:])], axis=-1)
    sib_r2b = _reduce_begin([dw_out_e.reshape(g_w_out_e.shape), dw_uq, dw_ukv], "r2b")
    dqn = mm(dq_lin, w_uq, tb=True, name="mla_q_dx", out=((T, Q_LORA), F32), deps=[sib_r2b[-1]])
    dkvn = mm(dkv_lin, w_ukv, tb=True, name="mla_kv_dx", out=((T, KV_LORA), F32), deps=[sib_r2b[-1]])
    grads_r2b, a_r2b, st_r2b = _reduce_continue(sib_r2b, dkvn, "r2b", hosted["job_index"])
    dproj, d_qnorm = rms_bwd(proj, e_q_norm, dqn, col_block=0, want_f32=False, into=dproj, name="q_norm_bwd",
                             deps=[st_r2b[-1]])
    dproj, d_kvnorm = rms_bwd(proj, e_kv_norm, dkvn, col_block=1, want_f32=False, into=dproj, name="kv_norm_bwd")
    dw_in_t_pad, (r_w_out_o, r_w_in_o) = mm(
        dproj, h0, ta=True, name="e_in_dw", out=(w_in_t.shape, BF), bm=_tile(w_in_t.shape[0], 640),
        jobs=[adam_job(grads_r1[0], a_r1[0], b_r1[0], o_w_out, m_o_w_out, v_o_w_out, 0, None),
              adam_job(grads_r1[1], a_r1[1], b_r1[1], o_w_in, m_o_w_in, v_o_w_in, 0, None)], **hosted)
    dw_in_t = unpack_w_in_t_grad(dw_in_t_pad, name="e_in_dw_unpack")
    sib_r3 = _reduce_begin([dw_in_t], "r3")
    dh0 = mm(dproj, w_in_t, name="e_in_dx", out=((T, D), F32), deps=[sib_r3[-1]])
    grads_r3, a_r3, st_r3 = _reduce_continue(sib_r3, dh0, "r3", hosted["job_index"])
    tok_r3 = st_r3[-1]
    grad_x, d_enorm = rms_bwd(x0, e_norm_mix, dh0, dres=dx1, want_bf=False, name="e_norm_bwd", deps=[tok_r3])
    b_r2 = chips_wait(st_r2, grad_x, name="reduce_chips_wait_r2")

    d_sgu_b = jnp.transpose(d_b_full[:, ::CH])
    d_sgu_w_tril = jnp.tril(d_sgu_w)
    rep = [("e_norm_mix", e_norm_mix, m_e_norm_mix, v_e_norm_mix, d_enorm),
           ("e_q_norm", e_q_norm, m_e_q_norm, v_e_q_norm, d_qnorm),
           ("e_kv_norm", e_kv_norm, m_e_kv_norm, v_e_kv_norm, d_kvnorm),
           ("e_v_norm", e_v_norm, m_e_v_norm, v_e_v_norm, d_vgain),
           ("e_sgu_w", e_sgu_w, m_e_sgu_w, v_e_sgu_w, d_sgu_w_tril),
           ("e_sgu_b", e_sgu_b, m_e_sgu_b, v_e_sgu_b, d_sgu_b),
           ("e_mla_out_norm", e_mla_out_norm, m_e_mla_out_norm, v_e_mla_out_norm, d_mla_out),
           ("e_sgu_out_norm", e_sgu_out_norm, m_e_sgu_out_norm, v_e_sgu_out_norm, d_sgu_out),
           ("mlp_norm", mlp_norm, m_mlp_norm, v_mlp_norm, jnp.concatenate([d_mlp0, d_mlp1], axis=0)),
           ("final_norm", final_norm, m_final_norm, v_final_norm, d_final)]
    sizes = [int(np.prod(r[1].shape)) for r in rep]
    n_rep = sum(sizes)
    n_all = n_rep + 4 * D + 1
    width = -(-n_all // (8 * LANES)) * LANES
    pad = 8 * width - n_all
    flat = jnp.concatenate([r[4].reshape(-1) for r in rep]
                           + [d_onorm_full.reshape(-1), dconv_full.reshape(-1), loss_part[0, :1],
                              jnp.zeros((pad,), F32)])
    small_started, small_token = gather_start([[flat.reshape(8, width)]], b_r2[0], name="gather_small_grads_start")

    def finish(grads, a_bufs, b_bufs, t, w, m, v, layer=0, prev=None, tag="", deps=()):
        return run_job(adam_job(grads[t], a_bufs[t], b_bufs[t], w, m, v, layer, prev), index=hosted["job_index"],
                       name=f"adam_{tag}", deps=deps)

    r_w1 = finish(grads_r2, a_r2, b_r2, 0, mlp_w1, m_mlp_w1, v_mlp_w1, 0, r_w1, tag="w1_l0", deps=[tok_r3, small_token])
    r_w2 = finish(grads_r2, a_r2, b_r2, 1, mlp_w2, m_mlp_w2, v_mlp_w2, 0, r_w2, tag="w2_l0", deps=[r_w1[1]])
    b_r2b = chips_wait(st_r2b, r_w2[1], name="reduce_chips_wait_r2b")
    r_w_out_e = finish(grads_r2b, a_r2b, b_r2b, 0, e_w_out, m_e_w_out, v_e_w_out, tag="e_w_out")
    r_w_uq = finish(grads_r2b, a_r2b, b_r2b, 1, e_w_uq, m_e_w_uq, v_e_w_uq, tag="e_w_uq")
    r_w_ukv = finish(grads_r2b, a_r2b, b_r2b, 2, e_w_ukv, m_e_w_ukv, v_e_w_ukv, tag="e_w_ukv")
    b_r3 = chips_wait(st_r3, r_w_out_e[1], name="reduce_chips_wait_r3")
    g_w_in_t = reduce_sum(grads_r3[0], a_r3[0], b_r3[0], name="sum_e_w_in")
    w_in_upd_t = adam_rows(g_w_in_t, jnp.transpose(e_w_in[0]), jnp.transpose(m_e_w_in[0]), jnp.transpose(v_e_w_in[0]),
                           name="adam_e_w_in")
    r_w_in = [jnp.transpose(t)[None] for t in (g_w_in_t, *w_in_upd_t)]

    small_srcs, small_lands = gather_wait(small_started[0], [r_w2[1]], name="gather_small_grads_wait")
    small_all = gather_finish(small_srcs, small_lands, name="gather_small_grads_finish")[0]
    summed = sum_rows8(small_all.reshape(N_DEV * 8, width), 8, name="sum_small_grads").reshape(-1)

    loss = summed[n_rep + 4 * D]

    def pack_rep(i):
        return jnp.concatenate([r[i].reshape(-1) for r in rep]).reshape(n_rep // LANES, LANES)

    g_rep = summed[:n_rep].reshape(n_rep // LANES, LANES)
    d_rep, nm_rep, nv_rep = adam_flat(g_rep, pack_rep(1), pack_rep(2), pack_rep(3), name="adam_replicated")

    def unpack_rep(flat2d):
        out, off = {}, 0
        f = flat2d.reshape(-1)
        for r, n in zip(rep, sizes):
            out[r[0]] = f[off:off + n].reshape(r[1].shape)
            off += n
        return out

    small = {"grad": unpack_rep(g_rep), "delta": unpack_rep(d_rep), "new_m": unpack_rep(nm_rep),
             "new_v": unpack_rep(nv_rep)}
    g_onorm = lax.dynamic_slice(summed[n_rep:n_rep + D].reshape(1, D), (0, me * d_shard), (1, d_shard))
    g_conv = lax.dynamic_slice(summed[n_rep + D:n_rep + 4 * D].reshape(3, D), (0, me * d_shard), (3, d_shard))

    def pack_sharded(norm_part, conv_part):
        return jnp.concatenate([norm_part, conv_part, jnp.zeros((4, d_shard), F32)], axis=0)

    g_sh = pack_sharded(g_onorm, g_conv)
    d_sh, nm_sh, nv_sh = adam_flat(g_sh, pack_sharded(o_norm_mix, o_conv_w[0]), pack_sharded(m_o_norm_mix, m_o_conv_w[0]),
                                   pack_sharded(v_o_norm_mix, v_o_conv_w[0]), name="adam_sharded_small")
    for kind, arr in (("grad", g_sh), ("delta", d_sh), ("new_m", nm_sh), ("new_v", nv_sh)):
        small[kind]["o_norm_mix"] = arr[0:1]
        small[kind]["o_conv_w"] = arr[1:4][None]

    big = {"e_w_in": r_w_in, "e_w_uq": r_w_uq, "e_w_ukv": r_w_ukv, "e_w_out": r_w_out_e, "o_w_in": r_w_in_o,
           "o_w_out": r_w_out_o, "mlp_w1": r_w1, "mlp_w2": r_w2}
    order = ["e_norm_mix", "e_w_in", "e_q_norm", "e_w_uq", "e_kv_norm", "e_w_ukv", "e_v_norm", "e_sgu_w", "e_sgu_b",
             "e_mla_out_norm", "e_sgu_out_norm", "e_w_out", "o_norm_mix", "o_w_in", "o_conv_w", "o_w_out", "mlp_norm",
             "mlp_w1", "mlp_w2", "final_norm"]
    result = [loss, grad_x[None]]
    for ki, kind in enumerate(("grad", "delta", "new_m", "new_v")):
        for nm in order:
            result.append(big[nm][ki] if nm in big else small[kind][nm])
    return tuple(result)
```

```python
import numpy as np
import jax
import jax.numpy as jnp
from jax import lax
from jax.experimental import pallas as pl
from jax.experimental.pallas import tpu as pltpu

BF = jnp.bfloat16
F32 = jnp.float32
MESH = pl.DeviceIdType.MESH
N_DEV = 8

EPS = 1e-6
HEADS = 8
Q_LORA = 512
KV_LORA = 512
QK_NOPE = 128
QK_ROPE = 64
HALF_ROPE = QK_ROPE // 2
V_HEAD = 128
HEAD_PAD = 256
ROPE_BASE = 10000.0
GROUPS = 8
CH = 128
CHUNK = 128
SGU_OUT = GROUPS * CH
MLA_OUT = HEADS * V_HEAD
ATTN_SCALE = float((QK_NOPE + QK_ROPE) ** -0.5)

ADAM_LR = 0.001
ADAM_B1 = 0.9
ADAM_B2 = 0.999
ADAM_EPS = 1e-08
ADAM_WD = 0.01
ADAM_STEP = 10
ADAM_C1 = 1.0 - ADAM_B1 ** ADAM_STEP
ADAM_C2 = 1.0 - ADAM_B2 ** ADAM_STEP

V7X_VMEM_BYTES = 64 * 2 ** 20
VMEM_LIMIT_CAP = V7X_VMEM_BYTES - 6 * 2 ** 20
LANES = 128
ROW_TILE = 128
ATTN_TILE = 1024
STREAM_BLOCK_ELEMS = 256 * 1024
MM_TILE = 1024
MM_K_TILE = 2048
MM_K_BLOCK_MAX = 4096


def _padded_bytes(block, dtype):
    dims = [d for d in block if d is not None]
    if len(dims) >= 1:
        dims[-1] = -(-dims[-1] // LANES) * LANES
    if len(dims) >= 2:
        dims[-2] = -(-dims[-2] // 16) * 16
    return int(np.prod(dims)) * jnp.dtype(dtype).itemsize


def _pcall(body, *, name, grid, ins, outs, scratch=(), semantics=None, aliases=None, prefetch=None, deps=()):
    any_spec = pl.BlockSpec(memory_space=pl.ANY)
    if deps:
        n_lead = len(ins) + (1 if prefetch is not None else 0)
        n_deps = len(deps)
        inner = body

        def body(*refs):
            inner(*refs[:n_lead], *refs[n_lead + n_deps:])

        ins = list(ins) + [(d, None, None) for d in deps]
    in_specs = [any_spec if b is None else pl.BlockSpec(b, m) for _, b, m in ins]
    out_specs = [any_spec if b is None else pl.BlockSpec(b, m) for _, _, b, m in outs]
    out_shape = [pltpu.HBM(s, d) for s, d, _, _ in outs]
    est = 0
    for a, b, _ in ins:
        if b is not None:
            est += 2 * _padded_bytes(b, a.dtype)
    for _, d, b, _ in outs:
        if b is not None:
            est += 2 * _padded_bytes(b, d)
    for s in scratch:
        if hasattr(s, "shape") and hasattr(s, "dtype"):
            est += _padded_bytes(s.shape, s.dtype)
    limit = int(min(VMEM_LIMIT_CAP, est + 16 * 2 ** 20))
    params = pltpu.CompilerParams(
        dimension_semantics=semantics or ("arbitrary",) * len(grid), vmem_limit_bytes=limit)
    args = [pltpu.with_memory_space_constraint(a, pltpu.HBM) for a, _, _ in ins]
    if prefetch is not None:
        grid_spec = pltpu.PrefetchScalarGridSpec(
            num_scalar_prefetch=1, grid=grid, in_specs=in_specs, out_specs=out_specs, scratch_shapes=list(scratch))
        call = pl.pallas_call(body, out_shape=out_shape, grid_spec=grid_spec, name=name, compiler_params=params,
                              input_output_aliases=aliases or {})
        return call(prefetch, *args)
    call = pl.pallas_call(body, out_shape=out_shape, grid=grid, in_specs=in_specs, out_specs=out_specs,
                          scratch_shapes=list(scratch), name=name, compiler_params=params,
                          input_output_aliases=aliases or {})
    return call(*args)


def _tile(dim, pref, quantum=LANES):
    if dim <= pref:
        return dim
    t = (pref // quantum) * quantum
    while t >= quantum:
        if dim % t == 0:
            return t
        t -= quantum
    return dim


def _vshape(arr_shape):
    if len(arr_shape) == 2:
        return tuple(arr_shape)
    s, r, c = arr_shape
    return (r, s * c)


def _vblock(arr_shape, br, bc, rc):
    if len(arr_shape) == 2:
        return (br, bc), (lambda *g: rc(*g))
    _, _, c = arr_shape
    assert c % bc == 0, (arr_shape, bc)
    per = c // bc

    def imap(*g):
        ri, ci = rc(*g)
        return (ci // per, ri, ci % per)

    return (None, br, bc), imap


def _shard_width(*shapes):
    w = None
    for s in shapes:
        if len(s) == 3:
            w = s[2] if w is None else int(np.gcd(w, s[2]))
    return w


def mm(a, b, *, name, ta=False, tb=False, out=None, outs=None, epi=None, epi_ins=(), bm=None, bn=None, bk=None,
       deps=(), jobs=(), job_index=None):
    av, bv = _vshape(a.shape), _vshape(b.shape)
    M, K = (av[1], av[0]) if ta else av
    K2, N = (bv[1], bv[0]) if tb else bv
    assert K == K2, (a.shape, b.shape, ta, tb)
    if outs is None:
        outs = [(out[0], out[1], None)]
    a_sw = _shard_width(a.shape)
    b_sw = _shard_width(b.shape)
    o_sw = _shard_width(*[o[0] for o in outs])
    m_lim = a_sw if (ta and a_sw) else None
    k_lim = [w for w in ((a_sw if not ta else None), (b_sw if tb else None)) if w]
    n_lim = [w for w in ((b_sw if not tb else None), o_sw) if w]
    if bm is None:
        bm = _tile(M, min([MM_TILE] + ([m_lim] if m_lim else [])))
    if bn is None:
        bn = _tile(N, min([MM_TILE] + n_lim))
    k_shards = 0
    if tb and len(b.shape) == 3 and bk is None and not (a_sw and not ta):
        k_shards = 1
        while 2 * k_shards <= b.shape[0] and 2 * k_shards * b_sw <= MM_K_BLOCK_MAX:
            k_shards *= 2
        bk = k_shards * b_sw
    if bk is None:
        bk = K if (K <= 4096 and not k_lim) else _tile(K, min([MM_K_TILE] + k_lim))
    assert M % bm == 0 and N % bn == 0 and K % bk == 0, (name, M, N, K, bm, bn, bk)
    nk = K // bk
    grid = (M // bm, N // bn, nk)
    if ta:
        a_blk, a_map = _vblock(a.shape, bk, bm, lambda i, j, k: (k, i))
    else:
        a_blk, a_map = _vblock(a.shape, bm, bk, lambda i, j, k: (i, k))
    if k_shards:
        b_blk, b_map = (k_shards, bn, b_sw), (lambda i, j, k: (k, j, 0))
    elif tb:
        b_blk, b_map = _vblock(b.shape, bn, bk, lambda i, j, k: (j, k))
    else:
        b_blk, b_map = _vblock(b.shape, bk, bn, lambda i, j, k: (k, j))
    dn = (((0 if ta else 1,), (1 if tb else 0,)), ((), ()))
    ins = [(a, a_blk, a_map), (b, b_blk, b_map)] + list(epi_ins)
    out_list = []
    for shape, dtype, cols in outs:
        cols = cols or bn
        blk, imap = _vblock(shape, bm, cols, lambda i, j, k: (i, j))
        out_list.append((shape, dtype, blk, imap))
    n_e, n_o = len(epi_ins), len(out_list)

    n_steps = grid[0] * grid[1] * nk
    built = [job(n_steps) for job in jobs]
    aliases = {}
    job_slices = []
    if built:
        def lin(i, j, k):
            return (i * grid[1] + j) * nk + k

        ins = [(arr, blk, None if blk is None else (lambda i, j, k, s, f=f: f(i, j, k))) for arr, blk, f in ins]
        out_list = [(sh, dt, blk, (lambda i, j, k, s, f=f: f(i, j, k))) for sh, dt, blk, f in out_list]
        n_main_in, n_main_out = len(ins), len(out_list)
        for jb in built:
            i0, o0 = len(ins), len(out_list)
            ins += [(arr, blk, None if blk is None else (lambda i, j, k, s, f=f: f(lin(i, j, k), s)))
                    for arr, blk, f in jb["ins"]]
            out_list += [(sh, dt, blk, (lambda i, j, k, s, f=f: f(lin(i, j, k), s))) for sh, dt, blk, f in jb["outs"]]
            aliases.update({1 + i0 + ai: o0 + ao for ai, ao in jb["aliases"].items()})
            job_slices.append((i0, len(jb["ins"]), o0, len(jb["outs"])))
    n_in_total = len(ins)

    def body(*refs):
        if built:
            refs = refs[1:]
        a_ref, b_ref = refs[0], refs[1]
        e_refs = refs[2:2 + n_e]
        o_refs = refs[n_in_total:n_in_total + n_o]
        for jb, (i0, ni, o0, no) in zip(built, job_slices):
            jb["fn"](refs[i0:i0 + ni], refs[n_in_total + o0:n_in_total + o0 + no])

        def finish(acc):
            res = epi(acc, *e_refs) if epi is not None else (acc,)
            for o_ref, r in zip(o_refs, res):
                o_ref[...] = r.astype(o_ref.dtype)

        x = a_ref[...].astype(BF)
        y = b_ref[...].astype(BF)
        if k_shards:
            p = None
            for s in range(k_shards):
                part = lax.dot_general(x[:, s * b_sw:(s + 1) * b_sw], y[s], dn, preferred_element_type=F32)
                p = part if p is None else p + part
        else:
            p = lax.dot_general(x, y, dn, preferred_element_type=F32)
        if nk == 1:
            finish(p)
        else:
            acc_ref = refs[-1]
            k = pl.program_id(2)

            @pl.when(k == 0)
            def _():
                acc_ref[...] = p

            @pl.when(k > 0)
            def _():
                acc_ref[...] += p

            @pl.when(k == nk - 1)
            def _():
                finish(acc_ref[...])

    scratch = [pltpu.VMEM((bm, bn), F32)] if nk > 1 else []
    res = _pcall(body, name=name, grid=grid, ins=ins, outs=out_list, scratch=scratch, deps=deps,
                 semantics=("parallel", "parallel", "arbitrary"), prefetch=job_index if built else None, aliases=aliases)
    main = res[0] if n_o == 1 else res[:n_o]
    if not built:
        return main
    return main, [res[o0:o0 + no] for _, _, o0, no in job_slices]


_GELU_K = float(np.sqrt(2.0 / np.pi))
_GELU_C = 0.044715


def _gelu(x):
    t = jnp.tanh(_GELU_K * (x + _GELU_C * (x * x * x)))
    return 0.5 * x * (1.0 + t)


def _gelu_grad(x):
    t = jnp.tanh(_GELU_K * (x + _GELU_C * (x * x * x)))
    return 0.5 * (1.0 + t) + 0.5 * x * (1.0 - t * t) * (_GELU_K * (1.0 + 3.0 * _GELU_C * (x * x)))


def _rstd(x):
    return lax.rsqrt(jnp.mean(x * x, axis=-1, keepdims=True) + EPS)


def _rms_bwd(x, gain, dy):
    r = _rstd(x)
    xh = x * r
    gdy = dy * gain
    dx = r * (gdy - xh * jnp.mean(gdy * xh, axis=-1, keepdims=True))
    return dx, dy * xh


def _rope_fwd(x, cos_t, sin_t):
    return x * cos_t + pltpu.roll(x, 2 * HALF_ROPE, 1) * sin_t


def _rope_bwd(dy, cos_t, sin_t):
    return dy * cos_t + pltpu.roll(dy * sin_t, 2 * HALF_ROPE, 1)


def _acc_rows(ref, val, first):
    s = jnp.sum(val, axis=0, keepdims=True)

    @pl.when(first)
    def _():
        ref[...] = s

    @pl.when(jnp.logical_not(first))
    def _():
        ref[...] += s


def rms_fwd(x, gain, *, name, col_block=0, width=None, deps=()):
    T = x.shape[0]
    width = width or x.shape[1]
    tm = _tile(T, ROW_TILE, 8)

    def body(x_ref, g_ref, o_ref):
        v = x_ref[...]
        o_ref[...] = (v * _rstd(v) * g_ref[...]).astype(BF)

    return _pcall(body, name=name, grid=(T // tm,),
                  ins=[(x, (tm, width), lambda i: (i, col_block)), (gain, (1, width), lambda i: (0, 0))],
                  outs=[((T, width), BF, (tm, width), lambda i: (i, 0))], semantics=("parallel",), deps=deps)[0]


def rms_bwd(x, gain, dy, *, name, col_block=0, dres=None, want_f32=True, want_bf=True, into=None, deps=()):
    T, width = dy.shape
    tm = _tile(T, ROW_TILE, 8)
    has_res = dres is not None

    def body(*refs):
        x_ref, g_ref, dy_ref = refs[:3]
        pos = 3
        res_ref = None
        if has_res:
            res_ref = refs[pos]
            pos += 1
        if into is not None:
            pos += 1
        outs = refs[pos:]
        dx, dg_rows = _rms_bwd(x_ref[...], g_ref[...], dy_ref[...])
        if has_res:
            dx = dx + res_ref[...]
        o = 0
        if want_f32:
            outs[o][...] = dx
            o += 1
        if want_bf:
            outs[o][...] = dx.astype(BF)
            o += 1
        _acc_rows(outs[o], dg_rows, pl.program_id(0) == 0)

    ins = [(x, (tm, width), lambda i: (i, col_block)), (gain, (1, width), lambda i: (0, 0)),
           (dy, (tm, width), lambda i: (i, 0))]
    if has_res:
        ins.append((dres, (tm, width), lambda i: (i, 0)))
    outs = []
    aliases = {}
    if want_f32:
        outs.append(((T, width), F32, (tm, width), lambda i: (i, 0)))
    if want_bf and into is not None:
        ins.append((into, None, None))
        aliases[len(ins) - 1] = len(outs)
        outs.append((into.shape, BF, (tm, width), lambda i: (i, col_block)))
    elif want_bf:
        outs.append(((T, width), BF, (tm, width), lambda i: (i, 0)))
    outs.append(((1, width), F32, (1, width), lambda i: (0, 0)))
    return _pcall(body, name=name, grid=(T // tm,), ins=ins, outs=outs, aliases=aliases, deps=deps)


def mla_prep(proj, q_norm, kv_norm, cos_t, sin_t, *, name):
    T = proj.shape[0]
    tm = _tile(T, ROW_TILE, 8)
    kr_block = (proj.shape[1] - LANES) // LANES

    def body(cq_ref, ckv_ref, kr_ref, qg_ref, kg_ref, cos_ref, sin_ref, qn_ref, kvn_ref, krope_ref):
        cq = cq_ref[...]
        qn_ref[...] = (cq * _rstd(cq) * qg_ref[...]).astype(BF)
        ckv = ckv_ref[...]
        kvn_ref[...] = (ckv * _rstd(ckv) * kg_ref[...]).astype(BF)
        krope_ref[...] = _rope_fwd(kr_ref[...], cos_ref[...], sin_ref[...]).astype(BF)

    return _pcall(
        body, name=name, grid=(T // tm,),
        ins=[(proj, (tm, Q_LORA), lambda i: (i, 0)), (proj, (tm, KV_LORA), lambda i: (i, 1)),
             (proj, (tm, LANES), lambda i: (i, kr_block)),
             (q_norm, (1, Q_LORA), lambda i: (0, 0)), (kv_norm, (1, KV_LORA), lambda i: (0, 0)),
             (cos_t, (tm, LANES), lambda i: (i, 0)), (sin_t, (tm, LANES), lambda i: (i, 0))],
        outs=[((T, Q_LORA), BF, (tm, Q_LORA), lambda i: (i, 0)), ((T, KV_LORA), BF, (tm, KV_LORA), lambda i: (i, 0)),
              ((T, LANES), BF, (tm, LANES), lambda i: (i, 0))],
        semantics=("parallel",))


def _attn_scores(q, k_blk, diagonal):
    s = lax.dot_general(q, k_blk, (((1,), (1,)), ((), ())), preferred_element_type=F32) * ATTN_SCALE
    if diagonal:
        row = lax.broadcasted_iota(jnp.int32, s.shape, 0)
        col = lax.broadcasted_iota(jnp.int32, s.shape, 1)
        s = jnp.where(col <= row, s, -jnp.inf)
    return s


def attn_fwd(q, k, v, *, name):
    T = q.shape[0]
    tq = _tile(T, ATTN_TILE, 8)

    def body(q_ref, k_ref, v_ref, o_ref, lse_ref):
        i = pl.program_id(1)
        qv = q_ref[...]

        def block(kb, carry, diagonal):
            m, l, acc = carry
            start = pl.multiple_of(kb * tq, tq)
            s = _attn_scores(qv, k_ref[pl.ds(start, tq), :], diagonal)
            m_new = jnp.maximum(m, jnp.max(s, axis=-1, keepdims=True))
            alpha = jnp.exp(m - m_new)
            p = jnp.exp(s - m_new)
            l = alpha * l + jnp.sum(p, axis=-1, keepdims=True)
            acc = alpha * acc + jnp.dot(p.astype(BF), v_ref[pl.ds(start, tq), :], preferred_element_type=F32)
            return m_new, l, acc

        init = (jnp.full((tq, 1), -jnp.inf, F32), jnp.zeros((tq, 1), F32), jnp.zeros((tq, V_HEAD), F32))
        carry = lax.fori_loop(0, i, lambda kb, c: block(kb, c, False), init)
        m, l, acc = block(i, carry, True)
        o_ref[...] = acc / l
        lse_ref[...] = jnp.broadcast_to(m + jnp.log(l), (tq, V_HEAD))

    return _pcall(
        body, name=name, grid=(HEADS, T // tq),
        ins=[(q, (tq, HEAD_PAD), lambda h, i: (i, h)), (k, (T, HEAD_PAD), lambda h, i: (0, h)),
             (v, (T, V_HEAD), lambda h, i: (0, h))],
        outs=[((T, MLA_OUT), F32, (tq, V_HEAD), lambda h, i: (i, h)),
              ((T, MLA_OUT), F32, (tq, V_HEAD), lambda h, i: (i, h))], semantics=("parallel", "parallel"))


def attn_bwd(q, k, v, o, lse, do, *, name):
    T = q.shape[0]
    tq = _tile(T, ATTN_TILE, 8)

    def body(q_ref, k_ref, v_ref, o_ref, lse_ref, do_ref, dq_ref, dk_ref, dv_ref):
        i = pl.program_id(1)

        @pl.when(i == 0)
        def _():
            dk_ref[...] = jnp.zeros_like(dk_ref)
            dv_ref[...] = jnp.zeros_like(dv_ref)

        qv = q_ref[...]
        do_t = do_ref[...]
        lse_v = lse_ref[:, 0:1]
        delta = jnp.sum(do_t.astype(F32) * o_ref[...], axis=-1, keepdims=True)

        def block(kb, dq, diagonal):
            start = pl.multiple_of(kb * tq, tq)
            k_blk = k_ref[pl.ds(start, tq), :]
            v_blk = v_ref[pl.ds(start, tq), :]
            p = jnp.exp(_attn_scores(qv, k_blk, diagonal) - lse_v)
            dp = lax.dot_general(do_t, v_blk, (((1,), (1,)), ((), ())), preferred_element_type=F32)
            ds = (p * (dp - delta) * ATTN_SCALE).astype(BF)
            dk_ref[pl.ds(start, tq), :] += lax.dot_general(ds, qv, (((0,), (0,)), ((), ())), preferred_element_type=F32)
            dv_ref[pl.ds(start, tq), :] += lax.dot_general(p.astype(BF), do_t, (((0,), (0,)), ((), ())),
                                                          preferred_element_type=F32)
            return dq + jnp.dot(ds, k_blk, preferred_element_type=F32)

        dq = lax.fori_loop(0, i, lambda kb, c: block(kb, c, False), jnp.zeros((tq, HEAD_PAD), F32))
        dq_ref[...] = block(i, dq, True)

    return _pcall(
        body, name=name, grid=(HEADS, T // tq),
        ins=[(q, (tq, HEAD_PAD), lambda h, i: (i, h)), (k, (T, HEAD_PAD), lambda h, i: (0, h)),
             (v, (T, V_HEAD), lambda h, i: (0, h)), (o, (tq, V_HEAD), lambda h, i: (i, h)),
             (lse, (tq, V_HEAD), lambda h, i: (i, h)), (do, (tq, V_HEAD), lambda h, i: (i, h))],
        outs=[((T, HEADS * HEAD_PAD), F32, (tq, HEAD_PAD), lambda h, i: (i, h)),
              ((T, HEADS * HEAD_PAD), F32, (T, HEAD_PAD), lambda h, i: (0, h)),
              ((T, MLA_OUT), F32, (T, V_HEAD), lambda h, i: (0, h))],
        semantics=("parallel", "arbitrary"))


def mla_bwd_prep(dq, dk, dv, cos_t, sin_t, dproj, *, name):
    T = dq.shape[0]
    tm = _tile(T, ROW_TILE, 8)
    kr_block = (dproj.shape[1] - LANES) // LANES

    def body(dq_ref, dk_ref, dv_ref, cos_ref, sin_ref, dproj_in, dql_ref, dkvl_ref, dkr_ref):
        cos_v, sin_v = cos_ref[...], sin_ref[...]
        kr = jnp.zeros((tm, LANES), F32)
        for h in range(HEADS):
            lo = h * HEAD_PAD
            dql_ref[:, lo:lo + QK_NOPE] = dq_ref[:, lo:lo + QK_NOPE].astype(BF)
            dql_ref[:, lo + QK_NOPE:lo + HEAD_PAD] = _rope_bwd(
                dq_ref[:, lo + QK_NOPE:lo + HEAD_PAD], cos_v, sin_v).astype(BF)
            dkvl_ref[:, lo:lo + QK_NOPE] = dk_ref[:, lo:lo + QK_NOPE].astype(BF)
            dkvl_ref[:, lo + QK_NOPE:lo + HEAD_PAD] = dv_ref[:, h * V_HEAD:(h + 1) * V_HEAD].astype(BF)
            kr = kr + dk_ref[:, lo + QK_NOPE:lo + HEAD_PAD]
        dkr_ref[...] = _rope_bwd(kr, cos_v, sin_v).astype(BF)

    W = HEADS * HEAD_PAD
    return _pcall(
        body, name=name, grid=(T // tm,),
        ins=[(dq, (tm, W), lambda i: (i, 0)), (dk, (tm, W), lambda i: (i, 0)), (dv, (tm, MLA_OUT), lambda i: (i, 0)),
             (cos_t, (tm, LANES), lambda i: (i, 0)), (sin_t, (tm, LANES), lambda i: (i, 0)), (dproj, None, None)],
        outs=[((T, W), BF, (tm, W), lambda i: (i, 0)), ((T, W), BF, (tm, W), lambda i: (i, 0)),
              (dproj.shape, BF, (tm, LANES), lambda i: (i, kr_block))],
        aliases={5: 2}, semantics=("parallel",))


def _group_norm_stats(vg):
    mu = jnp.mean(vg, axis=-1, keepdims=True)
    d = vg - mu
    r = lax.rsqrt(jnp.mean(d * d, axis=-1, keepdims=True) + EPS)
    return d * r, r


def mix_fwd(a, proj, g_mla, g_sgu, v_gain, w_tril, b_full, *, name):
    T = a.shape[0]
    tm = _tile(T, ROW_TILE, CHUNK)
    n_chunk = tm // CHUNK

    def body(a_ref, u_ref, v_ref, gm_ref, gs_ref, vg_ref, w_ref, b_ref, o_ref, s_scr):
        av = a_ref[...]
        o_ref[:, :MLA_OUT] = (av * _rstd(av) * gm_ref[...]).astype(BF)
        for g in range(GROUPS):
            sl = slice(g * CH, (g + 1) * CH)
            vhat, _ = _group_norm_stats(_gelu(v_ref[:, sl]))
            vn = (vhat * vg_ref[:, sl]).astype(BF)
            u = _gelu(u_ref[:, sl])
            for ci in range(n_chunk):
                rs = slice(ci * CHUNK, (ci + 1) * CHUNK)
                y = jnp.dot(w_ref[g], vn[rs], preferred_element_type=F32) + b_ref[:, sl]
                s_scr[rs, sl] = u[rs] * y
        s = s_scr[...]
        o_ref[:, MLA_OUT:] = (s * _rstd(s) * gs_ref[...]).astype(BF)

    return _pcall(
        body, name=name, grid=(T // tm,),
        ins=[(a, (tm, MLA_OUT), lambda i: (i, 0)), (proj, (tm, SGU_OUT), lambda i: (i, 1)),
             (proj, (tm, SGU_OUT), lambda i: (i, 2)), (g_mla, (1, MLA_OUT), lambda i: (0, 0)),
             (g_sgu, (1, SGU_OUT), lambda i: (0, 0)), (v_gain, (1, SGU_OUT), lambda i: (0, 0)),
             (w_tril, (GROUPS, CHUNK, CHUNK), lambda i: (0, 0, 0)), (b_full, (CHUNK, SGU_OUT), lambda i: (0, 0))],
        outs=[((T, MLA_OUT + SGU_OUT), BF, (tm, MLA_OUT + SGU_OUT), lambda i: (i, 0))],
        scratch=[pltpu.VMEM((tm, SGU_OUT), F32)], semantics=("parallel",))[0]


def mix_bwd(dmixed, a, proj, g_mla, g_sgu, v_gain, w_tril, w_tril_t, b_full, *, name, deps=()):
    T = a.shape[0]
    tm = _tile(T, ROW_TILE, CHUNK)
    n_chunk = tm // CHUNK
    uv0 = Q_LORA + KV_LORA

    def body(dm_a_ref, dm_s_ref, a_ref, u_ref, v_ref, gm_ref, gs_ref, vg_ref, w_ref, wt_ref, b_ref,
             da_ref, duv_ref, dgm_ref, dgs_ref, dvg_ref, dw_ref, db_ref, s_scr, y_scr):
        first = pl.program_id(0) == 0
        duv_ref[:, :uv0] = jnp.zeros((tm, uv0), BF)
        duv_ref[:, uv0 + 2 * SGU_OUT:] = jnp.zeros((tm, duv_ref.shape[1] - uv0 - 2 * SGU_OUT), BF)
        da, dgm_rows = _rms_bwd(a_ref[...], gm_ref[...], dm_a_ref[...])
        da_ref[...] = da.astype(BF)
        _acc_rows(dgm_ref, dgm_rows, first)

        for g in range(GROUPS):
            sl = slice(g * CH, (g + 1) * CH)
            vhat, _ = _group_norm_stats(_gelu(v_ref[:, sl]))
            vn = (vhat * vg_ref[:, sl]).astype(BF)
            u = _gelu(u_ref[:, sl])
            for ci in range(n_chunk):
                rs = slice(ci * CHUNK, (ci + 1) * CHUNK)
                y = jnp.dot(w_ref[g], vn[rs], preferred_element_type=F32) + b_ref[:, sl]
                y_scr[rs, sl] = y
                s_scr[rs, sl] = u[rs] * y
        ds, dgs_rows = _rms_bwd(s_scr[...], gs_ref[...], dm_s_ref[...])
        _acc_rows(dgs_ref, dgs_rows, first)
        s_scr[...] = ds

        @pl.when(first)
        def _():
            dw_ref[...] = jnp.zeros_like(dw_ref)
            db_ref[...] = jnp.zeros_like(db_ref)

        for g in range(GROUPS):
            sl = slice(g * CH, (g + 1) * CH)
            upre = u_ref[:, sl]
            vpre = v_ref[:, sl]
            u = _gelu(upre)
            vhat, r = _group_norm_stats(_gelu(vpre))
            gain = vg_ref[:, sl]
            vn = (vhat * gain).astype(BF)
            dsg = s_scr[:, sl]
            duv_ref[:, uv0 + g * CH:uv0 + (g + 1) * CH] = (dsg * y_scr[:, sl] * _gelu_grad(upre)).astype(BF)
            dy = dsg * u
            dyb = dy.astype(BF)
            dvn_parts = []
            for ci in range(n_chunk):
                rs = slice(ci * CHUNK, (ci + 1) * CHUNK)
                dvn_parts.append(jnp.dot(wt_ref[g], dyb[rs], preferred_element_type=F32))
                dw_ref[g] += lax.dot_general(dyb[rs], vn[rs], (((1,), (1,)), ((), ())), preferred_element_type=F32)
                db_ref[:, sl] += jnp.broadcast_to(jnp.sum(dy[rs], axis=-1, keepdims=True), (CHUNK, CH))
            dvn = dvn_parts[0] if n_chunk == 1 else jnp.concatenate(dvn_parts, axis=0)
            _acc_rows(dvg_ref.at[:, sl], dvn * vhat, first)
            dvh = dvn * gain
            dvg = r * (dvh - jnp.mean(dvh, axis=-1, keepdims=True)
                       - vhat * jnp.mean(dvh * vhat, axis=-1, keepdims=True))
            duv_ref[:, uv0 + SGU_OUT + g * CH:uv0 + SGU_OUT + (g + 1) * CH] = (dvg * _gelu_grad(vpre)).astype(BF)

    return _pcall(
        body, name=name, grid=(T // tm,),
        ins=[(dmixed, (tm, MLA_OUT), lambda i: (i, 0)), (dmixed, (tm, SGU_OUT), lambda i: (i, 1)),
             (a, (tm, MLA_OUT), lambda i: (i, 0)), (proj, (tm, SGU_OUT), lambda i: (i, 1)),
             (proj, (tm, SGU_OUT), lambda i: (i, 2)), (g_mla, (1, MLA_OUT), lambda i: (0, 0)),
             (g_sgu, (1, SGU_OUT), lambda i: (0, 0)), (v_gain, (1, SGU_OUT), lambda i: (0, 0)),
             (w_tril, (GROUPS, CHUNK, CHUNK), lambda i: (0, 0, 0)), (w_tril_t, (GROUPS, CHUNK, CHUNK), lambda i: (0, 0, 0)),
             (b_full, (CHUNK, SGU_OUT), lambda i: (0, 0))],
        outs=[((T, MLA_OUT), BF, (tm, MLA_OUT), lambda i: (i, 0)),
              ((T, proj.shape[1]), BF, (tm, proj.shape[1]), lambda i: (i, 0)),
              ((1, MLA_OUT), F32, (1, MLA_OUT), lambda i: (0, 0)), ((1, SGU_OUT), F32, (1, SGU_OUT), lambda i: (0, 0)),
              ((1, SGU_OUT), F32, (1, SGU_OUT), lambda i: (0, 0)),
              ((GROUPS, CHUNK, CHUNK), F32, (GROUPS, CHUNK, CHUNK), lambda i: (0, 0, 0)),
              ((CHUNK, SGU_OUT), F32, (CHUNK, SGU_OUT), lambda i: (0, 0))],
        scratch=[pltpu.VMEM((tm, SGU_OUT), F32), pltpu.VMEM((tm, SGU_OUT), F32)], deps=deps)


def _shift_down(z, n, row):
    return jnp.where(row >= n, pltpu.roll(z, n, 0), 0.0)


def _shift_up(z, n, row, T):
    return jnp.where(row < T - n, pltpu.roll(z, T - n, 0), 0.0)


def conv_fwd(proj, conv_w, *, name):
    T, D3 = proj.shape
    D = D3 // 3
    tn = _tile(D, 256)
    nj = D // tn

    def body(b_ref, c_ref, x_ref, w_ref, o_ref):
        row = lax.broadcasted_iota(jnp.int32, (T, tn), 0)
        z = c_ref[...] * x_ref[...]
        zc = w_ref[2:3, :] * z + w_ref[1:2, :] * _shift_down(z, 1, row) + w_ref[0:1, :] * _shift_down(z, 2, row)
        o_ref[...] = (b_ref[...] * zc).astype(BF)

    return _pcall(
        body, name=name, grid=(nj,),
        ins=[(proj, (T, tn), lambda j: (0, j)), (proj, (T, tn), lambda j: (0, nj + j)),
             (proj, (T, tn), lambda j: (0, 2 * nj + j)), (conv_w, (3, tn), lambda j: (0, j))],
        outs=[((T, D), BF, (T, tn), lambda j: (0, j))], semantics=("parallel",))[0]


def conv_bwd(dg, proj, conv_w, *, name, deps=()):
    T, D3 = proj.shape
    D = D3 // 3
    tn = _tile(D, 256)
    nj = D // tn

    def body(dg_ref, b_ref, c_ref, x_ref, w_ref, dp_ref, dw_ref, dc_scr, dx_scr):
        part = pl.program_id(1)

        @pl.when(part == 0)
        def _():
            row = lax.broadcasted_iota(jnp.int32, (T, tn), 0)
            c, x = c_ref[...], x_ref[...]
            z = c * x
            z1 = _shift_down(z, 1, row)
            z2 = _shift_down(z, 2, row)
            dgv = dg_ref[...]
            zc = w_ref[2:3, :] * z + w_ref[1:2, :] * z1 + w_ref[0:1, :] * z2
            dp_ref[...] = (dgv * zc).astype(BF)
            dzc = dgv * b_ref[...]
            dw_ref[0:1, :] = jnp.sum(dzc * z2, axis=0, keepdims=True)
            dw_ref[1:2, :] = jnp.sum(dzc * z1, axis=0, keepdims=True)
            dw_ref[2:3, :] = jnp.sum(dzc * z, axis=0, keepdims=True)
            dz = (w_ref[2:3, :] * dzc + w_ref[1:2, :] * _shift_up(dzc, 1, row, T)
                  + w_ref[0:1, :] * _shift_up(dzc, 2, row, T))
            dc_scr[...] = (dz * x).astype(BF)
            dx_scr[...] = (dz * c).astype(BF)

        @pl.when(part == 1)
        def _():
            dp_ref[...] = dc_scr[...]

        @pl.when(part == 2)
        def _():
            dp_ref[...] = dx_scr[...]

    return _pcall(
        body, name=name, grid=(nj, 3),
        ins=[(dg, (T, tn), lambda j, p: (0, j)), (proj, (T, tn), lambda j, p: (0, j)),
             (proj, (T, tn), lambda j, p: (0, nj + j)), (proj, (T, tn), lambda j, p: (0, 2 * nj + j)),
             (conv_w, (3, tn), lambda j, p: (0, j))],
        outs=[((T, D3), BF, (T, tn), lambda j, p: (0, p * nj + j)), ((3, D), F32, (3, tn), lambda j, p: (0, j))],
        scratch=[pltpu.VMEM((T, tn), BF), pltpu.VMEM((T, tn), BF)], semantics=("parallel", "arbitrary"), deps=deps)


def loss_bwd(x_parts, gain, target, *, name):
    T, D = target.shape
    tm = _tile(T, ROW_TILE, 8)
    n_x = len(x_parts)

    def body(*refs):
        x_refs = refs[:n_x]
        g_ref, t_ref, dx_ref, dxb_ref, dg_ref, loss_ref = refs[n_x:]
        first = pl.program_id(0) == 0
        xv = jnp.concatenate([r[...] for r in x_refs], axis=-1) if n_x > 1 else x_refs[0][...]
        r = _rstd(xv)
        xh = xv * r
        gain_v = g_ref[...]
        err = xh * gain_v - t_ref[...]
        part = 0.5 * jnp.sum(jnp.mean(err * err, axis=-1, keepdims=True), axis=0, keepdims=True)
        _acc_rows(loss_ref, jnp.broadcast_to(part, (1, LANES)), first)
        dy = err * (1.0 / D)
        gdy = dy * gain_v
        dx = r * (gdy - xh * jnp.mean(gdy * xh, axis=-1, keepdims=True))
        dx_ref[...] = dx
        dxb_ref[...] = dx.astype(BF)
        _acc_rows(dg_ref, dy * xh, first)

    return _pcall(
        body, name=name, grid=(T // tm,),
        ins=[(p, (tm, D // n_x), lambda i: (i, 0)) for p in x_parts]
        + [(gain, (1, D), lambda i: (0, 0)), (target, (tm, D), lambda i: (i, 0))],
        outs=[((T, D), F32, (tm, D), lambda i: (i, 0)), ((T, D), BF, (tm, D), lambda i: (i, 0)),
              ((1, D), F32, (1, D), lambda i: (0, 0)), ((1, LANES), F32, (1, LANES), lambda i: (0, 0))])


def _adamw(g, w, m, v):
    m = ADAM_B1 * m + (1.0 - ADAM_B1) * g
    v = ADAM_B2 * v + (1.0 - ADAM_B2) * (g * g)
    m_hat = m / ADAM_C1
    v_hat = v / ADAM_C2
    delta = -ADAM_LR * (m_hat / (jnp.sqrt(v_hat) + ADAM_EPS) + ADAM_WD * w)
    return delta, m, v


def adam_flat(g, w, m, v, *, name):
    def body(g_ref, w_ref, m_ref, v_ref, d_ref, nm_ref, nv_ref):
        d, nm, nv = _adamw(g_ref[...], w_ref[...], m_ref[...], v_ref[...])
        d_ref[...] = d
        nm_ref[...] = nm
        nv_ref[...] = nv

    blk = g.shape
    zero = lambda: (0, 0)
    return _pcall(body, name=name, grid=(),
                  ins=[(t, blk, zero) for t in (g, w, m, v)],
                  outs=[(blk, F32, blk, zero)] * 3)


def _chip_slots():
    x, y, c = lax.axis_index("x"), lax.axis_index("y"), lax.axis_index("c")
    chips = [(1 - x, y), (x, 1 - y), (1 - x, 1 - y)]
    return x, y, c, chips


def device_index():
    x, y, c, chips = _chip_slots()
    return jnp.stack([4 * x + 2 * y + c, 2 * x + y] + [4 * cx + 2 * cy + c for cx, cy in chips]
                     + [2 * cx + cy for cx, cy in chips]).astype(jnp.int32)


def _job_rows(R, C, n_steps):
    if n_steps is None:
        n_steps = max(1, R * C // STREAM_BLOCK_ELEMS)
    n_blk = max([d for d in range(1, n_steps + 1) if R % d == 0 and (R // d) % 16 == 0] or [1])
    return R // n_blk, n_blk


def run_job(job, *, index, name, deps=()):
    jb = job(None)
    n_in = len(jb["ins"])

    def body(idx_ref, *refs):
        jb["fn"](refs[:n_in], refs[n_in:n_in + len(jb["outs"])])

    return _pcall(body, name=name, grid=(jb["n_blk"],), ins=jb["ins"], outs=jb["outs"], prefetch=index,
                  aliases={1 + a: o for a, o in jb["aliases"].items()}, semantics=("parallel",), deps=deps)


def adam_job(gs, a_buf, b_buf, w, m, v, layer, prev):
    L, R, C = w.shape

    def build(n_steps):
        tr, n_blk = _job_rows(R, C, n_steps)
        blk = (None, tr, C)
        row = lambda t: jnp.minimum(t, n_blk - 1)
        ins = [(gs, blk, lambda t, s: (s[0], row(t), 0)), (a_buf, blk, lambda t, s: (s[1], row(t), 0))]
        ins += [(b_buf, blk, lambda t, s, j=j: (j, row(t), 0)) for j in range(3)]
        ins += [(p, blk, lambda t, s: (layer, row(t), 0)) for p in (w, m, v)]
        ins += [(p, None, None) for p in (prev or [])]

        def fn(i, o):
            g = ((((i[0][...].astype(F32) + i[1][...].astype(F32)) + i[2][...].astype(F32))
                  + i[3][...].astype(F32)) + i[4][...].astype(F32))
            d, nm, nv = _adamw(g, i[5][...], i[6][...], i[7][...])
            o[0][...] = g
            o[1][...] = d
            o[2][...] = nm
            o[3][...] = nv

        return dict(ins=ins, outs=[((L, R, C), F32, blk, lambda t, s: (layer, row(t), 0))] * 4, fn=fn,
                    aliases={8 + o: o for o in range(4)} if prev else {}, n_blk=n_blk)

    return build


def pair_job(gs, a_buf):
    _, R, C = gs.shape

    def build(n_steps):
        tr, n_blk = _job_rows(R, C, n_steps)
        blk = (None, tr, C)
        row = lambda t: jnp.minimum(t, n_blk - 1)
        ins = [(gs, blk, lambda t, s, j=j: (s[2 + j], row(t), 0)) for j in range(3)]
        ins += [(a_buf, blk, lambda t, s, j=j: (s[5 + j], row(t), 0)) for j in range(3)]

        def fn(i, o):
            for j in range(3):
                o[0][j] = (i[j][...].astype(F32) + i[3 + j][...].astype(F32)).astype(BF)

        return dict(ins=ins, outs=[((3, R, C), BF, (3, tr, C), lambda t, s: (0, row(t), 0))], fn=fn, aliases={},
                    n_blk=n_blk)

    return build


def reduce_sum(gs, a_buf, b_buf, *, name):
    _, R, C = gs.shape
    tr = _tile(R, 256, 16)
    x, y, c, _ = _chip_slots()
    idx = jnp.stack([4 * x + 2 * y + c, 2 * x + y]).astype(jnp.int32)

    def body(idx_ref, g_ref, a_ref, b0_ref, b1_ref, b2_ref, o_ref):
        o_ref[...] = ((((g_ref[...].astype(F32) + a_ref[...].astype(F32)) + b0_ref[...].astype(F32))
                       + b1_ref[...].astype(F32)) + b2_ref[...].astype(F32))

    blk3 = (None, tr, C)
    return _pcall(body, name=name, grid=(R // tr,),
                  ins=[(gs, blk3, lambda i, s: (s[0], i, 0)), (a_buf, blk3, lambda i, s: (s[1], i, 0)),
                       (b_buf, blk3, lambda i, s: (0, i, 0)), (b_buf, blk3, lambda i, s: (1, i, 0)),
                       (b_buf, blk3, lambda i, s: (2, i, 0))],
                  outs=[((R, C), F32, (tr, C), lambda i, s: (i, 0))], prefetch=idx, semantics=("parallel",))[0]


def adam_rows(g, w, m, v, *, name):
    R, C = g.shape
    tr = _tile(R, 256, 8)

    def body(g_ref, w_ref, m_ref, v_ref, d_ref, nm_ref, nv_ref):
        d, nm, nv = _adamw(g_ref[...], w_ref[...], m_ref[...], v_ref[...])
        d_ref[...] = d
        nm_ref[...] = nm
        nv_ref[...] = nv

    spec = ((tr, C), lambda i: (i, 0))
    return _pcall(body, name=name, grid=(R // tr,), ins=[(t, *spec) for t in (g, w, m, v)],
                  outs=[((R, C), F32, *spec)] * 3, semantics=("parallel",))


def sum_rows8(gathered, rows, *, name):
    W = gathered.shape[1]

    def body(g_ref, o_ref):
        acc = g_ref[0:rows, :]
        for d in range(1, N_DEV):
            acc = acc + g_ref[d * rows:(d + 1) * rows, :]
        o_ref[...] = acc

    return _pcall(body, name=name, grid=(), ins=[(gathered, gathered.shape, lambda: (0, 0))],
                  outs=[((rows, W), F32, (rows, W), lambda: (0, 0))])[0]


HBM_SPEC = pl.BlockSpec(memory_space=pltpu.HBM)
SEM_SPEC = pl.BlockSpec(memory_space=pltpu.SEMAPHORE)
ANY_SPEC = pl.BlockSpec(memory_space=pl.ANY)
DATAFLOW = pltpu.SideEffectType.DATAFLOW_SIDE_EFFECTING


def _in_hbm(v):
    return pltpu.with_memory_space_constraint(v, pltpu.HBM)


def _slot(p):
    return 4 * p[0] + 2 * p[1] + p[2]


def _gather_peers():
    x, y, c, chips = _chip_slots()
    return (x, y, c), [(x, y, 1 - c)] + [(*chip, c) for chip in chips]


def gather_start(groups, after, *, name):
    flat = [s for g in groups for s in g]
    n, n_g = len(flat), len(groups)
    where = [(gi, ti) for gi, g in enumerate(groups) for ti in range(len(g))]

    def body(*refs):
        src, land = refs[:n], refs[n:2 * n]
        sems = refs[2 * n + 1:2 * n + 1 + 2 * n_g]
        me, peers = _gather_peers()
        for t in range(n):
            gi, ti = where[t]
            for k, to in enumerate(peers):
                pltpu.make_async_remote_copy(
                    src_ref=src[t], dst_ref=land[t].at[_slot(me)], send_sem=sems[2 * gi].at[4 * ti + k],
                    recv_sem=sems[2 * gi + 1].at[4 * ti + k], device_id=to, device_id_type=MESH).start()
        refs[-1][...] = jnp.zeros_like(refs[-1])

    out_shape = []
    for g in groups:
        out_shape += [pltpu.SemaphoreType.DMA((4 * len(g),)), pltpu.SemaphoreType.DMA((4 * len(g),))]
    out_shape += [pltpu.HBM(s.shape, s.dtype) for s in flat]
    out_shape += [pltpu.HBM((N_DEV,) + s.shape, s.dtype) for s in flat]
    out_shape += [jax.ShapeDtypeStruct((8, LANES), F32)]
    aliases = {t: 2 * n_g + t for t in range(n)}
    aliases.update({n + t: 2 * n_g + n + t for t in range(n)})
    res = pl.pallas_call(
        body, name=name, out_shape=out_shape, in_specs=[HBM_SPEC] * (2 * n) + [ANY_SPEC],
        out_specs=[SEM_SPEC] * (2 * n_g) + [HBM_SPEC] * (2 * n) + [pl.BlockSpec(memory_space=pltpu.VMEM)],
        input_output_aliases=aliases, compiler_params=pltpu.CompilerParams(has_side_effects=DATAFLOW),
    )(*[_in_hbm(s) for s in flat], *[_in_hbm(lax.empty((N_DEV,) + s.shape, s.dtype)) for s in flat], after)
    out, off = [], 0
    for gi, g in enumerate(groups):
        k = len(g)
        out.append((res[2 * gi], res[2 * gi + 1], res[2 * n_g + off:2 * n_g + off + k],
                    res[2 * n_g + n + off:2 * n_g + n + off + k]))
        off += k
    return out, res[-1]


def gather_wait(started, after, *, name):
    send_sems, recv_sems, srcs, lands = started
    n = len(srcs)
    after = list(after)

    def body(*refs):
        src, land = refs[:n], refs[n:2 * n]
        send, recv = refs[2 * n], refs[2 * n + 1]
        _, peers = _gather_peers()
        for t in range(n):
            for k, frm in enumerate(peers):
                cp = pltpu.make_async_remote_copy(
                    src_ref=src[t], dst_ref=land[t].at[_slot(frm)], send_sem=send.at[4 * t + k],
                    recv_sem=recv.at[4 * t + k],
                    device_id=frm, device_id_type=MESH)
                cp.wait_send()
                cp.wait_recv()

    res = pl.pallas_call(
        body, name=name,
        out_shape=[pltpu.HBM(s.shape, s.dtype) for s in srcs] + [pltpu.HBM(l.shape, l.dtype) for l in lands],
        in_specs=[HBM_SPEC] * (2 * n) + [SEM_SPEC, SEM_SPEC] + [ANY_SPEC] * len(after),
        out_specs=[HBM_SPEC] * (2 * n), input_output_aliases={t: t for t in range(2 * n)},
        compiler_params=pltpu.CompilerParams(has_side_effects=DATAFLOW),
    )(*srcs, *lands, send_sems, recv_sems, *after)
    return res[:n], res[n:]


def place_own(src, land, *, name):
    R, C = src.shape
    tr = _tile(R, 512, 16)
    x, y, c, _ = _chip_slots()
    idx = jnp.stack([4 * x + 2 * y + c]).astype(jnp.int32)

    def body(idx_ref, s_ref, land_ref, o_ref):
        o_ref[...] = s_ref[...]

    return _pcall(body, name=name, grid=(R // tr,),
                  ins=[(src, (tr, C), lambda i, s: (i, 0)), (land, None, None)],
                  outs=[(land.shape, land.dtype, (None, tr, C), lambda i, s: (s[0], i, 0))],
                  prefetch=idx, aliases={2: 0}, semantics=("parallel",))[0]


def gather_finish(srcs, lands, *, name):
    n = len(srcs)

    def body(*refs):
        land = refs[n:2 * n]
        send_sems, recv_sems = refs[2 * n:]
        x, y, c, chips = _chip_slots()
        me, sibling = (x, y, c), (x, y, 1 - c)

        def copy(t, j, block, to):
            return pltpu.make_async_remote_copy(
                src_ref=land[t].at[_slot(block)], dst_ref=land[t].at[_slot(block)], send_sem=send_sems.at[t, j],
                recv_sem=recv_sems.at[t, j], device_id=to, device_id_type=MESH)

        sends = [copy(t, j, (*chip, c), sibling) for t in range(n) for j, chip in enumerate(chips)]
        for cp in sends:
            cp.start()
        for t in range(n):
            for j, chip in enumerate(chips):
                copy(t, j, (*chip, 1 - c), me).wait_recv()
        for cp in sends:
            cp.wait_send()

    passed = pl.pallas_call(
        body, name=name, out_shape=[jax.ShapeDtypeStruct(l.shape, l.dtype) for l in lands],
        in_specs=[ANY_SPEC] * n, out_specs=[ANY_SPEC] * n,
        input_output_aliases={t: t for t in range(n)},
        scratch_shapes=[pltpu.SemaphoreType.DMA((n, 3)), pltpu.SemaphoreType.DMA((n, 3))],
    )(*lands)
    return [place_own(s, l, name=f"{name}_own{t}") for t, (s, l) in enumerate(zip(srcs, passed))]


def chips_start(pairs, *, name):
    n = len(pairs)

    def body(*refs):
        src, land = refs[:n], refs[n:2 * n]
        send, recv = refs[2 * n], refs[2 * n + 1]
        token = refs[-1]
        x, y, c, chips = _chip_slots()
        for t in range(n):
            for j, chip in enumerate(chips):
                pltpu.make_async_remote_copy(
                    src_ref=src[t].at[j], dst_ref=land[t].at[j], send_sem=send.at[3 * t + j],
                    recv_sem=recv.at[3 * t + j], device_id=(*chip, c), device_id_type=MESH).start()
        token[...] = jnp.zeros_like(token)

    res = pl.pallas_call(
        body, name=name,
        out_shape=[pltpu.SemaphoreType.DMA((3 * n,)), pltpu.SemaphoreType.DMA((3 * n,))]
        + [pltpu.HBM(p.shape, p.dtype) for p in pairs] * 2 + [jax.ShapeDtypeStruct((8, LANES), F32)],
        in_specs=[HBM_SPEC] * (2 * n),
        out_specs=[SEM_SPEC, SEM_SPEC] + [HBM_SPEC] * (2 * n) + [pl.BlockSpec(memory_space=pltpu.VMEM)],
        input_output_aliases={t: 2 + t for t in range(2 * n)},
        compiler_params=pltpu.CompilerParams(has_side_effects=DATAFLOW),
    )(*[_in_hbm(p) for p in pairs], *[_in_hbm(lax.empty(p.shape, p.dtype)) for p in pairs])
    return res[0], res[1], res[2:2 + n], res[2 + n:2 + 2 * n], res[-1]


def chips_wait(started, after, *, name):
    send_sems, recv_sems, srcs, lands, _ = started
    n = len(srcs)

    def body(*refs):
        src, land = refs[:n], refs[n:2 * n]
        send, recv = refs[2 * n], refs[2 * n + 1]
        x, y, c, chips = _chip_slots()
        for t in range(n):
            for j, chip in enumerate(chips):
                cp = pltpu.make_async_remote_copy(
                    src_ref=src[t].at[j], dst_ref=land[t].at[j], send_sem=send.at[3 * t + j],
                    recv_sem=recv.at[3 * t + j], device_id=(*chip, c), device_id_type=MESH)
                cp.wait_send()
                cp.wait_recv()

    res = pl.pallas_call(
        body, name=name, out_shape=[pltpu.HBM(s.shape, s.dtype) for s in srcs] * 2,
        in_specs=[HBM_SPEC] * (2 * n) + [SEM_SPEC, SEM_SPEC, ANY_SPEC], out_specs=[HBM_SPEC] * (2 * n),
        input_output_aliases={t: t for t in range(2 * n)},
        compiler_params=pltpu.CompilerParams(has_side_effects=DATAFLOW),
    )(*srcs, *lands, send_sems, recv_sems, after)
    return res[n:]


def _sibling_copies(src, land, send, recv, n):
    x, y, c, _ = _chip_slots()
    return [pltpu.make_async_remote_copy(
        src_ref=src[t].at[4 * (q // 2) + 2 * (q % 2) + (1 - c)], dst_ref=land[t].at[q], send_sem=send.at[4 * t + q],
        recv_sem=recv.at[4 * t + q], device_id=(x, y, 1 - c), device_id_type=MESH)
        for t in range(n) for q in range(4)]


def sibling_start(gs, *, name):
    n = len(gs)

    def body(*refs):
        for cp in _sibling_copies(refs[:n], refs[n:2 * n], refs[2 * n], refs[2 * n + 1], n):
            cp.start()
        refs[-1][...] = jnp.zeros_like(refs[-1])

    lands = [lax.empty((4,) + g.shape[1:], g.dtype) for g in gs]
    res = pl.pallas_call(
        body, name=name,
        out_shape=[pltpu.SemaphoreType.DMA((4 * n,)), pltpu.SemaphoreType.DMA((4 * n,))]
        + [pltpu.HBM(g.shape, g.dtype) for g in gs] + [pltpu.HBM(l.shape, l.dtype) for l in lands]
        + [jax.ShapeDtypeStruct((8, LANES), F32)],
        in_specs=[HBM_SPEC] * (2 * n),
        out_specs=[SEM_SPEC, SEM_SPEC] + [HBM_SPEC] * (2 * n) + [pl.BlockSpec(memory_space=pltpu.VMEM)],
        input_output_aliases={t: 2 + t for t in range(2 * n)},
        compiler_params=pltpu.CompilerParams(has_side_effects=DATAFLOW),
    )(*[_in_hbm(g) for g in gs], *[_in_hbm(l) for l in lands])
    return res[0], res[1], res[2:2 + n], res[2 + n:2 + 2 * n], res[-1]


def sibling_wait(started, after, *, name):
    send_sems, recv_sems, srcs, lands, _ = started
    n = len(srcs)

    def body(*refs):
        for cp in _sibling_copies(refs[:n], refs[n:2 * n], refs[2 * n], refs[2 * n + 1], n):
            cp.wait_send()
            cp.wait_recv()

    res = pl.pallas_call(
        body, name=name,
        out_shape=[pltpu.HBM(s.shape, s.dtype) for s in srcs] + [pltpu.HBM(l.shape, l.dtype) for l in lands],
        in_specs=[HBM_SPEC] * (2 * n) + [SEM_SPEC, SEM_SPEC, ANY_SPEC], out_specs=[HBM_SPEC] * (2 * n),
        input_output_aliases={t: t for t in range(2 * n)},
        compiler_params=pltpu.CompilerParams(has_side_effects=DATAFLOW),
    )(*srcs, *lands, send_sems, recv_sems, after)
    return res[:n], res[n:]


def _rope_slab(cols):
    z = jnp.zeros(cols.shape[:-1] + (HALF_ROPE,), cols.dtype)
    return jnp.concatenate([cols[..., :HALF_ROPE], z, cols[..., HALF_ROPE:], z], axis=-1)


def _rope_unslab(slab):
    return jnp.concatenate([slab[..., :HALF_ROPE], slab[..., 2 * HALF_ROPE:3 * HALF_ROPE]], axis=-1)


def _pack_w_in_t(wt_g):
    s, c, d = wt_g.shape
    w = wt_g.reshape(s * c, d)
    c2, c3 = Q_LORA + KV_LORA, Q_LORA + KV_LORA + QK_ROPE
    r = w[c2:c3]
    z = jnp.zeros((HALF_ROPE, d), w.dtype)
    return jnp.concatenate([w[:c2], w[c3:], r[:HALF_ROPE], z, r[HALF_ROPE:], z], axis=0)


def unpack_w_in_t_grad(dwt, *, name):
    n_rows, d = dwt.shape
    kr = QK_ROPE
    n_out_rows = n_rows - kr
    blk = n_out_rows // 7
    assert blk * 7 == n_out_rows and blk % kr == 0 and n_rows % (2 * kr) == 0
    kr_row = Q_LORA + KV_LORA
    k_mix = kr_row // blk
    off = kr_row - k_mix * blk
    slab_block = (n_rows - 2 * kr) // (2 * kr)

    def body(prev_ref, in_ref, slab_ref, o_ref):
        k = pl.program_id(0)

        @pl.when(k < k_mix)
        def _():
            o_ref[...] = in_ref[...]

        @pl.when(k == k_mix)
        def _():
            o_ref[:off, :] = in_ref[:off, :]
            o_ref[off:off + HALF_ROPE, :] = slab_ref[:HALF_ROPE, :]
            o_ref[off + HALF_ROPE:off + kr, :] = slab_ref[2 * HALF_ROPE:3 * HALF_ROPE, :]
            o_ref[off + kr:, :] = in_ref[off:blk - kr, :]

        @pl.when(k > k_mix)
        def _():
            o_ref[:kr, :] = prev_ref[blk - kr:, :]
            o_ref[kr:, :] = in_ref[:blk - kr, :]

    out = _pcall(body, name=name, grid=(7,),
                 ins=[(dwt, (blk, d), lambda k: (jnp.maximum(k - 1, 0), 0)), (dwt, (blk, d), lambda k: (k, 0)),
                      (dwt, (2 * kr, d), lambda k: (slab_block, 0))],
                 outs=[((n_out_rows, d), dwt.dtype, (blk, d), lambda k: (k, 0))], semantics=("parallel",))[0]
    return out.reshape(N_DEV, n_out_rows // N_DEV, d)


def _rope_tables(positions):
    inv_freq = ROPE_BASE ** (-jnp.arange(0, QK_ROPE, 2, dtype=F32) / QK_ROPE)
    ang = positions.astype(F32)[:, None] * inv_freq
    cos, sin = jnp.cos(ang), jnp.sin(ang)
    z = jnp.zeros_like(cos)
    return jnp.concatenate([cos, z, cos, z], axis=-1), jnp.concatenate([-sin, z, sin, z], axis=-1)


def _mlp_up(x, gain, w1, tag):
    hn = rms_fwd(x, gain, name=f"mlp{tag}_norm")

    def act_epi(acc):
        a = jnp.maximum(acc, 0.0)
        return a, a * a

    T = x.shape[0]
    F = w1.shape[0] * w1.shape[2]
    a, act = mm(hn, w1, name=f"mlp{tag}_up", outs=[((T, F), BF, None), ((T, F), BF, None)], epi=act_epi)
    return hn, a, act


def _mlp_down(x, act, w2, tag, part=0):
    n = w2.shape[1]
    bm = _tile(x.shape[0], MM_TILE)
    bn = _tile(n, MM_TILE)
    per = n // bn
    return mm(act, w2, name=f"mlp{tag}_down{part}", out=((x.shape[0], n), F32), bm=bm, bn=bn,
              epi=lambda acc, r: (acc + r[...],), epi_ins=[(x, (bm, bn), lambda i, j, k: (i, part * per + j))])


def _mlp_bwd_weights(w1, w2, saved, dxb, tag):
    hn, a, act = saved
    T, D = dxb.shape
    F = a.shape[1]
    bm = _tile(T, MM_TILE)
    bn = _tile(F, min(MM_TILE, w1.shape[2]))
    dhid = mm(dxb, w2, tb=True, name=f"mlp{tag}_dhid", out=((T, F), BF), bm=bm, bn=bn,
              epi=lambda acc, a_ref: (2.0 * a_ref[...].astype(F32) * acc,),
              epi_ins=[(a, (bm, bn), lambda i, j, k: (i, j))])
    dw2 = mm(act, dxb, ta=True, name=f"mlp{tag}_dw2", out=((F, D), BF))
    dw1 = mm(hn, dhid, ta=True, name=f"mlp{tag}_dw1", out=(w1.shape, BF))
    return dhid, dw1, dw2.reshape(N_DEV, F // N_DEV, D)


def _reduce_begin(grads, tag):
    return sibling_start(grads, name=f"reduce_sibling_start_{tag}")


def _reduce_continue(sib, after, tag, index):
    grads, a_bufs = sibling_wait(sib, after, name=f"reduce_sibling_wait_{tag}")
    pairs = [run_job(pair_job(g, a), index=index, name=f"pair_sum_{tag}{t}")[0]
             for t, (g, a) in enumerate(zip(grads, a_bufs))]
    return grads, a_bufs, chips_start(pairs, name=f"reduce_chips_start_{tag}")


def kernel(x, positions, e_norm_mix, e_w_in, e_q_norm, e_w_uq, e_kv_norm, e_w_ukv, e_v_norm, e_sgu_w, e_sgu_b, e_mla_out_norm, e_sgu_out_norm, e_w_out, o_norm_mix, o_w_in, o_conv_w, o_w_out, mlp_norm, mlp_w1, mlp_w2, final_norm, loss_target, m_e_norm_mix, m_e_w_in, m_e_q_norm, m_e_w_uq, m_e_kv_norm, m_e_w_ukv, m_e_v_norm, m_e_sgu_w, m_e_sgu_b, m_e_mla_out_norm, m_e_sgu_out_norm, m_e_w_out, m_o_norm_mix, m_o_w_in, m_o_conv_w, m_o_w_out, m_mlp_norm, m_mlp_w1, m_mlp_w2, m_final_norm, v_e_norm_mix, v_e_w_in, v_e_q_norm, v_e_w_uq, v_e_kv_norm, v_e_w_ukv, v_e_v_norm, v_e_sgu_w, v_e_sgu_b, v_e_mla_out_norm, v_e_sgu_out_norm, v_e_w_out, v_o_norm_mix, v_o_w_in, v_o_conv_w, v_o_w_out, v_mlp_norm, v_mlp_w1, v_mlp_w2, v_final_norm):
    T, D = x.shape[1], x.shape[2]
    d_shard = o_norm_mix.shape[1]
    x0 = x[0]
    target = loss_target[0]
    me = 4 * lax.axis_index("x") + 2 * lax.axis_index("y") + lax.axis_index("c")

    bf = lambda s: s.astype(BF)
    gather_groups = [[bf(jnp.transpose(e_w_in[0])), bf(e_w_uq[0]), bf(e_w_ukv[0])], [bf(e_w_out[0]), bf(mlp_w1[0])],
                     [bf(mlp_w2[0]), bf(o_w_in[0])], [bf(o_w_out[0]), bf(mlp_w1[1])], [bf(mlp_w2[1])]]
    small_rows = jnp.concatenate([o_norm_mix, o_conv_w[0], jnp.zeros((4, d_shard), F32)], axis=0)
    gather_groups[0].insert(0, small_rows)
    started, start_token = gather_start(gather_groups[:1], x0, name="gather_start0")
    started_rest, rest_token = gather_start(gather_groups[1:], start_token, name="gather_start1")
    started += started_rest

    def gathered(gi, after):
        srcs, lands = gather_wait(started[gi], after, name=f"gather_wait{gi}")
        return gather_finish(srcs, lands, name=f"gather_finish{gi}")

    w_tril = jnp.tril(e_sgu_w[0])
    w_tril_b = w_tril.astype(BF)
    w_tril_tb = jnp.swapaxes(w_tril, 1, 2).astype(BF)
    b_full = jnp.repeat(e_sgu_b[0].T, CH, axis=1)
    v_gain = e_v_norm[0].reshape(1, SGU_OUT)
    cos_t, sin_t = _rope_tables(positions[0])
    mlp_gain = [mlp_norm[0:1], mlp_norm[1:2]]
    final_gain = final_norm.reshape(1, D)

    h0 = rms_fwd(x0, e_norm_mix, name="e_norm", deps=[rest_token])
    small_g, g_w_in_t, g_w_uq, w_ukv = gathered(0, [h0, cos_t, sin_t, w_tril_b, w_tril_tb, b_full])
    o_norm_full = small_g[:, 0, :].reshape(1, D)
    conv_w_full = jnp.transpose(small_g[:, 1:4, :], (1, 0, 2)).reshape(3, D)
    w_in_t = _pack_w_in_t(g_w_in_t)
    w_uq = jnp.concatenate([g_w_uq[..., :QK_NOPE], _rope_slab(g_w_uq[..., QK_NOPE:])], axis=-1)
    proj = mm(h0, w_in_t, tb=True, name="e_in", out=((T, w_in_t.shape[0]), F32), bn=_tile(w_in_t.shape[0], 640))
    qn, kvn, krope = mla_prep(proj, e_q_norm, e_kv_norm, cos_t, sin_t, name="mla_prep")
    bm = _tile(T, MM_TILE)

    def q_epi(acc, cos_ref, sin_ref):
        return (jnp.concatenate([acc[:, :QK_NOPE], _rope_fwd(acc[:, QK_NOPE:], cos_ref[...], sin_ref[...])], axis=-1),)

    q = mm(qn, w_uq, name="mla_q", out=((T, HEADS * HEAD_PAD), BF), bm=bm, bn=HEAD_PAD, epi=q_epi,
           epi_ins=[(cos_t, (bm, LANES), lambda i, j, k: (i, 0)), (sin_t, (bm, LANES), lambda i, j, k: (i, 0))])

    def kv_epi(acc, kr_ref):
        return jnp.concatenate([acc[:, :QK_NOPE].astype(BF), kr_ref[...]], axis=-1), acc[:, QK_NOPE:]

    k, v = mm(kvn, w_ukv, name="mla_kv", bm=bm, bn=HEAD_PAD, epi=kv_epi,
              outs=[((T, HEADS * HEAD_PAD), BF, HEAD_PAD), ((T, MLA_OUT), BF, V_HEAD)],
              epi_ins=[(krope, (bm, LANES), lambda i, j, k: (i, 0))])
    attn, attn_lse = attn_fwd(q, k, v, name="attn_fwd")
    mixed = mix_fwd(attn, proj, e_mla_out_norm, e_sgu_out_norm, v_gain, w_tril_b, b_full, name="mix_fwd")
    bn = _tile(D, MM_TILE)
    g_w_out_e, w1_0 = gathered(1, [mixed])
    w_out_e = g_w_out_e.reshape(-1, D)
    x1 = mm(mixed, w_out_e, name="e_out", out=((T, D), F32), bm=bm, bn=bn,
            epi=lambda acc, r: (acc + r[...],), epi_ins=[(x0, (bm, bn), lambda i, j, k: (i, j))])
    hn0, a0, act0 = _mlp_up(x1, mlp_gain[0], w1_0, 0)
    g_w2_0, g_w_in_o = gathered(2, [act0])
    w2_0 = g_w2_0.reshape(-1, D)
    x2 = _mlp_down(x1, act0, w2_0, 0)
    ho = rms_fwd(x2, o_norm_full, name="o_norm")
    proj_o = mm(ho, g_w_in_o, name="o_in", out=((T, 3 * D), F32))
    gated = conv_fwd(proj_o, conv_w_full, name="conv_fwd")
    g_w_out_o, w1_1 = gathered(3, [gated])
    w_out_o = g_w_out_o.reshape(-1, D)
    x3 = mm(gated, w_out_o, name="o_out", out=((T, D), F32), bm=bm, bn=bn,
            epi=lambda acc, r: (acc + r[...],), epi_ins=[(x2, (bm, bn), lambda i, j, k: (i, j))])
    hn1, a1, act1 = _mlp_up(x3, mlp_gain[1], w1_1, 1)
    (g_w2_1,) = gathered(4, [act1])
    w2_1 = g_w2_1.reshape(-1, D)
    x4 = _mlp_down(x3, act1, w2_1, 1)
    w1, w2 = [w1_0, w1_1], [w2_0, w2_1]

    dx4, dx4b, d_final, loss_part = loss_bwd([x4], final_gain, target, name="loss_bwd")

    hosted = dict(job_index=device_index())
    dhid1, dw1_1, dw2_1 = _mlp_bwd_weights(w1[1], w2[1], (hn1, a1, act1), dx4b, 1)
    sib_r0 = _reduce_begin([dw1_1, dw2_1], "r0")
    dhn1 = mm(dhid1, w1[1], tb=True, name="mlp1_dhn", out=((T, D), F32), deps=[sib_r0[-1]])
    grads_r0, a_r0 = sibling_wait(sib_r0, dhn1, name="reduce_sibling_wait_r0")
    dx3, dx3b, d_mlp1 = rms_bwd(x3, mlp_gain[1], dhn1, dres=dx4, name="mlp1_norm_bwd")

    dgated, ((pair_r0a,),) = mm(dx3b, w_out_o, tb=True, name="o_out_dx", out=((T, D), F32),
                                jobs=[pair_job(grads_r0[0], a_r0[0])], **hosted)
    dw_out_o, ((pair_r0b,),) = mm(gated, dx3b, ta=True, name="o_out_dw", out=((D, D), BF),
                                  jobs=[pair_job(grads_r0[1], a_r0[1])], **hosted)
    st_r0 = chips_start([pair_r0a, pair_r0b], name="reduce_chips_start_r0")
    dproj_o, dconv_full = conv_bwd(dgated, proj_o, conv_w_full, name="conv_bwd", deps=[st_r0[-1]])
    dw_in_o = mm(ho, dproj_o, ta=True, name="o_in_dw", out=(g_w_in_o.shape, BF))
    sib_r1 = _reduce_begin([dw_out_o.reshape(g_w_out_o.shape), dw_in_o], "r1")
    dho = mm(dproj_o, g_w_in_o, tb=True, name="o_in_dx", out=((T, D), F32), deps=[sib_r1[-1]])
    grads_r1, a_r1 = sibling_wait(sib_r1, dho, name="reduce_sibling_wait_r1")
    dx2, dx2b, d_onorm_full = rms_bwd(x2, o_norm_full, dho, dres=dx3, name="o_norm_bwd")

    d_ff = a0.shape[1]
    bm_h, bn_h = _tile(T, MM_TILE), _tile(d_ff, min(MM_TILE, w1[0].shape[2]))
    dhid0, ((pair_r1a,), (pair_r1b,)) = mm(
        dx2b, w2[0], tb=True, name="mlp0_dhid", out=((T, d_ff), BF), bm=bm_h, bn=bn_h,
        epi=lambda acc, a_ref: (2.0 * a_ref[...].astype(F32) * acc,),
        epi_ins=[(a0, (bm_h, bn_h), lambda i, j, k: (i, j))],
        jobs=[pair_job(grads_r1[0], a_r1[0]), pair_job(grads_r1[1], a_r1[1])], **hosted)
    st_r1 = chips_start([pair_r1a, pair_r1b], name="reduce_chips_start_r1")
    dw2_0 = mm(act0, dx2b, ta=True, name="mlp0_dw2", out=((d_ff, D), BF), deps=[st_r1[-1]])
    b_r0 = chips_wait(st_r0, dw2_0, name="reduce_chips_wait_r0")
    dw1_0, (r_w1, r_w2) = mm(
        hn0, dhid0, ta=True, name="mlp0_dw1", out=(w1[0].shape, BF),
        jobs=[adam_job(grads_r0[0], a_r0[0], b_r0[0], mlp_w1, m_mlp_w1, v_mlp_w1, 1, None),
              adam_job(grads_r0[1], a_r0[1], b_r0[1], mlp_w2, m_mlp_w2, v_mlp_w2, 1, None)], **hosted)
    sib_r2 = _reduce_begin([dw1_0, dw2_0.reshape(N_DEV, d_ff // N_DEV, D)], "r2")
    dhn0 = mm(dhid0, w1[0], tb=True, name="mlp0_dhn", out=((T, D), F32), deps=[sib_r2[-1]])
    grads_r2, a_r2 = sibling_wait(sib_r2, dhn0, name="reduce_sibling_wait_r2")
    dx1, dx1b, d_mlp0 = rms_bwd(x1, mlp_gain[0], dhn0, dres=dx2, name="mlp0_norm_bwd")

    dmixed, ((pair_r2a,),) = mm(dx1b, w_out_e, tb=True, name="e_out_dx", out=((T, MLA_OUT + SGU_OUT), F32),
                                jobs=[pair_job(grads_r2[0], a_r2[0])], **hosted)
    dw_out_e, ((pair_r2b,),) = mm(mixed, dx1b, ta=True, name="e_out_dw", out=(w_out_e.shape, BF),
                                  jobs=[pair_job(grads_r2[1], a_r2[1])], **hosted)
    st_r2 = chips_start([pair_r2a, pair_r2b], name="reduce_chips_start_r2")
    (dattn, dproj, d_mla_out, d_sgu_out, d_vgain, d_sgu_w, d_b_full) = mix_bwd(
        dmixed, attn, proj, e_mla_out_norm, e_sgu_out_norm, v_gain, w_tril_b, w_tril_tb, b_full, name="mix_bwd",
        deps=[st_r2[-1]])
    b_r1 = chips_wait(st_r1, dattn, name="reduce_chips_wait_r1")
    dq, dk, dv = attn_bwd(q, k, v, attn, attn_lse, dattn, name="attn_bwd")
    dq_lin, dkv_lin, dproj = mla_bwd_prep(dq, dk, dv, cos_t, sin_t, dproj, name="mla_bwd_prep")
    dw_uq_pad = mm(qn, dq_lin, ta=True, name="mla_q_dw", out=(w_uq.shape, BF))
    dw_ukv = mm(kvn, dkv_lin, ta=True, name="mla_kv_dw", out=(w_ukv.shape, BF))
    dw_uq = jnp.concatenate([dw_uq_pad[..., :QK_NOPE], _rope_unslab(dw_uq_pad[..., QK_NOPE:])], axis=-1)
    sib_r2b = _reduce_begin([dw_out_e.reshape(g_w_out_e.shape), dw_uq, dw_ukv], "r2b")
    dqn = mm(dq_lin, w_uq, tb=True, name="mla_q_dx", out=((T, Q_LORA), F32), deps=[sib_r2b[-1]])
    dkvn = mm(dkv_lin, w_ukv, tb=True, name="mla_kv_dx", out=((T, KV_LORA), F32), deps=[sib_r2b[-1]])
    grads_r2b, a_r2b, st_r2b = _reduce_continue(sib_r2b, dkvn, "r2b", hosted["job_index"])
    dproj, d_qnorm = rms_bwd(proj, e_q_norm, dqn, col_block=0, want_f32=False, into=dproj, name="q_norm_bwd",
                             deps=[st_r2b[-1]])
    dproj, d_kvnorm = rms_bwd(proj, e_kv_norm, dkvn, col_block=1, want_f32=False, into=dproj, name="kv_norm_bwd")
    dw_in_t_pad, (r_w_out_o, r_w_in_o) = mm(
        dproj, h0, ta=True, name="e_in_dw", out=(w_in_t.shape, BF), bm=_tile(w_in_t.shape[0], 640),
        jobs=[adam_job(grads_r1[0], a_r1[0], b_r1[0], o_w_out, m_o_w_out, v_o_w_out, 0, None),
              adam_job(grads_r1[1], a_r1[1], b_r1[1], o_w_in, m_o_w_in, v_o_w_in, 0, None)], **hosted)
    dw_in_t = unpack_w_in_t_grad(dw_in_t_pad, name="e_in_dw_unpack")
    sib_r3 = _reduce_begin([dw_in_t], "r3")
    dh0 = mm(dproj, w_in_t, name="e_in_dx", out=((T, D), F32), deps=[sib_r3[-1]])
    grads_r3, a_r3, st_r3 = _reduce_continue(sib_r3, dh0, "r3", hosted["job_index"])
    tok_r3 = st_r3[-1]
    grad_x, d_enorm = rms_bwd(x0, e_norm_mix, dh0, dres=dx1, want_bf=False, name="e_norm_bwd", deps=[tok_r3])
    b_r2 = chips_wait(st_r2, grad_x, name="reduce_chips_wait_r2")

    d_sgu_b = jnp.transpose(d_b_full[:, ::CH])
    d_sgu_w_tril = jnp.tril(d_sgu_w)
    rep = [("e_norm_mix", e_norm_mix, m_e_norm_mix, v_e_norm_mix, d_enorm),
           ("e_q_norm", e_q_norm, m_e_q_norm, v_e_q_norm, d_qnorm),
           ("e_kv_norm", e_kv_norm, m_e_kv_norm, v_e_kv_norm, d_kvnorm),
           ("e_v_norm", e_v_norm, m_e_v_norm, v_e_v_norm, d_vgain),
           ("e_sgu_w", e_sgu_w, m_e_sgu_w, v_e_sgu_w, d_sgu_w_tril),
           ("e_sgu_b", e_sgu_b, m_e_sgu_b, v_e_sgu_b, d_sgu_b),
           ("e_mla_out_norm", e_mla_out_norm, m_e_mla_out_norm, v_e_mla_out_norm, d_mla_out),
           ("e_sgu_out_norm", e_sgu_out_norm, m_e_sgu_out_norm, v_e_sgu_out_norm, d_sgu_out),
           ("mlp_norm", mlp_norm, m_mlp_norm, v_mlp_norm, jnp.concatenate([d_mlp0, d_mlp1], axis=0)),
           ("final_norm", final_norm, m_final_norm, v_final_norm, d_final)]
    sizes = [int(np.prod(r[1].shape)) for r in rep]
    n_rep = sum(sizes)
    n_all = n_rep + 4 * D + 1
    width = -(-n_all // (8 * LANES)) * LANES
    pad = 8 * width - n_all
    flat = jnp.concatenate([r[4].reshape(-1) for r in rep]
                           + [d_onorm_full.reshape(-1), dconv_full.reshape(-1), loss_part[0, :1],
                              jnp.zeros((pad,), F32)])
    small_started, small_token = gather_start([[flat.reshape(8, width)]], b_r2[0], name="gather_small_grads_start")

    def finish(grads, a_bufs, b_bufs, t, w, m, v, layer=0, prev=None, tag="", deps=()):
        return run_job(adam_job(grads[t], a_bufs[t], b_bufs[t], w, m, v, layer, prev), index=hosted["job_index"],
                       name=f"adam_{tag}", deps=deps)

    r_w1 = finish(grads_r2, a_r2, b_r2, 0, mlp_w1, m_mlp_w1, v_mlp_w1, 0, r_w1, tag="w1_l0", deps=[tok_r3, small_token])
    r_w2 = finish(grads_r2, a_r2, b_r2, 1, mlp_w2, m_mlp_w2, v_mlp_w2, 0, r_w2, tag="w2_l0", deps=[r_w1[1]])
    b_r2b = chips_wait(st_r2b, r_w2[1], name="reduce_chips_wait_r2b")
    r_w_out_e = finish(grads_r2b, a_r2b, b_r2b, 0, e_w_out, m_e_w_out, v_e_w_out, tag="e_w_out")
    r_w_uq = finish(grads_r2b, a_r2b, b_r2b, 1, e_w_uq, m_e_w_uq, v_e_w_uq, tag="e_w_uq")
    r_w_ukv = finish(grads_r2b, a_r2b, b_r2b, 2, e_w_ukv, m_e_w_ukv, v_e_w_ukv, tag="e_w_ukv")
    b_r3 = chips_wait(st_r3, r_w_out_e[1], name="reduce_chips_wait_r3")
    g_w_in_t = reduce_sum(grads_r3[0], a_r3[0], b_r3[0], name="sum_e_w_in")
    w_in_upd_t = adam_rows(g_w_in_t, jnp.transpose(e_w_in[0]), jnp.transpose(m_e_w_in[0]), jnp.transpose(v_e_w_in[0]),
                           name="adam_e_w_in")
    r_w_in = [jnp.transpose(t)[None] for t in (g_w_in_t, *w_in_upd_t)]

    small_srcs, small_lands = gather_wait(small_started[0], [r_w2[1]], name="gather_small_grads_wait")
    small_all = gather_finish(small_srcs, small_lands, name="gather_small_grads_finish")[0]
    summed = sum_rows8(small_all.reshape(N_DEV * 8, width), 8, name="sum_small_grads").reshape(-1)

    loss = summed[n_rep + 4 * D]

    def pack_rep(i):
        return jnp.concatenate([r[i].reshape(-1) for r in rep]).reshape(n_rep // LANES, LANES)

    g_rep = summed[:n_rep].reshape(n_rep // LANES, LANES)
    d_rep, nm_rep, nv_rep = adam_flat(g_rep, pack_rep(1), pack_rep(2), pack_rep(3), name="adam_replicated")

    def unpack_rep(flat2d):
        out, off = {}, 0
        f = flat2d.reshape(-1)
        for r, n in zip(rep, sizes):
            out[r[0]] = f[off:off + n].reshape(r[1].shape)
            off += n
        return out

    small = {"grad": unpack_rep(g_rep), "delta": unpack_rep(d_rep), "new_m": unpack_rep(nm_rep),
             "new_v": unpack_rep(nv_rep)}
    g_onorm = lax.dynamic_slice(summed[n_rep:n_rep + D].reshape(1, D), (0, me * d_shard), (1, d_shard))
    g_conv = lax.dynamic_slice(summed[n_rep + D:n_rep + 4 * D].reshape(3, D), (0, me * d_shard), (3, d_shard))

    def pack_sharded(norm_part, conv_part):
        return jnp.concatenate([norm_part, conv_part, jnp.zeros((4, d_shard), F32)], axis=0)

    g_sh = pack_sharded(g_onorm, g_conv)
    d_sh, nm_sh, nv_sh = adam_flat(g_sh, pack_sharded(o_norm_mix, o_conv_w[0]), pack_sharded(m_o_norm_mix, m_o_conv_w[0]),
                                   pack_sharded(v_o_norm_mix, v_o_conv_w[0]), name="adam_sharded_small")
    for kind, arr in (("grad", g_sh), ("delta", d_sh), ("new_m", nm_sh), ("new_v", nv_sh)):
        small[kind]["o_norm_mix"] = arr[0:1]
        small[kind]["o_conv_w"] = arr[1:4][None]

    big = {"e_w_in": r_w_in, "e_w_uq": r_w_uq, "e_w_ukv": r_w_ukv, "e_w_out": r_w_out_e, "o_w_in": r_w_in_o,
           "o_w_out": r_w_out_o, "mlp_w1": r_w1, "mlp_w2": r_w2}
    order = ["e_norm_mix", "e_w_in", "e_q_norm", "e_w_uq", "e_kv_norm", "e_w_ukv", "e_v_norm", "e_sgu_w", "e_sgu_b",
             "e_mla_out_norm", "e_sgu_out_norm", "e_w_out", "o_norm_mix", "o_w_in", "o_conv_w", "o_w_out", "mlp_norm",
             "mlp_w1", "mlp_w2", "final_norm"]
    result = [loss, grad_x[None]]
    for ki, kind in enumerate(("grad", "delta", "new_m", "new_v")):
        for nm in order:
            result.append(big[nm][ki] if nm in big else small[kind][nm])
    return tuple(result)
```

```python
import numpy as np
import jax
import jax.numpy as jnp
from jax import lax
from jax.experimental import pallas as pl
from jax.experimental.pallas import tpu as pltpu

BF = jnp.bfloat16
F32 = jnp.float32
MESH = pl.DeviceIdType.MESH
N_DEV = 8

EPS = 1e-6
HEADS = 8
Q_LORA = 512
KV_LORA = 512
QK_NOPE = 128
QK_ROPE = 64
HALF_ROPE = QK_ROPE // 2
V_HEAD = 128
HEAD_PAD = 256
ROPE_BASE = 10000.0
GROUPS = 8
CH = 128
CHUNK = 128
SGU_OUT = GROUPS * CH
MLA_OUT = HEADS * V_HEAD
ATTN_SCALE = float((QK_NOPE + QK_ROPE) ** -0.5)

ADAM_LR = 0.001
ADAM_B1 = 0.9
ADAM_B2 = 0.999
ADAM_EPS = 1e-08
ADAM_WD = 0.01
ADAM_STEP = 10
ADAM_C1 = 1.0 - ADAM_B1 ** ADAM_STEP
ADAM_C2 = 1.0 - ADAM_B2 ** ADAM_STEP

V7X_VMEM_BYTES = 64 * 2 ** 20
VMEM_LIMIT_CAP = V7X_VMEM_BYTES - 6 * 2 ** 20
LANES = 128
ROW_TILE = 256
ATTN_TILE = 1024
STREAM_BLOCK_ELEMS = 256 * 1024
MM_TILE = 1024
MM_K_TILE = 2048
MM_K_BLOCK_MAX = 4096


def _padded_bytes(block, dtype):
    dims = [d for d in block if d is not None]
    if len(dims) >= 1:
        dims[-1] = -(-dims[-1] // LANES) * LANES
    if len(dims) >= 2:
        dims[-2] = -(-dims[-2] // 16) * 16
    return int(np.prod(dims)) * jnp.dtype(dtype).itemsize


def _pcall(body, *, name, grid, ins, outs, scratch=(), semantics=None, aliases=None, prefetch=None, deps=()):
    any_spec = pl.BlockSpec(memory_space=pl.ANY)
    if deps:
        n_lead = len(ins) + (1 if prefetch is not None else 0)
        n_deps = len(deps)
        inner = body

        def body(*refs):
            inner(*refs[:n_lead], *refs[n_lead + n_deps:])

        ins = list(ins) + [(d, None, None) for d in deps]
    in_specs = [any_spec if b is None else pl.BlockSpec(b, m) for _, b, m in ins]
    out_specs = [any_spec if b is None else pl.BlockSpec(b, m) for _, _, b, m in outs]
    out_shape = [pltpu.HBM(s, d) for s, d, _, _ in outs]
    est = 0
    for a, b, _ in ins:
        if b is not None:
            est += 2 * _padded_bytes(b, a.dtype)
    for _, d, b, _ in outs:
        if b is not None:
            est += 2 * _padded_bytes(b, d)
    for s in scratch:
        if hasattr(s, "shape") and hasattr(s, "dtype"):
            est += _padded_bytes(s.shape, s.dtype)
    limit = int(min(VMEM_LIMIT_CAP, est + 16 * 2 ** 20))
    params = pltpu.CompilerParams(
        dimension_semantics=semantics or ("arbitrary",) * len(grid), vmem_limit_bytes=limit)
    args = [pltpu.with_memory_space_constraint(a, pltpu.HBM) for a, _, _ in ins]
    if prefetch is not None:
        grid_spec = pltpu.PrefetchScalarGridSpec(
            num_scalar_prefetch=1, grid=grid, in_specs=in_specs, out_specs=out_specs, scratch_shapes=list(scratch))
        call = pl.pallas_call(body, out_shape=out_shape, grid_spec=grid_spec, name=name, compiler_params=params,
                              input_output_aliases=aliases or {})
        return call(prefetch, *args)
    call = pl.pallas_call(body, out_shape=out_shape, grid=grid, in_specs=in_specs, out_specs=out_specs,
                          scratch_shapes=list(scratch), name=name, compiler_params=params,
                          input_output_aliases=aliases or {})
    return call(*args)


def _tile(dim, pref, quantum=LANES):
    if dim <= pref:
        return dim
    t = (pref // quantum) * quantum
    while t >= quantum:
        if dim % t == 0:
            return t
        t -= quantum
    return dim


def _vshape(arr_shape):
    if len(arr_shape) == 2:
        return tuple(arr_shape)
    s, r, c = arr_shape
    return (r, s * c)


def _vblock(arr_shape, br, bc, rc):
    if len(arr_shape) == 2:
        return (br, bc), (lambda *g: rc(*g))
    _, _, c = arr_shape
    assert c % bc == 0, (arr_shape, bc)
    per = c // bc

    def imap(*g):
        ri, ci = rc(*g)
        return (ci // per, ri, ci % per)

    return (None, br, bc), imap


def _shard_width(*shapes):
    w = None
    for s in shapes:
        if len(s) == 3:
            w = s[2] if w is None else int(np.gcd(w, s[2]))
    return w


def mm(a, b, *, name, ta=False, tb=False, out=None, outs=None, epi=None, epi_ins=(), bm=None, bn=None, bk=None,
       deps=(), jobs=(), job_index=None):
    av, bv = _vshape(a.shape), _vshape(b.shape)
    M, K = (av[1], av[0]) if ta else av
    K2, N = (bv[1], bv[0]) if tb else bv
    assert K == K2, (a.shape, b.shape, ta, tb)
    if outs is None:
        outs = [(out[0], out[1], None)]
    a_sw = _shard_width(a.shape)
    b_sw = _shard_width(b.shape)
    o_sw = _shard_width(*[o[0] for o in outs])
    m_lim = a_sw if (ta and a_sw) else None
    k_lim = [w for w in ((a_sw if not ta else None), (b_sw if tb else None)) if w]
    n_lim = [w for w in ((b_sw if not tb else None), o_sw) if w]
    if bm is None:
        bm = _tile(M, min([MM_TILE] + ([m_lim] if m_lim else [])))
    if bn is None:
        bn = _tile(N, min([MM_TILE] + n_lim))
    k_shards = 0
    if tb and len(b.shape) == 3 and bk is None and not (a_sw and not ta):
        k_shards = 1
        while 2 * k_shards <= b.shape[0] and 2 * k_shards * b_sw <= MM_K_BLOCK_MAX:
            k_shards *= 2
        bk = k_shards * b_sw
    if bk is None:
        bk = K if (K <= 4096 and not k_lim) else _tile(K, min([MM_K_TILE] + k_lim))
    assert M % bm == 0 and N % bn == 0 and K % bk == 0, (name, M, N, K, bm, bn, bk)
    nk = K // bk
    grid = (M // bm, N // bn, nk)
    if ta:
        a_blk, a_map = _vblock(a.shape, bk, bm, lambda i, j, k: (k, i))
    else:
        a_blk, a_map = _vblock(a.shape, bm, bk, lambda i, j, k: (i, k))
    if k_shards:
        b_blk, b_map = (k_shards, bn, b_sw), (lambda i, j, k: (k, j, 0))
    elif tb:
        b_blk, b_map = _vblock(b.shape, bn, bk, lambda i, j, k: (j, k))
    else:
        b_blk, b_map = _vblock(b.shape, bk, bn, lambda i, j, k: (k, j))
    dn = (((0 if ta else 1,), (1 if tb else 0,)), ((), ()))
    ins = [(a, a_blk, a_map), (b, b_blk, b_map)] + list(epi_ins)
    out_list = []
    for shape, dtype, cols in outs:
        cols = cols or bn
        blk, imap = _vblock(shape, bm, cols, lambda i, j, k: (i, j))
        out_list.append((shape, dtype, blk, imap))
    n_e, n_o = len(epi_ins), len(out_list)

    n_steps = grid[0] * grid[1] * nk
    built = [job(n_steps) for job in jobs]
    aliases = {}
    job_slices = []
    if built:
        def lin(i, j, k):
            return (i * grid[1] + j) * nk + k

        ins = [(arr, blk, None if blk is None else (lambda i, j, k, s, f=f: f(i, j, k))) for arr, blk, f in ins]
        out_list = [(sh, dt, blk, (lambda i, j, k, s, f=f: f(i, j, k))) for sh, dt, blk, f in out_list]
        n_main_in, n_main_out = len(ins), len(out_list)
        for jb in built:
            i0, o0 = len(ins), len(out_list)
            ins += [(arr, blk, None if blk is None else (lambda i, j, k, s, f=f: f(lin(i, j, k), s)))
                    for arr, blk, f in jb["ins"]]
            out_list += [(sh, dt, blk, (lambda i, j, k, s, f=f: f(lin(i, j, k), s))) for sh, dt, blk, f in jb["outs"]]
            aliases.update({1 + i0 + ai: o0 + ao for ai, ao in jb["aliases"].items()})
            job_slices.append((i0, len(jb["ins"]), o0, len(jb["outs"])))
    n_in_total = len(ins)

    def body(*refs):
        if built:
            refs = refs[1:]
        a_ref, b_ref = refs[0], refs[1]
        e_refs = refs[2:2 + n_e]
        o_refs = refs[n_in_total:n_in_total + n_o]
        for jb, (i0, ni, o0, no) in zip(built, job_slices):
            jb["fn"](refs[i0:i0 + ni], refs[n_in_total + o0:n_in_total + o0 + no])

        def finish(acc):
            res = epi(acc, *e_refs) if epi is not None else (acc,)
            for o_ref, r in zip(o_refs, res):
                o_ref[...] = r.astype(o_ref.dtype)

        x = a_ref[...].astype(BF)
        y = b_ref[...].astype(BF)
        if k_shards:
            p = None
            for s in range(k_shards):
                part = lax.dot_general(x[:, s * b_sw:(s + 1) * b_sw], y[s], dn, preferred_element_type=F32)
                p = part if p is None else p + part
        else:
            p = lax.dot_general(x, y, dn, preferred_element_type=F32)
        if nk == 1:
            finish(p)
        else:
            acc_ref = refs[-1]
            k = pl.program_id(2)

            @pl.when(k == 0)
            def _():
                acc_ref[...] = p

            @pl.when(k > 0)
            def _():
                acc_ref[...] += p

            @pl.when(k == nk - 1)
            def _():
                finish(acc_ref[...])

    scratch = [pltpu.VMEM((bm, bn), F32)] if nk > 1 else []
    res = _pcall(body, name=name, grid=grid, ins=ins, outs=out_list, scratch=scratch, deps=deps,
                 semantics=("parallel", "parallel", "arbitrary"), prefetch=job_index if built else None, aliases=aliases)
    main = res[0] if n_o == 1 else res[:n_o]
    if not built:
        return main
    return main, [res[o0:o0 + no] for _, _, o0, no in job_slices]


_GELU_K = float(np.sqrt(2.0 / np.pi))
_GELU_C = 0.044715


def _gelu(x):
    t = jnp.tanh(_GELU_K * (x + _GELU_C * (x * x * x)))
    return 0.5 * x * (1.0 + t)


def _gelu_grad(x):
    t = jnp.tanh(_GELU_K * (x + _GELU_C * (x * x * x)))
    return 0.5 * (1.0 + t) + 0.5 * x * (1.0 - t * t) * (_GELU_K * (1.0 + 3.0 * _GELU_C * (x * x)))


def _rstd(x):
    return lax.rsqrt(jnp.mean(x * x, axis=-1, keepdims=True) + EPS)


def _rms_bwd(x, gain, dy):
    r = _rstd(x)
    xh = x * r
    gdy = dy * gain
    dx = r * (gdy - xh * jnp.mean(gdy * xh, axis=-1, keepdims=True))
    return dx, dy * xh


def _rope_fwd(x, cos_t, sin_t):
    return x * cos_t + pltpu.roll(x, 2 * HALF_ROPE, 1) * sin_t


def _rope_bwd(dy, cos_t, sin_t):
    return dy * cos_t + pltpu.roll(dy * sin_t, 2 * HALF_ROPE, 1)


def _acc_rows(ref, val, first):
    s = jnp.sum(val, axis=0, keepdims=True)

    @pl.when(first)
    def _():
        ref[...] = s

    @pl.when(jnp.logical_not(first))
    def _():
        ref[...] += s


def rms_fwd(x, gain, *, name, col_block=0, width=None, deps=()):
    T = x.shape[0]
    width = width or x.shape[1]
    tm = _tile(T, ROW_TILE, 8)

    def body(x_ref, g_ref, o_ref):
        v = x_ref[...]
        o_ref[...] = (v * _rstd(v) * g_ref[...]).astype(BF)

    return _pcall(body, name=name, grid=(T // tm,),
                  ins=[(x, (tm, width), lambda i: (i, col_block)), (gain, (1, width), lambda i: (0, 0))],
                  outs=[((T, width), BF, (tm, width), lambda i: (i, 0))], semantics=("parallel",), deps=deps)[0]


def rms_bwd(x, gain, dy, *, name, col_block=0, dres=None, want_f32=True, want_bf=True, into=None, deps=()):
    T, width = dy.shape
    tm = _tile(T, ROW_TILE, 8)
    has_res = dres is not None

    def body(*refs):
        x_ref, g_ref, dy_ref = refs[:3]
        pos = 3
        res_ref = None
        if has_res:
            res_ref = refs[pos]
            pos += 1
        if into is not None:
            pos += 1
        outs = refs[pos:]
        dx, dg_rows = _rms_bwd(x_ref[...], g_ref[...], dy_ref[...])
        if has_res:
            dx = dx + res_ref[...]
        o = 0
        if want_f32:
            outs[o][...] = dx
            o += 1
        if want_bf:
            outs[o][...] = dx.astype(BF)
            o += 1
        _acc_rows(outs[o], dg_rows, pl.program_id(0) == 0)

    ins = [(x, (tm, width), lambda i: (i, col_block)), (gain, (1, width), lambda i: (0, 0)),
           (dy, (tm, width), lambda i: (i, 0))]
    if has_res:
        ins.append((dres, (tm, width), lambda i: (i, 0)))
    outs = []
    aliases = {}
    if want_f32:
        outs.append(((T, width), F32, (tm, width), lambda i: (i, 0)))
    if want_bf and into is not None:
        ins.append((into, None, None))
        aliases[len(ins) - 1] = len(outs)
        outs.append((into.shape, BF, (tm, width), lambda i: (i, col_block)))
    elif want_bf:
        outs.append(((T, width), BF, (tm, width), lambda i: (i, 0)))
    outs.append(((1, width), F32, (1, width), lambda i: (0, 0)))
    return _pcall(body, name=name, grid=(T // tm,), ins=ins, outs=outs, aliases=aliases, deps=deps)


def mla_prep(proj, q_norm, kv_norm, cos_t, sin_t, *, name):
    T = proj.shape[0]
    tm = _tile(T, ROW_TILE, 8)
    kr_block = (proj.shape[1] - LANES) // LANES

    def body(cq_ref, ckv_ref, kr_ref, qg_ref, kg_ref, cos_ref, sin_ref, qn_ref, kvn_ref, krope_ref):
        cq = cq_ref[...]
        qn_ref[...] = (cq * _rstd(cq) * qg_ref[...]).astype(BF)
        ckv = ckv_ref[...]
        kvn_ref[...] = (ckv * _rstd(ckv) * kg_ref[...]).astype(BF)
        krope_ref[...] = _rope_fwd(kr_ref[...], cos_ref[...], sin_ref[...]).astype(BF)

    return _pcall(
        body, name=name, grid=(T // tm,),
        ins=[(proj, (tm, Q_LORA), lambda i: (i, 0)), (proj, (tm, KV_LORA), lambda i: (i, 1)),
             (proj, (tm, LANES), lambda i: (i, kr_block)),
             (q_norm, (1, Q_LORA), lambda i: (0, 0)), (kv_norm, (1, KV_LORA), lambda i: (0, 0)),
             (cos_t, (tm, LANES), lambda i: (i, 0)), (sin_t, (tm, LANES), lambda i: (i, 0))],
        outs=[((T, Q_LORA), BF, (tm, Q_LORA), lambda i: (i, 0)), ((T, KV_LORA), BF, (tm, KV_LORA), lambda i: (i, 0)),
              ((T, LANES), BF, (tm, LANES), lambda i: (i, 0))],
        semantics=("parallel",))


def _attn_scores(q, k_blk, diagonal):
    s = lax.dot_general(q, k_blk, (((1,), (1,)), ((), ())), preferred_element_type=F32) * ATTN_SCALE
    if diagonal:
        row = lax.broadcasted_iota(jnp.int32, s.shape, 0)
        col = lax.broadcasted_iota(jnp.int32, s.shape, 1)
        s = jnp.where(col <= row, s, -jnp.inf)
    return s


def attn_fwd(q, k, v, *, name):
    T = q.shape[0]
    tq = _tile(T, ATTN_TILE, 8)

    def body(q_ref, k_ref, v_ref, o_ref, lse_ref):
        i = pl.program_id(1)
        qv = q_ref[...]

        def block(kb, carry, diagonal):
            m, l, acc = carry
            start = pl.multiple_of(kb * tq, tq)
            s = _attn_scores(qv, k_ref[pl.ds(start, tq), :], diagonal)
            m_new = jnp.maximum(m, jnp.max(s, axis=-1, keepdims=True))
            alpha = jnp.exp(m - m_new)
            p = jnp.exp(s - m_new)
            l = alpha * l + jnp.sum(p, axis=-1, keepdims=True)
            acc = alpha * acc + jnp.dot(p.astype(BF), v_ref[pl.ds(start, tq), :], preferred_element_type=F32)
            return m_new, l, acc

        init = (jnp.full((tq, 1), -jnp.inf, F32), jnp.zeros((tq, 1), F32), jnp.zeros((tq, V_HEAD), F32))
        carry = lax.fori_loop(0, i, lambda kb, c: block(kb, c, False), init)
        m, l, acc = block(i, carry, True)
        o_ref[...] = acc / l
        lse_ref[...] = jnp.broadcast_to(m + jnp.log(l), (tq, V_HEAD))

    return _pcall(
        body, name=name, grid=(HEADS, T // tq),
        ins=[(q, (tq, HEAD_PAD), lambda h, i: (i, h)), (k, (T, HEAD_PAD), lambda h, i: (0, h)),
             (v, (T, V_HEAD), lambda h, i: (0, h))],
        outs=[((T, MLA_OUT), F32, (tq, V_HEAD), lambda h, i: (i, h)),
              ((T, MLA_OUT), F32, (tq, V_HEAD), lambda h, i: (i, h))], semantics=("parallel", "parallel"))


def attn_bwd(q, k, v, o, lse, do, *, name):
    T = q.shape[0]
    tq = _tile(T, ATTN_TILE, 8)

    def body(q_ref, k_ref, v_ref, o_ref, lse_ref, do_ref, dq_ref, dk_ref, dv_ref):
        i = pl.program_id(1)

        @pl.when(i == 0)
        def _():
            dk_ref[...] = jnp.zeros_like(dk_ref)
            dv_ref[...] = jnp.zeros_like(dv_ref)

        qv = q_ref[...]
        do_t = do_ref[...]
        lse_v = lse_ref[:, 0:1]
        delta = jnp.sum(do_t.astype(F32) * o_ref[...], axis=-1, keepdims=True)

        def block(kb, dq, diagonal):
            start = pl.multiple_of(kb * tq, tq)
            k_blk = k_ref[pl.ds(start, tq), :]
            v_blk = v_ref[pl.ds(start, tq), :]
            p = jnp.exp(_attn_scores(qv, k_blk, diagonal) - lse_v)
            dp = lax.dot_general(do_t, v_blk, (((1,), (1,)), ((), ())), preferred_element_type=F32)
            ds = (p * (dp - delta)).astype(BF)
            dk_ref[pl.ds(start, tq), :] += ATTN_SCALE * lax.dot_general(
                ds, qv, (((0,), (0,)), ((), ())), preferred_element_type=F32)
            dv_ref[pl.ds(start, tq), :] += lax.dot_general(p.astype(BF), do_t, (((0,), (0,)), ((), ())),
                                                          preferred_element_type=F32)
            return dq + jnp.dot(ds, k_blk, preferred_element_type=F32)

        dq = lax.fori_loop(0, i, lambda kb, c: block(kb, c, False), jnp.zeros((tq, HEAD_PAD), F32))
        dq_ref[...] = ATTN_SCALE * block(i, dq, True)

    return _pcall(
        body, name=name, grid=(HEADS, T // tq),
        ins=[(q, (tq, HEAD_PAD), lambda h, i: (i, h)), (k, (T, HEAD_PAD), lambda h, i: (0, h)),
             (v, (T, V_HEAD), lambda h, i: (0, h)), (o, (tq, V_HEAD), lambda h, i: (i, h)),
             (lse, (tq, V_HEAD), lambda h, i: (i, h)), (do, (tq, V_HEAD), lambda h, i: (i, h))],
        outs=[((T, HEADS * HEAD_PAD), F32, (tq, HEAD_PAD), lambda h, i: (i, h)),
              ((T, HEADS * HEAD_PAD), F32, (T, HEAD_PAD), lambda h, i: (0, h)),
              ((T, MLA_OUT), F32, (T, V_HEAD), lambda h, i: (0, h))],
        semantics=("parallel", "arbitrary"))


def mla_bwd_prep(dq, dk, dv, cos_t, sin_t, dproj, *, name):
    T = dq.shape[0]
    tm = _tile(T, ROW_TILE, 8)
    kr_block = (dproj.shape[1] - LANES) // LANES

    def body(dq_ref, dk_ref, dv_ref, cos_ref, sin_ref, dproj_in, dql_ref, dkvl_ref, dkr_ref):
        cos_v, sin_v = cos_ref[...], sin_ref[...]
        kr = jnp.zeros((tm, LANES), F32)
        for h in range(HEADS):
            lo = h * HEAD_PAD
            dql_ref[:, lo:lo + QK_NOPE] = dq_ref[:, lo:lo + QK_NOPE].astype(BF)
            dql_ref[:, lo + QK_NOPE:lo + HEAD_PAD] = _rope_bwd(
                dq_ref[:, lo + QK_NOPE:lo + HEAD_PAD], cos_v, sin_v).astype(BF)
            dkvl_ref[:, lo:lo + QK_NOPE] = dk_ref[:, lo:lo + QK_NOPE].astype(BF)
            dkvl_ref[:, lo + QK_NOPE:lo + HEAD_PAD] = dv_ref[:, h * V_HEAD:(h + 1) * V_HEAD].astype(BF)
            kr = kr + dk_ref[:, lo + QK_NOPE:lo + HEAD_PAD]
        dkr_ref[...] = _rope_bwd(kr, cos_v, sin_v).astype(BF)

    W = HEADS * HEAD_PAD
    return _pcall(
        body, name=name, grid=(T // tm,),
        ins=[(dq, (tm, W), lambda i: (i, 0)), (dk, (tm, W), lambda i: (i, 0)), (dv, (tm, MLA_OUT), lambda i: (i, 0)),
             (cos_t, (tm, LANES), lambda i: (i, 0)), (sin_t, (tm, LANES), lambda i: (i, 0)), (dproj, None, None)],
        outs=[((T, W), BF, (tm, W), lambda i: (i, 0)), ((T, W), BF, (tm, W), lambda i: (i, 0)),
              (dproj.shape, BF, (tm, LANES), lambda i: (i, kr_block))],
        aliases={5: 2}, semantics=("parallel",))


def _group_norm_stats(vg):
    mu = jnp.mean(vg, axis=-1, keepdims=True)
    d = vg - mu
    r = lax.rsqrt(jnp.mean(d * d, axis=-1, keepdims=True) + EPS)
    return d * r, r


def mix_fwd(a, proj, g_mla, g_sgu, v_gain, w_tril, b_full, *, name):
    T = a.shape[0]
    tm = _tile(T, ROW_TILE, CHUNK)
    n_chunk = tm // CHUNK

    def body(a_ref, u_ref, v_ref, gm_ref, gs_ref, vg_ref, w_ref, b_ref, o_ref, s_scr):
        av = a_ref[...]
        o_ref[:, :MLA_OUT] = (av * _rstd(av) * gm_ref[...]).astype(BF)
        for g in range(GROUPS):
            sl = slice(g * CH, (g + 1) * CH)
            vhat, _ = _group_norm_stats(_gelu(v_ref[:, sl]))
            vn = (vhat * vg_ref[:, sl]).astype(BF)
            u = _gelu(u_ref[:, sl])
            for ci in range(n_chunk):
                rs = slice(ci * CHUNK, (ci + 1) * CHUNK)
                y = jnp.dot(w_ref[g], vn[rs], preferred_element_type=F32) + b_ref[:, sl]
                s_scr[rs, sl] = u[rs] * y
        s = s_scr[...]
        o_ref[:, MLA_OUT:] = (s * _rstd(s) * gs_ref[...]).astype(BF)

    return _pcall(
        body, name=name, grid=(T // tm,),
        ins=[(a, (tm, MLA_OUT), lambda i: (i, 0)), (proj, (tm, SGU_OUT), lambda i: (i, 1)),
             (proj, (tm, SGU_OUT), lambda i: (i, 2)), (g_mla, (1, MLA_OUT), lambda i: (0, 0)),
             (g_sgu, (1, SGU_OUT), lambda i: (0, 0)), (v_gain, (1, SGU_OUT), lambda i: (0, 0)),
             (w_tril, (GROUPS, CHUNK, CHUNK), lambda i: (0, 0, 0)), (b_full, (CHUNK, SGU_OUT), lambda i: (0, 0))],
        outs=[((T, MLA_OUT + SGU_OUT), BF, (tm, MLA_OUT + SGU_OUT), lambda i: (i, 0))],
        scratch=[pltpu.VMEM((tm, SGU_OUT), F32)], semantics=("parallel",))[0]


def mix_bwd(dmixed, a, proj, g_mla, g_sgu, v_gain, w_tril, w_tril_t, b_full, *, name, deps=()):
    T = a.shape[0]
    tm = _tile(T, ROW_TILE, CHUNK)
    n_chunk = tm // CHUNK
    uv0 = Q_LORA + KV_LORA

    def body(dm_a_ref, dm_s_ref, a_ref, u_ref, v_ref, gm_ref, gs_ref, vg_ref, w_ref, wt_ref, b_ref,
             da_ref, duv_ref, dgm_ref, dgs_ref, dvg_ref, dw_ref, db_ref, s_scr, y_scr):
        first = pl.program_id(0) == 0
        duv_ref[:, :uv0] = jnp.zeros((tm, uv0), BF)
        duv_ref[:, uv0 + 2 * SGU_OUT:] = jnp.zeros((tm, duv_ref.shape[1] - uv0 - 2 * SGU_OUT), BF)
        da, dgm_rows = _rms_bwd(a_ref[...], gm_ref[...], dm_a_ref[...])
        da_ref[...] = da.astype(BF)
        _acc_rows(dgm_ref, dgm_rows, first)

        for g in range(GROUPS):
            sl = slice(g * CH, (g + 1) * CH)
            vhat, _ = _group_norm_stats(_gelu(v_ref[:, sl]))
            vn = (vhat * vg_ref[:, sl]).astype(BF)
            u = _gelu(u_ref[:, sl])
            for ci in range(n_chunk):
                rs = slice(ci * CHUNK, (ci + 1) * CHUNK)
                y = jnp.dot(w_ref[g], vn[rs], preferred_element_type=F32) + b_ref[:, sl]
                y_scr[rs, sl] = y
                s_scr[rs, sl] = u[rs] * y
        ds, dgs_rows = _rms_bwd(s_scr[...], gs_ref[...], dm_s_ref[...])
        _acc_rows(dgs_ref, dgs_rows, first)
        s_scr[...] = ds

        @pl.when(first)
        def _():
            dw_ref[...] = jnp.zeros_like(dw_ref)
            db_ref[...] = jnp.zeros_like(db_ref)

        for g in range(GROUPS):
            sl = slice(g * CH, (g + 1) * CH)
            upre = u_ref[:, sl]
            vpre = v_ref[:, sl]
            u = _gelu(upre)
            vhat, r = _group_norm_stats(_gelu(vpre))
            gain = vg_ref[:, sl]
            vn = (vhat * gain).astype(BF)
            dsg = s_scr[:, sl]
            duv_ref[:, uv0 + g * CH:uv0 + (g + 1) * CH] = (dsg * y_scr[:, sl] * _gelu_grad(upre)).astype(BF)
            dy = dsg * u
            dyb = dy.astype(BF)
            dvn_parts = []
            for ci in range(n_chunk):
                rs = slice(ci * CHUNK, (ci + 1) * CHUNK)
                dvn_parts.append(jnp.dot(wt_ref[g], dyb[rs], preferred_element_type=F32))
                dw_ref[g] += lax.dot_general(dyb[rs], vn[rs], (((1,), (1,)), ((), ())), preferred_element_type=F32)
                db_ref[:, sl] += jnp.broadcast_to(jnp.sum(dy[rs], axis=-1, keepdims=True), (CHUNK, CH))
            dvn = dvn_parts[0] if n_chunk == 1 else jnp.concatenate(dvn_parts, axis=0)
            _acc_rows(dvg_ref.at[:, sl], dvn * vhat, first)
            dvh = dvn * gain
            dvg = r * (dvh - jnp.mean(dvh, axis=-1, keepdims=True)
                       - vhat * jnp.mean(dvh * vhat, axis=-1, keepdims=True))
            duv_ref[:, uv0 + SGU_OUT + g * CH:uv0 + SGU_OUT + (g + 1) * CH] = (dvg * _gelu_grad(vpre)).astype(BF)

    return _pcall(
        body, name=name, grid=(T // tm,),
        ins=[(dmixed, (tm, MLA_OUT), lambda i: (i, 0)), (dmixed, (tm, SGU_OUT), lambda i: (i, 1)),
             (a, (tm, MLA_OUT), lambda i: (i, 0)), (proj, (tm, SGU_OUT), lambda i: (i, 1)),
             (proj, (tm, SGU_OUT), lambda i: (i, 2)), (g_mla, (1, MLA_OUT), lambda i: (0, 0)),
             (g_sgu, (1, SGU_OUT), lambda i: (0, 0)), (v_gain, (1, SGU_OUT), lambda i: (0, 0)),
             (w_tril, (GROUPS, CHUNK, CHUNK), lambda i: (0, 0, 0)), (w_tril_t, (GROUPS, CHUNK, CHUNK), lambda i: (0, 0, 0)),
             (b_full, (CHUNK, SGU_OUT), lambda i: (0, 0))],
        outs=[((T, MLA_OUT), BF, (tm, MLA_OUT), lambda i: (i, 0)),
              ((T, proj.shape[1]), BF, (tm, proj.shape[1]), lambda i: (i, 0)),
              ((1, MLA_OUT), F32, (1, MLA_OUT), lambda i: (0, 0)), ((1, SGU_OUT), F32, (1, SGU_OUT), lambda i: (0, 0)),
              ((1, SGU_OUT), F32, (1, SGU_OUT), lambda i: (0, 0)),
              ((GROUPS, CHUNK, CHUNK), F32, (GROUPS, CHUNK, CHUNK), lambda i: (0, 0, 0)),
              ((CHUNK, SGU_OUT), F32, (CHUNK, SGU_OUT), lambda i: (0, 0))],
        scratch=[pltpu.VMEM((tm, SGU_OUT), F32), pltpu.VMEM((tm, SGU_OUT), F32)], deps=deps)


def _shift_down(z, n, row):
    return jnp.where(row >= n, pltpu.roll(z, n, 0), 0.0)


def _shift_up(z, n, row, T):
    return jnp.where(row < T - n, pltpu.roll(z, T - n, 0), 0.0)


def conv_fwd(proj, conv_w, *, name):
    T, D3 = proj.shape
    D = D3 // 3
    tn = _tile(D, 256)
    nj = D // tn

    def body(b_ref, c_ref, x_ref, w_ref, o_ref):
        row = lax.broadcasted_iota(jnp.int32, (T, tn), 0)
        z = c_ref[...] * x_ref[...]
        zc = w_ref[2:3, :] * z + w_ref[1:2, :] * _shift_down(z, 1, row) + w_ref[0:1, :] * _shift_down(z, 2, row)
        o_ref[...] = (b_ref[...] * zc).astype(BF)

    return _pcall(
        body, name=name, grid=(nj,),
        ins=[(proj, (T, tn), lambda j: (0, j)), (proj, (T, tn), lambda j: (0, nj + j)),
             (proj, (T, tn), lambda j: (0, 2 * nj + j)), (conv_w, (3, tn), lambda j: (0, j))],
        outs=[((T, D), BF, (T, tn), lambda j: (0, j))], semantics=("parallel",))[0]


def conv_bwd(dg, proj, conv_w, *, name, deps=()):
    T, D3 = proj.shape
    D = D3 // 3
    tn = _tile(D, 256)
    nj = D // tn

    def body(dg_ref, b_ref, c_ref, x_ref, w_ref, dp_ref, dw_ref, dc_scr, dx_scr):
        part = pl.program_id(1)

        @pl.when(part == 0)
        def _():
            row = lax.broadcasted_iota(jnp.int32, (T, tn), 0)
            c, x = c_ref[...], x_ref[...]
            z = c * x
            z1 = _shift_down(z, 1, row)
            z2 = _shift_down(z, 2, row)
            dgv = dg_ref[...]
            zc = w_ref[2:3, :] * z + w_ref[1:2, :] * z1 + w_ref[0:1, :] * z2
            dp_ref[...] = (dgv * zc).astype(BF)
            dzc = dgv * b_ref[...]
            dw_ref[0:1, :] = jnp.sum(dzc * z2, axis=0, keepdims=True)
            dw_ref[1:2, :] = jnp.sum(dzc * z1, axis=0, keepdims=True)
            dw_ref[2:3, :] = jnp.sum(dzc * z, axis=0, keepdims=True)
            dz = (w_ref[2:3, :] * dzc + w_ref[1:2, :] * _shift_up(dzc, 1, row, T)
                  + w_ref[0:1, :] * _shift_up(dzc, 2, row, T))
            dc_scr[...] = (dz * x).astype(BF)
            dx_scr[...] = (dz * c).astype(BF)

        @pl.when(part == 1)
        def _():
            dp_ref[...] = dc_scr[...]

        @pl.when(part == 2)
        def _():
            dp_ref[...] = dx_scr[...]

    return _pcall(
        body, name=name, grid=(nj, 3),
        ins=[(dg, (T, tn), lambda j, p: (0, j)), (proj, (T, tn), lambda j, p: (0, j)),
             (proj, (T, tn), lambda j, p: (0, nj + j)), (proj, (T, tn), lambda j, p: (0, 2 * nj + j)),
             (conv_w, (3, tn), lambda j, p: (0, j))],
        outs=[((T, D3), BF, (T, tn), lambda j, p: (0, p * nj + j)), ((3, D), F32, (3, tn), lambda j, p: (0, j))],
        scratch=[pltpu.VMEM((T, tn), BF), pltpu.VMEM((T, tn), BF)], semantics=("parallel", "arbitrary"), deps=deps)


def loss_bwd(x_parts, gain, target, *, name):
    T, D = target.shape
    tm = _tile(T, ROW_TILE, 8)
    n_x = len(x_parts)

    def body(*refs):
        x_refs = refs[:n_x]
        g_ref, t_ref, dx_ref, dxb_ref, dg_ref, loss_ref = refs[n_x:]
        first = pl.program_id(0) == 0
        xv = jnp.concatenate([r[...] for r in x_refs], axis=-1) if n_x > 1 else x_refs[0][...]
        r = _rstd(xv)
        xh = xv * r
        gain_v = g_ref[...]
        err = xh * gain_v - t_ref[...]
        part = 0.5 * jnp.sum(jnp.mean(err * err, axis=-1, keepdims=True), axis=0, keepdims=True)
        _acc_rows(loss_ref, jnp.broadcast_to(part, (1, LANES)), first)
        dy = err * (1.0 / D)
        gdy = dy * gain_v
        dx = r * (gdy - xh * jnp.mean(gdy * xh, axis=-1, keepdims=True))
        dx_ref[...] = dx
        dxb_ref[...] = dx.astype(BF)
        _acc_rows(dg_ref, dy * xh, first)

    return _pcall(
        body, name=name, grid=(T // tm,),
        ins=[(p, (tm, D // n_x), lambda i: (i, 0)) for p in x_parts]
        + [(gain, (1, D), lambda i: (0, 0)), (target, (tm, D), lambda i: (i, 0))],
        outs=[((T, D), F32, (tm, D), lambda i: (i, 0)), ((T, D), BF, (tm, D), lambda i: (i, 0)),
              ((1, D), F32, (1, D), lambda i: (0, 0)), ((1, LANES), F32, (1, LANES), lambda i: (0, 0))])


def _adamw(g, w, m, v):
    m = ADAM_B1 * m + (1.0 - ADAM_B1) * g
    v = ADAM_B2 * v + (1.0 - ADAM_B2) * (g * g)
    m_hat = m / ADAM_C1
    v_hat = v / ADAM_C2
    delta = -ADAM_LR * (m_hat / (jnp.sqrt(v_hat) + ADAM_EPS) + ADAM_WD * w)
    return delta, m, v


def adam_flat(g, w, m, v, *, name):
    def body(g_ref, w_ref, m_ref, v_ref, d_ref, nm_ref, nv_ref):
        d, nm, nv = _adamw(g_ref[...], w_ref[...], m_ref[...], v_ref[...])
        d_ref[...] = d
        nm_ref[...] = nm
        nv_ref[...] = nv

    blk = g.shape
    zero = lambda: (0, 0)
    return _pcall(body, name=name, grid=(),
                  ins=[(t, blk, zero) for t in (g, w, m, v)],
                  outs=[(blk, F32, blk, zero)] * 3)


def _chip_slots():
    x, y, c = lax.axis_index("x"), lax.axis_index("y"), lax.axis_index("c")
    chips = [(1 - x, y), (x, 1 - y), (1 - x, 1 - y)]
    return x, y, c, chips


def device_index():
    x, y, c, chips = _chip_slots()
    return jnp.stack([4 * x + 2 * y + c, 2 * x + y] + [4 * cx + 2 * cy + c for cx, cy in chips]
                     + [2 * cx + cy for cx, cy in chips]).astype(jnp.int32)


def _job_rows(R, C, n_steps):
    if n_steps is None:
        n_steps = max(1, R * C // STREAM_BLOCK_ELEMS)
    n_blk = max([d for d in range(1, n_steps + 1) if R % d == 0 and (R // d) % 16 == 0] or [1])
    return R // n_blk, n_blk


def run_job(job, *, index, name, deps=()):
    jb = job(None)
    n_in = len(jb["ins"])

    def body(idx_ref, *refs):
        jb["fn"](refs[:n_in], refs[n_in:n_in + len(jb["outs"])])

    return _pcall(body, name=name, grid=(jb["n_blk"],), ins=jb["ins"], outs=jb["outs"], prefetch=index,
                  aliases={1 + a: o for a, o in jb["aliases"].items()}, semantics=("parallel",), deps=deps)


def adam_job(gs, a_buf, b_buf, w, m, v, layer, prev):
    L, R, C = w.shape

    def build(n_steps):
        tr, n_blk = _job_rows(R, C, n_steps)
        blk = (None, tr, C)
        row = lambda t: jnp.minimum(t, n_blk - 1)
        ins = [(gs, blk, lambda t, s: (s[0], row(t), 0)), (a_buf, blk, lambda t, s: (s[1], row(t), 0))]
        ins += [(b_buf, blk, lambda t, s, j=j: (j, row(t), 0)) for j in range(3)]
        ins += [(p, blk, lambda t, s: (layer, row(t), 0)) for p in (w, m, v)]
        ins += [(p, None, None) for p in (prev or [])]

        def fn(i, o):
            g = ((((i[0][...].astype(F32) + i[1][...].astype(F32)) + i[2][...].astype(F32))
                  + i[3][...].astype(F32)) + i[4][...].astype(F32))
            d, nm, nv = _adamw(g, i[5][...], i[6][...], i[7][...])
            o[0][...] = g
            o[1][...] = d
            o[2][...] = nm
            o[3][...] = nv

        return dict(ins=ins, outs=[((L, R, C), F32, blk, lambda t, s: (layer, row(t), 0))] * 4, fn=fn,
                    aliases={8 + o: o for o in range(4)} if prev else {}, n_blk=n_blk)

    return build


def pair_job(gs, a_buf):
    _, R, C = gs.shape

    def build(n_steps):
        tr, n_blk = _job_rows(R, C, n_steps)
        blk = (None, tr, C)
        row = lambda t: jnp.minimum(t, n_blk - 1)
        ins = [(gs, blk, lambda t, s, j=j: (s[2 + j], row(t), 0)) for j in range(3)]
        ins += [(a_buf, blk, lambda t, s, j=j: (s[5 + j], row(t), 0)) for j in range(3)]

        def fn(i, o):
            for j in range(3):
                o[0][j] = (i[j][...].astype(F32) + i[3 + j][...].astype(F32)).astype(BF)

        return dict(ins=ins, outs=[((3, R, C), BF, (3, tr, C), lambda t, s: (0, row(t), 0))], fn=fn, aliases={},
                    n_blk=n_blk)

    return build


def reduce_sum(gs, a_buf, b_buf, *, name):
    _, R, C = gs.shape
    tr = _tile(R, 256, 16)
    x, y, c, _ = _chip_slots()
    idx = jnp.stack([4 * x + 2 * y + c, 2 * x + y]).astype(jnp.int32)

    def body(idx_ref, g_ref, a_ref, b0_ref, b1_ref, b2_ref, o_ref):
        o_ref[...] = ((((g_ref[...].astype(F32) + a_ref[...].astype(F32)) + b0_ref[...].astype(F32))
                       + b1_ref[...].astype(F32)) + b2_ref[...].astype(F32))

    blk3 = (None, tr, C)
    return _pcall(body, name=name, grid=(R // tr,),
                  ins=[(gs, blk3, lambda i, s: (s[0], i, 0)), (a_buf, blk3, lambda i, s: (s[1], i, 0)),
                       (b_buf, blk3, lambda i, s: (0, i, 0)), (b_buf, blk3, lambda i, s: (1, i, 0)),
                       (b_buf, blk3, lambda i, s: (2, i, 0))],
                  outs=[((R, C), F32, (tr, C), lambda i, s: (i, 0))], prefetch=idx, semantics=("parallel",))[0]


def adam_rows(g, w, m, v, *, name):
    R, C = g.shape
    tr = _tile(R, 256, 8)

    def body(g_ref, w_ref, m_ref, v_ref, d_ref, nm_ref, nv_ref):
        d, nm, nv = _adamw(g_ref[...], w_ref[...], m_ref[...], v_ref[...])
        d_ref[...] = d
        nm_ref[...] = nm
        nv_ref[...] = nv

    spec = ((tr, C), lambda i: (i, 0))
    return _pcall(body, name=name, grid=(R // tr,), ins=[(t, *spec) for t in (g, w, m, v)],
                  outs=[((R, C), F32, *spec)] * 3, semantics=("parallel",))


def sum_rows8(gathered, rows, *, name):
    W = gathered.shape[1]

    def body(g_ref, o_ref):
        acc = g_ref[0:rows, :]
        for d in range(1, N_DEV):
            acc = acc + g_ref[d * rows:(d + 1) * rows, :]
        o_ref[...] = acc

    return _pcall(body, name=name, grid=(), ins=[(gathered, gathered.shape, lambda: (0, 0))],
                  outs=[((rows, W), F32, (rows, W), lambda: (0, 0))])[0]


HBM_SPEC = pl.BlockSpec(memory_space=pltpu.HBM)
SEM_SPEC = pl.BlockSpec(memory_space=pltpu.SEMAPHORE)
ANY_SPEC = pl.BlockSpec(memory_space=pl.ANY)
DATAFLOW = pltpu.SideEffectType.DATAFLOW_SIDE_EFFECTING


def _in_hbm(v):
    return pltpu.with_memory_space_constraint(v, pltpu.HBM)


def _slot(p):
    return 4 * p[0] + 2 * p[1] + p[2]


def _gather_peers():
    x, y, c, chips = _chip_slots()
    return (x, y, c), [(x, y, 1 - c)] + [(*chip, c) for chip in chips]


def gather_start(groups, after, *, name):
    flat = [s for g in groups for s in g]
    n, n_g = len(flat), len(groups)
    where = [(gi, ti) for gi, g in enumerate(groups) for ti in range(len(g))]

    def body(*refs):
        src, land = refs[:n], refs[n:2 * n]
        sems = refs[2 * n + 1:2 * n + 1 + 2 * n_g]
        me, peers = _gather_peers()
        for t in range(n):
            gi, ti = where[t]
            for k, to in enumerate(peers):
                pltpu.make_async_remote_copy(
                    src_ref=src[t], dst_ref=land[t].at[_slot(me)], send_sem=sems[2 * gi].at[4 * ti + k],
                    recv_sem=sems[2 * gi + 1].at[4 * ti + k], device_id=to, device_id_type=MESH).start()
        refs[-1][...] = jnp.zeros_like(refs[-1])

    out_shape = []
    for g in groups:
        out_shape += [pltpu.SemaphoreType.DMA((4 * len(g),)), pltpu.SemaphoreType.DMA((4 * len(g),))]
    out_shape += [pltpu.HBM(s.shape, s.dtype) for s in flat]
    out_shape += [pltpu.HBM((N_DEV,) + s.shape, s.dtype) for s in flat]
    out_shape += [jax.ShapeDtypeStruct((8, LANES), F32)]
    aliases = {t: 2 * n_g + t for t in range(n)}
    aliases.update({n + t: 2 * n_g + n + t for t in range(n)})
    res = pl.pallas_call(
        body, name=name, out_shape=out_shape, in_specs=[HBM_SPEC] * (2 * n) + [ANY_SPEC],
        out_specs=[SEM_SPEC] * (2 * n_g) + [HBM_SPEC] * (2 * n) + [pl.BlockSpec(memory_space=pltpu.VMEM)],
        input_output_aliases=aliases, compiler_params=pltpu.CompilerParams(has_side_effects=DATAFLOW),
    )(*[_in_hbm(s) for s in flat], *[_in_hbm(lax.empty((N_DEV,) + s.shape, s.dtype)) for s in flat], after)
    out, off = [], 0
    for gi, g in enumerate(groups):
        k = len(g)
        out.append((res[2 * gi], res[2 * gi + 1], res[2 * n_g + off:2 * n_g + off + k],
                    res[2 * n_g + n + off:2 * n_g + n + off + k]))
        off += k
    return out, res[-1]


def gather_wait(started, after, *, name):
    send_sems, recv_sems, srcs, lands = started
    n = len(srcs)
    after = list(after)

    def body(*refs):
        src, land = refs[:n], refs[n:2 * n]
        send, recv = refs[2 * n], refs[2 * n + 1]
        _, peers = _gather_peers()
        for t in range(n):
            for k, frm in enumerate(peers):
                cp = pltpu.make_async_remote_copy(
                    src_ref=src[t], dst_ref=land[t].at[_slot(frm)], send_sem=send.at[4 * t + k],
                    recv_sem=recv.at[4 * t + k],
                    device_id=frm, device_id_type=MESH)
                cp.wait_send()
                cp.wait_recv()

    res = pl.pallas_call(
        body, name=name,
        out_shape=[pltpu.HBM(s.shape, s.dtype) for s in srcs] + [pltpu.HBM(l.shape, l.dtype) for l in lands],
        in_specs=[HBM_SPEC] * (2 * n) + [SEM_SPEC, SEM_SPEC] + [ANY_SPEC] * len(after),
        out_specs=[HBM_SPEC] * (2 * n), input_output_aliases={t: t for t in range(2 * n)},
        compiler_params=pltpu.CompilerParams(has_side_effects=DATAFLOW),
    )(*srcs, *lands, send_sems, recv_sems, *after)
    return res[:n], res[n:]


def place_own(src, land, *, name):
    R, C = src.shape
    tr = _tile(R, 512, 16)
    x, y, c, _ = _chip_slots()
    idx = jnp.stack([4 * x + 2 * y + c]).astype(jnp.int32)

    def body(idx_ref, s_ref, land_ref, o_ref):
        o_ref[...] = s_ref[...]

    return _pcall(body, name=name, grid=(R // tr,),
                  ins=[(src, (tr, C), lambda i, s: (i, 0)), (land, None, None)],
                  outs=[(land.shape, land.dtype, (None, tr, C), lambda i, s: (s[0], i, 0))],
                  prefetch=idx, aliases={2: 0}, semantics=("parallel",))[0]


def gather_finish(srcs, lands, *, name):
    n = len(srcs)

    def body(*refs):
        land = refs[n:2 * n]
        send_sems, recv_sems = refs[2 * n:]
        x, y, c, chips = _chip_slots()
        me, sibling = (x, y, c), (x, y, 1 - c)

        def copy(t, j, block, to):
            return pltpu.make_async_remote_copy(
                src_ref=land[t].at[_slot(block)], dst_ref=land[t].at[_slot(block)], send_sem=send_sems.at[t, j],
                recv_sem=recv_sems.at[t, j], device_id=to, device_id_type=MESH)

        sends = [copy(t, j, (*chip, c), sibling) for t in range(n) for j, chip in enumerate(chips)]
        for cp in sends:
            cp.start()
        for t in range(n):
            for j, chip in enumerate(chips):
                copy(t, j, (*chip, 1 - c), me).wait_recv()
        for cp in sends:
            cp.wait_send()

    passed = pl.pallas_call(
        body, name=name, out_shape=[jax.ShapeDtypeStruct(l.shape, l.dtype) for l in lands],
        in_specs=[ANY_SPEC] * n, out_specs=[ANY_SPEC] * n,
        input_output_aliases={t: t for t in range(n)},
        scratch_shapes=[pltpu.SemaphoreType.DMA((n, 3)), pltpu.SemaphoreType.DMA((n, 3))],
    )(*lands)
    return [place_own(s, l, name=f"{name}_own{t}") for t, (s, l) in enumerate(zip(srcs, passed))]


def chips_start(pairs, *, name):
    n = len(pairs)

    def body(*refs):
        src, land = refs[:n], refs[n:2 * n]
        send, recv = refs[2 * n], refs[2 * n + 1]
        token = refs[-1]
        x, y, c, chips = _chip_slots()
        for t in range(n):
            for j, chip in enumerate(chips):
                pltpu.make_async_remote_copy(
                    src_ref=src[t].at[j], dst_ref=land[t].at[j], send_sem=send.at[3 * t + j],
                    recv_sem=recv.at[3 * t + j], device_id=(*chip, c), device_id_type=MESH).start()
        token[...] = jnp.zeros_like(token)

    res = pl.pallas_call(
        body, name=name,
        out_shape=[pltpu.SemaphoreType.DMA((3 * n,)), pltpu.SemaphoreType.DMA((3 * n,))]
        + [pltpu.HBM(p.shape, p.dtype) for p in pairs] * 2 + [jax.ShapeDtypeStruct((8, LANES), F32)],
        in_specs=[HBM_SPEC] * (2 * n),
        out_specs=[SEM_SPEC, SEM_SPEC] + [HBM_SPEC] * (2 * n) + [pl.BlockSpec(memory_space=pltpu.VMEM)],
        input_output_aliases={t: 2 + t for t in range(2 * n)},
        compiler_params=pltpu.CompilerParams(has_side_effects=DATAFLOW),
    )(*[_in_hbm(p) for p in pairs], *[_in_hbm(lax.empty(p.shape, p.dtype)) for p in pairs])
    return res[0], res[1], res[2:2 + n], res[2 + n:2 + 2 * n], res[-1]


def chips_wait(started, after, *, name):
    send_sems, recv_sems, srcs, lands, _ = started
    n = len(srcs)

    def body(*refs):
        src, land = refs[:n], refs[n:2 * n]
        send, recv = refs[2 * n], refs[2 * n + 1]
        x, y, c, chips = _chip_slots()
        for t in range(n):
            for j, chip in enumerate(chips):
                cp = pltpu.make_async_remote_copy(
                    src_ref=src[t].at[j], dst_ref=land[t].at[j], send_sem=send.at[3 * t + j],
                    recv_sem=recv.at[3 * t + j], device_id=(*chip, c), device_id_type=MESH)
                cp.wait_send()
                cp.wait_recv()

    res = pl.pallas_call(
        body, name=name, out_shape=[pltpu.HBM(s.shape, s.dtype) for s in srcs] * 2,
        in_specs=[HBM_SPEC] * (2 * n) + [SEM_SPEC, SEM_SPEC, ANY_SPEC], out_specs=[HBM_SPEC] * (2 * n),
        input_output_aliases={t: t for t in range(2 * n)},
        compiler_params=pltpu.CompilerParams(has_side_effects=DATAFLOW),
    )(*srcs, *lands, send_sems, recv_sems, after)
    return res[n:]


def _sibling_copies(src, land, send, recv, n):
    x, y, c, _ = _chip_slots()
    return [pltpu.make_async_remote_copy(
        src_ref=src[t].at[4 * (q // 2) + 2 * (q % 2) + (1 - c)], dst_ref=land[t].at[q], send_sem=send.at[4 * t + q],
        recv_sem=recv.at[4 * t + q], device_id=(x, y, 1 - c), device_id_type=MESH)
        for t in range(n) for q in range(4)]


def sibling_start(gs, *, name):
    n = len(gs)

    def body(*refs):
        for cp in _sibling_copies(refs[:n], refs[n:2 * n], refs[2 * n], refs[2 * n + 1], n):
            cp.start()
        refs[-1][...] = jnp.zeros_like(refs[-1])

    lands = [lax.empty((4,) + g.shape[1:], g.dtype) for g in gs]
    res = pl.pallas_call(
        body, name=name,
        out_shape=[pltpu.SemaphoreType.DMA((4 * n,)), pltpu.SemaphoreType.DMA((4 * n,))]
        + [pltpu.HBM(g.shape, g.dtype) for g in gs] + [pltpu.HBM(l.shape, l.dtype) for l in lands]
        + [jax.ShapeDtypeStruct((8, LANES), F32)],
        in_specs=[HBM_SPEC] * (2 * n),
        out_specs=[SEM_SPEC, SEM_SPEC] + [HBM_SPEC] * (2 * n) + [pl.BlockSpec(memory_space=pltpu.VMEM)],
        input_output_aliases={t: 2 + t for t in range(2 * n)},
        compiler_params=pltpu.CompilerParams(has_side_effects=DATAFLOW),
    )(*[_in_hbm(g) for g in gs], *[_in_hbm(l) for l in lands])
    return res[0], res[1], res[2:2 + n], res[2 + n:2 + 2 * n], res[-1]


def sibling_wait(started, after, *, name):
    send_sems, recv_sems, srcs, lands, _ = started
    n = len(srcs)

    def body(*refs):
        for cp in _sibling_copies(refs[:n], refs[n:2 * n], refs[2 * n], refs[2 * n + 1], n):
            cp.wait_send()
            cp.wait_recv()

    res = pl.pallas_call(
        body, name=name,
        out_shape=[pltpu.HBM(s.shape, s.dtype) for s in srcs] + [pltpu.HBM(l.shape, l.dtype) for l in lands],
        in_specs=[HBM_SPEC] * (2 * n) + [SEM_SPEC, SEM_SPEC, ANY_SPEC], out_specs=[HBM_SPEC] * (2 * n),
        input_output_aliases={t: t for t in range(2 * n)},
        compiler_params=pltpu.CompilerParams(has_side_effects=DATAFLOW),
    )(*srcs, *lands, send_sems, recv_sems, after)
    return res[:n], res[n:]


def _rope_slab(cols):
    z = jnp.zeros(cols.shape[:-1] + (HALF_ROPE,), cols.dtype)
    return jnp.concatenate([cols[..., :HALF_ROPE], z, cols[..., HALF_ROPE:], z], axis=-1)


def _rope_unslab(slab):
    return jnp.concatenate([slab[..., :HALF_ROPE], slab[..., 2 * HALF_ROPE:3 * HALF_ROPE]], axis=-1)


def _pack_w_in_t(wt_g):
    s, c, d = wt_g.shape
    w = wt_g.reshape(s * c, d)
    c2, c3 = Q_LORA + KV_LORA, Q_LORA + KV_LORA + QK_ROPE
    r = w[c2:c3]
    z = jnp.zeros((HALF_ROPE, d), w.dtype)
    return jnp.concatenate([w[:c2], w[c3:], r[:HALF_ROPE], z, r[HALF_ROPE:], z], axis=0)


def unpack_w_in_t_grad(dwt, *, name):
    n_rows, d = dwt.shape
    kr = QK_ROPE
    n_out_rows = n_rows - kr
    blk = n_out_rows // 7
    assert blk * 7 == n_out_rows and blk % kr == 0 and n_rows % (2 * kr) == 0
    kr_row = Q_LORA + KV_LORA
    k_mix = kr_row // blk
    off = kr_row - k_mix * blk
    slab_block = (n_rows - 2 * kr) // (2 * kr)

    def body(prev_ref, in_ref, slab_ref, o_ref):
        k = pl.program_id(0)

        @pl.when(k < k_mix)
        def _():
            o_ref[...] = in_ref[...]

        @pl.when(k == k_mix)
        def _():
            o_ref[:off, :] = in_ref[:off, :]
            o_ref[off:off + HALF_ROPE, :] = slab_ref[:HALF_ROPE, :]
            o_ref[off + HALF_ROPE:off + kr, :] = slab_ref[2 * HALF_ROPE:3 * HALF_ROPE, :]
            o_ref[off + kr:, :] = in_ref[off:blk - kr, :]

        @pl.when(k > k_mix)
        def _():
            o_ref[:kr, :] = prev_ref[blk - kr:, :]
            o_ref[kr:, :] = in_ref[:blk - kr, :]

    out = _pcall(body, name=name, grid=(7,),
                 ins=[(dwt, (blk, d), lambda k: (jnp.maximum(k - 1, 0), 0)), (dwt, (blk, d), lambda k: (k, 0)),
                      (dwt, (2 * kr, d), lambda k: (slab_block, 0))],
                 outs=[((n_out_rows, d), dwt.dtype, (blk, d), lambda k: (k, 0))], semantics=("parallel",))[0]
    return out.reshape(N_DEV, n_out_rows // N_DEV, d)


def _rope_tables(positions):
    inv_freq = ROPE_BASE ** (-jnp.arange(0, QK_ROPE, 2, dtype=F32) / QK_ROPE)
    ang = positions.astype(F32)[:, None] * inv_freq
    cos, sin = jnp.cos(ang), jnp.sin(ang)
    z = jnp.zeros_like(cos)
    return jnp.concatenate([cos, z, cos, z], axis=-1), jnp.concatenate([-sin, z, sin, z], axis=-1)


def _mlp_up(x, gain, w1, tag):
    hn = rms_fwd(x, gain, name=f"mlp{tag}_norm")

    def act_epi(acc):
        a = jnp.maximum(acc, 0.0)
        return a, a * a

    T = x.shape[0]
    F = w1.shape[0] * w1.shape[2]
    a, act = mm(hn, w1, name=f"mlp{tag}_up", outs=[((T, F), BF, None), ((T, F), BF, None)], epi=act_epi)
    return hn, a, act


def _mlp_down(x, act, w2, tag, part=0):
    n = w2.shape[1]
    bm = _tile(x.shape[0], MM_TILE)
    bn = _tile(n, MM_TILE)
    per = n // bn
    return mm(act, w2, name=f"mlp{tag}_down{part}", out=((x.shape[0], n), F32), bm=bm, bn=bn,
              epi=lambda acc, r: (acc + r[...],), epi_ins=[(x, (bm, bn), lambda i, j, k: (i, part * per + j))])


def _mlp_bwd_weights(w1, w2, saved, dxb, tag):
    hn, a, act = saved
    T, D = dxb.shape
    F = a.shape[1]
    bm = _tile(T, MM_TILE)
    bn = _tile(F, min(MM_TILE, w1.shape[2]))
    dhid = mm(dxb, w2, tb=True, name=f"mlp{tag}_dhid", out=((T, F), BF), bm=bm, bn=bn,
              epi=lambda acc, a_ref: (2.0 * a_ref[...].astype(F32) * acc,),
              epi_ins=[(a, (bm, bn), lambda i, j, k: (i, j))])
    dw2 = mm(act, dxb, ta=True, name=f"mlp{tag}_dw2", out=((F, D), BF))
    dw1 = mm(hn, dhid, ta=True, name=f"mlp{tag}_dw1", out=(w1.shape, BF))
    return dhid, dw1, dw2.reshape(N_DEV, F // N_DEV, D)


def _reduce_begin(grads, tag):
    return sibling_start(grads, name=f"reduce_sibling_start_{tag}")


def _reduce_continue(sib, after, tag, index):
    grads, a_bufs = sibling_wait(sib, after, name=f"reduce_sibling_wait_{tag}")
    pairs = [run_job(pair_job(g, a), index=index, name=f"pair_sum_{tag}{t}")[0]
             for t, (g, a) in enumerate(zip(grads, a_bufs))]
    return grads, a_bufs, chips_start(pairs, name=f"reduce_chips_start_{tag}")


def kernel(x, positions, e_norm_mix, e_w_in, e_q_norm, e_w_uq, e_kv_norm, e_w_ukv, e_v_norm, e_sgu_w, e_sgu_b, e_mla_out_norm, e_sgu_out_norm, e_w_out, o_norm_mix, o_w_in, o_conv_w, o_w_out, mlp_norm, mlp_w1, mlp_w2, final_norm, loss_target, m_e_norm_mix, m_e_w_in, m_e_q_norm, m_e_w_uq, m_e_kv_norm, m_e_w_ukv, m_e_v_norm, m_e_sgu_w, m_e_sgu_b, m_e_mla_out_norm, m_e_sgu_out_norm, m_e_w_out, m_o_norm_mix, m_o_w_in, m_o_conv_w, m_o_w_out, m_mlp_norm, m_mlp_w1, m_mlp_w2, m_final_norm, v_e_norm_mix, v_e_w_in, v_e_q_norm, v_e_w_uq, v_e_kv_norm, v_e_w_ukv, v_e_v_norm, v_e_sgu_w, v_e_sgu_b, v_e_mla_out_norm, v_e_sgu_out_norm, v_e_w_out, v_o_norm_mix, v_o_w_in, v_o_conv_w, v_o_w_out, v_mlp_norm, v_mlp_w1, v_mlp_w2, v_final_norm):
    T, D = x.shape[1], x.shape[2]
    d_shard = o_norm_mix.shape[1]
    x0 = x[0]
    target = loss_target[0]
    me = 4 * lax.axis_index("x") + 2 * lax.axis_index("y") + lax.axis_index("c")

    bf = lambda s: s.astype(BF)
    gather_groups = [[bf(jnp.transpose(e_w_in[0])), bf(e_w_uq[0]), bf(e_w_ukv[0])], [bf(e_w_out[0]), bf(mlp_w1[0])],
                     [bf(mlp_w2[0]), bf(o_w_in[0])], [bf(o_w_out[0]), bf(mlp_w1[1])], [bf(mlp_w2[1])]]
    small_rows = jnp.concatenate([o_norm_mix, o_conv_w[0], jnp.zeros((4, d_shard), F32)], axis=0)
    gather_groups[0].insert(0, small_rows)
    started, start_token = gather_start(gather_groups[:1], x0, name="gather_start0")
    started_rest, rest_token = gather_start(gather_groups[1:], start_token, name="gather_start1")
    started += started_rest

    def gathered(gi, after):
        srcs, lands = gather_wait(started[gi], after, name=f"gather_wait{gi}")
        return gather_finish(srcs, lands, name=f"gather_finish{gi}")

    w_tril = jnp.tril(e_sgu_w[0])
    w_tril_b = w_tril.astype(BF)
    w_tril_tb = jnp.swapaxes(w_tril, 1, 2).astype(BF)
    b_full = jnp.repeat(e_sgu_b[0].T, CH, axis=1)
    v_gain = e_v_norm[0].reshape(1, SGU_OUT)
    cos_t, sin_t = _rope_tables(positions[0])
    mlp_gain = [mlp_norm[0:1], mlp_norm[1:2]]
    final_gain = final_norm.reshape(1, D)

    h0 = rms_fwd(x0, e_norm_mix, name="e_norm", deps=[rest_token])
    small_g, g_w_in_t, g_w_uq, w_ukv = gathered(0, [h0, cos_t, sin_t, w_tril_b, w_tril_tb, b_full])
    o_norm_full = small_g[:, 0, :].reshape(1, D)
    conv_w_full = jnp.transpose(small_g[:, 1:4, :], (1, 0, 2)).reshape(3, D)
    w_in_t = _pack_w_in_t(g_w_in_t)
    w_uq = jnp.concatenate([g_w_uq[..., :QK_NOPE], _rope_slab(g_w_uq[..., QK_NOPE:])], axis=-1)
    proj = mm(h0, w_in_t, tb=True, name="e_in", out=((T, w_in_t.shape[0]), F32), bn=_tile(w_in_t.shape[0], 640))
    qn, kvn, krope = mla_prep(proj, e_q_norm, e_kv_norm, cos_t, sin_t, name="mla_prep")
    bm = _tile(T, MM_TILE)

    def q_epi(acc, cos_ref, sin_ref):
        return (jnp.concatenate([acc[:, :QK_NOPE], _rope_fwd(acc[:, QK_NOPE:], cos_ref[...], sin_ref[...])], axis=-1),)

    q = mm(qn, w_uq, name="mla_q", out=((T, HEADS * HEAD_PAD), BF), bm=bm, bn=HEAD_PAD, epi=q_epi,
           epi_ins=[(cos_t, (bm, LANES), lambda i, j, k: (i, 0)), (sin_t, (bm, LANES), lambda i, j, k: (i, 0))])

    def kv_epi(acc, kr_ref):
        return jnp.concatenate([acc[:, :QK_NOPE].astype(BF), kr_ref[...]], axis=-1), acc[:, QK_NOPE:]

    k, v = mm(kvn, w_ukv, name="mla_kv", bm=bm, bn=HEAD_PAD, epi=kv_epi,
              outs=[((T, HEADS * HEAD_PAD), BF, HEAD_PAD), ((T, MLA_OUT), BF, V_HEAD)],
              epi_ins=[(krope, (bm, LANES), lambda i, j, k: (i, 0))])
    attn, attn_lse = attn_fwd(q, k, v, name="attn_fwd")
    mixed = mix_fwd(attn, proj, e_mla_out_norm, e_sgu_out_norm, v_gain, w_tril_b, b_full, name="mix_fwd")
    bn = _tile(D, MM_TILE)
    g_w_out_e, w1_0 = gathered(1, [mixed])
    w_out_e = g_w_out_e.reshape(-1, D)
    x1 = mm(mixed, w_out_e, name="e_out", out=((T, D), F32), bm=bm, bn=bn,
            epi=lambda acc, r: (acc + r[...],), epi_ins=[(x0, (bm, bn), lambda i, j, k: (i, j))])
    hn0, a0, act0 = _mlp_up(x1, mlp_gain[0], w1_0, 0)
    g_w2_0, g_w_in_o = gathered(2, [act0])
    w2_0 = g_w2_0.reshape(-1, D)
    x2 = _mlp_down(x1, act0, w2_0, 0)
    ho = rms_fwd(x2, o_norm_full, name="o_norm")
    proj_o = mm(ho, g_w_in_o, name="o_in", out=((T, 3 * D), F32))
    gated = conv_fwd(proj_o, conv_w_full, name="conv_fwd")
    g_w_out_o, w1_1 = gathered(3, [gated])
    w_out_o = g_w_out_o.reshape(-1, D)
    x3 = mm(gated, w_out_o, name="o_out", out=((T, D), F32), bm=bm, bn=bn,
            epi=lambda acc, r: (acc + r[...],), epi_ins=[(x2, (bm, bn), lambda i, j, k: (i, j))])
    hn1, a1, act1 = _mlp_up(x3, mlp_gain[1], w1_1, 1)
    (g_w2_1,) = gathered(4, [act1])
    w2_1 = g_w2_1.reshape(-1, D)
    x4 = _mlp_down(x3, act1, w2_1, 1)
    w1, w2 = [w1_0, w1_1], [w2_0, w2_1]

    dx4, dx4b, d_final, loss_part = loss_bwd([x4], final_gain, target, name="loss_bwd")

    hosted = dict(job_index=device_index())
    dhid1, dw1_1, dw2_1 = _mlp_bwd_weights(w1[1], w2[1], (hn1, a1, act1), dx4b, 1)
    sib_r0 = _reduce_begin([dw1_1, dw2_1], "r0")
    dhn1 = mm(dhid1, w1[1], tb=True, name="mlp1_dhn", out=((T, D), F32), deps=[sib_r0[-1]])
    grads_r0, a_r0 = sibling_wait(sib_r0, dhn1, name="reduce_sibling_wait_r0")
    dx3, dx3b, d_mlp1 = rms_bwd(x3, mlp_gain[1], dhn1, dres=dx4, name="mlp1_norm_bwd")

    dgated, ((pair_r0a,),) = mm(dx3b, w_out_o, tb=True, name="o_out_dx", out=((T, D), F32),
                                jobs=[pair_job(grads_r0[0], a_r0[0])], **hosted)
    dw_out_o, ((pair_r0b,),) = mm(gated, dx3b, ta=True, name="o_out_dw", out=((D, D), BF),
                                  jobs=[pair_job(grads_r0[1], a_r0[1])], **hosted)
    st_r0 = chips_start([pair_r0a, pair_r0b], name="reduce_chips_start_r0")
    dproj_o, dconv_full = conv_bwd(dgated, proj_o, conv_w_full, name="conv_bwd", deps=[st_r0[-1]])
    dw_in_o = mm(ho, dproj_o, ta=True, name="o_in_dw", out=(g_w_in_o.shape, BF))
    sib_r1 = _reduce_begin([dw_out_o.reshape(g_w_out_o.shape), dw_in_o], "r1")
    dho = mm(dproj_o, g_w_in_o, tb=True, name="o_in_dx", out=((T, D), F32), deps=[sib_r1[-1]])
    grads_r1, a_r1 = sibling_wait(sib_r1, dho, name="reduce_sibling_wait_r1")
    dx2, dx2b, d_onorm_full = rms_bwd(x2, o_norm_full, dho, dres=dx3, name="o_norm_bwd")

    d_ff = a0.shape[1]
    bm_h, bn_h = _tile(T, MM_TILE), _tile(d_ff, min(MM_TILE, w1[0].shape[2]))
    dhid0, ((pair_r1a,), (pair_r1b,)) = mm(
        dx2b, w2[0], tb=True, name="mlp0_dhid", out=((T, d_ff), BF), bm=bm_h, bn=bn_h,
        epi=lambda acc, a_ref: (2.0 * a_ref[...].astype(F32) * acc,),
        epi_ins=[(a0, (bm_h, bn_h), lambda i, j, k: (i, j))],
        jobs=[pair_job(grads_r1[0], a_r1[0]), pair_job(grads_r1[1], a_r1[1])], **hosted)
    st_r1 = chips_start([pair_r1a, pair_r1b], name="reduce_chips_start_r1")
    dw2_0 = mm(act0, dx2b, ta=True, name="mlp0_dw2", out=((d_ff, D), BF), deps=[st_r1[-1]])
    b_r0 = chips_wait(st_r0, dw2_0, name="reduce_chips_wait_r0")
    dw1_0, (r_w1, r_w2) = mm(
        hn0, dhid0, ta=True, name="mlp0_dw1", out=(w1[0].shape, BF),
        jobs=[adam_job(grads_r0[0], a_r0[0], b_r0[0], mlp_w1, m_mlp_w1, v_mlp_w1, 1, None),
              adam_job(grads_r0[1], a_r0[1], b_r0[1], mlp_w2, m_mlp_w2, v_mlp_w2, 1, None)], **hosted)
    sib_r2 = _reduce_begin([dw1_0, dw2_0.reshape(N_DEV, d_ff // N_DEV, D)], "r2")
    dhn0 = mm(dhid0, w1[0], tb=True, name="mlp0_dhn", out=((T, D), F32), deps=[sib_r2[-1]])
    grads_r2, a_r2 = sibling_wait(sib_r2, dhn0, name="reduce_sibling_wait_r2")
    dx1, dx1b, d_mlp0 = rms_bwd(x1, mlp_gain[0], dhn0, dres=dx2, name="mlp0_norm_bwd")

    dmixed, ((pair_r2a,),) = mm(dx1b, w_out_e, tb=True, name="e_out_dx", out=((T, MLA_OUT + SGU_OUT), F32),
                                jobs=[pair_job(grads_r2[0], a_r2[0])], **hosted)
    dw_out_e, ((pair_r2b,),) = mm(mixed, dx1b, ta=True, name="e_out_dw", out=(w_out_e.shape, BF),
                                  jobs=[pair_job(grads_r2[1], a_r2[1])], **hosted)
    st_r2 = chips_start([pair_r2a, pair_r2b], name="reduce_chips_start_r2")
    (dattn, dproj, d_mla_out, d_sgu_out, d_vgain, d_sgu_w, d_b_full) = mix_bwd(
        dmixed, attn, proj, e_mla_out_norm, e_sgu_out_norm, v_gain, w_tril_b, w_tril_tb, b_full, name="mix_bwd",
        deps=[st_r2[-1]])
    b_r1 = chips_wait(st_r1, dattn, name="reduce_chips_wait_r1")
    dq, dk, dv = attn_bwd(q, k, v, attn, attn_lse, dattn, name="attn_bwd")
    dq_lin, dkv_lin, dproj = mla_bwd_prep(dq, dk, dv, cos_t, sin_t, dproj, name="mla_bwd_prep")
    dw_uq_pad = mm(qn, dq_lin, ta=True, name="mla_q_dw", out=(w_uq.shape, BF))
    dw_ukv = mm(kvn, dkv_lin, ta=True, name="mla_kv_dw", out=(w_ukv.shape, BF))
    dw_uq = jnp.concatenate([dw_uq_pad[..., :QK_NOPE], _rope_unslab(dw_uq_pad[..., QK_NOPE:])], axis=-1)
    sib_r2b = _reduce_begin([dw_out_e.reshape(g_w_out_e.shape), dw_uq, dw_ukv], "r2b")
    dqn = mm(dq_lin, w_uq, tb=True, name="mla_q_dx", out=((T, Q_LORA), F32), deps=[sib_r2b[-1]])
    dkvn = mm(dkv_lin, w_ukv, tb=True, name="mla_kv_dx", out=((T, KV_LORA), F32), deps=[sib_r2b[-1]])
    grads_r2b, a_r2b, st_r2b = _reduce_continue(sib_r2b, dkvn, "r2b", hosted["job_index"])
    dproj, d_qnorm = rms_bwd(proj, e_q_norm, dqn, col_block=0, want_f32=False, into=dproj, name="q_norm_bwd",
                             deps=[st_r2b[-1]])
    dproj, d_kvnorm = rms_bwd(proj, e_kv_norm, dkvn, col_block=1, want_f32=False, into=dproj, name="kv_norm_bwd")
    dw_in_t_pad, (r_w_out_o, r_w_in_o) = mm(
        dproj, h0, ta=True, name="e_in_dw", out=(w_in_t.shape, BF), bm=_tile(w_in_t.shape[0], 640),
        jobs=[adam_job(grads_r1[0], a_r1[0], b_r1[0], o_w_out, m_o_w_out, v_o_w_out, 0, None),
              adam_job(grads_r1[1], a_r1[1], b_r1[1], o_w_in, m_o_w_in, v_o_w_in, 0, None)], **hosted)
    dw_in_t = unpack_w_in_t_grad(dw_in_t_pad, name="e_in_dw_unpack")
    sib_r3 = _reduce_begin([dw_in_t], "r3")
    dh0 = mm(dproj, w_in_t, name="e_in_dx", out=((T, D), F32), deps=[sib_r3[-1]])
    grads_r3, a_r3, st_r3 = _reduce_continue(sib_r3, dh0, "r3", hosted["job_index"])
    tok_r3 = st_r3[-1]
    grad_x, d_enorm = rms_bwd(x0, e_norm_mix, dh0, dres=dx1, want_bf=False, name="e_norm_bwd", deps=[tok_r3])
    b_r2 = chips_wait(st_r2, grad_x, name="reduce_chips_wait_r2")

    d_sgu_b = jnp.transpose(d_b_full[:, ::CH])
    d_sgu_w_tril = jnp.tril(d_sgu_w)
    rep = [("e_norm_mix", e_norm_mix, m_e_norm_mix, v_e_norm_mix, d_enorm),
           ("e_q_norm", e_q_norm, m_e_q_norm, v_e_q_norm, d_qnorm),
           ("e_kv_norm", e_kv_norm, m_e_kv_norm, v_e_kv_norm, d_kvnorm),
           ("e_v_norm", e_v_norm, m_e_v_norm, v_e_v_norm, d_vgain),
           ("e_sgu_w", e_sgu_w, m_e_sgu_w, v_e_sgu_w, d_sgu_w_tril),
           ("e_sgu_b", e_sgu_b, m_e_sgu_b, v_e_sgu_b, d_sgu_b),
           ("e_mla_out_norm", e_mla_out_norm, m_e_mla_out_norm, v_e_mla_out_norm, d_mla_out),
           ("e_sgu_out_norm", e_sgu_out_norm, m_e_sgu_out_norm, v_e_sgu_out_norm, d_sgu_out),
           ("mlp_norm", mlp_norm, m_mlp_norm, v_mlp_norm, jnp.concatenate([d_mlp0, d_mlp1], axis=0)),
           ("final_norm", final_norm, m_final_norm, v_final_norm, d_final)]
    sizes = [int(np.prod(r[1].shape)) for r in rep]
    n_rep = sum(sizes)
    n_all = n_rep + 4 * D + 1
    width = -(-n_all // (8 * LANES)) * LANES
    pad = 8 * width - n_all
    flat = jnp.concatenate([r[4].reshape(-1) for r in rep]
                           + [d_onorm_full.reshape(-1), dconv_full.reshape(-1), loss_part[0, :1],
                              jnp.zeros((pad,), F32)])
    small_started, small_token = gather_start([[flat.reshape(8, width)]], b_r2[0], name="gather_small_grads_start")

    def finish(grads, a_bufs, b_bufs, t, w, m, v, layer=0, prev=None, tag="", deps=()):
        return run_job(adam_job(grads[t], a_bufs[t], b_bufs[t], w, m, v, layer, prev), index=hosted["job_index"],
                       name=f"adam_{tag}", deps=deps)

    r_w1 = finish(grads_r2, a_r2, b_r2, 0, mlp_w1, m_mlp_w1, v_mlp_w1, 0, r_w1, tag="w1_l0", deps=[tok_r3, small_token])
    r_w2 = finish(grads_r2, a_r2, b_r2, 1, mlp_w2, m_mlp_w2, v_mlp_w2, 0, r_w2, tag="w2_l0", deps=[r_w1[1]])
    b_r2b = chips_wait(st_r2b, r_w2[1], name="reduce_chips_wait_r2b")
    r_w_out_e = finish(grads_r2b, a_r2b, b_r2b, 0, e_w_out, m_e_w_out, v_e_w_out, tag="e_w_out")
    r_w_uq = finish(grads_r2b, a_r2b, b_r2b, 1, e_w_uq, m_e_w_uq, v_e_w_uq, tag="e_w_uq")
    r_w_ukv = finish(grads_r2b, a_r2b, b_r2b, 2, e_w_ukv, m_e_w_ukv, v_e_w_ukv, tag="e_w_ukv")
    b_r3 = chips_wait(st_r3, r_w_out_e[1], name="reduce_chips_wait_r3")
    g_w_in_t = reduce_sum(grads_r3[0], a_r3[0], b_r3[0], name="sum_e_w_in")
    w_in_upd_t = adam_rows(g_w_in_t, jnp.transpose(e_w_in[0]), jnp.transpose(m_e_w_in[0]), jnp.transpose(v_e_w_in[0]),
                           name="adam_e_w_in")
    r_w_in = [jnp.transpose(t)[None] for t in (g_w_in_t, *w_in_upd_t)]

    small_srcs, small_lands = gather_wait(small_started[0], [r_w2[1]], name="gather_small_grads_wait")
    small_all = gather_finish(small_srcs, small_lands, name="gather_small_grads_finish")[0]
    summed = sum_rows8(small_all.reshape(N_DEV * 8, width), 8, name="sum_small_grads").reshape(-1)

    loss = summed[n_rep + 4 * D]

    def pack_rep(i):
        return jnp.concatenate([r[i].reshape(-1) for r in rep]).reshape(n_rep // LANES, LANES)

    g_rep = summed[:n_rep].reshape(n_rep // LANES, LANES)
    d_rep, nm_rep, nv_rep = adam_flat(g_rep, pack_rep(1), pack_rep(2), pack_rep(3), name="adam_replicated")

    def unpack_rep(flat2d):
        out, off = {}, 0
        f = flat2d.reshape(-1)
        for r, n in zip(rep, sizes):
            out[r[0]] = f[off:off + n].reshape(r[1].shape)
            off += n
        return out

    small = {"grad": unpack_rep(g_rep), "delta": unpack_rep(d_rep), "new_m": unpack_rep(nm_rep),
             "new_v": unpack_rep(nv_rep)}
    g_onorm = lax.dynamic_slice(summed[n_rep:n_rep + D].reshape(1, D), (0, me * d_shard), (1, d_shard))
    g_conv = lax.dynamic_slice(summed[n_rep + D:n_rep + 4 * D].reshape(3, D), (0, me * d_shard), (3, d_shard))

    def pack_sharded(norm_part, conv_part):
        return jnp.concatenate([norm_part, conv_part, jnp.zeros((4, d_shard), F32)], axis=0)

    g_sh = pack_sharded(g_onorm, g_conv)
    d_sh, nm_sh, nv_sh = adam_flat(g_sh, pack_sharded(o_norm_mix, o_conv_w[0]), pack_sharded(m_o_norm_mix, m_o_conv_w[0]),
                                   pack_sharded(v_o_norm_mix, v_o_conv_w[0]), name="adam_sharded_small")
    for kind, arr in (("grad", g_sh), ("delta", d_sh), ("new_m", nm_sh), ("new_v", nv_sh)):
        small[kind]["o_norm_mix"] = arr[0:1]
        small[kind]["o_conv_w"] = arr[1:4][None]

    big = {"e_w_in": r_w_in, "e_w_uq": r_w_uq, "e_w_ukv": r_w_ukv, "e_w_out": r_w_out_e, "o_w_in": r_w_in_o,
           "o_w_out": r_w_out_o, "mlp_w1": r_w1, "mlp_w2": r_w2}
    order = ["e_norm_mix", "e_w_in", "e_q_norm", "e_w_uq", "e_kv_norm", "e_w_ukv", "e_v_norm", "e_sgu_w", "e_sgu_b",
             "e_mla_out_norm", "e_sgu_out_norm", "e_w_out", "o_norm_mix", "o_w_in", "o_conv_w", "o_w_out", "mlp_norm",
             "mlp_w1", "mlp_w2", "final_norm"]
    result = [loss, grad_x[None]]
    for ki, kind in enumerate(("grad", "delta", "new_m", "new_v")):
        for nm in order:
            result.append(big[nm][ki] if nm in big else small[kind][nm])
    return tuple(result)
```
